```python
import math
import jax, jax.numpy as jnp
from jax import lax
import numpy as np

D_MODEL = 1024
BATCH = 8
SEQ = 2048
DEPTH = 1

CHUNK = 128
A_GROUPS = 8
A_GROUP_DIM = D_MODEL // A_GROUPS
A_WIDTH = A_GROUPS * A_GROUP_DIM
B_HEADS = 16
B_HEAD_DIM = D_MODEL // B_HEADS
B_WIDTH = B_HEADS * B_HEAD_DIM
DILATED_PATTERNS = ((128, 1), (512, 4), (2048, 16))
N_BRANCHES = 2
D_FF = 4 * D_MODEL
EPS = 1e-6
IN_SPLITS = (A_WIDTH, 2 * A_WIDTH, 2 * A_WIDTH + B_WIDTH, 2 * A_WIDTH + 2 * B_WIDTH,
             2 * A_WIDTH + 3 * B_WIDTH, 2 * A_WIDTH + 3 * B_WIDTH + D_MODEL)
IN_COLS = 2 * A_WIDTH + 3 * B_WIDTH + N_BRANCHES * D_MODEL

kernel_name = "hybrid_gmlp_dilated_alibi_block"


def rms_norm(x, g):
    xf = x.astype(jnp.float32)
    y = xf * lax.rsqrt(jnp.mean(xf * xf, axis=-1, keepdims=True) + EPS)
    return y.astype(x.dtype) * g


def layer_norm(x, g, b):
    xf = x.astype(jnp.float32)
    mu = jnp.mean(xf, axis=-1, keepdims=True)
    var = jnp.mean(jnp.square(xf - mu), axis=-1, keepdims=True)
    return ((xf - mu) * lax.rsqrt(var + EPS)).astype(x.dtype) * g + b


def alibi_slopes(n_heads):
    return jnp.exp2(-8.0 * jnp.arange(1, n_heads + 1, dtype=jnp.float32) / n_heads)


def chunked_spatial_gating(u, v, w_s, b_s, ln_g, ln_b):
    bsz, s, _ = v.shape
    v = layer_norm(v, ln_g, ln_b)
    vc = v.reshape(bsz, s // CHUNK, CHUNK, A_GROUPS, A_GROUP_DIM)
    causal = jnp.tril(jnp.ones((CHUNK, CHUNK), dtype=bool))
    ws = jnp.where(causal[None], w_s, jnp.zeros_like(w_s))
    mixed = jnp.einsum('gts,bcsgd->bctgd', ws, vc) + b_s.T[None, None, :, :, None]
    return u * mixed.reshape(bsz, s, A_WIDTH)


def dilated_window_attention(q, k, v, slopes, window, dilation):
    bsz, s, h, dh = q.shape
    n_back = window // dilation
    blk = n_back
    span = dilation * blk
    sp = -(-s // span) * span
    pad = sp - s
    sub_len = sp // dilation
    nb = sub_len // blk

    def to_sub(t):
        t = jnp.pad(t, ((0, 0), (0, pad), (0, 0), (0, 0)))
        t = jnp.moveaxis(t.reshape(bsz, sub_len, dilation, h, dh), 2, 1)
        return t.reshape(bsz, dilation, nb, blk, h, dh)

    def with_prev(t):
        prev = jnp.pad(t, ((0, 0), (0, 0), (1, 0), (0, 0), (0, 0), (0, 0)))[:, :, :-1]
        return jnp.concatenate([prev, t], axis=3)

    qs = to_sub(q)
    kb = with_prev(to_sub(k))
    vb = with_prev(to_sub(v))
    scores = jnp.einsum('brnqhd,brnkhd->brnhqk', qs, kb).astype(jnp.float32)
    qi = jnp.arange(blk)[:, None]
    ki = jnp.arange(2 * blk)[None, :]
    delta = qi + blk - ki
    blk_idx = jnp.arange(nb)[:, None, None]
    valid = (delta >= 0) & (delta <= n_back) & (blk_idx * blk + qi - delta >= 0)
    bias = -slopes[:, None, None] * (dilation * delta).astype(jnp.float32)
    scores = jnp.where(valid[None, None, :, None], scores + bias[None, None, None], -jnp.inf)
    m = jnp.max(scores, axis=-1, keepdims=True)
    p = jnp.exp(scores - m)
    den = jnp.sum(p, axis=-1)
    out = jnp.einsum('brnhqk,brnkhd->brnqhd', p, vb.astype(jnp.float32))
    out = out / jnp.swapaxes(den, -1, -2)[..., None]
    lse = jnp.swapaxes(m[..., 0] + jnp.log(den), -1, -2)

    def from_sub(t):
        rest = t.shape[4:]
        t = jnp.moveaxis(t.reshape((bsz, dilation, sub_len) + rest), 1, 2)
        return t.reshape((bsz, sp) + rest)[:, :s]

    return from_sub(out).astype(q.dtype), from_sub(lse)


def dilated_attention_mixture(q, k, v):
    slopes = alibi_slopes(B_HEADS)
    outs, lses = [], []
    for window, dilation in DILATED_PATTERNS:
        o, l = dilated_window_attention(q, k, v, slopes, window, dilation)
        outs.append(o)
        lses.append(l)
    w = jax.nn.softmax(jnp.stack(lses, axis=-1), axis=-1)
    o = jnp.stack(outs, axis=-1).astype(jnp.float32)
    return jnp.einsum('bshdp,bshp->bshd', o, w).astype(q.dtype)


def _fwd_setup_inputs(seed: int = 0) -> dict:
    key = jax.random.key(seed)
    ks = jax.random.split(key, 20)
    f32 = jnp.float32

    def nrm(k, shape, scale):
        return jax.random.normal(k, shape, f32) * scale

    def gain(k, n):
        return 1.0 + nrm(k, (DEPTH, n), 0.02)

    return {
        "x": nrm(ks[0], (BATCH, SEQ, D_MODEL), 1.0),
        "norm_mix_pre": gain(ks[1], D_MODEL),
        "w_in": nrm(ks[2], (DEPTH, D_MODEL, IN_COLS), D_MODEL ** -0.5),
        "b_gate": nrm(ks[3], (DEPTH, N_BRANCHES, D_MODEL), 0.02),
        "ln_v_g": gain(ks[4], A_WIDTH),
        "ln_v_b": nrm(ks[5], (DEPTH, A_WIDTH), 0.02),
        "w_s": nrm(ks[6], (DEPTH, A_GROUPS, CHUNK, CHUNK), 0.5 * CHUNK ** -0.5),
        "b_s": 1.0 + nrm(ks[7], (DEPTH, A_GROUPS, CHUNK), 0.02),
        "w_a_proj": nrm(ks[8], (DEPTH, A_WIDTH, D_MODEL), A_WIDTH ** -0.5),
        "w_b_proj": nrm(ks[9], (DEPTH, B_WIDTH, D_MODEL), B_WIDTH ** -0.5),
        "w_out": nrm(ks[10], (DEPTH, D_MODEL, D_MODEL), D_MODEL ** -0.5),
        "norm_mix_post": gain(ks[11], D_MODEL),
        "norm_ffn_pre": gain(ks[12], D_MODEL),
        "w_ff1": nrm(ks[13], (DEPTH, D_MODEL, D_FF), D_MODEL ** -0.5),
        "w_ff2": nrm(ks[14], (DEPTH, D_FF, D_MODEL), D_FF ** -0.5),
        "norm_ffn_post": gain(ks[15], D_MODEL),
    }


def _fwd_reference(x, norm_mix_pre, w_in, b_gate, ln_v_g, ln_v_b, w_s, b_s, w_a_proj, w_b_proj,
              w_out, norm_mix_post, norm_ffn_pre, w_ff1, w_ff2, norm_ffn_post):
    bsz, s, _ = x.shape
    q_scale = 1.0 / math.sqrt(B_HEAD_DIM)
    for l in range(DEPTH):
        h = rms_norm(x, norm_mix_pre[l])
        z = h @ w_in[l]
        u_a, v_a, q, k, v_b, g_a, g_b = jnp.split(z, IN_SPLITS, axis=-1)
        y_a = chunked_spatial_gating(jax.nn.gelu(u_a), jax.nn.gelu(v_a), w_s[l], b_s[l],
                                     ln_v_g[l], ln_v_b[l])
        q = q.reshape(bsz, s, B_HEADS, B_HEAD_DIM) * q_scale
        k = k.reshape(bsz, s, B_HEADS, B_HEAD_DIM)
        v_b = v_b.reshape(bsz, s, B_HEADS, B_HEAD_DIM)
        y_b = dilated_attention_mixture(q, k, v_b).reshape(bsz, s, B_WIDTH)
        merged = (jax.nn.sigmoid(g_a + b_gate[l, 0]) * (y_a @ w_a_proj[l])
                  + jax.nn.sigmoid(g_b + b_gate[l, 1]) * (y_b @ w_b_proj[l]))
        x = x + rms_norm(merged @ w_out[l], norm_mix_post[l])
        h = rms_norm(x, norm_ffn_pre[l])
        f = jnp.square(jax.nn.relu(h @ w_ff1[l])) @ w_ff2[l]
        x = x + rms_norm(f, norm_ffn_post[l])
    return x


import jax as _jax
import jax.numpy as _jnp

TWIN_FORMAT = 'train_step'
FWD_PARAMS = ['x', 'norm_mix_pre', 'w_in', 'b_gate', 'ln_v_g', 'ln_v_b', 'w_s', 'b_s', 'w_a_proj', 'w_b_proj', 'w_out', 'norm_mix_post', 'norm_ffn_pre', 'w_ff1', 'w_ff2', 'norm_ffn_post']
TWIN_WEIGHTS = ['norm_mix_pre', 'w_in', 'b_gate', 'ln_v_g', 'ln_v_b', 'w_s', 'b_s', 'w_a_proj', 'w_b_proj', 'w_out', 'norm_mix_post', 'norm_ffn_pre', 'w_ff1', 'w_ff2', 'norm_ffn_post']
TWIN_DIFF_INPUT = 'x'
TWIN_INPUTS = ['x', 'norm_mix_pre', 'w_in', 'b_gate', 'ln_v_g', 'ln_v_b', 'w_s', 'b_s', 'w_a_proj', 'w_b_proj', 'w_out', 'norm_mix_post', 'norm_ffn_pre', 'w_ff1', 'w_ff2', 'norm_ffn_post', 'loss_target', 'm_norm_mix_pre', 'm_w_in', 'm_b_gate', 'm_ln_v_g', 'm_ln_v_b', 'm_w_s', 'm_b_s', 'm_w_a_proj', 'm_w_b_proj', 'm_w_out', 'm_norm_mix_post', 'm_norm_ffn_pre', 'm_w_ff1', 'm_w_ff2', 'm_norm_ffn_post', 'v_norm_mix_pre', 'v_w_in', 'v_b_gate', 'v_ln_v_g', 'v_ln_v_b', 'v_w_s', 'v_b_s', 'v_w_a_proj', 'v_w_b_proj', 'v_w_out', 'v_norm_mix_post', 'v_norm_ffn_pre', 'v_w_ff1', 'v_w_ff2', 'v_norm_ffn_post']
TWIN_OUTPUTS = ['loss', 'grad_x', 'grad_norm_mix_pre', 'grad_w_in', 'grad_b_gate', 'grad_ln_v_g', 'grad_ln_v_b', 'grad_w_s', 'grad_b_s', 'grad_w_a_proj', 'grad_w_b_proj', 'grad_w_out', 'grad_norm_mix_post', 'grad_norm_ffn_pre', 'grad_w_ff1', 'grad_w_ff2', 'grad_norm_ffn_post', 'delta_norm_mix_pre', 'delta_w_in', 'delta_b_gate', 'delta_ln_v_g', 'delta_ln_v_b', 'delta_w_s', 'delta_b_s', 'delta_w_a_proj', 'delta_w_b_proj', 'delta_w_out', 'delta_norm_mix_post', 'delta_norm_ffn_pre', 'delta_w_ff1', 'delta_w_ff2', 'delta_norm_ffn_post', 'new_m_norm_mix_pre', 'new_m_w_in', 'new_m_b_gate', 'new_m_ln_v_g', 'new_m_ln_v_b', 'new_m_w_s', 'new_m_b_s', 'new_m_w_a_proj', 'new_m_w_b_proj', 'new_m_w_out', 'new_m_norm_mix_post', 'new_m_norm_ffn_pre', 'new_m_w_ff1', 'new_m_w_ff2', 'new_m_norm_ffn_post', 'new_v_norm_mix_pre', 'new_v_w_in', 'new_v_b_gate', 'new_v_ln_v_g', 'new_v_ln_v_b', 'new_v_w_s', 'new_v_b_s', 'new_v_w_a_proj', 'new_v_w_b_proj', 'new_v_w_out', 'new_v_norm_mix_post', 'new_v_norm_ffn_pre', 'new_v_w_ff1', 'new_v_w_ff2', 'new_v_norm_ffn_post']
TWIN_LEAF_KINDS = {'loss': 'loss', 'grad_x': 'grad_x', 'grad_norm_mix_pre': 'grad_w', 'grad_w_in': 'grad_w', 'grad_b_gate': 'grad_w', 'grad_ln_v_g': 'grad_w', 'grad_ln_v_b': 'grad_w', 'grad_w_s': 'grad_w', 'grad_b_s': 'grad_w', 'grad_w_a_proj': 'grad_w', 'grad_w_b_proj': 'grad_w', 'grad_w_out': 'grad_w', 'grad_norm_mix_post': 'grad_w', 'grad_norm_ffn_pre': 'grad_w', 'grad_w_ff1': 'grad_w', 'grad_w_ff2': 'grad_w', 'grad_norm_ffn_post': 'grad_w', 'delta_norm_mix_pre': 'delta_w', 'delta_w_in': 'delta_w', 'delta_b_gate': 'delta_w', 'delta_ln_v_g': 'delta_w', 'delta_ln_v_b': 'delta_w', 'delta_w_s': 'delta_w', 'delta_b_s': 'delta_w', 'delta_w_a_proj': 'delta_w', 'delta_w_b_proj': 'delta_w', 'delta_w_out': 'delta_w', 'delta_norm_mix_post': 'delta_w', 'delta_norm_ffn_pre': 'delta_w', 'delta_w_ff1': 'delta_w', 'delta_w_ff2': 'delta_w', 'delta_norm_ffn_post': 'delta_w', 'new_m_norm_mix_pre': 'new_m', 'new_m_w_in': 'new_m', 'new_m_b_gate': 'new_m', 'new_m_ln_v_g': 'new_m', 'new_m_ln_v_b': 'new_m', 'new_m_w_s': 'new_m', 'new_m_b_s': 'new_m', 'new_m_w_a_proj': 'new_m', 'new_m_w_b_proj': 'new_m', 'new_m_w_out': 'new_m', 'new_m_norm_mix_post': 'new_m', 'new_m_norm_ffn_pre': 'new_m', 'new_m_w_ff1': 'new_m', 'new_m_w_ff2': 'new_m', 'new_m_norm_ffn_post': 'new_m', 'new_v_norm_mix_pre': 'new_v', 'new_v_w_in': 'new_v', 'new_v_b_gate': 'new_v', 'new_v_ln_v_g': 'new_v', 'new_v_ln_v_b': 'new_v', 'new_v_w_s': 'new_v', 'new_v_b_s': 'new_v', 'new_v_w_a_proj': 'new_v', 'new_v_w_b_proj': 'new_v', 'new_v_w_out': 'new_v', 'new_v_norm_mix_post': 'new_v', 'new_v_norm_ffn_pre': 'new_v', 'new_v_w_ff1': 'new_v', 'new_v_w_ff2': 'new_v', 'new_v_norm_ffn_post': 'new_v'}


def _forward(args):
    return _fwd_reference(*[args[k] for k in FWD_PARAMS])


def _output_shape():
    out = _jax.eval_shape(lambda: _forward(_fwd_setup_inputs(0)))
    return out.shape, out.dtype

N_MICROBATCH = 1
ADAM_LR = 0.001
ADAM_B1 = 0.9
ADAM_B2 = 0.999
ADAM_EPS = 1e-08
ADAM_WD = 0.01
ADAM_STEP = 10
PER_EXAMPLE_BATCH_AXIS = {'x': 0, 'loss_target': 0}
SHARED_INPUTS = []
_WEIGHT_DTYPES = {'norm_mix_pre': _jnp.float32, 'w_in': _jnp.float32, 'b_gate': _jnp.float32, 'ln_v_g': _jnp.float32, 'ln_v_b': _jnp.float32, 'w_s': _jnp.float32, 'b_s': _jnp.float32, 'w_a_proj': _jnp.float32, 'w_b_proj': _jnp.float32, 'w_out': _jnp.float32, 'norm_mix_post': _jnp.float32, 'norm_ffn_pre': _jnp.float32, 'w_ff1': _jnp.float32, 'w_ff2': _jnp.float32, 'norm_ffn_post': _jnp.float32}
MOMENT_SCALE = {'norm_mix_pre': 5.629320e-01, 'w_in': 1.810575e-01, 'b_gate': 7.083873e-01, 'ln_v_g': 8.393383e-02, 'ln_v_b': 9.958907e-02, 'w_s': 1.610693e-01, 'b_s': 2.647896e-01, 'w_a_proj': 2.319515e+00, 'w_b_proj': 2.230631e-01, 'w_out': 2.427032e+00, 'norm_mix_post': 1.632986e+01, 'norm_ffn_pre': 8.459916e-01, 'w_ff1': 4.614615e-01, 'w_ff2': 2.278802e+00, 'norm_ffn_post': 1.674182e+01}


def _to_microbatches(a, axis):
    t = _jnp.moveaxis(a, axis, 0)
    t = t.reshape((N_MICROBATCH, t.shape[0] // N_MICROBATCH) + t.shape[1:])
    return _jnp.moveaxis(t, 1, axis + 1)


def setup_inputs(seed: int = 0) -> dict:
    inp = _fwd_setup_inputs(seed)
    key = _jax.random.fold_in(_jax.random.key(seed), 7919)
    shape, _ = _output_shape()
    out = dict(inp)
    out["loss_target"] = _jax.random.normal(_jax.random.fold_in(key, 0), shape, _jnp.float32)
    for i, name in enumerate(TWIN_WEIGHTS):
        w = inp[name].astype(_jnp.float32)
        if MOMENT_SCALE is None:
            s = _jnp.sqrt(_jnp.mean(_jnp.square(w)) + 1e-30)
        else:
            s = MOMENT_SCALE[name]
        km, kv = _jax.random.split(_jax.random.fold_in(key, i + 1))
        out[name] = w
        out["m_" + name] = s * _jax.random.normal(km, w.shape, _jnp.float32)
        out["v_" + name] = (s * s) * _jax.random.uniform(kv, w.shape, _jnp.float32, 0.5, 1.5)
    if N_MICROBATCH > 1:
        for name, axis in PER_EXAMPLE_BATCH_AXIS.items():
            out[name] = _to_microbatches(out[name], axis)
    return {'x': out['x'], 'norm_mix_pre': out['norm_mix_pre'], 'w_in': out['w_in'], 'b_gate': out['b_gate'], 'ln_v_g': out['ln_v_g'], 'ln_v_b': out['ln_v_b'], 'w_s': out['w_s'], 'b_s': out['b_s'], 'w_a_proj': out['w_a_proj'], 'w_b_proj': out['w_b_proj'], 'w_out': out['w_out'], 'norm_mix_post': out['norm_mix_post'], 'norm_ffn_pre': out['norm_ffn_pre'], 'w_ff1': out['w_ff1'], 'w_ff2': out['w_ff2'], 'norm_ffn_post': out['norm_ffn_post'], 'loss_target': out['loss_target'], 'm_norm_mix_pre': out['m_norm_mix_pre'], 'm_w_in': out['m_w_in'], 'm_b_gate': out['m_b_gate'], 'm_ln_v_g': out['m_ln_v_g'], 'm_ln_v_b': out['m_ln_v_b'], 'm_w_s': out['m_w_s'], 'm_b_s': out['m_b_s'], 'm_w_a_proj': out['m_w_a_proj'], 'm_w_b_proj': out['m_w_b_proj'], 'm_w_out': out['m_w_out'], 'm_norm_mix_post': out['m_norm_mix_post'], 'm_norm_ffn_pre': out['m_norm_ffn_pre'], 'm_w_ff1': out['m_w_ff1'], 'm_w_ff2': out['m_w_ff2'], 'm_norm_ffn_post': out['m_norm_ffn_post'], 'v_norm_mix_pre': out['v_norm_mix_pre'], 'v_w_in': out['v_w_in'], 'v_b_gate': out['v_b_gate'], 'v_ln_v_g': out['v_ln_v_g'], 'v_ln_v_b': out['v_ln_v_b'], 'v_w_s': out['v_w_s'], 'v_b_s': out['v_b_s'], 'v_w_a_proj': out['v_w_a_proj'], 'v_w_b_proj': out['v_w_b_proj'], 'v_w_out': out['v_w_out'], 'v_norm_mix_post': out['v_norm_mix_post'], 'v_norm_ffn_pre': out['v_norm_ffn_pre'], 'v_w_ff1': out['v_w_ff1'], 'v_w_ff2': out['v_w_ff2'], 'v_norm_ffn_post': out['v_norm_ffn_post']}


def _loss(weights, diff, rest, loss_target):
    with _jax.named_scope("forward"):
        args = {**rest, TWIN_DIFF_INPUT: diff, **{k: w.astype(_WEIGHT_DTYPES[k]) for k, w in weights.items()}}
        y = _forward(args)
    with _jax.named_scope("loss_head"):
        err = _jnp.square(y.astype(_jnp.float32) - loss_target)
        return 0.5 * _jnp.sum(_jnp.mean(err, axis=-1)) if err.ndim else 0.5 * err


def _adamw(w, g, m, v):
    m = ADAM_B1 * m + (1.0 - ADAM_B1) * g
    v = ADAM_B2 * v + (1.0 - ADAM_B2) * _jnp.square(g)
    m_hat = m / (1.0 - ADAM_B1 ** ADAM_STEP)
    v_hat = v / (1.0 - ADAM_B2 ** ADAM_STEP)
    delta = -ADAM_LR * (m_hat / (_jnp.sqrt(v_hat) + ADAM_EPS) + ADAM_WD * w)
    return delta, m, v


def reference(x, norm_mix_pre, w_in, b_gate, ln_v_g, ln_v_b, w_s, b_s, w_a_proj, w_b_proj, w_out, norm_mix_post, norm_ffn_pre, w_ff1, w_ff2, norm_ffn_post, loss_target, m_norm_mix_pre, m_w_in, m_b_gate, m_ln_v_g, m_ln_v_b, m_w_s, m_b_s, m_w_a_proj, m_w_b_proj, m_w_out, m_norm_mix_post, m_norm_ffn_pre, m_w_ff1, m_w_ff2, m_norm_ffn_post, v_norm_mix_pre, v_w_in, v_b_gate, v_ln_v_g, v_ln_v_b, v_w_s, v_b_s, v_w_a_proj, v_w_b_proj, v_w_out, v_norm_mix_post, v_norm_ffn_pre, v_w_ff1, v_w_ff2, v_norm_ffn_post):
    given = dict(x=x, norm_mix_pre=norm_mix_pre, w_in=w_in, b_gate=b_gate, ln_v_g=ln_v_g, ln_v_b=ln_v_b, w_s=w_s, b_s=b_s, w_a_proj=w_a_proj, w_b_proj=w_b_proj, w_out=w_out, norm_mix_post=norm_mix_post, norm_ffn_pre=norm_ffn_pre, w_ff1=w_ff1, w_ff2=w_ff2, norm_ffn_post=norm_ffn_post, loss_target=loss_target, m_norm_mix_pre=m_norm_mix_pre, m_w_in=m_w_in, m_b_gate=m_b_gate, m_ln_v_g=m_ln_v_g, m_ln_v_b=m_ln_v_b, m_w_s=m_w_s, m_b_s=m_b_s, m_w_a_proj=m_w_a_proj, m_w_b_proj=m_w_b_proj, m_w_out=m_w_out, m_norm_mix_post=m_norm_mix_post, m_norm_ffn_pre=m_norm_ffn_pre, m_w_ff1=m_w_ff1, m_w_ff2=m_w_ff2, m_norm_ffn_post=m_norm_ffn_post, v_norm_mix_pre=v_norm_mix_pre, v_w_in=v_w_in, v_b_gate=v_b_gate, v_ln_v_g=v_ln_v_g, v_ln_v_b=v_ln_v_b, v_w_s=v_w_s, v_b_s=v_b_s, v_w_a_proj=v_w_a_proj, v_w_b_proj=v_w_b_proj, v_w_out=v_w_out, v_norm_mix_post=v_norm_mix_post, v_norm_ffn_pre=v_norm_ffn_pre, v_w_ff1=v_w_ff1, v_w_ff2=v_w_ff2, v_norm_ffn_post=v_norm_ffn_post)
    weights = {n: given[n] for n in TWIN_WEIGHTS}
    shared = {n: given[n] for n in SHARED_INPUTS}
    per_example = {n: given[n] for n in ['x']}
    grad_fn = _jax.value_and_grad(_loss, argnums=(0, 1))

    def one_microbatch(ex, loss_target):
        ex = dict(ex)
        diff = ex.pop(TWIN_DIFF_INPUT)
        return grad_fn(weights, diff, {**shared, **ex}, loss_target)

    if N_MICROBATCH == 1:
        loss, (grad_w, grad_x) = one_microbatch(per_example, given["loss_target"])
    else:
        def body(carry, xs):
            loss_sum, grad_sum = carry
            l_k, (gw_k, gx_k) = one_microbatch(xs[0], xs[1])
            with _jax.named_scope("update"):
                return (loss_sum + l_k, _jax.tree.map(_jnp.add, grad_sum, gw_k)), gx_k

        init = (_jnp.zeros((), _jnp.float32), _jax.tree.map(_jnp.zeros_like, weights))
        (loss, grad_w), grad_x = _jax.lax.scan(body, init, (per_example, given["loss_target"]))
    with _jax.named_scope("update"):
        delta_w, new_m, new_v = {}, {}, {}
        for n in TWIN_WEIGHTS:
            delta_w[n], new_m[n], new_v[n] = _adamw(weights[n], grad_w[n], given["m_" + n], given["v_" + n])
    return (loss, grad_x, *[grad_w[n] for n in TWIN_WEIGHTS], *[delta_w[n] for n in TWIN_WEIGHTS],
            *[new_m[n] for n in TWIN_WEIGHTS], *[new_v[n] for n in TWIN_WEIGHTS])
```

```python
import functools
import math

import numpy as np
import jax
import jax.numpy as jnp
from jax import lax
from jax.experimental import pallas as pl
from jax.experimental.pallas import tpu as pltpu

F32 = jnp.float32
BF16 = jnp.bfloat16
MESH = pl.DeviceIdType.MESH

D = 1024
EPS = 1e-6
CHUNK = 128
GROUPS = 8
HEADS = 16
HEAD_DIM = 64
ATT_T = 256
N_CHIPS = 4
D_FF = 4 * D
IN_COLS = 7 * D
IN_SHARD = IN_COLS // N_CHIPS
MASKED = -1e30
VMEM_LIMIT = 56 * 2 ** 20

ADAM_LR, ADAM_B1, ADAM_B2, ADAM_EPS, ADAM_WD, ADAM_STEP = 0.001, 0.9, 0.999, 1e-08, 0.01, 10

NN = (((1,), (0,)), ((), ()))
NT = (((1,), (1,)), ((), ()))
TN = (((0,), (0,)), ((), ()))


def _dot(a, b, dims=NN):
    return lax.dot_general(a, b, dims, preferred_element_type=F32)


def _params(*sem):
    return pltpu.CompilerParams(dimension_semantics=sem or None, vmem_limit_bytes=VMEM_LIMIT)


def _rows(tr, c, col=0):
    return pl.BlockSpec((tr, c), lambda i: (i, col))


def _full(shape):
    n = len(shape)
    return pl.BlockSpec(shape, lambda *_: (0,) * n)


def _gelu(x):
    k = math.sqrt(2.0 / math.pi)
    return 0.5 * x * (1.0 + jnp.tanh(k * (x + 0.044715 * x * x * x)))


def _gelu_and_grad(x):
    k = math.sqrt(2.0 / math.pi)
    t = jnp.tanh(k * (x + 0.044715 * x * x * x))
    g = 0.5 * x * (1.0 + t)
    dg = 0.5 * (1.0 + t) + 0.5 * x * (1.0 - t * t) * (k * (1.0 + 3.0 * 0.044715 * x * x))
    return g, dg


def _sigmoid(x):
    return 1.0 / (1.0 + jnp.exp(-x))


def _rms(x):
    r = lax.rsqrt(jnp.mean(x * x, axis=-1, keepdims=True) + EPS)
    return x * r, r


def _rms_bwd(dn, xhat, r):
    return r * (dn - xhat * jnp.mean(dn * xhat, axis=-1, keepdims=True))


def norm_pre(x, g):
    s = x.shape[0]
    tr = 512

    def body(x_ref, g_ref, h_ref):
        xhat, _ = _rms(x_ref[...])
        h_ref[...] = (xhat * g_ref[...]).astype(BF16)

    return pl.pallas_call(
        body, name="norm_pre", grid=(s // tr,),
        in_specs=[_rows(tr, D), _full((1, D))], out_specs=_rows(tr, D),
        out_shape=jax.ShapeDtypeStruct((s, D), BF16), compiler_params=_params("parallel"),
    )(x, g)


def mm_in(h, wg):
    s = h.shape[0]
    tm, tn = 1024, IN_SHARD // 2
    per = IN_SHARD // tn

    def body(a_ref, b_ref, o_ref):
        o_ref[...] = _dot(a_ref[...], b_ref[...])

    return pl.pallas_call(
        body, name="mm_in", grid=(s // tm, IN_COLS // tn),
        in_specs=[pl.BlockSpec((tm, D), lambda i, j: (i, 0)),
                  pl.BlockSpec((None, D, tn), lambda i, j: (j // per, 0, j % per))],
        out_specs=pl.BlockSpec((tm, tn), lambda i, j: (i, j)),
        out_shape=jax.ShapeDtypeStruct((s, IN_COLS), F32), compiler_params=_params("parallel", "parallel"),
    )(h, wg)


def _tril_ws(ws_ref, g):
    r = lax.broadcasted_iota(jnp.int32, (CHUNK, CHUNK), 0)
    c = lax.broadcasted_iota(jnp.int32, (CHUNK, CHUNK), 1)
    return jnp.where(c <= r, ws_ref[g], 0.0).astype(BF16)


def _layer_norm(v):
    mu = jnp.mean(v, axis=-1, keepdims=True)
    d = v - mu
    rstd = lax.rsqrt(jnp.mean(d * d, axis=-1, keepdims=True) + EPS)
    return d * rstd, rstd


def gating_fwd(z, ln_g, ln_b, w_s, bs_t):
    s = z.shape[0]

    def body(u_ref, v_ref, lg_ref, lb_ref, ws_ref, bst_ref, ya_ref):
        ug = _gelu(u_ref[...])
        vhat, _ = _layer_norm(_gelu(v_ref[...]))
        vn = (vhat * lg_ref[...] + lb_ref[...]).astype(BF16)
        for g in range(GROUPS):
            cols = slice(g * CHUNK, (g + 1) * CHUNK)
            mixed = _dot(_tril_ws(ws_ref, g), vn[:, cols]) + bst_ref[:, g:g + 1]
            ya_ref[:, cols] = (ug[:, cols] * mixed).astype(BF16)

    return pl.pallas_call(
        body, name="gating_fwd", grid=(s // CHUNK,),
        in_specs=[_rows(CHUNK, D, 0), _rows(CHUNK, D, 1), _full((1, D)), _full((1, D)),
                  _full((GROUPS, CHUNK, CHUNK)), _full((CHUNK, GROUPS))],
        out_specs=_rows(CHUNK, D), out_shape=jax.ShapeDtypeStruct((s, D), BF16),
        compiler_params=_params("parallel"),
    )(z, z, ln_g, ln_b, w_s, bs_t)


def _attn_tables(s):
    nd = s // ATT_T
    r = np.arange(ATT_T)[None, :, None]
    c = np.arange(ATT_T)[None, None, :]
    delta = np.arange(nd)[:, None, None] * ATT_T + r - c
    count = np.zeros(delta.shape, np.int64)
    for window, dilation in ((128, 1), (512, 4), (2048, 16)):
        count += (delta >= 0) & (delta % dilation == 0) & (delta <= window)
    logc = np.where(count > 0, np.log(np.maximum(count, 1)), MASKED)
    return jnp.asarray(logc, F32), jnp.asarray(delta, F32)


def _head_slopes():
    sl = np.exp2(-8.0 * np.arange(1, HEADS + 1, dtype=np.float32) / HEADS).astype(np.float32)
    return jnp.asarray(np.repeat(sl, HEAD_DIM).reshape(HEADS // 2, 1, 2 * HEAD_DIM))


def attn_fwd(z, logc, dist, slopes):
    s = z.shape[0]
    nq = s // ATT_T
    t = ATT_T
    qcol, kcol, vcol = 2 * D // 128, 3 * D // 128, 4 * D // 128

    def body(q_ref, k_ref, v_ref, lc_ref, ds_ref, sl_ref, y_ref, lse_ref):
        qi = pl.program_id(1)
        lane = lax.broadcasted_iota(jnp.int32, (1, 128), 1)
        first = lane < HEAD_DIM
        q = q_ref[...] * (1.0 / math.sqrt(HEAD_DIM))
        q0 = jnp.where(first, q, 0.0).astype(BF16)
        q1 = jnp.where(first, 0.0, q).astype(BF16)
        sl0 = sl_ref[:, 0:1]
        sl1 = sl_ref[:, HEAD_DIM:HEAD_DIM + 1]

        def step(j, carry):
            m0, l0, m1, l1, acc = carry
            rows = pl.ds(pl.multiple_of(j * t, t), t)
            kj = k_ref[rows, :].astype(BF16)
            vj = v_ref[rows, :].astype(BF16)
            d = qi - j
            lc = lc_ref[d]
            ds = ds_ref[d]
            s0 = _dot(q0, kj, NT) + (lc - sl0 * ds)
            s1 = _dot(q1, kj, NT) + (lc - sl1 * ds)
            n0 = jnp.maximum(m0, jnp.max(s0, axis=-1, keepdims=True))
            n1 = jnp.maximum(m1, jnp.max(s1, axis=-1, keepdims=True))
            p0 = jnp.exp(s0 - n0)
            p1 = jnp.exp(s1 - n1)
            a0 = jnp.exp(m0 - n0)
            a1 = jnp.exp(m1 - n1)
            l0 = l0 * a0 + jnp.sum(p0, axis=-1, keepdims=True)
            l1 = l1 * a1 + jnp.sum(p1, axis=-1, keepdims=True)
            pv = jnp.where(first, _dot(p0.astype(BF16), vj), _dot(p1.astype(BF16), vj))
            acc = acc * jnp.where(first, a0, a1) + pv
            return n0, l0, n1, l1, acc

        col = jnp.full((t, 1), MASKED, F32)
        zero = jnp.zeros((t, 1), F32)
        m0, l0, m1, l1, acc = lax.fori_loop(0, qi + 1, step, (col, zero, col, zero, jnp.zeros((t, 128), F32)))
        y_ref[...] = (acc / jnp.where(first, l0, l1)).astype(BF16)
        lse_ref[...] = jnp.where(first, m0 + jnp.log(l0), m1 + jnp.log(l1))

    return pl.pallas_call(
        body, name="attn_fwd", grid=(HEADS // 2, nq),
        in_specs=[pl.BlockSpec((t, 128), lambda p, i: (i, qcol + p)),
                  pl.BlockSpec((s, 128), lambda p, i: (0, kcol + p)),
                  pl.BlockSpec((s, 128), lambda p, i: (0, vcol + p)),
                  _full((nq, t, t)), _full((nq, t, t)),
                  pl.BlockSpec((None, 1, 128), lambda p, i: (p, 0, 0))],
        out_specs=[pl.BlockSpec((t, 128), lambda p, i: (i, p)), pl.BlockSpec((t, 128), lambda p, i: (i, p))],
        out_shape=[jax.ShapeDtypeStruct((s, D), BF16), jax.ShapeDtypeStruct((s, D), F32)],
        compiler_params=_params("parallel", "parallel"),
    )(z, z, z, logc, dist, slopes)


def proj_merge(ya, yb, wa, wb, z, bg):
    s = ya.shape[0]
    tm = 512

    def body(ya_ref, yb_ref, wa_ref, wb_ref, ga_ref, gb_ref, bg_ref, mg_ref, pa_ref, pb_ref):
        pa = _dot(ya_ref[...], wa_ref[...])
        pb = _dot(yb_ref[...], wb_ref[...])
        sa = _sigmoid(ga_ref[...] + bg_ref[0:1, :])
        sb = _sigmoid(gb_ref[...] + bg_ref[1:2, :])
        mg_ref[...] = (sa * pa + sb * pb).astype(BF16)
        pa_ref[...] = pa.astype(BF16)
        pb_ref[...] = pb.astype(BF16)

    out = jax.ShapeDtypeStruct((s, D), BF16)
    return pl.pallas_call(
        body, name="proj_merge", grid=(s // tm,),
        in_specs=[_rows(tm, D), _rows(tm, D), _full((D, D)), _full((D, D)),
                  _rows(tm, D, 5), _rows(tm, D, 6), _full((2, D))],
        out_specs=[_rows(tm, D)] * 3, out_shape=[out] * 3, compiler_params=_params("parallel"),
    )(ya, yb, wa, wb, z, z, bg)


def out_norm(merged, w_out, x, g_post, g_fpre):
    s = x.shape[0]
    tm = 512

    def body(mg_ref, w_ref, x_ref, gp_ref, gf_ref, o_ref, x1_ref, h2_ref):
        o = _dot(mg_ref[...], w_ref[...])
        ohat, _ = _rms(o)
        x1 = x_ref[...] + ohat * gp_ref[...]
        x1hat, _ = _rms(x1)
        o_ref[...] = o
        x1_ref[...] = x1
        h2_ref[...] = (x1hat * gf_ref[...]).astype(BF16)

    return pl.pallas_call(
        body, name="out_norm", grid=(s // tm,),
        in_specs=[_rows(tm, D), _full((D, D)), _rows(tm, D), _full((1, D)), _full((1, D))],
        out_specs=[_rows(tm, D)] * 3,
        out_shape=[jax.ShapeDtypeStruct((s, D), F32), jax.ShapeDtypeStruct((s, D), F32),
                   jax.ShapeDtypeStruct((s, D), BF16)],
        compiler_params=_params("parallel"),
    )(merged, w_out, x, g_post, g_fpre)


def mm_ff1(h2, wg):
    s = h2.shape[0]
    tm = 1024

    def body(a_ref, b_ref, o_ref, r_ref):
        a = _dot(a_ref[...], b_ref[...])
        o_ref[...] = a
        r = jnp.maximum(a, 0.0)
        r_ref[...] = (r * r).astype(BF16)

    return pl.pallas_call(
        body, name="mm_ff1", grid=(s // tm, N_CHIPS),
        in_specs=[pl.BlockSpec((tm, D), lambda i, j: (i, 0)), pl.BlockSpec((None, D, D), lambda i, j: (j, 0, 0))],
        out_specs=[pl.BlockSpec((tm, D), lambda i, j: (i, j))] * 2,
        out_shape=[jax.ShapeDtypeStruct((s, D_FF), F32), jax.ShapeDtypeStruct((s, D_FF), BF16)],
        compiler_params=_params("parallel", "parallel"),
    )(h2, wg)


def ff2_loss(rl, w_ff2, x1, target, g_fpost):
    s = x1.shape[0]
    tm = 256

    def body(rl_ref, w_ref, x1_ref, t_ref, g_ref, dy_ref, df_ref, dg_ref, loss_ref):
        @pl.when(pl.program_id(0) == 0)
        def _():
            dg_ref[...] = jnp.zeros_like(dg_ref)
            loss_ref[...] = jnp.zeros_like(loss_ref)

        f = _dot(rl_ref[...], w_ref[...])
        fhat, r = _rms(f)
        err = x1_ref[...] + fhat * g_ref[...] - t_ref[...]
        loss_ref[...] += 0.5 * jnp.sum(jnp.mean(err * err, axis=-1, keepdims=True), axis=0, keepdims=True)
        dy = err * (1.0 / D)
        dy_ref[...] = dy
        dg_ref[...] += jnp.sum(dy * fhat, axis=0, keepdims=True)
        df_ref[...] = _rms_bwd(dy * g_ref[...], fhat, r).astype(BF16)

    return pl.pallas_call(
        body, name="ff2_loss", grid=(s // tm,),
        in_specs=[_rows(tm, D_FF), _full((D_FF, D)), _rows(tm, D), _rows(tm, D), _full((1, D))],
        out_specs=[_rows(tm, D), _rows(tm, D), _full((1, D)), _full((1, 1))],
        out_shape=[jax.ShapeDtypeStruct((s, D), F32), jax.ShapeDtypeStruct((s, D), BF16),
                   jax.ShapeDtypeStruct((1, D), F32), jax.ShapeDtypeStruct((1, 1), F32)],
        compiler_params=_params("arbitrary"),
    )(rl, w_ff2, x1, target, g_fpost)


def mm_tn(name, a, b, ta, tb, out_shape, out_spec):
    s = a.shape[0]

    def body(a_ref, b_ref, o_ref):
        o_ref[...] = _dot(a_ref[...], b_ref[...], TN)

    return pl.pallas_call(
        body, name=name, grid=(a.shape[1] // ta, b.shape[1] // tb),
        in_specs=[pl.BlockSpec((s, ta), lambda i, j: (0, i)), pl.BlockSpec((s, tb), lambda i, j: (0, j))],
        out_specs=out_spec, out_shape=jax.ShapeDtypeStruct(out_shape, F32),
        compiler_params=_params("parallel", "parallel"),
    )(a, b)


def mm_nt(name, a, w):
    s = a.shape[0]
    tm = 512

    def body(a_ref, w_ref, o_ref):
        o_ref[...] = _dot(a_ref[...], w_ref[...], NT)

    return pl.pallas_call(
        body, name=name, grid=(s // tm,), in_specs=[_rows(tm, D), _full((D, D))], out_specs=_rows(tm, D),
        out_shape=jax.ShapeDtypeStruct((s, D), F32), compiler_params=_params("parallel"),
    )(a, w)


def ff2_bwd(df, w_ff2, a):
    s = df.shape[0]
    tm = 1024

    def body(df_ref, w_ref, a_ref, da_ref):
        drl = _dot(df_ref[...], w_ref[...], NT)
        da_ref[...] = (drl * (2.0 * jnp.maximum(a_ref[...], 0.0))).astype(BF16)

    return pl.pallas_call(
        body, name="ff2_bwd", grid=(s // tm, D_FF // D),
        in_specs=[pl.BlockSpec((tm, D), lambda i, j: (i, 0)), pl.BlockSpec((D, D), lambda i, j: (j, 0)),
                  pl.BlockSpec((tm, D), lambda i, j: (i, j))],
        out_specs=pl.BlockSpec((tm, D), lambda i, j: (i, j)),
        out_shape=jax.ShapeDtypeStruct((s, D_FF), BF16), compiler_params=_params("parallel", "parallel"),
    )(df, w_ff2, a)


def ff1_bwd_norms(da, wg, x1, o, dy, g_fpre, g_post):
    s = x1.shape[0]
    tm = 512

    def body(da_ref, w_ref, x1_ref, o_ref, dy_ref, gf_ref, gp_ref, dx1_ref, do_ref, dgf_ref, dgp_ref, acc_ref):
        i, k = pl.program_id(0), pl.program_id(1)

        @pl.when((i == 0) & (k == 0))
        def _():
            dgf_ref[...] = jnp.zeros_like(dgf_ref)
            dgp_ref[...] = jnp.zeros_like(dgp_ref)

        part = _dot(da_ref[...], w_ref[...], NT)

        @pl.when(k == 0)
        def _():
            acc_ref[...] = part

        @pl.when(k > 0)
        def _():
            acc_ref[...] += part

        @pl.when(k == N_CHIPS - 1)
        def _():
            dh2 = acc_ref[...]
            x1hat, r2 = _rms(x1_ref[...])
            dgf_ref[...] += jnp.sum(dh2 * x1hat, axis=0, keepdims=True)
            dx1 = dy_ref[...] + _rms_bwd(dh2 * gf_ref[...], x1hat, r2)
            ohat, r1 = _rms(o_ref[...])
            dgp_ref[...] += jnp.sum(dx1 * ohat, axis=0, keepdims=True)
            dx1_ref[...] = dx1
            do_ref[...] = _rms_bwd(dx1 * gp_ref[...], ohat, r1).astype(BF16)

    row = pl.BlockSpec((tm, D), lambda i, k: (i, 0))
    vec = pl.BlockSpec((1, D), lambda i, k: (0, 0))
    return pl.pallas_call(
        body, name="ff1_bwd_norms", grid=(s // tm, N_CHIPS),
        in_specs=[pl.BlockSpec((tm, D), lambda i, k: (i, k)), pl.BlockSpec((None, D, D), lambda i, k: (k, 0, 0)),
                  row, row, row, vec, vec],
        out_specs=[row, row, vec, vec],
        out_shape=[jax.ShapeDtypeStruct((s, D), F32), jax.ShapeDtypeStruct((s, D), BF16),
                   jax.ShapeDtypeStruct((1, D), F32), jax.ShapeDtypeStruct((1, D), F32)],
        scratch_shapes=[pltpu.VMEM((tm, D), F32)], compiler_params=_params("arbitrary", "arbitrary"),
    )(da, wg, x1, o, dy, g_fpre, g_post)


def out_bwd_gates(do, w_out, pa, pb, z, bg):
    s = do.shape[0]
    tm = 512

    def body(do_ref, w_ref, pa_ref, pb_ref, ga_ref, gb_ref, bg_ref, dpa_ref, dpb_ref, dga_ref, dgb_ref, dbg_ref):
        @pl.when(pl.program_id(0) == 0)
        def _():
            dbg_ref[...] = jnp.zeros_like(dbg_ref)

        dm = _dot(do_ref[...], w_ref[...], NT)
        sa = _sigmoid(ga_ref[...] + bg_ref[0:1, :])
        sb = _sigmoid(gb_ref[...] + bg_ref[1:2, :])
        dpa_ref[...] = (dm * sa).astype(BF16)
        dpb_ref[...] = (dm * sb).astype(BF16)
        dga = dm * pa_ref[...].astype(F32) * (sa * (1.0 - sa))
        dgb = dm * pb_ref[...].astype(F32) * (sb * (1.0 - sb))
        dga_ref[...] = dga.astype(BF16)
        dgb_ref[...] = dgb.astype(BF16)
        dbg_ref[0:1, :] += jnp.sum(dga, axis=0, keepdims=True)
        dbg_ref[1:2, :] += jnp.sum(dgb, axis=0, keepdims=True)

    out = jax.ShapeDtypeStruct((s, D), BF16)
    return pl.pallas_call(
        body, name="out_bwd_gates", grid=(s // tm,),
        in_specs=[_rows(tm, D), _full((D, D)), _rows(tm, D), _rows(tm, D), _rows(tm, D, 5), _rows(tm, D, 6),
                  _full((2, D))],
        out_specs=[_rows(tm, D)] * 4 + [_full((2, D))],
        out_shape=[out] * 4 + [jax.ShapeDtypeStruct((2, D), F32)], compiler_params=_params("arbitrary"),
    )(do, w_out, pa, pb, z, z, bg)


def gating_bwd(z, dya, ln_g, ln_b, w_s, bs_t):
    s = z.shape[0]
    ones = functools.partial(jnp.ones, (8, CHUNK), BF16)

    def body(u_ref, v_ref, dya_ref, lg_ref, lb_ref, ws_ref, bst_ref,
             du_ref, dv_ref, dws_ref, dbs_ref, dlg_ref, dlb_ref, dvn_ref):
        ci = pl.program_id(0)

        @pl.when(ci == 0)
        def _():
            dws_ref[...] = jnp.zeros_like(dws_ref)
            dbs_ref[...] = jnp.zeros_like(dbs_ref)
            dlg_ref[...] = jnp.zeros_like(dlg_ref)
            dlb_ref[...] = jnp.zeros_like(dlb_ref)

        ug, dug_du = _gelu_and_grad(u_ref[...])
        vg, dvg_dv = _gelu_and_grad(v_ref[...])
        vhat, rstd = _layer_norm(vg)
        vn = (vhat * lg_ref[...] + lb_ref[...]).astype(BF16)
        dya = dya_ref[...]
        for g in range(GROUPS):
            cols = slice(g * CHUNK, (g + 1) * CHUNK)
            ws = _tril_ws(ws_ref, g)
            mixed = _dot(ws, vn[:, cols]) + bst_ref[:, g:g + 1]
            du_ref[:, cols] = (dya[:, cols] * mixed * dug_du[:, cols]).astype(BF16)
            dmix = (dya[:, cols] * ug[:, cols]).astype(BF16)
            dbs_ref[g] += _dot(ones(), dmix, NT)
            dws_ref[g] += _dot(dmix, vn[:, cols], NT)
            dvn_ref[:, cols] = _dot(ws, dmix, TN)
        dvn = dvn_ref[...]
        dlg_ref[...] += jnp.sum(dvn * vhat, axis=0, keepdims=True)
        dlb_ref[...] += jnp.sum(dvn, axis=0, keepdims=True)
        dvh = dvn * lg_ref[...]
        dvg = rstd * (dvh - jnp.mean(dvh, axis=-1, keepdims=True)
                      - vhat * jnp.mean(dvh * vhat, axis=-1, keepdims=True))
        dv_ref[...] = (dvg * dvg_dv).astype(BF16)

        @pl.when(ci == pl.num_programs(0) - 1)
        def _():
            r = lax.broadcasted_iota(jnp.int32, (CHUNK, CHUNK), 0)
            c = lax.broadcasted_iota(jnp.int32, (CHUNK, CHUNK), 1)
            for g in range(GROUPS):
                dws_ref[g] = jnp.where(c <= r, dws_ref[g], 0.0)

    out = jax.ShapeDtypeStruct((s, D), BF16)
    return pl.pallas_call(
        body, name="gating_bwd", grid=(s // CHUNK,),
        in_specs=[_rows(CHUNK, D, 0), _rows(CHUNK, D, 1), _rows(CHUNK, D), _full((1, D)), _full((1, D)),
                  _full((GROUPS, CHUNK, CHUNK)), _full((CHUNK, GROUPS))],
        out_specs=[_rows(CHUNK, D), _rows(CHUNK, D), _full((GROUPS, CHUNK, CHUNK)), _full((GROUPS, 8, CHUNK)),
                   _full((1, D)), _full((1, D))],
        out_shape=[out, out, jax.ShapeDtypeStruct((GROUPS, CHUNK, CHUNK), F32),
                   jax.ShapeDtypeStruct((GROUPS, 8, CHUNK), F32),
                   jax.ShapeDtypeStruct((1, D), F32), jax.ShapeDtypeStruct((1, D), F32)],
        scratch_shapes=[pltpu.VMEM((CHUNK, D), F32)], compiler_params=_params("arbitrary"),
    )(z, z, dya, ln_g, ln_b, w_s, bs_t)


def attn_bwd(z, yb, dyb, lse, logc, dist, slopes):
    s = z.shape[0]
    nq = s // ATT_T
    t = ATT_T
    qcol, kcol, vcol = 2 * D // 128, 3 * D // 128, 4 * D // 128
    scale = 1.0 / math.sqrt(HEAD_DIM)

    def body(q_ref, k_ref, v_ref, y_ref, dy_ref, lse_ref, lc_ref, ds_ref, sl_ref,
             dq_ref, dk_ref, dv_ref, dqacc_ref):
        j = pl.program_id(1)
        lane = lax.broadcasted_iota(jnp.int32, (1, 128), 1)
        first = lane < HEAD_DIM
        sl0 = sl_ref[:, 0:1]
        sl1 = sl_ref[:, HEAD_DIM:HEAD_DIM + 1]

        @pl.when(j == 0)
        def _():
            dqacc_ref[...] = jnp.zeros_like(dqacc_ref)

        kj = k_ref[...].astype(BF16)
        vj = v_ref[...].astype(BF16)
        k0 = jnp.where(first, kj, 0)
        k1 = jnp.where(first, 0, kj)
        v0 = jnp.where(first, vj, 0)
        v1 = jnp.where(first, 0, vj)

        def step(i, carry):
            dk, dv = carry
            rows = pl.ds(pl.multiple_of(i * t, t), t)
            q = (q_ref[rows, :] * scale).astype(BF16)
            do = dy_ref[rows, :]
            prod = do * y_ref[rows, :].astype(F32)
            dd0 = jnp.sum(jnp.where(first, prod, 0.0), axis=-1, keepdims=True)
            dd1 = jnp.sum(jnp.where(first, 0.0, prod), axis=-1, keepdims=True)
            do = do.astype(BF16)
            lse = lse_ref[rows, :]
            d = i - j
            lc = lc_ref[d]
            ds = ds_ref[d]
            p0 = jnp.exp(_dot(q, k0, NT) + (lc - sl0 * ds) - lse[:, 0:1])
            p1 = jnp.exp(_dot(q, k1, NT) + (lc - sl1 * ds) - lse[:, HEAD_DIM:HEAD_DIM + 1])
            g0 = (p0 * (_dot(do, v0, NT) - dd0)).astype(BF16)
            g1 = (p1 * (_dot(do, v1, NT) - dd1)).astype(BF16)
            dv = dv + jnp.where(first, _dot(p0.astype(BF16), do, TN), _dot(p1.astype(BF16), do, TN))
            dk = dk + jnp.where(first, _dot(g0, q, TN), _dot(g1, q, TN))
            dqacc_ref[rows, :] += jnp.where(first, _dot(g0, kj), _dot(g1, kj))
            return dk, dv

        zero = jnp.zeros((t, 128), F32)
        dk, dv = lax.fori_loop(j, nq, step, (zero, zero))
        dk_ref[...] = dk.astype(BF16)
        dv_ref[...] = dv.astype(BF16)

        @pl.when(j == nq - 1)
        def _():
            dq_ref[...] = (dqacc_ref[...] * scale).astype(BF16)

    colblock = lambda c: pl.BlockSpec((s, 128), lambda p, j: (0, c + p))
    blk = lambda c: pl.BlockSpec((t, 128), lambda p, j: (j, c + p))
    out = jax.ShapeDtypeStruct((s, D), BF16)
    return pl.pallas_call(
        body, name="attn_bwd", grid=(HEADS // 2, nq),
        in_specs=[colblock(qcol), blk(kcol), blk(vcol), colblock(0), colblock(0), colblock(0),
                  _full((nq, t, t)), _full((nq, t, t)), pl.BlockSpec((None, 1, 128), lambda p, j: (p, 0, 0))],
        out_specs=[colblock(0), blk(0), blk(0)], out_shape=[out] * 3,
        scratch_shapes=[pltpu.VMEM((s, 128), F32)], compiler_params=_params("arbitrary", "arbitrary"),
    )(z, z, z, yb, dyb, lse, logc, dist, slopes)


def in_bwd_norm(dz, wg, x, dx1, g_pre):
    s = x.shape[0]
    tm = 512

    def body(dz_ref, w_ref, x_ref, dx1_ref, g_ref, dx_ref, dg_ref, acc_ref):
        i, k = pl.program_id(0), pl.program_id(1)

        @pl.when((i == 0) & (k == 0))
        def _():
            dg_ref[...] = jnp.zeros_like(dg_ref)

        part = _dot(dz_ref[...], w_ref[...], NT)

        @pl.when(k == 0)
        def _():
            acc_ref[...] = part

        @pl.when(k > 0)
        def _():
            acc_ref[...] += part

        @pl.when(k == N_CHIPS - 1)
        def _():
            dh = acc_ref[...]
            xhat, r = _rms(x_ref[...])
            dg_ref[...] += jnp.sum(dh * xhat, axis=0, keepdims=True)
            dx_ref[...] = dx1_ref[...] + _rms_bwd(dh * g_ref[...], xhat, r)

    row = pl.BlockSpec((tm, D), lambda i, k: (i, 0))
    vec = pl.BlockSpec((1, D), lambda i, k: (0, 0))
    return pl.pallas_call(
        body, name="in_bwd_norm", grid=(s // tm, N_CHIPS),
        in_specs=[pl.BlockSpec((tm, IN_SHARD), lambda i, k: (i, k)),
                  pl.BlockSpec((None, D, IN_SHARD), lambda i, k: (k, 0, 0)), row, row, vec],
        out_specs=[row, vec],
        out_shape=[jax.ShapeDtypeStruct((s, D), F32), jax.ShapeDtypeStruct((1, D), F32)],
        scratch_shapes=[pltpu.VMEM((tm, D), F32)], compiler_params=_params("arbitrary", "arbitrary"),
    )(dz, wg, x, dx1, g_pre)


def _adamw_math(w, g, m, v):
    m = ADAM_B1 * m + (1.0 - ADAM_B1) * g
    v = ADAM_B2 * v + (1.0 - ADAM_B2) * (g * g)
    m_hat = m / (1.0 - ADAM_B1 ** ADAM_STEP)
    v_hat = v / (1.0 - ADAM_B2 ** ADAM_STEP)
    delta = -ADAM_LR * (m_hat / (jnp.sqrt(v_hat) + ADAM_EPS) + ADAM_WD * w)
    return delta, m, v


def adamw(name, w, g, m, v, tr):
    r, c = w.shape

    def body(w_ref, g_ref, m_ref, v_ref, d_ref, nm_ref, nv_ref):
        d_ref[...], nm_ref[...], nv_ref[...] = _adamw_math(w_ref[...], g_ref[...], m_ref[...], v_ref[...])

    out = jax.ShapeDtypeStruct((r, c), F32)
    return pl.pallas_call(
        body, name=name, grid=(r // tr,), in_specs=[_rows(tr, c)] * 4, out_specs=[_rows(tr, c)] * 3,
        out_shape=[out] * 3, compiler_params=_params("parallel"),
    )(w, g, m, v)


def add_halves(name, g, recv, c_idx, tr):
    n, h, c = recv.shape

    def body(c_ref, g_ref, r_ref, o_ref):
        o_ref[...] = (g_ref[...] + r_ref[...]).astype(BF16)

    nb = h // tr
    return pl.pallas_call(
        body, name=name,
        grid_spec=pltpu.PrefetchScalarGridSpec(
            num_scalar_prefetch=1, grid=(n, nb),
            in_specs=[pl.BlockSpec((None, tr, c), lambda k, i, c_ref: (k, c_ref[0] * nb + i, 0)),
                      pl.BlockSpec((None, tr, c), lambda k, i, c_ref: (k, i, 0))],
            out_specs=pl.BlockSpec((None, tr, c), lambda k, i, c_ref: (k, i, 0))),
        out_shape=jax.ShapeDtypeStruct((n, h, c), BF16), compiler_params=_params("parallel", "parallel"),
    )(c_idx, g, recv)


def sum_chips(name, parts, tr):
    n, h, c = parts.shape

    def body(p_ref, o_ref):
        acc = p_ref[0].astype(F32)
        for k in range(1, n):
            acc = acc + p_ref[k].astype(F32)
        o_ref[...] = acc

    return pl.pallas_call(
        body, name=name, grid=(h // tr,),
        in_specs=[pl.BlockSpec((n, tr, c), lambda i: (0, i, 0))], out_specs=_rows(tr, c),
        out_shape=jax.ShapeDtypeStruct((h, c), F32), compiler_params=_params("parallel"),
    )(parts)


ANY = pl.BlockSpec(memory_space=pl.ANY)


def _place():
    x, y, c = lax.axis_index("x"), lax.axis_index("y"), lax.axis_index("c")
    chips = [(1 - x, y), (x, 1 - y), (1 - x, 1 - y)]
    return x, y, c, chips


def gather_shards(shards):
    n = len(shards)

    def body(*refs):
        sh, out = refs[:n], refs[n:2 * n]
        send_sems, recv_sems, local_sems = refs[2 * n:]
        x, y, c, chips = _place()
        me = 2 * x + y
        sibling = (x, y, 1 - c)

        def half(w, chip, core):
            h = sh[w].shape[0] // 2
            return out[w].at[chip, pl.ds(core * h, h)]

        def copy(k, src, dst, to):
            return pltpu.make_async_remote_copy(src_ref=src, dst_ref=dst, send_sem=send_sems.at[k],
                                                recv_sem=recv_sems.at[k], device_id=to, device_id_type=MESH)

        local = [pltpu.make_async_copy(sh[w], out[w].at[me], local_sems.at[w]) for w in range(n)]
        for cp in local:
            cp.start()
        first, passed = [], []
        for w in range(n):
            h = sh[w].shape[0] // 2
            for j, (px, py) in enumerate(chips):
                first.append(copy(3 * w + j, sh[w].at[pl.ds(c * h, h)], half(w, me, c), (px, py, c)))
        for cp in first:
            cp.start()
        for w in range(n):
            for j, (px, py) in enumerate(chips):
                k = 3 * w + j
                got = half(w, 2 * px + py, c)
                copy(k, got, got, (px, py, c)).wait_recv()
                fwd = copy(3 * n + k, got, got, sibling)
                fwd.start()
                passed.append(fwd)
        for w in range(n):
            for j, (px, py) in enumerate(chips):
                got = half(w, 2 * px + py, 1 - c)
                copy(3 * n + 3 * w + j, got, got, sibling).wait_recv()
        for cp in first + passed:
            cp.wait_send()
        for cp in local:
            cp.wait()

    return pl.pallas_call(
        body, name="gather_shards", in_specs=[ANY] * n, out_specs=[ANY] * n,
        out_shape=[jax.ShapeDtypeStruct((N_CHIPS,) + a.shape, a.dtype) for a in shards],
        scratch_shapes=[pltpu.SemaphoreType.DMA((6 * n,)), pltpu.SemaphoreType.DMA((6 * n,)),
                        pltpu.SemaphoreType.DMA((n,))],
        compiler_params=pltpu.CompilerParams(has_side_effects=True),
    )(*shards)


def swap_halves(grads):
    n = len(grads)

    def body(*refs):
        g, out = refs[:n], refs[n:2 * n]
        send_sems, recv_sems = refs[2 * n:]
        x, y, c, _ = _place()
        copies = []
        for w in range(n):
            h = g[w].shape[1] // 2
            copies.append(pltpu.make_async_remote_copy(
                src_ref=g[w].at[:, pl.ds((1 - c) * h, h)], dst_ref=out[w], send_sem=send_sems.at[w],
                recv_sem=recv_sems.at[w], device_id=(x, y, 1 - c), device_id_type=MESH))
        for cp in copies:
            cp.start()
        for cp in copies:
            cp.wait()

    return pl.pallas_call(
        body, name="swap_halves", in_specs=[ANY] * n, out_specs=[ANY] * n,
        out_shape=[jax.ShapeDtypeStruct((a.shape[0], a.shape[1] // 2, a.shape[2]), a.dtype) for a in grads],
        scratch_shapes=[pltpu.SemaphoreType.DMA((n,)), pltpu.SemaphoreType.DMA((n,))],
        compiler_params=pltpu.CompilerParams(has_side_effects=True),
    )(*grads)


def scatter_chips(parts):
    n = len(parts)

    def body(*refs):
        p, out = refs[:n], refs[n:2 * n]
        send_sems, recv_sems, local_sems = refs[2 * n:]
        x, y, c, chips = _place()
        me = 2 * x + y
        local = [pltpu.make_async_copy(p[w].at[me], out[w].at[me], local_sems.at[w]) for w in range(n)]
        for cp in local:
            cp.start()
        copies = []
        for w in range(n):
            for j, (px, py) in enumerate(chips):
                copies.append(pltpu.make_async_remote_copy(
                    src_ref=p[w].at[2 * px + py], dst_ref=out[w].at[me], send_sem=send_sems.at[3 * w + j],
                    recv_sem=recv_sems.at[3 * w + j], device_id=(px, py, c), device_id_type=MESH))
        for cp in copies:
            cp.start()
        for w in range(n):
            for j, (px, py) in enumerate(chips):
                got = out[w].at[2 * px + py]
                pltpu.make_async_remote_copy(
                    src_ref=got, dst_ref=got, send_sem=send_sems.at[3 * w + j], recv_sem=recv_sems.at[3 * w + j],
                    device_id=(px, py, c), device_id_type=MESH).wait_recv()
        for cp in copies:
            cp.wait_send()
        for cp in local:
            cp.wait()

    return pl.pallas_call(
        body, name="scatter_chips", in_specs=[ANY] * n, out_specs=[ANY] * n,
        out_shape=[jax.ShapeDtypeStruct(a.shape, a.dtype) for a in parts],
        scratch_shapes=[pltpu.SemaphoreType.DMA((3 * n,)), pltpu.SemaphoreType.DMA((3 * n,)),
                        pltpu.SemaphoreType.DMA((n,))],
        compiler_params=pltpu.CompilerParams(has_side_effects=True),
    )(*parts)


def join_halves(halves):
    n = len(halves)

    def body(*refs):
        hv, out = refs[:n], refs[n:2 * n]
        send_sems, recv_sems, local_sems = refs[2 * n:]
        x, y, c, _ = _place()
        local, copies = [], []
        for w in range(n):
            h = hv[w].shape[0]
            local.append(pltpu.make_async_copy(hv[w], out[w].at[pl.ds(c * h, h)], local_sems.at[w]))
            copies.append(pltpu.make_async_remote_copy(
                src_ref=hv[w], dst_ref=out[w].at[pl.ds(c * h, h)], send_sem=send_sems.at[w],
                recv_sem=recv_sems.at[w], device_id=(x, y, 1 - c), device_id_type=MESH))
        for cp in local + copies:
            cp.start()
        for w in range(n):
            h = hv[w].shape[0]
            got = out[w].at[pl.ds((1 - c) * h, h)]
            pltpu.make_async_remote_copy(
                src_ref=got, dst_ref=got, send_sem=send_sems.at[w], recv_sem=recv_sems.at[w],
                device_id=(x, y, 1 - c), device_id_type=MESH).wait_recv()
        for cp in copies:
            cp.wait_send()
        for cp in local:
            cp.wait()

    return pl.pallas_call(
        body, name="join_halves", in_specs=[ANY] * n, out_specs=[ANY] * n,
        out_shape=[jax.ShapeDtypeStruct((2 * a.shape[0], a.shape[1]), a.dtype) for a in halves],
        scratch_shapes=[pltpu.SemaphoreType.DMA((n,)), pltpu.SemaphoreType.DMA((n,)),
                        pltpu.SemaphoreType.DMA((n,))],
        compiler_params=pltpu.CompilerParams(has_side_effects=True),
    )(*halves)


def allreduce_small(packed):
    r, c = packed.shape
    n_dev = 8

    def body(x_ref, all_ref, sum_ref, send_sems, recv_sems, local_sem):
        x, y, cc, chips = _place()
        me, sibling = (x, y, cc), (x, y, 1 - cc)

        def rows(px, py, pc):
            return all_ref.at[4 * px + 2 * py + pc]

        def copy(k, block, to, src=None):
            return pltpu.make_async_remote_copy(
                src_ref=rows(*block) if src is None else src, dst_ref=rows(*block), send_sem=send_sems.at[k],
                recv_sem=recv_sems.at[k], device_id=to, device_id_type=MESH)

        mine = pltpu.make_async_copy(x_ref, rows(*me), local_sem)
        mine.start()
        first = [copy(0, me, sibling, src=x_ref)]
        first += [copy(1 + j, me, (*chip, cc), src=x_ref) for j, chip in enumerate(chips)]
        for cp in first:
            cp.start()
        passed = [copy(4 + j, (*chip, cc), sibling) for j, chip in enumerate(chips)]
        for j, chip in enumerate(chips):
            copy(1 + j, (*chip, cc), me).wait_recv()
            passed[j].start()
        copy(0, sibling, me).wait_recv()
        for j, chip in enumerate(chips):
            copy(4 + j, (*chip, 1 - cc), me).wait_recv()
        for cp in first + passed:
            cp.wait_send()
        mine.wait()
        acc = all_ref[0]
        for k in range(1, n_dev):
            acc = acc + all_ref[k]
        sum_ref[...] = acc

    vm = pl.BlockSpec(memory_space=pltpu.VMEM)
    return pl.pallas_call(
        body, name="allreduce_small", in_specs=[vm], out_specs=[vm, vm],
        out_shape=[jax.ShapeDtypeStruct((n_dev, r, c), F32), jax.ShapeDtypeStruct((r, c), F32)],
        scratch_shapes=[pltpu.SemaphoreType.DMA((7,)), pltpu.SemaphoreType.DMA((7,)), pltpu.SemaphoreType.DMA],
        compiler_params=pltpu.CompilerParams(has_side_effects=True, vmem_limit_bytes=VMEM_LIMIT),
    )(packed)[1]


def local_step(x, target, vecs, w_s, bs_t, bg, wg_in, w_a, w_b, w_out, wg_ff1, w_ff2):
    g_pre, ln_g, ln_b, g_post, g_fpre, g_fpost = vecs
    s = x.shape[0]
    logc, dist = _attn_tables(s)
    slopes = _head_slopes()

    h = norm_pre(x, g_pre)
    z = mm_in(h, wg_in)
    ya = gating_fwd(z, ln_g, ln_b, w_s, bs_t)
    yb, lse = attn_fwd(z, logc, dist, slopes)
    merged, pa, pb = proj_merge(ya, yb, w_a, w_b, z, bg)
    o, x1, h2 = out_norm(merged, w_out, x, g_post, g_fpre)
    a, rl = mm_ff1(h2, wg_ff1)
    dy, df, d_gfpost, loss = ff2_loss(rl, w_ff2, x1, target, g_fpost)

    half_cols = pl.BlockSpec((D, D // 2), lambda i, j: (0, j))
    d_wff2 = mm_tn("dw_ff2", rl, df, D // 2, D, (D_FF, D), pl.BlockSpec((D // 2, D), lambda i, j: (i, 0)))
    da = ff2_bwd(df, w_ff2, a)
    d_wff1 = mm_tn("dw_ff1", h2, da, D, D // 2, (N_CHIPS, D, D),
                   pl.BlockSpec((None, D, D // 2), lambda i, j: (j // 2, 0, j % 2)))
    dx1, do, d_gfpre, d_gpost = ff1_bwd_norms(da, wg_ff1, x1, o, dy, g_fpre, g_post)
    d_wout = mm_tn("dw_out", merged, do, D, D // 2, (D, D), half_cols)
    dpa, dpb, dga, dgb, d_bg = out_bwd_gates(do, w_out, pa, pb, z, bg)
    d_wa = mm_tn("dw_a", ya, dpa, D, D // 2, (D, D), half_cols)
    d_wb = mm_tn("dw_b", yb, dpb, D, D // 2, (D, D), half_cols)
    dya = mm_nt("dy_a", dpa, w_a)
    dyb = mm_nt("dy_b", dpb, w_b)
    du, dv, d_ws, d_bs, d_lng, d_lnb = gating_bwd(z, dya, ln_g, ln_b, w_s, bs_t)
    dq, dk, dvb = attn_bwd(z, yb, dyb, lse, logc, dist, slopes)
    dz = jnp.concatenate([du, dv, dq, dk, dvb, dga, dgb], axis=1)
    half = IN_SHARD // 2
    d_win = mm_tn("dw_in", h, dz, D, half, (N_CHIPS, D, IN_SHARD),
                  pl.BlockSpec((None, D, half), lambda i, j: (j // 2, 0, j % 2)))
    dx, d_gpre = in_bwd_norm(dz, wg_in, x, dx1, g_pre)

    big = [d_win, d_wa.reshape(N_CHIPS, D // N_CHIPS, D), d_wb.reshape(N_CHIPS, D // N_CHIPS, D),
           d_wout.reshape(N_CHIPS, D // N_CHIPS, D), d_wff1, d_wff2.reshape(N_CHIPS, D, D)]
    small = dict(norm_mix_pre=d_gpre, b_gate=d_bg, ln_v_g=d_lng, ln_v_b=d_lnb, w_s=d_ws, b_s=d_bs[:, 0, :],
                 norm_mix_post=d_gpost, norm_ffn_pre=d_gfpre, norm_ffn_post=d_gfpost)
    return loss[0, 0], dx, big, small


BIG = ("w_in", "w_a_proj", "w_b_proj", "w_out", "w_ff1", "w_ff2")
SMALL = ("norm_mix_pre", "ln_v_g", "ln_v_b", "b_s", "norm_mix_post", "norm_ffn_pre", "norm_ffn_post", "w_s", "b_gate")
ORDER = ("norm_mix_pre", "w_in", "b_gate", "ln_v_g", "ln_v_b", "w_s", "b_s", "w_a_proj", "w_b_proj", "w_out",
         "norm_mix_post", "norm_ffn_pre", "w_ff1", "w_ff2", "norm_ffn_post")
PACK_ROWS = 144


def _pack_small(t):
    rows = [t[n].reshape(1, D) for n in SMALL[:7]] + [t["w_s"].reshape(128, D), t["b_gate"].reshape(2, D)]
    used = jnp.concatenate(rows, axis=0)
    return jnp.pad(used, ((0, PACK_ROWS - used.shape[0]), (0, 0)))


def _unpack_small(p, chip):
    out = {n: p[i:i + 1].reshape(1, D) for i, n in enumerate(SMALL[:7])}
    out["b_s"] = out["b_s"].reshape(1, GROUPS, CHUNK)
    out["w_s"] = p[7:135].reshape(1, GROUPS, CHUNK, CHUNK)
    out["b_gate"] = lax.dynamic_slice_in_dim(p[135:137], chip * (D // N_CHIPS), D // N_CHIPS, axis=1).reshape(1, 2, D // N_CHIPS)
    return out


def kernel(x, norm_mix_pre, w_in, b_gate, ln_v_g, ln_v_b, w_s, b_s, w_a_proj, w_b_proj, w_out, norm_mix_post, norm_ffn_pre, w_ff1, w_ff2, norm_ffn_post, loss_target, m_norm_mix_pre, m_w_in, m_b_gate, m_ln_v_g, m_ln_v_b, m_w_s, m_b_s, m_w_a_proj, m_w_b_proj, m_w_out, m_norm_mix_post, m_norm_ffn_pre, m_w_ff1, m_w_ff2, m_norm_ffn_post, v_norm_mix_pre, v_w_in, v_b_gate, v_ln_v_g, v_ln_v_b, v_w_s, v_b_s, v_w_a_proj, v_w_b_proj, v_w_out, v_norm_mix_post, v_norm_ffn_pre, v_w_ff1, v_w_ff2, v_norm_ffn_post):
    w = dict(norm_mix_pre=norm_mix_pre, w_in=w_in, b_gate=b_gate, ln_v_g=ln_v_g, ln_v_b=ln_v_b, w_s=w_s, b_s=b_s,
             w_a_proj=w_a_proj, w_b_proj=w_b_proj, w_out=w_out, norm_mix_post=norm_mix_post,
             norm_ffn_pre=norm_ffn_pre, w_ff1=w_ff1, w_ff2=w_ff2, norm_ffn_post=norm_ffn_post)
    m = dict(norm_mix_pre=m_norm_mix_pre, w_in=m_w_in, b_gate=m_b_gate, ln_v_g=m_ln_v_g, ln_v_b=m_ln_v_b, w_s=m_w_s,
             b_s=m_b_s, w_a_proj=m_w_a_proj, w_b_proj=m_w_b_proj, w_out=m_w_out, norm_mix_post=m_norm_mix_post,
             norm_ffn_pre=m_norm_ffn_pre, w_ff1=m_w_ff1, w_ff2=m_w_ff2, norm_ffn_post=m_norm_ffn_post)
    v = dict(norm_mix_pre=v_norm_mix_pre, w_in=v_w_in, b_gate=v_b_gate, ln_v_g=v_ln_v_g, ln_v_b=v_ln_v_b, w_s=v_w_s,
             b_s=v_b_s, w_a_proj=v_w_a_proj, w_b_proj=v_w_b_proj, w_out=v_w_out, norm_mix_post=v_norm_mix_post,
             norm_ffn_pre=v_norm_ffn_pre, w_ff1=v_w_ff1, w_ff2=v_w_ff2, norm_ffn_post=v_norm_ffn_post)
    chip = 2 * lax.axis_index("x") + lax.axis_index("y")
    core = lax.axis_index("c")

    shards = [w[n][0].astype(BF16) for n in BIG] + [jnp.pad(b_gate[0], ((0, 14), (0, 0)))]
    wg_in, wg_a, wg_b, wg_out, wg_ff1, wg_ff2, bg_all = gather_shards(shards)
    bg = jnp.transpose(bg_all[:, :2, :], (1, 0, 2)).reshape(2, D)

    vecs = (norm_mix_pre, ln_v_g, ln_v_b, norm_mix_post, norm_ffn_pre, norm_ffn_post)
    loss, dx, big, small = local_step(
        x[0], loss_target[0], vecs, w_s[0], b_s[0].T, bg, wg_in, wg_a.reshape(D, D), wg_b.reshape(D, D),
        wg_out.reshape(D, D), wg_ff1, wg_ff2.reshape(D_FF, D))
    loss = lax.psum(loss, ("x", "y", "c"))

    recv = swap_halves(big)
    c_idx = jnp.reshape(core, (1,)).astype(jnp.int32)
    parts = [add_halves("add_" + n, g, r, c_idx, min(r.shape[1], 256)) for n, g, r in zip(BIG, big, recv)]
    parts = scatter_chips(parts)
    halves = [sum_chips("sum_" + n, p, min(p.shape[1], 256)) for n, p in zip(BIG, parts)]
    grads = dict(zip(BIG, join_halves(halves)))

    grads_small = _unpack_small(allreduce_small(_pack_small(small)), chip)

    new = {}
    for n in BIG:
        shape = w[n].shape
        r, c = shape[1], shape[2]
        d, nm, nv = adamw("adamw_" + n, w[n][0], grads[n], m[n][0], v[n][0], min(r, 256))
        new[n] = (grads[n].reshape(shape), d.reshape(shape), nm.reshape(shape), nv.reshape(shape))
    full_bg = lambda t: jnp.zeros((2, D), F32).at[:, :D // N_CHIPS].set(t["b_gate"][0])
    packs = [_pack_small({**{n: t[n] for n in SMALL[:8]}, "b_gate": full_bg(t)}) for t in (w, m, v)]
    g_pack = _pack_small({**{n: grads_small[n] for n in SMALL[:8]},
                          "b_gate": jnp.zeros((2, D), F32).at[:, :D // N_CHIPS].set(grads_small["b_gate"][0])})
    d_p, m_p, v_p = adamw("adamw_small", packs[0], g_pack, packs[1], packs[2], PACK_ROWS)
    d_s, m_s, v_s = (_unpack_small(p, 0) for p in (d_p, m_p, v_p))
    for n in SMALL:
        new[n] = (grads_small[n].reshape(w[n].shape), d_s[n].reshape(w[n].shape), m_s[n].reshape(w[n].shape),
                  v_s[n].reshape(w[n].shape))

    outs = [loss, dx[None]]
    for i in range(4):
        outs += [new[n][i] for n in ORDER]
    return tuple(outs)
```

```python
import functools
import math

import numpy as np
import jax
import jax.numpy as jnp
from jax import lax
from jax.experimental import pallas as pl
from jax.experimental.pallas import tpu as pltpu

F32 = jnp.float32
BF16 = jnp.bfloat16
MESH = pl.DeviceIdType.MESH

D = 1024
EPS = 1e-6
CHUNK = 128
GROUPS = 8
HEADS = 16
HEAD_DIM = 64
ATT_T = 256
N_CHIPS = 4
D_FF = 4 * D
IN_COLS = 7 * D
IN_SHARD = IN_COLS // N_CHIPS
MASKED = -1e30
VMEM_LIMIT = 56 * 2 ** 20

ADAM_LR, ADAM_B1, ADAM_B2, ADAM_EPS, ADAM_WD, ADAM_STEP = 0.001, 0.9, 0.999, 1e-08, 0.01, 10

NN = (((1,), (0,)), ((), ()))
NT = (((1,), (1,)), ((), ()))
TN = (((0,), (0,)), ((), ()))


def _dot(a, b, dims=NN):
    return lax.dot_general(a, b, dims, preferred_element_type=F32)


def _params(*sem):
    return pltpu.CompilerParams(dimension_semantics=sem or None, vmem_limit_bytes=VMEM_LIMIT)


def _rows(tr, c, col=0):
    return pl.BlockSpec((tr, c), lambda i: (i, col))


def _full(shape):
    n = len(shape)
    return pl.BlockSpec(shape, lambda *_: (0,) * n)


def _gelu(x):
    k = math.sqrt(2.0 / math.pi)
    return 0.5 * x * (1.0 + jnp.tanh(k * (x + 0.044715 * x * x * x)))


def _gelu_and_grad(x):
    k = math.sqrt(2.0 / math.pi)
    t = jnp.tanh(k * (x + 0.044715 * x * x * x))
    g = 0.5 * x * (1.0 + t)
    dg = 0.5 * (1.0 + t) + 0.5 * x * (1.0 - t * t) * (k * (1.0 + 3.0 * 0.044715 * x * x))
    return g, dg


def _sigmoid(x):
    return 1.0 / (1.0 + jnp.exp(-x))


def _rms(x):
    r = lax.rsqrt(jnp.mean(x * x, axis=-1, keepdims=True) + EPS)
    return x * r, r


def _rms_bwd(dn, xhat, r):
    return r * (dn - xhat * jnp.mean(dn * xhat, axis=-1, keepdims=True))


def norm_pre(x, g):
    s = x.shape[0]
    tr = 512

    def body(x_ref, g_ref, h_ref):
        xhat, _ = _rms(x_ref[...])
        h_ref[...] = (xhat * g_ref[...]).astype(BF16)

    return pl.pallas_call(
        body, name="norm_pre", grid=(s // tr,),
        in_specs=[_rows(tr, D), _full((1, D))], out_specs=_rows(tr, D),
        out_shape=jax.ShapeDtypeStruct((s, D), BF16), compiler_params=_params("parallel"),
    )(x, g)


def mm_in(h, wg):
    s = h.shape[0]
    tm, tn = 1024, IN_SHARD // 2
    per = IN_SHARD // tn

    def body(a_ref, b_ref, o_ref):
        o_ref[...] = _dot(a_ref[...], b_ref[...])

    return pl.pallas_call(
        body, name="mm_in", grid=(s // tm, IN_COLS // tn),
        in_specs=[pl.BlockSpec((tm, D), lambda i, j: (i, 0)),
                  pl.BlockSpec((None, D, tn), lambda i, j: (j // per, 0, j % per))],
        out_specs=pl.BlockSpec((tm, tn), lambda i, j: (i, j)),
        out_shape=jax.ShapeDtypeStruct((s, IN_COLS), F32), compiler_params=_params("parallel", "parallel"),
    )(h, wg)


def _tril_ws(ws_ref, g):
    r = lax.broadcasted_iota(jnp.int32, (CHUNK, CHUNK), 0)
    c = lax.broadcasted_iota(jnp.int32, (CHUNK, CHUNK), 1)
    return jnp.where(c <= r, ws_ref[g], 0.0).astype(BF16)


def _layer_norm(v):
    mu = jnp.mean(v, axis=-1, keepdims=True)
    d = v - mu
    rstd = lax.rsqrt(jnp.mean(d * d, axis=-1, keepdims=True) + EPS)
    return d * rstd, rstd


def gating_fwd(z, ln_g, ln_b, w_s, bs_t):
    s = z.shape[0]

    def body(u_ref, v_ref, lg_ref, lb_ref, ws_ref, bst_ref, ya_ref):
        ug = _gelu(u_ref[...])
        vhat, _ = _layer_norm(_gelu(v_ref[...]))
        vn = (vhat * lg_ref[...] + lb_ref[...]).astype(BF16)
        for g in range(GROUPS):
            cols = slice(g * CHUNK, (g + 1) * CHUNK)
            mixed = _dot(_tril_ws(ws_ref, g), vn[:, cols]) + bst_ref[:, g:g + 1]
            ya_ref[:, cols] = (ug[:, cols] * mixed).astype(BF16)

    return pl.pallas_call(
        body, name="gating_fwd", grid=(s // CHUNK,),
        in_specs=[_rows(CHUNK, D, 0), _rows(CHUNK, D, 1), _full((1, D)), _full((1, D)),
                  _full((GROUPS, CHUNK, CHUNK)), _full((CHUNK, GROUPS))],
        out_specs=_rows(CHUNK, D), out_shape=jax.ShapeDtypeStruct((s, D), BF16),
        compiler_params=_params("parallel"),
    )(z, z, ln_g, ln_b, w_s, bs_t)


def _attn_tables(s):
    nd = s // ATT_T
    r = np.arange(ATT_T)[None, :, None]
    c = np.arange(ATT_T)[None, None, :]
    delta = np.arange(nd)[:, None, None] * ATT_T + r - c
    count = np.zeros(delta.shape, np.int64)
    for window, dilation in ((128, 1), (512, 4), (2048, 16)):
        count += (delta >= 0) & (delta % dilation == 0) & (delta <= window)
    logc = np.where(count > 0, np.log(np.maximum(count, 1)), MASKED)
    return jnp.asarray(logc, F32), jnp.asarray(delta, F32)


def _head_slopes():
    sl = np.exp2(-8.0 * np.arange(1, HEADS + 1, dtype=np.float32) / HEADS).astype(np.float32)
    return jnp.asarray(np.repeat(sl, HEAD_DIM).reshape(HEADS // 2, 1, 2 * HEAD_DIM))


def attn_fwd(z, logc, dist, slopes):
    s = z.shape[0]
    nq = s // ATT_T
    t = ATT_T
    qcol, kcol, vcol = 2 * D // 128, 3 * D // 128, 4 * D // 128

    def body(q_ref, k_ref, v_ref, lc_ref, ds_ref, sl_ref, y_ref, lse_ref):
        qi = pl.program_id(1)
        lane = lax.broadcasted_iota(jnp.int32, (1, 128), 1)
        first = lane < HEAD_DIM
        q = q_ref[...] * (1.0 / math.sqrt(HEAD_DIM))
        q0 = jnp.where(first, q, 0.0).astype(BF16)
        q1 = jnp.where(first, 0.0, q).astype(BF16)
        sl0 = sl_ref[:, 0:1]
        sl1 = sl_ref[:, HEAD_DIM:HEAD_DIM + 1]

        def step(j, carry):
            m0, l0, m1, l1, acc = carry
            rows = pl.ds(pl.multiple_of(j * t, t), t)
            kj = k_ref[rows, :].astype(BF16)
            vj = v_ref[rows, :].astype(BF16)
            d = qi - j
            lc = lc_ref[d]
            ds = ds_ref[d]
            s0 = _dot(q0, kj, NT) + (lc - sl0 * ds)
            s1 = _dot(q1, kj, NT) + (lc - sl1 * ds)
            n0 = jnp.maximum(m0, jnp.max(s0, axis=-1, keepdims=True))
            n1 = jnp.maximum(m1, jnp.max(s1, axis=-1, keepdims=True))
            p0 = jnp.exp(s0 - n0)
            p1 = jnp.exp(s1 - n1)
            a0 = jnp.exp(m0 - n0)
            a1 = jnp.exp(m1 - n1)
            l0 = l0 * a0 + jnp.sum(p0, axis=-1, keepdims=True)
            l1 = l1 * a1 + jnp.sum(p1, axis=-1, keepdims=True)
            pv = jnp.where(first, _dot(p0.astype(BF16), vj), _dot(p1.astype(BF16), vj))
            acc = acc * jnp.where(first, a0, a1) + pv
            return n0, l0, n1, l1, acc

        col = jnp.full((t, 1), MASKED, F32)
        zero = jnp.zeros((t, 1), F32)
        m0, l0, m1, l1, acc = lax.fori_loop(0, qi + 1, step, (col, zero, col, zero, jnp.zeros((t, 128), F32)))
        y_ref[...] = (acc / jnp.where(first, l0, l1)).astype(BF16)
        lse_ref[...] = jnp.where(first, m0 + jnp.log(l0), m1 + jnp.log(l1))

    return pl.pallas_call(
        body, name="attn_fwd", grid=(HEADS // 2, nq),
        in_specs=[pl.BlockSpec((t, 128), lambda p, i: (i, qcol + p)),
                  pl.BlockSpec((s, 128), lambda p, i: (0, kcol + p)),
                  pl.BlockSpec((s, 128), lambda p, i: (0, vcol + p)),
                  _full((nq, t, t)), _full((nq, t, t)),
                  pl.BlockSpec((None, 1, 128), lambda p, i: (p, 0, 0))],
        out_specs=[pl.BlockSpec((t, 128), lambda p, i: (i, p)), pl.BlockSpec((t, 128), lambda p, i: (i, p))],
        out_shape=[jax.ShapeDtypeStruct((s, D), BF16), jax.ShapeDtypeStruct((s, D), F32)],
        compiler_params=_params("parallel", "parallel"),
    )(z, z, z, logc, dist, slopes)


def proj_merge(ya, yb, wa, wb, z, bg):
    s = ya.shape[0]
    tm = 512

    def body(ya_ref, yb_ref, wa_ref, wb_ref, ga_ref, gb_ref, bg_ref, mg_ref, pa_ref, pb_ref):
        pa = _dot(ya_ref[...], wa_ref[...])
        pb = _dot(yb_ref[...], wb_ref[...])
        sa = _sigmoid(ga_ref[...] + bg_ref[0:1, :])
        sb = _sigmoid(gb_ref[...] + bg_ref[1:2, :])
        mg_ref[...] = (sa * pa + sb * pb).astype(BF16)
        pa_ref[...] = pa.astype(BF16)
        pb_ref[...] = pb.astype(BF16)

    out = jax.ShapeDtypeStruct((s, D), BF16)
    return pl.pallas_call(
        body, name="proj_merge", grid=(s // tm,),
        in_specs=[_rows(tm, D), _rows(tm, D), _full((D, D)), _full((D, D)),
                  _rows(tm, D, 5), _rows(tm, D, 6), _full((2, D))],
        out_specs=[_rows(tm, D)] * 3, out_shape=[out] * 3, compiler_params=_params("parallel"),
    )(ya, yb, wa, wb, z, z, bg)


def out_norm(merged, w_out, x, g_post, g_fpre):
    s = x.shape[0]
    tm = 512

    def body(mg_ref, w_ref, x_ref, gp_ref, gf_ref, o_ref, x1_ref, h2_ref):
        o = _dot(mg_ref[...], w_ref[...])
        ohat, _ = _rms(o)
        x1 = x_ref[...] + ohat * gp_ref[...]
        x1hat, _ = _rms(x1)
        o_ref[...] = o
        x1_ref[...] = x1
        h2_ref[...] = (x1hat * gf_ref[...]).astype(BF16)

    return pl.pallas_call(
        body, name="out_norm", grid=(s // tm,),
        in_specs=[_rows(tm, D), _full((D, D)), _rows(tm, D), _full((1, D)), _full((1, D))],
        out_specs=[_rows(tm, D)] * 3,
        out_shape=[jax.ShapeDtypeStruct((s, D), F32), jax.ShapeDtypeStruct((s, D), F32),
                   jax.ShapeDtypeStruct((s, D), BF16)],
        compiler_params=_params("parallel"),
    )(merged, w_out, x, g_post, g_fpre)


def mm_ff1(h2, wg):
    s = h2.shape[0]
    tm = 1024

    def body(a_ref, b_ref, o_ref, r_ref):
        a = _dot(a_ref[...], b_ref[...])
        o_ref[...] = a
        r = jnp.maximum(a, 0.0)
        r_ref[...] = (r * r).astype(BF16)

    return pl.pallas_call(
        body, name="mm_ff1", grid=(s // tm, N_CHIPS),
        in_specs=[pl.BlockSpec((tm, D), lambda i, j: (i, 0)), pl.BlockSpec((None, D, D), lambda i, j: (j, 0, 0))],
        out_specs=[pl.BlockSpec((tm, D), lambda i, j: (i, j))] * 2,
        out_shape=[jax.ShapeDtypeStruct((s, D_FF), F32), jax.ShapeDtypeStruct((s, D_FF), BF16)],
        compiler_params=_params("parallel", "parallel"),
    )(h2, wg)


def ff2_loss(rl, w_ff2, x1, target, g_fpost):
    s = x1.shape[0]
    tm = 256

    def body(rl_ref, w_ref, x1_ref, t_ref, g_ref, dy_ref, df_ref, dg_ref, loss_ref):
        @pl.when(pl.program_id(0) == 0)
        def _():
            dg_ref[...] = jnp.zeros_like(dg_ref)
            loss_ref[...] = jnp.zeros_like(loss_ref)

        f = _dot(rl_ref[...], w_ref[...])
        fhat, r = _rms(f)
        err = x1_ref[...] + fhat * g_ref[...] - t_ref[...]
        loss_ref[...] += 0.5 * jnp.sum(jnp.mean(err * err, axis=-1, keepdims=True), axis=0, keepdims=True)
        dy = err * (1.0 / D)
        dy_ref[...] = dy
        dg_ref[...] += jnp.sum(dy * fhat, axis=0, keepdims=True)
        df_ref[...] = _rms_bwd(dy * g_ref[...], fhat, r).astype(BF16)

    return pl.pallas_call(
        body, name="ff2_loss", grid=(s // tm,),
        in_specs=[_rows(tm, D_FF), _full((D_FF, D)), _rows(tm, D), _rows(tm, D), _full((1, D))],
        out_specs=[_rows(tm, D), _rows(tm, D), _full((1, D)), _full((1, 1))],
        out_shape=[jax.ShapeDtypeStruct((s, D), F32), jax.ShapeDtypeStruct((s, D), BF16),
                   jax.ShapeDtypeStruct((1, D), F32), jax.ShapeDtypeStruct((1, 1), F32)],
        compiler_params=_params("arbitrary"),
    )(rl, w_ff2, x1, target, g_fpost)


def mm_tn(name, a, b, ta, tb, out_shape, out_spec):
    s = a.shape[0]

    def body(a_ref, b_ref, o_ref):
        o_ref[...] = _dot(a_ref[...], b_ref[...], TN)

    return pl.pallas_call(
        body, name=name, grid=(a.shape[1] // ta, b.shape[1] // tb),
        in_specs=[pl.BlockSpec((s, ta), lambda i, j: (0, i)), pl.BlockSpec((s, tb), lambda i, j: (0, j))],
        out_specs=out_spec, out_shape=jax.ShapeDtypeStruct(out_shape, F32),
        compiler_params=_params("parallel", "parallel"),
    )(a, b)


def mm_nt(name, a, w):
    s = a.shape[0]
    tm = 512

    def body(a_ref, w_ref, o_ref):
        o_ref[...] = _dot(a_ref[...], w_ref[...], NT)

    return pl.pallas_call(
        body, name=name, grid=(s // tm,), in_specs=[_rows(tm, D), _full((D, D))], out_specs=_rows(tm, D),
        out_shape=jax.ShapeDtypeStruct((s, D), F32), compiler_params=_params("parallel"),
    )(a, w)


def ff2_bwd(df, w_ff2, a):
    s = df.shape[0]
    tm = 1024

    def body(df_ref, w_ref, a_ref, da_ref):
        drl = _dot(df_ref[...], w_ref[...], NT)
        da_ref[...] = (drl * (2.0 * jnp.maximum(a_ref[...], 0.0))).astype(BF16)

    return pl.pallas_call(
        body, name="ff2_bwd", grid=(s // tm, D_FF // D),
        in_specs=[pl.BlockSpec((tm, D), lambda i, j: (i, 0)), pl.BlockSpec((D, D), lambda i, j: (j, 0)),
                  pl.BlockSpec((tm, D), lambda i, j: (i, j))],
        out_specs=pl.BlockSpec((tm, D), lambda i, j: (i, j)),
        out_shape=jax.ShapeDtypeStruct((s, D_FF), BF16), compiler_params=_params("parallel", "parallel"),
    )(df, w_ff2, a)


def ff1_bwd_norms(da, wg, x1, o, dy, g_fpre, g_post):
    s = x1.shape[0]
    tm = 512

    def body(da_ref, w_ref, x1_ref, o_ref, dy_ref, gf_ref, gp_ref, dx1_ref, do_ref, dgf_ref, dgp_ref, acc_ref):
        i, k = pl.program_id(0), pl.program_id(1)

        @pl.when((i == 0) & (k == 0))
        def _():
            dgf_ref[...] = jnp.zeros_like(dgf_ref)
            dgp_ref[...] = jnp.zeros_like(dgp_ref)

        part = _dot(da_ref[...], w_ref[...], NT)

        @pl.when(k == 0)
        def _():
            acc_ref[...] = part

        @pl.when(k > 0)
        def _():
            acc_ref[...] += part

        @pl.when(k == N_CHIPS - 1)
        def _():
            dh2 = acc_ref[...]
            x1hat, r2 = _rms(x1_ref[...])
            dgf_ref[...] += jnp.sum(dh2 * x1hat, axis=0, keepdims=True)
            dx1 = dy_ref[...] + _rms_bwd(dh2 * gf_ref[...], x1hat, r2)
            ohat, r1 = _rms(o_ref[...])
            dgp_ref[...] += jnp.sum(dx1 * ohat, axis=0, keepdims=True)
            dx1_ref[...] = dx1
            do_ref[...] = _rms_bwd(dx1 * gp_ref[...], ohat, r1).astype(BF16)

    row = pl.BlockSpec((tm, D), lambda i, k: (i, 0))
    vec = pl.BlockSpec((1, D), lambda i, k: (0, 0))
    return pl.pallas_call(
        body, name="ff1_bwd_norms", grid=(s // tm, N_CHIPS),
        in_specs=[pl.BlockSpec((tm, D), lambda i, k: (i, k)), pl.BlockSpec((None, D, D), lambda i, k: (k, 0, 0)),
                  row, row, row, vec, vec],
        out_specs=[row, row, vec, vec],
        out_shape=[jax.ShapeDtypeStruct((s, D), F32), jax.ShapeDtypeStruct((s, D), BF16),
                   jax.ShapeDtypeStruct((1, D), F32), jax.ShapeDtypeStruct((1, D), F32)],
        scratch_shapes=[pltpu.VMEM((tm, D), F32)], compiler_params=_params("arbitrary", "arbitrary"),
    )(da, wg, x1, o, dy, g_fpre, g_post)


def out_bwd_gates(do, w_out, pa, pb, z, bg):
    s = do.shape[0]
    tm = 512

    def body(do_ref, w_ref, pa_ref, pb_ref, ga_ref, gb_ref, bg_ref, dpa_ref, dpb_ref, dga_ref, dgb_ref, dbg_ref):
        @pl.when(pl.program_id(0) == 0)
        def _():
            dbg_ref[...] = jnp.zeros_like(dbg_ref)

        dm = _dot(do_ref[...], w_ref[...], NT)
        sa = _sigmoid(ga_ref[...] + bg_ref[0:1, :])
        sb = _sigmoid(gb_ref[...] + bg_ref[1:2, :])
        dpa_ref[...] = (dm * sa).astype(BF16)
        dpb_ref[...] = (dm * sb).astype(BF16)
        dga = dm * pa_ref[...].astype(F32) * (sa * (1.0 - sa))
        dgb = dm * pb_ref[...].astype(F32) * (sb * (1.0 - sb))
        dga_ref[...] = dga.astype(BF16)
        dgb_ref[...] = dgb.astype(BF16)
        dbg_ref[0:1, :] += jnp.sum(dga, axis=0, keepdims=True)
        dbg_ref[1:2, :] += jnp.sum(dgb, axis=0, keepdims=True)

    out = jax.ShapeDtypeStruct((s, D), BF16)
    return pl.pallas_call(
        body, name="out_bwd_gates", grid=(s // tm,),
        in_specs=[_rows(tm, D), _full((D, D)), _rows(tm, D), _rows(tm, D), _rows(tm, D, 5), _rows(tm, D, 6),
                  _full((2, D))],
        out_specs=[_rows(tm, D)] * 4 + [_full((2, D))],
        out_shape=[out] * 4 + [jax.ShapeDtypeStruct((2, D), F32)], compiler_params=_params("arbitrary"),
    )(do, w_out, pa, pb, z, z, bg)


def gating_bwd(z, dya, ln_g, ln_b, w_s, bs_t):
    s = z.shape[0]
    ones = functools.partial(jnp.ones, (8, CHUNK), BF16)

    def body(u_ref, v_ref, dya_ref, lg_ref, lb_ref, ws_ref, bst_ref,
             du_ref, dv_ref, dws_ref, dbs_ref, dlg_ref, dlb_ref, dvn_ref):
        ci = pl.program_id(0)

        @pl.when(ci == 0)
        def _():
            dws_ref[...] = jnp.zeros_like(dws_ref)
            dbs_ref[...] = jnp.zeros_like(dbs_ref)
            dlg_ref[...] = jnp.zeros_like(dlg_ref)
            dlb_ref[...] = jnp.zeros_like(dlb_ref)

        ug, dug_du = _gelu_and_grad(u_ref[...])
        vg, dvg_dv = _gelu_and_grad(v_ref[...])
        vhat, rstd = _layer_norm(vg)
        vn = (vhat * lg_ref[...] + lb_ref[...]).astype(BF16)
        dya = dya_ref[...]
        for g in range(GROUPS):
            cols = slice(g * CHUNK, (g + 1) * CHUNK)
            ws = _tril_ws(ws_ref, g)
            mixed = _dot(ws, vn[:, cols]) + bst_ref[:, g:g + 1]
            du_ref[:, cols] = (dya[:, cols] * mixed * dug_du[:, cols]).astype(BF16)
            dmix = (dya[:, cols] * ug[:, cols]).astype(BF16)
            dbs_ref[g] += _dot(ones(), dmix, NT)
            dws_ref[g] += _dot(dmix, vn[:, cols], NT)
            dvn_ref[:, cols] = _dot(ws, dmix, TN)
        dvn = dvn_ref[...]
        dlg_ref[...] += jnp.sum(dvn * vhat, axis=0, keepdims=True)
        dlb_ref[...] += jnp.sum(dvn, axis=0, keepdims=True)
        dvh = dvn * lg_ref[...]
        dvg = rstd * (dvh - jnp.mean(dvh, axis=-1, keepdims=True)
                      - vhat * jnp.mean(dvh * vhat, axis=-1, keepdims=True))
        dv_ref[...] = (dvg * dvg_dv).astype(BF16)

        @pl.when(ci == pl.num_programs(0) - 1)
        def _():
            r = lax.broadcasted_iota(jnp.int32, (CHUNK, CHUNK), 0)
            c = lax.broadcasted_iota(jnp.int32, (CHUNK, CHUNK), 1)
            for g in range(GROUPS):
                dws_ref[g] = jnp.where(c <= r, dws_ref[g], 0.0)

    out = jax.ShapeDtypeStruct((s, D), BF16)
    return pl.pallas_call(
        body, name="gating_bwd", grid=(s // CHUNK,),
        in_specs=[_rows(CHUNK, D, 0), _rows(CHUNK, D, 1), _rows(CHUNK, D), _full((1, D)), _full((1, D)),
                  _full((GROUPS, CHUNK, CHUNK)), _full((CHUNK, GROUPS))],
        out_specs=[_rows(CHUNK, D), _rows(CHUNK, D), _full((GROUPS, CHUNK, CHUNK)), _full((GROUPS, 8, CHUNK)),
                   _full((1, D)), _full((1, D))],
        out_shape=[out, out, jax.ShapeDtypeStruct((GROUPS, CHUNK, CHUNK), F32),
                   jax.ShapeDtypeStruct((GROUPS, 8, CHUNK), F32),
                   jax.ShapeDtypeStruct((1, D), F32), jax.ShapeDtypeStruct((1, D), F32)],
        scratch_shapes=[pltpu.VMEM((CHUNK, D), F32)], compiler_params=_params("arbitrary"),
    )(z, z, dya, ln_g, ln_b, w_s, bs_t)


def attn_bwd(z, yb, dyb, lse, logc, dist, slopes):
    s = z.shape[0]
    nq = s // ATT_T
    t = ATT_T
    qcol, kcol, vcol = 2 * D // 128, 3 * D // 128, 4 * D // 128
    scale = 1.0 / math.sqrt(HEAD_DIM)

    def body(q_ref, k_ref, v_ref, y_ref, dy_ref, lse_ref, lc_ref, ds_ref, sl_ref,
             dq_ref, dk_ref, dv_ref, dqacc_ref):
        j = pl.program_id(1)
        lane = lax.broadcasted_iota(jnp.int32, (1, 128), 1)
        first = lane < HEAD_DIM
        sl0 = sl_ref[:, 0:1]
        sl1 = sl_ref[:, HEAD_DIM:HEAD_DIM + 1]

        @pl.when(j == 0)
        def _():
            dqacc_ref[...] = jnp.zeros_like(dqacc_ref)

        kj = k_ref[...].astype(BF16)
        vj = v_ref[...].astype(BF16)
        k0 = jnp.where(first, kj, 0)
        k1 = jnp.where(first, 0, kj)
        v0 = jnp.where(first, vj, 0)
        v1 = jnp.where(first, 0, vj)

        def step(i, carry):
            dk, dv = carry
            rows = pl.ds(pl.multiple_of(i * t, t), t)
            q = (q_ref[rows, :] * scale).astype(BF16)
            do = dy_ref[rows, :]
            prod = do * y_ref[rows, :].astype(F32)
            dd0 = jnp.sum(jnp.where(first, prod, 0.0), axis=-1, keepdims=True)
            dd1 = jnp.sum(jnp.where(first, 0.0, prod), axis=-1, keepdims=True)
            do = do.astype(BF16)
            lse = lse_ref[rows, :]
            d = i - j
            lc = lc_ref[d]
            ds = ds_ref[d]
            p0 = jnp.exp(_dot(q, k0, NT) + (lc - sl0 * ds) - lse[:, 0:1])
            p1 = jnp.exp(_dot(q, k1, NT) + (lc - sl1 * ds) - lse[:, HEAD_DIM:HEAD_DIM + 1])
            g0 = (p0 * (_dot(do, v0, NT) - dd0)).astype(BF16)
            g1 = (p1 * (_dot(do, v1, NT) - dd1)).astype(BF16)
            dv = dv + jnp.where(first, _dot(p0.astype(BF16), do, TN), _dot(p1.astype(BF16), do, TN))
            dk = dk + jnp.where(first, _dot(g0, q, TN), _dot(g1, q, TN))
            dqacc_ref[rows, :] += jnp.where(first, _dot(g0, kj), _dot(g1, kj))
            return dk, dv

        zero = jnp.zeros((t, 128), F32)
        dk, dv = lax.fori_loop(j, nq, step, (zero, zero))
        dk_ref[...] = dk.astype(BF16)
        dv_ref[...] = dv.astype(BF16)

        @pl.when(j == nq - 1)
        def _():
            dq_ref[...] = (dqacc_ref[...] * scale).astype(BF16)

    colblock = lambda c: pl.BlockSpec((s, 128), lambda p, j: (0, c + p))
    blk = lambda c: pl.BlockSpec((t, 128), lambda p, j: (j, c + p))
    out = jax.ShapeDtypeStruct((s, D), BF16)
    return pl.pallas_call(
        body, name="attn_bwd", grid=(HEADS // 2, nq),
        in_specs=[colblock(qcol), blk(kcol), blk(vcol), colblock(0), colblock(0), colblock(0),
                  _full((nq, t, t)), _full((nq, t, t)), pl.BlockSpec((None, 1, 128), lambda p, j: (p, 0, 0))],
        out_specs=[colblock(0), blk(0), blk(0)], out_shape=[out] * 3,
        scratch_shapes=[pltpu.VMEM((s, 128), F32)], compiler_params=_params("arbitrary", "arbitrary"),
    )(z, z, z, yb, dyb, lse, logc, dist, slopes)


def in_bwd_norm(dz, wg, x, dx1, g_pre):
    s = x.shape[0]
    tm = 512

    def body(dz_ref, w_ref, x_ref, dx1_ref, g_ref, dx_ref, dg_ref, acc_ref):
        i, k = pl.program_id(0), pl.program_id(1)

        @pl.when((i == 0) & (k == 0))
        def _():
            dg_ref[...] = jnp.zeros_like(dg_ref)

        part = _dot(dz_ref[...], w_ref[...], NT)

        @pl.when(k == 0)
        def _():
            acc_ref[...] = part

        @pl.when(k > 0)
        def _():
            acc_ref[...] += part

        @pl.when(k == N_CHIPS - 1)
        def _():
            dh = acc_ref[...]
            xhat, r = _rms(x_ref[...])
            dg_ref[...] += jnp.sum(dh * xhat, axis=0, keepdims=True)
            dx_ref[...] = dx1_ref[...] + _rms_bwd(dh * g_ref[...], xhat, r)

    row = pl.BlockSpec((tm, D), lambda i, k: (i, 0))
    vec = pl.BlockSpec((1, D), lambda i, k: (0, 0))
    return pl.pallas_call(
        body, name="in_bwd_norm", grid=(s // tm, N_CHIPS),
        in_specs=[pl.BlockSpec((tm, IN_SHARD), lambda i, k: (i, k)),
                  pl.BlockSpec((None, D, IN_SHARD), lambda i, k: (k, 0, 0)), row, row, vec],
        out_specs=[row, vec],
        out_shape=[jax.ShapeDtypeStruct((s, D), F32), jax.ShapeDtypeStruct((1, D), F32)],
        scratch_shapes=[pltpu.VMEM((tm, D), F32)], compiler_params=_params("arbitrary", "arbitrary"),
    )(dz, wg, x, dx1, g_pre)


def _adamw_math(w, g, m, v):
    m = ADAM_B1 * m + (1.0 - ADAM_B1) * g
    v = ADAM_B2 * v + (1.0 - ADAM_B2) * (g * g)
    m_hat = m / (1.0 - ADAM_B1 ** ADAM_STEP)
    v_hat = v / (1.0 - ADAM_B2 ** ADAM_STEP)
    delta = -ADAM_LR * (m_hat / (jnp.sqrt(v_hat) + ADAM_EPS) + ADAM_WD * w)
    return delta, m, v


def adamw(name, w, g, m, v, tr):
    r, c = w.shape

    def body(w_ref, g_ref, m_ref, v_ref, d_ref, nm_ref, nv_ref):
        d_ref[...], nm_ref[...], nv_ref[...] = _adamw_math(w_ref[...], g_ref[...], m_ref[...], v_ref[...])

    out = jax.ShapeDtypeStruct((r, c), F32)
    return pl.pallas_call(
        body, name=name, grid=(r // tr,), in_specs=[_rows(tr, c)] * 4, out_specs=[_rows(tr, c)] * 3,
        out_shape=[out] * 3, compiler_params=_params("parallel"),
    )(w, g, m, v)


def add_halves(name, g, recv, c_idx, tr):
    n, h, c = recv.shape

    def body(c_ref, g_ref, r_ref, o_ref):
        o_ref[...] = (g_ref[...] + r_ref[...]).astype(BF16)

    nb = h // tr
    return pl.pallas_call(
        body, name=name,
        grid_spec=pltpu.PrefetchScalarGridSpec(
            num_scalar_prefetch=1, grid=(n, nb),
            in_specs=[pl.BlockSpec((None, tr, c), lambda k, i, c_ref: (k, c_ref[0] * nb + i, 0)),
                      pl.BlockSpec((None, tr, c), lambda k, i, c_ref: (k, i, 0))],
            out_specs=pl.BlockSpec((None, tr, c), lambda k, i, c_ref: (k, i, 0))),
        out_shape=jax.ShapeDtypeStruct((n, h, c), BF16), compiler_params=_params("parallel", "parallel"),
    )(c_idx, g, recv)


def sum_chips(name, parts, recv, where, tr):
    n, h, c = recv.shape
    nb = h // tr

    def body(w_ref, p_ref, r_ref, o_ref):
        acc = p_ref[...].astype(F32)
        for k in range(n):
            acc = acc + r_ref[k].astype(F32)
        o_ref[...] = acc

    return pl.pallas_call(
        body, name=name,
        grid_spec=pltpu.PrefetchScalarGridSpec(
            num_scalar_prefetch=1, grid=(nb,),
            in_specs=[pl.BlockSpec((None, tr, c), lambda i, w_ref: (w_ref[0], i, 0)),
                      pl.BlockSpec((n, tr, c), lambda i, w_ref: (0, i, 0))],
            out_specs=pl.BlockSpec((tr, c), lambda i, w_ref: (w_ref[1] * nb + i, 0))),
        out_shape=jax.ShapeDtypeStruct((2 * h, c), F32), compiler_params=_params("parallel"),
    )(where, parts, recv)


def place_shard(name, shard, where, dtype, tr):
    r, c = shard.shape

    def body(w_ref, s_ref, o_ref):
        o_ref[...] = s_ref[...].astype(dtype)

    return pl.pallas_call(
        body, name=name,
        grid_spec=pltpu.PrefetchScalarGridSpec(
            num_scalar_prefetch=1, grid=(r // tr,),
            in_specs=[pl.BlockSpec((tr, c), lambda i, w_ref: (i, 0))],
            out_specs=pl.BlockSpec((None, tr, c), lambda i, w_ref: (w_ref[0], i, 0))),
        out_shape=jax.ShapeDtypeStruct((N_CHIPS, r, c), dtype), compiler_params=_params("parallel"),
    )(where, shard)


ANY = pl.BlockSpec(memory_space=pl.ANY)


def _place():
    x, y, c = lax.axis_index("x"), lax.axis_index("y"), lax.axis_index("c")
    chips = [(1 - x, y), (x, 1 - y), (1 - x, 1 - y)]
    return x, y, c, chips


def gather_shards(arrays):
    n = len(arrays)

    def body(*refs):
        out = refs[n:2 * n]
        send_sems, recv_sems = refs[2 * n:]
        x, y, c, chips = _place()
        me = 2 * x + y
        sibling = (x, y, 1 - c)

        def half(w, chip, core):
            h = out[w].shape[1] // 2
            return out[w].at[chip, pl.ds(core * h, h)]

        def copy(k, block, to):
            return pltpu.make_async_remote_copy(src_ref=block, dst_ref=block, send_sem=send_sems.at[k],
                                                recv_sem=recv_sems.at[k], device_id=to, device_id_type=MESH)

        first, passed = [], []
        for w in range(n):
            for j, (px, py) in enumerate(chips):
                first.append(copy(3 * w + j, half(w, me, c), (px, py, c)))
        for cp in first:
            cp.start()
        for w in range(n):
            for j, (px, py) in enumerate(chips):
                k = 3 * w + j
                got = half(w, 2 * px + py, c)
                copy(k, got, (px, py, c)).wait_recv()
                fwd = copy(3 * n + k, got, sibling)
                fwd.start()
                passed.append(fwd)
        for w in range(n):
            for j, (px, py) in enumerate(chips):
                copy(3 * n + 3 * w + j, half(w, 2 * px + py, 1 - c), sibling).wait_recv()
        for cp in first + passed:
            cp.wait_send()

    return pl.pallas_call(
        body, name="gather_shards", in_specs=[ANY] * n, out_specs=[ANY] * n,
        out_shape=[jax.ShapeDtypeStruct(a.shape, a.dtype) for a in arrays],
        input_output_aliases={w: w for w in range(n)},
        scratch_shapes=[pltpu.SemaphoreType.DMA((6 * n,)), pltpu.SemaphoreType.DMA((6 * n,))],
        compiler_params=pltpu.CompilerParams(has_side_effects=True),
    )(*arrays)


def swap_halves(grads):
    n = len(grads)

    def body(*refs):
        g, out = refs[:n], refs[n:2 * n]
        send_sems, recv_sems = refs[2 * n:]
        x, y, c, _ = _place()
        copies = []
        for w in range(n):
            h = g[w].shape[1] // 2
            copies.append(pltpu.make_async_remote_copy(
                src_ref=g[w].at[:, pl.ds((1 - c) * h, h)], dst_ref=out[w], send_sem=send_sems.at[w],
                recv_sem=recv_sems.at[w], device_id=(x, y, 1 - c), device_id_type=MESH))
        for cp in copies:
            cp.start()
        for cp in copies:
            cp.wait()

    return pl.pallas_call(
        body, name="swap_halves", in_specs=[ANY] * n, out_specs=[ANY] * n,
        out_shape=[jax.ShapeDtypeStruct((a.shape[0], a.shape[1] // 2, a.shape[2]), a.dtype) for a in grads],
        scratch_shapes=[pltpu.SemaphoreType.DMA((n,)), pltpu.SemaphoreType.DMA((n,))],
        compiler_params=pltpu.CompilerParams(has_side_effects=True),
    )(*grads)


def scatter_chips(parts):
    n = len(parts)

    def body(*refs):
        p, out = refs[:n], refs[n:2 * n]
        send_sems, recv_sems = refs[2 * n:]
        x, y, c, chips = _place()
        copies = []
        for w in range(n):
            for j, (px, py) in enumerate(chips):
                copies.append(pltpu.make_async_remote_copy(
                    src_ref=p[w].at[2 * px + py], dst_ref=out[w].at[j], send_sem=send_sems.at[3 * w + j],
                    recv_sem=recv_sems.at[3 * w + j], device_id=(px, py, c), device_id_type=MESH))
        for cp in copies:
            cp.start()
        for cp in copies:
            cp.wait()

    return pl.pallas_call(
        body, name="scatter_chips", in_specs=[ANY] * n, out_specs=[ANY] * n,
        out_shape=[jax.ShapeDtypeStruct((3,) + a.shape[1:], a.dtype) for a in parts],
        scratch_shapes=[pltpu.SemaphoreType.DMA((3 * n,)), pltpu.SemaphoreType.DMA((3 * n,))],
        compiler_params=pltpu.CompilerParams(has_side_effects=True),
    )(*parts)


def join_halves(arrays):
    n = len(arrays)

    def body(*refs):
        out = refs[n:2 * n]
        send_sems, recv_sems = refs[2 * n:]
        x, y, c, _ = _place()

        def copy(w, core):
            h = out[w].shape[0] // 2
            rows = out[w].at[pl.ds(core * h, h)]
            return pltpu.make_async_remote_copy(
                src_ref=rows, dst_ref=rows, send_sem=send_sems.at[w], recv_sem=recv_sems.at[w],
                device_id=(x, y, 1 - c), device_id_type=MESH)

        for w in range(n):
            copy(w, c).start()
        for w in range(n):
            copy(w, 1 - c).wait_recv()
        for w in range(n):
            copy(w, c).wait_send()

    return pl.pallas_call(
        body, name="join_halves", in_specs=[ANY] * n, out_specs=[ANY] * n,
        out_shape=[jax.ShapeDtypeStruct(a.shape, a.dtype) for a in arrays],
        input_output_aliases={w: w for w in range(n)},
        scratch_shapes=[pltpu.SemaphoreType.DMA((n,)), pltpu.SemaphoreType.DMA((n,))],
        compiler_params=pltpu.CompilerParams(has_side_effects=True),
    )(*arrays)


def allreduce_small(packed):
    r, c = packed.shape
    n_dev = 8

    def body(x_ref, all_ref, sum_ref, send_sems, recv_sems, local_sem):
        x, y, cc, chips = _place()
        me, sibling = (x, y, cc), (x, y, 1 - cc)

        def rows(px, py, pc):
            return all_ref.at[4 * px + 2 * py + pc]

        def copy(k, block, to, src=None):
            return pltpu.make_async_remote_copy(
                src_ref=rows(*block) if src is None else src, dst_ref=rows(*block), send_sem=send_sems.at[k],
                recv_sem=recv_sems.at[k], device_id=to, device_id_type=MESH)

        mine = pltpu.make_async_copy(x_ref, rows(*me), local_sem)
        mine.start()
        first = [copy(0, me, sibling, src=x_ref)]
        first += [copy(1 + j, me, (*chip, cc), src=x_ref) for j, chip in enumerate(chips)]
        for cp in first:
            cp.start()
        passed = [copy(4 + j, (*chip, cc), sibling) for j, chip in enumerate(chips)]
        for j, chip in enumerate(chips):
            copy(1 + j, (*chip, cc), me).wait_recv()
            passed[j].start()
        copy(0, sibling, me).wait_recv()
        for j, chip in enumerate(chips):
            copy(4 + j, (*chip, 1 - cc), me).wait_recv()
        for cp in first + passed:
            cp.wait_send()
        mine.wait()
        acc = all_ref[0]
        for k in range(1, n_dev):
            acc = acc + all_ref[k]
        sum_ref[...] = acc

    vm = pl.BlockSpec(memory_space=pltpu.VMEM)
    return pl.pallas_call(
        body, name="allreduce_small", in_specs=[vm], out_specs=[vm, vm],
        out_shape=[jax.ShapeDtypeStruct((n_dev, r, c), F32), jax.ShapeDtypeStruct((r, c), F32)],
        scratch_shapes=[pltpu.SemaphoreType.DMA((7,)), pltpu.SemaphoreType.DMA((7,)), pltpu.SemaphoreType.DMA],
        compiler_params=pltpu.CompilerParams(has_side_effects=True, vmem_limit_bytes=VMEM_LIMIT),
    )(packed)[1]


def local_step(x, target, vecs, w_s, bs_t, bg, wg_in, w_a, w_b, w_out, wg_ff1, w_ff2):
    g_pre, ln_g, ln_b, g_post, g_fpre, g_fpost = vecs
    s = x.shape[0]
    logc, dist = _attn_tables(s)
    slopes = _head_slopes()

    h = norm_pre(x, g_pre)
    z = mm_in(h, wg_in)
    ya = gating_fwd(z, ln_g, ln_b, w_s, bs_t)
    yb, lse = attn_fwd(z, logc, dist, slopes)
    merged, pa, pb = proj_merge(ya, yb, w_a, w_b, z, bg)
    o, x1, h2 = out_norm(merged, w_out, x, g_post, g_fpre)
    a, rl = mm_ff1(h2, wg_ff1)
    dy, df, d_gfpost, loss = ff2_loss(rl, w_ff2, x1, target, g_fpost)

    half_cols = pl.BlockSpec((D, D // 2), lambda i, j: (0, j))
    d_wff2 = mm_tn("dw_ff2", rl, df, D // 2, D, (D_FF, D), pl.BlockSpec((D // 2, D), lambda i, j: (i, 0)))
    da = ff2_bwd(df, w_ff2, a)
    d_wff1 = mm_tn("dw_ff1", h2, da, D, D // 2, (N_CHIPS, D, D),
                   pl.BlockSpec((None, D, D // 2), lambda i, j: (j // 2, 0, j % 2)))
    dx1, do, d_gfpre, d_gpost = ff1_bwd_norms(da, wg_ff1, x1, o, dy, g_fpre, g_post)
    d_wout = mm_tn("dw_out", merged, do, D, D // 2, (D, D), half_cols)
    dpa, dpb, dga, dgb, d_bg = out_bwd_gates(do, w_out, pa, pb, z, bg)
    d_wa = mm_tn("dw_a", ya, dpa, D, D // 2, (D, D), half_cols)
    d_wb = mm_tn("dw_b", yb, dpb, D, D // 2, (D, D), half_cols)
    dya = mm_nt("dy_a", dpa, w_a)
    dyb = mm_nt("dy_b", dpb, w_b)
    du, dv, d_ws, d_bs, d_lng, d_lnb = gating_bwd(z, dya, ln_g, ln_b, w_s, bs_t)
    dq, dk, dvb = attn_bwd(z, yb, dyb, lse, logc, dist, slopes)
    dz = jnp.concatenate([du, dv, dq, dk, dvb, dga, dgb], axis=1)
    half = IN_SHARD // 2
    d_win = mm_tn("dw_in", h, dz, D, half, (N_CHIPS, D, IN_SHARD),
                  pl.BlockSpec((None, D, half), lambda i, j: (j // 2, 0, j % 2)))
    dx, d_gpre = in_bwd_norm(dz, wg_in, x, dx1, g_pre)

    big = [d_win, d_wa.reshape(N_CHIPS, D // N_CHIPS, D), d_wb.reshape(N_CHIPS, D // N_CHIPS, D),
           d_wout.reshape(N_CHIPS, D // N_CHIPS, D), d_wff1, d_wff2.reshape(N_CHIPS, D, D)]
    small = dict(norm_mix_pre=d_gpre, b_gate=d_bg, ln_v_g=d_lng, ln_v_b=d_lnb, w_s=d_ws, b_s=d_bs[:, 0, :],
                 norm_mix_post=d_gpost, norm_ffn_pre=d_gfpre, norm_ffn_post=d_gfpost)
    return loss[0, 0], dx, big, small


BIG = ("w_in", "w_a_proj", "w_b_proj", "w_out", "w_ff1", "w_ff2")
SMALL = ("norm_mix_pre", "ln_v_g", "ln_v_b", "b_s", "norm_mix_post", "norm_ffn_pre", "norm_ffn_post", "w_s", "b_gate")
ORDER = ("norm_mix_pre", "w_in", "b_gate", "ln_v_g", "ln_v_b", "w_s", "b_s", "w_a_proj", "w_b_proj", "w_out",
         "norm_mix_post", "norm_ffn_pre", "w_ff1", "w_ff2", "norm_ffn_post")
PACK_ROWS = 144


def _pack_small(t):
    rows = [t[n].reshape(1, D) for n in SMALL[:7]] + [t["w_s"].reshape(128, D), t["b_gate"].reshape(2, D)]
    used = jnp.concatenate(rows, axis=0)
    return jnp.pad(used, ((0, PACK_ROWS - used.shape[0]), (0, 0)))


def _unpack_small(p, chip):
    out = {n: p[i:i + 1].reshape(1, D) for i, n in enumerate(SMALL[:7])}
    out["b_s"] = out["b_s"].reshape(1, GROUPS, CHUNK)
    out["w_s"] = p[7:135].reshape(1, GROUPS, CHUNK, CHUNK)
    out["b_gate"] = lax.dynamic_slice_in_dim(p[135:137], chip * (D // N_CHIPS), D // N_CHIPS, axis=1).reshape(1, 2, D // N_CHIPS)
    return out


def kernel(x, norm_mix_pre, w_in, b_gate, ln_v_g, ln_v_b, w_s, b_s, w_a_proj, w_b_proj, w_out, norm_mix_post, norm_ffn_pre, w_ff1, w_ff2, norm_ffn_post, loss_target, m_norm_mix_pre, m_w_in, m_b_gate, m_ln_v_g, m_ln_v_b, m_w_s, m_b_s, m_w_a_proj, m_w_b_proj, m_w_out, m_norm_mix_post, m_norm_ffn_pre, m_w_ff1, m_w_ff2, m_norm_ffn_post, v_norm_mix_pre, v_w_in, v_b_gate, v_ln_v_g, v_ln_v_b, v_w_s, v_b_s, v_w_a_proj, v_w_b_proj, v_w_out, v_norm_mix_post, v_norm_ffn_pre, v_w_ff1, v_w_ff2, v_norm_ffn_post):
    w = dict(norm_mix_pre=norm_mix_pre, w_in=w_in, b_gate=b_gate, ln_v_g=ln_v_g, ln_v_b=ln_v_b, w_s=w_s, b_s=b_s,
             w_a_proj=w_a_proj, w_b_proj=w_b_proj, w_out=w_out, norm_mix_post=norm_mix_post,
             norm_ffn_pre=norm_ffn_pre, w_ff1=w_ff1, w_ff2=w_ff2, norm_ffn_post=norm_ffn_post)
    m = dict(norm_mix_pre=m_norm_mix_pre, w_in=m_w_in, b_gate=m_b_gate, ln_v_g=m_ln_v_g, ln_v_b=m_ln_v_b, w_s=m_w_s,
             b_s=m_b_s, w_a_proj=m_w_a_proj, w_b_proj=m_w_b_proj, w_out=m_w_out, norm_mix_post=m_norm_mix_post,
             norm_ffn_pre=m_norm_ffn_pre, w_ff1=m_w_ff1, w_ff2=m_w_ff2, norm_ffn_post=m_norm_ffn_post)
    v = dict(norm_mix_pre=v_norm_mix_pre, w_in=v_w_in, b_gate=v_b_gate, ln_v_g=v_ln_v_g, ln_v_b=v_ln_v_b, w_s=v_w_s,
             b_s=v_b_s, w_a_proj=v_w_a_proj, w_b_proj=v_w_b_proj, w_out=v_w_out, norm_mix_post=v_norm_mix_post,
             norm_ffn_pre=v_norm_ffn_pre, w_ff1=v_w_ff1, w_ff2=v_w_ff2, norm_ffn_post=v_norm_ffn_post)
    chip = 2 * lax.axis_index("x") + lax.axis_index("y")
    core = lax.axis_index("c")

    where = jnp.stack([chip, core]).astype(jnp.int32)
    placed = [place_shard("place_" + n, w[n][0], where, BF16, min(w[n].shape[1], 256)) for n in BIG]
    placed.append(place_shard("place_b_gate", jnp.pad(b_gate[0], ((0, 14), (0, 0))), where, F32, 16))
    wg_in, wg_a, wg_b, wg_out, wg_ff1, wg_ff2, bg_all = gather_shards(placed)
    bg = jnp.transpose(bg_all[:, :2, :], (1, 0, 2)).reshape(2, D)

    vecs = (norm_mix_pre, ln_v_g, ln_v_b, norm_mix_post, norm_ffn_pre, norm_ffn_post)
    loss, dx, big, small = local_step(
        x[0], loss_target[0], vecs, w_s[0], b_s[0].T, bg, wg_in, wg_a.reshape(D, D), wg_b.reshape(D, D),
        wg_out.reshape(D, D), wg_ff1, wg_ff2.reshape(D_FF, D))
    loss = lax.psum(loss, ("x", "y", "c"))

    recv = swap_halves(big)
    c_idx = jnp.reshape(core, (1,)).astype(jnp.int32)
    parts = [add_halves("add_" + n, g, r, c_idx, min(r.shape[1], 256)) for n, g, r in zip(BIG, big, recv)]
    got = scatter_chips(parts)
    halves = [sum_chips("sum_" + n, p, r, where, min(p.shape[1], 256)) for n, p, r in zip(BIG, parts, got)]
    grads = dict(zip(BIG, join_halves(halves)))

    grads_small = _unpack_small(allreduce_small(_pack_small(small)), chip)

    new = {}
    for n in BIG:
        shape = w[n].shape
        r, c = shape[1], shape[2]
        d, nm, nv = adamw("adamw_" + n, w[n][0], grads[n], m[n][0], v[n][0], min(r, 256))
        new[n] = (grads[n].reshape(shape), d.reshape(shape), nm.reshape(shape), nv.reshape(shape))
    full_bg = lambda t: jnp.zeros((2, D), F32).at[:, :D // N_CHIPS].set(t["b_gate"][0])
    packs = [_pack_small({**{n: t[n] for n in SMALL[:8]}, "b_gate": full_bg(t)}) for t in (w, m, v)]
    g_pack = _pack_small({**{n: grads_small[n] for n in SMALL[:8]},
                          "b_gate": jnp.zeros((2, D), F32).at[:, :D // N_CHIPS].set(grads_small["b_gate"][0])})
    d_p, m_p, v_p = adamw("adamw_small", packs[0], g_pack, packs[1], packs[2], PACK_ROWS)
    d_s, m_s, v_s = (_unpack_small(p, 0) for p in (d_p, m_p, v_p))
    for n in SMALL:
        new[n] = (grads_small[n].reshape(w[n].shape), d_s[n].reshape(w[n].shape), m_s[n].reshape(w[n].shape),
                  v_s[n].reshape(w[n].shape))

    outs = [loss, dx[None]]
    for i in range(4):
        outs += [new[n][i] for n in ORDER]
    return tuple(outs)
```

```python
import functools
import math

import numpy as np
import jax
import jax.numpy as jnp
from jax import lax
from jax.experimental import pallas as pl
from jax.experimental.pallas import tpu as pltpu

F32 = jnp.float32
BF16 = jnp.bfloat16
MESH = pl.DeviceIdType.MESH

D = 1024
EPS = 1e-6
CHUNK = 128
GROUPS = 8
HEADS = 16
HEAD_DIM = 64
ATT_T = 256
N_CHIPS = 4
D_FF = 4 * D
IN_COLS = 7 * D
IN_SHARD = IN_COLS // N_CHIPS
MASKED = -1e30
VMEM_LIMIT = 56 * 2 ** 20

ADAM_LR, ADAM_B1, ADAM_B2, ADAM_EPS, ADAM_WD, ADAM_STEP = 0.001, 0.9, 0.999, 1e-08, 0.01, 10

NN = (((1,), (0,)), ((), ()))
NT = (((1,), (1,)), ((), ()))
TN = (((0,), (0,)), ((), ()))


def _dot(a, b, dims=NN):
    return lax.dot_general(a, b, dims, preferred_element_type=F32)


def _params(*sem):
    return pltpu.CompilerParams(dimension_semantics=sem or None, vmem_limit_bytes=VMEM_LIMIT)


def _rows(tr, c, col=0):
    return pl.BlockSpec((tr, c), lambda i: (i, col))


def _full(shape):
    n = len(shape)
    return pl.BlockSpec(shape, lambda *_: (0,) * n)


def _gelu(x):
    k = math.sqrt(2.0 / math.pi)
    return 0.5 * x * (1.0 + jnp.tanh(k * (x + 0.044715 * x * x * x)))


def _gelu_and_grad(x):
    k = math.sqrt(2.0 / math.pi)
    t = jnp.tanh(k * (x + 0.044715 * x * x * x))
    g = 0.5 * x * (1.0 + t)
    dg = 0.5 * (1.0 + t) + 0.5 * x * (1.0 - t * t) * (k * (1.0 + 3.0 * 0.044715 * x * x))
    return g, dg


def _sigmoid(x):
    return 1.0 / (1.0 + jnp.exp(-x))


def _rms(x):
    r = lax.rsqrt(jnp.mean(x * x, axis=-1, keepdims=True) + EPS)
    return x * r, r


def _rms_bwd(dn, xhat, r):
    return r * (dn - xhat * jnp.mean(dn * xhat, axis=-1, keepdims=True))


def norm_pre(x, g):
    s = x.shape[0]
    tr = 512

    def body(x_ref, g_ref, h_ref):
        xhat, _ = _rms(x_ref[...])
        h_ref[...] = (xhat * g_ref[...]).astype(BF16)

    return pl.pallas_call(
        body, name="norm_pre", grid=(s // tr,),
        in_specs=[_rows(tr, D), _full((1, D))], out_specs=_rows(tr, D),
        out_shape=jax.ShapeDtypeStruct((s, D), BF16), compiler_params=_params("parallel"),
    )(x, g)


def mm_in(h, wg):
    s = h.shape[0]
    tm, tn = 1024, IN_SHARD // 2
    per = IN_SHARD // tn

    def body(a_ref, b_ref, o_ref):
        o_ref[...] = _dot(a_ref[...], b_ref[...])

    return pl.pallas_call(
        body, name="mm_in", grid=(s // tm, IN_COLS // tn),
        in_specs=[pl.BlockSpec((tm, D), lambda i, j: (i, 0)),
                  pl.BlockSpec((None, D, tn), lambda i, j: (j // per, 0, j % per))],
        out_specs=pl.BlockSpec((tm, tn), lambda i, j: (i, j)),
        out_shape=jax.ShapeDtypeStruct((s, IN_COLS), F32), compiler_params=_params("parallel", "parallel"),
    )(h, wg)


def _tril_ws(ws_ref, g):
    r = lax.broadcasted_iota(jnp.int32, (CHUNK, CHUNK), 0)
    c = lax.broadcasted_iota(jnp.int32, (CHUNK, CHUNK), 1)
    return jnp.where(c <= r, ws_ref[g], 0.0).astype(BF16)


def _layer_norm(v):
    mu = jnp.mean(v, axis=-1, keepdims=True)
    d = v - mu
    rstd = lax.rsqrt(jnp.mean(d * d, axis=-1, keepdims=True) + EPS)
    return d * rstd, rstd


def gating_fwd(z, ln_g, ln_b, w_s, bs_t):
    s = z.shape[0]

    def body(u_ref, v_ref, lg_ref, lb_ref, ws_ref, bst_ref, ya_ref):
        ug = _gelu(u_ref[...])
        vhat, _ = _layer_norm(_gelu(v_ref[...]))
        vn = (vhat * lg_ref[...] + lb_ref[...]).astype(BF16)
        for g in range(GROUPS):
            cols = slice(g * CHUNK, (g + 1) * CHUNK)
            mixed = _dot(_tril_ws(ws_ref, g), vn[:, cols]) + bst_ref[:, g:g + 1]
            ya_ref[:, cols] = (ug[:, cols] * mixed).astype(BF16)

    return pl.pallas_call(
        body, name="gating_fwd", grid=(s // CHUNK,),
        in_specs=[_rows(CHUNK, D, 0), _rows(CHUNK, D, 1), _full((1, D)), _full((1, D)),
                  _full((GROUPS, CHUNK, CHUNK)), _full((CHUNK, GROUPS))],
        out_specs=_rows(CHUNK, D), out_shape=jax.ShapeDtypeStruct((s, D), BF16),
        compiler_params=_params("parallel"),
    )(z, z, ln_g, ln_b, w_s, bs_t)


def _attn_tables(s):
    nd = s // ATT_T
    r = np.arange(ATT_T)[None, :, None]
    c = np.arange(ATT_T)[None, None, :]
    delta = np.arange(nd)[:, None, None] * ATT_T + r - c
    count = np.zeros(delta.shape, np.int64)
    for window, dilation in ((128, 1), (512, 4), (2048, 16)):
        count += (delta >= 0) & (delta % dilation == 0) & (delta <= window)
    logc = np.where(count > 0, np.log(np.maximum(count, 1)), MASKED)
    return jnp.asarray(logc, F32)


AUG = 3


def _split3_np(x):
    terms, rest = [], np.asarray(x, np.float64)
    for _ in range(AUG):
        term = np.asarray(rest.astype(jnp.bfloat16), np.float64)
        terms.append(term)
        rest = rest - term
    return terms


def _split3(x):
    terms, rest = [], x
    for _ in range(AUG):
        term = rest.astype(BF16).astype(F32)
        terms.append(term)
        rest = rest - term
    return terms


def _alibi_tables(s):
    nb = s // ATT_T
    slopes = np.exp2(-8.0 * np.arange(1, HEADS + 1, dtype=np.float64) / HEADS)
    ka = np.zeros((HEADS // 2, 2, ATT_T, 128), np.float32)
    kb = np.zeros((HEADS // 2, 2, nb, 128), np.float32)
    for p in range(HEADS // 2):
        for e in range(2):
            base = HEAD_DIM * (1 - e)
            for a, term in enumerate(_split3_np(slopes[2 * p + e] * np.arange(ATT_T))):
                ka[p, e, :, base + a] = term
            for a, term in enumerate(_split3_np(slopes[2 * p + e] * ATT_T * np.arange(nb))):
                kb[p, e, :, base + AUG + a] = term
            ka[p, e, :, base + 2 * AUG:base + 3 * AUG] = 1.0
    return jnp.asarray(ka), jnp.asarray(kb)


def _head_masks():
    lane = lax.broadcasted_iota(jnp.int32, (1, 128), 1)
    first = lane < HEAD_DIM

    def ones(e, n):
        base = HEAD_DIM * (1 - e)
        return ((lane >= base) & (lane < base + n)).astype(F32)

    return first, lane, ones


def _place3(lane, at, terms, other):
    for a, term in enumerate(terms):
        other = jnp.where(lane == at + a, term, other)
    return other


def attn_fwd(z, logc, ka, kb):
    s = z.shape[0]
    nq = s // ATT_T
    t = ATT_T
    qcol, kcol, vcol = 2 * D // 128, 3 * D // 128, 4 * D // 128

    def body(q_ref, k_ref, v_ref, lc_ref, ka_ref, kb_ref, y_ref, lse_ref, q_s, k_s, v_s, m_s, l_s, acc_s):
        qi = pl.program_id(1)
        first, lane, ones = _head_masks()

        @pl.when(qi == 0)
        def _():
            sel = jnp.broadcast_to(first.astype(F32), (t, 128))
            for jb in range(nq):
                kj = k_ref[jb * t:(jb + 1) * t, :]
                vj = v_ref[jb * t:(jb + 1) * t, :]
                k_s[0, jb] = jnp.where(first, kj, ka_ref[0] + kb_ref[0, jb:jb + 1, :]).astype(BF16)
                k_s[1, jb] = jnp.where(first, ka_ref[1] + kb_ref[1, jb:jb + 1, :], kj).astype(BF16)
                v_s[jb, 0:t, 0:128] = jnp.where(first, vj, 0.0).astype(BF16)
                v_s[jb, t:2 * t, 0:128] = jnp.where(first, 0.0, vj).astype(BF16)
                v_s[jb, 0:t, 128:256] = sel.astype(BF16)
                v_s[jb, t:2 * t, 128:256] = (1.0 - sel).astype(BF16)

        q = q_ref[...] * (1.0 / math.sqrt(HEAD_DIM))
        q_s[0] = jnp.where(first, q, ones(0, 2 * AUG)).astype(BF16)
        q_s[1] = jnp.where(first, ones(1, 2 * AUG), q).astype(BF16)
        m_s[...] = jnp.full_like(m_s, MASKED)
        l_s[...] = jnp.zeros_like(l_s)
        acc_s[...] = jnp.zeros_like(acc_s)

        def scores(j):
            return _dot(q_s[0], k_s[0, j], NT), _dot(q_s[1], k_s[1, j], NT)

        def step(j, u):
            nxt = scores(j + 1)
            softmax_block(j, u)
            return nxt

        def softmax_block(j, u):
            lc = lc_ref[qi - j]
            u0 = u[0] + lc
            u1 = u[1] + lc
            m0, m1 = m_s[0], m_s[1]
            n0 = jnp.maximum(m0, jnp.max(u0, axis=-1, keepdims=True))
            n1 = jnp.maximum(m1, jnp.max(u1, axis=-1, keepdims=True))
            m_s[0], m_s[1] = n0, n1
            p = jnp.concatenate([jnp.exp(u0 - jnp.concatenate([n0, n0], axis=1)).astype(BF16),
                                 jnp.exp(u1 - jnp.concatenate([n1, n1], axis=1)).astype(BF16)], axis=1)
            pv = _dot(p, v_s[j])
            alpha = jnp.where(first, jnp.exp(m0 - n0), jnp.exp(m1 - n1))
            acc_s[...] = acc_s[...] * alpha + pv[:, 0:128]
            l_s[...] = l_s[...] * alpha + pv[:, 128:256]

        softmax_block(qi, lax.fori_loop(0, qi, step, scores(0)))
        y_ref[...] = (acc_s[...] / l_s[...]).astype(BF16)
        lse_ref[...] = jnp.where(first, m_s[0], m_s[1]) + jnp.log(l_s[...])

    return pl.pallas_call(
        body, name="attn_fwd", grid=(HEADS // 2, nq),
        in_specs=[pl.BlockSpec((t, 128), lambda p, i: (i, qcol + p)),
                  pl.BlockSpec((s, 128), lambda p, i: (0, kcol + p)),
                  pl.BlockSpec((s, 128), lambda p, i: (0, vcol + p)),
                  _full((nq, t, t)),
                  pl.BlockSpec((None, 2, t, 128), lambda p, i: (p, 0, 0, 0)),
                  pl.BlockSpec((None, 2, nq, 128), lambda p, i: (p, 0, 0, 0))],
        out_specs=[pl.BlockSpec((t, 128), lambda p, i: (i, p)), pl.BlockSpec((t, 128), lambda p, i: (i, p))],
        out_shape=[jax.ShapeDtypeStruct((s, D), BF16), jax.ShapeDtypeStruct((s, D), F32)],
        scratch_shapes=[pltpu.VMEM((2, t, 128), BF16), pltpu.VMEM((2, nq, t, 128), BF16),
                        pltpu.VMEM((nq, 2 * t, 256), BF16), pltpu.VMEM((2, t, 128), F32),
                        pltpu.VMEM((t, 128), F32), pltpu.VMEM((t, 128), F32)],
        compiler_params=_params("arbitrary", "arbitrary"),
    )(z, z, z, logc, ka, kb)


def proj_merge(ya, yb, wa, wb, z, bg):
    s = ya.shape[0]
    tm = 512

    def body(ya_ref, yb_ref, wa_ref, wb_ref, ga_ref, gb_ref, bg_ref, mg_ref, pa_ref, pb_ref):
        pa = _dot(ya_ref[...], wa_ref[...])
        pb = _dot(yb_ref[...], wb_ref[...])
        sa = _sigmoid(ga_ref[...] + bg_ref[0:1, :])
        sb = _sigmoid(gb_ref[...] + bg_ref[1:2, :])
        mg_ref[...] = (sa * pa + sb * pb).astype(BF16)
        pa_ref[...] = pa.astype(BF16)
        pb_ref[...] = pb.astype(BF16)

    out = jax.ShapeDtypeStruct((s, D), BF16)
    return pl.pallas_call(
        body, name="proj_merge", grid=(s // tm,),
        in_specs=[_rows(tm, D), _rows(tm, D), _full((D, D)), _full((D, D)),
                  _rows(tm, D, 5), _rows(tm, D, 6), _full((2, D))],
        out_specs=[_rows(tm, D)] * 3, out_shape=[out] * 3, compiler_params=_params("parallel"),
    )(ya, yb, wa, wb, z, z, bg)


def out_norm(merged, w_out, x, g_post, g_fpre):
    s = x.shape[0]
    tm = 512

    def body(mg_ref, w_ref, x_ref, gp_ref, gf_ref, o_ref, x1_ref, h2_ref):
        o = _dot(mg_ref[...], w_ref[...])
        ohat, _ = _rms(o)
        x1 = x_ref[...] + ohat * gp_ref[...]
        x1hat, _ = _rms(x1)
        o_ref[...] = o
        x1_ref[...] = x1
        h2_ref[...] = (x1hat * gf_ref[...]).astype(BF16)

    return pl.pallas_call(
        body, name="out_norm", grid=(s // tm,),
        in_specs=[_rows(tm, D), _full((D, D)), _rows(tm, D), _full((1, D)), _full((1, D))],
        out_specs=[_rows(tm, D)] * 3,
        out_shape=[jax.ShapeDtypeStruct((s, D), F32), jax.ShapeDtypeStruct((s, D), F32),
                   jax.ShapeDtypeStruct((s, D), BF16)],
        compiler_params=_params("parallel"),
    )(merged, w_out, x, g_post, g_fpre)


def mm_ff1(h2, wg):
    s = h2.shape[0]
    tm = 1024

    def body(a_ref, b_ref, o_ref, r_ref):
        a = _dot(a_ref[...], b_ref[...])
        o_ref[...] = a
        r = jnp.maximum(a, 0.0)
        r_ref[...] = (r * r).astype(BF16)

    return pl.pallas_call(
        body, name="mm_ff1", grid=(s // tm, N_CHIPS),
        in_specs=[pl.BlockSpec((tm, D), lambda i, j: (i, 0)), pl.BlockSpec((None, D, D), lambda i, j: (j, 0, 0))],
        out_specs=[pl.BlockSpec((tm, D), lambda i, j: (i, j))] * 2,
        out_shape=[jax.ShapeDtypeStruct((s, D_FF), F32), jax.ShapeDtypeStruct((s, D_FF), BF16)],
        compiler_params=_params("parallel", "parallel"),
    )(h2, wg)


def ff2_loss(rl, w_ff2, x1, target, g_fpost):
    s = x1.shape[0]
    tm = 256

    def body(rl_ref, w_ref, x1_ref, t_ref, g_ref, dy_ref, df_ref, dg_ref, loss_ref):
        @pl.when(pl.program_id(0) == 0)
        def _():
            dg_ref[...] = jnp.zeros_like(dg_ref)
            loss_ref[...] = jnp.zeros_like(loss_ref)

        f = _dot(rl_ref[...], w_ref[...])
        fhat, r = _rms(f)
        err = x1_ref[...] + fhat * g_ref[...] - t_ref[...]
        loss_ref[...] += 0.5 * jnp.sum(jnp.mean(err * err, axis=-1, keepdims=True), axis=0, keepdims=True)
        dy = err * (1.0 / D)
        dy_ref[...] = dy
        dg_ref[...] += jnp.sum(dy * fhat, axis=0, keepdims=True)
        df_ref[...] = _rms_bwd(dy * g_ref[...], fhat, r).astype(BF16)

    return pl.pallas_call(
        body, name="ff2_loss", grid=(s // tm,),
        in_specs=[_rows(tm, D_FF), _full((D_FF, D)), _rows(tm, D), _rows(tm, D), _full((1, D))],
        out_specs=[_rows(tm, D), _rows(tm, D), _full((1, D)), _full((1, 1))],
        out_shape=[jax.ShapeDtypeStruct((s, D), F32), jax.ShapeDtypeStruct((s, D), BF16),
                   jax.ShapeDtypeStruct((1, D), F32), jax.ShapeDtypeStruct((1, 1), F32)],
        compiler_params=_params("arbitrary"),
    )(rl, w_ff2, x1, target, g_fpost)


def mm_tn(name, a, b, ta, tb, out_shape, out_spec):
    s = a.shape[0]

    def body(a_ref, b_ref, o_ref):
        o_ref[...] = _dot(a_ref[...], b_ref[...], TN)

    return pl.pallas_call(
        body, name=name, grid=(a.shape[1] // ta, b.shape[1] // tb),
        in_specs=[pl.BlockSpec((s, ta), lambda i, j: (0, i)), pl.BlockSpec((s, tb), lambda i, j: (0, j))],
        out_specs=out_spec, out_shape=jax.ShapeDtypeStruct(out_shape, F32),
        compiler_params=_params("parallel", "parallel"),
    )(a, b)


def mm_nt(name, a, w):
    s = a.shape[0]
    tm = 512

    def body(a_ref, w_ref, o_ref):
        o_ref[...] = _dot(a_ref[...], w_ref[...], NT)

    return pl.pallas_call(
        body, name=name, grid=(s // tm,), in_specs=[_rows(tm, D), _full((D, D))], out_specs=_rows(tm, D),
        out_shape=jax.ShapeDtypeStruct((s, D), F32), compiler_params=_params("parallel"),
    )(a, w)


def ff2_bwd(df, w_ff2, a):
    s = df.shape[0]
    tm = 1024

    def body(df_ref, w_ref, a_ref, da_ref):
        drl = _dot(df_ref[...], w_ref[...], NT)
        da_ref[...] = (drl * (2.0 * jnp.maximum(a_ref[...], 0.0))).astype(BF16)

    return pl.pallas_call(
        body, name="ff2_bwd", grid=(s // tm, D_FF // D),
        in_specs=[pl.BlockSpec((tm, D), lambda i, j: (i, 0)), pl.BlockSpec((D, D), lambda i, j: (j, 0)),
                  pl.BlockSpec((tm, D), lambda i, j: (i, j))],
        out_specs=pl.BlockSpec((tm, D), lambda i, j: (i, j)),
        out_shape=jax.ShapeDtypeStruct((s, D_FF), BF16), compiler_params=_params("parallel", "parallel"),
    )(df, w_ff2, a)


def ff1_bwd_norms(da, wg, x1, o, dy, g_fpre, g_post):
    s = x1.shape[0]
    tm = 512

    def body(da_ref, w_ref, x1_ref, o_ref, dy_ref, gf_ref, gp_ref, dx1_ref, do_ref, dgf_ref, dgp_ref, acc_ref):
        i, k = pl.program_id(0), pl.program_id(1)

        @pl.when((i == 0) & (k == 0))
        def _():
            dgf_ref[...] = jnp.zeros_like(dgf_ref)
            dgp_ref[...] = jnp.zeros_like(dgp_ref)

        part = _dot(da_ref[...], w_ref[...], NT)

        @pl.when(k == 0)
        def _():
            acc_ref[...] = part

        @pl.when(k > 0)
        def _():
            acc_ref[...] += part

        @pl.when(k == N_CHIPS - 1)
        def _():
            dh2 = acc_ref[...]
            x1hat, r2 = _rms(x1_ref[...])
            dgf_ref[...] += jnp.sum(dh2 * x1hat, axis=0, keepdims=True)
            dx1 = dy_ref[...] + _rms_bwd(dh2 * gf_ref[...], x1hat, r2)
            ohat, r1 = _rms(o_ref[...])
            dgp_ref[...] += jnp.sum(dx1 * ohat, axis=0, keepdims=True)
            dx1_ref[...] = dx1
            do_ref[...] = _rms_bwd(dx1 * gp_ref[...], ohat, r1).astype(BF16)

    row = pl.BlockSpec((tm, D), lambda i, k: (i, 0))
    vec = pl.BlockSpec((1, D), lambda i, k: (0, 0))
    return pl.pallas_call(
        body, name="ff1_bwd_norms", grid=(s // tm, N_CHIPS),
        in_specs=[pl.BlockSpec((tm, D), lambda i, k: (i, k)), pl.BlockSpec((None, D, D), lambda i, k: (k, 0, 0)),
                  row, row, row, vec, vec],
        out_specs=[row, row, vec, vec],
        out_shape=[jax.ShapeDtypeStruct((s, D), F32), jax.ShapeDtypeStruct((s, D), BF16),
                   jax.ShapeDtypeStruct((1, D), F32), jax.ShapeDtypeStruct((1, D), F32)],
        scratch_shapes=[pltpu.VMEM((tm, D), F32)], compiler_params=_params("arbitrary", "arbitrary"),
    )(da, wg, x1, o, dy, g_fpre, g_post)


def out_bwd_gates(do, w_out, pa, pb, z, bg):
    s = do.shape[0]
    tm = 512

    def body(do_ref, w_ref, pa_ref, pb_ref, ga_ref, gb_ref, bg_ref, dpa_ref, dpb_ref, dga_ref, dgb_ref, dbg_ref):
        @pl.when(pl.program_id(0) == 0)
        def _():
            dbg_ref[...] = jnp.zeros_like(dbg_ref)

        dm = _dot(do_ref[...], w_ref[...], NT)
        sa = _sigmoid(ga_ref[...] + bg_ref[0:1, :])
        sb = _sigmoid(gb_ref[...] + bg_ref[1:2, :])
        dpa_ref[...] = (dm * sa).astype(BF16)
        dpb_ref[...] = (dm * sb).astype(BF16)
        dga = dm * pa_ref[...].astype(F32) * (sa * (1.0 - sa))
        dgb = dm * pb_ref[...].astype(F32) * (sb * (1.0 - sb))
        dga_ref[...] = dga.astype(BF16)
        dgb_ref[...] = dgb.astype(BF16)
        dbg_ref[0:1, :] += jnp.sum(dga, axis=0, keepdims=True)
        dbg_ref[1:2, :] += jnp.sum(dgb, axis=0, keepdims=True)

    out = jax.ShapeDtypeStruct((s, D), BF16)
    return pl.pallas_call(
        body, name="out_bwd_gates", grid=(s // tm,),
        in_specs=[_rows(tm, D), _full((D, D)), _rows(tm, D), _rows(tm, D), _rows(tm, D, 5), _rows(tm, D, 6),
                  _full((2, D))],
        out_specs=[_rows(tm, D)] * 4 + [_full((2, D))],
        out_shape=[out] * 4 + [jax.ShapeDtypeStruct((2, D), F32)], compiler_params=_params("arbitrary"),
    )(do, w_out, pa, pb, z, z, bg)


def gating_bwd(z, dya, ln_g, ln_b, w_s, bs_t):
    s = z.shape[0]
    ones = functools.partial(jnp.ones, (8, CHUNK), BF16)

    def body(u_ref, v_ref, dya_ref, lg_ref, lb_ref, ws_ref, bst_ref,
             du_ref, dv_ref, dws_ref, dbs_ref, dlg_ref, dlb_ref, dvn_ref):
        ci = pl.program_id(0)

        @pl.when(ci == 0)
        def _():
            dws_ref[...] = jnp.zeros_like(dws_ref)
            dbs_ref[...] = jnp.zeros_like(dbs_ref)
            dlg_ref[...] = jnp.zeros_like(dlg_ref)
            dlb_ref[...] = jnp.zeros_like(dlb_ref)

        ug, dug_du = _gelu_and_grad(u_ref[...])
        vg, dvg_dv = _gelu_and_grad(v_ref[...])
        vhat, rstd = _layer_norm(vg)
        vn = (vhat * lg_ref[...] + lb_ref[...]).astype(BF16)
        dya = dya_ref[...]
        for g in range(GROUPS):
            cols = slice(g * CHUNK, (g + 1) * CHUNK)
            ws = _tril_ws(ws_ref, g)
            mixed = _dot(ws, vn[:, cols]) + bst_ref[:, g:g + 1]
            du_ref[:, cols] = (dya[:, cols] * mixed * dug_du[:, cols]).astype(BF16)
            dmix = (dya[:, cols] * ug[:, cols]).astype(BF16)
            dbs_ref[g] += _dot(ones(), dmix, NT)
            dws_ref[g] += _dot(dmix, vn[:, cols], NT)
            dvn_ref[:, cols] = _dot(ws, dmix, TN)
        dvn = dvn_ref[...]
        dlg_ref[...] += jnp.sum(dvn * vhat, axis=0, keepdims=True)
        dlb_ref[...] += jnp.sum(dvn, axis=0, keepdims=True)
        dvh = dvn * lg_ref[...]
        dvg = rstd * (dvh - jnp.mean(dvh, axis=-1, keepdims=True)
                      - vhat * jnp.mean(dvh * vhat, axis=-1, keepdims=True))
        dv_ref[...] = (dvg * dvg_dv).astype(BF16)

        @pl.when(ci == pl.num_programs(0) - 1)
        def _():
            r = lax.broadcasted_iota(jnp.int32, (CHUNK, CHUNK), 0)
            c = lax.broadcasted_iota(jnp.int32, (CHUNK, CHUNK), 1)
            for g in range(GROUPS):
                dws_ref[g] = jnp.where(c <= r, dws_ref[g], 0.0)

    out = jax.ShapeDtypeStruct((s, D), BF16)
    return pl.pallas_call(
        body, name="gating_bwd", grid=(s // CHUNK,),
        in_specs=[_rows(CHUNK, D, 0), _rows(CHUNK, D, 1), _rows(CHUNK, D), _full((1, D)), _full((1, D)),
                  _full((GROUPS, CHUNK, CHUNK)), _full((CHUNK, GROUPS))],
        out_specs=[_rows(CHUNK, D), _rows(CHUNK, D), _full((GROUPS, CHUNK, CHUNK)), _full((GROUPS, 8, CHUNK)),
                   _full((1, D)), _full((1, D))],
        out_shape=[out, out, jax.ShapeDtypeStruct((GROUPS, CHUNK, CHUNK), F32),
                   jax.ShapeDtypeStruct((GROUPS, 8, CHUNK), F32),
                   jax.ShapeDtypeStruct((1, D), F32), jax.ShapeDtypeStruct((1, D), F32)],
        scratch_shapes=[pltpu.VMEM((CHUNK, D), F32)], compiler_params=_params("arbitrary"),
    )(z, z, dya, ln_g, ln_b, w_s, bs_t)


def attn_bwd(z, yb, dyb, lse, logc, ka, kb):
    s = z.shape[0]
    nq = s // ATT_T
    t = ATT_T
    qcol, kcol, vcol = 2 * D // 128, 3 * D // 128, 4 * D // 128
    scale = 1.0 / math.sqrt(HEAD_DIM)

    def body(q_ref, k_ref, v_ref, y_ref, dy_ref, lse_ref, lc_ref, ka_ref, kb_ref,
             dq_ref, dk_ref, dv_ref, qa_s, qt_s, da_s, dt_s, dq_s, dkt_s, dvt_s):
        j = pl.program_id(1)
        first, lane, ones = _head_masks()

        @pl.when(j == 0)
        def _():
            dq_s[...] = jnp.zeros_like(dq_s)
            for ib in range(nq):
                rows = slice(ib * t, (ib + 1) * t)
                q = q_ref[rows, :] * scale
                lse = lse_ref[rows, :]
                qa_s[0, ib] = jnp.where(first, q, _place3(lane, HEAD_DIM + 2 * AUG, _split3(-lse[:, 0:1]),
                                                          ones(0, 2 * AUG))).astype(BF16)
                qa_s[1, ib] = jnp.where(first, _place3(lane, 2 * AUG, _split3(-lse[:, HEAD_DIM:HEAD_DIM + 1]),
                                                       ones(1, 2 * AUG)), q).astype(BF16)
                qt_s[ib, :, 0:t] = jnp.where(first, q, 0.0).T.astype(BF16)
                qt_s[ib, :, t:2 * t] = jnp.where(first, 0.0, q).T.astype(BF16)
                do = dy_ref[rows, :]
                prod = do * y_ref[rows, :].astype(F32)
                dd0 = jnp.sum(jnp.where(first, prod, 0.0), axis=-1, keepdims=True)
                dd1 = jnp.sum(jnp.where(first, 0.0, prod), axis=-1, keepdims=True)
                da_s[0, ib] = jnp.where(first, do, _place3(lane, HEAD_DIM, _split3(-dd0), 0.0)).astype(BF16)
                da_s[1, ib] = jnp.where(first, _place3(lane, 0, _split3(-dd1), 0.0), do).astype(BF16)
                dt_s[ib, :, 0:t] = jnp.where(first, do, 0.0).T.astype(BF16)
                dt_s[ib, :, t:2 * t] = jnp.where(first, 0.0, do).T.astype(BF16)

        kj = k_ref[...]
        vj = v_ref[...]
        k0a = jnp.where(first, kj, ka_ref[0] + kb_ref[0, pl.ds(j, 1), :]).astype(BF16)
        k1a = jnp.where(first, ka_ref[1] + kb_ref[1, pl.ds(j, 1), :], kj).astype(BF16)
        kst = jnp.concatenate([jnp.where(first, kj, 0.0), jnp.where(first, 0.0, kj)], axis=0).astype(BF16)
        v0a = jnp.where(first, vj, ones(0, AUG)).astype(BF16)
        v1a = jnp.where(first, ones(1, AUG), vj).astype(BF16)
        dkt_s[...] = jnp.zeros_like(dkt_s)
        dvt_s[...] = jnp.zeros_like(dvt_s)

        def step(i, _):
            lc = lc_ref[i - j]
            p0 = jnp.exp(_dot(qa_s[0, i], k0a, NT) + lc)
            p1 = jnp.exp(_dot(qa_s[1, i], k1a, NT) + lc)
            e0 = (p0 * _dot(da_s[0, i], v0a, NT)).astype(BF16)
            e1 = (p1 * _dot(da_s[1, i], v1a, NT)).astype(BF16)
            rows = pl.ds(pl.multiple_of(i * t, t), t)
            dq_s[rows, :] += _dot(jnp.concatenate([e0, e1], axis=1), kst)
            dvt_s[...] += _dot(dt_s[i], jnp.concatenate([p0.astype(BF16), p1.astype(BF16)], axis=0))
            dkt_s[...] += _dot(qt_s[i], jnp.concatenate([e0, e1], axis=0))
            return 0

        lax.fori_loop(j, nq, step, 0)
        dk_ref[...] = dkt_s[...].T.astype(BF16)
        dv_ref[...] = dvt_s[...].T.astype(BF16)

        @pl.when(j == nq - 1)
        def _():
            dq_ref[...] = (dq_s[...] * scale).astype(BF16)

    colblock = lambda c: pl.BlockSpec((s, 128), lambda p, j: (0, c + p))
    blk = lambda c: pl.BlockSpec((t, 128), lambda p, j: (j, c + p))
    out = jax.ShapeDtypeStruct((s, D), BF16)
    return pl.pallas_call(
        body, name="attn_bwd", grid=(HEADS // 2, nq),
        in_specs=[colblock(qcol), blk(kcol), blk(vcol), colblock(0), colblock(0), colblock(0),
                  _full((nq, t, t)), pl.BlockSpec((None, 2, t, 128), lambda p, j: (p, 0, 0, 0)),
                  pl.BlockSpec((None, 2, nq, 128), lambda p, j: (p, 0, 0, 0))],
        out_specs=[colblock(0), blk(0), blk(0)], out_shape=[out] * 3,
        scratch_shapes=[pltpu.VMEM((2, nq, t, 128), BF16), pltpu.VMEM((nq, 128, 2 * t), BF16),
                        pltpu.VMEM((2, nq, t, 128), BF16), pltpu.VMEM((nq, 128, 2 * t), BF16),
                        pltpu.VMEM((s, 128), F32), pltpu.VMEM((128, t), F32), pltpu.VMEM((128, t), F32)],
        compiler_params=_params("arbitrary", "arbitrary"),
    )(z, z, z, yb, dyb, lse, logc, ka, kb)


def in_bwd_norm(dz, wg, x, dx1, g_pre):
    s = x.shape[0]
    tm = 512

    def body(dz_ref, w_ref, x_ref, dx1_ref, g_ref, dx_ref, dg_ref, acc_ref):
        i, k = pl.program_id(0), pl.program_id(1)

        @pl.when((i == 0) & (k == 0))
        def _():
            dg_ref[...] = jnp.zeros_like(dg_ref)

        part = _dot(dz_ref[...], w_ref[...], NT)

        @pl.when(k == 0)
        def _():
            acc_ref[...] = part

        @pl.when(k > 0)
        def _():
            acc_ref[...] += part

        @pl.when(k == N_CHIPS - 1)
        def _():
            dh = acc_ref[...]
            xhat, r = _rms(x_ref[...])
            dg_ref[...] += jnp.sum(dh * xhat, axis=0, keepdims=True)
            dx_ref[...] = dx1_ref[...] + _rms_bwd(dh * g_ref[...], xhat, r)

    row = pl.BlockSpec((tm, D), lambda i, k: (i, 0))
    vec = pl.BlockSpec((1, D), lambda i, k: (0, 0))
    return pl.pallas_call(
        body, name="in_bwd_norm", grid=(s // tm, N_CHIPS),
        in_specs=[pl.BlockSpec((tm, IN_SHARD), lambda i, k: (i, k)),
                  pl.BlockSpec((None, D, IN_SHARD), lambda i, k: (k, 0, 0)), row, row, vec],
        out_specs=[row, vec],
        out_shape=[jax.ShapeDtypeStruct((s, D), F32), jax.ShapeDtypeStruct((1, D), F32)],
        scratch_shapes=[pltpu.VMEM((tm, D), F32)], compiler_params=_params("arbitrary", "arbitrary"),
    )(dz, wg, x, dx1, g_pre)


def _adamw_math(w, g, m, v):
    m = ADAM_B1 * m + (1.0 - ADAM_B1) * g
    v = ADAM_B2 * v + (1.0 - ADAM_B2) * (g * g)
    m_hat = m / (1.0 - ADAM_B1 ** ADAM_STEP)
    v_hat = v / (1.0 - ADAM_B2 ** ADAM_STEP)
    delta = -ADAM_LR * (m_hat / (jnp.sqrt(v_hat) + ADAM_EPS) + ADAM_WD * w)
    return delta, m, v


def adamw(name, w, g, m, v, tr):
    r, c = w.shape

    def body(w_ref, g_ref, m_ref, v_ref, d_ref, nm_ref, nv_ref):
        d_ref[...], nm_ref[...], nv_ref[...] = _adamw_math(w_ref[...], g_ref[...], m_ref[...], v_ref[...])

    out = jax.ShapeDtypeStruct((r, c), F32)
    return pl.pallas_call(
        body, name=name, grid=(r // tr,), in_specs=[_rows(tr, c)] * 4, out_specs=[_rows(tr, c)] * 3,
        out_shape=[out] * 3, compiler_params=_params("parallel"),
    )(w, g, m, v)


def add_halves(name, g, recv, c_idx, tr):
    n, h, c = recv.shape

    def body(c_ref, g_ref, r_ref, o_ref):
        o_ref[...] = (g_ref[...] + r_ref[...]).astype(BF16)

    nb = h // tr
    return pl.pallas_call(
        body, name=name,
        grid_spec=pltpu.PrefetchScalarGridSpec(
            num_scalar_prefetch=1, grid=(n, nb),
            in_specs=[pl.BlockSpec((None, tr, c), lambda k, i, c_ref: (k, c_ref[0] * nb + i, 0)),
                      pl.BlockSpec((None, tr, c), lambda k, i, c_ref: (k, i, 0))],
            out_specs=pl.BlockSpec((None, tr, c), lambda k, i, c_ref: (k, i, 0))),
        out_shape=jax.ShapeDtypeStruct((n, h, c), BF16), compiler_params=_params("parallel", "parallel"),
    )(c_idx, g, recv)


def sum_chips(name, parts, recv, where, tr):
    n, h, c = recv.shape
    nb = h // tr

    def body(w_ref, p_ref, r_ref, o_ref):
        acc = p_ref[...].astype(F32)
        for k in range(n):
            acc = acc + r_ref[k].astype(F32)
        o_ref[...] = acc

    return pl.pallas_call(
        body, name=name,
        grid_spec=pltpu.PrefetchScalarGridSpec(
            num_scalar_prefetch=1, grid=(nb,),
            in_specs=[pl.BlockSpec((None, tr, c), lambda i, w_ref: (w_ref[0], i, 0)),
                      pl.BlockSpec((n, tr, c), lambda i, w_ref: (0, i, 0))],
            out_specs=pl.BlockSpec((tr, c), lambda i, w_ref: (w_ref[1] * nb + i, 0))),
        out_shape=jax.ShapeDtypeStruct((2 * h, c), F32), compiler_params=_params("parallel"),
    )(where, parts, recv)


def place_shard(name, shard, where, dtype, tr):
    r, c = shard.shape

    def body(w_ref, s_ref, o_ref):
        o_ref[...] = s_ref[...].astype(dtype)

    return pl.pallas_call(
        body, name=name,
        grid_spec=pltpu.PrefetchScalarGridSpec(
            num_scalar_prefetch=1, grid=(r // tr,),
            in_specs=[pl.BlockSpec((tr, c), lambda i, w_ref: (i, 0))],
            out_specs=pl.BlockSpec((None, tr, c), lambda i, w_ref: (w_ref[0], i, 0))),
        out_shape=jax.ShapeDtypeStruct((N_CHIPS, r, c), dtype), compiler_params=_params("parallel"),
    )(where, shard)


ANY = pl.BlockSpec(memory_space=pl.ANY)


def _place():
    x, y, c = lax.axis_index("x"), lax.axis_index("y"), lax.axis_index("c")
    chips = [(1 - x, y), (x, 1 - y), (1 - x, 1 - y)]
    return x, y, c, chips


def gather_shards(arrays):
    n = len(arrays)

    def body(*refs):
        out = refs[n:2 * n]
        send_sems, recv_sems = refs[2 * n:]
        x, y, c, chips = _place()
        me = 2 * x + y
        sibling = (x, y, 1 - c)

        def half(w, chip, core):
            h = out[w].shape[1] // 2
            return out[w].at[chip, pl.ds(core * h, h)]

        def copy(k, block, to):
            return pltpu.make_async_remote_copy(src_ref=block, dst_ref=block, send_sem=send_sems.at[k],
                                                recv_sem=recv_sems.at[k], device_id=to, device_id_type=MESH)

        first, passed = [], []
        for w in range(n):
            for j, (px, py) in enumerate(chips):
                first.append(copy(3 * w + j, half(w, me, c), (px, py, c)))
        for cp in first:
            cp.start()
        for w in range(n):
            for j, (px, py) in enumerate(chips):
                k = 3 * w + j
                got = half(w, 2 * px + py, c)
                copy(k, got, (px, py, c)).wait_recv()
                fwd = copy(3 * n + k, got, sibling)
                fwd.start()
                passed.append(fwd)
        for w in range(n):
            for j, (px, py) in enumerate(chips):
                copy(3 * n + 3 * w + j, half(w, 2 * px + py, 1 - c), sibling).wait_recv()
        for cp in first + passed:
            cp.wait_send()

    return pl.pallas_call(
        body, name="gather_shards", in_specs=[ANY] * n, out_specs=[ANY] * n,
        out_shape=[jax.ShapeDtypeStruct(a.shape, a.dtype) for a in arrays],
        input_output_aliases={w: w for w in range(n)},
        scratch_shapes=[pltpu.SemaphoreType.DMA((6 * n,)), pltpu.SemaphoreType.DMA((6 * n,))],
        compiler_params=pltpu.CompilerParams(has_side_effects=True),
    )(*arrays)


def swap_halves(grads):
    n = len(grads)

    def body(*refs):
        g, out = refs[:n], refs[n:2 * n]
        send_sems, recv_sems = refs[2 * n:]
        x, y, c, _ = _place()
        copies = []
        for w in range(n):
            h = g[w].shape[1] // 2
            copies.append(pltpu.make_async_remote_copy(
                src_ref=g[w].at[:, pl.ds((1 - c) * h, h)], dst_ref=out[w], send_sem=send_sems.at[w],
                recv_sem=recv_sems.at[w], device_id=(x, y, 1 - c), device_id_type=MESH))
        for cp in copies:
            cp.start()
        for cp in copies:
            cp.wait()

    return pl.pallas_call(
        body, name="swap_halves", in_specs=[ANY] * n, out_specs=[ANY] * n,
        out_shape=[jax.ShapeDtypeStruct((a.shape[0], a.shape[1] // 2, a.shape[2]), a.dtype) for a in grads],
        scratch_shapes=[pltpu.SemaphoreType.DMA((n,)), pltpu.SemaphoreType.DMA((n,))],
        compiler_params=pltpu.CompilerParams(has_side_effects=True),
    )(*grads)


def scatter_chips(parts):
    n = len(parts)

    def body(*refs):
        p, out = refs[:n], refs[n:2 * n]
        send_sems, recv_sems = refs[2 * n:]
        x, y, c, chips = _place()
        copies = []
        for w in range(n):
            for j, (px, py) in enumerate(chips):
                copies.append(pltpu.make_async_remote_copy(
                    src_ref=p[w].at[2 * px + py], dst_ref=out[w].at[j], send_sem=send_sems.at[3 * w + j],
                    recv_sem=recv_sems.at[3 * w + j], device_id=(px, py, c), device_id_type=MESH))
        for cp in copies:
            cp.start()
        for cp in copies:
            cp.wait()

    return pl.pallas_call(
        body, name="scatter_chips", in_specs=[ANY] * n, out_specs=[ANY] * n,
        out_shape=[jax.ShapeDtypeStruct((3,) + a.shape[1:], a.dtype) for a in parts],
        scratch_shapes=[pltpu.SemaphoreType.DMA((3 * n,)), pltpu.SemaphoreType.DMA((3 * n,))],
        compiler_params=pltpu.CompilerParams(has_side_effects=True),
    )(*parts)


def join_halves(arrays):
    n = len(arrays)

    def body(*refs):
        out = refs[n:2 * n]
        send_sems, recv_sems = refs[2 * n:]
        x, y, c, _ = _place()

        def copy(w, core):
            h = out[w].shape[0] // 2
            rows = out[w].at[pl.ds(core * h, h)]
            return pltpu.make_async_remote_copy(
                src_ref=rows, dst_ref=rows, send_sem=send_sems.at[w], recv_sem=recv_sems.at[w],
                device_id=(x, y, 1 - c), device_id_type=MESH)

        for w in range(n):
            copy(w, c).start()
        for w in range(n):
            copy(w, 1 - c).wait_recv()
        for w in range(n):
            copy(w, c).wait_send()

    return pl.pallas_call(
        body, name="join_halves", in_specs=[ANY] * n, out_specs=[ANY] * n,
        out_shape=[jax.ShapeDtypeStruct(a.shape, a.dtype) for a in arrays],
        input_output_aliases={w: w for w in range(n)},
        scratch_shapes=[pltpu.SemaphoreType.DMA((n,)), pltpu.SemaphoreType.DMA((n,))],
        compiler_params=pltpu.CompilerParams(has_side_effects=True),
    )(*arrays)


def allreduce_small(packed):
    r, c = packed.shape
    n_dev = 8

    def body(x_ref, all_ref, sum_ref, send_sems, recv_sems, local_sem):
        x, y, cc, chips = _place()
        me, sibling = (x, y, cc), (x, y, 1 - cc)

        def rows(px, py, pc):
            return all_ref.at[4 * px + 2 * py + pc]

        def copy(k, block, to, src=None):
            return pltpu.make_async_remote_copy(
                src_ref=rows(*block) if src is None else src, dst_ref=rows(*block), send_sem=send_sems.at[k],
                recv_sem=recv_sems.at[k], device_id=to, device_id_type=MESH)

        mine = pltpu.make_async_copy(x_ref, rows(*me), local_sem)
        mine.start()
        first = [copy(0, me, sibling, src=x_ref)]
        first += [copy(1 + j, me, (*chip, cc), src=x_ref) for j, chip in enumerate(chips)]
        for cp in first:
            cp.start()
        passed = [copy(4 + j, (*chip, cc), sibling) for j, chip in enumerate(chips)]
        for j, chip in enumerate(chips):
            copy(1 + j, (*chip, cc), me).wait_recv()
            passed[j].start()
        copy(0, sibling, me).wait_recv()
        for j, chip in enumerate(chips):
            copy(4 + j, (*chip, 1 - cc), me).wait_recv()
        for cp in first + passed:
            cp.wait_send()
        mine.wait()
        acc = all_ref[0]
        for k in range(1, n_dev):
            acc = acc + all_ref[k]
        sum_ref[...] = acc

    vm = pl.BlockSpec(memory_space=pltpu.VMEM)
    return pl.pallas_call(
        body, name="allreduce_small", in_specs=[vm], out_specs=[vm, vm],
        out_shape=[jax.ShapeDtypeStruct((n_dev, r, c), F32), jax.ShapeDtypeStruct((r, c), F32)],
        scratch_shapes=[pltpu.SemaphoreType.DMA((7,)), pltpu.SemaphoreType.DMA((7,)), pltpu.SemaphoreType.DMA],
        compiler_params=pltpu.CompilerParams(has_side_effects=True, vmem_limit_bytes=VMEM_LIMIT),
    )(packed)[1]


def local_step(x, target, vecs, w_s, bs_t, bg, wg_in, w_a, w_b, w_out, wg_ff1, w_ff2):
    g_pre, ln_g, ln_b, g_post, g_fpre, g_fpost = vecs
    s = x.shape[0]
    logc = _attn_tables(s)
    ka, kb = _alibi_tables(s)

    h = norm_pre(x, g_pre)
    z = mm_in(h, wg_in)
    ya = gating_fwd(z, ln_g, ln_b, w_s, bs_t)
    yb, lse = attn_fwd(z, logc, ka, kb)
    merged, pa, pb = proj_merge(ya, yb, w_a, w_b, z, bg)
    o, x1, h2 = out_norm(merged, w_out, x, g_post, g_fpre)
    a, rl = mm_ff1(h2, wg_ff1)
    dy, df, d_gfpost, loss = ff2_loss(rl, w_ff2, x1, target, g_fpost)

    half_cols = pl.BlockSpec((D, D // 2), lambda i, j: (0, j))
    d_wff2 = mm_tn("dw_ff2", rl, df, D // 2, D, (D_FF, D), pl.BlockSpec((D // 2, D), lambda i, j: (i, 0)))
    da = ff2_bwd(df, w_ff2, a)
    d_wff1 = mm_tn("dw_ff1", h2, da, D, D // 2, (N_CHIPS, D, D),
                   pl.BlockSpec((None, D, D // 2), lambda i, j: (j // 2, 0, j % 2)))
    dx1, do, d_gfpre, d_gpost = ff1_bwd_norms(da, wg_ff1, x1, o, dy, g_fpre, g_post)
    d_wout = mm_tn("dw_out", merged, do, D, D // 2, (D, D), half_cols)
    dpa, dpb, dga, dgb, d_bg = out_bwd_gates(do, w_out, pa, pb, z, bg)
    d_wa = mm_tn("dw_a", ya, dpa, D, D // 2, (D, D), half_cols)
    d_wb = mm_tn("dw_b", yb, dpb, D, D // 2, (D, D), half_cols)
    dya = mm_nt("dy_a", dpa, w_a)
    dyb = mm_nt("dy_b", dpb, w_b)
    du, dv, d_ws, d_bs, d_lng, d_lnb = gating_bwd(z, dya, ln_g, ln_b, w_s, bs_t)
    dq, dk, dvb = attn_bwd(z, yb, dyb, lse, logc, ka, kb)
    dz = jnp.concatenate([du, dv, dq, dk, dvb, dga, dgb], axis=1)
    half = IN_SHARD // 2
    d_win = mm_tn("dw_in", h, dz, D, half, (N_CHIPS, D, IN_SHARD),
                  pl.BlockSpec((None, D, half), lambda i, j: (j // 2, 0, j % 2)))
    dx, d_gpre = in_bwd_norm(dz, wg_in, x, dx1, g_pre)

    big = [d_win, d_wa.reshape(N_CHIPS, D // N_CHIPS, D), d_wb.reshape(N_CHIPS, D // N_CHIPS, D),
           d_wout.reshape(N_CHIPS, D // N_CHIPS, D), d_wff1, d_wff2.reshape(N_CHIPS, D, D)]
    small = dict(norm_mix_pre=d_gpre, b_gate=d_bg, ln_v_g=d_lng, ln_v_b=d_lnb, w_s=d_ws, b_s=d_bs[:, 0, :],
                 norm_mix_post=d_gpost, norm_ffn_pre=d_gfpre, norm_ffn_post=d_gfpost)
    return loss[0, 0], dx, big, small


BIG = ("w_in", "w_a_proj", "w_b_proj", "w_out", "w_ff1", "w_ff2")
SMALL = ("norm_mix_pre", "ln_v_g", "ln_v_b", "b_s", "norm_mix_post", "norm_ffn_pre", "norm_ffn_post", "w_s", "b_gate")
ORDER = ("norm_mix_pre", "w_in", "b_gate", "ln_v_g", "ln_v_b", "w_s", "b_s", "w_a_proj", "w_b_proj", "w_out",
         "norm_mix_post", "norm_ffn_pre", "w_ff1", "w_ff2", "norm_ffn_post")
PACK_ROWS = 144


def _pack_small(t):
    rows = [t[n].reshape(1, D) for n in SMALL[:7]] + [t["w_s"].reshape(128, D), t["b_gate"].reshape(2, D)]
    used = jnp.concatenate(rows, axis=0)
    return jnp.pad(used, ((0, PACK_ROWS - used.shape[0]), (0, 0)))


def _unpack_small(p, chip):
    out = {n: p[i:i + 1].reshape(1, D) for i, n in enumerate(SMALL[:7])}
    out["b_s"] = out["b_s"].reshape(1, GROUPS, CHUNK)
    out["w_s"] = p[7:135].reshape(1, GROUPS, CHUNK, CHUNK)
    out["b_gate"] = lax.dynamic_slice_in_dim(p[135:137], chip * (D // N_CHIPS), D // N_CHIPS, axis=1).reshape(1, 2, D // N_CHIPS)
    return out


def kernel(x, norm_mix_pre, w_in, b_gate, ln_v_g, ln_v_b, w_s, b_s, w_a_proj, w_b_proj, w_out, norm_mix_post, norm_ffn_pre, w_ff1, w_ff2, norm_ffn_post, loss_target, m_norm_mix_pre, m_w_in, m_b_gate, m_ln_v_g, m_ln_v_b, m_w_s, m_b_s, m_w_a_proj, m_w_b_proj, m_w_out, m_norm_mix_post, m_norm_ffn_pre, m_w_ff1, m_w_ff2, m_norm_ffn_post, v_norm_mix_pre, v_w_in, v_b_gate, v_ln_v_g, v_ln_v_b, v_w_s, v_b_s, v_w_a_proj, v_w_b_proj, v_w_out, v_norm_mix_post, v_norm_ffn_pre, v_w_ff1, v_w_ff2, v_norm_ffn_post):
    w = dict(norm_mix_pre=norm_mix_pre, w_in=w_in, b_gate=b_gate, ln_v_g=ln_v_g, ln_v_b=ln_v_b, w_s=w_s, b_s=b_s,
             w_a_proj=w_a_proj, w_b_proj=w_b_proj, w_out=w_out, norm_mix_post=norm_mix_post,
             norm_ffn_pre=norm_ffn_pre, w_ff1=w_ff1, w_ff2=w_ff2, norm_ffn_post=norm_ffn_post)
    m = dict(norm_mix_pre=m_norm_mix_pre, w_in=m_w_in, b_gate=m_b_gate, ln_v_g=m_ln_v_g, ln_v_b=m_ln_v_b, w_s=m_w_s,
             b_s=m_b_s, w_a_proj=m_w_a_proj, w_b_proj=m_w_b_proj, w_out=m_w_out, norm_mix_post=m_norm_mix_post,
             norm_ffn_pre=m_norm_ffn_pre, w_ff1=m_w_ff1, w_ff2=m_w_ff2, norm_ffn_post=m_norm_ffn_post)
    v = dict(norm_mix_pre=v_norm_mix_pre, w_in=v_w_in, b_gate=v_b_gate, ln_v_g=v_ln_v_g, ln_v_b=v_ln_v_b, w_s=v_w_s,
             b_s=v_b_s, w_a_proj=v_w_a_proj, w_b_proj=v_w_b_proj, w_out=v_w_out, norm_mix_post=v_norm_mix_post,
             norm_ffn_pre=v_norm_ffn_pre, w_ff1=v_w_ff1, w_ff2=v_w_ff2, norm_ffn_post=v_norm_ffn_post)
    chip = 2 * lax.axis_index("x") + lax.axis_index("y")
    core = lax.axis_index("c")

    where = jnp.stack([chip, core]).astype(jnp.int32)
    placed = [place_shard("place_" + n, w[n][0], where, BF16, min(w[n].shape[1], 256)) for n in BIG]
    placed.append(place_shard("place_b_gate", jnp.pad(b_gate[0], ((0, 14), (0, 0))), where, F32, 16))
    wg_in, wg_a, wg_b, wg_out, wg_ff1, wg_ff2, bg_all = gather_shards(placed)
    bg = jnp.transpose(bg_all[:, :2, :], (1, 0, 2)).reshape(2, D)

    vecs = (norm_mix_pre, ln_v_g, ln_v_b, norm_mix_post, norm_ffn_pre, norm_ffn_post)
    loss, dx, big, small = local_step(
        x[0], loss_target[0], vecs, w_s[0], b_s[0].T, bg, wg_in, wg_a.reshape(D, D), wg_b.reshape(D, D),
        wg_out.reshape(D, D), wg_ff1, wg_ff2.reshape(D_FF, D))
    loss = lax.psum(loss, ("x", "y", "c"))

    recv = swap_halves(big)
    c_idx = jnp.reshape(core, (1,)).astype(jnp.int32)
    parts = [add_halves("add_" + n, g, r, c_idx, min(r.shape[1], 256)) for n, g, r in zip(BIG, big, recv)]
    got = scatter_chips(parts)
    halves = [sum_chips("sum_" + n, p, r, where, min(p.shape[1], 256)) for n, p, r in zip(BIG, parts, got)]
    grads = dict(zip(BIG, join_halves(halves)))

    grads_small = _unpack_small(allreduce_small(_pack_small(small)), chip)

    new = {}
    for n in BIG:
        shape = w[n].shape
        r, c = shape[1], shape[2]
        d, nm, nv = adamw("adamw_" + n, w[n][0], grads[n], m[n][0], v[n][0], min(r, 256))
        new[n] = (grads[n].reshape(shape), d.reshape(shape), nm.reshape(shape), nv.reshape(shape))
    full_bg = lambda t: jnp.zeros((2, D), F32).at[:, :D // N_CHIPS].set(t["b_gate"][0])
    packs = [_pack_small({**{n: t[n] for n in SMALL[:8]}, "b_gate": full_bg(t)}) for t in (w, m, v)]
    g_pack = _pack_small({**{n: grads_small[n] for n in SMALL[:8]},
                          "b_gate": jnp.zeros((2, D), F32).at[:, :D // N_CHIPS].set(grads_small["b_gate"][0])})
    d_p, m_p, v_p = adamw("adamw_small", packs[0], g_pack, packs[1], packs[2], PACK_ROWS)
    d_s, m_s, v_s = (_unpack_small(p, 0) for p in (d_p, m_p, v_p))
    for n in SMALL:
        new[n] = (grads_small[n].reshape(w[n].shape), d_s[n].reshape(w[n].shape), m_s[n].reshape(w[n].shape),
                  v_s[n].reshape(w[n].shape))

    outs = [loss, dx[None]]
    for i in range(4):
        outs += [new[n][i] for n in ORDER]
    return tuple(outs)
```

```python
import functools
import math

import numpy as np
import jax
import jax.numpy as jnp
from jax import lax
from jax.experimental import pallas as pl
from jax.experimental.pallas import tpu as pltpu

F32 = jnp.float32
BF16 = jnp.bfloat16
MESH = pl.DeviceIdType.MESH

D = 1024
EPS = 1e-6
CHUNK = 128
GROUPS = 8
HEADS = 16
HEAD_DIM = 64
ATT_T = 256
N_CHIPS = 4
D_FF = 4 * D
IN_COLS = 7 * D
IN_SHARD = IN_COLS // N_CHIPS
MASKED = -1e30
VMEM_LIMIT = 56 * 2 ** 20

ADAM_LR, ADAM_B1, ADAM_B2, ADAM_EPS, ADAM_WD, ADAM_STEP = 0.001, 0.9, 0.999, 1e-08, 0.01, 10

NN = (((1,), (0,)), ((), ()))
NT = (((1,), (1,)), ((), ()))
TN = (((0,), (0,)), ((), ()))


def _dot(a, b, dims=NN):
    return lax.dot_general(a, b, dims, preferred_element_type=F32)


def _params(*sem, communicates=False):
    return pltpu.CompilerParams(dimension_semantics=sem or None, vmem_limit_bytes=VMEM_LIMIT,
                                has_side_effects=communicates)


def _rows(tr, c, col=0):
    return pl.BlockSpec((tr, c), lambda i: (i, col))


def _full(shape):
    n = len(shape)
    return pl.BlockSpec(shape, lambda *_: (0,) * n)


def _gelu(x):
    k = math.sqrt(2.0 / math.pi)
    return 0.5 * x * (1.0 + jnp.tanh(k * (x + 0.044715 * x * x * x)))


def _gelu_and_grad(x):
    k = math.sqrt(2.0 / math.pi)
    t = jnp.tanh(k * (x + 0.044715 * x * x * x))
    g = 0.5 * x * (1.0 + t)
    dg = 0.5 * (1.0 + t) + 0.5 * x * (1.0 - t * t) * (k * (1.0 + 3.0 * 0.044715 * x * x))
    return g, dg


def _sigmoid(x):
    return 1.0 / (1.0 + jnp.exp(-x))


def _rms(x):
    r = lax.rsqrt(jnp.mean(x * x, axis=-1, keepdims=True) + EPS)
    return x * r, r


def _rms_bwd(dn, xhat, r):
    return r * (dn - xhat * jnp.mean(dn * xhat, axis=-1, keepdims=True))


def norm_pre(x, g):
    s = x.shape[0]
    tr = 512

    def body(x_ref, g_ref, h_ref):
        xhat, _ = _rms(x_ref[...])
        h_ref[...] = (xhat * g_ref[...]).astype(BF16)

    return pl.pallas_call(
        body, name="norm_pre", grid=(s // tr,),
        in_specs=[_rows(tr, D), _full((1, D))], out_specs=_rows(tr, D),
        out_shape=jax.ShapeDtypeStruct((s, D), BF16), compiler_params=_params("parallel"),
    )(x, g)


def mm_in(h, wg):
    s = h.shape[0]
    tm, tn = 1024, IN_SHARD // 2
    per = IN_SHARD // tn

    def body(a_ref, b_ref, o_ref):
        o_ref[...] = _dot(a_ref[...], b_ref[...])

    return pl.pallas_call(
        body, name="mm_in", grid=(s // tm, IN_COLS // tn),
        in_specs=[pl.BlockSpec((tm, D), lambda i, j: (i, 0)),
                  pl.BlockSpec((None, D, tn), lambda i, j: (j // per, 0, j % per))],
        out_specs=pl.BlockSpec((tm, tn), lambda i, j: (i, j)),
        out_shape=jax.ShapeDtypeStruct((s, IN_COLS), F32), compiler_params=_params("parallel", "parallel"),
    )(h, wg)


def _tril_ws(ws_ref, g):
    r = lax.broadcasted_iota(jnp.int32, (CHUNK, CHUNK), 0)
    c = lax.broadcasted_iota(jnp.int32, (CHUNK, CHUNK), 1)
    return jnp.where(c <= r, ws_ref[g], 0.0).astype(BF16)


def _layer_norm(v):
    mu = jnp.mean(v, axis=-1, keepdims=True)
    d = v - mu
    rstd = lax.rsqrt(jnp.mean(d * d, axis=-1, keepdims=True) + EPS)
    return d * rstd, rstd


def gating_fwd(z, ln_g, ln_b, w_s, bs_t):
    s = z.shape[0]

    def body(u_ref, v_ref, lg_ref, lb_ref, ws_ref, bst_ref, ya_ref):
        ug = _gelu(u_ref[...])
        vhat, _ = _layer_norm(_gelu(v_ref[...]))
        vn = (vhat * lg_ref[...] + lb_ref[...]).astype(BF16)
        for g in range(GROUPS):
            cols = slice(g * CHUNK, (g + 1) * CHUNK)
            mixed = _dot(_tril_ws(ws_ref, g), vn[:, cols]) + bst_ref[:, g:g + 1]
            ya_ref[:, cols] = (ug[:, cols] * mixed).astype(BF16)

    return pl.pallas_call(
        body, name="gating_fwd", grid=(s // CHUNK,),
        in_specs=[_rows(CHUNK, D, 0), _rows(CHUNK, D, 1), _full((1, D)), _full((1, D)),
                  _full((GROUPS, CHUNK, CHUNK)), _full((CHUNK, GROUPS))],
        out_specs=_rows(CHUNK, D), out_shape=jax.ShapeDtypeStruct((s, D), BF16),
        compiler_params=_params("parallel"),
    )(z, z, ln_g, ln_b, w_s, bs_t)


def _attn_tables(s):
    nd = s // ATT_T
    r = np.arange(ATT_T)[None, :, None]
    c = np.arange(ATT_T)[None, None, :]
    delta = np.arange(nd)[:, None, None] * ATT_T + r - c
    count = np.zeros(delta.shape, np.int64)
    for window, dilation in ((128, 1), (512, 4), (2048, 16)):
        count += (delta >= 0) & (delta % dilation == 0) & (delta <= window)
    logc = np.where(count > 0, np.log(np.maximum(count, 1)), MASKED)
    return jnp.asarray(logc, F32)


AUG = 3


def _split3_np(x):
    terms, rest = [], np.asarray(x, np.float64)
    for _ in range(AUG):
        term = np.asarray(rest.astype(jnp.bfloat16), np.float64)
        terms.append(term)
        rest = rest - term
    return terms


def _split3(x):
    terms, rest = [], x
    for _ in range(AUG):
        term = rest.astype(BF16).astype(F32)
        terms.append(term)
        rest = rest - term
    return terms


def _alibi_tables(s):
    nb = s // ATT_T
    slopes = np.exp2(-8.0 * np.arange(1, HEADS + 1, dtype=np.float64) / HEADS)
    ka = np.zeros((HEADS // 2, 2, ATT_T, 128), np.float32)
    kb = np.zeros((HEADS // 2, 2, nb, 128), np.float32)
    for p in range(HEADS // 2):
        for e in range(2):
            base = HEAD_DIM * (1 - e)
            for a, term in enumerate(_split3_np(slopes[2 * p + e] * np.arange(ATT_T))):
                ka[p, e, :, base + a] = term
            for a, term in enumerate(_split3_np(slopes[2 * p + e] * ATT_T * np.arange(nb))):
                kb[p, e, :, base + AUG + a] = term
            ka[p, e, :, base + 2 * AUG:base + 3 * AUG] = 1.0
    return jnp.asarray(ka), jnp.asarray(kb)


def _head_masks():
    lane = lax.broadcasted_iota(jnp.int32, (1, 128), 1)
    first = lane < HEAD_DIM

    def ones(e, n):
        base = HEAD_DIM * (1 - e)
        return ((lane >= base) & (lane < base + n)).astype(F32)

    return first, lane, ones


def _place3(lane, at, terms, other):
    for a, term in enumerate(terms):
        other = jnp.where(lane == at + a, term, other)
    return other


def attn_fwd(z, logc, ka, kb, gathering):
    s = z.shape[0]
    nq = s // ATT_T
    t = ATT_T
    n = len(gathering)
    qcol, kcol, vcol = 2 * D // 128, 3 * D // 128, 4 * D // 128

    def body(*refs):
        q_ref, k_ref, v_ref, lc_ref, ka_ref, kb_ref = refs[:6]
        y_ref, lse_ref = refs[6 + n:8 + n]
        q_s, k_s, v_s, m_s, l_s, acc_s = refs[8 + 2 * n:14 + 2 * n]
        pair, qi = pl.program_id(0), pl.program_id(1)
        first, lane, ones = _head_masks()
        if n:
            send, pass_on, finish = _gather_phases(refs[8 + n:8 + 2 * n], *refs[14 + 2 * n:])
            pl.when((pair == 0) & (qi == 0))(send)
            pl.when((pair == HEADS // 2 - 2) & (qi == 0))(pass_on)

        @pl.when(qi == 0)
        def _():
            sel = jnp.broadcast_to(first.astype(F32), (t, 128))
            for jb in range(nq):
                kj = k_ref[jb * t:(jb + 1) * t, :]
                vj = v_ref[jb * t:(jb + 1) * t, :]
                k_s[0, jb] = jnp.where(first, kj, ka_ref[0] + kb_ref[0, jb:jb + 1, :]).astype(BF16)
                k_s[1, jb] = jnp.where(first, ka_ref[1] + kb_ref[1, jb:jb + 1, :], kj).astype(BF16)
                v_s[jb, 0:t, 0:128] = jnp.where(first, vj, 0.0).astype(BF16)
                v_s[jb, t:2 * t, 0:128] = jnp.where(first, 0.0, vj).astype(BF16)
                v_s[jb, 0:t, 128:256] = sel.astype(BF16)
                v_s[jb, t:2 * t, 128:256] = (1.0 - sel).astype(BF16)

        q = q_ref[...] * (1.0 / math.sqrt(HEAD_DIM))
        q_s[0] = jnp.where(first, q, ones(0, 2 * AUG)).astype(BF16)
        q_s[1] = jnp.where(first, ones(1, 2 * AUG), q).astype(BF16)
        m_s[...] = jnp.full_like(m_s, MASKED)
        l_s[...] = jnp.zeros_like(l_s)
        acc_s[...] = jnp.zeros_like(acc_s)

        def scores(j):
            return _dot(q_s[0], k_s[0, j], NT), _dot(q_s[1], k_s[1, j], NT)

        def step(j, u):
            nxt = scores(j + 1)
            softmax_block(j, u)
            return nxt

        def softmax_block(j, u):
            lc = lc_ref[qi - j]
            u0 = u[0] + lc
            u1 = u[1] + lc
            m0, m1 = m_s[0], m_s[1]
            n0 = jnp.maximum(m0, jnp.max(u0, axis=-1, keepdims=True))
            n1 = jnp.maximum(m1, jnp.max(u1, axis=-1, keepdims=True))
            m_s[0], m_s[1] = n0, n1
            p = jnp.concatenate([jnp.exp(u0 - jnp.concatenate([n0, n0], axis=1)).astype(BF16),
                                 jnp.exp(u1 - jnp.concatenate([n1, n1], axis=1)).astype(BF16)], axis=1)
            pv = _dot(p, v_s[j])
            alpha = jnp.where(first, jnp.exp(m0 - n0), jnp.exp(m1 - n1))
            acc_s[...] = acc_s[...] * alpha + pv[:, 0:128]
            l_s[...] = l_s[...] * alpha + pv[:, 128:256]

        softmax_block(qi, lax.fori_loop(0, qi, step, scores(0)))
        y_ref[...] = (acc_s[...] / l_s[...]).astype(BF16)
        lse_ref[...] = jnp.where(first, m_s[0], m_s[1]) + jnp.log(l_s[...])
        if n:
            pl.when((pair == HEADS // 2 - 1) & (qi == nq - 1))(finish)

    out = pl.pallas_call(
        body, name="attn_fwd", grid=(HEADS // 2, nq),
        in_specs=[pl.BlockSpec((t, 128), lambda p, i: (i, qcol + p)),
                  pl.BlockSpec((s, 128), lambda p, i: (0, kcol + p)),
                  pl.BlockSpec((s, 128), lambda p, i: (0, vcol + p)),
                  _full((nq, t, t)),
                  pl.BlockSpec((None, 2, t, 128), lambda p, i: (p, 0, 0, 0)),
                  pl.BlockSpec((None, 2, nq, 128), lambda p, i: (p, 0, 0, 0))] + [ANY] * n,
        out_specs=[pl.BlockSpec((t, 128), lambda p, i: (i, p)), pl.BlockSpec((t, 128), lambda p, i: (i, p))]
        + [ANY] * n,
        out_shape=[jax.ShapeDtypeStruct((s, D), BF16), jax.ShapeDtypeStruct((s, D), F32)]
        + [jax.ShapeDtypeStruct(a.shape, a.dtype) for a in gathering],
        input_output_aliases={6 + w: 2 + w for w in range(n)},
        scratch_shapes=[pltpu.VMEM((2, t, 128), BF16), pltpu.VMEM((2, nq, t, 128), BF16),
                        pltpu.VMEM((nq, 2 * t, 256), BF16), pltpu.VMEM((2, t, 128), F32),
                        pltpu.VMEM((t, 128), F32), pltpu.VMEM((t, 128), F32)] + (_gather_sems(n) if n else []),
        compiler_params=_params("arbitrary", "arbitrary", communicates=bool(n)),
    )(z, z, z, logc, ka, kb, *gathering)
    return out[0], out[1], out[2:]


def proj_merge(ya, yb, wa, wb, z, bg):
    s = ya.shape[0]
    tm = 512

    def body(ya_ref, yb_ref, wa_ref, wb_ref, ga_ref, gb_ref, bg_ref, mg_ref, pa_ref, pb_ref):
        pa = _dot(ya_ref[...], wa_ref[...])
        pb = _dot(yb_ref[...], wb_ref[...])
        sa = _sigmoid(ga_ref[...] + bg_ref[0:1, :])
        sb = _sigmoid(gb_ref[...] + bg_ref[1:2, :])
        mg_ref[...] = (sa * pa + sb * pb).astype(BF16)
        pa_ref[...] = pa.astype(BF16)
        pb_ref[...] = pb.astype(BF16)

    out = jax.ShapeDtypeStruct((s, D), BF16)
    return pl.pallas_call(
        body, name="proj_merge", grid=(s // tm,),
        in_specs=[_rows(tm, D), _rows(tm, D), _full((D, D)), _full((D, D)),
                  _rows(tm, D, 5), _rows(tm, D, 6), _full((2, D))],
        out_specs=[_rows(tm, D)] * 3, out_shape=[out] * 3, compiler_params=_params("parallel"),
    )(ya, yb, wa, wb, z, z, bg)


def out_norm(merged, w_out, x, g_post, g_fpre):
    s = x.shape[0]
    tm = 512

    def body(mg_ref, w_ref, x_ref, gp_ref, gf_ref, o_ref, x1_ref, h2_ref):
        o = _dot(mg_ref[...], w_ref[...])
        ohat, _ = _rms(o)
        x1 = x_ref[...] + ohat * gp_ref[...]
        x1hat, _ = _rms(x1)
        o_ref[...] = o
        x1_ref[...] = x1
        h2_ref[...] = (x1hat * gf_ref[...]).astype(BF16)

    return pl.pallas_call(
        body, name="out_norm", grid=(s // tm,),
        in_specs=[_rows(tm, D), _full((D, D)), _rows(tm, D), _full((1, D)), _full((1, D))],
        out_specs=[_rows(tm, D)] * 3,
        out_shape=[jax.ShapeDtypeStruct((s, D), F32), jax.ShapeDtypeStruct((s, D), F32),
                   jax.ShapeDtypeStruct((s, D), BF16)],
        compiler_params=_params("parallel"),
    )(merged, w_out, x, g_post, g_fpre)


def mm_ff1(h2, wg):
    s = h2.shape[0]
    tm = 1024

    def body(a_ref, b_ref, o_ref, r_ref):
        a = _dot(a_ref[...], b_ref[...])
        o_ref[...] = a
        r = jnp.maximum(a, 0.0)
        r_ref[...] = (r * r).astype(BF16)

    return pl.pallas_call(
        body, name="mm_ff1", grid=(s // tm, N_CHIPS),
        in_specs=[pl.BlockSpec((tm, D), lambda i, j: (i, 0)), pl.BlockSpec((None, D, D), lambda i, j: (j, 0, 0))],
        out_specs=[pl.BlockSpec((tm, D), lambda i, j: (i, j))] * 2,
        out_shape=[jax.ShapeDtypeStruct((s, D_FF), F32), jax.ShapeDtypeStruct((s, D_FF), BF16)],
        compiler_params=_params("parallel", "parallel"),
    )(h2, wg)


def ff2_loss(rl, w_ff2, x1, target, g_fpost):
    s = x1.shape[0]
    tm = 256

    def body(rl_ref, w_ref, x1_ref, t_ref, g_ref, dy_ref, df_ref, dg_ref, loss_ref):
        @pl.when(pl.program_id(0) == 0)
        def _():
            dg_ref[...] = jnp.zeros_like(dg_ref)
            loss_ref[...] = jnp.zeros_like(loss_ref)

        f = _dot(rl_ref[...], w_ref[...])
        fhat, r = _rms(f)
        err = x1_ref[...] + fhat * g_ref[...] - t_ref[...]
        loss_ref[...] += 0.5 * jnp.sum(jnp.mean(err * err, axis=-1, keepdims=True), axis=0, keepdims=True)
        dy = err * (1.0 / D)
        dy_ref[...] = dy
        dg_ref[...] += jnp.sum(dy * fhat, axis=0, keepdims=True)
        df_ref[...] = _rms_bwd(dy * g_ref[...], fhat, r).astype(BF16)

    return pl.pallas_call(
        body, name="ff2_loss", grid=(s // tm,),
        in_specs=[_rows(tm, D_FF), _full((D_FF, D)), _rows(tm, D), _rows(tm, D), _full((1, D))],
        out_specs=[_rows(tm, D), _rows(tm, D), _full((1, D)), _full((1, 1))],
        out_shape=[jax.ShapeDtypeStruct((s, D), F32), jax.ShapeDtypeStruct((s, D), BF16),
                   jax.ShapeDtypeStruct((1, D), F32), jax.ShapeDtypeStruct((1, 1), F32)],
        compiler_params=_params("arbitrary"),
    )(rl, w_ff2, x1, target, g_fpost)


def mm_tn(name, a, b, ta, tb, out_shape, out_spec):
    s = a.shape[0]

    def body(a_ref, b_ref, o_ref):
        o_ref[...] = _dot(a_ref[...], b_ref[...], TN)

    return pl.pallas_call(
        body, name=name, grid=(a.shape[1] // ta, b.shape[1] // tb),
        in_specs=[pl.BlockSpec((s, ta), lambda i, j: (0, i)), pl.BlockSpec((s, tb), lambda i, j: (0, j))],
        out_specs=out_spec, out_shape=jax.ShapeDtypeStruct(out_shape, F32),
        compiler_params=_params("parallel", "parallel"),
    )(a, b)


def mm_nt(name, a, w):
    s = a.shape[0]
    tm = 512

    def body(a_ref, w_ref, o_ref):
        o_ref[...] = _dot(a_ref[...], w_ref[...], NT)

    return pl.pallas_call(
        body, name=name, grid=(s // tm,), in_specs=[_rows(tm, D), _full((D, D))], out_specs=_rows(tm, D),
        out_shape=jax.ShapeDtypeStruct((s, D), F32), compiler_params=_params("parallel"),
    )(a, w)


def ff2_bwd(df, w_ff2, a):
    s = df.shape[0]
    tm = 1024

    def body(df_ref, w_ref, a_ref, da_ref):
        drl = _dot(df_ref[...], w_ref[...], NT)
        da_ref[...] = (drl * (2.0 * jnp.maximum(a_ref[...], 0.0))).astype(BF16)

    return pl.pallas_call(
        body, name="ff2_bwd", grid=(s // tm, D_FF // D),
        in_specs=[pl.BlockSpec((tm, D), lambda i, j: (i, 0)), pl.BlockSpec((D, D), lambda i, j: (j, 0)),
                  pl.BlockSpec((tm, D), lambda i, j: (i, j))],
        out_specs=pl.BlockSpec((tm, D), lambda i, j: (i, j)),
        out_shape=jax.ShapeDtypeStruct((s, D_FF), BF16), compiler_params=_params("parallel", "parallel"),
    )(df, w_ff2, a)


def ff1_bwd_norms(da, wg, x1, o, dy, g_fpre, g_post):
    s = x1.shape[0]
    tm = 512

    def body(da_ref, w_ref, x1_ref, o_ref, dy_ref, gf_ref, gp_ref, dx1_ref, do_ref, dgf_ref, dgp_ref, acc_ref):
        i, k = pl.program_id(0), pl.program_id(1)

        @pl.when((i == 0) & (k == 0))
        def _():
            dgf_ref[...] = jnp.zeros_like(dgf_ref)
            dgp_ref[...] = jnp.zeros_like(dgp_ref)

        part = _dot(da_ref[...], w_ref[...], NT)

        @pl.when(k == 0)
        def _():
            acc_ref[...] = part

        @pl.when(k > 0)
        def _():
            acc_ref[...] += part

        @pl.when(k == N_CHIPS - 1)
        def _():
            dh2 = acc_ref[...]
            x1hat, r2 = _rms(x1_ref[...])
            dgf_ref[...] += jnp.sum(dh2 * x1hat, axis=0, keepdims=True)
            dx1 = dy_ref[...] + _rms_bwd(dh2 * gf_ref[...], x1hat, r2)
            ohat, r1 = _rms(o_ref[...])
            dgp_ref[...] += jnp.sum(dx1 * ohat, axis=0, keepdims=True)
            dx1_ref[...] = dx1
            do_ref[...] = _rms_bwd(dx1 * gp_ref[...], ohat, r1).astype(BF16)

    row = pl.BlockSpec((tm, D), lambda i, k: (i, 0))
    vec = pl.BlockSpec((1, D), lambda i, k: (0, 0))
    return pl.pallas_call(
        body, name="ff1_bwd_norms", grid=(s // tm, N_CHIPS),
        in_specs=[pl.BlockSpec((tm, D), lambda i, k: (i, k)), pl.BlockSpec((None, D, D), lambda i, k: (k, 0, 0)),
                  row, row, row, vec, vec],
        out_specs=[row, row, vec, vec],
        out_shape=[jax.ShapeDtypeStruct((s, D), F32), jax.ShapeDtypeStruct((s, D), BF16),
                   jax.ShapeDtypeStruct((1, D), F32), jax.ShapeDtypeStruct((1, D), F32)],
        scratch_shapes=[pltpu.VMEM((tm, D), F32)], compiler_params=_params("arbitrary", "arbitrary"),
    )(da, wg, x1, o, dy, g_fpre, g_post)


def out_bwd_gates(do, w_out, pa, pb, z, bg):
    s = do.shape[0]
    tm = 512

    def body(do_ref, w_ref, pa_ref, pb_ref, ga_ref, gb_ref, bg_ref, dpa_ref, dpb_ref, dga_ref, dgb_ref, dbg_ref):
        @pl.when(pl.program_id(0) == 0)
        def _():
            dbg_ref[...] = jnp.zeros_like(dbg_ref)

        dm = _dot(do_ref[...], w_ref[...], NT)
        sa = _sigmoid(ga_ref[...] + bg_ref[0:1, :])
        sb = _sigmoid(gb_ref[...] + bg_ref[1:2, :])
        dpa_ref[...] = (dm * sa).astype(BF16)
        dpb_ref[...] = (dm * sb).astype(BF16)
        dga = dm * pa_ref[...].astype(F32) * (sa * (1.0 - sa))
        dgb = dm * pb_ref[...].astype(F32) * (sb * (1.0 - sb))
        dga_ref[...] = dga.astype(BF16)
        dgb_ref[...] = dgb.astype(BF16)
        dbg_ref[0:1, :] += jnp.sum(dga, axis=0, keepdims=True)
        dbg_ref[1:2, :] += jnp.sum(dgb, axis=0, keepdims=True)

    out = jax.ShapeDtypeStruct((s, D), BF16)
    return pl.pallas_call(
        body, name="out_bwd_gates", grid=(s // tm,),
        in_specs=[_rows(tm, D), _full((D, D)), _rows(tm, D), _rows(tm, D), _rows(tm, D, 5), _rows(tm, D, 6),
                  _full((2, D))],
        out_specs=[_rows(tm, D)] * 4 + [_full((2, D))],
        out_shape=[out] * 4 + [jax.ShapeDtypeStruct((2, D), F32)], compiler_params=_params("arbitrary"),
    )(do, w_out, pa, pb, z, z, bg)


def gating_bwd(z, dya, ln_g, ln_b, w_s, bs_t):
    s = z.shape[0]
    ones = functools.partial(jnp.ones, (8, CHUNK), BF16)

    def body(u_ref, v_ref, dya_ref, lg_ref, lb_ref, ws_ref, bst_ref,
             du_ref, dv_ref, dws_ref, dbs_ref, dlg_ref, dlb_ref, dvn_ref):
        ci = pl.program_id(0)

        @pl.when(ci == 0)
        def _():
            dws_ref[...] = jnp.zeros_like(dws_ref)
            dbs_ref[...] = jnp.zeros_like(dbs_ref)
            dlg_ref[...] = jnp.zeros_like(dlg_ref)
            dlb_ref[...] = jnp.zeros_like(dlb_ref)

        ug, dug_du = _gelu_and_grad(u_ref[...])
        vg, dvg_dv = _gelu_and_grad(v_ref[...])
        vhat, rstd = _layer_norm(vg)
        vn = (vhat * lg_ref[...] + lb_ref[...]).astype(BF16)
        dya = dya_ref[...]
        for g in range(GROUPS):
            cols = slice(g * CHUNK, (g + 1) * CHUNK)
            ws = _tril_ws(ws_ref, g)
            mixed = _dot(ws, vn[:, cols]) + bst_ref[:, g:g + 1]
            du_ref[:, cols] = (dya[:, cols] * mixed * dug_du[:, cols]).astype(BF16)
            dmix = (dya[:, cols] * ug[:, cols]).astype(BF16)
            dbs_ref[g] += _dot(ones(), dmix, NT)
            dws_ref[g] += _dot(dmix, vn[:, cols], NT)
            dvn_ref[:, cols] = _dot(ws, dmix, TN)
        dvn = dvn_ref[...]
        dlg_ref[...] += jnp.sum(dvn * vhat, axis=0, keepdims=True)
        dlb_ref[...] += jnp.sum(dvn, axis=0, keepdims=True)
        dvh = dvn * lg_ref[...]
        dvg = rstd * (dvh - jnp.mean(dvh, axis=-1, keepdims=True)
                      - vhat * jnp.mean(dvh * vhat, axis=-1, keepdims=True))
        dv_ref[...] = (dvg * dvg_dv).astype(BF16)

        @pl.when(ci == pl.num_programs(0) - 1)
        def _():
            r = lax.broadcasted_iota(jnp.int32, (CHUNK, CHUNK), 0)
            c = lax.broadcasted_iota(jnp.int32, (CHUNK, CHUNK), 1)
            for g in range(GROUPS):
                dws_ref[g] = jnp.where(c <= r, dws_ref[g], 0.0)

    out = jax.ShapeDtypeStruct((s, D), BF16)
    return pl.pallas_call(
        body, name="gating_bwd", grid=(s // CHUNK,),
        in_specs=[_rows(CHUNK, D, 0), _rows(CHUNK, D, 1), _rows(CHUNK, D), _full((1, D)), _full((1, D)),
                  _full((GROUPS, CHUNK, CHUNK)), _full((CHUNK, GROUPS))],
        out_specs=[_rows(CHUNK, D), _rows(CHUNK, D), _full((GROUPS, CHUNK, CHUNK)), _full((GROUPS, 8, CHUNK)),
                   _full((1, D)), _full((1, D))],
        out_shape=[out, out, jax.ShapeDtypeStruct((GROUPS, CHUNK, CHUNK), F32),
                   jax.ShapeDtypeStruct((GROUPS, 8, CHUNK), F32),
                   jax.ShapeDtypeStruct((1, D), F32), jax.ShapeDtypeStruct((1, D), F32)],
        scratch_shapes=[pltpu.VMEM((CHUNK, D), F32)], compiler_params=_params("arbitrary"),
    )(z, z, dya, ln_g, ln_b, w_s, bs_t)


def attn_bwd(z, yb, dyb, lse, logc, ka, kb, scattering):
    s = z.shape[0]
    nq = s // ATT_T
    t = ATT_T
    qcol, kcol, vcol = 2 * D // 128, 3 * D // 128, 4 * D // 128
    scale = 1.0 / math.sqrt(HEAD_DIM)

    n = len(scattering)

    def body(*refs):
        q_ref, k_ref, v_ref, y_ref, dy_ref, lse_ref, lc_ref, ka_ref, kb_ref = refs[:9]
        dq_ref, dk_ref, dv_ref = refs[9 + n:12 + n]
        qa_s, qt_s, da_s, dt_s, dq_s, dkt_s, dvt_s = refs[12 + 2 * n:19 + 2 * n]
        pair, j = pl.program_id(0), pl.program_id(1)
        first, lane, ones = _head_masks()
        if n:
            send, finish = _scatter_phases(refs[9:9 + n], refs[12 + n:12 + 2 * n], *refs[19 + 2 * n:])
            pl.when((pair == 0) & (j == 0))(send)

        @pl.when(j == 0)
        def _():
            dq_s[...] = jnp.zeros_like(dq_s)
            for ib in range(nq):
                rows = slice(ib * t, (ib + 1) * t)
                q = q_ref[rows, :] * scale
                lse = lse_ref[rows, :]
                qa_s[0, ib] = jnp.where(first, q, _place3(lane, HEAD_DIM + 2 * AUG, _split3(-lse[:, 0:1]),
                                                          ones(0, 2 * AUG))).astype(BF16)
                qa_s[1, ib] = jnp.where(first, _place3(lane, 2 * AUG, _split3(-lse[:, HEAD_DIM:HEAD_DIM + 1]),
                                                       ones(1, 2 * AUG)), q).astype(BF16)
                qt_s[ib, :, 0:t] = jnp.where(first, q, 0.0).T.astype(BF16)
                qt_s[ib, :, t:2 * t] = jnp.where(first, 0.0, q).T.astype(BF16)
                do = dy_ref[rows, :]
                prod = do * y_ref[rows, :].astype(F32)
                dd0 = jnp.sum(jnp.where(first, prod, 0.0), axis=-1, keepdims=True)
                dd1 = jnp.sum(jnp.where(first, 0.0, prod), axis=-1, keepdims=True)
                da_s[0, ib] = jnp.where(first, do, _place3(lane, HEAD_DIM, _split3(-dd0), 0.0)).astype(BF16)
                da_s[1, ib] = jnp.where(first, _place3(lane, 0, _split3(-dd1), 0.0), do).astype(BF16)
                dt_s[ib, :, 0:t] = jnp.where(first, do, 0.0).T.astype(BF16)
                dt_s[ib, :, t:2 * t] = jnp.where(first, 0.0, do).T.astype(BF16)

        kj = k_ref[...]
        vj = v_ref[...]
        k0a = jnp.where(first, kj, ka_ref[0] + kb_ref[0, pl.ds(j, 1), :]).astype(BF16)
        k1a = jnp.where(first, ka_ref[1] + kb_ref[1, pl.ds(j, 1), :], kj).astype(BF16)
        kst = jnp.concatenate([jnp.where(first, kj, 0.0), jnp.where(first, 0.0, kj)], axis=0).astype(BF16)
        v0a = jnp.where(first, vj, ones(0, AUG)).astype(BF16)
        v1a = jnp.where(first, ones(1, AUG), vj).astype(BF16)
        dkt_s[...] = jnp.zeros_like(dkt_s)
        dvt_s[...] = jnp.zeros_like(dvt_s)

        def step(i, _):
            lc = lc_ref[i - j]
            p0 = jnp.exp(_dot(qa_s[0, i], k0a, NT) + lc)
            p1 = jnp.exp(_dot(qa_s[1, i], k1a, NT) + lc)
            e0 = (p0 * _dot(da_s[0, i], v0a, NT)).astype(BF16)
            e1 = (p1 * _dot(da_s[1, i], v1a, NT)).astype(BF16)
            rows = pl.ds(pl.multiple_of(i * t, t), t)
            dq_s[rows, :] += _dot(jnp.concatenate([e0, e1], axis=1), kst)
            dvt_s[...] += _dot(dt_s[i], jnp.concatenate([p0.astype(BF16), p1.astype(BF16)], axis=0))
            dkt_s[...] += _dot(qt_s[i], jnp.concatenate([e0, e1], axis=0))
            return 0

        lax.fori_loop(j, nq, step, 0)
        dk_ref[...] = dkt_s[...].T.astype(BF16)
        dv_ref[...] = dvt_s[...].T.astype(BF16)

        @pl.when(j == nq - 1)
        def _():
            dq_ref[...] = (dq_s[...] * scale).astype(BF16)

        if n:
            pl.when((pair == HEADS // 2 - 1) & (j == nq - 1))(finish)

    colblock = lambda c: pl.BlockSpec((s, 128), lambda p, j: (0, c + p))
    blk = lambda c: pl.BlockSpec((t, 128), lambda p, j: (j, c + p))
    out = jax.ShapeDtypeStruct((s, D), BF16)
    res = pl.pallas_call(
        body, name="attn_bwd", grid=(HEADS // 2, nq),
        in_specs=[colblock(qcol), blk(kcol), blk(vcol), colblock(0), colblock(0), colblock(0),
                  _full((nq, t, t)), pl.BlockSpec((None, 2, t, 128), lambda p, j: (p, 0, 0, 0)),
                  pl.BlockSpec((None, 2, nq, 128), lambda p, j: (p, 0, 0, 0))] + [ANY] * n,
        out_specs=[colblock(0), blk(0), blk(0)] + [ANY] * n, out_shape=[out] * 3 + _scatter_shapes(scattering),
        scratch_shapes=[pltpu.VMEM((2, nq, t, 128), BF16), pltpu.VMEM((nq, 128, 2 * t), BF16),
                        pltpu.VMEM((2, nq, t, 128), BF16), pltpu.VMEM((nq, 128, 2 * t), BF16),
                        pltpu.VMEM((s, 128), F32), pltpu.VMEM((128, t), F32), pltpu.VMEM((128, t), F32)]
        + (_scatter_sems(n) if n else []),
        compiler_params=_params("arbitrary", "arbitrary", communicates=bool(n)),
    )(z, z, z, yb, dyb, lse, logc, ka, kb, *scattering)
    return res[0], res[1], res[2], res[3:]


def in_bwd_norm(dz, wg, x, dx1, g_pre, scattering):
    s = x.shape[0]
    tm = 512
    n = len(scattering)

    def body(*refs):
        dz_ref, w_ref, x_ref, dx1_ref, g_ref = refs[:5]
        dx_ref, dg_ref = refs[5 + n:7 + n]
        acc_ref = refs[7 + 2 * n]
        i, k = pl.program_id(0), pl.program_id(1)
        if n:
            send, finish = _scatter_phases(refs[5:5 + n], refs[7 + n:7 + 2 * n], *refs[8 + 2 * n:])
            pl.when((i == 0) & (k == 0))(send)

        @pl.when((i == 0) & (k == 0))
        def _():
            dg_ref[...] = jnp.zeros_like(dg_ref)

        part = _dot(dz_ref[...], w_ref[...], NT)

        @pl.when(k == 0)
        def _():
            acc_ref[...] = part

        @pl.when(k > 0)
        def _():
            acc_ref[...] += part

        @pl.when(k == N_CHIPS - 1)
        def _():
            dh = acc_ref[...]
            xhat, r = _rms(x_ref[...])
            dg_ref[...] += jnp.sum(dh * xhat, axis=0, keepdims=True)
            dx_ref[...] = dx1_ref[...] + _rms_bwd(dh * g_ref[...], xhat, r)

        if n:
            pl.when((i == s // tm - 1) & (k == N_CHIPS - 1))(finish)

    row = pl.BlockSpec((tm, D), lambda i, k: (i, 0))
    vec = pl.BlockSpec((1, D), lambda i, k: (0, 0))
    res = pl.pallas_call(
        body, name="in_bwd_norm", grid=(s // tm, N_CHIPS),
        in_specs=[pl.BlockSpec((tm, IN_SHARD), lambda i, k: (i, k)),
                  pl.BlockSpec((None, D, IN_SHARD), lambda i, k: (k, 0, 0)), row, row, vec] + [ANY] * n,
        out_specs=[row, vec] + [ANY] * n,
        out_shape=[jax.ShapeDtypeStruct((s, D), F32), jax.ShapeDtypeStruct((1, D), F32)]
        + _scatter_shapes(scattering),
        scratch_shapes=[pltpu.VMEM((tm, D), F32)] + (_scatter_sems(n) if n else []),
        compiler_params=_params("arbitrary", "arbitrary", communicates=bool(n)),
    )(dz, wg, x, dx1, g_pre, *scattering)
    return res[0], res[1], res[2:]


def _adamw_math(w, g, m, v):
    m = ADAM_B1 * m + (1.0 - ADAM_B1) * g
    v = ADAM_B2 * v + (1.0 - ADAM_B2) * (g * g)
    m_hat = m / (1.0 - ADAM_B1 ** ADAM_STEP)
    v_hat = v / (1.0 - ADAM_B2 ** ADAM_STEP)
    delta = -ADAM_LR * (m_hat / (jnp.sqrt(v_hat) + ADAM_EPS) + ADAM_WD * w)
    return delta, m, v


def adamw(name, w, g, m, v, tr):
    r, c = w.shape

    def body(w_ref, g_ref, m_ref, v_ref, d_ref, nm_ref, nv_ref):
        d_ref[...], nm_ref[...], nv_ref[...] = _adamw_math(w_ref[...], g_ref[...], m_ref[...], v_ref[...])

    out = jax.ShapeDtypeStruct((r, c), F32)
    return pl.pallas_call(
        body, name=name, grid=(r // tr,), in_specs=[_rows(tr, c)] * 4, out_specs=[_rows(tr, c)] * 3,
        out_shape=[out] * 3, compiler_params=_params("parallel"),
    )(w, g, m, v)


def add_halves(name, g, recv, c_idx, tr):
    n, h, c = recv.shape

    def body(c_ref, g_ref, r_ref, o_ref):
        o_ref[...] = (g_ref[...] + r_ref[...]).astype(BF16)

    nb = h // tr
    return pl.pallas_call(
        body, name=name,
        grid_spec=pltpu.PrefetchScalarGridSpec(
            num_scalar_prefetch=1, grid=(n, nb),
            in_specs=[pl.BlockSpec((None, tr, c), lambda k, i, c_ref: (k, c_ref[0] * nb + i, 0)),
                      pl.BlockSpec((None, tr, c), lambda k, i, c_ref: (k, i, 0))],
            out_specs=pl.BlockSpec((None, tr, c), lambda k, i, c_ref: (k, i, 0))),
        out_shape=jax.ShapeDtypeStruct((n, h, c), BF16), compiler_params=_params("parallel", "parallel"),
    )(c_idx, g, recv)


def sum_chips(name, parts, recv, where, tr):
    n, h, c = recv.shape
    nb = h // tr

    def body(w_ref, p_ref, r_ref, o_ref):
        acc = p_ref[...].astype(F32)
        for k in range(n):
            acc = acc + r_ref[k].astype(F32)
        o_ref[...] = acc

    return pl.pallas_call(
        body, name=name,
        grid_spec=pltpu.PrefetchScalarGridSpec(
            num_scalar_prefetch=1, grid=(nb,),
            in_specs=[pl.BlockSpec((None, tr, c), lambda i, w_ref: (w_ref[0], i, 0)),
                      pl.BlockSpec((n, tr, c), lambda i, w_ref: (0, i, 0))],
            out_specs=pl.BlockSpec((tr, c), lambda i, w_ref: (w_ref[1] * nb + i, 0))),
        out_shape=jax.ShapeDtypeStruct((2 * h, c), F32), compiler_params=_params("parallel"),
    )(where, parts, recv)


def place_shard(name, shard, where, dtype, tr):
    r, c = shard.shape

    def body(w_ref, s_ref, o_ref):
        o_ref[...] = s_ref[...].astype(dtype)

    return pl.pallas_call(
        body, name=name,
        grid_spec=pltpu.PrefetchScalarGridSpec(
            num_scalar_prefetch=1, grid=(r // tr,),
            in_specs=[pl.BlockSpec((tr, c), lambda i, w_ref: (i, 0))],
            out_specs=pl.BlockSpec((None, tr, c), lambda i, w_ref: (w_ref[0], i, 0))),
        out_shape=jax.ShapeDtypeStruct((N_CHIPS, r, c), dtype), compiler_params=_params("parallel"),
    )(where, shard)


ANY = pl.BlockSpec(memory_space=pl.ANY)


def _place():
    x, y, c = lax.axis_index("x"), lax.axis_index("y"), lax.axis_index("c")
    chips = [(1 - x, y), (x, 1 - y), (1 - x, 1 - y)]
    return x, y, c, chips


def gather_shards(arrays):
    n = len(arrays)

    def body(*refs):
        send, pass_on, finish = _gather_phases(refs[n:2 * n], *refs[2 * n:])
        send()
        pass_on()
        finish()

    return pl.pallas_call(
        body, name="gather_shards", in_specs=[ANY] * n, out_specs=[ANY] * n,
        out_shape=[jax.ShapeDtypeStruct(a.shape, a.dtype) for a in arrays],
        input_output_aliases={w: w for w in range(n)}, scratch_shapes=_gather_sems(n),
        compiler_params=pltpu.CompilerParams(has_side_effects=True),
    )(*arrays)


def _gather_sems(n):
    return [pltpu.SemaphoreType.DMA((6 * n,)), pltpu.SemaphoreType.DMA((6 * n,))]


def _gather_phases(out, send_sems, recv_sems):
    n = len(out)
    x, y, c, chips = _place()
    me = 2 * x + y
    sibling = (x, y, 1 - c)

    def half(w, chip, core):
        h = out[w].shape[1] // 2
        return out[w].at[chip, pl.ds(core * h, h)]

    def copy(k, block, to):
        return pltpu.make_async_remote_copy(src_ref=block, dst_ref=block, send_sem=send_sems.at[k],
                                            recv_sem=recv_sems.at[k], device_id=to, device_id_type=MESH)

    def over_ici(w, j, chip):
        return copy(3 * w + j, half(w, chip, c), (chips[j][0], chips[j][1], c))

    def over_d2d(w, j, core):
        return copy(3 * n + 3 * w + j, half(w, 2 * chips[j][0] + chips[j][1], core), sibling)

    pairs = [(w, j) for w in range(n) for j in range(3)]

    def send():
        for w, j in pairs:
            over_ici(w, j, me).start()

    def pass_on():
        for w, j in pairs:
            over_ici(w, j, 2 * chips[j][0] + chips[j][1]).wait_recv()
            over_d2d(w, j, c).start()

    def finish():
        for w, j in pairs:
            over_d2d(w, j, 1 - c).wait_recv()
        for w, j in pairs:
            over_ici(w, j, me).wait_send()
            over_d2d(w, j, c).wait_send()

    return send, pass_on, finish


def swap_halves(name, grads):
    n = len(grads)

    def body(*refs):
        g, out = refs[:n], refs[n:2 * n]
        send_sems, recv_sems = refs[2 * n:]
        x, y, c, _ = _place()
        copies = []
        for w in range(n):
            h = g[w].shape[1] // 2
            copies.append(pltpu.make_async_remote_copy(
                src_ref=g[w].at[:, pl.ds((1 - c) * h, h)], dst_ref=out[w], send_sem=send_sems.at[w],
                recv_sem=recv_sems.at[w], device_id=(x, y, 1 - c), device_id_type=MESH))
        for cp in copies:
            cp.start()
        for cp in copies:
            cp.wait()

    return pl.pallas_call(
        body, name=name, in_specs=[ANY] * n, out_specs=[ANY] * n,
        out_shape=[jax.ShapeDtypeStruct((a.shape[0], a.shape[1] // 2, a.shape[2]), a.dtype) for a in grads],
        scratch_shapes=[pltpu.SemaphoreType.DMA((n,)), pltpu.SemaphoreType.DMA((n,))],
        compiler_params=pltpu.CompilerParams(has_side_effects=True),
    )(*grads)


def scatter_chips(parts):
    n = len(parts)

    def body(*refs):
        send, finish = _scatter_phases(refs[:n], refs[n:2 * n], *refs[2 * n:])
        send()
        finish()

    return pl.pallas_call(
        body, name="scatter_chips", in_specs=[ANY] * n, out_specs=[ANY] * n,
        out_shape=_scatter_shapes(parts), scratch_shapes=_scatter_sems(n),
        compiler_params=pltpu.CompilerParams(has_side_effects=True),
    )(*parts)


def _scatter_shapes(parts):
    return [jax.ShapeDtypeStruct((3,) + a.shape[1:], a.dtype) for a in parts]


def _scatter_sems(n):
    return [pltpu.SemaphoreType.DMA((3 * n,)), pltpu.SemaphoreType.DMA((3 * n,))]


def _scatter_phases(p, out, send_sems, recv_sems):
    x, y, c, chips = _place()

    def copies():
        return [pltpu.make_async_remote_copy(
            src_ref=p[w].at[2 * px + py], dst_ref=out[w].at[j], send_sem=send_sems.at[3 * w + j],
            recv_sem=recv_sems.at[3 * w + j], device_id=(px, py, c), device_id_type=MESH)
            for w in range(len(p)) for j, (px, py) in enumerate(chips)]

    def send():
        for cp in copies():
            cp.start()

    def finish():
        for cp in copies():
            cp.wait()

    return send, finish


def join_halves(arrays):
    n = len(arrays)

    def body(*refs):
        out = refs[n:2 * n]
        send_sems, recv_sems = refs[2 * n:]
        x, y, c, _ = _place()

        def copy(w, core):
            h = out[w].shape[0] // 2
            rows = out[w].at[pl.ds(core * h, h)]
            return pltpu.make_async_remote_copy(
                src_ref=rows, dst_ref=rows, send_sem=send_sems.at[w], recv_sem=recv_sems.at[w],
                device_id=(x, y, 1 - c), device_id_type=MESH)

        for w in range(n):
            copy(w, c).start()
        for w in range(n):
            copy(w, 1 - c).wait_recv()
        for w in range(n):
            copy(w, c).wait_send()

    return pl.pallas_call(
        body, name="join_halves", in_specs=[ANY] * n, out_specs=[ANY] * n,
        out_shape=[jax.ShapeDtypeStruct(a.shape, a.dtype) for a in arrays],
        input_output_aliases={w: w for w in range(n)},
        scratch_shapes=[pltpu.SemaphoreType.DMA((n,)), pltpu.SemaphoreType.DMA((n,))],
        compiler_params=pltpu.CompilerParams(has_side_effects=True),
    )(*arrays)


def allreduce_small(packed):
    r, c = packed.shape
    n_dev = 8

    def body(x_ref, all_ref, sum_ref, send_sems, recv_sems, local_sem):
        x, y, cc, chips = _place()
        me, sibling = (x, y, cc), (x, y, 1 - cc)

        def rows(px, py, pc):
            return all_ref.at[4 * px + 2 * py + pc]

        def copy(k, block, to, src=None):
            return pltpu.make_async_remote_copy(
                src_ref=rows(*block) if src is None else src, dst_ref=rows(*block), send_sem=send_sems.at[k],
                recv_sem=recv_sems.at[k], device_id=to, device_id_type=MESH)

        mine = pltpu.make_async_copy(x_ref, rows(*me), local_sem)
        mine.start()
        first = [copy(0, me, sibling, src=x_ref)]
        first += [copy(1 + j, me, (*chip, cc), src=x_ref) for j, chip in enumerate(chips)]
        for cp in first:
            cp.start()
        passed = [copy(4 + j, (*chip, cc), sibling) for j, chip in enumerate(chips)]
        for j, chip in enumerate(chips):
            copy(1 + j, (*chip, cc), me).wait_recv()
            passed[j].start()
        copy(0, sibling, me).wait_recv()
        for j, chip in enumerate(chips):
            copy(4 + j, (*chip, 1 - cc), me).wait_recv()
        for cp in first + passed:
            cp.wait_send()
        mine.wait()
        acc = all_ref[0]
        for k in range(1, n_dev):
            acc = acc + all_ref[k]
        sum_ref[...] = acc

    vm = pl.BlockSpec(memory_space=pltpu.VMEM)
    return pl.pallas_call(
        body, name="allreduce_small", in_specs=[vm], out_specs=[vm, vm],
        out_shape=[jax.ShapeDtypeStruct((n_dev, r, c), F32), jax.ShapeDtypeStruct((r, c), F32)],
        scratch_shapes=[pltpu.SemaphoreType.DMA((7,)), pltpu.SemaphoreType.DMA((7,)), pltpu.SemaphoreType.DMA],
        compiler_params=pltpu.CompilerParams(has_side_effects=True, vmem_limit_bytes=VMEM_LIMIT),
    )(packed)[1]


def local_step(x, target, vecs, w_s, bs_t, bg, wg_in, late, gather_late=False, pre_reduce=None):
    g_pre, ln_g, ln_b, g_post, g_fpre, g_fpost = vecs
    s = x.shape[0]
    logc = _attn_tables(s)
    ka, kb = _alibi_tables(s)

    h = norm_pre(x, g_pre)
    z = mm_in(h, wg_in)
    ya = gating_fwd(z, ln_g, ln_b, w_s, bs_t)
    yb, lse, gathered = attn_fwd(z, logc, ka, kb, late if gather_late else [])
    wg_a, wg_b, wg_out, wg_ff1, wg_ff2 = gathered if gather_late else late
    w_a, w_b, w_out, w_ff2 = wg_a.reshape(D, D), wg_b.reshape(D, D), wg_out.reshape(D, D), wg_ff2.reshape(D_FF, D)
    merged, pa, pb = proj_merge(ya, yb, w_a, w_b, z, bg)
    o, x1, h2 = out_norm(merged, w_out, x, g_post, g_fpre)
    a, rl = mm_ff1(h2, wg_ff1)
    dy, df, d_gfpost, loss = ff2_loss(rl, w_ff2, x1, target, g_fpost)

    half_cols = pl.BlockSpec((D, D // 2), lambda i, j: (0, j))
    d_wff2 = mm_tn("dw_ff2", rl, df, D // 2, D, (D_FF, D), pl.BlockSpec((D // 2, D), lambda i, j: (i, 0)))
    da = ff2_bwd(df, w_ff2, a)
    d_wff1 = mm_tn("dw_ff1", h2, da, D, D // 2, (N_CHIPS, D, D),
                   pl.BlockSpec((None, D, D // 2), lambda i, j: (j // 2, 0, j % 2)))
    dx1, do, d_gfpre, d_gpost = ff1_bwd_norms(da, wg_ff1, x1, o, dy, g_fpre, g_post)
    d_wout = mm_tn("dw_out", merged, do, D, D // 2, (D, D), half_cols)
    dpa, dpb, dga, dgb, d_bg = out_bwd_gates(do, w_out, pa, pb, z, bg)
    d_wa = mm_tn("dw_a", ya, dpa, D, D // 2, (D, D), half_cols)
    d_wb = mm_tn("dw_b", yb, dpb, D, D // 2, (D, D), half_cols)
    dya = mm_nt("dy_a", dpa, w_a)
    dyb = mm_nt("dy_b", dpb, w_b)
    du, dv, d_ws, d_bs, d_lng, d_lnb = gating_bwd(z, dya, ln_g, ln_b, w_s, bs_t)
    early = [d_wa.reshape(N_CHIPS, D // N_CHIPS, D), d_wb.reshape(N_CHIPS, D // N_CHIPS, D),
             d_wout.reshape(N_CHIPS, D // N_CHIPS, D), d_wff1, d_wff2.reshape(N_CHIPS, D, D)]
    parts_early = pre_reduce(BIG[1:], early) if pre_reduce else []
    dq, dk, dvb, got_early = attn_bwd(z, yb, dyb, lse, logc, ka, kb, parts_early)
    dz = jnp.concatenate([du, dv, dq, dk, dvb, dga, dgb], axis=1)
    half = IN_SHARD // 2
    d_win = mm_tn("dw_in", h, dz, D, half, (N_CHIPS, D, IN_SHARD),
                  pl.BlockSpec((None, D, half), lambda i, j: (j // 2, 0, j % 2)))
    parts_late = pre_reduce(BIG[:1], [d_win]) if pre_reduce else []
    dx, d_gpre, got_late = in_bwd_norm(dz, wg_in, x, dx1, g_pre, parts_late)

    small = dict(norm_mix_pre=d_gpre, b_gate=d_bg, ln_v_g=d_lng, ln_v_b=d_lnb, w_s=d_ws, b_s=d_bs[:, 0, :],
                 norm_mix_post=d_gpost, norm_ffn_pre=d_gfpre, norm_ffn_post=d_gfpost)
    return loss[0, 0], dx, [d_win] + early, small, parts_late + parts_early, list(got_late) + list(got_early)


BIG = ("w_in", "w_a_proj", "w_b_proj", "w_out", "w_ff1", "w_ff2")
SMALL = ("norm_mix_pre", "ln_v_g", "ln_v_b", "b_s", "norm_mix_post", "norm_ffn_pre", "norm_ffn_post", "w_s", "b_gate")
ORDER = ("norm_mix_pre", "w_in", "b_gate", "ln_v_g", "ln_v_b", "w_s", "b_s", "w_a_proj", "w_b_proj", "w_out",
         "norm_mix_post", "norm_ffn_pre", "w_ff1", "w_ff2", "norm_ffn_post")
PACK_ROWS = 144


def _pack_small(t):
    rows = [t[n].reshape(1, D) for n in SMALL[:7]] + [t["w_s"].reshape(128, D), t["b_gate"].reshape(2, D)]
    used = jnp.concatenate(rows, axis=0)
    return jnp.pad(used, ((0, PACK_ROWS - used.shape[0]), (0, 0)))


def _unpack_small(p, chip):
    out = {n: p[i:i + 1].reshape(1, D) for i, n in enumerate(SMALL[:7])}
    out["b_s"] = out["b_s"].reshape(1, GROUPS, CHUNK)
    out["w_s"] = p[7:135].reshape(1, GROUPS, CHUNK, CHUNK)
    out["b_gate"] = lax.dynamic_slice_in_dim(p[135:137], chip * (D // N_CHIPS), D // N_CHIPS, axis=1).reshape(1, 2, D // N_CHIPS)
    return out


def kernel(x, norm_mix_pre, w_in, b_gate, ln_v_g, ln_v_b, w_s, b_s, w_a_proj, w_b_proj, w_out, norm_mix_post, norm_ffn_pre, w_ff1, w_ff2, norm_ffn_post, loss_target, m_norm_mix_pre, m_w_in, m_b_gate, m_ln_v_g, m_ln_v_b, m_w_s, m_b_s, m_w_a_proj, m_w_b_proj, m_w_out, m_norm_mix_post, m_norm_ffn_pre, m_w_ff1, m_w_ff2, m_norm_ffn_post, v_norm_mix_pre, v_w_in, v_b_gate, v_ln_v_g, v_ln_v_b, v_w_s, v_b_s, v_w_a_proj, v_w_b_proj, v_w_out, v_norm_mix_post, v_norm_ffn_pre, v_w_ff1, v_w_ff2, v_norm_ffn_post):
    w = dict(norm_mix_pre=norm_mix_pre, w_in=w_in, b_gate=b_gate, ln_v_g=ln_v_g, ln_v_b=ln_v_b, w_s=w_s, b_s=b_s,
             w_a_proj=w_a_proj, w_b_proj=w_b_proj, w_out=w_out, norm_mix_post=norm_mix_post,
             norm_ffn_pre=norm_ffn_pre, w_ff1=w_ff1, w_ff2=w_ff2, norm_ffn_post=norm_ffn_post)
    m = dict(norm_mix_pre=m_norm_mix_pre, w_in=m_w_in, b_gate=m_b_gate, ln_v_g=m_ln_v_g, ln_v_b=m_ln_v_b, w_s=m_w_s,
             b_s=m_b_s, w_a_proj=m_w_a_proj, w_b_proj=m_w_b_proj, w_out=m_w_out, norm_mix_post=m_norm_mix_post,
             norm_ffn_pre=m_norm_ffn_pre, w_ff1=m_w_ff1, w_ff2=m_w_ff2, norm_ffn_post=m_norm_ffn_post)
    v = dict(norm_mix_pre=v_norm_mix_pre, w_in=v_w_in, b_gate=v_b_gate, ln_v_g=v_ln_v_g, ln_v_b=v_ln_v_b, w_s=v_w_s,
             b_s=v_b_s, w_a_proj=v_w_a_proj, w_b_proj=v_w_b_proj, w_out=v_w_out, norm_mix_post=v_norm_mix_post,
             norm_ffn_pre=v_norm_ffn_pre, w_ff1=v_w_ff1, w_ff2=v_w_ff2, norm_ffn_post=v_norm_ffn_post)
    chip = 2 * lax.axis_index("x") + lax.axis_index("y")
    core = lax.axis_index("c")

    where = jnp.stack([chip, core]).astype(jnp.int32)
    placed = [place_shard("place_" + n, w[n][0], where, BF16, min(w[n].shape[1], 256)) for n in BIG]
    placed.append(place_shard("place_b_gate", jnp.pad(b_gate[0], ((0, 14), (0, 0))), where, F32, 16))
    wg_in, bg_all = gather_shards([placed[0], placed[6]])
    bg = jnp.transpose(bg_all[:, :2, :], (1, 0, 2)).reshape(2, D)
    c_idx = jnp.reshape(core, (1,)).astype(jnp.int32)

    def pre_reduce(names, grads):
        recv = swap_halves("swap_" + names[0], grads)
        return [add_halves("add_" + n, g, r, c_idx, min(r.shape[1], 256)) for n, g, r in zip(names, grads, recv)]

    vecs = (norm_mix_pre, ln_v_g, ln_v_b, norm_mix_post, norm_ffn_pre, norm_ffn_post)
    loss, dx, _, small, parts, got = local_step(
        x[0], loss_target[0], vecs, w_s[0], b_s[0].T, bg, wg_in, placed[1:6], gather_late=True,
        pre_reduce=pre_reduce)
    loss = lax.psum(loss, ("x", "y", "c"))

    halves = [sum_chips("sum_" + n, p, r, where, min(p.shape[1], 256)) for n, p, r in zip(BIG, parts, got)]
    grads = dict(zip(BIG, join_halves(halves)))

    grads_small = _unpack_small(allreduce_small(_pack_small(small)), chip)

    new = {}
    for n in BIG:
        shape = w[n].shape
        r, c = shape[1], shape[2]
        d, nm, nv = adamw("adamw_" + n, w[n][0], grads[n], m[n][0], v[n][0], min(r, 256))
        new[n] = (grads[n].reshape(shape), d.reshape(shape), nm.reshape(shape), nv.reshape(shape))
    full_bg = lambda t: jnp.zeros((2, D), F32).at[:, :D // N_CHIPS].set(t["b_gate"][0])
    packs = [_pack_small({**{n: t[n] for n in SMALL[:8]}, "b_gate": full_bg(t)}) for t in (w, m, v)]
    g_pack = _pack_small({**{n: grads_small[n] for n in SMALL[:8]},
                          "b_gate": jnp.zeros((2, D), F32).at[:, :D // N_CHIPS].set(grads_small["b_gate"][0])})
    d_p, m_p, v_p = adamw("adamw_small", packs[0], g_pack, packs[1], packs[2], PACK_ROWS)
    d_s, m_s, v_s = (_unpack_small(p, 0) for p in (d_p, m_p, v_p))
    for n in SMALL:
        new[n] = (grads_small[n].reshape(w[n].shape), d_s[n].reshape(w[n].shape), m_s[n].reshape(w[n].shape),
                  v_s[n].reshape(w[n].shape))

    outs = [loss, dx[None]]
    for i in range(4):
        outs += [new[n][i] for n in ORDER]
    return tuple(outs)
```

```python
import functools
import math

import numpy as np
import jax
import jax.numpy as jnp
from jax import lax
from jax.experimental import pallas as pl
from jax.experimental.pallas import tpu as pltpu

F32 = jnp.float32
BF16 = jnp.bfloat16
MESH = pl.DeviceIdType.MESH

D = 1024
EPS = 1e-6
CHUNK = 128
GROUPS = 8
HEADS = 16
HEAD_DIM = 64
ATT_T = 256
N_CHIPS = 4
D_FF = 4 * D
IN_COLS = 7 * D
IN_SHARD = IN_COLS // N_CHIPS
MASKED = -1e30
VMEM_LIMIT = 56 * 2 ** 20

ADAM_LR, ADAM_B1, ADAM_B2, ADAM_EPS, ADAM_WD, ADAM_STEP = 0.001, 0.9, 0.999, 1e-08, 0.01, 10

NN = (((1,), (0,)), ((), ()))
NT = (((1,), (1,)), ((), ()))
TN = (((0,), (0,)), ((), ()))


def _dot(a, b, dims=NN):
    return lax.dot_general(a, b, dims, preferred_element_type=F32)


def _params(*sem, communicates=False):
    return pltpu.CompilerParams(dimension_semantics=sem or None, vmem_limit_bytes=VMEM_LIMIT,
                                has_side_effects=communicates)


def _rows(tr, c, col=0):
    return pl.BlockSpec((tr, c), lambda i: (i, col))


def _full(shape):
    n = len(shape)
    return pl.BlockSpec(shape, lambda *_: (0,) * n)


def _gelu(x):
    k = math.sqrt(2.0 / math.pi)
    return 0.5 * x * (1.0 + jnp.tanh(k * (x + 0.044715 * x * x * x)))


def _gelu_and_grad(x):
    k = math.sqrt(2.0 / math.pi)
    t = jnp.tanh(k * (x + 0.044715 * x * x * x))
    g = 0.5 * x * (1.0 + t)
    dg = 0.5 * (1.0 + t) + 0.5 * x * (1.0 - t * t) * (k * (1.0 + 3.0 * 0.044715 * x * x))
    return g, dg


def _sigmoid(x):
    return 1.0 / (1.0 + jnp.exp(-x))


def _rms(x):
    r = lax.rsqrt(jnp.mean(x * x, axis=-1, keepdims=True) + EPS)
    return x * r, r


def _rms_bwd(dn, xhat, r):
    return r * (dn - xhat * jnp.mean(dn * xhat, axis=-1, keepdims=True))


def norm_pre(x, g):
    s = x.shape[0]
    tr = 512

    def body(x_ref, g_ref, h_ref):
        xhat, _ = _rms(x_ref[...])
        h_ref[...] = (xhat * g_ref[...]).astype(BF16)

    return pl.pallas_call(
        body, name="norm_pre", grid=(s // tr,),
        in_specs=[_rows(tr, D), _full((1, D))], out_specs=_rows(tr, D),
        out_shape=jax.ShapeDtypeStruct((s, D), BF16), compiler_params=_params("parallel"),
    )(x, g)


def mm_in(h, wg):
    s = h.shape[0]
    tm, tn = 1024, IN_SHARD // 2
    per = IN_SHARD // tn

    def body(a_ref, b_ref, o_ref):
        o_ref[...] = _dot(a_ref[...], b_ref[...])

    return pl.pallas_call(
        body, name="mm_in", grid=(s // tm, IN_COLS // tn),
        in_specs=[pl.BlockSpec((tm, D), lambda i, j: (i, 0)),
                  pl.BlockSpec((None, D, tn), lambda i, j: (j // per, 0, j % per))],
        out_specs=pl.BlockSpec((tm, tn), lambda i, j: (i, j)),
        out_shape=jax.ShapeDtypeStruct((s, IN_COLS), F32), compiler_params=_params("parallel", "parallel"),
    )(h, wg)


def _tril_ws(ws_ref, g):
    r = lax.broadcasted_iota(jnp.int32, (CHUNK, CHUNK), 0)
    c = lax.broadcasted_iota(jnp.int32, (CHUNK, CHUNK), 1)
    return jnp.where(c <= r, ws_ref[g], 0.0).astype(BF16)


def _layer_norm(v):
    mu = jnp.mean(v, axis=-1, keepdims=True)
    d = v - mu
    rstd = lax.rsqrt(jnp.mean(d * d, axis=-1, keepdims=True) + EPS)
    return d * rstd, rstd


def gating_fwd(z, ln_g, ln_b, w_s, bs_t):
    s = z.shape[0]

    def body(u_ref, v_ref, lg_ref, lb_ref, ws_ref, bst_ref, ya_ref):
        ug = _gelu(u_ref[...])
        vhat, _ = _layer_norm(_gelu(v_ref[...]))
        vn = (vhat * lg_ref[...] + lb_ref[...]).astype(BF16)
        for g in range(GROUPS):
            cols = slice(g * CHUNK, (g + 1) * CHUNK)
            mixed = _dot(_tril_ws(ws_ref, g), vn[:, cols]) + bst_ref[:, g:g + 1]
            ya_ref[:, cols] = (ug[:, cols] * mixed).astype(BF16)

    return pl.pallas_call(
        body, name="gating_fwd", grid=(s // CHUNK,),
        in_specs=[_rows(CHUNK, D, 0), _rows(CHUNK, D, 1), _full((1, D)), _full((1, D)),
                  _full((GROUPS, CHUNK, CHUNK)), _full((CHUNK, GROUPS))],
        out_specs=_rows(CHUNK, D), out_shape=jax.ShapeDtypeStruct((s, D), BF16),
        compiler_params=_params("parallel"),
    )(z, z, ln_g, ln_b, w_s, bs_t)


def _attn_tables(s):
    nd = s // ATT_T
    r = np.arange(ATT_T)[None, :, None]
    c = np.arange(ATT_T)[None, None, :]
    delta = np.arange(nd)[:, None, None] * ATT_T + r - c
    count = np.zeros(delta.shape, np.int64)
    for window, dilation in ((128, 1), (512, 4), (2048, 16)):
        count += (delta >= 0) & (delta % dilation == 0) & (delta <= window)
    logc = np.where(count > 0, np.log(np.maximum(count, 1)), MASKED)
    return jnp.asarray(logc, F32)


AUG = 3


def _split3_np(x):
    terms, rest = [], np.asarray(x, np.float64)
    for _ in range(AUG):
        term = np.asarray(rest.astype(jnp.bfloat16), np.float64)
        terms.append(term)
        rest = rest - term
    return terms


def _split3(x):
    terms, rest = [], x
    for _ in range(AUG):
        term = rest.astype(BF16).astype(F32)
        terms.append(term)
        rest = rest - term
    return terms


def _alibi_tables(s):
    nb = s // ATT_T
    slopes = np.exp2(-8.0 * np.arange(1, HEADS + 1, dtype=np.float64) / HEADS)
    ka = np.zeros((HEADS // 2, 2, ATT_T, 128), np.float32)
    kb = np.zeros((HEADS // 2, 2, nb, 128), np.float32)
    for p in range(HEADS // 2):
        for e in range(2):
            base = HEAD_DIM * (1 - e)
            for a, term in enumerate(_split3_np(slopes[2 * p + e] * np.arange(ATT_T))):
                ka[p, e, :, base + a] = term
            for a, term in enumerate(_split3_np(slopes[2 * p + e] * ATT_T * np.arange(nb))):
                kb[p, e, :, base + AUG + a] = term
            ka[p, e, :, base + 2 * AUG:base + 3 * AUG] = 1.0
    return jnp.asarray(ka), jnp.asarray(kb)


def _head_masks():
    lane = lax.broadcasted_iota(jnp.int32, (1, 128), 1)
    first = lane < HEAD_DIM

    def ones(e, n):
        base = HEAD_DIM * (1 - e)
        return ((lane >= base) & (lane < base + n)).astype(F32)

    return first, lane, ones


def _place3(lane, at, terms, other):
    for a, term in enumerate(terms):
        other = jnp.where(lane == at + a, term, other)
    return other


def attn_fwd(z, logc, ka, kb, gathering):
    s = z.shape[0]
    nq = s // ATT_T
    t = ATT_T
    n = len(gathering)
    qcol, kcol, vcol = 2 * D // 128, 3 * D // 128, 4 * D // 128

    def body(*refs):
        q_ref, k_ref, v_ref, lc_ref, ka_ref, kb_ref = refs[:6]
        y_ref, lse_ref = refs[6 + n:8 + n]
        q_s, k_s, v_s, m_s, l_s, acc_s = refs[8 + 2 * n:14 + 2 * n]
        pair, qi = pl.program_id(0), pl.program_id(1)
        first, lane, ones = _head_masks()
        if n:
            send, pass_on, finish = _gather_phases(refs[8 + n:8 + 2 * n], *refs[14 + 2 * n:])
            pl.when((pair == 0) & (qi == 0))(send)
            pl.when((pair == HEADS // 2 - 2) & (qi == 0))(pass_on)

        @pl.when(qi == 0)
        def _():
            sel = jnp.broadcast_to(first.astype(F32), (t, 128))
            for jb in range(nq):
                kj = k_ref[jb * t:(jb + 1) * t, :]
                vj = v_ref[jb * t:(jb + 1) * t, :]
                k_s[0, jb] = jnp.where(first, kj, ka_ref[0] + kb_ref[0, jb:jb + 1, :]).astype(BF16)
                k_s[1, jb] = jnp.where(first, ka_ref[1] + kb_ref[1, jb:jb + 1, :], kj).astype(BF16)
                v_s[jb, 0:t, 0:128] = jnp.where(first, vj, 0.0).astype(BF16)
                v_s[jb, t:2 * t, 0:128] = jnp.where(first, 0.0, vj).astype(BF16)
                v_s[jb, 0:t, 128:256] = sel.astype(BF16)
                v_s[jb, t:2 * t, 128:256] = (1.0 - sel).astype(BF16)

        q = q_ref[...] * (1.0 / math.sqrt(HEAD_DIM))
        q_s[0] = jnp.where(first, q, ones(0, 2 * AUG)).astype(BF16)
        q_s[1] = jnp.where(first, ones(1, 2 * AUG), q).astype(BF16)
        m_s[...] = jnp.full_like(m_s, MASKED)
        l_s[...] = jnp.zeros_like(l_s)
        acc_s[...] = jnp.zeros_like(acc_s)

        def scores(j):
            return _dot(q_s[0], k_s[0, j], NT), _dot(q_s[1], k_s[1, j], NT)

        def step(j, u):
            nxt = scores(j + 1)
            softmax_block(j, u)
            return nxt

        def softmax_block(j, u):
            lc = lc_ref[qi - j]
            u0 = u[0] + lc
            u1 = u[1] + lc
            m0, m1 = m_s[0], m_s[1]
            n0 = jnp.maximum(m0, jnp.max(u0, axis=-1, keepdims=True))
            n1 = jnp.maximum(m1, jnp.max(u1, axis=-1, keepdims=True))
            m_s[0], m_s[1] = n0, n1
            p = jnp.concatenate([jnp.exp(u0 - jnp.concatenate([n0, n0], axis=1)).astype(BF16),
                                 jnp.exp(u1 - jnp.concatenate([n1, n1], axis=1)).astype(BF16)], axis=1)
            pv = _dot(p, v_s[j])
            alpha = jnp.where(first, jnp.exp(m0 - n0), jnp.exp(m1 - n1))
            acc_s[...] = acc_s[...] * alpha + pv[:, 0:128]
            l_s[...] = l_s[...] * alpha + pv[:, 128:256]

        softmax_block(qi, lax.fori_loop(0, qi, step, scores(0)))
        y_ref[...] = (acc_s[...] / l_s[...]).astype(BF16)
        lse_ref[...] = jnp.where(first, m_s[0], m_s[1]) + jnp.log(l_s[...])
        if n:
            pl.when((pair == HEADS // 2 - 1) & (qi == nq - 1))(finish)

    out = pl.pallas_call(
        body, name="attn_fwd", grid=(HEADS // 2, nq),
        in_specs=[pl.BlockSpec((t, 128), lambda p, i: (i, qcol + p)),
                  pl.BlockSpec((s, 128), lambda p, i: (0, kcol + p)),
                  pl.BlockSpec((s, 128), lambda p, i: (0, vcol + p)),
                  _full((nq, t, t)),
                  pl.BlockSpec((None, 2, t, 128), lambda p, i: (p, 0, 0, 0)),
                  pl.BlockSpec((None, 2, nq, 128), lambda p, i: (p, 0, 0, 0))] + [ANY] * n,
        out_specs=[pl.BlockSpec((t, 128), lambda p, i: (i, p)), pl.BlockSpec((t, 128), lambda p, i: (i, p))]
        + [ANY] * n,
        out_shape=[jax.ShapeDtypeStruct((s, D), BF16), jax.ShapeDtypeStruct((s, D), F32)]
        + [jax.ShapeDtypeStruct(a.shape, a.dtype) for a in gathering],
        input_output_aliases={6 + w: 2 + w for w in range(n)},
        scratch_shapes=[pltpu.VMEM((2, t, 128), BF16), pltpu.VMEM((2, nq, t, 128), BF16),
                        pltpu.VMEM((nq, 2 * t, 256), BF16), pltpu.VMEM((2, t, 128), F32),
                        pltpu.VMEM((t, 128), F32), pltpu.VMEM((t, 128), F32)] + (_gather_sems(n) if n else []),
        compiler_params=_params("arbitrary", "arbitrary", communicates=bool(n)),
    )(z, z, z, logc, ka, kb, *gathering)
    return out[0], out[1], out[2:]


def proj_merge(ya, yb, wa, wb, z, bg):
    s = ya.shape[0]
    tm = 512

    def body(ya_ref, yb_ref, wa_ref, wb_ref, ga_ref, gb_ref, bg_ref, mg_ref, pa_ref, pb_ref):
        pa = _dot(ya_ref[...], wa_ref[...])
        pb = _dot(yb_ref[...], wb_ref[...])
        sa = _sigmoid(ga_ref[...] + bg_ref[0:1, :])
        sb = _sigmoid(gb_ref[...] + bg_ref[1:2, :])
        mg_ref[...] = (sa * pa + sb * pb).astype(BF16)
        pa_ref[...] = pa.astype(BF16)
        pb_ref[...] = pb.astype(BF16)

    out = jax.ShapeDtypeStruct((s, D), BF16)
    return pl.pallas_call(
        body, name="proj_merge", grid=(s // tm,),
        in_specs=[_rows(tm, D), _rows(tm, D), _full((D, D)), _full((D, D)),
                  _rows(tm, D, 5), _rows(tm, D, 6), _full((2, D))],
        out_specs=[_rows(tm, D)] * 3, out_shape=[out] * 3, compiler_params=_params("parallel"),
    )(ya, yb, wa, wb, z, z, bg)


def out_norm(merged, w_out, x, g_post, g_fpre):
    s = x.shape[0]
    tm = 512

    def body(mg_ref, w_ref, x_ref, gp_ref, gf_ref, o_ref, x1_ref, h2_ref):
        o = _dot(mg_ref[...], w_ref[...])
        ohat, _ = _rms(o)
        x1 = x_ref[...] + ohat * gp_ref[...]
        x1hat, _ = _rms(x1)
        o_ref[...] = o
        x1_ref[...] = x1
        h2_ref[...] = (x1hat * gf_ref[...]).astype(BF16)

    return pl.pallas_call(
        body, name="out_norm", grid=(s // tm,),
        in_specs=[_rows(tm, D), _full((D, D)), _rows(tm, D), _full((1, D)), _full((1, D))],
        out_specs=[_rows(tm, D)] * 3,
        out_shape=[jax.ShapeDtypeStruct((s, D), F32), jax.ShapeDtypeStruct((s, D), F32),
                   jax.ShapeDtypeStruct((s, D), BF16)],
        compiler_params=_params("parallel"),
    )(merged, w_out, x, g_post, g_fpre)


def mm_ff1(h2, wg):
    s = h2.shape[0]
    tm = 1024

    def body(a_ref, b_ref, o_ref, r_ref):
        a = _dot(a_ref[...], b_ref[...])
        o_ref[...] = a
        r = jnp.maximum(a, 0.0)
        r_ref[...] = (r * r).astype(BF16)

    return pl.pallas_call(
        body, name="mm_ff1", grid=(s // tm, N_CHIPS),
        in_specs=[pl.BlockSpec((tm, D), lambda i, j: (i, 0)), pl.BlockSpec((None, D, D), lambda i, j: (j, 0, 0))],
        out_specs=[pl.BlockSpec((tm, D), lambda i, j: (i, j))] * 2,
        out_shape=[jax.ShapeDtypeStruct((s, D_FF), F32), jax.ShapeDtypeStruct((s, D_FF), BF16)],
        compiler_params=_params("parallel", "parallel"),
    )(h2, wg)


def ff2_loss(rl, w_ff2, x1, target, g_fpost):
    s = x1.shape[0]
    tm = 256

    def body(rl_ref, w_ref, x1_ref, t_ref, g_ref, dy_ref, df_ref, dg_ref, loss_ref):
        @pl.when(pl.program_id(0) == 0)
        def _():
            dg_ref[...] = jnp.zeros_like(dg_ref)
            loss_ref[...] = jnp.zeros_like(loss_ref)

        f = _dot(rl_ref[...], w_ref[...])
        fhat, r = _rms(f)
        err = x1_ref[...] + fhat * g_ref[...] - t_ref[...]
        loss_ref[...] += 0.5 * jnp.sum(jnp.mean(err * err, axis=-1, keepdims=True), axis=0, keepdims=True)
        dy = err * (1.0 / D)
        dy_ref[...] = dy
        dg_ref[...] += jnp.sum(dy * fhat, axis=0, keepdims=True)
        df_ref[...] = _rms_bwd(dy * g_ref[...], fhat, r).astype(BF16)

    return pl.pallas_call(
        body, name="ff2_loss", grid=(s // tm,),
        in_specs=[_rows(tm, D_FF), _full((D_FF, D)), _rows(tm, D), _rows(tm, D), _full((1, D))],
        out_specs=[_rows(tm, D), _rows(tm, D), _full((1, D)), _full((1, 1))],
        out_shape=[jax.ShapeDtypeStruct((s, D), F32), jax.ShapeDtypeStruct((s, D), BF16),
                   jax.ShapeDtypeStruct((1, D), F32), jax.ShapeDtypeStruct((1, 1), F32)],
        compiler_params=_params("arbitrary"),
    )(rl, w_ff2, x1, target, g_fpost)


def mm_tn(name, a, b, ta, tb, out_shape, out_spec):
    s = a.shape[0]

    def body(a_ref, b_ref, o_ref):
        o_ref[...] = _dot(a_ref[...], b_ref[...], TN)

    return pl.pallas_call(
        body, name=name, grid=(a.shape[1] // ta, b.shape[1] // tb),
        in_specs=[pl.BlockSpec((s, ta), lambda i, j: (0, i)), pl.BlockSpec((s, tb), lambda i, j: (0, j))],
        out_specs=out_spec, out_shape=jax.ShapeDtypeStruct(out_shape, F32),
        compiler_params=_params("parallel", "parallel"),
    )(a, b)


def mm_nt(name, a, w):
    s = a.shape[0]
    tm = 512

    def body(a_ref, w_ref, o_ref):
        o_ref[...] = _dot(a_ref[...], w_ref[...], NT)

    return pl.pallas_call(
        body, name=name, grid=(s // tm,), in_specs=[_rows(tm, D), _full((D, D))], out_specs=_rows(tm, D),
        out_shape=jax.ShapeDtypeStruct((s, D), F32), compiler_params=_params("parallel"),
    )(a, w)


def ff2_bwd(df, w_ff2, a):
    s = df.shape[0]
    tm = 1024

    def body(df_ref, w_ref, a_ref, da_ref):
        drl = _dot(df_ref[...], w_ref[...], NT)
        da_ref[...] = (drl * (2.0 * jnp.maximum(a_ref[...], 0.0))).astype(BF16)

    return pl.pallas_call(
        body, name="ff2_bwd", grid=(s // tm, D_FF // D),
        in_specs=[pl.BlockSpec((tm, D), lambda i, j: (i, 0)), pl.BlockSpec((D, D), lambda i, j: (j, 0)),
                  pl.BlockSpec((tm, D), lambda i, j: (i, j))],
        out_specs=pl.BlockSpec((tm, D), lambda i, j: (i, j)),
        out_shape=jax.ShapeDtypeStruct((s, D_FF), BF16), compiler_params=_params("parallel", "parallel"),
    )(df, w_ff2, a)


def ff1_bwd_norms(da, wg, x1, o, dy, g_fpre, g_post, swapping):
    s = x1.shape[0]
    tm = 512
    n = len(swapping)

    def body(*refs):
        da_ref, w_ref, x1_ref, o_ref, dy_ref, gf_ref, gp_ref = refs[:7]
        dx1_ref, do_ref, dgf_ref, dgp_ref = refs[7 + n:11 + n]
        acc_ref = refs[11 + 2 * n]
        i, k = pl.program_id(0), pl.program_id(1)
        if n:
            send, finish = _swap_phases(refs[7:7 + n], refs[11 + n:11 + 2 * n], *refs[12 + 2 * n:])
            pl.when((i == 0) & (k == 0))(send)

        @pl.when((i == 0) & (k == 0))
        def _():
            dgf_ref[...] = jnp.zeros_like(dgf_ref)
            dgp_ref[...] = jnp.zeros_like(dgp_ref)

        part = _dot(da_ref[...], w_ref[...], NT)

        @pl.when(k == 0)
        def _():
            acc_ref[...] = part

        @pl.when(k > 0)
        def _():
            acc_ref[...] += part

        @pl.when(k == N_CHIPS - 1)
        def _():
            dh2 = acc_ref[...]
            x1hat, r2 = _rms(x1_ref[...])
            dgf_ref[...] += jnp.sum(dh2 * x1hat, axis=0, keepdims=True)
            dx1 = dy_ref[...] + _rms_bwd(dh2 * gf_ref[...], x1hat, r2)
            ohat, r1 = _rms(o_ref[...])
            dgp_ref[...] += jnp.sum(dx1 * ohat, axis=0, keepdims=True)
            dx1_ref[...] = dx1
            do_ref[...] = _rms_bwd(dx1 * gp_ref[...], ohat, r1).astype(BF16)

        if n:
            pl.when((i == s // tm - 1) & (k == N_CHIPS - 1))(finish)

    row = pl.BlockSpec((tm, D), lambda i, k: (i, 0))
    vec = pl.BlockSpec((1, D), lambda i, k: (0, 0))
    res = pl.pallas_call(
        body, name="ff1_bwd_norms", grid=(s // tm, N_CHIPS),
        in_specs=[pl.BlockSpec((tm, D), lambda i, k: (i, k)), pl.BlockSpec((None, D, D), lambda i, k: (k, 0, 0)),
                  row, row, row, vec, vec] + [ANY] * n,
        out_specs=[row, row, vec, vec] + [ANY] * n,
        out_shape=[jax.ShapeDtypeStruct((s, D), F32), jax.ShapeDtypeStruct((s, D), BF16),
                   jax.ShapeDtypeStruct((1, D), F32), jax.ShapeDtypeStruct((1, D), F32)] + _swap_shapes(swapping),
        scratch_shapes=[pltpu.VMEM((tm, D), F32)] + (_swap_sems(n) if n else []),
        compiler_params=_params("arbitrary", "arbitrary", communicates=bool(n)),
    )(da, wg, x1, o, dy, g_fpre, g_post, *swapping)
    return res[0], res[1], res[2], res[3], res[4:]


def out_bwd_gates(do, w_out, pa, pb, z, bg):
    s = do.shape[0]
    tm = 512

    def body(do_ref, w_ref, pa_ref, pb_ref, ga_ref, gb_ref, bg_ref, dpa_ref, dpb_ref, dga_ref, dgb_ref, dbg_ref):
        @pl.when(pl.program_id(0) == 0)
        def _():
            dbg_ref[...] = jnp.zeros_like(dbg_ref)

        dm = _dot(do_ref[...], w_ref[...], NT)
        sa = _sigmoid(ga_ref[...] + bg_ref[0:1, :])
        sb = _sigmoid(gb_ref[...] + bg_ref[1:2, :])
        dpa_ref[...] = (dm * sa).astype(BF16)
        dpb_ref[...] = (dm * sb).astype(BF16)
        dga = dm * pa_ref[...].astype(F32) * (sa * (1.0 - sa))
        dgb = dm * pb_ref[...].astype(F32) * (sb * (1.0 - sb))
        dga_ref[...] = dga.astype(BF16)
        dgb_ref[...] = dgb.astype(BF16)
        dbg_ref[0:1, :] += jnp.sum(dga, axis=0, keepdims=True)
        dbg_ref[1:2, :] += jnp.sum(dgb, axis=0, keepdims=True)

    out = jax.ShapeDtypeStruct((s, D), BF16)
    return pl.pallas_call(
        body, name="out_bwd_gates", grid=(s // tm,),
        in_specs=[_rows(tm, D), _full((D, D)), _rows(tm, D), _rows(tm, D), _rows(tm, D, 5), _rows(tm, D, 6),
                  _full((2, D))],
        out_specs=[_rows(tm, D)] * 4 + [_full((2, D))],
        out_shape=[out] * 4 + [jax.ShapeDtypeStruct((2, D), F32)], compiler_params=_params("arbitrary"),
    )(do, w_out, pa, pb, z, z, bg)


def gating_bwd(z, dya, ln_g, ln_b, w_s, bs_t, swapping):
    s = z.shape[0]
    ones = functools.partial(jnp.ones, (8, CHUNK), BF16)
    n = len(swapping)

    def body(*refs):
        u_ref, v_ref, dya_ref, lg_ref, lb_ref, ws_ref, bst_ref = refs[:7]
        du_ref, dv_ref, dws_ref, dbs_ref, dlg_ref, dlb_ref = refs[7 + n:13 + n]
        dvn_ref = refs[13 + 2 * n]
        ci = pl.program_id(0)
        if n:
            send, finish = _swap_phases(refs[7:7 + n], refs[13 + n:13 + 2 * n], *refs[14 + 2 * n:])
            pl.when(ci == 0)(send)

        @pl.when(ci == 0)
        def _():
            dws_ref[...] = jnp.zeros_like(dws_ref)
            dbs_ref[...] = jnp.zeros_like(dbs_ref)
            dlg_ref[...] = jnp.zeros_like(dlg_ref)
            dlb_ref[...] = jnp.zeros_like(dlb_ref)

        ug, dug_du = _gelu_and_grad(u_ref[...])
        vg, dvg_dv = _gelu_and_grad(v_ref[...])
        vhat, rstd = _layer_norm(vg)
        vn = (vhat * lg_ref[...] + lb_ref[...]).astype(BF16)
        dya = dya_ref[...]
        for g in range(GROUPS):
            cols = slice(g * CHUNK, (g + 1) * CHUNK)
            ws = _tril_ws(ws_ref, g)
            mixed = _dot(ws, vn[:, cols]) + bst_ref[:, g:g + 1]
            du_ref[:, cols] = (dya[:, cols] * mixed * dug_du[:, cols]).astype(BF16)
            dmix = (dya[:, cols] * ug[:, cols]).astype(BF16)
            dbs_ref[g] += _dot(ones(), dmix, NT)
            dws_ref[g] += _dot(dmix, vn[:, cols], NT)
            dvn_ref[:, cols] = _dot(ws, dmix, TN)
        dvn = dvn_ref[...]
        dlg_ref[...] += jnp.sum(dvn * vhat, axis=0, keepdims=True)
        dlb_ref[...] += jnp.sum(dvn, axis=0, keepdims=True)
        dvh = dvn * lg_ref[...]
        dvg = rstd * (dvh - jnp.mean(dvh, axis=-1, keepdims=True)
                      - vhat * jnp.mean(dvh * vhat, axis=-1, keepdims=True))
        dv_ref[...] = (dvg * dvg_dv).astype(BF16)

        @pl.when(ci == pl.num_programs(0) - 1)
        def _():
            r = lax.broadcasted_iota(jnp.int32, (CHUNK, CHUNK), 0)
            c = lax.broadcasted_iota(jnp.int32, (CHUNK, CHUNK), 1)
            for g in range(GROUPS):
                dws_ref[g] = jnp.where(c <= r, dws_ref[g], 0.0)

        if n:
            pl.when(ci == pl.num_programs(0) - 1)(finish)

    out = jax.ShapeDtypeStruct((s, D), BF16)
    res = pl.pallas_call(
        body, name="gating_bwd", grid=(s // CHUNK,),
        in_specs=[_rows(CHUNK, D, 0), _rows(CHUNK, D, 1), _rows(CHUNK, D), _full((1, D)), _full((1, D)),
                  _full((GROUPS, CHUNK, CHUNK)), _full((CHUNK, GROUPS))] + [ANY] * n,
        out_specs=[_rows(CHUNK, D), _rows(CHUNK, D), _full((GROUPS, CHUNK, CHUNK)), _full((GROUPS, 8, CHUNK)),
                   _full((1, D)), _full((1, D))] + [ANY] * n,
        out_shape=[out, out, jax.ShapeDtypeStruct((GROUPS, CHUNK, CHUNK), F32),
                   jax.ShapeDtypeStruct((GROUPS, 8, CHUNK), F32),
                   jax.ShapeDtypeStruct((1, D), F32), jax.ShapeDtypeStruct((1, D), F32)] + _swap_shapes(swapping),
        scratch_shapes=[pltpu.VMEM((CHUNK, D), F32)] + (_swap_sems(n) if n else []),
        compiler_params=_params("arbitrary", communicates=bool(n)),
    )(z, z, dya, ln_g, ln_b, w_s, bs_t, *swapping)
    return (*res[:6], res[6:])


def attn_bwd(z, yb, dyb, lse, logc, ka, kb, scattering):
    s = z.shape[0]
    nq = s // ATT_T
    t = ATT_T
    qcol, kcol, vcol = 2 * D // 128, 3 * D // 128, 4 * D // 128
    scale = 1.0 / math.sqrt(HEAD_DIM)

    n = len(scattering)

    def body(*refs):
        q_ref, k_ref, v_ref, y_ref, dy_ref, lse_ref, lc_ref, ka_ref, kb_ref = refs[:9]
        dq_ref, dk_ref, dv_ref = refs[9 + n:12 + n]
        qa_s, qt_s, da_s, dt_s, dq_s, dkt_s, dvt_s = refs[12 + 2 * n:19 + 2 * n]
        pair, j = pl.program_id(0), pl.program_id(1)
        first, lane, ones = _head_masks()
        if n:
            send, finish = _scatter_phases(refs[9:9 + n], refs[12 + n:12 + 2 * n], *refs[19 + 2 * n:])
            pl.when((pair == 0) & (j == 0))(send)

        @pl.when(j == 0)
        def _():
            dq_s[...] = jnp.zeros_like(dq_s)
            for ib in range(nq):
                rows = slice(ib * t, (ib + 1) * t)
                q = q_ref[rows, :] * scale
                lse = lse_ref[rows, :]
                qa_s[0, ib] = jnp.where(first, q, _place3(lane, HEAD_DIM + 2 * AUG, _split3(-lse[:, 0:1]),
                                                          ones(0, 2 * AUG))).astype(BF16)
                qa_s[1, ib] = jnp.where(first, _place3(lane, 2 * AUG, _split3(-lse[:, HEAD_DIM:HEAD_DIM + 1]),
                                                       ones(1, 2 * AUG)), q).astype(BF16)
                qt_s[ib, :, 0:t] = jnp.where(first, q, 0.0).T.astype(BF16)
                qt_s[ib, :, t:2 * t] = jnp.where(first, 0.0, q).T.astype(BF16)
                do = dy_ref[rows, :]
                prod = do * y_ref[rows, :].astype(F32)
                dd0 = jnp.sum(jnp.where(first, prod, 0.0), axis=-1, keepdims=True)
                dd1 = jnp.sum(jnp.where(first, 0.0, prod), axis=-1, keepdims=True)
                da_s[0, ib] = jnp.where(first, do, _place3(lane, HEAD_DIM, _split3(-dd0), 0.0)).astype(BF16)
                da_s[1, ib] = jnp.where(first, _place3(lane, 0, _split3(-dd1), 0.0), do).astype(BF16)
                dt_s[ib, :, 0:t] = jnp.where(first, do, 0.0).T.astype(BF16)
                dt_s[ib, :, t:2 * t] = jnp.where(first, 0.0, do).T.astype(BF16)

        kj = k_ref[...]
        vj = v_ref[...]
        k0a = jnp.where(first, kj, ka_ref[0] + kb_ref[0, pl.ds(j, 1), :]).astype(BF16)
        k1a = jnp.where(first, ka_ref[1] + kb_ref[1, pl.ds(j, 1), :], kj).astype(BF16)
        kst = jnp.concatenate([jnp.where(first, kj, 0.0), jnp.where(first, 0.0, kj)], axis=0).astype(BF16)
        v0a = jnp.where(first, vj, ones(0, AUG)).astype(BF16)
        v1a = jnp.where(first, ones(1, AUG), vj).astype(BF16)
        dkt_s[...] = jnp.zeros_like(dkt_s)
        dvt_s[...] = jnp.zeros_like(dvt_s)

        def step(i, _):
            lc = lc_ref[i - j]
            p0 = jnp.exp(_dot(qa_s[0, i], k0a, NT) + lc)
            p1 = jnp.exp(_dot(qa_s[1, i], k1a, NT) + lc)
            e0 = (p0 * _dot(da_s[0, i], v0a, NT)).astype(BF16)
            e1 = (p1 * _dot(da_s[1, i], v1a, NT)).astype(BF16)
            rows = pl.ds(pl.multiple_of(i * t, t), t)
            dq_s[rows, :] += _dot(jnp.concatenate([e0, e1], axis=1), kst)
            dvt_s[...] += _dot(dt_s[i], jnp.concatenate([p0.astype(BF16), p1.astype(BF16)], axis=0))
            dkt_s[...] += _dot(qt_s[i], jnp.concatenate([e0, e1], axis=0))
            return 0

        lax.fori_loop(j, nq, step, 0)
        dk_ref[...] = dkt_s[...].T.astype(BF16)
        dv_ref[...] = dvt_s[...].T.astype(BF16)

        @pl.when(j == nq - 1)
        def _():
            dq_ref[...] = (dq_s[...] * scale).astype(BF16)

        if n:
            pl.when((pair == HEADS // 2 - 1) & (j == nq - 1))(finish)

    colblock = lambda c: pl.BlockSpec((s, 128), lambda p, j: (0, c + p))
    blk = lambda c: pl.BlockSpec((t, 128), lambda p, j: (j, c + p))
    out = jax.ShapeDtypeStruct((s, D), BF16)
    res = pl.pallas_call(
        body, name="attn_bwd", grid=(HEADS // 2, nq),
        in_specs=[colblock(qcol), blk(kcol), blk(vcol), colblock(0), colblock(0), colblock(0),
                  _full((nq, t, t)), pl.BlockSpec((None, 2, t, 128), lambda p, j: (p, 0, 0, 0)),
                  pl.BlockSpec((None, 2, nq, 128), lambda p, j: (p, 0, 0, 0))] + [ANY] * n,
        out_specs=[colblock(0), blk(0), blk(0)] + [ANY] * n, out_shape=[out] * 3 + _scatter_shapes(scattering),
        scratch_shapes=[pltpu.VMEM((2, nq, t, 128), BF16), pltpu.VMEM((nq, 128, 2 * t), BF16),
                        pltpu.VMEM((2, nq, t, 128), BF16), pltpu.VMEM((nq, 128, 2 * t), BF16),
                        pltpu.VMEM((s, 128), F32), pltpu.VMEM((128, t), F32), pltpu.VMEM((128, t), F32)]
        + (_scatter_sems(n) if n else []),
        compiler_params=_params("arbitrary", "arbitrary", communicates=bool(n)),
    )(z, z, z, yb, dyb, lse, logc, ka, kb, *scattering)
    return res[0], res[1], res[2], res[3:]


def in_bwd_norm(dz, wg, x, dx1, g_pre, scattering):
    s = x.shape[0]
    tm = 512
    n = len(scattering)

    def body(*refs):
        dz_ref, w_ref, x_ref, dx1_ref, g_ref = refs[:5]
        dx_ref, dg_ref = refs[5 + n:7 + n]
        acc_ref = refs[7 + 2 * n]
        i, k = pl.program_id(0), pl.program_id(1)
        if n:
            send, finish = _scatter_phases(refs[5:5 + n], refs[7 + n:7 + 2 * n], *refs[8 + 2 * n:])
            pl.when((i == 0) & (k == 0))(send)

        @pl.when((i == 0) & (k == 0))
        def _():
            dg_ref[...] = jnp.zeros_like(dg_ref)

        part = _dot(dz_ref[...], w_ref[...], NT)

        @pl.when(k == 0)
        def _():
            acc_ref[...] = part

        @pl.when(k > 0)
        def _():
            acc_ref[...] += part

        @pl.when(k == N_CHIPS - 1)
        def _():
            dh = acc_ref[...]
            xhat, r = _rms(x_ref[...])
            dg_ref[...] += jnp.sum(dh * xhat, axis=0, keepdims=True)
            dx_ref[...] = dx1_ref[...] + _rms_bwd(dh * g_ref[...], xhat, r)

        if n:
            pl.when((i == s // tm - 1) & (k == N_CHIPS - 1))(finish)

    row = pl.BlockSpec((tm, D), lambda i, k: (i, 0))
    vec = pl.BlockSpec((1, D), lambda i, k: (0, 0))
    res = pl.pallas_call(
        body, name="in_bwd_norm", grid=(s // tm, N_CHIPS),
        in_specs=[pl.BlockSpec((tm, IN_SHARD), lambda i, k: (i, k)),
                  pl.BlockSpec((None, D, IN_SHARD), lambda i, k: (k, 0, 0)), row, row, vec] + [ANY] * n,
        out_specs=[row, vec] + [ANY] * n,
        out_shape=[jax.ShapeDtypeStruct((s, D), F32), jax.ShapeDtypeStruct((1, D), F32)]
        + _scatter_shapes(scattering),
        scratch_shapes=[pltpu.VMEM((tm, D), F32)] + (_scatter_sems(n) if n else []),
        compiler_params=_params("arbitrary", "arbitrary", communicates=bool(n)),
    )(dz, wg, x, dx1, g_pre, *scattering)
    return res[0], res[1], res[2:]


def _adamw_math(w, g, m, v):
    m = ADAM_B1 * m + (1.0 - ADAM_B1) * g
    v = ADAM_B2 * v + (1.0 - ADAM_B2) * (g * g)
    m_hat = m / (1.0 - ADAM_B1 ** ADAM_STEP)
    v_hat = v / (1.0 - ADAM_B2 ** ADAM_STEP)
    delta = -ADAM_LR * (m_hat / (jnp.sqrt(v_hat) + ADAM_EPS) + ADAM_WD * w)
    return delta, m, v


def adamw(name, w, g, m, v, tr):
    r, c = w.shape

    def body(w_ref, g_ref, m_ref, v_ref, d_ref, nm_ref, nv_ref):
        d_ref[...], nm_ref[...], nv_ref[...] = _adamw_math(w_ref[...], g_ref[...], m_ref[...], v_ref[...])

    out = jax.ShapeDtypeStruct((r, c), F32)
    return pl.pallas_call(
        body, name=name, grid=(r // tr,), in_specs=[_rows(tr, c)] * 4, out_specs=[_rows(tr, c)] * 3,
        out_shape=[out] * 3, compiler_params=_params("parallel"),
    )(w, g, m, v)


def add_halves(name, g, recv, c_idx, tr):
    n, h, c = recv.shape

    def body(c_ref, g_ref, r_ref, o_ref):
        o_ref[...] = (g_ref[...] + r_ref[...]).astype(BF16)

    nb = h // tr
    return pl.pallas_call(
        body, name=name,
        grid_spec=pltpu.PrefetchScalarGridSpec(
            num_scalar_prefetch=1, grid=(n, nb),
            in_specs=[pl.BlockSpec((None, tr, c), lambda k, i, c_ref: (k, c_ref[0] * nb + i, 0)),
                      pl.BlockSpec((None, tr, c), lambda k, i, c_ref: (k, i, 0))],
            out_specs=pl.BlockSpec((None, tr, c), lambda k, i, c_ref: (k, i, 0))),
        out_shape=jax.ShapeDtypeStruct((n, h, c), BF16), compiler_params=_params("parallel", "parallel"),
    )(c_idx, g, recv)


def sum_chips(name, parts, recv, where, tr):
    n, h, c = recv.shape
    nb = h // tr

    def body(w_ref, p_ref, r_ref, o_ref):
        acc = p_ref[...].astype(F32)
        for k in range(n):
            acc = acc + r_ref[k].astype(F32)
        o_ref[...] = acc

    return pl.pallas_call(
        body, name=name,
        grid_spec=pltpu.PrefetchScalarGridSpec(
            num_scalar_prefetch=1, grid=(nb,),
            in_specs=[pl.BlockSpec((None, tr, c), lambda i, w_ref: (w_ref[0], i, 0)),
                      pl.BlockSpec((n, tr, c), lambda i, w_ref: (0, i, 0))],
            out_specs=pl.BlockSpec((tr, c), lambda i, w_ref: (w_ref[1] * nb + i, 0))),
        out_shape=jax.ShapeDtypeStruct((2 * h, c), F32), compiler_params=_params("parallel"),
    )(where, parts, recv)


def place_shard(name, shard, where, dtype, tr):
    r, c = shard.shape

    def body(w_ref, s_ref, o_ref):
        o_ref[...] = s_ref[...].astype(dtype)

    return pl.pallas_call(
        body, name=name,
        grid_spec=pltpu.PrefetchScalarGridSpec(
            num_scalar_prefetch=1, grid=(r // tr,),
            in_specs=[pl.BlockSpec((tr, c), lambda i, w_ref: (i, 0))],
            out_specs=pl.BlockSpec((None, tr, c), lambda i, w_ref: (w_ref[0], i, 0))),
        out_shape=jax.ShapeDtypeStruct((N_CHIPS, r, c), dtype), compiler_params=_params("parallel"),
    )(where, shard)


ANY = pl.BlockSpec(memory_space=pl.ANY)


def _place():
    x, y, c = lax.axis_index("x"), lax.axis_index("y"), lax.axis_index("c")
    chips = [(1 - x, y), (x, 1 - y), (1 - x, 1 - y)]
    return x, y, c, chips


def gather_shards(arrays):
    n = len(arrays)

    def body(*refs):
        send, pass_on, finish = _gather_phases(refs[n:2 * n], *refs[2 * n:])
        send()
        pass_on()
        finish()

    return pl.pallas_call(
        body, name="gather_shards", in_specs=[ANY] * n, out_specs=[ANY] * n,
        out_shape=[jax.ShapeDtypeStruct(a.shape, a.dtype) for a in arrays],
        input_output_aliases={w: w for w in range(n)}, scratch_shapes=_gather_sems(n),
        compiler_params=pltpu.CompilerParams(has_side_effects=True),
    )(*arrays)


def _gather_sems(n):
    return [pltpu.SemaphoreType.DMA((6 * n,)), pltpu.SemaphoreType.DMA((6 * n,))]


def _gather_phases(out, send_sems, recv_sems):
    n = len(out)
    x, y, c, chips = _place()
    me = 2 * x + y
    sibling = (x, y, 1 - c)

    def half(w, chip, core):
        h = out[w].shape[1] // 2
        return out[w].at[chip, pl.ds(core * h, h)]

    def copy(k, block, to):
        return pltpu.make_async_remote_copy(src_ref=block, dst_ref=block, send_sem=send_sems.at[k],
                                            recv_sem=recv_sems.at[k], device_id=to, device_id_type=MESH)

    def over_ici(w, j, chip):
        return copy(3 * w + j, half(w, chip, c), (chips[j][0], chips[j][1], c))

    def over_d2d(w, j, core):
        return copy(3 * n + 3 * w + j, half(w, 2 * chips[j][0] + chips[j][1], core), sibling)

    pairs = [(w, j) for w in range(n) for j in range(3)]

    def send():
        for w, j in pairs:
            over_ici(w, j, me).start()

    def pass_on():
        for w, j in pairs:
            over_ici(w, j, 2 * chips[j][0] + chips[j][1]).wait_recv()
            over_d2d(w, j, c).start()

    def finish():
        for w, j in pairs:
            over_d2d(w, j, 1 - c).wait_recv()
        for w, j in pairs:
            over_ici(w, j, me).wait_send()
            over_d2d(w, j, c).wait_send()

    return send, pass_on, finish


def swap_halves(name, grads):
    n = len(grads)

    def body(*refs):
        send, finish = _swap_phases(refs[:n], refs[n:2 * n], *refs[2 * n:])
        send()
        finish()

    return pl.pallas_call(
        body, name=name, in_specs=[ANY] * n, out_specs=[ANY] * n, out_shape=_swap_shapes(grads),
        scratch_shapes=_swap_sems(n), compiler_params=pltpu.CompilerParams(has_side_effects=True),
    )(*grads)


def _swap_shapes(grads):
    return [jax.ShapeDtypeStruct((a.shape[0], a.shape[1] // 2, a.shape[2]), a.dtype) for a in grads]


def _swap_sems(n):
    return [pltpu.SemaphoreType.DMA((n,)), pltpu.SemaphoreType.DMA((n,))]


def _swap_phases(g, out, send_sems, recv_sems):
    x, y, c, _ = _place()

    def copies():
        return [pltpu.make_async_remote_copy(
            src_ref=g[w].at[:, pl.ds((1 - c) * (g[w].shape[1] // 2), g[w].shape[1] // 2)], dst_ref=out[w],
            send_sem=send_sems.at[w], recv_sem=recv_sems.at[w], device_id=(x, y, 1 - c), device_id_type=MESH)
            for w in range(len(g))]

    def send():
        for cp in copies():
            cp.start()

    def finish():
        for cp in copies():
            cp.wait()

    return send, finish


def scatter_chips(parts):
    n = len(parts)

    def body(*refs):
        send, finish = _scatter_phases(refs[:n], refs[n:2 * n], *refs[2 * n:])
        send()
        finish()

    return pl.pallas_call(
        body, name="scatter_chips", in_specs=[ANY] * n, out_specs=[ANY] * n,
        out_shape=_scatter_shapes(parts), scratch_shapes=_scatter_sems(n),
        compiler_params=pltpu.CompilerParams(has_side_effects=True),
    )(*parts)


def _scatter_shapes(parts):
    return [jax.ShapeDtypeStruct((3,) + a.shape[1:], a.dtype) for a in parts]


def _scatter_sems(n):
    return [pltpu.SemaphoreType.DMA((3 * n,)), pltpu.SemaphoreType.DMA((3 * n,))]


def _scatter_phases(p, out, send_sems, recv_sems):
    x, y, c, chips = _place()

    def copies():
        return [pltpu.make_async_remote_copy(
            src_ref=p[w].at[2 * px + py], dst_ref=out[w].at[j], send_sem=send_sems.at[3 * w + j],
            recv_sem=recv_sems.at[3 * w + j], device_id=(px, py, c), device_id_type=MESH)
            for w in range(len(p)) for j, (px, py) in enumerate(chips)]

    def send():
        for cp in copies():
            cp.start()

    def finish():
        for cp in copies():
            cp.wait()

    return send, finish


def join_halves(arrays):
    n = len(arrays)

    def body(*refs):
        out = refs[n:2 * n]
        send_sems, recv_sems = refs[2 * n:]
        x, y, c, _ = _place()

        def copy(w, core):
            h = out[w].shape[0] // 2
            rows = out[w].at[pl.ds(core * h, h)]
            return pltpu.make_async_remote_copy(
                src_ref=rows, dst_ref=rows, send_sem=send_sems.at[w], recv_sem=recv_sems.at[w],
                device_id=(x, y, 1 - c), device_id_type=MESH)

        for w in range(n):
            copy(w, c).start()
        for w in range(n):
            copy(w, 1 - c).wait_recv()
        for w in range(n):
            copy(w, c).wait_send()

    return pl.pallas_call(
        body, name="join_halves", in_specs=[ANY] * n, out_specs=[ANY] * n,
        out_shape=[jax.ShapeDtypeStruct(a.shape, a.dtype) for a in arrays],
        input_output_aliases={w: w for w in range(n)},
        scratch_shapes=[pltpu.SemaphoreType.DMA((n,)), pltpu.SemaphoreType.DMA((n,))],
        compiler_params=pltpu.CompilerParams(has_side_effects=True),
    )(*arrays)


def allreduce_small(packed):
    r, c = packed.shape
    n_dev = 8

    def body(x_ref, all_ref, sum_ref, send_sems, recv_sems, local_sem):
        x, y, cc, chips = _place()
        me, sibling = (x, y, cc), (x, y, 1 - cc)

        def rows(px, py, pc):
            return all_ref.at[4 * px + 2 * py + pc]

        def copy(k, block, to, src=None):
            return pltpu.make_async_remote_copy(
                src_ref=rows(*block) if src is None else src, dst_ref=rows(*block), send_sem=send_sems.at[k],
                recv_sem=recv_sems.at[k], device_id=to, device_id_type=MESH)

        mine = pltpu.make_async_copy(x_ref, rows(*me), local_sem)
        mine.start()
        first = [copy(0, me, sibling, src=x_ref)]
        first += [copy(1 + j, me, (*chip, cc), src=x_ref) for j, chip in enumerate(chips)]
        for cp in first:
            cp.start()
        passed = [copy(4 + j, (*chip, cc), sibling) for j, chip in enumerate(chips)]
        for j, chip in enumerate(chips):
            copy(1 + j, (*chip, cc), me).wait_recv()
            passed[j].start()
        copy(0, sibling, me).wait_recv()
        for j, chip in enumerate(chips):
            copy(4 + j, (*chip, 1 - cc), me).wait_recv()
        for cp in first + passed:
            cp.wait_send()
        mine.wait()
        acc = all_ref[0]
        for k in range(1, n_dev):
            acc = acc + all_ref[k]
        sum_ref[...] = acc

    vm = pl.BlockSpec(memory_space=pltpu.VMEM)
    return pl.pallas_call(
        body, name="allreduce_small", in_specs=[vm], out_specs=[vm, vm],
        out_shape=[jax.ShapeDtypeStruct((n_dev, r, c), F32), jax.ShapeDtypeStruct((r, c), F32)],
        scratch_shapes=[pltpu.SemaphoreType.DMA((7,)), pltpu.SemaphoreType.DMA((7,)), pltpu.SemaphoreType.DMA],
        compiler_params=pltpu.CompilerParams(has_side_effects=True, vmem_limit_bytes=VMEM_LIMIT),
    )(packed)[1]


def local_step(x, target, vecs, w_s, bs_t, bg, wg_in, late, core=None):
    on_mesh = core is not None

    def add(names, grads, recv):
        return [add_halves("add_" + n, g, r, core, min(r.shape[1], 256)) for n, g, r in zip(names, grads, recv)]

    g_pre, ln_g, ln_b, g_post, g_fpre, g_fpost = vecs
    s = x.shape[0]
    logc = _attn_tables(s)
    ka, kb = _alibi_tables(s)

    h = norm_pre(x, g_pre)
    z = mm_in(h, wg_in)
    ya = gating_fwd(z, ln_g, ln_b, w_s, bs_t)
    yb, lse, gathered = attn_fwd(z, logc, ka, kb, late if on_mesh else [])
    wg_a, wg_b, wg_out, wg_ff1, wg_ff2 = gathered if on_mesh else late
    w_a, w_b, w_out, w_ff2 = wg_a.reshape(D, D), wg_b.reshape(D, D), wg_out.reshape(D, D), wg_ff2.reshape(D_FF, D)
    merged, pa, pb = proj_merge(ya, yb, w_a, w_b, z, bg)
    o, x1, h2 = out_norm(merged, w_out, x, g_post, g_fpre)
    a, rl = mm_ff1(h2, wg_ff1)
    dy, df, d_gfpost, loss = ff2_loss(rl, w_ff2, x1, target, g_fpost)

    half_cols = pl.BlockSpec((D, D // 2), lambda i, j: (0, j))
    d_wff2 = mm_tn("dw_ff2", rl, df, D // 2, D, (D_FF, D), pl.BlockSpec((D // 2, D), lambda i, j: (i, 0)))
    da = ff2_bwd(df, w_ff2, a)
    d_wff1 = mm_tn("dw_ff1", h2, da, D, D // 2, (N_CHIPS, D, D),
                   pl.BlockSpec((None, D, D // 2), lambda i, j: (j // 2, 0, j % 2)))
    d_ff = [d_wff1, d_wff2.reshape(N_CHIPS, D, D)]
    dx1, do, d_gfpre, d_gpost, recv_ff = ff1_bwd_norms(da, wg_ff1, x1, o, dy, g_fpre, g_post, d_ff if on_mesh else [])
    d_wout = mm_tn("dw_out", merged, do, D, D // 2, (D, D), half_cols)
    dpa, dpb, dga, dgb, d_bg = out_bwd_gates(do, w_out, pa, pb, z, bg)
    d_wa = mm_tn("dw_a", ya, dpa, D, D // 2, (D, D), half_cols)
    d_wb = mm_tn("dw_b", yb, dpb, D, D // 2, (D, D), half_cols)
    dya = mm_nt("dy_a", dpa, w_a)
    dyb = mm_nt("dy_b", dpb, w_b)
    d_proj = [d_wa.reshape(N_CHIPS, D // N_CHIPS, D), d_wb.reshape(N_CHIPS, D // N_CHIPS, D),
              d_wout.reshape(N_CHIPS, D // N_CHIPS, D)]
    du, dv, d_ws, d_bs, d_lng, d_lnb, recv_proj = gating_bwd(z, dya, ln_g, ln_b, w_s, bs_t, d_proj if on_mesh else [])
    early = d_proj + d_ff
    parts_early = add(BIG[1:], early, list(recv_proj) + list(recv_ff)) if on_mesh else []
    dq, dk, dvb, got_early = attn_bwd(z, yb, dyb, lse, logc, ka, kb, parts_early)
    dz = jnp.concatenate([du, dv, dq, dk, dvb, dga, dgb], axis=1)
    half = IN_SHARD // 2
    d_win = mm_tn("dw_in", h, dz, D, half, (N_CHIPS, D, IN_SHARD),
                  pl.BlockSpec((None, D, half), lambda i, j: (j // 2, 0, j % 2)))
    parts_late = add(BIG[:1], [d_win], swap_halves("swap_w_in", [d_win])) if on_mesh else []
    dx, d_gpre, got_late = in_bwd_norm(dz, wg_in, x, dx1, g_pre, parts_late)

    small = dict(norm_mix_pre=d_gpre, b_gate=d_bg, ln_v_g=d_lng, ln_v_b=d_lnb, w_s=d_ws, b_s=d_bs[:, 0, :],
                 norm_mix_post=d_gpost, norm_ffn_pre=d_gfpre, norm_ffn_post=d_gfpost)
    return loss[0, 0], dx, [d_win] + early, small, parts_late + parts_early, list(got_late) + list(got_early)


BIG = ("w_in", "w_a_proj", "w_b_proj", "w_out", "w_ff1", "w_ff2")
SMALL = ("norm_mix_pre", "ln_v_g", "ln_v_b", "b_s", "norm_mix_post", "norm_ffn_pre", "norm_ffn_post", "w_s", "b_gate")
ORDER = ("norm_mix_pre", "w_in", "b_gate", "ln_v_g", "ln_v_b", "w_s", "b_s", "w_a_proj", "w_b_proj", "w_out",
         "norm_mix_post", "norm_ffn_pre", "w_ff1", "w_ff2", "norm_ffn_post")
PACK_ROWS = 144
LOSS_ROW = 137


def _pack_small(t):
    rows = [t[n].reshape(1, D) for n in SMALL[:7]] + [t["w_s"].reshape(128, D), t["b_gate"].reshape(2, D)]
    used = jnp.concatenate(rows, axis=0)
    return jnp.pad(used, ((0, PACK_ROWS - used.shape[0]), (0, 0)))


def _unpack_small(p, chip):
    out = {n: p[i:i + 1].reshape(1, D) for i, n in enumerate(SMALL[:7])}
    out["b_s"] = out["b_s"].reshape(1, GROUPS, CHUNK)
    out["w_s"] = p[7:135].reshape(1, GROUPS, CHUNK, CHUNK)
    out["b_gate"] = lax.dynamic_slice_in_dim(p[135:137], chip * (D // N_CHIPS), D // N_CHIPS, axis=1).reshape(1, 2, D // N_CHIPS)
    return out


def kernel(x, norm_mix_pre, w_in, b_gate, ln_v_g, ln_v_b, w_s, b_s, w_a_proj, w_b_proj, w_out, norm_mix_post, norm_ffn_pre, w_ff1, w_ff2, norm_ffn_post, loss_target, m_norm_mix_pre, m_w_in, m_b_gate, m_ln_v_g, m_ln_v_b, m_w_s, m_b_s, m_w_a_proj, m_w_b_proj, m_w_out, m_norm_mix_post, m_norm_ffn_pre, m_w_ff1, m_w_ff2, m_norm_ffn_post, v_norm_mix_pre, v_w_in, v_b_gate, v_ln_v_g, v_ln_v_b, v_w_s, v_b_s, v_w_a_proj, v_w_b_proj, v_w_out, v_norm_mix_post, v_norm_ffn_pre, v_w_ff1, v_w_ff2, v_norm_ffn_post):
    w = dict(norm_mix_pre=norm_mix_pre, w_in=w_in, b_gate=b_gate, ln_v_g=ln_v_g, ln_v_b=ln_v_b, w_s=w_s, b_s=b_s,
             w_a_proj=w_a_proj, w_b_proj=w_b_proj, w_out=w_out, norm_mix_post=norm_mix_post,
             norm_ffn_pre=norm_ffn_pre, w_ff1=w_ff1, w_ff2=w_ff2, norm_ffn_post=norm_ffn_post)
    m = dict(norm_mix_pre=m_norm_mix_pre, w_in=m_w_in, b_gate=m_b_gate, ln_v_g=m_ln_v_g, ln_v_b=m_ln_v_b, w_s=m_w_s,
             b_s=m_b_s, w_a_proj=m_w_a_proj, w_b_proj=m_w_b_proj, w_out=m_w_out, norm_mix_post=m_norm_mix_post,
             norm_ffn_pre=m_norm_ffn_pre, w_ff1=m_w_ff1, w_ff2=m_w_ff2, norm_ffn_post=m_norm_ffn_post)
    v = dict(norm_mix_pre=v_norm_mix_pre, w_in=v_w_in, b_gate=v_b_gate, ln_v_g=v_ln_v_g, ln_v_b=v_ln_v_b, w_s=v_w_s,
             b_s=v_b_s, w_a_proj=v_w_a_proj, w_b_proj=v_w_b_proj, w_out=v_w_out, norm_mix_post=v_norm_mix_post,
             norm_ffn_pre=v_norm_ffn_pre, w_ff1=v_w_ff1, w_ff2=v_w_ff2, norm_ffn_post=v_norm_ffn_post)
    chip = 2 * lax.axis_index("x") + lax.axis_index("y")
    core = lax.axis_index("c")

    where = jnp.stack([chip, core]).astype(jnp.int32)
    placed = [place_shard("place_" + n, w[n][0], where, BF16, min(w[n].shape[1], 256)) for n in BIG]
    placed.append(place_shard("place_b_gate", jnp.pad(b_gate[0], ((0, 14), (0, 0))), where, F32, 16))
    wg_in, bg_all = gather_shards([placed[0], placed[6]])
    bg = jnp.transpose(bg_all[:, :2, :], (1, 0, 2)).reshape(2, D)
    vecs = (norm_mix_pre, ln_v_g, ln_v_b, norm_mix_post, norm_ffn_pre, norm_ffn_post)
    loss, dx, _, small, parts, got = local_step(
        x[0], loss_target[0], vecs, w_s[0], b_s[0].T, bg, wg_in, placed[1:6],
        core=jnp.reshape(core, (1,)).astype(jnp.int32))

    halves = [sum_chips("sum_" + n, p, r, where, min(p.shape[1], 256)) for n, p, r in zip(BIG, parts, got)]
    grads = dict(zip(BIG, join_halves(halves)))

    summed = allreduce_small(_pack_small(small).at[LOSS_ROW, 0].set(loss))
    loss = summed[LOSS_ROW, 0]
    grads_small = _unpack_small(summed, chip)

    new = {}
    for n in BIG:
        shape = w[n].shape
        r, c = shape[1], shape[2]
        d, nm, nv = adamw("adamw_" + n, w[n][0], grads[n], m[n][0], v[n][0], min(r, 256))
        new[n] = (grads[n].reshape(shape), d.reshape(shape), nm.reshape(shape), nv.reshape(shape))
    full_bg = lambda t: jnp.zeros((2, D), F32).at[:, :D // N_CHIPS].set(t["b_gate"][0])
    packs = [_pack_small({**{n: t[n] for n in SMALL[:8]}, "b_gate": full_bg(t)}) for t in (w, m, v)]
    g_pack = _pack_small({**{n: grads_small[n] for n in SMALL[:8]},
                          "b_gate": jnp.zeros((2, D), F32).at[:, :D // N_CHIPS].set(grads_small["b_gate"][0])})
    d_p, m_p, v_p = adamw("adamw_small", packs[0], g_pack, packs[1], packs[2], PACK_ROWS)
    d_s, m_s, v_s = (_unpack_small(p, 0) for p in (d_p, m_p, v_p))
    for n in SMALL:
        new[n] = (grads_small[n].reshape(w[n].shape), d_s[n].reshape(w[n].shape), m_s[n].reshape(w[n].shape),
                  v_s[n].reshape(w[n].shape))

    outs = [loss, dx[None]]
    for i in range(4):
        outs += [new[n][i] for n in ORDER]
    return tuple(outs)
```

```python
import functools
import math

import numpy as np
import jax
import jax.numpy as jnp
from jax import lax
from jax.experimental import pallas as pl
from jax.experimental.pallas import tpu as pltpu

F32 = jnp.float32
BF16 = jnp.bfloat16
MESH = pl.DeviceIdType.MESH

D = 1024
EPS = 1e-6
CHUNK = 128
GROUPS = 8
HEADS = 16
HEAD_DIM = 64
ATT_T = 256
N_CHIPS = 4
D_FF = 4 * D
IN_COLS = 7 * D
IN_SHARD = IN_COLS // N_CHIPS
MASKED = -1e30
VMEM_LIMIT = 56 * 2 ** 20

ADAM_LR, ADAM_B1, ADAM_B2, ADAM_EPS, ADAM_WD, ADAM_STEP = 0.001, 0.9, 0.999, 1e-08, 0.01, 10

NN = (((1,), (0,)), ((), ()))
NT = (((1,), (1,)), ((), ()))
TN = (((0,), (0,)), ((), ()))


def _dot(a, b, dims=NN):
    return lax.dot_general(a, b, dims, preferred_element_type=F32)


def _params(*sem, communicates=False):
    return pltpu.CompilerParams(dimension_semantics=sem or None, vmem_limit_bytes=VMEM_LIMIT,
                                has_side_effects=communicates)


def _rows(tr, c, col=0):
    return pl.BlockSpec((tr, c), lambda i: (i, col))


def _full(shape):
    n = len(shape)
    return pl.BlockSpec(shape, lambda *_: (0,) * n)


def _gelu(x):
    k = math.sqrt(2.0 / math.pi)
    return 0.5 * x * (1.0 + jnp.tanh(k * (x + 0.044715 * x * x * x)))


def _gelu_and_grad(x):
    k = math.sqrt(2.0 / math.pi)
    t = jnp.tanh(k * (x + 0.044715 * x * x * x))
    g = 0.5 * x * (1.0 + t)
    dg = 0.5 * (1.0 + t) + 0.5 * x * (1.0 - t * t) * (k * (1.0 + 3.0 * 0.044715 * x * x))
    return g, dg


def _sigmoid(x):
    return 1.0 / (1.0 + jnp.exp(-x))


def _rms(x):
    r = lax.rsqrt(jnp.mean(x * x, axis=-1, keepdims=True) + EPS)
    return x * r, r


def _rms_bwd(dn, xhat, r):
    return r * (dn - xhat * jnp.mean(dn * xhat, axis=-1, keepdims=True))


def norm_pre(x, g):
    s = x.shape[0]
    tr = 512

    def body(x_ref, g_ref, h_ref):
        xhat, _ = _rms(x_ref[...])
        h_ref[...] = (xhat * g_ref[...]).astype(BF16)

    return pl.pallas_call(
        body, name="norm_pre", grid=(s // tr,),
        in_specs=[_rows(tr, D), _full((1, D))], out_specs=_rows(tr, D),
        out_shape=jax.ShapeDtypeStruct((s, D), BF16), compiler_params=_params("parallel"),
    )(x, g)


def mm_in(h, wg):
    s = h.shape[0]
    tm, tn = 1024, IN_SHARD // 2
    per = IN_SHARD // tn

    def body(a_ref, b_ref, o_ref):
        o_ref[...] = _dot(a_ref[...], b_ref[...])

    return pl.pallas_call(
        body, name="mm_in", grid=(s // tm, IN_COLS // tn),
        in_specs=[pl.BlockSpec((tm, D), lambda i, j: (i, 0)),
                  pl.BlockSpec((None, D, tn), lambda i, j: (j // per, 0, j % per))],
        out_specs=pl.BlockSpec((tm, tn), lambda i, j: (i, j)),
        out_shape=jax.ShapeDtypeStruct((s, IN_COLS), F32), compiler_params=_params("parallel", "parallel"),
    )(h, wg)


def _tril_ws(ws_ref, g):
    r = lax.broadcasted_iota(jnp.int32, (CHUNK, CHUNK), 0)
    c = lax.broadcasted_iota(jnp.int32, (CHUNK, CHUNK), 1)
    return jnp.where(c <= r, ws_ref[g], 0.0).astype(BF16)


def _layer_norm(v):
    mu = jnp.mean(v, axis=-1, keepdims=True)
    d = v - mu
    rstd = lax.rsqrt(jnp.mean(d * d, axis=-1, keepdims=True) + EPS)
    return d * rstd, rstd


def gating_fwd(z, ln_g, ln_b, w_s, bs_t):
    s = z.shape[0]

    def body(u_ref, v_ref, lg_ref, lb_ref, ws_ref, bst_ref, ya_ref):
        ug = _gelu(u_ref[...])
        vhat, _ = _layer_norm(_gelu(v_ref[...]))
        vn = (vhat * lg_ref[...] + lb_ref[...]).astype(BF16)
        for g in range(GROUPS):
            cols = slice(g * CHUNK, (g + 1) * CHUNK)
            mixed = _dot(_tril_ws(ws_ref, g), vn[:, cols]) + bst_ref[:, g:g + 1]
            ya_ref[:, cols] = (ug[:, cols] * mixed).astype(BF16)

    return pl.pallas_call(
        body, name="gating_fwd", grid=(s // CHUNK,),
        in_specs=[_rows(CHUNK, D, 0), _rows(CHUNK, D, 1), _full((1, D)), _full((1, D)),
                  _full((GROUPS, CHUNK, CHUNK)), _full((CHUNK, GROUPS))],
        out_specs=_rows(CHUNK, D), out_shape=jax.ShapeDtypeStruct((s, D), BF16),
        compiler_params=_params("parallel"),
    )(z, z, ln_g, ln_b, w_s, bs_t)


def _attn_tables(s):
    nd = s // ATT_T
    r = np.arange(ATT_T)[None, :, None]
    c = np.arange(ATT_T)[None, None, :]
    delta = np.arange(nd)[:, None, None] * ATT_T + r - c
    count = np.zeros(delta.shape, np.int64)
    for window, dilation in ((128, 1), (512, 4), (2048, 16)):
        count += (delta >= 0) & (delta % dilation == 0) & (delta <= window)
    logc = np.where(count > 0, np.log(np.maximum(count, 1)), MASKED)
    return jnp.asarray(logc, F32)


AUG = 3


def _split3_np(x):
    terms, rest = [], np.asarray(x, np.float64)
    for _ in range(AUG):
        term = np.asarray(rest.astype(jnp.bfloat16), np.float64)
        terms.append(term)
        rest = rest - term
    return terms


def _split3(x):
    terms, rest = [], x
    for _ in range(AUG):
        term = rest.astype(BF16).astype(F32)
        terms.append(term)
        rest = rest - term
    return terms


def _alibi_tables(s):
    nb = s // ATT_T
    slopes = np.exp2(-8.0 * np.arange(1, HEADS + 1, dtype=np.float64) / HEADS)
    ka = np.zeros((HEADS // 2, 2, ATT_T, 128), np.float32)
    kb = np.zeros((HEADS // 2, 2, nb, 128), np.float32)
    for p in range(HEADS // 2):
        for e in range(2):
            base = HEAD_DIM * (1 - e)
            for a, term in enumerate(_split3_np(slopes[2 * p + e] * np.arange(ATT_T))):
                ka[p, e, :, base + a] = term
            for a, term in enumerate(_split3_np(slopes[2 * p + e] * ATT_T * np.arange(nb))):
                kb[p, e, :, base + AUG + a] = term
            ka[p, e, :, base + 2 * AUG:base + 3 * AUG] = 1.0
    return jnp.asarray(ka), jnp.asarray(kb)


def _head_masks():
    lane = lax.broadcasted_iota(jnp.int32, (1, 128), 1)
    first = lane < HEAD_DIM

    def ones(e, n):
        base = HEAD_DIM * (1 - e)
        return ((lane >= base) & (lane < base + n)).astype(F32)

    return first, lane, ones


def _place3(lane, at, terms, other):
    for a, term in enumerate(terms):
        other = jnp.where(lane == at + a, term, other)
    return other


def attn_fwd(z, logc, ka, kb, gathering):
    s = z.shape[0]
    nq = s // ATT_T
    t = ATT_T
    n = len(gathering)
    qcol, kcol, vcol = 2 * D // 128, 3 * D // 128, 4 * D // 128

    def body(*refs):
        q_ref, k_ref, v_ref, lc_ref, ka_ref, kb_ref = refs[:6]
        y_ref, lse_ref = refs[6 + n:8 + n]
        q_s, k_s, v_s, m_s, l_s, acc_s = refs[8 + 2 * n:14 + 2 * n]
        pair, qi = pl.program_id(0), pl.program_id(1)
        first, lane, ones = _head_masks()
        if n:
            send, pass_on, finish = _gather_phases(refs[8 + n:8 + 2 * n], *refs[14 + 2 * n:])
            pl.when((pair == 0) & (qi == 0))(send)
            pl.when((pair == HEADS // 2 - 2) & (qi == 0))(pass_on)

        @pl.when(qi == 0)
        def _():
            sel = jnp.broadcast_to(first.astype(F32), (t, 128))
            for jb in range(nq):
                kj = k_ref[jb * t:(jb + 1) * t, :]
                vj = v_ref[jb * t:(jb + 1) * t, :]
                k_s[0, jb] = jnp.where(first, kj, ka_ref[0] + kb_ref[0, jb:jb + 1, :]).astype(BF16)
                k_s[1, jb] = jnp.where(first, ka_ref[1] + kb_ref[1, jb:jb + 1, :], kj).astype(BF16)
                v_s[jb, 0:t, 0:128] = jnp.where(first, vj, 0.0).astype(BF16)
                v_s[jb, t:2 * t, 0:128] = jnp.where(first, 0.0, vj).astype(BF16)
                v_s[jb, 0:t, 128:256] = sel.astype(BF16)
                v_s[jb, t:2 * t, 128:256] = (1.0 - sel).astype(BF16)

        q = q_ref[...] * (1.0 / math.sqrt(HEAD_DIM))
        q_s[0] = jnp.where(first, q, ones(0, 2 * AUG)).astype(BF16)
        q_s[1] = jnp.where(first, ones(1, 2 * AUG), q).astype(BF16)
        m_s[...] = jnp.full_like(m_s, MASKED)
        l_s[...] = jnp.zeros_like(l_s)
        acc_s[...] = jnp.zeros_like(acc_s)

        def scores(j):
            return _dot(q_s[0], k_s[0, j], NT), _dot(q_s[1], k_s[1, j], NT)

        def step(j, u):
            nxt = scores(j + 1)
            softmax_block(j, u)
            return nxt

        def softmax_block(j, u):
            lc = lc_ref[qi - j]
            u0 = u[0] + lc
            u1 = u[1] + lc
            m0, m1 = m_s[0], m_s[1]
            n0 = jnp.maximum(m0, jnp.max(u0, axis=-1, keepdims=True))
            n1 = jnp.maximum(m1, jnp.max(u1, axis=-1, keepdims=True))
            m_s[0], m_s[1] = n0, n1
            p = jnp.concatenate([jnp.exp(u0 - jnp.concatenate([n0, n0], axis=1)).astype(BF16),
                                 jnp.exp(u1 - jnp.concatenate([n1, n1], axis=1)).astype(BF16)], axis=1)
            pv = _dot(p, v_s[j])
            alpha = jnp.where(first, jnp.exp(m0 - n0), jnp.exp(m1 - n1))
            acc_s[...] = acc_s[...] * alpha + pv[:, 0:128]
            l_s[...] = l_s[...] * alpha + pv[:, 128:256]

        softmax_block(qi, lax.fori_loop(0, qi, step, scores(0)))
        y_ref[...] = (acc_s[...] / l_s[...]).astype(BF16)
        lse_ref[...] = jnp.where(first, m_s[0], m_s[1]) + jnp.log(l_s[...])
        if n:
            pl.when((pair == HEADS // 2 - 1) & (qi == nq - 1))(finish)

    out = pl.pallas_call(
        body, name="attn_fwd", grid=(HEADS // 2, nq),
        in_specs=[pl.BlockSpec((t, 128), lambda p, i: (i, qcol + p)),
                  pl.BlockSpec((s, 128), lambda p, i: (0, kcol + p)),
                  pl.BlockSpec((s, 128), lambda p, i: (0, vcol + p)),
                  _full((nq, t, t)),
                  pl.BlockSpec((None, 2, t, 128), lambda p, i: (p, 0, 0, 0)),
                  pl.BlockSpec((None, 2, nq, 128), lambda p, i: (p, 0, 0, 0))] + [ANY] * n,
        out_specs=[pl.BlockSpec((t, 128), lambda p, i: (i, p)), pl.BlockSpec((t, 128), lambda p, i: (i, p))]
        + [ANY] * n,
        out_shape=[jax.ShapeDtypeStruct((s, D), BF16), jax.ShapeDtypeStruct((s, D), F32)]
        + [jax.ShapeDtypeStruct(a.shape, a.dtype) for a in gathering],
        input_output_aliases={6 + w: 2 + w for w in range(n)},
        scratch_shapes=[pltpu.VMEM((2, t, 128), BF16), pltpu.VMEM((2, nq, t, 128), BF16),
                        pltpu.VMEM((nq, 2 * t, 256), BF16), pltpu.VMEM((2, t, 128), F32),
                        pltpu.VMEM((t, 128), F32), pltpu.VMEM((t, 128), F32)] + (_gather_sems(n) if n else []),
        compiler_params=_params("arbitrary", "arbitrary", communicates=bool(n)),
    )(z, z, z, logc, ka, kb, *gathering)
    return out[0], out[1], out[2:]


def proj_merge(ya, yb, wa, wb, z, bg):
    s = ya.shape[0]
    tm = 512

    def body(ya_ref, yb_ref, wa_ref, wb_ref, ga_ref, gb_ref, bg_ref, mg_ref, pa_ref, pb_ref):
        pa = _dot(ya_ref[...], wa_ref[...])
        pb = _dot(yb_ref[...], wb_ref[...])
        sa = _sigmoid(ga_ref[...] + bg_ref[0:1, :])
        sb = _sigmoid(gb_ref[...] + bg_ref[1:2, :])
        mg_ref[...] = (sa * pa + sb * pb).astype(BF16)
        pa_ref[...] = pa.astype(BF16)
        pb_ref[...] = pb.astype(BF16)

    out = jax.ShapeDtypeStruct((s, D), BF16)
    return pl.pallas_call(
        body, name="proj_merge", grid=(s // tm,),
        in_specs=[_rows(tm, D), _rows(tm, D), _full((D, D)), _full((D, D)),
                  _rows(tm, D, 5), _rows(tm, D, 6), _full((2, D))],
        out_specs=[_rows(tm, D)] * 3, out_shape=[out] * 3, compiler_params=_params("parallel"),
    )(ya, yb, wa, wb, z, z, bg)


def out_norm(merged, w_out, x, g_post, g_fpre):
    s = x.shape[0]
    tm = 512

    def body(mg_ref, w_ref, x_ref, gp_ref, gf_ref, o_ref, x1_ref, h2_ref):
        o = _dot(mg_ref[...], w_ref[...])
        ohat, _ = _rms(o)
        x1 = x_ref[...] + ohat * gp_ref[...]
        x1hat, _ = _rms(x1)
        o_ref[...] = o
        x1_ref[...] = x1
        h2_ref[...] = (x1hat * gf_ref[...]).astype(BF16)

    return pl.pallas_call(
        body, name="out_norm", grid=(s // tm,),
        in_specs=[_rows(tm, D), _full((D, D)), _rows(tm, D), _full((1, D)), _full((1, D))],
        out_specs=[_rows(tm, D)] * 3,
        out_shape=[jax.ShapeDtypeStruct((s, D), F32), jax.ShapeDtypeStruct((s, D), F32),
                   jax.ShapeDtypeStruct((s, D), BF16)],
        compiler_params=_params("parallel"),
    )(merged, w_out, x, g_post, g_fpre)


def mm_ff1(h2, wg):
    s = h2.shape[0]
    tm = 1024

    def body(a_ref, b_ref, o_ref, r_ref):
        a = _dot(a_ref[...], b_ref[...])
        o_ref[...] = a
        r = jnp.maximum(a, 0.0)
        r_ref[...] = (r * r).astype(BF16)

    return pl.pallas_call(
        body, name="mm_ff1", grid=(s // tm, N_CHIPS),
        in_specs=[pl.BlockSpec((tm, D), lambda i, j: (i, 0)), pl.BlockSpec((None, D, D), lambda i, j: (j, 0, 0))],
        out_specs=[pl.BlockSpec((tm, D), lambda i, j: (i, j))] * 2,
        out_shape=[jax.ShapeDtypeStruct((s, D_FF), F32), jax.ShapeDtypeStruct((s, D_FF), BF16)],
        compiler_params=_params("parallel", "parallel"),
    )(h2, wg)


def ff2_loss(rl, w_ff2, x1, target, g_fpost):
    s = x1.shape[0]
    tm = 256

    def body(rl_ref, w_ref, x1_ref, t_ref, g_ref, dy_ref, df_ref, dg_ref, loss_ref):
        @pl.when(pl.program_id(0) == 0)
        def _():
            dg_ref[...] = jnp.zeros_like(dg_ref)
            loss_ref[...] = jnp.zeros_like(loss_ref)

        f = _dot(rl_ref[...], w_ref[...])
        fhat, r = _rms(f)
        err = x1_ref[...] + fhat * g_ref[...] - t_ref[...]
        loss_ref[...] += 0.5 * jnp.sum(jnp.mean(err * err, axis=-1, keepdims=True), axis=0, keepdims=True)
        dy = err * (1.0 / D)
        dy_ref[...] = dy
        dg_ref[...] += jnp.sum(dy * fhat, axis=0, keepdims=True)
        df_ref[...] = _rms_bwd(dy * g_ref[...], fhat, r).astype(BF16)

    return pl.pallas_call(
        body, name="ff2_loss", grid=(s // tm,),
        in_specs=[_rows(tm, D_FF), _full((D_FF, D)), _rows(tm, D), _rows(tm, D), _full((1, D))],
        out_specs=[_rows(tm, D), _rows(tm, D), _full((1, D)), _full((1, 1))],
        out_shape=[jax.ShapeDtypeStruct((s, D), F32), jax.ShapeDtypeStruct((s, D), BF16),
                   jax.ShapeDtypeStruct((1, D), F32), jax.ShapeDtypeStruct((1, 1), F32)],
        compiler_params=_params("arbitrary"),
    )(rl, w_ff2, x1, target, g_fpost)


def mm_tn(name, a, b, ta, tb, out_shape, out_spec):
    s = a.shape[0]

    def body(a_ref, b_ref, o_ref):
        o_ref[...] = _dot(a_ref[...], b_ref[...], TN)

    return pl.pallas_call(
        body, name=name, grid=(a.shape[1] // ta, b.shape[1] // tb),
        in_specs=[pl.BlockSpec((s, ta), lambda i, j: (0, i)), pl.BlockSpec((s, tb), lambda i, j: (0, j))],
        out_specs=out_spec, out_shape=jax.ShapeDtypeStruct(out_shape, F32),
        compiler_params=_params("parallel", "parallel"),
    )(a, b)


def mm_nt(name, a, w):
    s = a.shape[0]
    tm = 512

    def body(a_ref, w_ref, o_ref):
        o_ref[...] = _dot(a_ref[...], w_ref[...], NT)

    return pl.pallas_call(
        body, name=name, grid=(s // tm,), in_specs=[_rows(tm, D), _full((D, D))], out_specs=_rows(tm, D),
        out_shape=jax.ShapeDtypeStruct((s, D), F32), compiler_params=_params("parallel"),
    )(a, w)


def ff2_bwd(df, w_ff2, a):
    s = df.shape[0]
    tm = 1024

    def body(df_ref, w_ref, a_ref, da_ref):
        drl = _dot(df_ref[...], w_ref[...], NT)
        da_ref[...] = (drl * (2.0 * jnp.maximum(a_ref[...], 0.0))).astype(BF16)

    return pl.pallas_call(
        body, name="ff2_bwd", grid=(s // tm, D_FF // D),
        in_specs=[pl.BlockSpec((tm, D), lambda i, j: (i, 0)), pl.BlockSpec((D, D), lambda i, j: (j, 0)),
                  pl.BlockSpec((tm, D), lambda i, j: (i, j))],
        out_specs=pl.BlockSpec((tm, D), lambda i, j: (i, j)),
        out_shape=jax.ShapeDtypeStruct((s, D_FF), BF16), compiler_params=_params("parallel", "parallel"),
    )(df, w_ff2, a)


def ff1_bwd_norms(da, wg, x1, o, dy, g_fpre, g_post, swapping):
    s = x1.shape[0]
    tm = 512
    n = len(swapping)

    def body(*refs):
        da_ref, w_ref, x1_ref, o_ref, dy_ref, gf_ref, gp_ref = refs[:7]
        dx1_ref, do_ref, dgf_ref, dgp_ref = refs[7 + n:11 + n]
        acc_ref = refs[11 + 2 * n]
        i, k = pl.program_id(0), pl.program_id(1)
        if n:
            send, finish = _swap_phases(refs[7:7 + n], refs[11 + n:11 + 2 * n], *refs[12 + 2 * n:])
            pl.when((i == 0) & (k == 0))(send)

        @pl.when((i == 0) & (k == 0))
        def _():
            dgf_ref[...] = jnp.zeros_like(dgf_ref)
            dgp_ref[...] = jnp.zeros_like(dgp_ref)

        part = _dot(da_ref[...], w_ref[...], NT)

        @pl.when(k == 0)
        def _():
            acc_ref[...] = part

        @pl.when(k > 0)
        def _():
            acc_ref[...] += part

        @pl.when(k == N_CHIPS - 1)
        def _():
            dh2 = acc_ref[...]
            x1hat, r2 = _rms(x1_ref[...])
            dgf_ref[...] += jnp.sum(dh2 * x1hat, axis=0, keepdims=True)
            dx1 = dy_ref[...] + _rms_bwd(dh2 * gf_ref[...], x1hat, r2)
            ohat, r1 = _rms(o_ref[...])
            dgp_ref[...] += jnp.sum(dx1 * ohat, axis=0, keepdims=True)
            dx1_ref[...] = dx1
            do_ref[...] = _rms_bwd(dx1 * gp_ref[...], ohat, r1).astype(BF16)

        if n:
            pl.when((i == s // tm - 1) & (k == N_CHIPS - 1))(finish)

    row = pl.BlockSpec((tm, D), lambda i, k: (i, 0))
    vec = pl.BlockSpec((1, D), lambda i, k: (0, 0))
    res = pl.pallas_call(
        body, name="ff1_bwd_norms", grid=(s // tm, N_CHIPS),
        in_specs=[pl.BlockSpec((tm, D), lambda i, k: (i, k)), pl.BlockSpec((None, D, D), lambda i, k: (k, 0, 0)),
                  row, row, row, vec, vec] + [ANY] * n,
        out_specs=[row, row, vec, vec] + [ANY] * n,
        out_shape=[jax.ShapeDtypeStruct((s, D), F32), jax.ShapeDtypeStruct((s, D), BF16),
                   jax.ShapeDtypeStruct((1, D), F32), jax.ShapeDtypeStruct((1, D), F32)] + _swap_shapes(swapping),
        scratch_shapes=[pltpu.VMEM((tm, D), F32)] + (_swap_sems(n) if n else []),
        compiler_params=_params("arbitrary", "arbitrary", communicates=bool(n)),
    )(da, wg, x1, o, dy, g_fpre, g_post, *swapping)
    return res[0], res[1], res[2], res[3], res[4:]


def out_bwd_gates(do, w_out, pa, pb, z, bg):
    s = do.shape[0]
    tm = 512

    def body(do_ref, w_ref, pa_ref, pb_ref, ga_ref, gb_ref, bg_ref, dpa_ref, dpb_ref, dga_ref, dgb_ref, dbg_ref):
        @pl.when(pl.program_id(0) == 0)
        def _():
            dbg_ref[...] = jnp.zeros_like(dbg_ref)

        dm = _dot(do_ref[...], w_ref[...], NT)
        sa = _sigmoid(ga_ref[...] + bg_ref[0:1, :])
        sb = _sigmoid(gb_ref[...] + bg_ref[1:2, :])
        dpa_ref[...] = (dm * sa).astype(BF16)
        dpb_ref[...] = (dm * sb).astype(BF16)
        dga = dm * pa_ref[...].astype(F32) * (sa * (1.0 - sa))
        dgb = dm * pb_ref[...].astype(F32) * (sb * (1.0 - sb))
        dga_ref[...] = dga.astype(BF16)
        dgb_ref[...] = dgb.astype(BF16)
        dbg_ref[0:1, :] += jnp.sum(dga, axis=0, keepdims=True)
        dbg_ref[1:2, :] += jnp.sum(dgb, axis=0, keepdims=True)

    out = jax.ShapeDtypeStruct((s, D), BF16)
    return pl.pallas_call(
        body, name="out_bwd_gates", grid=(s // tm,),
        in_specs=[_rows(tm, D), _full((D, D)), _rows(tm, D), _rows(tm, D), _rows(tm, D, 5), _rows(tm, D, 6),
                  _full((2, D))],
        out_specs=[_rows(tm, D)] * 4 + [_full((2, D))],
        out_shape=[out] * 4 + [jax.ShapeDtypeStruct((2, D), F32)], compiler_params=_params("arbitrary"),
    )(do, w_out, pa, pb, z, z, bg)


def gating_bwd(z, dya, ln_g, ln_b, w_s, bs_t, swapping):
    s = z.shape[0]
    ones = functools.partial(jnp.ones, (8, CHUNK), BF16)
    n = len(swapping)

    def body(*refs):
        u_ref, v_ref, dya_ref, lg_ref, lb_ref, ws_ref, bst_ref = refs[:7]
        du_ref, dv_ref, dws_ref, dbs_ref, dlg_ref, dlb_ref = refs[7 + n:13 + n]
        dvn_ref = refs[13 + 2 * n]
        ci = pl.program_id(0)
        if n:
            send, finish = _swap_phases(refs[7:7 + n], refs[13 + n:13 + 2 * n], *refs[14 + 2 * n:])
            pl.when(ci == 0)(send)

        @pl.when(ci == 0)
        def _():
            dws_ref[...] = jnp.zeros_like(dws_ref)
            dbs_ref[...] = jnp.zeros_like(dbs_ref)
            dlg_ref[...] = jnp.zeros_like(dlg_ref)
            dlb_ref[...] = jnp.zeros_like(dlb_ref)

        ug, dug_du = _gelu_and_grad(u_ref[...])
        vg, dvg_dv = _gelu_and_grad(v_ref[...])
        vhat, rstd = _layer_norm(vg)
        vn = (vhat * lg_ref[...] + lb_ref[...]).astype(BF16)
        dya = dya_ref[...]
        for g in range(GROUPS):
            cols = slice(g * CHUNK, (g + 1) * CHUNK)
            ws = _tril_ws(ws_ref, g)
            mixed = _dot(ws, vn[:, cols]) + bst_ref[:, g:g + 1]
            du_ref[:, cols] = (dya[:, cols] * mixed * dug_du[:, cols]).astype(BF16)
            dmix = (dya[:, cols] * ug[:, cols]).astype(BF16)
            dbs_ref[g] += _dot(ones(), dmix, NT)
            dws_ref[g] += _dot(dmix, vn[:, cols], NT)
            dvn_ref[:, cols] = _dot(ws, dmix, TN)
        dvn = dvn_ref[...]
        dlg_ref[...] += jnp.sum(dvn * vhat, axis=0, keepdims=True)
        dlb_ref[...] += jnp.sum(dvn, axis=0, keepdims=True)
        dvh = dvn * lg_ref[...]
        dvg = rstd * (dvh - jnp.mean(dvh, axis=-1, keepdims=True)
                      - vhat * jnp.mean(dvh * vhat, axis=-1, keepdims=True))
        dv_ref[...] = (dvg * dvg_dv).astype(BF16)

        @pl.when(ci == pl.num_programs(0) - 1)
        def _():
            r = lax.broadcasted_iota(jnp.int32, (CHUNK, CHUNK), 0)
            c = lax.broadcasted_iota(jnp.int32, (CHUNK, CHUNK), 1)
            for g in range(GROUPS):
                dws_ref[g] = jnp.where(c <= r, dws_ref[g], 0.0)

        if n:
            pl.when(ci == pl.num_programs(0) - 1)(finish)

    out = jax.ShapeDtypeStruct((s, D), BF16)
    res = pl.pallas_call(
        body, name="gating_bwd", grid=(s // CHUNK,),
        in_specs=[_rows(CHUNK, D, 0), _rows(CHUNK, D, 1), _rows(CHUNK, D), _full((1, D)), _full((1, D)),
                  _full((GROUPS, CHUNK, CHUNK)), _full((CHUNK, GROUPS))] + [ANY] * n,
        out_specs=[_rows(CHUNK, D), _rows(CHUNK, D), _full((GROUPS, CHUNK, CHUNK)), _full((GROUPS, 8, CHUNK)),
                   _full((1, D)), _full((1, D))] + [ANY] * n,
        out_shape=[out, out, jax.ShapeDtypeStruct((GROUPS, CHUNK, CHUNK), F32),
                   jax.ShapeDtypeStruct((GROUPS, 8, CHUNK), F32),
                   jax.ShapeDtypeStruct((1, D), F32), jax.ShapeDtypeStruct((1, D), F32)] + _swap_shapes(swapping),
        scratch_shapes=[pltpu.VMEM((CHUNK, D), F32)] + (_swap_sems(n) if n else []),
        compiler_params=_params("arbitrary", communicates=bool(n)),
    )(z, z, dya, ln_g, ln_b, w_s, bs_t, *swapping)
    return (*res[:6], res[6:])


def attn_bwd(z, yb, dyb, lse, logc, ka, kb, scattering):
    s = z.shape[0]
    nq = s // ATT_T
    t = ATT_T
    qcol, kcol, vcol = 2 * D // 128, 3 * D // 128, 4 * D // 128
    scale = 1.0 / math.sqrt(HEAD_DIM)

    n = len(scattering)

    def body(*refs):
        q_ref, k_ref, v_ref, y_ref, dy_ref, lse_ref, lc_ref, ka_ref, kb_ref = refs[:9]
        dq_ref, dk_ref, dv_ref = refs[9 + n:12 + n]
        qa_s, qt_s, da_s, dt_s, dq_s, dkt_s, dvt_s = refs[12 + 2 * n:19 + 2 * n]
        pair, j = pl.program_id(0), pl.program_id(1)
        first, lane, ones = _head_masks()
        if n:
            send, finish = _scatter_phases(refs[9:9 + n], refs[12 + n:12 + 2 * n], *refs[19 + 2 * n:])
            pl.when((pair == 0) & (j == 0))(send)

        @pl.when(j == 0)
        def _():
            dq_s[...] = jnp.zeros_like(dq_s)
            for ib in range(nq):
                rows = slice(ib * t, (ib + 1) * t)
                q = q_ref[rows, :] * scale
                lse = lse_ref[rows, :]
                qa_s[0, ib] = jnp.where(first, q, _place3(lane, HEAD_DIM + 2 * AUG, _split3(-lse[:, 0:1]),
                                                          ones(0, 2 * AUG))).astype(BF16)
                qa_s[1, ib] = jnp.where(first, _place3(lane, 2 * AUG, _split3(-lse[:, HEAD_DIM:HEAD_DIM + 1]),
                                                       ones(1, 2 * AUG)), q).astype(BF16)
                qt_s[ib, :, 0:t] = jnp.where(first, q, 0.0).T.astype(BF16)
                qt_s[ib, :, t:2 * t] = jnp.where(first, 0.0, q).T.astype(BF16)
                do = dy_ref[rows, :]
                prod = do * y_ref[rows, :].astype(F32)
                dd0 = jnp.sum(jnp.where(first, prod, 0.0), axis=-1, keepdims=True)
                dd1 = jnp.sum(jnp.where(first, 0.0, prod), axis=-1, keepdims=True)
                da_s[0, ib] = jnp.where(first, do, _place3(lane, HEAD_DIM, _split3(-dd0), 0.0)).astype(BF16)
                da_s[1, ib] = jnp.where(first, _place3(lane, 0, _split3(-dd1), 0.0), do).astype(BF16)
                dt_s[ib, :, 0:t] = jnp.where(first, do, 0.0).T.astype(BF16)
                dt_s[ib, :, t:2 * t] = jnp.where(first, 0.0, do).T.astype(BF16)

        kj = k_ref[...]
        vj = v_ref[...]
        k0a = jnp.where(first, kj, ka_ref[0] + kb_ref[0, pl.ds(j, 1), :]).astype(BF16)
        k1a = jnp.where(first, ka_ref[1] + kb_ref[1, pl.ds(j, 1), :], kj).astype(BF16)
        kst = jnp.concatenate([jnp.where(first, kj, 0.0), jnp.where(first, 0.0, kj)], axis=0).astype(BF16)
        v0a = jnp.where(first, vj, ones(0, AUG)).astype(BF16)
        v1a = jnp.where(first, ones(1, AUG), vj).astype(BF16)
        dkt_s[...] = jnp.zeros_like(dkt_s)
        dvt_s[...] = jnp.zeros_like(dvt_s)

        def step(i, _):
            lc = lc_ref[i - j]
            p0 = jnp.exp(_dot(qa_s[0, i], k0a, NT) + lc)
            p1 = jnp.exp(_dot(qa_s[1, i], k1a, NT) + lc)
            e0 = (p0 * _dot(da_s[0, i], v0a, NT)).astype(BF16)
            e1 = (p1 * _dot(da_s[1, i], v1a, NT)).astype(BF16)
            rows = pl.ds(pl.multiple_of(i * t, t), t)
            dq_s[rows, :] += _dot(jnp.concatenate([e0, e1], axis=1), kst)
            dvt_s[...] += _dot(dt_s[i], jnp.concatenate([p0.astype(BF16), p1.astype(BF16)], axis=0))
            dkt_s[...] += _dot(qt_s[i], jnp.concatenate([e0, e1], axis=0))
            return 0

        lax.fori_loop(j, nq, step, 0)
        dk_ref[...] = dkt_s[...].T.astype(BF16)
        dv_ref[...] = dvt_s[...].T.astype(BF16)

        @pl.when(j == nq - 1)
        def _():
            dq_ref[...] = (dq_s[...] * scale).astype(BF16)

        if n:
            pl.when((pair == HEADS // 2 - 1) & (j == nq - 1))(finish)

    colblock = lambda c: pl.BlockSpec((s, 128), lambda p, j: (0, c + p))
    blk = lambda c: pl.BlockSpec((t, 128), lambda p, j: (j, c + p))
    out = jax.ShapeDtypeStruct((s, D), BF16)
    res = pl.pallas_call(
        body, name="attn_bwd", grid=(HEADS // 2, nq),
        in_specs=[colblock(qcol), blk(kcol), blk(vcol), colblock(0), colblock(0), colblock(0),
                  _full((nq, t, t)), pl.BlockSpec((None, 2, t, 128), lambda p, j: (p, 0, 0, 0)),
                  pl.BlockSpec((None, 2, nq, 128), lambda p, j: (p, 0, 0, 0))] + [ANY] * n,
        out_specs=[colblock(0), blk(0), blk(0)] + [ANY] * n, out_shape=[out] * 3 + _scatter_shapes(scattering),
        scratch_shapes=[pltpu.VMEM((2, nq, t, 128), BF16), pltpu.VMEM((nq, 128, 2 * t), BF16),
                        pltpu.VMEM((2, nq, t, 128), BF16), pltpu.VMEM((nq, 128, 2 * t), BF16),
                        pltpu.VMEM((s, 128), F32), pltpu.VMEM((128, t), F32), pltpu.VMEM((128, t), F32)]
        + (_scatter_sems(n) if n else []),
        compiler_params=_params("arbitrary", "arbitrary", communicates=bool(n)),
    )(z, z, z, yb, dyb, lse, logc, ka, kb, *scattering)
    return res[0], res[1], res[2], res[3:]


def in_bwd_norm(dz, wg, x, dx1, g_pre, scattering):
    s = x.shape[0]
    tm = 512
    n = len(scattering)

    def body(*refs):
        dz_ref, w_ref, x_ref, dx1_ref, g_ref = refs[:5]
        dx_ref, dg_ref = refs[5 + n:7 + n]
        acc_ref = refs[7 + 2 * n]
        i, k = pl.program_id(0), pl.program_id(1)
        if n:
            send, finish = _scatter_phases(refs[5:5 + n], refs[7 + n:7 + 2 * n], *refs[8 + 2 * n:])
            pl.when((i == 0) & (k == 0))(send)

        @pl.when((i == 0) & (k == 0))
        def _():
            dg_ref[...] = jnp.zeros_like(dg_ref)

        part = _dot(dz_ref[...], w_ref[...], NT)

        @pl.when(k == 0)
        def _():
            acc_ref[...] = part

        @pl.when(k > 0)
        def _():
            acc_ref[...] += part

        @pl.when(k == N_CHIPS - 1)
        def _():
            dh = acc_ref[...]
            xhat, r = _rms(x_ref[...])
            dg_ref[...] += jnp.sum(dh * xhat, axis=0, keepdims=True)
            dx_ref[...] = dx1_ref[...] + _rms_bwd(dh * g_ref[...], xhat, r)

        if n:
            pl.when((i == s // tm - 1) & (k == N_CHIPS - 1))(finish)

    row = pl.BlockSpec((tm, D), lambda i, k: (i, 0))
    vec = pl.BlockSpec((1, D), lambda i, k: (0, 0))
    res = pl.pallas_call(
        body, name="in_bwd_norm", grid=(s // tm, N_CHIPS),
        in_specs=[pl.BlockSpec((tm, IN_SHARD), lambda i, k: (i, k)),
                  pl.BlockSpec((None, D, IN_SHARD), lambda i, k: (k, 0, 0)), row, row, vec] + [ANY] * n,
        out_specs=[row, vec] + [ANY] * n,
        out_shape=[jax.ShapeDtypeStruct((s, D), F32), jax.ShapeDtypeStruct((1, D), F32)]
        + _scatter_shapes(scattering),
        scratch_shapes=[pltpu.VMEM((tm, D), F32)] + (_scatter_sems(n) if n else []),
        compiler_params=_params("arbitrary", "arbitrary", communicates=bool(n)),
    )(dz, wg, x, dx1, g_pre, *scattering)
    return res[0], res[1], res[2:]


def _adamw_math(w, g, m, v):
    m = ADAM_B1 * m + (1.0 - ADAM_B1) * g
    v = ADAM_B2 * v + (1.0 - ADAM_B2) * (g * g)
    m_hat = m / (1.0 - ADAM_B1 ** ADAM_STEP)
    v_hat = v / (1.0 - ADAM_B2 ** ADAM_STEP)
    delta = -ADAM_LR * (m_hat / (jnp.sqrt(v_hat) + ADAM_EPS) + ADAM_WD * w)
    return delta, m, v


def adamw(name, w, g, m, v, tr):
    r, c = w.shape

    def body(w_ref, g_ref, m_ref, v_ref, go_ref, d_ref, nm_ref, nv_ref):
        g = g_ref[...]
        go_ref[...] = g
        d_ref[...], nm_ref[...], nv_ref[...] = _adamw_math(w_ref[...], g, m_ref[...], v_ref[...])

    out = jax.ShapeDtypeStruct((r, c), F32)
    return pl.pallas_call(
        body, name=name, grid=(r // tr,), in_specs=[_rows(tr, c)] * 4, out_specs=[_rows(tr, c)] * 4,
        out_shape=[out] * 4, compiler_params=_params("parallel"),
    )(w, g, m, v)


def add_halves(name, g, recv, c_idx, tr):
    n, h, c = recv.shape

    def body(c_ref, g_ref, r_ref, o_ref):
        o_ref[...] = (g_ref[...] + r_ref[...]).astype(BF16)

    nb = h // tr
    return pl.pallas_call(
        body, name=name,
        grid_spec=pltpu.PrefetchScalarGridSpec(
            num_scalar_prefetch=1, grid=(n, nb),
            in_specs=[pl.BlockSpec((None, tr, c), lambda k, i, c_ref: (k, c_ref[0] * nb + i, 0)),
                      pl.BlockSpec((None, tr, c), lambda k, i, c_ref: (k, i, 0))],
            out_specs=pl.BlockSpec((None, tr, c), lambda k, i, c_ref: (k, i, 0))),
        out_shape=jax.ShapeDtypeStruct((n, h, c), BF16), compiler_params=_params("parallel", "parallel"),
    )(c_idx, g, recv)


def sum_chips(name, parts, recv, where, tr):
    n, h, c = recv.shape
    nb = h // tr

    def body(w_ref, p_ref, r_ref, o_ref):
        acc = p_ref[...].astype(F32)
        for k in range(n):
            acc = acc + r_ref[k].astype(F32)
        o_ref[...] = acc

    return pl.pallas_call(
        body, name=name,
        grid_spec=pltpu.PrefetchScalarGridSpec(
            num_scalar_prefetch=1, grid=(nb,),
            in_specs=[pl.BlockSpec((None, tr, c), lambda i, w_ref: (w_ref[0], i, 0)),
                      pl.BlockSpec((n, tr, c), lambda i, w_ref: (0, i, 0))],
            out_specs=pl.BlockSpec((tr, c), lambda i, w_ref: (w_ref[1] * nb + i, 0))),
        out_shape=jax.ShapeDtypeStruct((2 * h, c), F32), compiler_params=_params("parallel"),
    )(where, parts, recv)


def place_shard(name, shard, where, dtype, tr):
    r, c = shard.shape

    def body(w_ref, s_ref, o_ref):
        o_ref[...] = s_ref[...].astype(dtype)

    return pl.pallas_call(
        body, name=name,
        grid_spec=pltpu.PrefetchScalarGridSpec(
            num_scalar_prefetch=1, grid=(r // tr,),
            in_specs=[pl.BlockSpec((tr, c), lambda i, w_ref: (i, 0))],
            out_specs=pl.BlockSpec((None, tr, c), lambda i, w_ref: (w_ref[0], i, 0))),
        out_shape=jax.ShapeDtypeStruct((N_CHIPS, r, c), dtype), compiler_params=_params("parallel"),
    )(where, shard)


ANY = pl.BlockSpec(memory_space=pl.ANY)


def _place():
    x, y, c = lax.axis_index("x"), lax.axis_index("y"), lax.axis_index("c")
    chips = [(1 - x, y), (x, 1 - y), (1 - x, 1 - y)]
    return x, y, c, chips


def gather_shards(arrays):
    n = len(arrays)

    def body(*refs):
        send, pass_on, finish = _gather_phases(refs[n:2 * n], *refs[2 * n:])
        send()
        pass_on()
        finish()

    return pl.pallas_call(
        body, name="gather_shards", in_specs=[ANY] * n, out_specs=[ANY] * n,
        out_shape=[jax.ShapeDtypeStruct(a.shape, a.dtype) for a in arrays],
        input_output_aliases={w: w for w in range(n)}, scratch_shapes=_gather_sems(n),
        compiler_params=pltpu.CompilerParams(has_side_effects=True),
    )(*arrays)


def _gather_sems(n):
    return [pltpu.SemaphoreType.DMA((6 * n,)), pltpu.SemaphoreType.DMA((6 * n,))]


def _gather_phases(out, send_sems, recv_sems):
    n = len(out)
    x, y, c, chips = _place()
    me = 2 * x + y
    sibling = (x, y, 1 - c)

    def half(w, chip, core):
        h = out[w].shape[1] // 2
        return out[w].at[chip, pl.ds(core * h, h)]

    def copy(k, block, to):
        return pltpu.make_async_remote_copy(src_ref=block, dst_ref=block, send_sem=send_sems.at[k],
                                            recv_sem=recv_sems.at[k], device_id=to, device_id_type=MESH)

    def over_ici(w, j, chip):
        return copy(3 * w + j, half(w, chip, c), (chips[j][0], chips[j][1], c))

    def over_d2d(w, j, core):
        return copy(3 * n + 3 * w + j, half(w, 2 * chips[j][0] + chips[j][1], core), sibling)

    pairs = [(w, j) for w in range(n) for j in range(3)]

    def send():
        for w, j in pairs:
            over_ici(w, j, me).start()

    def pass_on():
        for w, j in pairs:
            over_ici(w, j, 2 * chips[j][0] + chips[j][1]).wait_recv()
            over_d2d(w, j, c).start()

    def finish():
        for w, j in pairs:
            over_d2d(w, j, 1 - c).wait_recv()
        for w, j in pairs:
            over_ici(w, j, me).wait_send()
            over_d2d(w, j, c).wait_send()

    return send, pass_on, finish


def swap_halves(name, grads):
    n = len(grads)

    def body(*refs):
        send, finish = _swap_phases(refs[:n], refs[n:2 * n], *refs[2 * n:])
        send()
        finish()

    return pl.pallas_call(
        body, name=name, in_specs=[ANY] * n, out_specs=[ANY] * n, out_shape=_swap_shapes(grads),
        scratch_shapes=_swap_sems(n), compiler_params=pltpu.CompilerParams(has_side_effects=True),
    )(*grads)


def _swap_shapes(grads):
    return [jax.ShapeDtypeStruct((a.shape[0], a.shape[1] // 2, a.shape[2]), a.dtype) for a in grads]


def _swap_sems(n):
    return [pltpu.SemaphoreType.DMA((n,)), pltpu.SemaphoreType.DMA((n,))]


def _swap_phases(g, out, send_sems, recv_sems):
    x, y, c, _ = _place()

    def copies():
        return [pltpu.make_async_remote_copy(
            src_ref=g[w].at[:, pl.ds((1 - c) * (g[w].shape[1] // 2), g[w].shape[1] // 2)], dst_ref=out[w],
            send_sem=send_sems.at[w], recv_sem=recv_sems.at[w], device_id=(x, y, 1 - c), device_id_type=MESH)
            for w in range(len(g))]

    def send():
        for cp in copies():
            cp.start()

    def finish():
        for cp in copies():
            cp.wait()

    return send, finish


def scatter_chips(parts):
    n = len(parts)

    def body(*refs):
        send, finish = _scatter_phases(refs[:n], refs[n:2 * n], *refs[2 * n:])
        send()
        finish()

    return pl.pallas_call(
        body, name="scatter_chips", in_specs=[ANY] * n, out_specs=[ANY] * n,
        out_shape=_scatter_shapes(parts), scratch_shapes=_scatter_sems(n),
        compiler_params=pltpu.CompilerParams(has_side_effects=True),
    )(*parts)


def _scatter_shapes(parts):
    return [jax.ShapeDtypeStruct((3,) + a.shape[1:], a.dtype) for a in parts]


def _scatter_sems(n):
    return [pltpu.SemaphoreType.DMA((3 * n,)), pltpu.SemaphoreType.DMA((3 * n,))]


def _scatter_phases(p, out, send_sems, recv_sems):
    x, y, c, chips = _place()

    def copies():
        return [pltpu.make_async_remote_copy(
            src_ref=p[w].at[2 * px + py], dst_ref=out[w].at[j], send_sem=send_sems.at[3 * w + j],
            recv_sem=recv_sems.at[3 * w + j], device_id=(px, py, c), device_id_type=MESH)
            for w in range(len(p)) for j, (px, py) in enumerate(chips)]

    def send():
        for cp in copies():
            cp.start()

    def finish():
        for cp in copies():
            cp.wait()

    return send, finish


def join_halves(arrays):
    n = len(arrays)

    def body(*refs):
        out = refs[n:2 * n]
        send_sems, recv_sems = refs[2 * n:]
        x, y, c, _ = _place()

        def copy(w, core):
            h = out[w].shape[0] // 2
            rows = out[w].at[pl.ds(core * h, h)]
            return pltpu.make_async_remote_copy(
                src_ref=rows, dst_ref=rows, send_sem=send_sems.at[w], recv_sem=recv_sems.at[w],
                device_id=(x, y, 1 - c), device_id_type=MESH)

        for w in range(n):
            copy(w, c).start()
        for w in range(n):
            copy(w, 1 - c).wait_recv()
        for w in range(n):
            copy(w, c).wait_send()

    return pl.pallas_call(
        body, name="join_halves", in_specs=[ANY] * n, out_specs=[ANY] * n,
        out_shape=[jax.ShapeDtypeStruct(a.shape, a.dtype) for a in arrays],
        input_output_aliases={w: w for w in range(n)},
        scratch_shapes=[pltpu.SemaphoreType.DMA((n,)), pltpu.SemaphoreType.DMA((n,))],
        compiler_params=pltpu.CompilerParams(has_side_effects=True),
    )(*arrays)


def allreduce_small(packed):
    r, c = packed.shape
    n_dev = 8

    def body(x_ref, all_ref, sum_ref, send_sems, recv_sems, local_sem):
        x, y, cc, chips = _place()
        me, sibling = (x, y, cc), (x, y, 1 - cc)

        def rows(px, py, pc):
            return all_ref.at[4 * px + 2 * py + pc]

        def copy(k, block, to, src=None):
            return pltpu.make_async_remote_copy(
                src_ref=rows(*block) if src is None else src, dst_ref=rows(*block), send_sem=send_sems.at[k],
                recv_sem=recv_sems.at[k], device_id=to, device_id_type=MESH)

        mine = pltpu.make_async_copy(x_ref, rows(*me), local_sem)
        mine.start()
        first = [copy(0, me, sibling, src=x_ref)]
        first += [copy(1 + j, me, (*chip, cc), src=x_ref) for j, chip in enumerate(chips)]
        for cp in first:
            cp.start()
        passed = [copy(4 + j, (*chip, cc), sibling) for j, chip in enumerate(chips)]
        for j, chip in enumerate(chips):
            copy(1 + j, (*chip, cc), me).wait_recv()
            passed[j].start()
        copy(0, sibling, me).wait_recv()
        for j, chip in enumerate(chips):
            copy(4 + j, (*chip, 1 - cc), me).wait_recv()
        for cp in first + passed:
            cp.wait_send()
        mine.wait()
        acc = all_ref[0]
        for k in range(1, n_dev):
            acc = acc + all_ref[k]
        sum_ref[...] = acc

    vm = pl.BlockSpec(memory_space=pltpu.VMEM)
    return pl.pallas_call(
        body, name="allreduce_small", in_specs=[vm], out_specs=[vm, vm],
        out_shape=[jax.ShapeDtypeStruct((n_dev, r, c), F32), jax.ShapeDtypeStruct((r, c), F32)],
        scratch_shapes=[pltpu.SemaphoreType.DMA((7,)), pltpu.SemaphoreType.DMA((7,)), pltpu.SemaphoreType.DMA],
        compiler_params=pltpu.CompilerParams(has_side_effects=True, vmem_limit_bytes=VMEM_LIMIT),
    )(packed)[1]


def local_step(x, target, vecs, w_s, bs_t, bg, wg_in, late, core=None):
    on_mesh = core is not None

    def add(names, grads, recv):
        return [add_halves("add_" + n, g, r, core, min(r.shape[1], 256)) for n, g, r in zip(names, grads, recv)]

    g_pre, ln_g, ln_b, g_post, g_fpre, g_fpost = vecs
    s = x.shape[0]
    logc = _attn_tables(s)
    ka, kb = _alibi_tables(s)

    h = norm_pre(x, g_pre)
    z = mm_in(h, wg_in)
    ya = gating_fwd(z, ln_g, ln_b, w_s, bs_t)
    yb, lse, gathered = attn_fwd(z, logc, ka, kb, late if on_mesh else [])
    wg_a, wg_b, wg_out, wg_ff1, wg_ff2 = gathered if on_mesh else late
    w_a, w_b, w_out, w_ff2 = wg_a.reshape(D, D), wg_b.reshape(D, D), wg_out.reshape(D, D), wg_ff2.reshape(D_FF, D)
    merged, pa, pb = proj_merge(ya, yb, w_a, w_b, z, bg)
    o, x1, h2 = out_norm(merged, w_out, x, g_post, g_fpre)
    a, rl = mm_ff1(h2, wg_ff1)
    dy, df, d_gfpost, loss = ff2_loss(rl, w_ff2, x1, target, g_fpost)

    half_cols = pl.BlockSpec((D, D // 2), lambda i, j: (0, j))
    d_wff2 = mm_tn("dw_ff2", rl, df, D // 2, D, (D_FF, D), pl.BlockSpec((D // 2, D), lambda i, j: (i, 0)))
    da = ff2_bwd(df, w_ff2, a)
    d_wff1 = mm_tn("dw_ff1", h2, da, D, D // 2, (N_CHIPS, D, D),
                   pl.BlockSpec((None, D, D // 2), lambda i, j: (j // 2, 0, j % 2)))
    d_ff = [d_wff1, d_wff2.reshape(N_CHIPS, D, D)]
    dx1, do, d_gfpre, d_gpost, recv_ff = ff1_bwd_norms(da, wg_ff1, x1, o, dy, g_fpre, g_post, d_ff if on_mesh else [])
    d_wout = mm_tn("dw_out", merged, do, D, D // 2, (D, D), half_cols)
    dpa, dpb, dga, dgb, d_bg = out_bwd_gates(do, w_out, pa, pb, z, bg)
    d_wa = mm_tn("dw_a", ya, dpa, D, D // 2, (D, D), half_cols)
    d_wb = mm_tn("dw_b", yb, dpb, D, D // 2, (D, D), half_cols)
    dya = mm_nt("dy_a", dpa, w_a)
    dyb = mm_nt("dy_b", dpb, w_b)
    d_proj = [d_wa.reshape(N_CHIPS, D // N_CHIPS, D), d_wb.reshape(N_CHIPS, D // N_CHIPS, D),
              d_wout.reshape(N_CHIPS, D // N_CHIPS, D)]
    du, dv, d_ws, d_bs, d_lng, d_lnb, recv_proj = gating_bwd(z, dya, ln_g, ln_b, w_s, bs_t, d_proj if on_mesh else [])
    early = d_proj + d_ff
    parts_early = add(BIG[1:], early, list(recv_proj) + list(recv_ff)) if on_mesh else []
    dq, dk, dvb, got_early = attn_bwd(z, yb, dyb, lse, logc, ka, kb, parts_early)
    dz = jnp.concatenate([du, dv, dq, dk, dvb, dga, dgb], axis=1)
    half = IN_SHARD // 2
    d_win = mm_tn("dw_in", h, dz, D, half, (N_CHIPS, D, IN_SHARD),
                  pl.BlockSpec((None, D, half), lambda i, j: (j // 2, 0, j % 2)))
    parts_late = add(BIG[:1], [d_win], swap_halves("swap_w_in", [d_win])) if on_mesh else []
    dx, d_gpre, got_late = in_bwd_norm(dz, wg_in, x, dx1, g_pre, parts_late)

    small = dict(norm_mix_pre=d_gpre, b_gate=d_bg, ln_v_g=d_lng, ln_v_b=d_lnb, w_s=d_ws, b_s=d_bs[:, 0, :],
                 norm_mix_post=d_gpost, norm_ffn_pre=d_gfpre, norm_ffn_post=d_gfpost)
    return loss[0, 0], dx, [d_win] + early, small, parts_late + parts_early, list(got_late) + list(got_early)


BIG = ("w_in", "w_a_proj", "w_b_proj", "w_out", "w_ff1", "w_ff2")
SMALL = ("norm_mix_pre", "ln_v_g", "ln_v_b", "b_s", "norm_mix_post", "norm_ffn_pre", "norm_ffn_post", "w_s", "b_gate")
ORDER = ("norm_mix_pre", "w_in", "b_gate", "ln_v_g", "ln_v_b", "w_s", "b_s", "w_a_proj", "w_b_proj", "w_out",
         "norm_mix_post", "norm_ffn_pre", "w_ff1", "w_ff2", "norm_ffn_post")
VEC_ROWS = D // 128
WS_ROW = 7 * VEC_ROWS
BG_ROW = WS_ROW + GROUPS * CHUNK
LOSS_ROW = BG_ROW + 2 * VEC_ROWS
PACK_ROWS = LOSS_ROW + 8


def pack_small(small, loss):
    vectors = [small[n] for n in SMALL[:7]]

    def body(*refs):
        out = refs[-1]
        ws_ref, bg_ref, loss_ref = refs[7:10]
        for i, n in enumerate(SMALL[:7]):
            if n == "b_s":
                out[i * VEC_ROWS:(i + 1) * VEC_ROWS, :] = refs[i][...]
            else:
                for j in range(VEC_ROWS):
                    out[i * VEC_ROWS + j:i * VEC_ROWS + j + 1, :] = refs[i][:, j * 128:(j + 1) * 128]
        for g in range(GROUPS):
            out[WS_ROW + g * CHUNK:WS_ROW + (g + 1) * CHUNK, :] = ws_ref[g]
        for r in range(2):
            for j in range(VEC_ROWS):
                row = BG_ROW + r * VEC_ROWS + j
                out[row:row + 1, :] = bg_ref[r:r + 1, j * 128:(j + 1) * 128]
        lane = lax.broadcasted_iota(jnp.int32, (8, 128), 1)
        sub = lax.broadcasted_iota(jnp.int32, (8, 128), 0)
        out[LOSS_ROW:LOSS_ROW + 8, :] = jnp.where((lane == 0) & (sub == 0), loss_ref[...], 0.0)

    return pl.pallas_call(
        body, name="pack_small", out_shape=jax.ShapeDtypeStruct((PACK_ROWS, 128), F32),
        compiler_params=_params(),
    )(*vectors, small["w_s"], small["b_gate"], loss)


def adamw_small(summed, chip, w, m, v):
    shapes = {n: (1, D) for n in SMALL}
    shapes.update(b_s=(GROUPS, CHUNK), w_s=(GROUPS * CHUNK, CHUNK), b_gate=(2, D // N_CHIPS))
    flat = lambda t: [t[n].reshape(shapes[n]) for n in SMALL]
    per = D // N_CHIPS // 128

    def body(chip_ref, sum_ref, *refs):
        params, outs = refs[:27], refs[27:]
        sub = lax.broadcasted_iota(jnp.int32, (VEC_ROWS, 128), 0)

        def gate_row(r):
            rows = sum_ref[BG_ROW + r * VEC_ROWS:BG_ROW + (r + 1) * VEC_ROWS, :]
            return jnp.concatenate([jnp.sum(jnp.where(sub == per * chip_ref[0] + j, rows, 0.0), axis=0, keepdims=True)
                                    for j in range(per)], axis=1)

        for i, n in enumerate(SMALL):
            if n == "b_s":
                g = sum_ref[i * VEC_ROWS:(i + 1) * VEC_ROWS, :]
            elif n == "w_s":
                g = sum_ref[WS_ROW:BG_ROW, :]
            elif n == "b_gate":
                g = jnp.concatenate([gate_row(0), gate_row(1)], axis=0)
            else:
                g = jnp.concatenate([sum_ref[i * VEC_ROWS + j:i * VEC_ROWS + j + 1, :] for j in range(VEC_ROWS)],
                                    axis=1)
            delta, nm, nv = _adamw_math(params[i][...], g, params[9 + i][...], params[18 + i][...])
            outs[4 * i][...], outs[4 * i + 1][...], outs[4 * i + 2][...], outs[4 * i + 3][...] = g, delta, nm, nv

    vm = pl.BlockSpec(memory_space=pltpu.VMEM)
    res = pl.pallas_call(
        body, name="adamw_small",
        in_specs=[pl.BlockSpec(memory_space=pltpu.SMEM)] + [vm] * 28, out_specs=[vm] * 36,
        out_shape=[jax.ShapeDtypeStruct(shapes[n], F32) for n in SMALL for _ in range(4)],
        compiler_params=_params(),
    )(chip, summed, *flat(w), *flat(m), *flat(v))
    return {n: tuple(r.reshape(w[n].shape) for r in res[4 * i:4 * i + 4]) for i, n in enumerate(SMALL)}


def kernel(x, norm_mix_pre, w_in, b_gate, ln_v_g, ln_v_b, w_s, b_s, w_a_proj, w_b_proj, w_out, norm_mix_post, norm_ffn_pre, w_ff1, w_ff2, norm_ffn_post, loss_target, m_norm_mix_pre, m_w_in, m_b_gate, m_ln_v_g, m_ln_v_b, m_w_s, m_b_s, m_w_a_proj, m_w_b_proj, m_w_out, m_norm_mix_post, m_norm_ffn_pre, m_w_ff1, m_w_ff2, m_norm_ffn_post, v_norm_mix_pre, v_w_in, v_b_gate, v_ln_v_g, v_ln_v_b, v_w_s, v_b_s, v_w_a_proj, v_w_b_proj, v_w_out, v_norm_mix_post, v_norm_ffn_pre, v_w_ff1, v_w_ff2, v_norm_ffn_post):
    w = dict(norm_mix_pre=norm_mix_pre, w_in=w_in, b_gate=b_gate, ln_v_g=ln_v_g, ln_v_b=ln_v_b, w_s=w_s, b_s=b_s,
             w_a_proj=w_a_proj, w_b_proj=w_b_proj, w_out=w_out, norm_mix_post=norm_mix_post,
             norm_ffn_pre=norm_ffn_pre, w_ff1=w_ff1, w_ff2=w_ff2, norm_ffn_post=norm_ffn_post)
    m = dict(norm_mix_pre=m_norm_mix_pre, w_in=m_w_in, b_gate=m_b_gate, ln_v_g=m_ln_v_g, ln_v_b=m_ln_v_b, w_s=m_w_s,
             b_s=m_b_s, w_a_proj=m_w_a_proj, w_b_proj=m_w_b_proj, w_out=m_w_out, norm_mix_post=m_norm_mix_post,
             norm_ffn_pre=m_norm_ffn_pre, w_ff1=m_w_ff1, w_ff2=m_w_ff2, norm_ffn_post=m_norm_ffn_post)
    v = dict(norm_mix_pre=v_norm_mix_pre, w_in=v_w_in, b_gate=v_b_gate, ln_v_g=v_ln_v_g, ln_v_b=v_ln_v_b, w_s=v_w_s,
             b_s=v_b_s, w_a_proj=v_w_a_proj, w_b_proj=v_w_b_proj, w_out=v_w_out, norm_mix_post=v_norm_mix_post,
             norm_ffn_pre=v_norm_ffn_pre, w_ff1=v_w_ff1, w_ff2=v_w_ff2, norm_ffn_post=v_norm_ffn_post)
    chip = 2 * lax.axis_index("x") + lax.axis_index("y")
    core = lax.axis_index("c")

    where = jnp.stack([chip, core]).astype(jnp.int32)
    placed = [place_shard("place_" + n, w[n][0], where, BF16, min(w[n].shape[1], 256)) for n in BIG]
    placed.append(place_shard("place_b_gate", jnp.pad(b_gate[0], ((0, 14), (0, 0))), where, F32, 16))
    wg_in, bg_all = gather_shards([placed[0], placed[6]])
    bg = jnp.transpose(bg_all[:, :2, :], (1, 0, 2)).reshape(2, D)
    vecs = (norm_mix_pre, ln_v_g, ln_v_b, norm_mix_post, norm_ffn_pre, norm_ffn_post)
    loss, dx, _, small, parts, got = local_step(
        x[0], loss_target[0], vecs, w_s[0], b_s[0].T, bg, wg_in, placed[1:6],
        core=jnp.reshape(core, (1,)).astype(jnp.int32))

    halves = [sum_chips("sum_" + n, p, r, where, min(p.shape[1], 256)) for n, p, r in zip(BIG, parts, got)]
    grads = dict(zip(BIG, join_halves(halves)))

    summed = allreduce_small(pack_small(small, loss.reshape(1, 1)))
    loss = summed[LOSS_ROW, 0]

    new = adamw_small(summed, jnp.reshape(chip, (1,)).astype(jnp.int32), w, m, v)
    for n in BIG:
        shape = w[n].shape
        res = adamw("adamw_" + n, w[n][0], grads[n], m[n][0], v[n][0], min(shape[1], 256))
        new[n] = tuple(r.reshape(shape) for r in res)

    outs = [loss, dx[None]]
    for i in range(4):
        outs += [new[n][i] for n in ORDER]
    return tuple(outs)
```

```python
import functools
import math

import numpy as np
import jax
import jax.numpy as jnp
from jax import lax
from jax.experimental import pallas as pl
from jax.experimental.pallas import tpu as pltpu

F32 = jnp.float32
BF16 = jnp.bfloat16
MESH = pl.DeviceIdType.MESH

D = 1024
EPS = 1e-6
CHUNK = 128
GROUPS = 8
HEADS = 16
HEAD_DIM = 64
ATT_T = 256
ATT_GROUP = 4
ATT_BWD_GROUP = 2
N_CHIPS = 4
D_FF = 4 * D
IN_COLS = 7 * D
IN_SHARD = IN_COLS // N_CHIPS
MASKED = -1e30
VMEM_LIMIT = 56 * 2 ** 20

ADAM_LR, ADAM_B1, ADAM_B2, ADAM_EPS, ADAM_WD, ADAM_STEP = 0.001, 0.9, 0.999, 1e-08, 0.01, 10

NN = (((1,), (0,)), ((), ()))
NT = (((1,), (1,)), ((), ()))
TN = (((0,), (0,)), ((), ()))


def _dot(a, b, dims=NN):
    return lax.dot_general(a, b, dims, preferred_element_type=F32)


def _params(*sem, communicates=False):
    return pltpu.CompilerParams(dimension_semantics=sem or None, vmem_limit_bytes=VMEM_LIMIT,
                                has_side_effects=communicates)


def _rows(tr, c, col=0):
    return pl.BlockSpec((tr, c), lambda i: (i, col))


def _full(shape):
    n = len(shape)
    return pl.BlockSpec(shape, lambda *_: (0,) * n)


def _gelu(x):
    k = math.sqrt(2.0 / math.pi)
    return 0.5 * x * (1.0 + jnp.tanh(k * (x + 0.044715 * x * x * x)))


def _gelu_and_grad(x):
    k = math.sqrt(2.0 / math.pi)
    t = jnp.tanh(k * (x + 0.044715 * x * x * x))
    g = 0.5 * x * (1.0 + t)
    dg = 0.5 * (1.0 + t) + 0.5 * x * (1.0 - t * t) * (k * (1.0 + 3.0 * 0.044715 * x * x))
    return g, dg


def _sigmoid(x):
    return 1.0 / (1.0 + jnp.exp(-x))


def _rms(x):
    r = lax.rsqrt(jnp.mean(x * x, axis=-1, keepdims=True) + EPS)
    return x * r, r


def _rms_bwd(dn, xhat, r):
    return r * (dn - xhat * jnp.mean(dn * xhat, axis=-1, keepdims=True))


def norm_pre(x, g):
    s = x.shape[0]
    tr = 512

    def body(x_ref, g_ref, h_ref):
        xhat, _ = _rms(x_ref[...])
        h_ref[...] = (xhat * g_ref[...]).astype(BF16)

    return pl.pallas_call(
        body, name="norm_pre", grid=(s // tr,),
        in_specs=[_rows(tr, D), _full((1, D))], out_specs=_rows(tr, D),
        out_shape=jax.ShapeDtypeStruct((s, D), BF16), compiler_params=_params("parallel"),
    )(x, g)


def mm_in(h, wg, gathering):
    s = h.shape[0]
    tm, tn = 1024, IN_SHARD // 2
    per = IN_SHARD // tn
    n = len(gathering)
    ni, nj = s // tm, IN_COLS // tn

    def body(*refs):
        a_ref, b_ref, o_ref = refs[0], refs[1], refs[2 + n]
        i, j = pl.program_id(0), pl.program_id(1)
        if n:
            send, pass_on, finish = _gather_phases(refs[3 + n:3 + 2 * n], *refs[3 + 2 * n:])
            pl.when((i == 0) & (j == 0))(send)
            pl.when((i == ni - 1) & (j == nj // 2))(pass_on)
        o_ref[...] = _dot(a_ref[...], b_ref[...])
        if n:
            pl.when((i == ni - 1) & (j == nj - 1))(finish)

    out = pl.pallas_call(
        body, name="mm_in", grid=(ni, nj),
        in_specs=[pl.BlockSpec((tm, D), lambda i, j: (i, 0)),
                  pl.BlockSpec((None, D, tn), lambda i, j: (j // per, 0, j % per))] + [ANY] * n,
        out_specs=[pl.BlockSpec((tm, tn), lambda i, j: (i, j))] + [ANY] * n,
        out_shape=[jax.ShapeDtypeStruct((s, IN_COLS), F32)]
        + [jax.ShapeDtypeStruct(a.shape, a.dtype) for a in gathering],
        input_output_aliases={2 + w: 1 + w for w in range(n)},
        scratch_shapes=_gather_sems(n) if n else [],
        compiler_params=_params("arbitrary", "arbitrary", communicates=bool(n)),
    )(h, wg, *gathering)
    return out[0], out[1:]


def _tril_ws(ws_ref, g):
    r = lax.broadcasted_iota(jnp.int32, (CHUNK, CHUNK), 0)
    c = lax.broadcasted_iota(jnp.int32, (CHUNK, CHUNK), 1)
    return jnp.where(c <= r, ws_ref[g], 0.0).astype(BF16)


def _layer_norm(v):
    mu = jnp.mean(v, axis=-1, keepdims=True)
    d = v - mu
    rstd = lax.rsqrt(jnp.mean(d * d, axis=-1, keepdims=True) + EPS)
    return d * rstd, rstd


def gating_fwd(z, ln_g, ln_b, w_s, bs_t):
    s = z.shape[0]

    def body(u_ref, v_ref, lg_ref, lb_ref, ws_ref, bst_ref, ya_ref):
        ug = _gelu(u_ref[...])
        vhat, _ = _layer_norm(_gelu(v_ref[...]))
        vn = (vhat * lg_ref[...] + lb_ref[...]).astype(BF16)
        for g in range(GROUPS):
            cols = slice(g * CHUNK, (g + 1) * CHUNK)
            mixed = _dot(_tril_ws(ws_ref, g), vn[:, cols]) + bst_ref[:, g:g + 1]
            ya_ref[:, cols] = (ug[:, cols] * mixed).astype(BF16)

    return pl.pallas_call(
        body, name="gating_fwd", grid=(s // CHUNK,),
        in_specs=[_rows(CHUNK, D, 0), _rows(CHUNK, D, 1), _full((1, D)), _full((1, D)),
                  _full((GROUPS, CHUNK, CHUNK)), _full((CHUNK, GROUPS))],
        out_specs=_rows(CHUNK, D), out_shape=jax.ShapeDtypeStruct((s, D), BF16),
        compiler_params=_params("parallel"),
    )(z, z, ln_g, ln_b, w_s, bs_t)


def _attn_tables(s):
    nd = s // ATT_T
    r = np.arange(ATT_T)[None, :, None]
    c = np.arange(ATT_T)[None, None, :]
    delta = np.arange(nd)[:, None, None] * ATT_T + r - c
    count = np.zeros(delta.shape, np.int64)
    for window, dilation in ((128, 1), (512, 4), (2048, 16)):
        count += (delta >= 0) & (delta % dilation == 0) & (delta <= window)
    logc = np.where(count > 0, np.log(np.maximum(count, 1)), MASKED)
    return jnp.asarray(logc, F32)


AUG = 3


def _split3_np(x):
    terms, rest = [], np.asarray(x, np.float64)
    for _ in range(AUG):
        term = np.asarray(rest.astype(jnp.bfloat16), np.float64)
        terms.append(term)
        rest = rest - term
    return terms


def _split3(x):
    terms, rest = [], x
    for _ in range(AUG):
        term = rest.astype(BF16).astype(F32)
        terms.append(term)
        rest = rest - term
    return terms


def _alibi_tables(s):
    nb = s // ATT_T
    slopes = np.exp2(-8.0 * np.arange(1, HEADS + 1, dtype=np.float64) / HEADS)
    ka = np.zeros((HEADS // 2, 2, ATT_T, 128), np.float32)
    kb = np.zeros((HEADS // 2, 2, nb, 128), np.float32)
    for p in range(HEADS // 2):
        for e in range(2):
            base = HEAD_DIM * (1 - e)
            for a, term in enumerate(_split3_np(slopes[2 * p + e] * np.arange(ATT_T))):
                ka[p, e, :, base + a] = term
            for a, term in enumerate(_split3_np(slopes[2 * p + e] * ATT_T * np.arange(nb))):
                kb[p, e, :, base + AUG + a] = term
            ka[p, e, :, base + 2 * AUG:base + 3 * AUG] = 1.0
    return jnp.asarray(ka), jnp.asarray(kb)


def _head_masks():
    lane = lax.broadcasted_iota(jnp.int32, (1, 128), 1)
    first = lane < HEAD_DIM

    def ones(e, n):
        base = HEAD_DIM * (1 - e)
        return ((lane >= base) & (lane < base + n)).astype(F32)

    return first, lane, ones


def _place3(lane, at, terms, other):
    for a, term in enumerate(terms):
        other = jnp.where(lane == at + a, term, other)
    return other


def attn_fwd(z, logc, ka, kb, gathering):
    s = z.shape[0]
    nq = s // ATT_T
    t = ATT_T
    n = len(gathering)
    grp = ATT_GROUP
    ngrp = HEADS // 2 // grp
    wide = 128 * grp
    qcol, kcol, vcol = 2 * D // wide, 3 * D // wide, 4 * D // wide

    def body(*refs):
        q_ref, k_ref, v_ref, lc_ref, ka_ref, kb_ref = refs[:6]
        y_ref, lse_ref = refs[6 + n:8 + n]
        q_s, k_s, v_s, m_s, l_s, acc_s = refs[8 + 2 * n:14 + 2 * n]
        gi, qi = pl.program_id(0), pl.program_id(1)
        first, lane, ones = _head_masks()
        if n:
            send, pass_on, finish = _gather_phases(refs[8 + n:8 + 2 * n], *refs[14 + 2 * n:])
            pl.when((gi == 0) & (qi == 0))(send)
            pl.when((gi == ngrp - 1) & (qi == 0))(pass_on)

        @pl.when(qi == 0)
        def _():
            sel = jnp.broadcast_to(first.astype(F32), (t, 128))
            for pr in range(grp):
                cols = slice(pr * 128, (pr + 1) * 128)
                for jb in range(nq):
                    kj = k_ref[jb * t:(jb + 1) * t, cols]
                    vj = v_ref[jb * t:(jb + 1) * t, cols]
                    k_s[pr, 0, jb] = jnp.where(first, kj, ka_ref[pr, 0] + kb_ref[pr, 0, jb:jb + 1, :]).astype(BF16)
                    k_s[pr, 1, jb] = jnp.where(first, ka_ref[pr, 1] + kb_ref[pr, 1, jb:jb + 1, :], kj).astype(BF16)
                    v_s[pr, jb, 0:t, 0:128] = jnp.where(first, vj, 0.0).astype(BF16)
                    v_s[pr, jb, t:2 * t, 0:128] = jnp.where(first, 0.0, vj).astype(BF16)
                    v_s[pr, jb, 0:t, 128:256] = sel.astype(BF16)
                    v_s[pr, jb, t:2 * t, 128:256] = (1.0 - sel).astype(BF16)

        for pr in range(grp):
            q = q_ref[:, pr * 128:(pr + 1) * 128] * (1.0 / math.sqrt(HEAD_DIM))
            q_s[pr, 0] = jnp.where(first, q, ones(0, 2 * AUG)).astype(BF16)
            q_s[pr, 1] = jnp.where(first, ones(1, 2 * AUG), q).astype(BF16)
        m_s[...] = jnp.full_like(m_s, MASKED)
        l_s[...] = jnp.zeros_like(l_s)
        acc_s[...] = jnp.zeros_like(acc_s)

        def scores(j):
            return tuple(_dot(q_s[pr, e], k_s[pr, e, j], NT) for pr in range(grp) for e in range(2))

        def step(j, carry):
            softmax_block(j, scores(j))
            return carry

        def softmax_block(j, u):
            lc = lc_ref[qi - j]
            for pr in range(grp):
                u0 = u[2 * pr] + lc
                u1 = u[2 * pr + 1] + lc
                m0, m1 = m_s[pr, 0], m_s[pr, 1]
                n0 = jnp.maximum(m0, jnp.max(u0, axis=-1, keepdims=True))
                n1 = jnp.maximum(m1, jnp.max(u1, axis=-1, keepdims=True))
                m_s[pr, 0], m_s[pr, 1] = n0, n1
                p = jnp.concatenate([jnp.exp(u0 - jnp.concatenate([n0, n0], axis=1)).astype(BF16),
                                     jnp.exp(u1 - jnp.concatenate([n1, n1], axis=1)).astype(BF16)], axis=1)
                pv = _dot(p, v_s[pr, j])
                alpha = jnp.where(first, jnp.exp(m0 - n0), jnp.exp(m1 - n1))
                acc_s[pr] = acc_s[pr] * alpha + pv[:, 0:128]
                l_s[pr] = l_s[pr] * alpha + pv[:, 128:256]

        lax.fori_loop(0, qi + 1, step, 0)
        for pr in range(grp):
            cols = slice(pr * 128, (pr + 1) * 128)
            y_ref[:, cols] = (acc_s[pr] / l_s[pr]).astype(BF16)
            lse_ref[:, cols] = jnp.where(first, m_s[pr, 0], m_s[pr, 1]) + jnp.log(l_s[pr])
        if n:
            pl.when((gi == ngrp - 1) & (qi == nq - 1))(finish)

    out = pl.pallas_call(
        body, name="attn_fwd", grid=(ngrp, nq),
        in_specs=[pl.BlockSpec((t, wide), lambda g, i: (i, qcol + g)),
                  pl.BlockSpec((s, wide), lambda g, i: (0, kcol + g)),
                  pl.BlockSpec((s, wide), lambda g, i: (0, vcol + g)),
                  _full((nq, t, t)),
                  pl.BlockSpec((grp, 2, t, 128), lambda g, i: (g, 0, 0, 0)),
                  pl.BlockSpec((grp, 2, nq, 128), lambda g, i: (g, 0, 0, 0))] + [ANY] * n,
        out_specs=[pl.BlockSpec((t, wide), lambda g, i: (i, g)), pl.BlockSpec((t, wide), lambda g, i: (i, g))]
        + [ANY] * n,
        out_shape=[jax.ShapeDtypeStruct((s, D), BF16), jax.ShapeDtypeStruct((s, D), F32)]
        + [jax.ShapeDtypeStruct(a.shape, a.dtype) for a in gathering],
        input_output_aliases={6 + w: 2 + w for w in range(n)},
        scratch_shapes=[pltpu.VMEM((grp, 2, t, 128), BF16), pltpu.VMEM((grp, 2, nq, t, 128), BF16),
                        pltpu.VMEM((grp, nq, 2 * t, 256), BF16), pltpu.VMEM((grp, 2, t, 128), F32),
                        pltpu.VMEM((grp, t, 128), F32), pltpu.VMEM((grp, t, 128), F32)]
        + (_gather_sems(n) if n else []),
        compiler_params=_params("arbitrary", "arbitrary", communicates=bool(n)),
    )(z, z, z, logc, ka, kb, *gathering)
    return out[0], out[1], out[2:]


def proj_merge(ya, yb, wa, wb, z, bg):
    s = ya.shape[0]
    tm = 512

    def body(ya_ref, yb_ref, wa_ref, wb_ref, ga_ref, gb_ref, bg_ref, mg_ref, pa_ref, pb_ref):
        pa = _dot(ya_ref[...], wa_ref[...])
        pb = _dot(yb_ref[...], wb_ref[...])
        sa = _sigmoid(ga_ref[...] + bg_ref[0:1, :])
        sb = _sigmoid(gb_ref[...] + bg_ref[1:2, :])
        mg_ref[...] = (sa * pa + sb * pb).astype(BF16)
        pa_ref[...] = pa.astype(BF16)
        pb_ref[...] = pb.astype(BF16)

    out = jax.ShapeDtypeStruct((s, D), BF16)
    return pl.pallas_call(
        body, name="proj_merge", grid=(s // tm,),
        in_specs=[_rows(tm, D), _rows(tm, D), _full((D, D)), _full((D, D)),
                  _rows(tm, D, 5), _rows(tm, D, 6), _full((2, D))],
        out_specs=[_rows(tm, D)] * 3, out_shape=[out] * 3, compiler_params=_params("parallel"),
    )(ya, yb, wa, wb, z, z, bg)


def out_norm(merged, w_out, x, g_post, g_fpre):
    s = x.shape[0]
    tm = 512

    def body(mg_ref, w_ref, x_ref, gp_ref, gf_ref, o_ref, x1_ref, h2_ref):
        o = _dot(mg_ref[...], w_ref[...])
        ohat, _ = _rms(o)
        x1 = x_ref[...] + ohat * gp_ref[...]
        x1hat, _ = _rms(x1)
        o_ref[...] = o
        x1_ref[...] = x1
        h2_ref[...] = (x1hat * gf_ref[...]).astype(BF16)

    return pl.pallas_call(
        body, name="out_norm", grid=(s // tm,),
        in_specs=[_rows(tm, D), _full((D, D)), _rows(tm, D), _full((1, D)), _full((1, D))],
        out_specs=[_rows(tm, D)] * 3,
        out_shape=[jax.ShapeDtypeStruct((s, D), F32), jax.ShapeDtypeStruct((s, D), F32),
                   jax.ShapeDtypeStruct((s, D), BF16)],
        compiler_params=_params("parallel"),
    )(merged, w_out, x, g_post, g_fpre)


def mm_ff1(h2, wg):
    s = h2.shape[0]
    tm = 1024

    def body(a_ref, b_ref, o_ref, r_ref):
        a = _dot(a_ref[...], b_ref[...])
        o_ref[...] = a
        r = jnp.maximum(a, 0.0)
        r_ref[...] = (r * r).astype(BF16)

    return pl.pallas_call(
        body, name="mm_ff1", grid=(s // tm, N_CHIPS),
        in_specs=[pl.BlockSpec((tm, D), lambda i, j: (i, 0)), pl.BlockSpec((None, D, D), lambda i, j: (j, 0, 0))],
        out_specs=[pl.BlockSpec((tm, D), lambda i, j: (i, j))] * 2,
        out_shape=[jax.ShapeDtypeStruct((s, D_FF), F32), jax.ShapeDtypeStruct((s, D_FF), BF16)],
        compiler_params=_params("parallel", "parallel"),
    )(h2, wg)


def ff2_loss(rl, w_ff2, x1, target, g_fpost):
    s = x1.shape[0]
    tm = 256

    def body(rl_ref, w_ref, x1_ref, t_ref, g_ref, dy_ref, df_ref, dg_ref, loss_ref):
        @pl.when(pl.program_id(0) == 0)
        def _():
            dg_ref[...] = jnp.zeros_like(dg_ref)
            loss_ref[...] = jnp.zeros_like(loss_ref)

        f = _dot(rl_ref[...], w_ref[...])
        fhat, r = _rms(f)
        err = x1_ref[...] + fhat * g_ref[...] - t_ref[...]
        loss_ref[...] += 0.5 * jnp.sum(jnp.mean(err * err, axis=-1, keepdims=True), axis=0, keepdims=True)
        dy = err * (1.0 / D)
        dy_ref[...] = dy
        dg_ref[...] += jnp.sum(dy * fhat, axis=0, keepdims=True)
        df_ref[...] = _rms_bwd(dy * g_ref[...], fhat, r).astype(BF16)

    return pl.pallas_call(
        body, name="ff2_loss", grid=(s // tm,),
        in_specs=[_rows(tm, D_FF), _full((D_FF, D)), _rows(tm, D), _rows(tm, D), _full((1, D))],
        out_specs=[_rows(tm, D), _rows(tm, D), _full((1, D)), _full((1, 1))],
        out_shape=[jax.ShapeDtypeStruct((s, D), F32), jax.ShapeDtypeStruct((s, D), BF16),
                   jax.ShapeDtypeStruct((1, D), F32), jax.ShapeDtypeStruct((1, 1), F32)],
        compiler_params=_params("arbitrary"),
    )(rl, w_ff2, x1, target, g_fpost)


def mm_tn(name, a, b, ta, tb, out_shape, out_spec):
    s = a.shape[0]

    def body(a_ref, b_ref, o_ref):
        o_ref[...] = _dot(a_ref[...], b_ref[...], TN)

    return pl.pallas_call(
        body, name=name, grid=(a.shape[1] // ta, b.shape[1] // tb),
        in_specs=[pl.BlockSpec((s, ta), lambda i, j: (0, i)), pl.BlockSpec((s, tb), lambda i, j: (0, j))],
        out_specs=out_spec, out_shape=jax.ShapeDtypeStruct(out_shape, F32),
        compiler_params=_params("parallel", "parallel"),
    )(a, b)


def mm_nt(name, a, w):
    s = a.shape[0]
    tm = 512

    def body(a_ref, w_ref, o_ref):
        o_ref[...] = _dot(a_ref[...], w_ref[...], NT)

    return pl.pallas_call(
        body, name=name, grid=(s // tm,), in_specs=[_rows(tm, D), _full((D, D))], out_specs=_rows(tm, D),
        out_shape=jax.ShapeDtypeStruct((s, D), F32), compiler_params=_params("parallel"),
    )(a, w)


def ff2_bwd(df, w_ff2, a):
    s = df.shape[0]
    tm = 1024

    def body(df_ref, w_ref, a_ref, da_ref):
        drl = _dot(df_ref[...], w_ref[...], NT)
        da_ref[...] = (drl * (2.0 * jnp.maximum(a_ref[...], 0.0))).astype(BF16)

    return pl.pallas_call(
        body, name="ff2_bwd", grid=(s // tm, D_FF // D),
        in_specs=[pl.BlockSpec((tm, D), lambda i, j: (i, 0)), pl.BlockSpec((D, D), lambda i, j: (j, 0)),
                  pl.BlockSpec((tm, D), lambda i, j: (i, j))],
        out_specs=pl.BlockSpec((tm, D), lambda i, j: (i, j)),
        out_shape=jax.ShapeDtypeStruct((s, D_FF), BF16), compiler_params=_params("parallel", "parallel"),
    )(df, w_ff2, a)


def ff1_bwd_norms(da, wg, x1, o, dy, g_fpre, g_post, swapping):
    s = x1.shape[0]
    tm = 512
    n = len(swapping)

    def body(*refs):
        da_ref, w_ref, x1_ref, o_ref, dy_ref, gf_ref, gp_ref = refs[:7]
        dx1_ref, do_ref, dgf_ref, dgp_ref = refs[7 + n:11 + n]
        acc_ref = refs[11 + 2 * n]
        i, k = pl.program_id(0), pl.program_id(1)
        if n:
            send, finish = _swap_phases(refs[7:7 + n], refs[11 + n:11 + 2 * n], *refs[12 + 2 * n:])
            pl.when((i == 0) & (k == 0))(send)

        @pl.when((i == 0) & (k == 0))
        def _():
            dgf_ref[...] = jnp.zeros_like(dgf_ref)
            dgp_ref[...] = jnp.zeros_like(dgp_ref)

        part = _dot(da_ref[...], w_ref[...], NT)

        @pl.when(k == 0)
        def _():
            acc_ref[...] = part

        @pl.when(k > 0)
        def _():
            acc_ref[...] += part

        @pl.when(k == N_CHIPS - 1)
        def _():
            dh2 = acc_ref[...]
            x1hat, r2 = _rms(x1_ref[...])
            dgf_ref[...] += jnp.sum(dh2 * x1hat, axis=0, keepdims=True)
            dx1 = dy_ref[...] + _rms_bwd(dh2 * gf_ref[...], x1hat, r2)
            ohat, r1 = _rms(o_ref[...])
            dgp_ref[...] += jnp.sum(dx1 * ohat, axis=0, keepdims=True)
            dx1_ref[...] = dx1
            do_ref[...] = _rms_bwd(dx1 * gp_ref[...], ohat, r1).astype(BF16)

        if n:
            pl.when((i == s // tm - 1) & (k == N_CHIPS - 1))(finish)

    row = pl.BlockSpec((tm, D), lambda i, k: (i, 0))
    vec = pl.BlockSpec((1, D), lambda i, k: (0, 0))
    res = pl.pallas_call(
        body, name="ff1_bwd_norms", grid=(s // tm, N_CHIPS),
        in_specs=[pl.BlockSpec((tm, D), lambda i, k: (i, k)), pl.BlockSpec((None, D, D), lambda i, k: (k, 0, 0)),
                  row, row, row, vec, vec] + [ANY] * n,
        out_specs=[row, row, vec, vec] + [ANY] * n,
        out_shape=[jax.ShapeDtypeStruct((s, D), F32), jax.ShapeDtypeStruct((s, D), BF16),
                   jax.ShapeDtypeStruct((1, D), F32), jax.ShapeDtypeStruct((1, D), F32)] + _swap_shapes(swapping),
        scratch_shapes=[pltpu.VMEM((tm, D), F32)] + (_swap_sems(n) if n else []),
        compiler_params=_params("arbitrary", "arbitrary", communicates=bool(n)),
    )(da, wg, x1, o, dy, g_fpre, g_post, *swapping)
    return res[0], res[1], res[2], res[3], res[4:]


def out_bwd_gates(do, w_out, pa, pb, z, bg):
    s = do.shape[0]
    tm = 512

    def body(do_ref, w_ref, pa_ref, pb_ref, ga_ref, gb_ref, bg_ref, dpa_ref, dpb_ref, dga_ref, dgb_ref, dbg_ref):
        @pl.when(pl.program_id(0) == 0)
        def _():
            dbg_ref[...] = jnp.zeros_like(dbg_ref)

        dm = _dot(do_ref[...], w_ref[...], NT)
        sa = _sigmoid(ga_ref[...] + bg_ref[0:1, :])
        sb = _sigmoid(gb_ref[...] + bg_ref[1:2, :])
        dpa_ref[...] = (dm * sa).astype(BF16)
        dpb_ref[...] = (dm * sb).astype(BF16)
        dga = dm * pa_ref[...].astype(F32) * (sa * (1.0 - sa))
        dgb = dm * pb_ref[...].astype(F32) * (sb * (1.0 - sb))
        dga_ref[...] = dga.astype(BF16)
        dgb_ref[...] = dgb.astype(BF16)
        dbg_ref[0:1, :] += jnp.sum(dga, axis=0, keepdims=True)
        dbg_ref[1:2, :] += jnp.sum(dgb, axis=0, keepdims=True)

    out = jax.ShapeDtypeStruct((s, D), BF16)
    return pl.pallas_call(
        body, name="out_bwd_gates", grid=(s // tm,),
        in_specs=[_rows(tm, D), _full((D, D)), _rows(tm, D), _rows(tm, D), _rows(tm, D, 5), _rows(tm, D, 6),
                  _full((2, D))],
        out_specs=[_rows(tm, D)] * 4 + [_full((2, D))],
        out_shape=[out] * 4 + [jax.ShapeDtypeStruct((2, D), F32)], compiler_params=_params("arbitrary"),
    )(do, w_out, pa, pb, z, z, bg)


def gating_bwd(z, dya, ln_g, ln_b, w_s, bs_t, swapping):
    s = z.shape[0]
    ones = functools.partial(jnp.ones, (8, CHUNK), BF16)
    n = len(swapping)

    def body(*refs):
        u_ref, v_ref, dya_ref, lg_ref, lb_ref, ws_ref, bst_ref = refs[:7]
        du_ref, dv_ref, dws_ref, dbs_ref, dlg_ref, dlb_ref = refs[7 + n:13 + n]
        dvn_ref = refs[13 + 2 * n]
        ci = pl.program_id(0)
        if n:
            send, finish = _swap_phases(refs[7:7 + n], refs[13 + n:13 + 2 * n], *refs[14 + 2 * n:])
            pl.when(ci == 0)(send)

        @pl.when(ci == 0)
        def _():
            dws_ref[...] = jnp.zeros_like(dws_ref)
            dbs_ref[...] = jnp.zeros_like(dbs_ref)
            dlg_ref[...] = jnp.zeros_like(dlg_ref)
            dlb_ref[...] = jnp.zeros_like(dlb_ref)

        ug, dug_du = _gelu_and_grad(u_ref[...])
        vg, dvg_dv = _gelu_and_grad(v_ref[...])
        vhat, rstd = _layer_norm(vg)
        vn = (vhat * lg_ref[...] + lb_ref[...]).astype(BF16)
        dya = dya_ref[...]
        for g in range(GROUPS):
            cols = slice(g * CHUNK, (g + 1) * CHUNK)
            ws = _tril_ws(ws_ref, g)
            mixed = _dot(ws, vn[:, cols]) + bst_ref[:, g:g + 1]
            du_ref[:, cols] = (dya[:, cols] * mixed * dug_du[:, cols]).astype(BF16)
            dmix = (dya[:, cols] * ug[:, cols]).astype(BF16)
            dbs_ref[g] += _dot(ones(), dmix, NT)
            dws_ref[g] += _dot(dmix, vn[:, cols], NT)
            dvn_ref[:, cols] = _dot(ws, dmix, TN)
        dvn = dvn_ref[...]
        dlg_ref[...] += jnp.sum(dvn * vhat, axis=0, keepdims=True)
        dlb_ref[...] += jnp.sum(dvn, axis=0, keepdims=True)
        dvh = dvn * lg_ref[...]
        dvg = rstd * (dvh - jnp.mean(dvh, axis=-1, keepdims=True)
                      - vhat * jnp.mean(dvh * vhat, axis=-1, keepdims=True))
        dv_ref[...] = (dvg * dvg_dv).astype(BF16)

        @pl.when(ci == pl.num_programs(0) - 1)
        def _():
            r = lax.broadcasted_iota(jnp.int32, (CHUNK, CHUNK), 0)
            c = lax.broadcasted_iota(jnp.int32, (CHUNK, CHUNK), 1)
            for g in range(GROUPS):
                dws_ref[g] = jnp.where(c <= r, dws_ref[g], 0.0)

        if n:
            pl.when(ci == pl.num_programs(0) - 1)(finish)

    out = jax.ShapeDtypeStruct((s, D), BF16)
    res = pl.pallas_call(
        body, name="gating_bwd", grid=(s // CHUNK,),
        in_specs=[_rows(CHUNK, D, 0), _rows(CHUNK, D, 1), _rows(CHUNK, D), _full((1, D)), _full((1, D)),
                  _full((GROUPS, CHUNK, CHUNK)), _full((CHUNK, GROUPS))] + [ANY] * n,
        out_specs=[_rows(CHUNK, D), _rows(CHUNK, D), _full((GROUPS, CHUNK, CHUNK)), _full((GROUPS, 8, CHUNK)),
                   _full((1, D)), _full((1, D))] + [ANY] * n,
        out_shape=[out, out, jax.ShapeDtypeStruct((GROUPS, CHUNK, CHUNK), F32),
                   jax.ShapeDtypeStruct((GROUPS, 8, CHUNK), F32),
                   jax.ShapeDtypeStruct((1, D), F32), jax.ShapeDtypeStruct((1, D), F32)] + _swap_shapes(swapping),
        scratch_shapes=[pltpu.VMEM((CHUNK, D), F32)] + (_swap_sems(n) if n else []),
        compiler_params=_params("arbitrary", communicates=bool(n)),
    )(z, z, dya, ln_g, ln_b, w_s, bs_t, *swapping)
    return (*res[:6], res[6:])


def attn_bwd(z, yb, dyb, lse, logc, ka, kb, scattering):
    s = z.shape[0]
    nq = s // ATT_T
    t = ATT_T
    grp = ATT_BWD_GROUP
    ngrp = HEADS // 2 // grp
    wide = 128 * grp
    qcol, kcol, vcol = 2 * D // wide, 3 * D // wide, 4 * D // wide
    scale = 1.0 / math.sqrt(HEAD_DIM)
    n = len(scattering)

    def body(*refs):
        q_ref, k_ref, v_ref, y_ref, dy_ref, lse_ref, lc_ref, ka_ref, kb_ref = refs[:9]
        dq_ref, dk_ref, dv_ref = refs[9 + n:12 + n]
        qa_s, qt_s, da_s, dt_s, dq_s, dkt_s, dvt_s = refs[12 + 2 * n:19 + 2 * n]
        gi, j = pl.program_id(0), pl.program_id(1)
        first, lane, ones = _head_masks()
        if n:
            send, finish = _scatter_phases(refs[9:9 + n], refs[12 + n:12 + 2 * n], *refs[19 + 2 * n:])
            pl.when((gi == 0) & (j == 0))(send)

        @pl.when(j == 0)
        def _():
            dq_s[...] = jnp.zeros_like(dq_s)
            for pr in range(grp):
                cols = slice(pr * 128, (pr + 1) * 128)
                for ib in range(nq):
                    rows = slice(ib * t, (ib + 1) * t)
                    q = q_ref[rows, cols] * scale
                    lse = lse_ref[rows, cols]
                    qa_s[pr, 0, ib] = jnp.where(first, q, _place3(lane, HEAD_DIM + 2 * AUG, _split3(-lse[:, 0:1]),
                                                                  ones(0, 2 * AUG))).astype(BF16)
                    qa_s[pr, 1, ib] = jnp.where(
                        first, _place3(lane, 2 * AUG, _split3(-lse[:, HEAD_DIM:HEAD_DIM + 1]), ones(1, 2 * AUG)),
                        q).astype(BF16)
                    qt_s[pr, ib, :, 0:t] = jnp.where(first, q, 0.0).T.astype(BF16)
                    qt_s[pr, ib, :, t:2 * t] = jnp.where(first, 0.0, q).T.astype(BF16)
                    do = dy_ref[rows, cols]
                    prod = do * y_ref[rows, cols].astype(F32)
                    dd0 = jnp.sum(jnp.where(first, prod, 0.0), axis=-1, keepdims=True)
                    dd1 = jnp.sum(jnp.where(first, 0.0, prod), axis=-1, keepdims=True)
                    da_s[pr, 0, ib] = jnp.where(first, do, _place3(lane, HEAD_DIM, _split3(-dd0), 0.0)).astype(BF16)
                    da_s[pr, 1, ib] = jnp.where(first, _place3(lane, 0, _split3(-dd1), 0.0), do).astype(BF16)
                    dt_s[pr, ib, :, 0:t] = jnp.where(first, do, 0.0).T.astype(BF16)
                    dt_s[pr, ib, :, t:2 * t] = jnp.where(first, 0.0, do).T.astype(BF16)

        keys = []
        for pr in range(grp):
            kj = k_ref[:, pr * 128:(pr + 1) * 128]
            vj = v_ref[:, pr * 128:(pr + 1) * 128]
            keys.append((
                jnp.where(first, kj, ka_ref[pr, 0] + kb_ref[pr, 0, pl.ds(j, 1), :]).astype(BF16),
                jnp.where(first, ka_ref[pr, 1] + kb_ref[pr, 1, pl.ds(j, 1), :], kj).astype(BF16),
                jnp.concatenate([jnp.where(first, kj, 0.0), jnp.where(first, 0.0, kj)], axis=0).astype(BF16),
                jnp.where(first, vj, ones(0, AUG)).astype(BF16),
                jnp.where(first, ones(1, AUG), vj).astype(BF16)))
        dkt_s[...] = jnp.zeros_like(dkt_s)
        dvt_s[...] = jnp.zeros_like(dvt_s)

        def step(i, _):
            lc = lc_ref[i - j]
            rows = pl.ds(pl.multiple_of(i * t, t), t)
            for pr in range(grp):
                k0a, k1a, kst, v0a, v1a = keys[pr]
                p0 = jnp.exp(_dot(qa_s[pr, 0, i], k0a, NT) + lc)
                p1 = jnp.exp(_dot(qa_s[pr, 1, i], k1a, NT) + lc)
                e0 = (p0 * _dot(da_s[pr, 0, i], v0a, NT)).astype(BF16)
                e1 = (p1 * _dot(da_s[pr, 1, i], v1a, NT)).astype(BF16)
                dq_s[pr, rows, :] += _dot(jnp.concatenate([e0, e1], axis=1), kst)
                dvt_s[pr] += _dot(dt_s[pr, i], jnp.concatenate([p0.astype(BF16), p1.astype(BF16)], axis=0))
                dkt_s[pr] += _dot(qt_s[pr, i], jnp.concatenate([e0, e1], axis=0))
            return 0

        lax.fori_loop(j, nq, step, 0)
        for pr in range(grp):
            dk_ref[:, pr * 128:(pr + 1) * 128] = dkt_s[pr].T.astype(BF16)
            dv_ref[:, pr * 128:(pr + 1) * 128] = dvt_s[pr].T.astype(BF16)

        @pl.when(j == nq - 1)
        def _():
            for pr in range(grp):
                dq_ref[:, pr * 128:(pr + 1) * 128] = (dq_s[pr] * scale).astype(BF16)

        if n:
            pl.when((gi == ngrp - 1) & (j == nq - 1))(finish)

    colblock = lambda c: pl.BlockSpec((s, wide), lambda g, j: (0, c + g))
    blk = lambda c: pl.BlockSpec((t, wide), lambda g, j: (j, c + g))
    out = jax.ShapeDtypeStruct((s, D), BF16)
    res = pl.pallas_call(
        body, name="attn_bwd", grid=(ngrp, nq),
        in_specs=[colblock(qcol), blk(kcol), blk(vcol), colblock(0), colblock(0), colblock(0),
                  _full((nq, t, t)), pl.BlockSpec((grp, 2, t, 128), lambda g, j: (g, 0, 0, 0)),
                  pl.BlockSpec((grp, 2, nq, 128), lambda g, j: (g, 0, 0, 0))] + [ANY] * n,
        out_specs=[colblock(0), blk(0), blk(0)] + [ANY] * n, out_shape=[out] * 3 + _scatter_shapes(scattering),
        scratch_shapes=[pltpu.VMEM((grp, 2, nq, t, 128), BF16), pltpu.VMEM((grp, nq, 128, 2 * t), BF16),
                        pltpu.VMEM((grp, 2, nq, t, 128), BF16), pltpu.VMEM((grp, nq, 128, 2 * t), BF16),
                        pltpu.VMEM((grp, s, 128), F32), pltpu.VMEM((grp, 128, t), F32),
                        pltpu.VMEM((grp, 128, t), F32)]
        + (_scatter_sems(n) if n else []),
        compiler_params=_params("arbitrary", "arbitrary", communicates=bool(n)),
    )(z, z, z, yb, dyb, lse, logc, ka, kb, *scattering)
    return res[0], res[1], res[2], res[3:]


def in_bwd_norm(dz, wg, x, dx1, g_pre, scattering):
    s = x.shape[0]
    tm = 512
    n = len(scattering)

    def body(*refs):
        dz_ref, w_ref, x_ref, dx1_ref, g_ref = refs[:5]
        dx_ref, dg_ref = refs[5 + n:7 + n]
        acc_ref = refs[7 + 2 * n]
        i, k = pl.program_id(0), pl.program_id(1)
        if n:
            send, finish = _scatter_phases(refs[5:5 + n], refs[7 + n:7 + 2 * n], *refs[8 + 2 * n:])
            pl.when((i == 0) & (k == 0))(send)

        @pl.when((i == 0) & (k == 0))
        def _():
            dg_ref[...] = jnp.zeros_like(dg_ref)

        part = _dot(dz_ref[...], w_ref[...], NT)

        @pl.when(k == 0)
        def _():
            acc_ref[...] = part

        @pl.when(k > 0)
        def _():
            acc_ref[...] += part

        @pl.when(k == N_CHIPS - 1)
        def _():
            dh = acc_ref[...]
            xhat, r = _rms(x_ref[...])
            dg_ref[...] += jnp.sum(dh * xhat, axis=0, keepdims=True)
            dx_ref[...] = dx1_ref[...] + _rms_bwd(dh * g_ref[...], xhat, r)

        if n:
            pl.when((i == s // tm - 1) & (k == N_CHIPS - 1))(finish)

    row = pl.BlockSpec((tm, D), lambda i, k: (i, 0))
    vec = pl.BlockSpec((1, D), lambda i, k: (0, 0))
    res = pl.pallas_call(
        body, name="in_bwd_norm", grid=(s // tm, N_CHIPS),
        in_specs=[pl.BlockSpec((tm, IN_SHARD), lambda i, k: (i, k)),
                  pl.BlockSpec((None, D, IN_SHARD), lambda i, k: (k, 0, 0)), row, row, vec] + [ANY] * n,
        out_specs=[row, vec] + [ANY] * n,
        out_shape=[jax.ShapeDtypeStruct((s, D), F32), jax.ShapeDtypeStruct((1, D), F32)]
        + _scatter_shapes(scattering),
        scratch_shapes=[pltpu.VMEM((tm, D), F32)] + (_scatter_sems(n) if n else []),
        compiler_params=_params("arbitrary", "arbitrary", communicates=bool(n)),
    )(dz, wg, x, dx1, g_pre, *scattering)
    return res[0], res[1], res[2:]


def _adamw_math(w, g, m, v):
    m = ADAM_B1 * m + (1.0 - ADAM_B1) * g
    v = ADAM_B2 * v + (1.0 - ADAM_B2) * (g * g)
    m_hat = m / (1.0 - ADAM_B1 ** ADAM_STEP)
    v_hat = v / (1.0 - ADAM_B2 ** ADAM_STEP)
    delta = -ADAM_LR * (m_hat / (jnp.sqrt(v_hat) + ADAM_EPS) + ADAM_WD * w)
    return delta, m, v


def adamw(name, w, g, m, v, tr):
    r, c = w.shape

    def body(w_ref, g_ref, m_ref, v_ref, go_ref, d_ref, nm_ref, nv_ref):
        g = g_ref[...]
        go_ref[...] = g
        d_ref[...], nm_ref[...], nv_ref[...] = _adamw_math(w_ref[...], g, m_ref[...], v_ref[...])

    out = jax.ShapeDtypeStruct((r, c), F32)
    return pl.pallas_call(
        body, name=name, grid=(r // tr,), in_specs=[_rows(tr, c)] * 4, out_specs=[_rows(tr, c)] * 4,
        out_shape=[out] * 4, compiler_params=_params("parallel"),
    )(w, g, m, v)


def add_halves(name, g, recv, c_idx, tr):
    n, h, c = recv.shape

    def body(c_ref, g_ref, r_ref, o_ref):
        o_ref[...] = (g_ref[...] + r_ref[...]).astype(BF16)

    nb = h // tr
    return pl.pallas_call(
        body, name=name,
        grid_spec=pltpu.PrefetchScalarGridSpec(
            num_scalar_prefetch=1, grid=(n, nb),
            in_specs=[pl.BlockSpec((None, tr, c), lambda k, i, c_ref: (k, c_ref[0] * nb + i, 0)),
                      pl.BlockSpec((None, tr, c), lambda k, i, c_ref: (k, i, 0))],
            out_specs=pl.BlockSpec((None, tr, c), lambda k, i, c_ref: (k, i, 0))),
        out_shape=jax.ShapeDtypeStruct((n, h, c), BF16), compiler_params=_params("parallel", "parallel"),
    )(c_idx, g, recv)


def sum_chips(name, parts, recv, where, tr):
    n, h, c = recv.shape
    nb = h // tr

    def body(w_ref, p_ref, r_ref, o_ref):
        acc = p_ref[...].astype(F32)
        for k in range(n):
            acc = acc + r_ref[k].astype(F32)
        o_ref[...] = acc

    return pl.pallas_call(
        body, name=name,
        grid_spec=pltpu.PrefetchScalarGridSpec(
            num_scalar_prefetch=1, grid=(nb,),
            in_specs=[pl.BlockSpec((None, tr, c), lambda i, w_ref: (w_ref[0], i, 0)),
                      pl.BlockSpec((n, tr, c), lambda i, w_ref: (0, i, 0))],
            out_specs=pl.BlockSpec((tr, c), lambda i, w_ref: (w_ref[1] * nb + i, 0))),
        out_shape=jax.ShapeDtypeStruct((2 * h, c), F32), compiler_params=_params("parallel"),
    )(where, parts, recv)


def place_shard(name, shard, where, dtype, tr):
    r, c = shard.shape

    def body(w_ref, s_ref, o_ref):
        o_ref[...] = s_ref[...].astype(dtype)

    return pl.pallas_call(
        body, name=name,
        grid_spec=pltpu.PrefetchScalarGridSpec(
            num_scalar_prefetch=1, grid=(r // tr,),
            in_specs=[pl.BlockSpec((tr, c), lambda i, w_ref: (i, 0))],
            out_specs=pl.BlockSpec((None, tr, c), lambda i, w_ref: (w_ref[0], i, 0))),
        out_shape=jax.ShapeDtypeStruct((N_CHIPS, r, c), dtype), compiler_params=_params("parallel"),
    )(where, shard)


ANY = pl.BlockSpec(memory_space=pl.ANY)


def _place():
    x, y, c = lax.axis_index("x"), lax.axis_index("y"), lax.axis_index("c")
    chips = [(1 - x, y), (x, 1 - y), (1 - x, 1 - y)]
    return x, y, c, chips


def gather_shards(arrays):
    n = len(arrays)

    def body(*refs):
        send, pass_on, finish = _gather_phases(refs[n:2 * n], *refs[2 * n:])
        send()
        pass_on()
        finish()

    return pl.pallas_call(
        body, name="gather_shards", in_specs=[ANY] * n, out_specs=[ANY] * n,
        out_shape=[jax.ShapeDtypeStruct(a.shape, a.dtype) for a in arrays],
        input_output_aliases={w: w for w in range(n)}, scratch_shapes=_gather_sems(n),
        compiler_params=pltpu.CompilerParams(has_side_effects=True),
    )(*arrays)


def _gather_sems(n):
    return [pltpu.SemaphoreType.DMA((6 * n,)), pltpu.SemaphoreType.DMA((6 * n,))]


def _gather_phases(out, send_sems, recv_sems):
    n = len(out)
    x, y, c, chips = _place()
    me = 2 * x + y
    sibling = (x, y, 1 - c)

    def half(w, chip, core):
        h = out[w].shape[1] // 2
        return out[w].at[chip, pl.ds(core * h, h)]

    def copy(k, block, to):
        return pltpu.make_async_remote_copy(src_ref=block, dst_ref=block, send_sem=send_sems.at[k],
                                            recv_sem=recv_sems.at[k], device_id=to, device_id_type=MESH)

    def over_ici(w, j, chip):
        return copy(3 * w + j, half(w, chip, c), (chips[j][0], chips[j][1], c))

    def over_d2d(w, j, core):
        return copy(3 * n + 3 * w + j, half(w, 2 * chips[j][0] + chips[j][1], core), sibling)

    pairs = [(w, j) for w in range(n) for j in range(3)]

    def send():
        for w, j in pairs:
            over_ici(w, j, me).start()

    def pass_on():
        for w, j in pairs:
            over_ici(w, j, 2 * chips[j][0] + chips[j][1]).wait_recv()
            over_d2d(w, j, c).start()

    def finish():
        for w, j in pairs:
            over_d2d(w, j, 1 - c).wait_recv()
        for w, j in pairs:
            over_ici(w, j, me).wait_send()
            over_d2d(w, j, c).wait_send()

    return send, pass_on, finish


def swap_halves(name, grads):
    n = len(grads)

    def body(*refs):
        send, finish = _swap_phases(refs[:n], refs[n:2 * n], *refs[2 * n:])
        send()
        finish()

    return pl.pallas_call(
        body, name=name, in_specs=[ANY] * n, out_specs=[ANY] * n, out_shape=_swap_shapes(grads),
        scratch_shapes=_swap_sems(n), compiler_params=pltpu.CompilerParams(has_side_effects=True),
    )(*grads)


def _swap_shapes(grads):
    return [jax.ShapeDtypeStruct((a.shape[0], a.shape[1] // 2, a.shape[2]), a.dtype) for a in grads]


def _swap_sems(n):
    return [pltpu.SemaphoreType.DMA((n,)), pltpu.SemaphoreType.DMA((n,))]


def _swap_phases(g, out, send_sems, recv_sems):
    x, y, c, _ = _place()

    def copies():
        return [pltpu.make_async_remote_copy(
            src_ref=g[w].at[:, pl.ds((1 - c) * (g[w].shape[1] // 2), g[w].shape[1] // 2)], dst_ref=out[w],
            send_sem=send_sems.at[w], recv_sem=recv_sems.at[w], device_id=(x, y, 1 - c), device_id_type=MESH)
            for w in range(len(g))]

    def send():
        for cp in copies():
            cp.start()

    def finish():
        for cp in copies():
            cp.wait()

    return send, finish


def scatter_chips(parts):
    n = len(parts)

    def body(*refs):
        send, finish = _scatter_phases(refs[:n], refs[n:2 * n], *refs[2 * n:])
        send()
        finish()

    return pl.pallas_call(
        body, name="scatter_chips", in_specs=[ANY] * n, out_specs=[ANY] * n,
        out_shape=_scatter_shapes(parts), scratch_shapes=_scatter_sems(n),
        compiler_params=pltpu.CompilerParams(has_side_effects=True),
    )(*parts)


def _scatter_shapes(parts):
    return [jax.ShapeDtypeStruct((3,) + a.shape[1:], a.dtype) for a in parts]


def _scatter_sems(n):
    return [pltpu.SemaphoreType.DMA((3 * n,)), pltpu.SemaphoreType.DMA((3 * n,))]


def _scatter_phases(p, out, send_sems, recv_sems):
    x, y, c, chips = _place()

    def copies():
        return [pltpu.make_async_remote_copy(
            src_ref=p[w].at[2 * px + py], dst_ref=out[w].at[j], send_sem=send_sems.at[3 * w + j],
            recv_sem=recv_sems.at[3 * w + j], device_id=(px, py, c), device_id_type=MESH)
            for w in range(len(p)) for j, (px, py) in enumerate(chips)]

    def send():
        for cp in copies():
            cp.start()

    def finish():
        for cp in copies():
            cp.wait()

    return send, finish


def join_halves(arrays):
    n = len(arrays)

    def body(*refs):
        out = refs[n:2 * n]
        send_sems, recv_sems = refs[2 * n:]
        x, y, c, _ = _place()

        def copy(w, core):
            h = out[w].shape[0] // 2
            rows = out[w].at[pl.ds(core * h, h)]
            return pltpu.make_async_remote_copy(
                src_ref=rows, dst_ref=rows, send_sem=send_sems.at[w], recv_sem=recv_sems.at[w],
                device_id=(x, y, 1 - c), device_id_type=MESH)

        for w in range(n):
            copy(w, c).start()
        for w in range(n):
            copy(w, 1 - c).wait_recv()
        for w in range(n):
            copy(w, c).wait_send()

    return pl.pallas_call(
        body, name="join_halves", in_specs=[ANY] * n, out_specs=[ANY] * n,
        out_shape=[jax.ShapeDtypeStruct(a.shape, a.dtype) for a in arrays],
        input_output_aliases={w: w for w in range(n)},
        scratch_shapes=[pltpu.SemaphoreType.DMA((n,)), pltpu.SemaphoreType.DMA((n,))],
        compiler_params=pltpu.CompilerParams(has_side_effects=True),
    )(*arrays)


def allreduce_small(packed):
    r, c = packed.shape
    n_dev = 8

    def body(x_ref, all_ref, sum_ref, send_sems, recv_sems, local_sem):
        x, y, cc, chips = _place()
        me, sibling = (x, y, cc), (x, y, 1 - cc)

        def rows(px, py, pc):
            return all_ref.at[4 * px + 2 * py + pc]

        def copy(k, block, to, src=None):
            return pltpu.make_async_remote_copy(
                src_ref=rows(*block) if src is None else src, dst_ref=rows(*block), send_sem=send_sems.at[k],
                recv_sem=recv_sems.at[k], device_id=to, device_id_type=MESH)

        mine = pltpu.make_async_copy(x_ref, rows(*me), local_sem)
        mine.start()
        first = [copy(0, me, sibling, src=x_ref)]
        first += [copy(1 + j, me, (*chip, cc), src=x_ref) for j, chip in enumerate(chips)]
        for cp in first:
            cp.start()
        passed = [copy(4 + j, (*chip, cc), sibling) for j, chip in enumerate(chips)]
        for j, chip in enumerate(chips):
            copy(1 + j, (*chip, cc), me).wait_recv()
            passed[j].start()
        copy(0, sibling, me).wait_recv()
        for j, chip in enumerate(chips):
            copy(4 + j, (*chip, 1 - cc), me).wait_recv()
        for cp in first + passed:
            cp.wait_send()
        mine.wait()
        acc = all_ref[0]
        for k in range(1, n_dev):
            acc = acc + all_ref[k]
        sum_ref[...] = acc

    vm = pl.BlockSpec(memory_space=pltpu.VMEM)
    return pl.pallas_call(
        body, name="allreduce_small", in_specs=[vm], out_specs=[vm, vm],
        out_shape=[jax.ShapeDtypeStruct((n_dev, r, c), F32), jax.ShapeDtypeStruct((r, c), F32)],
        scratch_shapes=[pltpu.SemaphoreType.DMA((7,)), pltpu.SemaphoreType.DMA((7,)), pltpu.SemaphoreType.DMA],
        compiler_params=pltpu.CompilerParams(has_side_effects=True, vmem_limit_bytes=VMEM_LIMIT),
    )(packed)[1]


def local_step(x, target, vecs, w_s, bs_t, bg, wg_in, late, core=None):
    on_mesh = core is not None

    def add(names, grads, recv):
        return [add_halves("add_" + n, g, r, core, min(r.shape[1], 256)) for n, g, r in zip(names, grads, recv)]

    g_pre, ln_g, ln_b, g_post, g_fpre, g_fpost = vecs
    s = x.shape[0]
    logc = _attn_tables(s)
    ka, kb = _alibi_tables(s)

    h = norm_pre(x, g_pre)
    z, proj = mm_in(h, wg_in, late[:3] if on_mesh else [])
    ya = gating_fwd(z, ln_g, ln_b, w_s, bs_t)
    yb, lse, ff = attn_fwd(z, logc, ka, kb, late[3:] if on_mesh else [])
    wg_a, wg_b, wg_out, wg_ff1, wg_ff2 = list(proj) + list(ff) if on_mesh else late
    w_a, w_b, w_out, w_ff2 = wg_a.reshape(D, D), wg_b.reshape(D, D), wg_out.reshape(D, D), wg_ff2.reshape(D_FF, D)
    merged, pa, pb = proj_merge(ya, yb, w_a, w_b, z, bg)
    o, x1, h2 = out_norm(merged, w_out, x, g_post, g_fpre)
    a, rl = mm_ff1(h2, wg_ff1)
    dy, df, d_gfpost, loss = ff2_loss(rl, w_ff2, x1, target, g_fpost)

    half_cols = pl.BlockSpec((D, D // 2), lambda i, j: (0, j))
    d_wff2 = mm_tn("dw_ff2", rl, df, D // 2, D, (D_FF, D), pl.BlockSpec((D // 2, D), lambda i, j: (i, 0)))
    da = ff2_bwd(df, w_ff2, a)
    d_wff1 = mm_tn("dw_ff1", h2, da, D, D // 2, (N_CHIPS, D, D),
                   pl.BlockSpec((None, D, D // 2), lambda i, j: (j // 2, 0, j % 2)))
    d_ff = [d_wff1, d_wff2.reshape(N_CHIPS, D, D)]
    dx1, do, d_gfpre, d_gpost, recv_ff = ff1_bwd_norms(da, wg_ff1, x1, o, dy, g_fpre, g_post, d_ff if on_mesh else [])
    d_wout = mm_tn("dw_out", merged, do, D, D // 2, (D, D), half_cols)
    dpa, dpb, dga, dgb, d_bg = out_bwd_gates(do, w_out, pa, pb, z, bg)
    d_wa = mm_tn("dw_a", ya, dpa, D, D // 2, (D, D), half_cols)
    d_wb = mm_tn("dw_b", yb, dpb, D, D // 2, (D, D), half_cols)
    dya = mm_nt("dy_a", dpa, w_a)
    dyb = mm_nt("dy_b", dpb, w_b)
    d_proj = [d_wa.reshape(N_CHIPS, D // N_CHIPS, D), d_wb.reshape(N_CHIPS, D // N_CHIPS, D),
              d_wout.reshape(N_CHIPS, D // N_CHIPS, D)]
    du, dv, d_ws, d_bs, d_lng, d_lnb, recv_proj = gating_bwd(z, dya, ln_g, ln_b, w_s, bs_t, d_proj if on_mesh else [])
    early = d_proj + d_ff
    parts_early = add(BIG[1:], early, list(recv_proj) + list(recv_ff)) if on_mesh else []
    dq, dk, dvb, got_early = attn_bwd(z, yb, dyb, lse, logc, ka, kb, parts_early)
    dz = jnp.concatenate([du, dv, dq, dk, dvb, dga, dgb], axis=1)
    half = IN_SHARD // 2
    d_win = mm_tn("dw_in", h, dz, D, half, (N_CHIPS, D, IN_SHARD),
                  pl.BlockSpec((None, D, half), lambda i, j: (j // 2, 0, j % 2)))
    parts_late = add(BIG[:1], [d_win], swap_halves("swap_w_in", [d_win])) if on_mesh else []
    dx, d_gpre, got_late = in_bwd_norm(dz, wg_in, x, dx1, g_pre, parts_late)

    small = dict(norm_mix_pre=d_gpre, b_gate=d_bg, ln_v_g=d_lng, ln_v_b=d_lnb, w_s=d_ws, b_s=d_bs[:, 0, :],
                 norm_mix_post=d_gpost, norm_ffn_pre=d_gfpre, norm_ffn_post=d_gfpost)
    return loss[0, 0], dx, [d_win] + early, small, parts_late + parts_early, list(got_late) + list(got_early)


BIG = ("w_in", "w_a_proj", "w_b_proj", "w_out", "w_ff1", "w_ff2")
SMALL = ("norm_mix_pre", "ln_v_g", "ln_v_b", "b_s", "norm_mix_post", "norm_ffn_pre", "norm_ffn_post", "w_s", "b_gate")
ORDER = ("norm_mix_pre", "w_in", "b_gate", "ln_v_g", "ln_v_b", "w_s", "b_s", "w_a_proj", "w_b_proj", "w_out",
         "norm_mix_post", "norm_ffn_pre", "w_ff1", "w_ff2", "norm_ffn_post")
VEC_ROWS = D // 128
WS_ROW = 7 * VEC_ROWS
BG_ROW = WS_ROW + GROUPS * CHUNK
LOSS_ROW = BG_ROW + 2 * VEC_ROWS
PACK_ROWS = LOSS_ROW + 8


def pack_small(small, loss):
    vectors = [small[n] for n in SMALL[:7]]

    def body(*refs):
        out = refs[-1]
        ws_ref, bg_ref, loss_ref = refs[7:10]
        for i, n in enumerate(SMALL[:7]):
            if n == "b_s":
                out[i * VEC_ROWS:(i + 1) * VEC_ROWS, :] = refs[i][...]
            else:
                for j in range(VEC_ROWS):
                    out[i * VEC_ROWS + j:i * VEC_ROWS + j + 1, :] = refs[i][:, j * 128:(j + 1) * 128]
        for g in range(GROUPS):
            out[WS_ROW + g * CHUNK:WS_ROW + (g + 1) * CHUNK, :] = ws_ref[g]
        for r in range(2):
            for j in range(VEC_ROWS):
                row = BG_ROW + r * VEC_ROWS + j
                out[row:row + 1, :] = bg_ref[r:r + 1, j * 128:(j + 1) * 128]
        lane = lax.broadcasted_iota(jnp.int32, (8, 128), 1)
        sub = lax.broadcasted_iota(jnp.int32, (8, 128), 0)
        out[LOSS_ROW:LOSS_ROW + 8, :] = jnp.where((lane == 0) & (sub == 0), loss_ref[...], 0.0)

    return pl.pallas_call(
        body, name="pack_small", out_shape=jax.ShapeDtypeStruct((PACK_ROWS, 128), F32),
        compiler_params=_params(),
    )(*vectors, small["w_s"], small["b_gate"], loss)


def adamw_small(summed, chip, w, m, v):
    shapes = {n: (1, D) for n in SMALL}
    shapes.update(b_s=(GROUPS, CHUNK), w_s=(GROUPS * CHUNK, CHUNK), b_gate=(2, D // N_CHIPS))
    flat = lambda t: [t[n].reshape(shapes[n]) for n in SMALL]
    per = D // N_CHIPS // 128

    def body(chip_ref, sum_ref, *refs):
        params, outs = refs[:27], refs[27:]
        sub = lax.broadcasted_iota(jnp.int32, (VEC_ROWS, 128), 0)

        def gate_row(r):
            rows = sum_ref[BG_ROW + r * VEC_ROWS:BG_ROW + (r + 1) * VEC_ROWS, :]
            return jnp.concatenate([jnp.sum(jnp.where(sub == per * chip_ref[0] + j, rows, 0.0), axis=0, keepdims=True)
                                    for j in range(per)], axis=1)

        for i, n in enumerate(SMALL):
            if n == "b_s":
                g = sum_ref[i * VEC_ROWS:(i + 1) * VEC_ROWS, :]
            elif n == "w_s":
                g = sum_ref[WS_ROW:BG_ROW, :]
            elif n == "b_gate":
                g = jnp.concatenate([gate_row(0), gate_row(1)], axis=0)
            else:
                g = jnp.concatenate([sum_ref[i * VEC_ROWS + j:i * VEC_ROWS + j + 1, :] for j in range(VEC_ROWS)],
                                    axis=1)
            delta, nm, nv = _adamw_math(params[i][...], g, params[9 + i][...], params[18 + i][...])
            outs[4 * i][...], outs[4 * i + 1][...], outs[4 * i + 2][...], outs[4 * i + 3][...] = g, delta, nm, nv

    vm = pl.BlockSpec(memory_space=pltpu.VMEM)
    res = pl.pallas_call(
        body, name="adamw_small",
        in_specs=[pl.BlockSpec(memory_space=pltpu.SMEM)] + [vm] * 28, out_specs=[vm] * 36,
        out_shape=[jax.ShapeDtypeStruct(shapes[n], F32) for n in SMALL for _ in range(4)],
        compiler_params=_params(),
    )(chip, summed, *flat(w), *flat(m), *flat(v))
    return {n: tuple(r.reshape(w[n].shape) for r in res[4 * i:4 * i + 4]) for i, n in enumerate(SMALL)}


def kernel(x, norm_mix_pre, w_in, b_gate, ln_v_g, ln_v_b, w_s, b_s, w_a_proj, w_b_proj, w_out, norm_mix_post, norm_ffn_pre, w_ff1, w_ff2, norm_ffn_post, loss_target, m_norm_mix_pre, m_w_in, m_b_gate, m_ln_v_g, m_ln_v_b, m_w_s, m_b_s, m_w_a_proj, m_w_b_proj, m_w_out, m_norm_mix_post, m_norm_ffn_pre, m_w_ff1, m_w_ff2, m_norm_ffn_post, v_norm_mix_pre, v_w_in, v_b_gate, v_ln_v_g, v_ln_v_b, v_w_s, v_b_s, v_w_a_proj, v_w_b_proj, v_w_out, v_norm_mix_post, v_norm_ffn_pre, v_w_ff1, v_w_ff2, v_norm_ffn_post):
    w = dict(norm_mix_pre=norm_mix_pre, w_in=w_in, b_gate=b_gate, ln_v_g=ln_v_g, ln_v_b=ln_v_b, w_s=w_s, b_s=b_s,
             w_a_proj=w_a_proj, w_b_proj=w_b_proj, w_out=w_out, norm_mix_post=norm_mix_post,
             norm_ffn_pre=norm_ffn_pre, w_ff1=w_ff1, w_ff2=w_ff2, norm_ffn_post=norm_ffn_post)
    m = dict(norm_mix_pre=m_norm_mix_pre, w_in=m_w_in, b_gate=m_b_gate, ln_v_g=m_ln_v_g, ln_v_b=m_ln_v_b, w_s=m_w_s,
             b_s=m_b_s, w_a_proj=m_w_a_proj, w_b_proj=m_w_b_proj, w_out=m_w_out, norm_mix_post=m_norm_mix_post,
             norm_ffn_pre=m_norm_ffn_pre, w_ff1=m_w_ff1, w_ff2=m_w_ff2, norm_ffn_post=m_norm_ffn_post)
    v = dict(norm_mix_pre=v_norm_mix_pre, w_in=v_w_in, b_gate=v_b_gate, ln_v_g=v_ln_v_g, ln_v_b=v_ln_v_b, w_s=v_w_s,
             b_s=v_b_s, w_a_proj=v_w_a_proj, w_b_proj=v_w_b_proj, w_out=v_w_out, norm_mix_post=v_norm_mix_post,
             norm_ffn_pre=v_norm_ffn_pre, w_ff1=v_w_ff1, w_ff2=v_w_ff2, norm_ffn_post=v_norm_ffn_post)
    chip = 2 * lax.axis_index("x") + lax.axis_index("y")
    core = lax.axis_index("c")

    where = jnp.stack([chip, core]).astype(jnp.int32)
    placed = [place_shard("place_" + n, w[n][0], where, BF16, min(w[n].shape[1], 256)) for n in BIG]
    placed.append(place_shard("place_b_gate", jnp.pad(b_gate[0], ((0, 14), (0, 0))), where, F32, 16))
    wg_in, bg_all = gather_shards([placed[0], placed[6]])
    bg = jnp.transpose(bg_all[:, :2, :], (1, 0, 2)).reshape(2, D)
    vecs = (norm_mix_pre, ln_v_g, ln_v_b, norm_mix_post, norm_ffn_pre, norm_ffn_post)
    loss, dx, _, small, parts, got = local_step(
        x[0], loss_target[0], vecs, w_s[0], b_s[0].T, bg, wg_in, placed[1:6],
        core=jnp.reshape(core, (1,)).astype(jnp.int32))

    halves = [sum_chips("sum_" + n, p, r, where, min(p.shape[1], 256)) for n, p, r in zip(BIG, parts, got)]
    grads = dict(zip(BIG, join_halves(halves)))

    summed = allreduce_small(pack_small(small, loss.reshape(1, 1)))
    loss = summed[LOSS_ROW, 0]

    new = adamw_small(summed, jnp.reshape(chip, (1,)).astype(jnp.int32), w, m, v)
    for n in BIG:
        shape = w[n].shape
        res = adamw("adamw_" + n, w[n][0], grads[n], m[n][0], v[n][0], min(shape[1], 256))
        new[n] = tuple(r.reshape(shape) for r in res)

    outs = [loss, dx[None]]
    for i in range(4):
        outs += [new[n][i] for n in ORDER]
    return tuple(outs)
```

```python
import functools
import math

import numpy as np
import jax
import jax.numpy as jnp
from jax import lax
from jax.experimental import pallas as pl
from jax.experimental.pallas import tpu as pltpu

F32 = jnp.float32
BF16 = jnp.bfloat16
MESH = pl.DeviceIdType.MESH

D = 1024
EPS = 1e-6
CHUNK = 128
GROUPS = 8
HEADS = 16
HEAD_DIM = 64
ATT_T = 256
ATT_GROUP = 4
ATT_BWD_GROUP = 2
N_CHIPS = 4
D_FF = 4 * D
IN_COLS = 7 * D
IN_SHARD = IN_COLS // N_CHIPS
MASKED = -1e30
VMEM_LIMIT = 56 * 2 ** 20

ADAM_LR, ADAM_B1, ADAM_B2, ADAM_EPS, ADAM_WD, ADAM_STEP = 0.001, 0.9, 0.999, 1e-08, 0.01, 10

NN = (((1,), (0,)), ((), ()))
NT = (((1,), (1,)), ((), ()))
TN = (((0,), (0,)), ((), ()))


def _dot(a, b, dims=NN):
    return lax.dot_general(a, b, dims, preferred_element_type=F32)


def _params(*sem, communicates=False):
    return pltpu.CompilerParams(dimension_semantics=sem or None, vmem_limit_bytes=VMEM_LIMIT,
                                has_side_effects=communicates)


def _rows(tr, c, col=0):
    return pl.BlockSpec((tr, c), lambda i: (i, col))


def _full(shape):
    n = len(shape)
    return pl.BlockSpec(shape, lambda *_: (0,) * n)


def _gelu(x):
    k = math.sqrt(2.0 / math.pi)
    return 0.5 * x * (1.0 + jnp.tanh(k * (x + 0.044715 * x * x * x)))


def _gelu_and_grad(x):
    k = math.sqrt(2.0 / math.pi)
    t = jnp.tanh(k * (x + 0.044715 * x * x * x))
    g = 0.5 * x * (1.0 + t)
    dg = 0.5 * (1.0 + t) + 0.5 * x * (1.0 - t * t) * (k * (1.0 + 3.0 * 0.044715 * x * x))
    return g, dg


def _sigmoid(x):
    return 1.0 / (1.0 + jnp.exp(-x))


def _rms(x):
    r = lax.rsqrt(jnp.mean(x * x, axis=-1, keepdims=True) + EPS)
    return x * r, r


def _rms_bwd(dn, xhat, r):
    return r * (dn - xhat * jnp.mean(dn * xhat, axis=-1, keepdims=True))


def norm_pre(x, g):
    s = x.shape[0]
    tr = 512

    def body(x_ref, g_ref, h_ref):
        xhat, _ = _rms(x_ref[...])
        h_ref[...] = (xhat * g_ref[...]).astype(BF16)

    return pl.pallas_call(
        body, name="norm_pre", grid=(s // tr,),
        in_specs=[_rows(tr, D), _full((1, D))], out_specs=_rows(tr, D),
        out_shape=jax.ShapeDtypeStruct((s, D), BF16), compiler_params=_params("parallel"),
    )(x, g)


def mm_in(h, wg, gathering):
    s = h.shape[0]
    tm, tn = 1024, IN_SHARD // 2
    per = IN_SHARD // tn
    n = len(gathering)
    ni, nj = s // tm, IN_COLS // tn

    def body(*refs):
        a_ref, b_ref, o_ref = refs[0], refs[1], refs[2 + n]
        i, j = pl.program_id(0), pl.program_id(1)
        if n:
            send, pass_on, finish = _gather_phases(refs[3 + n:3 + 2 * n], *refs[3 + 2 * n:])
            pl.when((i == 0) & (j == 0))(send)
            pl.when((i == ni - 1) & (j == nj // 2))(pass_on)
        o_ref[...] = _dot(a_ref[...], b_ref[...])
        if n:
            pl.when((i == ni - 1) & (j == nj - 1))(finish)

    out = pl.pallas_call(
        body, name="mm_in", grid=(ni, nj),
        in_specs=[pl.BlockSpec((tm, D), lambda i, j: (i, 0)),
                  pl.BlockSpec((None, D, tn), lambda i, j: (j // per, 0, j % per))] + [ANY] * n,
        out_specs=[pl.BlockSpec((tm, tn), lambda i, j: (i, j))] + [ANY] * n,
        out_shape=[jax.ShapeDtypeStruct((s, IN_COLS), F32)]
        + [jax.ShapeDtypeStruct(a.shape, a.dtype) for a in gathering],
        input_output_aliases={2 + w: 1 + w for w in range(n)},
        scratch_shapes=_gather_sems(n) if n else [],
        compiler_params=_params("arbitrary", "arbitrary", communicates=bool(n)),
    )(h, wg, *gathering)
    return out[0], out[1:]


def _tril_ws(ws_ref, g):
    r = lax.broadcasted_iota(jnp.int32, (CHUNK, CHUNK), 0)
    c = lax.broadcasted_iota(jnp.int32, (CHUNK, CHUNK), 1)
    return jnp.where(c <= r, ws_ref[g], 0.0).astype(BF16)


def _layer_norm(v):
    mu = jnp.mean(v, axis=-1, keepdims=True)
    d = v - mu
    rstd = lax.rsqrt(jnp.mean(d * d, axis=-1, keepdims=True) + EPS)
    return d * rstd, rstd


def gating_fwd(z, ln_g, ln_b, w_s, bs_t, gathering):
    s = z.shape[0]
    n = len(gathering)
    steps = s // CHUNK

    def body(*refs):
        u_ref, v_ref, lg_ref, lb_ref, ws_ref, bst_ref = refs[:6]
        ya_ref = refs[6 + n]
        ci = pl.program_id(0)
        if n:
            send, pass_on, finish = _gather_phases(refs[7 + n:7 + 2 * n], *refs[7 + 2 * n:])
            pl.when(ci == 0)(send)
            pl.when(ci == steps * 3 // 4)(pass_on)
        ug = _gelu(u_ref[...])
        vhat, _ = _layer_norm(_gelu(v_ref[...]))
        vn = (vhat * lg_ref[...] + lb_ref[...]).astype(BF16)
        for g in range(GROUPS):
            cols = slice(g * CHUNK, (g + 1) * CHUNK)
            mixed = _dot(_tril_ws(ws_ref, g), vn[:, cols]) + bst_ref[:, g:g + 1]
            ya_ref[:, cols] = (ug[:, cols] * mixed).astype(BF16)
        if n:
            pl.when(ci == steps - 1)(finish)

    out = pl.pallas_call(
        body, name="gating_fwd", grid=(steps,),
        in_specs=[_rows(CHUNK, D, 0), _rows(CHUNK, D, 1), _full((1, D)), _full((1, D)),
                  _full((GROUPS, CHUNK, CHUNK)), _full((CHUNK, GROUPS))] + [ANY] * n,
        out_specs=[_rows(CHUNK, D)] + [ANY] * n,
        out_shape=[jax.ShapeDtypeStruct((s, D), BF16)] + [jax.ShapeDtypeStruct(a.shape, a.dtype) for a in gathering],
        input_output_aliases={6 + w: 1 + w for w in range(n)},
        scratch_shapes=_gather_sems(n) if n else [],
        compiler_params=_params("arbitrary", communicates=bool(n)),
    )(z, z, ln_g, ln_b, w_s, bs_t, *gathering)
    return out[0], out[1:]


def _attn_tables(s):
    nd = s // ATT_T
    r = np.arange(ATT_T)[None, :, None]
    c = np.arange(ATT_T)[None, None, :]
    delta = np.arange(nd)[:, None, None] * ATT_T + r - c
    count = np.zeros(delta.shape, np.int64)
    for window, dilation in ((128, 1), (512, 4), (2048, 16)):
        count += (delta >= 0) & (delta % dilation == 0) & (delta <= window)
    logc = np.where(count > 0, np.log(np.maximum(count, 1)), MASKED)
    return jnp.asarray(logc, F32)


AUG = 3


def _split3_np(x):
    terms, rest = [], np.asarray(x, np.float64)
    for _ in range(AUG):
        term = np.asarray(rest.astype(jnp.bfloat16), np.float64)
        terms.append(term)
        rest = rest - term
    return terms


def _split3(x):
    terms, rest = [], x
    for _ in range(AUG):
        term = rest.astype(BF16).astype(F32)
        terms.append(term)
        rest = rest - term
    return terms


def _alibi_tables(s):
    nb = s // ATT_T
    slopes = np.exp2(-8.0 * np.arange(1, HEADS + 1, dtype=np.float64) / HEADS)
    ka = np.zeros((HEADS // 2, 2, ATT_T, 128), np.float32)
    kb = np.zeros((HEADS // 2, 2, nb, 128), np.float32)
    for p in range(HEADS // 2):
        for e in range(2):
            base = HEAD_DIM * (1 - e)
            for a, term in enumerate(_split3_np(slopes[2 * p + e] * np.arange(ATT_T))):
                ka[p, e, :, base + a] = term
            for a, term in enumerate(_split3_np(slopes[2 * p + e] * ATT_T * np.arange(nb))):
                kb[p, e, :, base + AUG + a] = term
            ka[p, e, :, base + 2 * AUG:base + 3 * AUG] = 1.0
    return jnp.asarray(ka), jnp.asarray(kb)


def _head_masks():
    lane = lax.broadcasted_iota(jnp.int32, (1, 128), 1)
    first = lane < HEAD_DIM

    def ones(e, n):
        base = HEAD_DIM * (1 - e)
        return ((lane >= base) & (lane < base + n)).astype(F32)

    return first, lane, ones


def _place3(lane, at, terms, other):
    for a, term in enumerate(terms):
        other = jnp.where(lane == at + a, term, other)
    return other


def attn_fwd(z, logc, ka, kb, gathering):
    s = z.shape[0]
    nq = s // ATT_T
    t = ATT_T
    n = len(gathering)
    grp = ATT_GROUP
    ngrp = HEADS // 2 // grp
    wide = 128 * grp
    qcol, kcol, vcol = 2 * D // wide, 3 * D // wide, 4 * D // wide

    def body(*refs):
        q_ref, k_ref, v_ref, lc_ref, ka_ref, kb_ref = refs[:6]
        y_ref, lse_ref = refs[6 + n:8 + n]
        q_s, k_s, v_s, m_s, l_s, acc_s = refs[8 + 2 * n:14 + 2 * n]
        gi, qi = pl.program_id(0), pl.program_id(1)
        first, lane, ones = _head_masks()
        if n:
            send, pass_on, finish = _gather_phases(refs[8 + n:8 + 2 * n], *refs[14 + 2 * n:])
            pl.when((gi == 0) & (qi == 0))(send)
            pl.when((gi == ngrp - 1) & (qi == 0))(pass_on)

        @pl.when(qi == 0)
        def _():
            sel = jnp.broadcast_to(first.astype(F32), (t, 128))
            for pr in range(grp):
                cols = slice(pr * 128, (pr + 1) * 128)
                for jb in range(nq):
                    kj = k_ref[jb * t:(jb + 1) * t, cols]
                    vj = v_ref[jb * t:(jb + 1) * t, cols]
                    k_s[pr, 0, jb] = jnp.where(first, kj, ka_ref[pr, 0] + kb_ref[pr, 0, jb:jb + 1, :]).astype(BF16)
                    k_s[pr, 1, jb] = jnp.where(first, ka_ref[pr, 1] + kb_ref[pr, 1, jb:jb + 1, :], kj).astype(BF16)
                    v_s[pr, jb, 0:t, 0:128] = jnp.where(first, vj, 0.0).astype(BF16)
                    v_s[pr, jb, t:2 * t, 0:128] = jnp.where(first, 0.0, vj).astype(BF16)
                    v_s[pr, jb, 0:t, 128:256] = sel.astype(BF16)
                    v_s[pr, jb, t:2 * t, 128:256] = (1.0 - sel).astype(BF16)

        for pr in range(grp):
            q = q_ref[:, pr * 128:(pr + 1) * 128] * (1.0 / math.sqrt(HEAD_DIM))
            q_s[pr, 0] = jnp.where(first, q, ones(0, 2 * AUG)).astype(BF16)
            q_s[pr, 1] = jnp.where(first, ones(1, 2 * AUG), q).astype(BF16)
        m_s[...] = jnp.full_like(m_s, MASKED)
        l_s[...] = jnp.zeros_like(l_s)
        acc_s[...] = jnp.zeros_like(acc_s)

        def scores(j):
            return tuple(_dot(q_s[pr, e], k_s[pr, e, j], NT) for pr in range(grp) for e in range(2))

        def step(j, carry):
            softmax_block(j, scores(j))
            return carry

        def softmax_block(j, u):
            lc = lc_ref[qi - j]
            for pr in range(grp):
                u0 = u[2 * pr] + lc
                u1 = u[2 * pr + 1] + lc
                m0, m1 = m_s[pr, 0], m_s[pr, 1]
                n0 = jnp.maximum(m0, jnp.max(u0, axis=-1, keepdims=True))
                n1 = jnp.maximum(m1, jnp.max(u1, axis=-1, keepdims=True))
                m_s[pr, 0], m_s[pr, 1] = n0, n1
                p = jnp.concatenate([jnp.exp(u0 - jnp.concatenate([n0, n0], axis=1)).astype(BF16),
                                     jnp.exp(u1 - jnp.concatenate([n1, n1], axis=1)).astype(BF16)], axis=1)
                pv = _dot(p, v_s[pr, j])
                alpha = jnp.where(first, jnp.exp(m0 - n0), jnp.exp(m1 - n1))
                acc_s[pr] = acc_s[pr] * alpha + pv[:, 0:128]
                l_s[pr] = l_s[pr] * alpha + pv[:, 128:256]

        lax.fori_loop(0, qi + 1, step, 0)
        for pr in range(grp):
            cols = slice(pr * 128, (pr + 1) * 128)
            y_ref[:, cols] = (acc_s[pr] / l_s[pr]).astype(BF16)
            lse_ref[:, cols] = jnp.where(first, m_s[pr, 0], m_s[pr, 1]) + jnp.log(l_s[pr])
        if n:
            pl.when((gi == ngrp - 1) & (qi == nq - 1))(finish)

    out = pl.pallas_call(
        body, name="attn_fwd", grid=(ngrp, nq),
        in_specs=[pl.BlockSpec((t, wide), lambda g, i: (i, qcol + g)),
                  pl.BlockSpec((s, wide), lambda g, i: (0, kcol + g)),
                  pl.BlockSpec((s, wide), lambda g, i: (0, vcol + g)),
                  _full((nq, t, t)),
                  pl.BlockSpec((grp, 2, t, 128), lambda g, i: (g, 0, 0, 0)),
                  pl.BlockSpec((grp, 2, nq, 128), lambda g, i: (g, 0, 0, 0))] + [ANY] * n,
        out_specs=[pl.BlockSpec((t, wide), lambda g, i: (i, g)), pl.BlockSpec((t, wide), lambda g, i: (i, g))]
        + [ANY] * n,
        out_shape=[jax.ShapeDtypeStruct((s, D), BF16), jax.ShapeDtypeStruct((s, D), F32)]
        + [jax.ShapeDtypeStruct(a.shape, a.dtype) for a in gathering],
        input_output_aliases={6 + w: 2 + w for w in range(n)},
        scratch_shapes=[pltpu.VMEM((grp, 2, t, 128), BF16), pltpu.VMEM((grp, 2, nq, t, 128), BF16),
                        pltpu.VMEM((grp, nq, 2 * t, 256), BF16), pltpu.VMEM((grp, 2, t, 128), F32),
                        pltpu.VMEM((grp, t, 128), F32), pltpu.VMEM((grp, t, 128), F32)]
        + (_gather_sems(n) if n else []),
        compiler_params=_params("arbitrary", "arbitrary", communicates=bool(n)),
    )(z, z, z, logc, ka, kb, *gathering)
    return out[0], out[1], out[2:]


def proj_merge(ya, yb, wa, wb, z, bg):
    s = ya.shape[0]
    tm = 512

    def body(ya_ref, yb_ref, wa_ref, wb_ref, ga_ref, gb_ref, bg_ref, mg_ref, pa_ref, pb_ref):
        pa = _dot(ya_ref[...], wa_ref[...])
        pb = _dot(yb_ref[...], wb_ref[...])
        sa = _sigmoid(ga_ref[...] + bg_ref[0:1, :])
        sb = _sigmoid(gb_ref[...] + bg_ref[1:2, :])
        mg_ref[...] = (sa * pa + sb * pb).astype(BF16)
        pa_ref[...] = pa.astype(BF16)
        pb_ref[...] = pb.astype(BF16)

    out = jax.ShapeDtypeStruct((s, D), BF16)
    return pl.pallas_call(
        body, name="proj_merge", grid=(s // tm,),
        in_specs=[_rows(tm, D), _rows(tm, D), _full((D, D)), _full((D, D)),
                  _rows(tm, D, 5), _rows(tm, D, 6), _full((2, D))],
        out_specs=[_rows(tm, D)] * 3, out_shape=[out] * 3, compiler_params=_params("parallel"),
    )(ya, yb, wa, wb, z, z, bg)


def out_norm(merged, w_out, x, g_post, g_fpre):
    s = x.shape[0]
    tm = 512

    def body(mg_ref, w_ref, x_ref, gp_ref, gf_ref, o_ref, x1_ref, h2_ref):
        o = _dot(mg_ref[...], w_ref[...])
        ohat, _ = _rms(o)
        x1 = x_ref[...] + ohat * gp_ref[...]
        x1hat, _ = _rms(x1)
        o_ref[...] = o
        x1_ref[...] = x1
        h2_ref[...] = (x1hat * gf_ref[...]).astype(BF16)

    return pl.pallas_call(
        body, name="out_norm", grid=(s // tm,),
        in_specs=[_rows(tm, D), _full((D, D)), _rows(tm, D), _full((1, D)), _full((1, D))],
        out_specs=[_rows(tm, D)] * 3,
        out_shape=[jax.ShapeDtypeStruct((s, D), F32), jax.ShapeDtypeStruct((s, D), F32),
                   jax.ShapeDtypeStruct((s, D), BF16)],
        compiler_params=_params("parallel"),
    )(merged, w_out, x, g_post, g_fpre)


def mm_ff1(h2, wg):
    s = h2.shape[0]
    tm = 1024

    def body(a_ref, b_ref, o_ref, r_ref):
        a = _dot(a_ref[...], b_ref[...])
        o_ref[...] = a
        r = jnp.maximum(a, 0.0)
        r_ref[...] = (r * r).astype(BF16)

    return pl.pallas_call(
        body, name="mm_ff1", grid=(s // tm, N_CHIPS),
        in_specs=[pl.BlockSpec((tm, D), lambda i, j: (i, 0)), pl.BlockSpec((None, D, D), lambda i, j: (j, 0, 0))],
        out_specs=[pl.BlockSpec((tm, D), lambda i, j: (i, j))] * 2,
        out_shape=[jax.ShapeDtypeStruct((s, D_FF), F32), jax.ShapeDtypeStruct((s, D_FF), BF16)],
        compiler_params=_params("parallel", "parallel"),
    )(h2, wg)


def ff2_loss(rl, w_ff2, x1, target, g_fpost):
    s = x1.shape[0]
    tm = 256

    def body(rl_ref, w_ref, x1_ref, t_ref, g_ref, dy_ref, df_ref, dg_ref, loss_ref):
        @pl.when(pl.program_id(0) == 0)
        def _():
            dg_ref[...] = jnp.zeros_like(dg_ref)
            loss_ref[...] = jnp.zeros_like(loss_ref)

        f = _dot(rl_ref[...], w_ref[...])
        fhat, r = _rms(f)
        err = x1_ref[...] + fhat * g_ref[...] - t_ref[...]
        loss_ref[...] += 0.5 * jnp.sum(jnp.mean(err * err, axis=-1, keepdims=True), axis=0, keepdims=True)
        dy = err * (1.0 / D)
        dy_ref[...] = dy
        dg_ref[...] += jnp.sum(dy * fhat, axis=0, keepdims=True)
        df_ref[...] = _rms_bwd(dy * g_ref[...], fhat, r).astype(BF16)

    return pl.pallas_call(
        body, name="ff2_loss", grid=(s // tm,),
        in_specs=[_rows(tm, D_FF), _full((D_FF, D)), _rows(tm, D), _rows(tm, D), _full((1, D))],
        out_specs=[_rows(tm, D), _rows(tm, D), _full((1, D)), _full((1, 1))],
        out_shape=[jax.ShapeDtypeStruct((s, D), F32), jax.ShapeDtypeStruct((s, D), BF16),
                   jax.ShapeDtypeStruct((1, D), F32), jax.ShapeDtypeStruct((1, 1), F32)],
        compiler_params=_params("arbitrary"),
    )(rl, w_ff2, x1, target, g_fpost)


def mm_tn(name, a, b, ta, tb, out_shape, out_spec):
    s = a.shape[0]

    def body(a_ref, b_ref, o_ref):
        o_ref[...] = _dot(a_ref[...], b_ref[...], TN)

    return pl.pallas_call(
        body, name=name, grid=(a.shape[1] // ta, b.shape[1] // tb),
        in_specs=[pl.BlockSpec((s, ta), lambda i, j: (0, i)), pl.BlockSpec((s, tb), lambda i, j: (0, j))],
        out_specs=out_spec, out_shape=jax.ShapeDtypeStruct(out_shape, F32),
        compiler_params=_params("parallel", "parallel"),
    )(a, b)


def mm_nt(name, a, w):
    s = a.shape[0]
    tm = 512

    def body(a_ref, w_ref, o_ref):
        o_ref[...] = _dot(a_ref[...], w_ref[...], NT)

    return pl.pallas_call(
        body, name=name, grid=(s // tm,), in_specs=[_rows(tm, D), _full((D, D))], out_specs=_rows(tm, D),
        out_shape=jax.ShapeDtypeStruct((s, D), F32), compiler_params=_params("parallel"),
    )(a, w)


def ff2_bwd(df, w_ff2, a):
    s = df.shape[0]
    tm = 1024

    def body(df_ref, w_ref, a_ref, da_ref):
        drl = _dot(df_ref[...], w_ref[...], NT)
        da_ref[...] = (drl * (2.0 * jnp.maximum(a_ref[...], 0.0))).astype(BF16)

    return pl.pallas_call(
        body, name="ff2_bwd", grid=(s // tm, D_FF // D),
        in_specs=[pl.BlockSpec((tm, D), lambda i, j: (i, 0)), pl.BlockSpec((D, D), lambda i, j: (j, 0)),
                  pl.BlockSpec((tm, D), lambda i, j: (i, j))],
        out_specs=pl.BlockSpec((tm, D), lambda i, j: (i, j)),
        out_shape=jax.ShapeDtypeStruct((s, D_FF), BF16), compiler_params=_params("parallel", "parallel"),
    )(df, w_ff2, a)


def ff1_bwd_norms(da, wg, x1, o, dy, g_fpre, g_post, swapping):
    s = x1.shape[0]
    tm = 512
    n = len(swapping)

    def body(*refs):
        da_ref, w_ref, x1_ref, o_ref, dy_ref, gf_ref, gp_ref = refs[:7]
        dx1_ref, do_ref, dgf_ref, dgp_ref = refs[7 + n:11 + n]
        acc_ref = refs[11 + 2 * n]
        i, k = pl.program_id(0), pl.program_id(1)
        if n:
            send, finish = _swap_phases(refs[7:7 + n], refs[11 + n:11 + 2 * n], *refs[12 + 2 * n:])
            pl.when((i == 0) & (k == 0))(send)

        @pl.when((i == 0) & (k == 0))
        def _():
            dgf_ref[...] = jnp.zeros_like(dgf_ref)
            dgp_ref[...] = jnp.zeros_like(dgp_ref)

        part = _dot(da_ref[...], w_ref[...], NT)

        @pl.when(k == 0)
        def _():
            acc_ref[...] = part

        @pl.when(k > 0)
        def _():
            acc_ref[...] += part

        @pl.when(k == N_CHIPS - 1)
        def _():
            dh2 = acc_ref[...]
            x1hat, r2 = _rms(x1_ref[...])
            dgf_ref[...] += jnp.sum(dh2 * x1hat, axis=0, keepdims=True)
            dx1 = dy_ref[...] + _rms_bwd(dh2 * gf_ref[...], x1hat, r2)
            ohat, r1 = _rms(o_ref[...])
            dgp_ref[...] += jnp.sum(dx1 * ohat, axis=0, keepdims=True)
            dx1_ref[...] = dx1
            do_ref[...] = _rms_bwd(dx1 * gp_ref[...], ohat, r1).astype(BF16)

        if n:
            pl.when((i == s // tm - 1) & (k == N_CHIPS - 1))(finish)

    row = pl.BlockSpec((tm, D), lambda i, k: (i, 0))
    vec = pl.BlockSpec((1, D), lambda i, k: (0, 0))
    res = pl.pallas_call(
        body, name="ff1_bwd_norms", grid=(s // tm, N_CHIPS),
        in_specs=[pl.BlockSpec((tm, D), lambda i, k: (i, k)), pl.BlockSpec((None, D, D), lambda i, k: (k, 0, 0)),
                  row, row, row, vec, vec] + [ANY] * n,
        out_specs=[row, row, vec, vec] + [ANY] * n,
        out_shape=[jax.ShapeDtypeStruct((s, D), F32), jax.ShapeDtypeStruct((s, D), BF16),
                   jax.ShapeDtypeStruct((1, D), F32), jax.ShapeDtypeStruct((1, D), F32)] + _swap_shapes(swapping),
        scratch_shapes=[pltpu.VMEM((tm, D), F32)] + (_swap_sems(n) if n else []),
        compiler_params=_params("arbitrary", "arbitrary", communicates=bool(n)),
    )(da, wg, x1, o, dy, g_fpre, g_post, *swapping)
    return res[0], res[1], res[2], res[3], res[4:]


def out_bwd_gates(do, w_out, pa, pb, z, bg):
    s = do.shape[0]
    tm = 512

    def body(do_ref, w_ref, pa_ref, pb_ref, ga_ref, gb_ref, bg_ref, dpa_ref, dpb_ref, dga_ref, dgb_ref, dbg_ref):
        @pl.when(pl.program_id(0) == 0)
        def _():
            dbg_ref[...] = jnp.zeros_like(dbg_ref)

        dm = _dot(do_ref[...], w_ref[...], NT)
        sa = _sigmoid(ga_ref[...] + bg_ref[0:1, :])
        sb = _sigmoid(gb_ref[...] + bg_ref[1:2, :])
        dpa_ref[...] = (dm * sa).astype(BF16)
        dpb_ref[...] = (dm * sb).astype(BF16)
        dga = dm * pa_ref[...].astype(F32) * (sa * (1.0 - sa))
        dgb = dm * pb_ref[...].astype(F32) * (sb * (1.0 - sb))
        dga_ref[...] = dga.astype(BF16)
        dgb_ref[...] = dgb.astype(BF16)
        dbg_ref[0:1, :] += jnp.sum(dga, axis=0, keepdims=True)
        dbg_ref[1:2, :] += jnp.sum(dgb, axis=0, keepdims=True)

    out = jax.ShapeDtypeStruct((s, D), BF16)
    return pl.pallas_call(
        body, name="out_bwd_gates", grid=(s // tm,),
        in_specs=[_rows(tm, D), _full((D, D)), _rows(tm, D), _rows(tm, D), _rows(tm, D, 5), _rows(tm, D, 6),
                  _full((2, D))],
        out_specs=[_rows(tm, D)] * 4 + [_full((2, D))],
        out_shape=[out] * 4 + [jax.ShapeDtypeStruct((2, D), F32)], compiler_params=_params("arbitrary"),
    )(do, w_out, pa, pb, z, z, bg)


def gating_bwd(z, dya, ln_g, ln_b, w_s, bs_t, swapping):
    s = z.shape[0]
    ones = functools.partial(jnp.ones, (8, CHUNK), BF16)
    n = len(swapping)

    def body(*refs):
        u_ref, v_ref, dya_ref, lg_ref, lb_ref, ws_ref, bst_ref = refs[:7]
        du_ref, dv_ref, dws_ref, dbs_ref, dlg_ref, dlb_ref = refs[7 + n:13 + n]
        dvn_ref = refs[13 + 2 * n]
        ci = pl.program_id(0)
        if n:
            send, finish = _swap_phases(refs[7:7 + n], refs[13 + n:13 + 2 * n], *refs[14 + 2 * n:])
            pl.when(ci == 0)(send)

        @pl.when(ci == 0)
        def _():
            dws_ref[...] = jnp.zeros_like(dws_ref)
            dbs_ref[...] = jnp.zeros_like(dbs_ref)
            dlg_ref[...] = jnp.zeros_like(dlg_ref)
            dlb_ref[...] = jnp.zeros_like(dlb_ref)

        ug, dug_du = _gelu_and_grad(u_ref[...])
        vg, dvg_dv = _gelu_and_grad(v_ref[...])
        vhat, rstd = _layer_norm(vg)
        vn = (vhat * lg_ref[...] + lb_ref[...]).astype(BF16)
        dya = dya_ref[...]
        for g in range(GROUPS):
            cols = slice(g * CHUNK, (g + 1) * CHUNK)
            ws = _tril_ws(ws_ref, g)
            mixed = _dot(ws, vn[:, cols]) + bst_ref[:, g:g + 1]
            du_ref[:, cols] = (dya[:, cols] * mixed * dug_du[:, cols]).astype(BF16)
            dmix = (dya[:, cols] * ug[:, cols]).astype(BF16)
            dbs_ref[g] += _dot(ones(), dmix, NT)
            dws_ref[g] += _dot(dmix, vn[:, cols], NT)
            dvn_ref[:, cols] = _dot(ws, dmix, TN)
        dvn = dvn_ref[...]
        dlg_ref[...] += jnp.sum(dvn * vhat, axis=0, keepdims=True)
        dlb_ref[...] += jnp.sum(dvn, axis=0, keepdims=True)
        dvh = dvn * lg_ref[...]
        dvg = rstd * (dvh - jnp.mean(dvh, axis=-1, keepdims=True)
                      - vhat * jnp.mean(dvh * vhat, axis=-1, keepdims=True))
        dv_ref[...] = (dvg * dvg_dv).astype(BF16)

        @pl.when(ci == pl.num_programs(0) - 1)
        def _():
            r = lax.broadcasted_iota(jnp.int32, (CHUNK, CHUNK), 0)
            c = lax.broadcasted_iota(jnp.int32, (CHUNK, CHUNK), 1)
            for g in range(GROUPS):
                dws_ref[g] = jnp.where(c <= r, dws_ref[g], 0.0)

        if n:
            pl.when(ci == pl.num_programs(0) - 1)(finish)

    out = jax.ShapeDtypeStruct((s, D), BF16)
    res = pl.pallas_call(
        body, name="gating_bwd", grid=(s // CHUNK,),
        in_specs=[_rows(CHUNK, D, 0), _rows(CHUNK, D, 1), _rows(CHUNK, D), _full((1, D)), _full((1, D)),
                  _full((GROUPS, CHUNK, CHUNK)), _full((CHUNK, GROUPS))] + [ANY] * n,
        out_specs=[_rows(CHUNK, D), _rows(CHUNK, D), _full((GROUPS, CHUNK, CHUNK)), _full((GROUPS, 8, CHUNK)),
                   _full((1, D)), _full((1, D))] + [ANY] * n,
        out_shape=[out, out, jax.ShapeDtypeStruct((GROUPS, CHUNK, CHUNK), F32),
                   jax.ShapeDtypeStruct((GROUPS, 8, CHUNK), F32),
                   jax.ShapeDtypeStruct((1, D), F32), jax.ShapeDtypeStruct((1, D), F32)] + _swap_shapes(swapping),
        scratch_shapes=[pltpu.VMEM((CHUNK, D), F32)] + (_swap_sems(n) if n else []),
        compiler_params=_params("arbitrary", communicates=bool(n)),
    )(z, z, dya, ln_g, ln_b, w_s, bs_t, *swapping)
    return (*res[:6], res[6:])


def attn_bwd(z, yb, dyb, lse, logc, ka, kb, scattering):
    s = z.shape[0]
    nq = s // ATT_T
    t = ATT_T
    grp = ATT_BWD_GROUP
    ngrp = HEADS // 2 // grp
    wide = 128 * grp
    qcol, kcol, vcol = 2 * D // wide, 3 * D // wide, 4 * D // wide
    scale = 1.0 / math.sqrt(HEAD_DIM)
    n = len(scattering)

    def body(*refs):
        q_ref, k_ref, v_ref, y_ref, dy_ref, lse_ref, lc_ref, ka_ref, kb_ref = refs[:9]
        dq_ref, dk_ref, dv_ref = refs[9 + n:12 + n]
        qa_s, qt_s, da_s, dt_s, dq_s, dkt_s, dvt_s = refs[12 + 2 * n:19 + 2 * n]
        gi, j = pl.program_id(0), pl.program_id(1)
        first, lane, ones = _head_masks()
        if n:
            send, finish = _scatter_phases(refs[9:9 + n], refs[12 + n:12 + 2 * n], *refs[19 + 2 * n:])
            pl.when((gi == 0) & (j == 0))(send)

        @pl.when(j == 0)
        def _():
            dq_s[...] = jnp.zeros_like(dq_s)
            for pr in range(grp):
                cols = slice(pr * 128, (pr + 1) * 128)
                for ib in range(nq):
                    rows = slice(ib * t, (ib + 1) * t)
                    q = q_ref[rows, cols] * scale
                    lse = lse_ref[rows, cols]
                    qa_s[pr, 0, ib] = jnp.where(first, q, _place3(lane, HEAD_DIM + 2 * AUG, _split3(-lse[:, 0:1]),
                                                                  ones(0, 2 * AUG))).astype(BF16)
                    qa_s[pr, 1, ib] = jnp.where(
                        first, _place3(lane, 2 * AUG, _split3(-lse[:, HEAD_DIM:HEAD_DIM + 1]), ones(1, 2 * AUG)),
                        q).astype(BF16)
                    qt_s[pr, ib, :, 0:t] = jnp.where(first, q, 0.0).T.astype(BF16)
                    qt_s[pr, ib, :, t:2 * t] = jnp.where(first, 0.0, q).T.astype(BF16)
                    do = dy_ref[rows, cols]
                    prod = do * y_ref[rows, cols].astype(F32)
                    dd0 = jnp.sum(jnp.where(first, prod, 0.0), axis=-1, keepdims=True)
                    dd1 = jnp.sum(jnp.where(first, 0.0, prod), axis=-1, keepdims=True)
                    da_s[pr, 0, ib] = jnp.where(first, do, _place3(lane, HEAD_DIM, _split3(-dd0), 0.0)).astype(BF16)
                    da_s[pr, 1, ib] = jnp.where(first, _place3(lane, 0, _split3(-dd1), 0.0), do).astype(BF16)
                    dt_s[pr, ib, :, 0:t] = jnp.where(first, do, 0.0).T.astype(BF16)
                    dt_s[pr, ib, :, t:2 * t] = jnp.where(first, 0.0, do).T.astype(BF16)

        keys = []
        for pr in range(grp):
            kj = k_ref[:, pr * 128:(pr + 1) * 128]
            vj = v_ref[:, pr * 128:(pr + 1) * 128]
            keys.append((
                jnp.where(first, kj, ka_ref[pr, 0] + kb_ref[pr, 0, pl.ds(j, 1), :]).astype(BF16),
                jnp.where(first, ka_ref[pr, 1] + kb_ref[pr, 1, pl.ds(j, 1), :], kj).astype(BF16),
                jnp.concatenate([jnp.where(first, kj, 0.0), jnp.where(first, 0.0, kj)], axis=0).astype(BF16),
                jnp.where(first, vj, ones(0, AUG)).astype(BF16),
                jnp.where(first, ones(1, AUG), vj).astype(BF16)))
        dkt_s[...] = jnp.zeros_like(dkt_s)
        dvt_s[...] = jnp.zeros_like(dvt_s)

        def step(i, _):
            lc = lc_ref[i - j]
            rows = pl.ds(pl.multiple_of(i * t, t), t)
            for pr in range(grp):
                k0a, k1a, kst, v0a, v1a = keys[pr]
                p0 = jnp.exp(_dot(qa_s[pr, 0, i], k0a, NT) + lc)
                p1 = jnp.exp(_dot(qa_s[pr, 1, i], k1a, NT) + lc)
                e0 = (p0 * _dot(da_s[pr, 0, i], v0a, NT)).astype(BF16)
                e1 = (p1 * _dot(da_s[pr, 1, i], v1a, NT)).astype(BF16)
                dq_s[pr, rows, :] += _dot(jnp.concatenate([e0, e1], axis=1), kst)
                dvt_s[pr] += _dot(dt_s[pr, i], jnp.concatenate([p0.astype(BF16), p1.astype(BF16)], axis=0))
                dkt_s[pr] += _dot(qt_s[pr, i], jnp.concatenate([e0, e1], axis=0))
            return 0

        lax.fori_loop(j, nq, step, 0)
        for pr in range(grp):
            dk_ref[:, pr * 128:(pr + 1) * 128] = dkt_s[pr].T.astype(BF16)
            dv_ref[:, pr * 128:(pr + 1) * 128] = dvt_s[pr].T.astype(BF16)

        @pl.when(j == nq - 1)
        def _():
            for pr in range(grp):
                dq_ref[:, pr * 128:(pr + 1) * 128] = (dq_s[pr] * scale).astype(BF16)

        if n:
            pl.when((gi == ngrp - 1) & (j == nq - 1))(finish)

    colblock = lambda c: pl.BlockSpec((s, wide), lambda g, j: (0, c + g))
    blk = lambda c: pl.BlockSpec((t, wide), lambda g, j: (j, c + g))
    out = jax.ShapeDtypeStruct((s, D), BF16)
    res = pl.pallas_call(
        body, name="attn_bwd", grid=(ngrp, nq),
        in_specs=[colblock(qcol), blk(kcol), blk(vcol), colblock(0), colblock(0), colblock(0),
                  _full((nq, t, t)), pl.BlockSpec((grp, 2, t, 128), lambda g, j: (g, 0, 0, 0)),
                  pl.BlockSpec((grp, 2, nq, 128), lambda g, j: (g, 0, 0, 0))] + [ANY] * n,
        out_specs=[colblock(0), blk(0), blk(0)] + [ANY] * n, out_shape=[out] * 3 + _scatter_shapes(scattering),
        scratch_shapes=[pltpu.VMEM((grp, 2, nq, t, 128), BF16), pltpu.VMEM((grp, nq, 128, 2 * t), BF16),
                        pltpu.VMEM((grp, 2, nq, t, 128), BF16), pltpu.VMEM((grp, nq, 128, 2 * t), BF16),
                        pltpu.VMEM((grp, s, 128), F32), pltpu.VMEM((grp, 128, t), F32),
                        pltpu.VMEM((grp, 128, t), F32)]
        + (_scatter_sems(n) if n else []),
        compiler_params=_params("arbitrary", "arbitrary", communicates=bool(n)),
    )(z, z, z, yb, dyb, lse, logc, ka, kb, *scattering)
    return res[0], res[1], res[2], res[3:]


def in_bwd_norm(dz, wg, x, dx1, g_pre, scattering):
    s = x.shape[0]
    tm = 512
    n = len(scattering)

    def body(*refs):
        dz_ref, w_ref, x_ref, dx1_ref, g_ref = refs[:5]
        dx_ref, dg_ref = refs[5 + n:7 + n]
        acc_ref = refs[7 + 2 * n]
        i, k = pl.program_id(0), pl.program_id(1)
        if n:
            send, finish = _scatter_phases(refs[5:5 + n], refs[7 + n:7 + 2 * n], *refs[8 + 2 * n:])
            pl.when((i == 0) & (k == 0))(send)

        @pl.when((i == 0) & (k == 0))
        def _():
            dg_ref[...] = jnp.zeros_like(dg_ref)

        part = _dot(dz_ref[...], w_ref[...], NT)

        @pl.when(k == 0)
        def _():
            acc_ref[...] = part

        @pl.when(k > 0)
        def _():
            acc_ref[...] += part

        @pl.when(k == N_CHIPS - 1)
        def _():
            dh = acc_ref[...]
            xhat, r = _rms(x_ref[...])
            dg_ref[...] += jnp.sum(dh * xhat, axis=0, keepdims=True)
            dx_ref[...] = dx1_ref[...] + _rms_bwd(dh * g_ref[...], xhat, r)

        if n:
            pl.when((i == s // tm - 1) & (k == N_CHIPS - 1))(finish)

    row = pl.BlockSpec((tm, D), lambda i, k: (i, 0))
    vec = pl.BlockSpec((1, D), lambda i, k: (0, 0))
    res = pl.pallas_call(
        body, name="in_bwd_norm", grid=(s // tm, N_CHIPS),
        in_specs=[pl.BlockSpec((tm, IN_SHARD), lambda i, k: (i, k)),
                  pl.BlockSpec((None, D, IN_SHARD), lambda i, k: (k, 0, 0)), row, row, vec] + [ANY] * n,
        out_specs=[row, vec] + [ANY] * n,
        out_shape=[jax.ShapeDtypeStruct((s, D), F32), jax.ShapeDtypeStruct((1, D), F32)]
        + _scatter_shapes(scattering),
        scratch_shapes=[pltpu.VMEM((tm, D), F32)] + (_scatter_sems(n) if n else []),
        compiler_params=_params("arbitrary", "arbitrary", communicates=bool(n)),
    )(dz, wg, x, dx1, g_pre, *scattering)
    return res[0], res[1], res[2:]


def _adamw_math(w, g, m, v):
    m = ADAM_B1 * m + (1.0 - ADAM_B1) * g
    v = ADAM_B2 * v + (1.0 - ADAM_B2) * (g * g)
    m_hat = m / (1.0 - ADAM_B1 ** ADAM_STEP)
    v_hat = v / (1.0 - ADAM_B2 ** ADAM_STEP)
    delta = -ADAM_LR * (m_hat / (jnp.sqrt(v_hat) + ADAM_EPS) + ADAM_WD * w)
    return delta, m, v


def adamw(name, w, g, m, v, tr):
    r, c = w.shape

    def body(w_ref, g_ref, m_ref, v_ref, go_ref, d_ref, nm_ref, nv_ref):
        g = g_ref[...]
        go_ref[...] = g
        d_ref[...], nm_ref[...], nv_ref[...] = _adamw_math(w_ref[...], g, m_ref[...], v_ref[...])

    out = jax.ShapeDtypeStruct((r, c), F32)
    return pl.pallas_call(
        body, name=name, grid=(r // tr,), in_specs=[_rows(tr, c)] * 4, out_specs=[_rows(tr, c)] * 4,
        out_shape=[out] * 4, compiler_params=_params("parallel"),
    )(w, g, m, v)


def add_halves(name, g, recv, c_idx, tr):
    n, h, c = recv.shape

    def body(c_ref, g_ref, r_ref, o_ref):
        o_ref[...] = (g_ref[...] + r_ref[...]).astype(BF16)

    nb = h // tr
    return pl.pallas_call(
        body, name=name,
        grid_spec=pltpu.PrefetchScalarGridSpec(
            num_scalar_prefetch=1, grid=(n, nb),
            in_specs=[pl.BlockSpec((None, tr, c), lambda k, i, c_ref: (k, c_ref[0] * nb + i, 0)),
                      pl.BlockSpec((None, tr, c), lambda k, i, c_ref: (k, i, 0))],
            out_specs=pl.BlockSpec((None, tr, c), lambda k, i, c_ref: (k, i, 0))),
        out_shape=jax.ShapeDtypeStruct((n, h, c), BF16), compiler_params=_params("parallel", "parallel"),
    )(c_idx, g, recv)


def sum_chips(name, parts, recv, where, tr):
    n, h, c = recv.shape
    nb = h // tr

    def body(w_ref, p_ref, r_ref, o_ref):
        acc = p_ref[...].astype(F32)
        for k in range(n):
            acc = acc + r_ref[k].astype(F32)
        o_ref[...] = acc

    return pl.pallas_call(
        body, name=name,
        grid_spec=pltpu.PrefetchScalarGridSpec(
            num_scalar_prefetch=1, grid=(nb,),
            in_specs=[pl.BlockSpec((None, tr, c), lambda i, w_ref: (w_ref[0], i, 0)),
                      pl.BlockSpec((n, tr, c), lambda i, w_ref: (0, i, 0))],
            out_specs=pl.BlockSpec((tr, c), lambda i, w_ref: (w_ref[1] * nb + i, 0))),
        out_shape=jax.ShapeDtypeStruct((2 * h, c), F32), compiler_params=_params("parallel"),
    )(where, parts, recv)


def place_shard(name, shard, where, dtype, tr):
    r, c = shard.shape

    def body(w_ref, s_ref, o_ref):
        o_ref[...] = s_ref[...].astype(dtype)

    return pl.pallas_call(
        body, name=name,
        grid_spec=pltpu.PrefetchScalarGridSpec(
            num_scalar_prefetch=1, grid=(r // tr,),
            in_specs=[pl.BlockSpec((tr, c), lambda i, w_ref: (i, 0))],
            out_specs=pl.BlockSpec((None, tr, c), lambda i, w_ref: (w_ref[0], i, 0))),
        out_shape=jax.ShapeDtypeStruct((N_CHIPS, r, c), dtype), compiler_params=_params("parallel"),
    )(where, shard)


ANY = pl.BlockSpec(memory_space=pl.ANY)


def _place():
    x, y, c = lax.axis_index("x"), lax.axis_index("y"), lax.axis_index("c")
    chips = [(1 - x, y), (x, 1 - y), (1 - x, 1 - y)]
    return x, y, c, chips


def gather_shards(arrays):
    n = len(arrays)

    def body(*refs):
        send, pass_on, finish = _gather_phases(refs[n:2 * n], *refs[2 * n:])
        send()
        pass_on()
        finish()

    return pl.pallas_call(
        body, name="gather_shards", in_specs=[ANY] * n, out_specs=[ANY] * n,
        out_shape=[jax.ShapeDtypeStruct(a.shape, a.dtype) for a in arrays],
        input_output_aliases={w: w for w in range(n)}, scratch_shapes=_gather_sems(n),
        compiler_params=pltpu.CompilerParams(has_side_effects=True),
    )(*arrays)


def _gather_sems(n):
    return [pltpu.SemaphoreType.DMA((6 * n,)), pltpu.SemaphoreType.DMA((6 * n,))]


def _gather_phases(out, send_sems, recv_sems):
    n = len(out)
    x, y, c, chips = _place()
    me = 2 * x + y
    sibling = (x, y, 1 - c)

    def half(w, chip, core):
        h = out[w].shape[1] // 2
        return out[w].at[chip, pl.ds(core * h, h)]

    def copy(k, block, to):
        return pltpu.make_async_remote_copy(src_ref=block, dst_ref=block, send_sem=send_sems.at[k],
                                            recv_sem=recv_sems.at[k], device_id=to, device_id_type=MESH)

    def over_ici(w, j, chip):
        return copy(3 * w + j, half(w, chip, c), (chips[j][0], chips[j][1], c))

    def over_d2d(w, j, core):
        return copy(3 * n + 3 * w + j, half(w, 2 * chips[j][0] + chips[j][1], core), sibling)

    pairs = [(w, j) for w in range(n) for j in range(3)]

    def send():
        for w, j in pairs:
            over_ici(w, j, me).start()

    def pass_on():
        for w, j in pairs:
            over_ici(w, j, 2 * chips[j][0] + chips[j][1]).wait_recv()
            over_d2d(w, j, c).start()

    def finish():
        for w, j in pairs:
            over_d2d(w, j, 1 - c).wait_recv()
        for w, j in pairs:
            over_ici(w, j, me).wait_send()
            over_d2d(w, j, c).wait_send()

    return send, pass_on, finish


def swap_halves(name, grads):
    n = len(grads)

    def body(*refs):
        send, finish = _swap_phases(refs[:n], refs[n:2 * n], *refs[2 * n:])
        send()
        finish()

    return pl.pallas_call(
        body, name=name, in_specs=[ANY] * n, out_specs=[ANY] * n, out_shape=_swap_shapes(grads),
        scratch_shapes=_swap_sems(n), compiler_params=pltpu.CompilerParams(has_side_effects=True),
    )(*grads)


def _swap_shapes(grads):
    return [jax.ShapeDtypeStruct((a.shape[0], a.shape[1] // 2, a.shape[2]), a.dtype) for a in grads]


def _swap_sems(n):
    return [pltpu.SemaphoreType.DMA((n,)), pltpu.SemaphoreType.DMA((n,))]


def _swap_phases(g, out, send_sems, recv_sems):
    x, y, c, _ = _place()

    def copies():
        return [pltpu.make_async_remote_copy(
            src_ref=g[w].at[:, pl.ds((1 - c) * (g[w].shape[1] // 2), g[w].shape[1] // 2)], dst_ref=out[w],
            send_sem=send_sems.at[w], recv_sem=recv_sems.at[w], device_id=(x, y, 1 - c), device_id_type=MESH)
            for w in range(len(g))]

    def send():
        for cp in copies():
            cp.start()

    def finish():
        for cp in copies():
            cp.wait()

    return send, finish


def _send_phases(g, out, send_sems, recv_sems):
    x, y, c, _ = _place()

    def copies():
        return [pltpu.make_async_remote_copy(
            src_ref=g[w], dst_ref=out[w], send_sem=send_sems.at[w], recv_sem=recv_sems.at[w],
            device_id=(x, y, 1 - c), device_id_type=MESH) for w in range(len(g))]

    def send():
        for cp in copies():
            cp.start()

    def finish():
        for cp in copies():
            cp.wait()

    return send, finish


def dw_in_half(name, h, dz, which, sending):
    s = h.shape[0]
    hh, tb = D // 2, IN_SHARD // 2
    n = len(sending)
    steps = IN_COLS // tb

    def body(w_ref, *refs):
        a_ref, b_ref, o_ref = refs[0], refs[1], refs[2 + n]
        j = pl.program_id(0)
        if n:
            send, finish = _send_phases(refs[2:2 + n], refs[3 + n:3 + 2 * n], *refs[3 + 2 * n:])
            pl.when(j == 0)(send)
        o_ref[...] = _dot(a_ref[...], b_ref[...], TN)
        if n:
            pl.when(j == steps - 1)(finish)

    out = pl.pallas_call(
        body, name=name,
        grid_spec=pltpu.PrefetchScalarGridSpec(
            num_scalar_prefetch=1, grid=(steps,),
            in_specs=[pl.BlockSpec((s, hh), lambda j, w: (0, w[0])), pl.BlockSpec((s, tb), lambda j, w: (0, j))]
            + [ANY] * n,
            out_specs=[pl.BlockSpec((None, hh, tb), lambda j, w: (j // 2, 0, j % 2))] + [ANY] * n,
            scratch_shapes=_swap_sems(n) if n else []),
        out_shape=[jax.ShapeDtypeStruct((N_CHIPS, hh, IN_SHARD), F32)]
        + [jax.ShapeDtypeStruct(a.shape, a.dtype) for a in sending],
        compiler_params=_params("arbitrary", communicates=bool(n)),
    )(which, h, dz, *sending)
    return out[0], out[1:]


def scatter_chips(parts):
    n = len(parts)

    def body(*refs):
        send, finish = _scatter_phases(refs[:n], refs[n:2 * n], *refs[2 * n:])
        send()
        finish()

    return pl.pallas_call(
        body, name="scatter_chips", in_specs=[ANY] * n, out_specs=[ANY] * n,
        out_shape=_scatter_shapes(parts), scratch_shapes=_scatter_sems(n),
        compiler_params=pltpu.CompilerParams(has_side_effects=True),
    )(*parts)


def _scatter_shapes(parts):
    return [jax.ShapeDtypeStruct((3,) + a.shape[1:], a.dtype) for a in parts]


def _scatter_sems(n):
    return [pltpu.SemaphoreType.DMA((3 * n,)), pltpu.SemaphoreType.DMA((3 * n,))]


def _scatter_phases(p, out, send_sems, recv_sems):
    x, y, c, chips = _place()

    def copies():
        return [pltpu.make_async_remote_copy(
            src_ref=p[w].at[2 * px + py], dst_ref=out[w].at[j], send_sem=send_sems.at[3 * w + j],
            recv_sem=recv_sems.at[3 * w + j], device_id=(px, py, c), device_id_type=MESH)
            for w in range(len(p)) for j, (px, py) in enumerate(chips)]

    def send():
        for cp in copies():
            cp.start()

    def finish():
        for cp in copies():
            cp.wait()

    return send, finish


def join_halves(arrays):
    n = len(arrays)

    def body(*refs):
        out = refs[n:2 * n]
        send_sems, recv_sems = refs[2 * n:]
        x, y, c, _ = _place()

        def copy(w, core):
            h = out[w].shape[0] // 2
            rows = out[w].at[pl.ds(core * h, h)]
            return pltpu.make_async_remote_copy(
                src_ref=rows, dst_ref=rows, send_sem=send_sems.at[w], recv_sem=recv_sems.at[w],
                device_id=(x, y, 1 - c), device_id_type=MESH)

        for w in range(n):
            copy(w, c).start()
        for w in range(n):
            copy(w, 1 - c).wait_recv()
        for w in range(n):
            copy(w, c).wait_send()

    return pl.pallas_call(
        body, name="join_halves", in_specs=[ANY] * n, out_specs=[ANY] * n,
        out_shape=[jax.ShapeDtypeStruct(a.shape, a.dtype) for a in arrays],
        input_output_aliases={w: w for w in range(n)},
        scratch_shapes=[pltpu.SemaphoreType.DMA((n,)), pltpu.SemaphoreType.DMA((n,))],
        compiler_params=pltpu.CompilerParams(has_side_effects=True),
    )(*arrays)


def allreduce_small(packed):
    r, c = packed.shape
    n_dev = 8

    def body(x_ref, all_ref, sum_ref, send_sems, recv_sems, local_sem):
        x, y, cc, chips = _place()
        me, sibling = (x, y, cc), (x, y, 1 - cc)

        def rows(px, py, pc):
            return all_ref.at[4 * px + 2 * py + pc]

        def copy(k, block, to, src=None):
            return pltpu.make_async_remote_copy(
                src_ref=rows(*block) if src is None else src, dst_ref=rows(*block), send_sem=send_sems.at[k],
                recv_sem=recv_sems.at[k], device_id=to, device_id_type=MESH)

        mine = pltpu.make_async_copy(x_ref, rows(*me), local_sem)
        mine.start()
        first = [copy(0, me, sibling, src=x_ref)]
        first += [copy(1 + j, me, (*chip, cc), src=x_ref) for j, chip in enumerate(chips)]
        for cp in first:
            cp.start()
        passed = [copy(4 + j, (*chip, cc), sibling) for j, chip in enumerate(chips)]
        for j, chip in enumerate(chips):
            copy(1 + j, (*chip, cc), me).wait_recv()
            passed[j].start()
        copy(0, sibling, me).wait_recv()
        for j, chip in enumerate(chips):
            copy(4 + j, (*chip, 1 - cc), me).wait_recv()
        for cp in first + passed:
            cp.wait_send()
        mine.wait()
        acc = all_ref[0]
        for k in range(1, n_dev):
            acc = acc + all_ref[k]
        sum_ref[...] = acc

    vm = pl.BlockSpec(memory_space=pltpu.VMEM)
    return pl.pallas_call(
        body, name="allreduce_small", in_specs=[vm], out_specs=[vm, vm],
        out_shape=[jax.ShapeDtypeStruct((n_dev, r, c), F32), jax.ShapeDtypeStruct((r, c), F32)],
        scratch_shapes=[pltpu.SemaphoreType.DMA((7,)), pltpu.SemaphoreType.DMA((7,)), pltpu.SemaphoreType.DMA],
        compiler_params=pltpu.CompilerParams(has_side_effects=True, vmem_limit_bytes=VMEM_LIMIT),
    )(packed)[1]


def local_step(x, target, vecs, w_s, bs_t, bg, wg_in, late, core=None):
    on_mesh = core is not None

    def add(names, grads, recv):
        return [add_halves("add_" + n, g, r, core, min(r.shape[1], 256)) for n, g, r in zip(names, grads, recv)]

    g_pre, ln_g, ln_b, g_post, g_fpre, g_fpost = vecs
    s = x.shape[0]
    logc = _attn_tables(s)
    ka, kb = _alibi_tables(s)

    h = norm_pre(x, g_pre)
    wg_a, wg_b, wg_out, wg_ff1, wg_ff2 = late
    z, got = mm_in(h, wg_in, [wg_ff2] if on_mesh else [])
    wg_ff2 = got[0] if on_mesh else wg_ff2
    ya, got = gating_fwd(z, ln_g, ln_b, w_s, bs_t, [wg_out] if on_mesh else [])
    wg_out = got[0] if on_mesh else wg_out
    yb, lse, got = attn_fwd(z, logc, ka, kb, [wg_a, wg_b, wg_ff1] if on_mesh else [])
    wg_a, wg_b, wg_ff1 = got if on_mesh else (wg_a, wg_b, wg_ff1)
    w_a, w_b, w_out, w_ff2 = wg_a.reshape(D, D), wg_b.reshape(D, D), wg_out.reshape(D, D), wg_ff2.reshape(D_FF, D)
    merged, pa, pb = proj_merge(ya, yb, w_a, w_b, z, bg)
    o, x1, h2 = out_norm(merged, w_out, x, g_post, g_fpre)
    a, rl = mm_ff1(h2, wg_ff1)
    dy, df, d_gfpost, loss = ff2_loss(rl, w_ff2, x1, target, g_fpost)

    half_cols = pl.BlockSpec((D, D // 2), lambda i, j: (0, j))
    d_wff2 = mm_tn("dw_ff2", rl, df, D // 2, D, (D_FF, D), pl.BlockSpec((D // 2, D), lambda i, j: (i, 0)))
    da = ff2_bwd(df, w_ff2, a)
    d_wff1 = mm_tn("dw_ff1", h2, da, D, D // 2, (N_CHIPS, D, D),
                   pl.BlockSpec((None, D, D // 2), lambda i, j: (j // 2, 0, j % 2)))
    d_ff = [d_wff1, d_wff2.reshape(N_CHIPS, D, D)]
    dx1, do, d_gfpre, d_gpost, recv_ff = ff1_bwd_norms(da, wg_ff1, x1, o, dy, g_fpre, g_post, d_ff if on_mesh else [])
    d_wout = mm_tn("dw_out", merged, do, D, D // 2, (D, D), half_cols)
    dpa, dpb, dga, dgb, d_bg = out_bwd_gates(do, w_out, pa, pb, z, bg)
    d_wa = mm_tn("dw_a", ya, dpa, D, D // 2, (D, D), half_cols)
    d_wb = mm_tn("dw_b", yb, dpb, D, D // 2, (D, D), half_cols)
    dya = mm_nt("dy_a", dpa, w_a)
    dyb = mm_nt("dy_b", dpb, w_b)
    d_proj = [d_wa.reshape(N_CHIPS, D // N_CHIPS, D), d_wb.reshape(N_CHIPS, D // N_CHIPS, D),
              d_wout.reshape(N_CHIPS, D // N_CHIPS, D)]
    du, dv, d_ws, d_bs, d_lng, d_lnb, recv_proj = gating_bwd(z, dya, ln_g, ln_b, w_s, bs_t, d_proj if on_mesh else [])
    early = d_proj + d_ff
    parts_early = add(BIG[1:], early, list(recv_proj) + list(recv_ff)) if on_mesh else []
    dq, dk, dvb, got_early = attn_bwd(z, yb, dyb, lse, logc, ka, kb, parts_early)
    dz = jnp.concatenate([du, dv, dq, dk, dvb, dga, dgb], axis=1)
    if on_mesh:
        for_sibling, _ = dw_in_half("dw_in_sibling", h, dz, 1 - core, [])
        mine, from_sibling = dw_in_half("dw_in_mine", h, dz, core, [for_sibling])
        d_win = None
        parts_late = [add_halves("add_w_in", mine, from_sibling[0], jnp.zeros((1,), jnp.int32), 256)]
    else:
        half = IN_SHARD // 2
        d_win = mm_tn("dw_in", h, dz, D, half, (N_CHIPS, D, IN_SHARD),
                      pl.BlockSpec((None, D, half), lambda i, j: (j // 2, 0, j % 2)))
        parts_late = []
    dx, d_gpre, got_late = in_bwd_norm(dz, wg_in, x, dx1, g_pre, parts_late)

    small = dict(norm_mix_pre=d_gpre, b_gate=d_bg, ln_v_g=d_lng, ln_v_b=d_lnb, w_s=d_ws, b_s=d_bs[:, 0, :],
                 norm_mix_post=d_gpost, norm_ffn_pre=d_gfpre, norm_ffn_post=d_gfpost)
    return loss[0, 0], dx, [d_win] + early, small, parts_late + parts_early, list(got_late) + list(got_early)


BIG = ("w_in", "w_a_proj", "w_b_proj", "w_out", "w_ff1", "w_ff2")
SMALL = ("norm_mix_pre", "ln_v_g", "ln_v_b", "b_s", "norm_mix_post", "norm_ffn_pre", "norm_ffn_post", "w_s", "b_gate")
ORDER = ("norm_mix_pre", "w_in", "b_gate", "ln_v_g", "ln_v_b", "w_s", "b_s", "w_a_proj", "w_b_proj", "w_out",
         "norm_mix_post", "norm_ffn_pre", "w_ff1", "w_ff2", "norm_ffn_post")
VEC_ROWS = D // 128
WS_ROW = 7 * VEC_ROWS
BG_ROW = WS_ROW + GROUPS * CHUNK
LOSS_ROW = BG_ROW + 2 * VEC_ROWS
PACK_ROWS = LOSS_ROW + 8


def pack_small(small, loss):
    vectors = [small[n] for n in SMALL[:7]]

    def body(*refs):
        out = refs[-1]
        ws_ref, bg_ref, loss_ref = refs[7:10]
        for i, n in enumerate(SMALL[:7]):
            if n == "b_s":
                out[i * VEC_ROWS:(i + 1) * VEC_ROWS, :] = refs[i][...]
            else:
                for j in range(VEC_ROWS):
                    out[i * VEC_ROWS + j:i * VEC_ROWS + j + 1, :] = refs[i][:, j * 128:(j + 1) * 128]
        for g in range(GROUPS):
            out[WS_ROW + g * CHUNK:WS_ROW + (g + 1) * CHUNK, :] = ws_ref[g]
        for r in range(2):
            for j in range(VEC_ROWS):
                row = BG_ROW + r * VEC_ROWS + j
                out[row:row + 1, :] = bg_ref[r:r + 1, j * 128:(j + 1) * 128]
        lane = lax.broadcasted_iota(jnp.int32, (8, 128), 1)
        sub = lax.broadcasted_iota(jnp.int32, (8, 128), 0)
        out[LOSS_ROW:LOSS_ROW + 8, :] = jnp.where((lane == 0) & (sub == 0), loss_ref[...], 0.0)

    return pl.pallas_call(
        body, name="pack_small", out_shape=jax.ShapeDtypeStruct((PACK_ROWS, 128), F32),
        compiler_params=_params(),
    )(*vectors, small["w_s"], small["b_gate"], loss)


def adamw_small(summed, chip, w, m, v):
    shapes = {n: (1, D) for n in SMALL}
    shapes.update(b_s=(GROUPS, CHUNK), w_s=(GROUPS * CHUNK, CHUNK), b_gate=(2, D // N_CHIPS))
    flat = lambda t: [t[n].reshape(shapes[n]) for n in SMALL]
    per = D // N_CHIPS // 128

    def body(chip_ref, sum_ref, *refs):
        params, outs = refs[:27], refs[27:]
        sub = lax.broadcasted_iota(jnp.int32, (VEC_ROWS, 128), 0)

        def gate_row(r):
            rows = sum_ref[BG_ROW + r * VEC_ROWS:BG_ROW + (r + 1) * VEC_ROWS, :]
            return jnp.concatenate([jnp.sum(jnp.where(sub == per * chip_ref[0] + j, rows, 0.0), axis=0, keepdims=True)
                                    for j in range(per)], axis=1)

        for i, n in enumerate(SMALL):
            if n == "b_s":
                g = sum_ref[i * VEC_ROWS:(i + 1) * VEC_ROWS, :]
            elif n == "w_s":
                g = sum_ref[WS_ROW:BG_ROW, :]
            elif n == "b_gate":
                g = jnp.concatenate([gate_row(0), gate_row(1)], axis=0)
            else:
                g = jnp.concatenate([sum_ref[i * VEC_ROWS + j:i * VEC_ROWS + j + 1, :] for j in range(VEC_ROWS)],
                                    axis=1)
            delta, nm, nv = _adamw_math(params[i][...], g, params[9 + i][...], params[18 + i][...])
            outs[4 * i][...], outs[4 * i + 1][...], outs[4 * i + 2][...], outs[4 * i + 3][...] = g, delta, nm, nv

    vm = pl.BlockSpec(memory_space=pltpu.VMEM)
    res = pl.pallas_call(
        body, name="adamw_small",
        in_specs=[pl.BlockSpec(memory_space=pltpu.SMEM)] + [vm] * 28, out_specs=[vm] * 36,
        out_shape=[jax.ShapeDtypeStruct(shapes[n], F32) for n in SMALL for _ in range(4)],
        compiler_params=_params(),
    )(chip, summed, *flat(w), *flat(m), *flat(v))
    return {n: tuple(r.reshape(w[n].shape) for r in res[4 * i:4 * i + 4]) for i, n in enumerate(SMALL)}


def kernel(x, norm_mix_pre, w_in, b_gate, ln_v_g, ln_v_b, w_s, b_s, w_a_proj, w_b_proj, w_out, norm_mix_post, norm_ffn_pre, w_ff1, w_ff2, norm_ffn_post, loss_target, m_norm_mix_pre, m_w_in, m_b_gate, m_ln_v_g, m_ln_v_b, m_w_s, m_b_s, m_w_a_proj, m_w_b_proj, m_w_out, m_norm_mix_post, m_norm_ffn_pre, m_w_ff1, m_w_ff2, m_norm_ffn_post, v_norm_mix_pre, v_w_in, v_b_gate, v_ln_v_g, v_ln_v_b, v_w_s, v_b_s, v_w_a_proj, v_w_b_proj, v_w_out, v_norm_mix_post, v_norm_ffn_pre, v_w_ff1, v_w_ff2, v_norm_ffn_post):
    w = dict(norm_mix_pre=norm_mix_pre, w_in=w_in, b_gate=b_gate, ln_v_g=ln_v_g, ln_v_b=ln_v_b, w_s=w_s, b_s=b_s,
             w_a_proj=w_a_proj, w_b_proj=w_b_proj, w_out=w_out, norm_mix_post=norm_mix_post,
             norm_ffn_pre=norm_ffn_pre, w_ff1=w_ff1, w_ff2=w_ff2, norm_ffn_post=norm_ffn_post)
    m = dict(norm_mix_pre=m_norm_mix_pre, w_in=m_w_in, b_gate=m_b_gate, ln_v_g=m_ln_v_g, ln_v_b=m_ln_v_b, w_s=m_w_s,
             b_s=m_b_s, w_a_proj=m_w_a_proj, w_b_proj=m_w_b_proj, w_out=m_w_out, norm_mix_post=m_norm_mix_post,
             norm_ffn_pre=m_norm_ffn_pre, w_ff1=m_w_ff1, w_ff2=m_w_ff2, norm_ffn_post=m_norm_ffn_post)
    v = dict(norm_mix_pre=v_norm_mix_pre, w_in=v_w_in, b_gate=v_b_gate, ln_v_g=v_ln_v_g, ln_v_b=v_ln_v_b, w_s=v_w_s,
             b_s=v_b_s, w_a_proj=v_w_a_proj, w_b_proj=v_w_b_proj, w_out=v_w_out, norm_mix_post=v_norm_mix_post,
             norm_ffn_pre=v_norm_ffn_pre, w_ff1=v_w_ff1, w_ff2=v_w_ff2, norm_ffn_post=v_norm_ffn_post)
    chip = 2 * lax.axis_index("x") + lax.axis_index("y")
    core = lax.axis_index("c")

    where = jnp.stack([chip, core]).astype(jnp.int32)
    placed = [place_shard("place_" + n, w[n][0], where, BF16, min(w[n].shape[1], 256)) for n in BIG]
    placed.append(place_shard("place_b_gate", jnp.pad(b_gate[0], ((0, 14), (0, 0))), where, F32, 16))
    wg_in, bg_all = gather_shards([placed[0], placed[6]])
    bg = jnp.transpose(bg_all[:, :2, :], (1, 0, 2)).reshape(2, D)
    vecs = (norm_mix_pre, ln_v_g, ln_v_b, norm_mix_post, norm_ffn_pre, norm_ffn_post)
    loss, dx, _, small, parts, got = local_step(
        x[0], loss_target[0], vecs, w_s[0], b_s[0].T, bg, wg_in, placed[1:6],
        core=jnp.reshape(core, (1,)).astype(jnp.int32))

    halves = [sum_chips("sum_" + n, p, r, where, min(p.shape[1], 256)) for n, p, r in zip(BIG, parts, got)]
    grads = dict(zip(BIG, join_halves(halves)))

    summed = allreduce_small(pack_small(small, loss.reshape(1, 1)))
    loss = summed[LOSS_ROW, 0]

    new = adamw_small(summed, jnp.reshape(chip, (1,)).astype(jnp.int32), w, m, v)
    for n in BIG:
        shape = w[n].shape
        res = adamw("adamw_" + n, w[n][0], grads[n], m[n][0], v[n][0], min(shape[1], 256))
        new[n] = tuple(r.reshape(shape) for r in res)

    outs = [loss, dx[None]]
    for i in range(4):
        outs += [new[n][i] for n in ORDER]
    return tuple(outs)
```

```python
import functools
import math
import typing

import numpy as np
import jax
import jax.numpy as jnp
from jax import lax
from jax.experimental import pallas as pl
from jax.experimental.pallas import tpu as pltpu

F32 = jnp.float32
BF16 = jnp.bfloat16
MESH = pl.DeviceIdType.MESH

D = 1024
EPS = 1e-6
CHUNK = 128
GROUPS = 8
HEADS = 16
HEAD_DIM = 64
ATT_T = 256
ATT_GROUP = 4
ATT_BWD_GROUP = 2
N_CHIPS = 4
D_FF = 4 * D
IN_COLS = 7 * D
IN_SHARD = IN_COLS // N_CHIPS
MASKED = -1e30
VMEM_LIMIT = 56 * 2 ** 20

ADAM_LR, ADAM_B1, ADAM_B2, ADAM_EPS, ADAM_WD, ADAM_STEP = 0.001, 0.9, 0.999, 1e-08, 0.01, 10

NN = (((1,), (0,)), ((), ()))
NT = (((1,), (1,)), ((), ()))
TN = (((0,), (0,)), ((), ()))


def _dot(a, b, dims=NN):
    return lax.dot_general(a, b, dims, preferred_element_type=F32)


def _params(*sem, communicates=False):
    return pltpu.CompilerParams(dimension_semantics=sem or None, vmem_limit_bytes=VMEM_LIMIT,
                                has_side_effects=communicates)


def _rows(tr, c, col=0):
    return pl.BlockSpec((tr, c), lambda i: (i, col))


def _full(shape):
    n = len(shape)
    return pl.BlockSpec(shape, lambda *_: (0,) * n)


def _gelu(x):
    k = math.sqrt(2.0 / math.pi)
    return 0.5 * x * (1.0 + jnp.tanh(k * (x + 0.044715 * x * x * x)))


def _gelu_and_grad(x):
    k = math.sqrt(2.0 / math.pi)
    t = jnp.tanh(k * (x + 0.044715 * x * x * x))
    g = 0.5 * x * (1.0 + t)
    dg = 0.5 * (1.0 + t) + 0.5 * x * (1.0 - t * t) * (k * (1.0 + 3.0 * 0.044715 * x * x))
    return g, dg


def _sigmoid(x):
    return 1.0 / (1.0 + jnp.exp(-x))


def _rms(x):
    r = lax.rsqrt(jnp.mean(x * x, axis=-1, keepdims=True) + EPS)
    return x * r, r


def _rms_bwd(dn, xhat, r):
    return r * (dn - xhat * jnp.mean(dn * xhat, axis=-1, keepdims=True))


def norm_pre(x, g):
    s = x.shape[0]
    tr = 512

    def body(x_ref, g_ref, h_ref):
        xhat, _ = _rms(x_ref[...])
        h_ref[...] = (xhat * g_ref[...]).astype(BF16)

    return pl.pallas_call(
        body, name="norm_pre", grid=(s // tr,),
        in_specs=[_rows(tr, D), _full((1, D))], out_specs=_rows(tr, D),
        out_shape=jax.ShapeDtypeStruct((s, D), BF16), compiler_params=_params("parallel"),
    )(x, g)


def mm_in(h, wg, gathering):
    s = h.shape[0]
    tm, tn = 1024, IN_SHARD // 2
    per = IN_SHARD // tn
    n = len(gathering)
    ni, nj = s // tm, IN_COLS // tn

    def body(*refs):
        a_ref, b_ref, o_ref = refs[0], refs[1], refs[2 + n]
        i, j = pl.program_id(0), pl.program_id(1)
        if n:
            send, pass_on, finish = _gather_phases(refs[3 + n:3 + 2 * n], *refs[3 + 2 * n:], _spans(gathering))
            pl.when((i == 0) & (j == 0))(send)
            pl.when((i == ni - 1) & (j == nj // 2))(pass_on)
        o_ref[...] = _dot(a_ref[...], b_ref[...]).astype(BF16)
        if n:
            pl.when((i == ni - 1) & (j == nj - 1))(finish)

    out = pl.pallas_call(
        body, name="mm_in", grid=(ni, nj),
        in_specs=[pl.BlockSpec((tm, D), lambda i, j: (i, 0)),
                  pl.BlockSpec((None, D, tn), lambda i, j: (j // per, 0, j % per))] + [ANY] * n,
        out_specs=[pl.BlockSpec((tm, tn), lambda i, j: (i, j))] + [ANY] * n,
        out_shape=[jax.ShapeDtypeStruct((s, IN_COLS), BF16)]
        + [jax.ShapeDtypeStruct(a.shape, a.dtype) for a in _arrays(gathering)],
        input_output_aliases={2 + w: 1 + w for w in range(n)},
        scratch_shapes=_gather_sems(n) if n else [],
        compiler_params=_params("arbitrary", "arbitrary", communicates=bool(n)),
    )(h, wg, *_arrays(gathering))
    return out[0], out[1:]


def _tril_ws(ws_ref, g):
    r = lax.broadcasted_iota(jnp.int32, (CHUNK, CHUNK), 0)
    c = lax.broadcasted_iota(jnp.int32, (CHUNK, CHUNK), 1)
    return jnp.where(c <= r, ws_ref[g], 0.0).astype(BF16)


def _layer_norm(v):
    mu = jnp.mean(v, axis=-1, keepdims=True)
    d = v - mu
    rstd = lax.rsqrt(jnp.mean(d * d, axis=-1, keepdims=True) + EPS)
    return d * rstd, rstd


def gating_fwd(z, ln_g, ln_b, w_s, bs_t, gathering):
    s = z.shape[0]
    n = len(gathering)
    steps = s // CHUNK

    def body(*refs):
        u_ref, v_ref, lg_ref, lb_ref, ws_ref, bst_ref = refs[:6]
        ya_ref = refs[6 + n]
        ci = pl.program_id(0)
        if n:
            send, pass_on, finish = _gather_phases(refs[7 + n:7 + 2 * n], *refs[7 + 2 * n:], _spans(gathering))
            pl.when(ci == 0)(send)
            pl.when(ci == steps * 3 // 4)(pass_on)
        ug = _gelu(u_ref[...].astype(F32))
        vhat, _ = _layer_norm(_gelu(v_ref[...].astype(F32)))
        vn = (vhat * lg_ref[...] + lb_ref[...]).astype(BF16)
        for g in range(GROUPS):
            cols = slice(g * CHUNK, (g + 1) * CHUNK)
            mixed = _dot(_tril_ws(ws_ref, g), vn[:, cols]) + bst_ref[:, g:g + 1]
            ya_ref[:, cols] = (ug[:, cols] * mixed).astype(BF16)
        if n:
            pl.when(ci == steps - 1)(finish)

    out = pl.pallas_call(
        body, name="gating_fwd", grid=(steps,),
        in_specs=[_rows(CHUNK, D, 0), _rows(CHUNK, D, 1), _full((1, D)), _full((1, D)),
                  _full((GROUPS, CHUNK, CHUNK)), _full((CHUNK, GROUPS))] + [ANY] * n,
        out_specs=[_rows(CHUNK, D)] + [ANY] * n,
        out_shape=[jax.ShapeDtypeStruct((s, D), BF16)]
        + [jax.ShapeDtypeStruct(a.shape, a.dtype) for a in _arrays(gathering)],
        input_output_aliases={6 + w: 1 + w for w in range(n)},
        scratch_shapes=_gather_sems(n) if n else [],
        compiler_params=_params("arbitrary", communicates=bool(n)),
    )(z, z, ln_g, ln_b, w_s, bs_t, *_arrays(gathering))
    return out[0], out[1:]


def _attn_tables(s):
    nd = s // ATT_T
    r = np.arange(ATT_T)[None, :, None]
    c = np.arange(ATT_T)[None, None, :]
    delta = np.arange(nd)[:, None, None] * ATT_T + r - c
    count = np.zeros(delta.shape, np.int64)
    for window, dilation in ((128, 1), (512, 4), (2048, 16)):
        count += (delta >= 0) & (delta % dilation == 0) & (delta <= window)
    logc = np.where(count > 0, np.log(np.maximum(count, 1)), MASKED)
    return jnp.asarray(logc, F32)


AUG = 3


def _split3_np(x):
    terms, rest = [], np.asarray(x, np.float64)
    for _ in range(AUG):
        term = np.asarray(rest.astype(jnp.bfloat16), np.float64)
        terms.append(term)
        rest = rest - term
    return terms


def _split3(x):
    terms, rest = [], x
    for _ in range(AUG):
        term = rest.astype(BF16).astype(F32)
        terms.append(term)
        rest = rest - term
    return terms


def _alibi_tables(s):
    nb = s // ATT_T
    slopes = np.exp2(-8.0 * np.arange(1, HEADS + 1, dtype=np.float64) / HEADS)
    ka = np.zeros((HEADS // 2, 2, ATT_T, 128), np.float32)
    kb = np.zeros((HEADS // 2, 2, nb, 128), np.float32)
    for p in range(HEADS // 2):
        for e in range(2):
            base = HEAD_DIM * (1 - e)
            for a, term in enumerate(_split3_np(slopes[2 * p + e] * np.arange(ATT_T))):
                ka[p, e, :, base + a] = term
            for a, term in enumerate(_split3_np(slopes[2 * p + e] * ATT_T * np.arange(nb))):
                kb[p, e, :, base + AUG + a] = term
            ka[p, e, :, base + 2 * AUG:base + 3 * AUG] = 1.0
    return jnp.asarray(ka), jnp.asarray(kb)


def _head_masks():
    lane = lax.broadcasted_iota(jnp.int32, (1, 128), 1)
    first = lane < HEAD_DIM

    def ones(e, n):
        base = HEAD_DIM * (1 - e)
        return ((lane >= base) & (lane < base + n)).astype(F32)

    return first, lane, ones


def _place3(lane, at, terms, other):
    for a, term in enumerate(terms):
        other = jnp.where(lane == at + a, term, other)
    return other


def attn_fwd(z, logc, ka, kb, gathering):
    s = z.shape[0]
    nq = s // ATT_T
    t = ATT_T
    n = len(gathering)
    grp = ATT_GROUP
    ngrp = HEADS // 2 // grp
    wide = 128 * grp
    qcol, kcol, vcol = 2 * D // wide, 3 * D // wide, 4 * D // wide

    def body(*refs):
        q_ref, k_ref, v_ref, lc_ref, ka_ref, kb_ref = refs[:6]
        y_ref, lse_ref = refs[6 + n:8 + n]
        q_s, k_s, v_s, m_s, l_s, acc_s = refs[8 + 2 * n:14 + 2 * n]
        gi, qi = pl.program_id(0), pl.program_id(1)
        first, lane, ones = _head_masks()
        if n:
            send, pass_on, finish = _gather_phases(refs[8 + n:8 + 2 * n], *refs[14 + 2 * n:], _spans(gathering))
            pl.when((gi == 0) & (qi == 0))(send)
            pl.when((gi == ngrp - 1) & (qi == 0))(pass_on)

        @pl.when(qi == 0)
        def _():
            sel = jnp.broadcast_to(first.astype(F32), (t, 128))
            for pr in range(grp):
                cols = slice(pr * 128, (pr + 1) * 128)
                for jb in range(nq):
                    kj = k_ref[jb * t:(jb + 1) * t, cols].astype(F32)
                    vj = v_ref[jb * t:(jb + 1) * t, cols].astype(F32)
                    k_s[pr, 0, jb] = jnp.where(first, kj, ka_ref[pr, 0] + kb_ref[pr, 0, jb:jb + 1, :]).astype(BF16)
                    k_s[pr, 1, jb] = jnp.where(first, ka_ref[pr, 1] + kb_ref[pr, 1, jb:jb + 1, :], kj).astype(BF16)
                    v_s[pr, jb, 0:t, 0:128] = jnp.where(first, vj, 0.0).astype(BF16)
                    v_s[pr, jb, t:2 * t, 0:128] = jnp.where(first, 0.0, vj).astype(BF16)
                    v_s[pr, jb, 0:t, 128:256] = sel.astype(BF16)
                    v_s[pr, jb, t:2 * t, 128:256] = (1.0 - sel).astype(BF16)

        for pr in range(grp):
            q = q_ref[:, pr * 128:(pr + 1) * 128].astype(F32) * (1.0 / math.sqrt(HEAD_DIM))
            q_s[pr, 0] = jnp.where(first, q, ones(0, 2 * AUG)).astype(BF16)
            q_s[pr, 1] = jnp.where(first, ones(1, 2 * AUG), q).astype(BF16)
        m_s[...] = jnp.full_like(m_s, MASKED)
        l_s[...] = jnp.zeros_like(l_s)
        acc_s[...] = jnp.zeros_like(acc_s)

        def scores(j):
            return tuple(_dot(q_s[pr, e], k_s[pr, e, j], NT) for pr in range(grp) for e in range(2))

        def step(j, carry):
            softmax_block(j, scores(j))
            return carry

        def softmax_block(j, u):
            lc = lc_ref[qi - j]
            for pr in range(grp):
                u0 = u[2 * pr] + lc
                u1 = u[2 * pr + 1] + lc
                m0, m1 = m_s[pr, 0], m_s[pr, 1]
                n0 = jnp.maximum(m0, jnp.max(u0, axis=-1, keepdims=True))
                n1 = jnp.maximum(m1, jnp.max(u1, axis=-1, keepdims=True))
                m_s[pr, 0], m_s[pr, 1] = n0, n1
                p = jnp.concatenate([jnp.exp(u0 - jnp.concatenate([n0, n0], axis=1)).astype(BF16),
                                     jnp.exp(u1 - jnp.concatenate([n1, n1], axis=1)).astype(BF16)], axis=1)
                pv = _dot(p, v_s[pr, j])
                alpha = jnp.where(first, jnp.exp(m0 - n0), jnp.exp(m1 - n1))
                acc_s[pr] = acc_s[pr] * alpha + pv[:, 0:128]
                l_s[pr] = l_s[pr] * alpha + pv[:, 128:256]

        lax.fori_loop(0, qi + 1, step, 0)
        for pr in range(grp):
            cols = slice(pr * 128, (pr + 1) * 128)
            y_ref[:, cols] = (acc_s[pr] / l_s[pr]).astype(BF16)
            lse_ref[:, cols] = jnp.where(first, m_s[pr, 0], m_s[pr, 1]) + jnp.log(l_s[pr])
        if n:
            pl.when((gi == ngrp - 1) & (qi == nq - 1))(finish)

    out = pl.pallas_call(
        body, name="attn_fwd", grid=(ngrp, nq),
        in_specs=[pl.BlockSpec((t, wide), lambda g, i: (i, qcol + g)),
                  pl.BlockSpec((s, wide), lambda g, i: (0, kcol + g)),
                  pl.BlockSpec((s, wide), lambda g, i: (0, vcol + g)),
                  _full((nq, t, t)),
                  pl.BlockSpec((grp, 2, t, 128), lambda g, i: (g, 0, 0, 0)),
                  pl.BlockSpec((grp, 2, nq, 128), lambda g, i: (g, 0, 0, 0))] + [ANY] * n,
        out_specs=[pl.BlockSpec((t, wide), lambda g, i: (i, g)), pl.BlockSpec((t, wide), lambda g, i: (i, g))]
        + [ANY] * n,
        out_shape=[jax.ShapeDtypeStruct((s, D), BF16), jax.ShapeDtypeStruct((s, D), F32)]
        + [jax.ShapeDtypeStruct(a.shape, a.dtype) for a in _arrays(gathering)],
        input_output_aliases={6 + w: 2 + w for w in range(n)},
        scratch_shapes=[pltpu.VMEM((grp, 2, t, 128), BF16), pltpu.VMEM((grp, 2, nq, t, 128), BF16),
                        pltpu.VMEM((grp, nq, 2 * t, 256), BF16), pltpu.VMEM((grp, 2, t, 128), F32),
                        pltpu.VMEM((grp, t, 128), F32), pltpu.VMEM((grp, t, 128), F32)]
        + (_gather_sems(n) if n else []),
        compiler_params=_params("arbitrary", "arbitrary", communicates=bool(n)),
    )(z, z, z, logc, ka, kb, *_arrays(gathering))
    return out[0], out[1], out[2:]


def proj_merge(ya, yb, wa, wb, z, bg):
    s = ya.shape[0]
    tm = 512

    def body(ya_ref, yb_ref, wa_ref, wb_ref, ga_ref, gb_ref, bg_ref, mg_ref, pa_ref, pb_ref):
        pa = _dot(ya_ref[...], wa_ref[...])
        pb = _dot(yb_ref[...], wb_ref[...])
        sa = _sigmoid(ga_ref[...] + bg_ref[0:1, :])
        sb = _sigmoid(gb_ref[...] + bg_ref[1:2, :])
        mg_ref[...] = (sa * pa + sb * pb).astype(BF16)
        pa_ref[...] = pa.astype(BF16)
        pb_ref[...] = pb.astype(BF16)

    out = jax.ShapeDtypeStruct((s, D), BF16)
    return pl.pallas_call(
        body, name="proj_merge", grid=(s // tm,),
        in_specs=[_rows(tm, D), _rows(tm, D), _full((D, D)), _full((D, D)),
                  _rows(tm, D, 5), _rows(tm, D, 6), _full((2, D))],
        out_specs=[_rows(tm, D)] * 3, out_shape=[out] * 3, compiler_params=_params("parallel"),
    )(ya, yb, wa, wb, z, z, bg)


def out_norm(merged, w_out, x, g_post, g_fpre):
    s = x.shape[0]
    tm = 512

    def body(mg_ref, w_ref, x_ref, gp_ref, gf_ref, o_ref, x1_ref, h2_ref):
        o = _dot(mg_ref[...], w_ref[...])
        ohat, _ = _rms(o)
        x1 = x_ref[...] + ohat * gp_ref[...]
        x1hat, _ = _rms(x1)
        o_ref[...] = o
        x1_ref[...] = x1
        h2_ref[...] = (x1hat * gf_ref[...]).astype(BF16)

    return pl.pallas_call(
        body, name="out_norm", grid=(s // tm,),
        in_specs=[_rows(tm, D), _full((D, D)), _rows(tm, D), _full((1, D)), _full((1, D))],
        out_specs=[_rows(tm, D)] * 3,
        out_shape=[jax.ShapeDtypeStruct((s, D), F32), jax.ShapeDtypeStruct((s, D), F32),
                   jax.ShapeDtypeStruct((s, D), BF16)],
        compiler_params=_params("parallel"),
    )(merged, w_out, x, g_post, g_fpre)


def mm_ff1(h2, wg):
    s = h2.shape[0]
    tm = 1024

    def body(a_ref, b_ref, o_ref, r_ref):
        a = _dot(a_ref[...], b_ref[...])
        o_ref[...] = a.astype(BF16)
        r = jnp.maximum(a, 0.0)
        r_ref[...] = (r * r).astype(BF16)

    return pl.pallas_call(
        body, name="mm_ff1", grid=(s // tm, N_CHIPS),
        in_specs=[pl.BlockSpec((tm, D), lambda i, j: (i, 0)), pl.BlockSpec((None, D, D), lambda i, j: (j, 0, 0))],
        out_specs=[pl.BlockSpec((tm, D), lambda i, j: (i, j))] * 2,
        out_shape=[jax.ShapeDtypeStruct((s, D_FF), BF16), jax.ShapeDtypeStruct((s, D_FF), BF16)],
        compiler_params=_params("parallel", "parallel"),
    )(h2, wg)


def ff2_loss(rl, w_ff2, x1, target, g_fpost):
    s = x1.shape[0]
    tm = 256

    def body(rl_ref, w_ref, x1_ref, t_ref, g_ref, dy_ref, df_ref, dg_ref, loss_ref):
        @pl.when(pl.program_id(0) == 0)
        def _():
            dg_ref[...] = jnp.zeros_like(dg_ref)
            loss_ref[...] = jnp.zeros_like(loss_ref)

        f = _dot(rl_ref[...], w_ref[...])
        fhat, r = _rms(f)
        err = x1_ref[...] + fhat * g_ref[...] - t_ref[...]
        loss_ref[...] += 0.5 * jnp.sum(jnp.mean(err * err, axis=-1, keepdims=True), axis=0, keepdims=True)
        dy = err * (1.0 / D)
        dy_ref[...] = dy
        dg_ref[...] += jnp.sum(dy * fhat, axis=0, keepdims=True)
        df_ref[...] = _rms_bwd(dy * g_ref[...], fhat, r).astype(BF16)

    return pl.pallas_call(
        body, name="ff2_loss", grid=(s // tm,),
        in_specs=[_rows(tm, D_FF), _full((D_FF, D)), _rows(tm, D), _rows(tm, D), _full((1, D))],
        out_specs=[_rows(tm, D), _rows(tm, D), _full((1, D)), _full((1, 1))],
        out_shape=[jax.ShapeDtypeStruct((s, D), F32), jax.ShapeDtypeStruct((s, D), BF16),
                   jax.ShapeDtypeStruct((1, D), F32), jax.ShapeDtypeStruct((1, 1), F32)],
        compiler_params=_params("arbitrary"),
    )(rl, w_ff2, x1, target, g_fpost)


def mm_tn(name, a, b, ta, tb, out_shape, out_spec):
    s = a.shape[0]

    def body(a_ref, b_ref, o_ref):
        o_ref[...] = _dot(a_ref[...], b_ref[...], TN)

    return pl.pallas_call(
        body, name=name, grid=(a.shape[1] // ta, b.shape[1] // tb),
        in_specs=[pl.BlockSpec((s, ta), lambda i, j: (0, i)), pl.BlockSpec((s, tb), lambda i, j: (0, j))],
        out_specs=out_spec, out_shape=jax.ShapeDtypeStruct(out_shape, F32),
        compiler_params=_params("parallel", "parallel"),
    )(a, b)


def mm_nt(name, a, w):
    s = a.shape[0]
    tm = 512

    def body(a_ref, w_ref, o_ref):
        o_ref[...] = _dot(a_ref[...], w_ref[...], NT).astype(BF16)

    return pl.pallas_call(
        body, name=name, grid=(s // tm,), in_specs=[_rows(tm, D), _full((D, D))], out_specs=_rows(tm, D),
        out_shape=jax.ShapeDtypeStruct((s, D), BF16), compiler_params=_params("parallel"),
    )(a, w)


def ff2_bwd(df, w_ff2, a):
    s = df.shape[0]
    tm = 1024

    def body(df_ref, w_ref, a_ref, da_ref):
        drl = _dot(df_ref[...], w_ref[...], NT)
        da_ref[...] = (drl * (2.0 * jnp.maximum(a_ref[...].astype(F32), 0.0))).astype(BF16)

    return pl.pallas_call(
        body, name="ff2_bwd", grid=(s // tm, D_FF // D),
        in_specs=[pl.BlockSpec((tm, D), lambda i, j: (i, 0)), pl.BlockSpec((D, D), lambda i, j: (j, 0)),
                  pl.BlockSpec((tm, D), lambda i, j: (i, j))],
        out_specs=pl.BlockSpec((tm, D), lambda i, j: (i, j)),
        out_shape=jax.ShapeDtypeStruct((s, D_FF), BF16), compiler_params=_params("parallel", "parallel"),
    )(df, w_ff2, a)


def ff1_bwd_norms(da, wg, x1, o, dy, g_fpre, g_post, swapping):
    s = x1.shape[0]
    tm = 512
    n = len(swapping)

    def body(*refs):
        da_ref, w_ref, x1_ref, o_ref, dy_ref, gf_ref, gp_ref = refs[:7]
        dx1_ref, do_ref, dgf_ref, dgp_ref = refs[7 + n:11 + n]
        acc_ref = refs[11 + 2 * n]
        i, k = pl.program_id(0), pl.program_id(1)
        if n:
            send, finish = _swap_phases(refs[7:7 + n], refs[11 + n:11 + 2 * n], *refs[12 + 2 * n:])
            pl.when((i == 0) & (k == 0))(send)

        @pl.when((i == 0) & (k == 0))
        def _():
            dgf_ref[...] = jnp.zeros_like(dgf_ref)
            dgp_ref[...] = jnp.zeros_like(dgp_ref)

        part = _dot(da_ref[...], w_ref[...], NT)

        @pl.when(k == 0)
        def _():
            acc_ref[...] = part

        @pl.when(k > 0)
        def _():
            acc_ref[...] += part

        @pl.when(k == N_CHIPS - 1)
        def _():
            dh2 = acc_ref[...]
            x1hat, r2 = _rms(x1_ref[...])
            dgf_ref[...] += jnp.sum(dh2 * x1hat, axis=0, keepdims=True)
            dx1 = dy_ref[...] + _rms_bwd(dh2 * gf_ref[...], x1hat, r2)
            ohat, r1 = _rms(o_ref[...])
            dgp_ref[...] += jnp.sum(dx1 * ohat, axis=0, keepdims=True)
            dx1_ref[...] = dx1
            do_ref[...] = _rms_bwd(dx1 * gp_ref[...], ohat, r1).astype(BF16)

        if n:
            pl.when((i == s // tm - 1) & (k == N_CHIPS - 1))(finish)

    row = pl.BlockSpec((tm, D), lambda i, k: (i, 0))
    vec = pl.BlockSpec((1, D), lambda i, k: (0, 0))
    res = pl.pallas_call(
        body, name="ff1_bwd_norms", grid=(s // tm, N_CHIPS),
        in_specs=[pl.BlockSpec((tm, D), lambda i, k: (i, k)), pl.BlockSpec((None, D, D), lambda i, k: (k, 0, 0)),
                  row, row, row, vec, vec] + [ANY] * n,
        out_specs=[row, row, vec, vec] + [ANY] * n,
        out_shape=[jax.ShapeDtypeStruct((s, D), F32), jax.ShapeDtypeStruct((s, D), BF16),
                   jax.ShapeDtypeStruct((1, D), F32), jax.ShapeDtypeStruct((1, D), F32)] + _swap_shapes(swapping),
        scratch_shapes=[pltpu.VMEM((tm, D), F32)] + (_swap_sems(n) if n else []),
        compiler_params=_params("arbitrary", "arbitrary", communicates=bool(n)),
    )(da, wg, x1, o, dy, g_fpre, g_post, *swapping)
    return res[0], res[1], res[2], res[3], res[4:]


def out_bwd_gates(do, w_out, pa, pb, z, bg):
    s = do.shape[0]
    tm = 512

    def body(do_ref, w_ref, pa_ref, pb_ref, ga_ref, gb_ref, bg_ref, dpa_ref, dpb_ref, dga_ref, dgb_ref, dbg_ref):
        @pl.when(pl.program_id(0) == 0)
        def _():
            dbg_ref[...] = jnp.zeros_like(dbg_ref)

        dm = _dot(do_ref[...], w_ref[...], NT)
        sa = _sigmoid(ga_ref[...] + bg_ref[0:1, :])
        sb = _sigmoid(gb_ref[...] + bg_ref[1:2, :])
        dpa_ref[...] = (dm * sa).astype(BF16)
        dpb_ref[...] = (dm * sb).astype(BF16)
        dga = dm * pa_ref[...].astype(F32) * (sa * (1.0 - sa))
        dgb = dm * pb_ref[...].astype(F32) * (sb * (1.0 - sb))
        dga_ref[...] = dga.astype(BF16)
        dgb_ref[...] = dgb.astype(BF16)
        dbg_ref[0:1, :] += jnp.sum(dga, axis=0, keepdims=True)
        dbg_ref[1:2, :] += jnp.sum(dgb, axis=0, keepdims=True)

    out = jax.ShapeDtypeStruct((s, D), BF16)
    return pl.pallas_call(
        body, name="out_bwd_gates", grid=(s // tm,),
        in_specs=[_rows(tm, D), _full((D, D)), _rows(tm, D), _rows(tm, D), _rows(tm, D, 5), _rows(tm, D, 6),
                  _full((2, D))],
        out_specs=[_rows(tm, D)] * 4 + [_full((2, D))],
        out_shape=[out] * 4 + [jax.ShapeDtypeStruct((2, D), F32)], compiler_params=_params("arbitrary"),
    )(do, w_out, pa, pb, z, z, bg)


def gating_bwd(z, dya, ln_g, ln_b, w_s, bs_t, swapping):
    s = z.shape[0]
    ones = functools.partial(jnp.ones, (8, CHUNK), BF16)
    n = len(swapping)

    def body(*refs):
        u_ref, v_ref, dya_ref, lg_ref, lb_ref, ws_ref, bst_ref = refs[:7]
        du_ref, dv_ref, dws_ref, dbs_ref, dlg_ref, dlb_ref = refs[7 + n:13 + n]
        dvn_ref = refs[13 + 2 * n]
        ci = pl.program_id(0)
        if n:
            send, finish = _swap_phases(refs[7:7 + n], refs[13 + n:13 + 2 * n], *refs[14 + 2 * n:])
            pl.when(ci == 0)(send)

        @pl.when(ci == 0)
        def _():
            dws_ref[...] = jnp.zeros_like(dws_ref)
            dbs_ref[...] = jnp.zeros_like(dbs_ref)
            dlg_ref[...] = jnp.zeros_like(dlg_ref)
            dlb_ref[...] = jnp.zeros_like(dlb_ref)

        ug, dug_du = _gelu_and_grad(u_ref[...].astype(F32))
        vg, dvg_dv = _gelu_and_grad(v_ref[...].astype(F32))
        vhat, rstd = _layer_norm(vg)
        vn = (vhat * lg_ref[...] + lb_ref[...]).astype(BF16)
        dya = dya_ref[...].astype(F32)
        for g in range(GROUPS):
            cols = slice(g * CHUNK, (g + 1) * CHUNK)
            ws = _tril_ws(ws_ref, g)
            mixed = _dot(ws, vn[:, cols]) + bst_ref[:, g:g + 1]
            du_ref[:, cols] = (dya[:, cols] * mixed * dug_du[:, cols]).astype(BF16)
            dmix = (dya[:, cols] * ug[:, cols]).astype(BF16)
            dbs_ref[g] += _dot(ones(), dmix, NT)
            dws_ref[g] += _dot(dmix, vn[:, cols], NT)
            dvn_ref[:, cols] = _dot(ws, dmix, TN)
        dvn = dvn_ref[...]
        dlg_ref[...] += jnp.sum(dvn * vhat, axis=0, keepdims=True)
        dlb_ref[...] += jnp.sum(dvn, axis=0, keepdims=True)
        dvh = dvn * lg_ref[...]
        dvg = rstd * (dvh - jnp.mean(dvh, axis=-1, keepdims=True)
                      - vhat * jnp.mean(dvh * vhat, axis=-1, keepdims=True))
        dv_ref[...] = (dvg * dvg_dv).astype(BF16)

        @pl.when(ci == pl.num_programs(0) - 1)
        def _():
            r = lax.broadcasted_iota(jnp.int32, (CHUNK, CHUNK), 0)
            c = lax.broadcasted_iota(jnp.int32, (CHUNK, CHUNK), 1)
            for g in range(GROUPS):
                dws_ref[g] = jnp.where(c <= r, dws_ref[g], 0.0)

        if n:
            pl.when(ci == pl.num_programs(0) - 1)(finish)

    out = jax.ShapeDtypeStruct((s, D), BF16)
    res = pl.pallas_call(
        body, name="gating_bwd", grid=(s // CHUNK,),
        in_specs=[_rows(CHUNK, D, 0), _rows(CHUNK, D, 1), _rows(CHUNK, D), _full((1, D)), _full((1, D)),
                  _full((GROUPS, CHUNK, CHUNK)), _full((CHUNK, GROUPS))] + [ANY] * n,
        out_specs=[_rows(CHUNK, D), _rows(CHUNK, D), _full((GROUPS, CHUNK, CHUNK)), _full((GROUPS, 8, CHUNK)),
                   _full((1, D)), _full((1, D))] + [ANY] * n,
        out_shape=[out, out, jax.ShapeDtypeStruct((GROUPS, CHUNK, CHUNK), F32),
                   jax.ShapeDtypeStruct((GROUPS, 8, CHUNK), F32),
                   jax.ShapeDtypeStruct((1, D), F32), jax.ShapeDtypeStruct((1, D), F32)] + _swap_shapes(swapping),
        scratch_shapes=[pltpu.VMEM((CHUNK, D), F32)] + (_swap_sems(n) if n else []),
        compiler_params=_params("arbitrary", communicates=bool(n)),
    )(z, z, dya, ln_g, ln_b, w_s, bs_t, *swapping)
    return (*res[:6], res[6:])


def attn_bwd(z, yb, dyb, lse, logc, ka, kb, scattering):
    s = z.shape[0]
    nq = s // ATT_T
    t = ATT_T
    grp = ATT_BWD_GROUP
    ngrp = HEADS // 2 // grp
    wide = 128 * grp
    qcol, kcol, vcol = 2 * D // wide, 3 * D // wide, 4 * D // wide
    scale = 1.0 / math.sqrt(HEAD_DIM)
    n = len(scattering)

    def body(*refs):
        q_ref, k_ref, v_ref, y_ref, dy_ref, lse_ref, lc_ref, ka_ref, kb_ref = refs[:9]
        dq_ref, dk_ref, dv_ref = refs[9 + n:12 + n]
        qa_s, qt_s, da_s, dt_s, dq_s, dkt_s, dvt_s = refs[12 + 2 * n:19 + 2 * n]
        gi, j = pl.program_id(0), pl.program_id(1)
        first, lane, ones = _head_masks()
        if n:
            send, finish = _scatter_phases(refs[9:9 + n], refs[12 + n:12 + 2 * n], *refs[19 + 2 * n:])
            pl.when((gi == 0) & (j == 0))(send)

        @pl.when(j == 0)
        def _():
            dq_s[...] = jnp.zeros_like(dq_s)
            for pr in range(grp):
                cols = slice(pr * 128, (pr + 1) * 128)
                for ib in range(nq):
                    rows = slice(ib * t, (ib + 1) * t)
                    q = q_ref[rows, cols].astype(F32) * scale
                    lse = lse_ref[rows, cols]
                    qa_s[pr, 0, ib] = jnp.where(first, q, _place3(lane, HEAD_DIM + 2 * AUG, _split3(-lse[:, 0:1]),
                                                                  ones(0, 2 * AUG))).astype(BF16)
                    qa_s[pr, 1, ib] = jnp.where(
                        first, _place3(lane, 2 * AUG, _split3(-lse[:, HEAD_DIM:HEAD_DIM + 1]), ones(1, 2 * AUG)),
                        q).astype(BF16)
                    qt_s[pr, ib, :, 0:t] = jnp.where(first, q, 0.0).T.astype(BF16)
                    qt_s[pr, ib, :, t:2 * t] = jnp.where(first, 0.0, q).T.astype(BF16)
                    do = dy_ref[rows, cols].astype(F32)
                    prod = do * y_ref[rows, cols].astype(F32)
                    dd0 = jnp.sum(jnp.where(first, prod, 0.0), axis=-1, keepdims=True)
                    dd1 = jnp.sum(jnp.where(first, 0.0, prod), axis=-1, keepdims=True)
                    da_s[pr, 0, ib] = jnp.where(first, do, _place3(lane, HEAD_DIM, _split3(-dd0), 0.0)).astype(BF16)
                    da_s[pr, 1, ib] = jnp.where(first, _place3(lane, 0, _split3(-dd1), 0.0), do).astype(BF16)
                    dt_s[pr, ib, :, 0:t] = jnp.where(first, do, 0.0).T.astype(BF16)
                    dt_s[pr, ib, :, t:2 * t] = jnp.where(first, 0.0, do).T.astype(BF16)

        keys = []
        for pr in range(grp):
            kj = k_ref[:, pr * 128:(pr + 1) * 128].astype(F32)
            vj = v_ref[:, pr * 128:(pr + 1) * 128].astype(F32)
            keys.append((
                jnp.where(first, kj, ka_ref[pr, 0] + kb_ref[pr, 0, pl.ds(j, 1), :]).astype(BF16),
                jnp.where(first, ka_ref[pr, 1] + kb_ref[pr, 1, pl.ds(j, 1), :], kj).astype(BF16),
                jnp.concatenate([jnp.where(first, kj, 0.0), jnp.where(first, 0.0, kj)], axis=0).astype(BF16),
                jnp.where(first, vj, ones(0, AUG)).astype(BF16),
                jnp.where(first, ones(1, AUG), vj).astype(BF16)))
        dkt_s[...] = jnp.zeros_like(dkt_s)
        dvt_s[...] = jnp.zeros_like(dvt_s)

        def step(i, _):
            lc = lc_ref[i - j]
            rows = pl.ds(pl.multiple_of(i * t, t), t)
            for pr in range(grp):
                k0a, k1a, kst, v0a, v1a = keys[pr]
                p0 = jnp.exp(_dot(qa_s[pr, 0, i], k0a, NT) + lc)
                p1 = jnp.exp(_dot(qa_s[pr, 1, i], k1a, NT) + lc)
                e0 = (p0 * _dot(da_s[pr, 0, i], v0a, NT)).astype(BF16)
                e1 = (p1 * _dot(da_s[pr, 1, i], v1a, NT)).astype(BF16)
                dq_s[pr, rows, :] += _dot(jnp.concatenate([e0, e1], axis=1), kst)
                dvt_s[pr] += _dot(dt_s[pr, i], jnp.concatenate([p0.astype(BF16), p1.astype(BF16)], axis=0))
                dkt_s[pr] += _dot(qt_s[pr, i], jnp.concatenate([e0, e1], axis=0))
            return 0

        lax.fori_loop(j, nq, step, 0)
        for pr in range(grp):
            dk_ref[:, pr * 128:(pr + 1) * 128] = dkt_s[pr].T.astype(BF16)
            dv_ref[:, pr * 128:(pr + 1) * 128] = dvt_s[pr].T.astype(BF16)

        @pl.when(j == nq - 1)
        def _():
            for pr in range(grp):
                dq_ref[:, pr * 128:(pr + 1) * 128] = (dq_s[pr] * scale).astype(BF16)

        if n:
            pl.when((gi == ngrp - 1) & (j == nq - 1))(finish)

    colblock = lambda c: pl.BlockSpec((s, wide), lambda g, j: (0, c + g))
    blk = lambda c: pl.BlockSpec((t, wide), lambda g, j: (j, c + g))
    out = jax.ShapeDtypeStruct((s, D), BF16)
    res = pl.pallas_call(
        body, name="attn_bwd", grid=(ngrp, nq),
        in_specs=[colblock(qcol), blk(kcol), blk(vcol), colblock(0), colblock(0), colblock(0),
                  _full((nq, t, t)), pl.BlockSpec((grp, 2, t, 128), lambda g, j: (g, 0, 0, 0)),
                  pl.BlockSpec((grp, 2, nq, 128), lambda g, j: (g, 0, 0, 0))] + [ANY] * n,
        out_specs=[colblock(0), blk(0), blk(0)] + [ANY] * n, out_shape=[out] * 3 + _scatter_shapes(scattering),
        scratch_shapes=[pltpu.VMEM((grp, 2, nq, t, 128), BF16), pltpu.VMEM((grp, nq, 128, 2 * t), BF16),
                        pltpu.VMEM((grp, 2, nq, t, 128), BF16), pltpu.VMEM((grp, nq, 128, 2 * t), BF16),
                        pltpu.VMEM((grp, s, 128), F32), pltpu.VMEM((grp, 128, t), F32),
                        pltpu.VMEM((grp, 128, t), F32)]
        + (_scatter_sems(n) if n else []),
        compiler_params=_params("arbitrary", "arbitrary", communicates=bool(n)),
    )(z, z, z, yb, dyb, lse, logc, ka, kb, *scattering)
    return res[0], res[1], res[2], res[3:]


def in_bwd_norm(dz, wg, x, dx1, g_pre, scattering):
    s = x.shape[0]
    tm = 512
    n = len(scattering)

    def body(*refs):
        dz_ref, w_ref, x_ref, dx1_ref, g_ref = refs[:5]
        dx_ref, dg_ref = refs[5 + n:7 + n]
        acc_ref = refs[7 + 2 * n]
        i, k = pl.program_id(0), pl.program_id(1)
        if n:
            send, finish = _scatter_phases(refs[5:5 + n], refs[7 + n:7 + 2 * n], *refs[8 + 2 * n:])
            pl.when((i == 0) & (k == 0))(send)

        @pl.when((i == 0) & (k == 0))
        def _():
            dg_ref[...] = jnp.zeros_like(dg_ref)

        part = _dot(dz_ref[...], w_ref[...], NT)

        @pl.when(k == 0)
        def _():
            acc_ref[...] = part

        @pl.when(k > 0)
        def _():
            acc_ref[...] += part

        @pl.when(k == N_CHIPS - 1)
        def _():
            dh = acc_ref[...]
            xhat, r = _rms(x_ref[...])
            dg_ref[...] += jnp.sum(dh * xhat, axis=0, keepdims=True)
            dx_ref[...] = dx1_ref[...] + _rms_bwd(dh * g_ref[...], xhat, r)

        if n:
            pl.when((i == s // tm - 1) & (k == N_CHIPS - 1))(finish)

    row = pl.BlockSpec((tm, D), lambda i, k: (i, 0))
    vec = pl.BlockSpec((1, D), lambda i, k: (0, 0))
    res = pl.pallas_call(
        body, name="in_bwd_norm", grid=(s // tm, N_CHIPS),
        in_specs=[pl.BlockSpec((tm, IN_SHARD), lambda i, k: (i, k)),
                  pl.BlockSpec((None, D, IN_SHARD), lambda i, k: (k, 0, 0)), row, row, vec] + [ANY] * n,
        out_specs=[row, vec] + [ANY] * n,
        out_shape=[jax.ShapeDtypeStruct((s, D), F32), jax.ShapeDtypeStruct((1, D), F32)]
        + _scatter_shapes(scattering),
        scratch_shapes=[pltpu.VMEM((tm, D), F32)] + (_scatter_sems(n) if n else []),
        compiler_params=_params("arbitrary", "arbitrary", communicates=bool(n)),
    )(dz, wg, x, dx1, g_pre, *scattering)
    return res[0], res[1], res[2:]


def _adamw_math(w, g, m, v):
    m = ADAM_B1 * m + (1.0 - ADAM_B1) * g
    v = ADAM_B2 * v + (1.0 - ADAM_B2) * (g * g)
    m_hat = m / (1.0 - ADAM_B1 ** ADAM_STEP)
    v_hat = v / (1.0 - ADAM_B2 ** ADAM_STEP)
    delta = -ADAM_LR * (m_hat / (jnp.sqrt(v_hat) + ADAM_EPS) + ADAM_WD * w)
    return delta, m, v


def adamw(name, w, g, m, v, tr):
    r, c = w.shape

    def body(w_ref, g_ref, m_ref, v_ref, go_ref, d_ref, nm_ref, nv_ref):
        g = g_ref[...]
        go_ref[...] = g
        d_ref[...], nm_ref[...], nv_ref[...] = _adamw_math(w_ref[...], g, m_ref[...], v_ref[...])

    out = jax.ShapeDtypeStruct((r, c), F32)
    return pl.pallas_call(
        body, name=name, grid=(r // tr,), in_specs=[_rows(tr, c)] * 4, out_specs=[_rows(tr, c)] * 4,
        out_shape=[out] * 4, compiler_params=_params("parallel"),
    )(w, g, m, v)


def add_halves(name, g, recv, c_idx, tr):
    n, h, c = recv.shape

    def body(c_ref, g_ref, r_ref, o_ref):
        o_ref[...] = (g_ref[...] + r_ref[...]).astype(BF16)

    nb = h // tr
    return pl.pallas_call(
        body, name=name,
        grid_spec=pltpu.PrefetchScalarGridSpec(
            num_scalar_prefetch=1, grid=(n, nb),
            in_specs=[pl.BlockSpec((None, tr, c), lambda k, i, c_ref: (k, c_ref[0] * nb + i, 0)),
                      pl.BlockSpec((None, tr, c), lambda k, i, c_ref: (k, i, 0))],
            out_specs=pl.BlockSpec((None, tr, c), lambda k, i, c_ref: (k, i, 0))),
        out_shape=jax.ShapeDtypeStruct((n, h, c), BF16), compiler_params=_params("parallel", "parallel"),
    )(c_idx, g, recv)


def sum_chips(name, parts, recv, where, tr):
    n, h, c = recv.shape
    nb = h // tr

    def body(w_ref, p_ref, r_ref, o_ref):
        acc = p_ref[...].astype(F32)
        for k in range(n):
            acc = acc + r_ref[k].astype(F32)
        o_ref[...] = acc

    return pl.pallas_call(
        body, name=name,
        grid_spec=pltpu.PrefetchScalarGridSpec(
            num_scalar_prefetch=1, grid=(nb,),
            in_specs=[pl.BlockSpec((None, tr, c), lambda i, w_ref: (w_ref[0], i, 0)),
                      pl.BlockSpec((n, tr, c), lambda i, w_ref: (0, i, 0))],
            out_specs=pl.BlockSpec((tr, c), lambda i, w_ref: (w_ref[1] * nb + i, 0))),
        out_shape=jax.ShapeDtypeStruct((2 * h, c), F32), compiler_params=_params("parallel"),
    )(where, parts, recv)


def place_shard(name, shard, where, dtype, tr):
    r, c = shard.shape

    def body(w_ref, s_ref, o_ref):
        o_ref[...] = s_ref[...].astype(dtype)

    return pl.pallas_call(
        body, name=name,
        grid_spec=pltpu.PrefetchScalarGridSpec(
            num_scalar_prefetch=1, grid=(r // tr,),
            in_specs=[pl.BlockSpec((tr, c), lambda i, w_ref: (i, 0))],
            out_specs=pl.BlockSpec((None, tr, c), lambda i, w_ref: (w_ref[0], i, 0))),
        out_shape=jax.ShapeDtypeStruct((N_CHIPS, r, c), dtype), compiler_params=_params("parallel"),
    )(where, shard)


ANY = pl.BlockSpec(memory_space=pl.ANY)


def _place():
    x, y, c = lax.axis_index("x"), lax.axis_index("y"), lax.axis_index("c")
    chips = [(1 - x, y), (x, 1 - y), (1 - x, 1 - y)]
    return x, y, c, chips


def gather_shards(arrays):
    n = len(arrays)

    def body(*refs):
        send, pass_on, finish = _gather_phases(refs[n:2 * n], *refs[2 * n:], _spans(arrays))
        send()
        pass_on()
        finish()

    return pl.pallas_call(
        body, name="gather_shards", in_specs=[ANY] * n, out_specs=[ANY] * n,
        out_shape=[jax.ShapeDtypeStruct(a.shape, a.dtype) for a in _arrays(arrays)],
        input_output_aliases={w: w for w in range(n)}, scratch_shapes=_gather_sems(n),
        compiler_params=pltpu.CompilerParams(has_side_effects=True),
    )(*arrays)


def _gather_sems(n):
    return [pltpu.SemaphoreType.DMA((6 * n,)), pltpu.SemaphoreType.DMA((6 * n,))]


class Span(typing.NamedTuple):
    array: jax.Array
    lo: int
    hi: int


def _arrays(gathering):
    return [g.array if isinstance(g, Span) else g for g in gathering]


def _spans(gathering):
    return [(g.lo, g.hi) if isinstance(g, Span) else (0, g.shape[1]) for g in gathering]


def _gather_phases(out, send_sems, recv_sems, spans):
    n = len(out)
    x, y, c, chips = _place()
    me = 2 * x + y
    sibling = (x, y, 1 - c)

    def half(w, chip, core):
        lo, hi = spans[w]
        h = (hi - lo) // 2
        return out[w].at[chip, pl.ds(lo + core * h, h)]

    def copy(k, block, to):
        return pltpu.make_async_remote_copy(src_ref=block, dst_ref=block, send_sem=send_sems.at[k],
                                            recv_sem=recv_sems.at[k], device_id=to, device_id_type=MESH)

    def over_ici(w, j, chip):
        return copy(3 * w + j, half(w, chip, c), (chips[j][0], chips[j][1], c))

    def over_d2d(w, j, core):
        return copy(3 * n + 3 * w + j, half(w, 2 * chips[j][0] + chips[j][1], core), sibling)

    pairs = [(w, j) for w in range(n) for j in range(3)]

    def send():
        for w, j in pairs:
            over_ici(w, j, me).start()

    def pass_on():
        for w, j in pairs:
            over_ici(w, j, 2 * chips[j][0] + chips[j][1]).wait_recv()
            over_d2d(w, j, c).start()

    def finish():
        for w, j in pairs:
            over_d2d(w, j, 1 - c).wait_recv()
        for w, j in pairs:
            over_ici(w, j, me).wait_send()
            over_d2d(w, j, c).wait_send()

    return send, pass_on, finish


def swap_halves(name, grads):
    n = len(grads)

    def body(*refs):
        send, finish = _swap_phases(refs[:n], refs[n:2 * n], *refs[2 * n:])
        send()
        finish()

    return pl.pallas_call(
        body, name=name, in_specs=[ANY] * n, out_specs=[ANY] * n, out_shape=_swap_shapes(grads),
        scratch_shapes=_swap_sems(n), compiler_params=pltpu.CompilerParams(has_side_effects=True),
    )(*grads)


def _swap_shapes(grads):
    return [jax.ShapeDtypeStruct((a.shape[0], a.shape[1] // 2, a.shape[2]), a.dtype) for a in grads]


def _swap_sems(n):
    return [pltpu.SemaphoreType.DMA((n,)), pltpu.SemaphoreType.DMA((n,))]


def _swap_phases(g, out, send_sems, recv_sems):
    x, y, c, _ = _place()

    def copies():
        return [pltpu.make_async_remote_copy(
            src_ref=g[w].at[:, pl.ds((1 - c) * (g[w].shape[1] // 2), g[w].shape[1] // 2)], dst_ref=out[w],
            send_sem=send_sems.at[w], recv_sem=recv_sems.at[w], device_id=(x, y, 1 - c), device_id_type=MESH)
            for w in range(len(g))]

    def send():
        for cp in copies():
            cp.start()

    def finish():
        for cp in copies():
            cp.wait()

    return send, finish


def _send_phases(g, out, send_sems, recv_sems):
    x, y, c, _ = _place()

    def copies():
        return [pltpu.make_async_remote_copy(
            src_ref=g[w], dst_ref=out[w], send_sem=send_sems.at[w], recv_sem=recv_sems.at[w],
            device_id=(x, y, 1 - c), device_id_type=MESH) for w in range(len(g))]

    def send():
        for cp in copies():
            cp.start()

    def finish():
        for cp in copies():
            cp.wait()

    return send, finish


def dw_in_half(name, h, dz, which, sending):
    s = h.shape[0]
    hh, tb = D // 2, IN_SHARD // 2
    n = len(sending)
    steps = IN_COLS // tb

    def body(w_ref, *refs):
        a_ref, b_ref, o_ref = refs[0], refs[1], refs[2 + n]
        j = pl.program_id(0)
        if n:
            send, finish = _send_phases(refs[2:2 + n], refs[3 + n:3 + 2 * n], *refs[3 + 2 * n:])
            pl.when(j == 0)(send)
        o_ref[...] = _dot(a_ref[...], b_ref[...], TN)
        if n:
            pl.when(j == steps - 1)(finish)

    out = pl.pallas_call(
        body, name=name,
        grid_spec=pltpu.PrefetchScalarGridSpec(
            num_scalar_prefetch=1, grid=(steps,),
            in_specs=[pl.BlockSpec((s, hh), lambda j, w: (0, w[0])), pl.BlockSpec((s, tb), lambda j, w: (0, j))]
            + [ANY] * n,
            out_specs=[pl.BlockSpec((None, hh, tb), lambda j, w: (j // 2, 0, j % 2))] + [ANY] * n,
            scratch_shapes=_swap_sems(n) if n else []),
        out_shape=[jax.ShapeDtypeStruct((N_CHIPS, hh, IN_SHARD), F32)]
        + [jax.ShapeDtypeStruct(a.shape, a.dtype) for a in sending],
        compiler_params=_params("arbitrary", communicates=bool(n)),
    )(which, h, dz, *sending)
    return out[0], out[1:]


def scatter_chips(parts):
    n = len(parts)

    def body(*refs):
        send, finish = _scatter_phases(refs[:n], refs[n:2 * n], *refs[2 * n:])
        send()
        finish()

    return pl.pallas_call(
        body, name="scatter_chips", in_specs=[ANY] * n, out_specs=[ANY] * n,
        out_shape=_scatter_shapes(parts), scratch_shapes=_scatter_sems(n),
        compiler_params=pltpu.CompilerParams(has_side_effects=True),
    )(*parts)


def _scatter_shapes(parts):
    return [jax.ShapeDtypeStruct((3,) + a.shape[1:], a.dtype) for a in parts]


def _scatter_sems(n):
    return [pltpu.SemaphoreType.DMA((3 * n,)), pltpu.SemaphoreType.DMA((3 * n,))]


def _scatter_phases(p, out, send_sems, recv_sems):
    x, y, c, chips = _place()

    def copies():
        return [pltpu.make_async_remote_copy(
            src_ref=p[w].at[2 * px + py], dst_ref=out[w].at[j], send_sem=send_sems.at[3 * w + j],
            recv_sem=recv_sems.at[3 * w + j], device_id=(px, py, c), device_id_type=MESH)
            for w in range(len(p)) for j, (px, py) in enumerate(chips)]

    def send():
        for cp in copies():
            cp.start()

    def finish():
        for cp in copies():
            cp.wait()

    return send, finish


def join_halves(arrays):
    n = len(arrays)

    def body(*refs):
        out = refs[n:2 * n]
        send_sems, recv_sems = refs[2 * n:]
        x, y, c, _ = _place()

        def copy(w, core):
            h = out[w].shape[0] // 2
            rows = out[w].at[pl.ds(core * h, h)]
            return pltpu.make_async_remote_copy(
                src_ref=rows, dst_ref=rows, send_sem=send_sems.at[w], recv_sem=recv_sems.at[w],
                device_id=(x, y, 1 - c), device_id_type=MESH)

        for w in range(n):
            copy(w, c).start()
        for w in range(n):
            copy(w, 1 - c).wait_recv()
        for w in range(n):
            copy(w, c).wait_send()

    return pl.pallas_call(
        body, name="join_halves", in_specs=[ANY] * n, out_specs=[ANY] * n,
        out_shape=[jax.ShapeDtypeStruct(a.shape, a.dtype) for a in arrays],
        input_output_aliases={w: w for w in range(n)},
        scratch_shapes=[pltpu.SemaphoreType.DMA((n,)), pltpu.SemaphoreType.DMA((n,))],
        compiler_params=pltpu.CompilerParams(has_side_effects=True),
    )(*arrays)


def allreduce_small(packed):
    r, c = packed.shape
    n_dev = 8

    def body(x_ref, all_ref, sum_ref, send_sems, recv_sems, local_sem):
        x, y, cc, chips = _place()
        me, sibling = (x, y, cc), (x, y, 1 - cc)

        def rows(px, py, pc):
            return all_ref.at[4 * px + 2 * py + pc]

        def copy(k, block, to, src=None):
            return pltpu.make_async_remote_copy(
                src_ref=rows(*block) if src is None else src, dst_ref=rows(*block), send_sem=send_sems.at[k],
                recv_sem=recv_sems.at[k], device_id=to, device_id_type=MESH)

        mine = pltpu.make_async_copy(x_ref, rows(*me), local_sem)
        mine.start()
        first = [copy(0, me, sibling, src=x_ref)]
        first += [copy(1 + j, me, (*chip, cc), src=x_ref) for j, chip in enumerate(chips)]
        for cp in first:
            cp.start()
        passed = [copy(4 + j, (*chip, cc), sibling) for j, chip in enumerate(chips)]
        for j, chip in enumerate(chips):
            copy(1 + j, (*chip, cc), me).wait_recv()
            passed[j].start()
        copy(0, sibling, me).wait_recv()
        for j, chip in enumerate(chips):
            copy(4 + j, (*chip, 1 - cc), me).wait_recv()
        for cp in first + passed:
            cp.wait_send()
        mine.wait()
        acc = all_ref[0]
        for k in range(1, n_dev):
            acc = acc + all_ref[k]
        sum_ref[...] = acc

    vm = pl.BlockSpec(memory_space=pltpu.VMEM)
    return pl.pallas_call(
        body, name="allreduce_small", in_specs=[vm], out_specs=[vm, vm],
        out_shape=[jax.ShapeDtypeStruct((n_dev, r, c), F32), jax.ShapeDtypeStruct((r, c), F32)],
        scratch_shapes=[pltpu.SemaphoreType.DMA((7,)), pltpu.SemaphoreType.DMA((7,)), pltpu.SemaphoreType.DMA],
        compiler_params=pltpu.CompilerParams(has_side_effects=True, vmem_limit_bytes=VMEM_LIMIT),
    )(packed)[1]


def local_step(x, target, vecs, w_s, bs_t, bg, wg_in, late, core=None):
    on_mesh = core is not None

    def add(names, grads, recv):
        return [add_halves("add_" + n, g, r, core, min(r.shape[1], 256)) for n, g, r in zip(names, grads, recv)]

    g_pre, ln_g, ln_b, g_post, g_fpre, g_fpost = vecs
    s = x.shape[0]
    logc = _attn_tables(s)
    ka, kb = _alibi_tables(s)

    h = norm_pre(x, g_pre)
    wg_a, wg_b, wg_out, wg_ff1, wg_ff2 = late
    if on_mesh:
        cut = D // 4
        z, (wg_a, wg_b, wg_out) = mm_in(h, wg_in, [wg_a, wg_b, wg_out])
        ya, (wg_ff2,) = gating_fwd(z, ln_g, ln_b, w_s, bs_t, [Span(wg_ff2, 0, cut)])
        yb, lse, (wg_ff1, wg_ff2) = attn_fwd(z, logc, ka, kb, [wg_ff1, Span(wg_ff2, cut, D)])
    else:
        z, _ = mm_in(h, wg_in, [])
        ya, _ = gating_fwd(z, ln_g, ln_b, w_s, bs_t, [])
        yb, lse, _ = attn_fwd(z, logc, ka, kb, [])
    w_a, w_b, w_out, w_ff2 = wg_a.reshape(D, D), wg_b.reshape(D, D), wg_out.reshape(D, D), wg_ff2.reshape(D_FF, D)
    merged, pa, pb = proj_merge(ya, yb, w_a, w_b, z, bg)
    o, x1, h2 = out_norm(merged, w_out, x, g_post, g_fpre)
    a, rl = mm_ff1(h2, wg_ff1)
    dy, df, d_gfpost, loss = ff2_loss(rl, w_ff2, x1, target, g_fpost)

    half_cols = pl.BlockSpec((D, D // 2), lambda i, j: (0, j))
    d_wff2 = mm_tn("dw_ff2", rl, df, D // 2, D, (D_FF, D), pl.BlockSpec((D // 2, D), lambda i, j: (i, 0)))
    da = ff2_bwd(df, w_ff2, a)
    d_wff1 = mm_tn("dw_ff1", h2, da, D, D // 2, (N_CHIPS, D, D),
                   pl.BlockSpec((None, D, D // 2), lambda i, j: (j // 2, 0, j % 2)))
    d_ff = [d_wff1, d_wff2.reshape(N_CHIPS, D, D)]
    dx1, do, d_gfpre, d_gpost, recv_ff = ff1_bwd_norms(da, wg_ff1, x1, o, dy, g_fpre, g_post, d_ff if on_mesh else [])
    d_wout = mm_tn("dw_out", merged, do, D, D // 2, (D, D), half_cols)
    dpa, dpb, dga, dgb, d_bg = out_bwd_gates(do, w_out, pa, pb, z, bg)
    d_wa = mm_tn("dw_a", ya, dpa, D, D // 2, (D, D), half_cols)
    d_wb = mm_tn("dw_b", yb, dpb, D, D // 2, (D, D), half_cols)
    dya = mm_nt("dy_a", dpa, w_a)
    dyb = mm_nt("dy_b", dpb, w_b)
    d_proj = [d_wa.reshape(N_CHIPS, D // N_CHIPS, D), d_wb.reshape(N_CHIPS, D // N_CHIPS, D),
              d_wout.reshape(N_CHIPS, D // N_CHIPS, D)]
    du, dv, d_ws, d_bs, d_lng, d_lnb, recv_proj = gating_bwd(z, dya, ln_g, ln_b, w_s, bs_t, d_proj if on_mesh else [])
    early = d_proj + d_ff
    parts_early = add(BIG[1:], early, list(recv_proj) + list(recv_ff)) if on_mesh else []
    dq, dk, dvb, got_early = attn_bwd(z, yb, dyb, lse, logc, ka, kb, parts_early)
    dz = jnp.concatenate([du, dv, dq, dk, dvb, dga, dgb], axis=1)
    if on_mesh:
        for_sibling, _ = dw_in_half("dw_in_sibling", h, dz, 1 - core, [])
        mine, from_sibling = dw_in_half("dw_in_mine", h, dz, core, [for_sibling])
        d_win = None
        parts_late = [add_halves("add_w_in", mine, from_sibling[0], jnp.zeros((1,), jnp.int32), 256)]
    else:
        half = IN_SHARD // 2
        d_win = mm_tn("dw_in", h, dz, D, half, (N_CHIPS, D, IN_SHARD),
                      pl.BlockSpec((None, D, half), lambda i, j: (j // 2, 0, j % 2)))
        parts_late = []
    dx, d_gpre, got_late = in_bwd_norm(dz, wg_in, x, dx1, g_pre, parts_late)

    small = dict(norm_mix_pre=d_gpre, b_gate=d_bg, ln_v_g=d_lng, ln_v_b=d_lnb, w_s=d_ws, b_s=d_bs[:, 0, :],
                 norm_mix_post=d_gpost, norm_ffn_pre=d_gfpre, norm_ffn_post=d_gfpost)
    return loss[0, 0], dx, [d_win] + early, small, parts_late + parts_early, list(got_late) + list(got_early)


BIG = ("w_in", "w_a_proj", "w_b_proj", "w_out", "w_ff1", "w_ff2")
SMALL = ("norm_mix_pre", "ln_v_g", "ln_v_b", "b_s", "norm_mix_post", "norm_ffn_pre", "norm_ffn_post", "w_s", "b_gate")
ORDER = ("norm_mix_pre", "w_in", "b_gate", "ln_v_g", "ln_v_b", "w_s", "b_s", "w_a_proj", "w_b_proj", "w_out",
         "norm_mix_post", "norm_ffn_pre", "w_ff1", "w_ff2", "norm_ffn_post")
VEC_ROWS = D // 128
WS_ROW = 7 * VEC_ROWS
BG_ROW = WS_ROW + GROUPS * CHUNK
LOSS_ROW = BG_ROW + 2 * VEC_ROWS
PACK_ROWS = LOSS_ROW + 8


def pack_small(small, loss):
    vectors = [small[n] for n in SMALL[:7]]

    def body(*refs):
        out = refs[-1]
        ws_ref, bg_ref, loss_ref = refs[7:10]
        for i, n in enumerate(SMALL[:7]):
            if n == "b_s":
                out[i * VEC_ROWS:(i + 1) * VEC_ROWS, :] = refs[i][...]
            else:
                for j in range(VEC_ROWS):
                    out[i * VEC_ROWS + j:i * VEC_ROWS + j + 1, :] = refs[i][:, j * 128:(j + 1) * 128]
        for g in range(GROUPS):
            out[WS_ROW + g * CHUNK:WS_ROW + (g + 1) * CHUNK, :] = ws_ref[g]
        for r in range(2):
            for j in range(VEC_ROWS):
                row = BG_ROW + r * VEC_ROWS + j
                out[row:row + 1, :] = bg_ref[r:r + 1, j * 128:(j + 1) * 128]
        lane = lax.broadcasted_iota(jnp.int32, (8, 128), 1)
        sub = lax.broadcasted_iota(jnp.int32, (8, 128), 0)
        out[LOSS_ROW:LOSS_ROW + 8, :] = jnp.where((lane == 0) & (sub == 0), loss_ref[...], 0.0)

    return pl.pallas_call(
        body, name="pack_small", out_shape=jax.ShapeDtypeStruct((PACK_ROWS, 128), F32),
        compiler_params=_params(),
    )(*vectors, small["w_s"], small["b_gate"], loss)


def adamw_small(summed, chip, w, m, v):
    shapes = {n: (1, D) for n in SMALL}
    shapes.update(b_s=(GROUPS, CHUNK), w_s=(GROUPS * CHUNK, CHUNK), b_gate=(2, D // N_CHIPS))
    flat = lambda t: [t[n].reshape(shapes[n]) for n in SMALL]
    per = D // N_CHIPS // 128

    def body(chip_ref, sum_ref, *refs):
        params, outs = refs[:27], refs[27:]
        sub = lax.broadcasted_iota(jnp.int32, (VEC_ROWS, 128), 0)

        def gate_row(r):
            rows = sum_ref[BG_ROW + r * VEC_ROWS:BG_ROW + (r + 1) * VEC_ROWS, :]
            return jnp.concatenate([jnp.sum(jnp.where(sub == per * chip_ref[0] + j, rows, 0.0), axis=0, keepdims=True)
                                    for j in range(per)], axis=1)

        for i, n in enumerate(SMALL):
            if n == "b_s":
                g = sum_ref[i * VEC_ROWS:(i + 1) * VEC_ROWS, :]
            elif n == "w_s":
                g = sum_ref[WS_ROW:BG_ROW, :]
            elif n == "b_gate":
                g = jnp.concatenate([gate_row(0), gate_row(1)], axis=0)
            else:
                g = jnp.concatenate([sum_ref[i * VEC_ROWS + j:i * VEC_ROWS + j + 1, :] for j in range(VEC_ROWS)],
                                    axis=1)
            delta, nm, nv = _adamw_math(params[i][...], g, params[9 + i][...], params[18 + i][...])
            outs[4 * i][...], outs[4 * i + 1][...], outs[4 * i + 2][...], outs[4 * i + 3][...] = g, delta, nm, nv

    vm = pl.BlockSpec(memory_space=pltpu.VMEM)
    res = pl.pallas_call(
        body, name="adamw_small",
        in_specs=[pl.BlockSpec(memory_space=pltpu.SMEM)] + [vm] * 28, out_specs=[vm] * 36,
        out_shape=[jax.ShapeDtypeStruct(shapes[n], F32) for n in SMALL for _ in range(4)],
        compiler_params=_params(),
    )(chip, summed, *flat(w), *flat(m), *flat(v))
    return {n: tuple(r.reshape(w[n].shape) for r in res[4 * i:4 * i + 4]) for i, n in enumerate(SMALL)}


def kernel(x, norm_mix_pre, w_in, b_gate, ln_v_g, ln_v_b, w_s, b_s, w_a_proj, w_b_proj, w_out, norm_mix_post, norm_ffn_pre, w_ff1, w_ff2, norm_ffn_post, loss_target, m_norm_mix_pre, m_w_in, m_b_gate, m_ln_v_g, m_ln_v_b, m_w_s, m_b_s, m_w_a_proj, m_w_b_proj, m_w_out, m_norm_mix_post, m_norm_ffn_pre, m_w_ff1, m_w_ff2, m_norm_ffn_post, v_norm_mix_pre, v_w_in, v_b_gate, v_ln_v_g, v_ln_v_b, v_w_s, v_b_s, v_w_a_proj, v_w_b_proj, v_w_out, v_norm_mix_post, v_norm_ffn_pre, v_w_ff1, v_w_ff2, v_norm_ffn_post):
    w = dict(norm_mix_pre=norm_mix_pre, w_in=w_in, b_gate=b_gate, ln_v_g=ln_v_g, ln_v_b=ln_v_b, w_s=w_s, b_s=b_s,
             w_a_proj=w_a_proj, w_b_proj=w_b_proj, w_out=w_out, norm_mix_post=norm_mix_post,
             norm_ffn_pre=norm_ffn_pre, w_ff1=w_ff1, w_ff2=w_ff2, norm_ffn_post=norm_ffn_post)
    m = dict(norm_mix_pre=m_norm_mix_pre, w_in=m_w_in, b_gate=m_b_gate, ln_v_g=m_ln_v_g, ln_v_b=m_ln_v_b, w_s=m_w_s,
             b_s=m_b_s, w_a_proj=m_w_a_proj, w_b_proj=m_w_b_proj, w_out=m_w_out, norm_mix_post=m_norm_mix_post,
             norm_ffn_pre=m_norm_ffn_pre, w_ff1=m_w_ff1, w_ff2=m_w_ff2, norm_ffn_post=m_norm_ffn_post)
    v = dict(norm_mix_pre=v_norm_mix_pre, w_in=v_w_in, b_gate=v_b_gate, ln_v_g=v_ln_v_g, ln_v_b=v_ln_v_b, w_s=v_w_s,
             b_s=v_b_s, w_a_proj=v_w_a_proj, w_b_proj=v_w_b_proj, w_out=v_w_out, norm_mix_post=v_norm_mix_post,
             norm_ffn_pre=v_norm_ffn_pre, w_ff1=v_w_ff1, w_ff2=v_w_ff2, norm_ffn_post=v_norm_ffn_post)
    chip = 2 * lax.axis_index("x") + lax.axis_index("y")
    core = lax.axis_index("c")

    where = jnp.stack([chip, core]).astype(jnp.int32)
    placed = [place_shard("place_" + n, w[n][0], where, BF16, min(w[n].shape[1], 256)) for n in BIG]
    placed.append(place_shard("place_b_gate", jnp.pad(b_gate[0], ((0, 14), (0, 0))), where, F32, 16))
    wg_in, bg_all = gather_shards([placed[0], placed[6]])
    bg = jnp.transpose(bg_all[:, :2, :], (1, 0, 2)).reshape(2, D)
    vecs = (norm_mix_pre, ln_v_g, ln_v_b, norm_mix_post, norm_ffn_pre, norm_ffn_post)
    loss, dx, _, small, parts, got = local_step(
        x[0], loss_target[0], vecs, w_s[0], b_s[0].T, bg, wg_in, placed[1:6],
        core=jnp.reshape(core, (1,)).astype(jnp.int32))

    halves = [sum_chips("sum_" + n, p, r, where, min(p.shape[1], 256)) for n, p, r in zip(BIG, parts, got)]
    grads = dict(zip(BIG, join_halves(halves)))

    summed = allreduce_small(pack_small(small, loss.reshape(1, 1)))
    loss = summed[LOSS_ROW, 0]

    new = adamw_small(summed, jnp.reshape(chip, (1,)).astype(jnp.int32), w, m, v)
    for n in BIG:
        shape = w[n].shape
        res = adamw("adamw_" + n, w[n][0], grads[n], m[n][0], v[n][0], min(shape[1], 256))
        new[n] = tuple(r.reshape(shape) for r in res)

    outs = [loss, dx[None]]
    for i in range(4):
        outs += [new[n][i] for n in ORDER]
    return tuple(outs)
```

```python
import functools
import math
import typing

import numpy as np
import jax
import jax.numpy as jnp
from jax import lax
from jax.experimental import pallas as pl
from jax.experimental.pallas import tpu as pltpu

F32 = jnp.float32
BF16 = jnp.bfloat16
MESH = pl.DeviceIdType.MESH

D = 1024
EPS = 1e-6
CHUNK = 128
GROUPS = 8
HEADS = 16
HEAD_DIM = 64
ATT_T = 256
ATT_GROUP = 4
ATT_BWD_GROUP = 2
N_CHIPS = 4
D_FF = 4 * D
IN_COLS = 7 * D
IN_SHARD = IN_COLS // N_CHIPS
MASKED = -1e30
VMEM_LIMIT = 56 * 2 ** 20

ADAM_LR, ADAM_B1, ADAM_B2, ADAM_EPS, ADAM_WD, ADAM_STEP = 0.001, 0.9, 0.999, 1e-08, 0.01, 10

NN = (((1,), (0,)), ((), ()))
NT = (((1,), (1,)), ((), ()))
TN = (((0,), (0,)), ((), ()))


def _dot(a, b, dims=NN):
    return lax.dot_general(a, b, dims, preferred_element_type=F32)


def _params(*sem, communicates=False):
    return pltpu.CompilerParams(dimension_semantics=sem or None, vmem_limit_bytes=VMEM_LIMIT,
                                has_side_effects=communicates)


def _rows(tr, c, col=0):
    return pl.BlockSpec((tr, c), lambda i: (i, col))


def _full(shape):
    n = len(shape)
    return pl.BlockSpec(shape, lambda *_: (0,) * n)


def _gelu(x):
    k = math.sqrt(2.0 / math.pi)
    return 0.5 * x * (1.0 + jnp.tanh(k * (x + 0.044715 * x * x * x)))


def _gelu_and_grad(x):
    k = math.sqrt(2.0 / math.pi)
    t = jnp.tanh(k * (x + 0.044715 * x * x * x))
    g = 0.5 * x * (1.0 + t)
    dg = 0.5 * (1.0 + t) + 0.5 * x * (1.0 - t * t) * (k * (1.0 + 3.0 * 0.044715 * x * x))
    return g, dg


def _sigmoid(x):
    return 1.0 / (1.0 + jnp.exp(-x))


def _rms(x):
    r = lax.rsqrt(jnp.mean(x * x, axis=-1, keepdims=True) + EPS)
    return x * r, r


def _rms_bwd(dn, xhat, r):
    return r * (dn - xhat * jnp.mean(dn * xhat, axis=-1, keepdims=True))


def norm_pre(x, g):
    s = x.shape[0]
    tr = 512

    def body(x_ref, g_ref, h_ref):
        xhat, _ = _rms(x_ref[...])
        h_ref[...] = (xhat * g_ref[...]).astype(BF16)

    return pl.pallas_call(
        body, name="norm_pre", grid=(s // tr,),
        in_specs=[_rows(tr, D), _full((1, D))], out_specs=_rows(tr, D),
        out_shape=jax.ShapeDtypeStruct((s, D), BF16), compiler_params=_params("parallel"),
    )(x, g)


def mm_in(name, h, wg, order, first, count, z, gathering):
    s = h.shape[0]
    tm, tn = 1024, IN_SHARD // 2
    per = IN_SHARD // tn
    n = len(gathering)
    nj, ni = count * per, s // tm
    has_z = z is not None
    arrays = _arrays(gathering)
    at = [k for k, a in enumerate(arrays) if a is wg][0]

    def body(order_ref, *refs):
        a_ref = refs[0]
        o_ref = refs[1 + has_z + n]
        held = refs[2 + has_z + n:2 + has_z + 2 * n]
        tile, tile_sem = refs[2 + has_z + 2 * n:4 + has_z + 2 * n]
        j, i = pl.program_id(0), pl.program_id(1)
        send, pass_on, finish = _gather_phases(held, *refs[4 + has_z + 2 * n:], _spans(gathering))

        def fetch(t):
            chip = order_ref[first + t // per]
            return pltpu.make_async_copy(held[at].at[chip, :, pl.ds((t % per) * tn, tn)], tile.at[t % 2],
                                         tile_sem.at[t % 2])

        @pl.when(i == 0)
        def _():
            @pl.when(j == 0)
            def _():
                send()
                fetch(0).start()

            fetch(j).wait()

            @pl.when(j + 1 < nj)
            def _():
                fetch(j + 1).start()

        pl.when((j == nj - 1) & (i == 0))(pass_on)
        rows = pl.ds(pl.multiple_of(i * tm, tm), tm)
        o_ref[...] = _dot(a_ref[rows, :], tile[j % 2]).astype(BF16)
        pl.when((j == nj - 1) & (i == ni - 1))(finish)

    out = pl.pallas_call(
        body, name=name,
        grid_spec=pltpu.PrefetchScalarGridSpec(
            num_scalar_prefetch=1, grid=(nj, ni),
            in_specs=[pl.BlockSpec((s, D), lambda j, i, o: (0, 0))] + [ANY] * (has_z + n),
            out_specs=[pl.BlockSpec((tm, tn), lambda j, i, o: (i, o[first + j // per] * per + j % per))] + [ANY] * n,
            scratch_shapes=[pltpu.VMEM((2, D, tn), BF16), pltpu.SemaphoreType.DMA((2,))] + _gather_sems(n)),
        out_shape=[jax.ShapeDtypeStruct((s, IN_COLS), BF16)] + [jax.ShapeDtypeStruct(a.shape, a.dtype) for a in arrays],
        input_output_aliases={**({2: 0} if has_z else {}), **{2 + has_z + w: 1 + w for w in range(n)}},
        compiler_params=_params("arbitrary", "arbitrary", communicates=True),
    )(order, h, *([z] if has_z else []), *arrays)
    return out[0], out[1:]


def _tril_ws(ws_ref, g):
    r = lax.broadcasted_iota(jnp.int32, (CHUNK, CHUNK), 0)
    c = lax.broadcasted_iota(jnp.int32, (CHUNK, CHUNK), 1)
    return jnp.where(c <= r, ws_ref[g], 0.0).astype(BF16)


def _layer_norm(v):
    mu = jnp.mean(v, axis=-1, keepdims=True)
    d = v - mu
    rstd = lax.rsqrt(jnp.mean(d * d, axis=-1, keepdims=True) + EPS)
    return d * rstd, rstd


def gating_fwd(z, ln_g, ln_b, w_s, bs_t, gathering):
    s = z.shape[0]
    n = len(gathering)
    steps = s // CHUNK

    def body(*refs):
        u_ref, v_ref, lg_ref, lb_ref, ws_ref, bst_ref = refs[:6]
        ya_ref = refs[6 + n]
        ci = pl.program_id(0)
        if n:
            send, pass_on, finish = _gather_phases(refs[7 + n:7 + 2 * n], *refs[7 + 2 * n:], _spans(gathering))
            pl.when(ci == 0)(send)
            pl.when(ci == steps * 3 // 4)(pass_on)
        ug = _gelu(u_ref[...].astype(F32))
        vhat, _ = _layer_norm(_gelu(v_ref[...].astype(F32)))
        vn = (vhat * lg_ref[...] + lb_ref[...]).astype(BF16)
        for g in range(GROUPS):
            cols = slice(g * CHUNK, (g + 1) * CHUNK)
            mixed = _dot(_tril_ws(ws_ref, g), vn[:, cols]) + bst_ref[:, g:g + 1]
            ya_ref[:, cols] = (ug[:, cols] * mixed).astype(BF16)
        if n:
            pl.when(ci == steps - 1)(finish)

    out = pl.pallas_call(
        body, name="gating_fwd", grid=(steps,),
        in_specs=[_rows(CHUNK, D, 0), _rows(CHUNK, D, 1), _full((1, D)), _full((1, D)),
                  _full((GROUPS, CHUNK, CHUNK)), _full((CHUNK, GROUPS))] + [ANY] * n,
        out_specs=[_rows(CHUNK, D)] + [ANY] * n,
        out_shape=[jax.ShapeDtypeStruct((s, D), BF16)]
        + [jax.ShapeDtypeStruct(a.shape, a.dtype) for a in _arrays(gathering)],
        input_output_aliases={6 + w: 1 + w for w in range(n)},
        scratch_shapes=_gather_sems(n) if n else [],
        compiler_params=_params("arbitrary", communicates=bool(n)),
    )(z, z, ln_g, ln_b, w_s, bs_t, *_arrays(gathering))
    return out[0], out[1:]


def _attn_tables(s):
    nd = s // ATT_T
    r = np.arange(ATT_T)[None, :, None]
    c = np.arange(ATT_T)[None, None, :]
    delta = np.arange(nd)[:, None, None] * ATT_T + r - c
    count = np.zeros(delta.shape, np.int64)
    for window, dilation in ((128, 1), (512, 4), (2048, 16)):
        count += (delta >= 0) & (delta % dilation == 0) & (delta <= window)
    logc = np.where(count > 0, np.log(np.maximum(count, 1)), MASKED)
    return jnp.asarray(logc, F32)


AUG = 3


def _split3_np(x):
    terms, rest = [], np.asarray(x, np.float64)
    for _ in range(AUG):
        term = np.asarray(rest.astype(jnp.bfloat16), np.float64)
        terms.append(term)
        rest = rest - term
    return terms


def _split3(x):
    terms, rest = [], x
    for _ in range(AUG):
        term = rest.astype(BF16).astype(F32)
        terms.append(term)
        rest = rest - term
    return terms


def _alibi_tables(s):
    nb = s // ATT_T
    slopes = np.exp2(-8.0 * np.arange(1, HEADS + 1, dtype=np.float64) / HEADS)
    ka = np.zeros((HEADS // 2, 2, ATT_T, 128), np.float32)
    kb = np.zeros((HEADS // 2, 2, nb, 128), np.float32)
    for p in range(HEADS // 2):
        for e in range(2):
            base = HEAD_DIM * (1 - e)
            for a, term in enumerate(_split3_np(slopes[2 * p + e] * np.arange(ATT_T))):
                ka[p, e, :, base + a] = term
            for a, term in enumerate(_split3_np(slopes[2 * p + e] * ATT_T * np.arange(nb))):
                kb[p, e, :, base + AUG + a] = term
            ka[p, e, :, base + 2 * AUG:base + 3 * AUG] = 1.0
    return jnp.asarray(ka), jnp.asarray(kb)


def _head_masks():
    lane = lax.broadcasted_iota(jnp.int32, (1, 128), 1)
    first = lane < HEAD_DIM

    def ones(e, n):
        base = HEAD_DIM * (1 - e)
        return ((lane >= base) & (lane < base + n)).astype(F32)

    return first, lane, ones


def _place3(lane, at, terms, other):
    for a, term in enumerate(terms):
        other = jnp.where(lane == at + a, term, other)
    return other


def attn_fwd(z, logc, ka, kb, gathering):
    s = z.shape[0]
    nq = s // ATT_T
    t = ATT_T
    n = len(gathering)
    grp = ATT_GROUP
    ngrp = HEADS // 2 // grp
    wide = 128 * grp
    qcol, kcol, vcol = 2 * D // wide, 3 * D // wide, 4 * D // wide

    def body(*refs):
        q_ref, k_ref, v_ref, lc_ref, ka_ref, kb_ref = refs[:6]
        y_ref, lse_ref = refs[6 + n:8 + n]
        q_s, k_s, v_s, m_s, l_s, acc_s = refs[8 + 2 * n:14 + 2 * n]
        gi, qi = pl.program_id(0), pl.program_id(1)
        first, lane, ones = _head_masks()
        if n:
            send, pass_on, finish = _gather_phases(refs[8 + n:8 + 2 * n], *refs[14 + 2 * n:], _spans(gathering))
            pl.when((gi == 0) & (qi == 0))(send)
            pl.when((gi == ngrp - 1) & (qi == nq * 3 // 4))(pass_on)

        @pl.when(qi == 0)
        def _():
            sel = jnp.broadcast_to(first.astype(F32), (t, 128))
            for pr in range(grp):
                cols = slice(pr * 128, (pr + 1) * 128)
                for jb in range(nq):
                    kj = k_ref[jb * t:(jb + 1) * t, cols].astype(F32)
                    vj = v_ref[jb * t:(jb + 1) * t, cols].astype(F32)
                    k_s[pr, 0, jb] = jnp.where(first, kj, ka_ref[pr, 0] + kb_ref[pr, 0, jb:jb + 1, :]).astype(BF16)
                    k_s[pr, 1, jb] = jnp.where(first, ka_ref[pr, 1] + kb_ref[pr, 1, jb:jb + 1, :], kj).astype(BF16)
                    v_s[pr, jb, 0:t, 0:128] = jnp.where(first, vj, 0.0).astype(BF16)
                    v_s[pr, jb, t:2 * t, 0:128] = jnp.where(first, 0.0, vj).astype(BF16)
                    v_s[pr, jb, 0:t, 128:256] = sel.astype(BF16)
                    v_s[pr, jb, t:2 * t, 128:256] = (1.0 - sel).astype(BF16)

        for pr in range(grp):
            q = q_ref[:, pr * 128:(pr + 1) * 128].astype(F32) * (1.0 / math.sqrt(HEAD_DIM))
            q_s[pr, 0] = jnp.where(first, q, ones(0, 2 * AUG)).astype(BF16)
            q_s[pr, 1] = jnp.where(first, ones(1, 2 * AUG), q).astype(BF16)
        m_s[...] = jnp.full_like(m_s, MASKED)
        l_s[...] = jnp.zeros_like(l_s)
        acc_s[...] = jnp.zeros_like(acc_s)

        def scores(j):
            return tuple(_dot(q_s[pr, e], k_s[pr, e, j], NT) for pr in range(grp) for e in range(2))

        def step(j, carry):
            softmax_block(j, scores(j))
            return carry

        def softmax_block(j, u):
            lc = lc_ref[qi - j]
            for pr in range(grp):
                u0 = u[2 * pr] + lc
                u1 = u[2 * pr + 1] + lc
                m0, m1 = m_s[pr, 0], m_s[pr, 1]
                n0 = jnp.maximum(m0, jnp.max(u0, axis=-1, keepdims=True))
                n1 = jnp.maximum(m1, jnp.max(u1, axis=-1, keepdims=True))
                m_s[pr, 0], m_s[pr, 1] = n0, n1
                p = jnp.concatenate([jnp.exp(u0 - jnp.concatenate([n0, n0], axis=1)).astype(BF16),
                                     jnp.exp(u1 - jnp.concatenate([n1, n1], axis=1)).astype(BF16)], axis=1)
                pv = _dot(p, v_s[pr, j])
                alpha = jnp.where(first, jnp.exp(m0 - n0), jnp.exp(m1 - n1))
                acc_s[pr] = acc_s[pr] * alpha + pv[:, 0:128]
                l_s[pr] = l_s[pr] * alpha + pv[:, 128:256]

        lax.fori_loop(0, qi + 1, step, 0)
        for pr in range(grp):
            cols = slice(pr * 128, (pr + 1) * 128)
            y_ref[:, cols] = (acc_s[pr] / l_s[pr]).astype(BF16)
            lse_ref[:, cols] = jnp.where(first, m_s[pr, 0], m_s[pr, 1]) + jnp.log(l_s[pr])
        if n:
            pl.when((gi == ngrp - 1) & (qi == nq - 1))(finish)

    out = pl.pallas_call(
        body, name="attn_fwd", grid=(ngrp, nq),
        in_specs=[pl.BlockSpec((t, wide), lambda g, i: (i, qcol + g)),
                  pl.BlockSpec((s, wide), lambda g, i: (0, kcol + g)),
                  pl.BlockSpec((s, wide), lambda g, i: (0, vcol + g)),
                  _full((nq, t, t)),
                  pl.BlockSpec((grp, 2, t, 128), lambda g, i: (g, 0, 0, 0)),
                  pl.BlockSpec((grp, 2, nq, 128), lambda g, i: (g, 0, 0, 0))] + [ANY] * n,
        out_specs=[pl.BlockSpec((t, wide), lambda g, i: (i, g)), pl.BlockSpec((t, wide), lambda g, i: (i, g))]
        + [ANY] * n,
        out_shape=[jax.ShapeDtypeStruct((s, D), BF16), jax.ShapeDtypeStruct((s, D), F32)]
        + [jax.ShapeDtypeStruct(a.shape, a.dtype) for a in _arrays(gathering)],
        input_output_aliases={6 + w: 2 + w for w in range(n)},
        scratch_shapes=[pltpu.VMEM((grp, 2, t, 128), BF16), pltpu.VMEM((grp, 2, nq, t, 128), BF16),
                        pltpu.VMEM((grp, nq, 2 * t, 256), BF16), pltpu.VMEM((grp, 2, t, 128), F32),
                        pltpu.VMEM((grp, t, 128), F32), pltpu.VMEM((grp, t, 128), F32)]
        + (_gather_sems(n) if n else []),
        compiler_params=_params("arbitrary", "arbitrary", communicates=bool(n)),
    )(z, z, z, logc, ka, kb, *_arrays(gathering))
    return out[0], out[1], out[2:]


def proj_merge(ya, yb, wa, wb, z, bg):
    s = ya.shape[0]
    tm = 512

    def body(ya_ref, yb_ref, wa_ref, wb_ref, ga_ref, gb_ref, bg_ref, mg_ref, pa_ref, pb_ref):
        pa = _dot(ya_ref[...], wa_ref[...])
        pb = _dot(yb_ref[...], wb_ref[...])
        sa = _sigmoid(ga_ref[...] + bg_ref[0:1, :])
        sb = _sigmoid(gb_ref[...] + bg_ref[1:2, :])
        mg_ref[...] = (sa * pa + sb * pb).astype(BF16)
        pa_ref[...] = pa.astype(BF16)
        pb_ref[...] = pb.astype(BF16)

    out = jax.ShapeDtypeStruct((s, D), BF16)
    return pl.pallas_call(
        body, name="proj_merge", grid=(s // tm,),
        in_specs=[_rows(tm, D), _rows(tm, D), _full((D, D)), _full((D, D)),
                  _rows(tm, D, 5), _rows(tm, D, 6), _full((2, D))],
        out_specs=[_rows(tm, D)] * 3, out_shape=[out] * 3, compiler_params=_params("parallel"),
    )(ya, yb, wa, wb, z, z, bg)


def out_norm(merged, w_out, x, g_post, g_fpre):
    s = x.shape[0]
    tm = 512

    def body(mg_ref, w_ref, x_ref, gp_ref, gf_ref, o_ref, x1_ref, h2_ref):
        o = _dot(mg_ref[...], w_ref[...])
        ohat, _ = _rms(o)
        x1 = x_ref[...] + ohat * gp_ref[...]
        x1hat, _ = _rms(x1)
        o_ref[...] = o
        x1_ref[...] = x1
        h2_ref[...] = (x1hat * gf_ref[...]).astype(BF16)

    return pl.pallas_call(
        body, name="out_norm", grid=(s // tm,),
        in_specs=[_rows(tm, D), _full((D, D)), _rows(tm, D), _full((1, D)), _full((1, D))],
        out_specs=[_rows(tm, D)] * 3,
        out_shape=[jax.ShapeDtypeStruct((s, D), F32), jax.ShapeDtypeStruct((s, D), F32),
                   jax.ShapeDtypeStruct((s, D), BF16)],
        compiler_params=_params("parallel"),
    )(merged, w_out, x, g_post, g_fpre)


def mm_ff1(h2, wg):
    s = h2.shape[0]
    tm = 1024

    def body(a_ref, b_ref, o_ref, r_ref):
        a = _dot(a_ref[...], b_ref[...])
        o_ref[...] = a.astype(BF16)
        r = jnp.maximum(a, 0.0)
        r_ref[...] = (r * r).astype(BF16)

    return pl.pallas_call(
        body, name="mm_ff1", grid=(s // tm, N_CHIPS),
        in_specs=[pl.BlockSpec((tm, D), lambda i, j: (i, 0)), pl.BlockSpec((None, D, D), lambda i, j: (j, 0, 0))],
        out_specs=[pl.BlockSpec((tm, D), lambda i, j: (i, j))] * 2,
        out_shape=[jax.ShapeDtypeStruct((s, D_FF), BF16), jax.ShapeDtypeStruct((s, D_FF), BF16)],
        compiler_params=_params("parallel", "parallel"),
    )(h2, wg)


def ff2_loss(rl, w_ff2, x1, target, g_fpost):
    s = x1.shape[0]
    tm = 256

    def body(rl_ref, w_ref, x1_ref, t_ref, g_ref, dy_ref, df_ref, dg_ref, loss_ref):
        @pl.when(pl.program_id(0) == 0)
        def _():
            dg_ref[...] = jnp.zeros_like(dg_ref)
            loss_ref[...] = jnp.zeros_like(loss_ref)

        f = _dot(rl_ref[...], w_ref[...])
        fhat, r = _rms(f)
        err = x1_ref[...] + fhat * g_ref[...] - t_ref[...]
        loss_ref[...] += 0.5 * jnp.sum(jnp.mean(err * err, axis=-1, keepdims=True), axis=0, keepdims=True)
        dy = err * (1.0 / D)
        dy_ref[...] = dy
        dg_ref[...] += jnp.sum(dy * fhat, axis=0, keepdims=True)
        df_ref[...] = _rms_bwd(dy * g_ref[...], fhat, r).astype(BF16)

    return pl.pallas_call(
        body, name="ff2_loss", grid=(s // tm,),
        in_specs=[_rows(tm, D_FF), _full((D_FF, D)), _rows(tm, D), _rows(tm, D), _full((1, D))],
        out_specs=[_rows(tm, D), _rows(tm, D), _full((1, D)), _full((1, 1))],
        out_shape=[jax.ShapeDtypeStruct((s, D), F32), jax.ShapeDtypeStruct((s, D), BF16),
                   jax.ShapeDtypeStruct((1, D), F32), jax.ShapeDtypeStruct((1, 1), F32)],
        compiler_params=_params("arbitrary"),
    )(rl, w_ff2, x1, target, g_fpost)


def mm_tn(name, a, b, ta, tb, out_shape, out_spec):
    s = a.shape[0]

    def body(a_ref, b_ref, o_ref):
        o_ref[...] = _dot(a_ref[...], b_ref[...], TN)

    return pl.pallas_call(
        body, name=name, grid=(a.shape[1] // ta, b.shape[1] // tb),
        in_specs=[pl.BlockSpec((s, ta), lambda i, j: (0, i)), pl.BlockSpec((s, tb), lambda i, j: (0, j))],
        out_specs=out_spec, out_shape=jax.ShapeDtypeStruct(out_shape, F32),
        compiler_params=_params("parallel", "parallel"),
    )(a, b)


def mm_nt(name, a, w):
    s = a.shape[0]
    tm = 512

    def body(a_ref, w_ref, o_ref):
        o_ref[...] = _dot(a_ref[...], w_ref[...], NT).astype(BF16)

    return pl.pallas_call(
        body, name=name, grid=(s // tm,), in_specs=[_rows(tm, D), _full((D, D))], out_specs=_rows(tm, D),
        out_shape=jax.ShapeDtypeStruct((s, D), BF16), compiler_params=_params("parallel"),
    )(a, w)


def ff2_bwd(df, w_ff2, a):
    s = df.shape[0]
    tm = 1024

    def body(df_ref, w_ref, a_ref, da_ref):
        drl = _dot(df_ref[...], w_ref[...], NT)
        da_ref[...] = (drl * (2.0 * jnp.maximum(a_ref[...].astype(F32), 0.0))).astype(BF16)

    return pl.pallas_call(
        body, name="ff2_bwd", grid=(s // tm, D_FF // D),
        in_specs=[pl.BlockSpec((tm, D), lambda i, j: (i, 0)), pl.BlockSpec((D, D), lambda i, j: (j, 0)),
                  pl.BlockSpec((tm, D), lambda i, j: (i, j))],
        out_specs=pl.BlockSpec((tm, D), lambda i, j: (i, j)),
        out_shape=jax.ShapeDtypeStruct((s, D_FF), BF16), compiler_params=_params("parallel", "parallel"),
    )(df, w_ff2, a)


def ff1_bwd_norms(da, wg, x1, o, dy, g_fpre, g_post, swapping):
    s = x1.shape[0]
    tm = 512
    n = len(swapping)

    def body(*refs):
        da_ref, w_ref, x1_ref, o_ref, dy_ref, gf_ref, gp_ref = refs[:7]
        dx1_ref, do_ref, dgf_ref, dgp_ref = refs[7 + n:11 + n]
        acc_ref = refs[11 + 2 * n]
        i, k = pl.program_id(0), pl.program_id(1)
        if n:
            send, finish = _swap_phases(refs[7:7 + n], refs[11 + n:11 + 2 * n], *refs[12 + 2 * n:])
            pl.when((i == 0) & (k == 0))(send)

        @pl.when((i == 0) & (k == 0))
        def _():
            dgf_ref[...] = jnp.zeros_like(dgf_ref)
            dgp_ref[...] = jnp.zeros_like(dgp_ref)

        part = _dot(da_ref[...], w_ref[...], NT)

        @pl.when(k == 0)
        def _():
            acc_ref[...] = part

        @pl.when(k > 0)
        def _():
            acc_ref[...] += part

        @pl.when(k == N_CHIPS - 1)
        def _():
            dh2 = acc_ref[...]
            x1hat, r2 = _rms(x1_ref[...])
            dgf_ref[...] += jnp.sum(dh2 * x1hat, axis=0, keepdims=True)
            dx1 = dy_ref[...] + _rms_bwd(dh2 * gf_ref[...], x1hat, r2)
            ohat, r1 = _rms(o_ref[...])
            dgp_ref[...] += jnp.sum(dx1 * ohat, axis=0, keepdims=True)
            dx1_ref[...] = dx1
            do_ref[...] = _rms_bwd(dx1 * gp_ref[...], ohat, r1).astype(BF16)

        if n:
            pl.when((i == s // tm - 1) & (k == N_CHIPS - 1))(finish)

    row = pl.BlockSpec((tm, D), lambda i, k: (i, 0))
    vec = pl.BlockSpec((1, D), lambda i, k: (0, 0))
    res = pl.pallas_call(
        body, name="ff1_bwd_norms", grid=(s // tm, N_CHIPS),
        in_specs=[pl.BlockSpec((tm, D), lambda i, k: (i, k)), pl.BlockSpec((None, D, D), lambda i, k: (k, 0, 0)),
                  row, row, row, vec, vec] + [ANY] * n,
        out_specs=[row, row, vec, vec] + [ANY] * n,
        out_shape=[jax.ShapeDtypeStruct((s, D), F32), jax.ShapeDtypeStruct((s, D), BF16),
                   jax.ShapeDtypeStruct((1, D), F32), jax.ShapeDtypeStruct((1, D), F32)] + _swap_shapes(swapping),
        scratch_shapes=[pltpu.VMEM((tm, D), F32)] + (_swap_sems(n) if n else []),
        compiler_params=_params("arbitrary", "arbitrary", communicates=bool(n)),
    )(da, wg, x1, o, dy, g_fpre, g_post, *swapping)
    return res[0], res[1], res[2], res[3], res[4:]


def out_bwd_gates(do, w_out, pa, pb, z, bg):
    s = do.shape[0]
    tm = 512

    def body(do_ref, w_ref, pa_ref, pb_ref, ga_ref, gb_ref, bg_ref, dpa_ref, dpb_ref, dga_ref, dgb_ref, dbg_ref):
        @pl.when(pl.program_id(0) == 0)
        def _():
            dbg_ref[...] = jnp.zeros_like(dbg_ref)

        dm = _dot(do_ref[...], w_ref[...], NT)
        sa = _sigmoid(ga_ref[...] + bg_ref[0:1, :])
        sb = _sigmoid(gb_ref[...] + bg_ref[1:2, :])
        dpa_ref[...] = (dm * sa).astype(BF16)
        dpb_ref[...] = (dm * sb).astype(BF16)
        dga = dm * pa_ref[...].astype(F32) * (sa * (1.0 - sa))
        dgb = dm * pb_ref[...].astype(F32) * (sb * (1.0 - sb))
        dga_ref[...] = dga.astype(BF16)
        dgb_ref[...] = dgb.astype(BF16)
        dbg_ref[0:1, :] += jnp.sum(dga, axis=0, keepdims=True)
        dbg_ref[1:2, :] += jnp.sum(dgb, axis=0, keepdims=True)

    out = jax.ShapeDtypeStruct((s, D), BF16)
    return pl.pallas_call(
        body, name="out_bwd_gates", grid=(s // tm,),
        in_specs=[_rows(tm, D), _full((D, D)), _rows(tm, D), _rows(tm, D), _rows(tm, D, 5), _rows(tm, D, 6),
                  _full((2, D))],
        out_specs=[_rows(tm, D)] * 4 + [_full((2, D))],
        out_shape=[out] * 4 + [jax.ShapeDtypeStruct((2, D), F32)], compiler_params=_params("arbitrary"),
    )(do, w_out, pa, pb, z, z, bg)


def gating_bwd(z, dya, ln_g, ln_b, w_s, bs_t, swapping):
    s = z.shape[0]
    ones = functools.partial(jnp.ones, (8, CHUNK), BF16)
    n = len(swapping)

    def body(*refs):
        u_ref, v_ref, dya_ref, lg_ref, lb_ref, ws_ref, bst_ref = refs[:7]
        du_ref, dv_ref, dws_ref, dbs_ref, dlg_ref, dlb_ref = refs[7 + n:13 + n]
        dvn_ref = refs[13 + 2 * n]
        ci = pl.program_id(0)
        if n:
            send, finish = _swap_phases(refs[7:7 + n], refs[13 + n:13 + 2 * n], *refs[14 + 2 * n:])
            pl.when(ci == 0)(send)

        @pl.when(ci == 0)
        def _():
            dws_ref[...] = jnp.zeros_like(dws_ref)
            dbs_ref[...] = jnp.zeros_like(dbs_ref)
            dlg_ref[...] = jnp.zeros_like(dlg_ref)
            dlb_ref[...] = jnp.zeros_like(dlb_ref)

        ug, dug_du = _gelu_and_grad(u_ref[...].astype(F32))
        vg, dvg_dv = _gelu_and_grad(v_ref[...].astype(F32))
        vhat, rstd = _layer_norm(vg)
        vn = (vhat * lg_ref[...] + lb_ref[...]).astype(BF16)
        dya = dya_ref[...].astype(F32)
        for g in range(GROUPS):
            cols = slice(g * CHUNK, (g + 1) * CHUNK)
            ws = _tril_ws(ws_ref, g)
            mixed = _dot(ws, vn[:, cols]) + bst_ref[:, g:g + 1]
            du_ref[:, cols] = (dya[:, cols] * mixed * dug_du[:, cols]).astype(BF16)
            dmix = (dya[:, cols] * ug[:, cols]).astype(BF16)
            dbs_ref[g] += _dot(ones(), dmix, NT)
            dws_ref[g] += _dot(dmix, vn[:, cols], NT)
            dvn_ref[:, cols] = _dot(ws, dmix, TN)
        dvn = dvn_ref[...]
        dlg_ref[...] += jnp.sum(dvn * vhat, axis=0, keepdims=True)
        dlb_ref[...] += jnp.sum(dvn, axis=0, keepdims=True)
        dvh = dvn * lg_ref[...]
        dvg = rstd * (dvh - jnp.mean(dvh, axis=-1, keepdims=True)
                      - vhat * jnp.mean(dvh * vhat, axis=-1, keepdims=True))
        dv_ref[...] = (dvg * dvg_dv).astype(BF16)

        @pl.when(ci == pl.num_programs(0) - 1)
        def _():
            r = lax.broadcasted_iota(jnp.int32, (CHUNK, CHUNK), 0)
            c = lax.broadcasted_iota(jnp.int32, (CHUNK, CHUNK), 1)
            for g in range(GROUPS):
                dws_ref[g] = jnp.where(c <= r, dws_ref[g], 0.0)

        if n:
            pl.when(ci == pl.num_programs(0) - 1)(finish)

    out = jax.ShapeDtypeStruct((s, D), BF16)
    res = pl.pallas_call(
        body, name="gating_bwd", grid=(s // CHUNK,),
        in_specs=[_rows(CHUNK, D, 0), _rows(CHUNK, D, 1), _rows(CHUNK, D), _full((1, D)), _full((1, D)),
                  _full((GROUPS, CHUNK, CHUNK)), _full((CHUNK, GROUPS))] + [ANY] * n,
        out_specs=[_rows(CHUNK, D), _rows(CHUNK, D), _full((GROUPS, CHUNK, CHUNK)), _full((GROUPS, 8, CHUNK)),
                   _full((1, D)), _full((1, D))] + [ANY] * n,
        out_shape=[out, out, jax.ShapeDtypeStruct((GROUPS, CHUNK, CHUNK), F32),
                   jax.ShapeDtypeStruct((GROUPS, 8, CHUNK), F32),
                   jax.ShapeDtypeStruct((1, D), F32), jax.ShapeDtypeStruct((1, D), F32)] + _swap_shapes(swapping),
        scratch_shapes=[pltpu.VMEM((CHUNK, D), F32)] + (_swap_sems(n) if n else []),
        compiler_params=_params("arbitrary", communicates=bool(n)),
    )(z, z, dya, ln_g, ln_b, w_s, bs_t, *swapping)
    return (*res[:6], res[6:])


def attn_bwd(z, yb, dyb, lse, logc, ka, kb, scattering):
    s = z.shape[0]
    nq = s // ATT_T
    t = ATT_T
    grp = ATT_BWD_GROUP
    ngrp = HEADS // 2 // grp
    wide = 128 * grp
    qcol, kcol, vcol = 2 * D // wide, 3 * D // wide, 4 * D // wide
    scale = 1.0 / math.sqrt(HEAD_DIM)
    n = len(scattering)

    def body(*refs):
        q_ref, k_ref, v_ref, y_ref, dy_ref, lse_ref, lc_ref, ka_ref, kb_ref = refs[:9]
        dq_ref, dk_ref, dv_ref = refs[9 + n:12 + n]
        qa_s, qt_s, da_s, dt_s, dq_s, dkt_s, dvt_s = refs[12 + 2 * n:19 + 2 * n]
        gi, j = pl.program_id(0), pl.program_id(1)
        first, lane, ones = _head_masks()
        if n:
            send, finish = _scatter_phases(refs[9:9 + n], refs[12 + n:12 + 2 * n], *refs[19 + 2 * n:])
            pl.when((gi == 0) & (j == 0))(send)

        @pl.when(j == 0)
        def _():
            dq_s[...] = jnp.zeros_like(dq_s)
            for pr in range(grp):
                cols = slice(pr * 128, (pr + 1) * 128)
                for ib in range(nq):
                    rows = slice(ib * t, (ib + 1) * t)
                    q = q_ref[rows, cols].astype(F32) * scale
                    lse = lse_ref[rows, cols]
                    qa_s[pr, 0, ib] = jnp.where(first, q, _place3(lane, HEAD_DIM + 2 * AUG, _split3(-lse[:, 0:1]),
                                                                  ones(0, 2 * AUG))).astype(BF16)
                    qa_s[pr, 1, ib] = jnp.where(
                        first, _place3(lane, 2 * AUG, _split3(-lse[:, HEAD_DIM:HEAD_DIM + 1]), ones(1, 2 * AUG)),
                        q).astype(BF16)
                    qt_s[pr, ib, :, 0:t] = jnp.where(first, q, 0.0).T.astype(BF16)
                    qt_s[pr, ib, :, t:2 * t] = jnp.where(first, 0.0, q).T.astype(BF16)
                    do = dy_ref[rows, cols].astype(F32)
                    prod = do * y_ref[rows, cols].astype(F32)
                    dd0 = jnp.sum(jnp.where(first, prod, 0.0), axis=-1, keepdims=True)
                    dd1 = jnp.sum(jnp.where(first, 0.0, prod), axis=-1, keepdims=True)
                    da_s[pr, 0, ib] = jnp.where(first, do, _place3(lane, HEAD_DIM, _split3(-dd0), 0.0)).astype(BF16)
                    da_s[pr, 1, ib] = jnp.where(first, _place3(lane, 0, _split3(-dd1), 0.0), do).astype(BF16)
                    dt_s[pr, ib, :, 0:t] = jnp.where(first, do, 0.0).T.astype(BF16)
                    dt_s[pr, ib, :, t:2 * t] = jnp.where(first, 0.0, do).T.astype(BF16)

        keys = []
        for pr in range(grp):
            kj = k_ref[:, pr * 128:(pr + 1) * 128].astype(F32)
            vj = v_ref[:, pr * 128:(pr + 1) * 128].astype(F32)
            keys.append((
                jnp.where(first, kj, ka_ref[pr, 0] + kb_ref[pr, 0, pl.ds(j, 1), :]).astype(BF16),
                jnp.where(first, ka_ref[pr, 1] + kb_ref[pr, 1, pl.ds(j, 1), :], kj).astype(BF16),
                jnp.concatenate([jnp.where(first, kj, 0.0), jnp.where(first, 0.0, kj)], axis=0).astype(BF16),
                jnp.where(first, vj, ones(0, AUG)).astype(BF16),
                jnp.where(first, ones(1, AUG), vj).astype(BF16)))
        dkt_s[...] = jnp.zeros_like(dkt_s)
        dvt_s[...] = jnp.zeros_like(dvt_s)

        def step(i, _):
            lc = lc_ref[i - j]
            rows = pl.ds(pl.multiple_of(i * t, t), t)
            for pr in range(grp):
                k0a, k1a, kst, v0a, v1a = keys[pr]
                p0 = jnp.exp(_dot(qa_s[pr, 0, i], k0a, NT) + lc)
                p1 = jnp.exp(_dot(qa_s[pr, 1, i], k1a, NT) + lc)
                e0 = (p0 * _dot(da_s[pr, 0, i], v0a, NT)).astype(BF16)
                e1 = (p1 * _dot(da_s[pr, 1, i], v1a, NT)).astype(BF16)
                dq_s[pr, rows, :] += _dot(jnp.concatenate([e0, e1], axis=1), kst)
                dvt_s[pr] += _dot(dt_s[pr, i], jnp.concatenate([p0.astype(BF16), p1.astype(BF16)], axis=0))
                dkt_s[pr] += _dot(qt_s[pr, i], jnp.concatenate([e0, e1], axis=0))
            return 0

        lax.fori_loop(j, nq, step, 0)
        for pr in range(grp):
            dk_ref[:, pr * 128:(pr + 1) * 128] = dkt_s[pr].T.astype(BF16)
            dv_ref[:, pr * 128:(pr + 1) * 128] = dvt_s[pr].T.astype(BF16)

        @pl.when(j == nq - 1)
        def _():
            for pr in range(grp):
                dq_ref[:, pr * 128:(pr + 1) * 128] = (dq_s[pr] * scale).astype(BF16)

        if n:
            pl.when((gi == ngrp - 1) & (j == nq - 1))(finish)

    colblock = lambda c: pl.BlockSpec((s, wide), lambda g, j: (0, c + g))
    blk = lambda c: pl.BlockSpec((t, wide), lambda g, j: (j, c + g))
    out = jax.ShapeDtypeStruct((s, D), BF16)
    res = pl.pallas_call(
        body, name="attn_bwd", grid=(ngrp, nq),
        in_specs=[colblock(qcol), blk(kcol), blk(vcol), colblock(0), colblock(0), colblock(0),
                  _full((nq, t, t)), pl.BlockSpec((grp, 2, t, 128), lambda g, j: (g, 0, 0, 0)),
                  pl.BlockSpec((grp, 2, nq, 128), lambda g, j: (g, 0, 0, 0))] + [ANY] * n,
        out_specs=[colblock(0), blk(0), blk(0)] + [ANY] * n, out_shape=[out] * 3 + _scatter_shapes(scattering),
        scratch_shapes=[pltpu.VMEM((grp, 2, nq, t, 128), BF16), pltpu.VMEM((grp, nq, 128, 2 * t), BF16),
                        pltpu.VMEM((grp, 2, nq, t, 128), BF16), pltpu.VMEM((grp, nq, 128, 2 * t), BF16),
                        pltpu.VMEM((grp, s, 128), F32), pltpu.VMEM((grp, 128, t), F32),
                        pltpu.VMEM((grp, 128, t), F32)]
        + (_scatter_sems(n) if n else []),
        compiler_params=_params("arbitrary", "arbitrary", communicates=bool(n)),
    )(z, z, z, yb, dyb, lse, logc, ka, kb, *scattering)
    return res[0], res[1], res[2], res[3:]


def in_bwd_norm(dz, wg, x, dx1, g_pre, scattering):
    s = x.shape[0]
    tm = 512
    n = len(scattering)

    def body(*refs):
        dz_ref, w_ref, x_ref, dx1_ref, g_ref = refs[:5]
        dx_ref, dg_ref = refs[5 + n:7 + n]
        acc_ref = refs[7 + 2 * n]
        i, k = pl.program_id(0), pl.program_id(1)
        if n:
            send, finish = _scatter_phases(refs[5:5 + n], refs[7 + n:7 + 2 * n], *refs[8 + 2 * n:])
            pl.when((i == 0) & (k == 0))(send)

        @pl.when((i == 0) & (k == 0))
        def _():
            dg_ref[...] = jnp.zeros_like(dg_ref)

        part = _dot(dz_ref[...], w_ref[...], NT)

        @pl.when(k == 0)
        def _():
            acc_ref[...] = part

        @pl.when(k > 0)
        def _():
            acc_ref[...] += part

        @pl.when(k == N_CHIPS - 1)
        def _():
            dh = acc_ref[...]
            xhat, r = _rms(x_ref[...])
            dg_ref[...] += jnp.sum(dh * xhat, axis=0, keepdims=True)
            dx_ref[...] = dx1_ref[...] + _rms_bwd(dh * g_ref[...], xhat, r)

        if n:
            pl.when((i == s // tm - 1) & (k == N_CHIPS - 1))(finish)

    row = pl.BlockSpec((tm, D), lambda i, k: (i, 0))
    vec = pl.BlockSpec((1, D), lambda i, k: (0, 0))
    res = pl.pallas_call(
        body, name="in_bwd_norm", grid=(s // tm, N_CHIPS),
        in_specs=[pl.BlockSpec((tm, IN_SHARD), lambda i, k: (i, k)),
                  pl.BlockSpec((None, D, IN_SHARD), lambda i, k: (k, 0, 0)), row, row, vec] + [ANY] * n,
        out_specs=[row, vec] + [ANY] * n,
        out_shape=[jax.ShapeDtypeStruct((s, D), F32), jax.ShapeDtypeStruct((1, D), F32)]
        + _scatter_shapes(scattering),
        scratch_shapes=[pltpu.VMEM((tm, D), F32)] + (_scatter_sems(n) if n else []),
        compiler_params=_params("arbitrary", "arbitrary", communicates=bool(n)),
    )(dz, wg, x, dx1, g_pre, *scattering)
    return res[0], res[1], res[2:]


def _adamw_math(w, g, m, v):
    m = ADAM_B1 * m + (1.0 - ADAM_B1) * g
    v = ADAM_B2 * v + (1.0 - ADAM_B2) * (g * g)
    m_hat = m / (1.0 - ADAM_B1 ** ADAM_STEP)
    v_hat = v / (1.0 - ADAM_B2 ** ADAM_STEP)
    delta = -ADAM_LR * (m_hat / (jnp.sqrt(v_hat) + ADAM_EPS) + ADAM_WD * w)
    return delta, m, v


def adamw(name, w, g, m, v, tr):
    r, c = w.shape

    def body(w_ref, g_ref, m_ref, v_ref, go_ref, d_ref, nm_ref, nv_ref):
        g = g_ref[...]
        go_ref[...] = g
        d_ref[...], nm_ref[...], nv_ref[...] = _adamw_math(w_ref[...], g, m_ref[...], v_ref[...])

    out = jax.ShapeDtypeStruct((r, c), F32)
    return pl.pallas_call(
        body, name=name, grid=(r // tr,), in_specs=[_rows(tr, c)] * 4, out_specs=[_rows(tr, c)] * 4,
        out_shape=[out] * 4, compiler_params=_params("parallel"),
    )(w, g, m, v)


def add_halves(name, g, recv, c_idx, tr):
    n, h, c = recv.shape

    def body(c_ref, g_ref, r_ref, o_ref):
        o_ref[...] = (g_ref[...] + r_ref[...]).astype(BF16)

    nb = h // tr
    return pl.pallas_call(
        body, name=name,
        grid_spec=pltpu.PrefetchScalarGridSpec(
            num_scalar_prefetch=1, grid=(n, nb),
            in_specs=[pl.BlockSpec((None, tr, c), lambda k, i, c_ref: (k, c_ref[0] * nb + i, 0)),
                      pl.BlockSpec((None, tr, c), lambda k, i, c_ref: (k, i, 0))],
            out_specs=pl.BlockSpec((None, tr, c), lambda k, i, c_ref: (k, i, 0))),
        out_shape=jax.ShapeDtypeStruct((n, h, c), BF16), compiler_params=_params("parallel", "parallel"),
    )(c_idx, g, recv)


def sum_chips(name, parts, recv, where, tr):
    n, h, c = recv.shape
    nb = h // tr

    def body(w_ref, p_ref, r_ref, o_ref):
        acc = p_ref[...].astype(F32)
        for k in range(n):
            acc = acc + r_ref[k].astype(F32)
        o_ref[...] = acc

    return pl.pallas_call(
        body, name=name,
        grid_spec=pltpu.PrefetchScalarGridSpec(
            num_scalar_prefetch=1, grid=(nb,),
            in_specs=[pl.BlockSpec((None, tr, c), lambda i, w_ref: (w_ref[0], i, 0)),
                      pl.BlockSpec((n, tr, c), lambda i, w_ref: (0, i, 0))],
            out_specs=pl.BlockSpec((tr, c), lambda i, w_ref: (w_ref[1] * nb + i, 0))),
        out_shape=jax.ShapeDtypeStruct((2 * h, c), F32), compiler_params=_params("parallel"),
    )(where, parts, recv)


def place_shard(name, shard, where, dtype, tr):
    r, c = shard.shape

    def body(w_ref, s_ref, o_ref):
        o_ref[...] = s_ref[...].astype(dtype)

    return pl.pallas_call(
        body, name=name,
        grid_spec=pltpu.PrefetchScalarGridSpec(
            num_scalar_prefetch=1, grid=(r // tr,),
            in_specs=[pl.BlockSpec((tr, c), lambda i, w_ref: (i, 0))],
            out_specs=pl.BlockSpec((None, tr, c), lambda i, w_ref: (w_ref[0], i, 0))),
        out_shape=jax.ShapeDtypeStruct((N_CHIPS, r, c), dtype), compiler_params=_params("parallel"),
    )(where, shard)


ANY = pl.BlockSpec(memory_space=pl.ANY)


def _place():
    x, y, c = lax.axis_index("x"), lax.axis_index("y"), lax.axis_index("c")
    chips = [(1 - x, y), (x, 1 - y), (1 - x, 1 - y)]
    return x, y, c, chips


def gather_shards(arrays):
    n = len(arrays)

    def body(*refs):
        send, pass_on, finish = _gather_phases(refs[n:2 * n], *refs[2 * n:], _spans(arrays))
        send()
        pass_on()
        finish()

    return pl.pallas_call(
        body, name="gather_shards", in_specs=[ANY] * n, out_specs=[ANY] * n,
        out_shape=[jax.ShapeDtypeStruct(a.shape, a.dtype) for a in _arrays(arrays)],
        input_output_aliases={w: w for w in range(n)}, scratch_shapes=_gather_sems(n),
        compiler_params=pltpu.CompilerParams(has_side_effects=True),
    )(*arrays)


def _gather_sems(n):
    return [pltpu.SemaphoreType.DMA((6 * n,)), pltpu.SemaphoreType.DMA((6 * n,))]


class Span(typing.NamedTuple):
    array: jax.Array
    lo: int
    hi: int
    ways: tuple = (0, 1, 2)


def _arrays(gathering):
    return [g.array if isinstance(g, Span) else g for g in gathering]


def _spans(gathering):
    return [(g.lo, g.hi, g.ways) if isinstance(g, Span) else (0, g.shape[1], (0, 1, 2)) for g in gathering]


def _gather_phases(out, send_sems, recv_sems, spans):
    n = len(out)
    if not any(ways for _, _, ways in spans):
        return (lambda: None,) * 3
    x, y, c, chips = _place()
    me = 2 * x + y
    sibling = (x, y, 1 - c)

    def half(w, chip, core):
        lo, hi, _ = spans[w]
        h = (hi - lo) // 2
        return out[w].at[chip, pl.ds(lo + core * h, h)]

    def copy(k, block, to):
        return pltpu.make_async_remote_copy(src_ref=block, dst_ref=block, send_sem=send_sems.at[k],
                                            recv_sem=recv_sems.at[k], device_id=to, device_id_type=MESH)

    def over_ici(w, j, chip):
        return copy(3 * w + j, half(w, chip, c), (chips[j][0], chips[j][1], c))

    def over_d2d(w, j, core):
        return copy(3 * n + 3 * w + j, half(w, 2 * chips[j][0] + chips[j][1], core), sibling)

    pairs = [(w, j) for w in range(n) for j in spans[w][2]]

    def send():
        for w, j in pairs:
            over_ici(w, j, me).start()

    def pass_on():
        for w, j in pairs:
            over_ici(w, j, 2 * chips[j][0] + chips[j][1]).wait_recv()
            over_d2d(w, j, c).start()

    def finish():
        for w, j in pairs:
            over_d2d(w, j, 1 - c).wait_recv()
        for w, j in pairs:
            over_ici(w, j, me).wait_send()
            over_d2d(w, j, c).wait_send()

    return send, pass_on, finish


def swap_halves(name, grads):
    n = len(grads)

    def body(*refs):
        send, finish = _swap_phases(refs[:n], refs[n:2 * n], *refs[2 * n:])
        send()
        finish()

    return pl.pallas_call(
        body, name=name, in_specs=[ANY] * n, out_specs=[ANY] * n, out_shape=_swap_shapes(grads),
        scratch_shapes=_swap_sems(n), compiler_params=pltpu.CompilerParams(has_side_effects=True),
    )(*grads)


def _swap_shapes(grads):
    return [jax.ShapeDtypeStruct((a.shape[0], a.shape[1] // 2, a.shape[2]), a.dtype) for a in grads]


def _swap_sems(n):
    return [pltpu.SemaphoreType.DMA((n,)), pltpu.SemaphoreType.DMA((n,))]


def _swap_phases(g, out, send_sems, recv_sems):
    x, y, c, _ = _place()

    def copies():
        return [pltpu.make_async_remote_copy(
            src_ref=g[w].at[:, pl.ds((1 - c) * (g[w].shape[1] // 2), g[w].shape[1] // 2)], dst_ref=out[w],
            send_sem=send_sems.at[w], recv_sem=recv_sems.at[w], device_id=(x, y, 1 - c), device_id_type=MESH)
            for w in range(len(g))]

    def send():
        for cp in copies():
            cp.start()

    def finish():
        for cp in copies():
            cp.wait()

    return send, finish


def _send_phases(g, out, send_sems, recv_sems):
    x, y, c, _ = _place()

    def copies():
        return [pltpu.make_async_remote_copy(
            src_ref=g[w], dst_ref=out[w], send_sem=send_sems.at[w], recv_sem=recv_sems.at[w],
            device_id=(x, y, 1 - c), device_id_type=MESH) for w in range(len(g))]

    def send():
        for cp in copies():
            cp.start()

    def finish():
        for cp in copies():
            cp.wait()

    return send, finish


def dw_in_half(name, h, dz, which, sending):
    s = h.shape[0]
    hh, tb = D // 2, IN_SHARD // 2
    n = len(sending)
    steps = IN_COLS // tb

    def body(w_ref, *refs):
        a_ref, b_ref, o_ref = refs[0], refs[1], refs[2 + n]
        j = pl.program_id(0)
        if n:
            send, finish = _send_phases(refs[2:2 + n], refs[3 + n:3 + 2 * n], *refs[3 + 2 * n:])
            pl.when(j == 0)(send)
        o_ref[...] = _dot(a_ref[...], b_ref[...], TN)
        if n:
            pl.when(j == steps - 1)(finish)

    out = pl.pallas_call(
        body, name=name,
        grid_spec=pltpu.PrefetchScalarGridSpec(
            num_scalar_prefetch=1, grid=(steps,),
            in_specs=[pl.BlockSpec((s, hh), lambda j, w: (0, w[0])), pl.BlockSpec((s, tb), lambda j, w: (0, j))]
            + [ANY] * n,
            out_specs=[pl.BlockSpec((None, hh, tb), lambda j, w: (j // 2, 0, j % 2))] + [ANY] * n,
            scratch_shapes=_swap_sems(n) if n else []),
        out_shape=[jax.ShapeDtypeStruct((N_CHIPS, hh, IN_SHARD), F32)]
        + [jax.ShapeDtypeStruct(a.shape, a.dtype) for a in sending],
        compiler_params=_params("arbitrary", communicates=bool(n)),
    )(which, h, dz, *sending)
    return out[0], out[1:]


def scatter_chips(parts):
    n = len(parts)

    def body(*refs):
        send, finish = _scatter_phases(refs[:n], refs[n:2 * n], *refs[2 * n:])
        send()
        finish()

    return pl.pallas_call(
        body, name="scatter_chips", in_specs=[ANY] * n, out_specs=[ANY] * n,
        out_shape=_scatter_shapes(parts), scratch_shapes=_scatter_sems(n),
        compiler_params=pltpu.CompilerParams(has_side_effects=True),
    )(*parts)


def _scatter_shapes(parts):
    return [jax.ShapeDtypeStruct((3,) + a.shape[1:], a.dtype) for a in parts]


def _scatter_sems(n):
    return [pltpu.SemaphoreType.DMA((3 * n,)), pltpu.SemaphoreType.DMA((3 * n,))]


def _scatter_phases(p, out, send_sems, recv_sems):
    x, y, c, chips = _place()

    def copies():
        return [pltpu.make_async_remote_copy(
            src_ref=p[w].at[2 * px + py], dst_ref=out[w].at[j], send_sem=send_sems.at[3 * w + j],
            recv_sem=recv_sems.at[3 * w + j], device_id=(px, py, c), device_id_type=MESH)
            for w in range(len(p)) for j, (px, py) in enumerate(chips)]

    def send():
        for cp in copies():
            cp.start()

    def finish():
        for cp in copies():
            cp.wait()

    return send, finish


def join_halves(arrays):
    n = len(arrays)

    def body(*refs):
        out = refs[n:2 * n]
        send_sems, recv_sems = refs[2 * n:]
        x, y, c, _ = _place()

        def copy(w, core):
            h = out[w].shape[0] // 2
            rows = out[w].at[pl.ds(core * h, h)]
            return pltpu.make_async_remote_copy(
                src_ref=rows, dst_ref=rows, send_sem=send_sems.at[w], recv_sem=recv_sems.at[w],
                device_id=(x, y, 1 - c), device_id_type=MESH)

        for w in range(n):
            copy(w, c).start()
        for w in range(n):
            copy(w, 1 - c).wait_recv()
        for w in range(n):
            copy(w, c).wait_send()

    return pl.pallas_call(
        body, name="join_halves", in_specs=[ANY] * n, out_specs=[ANY] * n,
        out_shape=[jax.ShapeDtypeStruct(a.shape, a.dtype) for a in arrays],
        input_output_aliases={w: w for w in range(n)},
        scratch_shapes=[pltpu.SemaphoreType.DMA((n,)), pltpu.SemaphoreType.DMA((n,))],
        compiler_params=pltpu.CompilerParams(has_side_effects=True),
    )(*arrays)


def allreduce_small(packed):
    r, c = packed.shape
    n_dev = 8

    def body(x_ref, all_ref, sum_ref, send_sems, recv_sems, local_sem):
        x, y, cc, chips = _place()
        me, sibling = (x, y, cc), (x, y, 1 - cc)

        def rows(px, py, pc):
            return all_ref.at[4 * px + 2 * py + pc]

        def copy(k, block, to, src=None):
            return pltpu.make_async_remote_copy(
                src_ref=rows(*block) if src is None else src, dst_ref=rows(*block), send_sem=send_sems.at[k],
                recv_sem=recv_sems.at[k], device_id=to, device_id_type=MESH)

        mine = pltpu.make_async_copy(x_ref, rows(*me), local_sem)
        mine.start()
        first = [copy(0, me, sibling, src=x_ref)]
        first += [copy(1 + j, me, (*chip, cc), src=x_ref) for j, chip in enumerate(chips)]
        for cp in first:
            cp.start()
        passed = [copy(4 + j, (*chip, cc), sibling) for j, chip in enumerate(chips)]
        for j, chip in enumerate(chips):
            copy(1 + j, (*chip, cc), me).wait_recv()
            passed[j].start()
        copy(0, sibling, me).wait_recv()
        for j, chip in enumerate(chips):
            copy(4 + j, (*chip, 1 - cc), me).wait_recv()
        for cp in first + passed:
            cp.wait_send()
        mine.wait()
        acc = all_ref[0]
        for k in range(1, n_dev):
            acc = acc + all_ref[k]
        sum_ref[...] = acc

    vm = pl.BlockSpec(memory_space=pltpu.VMEM)
    return pl.pallas_call(
        body, name="allreduce_small", in_specs=[vm], out_specs=[vm, vm],
        out_shape=[jax.ShapeDtypeStruct((n_dev, r, c), F32), jax.ShapeDtypeStruct((r, c), F32)],
        scratch_shapes=[pltpu.SemaphoreType.DMA((7,)), pltpu.SemaphoreType.DMA((7,)), pltpu.SemaphoreType.DMA],
        compiler_params=pltpu.CompilerParams(has_side_effects=True, vmem_limit_bytes=VMEM_LIMIT),
    )(packed)[1]


def local_step(x, target, vecs, w_s, bs_t, bg, wg_in, late, core=None, order=None):
    on_mesh = core is not None

    def add(names, grads, recv):
        return [add_halves("add_" + n, g, r, core, min(r.shape[1], 256)) for n, g, r in zip(names, grads, recv)]

    g_pre, ln_g, ln_b, g_post, g_fpre, g_fpost = vecs
    s = x.shape[0]
    if order is None:
        order = jnp.arange(N_CHIPS, dtype=jnp.int32)
    logc = _attn_tables(s)
    ka, kb = _alibi_tables(s)

    h = norm_pre(x, g_pre)
    wg_a, wg_b, wg_out, wg_ff1, wg_ff2 = late
    if on_mesh:
        cut = D // 4
        z, (wg_in,) = mm_in("mm_in_own", h, wg_in, order, 0, 1, None, [Span(wg_in, 0, D, (0, 1))])
        z, (wg_in,) = mm_in("mm_in_near", h, wg_in, order, 1, 2, z, [Span(wg_in, 0, D, (2,))])
        z, (wg_in, wg_out) = mm_in("mm_in_far", h, wg_in, order, 3, 1, z, [Span(wg_in, 0, D, ()), wg_out])
        ya, (wg_ff2,) = gating_fwd(z, ln_g, ln_b, w_s, bs_t, [Span(wg_ff2, 0, cut)])
        yb, lse, (wg_a, wg_b, wg_ff1, wg_ff2, bg) = attn_fwd(
            z, logc, ka, kb, [wg_a, wg_b, wg_ff1, Span(wg_ff2, cut, D), bg])
        bg = jnp.transpose(bg[:, :2, :], (1, 0, 2)).reshape(2, D)
    else:
        z, _ = mm_in("mm_in", h, wg_in, order, 0, N_CHIPS, None, [Span(wg_in, 0, D, ())])
        ya, _ = gating_fwd(z, ln_g, ln_b, w_s, bs_t, [])
        yb, lse, _ = attn_fwd(z, logc, ka, kb, [])
    w_a, w_b, w_out, w_ff2 = wg_a.reshape(D, D), wg_b.reshape(D, D), wg_out.reshape(D, D), wg_ff2.reshape(D_FF, D)
    merged, pa, pb = proj_merge(ya, yb, w_a, w_b, z, bg)
    o, x1, h2 = out_norm(merged, w_out, x, g_post, g_fpre)
    a, rl = mm_ff1(h2, wg_ff1)
    dy, df, d_gfpost, loss = ff2_loss(rl, w_ff2, x1, target, g_fpost)

    half_cols = pl.BlockSpec((D, D // 2), lambda i, j: (0, j))
    d_wff2 = mm_tn("dw_ff2", rl, df, D // 2, D, (D_FF, D), pl.BlockSpec((D // 2, D), lambda i, j: (i, 0)))
    da = ff2_bwd(df, w_ff2, a)
    d_wff1 = mm_tn("dw_ff1", h2, da, D, D // 2, (N_CHIPS, D, D),
                   pl.BlockSpec((None, D, D // 2), lambda i, j: (j // 2, 0, j % 2)))
    d_ff = [d_wff1, d_wff2.reshape(N_CHIPS, D, D)]
    dx1, do, d_gfpre, d_gpost, recv_ff = ff1_bwd_norms(da, wg_ff1, x1, o, dy, g_fpre, g_post, d_ff if on_mesh else [])
    d_wout = mm_tn("dw_out", merged, do, D, D // 2, (D, D), half_cols)
    dpa, dpb, dga, dgb, d_bg = out_bwd_gates(do, w_out, pa, pb, z, bg)
    d_wa = mm_tn("dw_a", ya, dpa, D, D // 2, (D, D), half_cols)
    d_wb = mm_tn("dw_b", yb, dpb, D, D // 2, (D, D), half_cols)
    dya = mm_nt("dy_a", dpa, w_a)
    dyb = mm_nt("dy_b", dpb, w_b)
    d_proj = [d_wa.reshape(N_CHIPS, D // N_CHIPS, D), d_wb.reshape(N_CHIPS, D // N_CHIPS, D),
              d_wout.reshape(N_CHIPS, D // N_CHIPS, D)]
    du, dv, d_ws, d_bs, d_lng, d_lnb, recv_proj = gating_bwd(z, dya, ln_g, ln_b, w_s, bs_t, d_proj if on_mesh else [])
    early = d_proj + d_ff
    parts_early = add(BIG[1:], early, list(recv_proj) + list(recv_ff)) if on_mesh else []
    dq, dk, dvb, got_early = attn_bwd(z, yb, dyb, lse, logc, ka, kb, parts_early)
    dz = jnp.concatenate([du, dv, dq, dk, dvb, dga, dgb], axis=1)
    if on_mesh:
        for_sibling, _ = dw_in_half("dw_in_sibling", h, dz, 1 - core, [])
        mine, from_sibling = dw_in_half("dw_in_mine", h, dz, core, [for_sibling])
        d_win = None
        parts_late = [add_halves("add_w_in", mine, from_sibling[0], jnp.zeros((1,), jnp.int32), 256)]
    else:
        half = IN_SHARD // 2
        d_win = mm_tn("dw_in", h, dz, D, half, (N_CHIPS, D, IN_SHARD),
                      pl.BlockSpec((None, D, half), lambda i, j: (j // 2, 0, j % 2)))
        parts_late = []
    dx, d_gpre, got_late = in_bwd_norm(dz, wg_in, x, dx1, g_pre, parts_late)

    small = dict(norm_mix_pre=d_gpre, b_gate=d_bg, ln_v_g=d_lng, ln_v_b=d_lnb, w_s=d_ws, b_s=d_bs[:, 0, :],
                 norm_mix_post=d_gpost, norm_ffn_pre=d_gfpre, norm_ffn_post=d_gfpost)
    return loss[0, 0], dx, [d_win] + early, small, parts_late + parts_early, list(got_late) + list(got_early)


BIG = ("w_in", "w_a_proj", "w_b_proj", "w_out", "w_ff1", "w_ff2")
SMALL = ("norm_mix_pre", "ln_v_g", "ln_v_b", "b_s", "norm_mix_post", "norm_ffn_pre", "norm_ffn_post", "w_s", "b_gate")
ORDER = ("norm_mix_pre", "w_in", "b_gate", "ln_v_g", "ln_v_b", "w_s", "b_s", "w_a_proj", "w_b_proj", "w_out",
         "norm_mix_post", "norm_ffn_pre", "w_ff1", "w_ff2", "norm_ffn_post")
VEC_ROWS = D // 128
WS_ROW = 7 * VEC_ROWS
BG_ROW = WS_ROW + GROUPS * CHUNK
LOSS_ROW = BG_ROW + 2 * VEC_ROWS
PACK_ROWS = LOSS_ROW + 8


def pack_small(small, loss):
    vectors = [small[n] for n in SMALL[:7]]

    def body(*refs):
        out = refs[-1]
        ws_ref, bg_ref, loss_ref = refs[7:10]
        for i, n in enumerate(SMALL[:7]):
            if n == "b_s":
                out[i * VEC_ROWS:(i + 1) * VEC_ROWS, :] = refs[i][...]
            else:
                for j in range(VEC_ROWS):
                    out[i * VEC_ROWS + j:i * VEC_ROWS + j + 1, :] = refs[i][:, j * 128:(j + 1) * 128]
        for g in range(GROUPS):
            out[WS_ROW + g * CHUNK:WS_ROW + (g + 1) * CHUNK, :] = ws_ref[g]
        for r in range(2):
            for j in range(VEC_ROWS):
                row = BG_ROW + r * VEC_ROWS + j
                out[row:row + 1, :] = bg_ref[r:r + 1, j * 128:(j + 1) * 128]
        lane = lax.broadcasted_iota(jnp.int32, (8, 128), 1)
        sub = lax.broadcasted_iota(jnp.int32, (8, 128), 0)
        out[LOSS_ROW:LOSS_ROW + 8, :] = jnp.where((lane == 0) & (sub == 0), loss_ref[...], 0.0)

    return pl.pallas_call(
        body, name="pack_small", out_shape=jax.ShapeDtypeStruct((PACK_ROWS, 128), F32),
        compiler_params=_params(),
    )(*vectors, small["w_s"], small["b_gate"], loss)


def adamw_small(summed, chip, w, m, v):
    shapes = {n: (1, D) for n in SMALL}
    shapes.update(b_s=(GROUPS, CHUNK), w_s=(GROUPS * CHUNK, CHUNK), b_gate=(2, D // N_CHIPS))
    flat = lambda t: [t[n].reshape(shapes[n]) for n in SMALL]
    per = D // N_CHIPS // 128

    def body(chip_ref, sum_ref, *refs):
        params, outs = refs[:27], refs[27:]
        sub = lax.broadcasted_iota(jnp.int32, (VEC_ROWS, 128), 0)

        def gate_row(r):
            rows = sum_ref[BG_ROW + r * VEC_ROWS:BG_ROW + (r + 1) * VEC_ROWS, :]
            return jnp.concatenate([jnp.sum(jnp.where(sub == per * chip_ref[0] + j, rows, 0.0), axis=0, keepdims=True)
                                    for j in range(per)], axis=1)

        for i, n in enumerate(SMALL):
            if n == "b_s":
                g = sum_ref[i * VEC_ROWS:(i + 1) * VEC_ROWS, :]
            elif n == "w_s":
                g = sum_ref[WS_ROW:BG_ROW, :]
            elif n == "b_gate":
                g = jnp.concatenate([gate_row(0), gate_row(1)], axis=0)
            else:
                g = jnp.concatenate([sum_ref[i * VEC_ROWS + j:i * VEC_ROWS + j + 1, :] for j in range(VEC_ROWS)],
                                    axis=1)
            delta, nm, nv = _adamw_math(params[i][...], g, params[9 + i][...], params[18 + i][...])
            outs[4 * i][...], outs[4 * i + 1][...], outs[4 * i + 2][...], outs[4 * i + 3][...] = g, delta, nm, nv

    vm = pl.BlockSpec(memory_space=pltpu.VMEM)
    res = pl.pallas_call(
        body, name="adamw_small",
        in_specs=[pl.BlockSpec(memory_space=pltpu.SMEM)] + [vm] * 28, out_specs=[vm] * 36,
        out_shape=[jax.ShapeDtypeStruct(shapes[n], F32) for n in SMALL for _ in range(4)],
        compiler_params=_params(),
    )(chip, summed, *flat(w), *flat(m), *flat(v))
    return {n: tuple(r.reshape(w[n].shape) for r in res[4 * i:4 * i + 4]) for i, n in enumerate(SMALL)}


def kernel(x, norm_mix_pre, w_in, b_gate, ln_v_g, ln_v_b, w_s, b_s, w_a_proj, w_b_proj, w_out, norm_mix_post, norm_ffn_pre, w_ff1, w_ff2, norm_ffn_post, loss_target, m_norm_mix_pre, m_w_in, m_b_gate, m_ln_v_g, m_ln_v_b, m_w_s, m_b_s, m_w_a_proj, m_w_b_proj, m_w_out, m_norm_mix_post, m_norm_ffn_pre, m_w_ff1, m_w_ff2, m_norm_ffn_post, v_norm_mix_pre, v_w_in, v_b_gate, v_ln_v_g, v_ln_v_b, v_w_s, v_b_s, v_w_a_proj, v_w_b_proj, v_w_out, v_norm_mix_post, v_norm_ffn_pre, v_w_ff1, v_w_ff2, v_norm_ffn_post):
    w = dict(norm_mix_pre=norm_mix_pre, w_in=w_in, b_gate=b_gate, ln_v_g=ln_v_g, ln_v_b=ln_v_b, w_s=w_s, b_s=b_s,
             w_a_proj=w_a_proj, w_b_proj=w_b_proj, w_out=w_out, norm_mix_post=norm_mix_post,
             norm_ffn_pre=norm_ffn_pre, w_ff1=w_ff1, w_ff2=w_ff2, norm_ffn_post=norm_ffn_post)
    m = dict(norm_mix_pre=m_norm_mix_pre, w_in=m_w_in, b_gate=m_b_gate, ln_v_g=m_ln_v_g, ln_v_b=m_ln_v_b, w_s=m_w_s,
             b_s=m_b_s, w_a_proj=m_w_a_proj, w_b_proj=m_w_b_proj, w_out=m_w_out, norm_mix_post=m_norm_mix_post,
             norm_ffn_pre=m_norm_ffn_pre, w_ff1=m_w_ff1, w_ff2=m_w_ff2, norm_ffn_post=m_norm_ffn_post)
    v = dict(norm_mix_pre=v_norm_mix_pre, w_in=v_w_in, b_gate=v_b_gate, ln_v_g=v_ln_v_g, ln_v_b=v_ln_v_b, w_s=v_w_s,
             b_s=v_b_s, w_a_proj=v_w_a_proj, w_b_proj=v_w_b_proj, w_out=v_w_out, norm_mix_post=v_norm_mix_post,
             norm_ffn_pre=v_norm_ffn_pre, w_ff1=v_w_ff1, w_ff2=v_w_ff2, norm_ffn_post=v_norm_ffn_post)
    chip = 2 * lax.axis_index("x") + lax.axis_index("y")
    core = lax.axis_index("c")

    where = jnp.stack([chip, core]).astype(jnp.int32)
    placed = [place_shard("place_" + n, w[n][0], where, BF16, min(w[n].shape[1], 256)) for n in BIG]
    placed.append(place_shard("place_b_gate", jnp.pad(b_gate[0], ((0, 14), (0, 0))), where, F32, 16))
    vecs = (norm_mix_pre, ln_v_g, ln_v_b, norm_mix_post, norm_ffn_pre, norm_ffn_post)
    loss, dx, _, small, parts, got = local_step(
        x[0], loss_target[0], vecs, w_s[0], b_s[0].T, placed[6], placed[0], placed[1:6],
        core=jnp.reshape(core, (1,)).astype(jnp.int32),
        order=jnp.stack([chip, chip ^ 2, chip ^ 1, chip ^ 3]).astype(jnp.int32))

    halves = [sum_chips("sum_" + n, p, r, where, min(p.shape[1], 256)) for n, p, r in zip(BIG, parts, got)]
    grads = dict(zip(BIG, join_halves(halves)))

    summed = allreduce_small(pack_small(small, loss.reshape(1, 1)))
    loss = summed[LOSS_ROW, 0]

    new = adamw_small(summed, jnp.reshape(chip, (1,)).astype(jnp.int32), w, m, v)
    for n in BIG:
        shape = w[n].shape
        res = adamw("adamw_" + n, w[n][0], grads[n], m[n][0], v[n][0], min(shape[1], 256))
        new[n] = tuple(r.reshape(shape) for r in res)

    outs = [loss, dx[None]]
    for i in range(4):
        outs += [new[n][i] for n in ORDER]
    return tuple(outs)
```

```python
import functools
import math
import typing

import numpy as np
import jax
import jax.numpy as jnp
from jax import lax
from jax.experimental import pallas as pl
from jax.experimental.pallas import tpu as pltpu

F32 = jnp.float32
BF16 = jnp.bfloat16
MESH = pl.DeviceIdType.MESH

D = 1024
EPS = 1e-6
CHUNK = 128
GROUPS = 8
HEADS = 16
HEAD_DIM = 64
ATT_T = 256
ATT_GROUP = 4
ATT_BWD_GROUP = 2
N_CHIPS = 4
D_FF = 4 * D
IN_COLS = 7 * D
IN_SHARD = IN_COLS // N_CHIPS
MASKED = -1e30
VMEM_LIMIT = 56 * 2 ** 20

ADAM_LR, ADAM_B1, ADAM_B2, ADAM_EPS, ADAM_WD, ADAM_STEP = 0.001, 0.9, 0.999, 1e-08, 0.01, 10

NN = (((1,), (0,)), ((), ()))
NT = (((1,), (1,)), ((), ()))
TN = (((0,), (0,)), ((), ()))


def _dot(a, b, dims=NN):
    return lax.dot_general(a, b, dims, preferred_element_type=F32)


def _params(*sem, communicates=False):
    return pltpu.CompilerParams(dimension_semantics=sem or None, vmem_limit_bytes=VMEM_LIMIT,
                                has_side_effects=communicates)


def _rows(tr, c, col=0):
    return pl.BlockSpec((tr, c), lambda i: (i, col))


def _full(shape):
    n = len(shape)
    return pl.BlockSpec(shape, lambda *_: (0,) * n)


def _gelu(x):
    k = math.sqrt(2.0 / math.pi)
    return 0.5 * x * (1.0 + jnp.tanh(k * (x + 0.044715 * x * x * x)))


def _gelu_and_grad(x):
    k = math.sqrt(2.0 / math.pi)
    t = jnp.tanh(k * (x + 0.044715 * x * x * x))
    g = 0.5 * x * (1.0 + t)
    dg = 0.5 * (1.0 + t) + 0.5 * x * (1.0 - t * t) * (k * (1.0 + 3.0 * 0.044715 * x * x))
    return g, dg


def _sigmoid(x):
    return 1.0 / (1.0 + jnp.exp(-x))


def _rms(x):
    r = lax.rsqrt(jnp.mean(x * x, axis=-1, keepdims=True) + EPS)
    return x * r, r


def _rms_bwd(dn, xhat, r):
    return r * (dn - xhat * jnp.mean(dn * xhat, axis=-1, keepdims=True))


def norm_pre(x, g):
    s = x.shape[0]
    tr = 512

    def body(x_ref, g_ref, h_ref):
        xhat, _ = _rms(x_ref[...])
        h_ref[...] = (xhat * g_ref[...]).astype(BF16)

    return pl.pallas_call(
        body, name="norm_pre", grid=(s // tr,),
        in_specs=[_rows(tr, D), _full((1, D))], out_specs=_rows(tr, D),
        out_shape=jax.ShapeDtypeStruct((s, D), BF16), compiler_params=_params("parallel"),
    )(x, g)


def mm_in(name, h, wg, order, first, count, z, gathering, relay=False):
    s = h.shape[0]
    tm, tn = 1024, IN_SHARD // 2
    per = IN_SHARD // tn
    n = len(gathering)
    nj, ni = count * per, s // tm
    has_z = z is not None
    arrays = _arrays(gathering)
    at = [k for k, a in enumerate(arrays) if a is wg][0]

    def body(order_ref, *refs):
        a_ref = refs[0]
        o_ref = refs[1 + has_z + n]
        held = refs[2 + has_z + n:2 + has_z + 2 * n]
        tile, tile_sem = refs[2 + has_z + 2 * n:4 + has_z + 2 * n]
        j, i = pl.program_id(0), pl.program_id(1)
        if relay:
            send, pass_on, finish = _relay_phases(held[at], *refs[4 + has_z + 2 * n:])
        else:
            send, pass_on, finish = _gather_phases(held, *refs[4 + has_z + 2 * n:], _spans(gathering))

        def fetch(t):
            chip = order_ref[first + t // per]
            return pltpu.make_async_copy(held[at].at[chip, :, pl.ds((t % per) * tn, tn)], tile.at[t % 2],
                                         tile_sem.at[t % 2])

        @pl.when(i == 0)
        def _():
            @pl.when(j == 0)
            def _():
                send()
                fetch(0).start()

            fetch(j).wait()

            @pl.when(j + 1 < nj)
            def _():
                fetch(j + 1).start()

        pl.when((j == nj - 1) & (i == 0))(pass_on)
        rows = pl.ds(pl.multiple_of(i * tm, tm), tm)
        o_ref[...] = _dot(a_ref[rows, :], tile[j % 2]).astype(BF16)
        pl.when((j == nj - 1) & (i == ni - 1))(finish)

    out = pl.pallas_call(
        body, name=name,
        grid_spec=pltpu.PrefetchScalarGridSpec(
            num_scalar_prefetch=1, grid=(nj, ni),
            in_specs=[pl.BlockSpec((s, D), lambda j, i, o: (0, 0))] + [ANY] * (has_z + n),
            out_specs=[pl.BlockSpec((tm, tn), lambda j, i, o: (i, o[first + j // per] * per + j % per))] + [ANY] * n,
            scratch_shapes=[pltpu.VMEM((2, D, tn), BF16), pltpu.SemaphoreType.DMA((2,))]
            + (_relay_sems() if relay else _gather_sems(n))),
        out_shape=[jax.ShapeDtypeStruct((s, IN_COLS), BF16)] + [jax.ShapeDtypeStruct(a.shape, a.dtype) for a in arrays],
        input_output_aliases={**({2: 0} if has_z else {}), **{2 + has_z + w: 1 + w for w in range(n)}},
        compiler_params=_params("arbitrary", "arbitrary", communicates=True),
    )(order, h, *([z] if has_z else []), *arrays)
    return out[0], out[1:]


def _tril_ws(ws_ref, g):
    r = lax.broadcasted_iota(jnp.int32, (CHUNK, CHUNK), 0)
    c = lax.broadcasted_iota(jnp.int32, (CHUNK, CHUNK), 1)
    return jnp.where(c <= r, ws_ref[g], 0.0).astype(BF16)


def _layer_norm(v):
    mu = jnp.mean(v, axis=-1, keepdims=True)
    d = v - mu
    rstd = lax.rsqrt(jnp.mean(d * d, axis=-1, keepdims=True) + EPS)
    return d * rstd, rstd


def gating_fwd(z, ln_g, ln_b, w_s, bs_t, gathering):
    s = z.shape[0]
    n = len(gathering)
    steps = s // CHUNK

    def body(*refs):
        u_ref, v_ref, lg_ref, lb_ref, ws_ref, bst_ref = refs[:6]
        ya_ref = refs[6 + n]
        ci = pl.program_id(0)
        if n:
            send, pass_on, finish = _gather_phases(refs[7 + n:7 + 2 * n], *refs[7 + 2 * n:], _spans(gathering))
            pl.when(ci == 0)(send)
            pl.when(ci == steps * 3 // 4)(pass_on)
        ug = _gelu(u_ref[...].astype(F32))
        vhat, _ = _layer_norm(_gelu(v_ref[...].astype(F32)))
        vn = (vhat * lg_ref[...] + lb_ref[...]).astype(BF16)
        for g in range(GROUPS):
            cols = slice(g * CHUNK, (g + 1) * CHUNK)
            mixed = _dot(_tril_ws(ws_ref, g), vn[:, cols]) + bst_ref[:, g:g + 1]
            ya_ref[:, cols] = (ug[:, cols] * mixed).astype(BF16)
        if n:
            pl.when(ci == steps - 1)(finish)

    out = pl.pallas_call(
        body, name="gating_fwd", grid=(steps,),
        in_specs=[_rows(CHUNK, D, 0), _rows(CHUNK, D, 1), _full((1, D)), _full((1, D)),
                  _full((GROUPS, CHUNK, CHUNK)), _full((CHUNK, GROUPS))] + [ANY] * n,
        out_specs=[_rows(CHUNK, D)] + [ANY] * n,
        out_shape=[jax.ShapeDtypeStruct((s, D), BF16)]
        + [jax.ShapeDtypeStruct(a.shape, a.dtype) for a in _arrays(gathering)],
        input_output_aliases={6 + w: 1 + w for w in range(n)},
        scratch_shapes=_gather_sems(n) if n else [],
        compiler_params=_params("arbitrary", communicates=bool(n)),
    )(z, z, ln_g, ln_b, w_s, bs_t, *_arrays(gathering))
    return out[0], out[1:]


def _attn_tables(s):
    nd = s // ATT_T
    r = np.arange(ATT_T)[None, :, None]
    c = np.arange(ATT_T)[None, None, :]
    delta = np.arange(nd)[:, None, None] * ATT_T + r - c
    count = np.zeros(delta.shape, np.int64)
    for window, dilation in ((128, 1), (512, 4), (2048, 16)):
        count += (delta >= 0) & (delta % dilation == 0) & (delta <= window)
    logc = np.where(count > 0, np.log(np.maximum(count, 1)), MASKED)
    return jnp.asarray(logc, F32)


AUG = 3


def _split3_np(x):
    terms, rest = [], np.asarray(x, np.float64)
    for _ in range(AUG):
        term = np.asarray(rest.astype(jnp.bfloat16), np.float64)
        terms.append(term)
        rest = rest - term
    return terms


def _split3(x):
    terms, rest = [], x
    for _ in range(AUG):
        term = rest.astype(BF16).astype(F32)
        terms.append(term)
        rest = rest - term
    return terms


def _alibi_tables(s):
    nb = s // ATT_T
    slopes = np.exp2(-8.0 * np.arange(1, HEADS + 1, dtype=np.float64) / HEADS)
    ka = np.zeros((HEADS // 2, 2, ATT_T, 128), np.float32)
    kb = np.zeros((HEADS // 2, 2, nb, 128), np.float32)
    for p in range(HEADS // 2):
        for e in range(2):
            base = HEAD_DIM * (1 - e)
            for a, term in enumerate(_split3_np(slopes[2 * p + e] * np.arange(ATT_T))):
                ka[p, e, :, base + a] = term
            for a, term in enumerate(_split3_np(slopes[2 * p + e] * ATT_T * np.arange(nb))):
                kb[p, e, :, base + AUG + a] = term
            ka[p, e, :, base + 2 * AUG:base + 3 * AUG] = 1.0
    return jnp.asarray(ka), jnp.asarray(kb)


def _head_masks():
    lane = lax.broadcasted_iota(jnp.int32, (1, 128), 1)
    first = lane < HEAD_DIM

    def ones(e, n):
        base = HEAD_DIM * (1 - e)
        return ((lane >= base) & (lane < base + n)).astype(F32)

    return first, lane, ones


def _place3(lane, at, terms, other):
    for a, term in enumerate(terms):
        other = jnp.where(lane == at + a, term, other)
    return other


def attn_fwd(z, logc, ka, kb, gathering):
    s = z.shape[0]
    nq = s // ATT_T
    t = ATT_T
    n = len(gathering)
    grp = ATT_GROUP
    ngrp = HEADS // 2 // grp
    wide = 128 * grp
    qcol, kcol, vcol = 2 * D // wide, 3 * D // wide, 4 * D // wide

    def body(*refs):
        q_ref, k_ref, v_ref, lc_ref, ka_ref, kb_ref = refs[:6]
        y_ref, lse_ref = refs[6 + n:8 + n]
        q_s, k_s, v_s, m_s, l_s, acc_s = refs[8 + 2 * n:14 + 2 * n]
        gi, qi = pl.program_id(0), pl.program_id(1)
        first, lane, ones = _head_masks()
        if n:
            send, pass_on, finish = _gather_phases(refs[8 + n:8 + 2 * n], *refs[14 + 2 * n:], _spans(gathering))
            pl.when((gi == 0) & (qi == 0))(send)
            pl.when((gi == ngrp - 1) & (qi == nq * 3 // 4))(pass_on)

        @pl.when(qi == 0)
        def _():
            sel = jnp.broadcast_to(first.astype(F32), (t, 128))
            for pr in range(grp):
                cols = slice(pr * 128, (pr + 1) * 128)
                for jb in range(nq):
                    kj = k_ref[jb * t:(jb + 1) * t, cols].astype(F32)
                    vj = v_ref[jb * t:(jb + 1) * t, cols].astype(F32)
                    k_s[pr, 0, jb] = jnp.where(first, kj, ka_ref[pr, 0] + kb_ref[pr, 0, jb:jb + 1, :]).astype(BF16)
                    k_s[pr, 1, jb] = jnp.where(first, ka_ref[pr, 1] + kb_ref[pr, 1, jb:jb + 1, :], kj).astype(BF16)
                    v_s[pr, jb, 0:t, 0:128] = jnp.where(first, vj, 0.0).astype(BF16)
                    v_s[pr, jb, t:2 * t, 0:128] = jnp.where(first, 0.0, vj).astype(BF16)
                    v_s[pr, jb, 0:t, 128:256] = sel.astype(BF16)
                    v_s[pr, jb, t:2 * t, 128:256] = (1.0 - sel).astype(BF16)

        for pr in range(grp):
            q = q_ref[:, pr * 128:(pr + 1) * 128].astype(F32) * (1.0 / math.sqrt(HEAD_DIM))
            q_s[pr, 0] = jnp.where(first, q, ones(0, 2 * AUG)).astype(BF16)
            q_s[pr, 1] = jnp.where(first, ones(1, 2 * AUG), q).astype(BF16)
        m_s[...] = jnp.full_like(m_s, MASKED)
        l_s[...] = jnp.zeros_like(l_s)
        acc_s[...] = jnp.zeros_like(acc_s)

        def scores(j):
            return tuple(_dot(q_s[pr, e], k_s[pr, e, j], NT) for pr in range(grp) for e in range(2))

        def step(j, carry):
            softmax_block(j, scores(j))
            return carry

        def softmax_block(j, u):
            lc = lc_ref[qi - j]
            for pr in range(grp):
                u0 = u[2 * pr] + lc
                u1 = u[2 * pr + 1] + lc
                m0, m1 = m_s[pr, 0], m_s[pr, 1]
                n0 = jnp.maximum(m0, jnp.max(u0, axis=-1, keepdims=True))
                n1 = jnp.maximum(m1, jnp.max(u1, axis=-1, keepdims=True))
                m_s[pr, 0], m_s[pr, 1] = n0, n1
                p = jnp.concatenate([jnp.exp(u0 - jnp.concatenate([n0, n0], axis=1)).astype(BF16),
                                     jnp.exp(u1 - jnp.concatenate([n1, n1], axis=1)).astype(BF16)], axis=1)
                pv = _dot(p, v_s[pr, j])
                alpha = jnp.where(first, jnp.exp(m0 - n0), jnp.exp(m1 - n1))
                acc_s[pr] = acc_s[pr] * alpha + pv[:, 0:128]
                l_s[pr] = l_s[pr] * alpha + pv[:, 128:256]

        lax.fori_loop(0, qi + 1, step, 0)
        for pr in range(grp):
            cols = slice(pr * 128, (pr + 1) * 128)
            y_ref[:, cols] = (acc_s[pr] / l_s[pr]).astype(BF16)
            lse_ref[:, cols] = jnp.where(first, m_s[pr, 0], m_s[pr, 1]) + jnp.log(l_s[pr])
        if n:
            pl.when((gi == ngrp - 1) & (qi == nq - 1))(finish)

    out = pl.pallas_call(
        body, name="attn_fwd", grid=(ngrp, nq),
        in_specs=[pl.BlockSpec((t, wide), lambda g, i: (i, qcol + g)),
                  pl.BlockSpec((s, wide), lambda g, i: (0, kcol + g)),
                  pl.BlockSpec((s, wide), lambda g, i: (0, vcol + g)),
                  _full((nq, t, t)),
                  pl.BlockSpec((grp, 2, t, 128), lambda g, i: (g, 0, 0, 0)),
                  pl.BlockSpec((grp, 2, nq, 128), lambda g, i: (g, 0, 0, 0))] + [ANY] * n,
        out_specs=[pl.BlockSpec((t, wide), lambda g, i: (i, g)), pl.BlockSpec((t, wide), lambda g, i: (i, g))]
        + [ANY] * n,
        out_shape=[jax.ShapeDtypeStruct((s, D), BF16), jax.ShapeDtypeStruct((s, D), F32)]
        + [jax.ShapeDtypeStruct(a.shape, a.dtype) for a in _arrays(gathering)],
        input_output_aliases={6 + w: 2 + w for w in range(n)},
        scratch_shapes=[pltpu.VMEM((grp, 2, t, 128), BF16), pltpu.VMEM((grp, 2, nq, t, 128), BF16),
                        pltpu.VMEM((grp, nq, 2 * t, 256), BF16), pltpu.VMEM((grp, 2, t, 128), F32),
                        pltpu.VMEM((grp, t, 128), F32), pltpu.VMEM((grp, t, 128), F32)]
        + (_gather_sems(n) if n else []),
        compiler_params=_params("arbitrary", "arbitrary", communicates=bool(n)),
    )(z, z, z, logc, ka, kb, *_arrays(gathering))
    return out[0], out[1], out[2:]


def proj_merge(ya, yb, wa, wb, z, bg):
    s = ya.shape[0]
    tm = 512

    def body(ya_ref, yb_ref, wa_ref, wb_ref, ga_ref, gb_ref, bg_ref, mg_ref, pa_ref, pb_ref):
        pa = _dot(ya_ref[...], wa_ref[...])
        pb = _dot(yb_ref[...], wb_ref[...])
        sa = _sigmoid(ga_ref[...] + bg_ref[0:1, :])
        sb = _sigmoid(gb_ref[...] + bg_ref[1:2, :])
        mg_ref[...] = (sa * pa + sb * pb).astype(BF16)
        pa_ref[...] = pa.astype(BF16)
        pb_ref[...] = pb.astype(BF16)

    out = jax.ShapeDtypeStruct((s, D), BF16)
    return pl.pallas_call(
        body, name="proj_merge", grid=(s // tm,),
        in_specs=[_rows(tm, D), _rows(tm, D), _full((D, D)), _full((D, D)),
                  _rows(tm, D, 5), _rows(tm, D, 6), _full((2, D))],
        out_specs=[_rows(tm, D)] * 3, out_shape=[out] * 3, compiler_params=_params("parallel"),
    )(ya, yb, wa, wb, z, z, bg)


def out_norm(merged, w_out, x, g_post, g_fpre):
    s = x.shape[0]
    tm = 512

    def body(mg_ref, w_ref, x_ref, gp_ref, gf_ref, o_ref, x1_ref, h2_ref):
        o = _dot(mg_ref[...], w_ref[...])
        ohat, _ = _rms(o)
        x1 = x_ref[...] + ohat * gp_ref[...]
        x1hat, _ = _rms(x1)
        o_ref[...] = o
        x1_ref[...] = x1
        h2_ref[...] = (x1hat * gf_ref[...]).astype(BF16)

    return pl.pallas_call(
        body, name="out_norm", grid=(s // tm,),
        in_specs=[_rows(tm, D), _full((D, D)), _rows(tm, D), _full((1, D)), _full((1, D))],
        out_specs=[_rows(tm, D)] * 3,
        out_shape=[jax.ShapeDtypeStruct((s, D), F32), jax.ShapeDtypeStruct((s, D), F32),
                   jax.ShapeDtypeStruct((s, D), BF16)],
        compiler_params=_params("parallel"),
    )(merged, w_out, x, g_post, g_fpre)


def mm_ff1(h2, wg):
    s = h2.shape[0]
    tm = 1024

    def body(a_ref, b_ref, o_ref, r_ref):
        a = _dot(a_ref[...], b_ref[...])
        o_ref[...] = a.astype(BF16)
        r = jnp.maximum(a, 0.0)
        r_ref[...] = (r * r).astype(BF16)

    return pl.pallas_call(
        body, name="mm_ff1", grid=(s // tm, N_CHIPS),
        in_specs=[pl.BlockSpec((tm, D), lambda i, j: (i, 0)), pl.BlockSpec((None, D, D), lambda i, j: (j, 0, 0))],
        out_specs=[pl.BlockSpec((tm, D), lambda i, j: (i, j))] * 2,
        out_shape=[jax.ShapeDtypeStruct((s, D_FF), BF16), jax.ShapeDtypeStruct((s, D_FF), BF16)],
        compiler_params=_params("parallel", "parallel"),
    )(h2, wg)


def ff2_loss(rl, w_ff2, x1, target, g_fpost):
    s = x1.shape[0]
    tm = 256

    def body(rl_ref, w_ref, x1_ref, t_ref, g_ref, dy_ref, df_ref, dg_ref, loss_ref):
        @pl.when(pl.program_id(0) == 0)
        def _():
            dg_ref[...] = jnp.zeros_like(dg_ref)
            loss_ref[...] = jnp.zeros_like(loss_ref)

        f = _dot(rl_ref[...], w_ref[...])
        fhat, r = _rms(f)
        err = x1_ref[...] + fhat * g_ref[...] - t_ref[...]
        loss_ref[...] += 0.5 * jnp.sum(jnp.mean(err * err, axis=-1, keepdims=True), axis=0, keepdims=True)
        dy = err * (1.0 / D)
        dy_ref[...] = dy
        dg_ref[...] += jnp.sum(dy * fhat, axis=0, keepdims=True)
        df_ref[...] = _rms_bwd(dy * g_ref[...], fhat, r).astype(BF16)

    return pl.pallas_call(
        body, name="ff2_loss", grid=(s // tm,),
        in_specs=[_rows(tm, D_FF), _full((D_FF, D)), _rows(tm, D), _rows(tm, D), _full((1, D))],
        out_specs=[_rows(tm, D), _rows(tm, D), _full((1, D)), _full((1, 1))],
        out_shape=[jax.ShapeDtypeStruct((s, D), F32), jax.ShapeDtypeStruct((s, D), BF16),
                   jax.ShapeDtypeStruct((1, D), F32), jax.ShapeDtypeStruct((1, 1), F32)],
        compiler_params=_params("arbitrary"),
    )(rl, w_ff2, x1, target, g_fpost)


def mm_tn(name, a, b, ta, tb, out_shape, out_spec):
    s = a.shape[0]

    def body(a_ref, b_ref, o_ref):
        o_ref[...] = _dot(a_ref[...], b_ref[...], TN)

    return pl.pallas_call(
        body, name=name, grid=(a.shape[1] // ta, b.shape[1] // tb),
        in_specs=[pl.BlockSpec((s, ta), lambda i, j: (0, i)), pl.BlockSpec((s, tb), lambda i, j: (0, j))],
        out_specs=out_spec, out_shape=jax.ShapeDtypeStruct(out_shape, F32),
        compiler_params=_params("parallel", "parallel"),
    )(a, b)


def mm_nt(name, a, w):
    s = a.shape[0]
    tm = 512

    def body(a_ref, w_ref, o_ref):
        o_ref[...] = _dot(a_ref[...], w_ref[...], NT).astype(BF16)

    return pl.pallas_call(
        body, name=name, grid=(s // tm,), in_specs=[_rows(tm, D), _full((D, D))], out_specs=_rows(tm, D),
        out_shape=jax.ShapeDtypeStruct((s, D), BF16), compiler_params=_params("parallel"),
    )(a, w)


def ff2_bwd(df, w_ff2, a):
    s = df.shape[0]
    tm = 1024

    def body(df_ref, w_ref, a_ref, da_ref):
        drl = _dot(df_ref[...], w_ref[...], NT)
        da_ref[...] = (drl * (2.0 * jnp.maximum(a_ref[...].astype(F32), 0.0))).astype(BF16)

    return pl.pallas_call(
        body, name="ff2_bwd", grid=(s // tm, D_FF // D),
        in_specs=[pl.BlockSpec((tm, D), lambda i, j: (i, 0)), pl.BlockSpec((D, D), lambda i, j: (j, 0)),
                  pl.BlockSpec((tm, D), lambda i, j: (i, j))],
        out_specs=pl.BlockSpec((tm, D), lambda i, j: (i, j)),
        out_shape=jax.ShapeDtypeStruct((s, D_FF), BF16), compiler_params=_params("parallel", "parallel"),
    )(df, w_ff2, a)


def ff1_bwd_norms(da, wg, x1, o, dy, g_fpre, g_post, swapping):
    s = x1.shape[0]
    tm = 512
    n = len(swapping)

    def body(*refs):
        da_ref, w_ref, x1_ref, o_ref, dy_ref, gf_ref, gp_ref = refs[:7]
        dx1_ref, do_ref, dgf_ref, dgp_ref = refs[7 + n:11 + n]
        acc_ref = refs[11 + 2 * n]
        i, k = pl.program_id(0), pl.program_id(1)
        if n:
            send, finish = _swap_phases(refs[7:7 + n], refs[11 + n:11 + 2 * n], *refs[12 + 2 * n:])
            pl.when((i == 0) & (k == 0))(send)

        @pl.when((i == 0) & (k == 0))
        def _():
            dgf_ref[...] = jnp.zeros_like(dgf_ref)
            dgp_ref[...] = jnp.zeros_like(dgp_ref)

        part = _dot(da_ref[...], w_ref[...], NT)

        @pl.when(k == 0)
        def _():
            acc_ref[...] = part

        @pl.when(k > 0)
        def _():
            acc_ref[...] += part

        @pl.when(k == N_CHIPS - 1)
        def _():
            dh2 = acc_ref[...]
            x1hat, r2 = _rms(x1_ref[...])
            dgf_ref[...] += jnp.sum(dh2 * x1hat, axis=0, keepdims=True)
            dx1 = dy_ref[...] + _rms_bwd(dh2 * gf_ref[...], x1hat, r2)
            ohat, r1 = _rms(o_ref[...])
            dgp_ref[...] += jnp.sum(dx1 * ohat, axis=0, keepdims=True)
            dx1_ref[...] = dx1
            do_ref[...] = _rms_bwd(dx1 * gp_ref[...], ohat, r1).astype(BF16)

        if n:
            pl.when((i == s // tm - 1) & (k == N_CHIPS - 1))(finish)

    row = pl.BlockSpec((tm, D), lambda i, k: (i, 0))
    vec = pl.BlockSpec((1, D), lambda i, k: (0, 0))
    res = pl.pallas_call(
        body, name="ff1_bwd_norms", grid=(s // tm, N_CHIPS),
        in_specs=[pl.BlockSpec((tm, D), lambda i, k: (i, k)), pl.BlockSpec((None, D, D), lambda i, k: (k, 0, 0)),
                  row, row, row, vec, vec] + [ANY] * n,
        out_specs=[row, row, vec, vec] + [ANY] * n,
        out_shape=[jax.ShapeDtypeStruct((s, D), F32), jax.ShapeDtypeStruct((s, D), BF16),
                   jax.ShapeDtypeStruct((1, D), F32), jax.ShapeDtypeStruct((1, D), F32)] + _swap_shapes(swapping),
        scratch_shapes=[pltpu.VMEM((tm, D), F32)] + (_swap_sems(n) if n else []),
        compiler_params=_params("arbitrary", "arbitrary", communicates=bool(n)),
    )(da, wg, x1, o, dy, g_fpre, g_post, *swapping)
    return res[0], res[1], res[2], res[3], res[4:]


def out_bwd_gates(do, w_out, pa, pb, z, bg):
    s = do.shape[0]
    tm = 512

    def body(do_ref, w_ref, pa_ref, pb_ref, ga_ref, gb_ref, bg_ref, dpa_ref, dpb_ref, dga_ref, dgb_ref, dbg_ref):
        @pl.when(pl.program_id(0) == 0)
        def _():
            dbg_ref[...] = jnp.zeros_like(dbg_ref)

        dm = _dot(do_ref[...], w_ref[...], NT)
        sa = _sigmoid(ga_ref[...] + bg_ref[0:1, :])
        sb = _sigmoid(gb_ref[...] + bg_ref[1:2, :])
        dpa_ref[...] = (dm * sa).astype(BF16)
        dpb_ref[...] = (dm * sb).astype(BF16)
        dga = dm * pa_ref[...].astype(F32) * (sa * (1.0 - sa))
        dgb = dm * pb_ref[...].astype(F32) * (sb * (1.0 - sb))
        dga_ref[...] = dga.astype(BF16)
        dgb_ref[...] = dgb.astype(BF16)
        dbg_ref[0:1, :] += jnp.sum(dga, axis=0, keepdims=True)
        dbg_ref[1:2, :] += jnp.sum(dgb, axis=0, keepdims=True)

    out = jax.ShapeDtypeStruct((s, D), BF16)
    return pl.pallas_call(
        body, name="out_bwd_gates", grid=(s // tm,),
        in_specs=[_rows(tm, D), _full((D, D)), _rows(tm, D), _rows(tm, D), _rows(tm, D, 5), _rows(tm, D, 6),
                  _full((2, D))],
        out_specs=[_rows(tm, D)] * 4 + [_full((2, D))],
        out_shape=[out] * 4 + [jax.ShapeDtypeStruct((2, D), F32)], compiler_params=_params("arbitrary"),
    )(do, w_out, pa, pb, z, z, bg)


def gating_bwd(z, dya, ln_g, ln_b, w_s, bs_t, swapping):
    s = z.shape[0]
    ones = functools.partial(jnp.ones, (8, CHUNK), BF16)
    n = len(swapping)

    def body(*refs):
        u_ref, v_ref, dya_ref, lg_ref, lb_ref, ws_ref, bst_ref = refs[:7]
        du_ref, dv_ref, dws_ref, dbs_ref, dlg_ref, dlb_ref = refs[7 + n:13 + n]
        dvn_ref = refs[13 + 2 * n]
        ci = pl.program_id(0)
        if n:
            send, finish = _swap_phases(refs[7:7 + n], refs[13 + n:13 + 2 * n], *refs[14 + 2 * n:])
            pl.when(ci == 0)(send)

        @pl.when(ci == 0)
        def _():
            dws_ref[...] = jnp.zeros_like(dws_ref)
            dbs_ref[...] = jnp.zeros_like(dbs_ref)
            dlg_ref[...] = jnp.zeros_like(dlg_ref)
            dlb_ref[...] = jnp.zeros_like(dlb_ref)

        ug, dug_du = _gelu_and_grad(u_ref[...].astype(F32))
        vg, dvg_dv = _gelu_and_grad(v_ref[...].astype(F32))
        vhat, rstd = _layer_norm(vg)
        vn = (vhat * lg_ref[...] + lb_ref[...]).astype(BF16)
        dya = dya_ref[...].astype(F32)
        for g in range(GROUPS):
            cols = slice(g * CHUNK, (g + 1) * CHUNK)
            ws = _tril_ws(ws_ref, g)
            mixed = _dot(ws, vn[:, cols]) + bst_ref[:, g:g + 1]
            du_ref[:, cols] = (dya[:, cols] * mixed * dug_du[:, cols]).astype(BF16)
            dmix = (dya[:, cols] * ug[:, cols]).astype(BF16)
            dbs_ref[g] += _dot(ones(), dmix, NT)
            dws_ref[g] += _dot(dmix, vn[:, cols], NT)
            dvn_ref[:, cols] = _dot(ws, dmix, TN)
        dvn = dvn_ref[...]
        dlg_ref[...] += jnp.sum(dvn * vhat, axis=0, keepdims=True)
        dlb_ref[...] += jnp.sum(dvn, axis=0, keepdims=True)
        dvh = dvn * lg_ref[...]
        dvg = rstd * (dvh - jnp.mean(dvh, axis=-1, keepdims=True)
                      - vhat * jnp.mean(dvh * vhat, axis=-1, keepdims=True))
        dv_ref[...] = (dvg * dvg_dv).astype(BF16)

        @pl.when(ci == pl.num_programs(0) - 1)
        def _():
            r = lax.broadcasted_iota(jnp.int32, (CHUNK, CHUNK), 0)
            c = lax.broadcasted_iota(jnp.int32, (CHUNK, CHUNK), 1)
            for g in range(GROUPS):
                dws_ref[g] = jnp.where(c <= r, dws_ref[g], 0.0)

        if n:
            pl.when(ci == pl.num_programs(0) - 1)(finish)

    out = jax.ShapeDtypeStruct((s, D), BF16)
    res = pl.pallas_call(
        body, name="gating_bwd", grid=(s // CHUNK,),
        in_specs=[_rows(CHUNK, D, 0), _rows(CHUNK, D, 1), _rows(CHUNK, D), _full((1, D)), _full((1, D)),
                  _full((GROUPS, CHUNK, CHUNK)), _full((CHUNK, GROUPS))] + [ANY] * n,
        out_specs=[_rows(CHUNK, D), _rows(CHUNK, D), _full((GROUPS, CHUNK, CHUNK)), _full((GROUPS, 8, CHUNK)),
                   _full((1, D)), _full((1, D))] + [ANY] * n,
        out_shape=[out, out, jax.ShapeDtypeStruct((GROUPS, CHUNK, CHUNK), F32),
                   jax.ShapeDtypeStruct((GROUPS, 8, CHUNK), F32),
                   jax.ShapeDtypeStruct((1, D), F32), jax.ShapeDtypeStruct((1, D), F32)] + _swap_shapes(swapping),
        scratch_shapes=[pltpu.VMEM((CHUNK, D), F32)] + (_swap_sems(n) if n else []),
        compiler_params=_params("arbitrary", communicates=bool(n)),
    )(z, z, dya, ln_g, ln_b, w_s, bs_t, *swapping)
    return (*res[:6], res[6:])


def attn_bwd(z, yb, dyb, lse, logc, ka, kb, scattering):
    s = z.shape[0]
    nq = s // ATT_T
    t = ATT_T
    grp = ATT_BWD_GROUP
    ngrp = HEADS // 2 // grp
    wide = 128 * grp
    qcol, kcol, vcol = 2 * D // wide, 3 * D // wide, 4 * D // wide
    scale = 1.0 / math.sqrt(HEAD_DIM)
    n = len(scattering)

    def body(*refs):
        q_ref, k_ref, v_ref, y_ref, dy_ref, lse_ref, lc_ref, ka_ref, kb_ref = refs[:9]
        dq_ref, dk_ref, dv_ref = refs[9 + n:12 + n]
        qa_s, qt_s, da_s, dt_s, dq_s, dkt_s, dvt_s = refs[12 + 2 * n:19 + 2 * n]
        gi, j = pl.program_id(0), pl.program_id(1)
        first, lane, ones = _head_masks()
        if n:
            send, finish = _scatter_phases(refs[9:9 + n], refs[12 + n:12 + 2 * n], *refs[19 + 2 * n:])
            pl.when((gi == 0) & (j == 0))(send)

        @pl.when(j == 0)
        def _():
            dq_s[...] = jnp.zeros_like(dq_s)
            for pr in range(grp):
                cols = slice(pr * 128, (pr + 1) * 128)
                for ib in range(nq):
                    rows = slice(ib * t, (ib + 1) * t)
                    q = q_ref[rows, cols].astype(F32) * scale
                    lse = lse_ref[rows, cols]
                    qa_s[pr, 0, ib] = jnp.where(first, q, _place3(lane, HEAD_DIM + 2 * AUG, _split3(-lse[:, 0:1]),
                                                                  ones(0, 2 * AUG))).astype(BF16)
                    qa_s[pr, 1, ib] = jnp.where(
                        first, _place3(lane, 2 * AUG, _split3(-lse[:, HEAD_DIM:HEAD_DIM + 1]), ones(1, 2 * AUG)),
                        q).astype(BF16)
                    qt_s[pr, ib, :, 0:t] = jnp.where(first, q, 0.0).T.astype(BF16)
                    qt_s[pr, ib, :, t:2 * t] = jnp.where(first, 0.0, q).T.astype(BF16)
                    do = dy_ref[rows, cols].astype(F32)
                    prod = do * y_ref[rows, cols].astype(F32)
                    dd0 = jnp.sum(jnp.where(first, prod, 0.0), axis=-1, keepdims=True)
                    dd1 = jnp.sum(jnp.where(first, 0.0, prod), axis=-1, keepdims=True)
                    da_s[pr, 0, ib] = jnp.where(first, do, _place3(lane, HEAD_DIM, _split3(-dd0), 0.0)).astype(BF16)
                    da_s[pr, 1, ib] = jnp.where(first, _place3(lane, 0, _split3(-dd1), 0.0), do).astype(BF16)
                    dt_s[pr, ib, :, 0:t] = jnp.where(first, do, 0.0).T.astype(BF16)
                    dt_s[pr, ib, :, t:2 * t] = jnp.where(first, 0.0, do).T.astype(BF16)

        keys = []
        for pr in range(grp):
            kj = k_ref[:, pr * 128:(pr + 1) * 128].astype(F32)
            vj = v_ref[:, pr * 128:(pr + 1) * 128].astype(F32)
            keys.append((
                jnp.where(first, kj, ka_ref[pr, 0] + kb_ref[pr, 0, pl.ds(j, 1), :]).astype(BF16),
                jnp.where(first, ka_ref[pr, 1] + kb_ref[pr, 1, pl.ds(j, 1), :], kj).astype(BF16),
                jnp.concatenate([jnp.where(first, kj, 0.0), jnp.where(first, 0.0, kj)], axis=0).astype(BF16),
                jnp.where(first, vj, ones(0, AUG)).astype(BF16),
                jnp.where(first, ones(1, AUG), vj).astype(BF16)))
        dkt_s[...] = jnp.zeros_like(dkt_s)
        dvt_s[...] = jnp.zeros_like(dvt_s)

        def step(i, _):
            lc = lc_ref[i - j]
            rows = pl.ds(pl.multiple_of(i * t, t), t)
            for pr in range(grp):
                k0a, k1a, kst, v0a, v1a = keys[pr]
                p0 = jnp.exp(_dot(qa_s[pr, 0, i], k0a, NT) + lc)
                p1 = jnp.exp(_dot(qa_s[pr, 1, i], k1a, NT) + lc)
                e0 = (p0 * _dot(da_s[pr, 0, i], v0a, NT)).astype(BF16)
                e1 = (p1 * _dot(da_s[pr, 1, i], v1a, NT)).astype(BF16)
                dq_s[pr, rows, :] += _dot(jnp.concatenate([e0, e1], axis=1), kst)
                dvt_s[pr] += _dot(dt_s[pr, i], jnp.concatenate([p0.astype(BF16), p1.astype(BF16)], axis=0))
                dkt_s[pr] += _dot(qt_s[pr, i], jnp.concatenate([e0, e1], axis=0))
            return 0

        lax.fori_loop(j, nq, step, 0)
        for pr in range(grp):
            dk_ref[:, pr * 128:(pr + 1) * 128] = dkt_s[pr].T.astype(BF16)
            dv_ref[:, pr * 128:(pr + 1) * 128] = dvt_s[pr].T.astype(BF16)

        @pl.when(j == nq - 1)
        def _():
            for pr in range(grp):
                dq_ref[:, pr * 128:(pr + 1) * 128] = (dq_s[pr] * scale).astype(BF16)

        if n:
            pl.when((gi == ngrp - 1) & (j == nq - 1))(finish)

    colblock = lambda c: pl.BlockSpec((s, wide), lambda g, j: (0, c + g))
    blk = lambda c: pl.BlockSpec((t, wide), lambda g, j: (j, c + g))
    out = jax.ShapeDtypeStruct((s, D), BF16)
    res = pl.pallas_call(
        body, name="attn_bwd", grid=(ngrp, nq),
        in_specs=[colblock(qcol), blk(kcol), blk(vcol), colblock(0), colblock(0), colblock(0),
                  _full((nq, t, t)), pl.BlockSpec((grp, 2, t, 128), lambda g, j: (g, 0, 0, 0)),
                  pl.BlockSpec((grp, 2, nq, 128), lambda g, j: (g, 0, 0, 0))] + [ANY] * n,
        out_specs=[colblock(0), blk(0), blk(0)] + [ANY] * n, out_shape=[out] * 3 + _scatter_shapes(scattering),
        scratch_shapes=[pltpu.VMEM((grp, 2, nq, t, 128), BF16), pltpu.VMEM((grp, nq, 128, 2 * t), BF16),
                        pltpu.VMEM((grp, 2, nq, t, 128), BF16), pltpu.VMEM((grp, nq, 128, 2 * t), BF16),
                        pltpu.VMEM((grp, s, 128), F32), pltpu.VMEM((grp, 128, t), F32),
                        pltpu.VMEM((grp, 128, t), F32)]
        + (_scatter_sems(n) if n else []),
        compiler_params=_params("arbitrary", "arbitrary", communicates=bool(n)),
    )(z, z, z, yb, dyb, lse, logc, ka, kb, *scattering)
    return res[0], res[1], res[2], res[3:]


def in_bwd_norm(dz, wg, x, dx1, g_pre, scattering):
    s = x.shape[0]
    tm = 512
    n = len(scattering)

    def body(*refs):
        dz_ref, w_ref, x_ref, dx1_ref, g_ref = refs[:5]
        dx_ref, dg_ref = refs[5 + n:7 + n]
        acc_ref = refs[7 + 2 * n]
        i, k = pl.program_id(0), pl.program_id(1)
        if n:
            send, finish = _scatter_phases(refs[5:5 + n], refs[7 + n:7 + 2 * n], *refs[8 + 2 * n:])
            pl.when((i == 0) & (k == 0))(send)

        @pl.when((i == 0) & (k == 0))
        def _():
            dg_ref[...] = jnp.zeros_like(dg_ref)

        part = _dot(dz_ref[...], w_ref[...], NT)

        @pl.when(k == 0)
        def _():
            acc_ref[...] = part

        @pl.when(k > 0)
        def _():
            acc_ref[...] += part

        @pl.when(k == N_CHIPS - 1)
        def _():
            dh = acc_ref[...]
            xhat, r = _rms(x_ref[...])
            dg_ref[...] += jnp.sum(dh * xhat, axis=0, keepdims=True)
            dx_ref[...] = dx1_ref[...] + _rms_bwd(dh * g_ref[...], xhat, r)

        if n:
            pl.when((i == s // tm - 1) & (k == N_CHIPS - 1))(finish)

    row = pl.BlockSpec((tm, D), lambda i, k: (i, 0))
    vec = pl.BlockSpec((1, D), lambda i, k: (0, 0))
    res = pl.pallas_call(
        body, name="in_bwd_norm", grid=(s // tm, N_CHIPS),
        in_specs=[pl.BlockSpec((tm, IN_SHARD), lambda i, k: (i, k)),
                  pl.BlockSpec((None, D, IN_SHARD), lambda i, k: (k, 0, 0)), row, row, vec] + [ANY] * n,
        out_specs=[row, vec] + [ANY] * n,
        out_shape=[jax.ShapeDtypeStruct((s, D), F32), jax.ShapeDtypeStruct((1, D), F32)]
        + _scatter_shapes(scattering),
        scratch_shapes=[pltpu.VMEM((tm, D), F32)] + (_scatter_sems(n) if n else []),
        compiler_params=_params("arbitrary", "arbitrary", communicates=bool(n)),
    )(dz, wg, x, dx1, g_pre, *scattering)
    return res[0], res[1], res[2:]


def _adamw_math(w, g, m, v):
    m = ADAM_B1 * m + (1.0 - ADAM_B1) * g
    v = ADAM_B2 * v + (1.0 - ADAM_B2) * (g * g)
    m_hat = m / (1.0 - ADAM_B1 ** ADAM_STEP)
    v_hat = v / (1.0 - ADAM_B2 ** ADAM_STEP)
    delta = -ADAM_LR * (m_hat / (jnp.sqrt(v_hat) + ADAM_EPS) + ADAM_WD * w)
    return delta, m, v


def adamw(name, w, g, m, v, tr):
    r, c = w.shape

    def body(w_ref, g_ref, m_ref, v_ref, go_ref, d_ref, nm_ref, nv_ref):
        g = g_ref[...]
        go_ref[...] = g
        d_ref[...], nm_ref[...], nv_ref[...] = _adamw_math(w_ref[...], g, m_ref[...], v_ref[...])

    out = jax.ShapeDtypeStruct((r, c), F32)
    return pl.pallas_call(
        body, name=name, grid=(r // tr,), in_specs=[_rows(tr, c)] * 4, out_specs=[_rows(tr, c)] * 4,
        out_shape=[out] * 4, compiler_params=_params("parallel"),
    )(w, g, m, v)


def add_halves(name, g, recv, c_idx, tr):
    n, h, c = recv.shape

    def body(c_ref, g_ref, r_ref, o_ref):
        o_ref[...] = (g_ref[...] + r_ref[...]).astype(BF16)

    nb = h // tr
    return pl.pallas_call(
        body, name=name,
        grid_spec=pltpu.PrefetchScalarGridSpec(
            num_scalar_prefetch=1, grid=(n, nb),
            in_specs=[pl.BlockSpec((None, tr, c), lambda k, i, c_ref: (k, c_ref[0] * nb + i, 0)),
                      pl.BlockSpec((None, tr, c), lambda k, i, c_ref: (k, i, 0))],
            out_specs=pl.BlockSpec((None, tr, c), lambda k, i, c_ref: (k, i, 0))),
        out_shape=jax.ShapeDtypeStruct((n, h, c), BF16), compiler_params=_params("parallel", "parallel"),
    )(c_idx, g, recv)


def sum_chips(name, parts, recv, where, tr):
    n, h, c = recv.shape
    nb = h // tr

    def body(w_ref, p_ref, r_ref, o_ref):
        acc = p_ref[...].astype(F32)
        for k in range(n):
            acc = acc + r_ref[k].astype(F32)
        o_ref[...] = acc

    return pl.pallas_call(
        body, name=name,
        grid_spec=pltpu.PrefetchScalarGridSpec(
            num_scalar_prefetch=1, grid=(nb,),
            in_specs=[pl.BlockSpec((None, tr, c), lambda i, w_ref: (w_ref[0], i, 0)),
                      pl.BlockSpec((n, tr, c), lambda i, w_ref: (0, i, 0))],
            out_specs=pl.BlockSpec((tr, c), lambda i, w_ref: (w_ref[1] * nb + i, 0))),
        out_shape=jax.ShapeDtypeStruct((2 * h, c), F32), compiler_params=_params("parallel"),
    )(where, parts, recv)


def place_shard(name, shard, where, dtype, tr):
    r, c = shard.shape

    def body(w_ref, s_ref, o_ref):
        o_ref[...] = s_ref[...].astype(dtype)

    return pl.pallas_call(
        body, name=name,
        grid_spec=pltpu.PrefetchScalarGridSpec(
            num_scalar_prefetch=1, grid=(r // tr,),
            in_specs=[pl.BlockSpec((tr, c), lambda i, w_ref: (i, 0))],
            out_specs=pl.BlockSpec((None, tr, c), lambda i, w_ref: (w_ref[0], i, 0))),
        out_shape=jax.ShapeDtypeStruct((N_CHIPS, r, c), dtype), compiler_params=_params("parallel"),
    )(where, shard)


ANY = pl.BlockSpec(memory_space=pl.ANY)


def _place():
    x, y, c = lax.axis_index("x"), lax.axis_index("y"), lax.axis_index("c")
    chips = [(1 - x, y), (x, 1 - y), (1 - x, 1 - y)]
    return x, y, c, chips


def gather_shards(arrays):
    n = len(arrays)

    def body(*refs):
        send, pass_on, finish = _gather_phases(refs[n:2 * n], *refs[2 * n:], _spans(arrays))
        send()
        pass_on()
        finish()

    return pl.pallas_call(
        body, name="gather_shards", in_specs=[ANY] * n, out_specs=[ANY] * n,
        out_shape=[jax.ShapeDtypeStruct(a.shape, a.dtype) for a in _arrays(arrays)],
        input_output_aliases={w: w for w in range(n)}, scratch_shapes=_gather_sems(n),
        compiler_params=pltpu.CompilerParams(has_side_effects=True),
    )(*arrays)


def _gather_sems(n):
    return [pltpu.SemaphoreType.DMA((6 * n,)), pltpu.SemaphoreType.DMA((6 * n,))]


class Span(typing.NamedTuple):
    array: jax.Array
    lo: int
    hi: int
    ways: tuple = (0, 1, 2)


def _arrays(gathering):
    return [g.array if isinstance(g, Span) else g for g in gathering]


def _spans(gathering):
    return [(g.lo, g.hi, g.ways) if isinstance(g, Span) else (0, g.shape[1], (0, 1, 2)) for g in gathering]


def _gather_phases(out, send_sems, recv_sems, spans):
    n = len(out)
    if not any(ways for _, _, ways in spans):
        return (lambda: None,) * 3
    x, y, c, chips = _place()
    me = 2 * x + y
    sibling = (x, y, 1 - c)

    def half(w, chip, core):
        lo, hi, _ = spans[w]
        h = (hi - lo) // 2
        return out[w].at[chip, pl.ds(lo + core * h, h)]

    def copy(k, block, to):
        return pltpu.make_async_remote_copy(src_ref=block, dst_ref=block, send_sem=send_sems.at[k],
                                            recv_sem=recv_sems.at[k], device_id=to, device_id_type=MESH)

    def over_ici(w, j, chip):
        return copy(3 * w + j, half(w, chip, c), (chips[j][0], chips[j][1], c))

    def over_d2d(w, j, core):
        return copy(3 * n + 3 * w + j, half(w, 2 * chips[j][0] + chips[j][1], core), sibling)

    pairs = [(w, j) for w in range(n) for j in spans[w][2]]

    def send():
        for w, j in pairs:
            over_ici(w, j, me).start()

    def pass_on():
        for w, j in pairs:
            over_ici(w, j, 2 * chips[j][0] + chips[j][1]).wait_recv()
            over_d2d(w, j, c).start()

    def finish():
        for w, j in pairs:
            over_d2d(w, j, 1 - c).wait_recv()
        for w, j in pairs:
            over_ici(w, j, me).wait_send()
            over_d2d(w, j, c).wait_send()

    return send, pass_on, finish


def _relay_sems():
    return [pltpu.SemaphoreType.DMA((4,)), pltpu.SemaphoreType.DMA((4,))]


def _relay_phases(out, send_sems, recv_sems):
    x, y, c, chips = _place()
    sibling = (x, y, 1 - c)
    rows = out.shape[1]
    quarter = rows // 4
    far = 2 * chips[2][0] + chips[2][1]

    def piece(chip, way, core):
        return out.at[chip, pl.ds(way * (rows // 2) + core * quarter, quarter)]

    def copy(k, block, to):
        return pltpu.make_async_remote_copy(src_ref=block, dst_ref=block, send_sem=send_sems.at[k],
                                            recv_sem=recv_sems.at[k], device_id=to, device_id_type=MESH)

    def over_ici(way, chip):
        return copy(way, piece(chip, way, c), (chips[way][0], chips[way][1], c))

    def over_d2d(way, core):
        return copy(2 + way, piece(far, way, core), sibling)

    def send():
        for way in range(2):
            other = chips[1 - way]
            over_ici(way, 2 * other[0] + other[1]).start()

    def pass_on():
        for way in range(2):
            over_ici(way, far).wait_recv()
            over_d2d(way, c).start()

    def finish():
        for way in range(2):
            over_d2d(way, 1 - c).wait_recv()
        for way in range(2):
            other = chips[1 - way]
            over_ici(way, 2 * other[0] + other[1]).wait_send()
            over_d2d(way, c).wait_send()

    return send, pass_on, finish


def swap_halves(name, grads):
    n = len(grads)

    def body(*refs):
        send, finish = _swap_phases(refs[:n], refs[n:2 * n], *refs[2 * n:])
        send()
        finish()

    return pl.pallas_call(
        body, name=name, in_specs=[ANY] * n, out_specs=[ANY] * n, out_shape=_swap_shapes(grads),
        scratch_shapes=_swap_sems(n), compiler_params=pltpu.CompilerParams(has_side_effects=True),
    )(*grads)


def _swap_shapes(grads):
    return [jax.ShapeDtypeStruct((a.shape[0], a.shape[1] // 2, a.shape[2]), a.dtype) for a in grads]


def _swap_sems(n):
    return [pltpu.SemaphoreType.DMA((n,)), pltpu.SemaphoreType.DMA((n,))]


def _swap_phases(g, out, send_sems, recv_sems):
    x, y, c, _ = _place()

    def copies():
        return [pltpu.make_async_remote_copy(
            src_ref=g[w].at[:, pl.ds((1 - c) * (g[w].shape[1] // 2), g[w].shape[1] // 2)], dst_ref=out[w],
            send_sem=send_sems.at[w], recv_sem=recv_sems.at[w], device_id=(x, y, 1 - c), device_id_type=MESH)
            for w in range(len(g))]

    def send():
        for cp in copies():
            cp.start()

    def finish():
        for cp in copies():
            cp.wait()

    return send, finish


def _send_phases(g, out, send_sems, recv_sems):
    x, y, c, _ = _place()

    def copies():
        return [pltpu.make_async_remote_copy(
            src_ref=g[w], dst_ref=out[w], send_sem=send_sems.at[w], recv_sem=recv_sems.at[w],
            device_id=(x, y, 1 - c), device_id_type=MESH) for w in range(len(g))]

    def send():
        for cp in copies():
            cp.start()

    def finish():
        for cp in copies():
            cp.wait()

    return send, finish


def dw_in_half(name, h, dz, which, sending):
    s = h.shape[0]
    hh, tb = D // 2, IN_SHARD // 2
    n = len(sending)
    steps = IN_COLS // tb

    def body(w_ref, *refs):
        a_ref, b_ref, o_ref = refs[0], refs[1], refs[2 + n]
        j = pl.program_id(0)
        if n:
            send, finish = _send_phases(refs[2:2 + n], refs[3 + n:3 + 2 * n], *refs[3 + 2 * n:])
            pl.when(j == 0)(send)
        o_ref[...] = _dot(a_ref[...], b_ref[...], TN)
        if n:
            pl.when(j == steps - 1)(finish)

    out = pl.pallas_call(
        body, name=name,
        grid_spec=pltpu.PrefetchScalarGridSpec(
            num_scalar_prefetch=1, grid=(steps,),
            in_specs=[pl.BlockSpec((s, hh), lambda j, w: (0, w[0])), pl.BlockSpec((s, tb), lambda j, w: (0, j))]
            + [ANY] * n,
            out_specs=[pl.BlockSpec((None, hh, tb), lambda j, w: (j // 2, 0, j % 2))] + [ANY] * n,
            scratch_shapes=_swap_sems(n) if n else []),
        out_shape=[jax.ShapeDtypeStruct((N_CHIPS, hh, IN_SHARD), F32)]
        + [jax.ShapeDtypeStruct(a.shape, a.dtype) for a in sending],
        compiler_params=_params("arbitrary", communicates=bool(n)),
    )(which, h, dz, *sending)
    return out[0], out[1:]


def scatter_chips(parts):
    n = len(parts)

    def body(*refs):
        send, finish = _scatter_phases(refs[:n], refs[n:2 * n], *refs[2 * n:])
        send()
        finish()

    return pl.pallas_call(
        body, name="scatter_chips", in_specs=[ANY] * n, out_specs=[ANY] * n,
        out_shape=_scatter_shapes(parts), scratch_shapes=_scatter_sems(n),
        compiler_params=pltpu.CompilerParams(has_side_effects=True),
    )(*parts)


def _scatter_shapes(parts):
    return [jax.ShapeDtypeStruct((3,) + a.shape[1:], a.dtype) for a in parts]


def _scatter_sems(n):
    return [pltpu.SemaphoreType.DMA((3 * n,)), pltpu.SemaphoreType.DMA((3 * n,))]


def _scatter_phases(p, out, send_sems, recv_sems):
    x, y, c, chips = _place()

    def copies():
        return [pltpu.make_async_remote_copy(
            src_ref=p[w].at[2 * px + py], dst_ref=out[w].at[j], send_sem=send_sems.at[3 * w + j],
            recv_sem=recv_sems.at[3 * w + j], device_id=(px, py, c), device_id_type=MESH)
            for w in range(len(p)) for j, (px, py) in enumerate(chips)]

    def send():
        for cp in copies():
            cp.start()

    def finish():
        for cp in copies():
            cp.wait()

    return send, finish


def join_halves(arrays):
    n = len(arrays)

    def body(*refs):
        out = refs[n:2 * n]
        send_sems, recv_sems = refs[2 * n:]
        x, y, c, _ = _place()

        def copy(w, core):
            h = out[w].shape[0] // 2
            rows = out[w].at[pl.ds(core * h, h)]
            return pltpu.make_async_remote_copy(
                src_ref=rows, dst_ref=rows, send_sem=send_sems.at[w], recv_sem=recv_sems.at[w],
                device_id=(x, y, 1 - c), device_id_type=MESH)

        for w in range(n):
            copy(w, c).start()
        for w in range(n):
            copy(w, 1 - c).wait_recv()
        for w in range(n):
            copy(w, c).wait_send()

    return pl.pallas_call(
        body, name="join_halves", in_specs=[ANY] * n, out_specs=[ANY] * n,
        out_shape=[jax.ShapeDtypeStruct(a.shape, a.dtype) for a in arrays],
        input_output_aliases={w: w for w in range(n)},
        scratch_shapes=[pltpu.SemaphoreType.DMA((n,)), pltpu.SemaphoreType.DMA((n,))],
        compiler_params=pltpu.CompilerParams(has_side_effects=True),
    )(*arrays)


def allreduce_small(packed):
    r, c = packed.shape
    n_dev = 8

    def body(x_ref, all_ref, sum_ref, send_sems, recv_sems, local_sem):
        x, y, cc, chips = _place()
        me, sibling = (x, y, cc), (x, y, 1 - cc)

        def rows(px, py, pc):
            return all_ref.at[4 * px + 2 * py + pc]

        def copy(k, block, to, src=None):
            return pltpu.make_async_remote_copy(
                src_ref=rows(*block) if src is None else src, dst_ref=rows(*block), send_sem=send_sems.at[k],
                recv_sem=recv_sems.at[k], device_id=to, device_id_type=MESH)

        mine = pltpu.make_async_copy(x_ref, rows(*me), local_sem)
        mine.start()
        first = [copy(0, me, sibling, src=x_ref)]
        first += [copy(1 + j, me, (*chip, cc), src=x_ref) for j, chip in enumerate(chips)]
        for cp in first:
            cp.start()
        passed = [copy(4 + j, (*chip, cc), sibling) for j, chip in enumerate(chips)]
        for j, chip in enumerate(chips):
            copy(1 + j, (*chip, cc), me).wait_recv()
            passed[j].start()
        copy(0, sibling, me).wait_recv()
        for j, chip in enumerate(chips):
            copy(4 + j, (*chip, 1 - cc), me).wait_recv()
        for cp in first + passed:
            cp.wait_send()
        mine.wait()
        acc = all_ref[0]
        for k in range(1, n_dev):
            acc = acc + all_ref[k]
        sum_ref[...] = acc

    vm = pl.BlockSpec(memory_space=pltpu.VMEM)
    return pl.pallas_call(
        body, name="allreduce_small", in_specs=[vm], out_specs=[vm, vm],
        out_shape=[jax.ShapeDtypeStruct((n_dev, r, c), F32), jax.ShapeDtypeStruct((r, c), F32)],
        scratch_shapes=[pltpu.SemaphoreType.DMA((7,)), pltpu.SemaphoreType.DMA((7,)), pltpu.SemaphoreType.DMA],
        compiler_params=pltpu.CompilerParams(has_side_effects=True, vmem_limit_bytes=VMEM_LIMIT),
    )(packed)[1]


def local_step(x, target, vecs, w_s, bs_t, bg, wg_in, late, core=None, order=None):
    on_mesh = core is not None

    def add(names, grads, recv):
        return [add_halves("add_" + n, g, r, core, min(r.shape[1], 256)) for n, g, r in zip(names, grads, recv)]

    g_pre, ln_g, ln_b, g_post, g_fpre, g_fpost = vecs
    s = x.shape[0]
    if order is None:
        order = jnp.arange(N_CHIPS, dtype=jnp.int32)
    logc = _attn_tables(s)
    ka, kb = _alibi_tables(s)

    h = norm_pre(x, g_pre)
    wg_a, wg_b, wg_out, wg_ff1, wg_ff2 = late
    if on_mesh:
        cut = D // 4
        z, (wg_in,) = mm_in("mm_in_own", h, wg_in, order, 0, 1, None, [Span(wg_in, 0, D, (0, 1))])
        z, (wg_in,) = mm_in("mm_in_near", h, wg_in, order, 1, 2, z, [wg_in], relay=True)
        z, (wg_in, wg_out) = mm_in("mm_in_far", h, wg_in, order, 3, 1, z, [Span(wg_in, 0, D, ()), wg_out])
        ya, (wg_ff2,) = gating_fwd(z, ln_g, ln_b, w_s, bs_t, [Span(wg_ff2, 0, cut)])
        yb, lse, (wg_a, wg_b, wg_ff1, wg_ff2, bg) = attn_fwd(
            z, logc, ka, kb, [wg_a, wg_b, wg_ff1, Span(wg_ff2, cut, D), bg])
        bg = jnp.transpose(bg[:, :2, :], (1, 0, 2)).reshape(2, D)
    else:
        z, _ = mm_in("mm_in", h, wg_in, order, 0, N_CHIPS, None, [Span(wg_in, 0, D, ())])
        ya, _ = gating_fwd(z, ln_g, ln_b, w_s, bs_t, [])
        yb, lse, _ = attn_fwd(z, logc, ka, kb, [])
    w_a, w_b, w_out, w_ff2 = wg_a.reshape(D, D), wg_b.reshape(D, D), wg_out.reshape(D, D), wg_ff2.reshape(D_FF, D)
    merged, pa, pb = proj_merge(ya, yb, w_a, w_b, z, bg)
    o, x1, h2 = out_norm(merged, w_out, x, g_post, g_fpre)
    a, rl = mm_ff1(h2, wg_ff1)
    dy, df, d_gfpost, loss = ff2_loss(rl, w_ff2, x1, target, g_fpost)

    half_cols = pl.BlockSpec((D, D // 2), lambda i, j: (0, j))
    d_wff2 = mm_tn("dw_ff2", rl, df, D // 2, D, (D_FF, D), pl.BlockSpec((D // 2, D), lambda i, j: (i, 0)))
    da = ff2_bwd(df, w_ff2, a)
    d_wff1 = mm_tn("dw_ff1", h2, da, D, D // 2, (N_CHIPS, D, D),
                   pl.BlockSpec((None, D, D // 2), lambda i, j: (j // 2, 0, j % 2)))
    d_ff = [d_wff1, d_wff2.reshape(N_CHIPS, D, D)]
    dx1, do, d_gfpre, d_gpost, recv_ff = ff1_bwd_norms(da, wg_ff1, x1, o, dy, g_fpre, g_post, d_ff if on_mesh else [])
    d_wout = mm_tn("dw_out", merged, do, D, D // 2, (D, D), half_cols)
    dpa, dpb, dga, dgb, d_bg = out_bwd_gates(do, w_out, pa, pb, z, bg)
    d_wa = mm_tn("dw_a", ya, dpa, D, D // 2, (D, D), half_cols)
    d_wb = mm_tn("dw_b", yb, dpb, D, D // 2, (D, D), half_cols)
    dya = mm_nt("dy_a", dpa, w_a)
    dyb = mm_nt("dy_b", dpb, w_b)
    d_proj = [d_wa.reshape(N_CHIPS, D // N_CHIPS, D), d_wb.reshape(N_CHIPS, D // N_CHIPS, D),
              d_wout.reshape(N_CHIPS, D // N_CHIPS, D)]
    du, dv, d_ws, d_bs, d_lng, d_lnb, recv_proj = gating_bwd(z, dya, ln_g, ln_b, w_s, bs_t, d_proj if on_mesh else [])
    early = d_proj + d_ff
    parts_early = add(BIG[1:], early, list(recv_proj) + list(recv_ff)) if on_mesh else []
    dq, dk, dvb, got_early = attn_bwd(z, yb, dyb, lse, logc, ka, kb, parts_early)
    dz = jnp.concatenate([du, dv, dq, dk, dvb, dga, dgb], axis=1)
    if on_mesh:
        for_sibling, _ = dw_in_half("dw_in_sibling", h, dz, 1 - core, [])
        mine, from_sibling = dw_in_half("dw_in_mine", h, dz, core, [for_sibling])
        d_win = None
        parts_late = [add_halves("add_w_in", mine, from_sibling[0], jnp.zeros((1,), jnp.int32), 256)]
    else:
        half = IN_SHARD // 2
        d_win = mm_tn("dw_in", h, dz, D, half, (N_CHIPS, D, IN_SHARD),
                      pl.BlockSpec((None, D, half), lambda i, j: (j // 2, 0, j % 2)))
        parts_late = []
    dx, d_gpre, got_late = in_bwd_norm(dz, wg_in, x, dx1, g_pre, parts_late)

    small = dict(norm_mix_pre=d_gpre, b_gate=d_bg, ln_v_g=d_lng, ln_v_b=d_lnb, w_s=d_ws, b_s=d_bs[:, 0, :],
                 norm_mix_post=d_gpost, norm_ffn_pre=d_gfpre, norm_ffn_post=d_gfpost)
    return loss[0, 0], dx, [d_win] + early, small, parts_late + parts_early, list(got_late) + list(got_early)


BIG = ("w_in", "w_a_proj", "w_b_proj", "w_out", "w_ff1", "w_ff2")
SMALL = ("norm_mix_pre", "ln_v_g", "ln_v_b", "b_s", "norm_mix_post", "norm_ffn_pre", "norm_ffn_post", "w_s", "b_gate")
ORDER = ("norm_mix_pre", "w_in", "b_gate", "ln_v_g", "ln_v_b", "w_s", "b_s", "w_a_proj", "w_b_proj", "w_out",
         "norm_mix_post", "norm_ffn_pre", "w_ff1", "w_ff2", "norm_ffn_post")
VEC_ROWS = D // 128
WS_ROW = 7 * VEC_ROWS
BG_ROW = WS_ROW + GROUPS * CHUNK
LOSS_ROW = BG_ROW + 2 * VEC_ROWS
PACK_ROWS = LOSS_ROW + 8


def pack_small(small, loss):
    vectors = [small[n] for n in SMALL[:7]]

    def body(*refs):
        out = refs[-1]
        ws_ref, bg_ref, loss_ref = refs[7:10]
        for i, n in enumerate(SMALL[:7]):
            if n == "b_s":
                out[i * VEC_ROWS:(i + 1) * VEC_ROWS, :] = refs[i][...]
            else:
                for j in range(VEC_ROWS):
                    out[i * VEC_ROWS + j:i * VEC_ROWS + j + 1, :] = refs[i][:, j * 128:(j + 1) * 128]
        for g in range(GROUPS):
            out[WS_ROW + g * CHUNK:WS_ROW + (g + 1) * CHUNK, :] = ws_ref[g]
        for r in range(2):
            for j in range(VEC_ROWS):
                row = BG_ROW + r * VEC_ROWS + j
                out[row:row + 1, :] = bg_ref[r:r + 1, j * 128:(j + 1) * 128]
        lane = lax.broadcasted_iota(jnp.int32, (8, 128), 1)
        sub = lax.broadcasted_iota(jnp.int32, (8, 128), 0)
        out[LOSS_ROW:LOSS_ROW + 8, :] = jnp.where((lane == 0) & (sub == 0), loss_ref[...], 0.0)

    return pl.pallas_call(
        body, name="pack_small", out_shape=jax.ShapeDtypeStruct((PACK_ROWS, 128), F32),
        compiler_params=_params(),
    )(*vectors, small["w_s"], small["b_gate"], loss)


def adamw_small(summed, chip, w, m, v):
    shapes = {n: (1, D) for n in SMALL}
    shapes.update(b_s=(GROUPS, CHUNK), w_s=(GROUPS * CHUNK, CHUNK), b_gate=(2, D // N_CHIPS))
    flat = lambda t: [t[n].reshape(shapes[n]) for n in SMALL]
    per = D // N_CHIPS // 128

    def body(chip_ref, sum_ref, *refs):
        params, outs = refs[:27], refs[27:]
        sub = lax.broadcasted_iota(jnp.int32, (VEC_ROWS, 128), 0)

        def gate_row(r):
            rows = sum_ref[BG_ROW + r * VEC_ROWS:BG_ROW + (r + 1) * VEC_ROWS, :]
            return jnp.concatenate([jnp.sum(jnp.where(sub == per * chip_ref[0] + j, rows, 0.0), axis=0, keepdims=True)
                                    for j in range(per)], axis=1)

        for i, n in enumerate(SMALL):
            if n == "b_s":
                g = sum_ref[i * VEC_ROWS:(i + 1) * VEC_ROWS, :]
            elif n == "w_s":
                g = sum_ref[WS_ROW:BG_ROW, :]
            elif n == "b_gate":
                g = jnp.concatenate([gate_row(0), gate_row(1)], axis=0)
            else:
                g = jnp.concatenate([sum_ref[i * VEC_ROWS + j:i * VEC_ROWS + j + 1, :] for j in range(VEC_ROWS)],
                                    axis=1)
            delta, nm, nv = _adamw_math(params[i][...], g, params[9 + i][...], params[18 + i][...])
            outs[4 * i][...], outs[4 * i + 1][...], outs[4 * i + 2][...], outs[4 * i + 3][...] = g, delta, nm, nv

    vm = pl.BlockSpec(memory_space=pltpu.VMEM)
    res = pl.pallas_call(
        body, name="adamw_small",
        in_specs=[pl.BlockSpec(memory_space=pltpu.SMEM)] + [vm] * 28, out_specs=[vm] * 36,
        out_shape=[jax.ShapeDtypeStruct(shapes[n], F32) for n in SMALL for _ in range(4)],
        compiler_params=_params(),
    )(chip, summed, *flat(w), *flat(m), *flat(v))
    return {n: tuple(r.reshape(w[n].shape) for r in res[4 * i:4 * i + 4]) for i, n in enumerate(SMALL)}


def kernel(x, norm_mix_pre, w_in, b_gate, ln_v_g, ln_v_b, w_s, b_s, w_a_proj, w_b_proj, w_out, norm_mix_post, norm_ffn_pre, w_ff1, w_ff2, norm_ffn_post, loss_target, m_norm_mix_pre, m_w_in, m_b_gate, m_ln_v_g, m_ln_v_b, m_w_s, m_b_s, m_w_a_proj, m_w_b_proj, m_w_out, m_norm_mix_post, m_norm_ffn_pre, m_w_ff1, m_w_ff2, m_norm_ffn_post, v_norm_mix_pre, v_w_in, v_b_gate, v_ln_v_g, v_ln_v_b, v_w_s, v_b_s, v_w_a_proj, v_w_b_proj, v_w_out, v_norm_mix_post, v_norm_ffn_pre, v_w_ff1, v_w_ff2, v_norm_ffn_post):
    w = dict(norm_mix_pre=norm_mix_pre, w_in=w_in, b_gate=b_gate, ln_v_g=ln_v_g, ln_v_b=ln_v_b, w_s=w_s, b_s=b_s,
             w_a_proj=w_a_proj, w_b_proj=w_b_proj, w_out=w_out, norm_mix_post=norm_mix_post,
             norm_ffn_pre=norm_ffn_pre, w_ff1=w_ff1, w_ff2=w_ff2, norm_ffn_post=norm_ffn_post)
    m = dict(norm_mix_pre=m_norm_mix_pre, w_in=m_w_in, b_gate=m_b_gate, ln_v_g=m_ln_v_g, ln_v_b=m_ln_v_b, w_s=m_w_s,
             b_s=m_b_s, w_a_proj=m_w_a_proj, w_b_proj=m_w_b_proj, w_out=m_w_out, norm_mix_post=m_norm_mix_post,
             norm_ffn_pre=m_norm_ffn_pre, w_ff1=m_w_ff1, w_ff2=m_w_ff2, norm_ffn_post=m_norm_ffn_post)
    v = dict(norm_mix_pre=v_norm_mix_pre, w_in=v_w_in, b_gate=v_b_gate, ln_v_g=v_ln_v_g, ln_v_b=v_ln_v_b, w_s=v_w_s,
             b_s=v_b_s, w_a_proj=v_w_a_proj, w_b_proj=v_w_b_proj, w_out=v_w_out, norm_mix_post=v_norm_mix_post,
             norm_ffn_pre=v_norm_ffn_pre, w_ff1=v_w_ff1, w_ff2=v_w_ff2, norm_ffn_post=v_norm_ffn_post)
    chip = 2 * lax.axis_index("x") + lax.axis_index("y")
    core = lax.axis_index("c")

    where = jnp.stack([chip, core]).astype(jnp.int32)
    placed = [place_shard("place_" + n, w[n][0], where, BF16, min(w[n].shape[1], 256)) for n in BIG]
    placed.append(place_shard("place_b_gate", jnp.pad(b_gate[0], ((0, 14), (0, 0))), where, F32, 16))
    vecs = (norm_mix_pre, ln_v_g, ln_v_b, norm_mix_post, norm_ffn_pre, norm_ffn_post)
    loss, dx, _, small, parts, got = local_step(
        x[0], loss_target[0], vecs, w_s[0], b_s[0].T, placed[6], placed[0], placed[1:6],
        core=jnp.reshape(core, (1,)).astype(jnp.int32),
        order=jnp.stack([chip, chip ^ 2, chip ^ 1, chip ^ 3]).astype(jnp.int32))

    halves = [sum_chips("sum_" + n, p, r, where, min(p.shape[1], 256)) for n, p, r in zip(BIG, parts, got)]
    grads = dict(zip(BIG, join_halves(halves)))

    summed = allreduce_small(pack_small(small, loss.reshape(1, 1)))
    loss = summed[LOSS_ROW, 0]

    new = adamw_small(summed, jnp.reshape(chip, (1,)).astype(jnp.int32), w, m, v)
    for n in BIG:
        shape = w[n].shape
        res = adamw("adamw_" + n, w[n][0], grads[n], m[n][0], v[n][0], min(shape[1], 256))
        new[n] = tuple(r.reshape(shape) for r in res)

    outs = [loss, dx[None]]
    for i in range(4):
        outs += [new[n][i] for n in ORDER]
    return tuple(outs)
```

```python
import functools
import math
import typing

import numpy as np
import jax
import jax.numpy as jnp
from jax import lax
from jax.experimental import pallas as pl
from jax.experimental.pallas import tpu as pltpu

F32 = jnp.float32
BF16 = jnp.bfloat16
MESH = pl.DeviceIdType.MESH

D = 1024
EPS = 1e-6
CHUNK = 128
GROUPS = 8
HEADS = 16
HEAD_DIM = 64
ATT_T = 256
ATT_GROUP = 4
ATT_BWD_GROUP = 2
N_CHIPS = 4
D_FF = 4 * D
IN_COLS = 7 * D
IN_SHARD = IN_COLS // N_CHIPS
MASKED = -1e30
VMEM_LIMIT = 56 * 2 ** 20

ADAM_LR, ADAM_B1, ADAM_B2, ADAM_EPS, ADAM_WD, ADAM_STEP = 0.001, 0.9, 0.999, 1e-08, 0.01, 10

NN = (((1,), (0,)), ((), ()))
NT = (((1,), (1,)), ((), ()))
TN = (((0,), (0,)), ((), ()))


def _dot(a, b, dims=NN):
    return lax.dot_general(a, b, dims, preferred_element_type=F32)


def _params(*sem, communicates=False):
    return pltpu.CompilerParams(dimension_semantics=sem or None, vmem_limit_bytes=VMEM_LIMIT,
                                has_side_effects=communicates)


def _rows(tr, c, col=0):
    return pl.BlockSpec((tr, c), lambda i: (i, col))


def _full(shape):
    n = len(shape)
    return pl.BlockSpec(shape, lambda *_: (0,) * n)


def _gelu(x):
    k = math.sqrt(2.0 / math.pi)
    return 0.5 * x * (1.0 + jnp.tanh(k * (x + 0.044715 * x * x * x)))


def _gelu_and_grad(x):
    k = math.sqrt(2.0 / math.pi)
    t = jnp.tanh(k * (x + 0.044715 * x * x * x))
    g = 0.5 * x * (1.0 + t)
    dg = 0.5 * (1.0 + t) + 0.5 * x * (1.0 - t * t) * (k * (1.0 + 3.0 * 0.044715 * x * x))
    return g, dg


def _sigmoid(x):
    return 1.0 / (1.0 + jnp.exp(-x))


def _rms(x):
    r = lax.rsqrt(jnp.mean(x * x, axis=-1, keepdims=True) + EPS)
    return x * r, r


def _rms_bwd(dn, xhat, r):
    return r * (dn - xhat * jnp.mean(dn * xhat, axis=-1, keepdims=True))


def norm_pre(x, g):
    s = x.shape[0]
    tr = 512

    def body(x_ref, g_ref, h_ref):
        xhat, _ = _rms(x_ref[...])
        h_ref[...] = (xhat * g_ref[...]).astype(BF16)

    return pl.pallas_call(
        body, name="norm_pre", grid=(s // tr,),
        in_specs=[_rows(tr, D), _full((1, D))], out_specs=_rows(tr, D),
        out_shape=jax.ShapeDtypeStruct((s, D), BF16), compiler_params=_params("parallel"),
    )(x, g)


def mm_in(name, h, wg, order, first, count, z, gathering, relay=False, casting=()):
    s = h.shape[0]
    tm, tn = 1024, IN_SHARD // 2
    per = IN_SHARD // tn
    n, m = len(gathering), len(casting)
    nj, ni = count * per, s // tm
    has_z = z is not None
    arrays = _arrays(gathering)
    at = [k for k, a in enumerate(arrays) if a is wg][0]

    def body(order_ref, *refs):
        a_ref = refs[0]
        cast_in = refs[1 + has_z + n:1 + has_z + n + m]
        o_ref = refs[1 + has_z + n + m]
        held = refs[2 + has_z + n + m:2 + has_z + 2 * n + m]
        cast_out = refs[2 + has_z + 2 * n + m:2 + has_z + 2 * n + 2 * m]
        tile, tile_sem = refs[2 + has_z + 2 * n + 2 * m:4 + has_z + 2 * n + 2 * m]
        j, i = pl.program_id(0), pl.program_id(1)
        sems = refs[4 + has_z + 2 * n + 2 * m:]
        for src, dst in zip(cast_in, cast_out):
            dst[...] = src[...].astype(BF16)
        phases = [_gather_phases(held, *sems[:2], _spans(gathering))]
        if relay:
            phases.append(_relay_phases(held[at], *sems[2:]))
        def each(fs):
            def run():
                for f in fs:
                    f()
            return run

        send, pass_on, finish = [each(fs) for fs in zip(*phases)]

        def fetch(t):
            chip = order_ref[first + t // per]
            return pltpu.make_async_copy(held[at].at[chip, :, pl.ds((t % per) * tn, tn)], tile.at[t % 2],
                                         tile_sem.at[t % 2])

        @pl.when(i == 0)
        def _():
            @pl.when(j == 0)
            def _():
                send()
                fetch(0).start()

            fetch(j).wait()

            @pl.when(j + 1 < nj)
            def _():
                fetch(j + 1).start()

        pl.when((j == nj - 1) & (i == 0))(pass_on)
        rows = pl.ds(pl.multiple_of(i * tm, tm), tm)
        o_ref[...] = _dot(a_ref[rows, :], tile[j % 2]).astype(BF16)
        pl.when((j == nj - 1) & (i == ni - 1))(finish)

    steps = nj * ni
    out = pl.pallas_call(
        body, name=name,
        grid_spec=pltpu.PrefetchScalarGridSpec(
            num_scalar_prefetch=1, grid=(nj, ni),
            in_specs=[pl.BlockSpec((s, D), lambda j, i, o: (0, 0))] + [ANY] * (has_z + n)
            + [pl.BlockSpec((a.shape[0] // steps, a.shape[1]), lambda j, i, o: (j * ni + i, 0)) for a in casting],
            out_specs=[pl.BlockSpec((tm, tn), lambda j, i, o: (i, o[first + j // per] * per + j % per))] + [ANY] * n
            + [pl.BlockSpec((None, a.shape[0] // steps, a.shape[1]), lambda j, i, o: (o[0], j * ni + i, 0))
               for a in casting],
            scratch_shapes=[pltpu.VMEM((2, D, tn), BF16), pltpu.SemaphoreType.DMA((2,))] + _gather_sems(n)
            + (_relay_sems() if relay else [])),
        out_shape=[jax.ShapeDtypeStruct((s, IN_COLS), BF16)] + [jax.ShapeDtypeStruct(a.shape, a.dtype) for a in arrays]
        + [jax.ShapeDtypeStruct((N_CHIPS,) + a.shape, BF16) for a in casting],
        input_output_aliases={**({2: 0} if has_z else {}), **{2 + has_z + w: 1 + w for w in range(n)}},
        compiler_params=_params("arbitrary", "arbitrary", communicates=True),
    )(order, h, *([z] if has_z else []), *arrays, *casting)
    return out[0], out[1:1 + n], out[1 + n:]


def _tril_ws(ws_ref, g):
    r = lax.broadcasted_iota(jnp.int32, (CHUNK, CHUNK), 0)
    c = lax.broadcasted_iota(jnp.int32, (CHUNK, CHUNK), 1)
    return jnp.where(c <= r, ws_ref[g], 0.0).astype(BF16)


def _layer_norm(v):
    mu = jnp.mean(v, axis=-1, keepdims=True)
    d = v - mu
    rstd = lax.rsqrt(jnp.mean(d * d, axis=-1, keepdims=True) + EPS)
    return d * rstd, rstd


def gating_fwd(z, ln_g, ln_b, w_s, bs_t, gathering):
    s = z.shape[0]
    n = len(gathering)
    steps = s // CHUNK

    def body(*refs):
        u_ref, v_ref, lg_ref, lb_ref, ws_ref, bst_ref = refs[:6]
        ya_ref = refs[6 + n]
        ci = pl.program_id(0)
        if n:
            send, pass_on, finish = _gather_phases(refs[7 + n:7 + 2 * n], *refs[7 + 2 * n:], _spans(gathering))
            pl.when(ci == 0)(send)
            pl.when(ci == steps * 3 // 4)(pass_on)
        ug = _gelu(u_ref[...].astype(F32))
        vhat, _ = _layer_norm(_gelu(v_ref[...].astype(F32)))
        vn = (vhat * lg_ref[...] + lb_ref[...]).astype(BF16)
        for g in range(GROUPS):
            cols = slice(g * CHUNK, (g + 1) * CHUNK)
            mixed = _dot(_tril_ws(ws_ref, g), vn[:, cols]) + bst_ref[:, g:g + 1]
            ya_ref[:, cols] = (ug[:, cols] * mixed).astype(BF16)
        if n:
            pl.when(ci == steps - 1)(finish)

    out = pl.pallas_call(
        body, name="gating_fwd", grid=(steps,),
        in_specs=[_rows(CHUNK, D, 0), _rows(CHUNK, D, 1), _full((1, D)), _full((1, D)),
                  _full((GROUPS, CHUNK, CHUNK)), _full((CHUNK, GROUPS))] + [ANY] * n,
        out_specs=[_rows(CHUNK, D)] + [ANY] * n,
        out_shape=[jax.ShapeDtypeStruct((s, D), BF16)]
        + [jax.ShapeDtypeStruct(a.shape, a.dtype) for a in _arrays(gathering)],
        input_output_aliases={6 + w: 1 + w for w in range(n)},
        scratch_shapes=_gather_sems(n) if n else [],
        compiler_params=_params("arbitrary", communicates=bool(n)),
    )(z, z, ln_g, ln_b, w_s, bs_t, *_arrays(gathering))
    return out[0], out[1:]


def _attn_tables(s):
    nd = s // ATT_T
    r = np.arange(ATT_T)[None, :, None]
    c = np.arange(ATT_T)[None, None, :]
    delta = np.arange(nd)[:, None, None] * ATT_T + r - c
    count = np.zeros(delta.shape, np.int64)
    for window, dilation in ((128, 1), (512, 4), (2048, 16)):
        count += (delta >= 0) & (delta % dilation == 0) & (delta <= window)
    logc = np.where(count > 0, np.log(np.maximum(count, 1)), MASKED)
    return jnp.asarray(logc, F32)


AUG = 3


def _split3_np(x):
    terms, rest = [], np.asarray(x, np.float64)
    for _ in range(AUG):
        term = np.asarray(rest.astype(jnp.bfloat16), np.float64)
        terms.append(term)
        rest = rest - term
    return terms


def _split3(x):
    terms, rest = [], x
    for _ in range(AUG):
        term = rest.astype(BF16).astype(F32)
        terms.append(term)
        rest = rest - term
    return terms


def _alibi_tables(s):
    nb = s // ATT_T
    slopes = np.exp2(-8.0 * np.arange(1, HEADS + 1, dtype=np.float64) / HEADS)
    ka = np.zeros((HEADS // 2, 2, ATT_T, 128), np.float32)
    kb = np.zeros((HEADS // 2, 2, nb, 128), np.float32)
    for p in range(HEADS // 2):
        for e in range(2):
            base = HEAD_DIM * (1 - e)
            for a, term in enumerate(_split3_np(slopes[2 * p + e] * np.arange(ATT_T))):
                ka[p, e, :, base + a] = term
            for a, term in enumerate(_split3_np(slopes[2 * p + e] * ATT_T * np.arange(nb))):
                kb[p, e, :, base + AUG + a] = term
            ka[p, e, :, base + 2 * AUG:base + 3 * AUG] = 1.0
    return jnp.asarray(ka), jnp.asarray(kb)


def _head_masks():
    lane = lax.broadcasted_iota(jnp.int32, (1, 128), 1)
    first = lane < HEAD_DIM

    def ones(e, n):
        base = HEAD_DIM * (1 - e)
        return ((lane >= base) & (lane < base + n)).astype(F32)

    return first, lane, ones


def _place3(lane, at, terms, other):
    for a, term in enumerate(terms):
        other = jnp.where(lane == at + a, term, other)
    return other


def attn_fwd(z, logc, ka, kb, gathering):
    s = z.shape[0]
    nq = s // ATT_T
    t = ATT_T
    n = len(gathering)
    grp = ATT_GROUP
    ngrp = HEADS // 2 // grp
    wide = 128 * grp
    qcol, kcol, vcol = 2 * D // wide, 3 * D // wide, 4 * D // wide

    def body(*refs):
        q_ref, k_ref, v_ref, lc_ref, ka_ref, kb_ref = refs[:6]
        y_ref, lse_ref = refs[6 + n:8 + n]
        q_s, k_s, v_s, m_s, l_s, acc_s = refs[8 + 2 * n:14 + 2 * n]
        gi, qi = pl.program_id(0), pl.program_id(1)
        first, lane, ones = _head_masks()
        if n:
            send, pass_on, finish = _gather_phases(refs[8 + n:8 + 2 * n], *refs[14 + 2 * n:], _spans(gathering))
            pl.when((gi == 0) & (qi == 0))(send)
            pl.when((gi == ngrp - 1) & (qi == nq * 3 // 4))(pass_on)

        @pl.when(qi == 0)
        def _():
            sel = jnp.broadcast_to(first.astype(F32), (t, 128))
            for pr in range(grp):
                cols = slice(pr * 128, (pr + 1) * 128)
                for jb in range(nq):
                    kj = k_ref[jb * t:(jb + 1) * t, cols].astype(F32)
                    vj = v_ref[jb * t:(jb + 1) * t, cols].astype(F32)
                    k_s[pr, 0, jb] = jnp.where(first, kj, ka_ref[pr, 0] + kb_ref[pr, 0, jb:jb + 1, :]).astype(BF16)
                    k_s[pr, 1, jb] = jnp.where(first, ka_ref[pr, 1] + kb_ref[pr, 1, jb:jb + 1, :], kj).astype(BF16)
                    v_s[pr, jb, 0:t, 0:128] = jnp.where(first, vj, 0.0).astype(BF16)
                    v_s[pr, jb, t:2 * t, 0:128] = jnp.where(first, 0.0, vj).astype(BF16)
                    v_s[pr, jb, 0:t, 128:256] = sel.astype(BF16)
                    v_s[pr, jb, t:2 * t, 128:256] = (1.0 - sel).astype(BF16)

        for pr in range(grp):
            q = q_ref[:, pr * 128:(pr + 1) * 128].astype(F32) * (1.0 / math.sqrt(HEAD_DIM))
            q_s[pr, 0] = jnp.where(first, q, ones(0, 2 * AUG)).astype(BF16)
            q_s[pr, 1] = jnp.where(first, ones(1, 2 * AUG), q).astype(BF16)
        m_s[...] = jnp.full_like(m_s, MASKED)
        l_s[...] = jnp.zeros_like(l_s)
        acc_s[...] = jnp.zeros_like(acc_s)

        def scores(j):
            return tuple(_dot(q_s[pr, e], k_s[pr, e, j], NT) for pr in range(grp) for e in range(2))

        def step(j, carry):
            softmax_block(j, scores(j))
            return carry

        def softmax_block(j, u):
            lc = lc_ref[qi - j]
            for pr in range(grp):
                u0 = u[2 * pr] + lc
                u1 = u[2 * pr + 1] + lc
                m0, m1 = m_s[pr, 0], m_s[pr, 1]
                n0 = jnp.maximum(m0, jnp.max(u0, axis=-1, keepdims=True))
                n1 = jnp.maximum(m1, jnp.max(u1, axis=-1, keepdims=True))
                m_s[pr, 0], m_s[pr, 1] = n0, n1
                p = jnp.concatenate([jnp.exp(u0 - jnp.concatenate([n0, n0], axis=1)).astype(BF16),
                                     jnp.exp(u1 - jnp.concatenate([n1, n1], axis=1)).astype(BF16)], axis=1)
                pv = _dot(p, v_s[pr, j])
                alpha = jnp.where(first, jnp.exp(m0 - n0), jnp.exp(m1 - n1))
                acc_s[pr] = acc_s[pr] * alpha + pv[:, 0:128]
                l_s[pr] = l_s[pr] * alpha + pv[:, 128:256]

        lax.fori_loop(0, qi + 1, step, 0)
        for pr in range(grp):
            cols = slice(pr * 128, (pr + 1) * 128)
            y_ref[:, cols] = (acc_s[pr] / l_s[pr]).astype(BF16)
            lse_ref[:, cols] = jnp.where(first, m_s[pr, 0], m_s[pr, 1]) + jnp.log(l_s[pr])
        if n:
            pl.when((gi == ngrp - 1) & (qi == nq - 1))(finish)

    out = pl.pallas_call(
        body, name="attn_fwd", grid=(ngrp, nq),
        in_specs=[pl.BlockSpec((t, wide), lambda g, i: (i, qcol + g)),
                  pl.BlockSpec((s, wide), lambda g, i: (0, kcol + g)),
                  pl.BlockSpec((s, wide), lambda g, i: (0, vcol + g)),
                  _full((nq, t, t)),
                  pl.BlockSpec((grp, 2, t, 128), lambda g, i: (g, 0, 0, 0)),
                  pl.BlockSpec((grp, 2, nq, 128), lambda g, i: (g, 0, 0, 0))] + [ANY] * n,
        out_specs=[pl.BlockSpec((t, wide), lambda g, i: (i, g)), pl.BlockSpec((t, wide), lambda g, i: (i, g))]
        + [ANY] * n,
        out_shape=[jax.ShapeDtypeStruct((s, D), BF16), jax.ShapeDtypeStruct((s, D), F32)]
        + [jax.ShapeDtypeStruct(a.shape, a.dtype) for a in _arrays(gathering)],
        input_output_aliases={6 + w: 2 + w for w in range(n)},
        scratch_shapes=[pltpu.VMEM((grp, 2, t, 128), BF16), pltpu.VMEM((grp, 2, nq, t, 128), BF16),
                        pltpu.VMEM((grp, nq, 2 * t, 256), BF16), pltpu.VMEM((grp, 2, t, 128), F32),
                        pltpu.VMEM((grp, t, 128), F32), pltpu.VMEM((grp, t, 128), F32)]
        + (_gather_sems(n) if n else []),
        compiler_params=_params("arbitrary", "arbitrary", communicates=bool(n)),
    )(z, z, z, logc, ka, kb, *_arrays(gathering))
    return out[0], out[1], out[2:]


def proj_merge(ya, yb, wa, wb, z, bg):
    s = ya.shape[0]
    tm = 512

    def body(ya_ref, yb_ref, wa_ref, wb_ref, ga_ref, gb_ref, bg_ref, mg_ref, pa_ref, pb_ref):
        pa = _dot(ya_ref[...], wa_ref[...])
        pb = _dot(yb_ref[...], wb_ref[...])
        sa = _sigmoid(ga_ref[...] + bg_ref[0:1, :])
        sb = _sigmoid(gb_ref[...] + bg_ref[1:2, :])
        mg_ref[...] = (sa * pa + sb * pb).astype(BF16)
        pa_ref[...] = pa.astype(BF16)
        pb_ref[...] = pb.astype(BF16)

    out = jax.ShapeDtypeStruct((s, D), BF16)
    return pl.pallas_call(
        body, name="proj_merge", grid=(s // tm,),
        in_specs=[_rows(tm, D), _rows(tm, D), _full((D, D)), _full((D, D)),
                  _rows(tm, D, 5), _rows(tm, D, 6), _full((2, D))],
        out_specs=[_rows(tm, D)] * 3, out_shape=[out] * 3, compiler_params=_params("parallel"),
    )(ya, yb, wa, wb, z, z, bg)


def out_norm(merged, w_out, x, g_post, g_fpre):
    s = x.shape[0]
    tm = 512

    def body(mg_ref, w_ref, x_ref, gp_ref, gf_ref, o_ref, x1_ref, h2_ref):
        o = _dot(mg_ref[...], w_ref[...])
        ohat, _ = _rms(o)
        x1 = x_ref[...] + ohat * gp_ref[...]
        x1hat, _ = _rms(x1)
        o_ref[...] = o
        x1_ref[...] = x1
        h2_ref[...] = (x1hat * gf_ref[...]).astype(BF16)

    return pl.pallas_call(
        body, name="out_norm", grid=(s // tm,),
        in_specs=[_rows(tm, D), _full((D, D)), _rows(tm, D), _full((1, D)), _full((1, D))],
        out_specs=[_rows(tm, D)] * 3,
        out_shape=[jax.ShapeDtypeStruct((s, D), F32), jax.ShapeDtypeStruct((s, D), F32),
                   jax.ShapeDtypeStruct((s, D), BF16)],
        compiler_params=_params("parallel"),
    )(merged, w_out, x, g_post, g_fpre)


def mm_ff1(h2, wg):
    s = h2.shape[0]
    tm = 1024

    def body(a_ref, b_ref, o_ref, r_ref):
        a = _dot(a_ref[...], b_ref[...])
        o_ref[...] = a.astype(BF16)
        r = jnp.maximum(a, 0.0)
        r_ref[...] = (r * r).astype(BF16)

    return pl.pallas_call(
        body, name="mm_ff1", grid=(s // tm, N_CHIPS),
        in_specs=[pl.BlockSpec((tm, D), lambda i, j: (i, 0)), pl.BlockSpec((None, D, D), lambda i, j: (j, 0, 0))],
        out_specs=[pl.BlockSpec((tm, D), lambda i, j: (i, j))] * 2,
        out_shape=[jax.ShapeDtypeStruct((s, D_FF), BF16), jax.ShapeDtypeStruct((s, D_FF), BF16)],
        compiler_params=_params("parallel", "parallel"),
    )(h2, wg)


def ff2_loss(rl, w_ff2, x1, target, g_fpost):
    s = x1.shape[0]
    tm = 256

    def body(rl_ref, w_ref, x1_ref, t_ref, g_ref, dy_ref, df_ref, dg_ref, loss_ref):
        @pl.when(pl.program_id(0) == 0)
        def _():
            dg_ref[...] = jnp.zeros_like(dg_ref)
            loss_ref[...] = jnp.zeros_like(loss_ref)

        f = _dot(rl_ref[...], w_ref[...])
        fhat, r = _rms(f)
        err = x1_ref[...] + fhat * g_ref[...] - t_ref[...]
        loss_ref[...] += 0.5 * jnp.sum(jnp.mean(err * err, axis=-1, keepdims=True), axis=0, keepdims=True)
        dy = err * (1.0 / D)
        dy_ref[...] = dy
        dg_ref[...] += jnp.sum(dy * fhat, axis=0, keepdims=True)
        df_ref[...] = _rms_bwd(dy * g_ref[...], fhat, r).astype(BF16)

    return pl.pallas_call(
        body, name="ff2_loss", grid=(s // tm,),
        in_specs=[_rows(tm, D_FF), _full((D_FF, D)), _rows(tm, D), _rows(tm, D), _full((1, D))],
        out_specs=[_rows(tm, D), _rows(tm, D), _full((1, D)), _full((1, 1))],
        out_shape=[jax.ShapeDtypeStruct((s, D), F32), jax.ShapeDtypeStruct((s, D), BF16),
                   jax.ShapeDtypeStruct((1, D), F32), jax.ShapeDtypeStruct((1, 1), F32)],
        compiler_params=_params("arbitrary"),
    )(rl, w_ff2, x1, target, g_fpost)


def mm_tn(name, a, b, ta, tb, out_shape, out_spec):
    s = a.shape[0]

    def body(a_ref, b_ref, o_ref):
        o_ref[...] = _dot(a_ref[...], b_ref[...], TN)

    return pl.pallas_call(
        body, name=name, grid=(a.shape[1] // ta, b.shape[1] // tb),
        in_specs=[pl.BlockSpec((s, ta), lambda i, j: (0, i)), pl.BlockSpec((s, tb), lambda i, j: (0, j))],
        out_specs=out_spec, out_shape=jax.ShapeDtypeStruct(out_shape, F32),
        compiler_params=_params("parallel", "parallel"),
    )(a, b)


def mm_nt(name, a, w):
    s = a.shape[0]
    tm = 512

    def body(a_ref, w_ref, o_ref):
        o_ref[...] = _dot(a_ref[...], w_ref[...], NT).astype(BF16)

    return pl.pallas_call(
        body, name=name, grid=(s // tm,), in_specs=[_rows(tm, D), _full((D, D))], out_specs=_rows(tm, D),
        out_shape=jax.ShapeDtypeStruct((s, D), BF16), compiler_params=_params("parallel"),
    )(a, w)


def ff2_bwd(df, w_ff2, a):
    s = df.shape[0]
    tm = 1024

    def body(df_ref, w_ref, a_ref, da_ref):
        drl = _dot(df_ref[...], w_ref[...], NT)
        da_ref[...] = (drl * (2.0 * jnp.maximum(a_ref[...].astype(F32), 0.0))).astype(BF16)

    return pl.pallas_call(
        body, name="ff2_bwd", grid=(s // tm, D_FF // D),
        in_specs=[pl.BlockSpec((tm, D), lambda i, j: (i, 0)), pl.BlockSpec((D, D), lambda i, j: (j, 0)),
                  pl.BlockSpec((tm, D), lambda i, j: (i, j))],
        out_specs=pl.BlockSpec((tm, D), lambda i, j: (i, j)),
        out_shape=jax.ShapeDtypeStruct((s, D_FF), BF16), compiler_params=_params("parallel", "parallel"),
    )(df, w_ff2, a)


def ff1_bwd_norms(da, wg, x1, o, dy, g_fpre, g_post, swapping):
    s = x1.shape[0]
    tm = 512
    n = len(swapping)

    def body(*refs):
        da_ref, w_ref, x1_ref, o_ref, dy_ref, gf_ref, gp_ref = refs[:7]
        dx1_ref, do_ref, dgf_ref, dgp_ref = refs[7 + n:11 + n]
        acc_ref = refs[11 + 2 * n]
        i, k = pl.program_id(0), pl.program_id(1)
        if n:
            send, finish = _swap_phases(refs[7:7 + n], refs[11 + n:11 + 2 * n], *refs[12 + 2 * n:])
            pl.when((i == 0) & (k == 0))(send)

        @pl.when((i == 0) & (k == 0))
        def _():
            dgf_ref[...] = jnp.zeros_like(dgf_ref)
            dgp_ref[...] = jnp.zeros_like(dgp_ref)

        part = _dot(da_ref[...], w_ref[...], NT)

        @pl.when(k == 0)
        def _():
            acc_ref[...] = part

        @pl.when(k > 0)
        def _():
            acc_ref[...] += part

        @pl.when(k == N_CHIPS - 1)
        def _():
            dh2 = acc_ref[...]
            x1hat, r2 = _rms(x1_ref[...])
            dgf_ref[...] += jnp.sum(dh2 * x1hat, axis=0, keepdims=True)
            dx1 = dy_ref[...] + _rms_bwd(dh2 * gf_ref[...], x1hat, r2)
            ohat, r1 = _rms(o_ref[...])
            dgp_ref[...] += jnp.sum(dx1 * ohat, axis=0, keepdims=True)
            dx1_ref[...] = dx1
            do_ref[...] = _rms_bwd(dx1 * gp_ref[...], ohat, r1).astype(BF16)

        if n:
            pl.when((i == s // tm - 1) & (k == N_CHIPS - 1))(finish)

    row = pl.BlockSpec((tm, D), lambda i, k: (i, 0))
    vec = pl.BlockSpec((1, D), lambda i, k: (0, 0))
    res = pl.pallas_call(
        body, name="ff1_bwd_norms", grid=(s // tm, N_CHIPS),
        in_specs=[pl.BlockSpec((tm, D), lambda i, k: (i, k)), pl.BlockSpec((None, D, D), lambda i, k: (k, 0, 0)),
                  row, row, row, vec, vec] + [ANY] * n,
        out_specs=[row, row, vec, vec] + [ANY] * n,
        out_shape=[jax.ShapeDtypeStruct((s, D), F32), jax.ShapeDtypeStruct((s, D), BF16),
                   jax.ShapeDtypeStruct((1, D), F32), jax.ShapeDtypeStruct((1, D), F32)] + _swap_shapes(swapping),
        scratch_shapes=[pltpu.VMEM((tm, D), F32)] + (_swap_sems(n) if n else []),
        compiler_params=_params("arbitrary", "arbitrary", communicates=bool(n)),
    )(da, wg, x1, o, dy, g_fpre, g_post, *swapping)
    return res[0], res[1], res[2], res[3], res[4:]


def out_bwd_gates(do, w_out, pa, pb, z, bg):
    s = do.shape[0]
    tm = 512

    def body(do_ref, w_ref, pa_ref, pb_ref, ga_ref, gb_ref, bg_ref, dpa_ref, dpb_ref, dga_ref, dgb_ref, dbg_ref):
        @pl.when(pl.program_id(0) == 0)
        def _():
            dbg_ref[...] = jnp.zeros_like(dbg_ref)

        dm = _dot(do_ref[...], w_ref[...], NT)
        sa = _sigmoid(ga_ref[...] + bg_ref[0:1, :])
        sb = _sigmoid(gb_ref[...] + bg_ref[1:2, :])
        dpa_ref[...] = (dm * sa).astype(BF16)
        dpb_ref[...] = (dm * sb).astype(BF16)
        dga = dm * pa_ref[...].astype(F32) * (sa * (1.0 - sa))
        dgb = dm * pb_ref[...].astype(F32) * (sb * (1.0 - sb))
        dga_ref[...] = dga.astype(BF16)
        dgb_ref[...] = dgb.astype(BF16)
        dbg_ref[0:1, :] += jnp.sum(dga, axis=0, keepdims=True)
        dbg_ref[1:2, :] += jnp.sum(dgb, axis=0, keepdims=True)

    out = jax.ShapeDtypeStruct((s, D), BF16)
    return pl.pallas_call(
        body, name="out_bwd_gates", grid=(s // tm,),
        in_specs=[_rows(tm, D), _full((D, D)), _rows(tm, D), _rows(tm, D), _rows(tm, D, 5), _rows(tm, D, 6),
                  _full((2, D))],
        out_specs=[_rows(tm, D)] * 4 + [_full((2, D))],
        out_shape=[out] * 4 + [jax.ShapeDtypeStruct((2, D), F32)], compiler_params=_params("arbitrary"),
    )(do, w_out, pa, pb, z, z, bg)


def gating_bwd(z, dya, ln_g, ln_b, w_s, bs_t, swapping):
    s = z.shape[0]
    ones = functools.partial(jnp.ones, (8, CHUNK), BF16)
    n = len(swapping)

    def body(*refs):
        u_ref, v_ref, dya_ref, lg_ref, lb_ref, ws_ref, bst_ref = refs[:7]
        du_ref, dv_ref, dws_ref, dbs_ref, dlg_ref, dlb_ref = refs[7 + n:13 + n]
        dvn_ref = refs[13 + 2 * n]
        ci = pl.program_id(0)
        if n:
            send, finish = _swap_phases(refs[7:7 + n], refs[13 + n:13 + 2 * n], *refs[14 + 2 * n:])
            pl.when(ci == 0)(send)

        @pl.when(ci == 0)
        def _():
            dws_ref[...] = jnp.zeros_like(dws_ref)
            dbs_ref[...] = jnp.zeros_like(dbs_ref)
            dlg_ref[...] = jnp.zeros_like(dlg_ref)
            dlb_ref[...] = jnp.zeros_like(dlb_ref)

        ug, dug_du = _gelu_and_grad(u_ref[...].astype(F32))
        vg, dvg_dv = _gelu_and_grad(v_ref[...].astype(F32))
        vhat, rstd = _layer_norm(vg)
        vn = (vhat * lg_ref[...] + lb_ref[...]).astype(BF16)
        dya = dya_ref[...].astype(F32)
        for g in range(GROUPS):
            cols = slice(g * CHUNK, (g + 1) * CHUNK)
            ws = _tril_ws(ws_ref, g)
            mixed = _dot(ws, vn[:, cols]) + bst_ref[:, g:g + 1]
            du_ref[:, cols] = (dya[:, cols] * mixed * dug_du[:, cols]).astype(BF16)
            dmix = (dya[:, cols] * ug[:, cols]).astype(BF16)
            dbs_ref[g] += _dot(ones(), dmix, NT)
            dws_ref[g] += _dot(dmix, vn[:, cols], NT)
            dvn_ref[:, cols] = _dot(ws, dmix, TN)
        dvn = dvn_ref[...]
        dlg_ref[...] += jnp.sum(dvn * vhat, axis=0, keepdims=True)
        dlb_ref[...] += jnp.sum(dvn, axis=0, keepdims=True)
        dvh = dvn * lg_ref[...]
        dvg = rstd * (dvh - jnp.mean(dvh, axis=-1, keepdims=True)
                      - vhat * jnp.mean(dvh * vhat, axis=-1, keepdims=True))
        dv_ref[...] = (dvg * dvg_dv).astype(BF16)

        @pl.when(ci == pl.num_programs(0) - 1)
        def _():
            r = lax.broadcasted_iota(jnp.int32, (CHUNK, CHUNK), 0)
            c = lax.broadcasted_iota(jnp.int32, (CHUNK, CHUNK), 1)
            for g in range(GROUPS):
                dws_ref[g] = jnp.where(c <= r, dws_ref[g], 0.0)

        if n:
            pl.when(ci == pl.num_programs(0) - 1)(finish)

    out = jax.ShapeDtypeStruct((s, D), BF16)
    res = pl.pallas_call(
        body, name="gating_bwd", grid=(s // CHUNK,),
        in_specs=[_rows(CHUNK, D, 0), _rows(CHUNK, D, 1), _rows(CHUNK, D), _full((1, D)), _full((1, D)),
                  _full((GROUPS, CHUNK, CHUNK)), _full((CHUNK, GROUPS))] + [ANY] * n,
        out_specs=[_rows(CHUNK, D), _rows(CHUNK, D), _full((GROUPS, CHUNK, CHUNK)), _full((GROUPS, 8, CHUNK)),
                   _full((1, D)), _full((1, D))] + [ANY] * n,
        out_shape=[out, out, jax.ShapeDtypeStruct((GROUPS, CHUNK, CHUNK), F32),
                   jax.ShapeDtypeStruct((GROUPS, 8, CHUNK), F32),
                   jax.ShapeDtypeStruct((1, D), F32), jax.ShapeDtypeStruct((1, D), F32)] + _swap_shapes(swapping),
        scratch_shapes=[pltpu.VMEM((CHUNK, D), F32)] + (_swap_sems(n) if n else []),
        compiler_params=_params("arbitrary", communicates=bool(n)),
    )(z, z, dya, ln_g, ln_b, w_s, bs_t, *swapping)
    return (*res[:6], res[6:])


def attn_bwd(z, yb, dyb, lse, logc, ka, kb, scattering):
    s = z.shape[0]
    nq = s // ATT_T
    t = ATT_T
    grp = ATT_BWD_GROUP
    ngrp = HEADS // 2 // grp
    wide = 128 * grp
    qcol, kcol, vcol = 2 * D // wide, 3 * D // wide, 4 * D // wide
    scale = 1.0 / math.sqrt(HEAD_DIM)
    n = len(scattering)

    def body(*refs):
        q_ref, k_ref, v_ref, y_ref, dy_ref, lse_ref, lc_ref, ka_ref, kb_ref = refs[:9]
        dq_ref, dk_ref, dv_ref = refs[9 + n:12 + n]
        qa_s, qt_s, da_s, dt_s, dq_s, dkt_s, dvt_s = refs[12 + 2 * n:19 + 2 * n]
        gi, j = pl.program_id(0), pl.program_id(1)
        first, lane, ones = _head_masks()
        if n:
            send, finish = _scatter_phases(refs[9:9 + n], refs[12 + n:12 + 2 * n], *refs[19 + 2 * n:])
            pl.when((gi == 0) & (j == 0))(send)

        @pl.when(j == 0)
        def _():
            dq_s[...] = jnp.zeros_like(dq_s)
            for pr in range(grp):
                cols = slice(pr * 128, (pr + 1) * 128)
                for ib in range(nq):
                    rows = slice(ib * t, (ib + 1) * t)
                    q = q_ref[rows, cols].astype(F32) * scale
                    lse = lse_ref[rows, cols]
                    qa_s[pr, 0, ib] = jnp.where(first, q, _place3(lane, HEAD_DIM + 2 * AUG, _split3(-lse[:, 0:1]),
                                                                  ones(0, 2 * AUG))).astype(BF16)
                    qa_s[pr, 1, ib] = jnp.where(
                        first, _place3(lane, 2 * AUG, _split3(-lse[:, HEAD_DIM:HEAD_DIM + 1]), ones(1, 2 * AUG)),
                        q).astype(BF16)
                    qt_s[pr, ib, :, 0:t] = jnp.where(first, q, 0.0).T.astype(BF16)
                    qt_s[pr, ib, :, t:2 * t] = jnp.where(first, 0.0, q).T.astype(BF16)
                    do = dy_ref[rows, cols].astype(F32)
                    prod = do * y_ref[rows, cols].astype(F32)
                    dd0 = jnp.sum(jnp.where(first, prod, 0.0), axis=-1, keepdims=True)
                    dd1 = jnp.sum(jnp.where(first, 0.0, prod), axis=-1, keepdims=True)
                    da_s[pr, 0, ib] = jnp.where(first, do, _place3(lane, HEAD_DIM, _split3(-dd0), 0.0)).astype(BF16)
                    da_s[pr, 1, ib] = jnp.where(first, _place3(lane, 0, _split3(-dd1), 0.0), do).astype(BF16)
                    dt_s[pr, ib, :, 0:t] = jnp.where(first, do, 0.0).T.astype(BF16)
                    dt_s[pr, ib, :, t:2 * t] = jnp.where(first, 0.0, do).T.astype(BF16)

        keys = []
        for pr in range(grp):
            kj = k_ref[:, pr * 128:(pr + 1) * 128].astype(F32)
            vj = v_ref[:, pr * 128:(pr + 1) * 128].astype(F32)
            keys.append((
                jnp.where(first, kj, ka_ref[pr, 0] + kb_ref[pr, 0, pl.ds(j, 1), :]).astype(BF16),
                jnp.where(first, ka_ref[pr, 1] + kb_ref[pr, 1, pl.ds(j, 1), :], kj).astype(BF16),
                jnp.concatenate([jnp.where(first, kj, 0.0), jnp.where(first, 0.0, kj)], axis=0).astype(BF16),
                jnp.where(first, vj, ones(0, AUG)).astype(BF16),
                jnp.where(first, ones(1, AUG), vj).astype(BF16)))
        dkt_s[...] = jnp.zeros_like(dkt_s)
        dvt_s[...] = jnp.zeros_like(dvt_s)

        def step(i, _):
            lc = lc_ref[i - j]
            rows = pl.ds(pl.multiple_of(i * t, t), t)
            for pr in range(grp):
                k0a, k1a, kst, v0a, v1a = keys[pr]
                p0 = jnp.exp(_dot(qa_s[pr, 0, i], k0a, NT) + lc)
                p1 = jnp.exp(_dot(qa_s[pr, 1, i], k1a, NT) + lc)
                e0 = (p0 * _dot(da_s[pr, 0, i], v0a, NT)).astype(BF16)
                e1 = (p1 * _dot(da_s[pr, 1, i], v1a, NT)).astype(BF16)
                dq_s[pr, rows, :] += _dot(jnp.concatenate([e0, e1], axis=1), kst)
                dvt_s[pr] += _dot(dt_s[pr, i], jnp.concatenate([p0.astype(BF16), p1.astype(BF16)], axis=0))
                dkt_s[pr] += _dot(qt_s[pr, i], jnp.concatenate([e0, e1], axis=0))
            return 0

        lax.fori_loop(j, nq, step, 0)
        for pr in range(grp):
            dk_ref[:, pr * 128:(pr + 1) * 128] = dkt_s[pr].T.astype(BF16)
            dv_ref[:, pr * 128:(pr + 1) * 128] = dvt_s[pr].T.astype(BF16)

        @pl.when(j == nq - 1)
        def _():
            for pr in range(grp):
                dq_ref[:, pr * 128:(pr + 1) * 128] = (dq_s[pr] * scale).astype(BF16)

        if n:
            pl.when((gi == ngrp - 1) & (j == nq - 1))(finish)

    colblock = lambda c: pl.BlockSpec((s, wide), lambda g, j: (0, c + g))
    blk = lambda c: pl.BlockSpec((t, wide), lambda g, j: (j, c + g))
    out = jax.ShapeDtypeStruct((s, D), BF16)
    res = pl.pallas_call(
        body, name="attn_bwd", grid=(ngrp, nq),
        in_specs=[colblock(qcol), blk(kcol), blk(vcol), colblock(0), colblock(0), colblock(0),
                  _full((nq, t, t)), pl.BlockSpec((grp, 2, t, 128), lambda g, j: (g, 0, 0, 0)),
                  pl.BlockSpec((grp, 2, nq, 128), lambda g, j: (g, 0, 0, 0))] + [ANY] * n,
        out_specs=[colblock(0), blk(0), blk(0)] + [ANY] * n, out_shape=[out] * 3 + _scatter_shapes(scattering),
        scratch_shapes=[pltpu.VMEM((grp, 2, nq, t, 128), BF16), pltpu.VMEM((grp, nq, 128, 2 * t), BF16),
                        pltpu.VMEM((grp, 2, nq, t, 128), BF16), pltpu.VMEM((grp, nq, 128, 2 * t), BF16),
                        pltpu.VMEM((grp, s, 128), F32), pltpu.VMEM((grp, 128, t), F32),
                        pltpu.VMEM((grp, 128, t), F32)]
        + (_scatter_sems(n) if n else []),
        compiler_params=_params("arbitrary", "arbitrary", communicates=bool(n)),
    )(z, z, z, yb, dyb, lse, logc, ka, kb, *scattering)
    return res[0], res[1], res[2], res[3:]


def in_bwd_norm(dz, wg, x, dx1, g_pre, scattering):
    s = x.shape[0]
    tm = 512
    n = len(scattering)

    def body(*refs):
        dz_ref, w_ref, x_ref, dx1_ref, g_ref = refs[:5]
        dx_ref, dg_ref = refs[5 + n:7 + n]
        acc_ref = refs[7 + 2 * n]
        i, k = pl.program_id(0), pl.program_id(1)
        if n:
            send, finish = _scatter_phases(refs[5:5 + n], refs[7 + n:7 + 2 * n], *refs[8 + 2 * n:])
            pl.when((i == 0) & (k == 0))(send)

        @pl.when((i == 0) & (k == 0))
        def _():
            dg_ref[...] = jnp.zeros_like(dg_ref)

        part = _dot(dz_ref[...], w_ref[...], NT)

        @pl.when(k == 0)
        def _():
            acc_ref[...] = part

        @pl.when(k > 0)
        def _():
            acc_ref[...] += part

        @pl.when(k == N_CHIPS - 1)
        def _():
            dh = acc_ref[...]
            xhat, r = _rms(x_ref[...])
            dg_ref[...] += jnp.sum(dh * xhat, axis=0, keepdims=True)
            dx_ref[...] = dx1_ref[...] + _rms_bwd(dh * g_ref[...], xhat, r)

        if n:
            pl.when((i == s // tm - 1) & (k == N_CHIPS - 1))(finish)

    row = pl.BlockSpec((tm, D), lambda i, k: (i, 0))
    vec = pl.BlockSpec((1, D), lambda i, k: (0, 0))
    res = pl.pallas_call(
        body, name="in_bwd_norm", grid=(s // tm, N_CHIPS),
        in_specs=[pl.BlockSpec((tm, IN_SHARD), lambda i, k: (i, k)),
                  pl.BlockSpec((None, D, IN_SHARD), lambda i, k: (k, 0, 0)), row, row, vec] + [ANY] * n,
        out_specs=[row, vec] + [ANY] * n,
        out_shape=[jax.ShapeDtypeStruct((s, D), F32), jax.ShapeDtypeStruct((1, D), F32)]
        + _scatter_shapes(scattering),
        scratch_shapes=[pltpu.VMEM((tm, D), F32)] + (_scatter_sems(n) if n else []),
        compiler_params=_params("arbitrary", "arbitrary", communicates=bool(n)),
    )(dz, wg, x, dx1, g_pre, *scattering)
    return res[0], res[1], res[2:]


def _adamw_math(w, g, m, v):
    m = ADAM_B1 * m + (1.0 - ADAM_B1) * g
    v = ADAM_B2 * v + (1.0 - ADAM_B2) * (g * g)
    m_hat = m / (1.0 - ADAM_B1 ** ADAM_STEP)
    v_hat = v / (1.0 - ADAM_B2 ** ADAM_STEP)
    delta = -ADAM_LR * (m_hat / (jnp.sqrt(v_hat) + ADAM_EPS) + ADAM_WD * w)
    return delta, m, v


def adamw(name, w, g, m, v, tr):
    r, c = w.shape

    def body(w_ref, g_ref, m_ref, v_ref, go_ref, d_ref, nm_ref, nv_ref):
        g = g_ref[...]
        go_ref[...] = g
        d_ref[...], nm_ref[...], nv_ref[...] = _adamw_math(w_ref[...], g, m_ref[...], v_ref[...])

    out = jax.ShapeDtypeStruct((r, c), F32)
    return pl.pallas_call(
        body, name=name, grid=(r // tr,), in_specs=[_rows(tr, c)] * 4, out_specs=[_rows(tr, c)] * 4,
        out_shape=[out] * 4, compiler_params=_params("parallel"),
    )(w, g, m, v)


def add_halves(name, g, recv, c_idx, tr):
    n, h, c = recv.shape

    def body(c_ref, g_ref, r_ref, o_ref):
        o_ref[...] = (g_ref[...] + r_ref[...]).astype(BF16)

    nb = h // tr
    return pl.pallas_call(
        body, name=name,
        grid_spec=pltpu.PrefetchScalarGridSpec(
            num_scalar_prefetch=1, grid=(n, nb),
            in_specs=[pl.BlockSpec((None, tr, c), lambda k, i, c_ref: (k, c_ref[0] * nb + i, 0)),
                      pl.BlockSpec((None, tr, c), lambda k, i, c_ref: (k, i, 0))],
            out_specs=pl.BlockSpec((None, tr, c), lambda k, i, c_ref: (k, i, 0))),
        out_shape=jax.ShapeDtypeStruct((n, h, c), BF16), compiler_params=_params("parallel", "parallel"),
    )(c_idx, g, recv)


def sum_chips(name, parts, recv, where, tr):
    n, h, c = recv.shape
    nb = h // tr

    def body(w_ref, p_ref, r_ref, o_ref):
        acc = p_ref[...].astype(F32)
        for k in range(n):
            acc = acc + r_ref[k].astype(F32)
        o_ref[...] = acc

    return pl.pallas_call(
        body, name=name,
        grid_spec=pltpu.PrefetchScalarGridSpec(
            num_scalar_prefetch=1, grid=(nb,),
            in_specs=[pl.BlockSpec((None, tr, c), lambda i, w_ref: (w_ref[0], i, 0)),
                      pl.BlockSpec((n, tr, c), lambda i, w_ref: (0, i, 0))],
            out_specs=pl.BlockSpec((tr, c), lambda i, w_ref: (w_ref[1] * nb + i, 0))),
        out_shape=jax.ShapeDtypeStruct((2 * h, c), F32), compiler_params=_params("parallel"),
    )(where, parts, recv)


def place_shard(name, shard, where, dtype, tr):
    r, c = shard.shape

    def body(w_ref, s_ref, o_ref):
        o_ref[...] = s_ref[...].astype(dtype)

    return pl.pallas_call(
        body, name=name,
        grid_spec=pltpu.PrefetchScalarGridSpec(
            num_scalar_prefetch=1, grid=(r // tr,),
            in_specs=[pl.BlockSpec((tr, c), lambda i, w_ref: (i, 0))],
            out_specs=pl.BlockSpec((None, tr, c), lambda i, w_ref: (w_ref[0], i, 0))),
        out_shape=jax.ShapeDtypeStruct((N_CHIPS, r, c), dtype), compiler_params=_params("parallel"),
    )(where, shard)


ANY = pl.BlockSpec(memory_space=pl.ANY)


def _place():
    x, y, c = lax.axis_index("x"), lax.axis_index("y"), lax.axis_index("c")
    chips = [(1 - x, y), (x, 1 - y), (1 - x, 1 - y)]
    return x, y, c, chips


def gather_shards(arrays):
    n = len(arrays)

    def body(*refs):
        send, pass_on, finish = _gather_phases(refs[n:2 * n], *refs[2 * n:], _spans(arrays))
        send()
        pass_on()
        finish()

    return pl.pallas_call(
        body, name="gather_shards", in_specs=[ANY] * n, out_specs=[ANY] * n,
        out_shape=[jax.ShapeDtypeStruct(a.shape, a.dtype) for a in _arrays(arrays)],
        input_output_aliases={w: w for w in range(n)}, scratch_shapes=_gather_sems(n),
        compiler_params=pltpu.CompilerParams(has_side_effects=True),
    )(*arrays)


def _gather_sems(n):
    return [pltpu.SemaphoreType.DMA((6 * n,)), pltpu.SemaphoreType.DMA((6 * n,))]


class Span(typing.NamedTuple):
    array: jax.Array
    lo: int
    hi: int
    ways: tuple = (0, 1, 2)


def _arrays(gathering):
    return [g.array if isinstance(g, Span) else g for g in gathering]


def _spans(gathering):
    return [(g.lo, g.hi, g.ways) if isinstance(g, Span) else (0, g.shape[1], (0, 1, 2)) for g in gathering]


def _gather_phases(out, send_sems, recv_sems, spans):
    n = len(out)
    if not any(ways for _, _, ways in spans):
        return (lambda: None,) * 3
    x, y, c, chips = _place()
    me = 2 * x + y
    sibling = (x, y, 1 - c)

    def half(w, chip, core):
        lo, hi, _ = spans[w]
        h = (hi - lo) // 2
        return out[w].at[chip, pl.ds(lo + core * h, h)]

    def copy(k, block, to):
        return pltpu.make_async_remote_copy(src_ref=block, dst_ref=block, send_sem=send_sems.at[k],
                                            recv_sem=recv_sems.at[k], device_id=to, device_id_type=MESH)

    def over_ici(w, j, chip):
        return copy(3 * w + j, half(w, chip, c), (chips[j][0], chips[j][1], c))

    def over_d2d(w, j, core):
        return copy(3 * n + 3 * w + j, half(w, 2 * chips[j][0] + chips[j][1], core), sibling)

    pairs = [(w, j) for w in range(n) for j in spans[w][2]]

    def send():
        for w, j in pairs:
            over_ici(w, j, me).start()

    def pass_on():
        for w, j in pairs:
            over_ici(w, j, 2 * chips[j][0] + chips[j][1]).wait_recv()
            over_d2d(w, j, c).start()

    def finish():
        for w, j in pairs:
            over_d2d(w, j, 1 - c).wait_recv()
        for w, j in pairs:
            over_ici(w, j, me).wait_send()
            over_d2d(w, j, c).wait_send()

    return send, pass_on, finish


def _relay_sems():
    return [pltpu.SemaphoreType.DMA((4,)), pltpu.SemaphoreType.DMA((4,))]


def _relay_phases(out, send_sems, recv_sems):
    x, y, c, chips = _place()
    sibling = (x, y, 1 - c)
    rows = out.shape[1]
    quarter = rows // 4
    far = 2 * chips[2][0] + chips[2][1]

    def piece(chip, way, core):
        return out.at[chip, pl.ds(way * (rows // 2) + core * quarter, quarter)]

    def copy(k, block, to):
        return pltpu.make_async_remote_copy(src_ref=block, dst_ref=block, send_sem=send_sems.at[k],
                                            recv_sem=recv_sems.at[k], device_id=to, device_id_type=MESH)

    def over_ici(way, chip):
        return copy(way, piece(chip, way, c), (chips[way][0], chips[way][1], c))

    def over_d2d(way, core):
        return copy(2 + way, piece(far, way, core), sibling)

    def send():
        for way in range(2):
            other = chips[1 - way]
            over_ici(way, 2 * other[0] + other[1]).start()

    def pass_on():
        for way in range(2):
            over_ici(way, far).wait_recv()
            over_d2d(way, c).start()

    def finish():
        for way in range(2):
            over_d2d(way, 1 - c).wait_recv()
        for way in range(2):
            other = chips[1 - way]
            over_ici(way, 2 * other[0] + other[1]).wait_send()
            over_d2d(way, c).wait_send()

    return send, pass_on, finish


def swap_halves(name, grads):
    n = len(grads)

    def body(*refs):
        send, finish = _swap_phases(refs[:n], refs[n:2 * n], *refs[2 * n:])
        send()
        finish()

    return pl.pallas_call(
        body, name=name, in_specs=[ANY] * n, out_specs=[ANY] * n, out_shape=_swap_shapes(grads),
        scratch_shapes=_swap_sems(n), compiler_params=pltpu.CompilerParams(has_side_effects=True),
    )(*grads)


def _swap_shapes(grads):
    return [jax.ShapeDtypeStruct((a.shape[0], a.shape[1] // 2, a.shape[2]), a.dtype) for a in grads]


def _swap_sems(n):
    return [pltpu.SemaphoreType.DMA((n,)), pltpu.SemaphoreType.DMA((n,))]


def _swap_phases(g, out, send_sems, recv_sems):
    x, y, c, _ = _place()

    def copies():
        return [pltpu.make_async_remote_copy(
            src_ref=g[w].at[:, pl.ds((1 - c) * (g[w].shape[1] // 2), g[w].shape[1] // 2)], dst_ref=out[w],
            send_sem=send_sems.at[w], recv_sem=recv_sems.at[w], device_id=(x, y, 1 - c), device_id_type=MESH)
            for w in range(len(g))]

    def send():
        for cp in copies():
            cp.start()

    def finish():
        for cp in copies():
            cp.wait()

    return send, finish


def _send_phases(g, out, send_sems, recv_sems):
    x, y, c, _ = _place()

    def copies():
        return [pltpu.make_async_remote_copy(
            src_ref=g[w], dst_ref=out[w], send_sem=send_sems.at[w], recv_sem=recv_sems.at[w],
            device_id=(x, y, 1 - c), device_id_type=MESH) for w in range(len(g))]

    def send():
        for cp in copies():
            cp.start()

    def finish():
        for cp in copies():
            cp.wait()

    return send, finish


def dw_in_half(name, h, dz, which, sending):
    s = h.shape[0]
    hh, tb = D // 2, IN_SHARD // 2
    n = len(sending)
    steps = IN_COLS // tb

    def body(w_ref, *refs):
        a_ref, b_ref, o_ref = refs[0], refs[1], refs[2 + n]
        j = pl.program_id(0)
        if n:
            send, finish = _send_phases(refs[2:2 + n], refs[3 + n:3 + 2 * n], *refs[3 + 2 * n:])
            pl.when(j == 0)(send)
        o_ref[...] = _dot(a_ref[...], b_ref[...], TN)
        if n:
            pl.when(j == steps - 1)(finish)

    out = pl.pallas_call(
        body, name=name,
        grid_spec=pltpu.PrefetchScalarGridSpec(
            num_scalar_prefetch=1, grid=(steps,),
            in_specs=[pl.BlockSpec((s, hh), lambda j, w: (0, w[0])), pl.BlockSpec((s, tb), lambda j, w: (0, j))]
            + [ANY] * n,
            out_specs=[pl.BlockSpec((None, hh, tb), lambda j, w: (j // 2, 0, j % 2))] + [ANY] * n,
            scratch_shapes=_swap_sems(n) if n else []),
        out_shape=[jax.ShapeDtypeStruct((N_CHIPS, hh, IN_SHARD), F32)]
        + [jax.ShapeDtypeStruct(a.shape, a.dtype) for a in sending],
        compiler_params=_params("arbitrary", communicates=bool(n)),
    )(which, h, dz, *sending)
    return out[0], out[1:]


def scatter_chips(parts):
    n = len(parts)

    def body(*refs):
        send, finish = _scatter_phases(refs[:n], refs[n:2 * n], *refs[2 * n:])
        send()
        finish()

    return pl.pallas_call(
        body, name="scatter_chips", in_specs=[ANY] * n, out_specs=[ANY] * n,
        out_shape=_scatter_shapes(parts), scratch_shapes=_scatter_sems(n),
        compiler_params=pltpu.CompilerParams(has_side_effects=True),
    )(*parts)


def _scatter_shapes(parts):
    return [jax.ShapeDtypeStruct((3,) + a.shape[1:], a.dtype) for a in parts]


def _scatter_sems(n):
    return [pltpu.SemaphoreType.DMA((3 * n,)), pltpu.SemaphoreType.DMA((3 * n,))]


def _scatter_phases(p, out, send_sems, recv_sems):
    x, y, c, chips = _place()

    def copies():
        return [pltpu.make_async_remote_copy(
            src_ref=p[w].at[2 * px + py], dst_ref=out[w].at[j], send_sem=send_sems.at[3 * w + j],
            recv_sem=recv_sems.at[3 * w + j], device_id=(px, py, c), device_id_type=MESH)
            for w in range(len(p)) for j, (px, py) in enumerate(chips)]

    def send():
        for cp in copies():
            cp.start()

    def finish():
        for cp in copies():
            cp.wait()

    return send, finish


def join_halves(arrays):
    n = len(arrays)

    def body(*refs):
        out = refs[n:2 * n]
        send_sems, recv_sems = refs[2 * n:]
        x, y, c, _ = _place()

        def copy(w, core):
            h = out[w].shape[0] // 2
            rows = out[w].at[pl.ds(core * h, h)]
            return pltpu.make_async_remote_copy(
                src_ref=rows, dst_ref=rows, send_sem=send_sems.at[w], recv_sem=recv_sems.at[w],
                device_id=(x, y, 1 - c), device_id_type=MESH)

        for w in range(n):
            copy(w, c).start()
        for w in range(n):
            copy(w, 1 - c).wait_recv()
        for w in range(n):
            copy(w, c).wait_send()

    return pl.pallas_call(
        body, name="join_halves", in_specs=[ANY] * n, out_specs=[ANY] * n,
        out_shape=[jax.ShapeDtypeStruct(a.shape, a.dtype) for a in arrays],
        input_output_aliases={w: w for w in range(n)},
        scratch_shapes=[pltpu.SemaphoreType.DMA((n,)), pltpu.SemaphoreType.DMA((n,))],
        compiler_params=pltpu.CompilerParams(has_side_effects=True),
    )(*arrays)


def allreduce_small(packed):
    r, c = packed.shape
    n_dev = 8

    def body(x_ref, all_ref, sum_ref, send_sems, recv_sems, local_sem):
        x, y, cc, chips = _place()
        me, sibling = (x, y, cc), (x, y, 1 - cc)

        def rows(px, py, pc):
            return all_ref.at[4 * px + 2 * py + pc]

        def copy(k, block, to, src=None):
            return pltpu.make_async_remote_copy(
                src_ref=rows(*block) if src is None else src, dst_ref=rows(*block), send_sem=send_sems.at[k],
                recv_sem=recv_sems.at[k], device_id=to, device_id_type=MESH)

        mine = pltpu.make_async_copy(x_ref, rows(*me), local_sem)
        mine.start()
        first = [copy(0, me, sibling, src=x_ref)]
        first += [copy(1 + j, me, (*chip, cc), src=x_ref) for j, chip in enumerate(chips)]
        for cp in first:
            cp.start()
        passed = [copy(4 + j, (*chip, cc), sibling) for j, chip in enumerate(chips)]
        for j, chip in enumerate(chips):
            copy(1 + j, (*chip, cc), me).wait_recv()
            passed[j].start()
        copy(0, sibling, me).wait_recv()
        for j, chip in enumerate(chips):
            copy(4 + j, (*chip, 1 - cc), me).wait_recv()
        for cp in first + passed:
            cp.wait_send()
        mine.wait()
        acc = all_ref[0]
        for k in range(1, n_dev):
            acc = acc + all_ref[k]
        sum_ref[...] = acc

    vm = pl.BlockSpec(memory_space=pltpu.VMEM)
    return pl.pallas_call(
        body, name="allreduce_small", in_specs=[vm], out_specs=[vm, vm],
        out_shape=[jax.ShapeDtypeStruct((n_dev, r, c), F32), jax.ShapeDtypeStruct((r, c), F32)],
        scratch_shapes=[pltpu.SemaphoreType.DMA((7,)), pltpu.SemaphoreType.DMA((7,)), pltpu.SemaphoreType.DMA],
        compiler_params=pltpu.CompilerParams(has_side_effects=True, vmem_limit_bytes=VMEM_LIMIT),
    )(packed)[1]


def local_step(x, target, vecs, w_s, bs_t, bg, wg_in, late, core=None, order=None):
    on_mesh = core is not None

    def add(names, grads, recv):
        return [add_halves("add_" + n, g, r, core, min(r.shape[1], 256)) for n, g, r in zip(names, grads, recv)]

    g_pre, ln_g, ln_b, g_post, g_fpre, g_fpost = vecs
    s = x.shape[0]
    if order is None:
        order = jnp.arange(N_CHIPS, dtype=jnp.int32)
    logc = _attn_tables(s)
    ka, kb = _alibi_tables(s)

    h = norm_pre(x, g_pre)
    if not on_mesh:
        wg_a, wg_b, wg_out, wg_ff1, wg_ff2 = late
    if on_mesh:
        cut = D // 4
        z, (wg_in,), (wg_a, wg_b, wg_out, wg_ff1, wg_ff2) = mm_in(
            "mm_in_own", h, wg_in, order, 0, 1, None, [Span(wg_in, 0, D, (0, 1))], casting=late)
        z, (wg_in, wg_a), _ = mm_in("mm_in_near", h, wg_in, order, 1, 2, z, [Span(wg_in, 0, D, ()), wg_a],
                                    relay=True)
        z, (wg_in, wg_out, wg_b), _ = mm_in("mm_in_far", h, wg_in, order, 3, 1, z,
                                            [Span(wg_in, 0, D, ()), wg_out, wg_b])
        ya, (wg_ff2,) = gating_fwd(z, ln_g, ln_b, w_s, bs_t, [Span(wg_ff2, 0, cut)])
        yb, lse, (wg_ff1, wg_ff2, bg) = attn_fwd(z, logc, ka, kb, [wg_ff1, Span(wg_ff2, cut, D), bg])
        bg = jnp.transpose(bg[:, :2, :], (1, 0, 2)).reshape(2, D)
    else:
        z, _, _ = mm_in("mm_in", h, wg_in, order, 0, N_CHIPS, None, [Span(wg_in, 0, D, ())])
        ya, _ = gating_fwd(z, ln_g, ln_b, w_s, bs_t, [])
        yb, lse, _ = attn_fwd(z, logc, ka, kb, [])
    w_a, w_b, w_out, w_ff2 = wg_a.reshape(D, D), wg_b.reshape(D, D), wg_out.reshape(D, D), wg_ff2.reshape(D_FF, D)
    merged, pa, pb = proj_merge(ya, yb, w_a, w_b, z, bg)
    o, x1, h2 = out_norm(merged, w_out, x, g_post, g_fpre)
    a, rl = mm_ff1(h2, wg_ff1)
    dy, df, d_gfpost, loss = ff2_loss(rl, w_ff2, x1, target, g_fpost)

    half_cols = pl.BlockSpec((D, D // 2), lambda i, j: (0, j))
    d_wff2 = mm_tn("dw_ff2", rl, df, D // 2, D, (D_FF, D), pl.BlockSpec((D // 2, D), lambda i, j: (i, 0)))
    da = ff2_bwd(df, w_ff2, a)
    d_wff1 = mm_tn("dw_ff1", h2, da, D, D // 2, (N_CHIPS, D, D),
                   pl.BlockSpec((None, D, D // 2), lambda i, j: (j // 2, 0, j % 2)))
    d_ff = [d_wff1, d_wff2.reshape(N_CHIPS, D, D)]
    dx1, do, d_gfpre, d_gpost, recv_ff = ff1_bwd_norms(da, wg_ff1, x1, o, dy, g_fpre, g_post, d_ff if on_mesh else [])
    d_wout = mm_tn("dw_out", merged, do, D, D // 2, (D, D), half_cols)
    dpa, dpb, dga, dgb, d_bg = out_bwd_gates(do, w_out, pa, pb, z, bg)
    d_wa = mm_tn("dw_a", ya, dpa, D, D // 2, (D, D), half_cols)
    d_wb = mm_tn("dw_b", yb, dpb, D, D // 2, (D, D), half_cols)
    dya = mm_nt("dy_a", dpa, w_a)
    dyb = mm_nt("dy_b", dpb, w_b)
    d_proj = [d_wa.reshape(N_CHIPS, D // N_CHIPS, D), d_wb.reshape(N_CHIPS, D // N_CHIPS, D),
              d_wout.reshape(N_CHIPS, D // N_CHIPS, D)]
    du, dv, d_ws, d_bs, d_lng, d_lnb, recv_proj = gating_bwd(z, dya, ln_g, ln_b, w_s, bs_t, d_proj if on_mesh else [])
    early = d_proj + d_ff
    parts_early = add(BIG[1:], early, list(recv_proj) + list(recv_ff)) if on_mesh else []
    dq, dk, dvb, got_early = attn_bwd(z, yb, dyb, lse, logc, ka, kb, parts_early)
    dz = jnp.concatenate([du, dv, dq, dk, dvb, dga, dgb], axis=1)
    if on_mesh:
        for_sibling, _ = dw_in_half("dw_in_sibling", h, dz, 1 - core, [])
        mine, from_sibling = dw_in_half("dw_in_mine", h, dz, core, [for_sibling])
        d_win = None
        parts_late = [add_halves("add_w_in", mine, from_sibling[0], jnp.zeros((1,), jnp.int32), 256)]
    else:
        half = IN_SHARD // 2
        d_win = mm_tn("dw_in", h, dz, D, half, (N_CHIPS, D, IN_SHARD),
                      pl.BlockSpec((None, D, half), lambda i, j: (j // 2, 0, j % 2)))
        parts_late = []
    dx, d_gpre, got_late = in_bwd_norm(dz, wg_in, x, dx1, g_pre, parts_late)

    small = dict(norm_mix_pre=d_gpre, b_gate=d_bg, ln_v_g=d_lng, ln_v_b=d_lnb, w_s=d_ws, b_s=d_bs[:, 0, :],
                 norm_mix_post=d_gpost, norm_ffn_pre=d_gfpre, norm_ffn_post=d_gfpost)
    return loss[0, 0], dx, [d_win] + early, small, parts_late + parts_early, list(got_late) + list(got_early)


BIG = ("w_in", "w_a_proj", "w_b_proj", "w_out", "w_ff1", "w_ff2")
SMALL = ("norm_mix_pre", "ln_v_g", "ln_v_b", "b_s", "norm_mix_post", "norm_ffn_pre", "norm_ffn_post", "w_s", "b_gate")
ORDER = ("norm_mix_pre", "w_in", "b_gate", "ln_v_g", "ln_v_b", "w_s", "b_s", "w_a_proj", "w_b_proj", "w_out",
         "norm_mix_post", "norm_ffn_pre", "w_ff1", "w_ff2", "norm_ffn_post")
VEC_ROWS = D // 128
WS_ROW = 7 * VEC_ROWS
BG_ROW = WS_ROW + GROUPS * CHUNK
LOSS_ROW = BG_ROW + 2 * VEC_ROWS
PACK_ROWS = LOSS_ROW + 8


def pack_small(small, loss):
    vectors = [small[n] for n in SMALL[:7]]

    def body(*refs):
        out = refs[-1]
        ws_ref, bg_ref, loss_ref = refs[7:10]
        for i, n in enumerate(SMALL[:7]):
            if n == "b_s":
                out[i * VEC_ROWS:(i + 1) * VEC_ROWS, :] = refs[i][...]
            else:
                for j in range(VEC_ROWS):
                    out[i * VEC_ROWS + j:i * VEC_ROWS + j + 1, :] = refs[i][:, j * 128:(j + 1) * 128]
        for g in range(GROUPS):
            out[WS_ROW + g * CHUNK:WS_ROW + (g + 1) * CHUNK, :] = ws_ref[g]
        for r in range(2):
            for j in range(VEC_ROWS):
                row = BG_ROW + r * VEC_ROWS + j
                out[row:row + 1, :] = bg_ref[r:r + 1, j * 128:(j + 1) * 128]
        lane = lax.broadcasted_iota(jnp.int32, (8, 128), 1)
        sub = lax.broadcasted_iota(jnp.int32, (8, 128), 0)
        out[LOSS_ROW:LOSS_ROW + 8, :] = jnp.where((lane == 0) & (sub == 0), loss_ref[...], 0.0)

    return pl.pallas_call(
        body, name="pack_small", out_shape=jax.ShapeDtypeStruct((PACK_ROWS, 128), F32),
        compiler_params=_params(),
    )(*vectors, small["w_s"], small["b_gate"], loss)


def adamw_small(summed, chip, w, m, v):
    shapes = {n: (1, D) for n in SMALL}
    shapes.update(b_s=(GROUPS, CHUNK), w_s=(GROUPS * CHUNK, CHUNK), b_gate=(2, D // N_CHIPS))
    flat = lambda t: [t[n].reshape(shapes[n]) for n in SMALL]
    per = D // N_CHIPS // 128

    def body(chip_ref, sum_ref, *refs):
        params, outs = refs[:27], refs[27:]
        sub = lax.broadcasted_iota(jnp.int32, (VEC_ROWS, 128), 0)

        def gate_row(r):
            rows = sum_ref[BG_ROW + r * VEC_ROWS:BG_ROW + (r + 1) * VEC_ROWS, :]
            return jnp.concatenate([jnp.sum(jnp.where(sub == per * chip_ref[0] + j, rows, 0.0), axis=0, keepdims=True)
                                    for j in range(per)], axis=1)

        for i, n in enumerate(SMALL):
            if n == "b_s":
                g = sum_ref[i * VEC_ROWS:(i + 1) * VEC_ROWS, :]
            elif n == "w_s":
                g = sum_ref[WS_ROW:BG_ROW, :]
            elif n == "b_gate":
                g = jnp.concatenate([gate_row(0), gate_row(1)], axis=0)
            else:
                g = jnp.concatenate([sum_ref[i * VEC_ROWS + j:i * VEC_ROWS + j + 1, :] for j in range(VEC_ROWS)],
                                    axis=1)
            delta, nm, nv = _adamw_math(params[i][...], g, params[9 + i][...], params[18 + i][...])
            outs[4 * i][...], outs[4 * i + 1][...], outs[4 * i + 2][...], outs[4 * i + 3][...] = g, delta, nm, nv

    vm = pl.BlockSpec(memory_space=pltpu.VMEM)
    res = pl.pallas_call(
        body, name="adamw_small",
        in_specs=[pl.BlockSpec(memory_space=pltpu.SMEM)] + [vm] * 28, out_specs=[vm] * 36,
        out_shape=[jax.ShapeDtypeStruct(shapes[n], F32) for n in SMALL for _ in range(4)],
        compiler_params=_params(),
    )(chip, summed, *flat(w), *flat(m), *flat(v))
    return {n: tuple(r.reshape(w[n].shape) for r in res[4 * i:4 * i + 4]) for i, n in enumerate(SMALL)}


def kernel(x, norm_mix_pre, w_in, b_gate, ln_v_g, ln_v_b, w_s, b_s, w_a_proj, w_b_proj, w_out, norm_mix_post, norm_ffn_pre, w_ff1, w_ff2, norm_ffn_post, loss_target, m_norm_mix_pre, m_w_in, m_b_gate, m_ln_v_g, m_ln_v_b, m_w_s, m_b_s, m_w_a_proj, m_w_b_proj, m_w_out, m_norm_mix_post, m_norm_ffn_pre, m_w_ff1, m_w_ff2, m_norm_ffn_post, v_norm_mix_pre, v_w_in, v_b_gate, v_ln_v_g, v_ln_v_b, v_w_s, v_b_s, v_w_a_proj, v_w_b_proj, v_w_out, v_norm_mix_post, v_norm_ffn_pre, v_w_ff1, v_w_ff2, v_norm_ffn_post):
    w = dict(norm_mix_pre=norm_mix_pre, w_in=w_in, b_gate=b_gate, ln_v_g=ln_v_g, ln_v_b=ln_v_b, w_s=w_s, b_s=b_s,
             w_a_proj=w_a_proj, w_b_proj=w_b_proj, w_out=w_out, norm_mix_post=norm_mix_post,
             norm_ffn_pre=norm_ffn_pre, w_ff1=w_ff1, w_ff2=w_ff2, norm_ffn_post=norm_ffn_post)
    m = dict(norm_mix_pre=m_norm_mix_pre, w_in=m_w_in, b_gate=m_b_gate, ln_v_g=m_ln_v_g, ln_v_b=m_ln_v_b, w_s=m_w_s,
             b_s=m_b_s, w_a_proj=m_w_a_proj, w_b_proj=m_w_b_proj, w_out=m_w_out, norm_mix_post=m_norm_mix_post,
             norm_ffn_pre=m_norm_ffn_pre, w_ff1=m_w_ff1, w_ff2=m_w_ff2, norm_ffn_post=m_norm_ffn_post)
    v = dict(norm_mix_pre=v_norm_mix_pre, w_in=v_w_in, b_gate=v_b_gate, ln_v_g=v_ln_v_g, ln_v_b=v_ln_v_b, w_s=v_w_s,
             b_s=v_b_s, w_a_proj=v_w_a_proj, w_b_proj=v_w_b_proj, w_out=v_w_out, norm_mix_post=v_norm_mix_post,
             norm_ffn_pre=v_norm_ffn_pre, w_ff1=v_w_ff1, w_ff2=v_w_ff2, norm_ffn_post=v_norm_ffn_post)
    chip = 2 * lax.axis_index("x") + lax.axis_index("y")
    core = lax.axis_index("c")

    where = jnp.stack([chip, core]).astype(jnp.int32)
    wg_in = place_shard("place_w_in", w_in[0], where, BF16, 256)
    bg_all = place_shard("place_b_gate", jnp.pad(b_gate[0], ((0, 14), (0, 0))), where, F32, 16)
    vecs = (norm_mix_pre, ln_v_g, ln_v_b, norm_mix_post, norm_ffn_pre, norm_ffn_post)
    loss, dx, _, small, parts, got = local_step(
        x[0], loss_target[0], vecs, w_s[0], b_s[0].T, bg_all, wg_in, [w[n][0] for n in BIG[1:]],
        core=jnp.reshape(core, (1,)).astype(jnp.int32),
        order=jnp.stack([chip, chip ^ 2, chip ^ 1, chip ^ 3]).astype(jnp.int32))

    halves = [sum_chips("sum_" + n, p, r, where, min(p.shape[1], 256)) for n, p, r in zip(BIG, parts, got)]
    grads = dict(zip(BIG, join_halves(halves)))

    summed = allreduce_small(pack_small(small, loss.reshape(1, 1)))
    loss = summed[LOSS_ROW, 0]

    new = adamw_small(summed, jnp.reshape(chip, (1,)).astype(jnp.int32), w, m, v)
    for n in BIG:
        shape = w[n].shape
        res = adamw("adamw_" + n, w[n][0], grads[n], m[n][0], v[n][0], min(shape[1], 256))
        new[n] = tuple(r.reshape(shape) for r in res)

    outs = [loss, dx[None]]
    for i in range(4):
        outs += [new[n][i] for n in ORDER]
    return tuple(outs)
```

```python
import functools
import math
import typing

import numpy as np
import jax
import jax.numpy as jnp
from jax import lax
from jax.experimental import pallas as pl
from jax.experimental.pallas import tpu as pltpu

F32 = jnp.float32
BF16 = jnp.bfloat16
MESH = pl.DeviceIdType.MESH

D = 1024
EPS = 1e-6
CHUNK = 128
GROUPS = 8
HEADS = 16
HEAD_DIM = 64
ATT_T = 256
ATT_GROUP = 4
ATT_BWD_GROUP = 2
N_CHIPS = 4
D_FF = 4 * D
IN_COLS = 7 * D
IN_SHARD = IN_COLS // N_CHIPS
MASKED = -1e30
VMEM_LIMIT = 56 * 2 ** 20

ADAM_LR, ADAM_B1, ADAM_B2, ADAM_EPS, ADAM_WD, ADAM_STEP = 0.001, 0.9, 0.999, 1e-08, 0.01, 10

NN = (((1,), (0,)), ((), ()))
NT = (((1,), (1,)), ((), ()))
TN = (((0,), (0,)), ((), ()))


def _dot(a, b, dims=NN):
    return lax.dot_general(a, b, dims, preferred_element_type=F32)


def _params(*sem, communicates=False):
    return pltpu.CompilerParams(dimension_semantics=sem or None, vmem_limit_bytes=VMEM_LIMIT,
                                has_side_effects=communicates)


def _rows(tr, c, col=0):
    return pl.BlockSpec((tr, c), lambda i: (i, col))


def _full(shape):
    n = len(shape)
    return pl.BlockSpec(shape, lambda *_: (0,) * n)


def _gelu(x):
    k = math.sqrt(2.0 / math.pi)
    return 0.5 * x * (1.0 + jnp.tanh(k * (x + 0.044715 * x * x * x)))


def _gelu_and_grad(x):
    k = math.sqrt(2.0 / math.pi)
    t = jnp.tanh(k * (x + 0.044715 * x * x * x))
    g = 0.5 * x * (1.0 + t)
    dg = 0.5 * (1.0 + t) + 0.5 * x * (1.0 - t * t) * (k * (1.0 + 3.0 * 0.044715 * x * x))
    return g, dg


def _sigmoid(x):
    return 1.0 / (1.0 + jnp.exp(-x))


def _rms(x):
    r = lax.rsqrt(jnp.mean(x * x, axis=-1, keepdims=True) + EPS)
    return x * r, r


def _rms_bwd(dn, xhat, r):
    return r * (dn - xhat * jnp.mean(dn * xhat, axis=-1, keepdims=True))


def norm_pre(x, g):
    s = x.shape[0]
    tr = 512

    def body(x_ref, g_ref, h_ref):
        xhat, _ = _rms(x_ref[...])
        h_ref[...] = (xhat * g_ref[...]).astype(BF16)

    return pl.pallas_call(
        body, name="norm_pre", grid=(s // tr,),
        in_specs=[_rows(tr, D), _full((1, D))], out_specs=_rows(tr, D),
        out_shape=jax.ShapeDtypeStruct((s, D), BF16), compiler_params=_params("parallel"),
    )(x, g)


def mm_in(name, h, wg, order, first, count, z, gathering, relay=False, casting=()):
    s = h.shape[0]
    tm, tn = 1024, IN_SHARD // 2
    per = IN_SHARD // tn
    n, m = len(gathering), len(casting)
    nj, ni = count * per, s // tm
    has_z = z is not None
    arrays = _arrays(gathering)
    at = [k for k, a in enumerate(arrays) if a is wg][0]

    def body(order_ref, *refs):
        a_ref = refs[0]
        cast_in = refs[1 + has_z + n:1 + has_z + n + m]
        o_ref = refs[1 + has_z + n + m]
        held = refs[2 + has_z + n + m:2 + has_z + 2 * n + m]
        cast_out = refs[2 + has_z + 2 * n + m:2 + has_z + 2 * n + 2 * m]
        tile, tile_sem = refs[2 + has_z + 2 * n + 2 * m:4 + has_z + 2 * n + 2 * m]
        j, i = pl.program_id(0), pl.program_id(1)
        sems = refs[4 + has_z + 2 * n + 2 * m:]
        for src, dst in zip(cast_in, cast_out):
            dst[...] = src[...].astype(BF16)
        phases = [_gather_phases(held, *sems[:2], _spans(gathering))]
        if relay:
            phases.append(_relay_phases(held[at], *sems[2:]))
        def each(fs):
            def run():
                for f in fs:
                    f()
            return run

        send, pass_on, finish = [each(fs) for fs in zip(*phases)]

        def fetch(t):
            chip = order_ref[first + t // per]
            return pltpu.make_async_copy(held[at].at[chip, :, pl.ds((t % per) * tn, tn)], tile.at[t % 2],
                                         tile_sem.at[t % 2])

        @pl.when(i == 0)
        def _():
            @pl.when(j == 0)
            def _():
                send()
                fetch(0).start()

            fetch(j).wait()

            @pl.when(j + 1 < nj)
            def _():
                fetch(j + 1).start()

        pl.when((j == nj - 1) & (i == 0))(pass_on)
        rows = pl.ds(pl.multiple_of(i * tm, tm), tm)
        o_ref[...] = _dot(a_ref[rows, :], tile[j % 2]).astype(BF16)
        pl.when((j == nj - 1) & (i == ni - 1))(finish)

    steps = nj * ni
    out = pl.pallas_call(
        body, name=name,
        grid_spec=pltpu.PrefetchScalarGridSpec(
            num_scalar_prefetch=1, grid=(nj, ni),
            in_specs=[pl.BlockSpec((s, D), lambda j, i, o: (0, 0))] + [ANY] * (has_z + n)
            + [pl.BlockSpec((a.shape[0] // steps, a.shape[1]), lambda j, i, o: (j * ni + i, 0)) for a in casting],
            out_specs=[pl.BlockSpec((tm, tn), lambda j, i, o: (i, o[first + j // per] * per + j % per))] + [ANY] * n
            + [pl.BlockSpec((None, a.shape[0] // steps, a.shape[1]), lambda j, i, o: (o[0], j * ni + i, 0))
               for a in casting],
            scratch_shapes=[pltpu.VMEM((2, D, tn), BF16), pltpu.SemaphoreType.DMA((2,))] + _gather_sems(n)
            + (_relay_sems() if relay else [])),
        out_shape=[jax.ShapeDtypeStruct((s, IN_COLS), BF16)] + [jax.ShapeDtypeStruct(a.shape, a.dtype) for a in arrays]
        + [jax.ShapeDtypeStruct((N_CHIPS,) + a.shape, BF16) for a in casting],
        input_output_aliases={**({2: 0} if has_z else {}), **{2 + has_z + w: 1 + w for w in range(n)}},
        compiler_params=_params("arbitrary", "arbitrary", communicates=True),
    )(order, h, *([z] if has_z else []), *arrays, *casting)
    return out[0], out[1:1 + n], out[1 + n:]


def _tril_ws(ws_ref, g):
    r = lax.broadcasted_iota(jnp.int32, (CHUNK, CHUNK), 0)
    c = lax.broadcasted_iota(jnp.int32, (CHUNK, CHUNK), 1)
    return jnp.where(c <= r, ws_ref[g], 0.0).astype(BF16)


def _layer_norm(v):
    mu = jnp.mean(v, axis=-1, keepdims=True)
    d = v - mu
    rstd = lax.rsqrt(jnp.mean(d * d, axis=-1, keepdims=True) + EPS)
    return d * rstd, rstd


def gating_fwd(z, ln_g, ln_b, w_s, bs_t, gathering):
    s = z.shape[0]
    n = len(gathering)
    steps = s // CHUNK

    def body(*refs):
        u_ref, v_ref, lg_ref, lb_ref, ws_ref, bst_ref = refs[:6]
        ya_ref = refs[6 + n]
        ci = pl.program_id(0)
        if n:
            send, pass_on, finish = _gather_phases(refs[7 + n:7 + 2 * n], *refs[7 + 2 * n:], _spans(gathering))
            pl.when(ci == 0)(send)
            pl.when(ci == steps * 3 // 4)(pass_on)
        ug = _gelu(u_ref[...].astype(F32))
        vhat, _ = _layer_norm(_gelu(v_ref[...].astype(F32)))
        vn = (vhat * lg_ref[...] + lb_ref[...]).astype(BF16)
        for g in range(GROUPS):
            cols = slice(g * CHUNK, (g + 1) * CHUNK)
            mixed = _dot(_tril_ws(ws_ref, g), vn[:, cols]) + bst_ref[:, g:g + 1]
            ya_ref[:, cols] = (ug[:, cols] * mixed).astype(BF16)
        if n:
            pl.when(ci == steps - 1)(finish)

    out = pl.pallas_call(
        body, name="gating_fwd", grid=(steps,),
        in_specs=[_rows(CHUNK, D, 0), _rows(CHUNK, D, 1), _full((1, D)), _full((1, D)),
                  _full((GROUPS, CHUNK, CHUNK)), _full((CHUNK, GROUPS))] + [ANY] * n,
        out_specs=[_rows(CHUNK, D)] + [ANY] * n,
        out_shape=[jax.ShapeDtypeStruct((s, D), BF16)]
        + [jax.ShapeDtypeStruct(a.shape, a.dtype) for a in _arrays(gathering)],
        input_output_aliases={6 + w: 1 + w for w in range(n)},
        scratch_shapes=_gather_sems(n) if n else [],
        compiler_params=_params("arbitrary", communicates=bool(n)),
    )(z, z, ln_g, ln_b, w_s, bs_t, *_arrays(gathering))
    return out[0], out[1:]


def _attn_tables(s):
    nd = s // ATT_T
    r = np.arange(ATT_T)[None, :, None]
    c = np.arange(ATT_T)[None, None, :]
    delta = np.arange(nd)[:, None, None] * ATT_T + r - c
    count = np.zeros(delta.shape, np.int64)
    for window, dilation in ((128, 1), (512, 4), (2048, 16)):
        count += (delta >= 0) & (delta % dilation == 0) & (delta <= window)
    logc = np.where(count > 0, np.log(np.maximum(count, 1)), MASKED)
    return jnp.asarray(logc, F32)


AUG = 3


def _split3_np(x):
    terms, rest = [], np.asarray(x, np.float64)
    for _ in range(AUG):
        term = np.asarray(rest.astype(jnp.bfloat16), np.float64)
        terms.append(term)
        rest = rest - term
    return terms


def _split3(x):
    terms, rest = [], x
    for _ in range(AUG):
        term = rest.astype(BF16).astype(F32)
        terms.append(term)
        rest = rest - term
    return terms


def _alibi_tables(s):
    nb = s // ATT_T
    slopes = np.exp2(-8.0 * np.arange(1, HEADS + 1, dtype=np.float64) / HEADS)
    ka = np.zeros((HEADS // 2, 2, ATT_T, 128), np.float32)
    kb = np.zeros((HEADS // 2, 2, nb, 128), np.float32)
    for p in range(HEADS // 2):
        for e in range(2):
            base = HEAD_DIM * (1 - e)
            for a, term in enumerate(_split3_np(slopes[2 * p + e] * np.arange(ATT_T))):
                ka[p, e, :, base + a] = term
            for a, term in enumerate(_split3_np(slopes[2 * p + e] * ATT_T * np.arange(nb))):
                kb[p, e, :, base + AUG + a] = term
            ka[p, e, :, base + 2 * AUG:base + 3 * AUG] = 1.0
    return jnp.asarray(ka), jnp.asarray(kb)


def _head_masks():
    lane = lax.broadcasted_iota(jnp.int32, (1, 128), 1)
    first = lane < HEAD_DIM

    def ones(e, n):
        base = HEAD_DIM * (1 - e)
        return ((lane >= base) & (lane < base + n)).astype(F32)

    return first, lane, ones


def _place3(lane, at, terms, other):
    for a, term in enumerate(terms):
        other = jnp.where(lane == at + a, term, other)
    return other


def attn_fwd(z, logc, ka, kb, gathering):
    s = z.shape[0]
    nq = s // ATT_T
    t = ATT_T
    n = len(gathering)
    grp = ATT_GROUP
    ngrp = HEADS // 2 // grp
    wide = 128 * grp
    qcol, kcol, vcol = 2 * D // wide, 3 * D // wide, 4 * D // wide

    def body(*refs):
        q_ref, k_ref, v_ref, lc_ref, ka_ref, kb_ref = refs[:6]
        y_ref, lse_ref = refs[6 + n:8 + n]
        q_s, k_s, v_s, m_s, l_s, acc_s = refs[8 + 2 * n:14 + 2 * n]
        gi, qi = pl.program_id(0), pl.program_id(1)
        first, lane, ones = _head_masks()
        if n:
            send, pass_on, finish = _gather_phases(refs[8 + n:8 + 2 * n], *refs[14 + 2 * n:], _spans(gathering))
            pl.when((gi == 0) & (qi == 0))(send)
            pl.when((gi == ngrp - 1) & (qi == nq * 3 // 4))(pass_on)

        @pl.when(qi == 0)
        def _():
            sel = jnp.broadcast_to(first.astype(F32), (t, 128))
            for pr in range(grp):
                cols = slice(pr * 128, (pr + 1) * 128)
                for jb in range(nq):
                    kj = k_ref[jb * t:(jb + 1) * t, cols].astype(F32)
                    vj = v_ref[jb * t:(jb + 1) * t, cols].astype(F32)
                    k_s[pr, 0, jb] = jnp.where(first, kj, ka_ref[pr, 0] + kb_ref[pr, 0, jb:jb + 1, :]).astype(BF16)
                    k_s[pr, 1, jb] = jnp.where(first, ka_ref[pr, 1] + kb_ref[pr, 1, jb:jb + 1, :], kj).astype(BF16)
                    v_s[pr, jb, 0:t, 0:128] = jnp.where(first, vj, 0.0).astype(BF16)
                    v_s[pr, jb, t:2 * t, 0:128] = jnp.where(first, 0.0, vj).astype(BF16)
                    v_s[pr, jb, 0:t, 128:256] = sel.astype(BF16)
                    v_s[pr, jb, t:2 * t, 128:256] = (1.0 - sel).astype(BF16)

        for pr in range(grp):
            q = q_ref[:, pr * 128:(pr + 1) * 128].astype(F32) * (1.0 / math.sqrt(HEAD_DIM))
            q_s[pr, 0] = jnp.where(first, q, ones(0, 2 * AUG)).astype(BF16)
            q_s[pr, 1] = jnp.where(first, ones(1, 2 * AUG), q).astype(BF16)
        m_s[...] = jnp.full_like(m_s, MASKED)
        l_s[...] = jnp.zeros_like(l_s)
        acc_s[...] = jnp.zeros_like(acc_s)

        def scores(j):
            return tuple(_dot(q_s[pr, e], k_s[pr, e, j], NT) for pr in range(grp) for e in range(2))

        def step(j, carry):
            softmax_block(j, scores(j))
            return carry

        def softmax_block(j, u):
            lc = lc_ref[qi - j]
            for pr in range(grp):
                u0 = u[2 * pr] + lc
                u1 = u[2 * pr + 1] + lc
                m0, m1 = m_s[pr, 0], m_s[pr, 1]
                n0 = jnp.maximum(m0, jnp.max(u0, axis=-1, keepdims=True))
                n1 = jnp.maximum(m1, jnp.max(u1, axis=-1, keepdims=True))
                m_s[pr, 0], m_s[pr, 1] = n0, n1
                p = jnp.concatenate([jnp.exp(u0 - jnp.concatenate([n0, n0], axis=1)).astype(BF16),
                                     jnp.exp(u1 - jnp.concatenate([n1, n1], axis=1)).astype(BF16)], axis=1)
                pv = _dot(p, v_s[pr, j])
                alpha = jnp.where(first, jnp.exp(m0 - n0), jnp.exp(m1 - n1))
                acc_s[pr] = acc_s[pr] * alpha + pv[:, 0:128]
                l_s[pr] = l_s[pr] * alpha + pv[:, 128:256]

        lax.fori_loop(0, qi + 1, step, 0)
        for pr in range(grp):
            cols = slice(pr * 128, (pr + 1) * 128)
            y_ref[:, cols] = (acc_s[pr] / l_s[pr]).astype(BF16)
            lse_ref[:, cols] = jnp.where(first, m_s[pr, 0], m_s[pr, 1]) + jnp.log(l_s[pr])
        if n:
            pl.when((gi == ngrp - 1) & (qi == nq - 1))(finish)

    out = pl.pallas_call(
        body, name="attn_fwd", grid=(ngrp, nq),
        in_specs=[pl.BlockSpec((t, wide), lambda g, i: (i, qcol + g)),
                  pl.BlockSpec((s, wide), lambda g, i: (0, kcol + g)),
                  pl.BlockSpec((s, wide), lambda g, i: (0, vcol + g)),
                  _full((nq, t, t)),
                  pl.BlockSpec((grp, 2, t, 128), lambda g, i: (g, 0, 0, 0)),
                  pl.BlockSpec((grp, 2, nq, 128), lambda g, i: (g, 0, 0, 0))] + [ANY] * n,
        out_specs=[pl.BlockSpec((t, wide), lambda g, i: (i, g)), pl.BlockSpec((t, wide), lambda g, i: (i, g))]
        + [ANY] * n,
        out_shape=[jax.ShapeDtypeStruct((s, D), BF16), jax.ShapeDtypeStruct((s, D), F32)]
        + [jax.ShapeDtypeStruct(a.shape, a.dtype) for a in _arrays(gathering)],
        input_output_aliases={6 + w: 2 + w for w in range(n)},
        scratch_shapes=[pltpu.VMEM((grp, 2, t, 128), BF16), pltpu.VMEM((grp, 2, nq, t, 128), BF16),
                        pltpu.VMEM((grp, nq, 2 * t, 256), BF16), pltpu.VMEM((grp, 2, t, 128), F32),
                        pltpu.VMEM((grp, t, 128), F32), pltpu.VMEM((grp, t, 128), F32)]
        + (_gather_sems(n) if n else []),
        compiler_params=_params("arbitrary", "arbitrary", communicates=bool(n)),
    )(z, z, z, logc, ka, kb, *_arrays(gathering))
    return out[0], out[1], out[2:]


def proj_merge(ya, yb, wa, wb, z, bg, gathering):
    s = ya.shape[0]
    tm = 512
    n = len(gathering)
    steps = s // tm

    def body(*refs):
        ya_ref, yb_ref, wa_ref, wb_ref, ga_ref, gb_ref, bg_ref = refs[:7]
        mg_ref, pa_ref, pb_ref = refs[7 + n:10 + n]
        i = pl.program_id(0)
        if n:
            send, pass_on, finish = _gather_phases(refs[10 + n:10 + 2 * n], *refs[10 + 2 * n:], _spans(gathering))
            pl.when(i == 0)(send)
            pl.when(i == steps - 1)(pass_on)
        pa = _dot(ya_ref[...], wa_ref[...])
        pb = _dot(yb_ref[...], wb_ref[...])
        sa = _sigmoid(ga_ref[...] + bg_ref[0:1, :])
        sb = _sigmoid(gb_ref[...] + bg_ref[1:2, :])
        mg_ref[...] = (sa * pa + sb * pb).astype(BF16)
        pa_ref[...] = pa.astype(BF16)
        pb_ref[...] = pb.astype(BF16)
        if n:
            pl.when(i == steps - 1)(finish)

    out = jax.ShapeDtypeStruct((s, D), BF16)
    res = pl.pallas_call(
        body, name="proj_merge", grid=(steps,),
        in_specs=[_rows(tm, D), _rows(tm, D), _full((D, D)), _full((D, D)),
                  _rows(tm, D, 5), _rows(tm, D, 6), _full((2, D))] + [ANY] * n,
        out_specs=[_rows(tm, D)] * 3 + [ANY] * n,
        out_shape=[out] * 3 + [jax.ShapeDtypeStruct(a.shape, a.dtype) for a in _arrays(gathering)],
        input_output_aliases={7 + w: 3 + w for w in range(n)},
        scratch_shapes=_gather_sems(n) if n else [],
        compiler_params=_params("arbitrary", communicates=bool(n)),
    )(ya, yb, wa, wb, z, z, bg, *_arrays(gathering))
    return res[0], res[1], res[2], res[3:]


def out_norm(merged, w_out, x, g_post, g_fpre):
    s = x.shape[0]
    tm = 512

    def body(mg_ref, w_ref, x_ref, gp_ref, gf_ref, o_ref, x1_ref, h2_ref):
        o = _dot(mg_ref[...], w_ref[...])
        ohat, _ = _rms(o)
        x1 = x_ref[...] + ohat * gp_ref[...]
        x1hat, _ = _rms(x1)
        o_ref[...] = o
        x1_ref[...] = x1
        h2_ref[...] = (x1hat * gf_ref[...]).astype(BF16)

    return pl.pallas_call(
        body, name="out_norm", grid=(s // tm,),
        in_specs=[_rows(tm, D), _full((D, D)), _rows(tm, D), _full((1, D)), _full((1, D))],
        out_specs=[_rows(tm, D)] * 3,
        out_shape=[jax.ShapeDtypeStruct((s, D), F32), jax.ShapeDtypeStruct((s, D), F32),
                   jax.ShapeDtypeStruct((s, D), BF16)],
        compiler_params=_params("parallel"),
    )(merged, w_out, x, g_post, g_fpre)


def mm_ff1(h2, wg, gathering):
    s = h2.shape[0]
    tm = 1024
    n = len(gathering)
    ni = s // tm

    def body(*refs):
        a_ref, b_ref = refs[:2]
        o_ref, r_ref = refs[2 + n:4 + n]
        i, j = pl.program_id(0), pl.program_id(1)
        if n:
            send, pass_on, finish = _gather_phases(refs[4 + n:4 + 2 * n], *refs[4 + 2 * n:], _spans(gathering))
            pl.when((i == 0) & (j == 0))(send)
            pl.when((i == ni - 1) & (j == N_CHIPS // 2))(pass_on)
        a = _dot(a_ref[...], b_ref[...])
        o_ref[...] = a.astype(BF16)
        r = jnp.maximum(a, 0.0)
        r_ref[...] = (r * r).astype(BF16)
        if n:
            pl.when((i == ni - 1) & (j == N_CHIPS - 1))(finish)

    res = pl.pallas_call(
        body, name="mm_ff1", grid=(ni, N_CHIPS),
        in_specs=[pl.BlockSpec((tm, D), lambda i, j: (i, 0)), pl.BlockSpec((None, D, D), lambda i, j: (j, 0, 0))]
        + [ANY] * n,
        out_specs=[pl.BlockSpec((tm, D), lambda i, j: (i, j))] * 2 + [ANY] * n,
        out_shape=[jax.ShapeDtypeStruct((s, D_FF), BF16), jax.ShapeDtypeStruct((s, D_FF), BF16)]
        + [jax.ShapeDtypeStruct(a.shape, a.dtype) for a in _arrays(gathering)],
        input_output_aliases={2 + w: 2 + w for w in range(n)},
        scratch_shapes=_gather_sems(n) if n else [],
        compiler_params=_params("arbitrary", "arbitrary", communicates=bool(n)),
    )(h2, wg, *_arrays(gathering))
    return res[0], res[1], res[2:]


def ff2_loss(rl, w_ff2, x1, target, g_fpost):
    s = x1.shape[0]
    tm = 256

    def body(rl_ref, w_ref, x1_ref, t_ref, g_ref, dy_ref, df_ref, dg_ref, loss_ref):
        @pl.when(pl.program_id(0) == 0)
        def _():
            dg_ref[...] = jnp.zeros_like(dg_ref)
            loss_ref[...] = jnp.zeros_like(loss_ref)

        f = _dot(rl_ref[...], w_ref[...])
        fhat, r = _rms(f)
        err = x1_ref[...] + fhat * g_ref[...] - t_ref[...]
        loss_ref[...] += 0.5 * jnp.sum(jnp.mean(err * err, axis=-1, keepdims=True), axis=0, keepdims=True)
        dy = err * (1.0 / D)
        dy_ref[...] = dy
        dg_ref[...] += jnp.sum(dy * fhat, axis=0, keepdims=True)
        df_ref[...] = _rms_bwd(dy * g_ref[...], fhat, r).astype(BF16)

    return pl.pallas_call(
        body, name="ff2_loss", grid=(s // tm,),
        in_specs=[_rows(tm, D_FF), _full((D_FF, D)), _rows(tm, D), _rows(tm, D), _full((1, D))],
        out_specs=[_rows(tm, D), _rows(tm, D), _full((1, D)), _full((1, 1))],
        out_shape=[jax.ShapeDtypeStruct((s, D), F32), jax.ShapeDtypeStruct((s, D), BF16),
                   jax.ShapeDtypeStruct((1, D), F32), jax.ShapeDtypeStruct((1, 1), F32)],
        compiler_params=_params("arbitrary"),
    )(rl, w_ff2, x1, target, g_fpost)


def mm_tn(name, a, b, ta, tb, out_shape, out_spec):
    s = a.shape[0]

    def body(a_ref, b_ref, o_ref):
        o_ref[...] = _dot(a_ref[...], b_ref[...], TN)

    return pl.pallas_call(
        body, name=name, grid=(a.shape[1] // ta, b.shape[1] // tb),
        in_specs=[pl.BlockSpec((s, ta), lambda i, j: (0, i)), pl.BlockSpec((s, tb), lambda i, j: (0, j))],
        out_specs=out_spec, out_shape=jax.ShapeDtypeStruct(out_shape, F32),
        compiler_params=_params("parallel", "parallel"),
    )(a, b)


def mm_nt(name, a, w):
    s = a.shape[0]
    tm = 512

    def body(a_ref, w_ref, o_ref):
        o_ref[...] = _dot(a_ref[...], w_ref[...], NT).astype(BF16)

    return pl.pallas_call(
        body, name=name, grid=(s // tm,), in_specs=[_rows(tm, D), _full((D, D))], out_specs=_rows(tm, D),
        out_shape=jax.ShapeDtypeStruct((s, D), BF16), compiler_params=_params("parallel"),
    )(a, w)


def ff2_bwd(df, w_ff2, a):
    s = df.shape[0]
    tm = 1024

    def body(df_ref, w_ref, a_ref, da_ref):
        drl = _dot(df_ref[...], w_ref[...], NT)
        da_ref[...] = (drl * (2.0 * jnp.maximum(a_ref[...].astype(F32), 0.0))).astype(BF16)

    return pl.pallas_call(
        body, name="ff2_bwd", grid=(s // tm, D_FF // D),
        in_specs=[pl.BlockSpec((tm, D), lambda i, j: (i, 0)), pl.BlockSpec((D, D), lambda i, j: (j, 0)),
                  pl.BlockSpec((tm, D), lambda i, j: (i, j))],
        out_specs=pl.BlockSpec((tm, D), lambda i, j: (i, j)),
        out_shape=jax.ShapeDtypeStruct((s, D_FF), BF16), compiler_params=_params("parallel", "parallel"),
    )(df, w_ff2, a)


def ff1_bwd_norms(da, wg, x1, o, dy, g_fpre, g_post, swapping):
    s = x1.shape[0]
    tm = 512
    n = len(swapping)

    def body(*refs):
        da_ref, w_ref, x1_ref, o_ref, dy_ref, gf_ref, gp_ref = refs[:7]
        dx1_ref, do_ref, dgf_ref, dgp_ref = refs[7 + n:11 + n]
        acc_ref = refs[11 + 2 * n]
        i, k = pl.program_id(0), pl.program_id(1)
        if n:
            send, finish = _swap_phases(refs[7:7 + n], refs[11 + n:11 + 2 * n], *refs[12 + 2 * n:])
            pl.when((i == 0) & (k == 0))(send)

        @pl.when((i == 0) & (k == 0))
        def _():
            dgf_ref[...] = jnp.zeros_like(dgf_ref)
            dgp_ref[...] = jnp.zeros_like(dgp_ref)

        part = _dot(da_ref[...], w_ref[...], NT)

        @pl.when(k == 0)
        def _():
            acc_ref[...] = part

        @pl.when(k > 0)
        def _():
            acc_ref[...] += part

        @pl.when(k == N_CHIPS - 1)
        def _():
            dh2 = acc_ref[...]
            x1hat, r2 = _rms(x1_ref[...])
            dgf_ref[...] += jnp.sum(dh2 * x1hat, axis=0, keepdims=True)
            dx1 = dy_ref[...] + _rms_bwd(dh2 * gf_ref[...], x1hat, r2)
            ohat, r1 = _rms(o_ref[...])
            dgp_ref[...] += jnp.sum(dx1 * ohat, axis=0, keepdims=True)
            dx1_ref[...] = dx1
            do_ref[...] = _rms_bwd(dx1 * gp_ref[...], ohat, r1).astype(BF16)

        if n:
            pl.when((i == s // tm - 1) & (k == N_CHIPS - 1))(finish)

    row = pl.BlockSpec((tm, D), lambda i, k: (i, 0))
    vec = pl.BlockSpec((1, D), lambda i, k: (0, 0))
    res = pl.pallas_call(
        body, name="ff1_bwd_norms", grid=(s // tm, N_CHIPS),
        in_specs=[pl.BlockSpec((tm, D), lambda i, k: (i, k)), pl.BlockSpec((None, D, D), lambda i, k: (k, 0, 0)),
                  row, row, row, vec, vec] + [ANY] * n,
        out_specs=[row, row, vec, vec] + [ANY] * n,
        out_shape=[jax.ShapeDtypeStruct((s, D), F32), jax.ShapeDtypeStruct((s, D), BF16),
                   jax.ShapeDtypeStruct((1, D), F32), jax.ShapeDtypeStruct((1, D), F32)] + _swap_shapes(swapping),
        scratch_shapes=[pltpu.VMEM((tm, D), F32)] + (_swap_sems(n) if n else []),
        compiler_params=_params("arbitrary", "arbitrary", communicates=bool(n)),
    )(da, wg, x1, o, dy, g_fpre, g_post, *swapping)
    return res[0], res[1], res[2], res[3], res[4:]


def out_bwd_gates(do, w_out, pa, pb, z, bg):
    s = do.shape[0]
    tm = 512

    def body(do_ref, w_ref, pa_ref, pb_ref, ga_ref, gb_ref, bg_ref, dpa_ref, dpb_ref, dga_ref, dgb_ref, dbg_ref):
        @pl.when(pl.program_id(0) == 0)
        def _():
            dbg_ref[...] = jnp.zeros_like(dbg_ref)

        dm = _dot(do_ref[...], w_ref[...], NT)
        sa = _sigmoid(ga_ref[...] + bg_ref[0:1, :])
        sb = _sigmoid(gb_ref[...] + bg_ref[1:2, :])
        dpa_ref[...] = (dm * sa).astype(BF16)
        dpb_ref[...] = (dm * sb).astype(BF16)
        dga = dm * pa_ref[...].astype(F32) * (sa * (1.0 - sa))
        dgb = dm * pb_ref[...].astype(F32) * (sb * (1.0 - sb))
        dga_ref[...] = dga.astype(BF16)
        dgb_ref[...] = dgb.astype(BF16)
        dbg_ref[0:1, :] += jnp.sum(dga, axis=0, keepdims=True)
        dbg_ref[1:2, :] += jnp.sum(dgb, axis=0, keepdims=True)

    out = jax.ShapeDtypeStruct((s, D), BF16)
    return pl.pallas_call(
        body, name="out_bwd_gates", grid=(s // tm,),
        in_specs=[_rows(tm, D), _full((D, D)), _rows(tm, D), _rows(tm, D), _rows(tm, D, 5), _rows(tm, D, 6),
                  _full((2, D))],
        out_specs=[_rows(tm, D)] * 4 + [_full((2, D))],
        out_shape=[out] * 4 + [jax.ShapeDtypeStruct((2, D), F32)], compiler_params=_params("arbitrary"),
    )(do, w_out, pa, pb, z, z, bg)


def gating_bwd(z, dya, ln_g, ln_b, w_s, bs_t, swapping):
    s = z.shape[0]
    ones = functools.partial(jnp.ones, (8, CHUNK), BF16)
    n = len(swapping)

    def body(*refs):
        u_ref, v_ref, dya_ref, lg_ref, lb_ref, ws_ref, bst_ref = refs[:7]
        du_ref, dv_ref, dws_ref, dbs_ref, dlg_ref, dlb_ref = refs[7 + n:13 + n]
        dvn_ref = refs[13 + 2 * n]
        ci = pl.program_id(0)
        if n:
            send, finish = _swap_phases(refs[7:7 + n], refs[13 + n:13 + 2 * n], *refs[14 + 2 * n:])
            pl.when(ci == 0)(send)

        @pl.when(ci == 0)
        def _():
            dws_ref[...] = jnp.zeros_like(dws_ref)
            dbs_ref[...] = jnp.zeros_like(dbs_ref)
            dlg_ref[...] = jnp.zeros_like(dlg_ref)
            dlb_ref[...] = jnp.zeros_like(dlb_ref)

        ug, dug_du = _gelu_and_grad(u_ref[...].astype(F32))
        vg, dvg_dv = _gelu_and_grad(v_ref[...].astype(F32))
        vhat, rstd = _layer_norm(vg)
        vn = (vhat * lg_ref[...] + lb_ref[...]).astype(BF16)
        dya = dya_ref[...].astype(F32)
        for g in range(GROUPS):
            cols = slice(g * CHUNK, (g + 1) * CHUNK)
            ws = _tril_ws(ws_ref, g)
            mixed = _dot(ws, vn[:, cols]) + bst_ref[:, g:g + 1]
            du_ref[:, cols] = (dya[:, cols] * mixed * dug_du[:, cols]).astype(BF16)
            dmix = (dya[:, cols] * ug[:, cols]).astype(BF16)
            dbs_ref[g] += _dot(ones(), dmix, NT)
            dws_ref[g] += _dot(dmix, vn[:, cols], NT)
            dvn_ref[:, cols] = _dot(ws, dmix, TN)
        dvn = dvn_ref[...]
        dlg_ref[...] += jnp.sum(dvn * vhat, axis=0, keepdims=True)
        dlb_ref[...] += jnp.sum(dvn, axis=0, keepdims=True)
        dvh = dvn * lg_ref[...]
        dvg = rstd * (dvh - jnp.mean(dvh, axis=-1, keepdims=True)
                      - vhat * jnp.mean(dvh * vhat, axis=-1, keepdims=True))
        dv_ref[...] = (dvg * dvg_dv).astype(BF16)

        @pl.when(ci == pl.num_programs(0) - 1)
        def _():
            r = lax.broadcasted_iota(jnp.int32, (CHUNK, CHUNK), 0)
            c = lax.broadcasted_iota(jnp.int32, (CHUNK, CHUNK), 1)
            for g in range(GROUPS):
                dws_ref[g] = jnp.where(c <= r, dws_ref[g], 0.0)

        if n:
            pl.when(ci == pl.num_programs(0) - 1)(finish)

    out = jax.ShapeDtypeStruct((s, D), BF16)
    res = pl.pallas_call(
        body, name="gating_bwd", grid=(s // CHUNK,),
        in_specs=[_rows(CHUNK, D, 0), _rows(CHUNK, D, 1), _rows(CHUNK, D), _full((1, D)), _full((1, D)),
                  _full((GROUPS, CHUNK, CHUNK)), _full((CHUNK, GROUPS))] + [ANY] * n,
        out_specs=[_rows(CHUNK, D), _rows(CHUNK, D), _full((GROUPS, CHUNK, CHUNK)), _full((GROUPS, 8, CHUNK)),
                   _full((1, D)), _full((1, D))] + [ANY] * n,
        out_shape=[out, out, jax.ShapeDtypeStruct((GROUPS, CHUNK, CHUNK), F32),
                   jax.ShapeDtypeStruct((GROUPS, 8, CHUNK), F32),
                   jax.ShapeDtypeStruct((1, D), F32), jax.ShapeDtypeStruct((1, D), F32)] + _swap_shapes(swapping),
        scratch_shapes=[pltpu.VMEM((CHUNK, D), F32)] + (_swap_sems(n) if n else []),
        compiler_params=_params("arbitrary", communicates=bool(n)),
    )(z, z, dya, ln_g, ln_b, w_s, bs_t, *swapping)
    return (*res[:6], res[6:])


def attn_bwd(z, yb, dyb, lse, logc, ka, kb, scattering):
    s = z.shape[0]
    nq = s // ATT_T
    t = ATT_T
    grp = ATT_BWD_GROUP
    ngrp = HEADS // 2 // grp
    wide = 128 * grp
    qcol, kcol, vcol = 2 * D // wide, 3 * D // wide, 4 * D // wide
    scale = 1.0 / math.sqrt(HEAD_DIM)
    n = len(scattering)

    def body(*refs):
        q_ref, k_ref, v_ref, y_ref, dy_ref, lse_ref, lc_ref, ka_ref, kb_ref = refs[:9]
        dq_ref, dk_ref, dv_ref = refs[9 + n:12 + n]
        qa_s, qt_s, da_s, dt_s, dq_s, dkt_s, dvt_s = refs[12 + 2 * n:19 + 2 * n]
        gi, j = pl.program_id(0), pl.program_id(1)
        first, lane, ones = _head_masks()
        if n:
            send, finish = _scatter_phases(refs[9:9 + n], refs[12 + n:12 + 2 * n], *refs[19 + 2 * n:])
            pl.when((gi == 0) & (j == 0))(send)

        @pl.when(j == 0)
        def _():
            dq_s[...] = jnp.zeros_like(dq_s)
            for pr in range(grp):
                cols = slice(pr * 128, (pr + 1) * 128)
                for ib in range(nq):
                    rows = slice(ib * t, (ib + 1) * t)
                    q = q_ref[rows, cols].astype(F32) * scale
                    lse = lse_ref[rows, cols]
                    qa_s[pr, 0, ib] = jnp.where(first, q, _place3(lane, HEAD_DIM + 2 * AUG, _split3(-lse[:, 0:1]),
                                                                  ones(0, 2 * AUG))).astype(BF16)
                    qa_s[pr, 1, ib] = jnp.where(
                        first, _place3(lane, 2 * AUG, _split3(-lse[:, HEAD_DIM:HEAD_DIM + 1]), ones(1, 2 * AUG)),
                        q).astype(BF16)
                    qt_s[pr, ib, :, 0:t] = jnp.where(first, q, 0.0).T.astype(BF16)
                    qt_s[pr, ib, :, t:2 * t] = jnp.where(first, 0.0, q).T.astype(BF16)
                    do = dy_ref[rows, cols].astype(F32)
                    prod = do * y_ref[rows, cols].astype(F32)
                    dd0 = jnp.sum(jnp.where(first, prod, 0.0), axis=-1, keepdims=True)
                    dd1 = jnp.sum(jnp.where(first, 0.0, prod), axis=-1, keepdims=True)
                    da_s[pr, 0, ib] = jnp.where(first, do, _place3(lane, HEAD_DIM, _split3(-dd0), 0.0)).astype(BF16)
                    da_s[pr, 1, ib] = jnp.where(first, _place3(lane, 0, _split3(-dd1), 0.0), do).astype(BF16)
                    dt_s[pr, ib, :, 0:t] = jnp.where(first, do, 0.0).T.astype(BF16)
                    dt_s[pr, ib, :, t:2 * t] = jnp.where(first, 0.0, do).T.astype(BF16)

        keys = []
        for pr in range(grp):
            kj = k_ref[:, pr * 128:(pr + 1) * 128].astype(F32)
            vj = v_ref[:, pr * 128:(pr + 1) * 128].astype(F32)
            keys.append((
                jnp.where(first, kj, ka_ref[pr, 0] + kb_ref[pr, 0, pl.ds(j, 1), :]).astype(BF16),
                jnp.where(first, ka_ref[pr, 1] + kb_ref[pr, 1, pl.ds(j, 1), :], kj).astype(BF16),
                jnp.concatenate([jnp.where(first, kj, 0.0), jnp.where(first, 0.0, kj)], axis=0).astype(BF16),
                jnp.where(first, vj, ones(0, AUG)).astype(BF16),
                jnp.where(first, ones(1, AUG), vj).astype(BF16)))
        dkt_s[...] = jnp.zeros_like(dkt_s)
        dvt_s[...] = jnp.zeros_like(dvt_s)

        def step(i, _):
            lc = lc_ref[i - j]
            rows = pl.ds(pl.multiple_of(i * t, t), t)
            for pr in range(grp):
                k0a, k1a, kst, v0a, v1a = keys[pr]
                p0 = jnp.exp(_dot(qa_s[pr, 0, i], k0a, NT) + lc)
                p1 = jnp.exp(_dot(qa_s[pr, 1, i], k1a, NT) + lc)
                e0 = (p0 * _dot(da_s[pr, 0, i], v0a, NT)).astype(BF16)
                e1 = (p1 * _dot(da_s[pr, 1, i], v1a, NT)).astype(BF16)
                dq_s[pr, rows, :] += _dot(jnp.concatenate([e0, e1], axis=1), kst)
                dvt_s[pr] += _dot(dt_s[pr, i], jnp.concatenate([p0.astype(BF16), p1.astype(BF16)], axis=0))
                dkt_s[pr] += _dot(qt_s[pr, i], jnp.concatenate([e0, e1], axis=0))
            return 0

        lax.fori_loop(j, nq, step, 0)
        for pr in range(grp):
            dk_ref[:, pr * 128:(pr + 1) * 128] = dkt_s[pr].T.astype(BF16)
            dv_ref[:, pr * 128:(pr + 1) * 128] = dvt_s[pr].T.astype(BF16)

        @pl.when(j == nq - 1)
        def _():
            for pr in range(grp):
                dq_ref[:, pr * 128:(pr + 1) * 128] = (dq_s[pr] * scale).astype(BF16)

        if n:
            pl.when((gi == ngrp - 1) & (j == nq - 1))(finish)

    colblock = lambda c: pl.BlockSpec((s, wide), lambda g, j: (0, c + g))
    blk = lambda c: pl.BlockSpec((t, wide), lambda g, j: (j, c + g))
    out = jax.ShapeDtypeStruct((s, D), BF16)
    res = pl.pallas_call(
        body, name="attn_bwd", grid=(ngrp, nq),
        in_specs=[colblock(qcol), blk(kcol), blk(vcol), colblock(0), colblock(0), colblock(0),
                  _full((nq, t, t)), pl.BlockSpec((grp, 2, t, 128), lambda g, j: (g, 0, 0, 0)),
                  pl.BlockSpec((grp, 2, nq, 128), lambda g, j: (g, 0, 0, 0))] + [ANY] * n,
        out_specs=[colblock(0), blk(0), blk(0)] + [ANY] * n, out_shape=[out] * 3 + _scatter_shapes(scattering),
        scratch_shapes=[pltpu.VMEM((grp, 2, nq, t, 128), BF16), pltpu.VMEM((grp, nq, 128, 2 * t), BF16),
                        pltpu.VMEM((grp, 2, nq, t, 128), BF16), pltpu.VMEM((grp, nq, 128, 2 * t), BF16),
                        pltpu.VMEM((grp, s, 128), F32), pltpu.VMEM((grp, 128, t), F32),
                        pltpu.VMEM((grp, 128, t), F32)]
        + (_scatter_sems(n) if n else []),
        compiler_params=_params("arbitrary", "arbitrary", communicates=bool(n)),
    )(z, z, z, yb, dyb, lse, logc, ka, kb, *scattering)
    return res[0], res[1], res[2], res[3:]


def in_bwd_norm(dz, wg, x, dx1, g_pre, scattering):
    s = x.shape[0]
    tm = 512
    n = len(scattering)

    def body(*refs):
        dz_ref, w_ref, x_ref, dx1_ref, g_ref = refs[:5]
        dx_ref, dg_ref = refs[5 + n:7 + n]
        acc_ref = refs[7 + 2 * n]
        i, k = pl.program_id(0), pl.program_id(1)
        if n:
            send, finish = _scatter_phases(refs[5:5 + n], refs[7 + n:7 + 2 * n], *refs[8 + 2 * n:])
            pl.when((i == 0) & (k == 0))(send)

        @pl.when((i == 0) & (k == 0))
        def _():
            dg_ref[...] = jnp.zeros_like(dg_ref)

        part = _dot(dz_ref[...], w_ref[...], NT)

        @pl.when(k == 0)
        def _():
            acc_ref[...] = part

        @pl.when(k > 0)
        def _():
            acc_ref[...] += part

        @pl.when(k == N_CHIPS - 1)
        def _():
            dh = acc_ref[...]
            xhat, r = _rms(x_ref[...])
            dg_ref[...] += jnp.sum(dh * xhat, axis=0, keepdims=True)
            dx_ref[...] = dx1_ref[...] + _rms_bwd(dh * g_ref[...], xhat, r)

        if n:
            pl.when((i == s // tm - 1) & (k == N_CHIPS - 1))(finish)

    row = pl.BlockSpec((tm, D), lambda i, k: (i, 0))
    vec = pl.BlockSpec((1, D), lambda i, k: (0, 0))
    res = pl.pallas_call(
        body, name="in_bwd_norm", grid=(s // tm, N_CHIPS),
        in_specs=[pl.BlockSpec((tm, IN_SHARD), lambda i, k: (i, k)),
                  pl.BlockSpec((None, D, IN_SHARD), lambda i, k: (k, 0, 0)), row, row, vec] + [ANY] * n,
        out_specs=[row, vec] + [ANY] * n,
        out_shape=[jax.ShapeDtypeStruct((s, D), F32), jax.ShapeDtypeStruct((1, D), F32)]
        + _scatter_shapes(scattering),
        scratch_shapes=[pltpu.VMEM((tm, D), F32)] + (_scatter_sems(n) if n else []),
        compiler_params=_params("arbitrary", "arbitrary", communicates=bool(n)),
    )(dz, wg, x, dx1, g_pre, *scattering)
    return res[0], res[1], res[2:]


def _adamw_math(w, g, m, v):
    m = ADAM_B1 * m + (1.0 - ADAM_B1) * g
    v = ADAM_B2 * v + (1.0 - ADAM_B2) * (g * g)
    m_hat = m / (1.0 - ADAM_B1 ** ADAM_STEP)
    v_hat = v / (1.0 - ADAM_B2 ** ADAM_STEP)
    delta = -ADAM_LR * (m_hat / (jnp.sqrt(v_hat) + ADAM_EPS) + ADAM_WD * w)
    return delta, m, v


def adamw(name, w, g, m, v, tr):
    r, c = w.shape

    def body(w_ref, g_ref, m_ref, v_ref, go_ref, d_ref, nm_ref, nv_ref):
        g = g_ref[...]
        go_ref[...] = g
        d_ref[...], nm_ref[...], nv_ref[...] = _adamw_math(w_ref[...], g, m_ref[...], v_ref[...])

    out = jax.ShapeDtypeStruct((r, c), F32)
    return pl.pallas_call(
        body, name=name, grid=(r // tr,), in_specs=[_rows(tr, c)] * 4, out_specs=[_rows(tr, c)] * 4,
        out_shape=[out] * 4, compiler_params=_params("parallel"),
    )(w, g, m, v)


def add_halves(name, g, recv, c_idx, tr):
    n, h, c = recv.shape

    def body(c_ref, g_ref, r_ref, o_ref):
        o_ref[...] = (g_ref[...] + r_ref[...]).astype(BF16)

    nb = h // tr
    return pl.pallas_call(
        body, name=name,
        grid_spec=pltpu.PrefetchScalarGridSpec(
            num_scalar_prefetch=1, grid=(n, nb),
            in_specs=[pl.BlockSpec((None, tr, c), lambda k, i, c_ref: (k, c_ref[0] * nb + i, 0)),
                      pl.BlockSpec((None, tr, c), lambda k, i, c_ref: (k, i, 0))],
            out_specs=pl.BlockSpec((None, tr, c), lambda k, i, c_ref: (k, i, 0))),
        out_shape=jax.ShapeDtypeStruct((n, h, c), BF16), compiler_params=_params("parallel", "parallel"),
    )(c_idx, g, recv)


def sum_chips(name, parts, recv, where, tr):
    n, h, c = recv.shape
    nb = h // tr

    def body(w_ref, p_ref, r_ref, o_ref):
        acc = p_ref[...].astype(F32)
        for k in range(n):
            acc = acc + r_ref[k].astype(F32)
        o_ref[...] = acc

    return pl.pallas_call(
        body, name=name,
        grid_spec=pltpu.PrefetchScalarGridSpec(
            num_scalar_prefetch=1, grid=(nb,),
            in_specs=[pl.BlockSpec((None, tr, c), lambda i, w_ref: (w_ref[0], i, 0)),
                      pl.BlockSpec((n, tr, c), lambda i, w_ref: (0, i, 0))],
            out_specs=pl.BlockSpec((tr, c), lambda i, w_ref: (w_ref[1] * nb + i, 0))),
        out_shape=jax.ShapeDtypeStruct((2 * h, c), F32), compiler_params=_params("parallel"),
    )(where, parts, recv)


def place_shard(name, shard, where, dtype, tr):
    r, c = shard.shape

    def body(w_ref, s_ref, o_ref):
        o_ref[...] = s_ref[...].astype(dtype)

    return pl.pallas_call(
        body, name=name,
        grid_spec=pltpu.PrefetchScalarGridSpec(
            num_scalar_prefetch=1, grid=(r // tr,),
            in_specs=[pl.BlockSpec((tr, c), lambda i, w_ref: (i, 0))],
            out_specs=pl.BlockSpec((None, tr, c), lambda i, w_ref: (w_ref[0], i, 0))),
        out_shape=jax.ShapeDtypeStruct((N_CHIPS, r, c), dtype), compiler_params=_params("parallel"),
    )(where, shard)


ANY = pl.BlockSpec(memory_space=pl.ANY)


def _place():
    x, y, c = lax.axis_index("x"), lax.axis_index("y"), lax.axis_index("c")
    chips = [(1 - x, y), (x, 1 - y), (1 - x, 1 - y)]
    return x, y, c, chips


def gather_shards(arrays):
    n = len(arrays)

    def body(*refs):
        send, pass_on, finish = _gather_phases(refs[n:2 * n], *refs[2 * n:], _spans(arrays))
        send()
        pass_on()
        finish()

    return pl.pallas_call(
        body, name="gather_shards", in_specs=[ANY] * n, out_specs=[ANY] * n,
        out_shape=[jax.ShapeDtypeStruct(a.shape, a.dtype) for a in _arrays(arrays)],
        input_output_aliases={w: w for w in range(n)}, scratch_shapes=_gather_sems(n),
        compiler_params=pltpu.CompilerParams(has_side_effects=True),
    )(*arrays)


def _gather_sems(n):
    return [pltpu.SemaphoreType.DMA((6 * n,)), pltpu.SemaphoreType.DMA((6 * n,))]


class Span(typing.NamedTuple):
    array: jax.Array
    lo: int
    hi: int
    ways: tuple = (0, 1, 2)


def _arrays(gathering):
    return [g.array if isinstance(g, Span) else g for g in gathering]


def _spans(gathering):
    return [(g.lo, g.hi, g.ways) if isinstance(g, Span) else (0, g.shape[1], (0, 1, 2)) for g in gathering]


def _gather_phases(out, send_sems, recv_sems, spans):
    n = len(out)
    if not any(ways for _, _, ways in spans):
        return (lambda: None,) * 3
    x, y, c, chips = _place()
    me = 2 * x + y
    sibling = (x, y, 1 - c)

    def half(w, chip, core):
        lo, hi, _ = spans[w]
        h = (hi - lo) // 2
        return out[w].at[chip, pl.ds(lo + core * h, h)]

    def copy(k, block, to):
        return pltpu.make_async_remote_copy(src_ref=block, dst_ref=block, send_sem=send_sems.at[k],
                                            recv_sem=recv_sems.at[k], device_id=to, device_id_type=MESH)

    def over_ici(w, j, chip):
        return copy(3 * w + j, half(w, chip, c), (chips[j][0], chips[j][1], c))

    def over_d2d(w, j, core):
        return copy(3 * n + 3 * w + j, half(w, 2 * chips[j][0] + chips[j][1], core), sibling)

    pairs = [(w, j) for w in range(n) for j in spans[w][2]]

    def send():
        for w, j in pairs:
            over_ici(w, j, me).start()

    def pass_on():
        for w, j in pairs:
            over_ici(w, j, 2 * chips[j][0] + chips[j][1]).wait_recv()
            over_d2d(w, j, c).start()

    def finish():
        for w, j in pairs:
            over_d2d(w, j, 1 - c).wait_recv()
        for w, j in pairs:
            over_ici(w, j, me).wait_send()
            over_d2d(w, j, c).wait_send()

    return send, pass_on, finish


def _relay_sems():
    return [pltpu.SemaphoreType.DMA((4,)), pltpu.SemaphoreType.DMA((4,))]


def _relay_phases(out, send_sems, recv_sems):
    x, y, c, chips = _place()
    sibling = (x, y, 1 - c)
    rows = out.shape[1]
    quarter = rows // 4
    far = 2 * chips[2][0] + chips[2][1]

    def piece(chip, way, core):
        return out.at[chip, pl.ds(way * (rows // 2) + core * quarter, quarter)]

    def copy(k, block, to):
        return pltpu.make_async_remote_copy(src_ref=block, dst_ref=block, send_sem=send_sems.at[k],
                                            recv_sem=recv_sems.at[k], device_id=to, device_id_type=MESH)

    def over_ici(way, chip):
        return copy(way, piece(chip, way, c), (chips[way][0], chips[way][1], c))

    def over_d2d(way, core):
        return copy(2 + way, piece(far, way, core), sibling)

    def send():
        for way in range(2):
            other = chips[1 - way]
            over_ici(way, 2 * other[0] + other[1]).start()

    def pass_on():
        for way in range(2):
            over_ici(way, far).wait_recv()
            over_d2d(way, c).start()

    def finish():
        for way in range(2):
            over_d2d(way, 1 - c).wait_recv()
        for way in range(2):
            other = chips[1 - way]
            over_ici(way, 2 * other[0] + other[1]).wait_send()
            over_d2d(way, c).wait_send()

    return send, pass_on, finish


def swap_halves(name, grads):
    n = len(grads)

    def body(*refs):
        send, finish = _swap_phases(refs[:n], refs[n:2 * n], *refs[2 * n:])
        send()
        finish()

    return pl.pallas_call(
        body, name=name, in_specs=[ANY] * n, out_specs=[ANY] * n, out_shape=_swap_shapes(grads),
        scratch_shapes=_swap_sems(n), compiler_params=pltpu.CompilerParams(has_side_effects=True),
    )(*grads)


def _swap_shapes(grads):
    return [jax.ShapeDtypeStruct((a.shape[0], a.shape[1] // 2, a.shape[2]), a.dtype) for a in grads]


def _swap_sems(n):
    return [pltpu.SemaphoreType.DMA((n,)), pltpu.SemaphoreType.DMA((n,))]


def _swap_phases(g, out, send_sems, recv_sems):
    x, y, c, _ = _place()

    def copies():
        return [pltpu.make_async_remote_copy(
            src_ref=g[w].at[:, pl.ds((1 - c) * (g[w].shape[1] // 2), g[w].shape[1] // 2)], dst_ref=out[w],
            send_sem=send_sems.at[w], recv_sem=recv_sems.at[w], device_id=(x, y, 1 - c), device_id_type=MESH)
            for w in range(len(g))]

    def send():
        for cp in copies():
            cp.start()

    def finish():
        for cp in copies():
            cp.wait()

    return send, finish


def _send_phases(g, out, send_sems, recv_sems):
    x, y, c, _ = _place()

    def copies():
        return [pltpu.make_async_remote_copy(
            src_ref=g[w], dst_ref=out[w], send_sem=send_sems.at[w], recv_sem=recv_sems.at[w],
            device_id=(x, y, 1 - c), device_id_type=MESH) for w in range(len(g))]

    def send():
        for cp in copies():
            cp.start()

    def finish():
        for cp in copies():
            cp.wait()

    return send, finish


def dw_in_half(name, h, dz, which, sending):
    s = h.shape[0]
    hh, tb = D // 2, IN_SHARD // 2
    n = len(sending)
    steps = IN_COLS // tb

    def body(w_ref, *refs):
        a_ref, b_ref, o_ref = refs[0], refs[1], refs[2 + n]
        j = pl.program_id(0)
        if n:
            send, finish = _send_phases(refs[2:2 + n], refs[3 + n:3 + 2 * n], *refs[3 + 2 * n:])
            pl.when(j == 0)(send)
        o_ref[...] = _dot(a_ref[...], b_ref[...], TN)
        if n:
            pl.when(j == steps - 1)(finish)

    out = pl.pallas_call(
        body, name=name,
        grid_spec=pltpu.PrefetchScalarGridSpec(
            num_scalar_prefetch=1, grid=(steps,),
            in_specs=[pl.BlockSpec((s, hh), lambda j, w: (0, w[0])), pl.BlockSpec((s, tb), lambda j, w: (0, j))]
            + [ANY] * n,
            out_specs=[pl.BlockSpec((None, hh, tb), lambda j, w: (j // 2, 0, j % 2))] + [ANY] * n,
            scratch_shapes=_swap_sems(n) if n else []),
        out_shape=[jax.ShapeDtypeStruct((N_CHIPS, hh, IN_SHARD), F32)]
        + [jax.ShapeDtypeStruct(a.shape, a.dtype) for a in sending],
        compiler_params=_params("arbitrary", communicates=bool(n)),
    )(which, h, dz, *sending)
    return out[0], out[1:]


def scatter_chips(parts):
    n = len(parts)

    def body(*refs):
        send, finish = _scatter_phases(refs[:n], refs[n:2 * n], *refs[2 * n:])
        send()
        finish()

    return pl.pallas_call(
        body, name="scatter_chips", in_specs=[ANY] * n, out_specs=[ANY] * n,
        out_shape=_scatter_shapes(parts), scratch_shapes=_scatter_sems(n),
        compiler_params=pltpu.CompilerParams(has_side_effects=True),
    )(*parts)


def _scatter_shapes(parts):
    return [jax.ShapeDtypeStruct((3,) + a.shape[1:], a.dtype) for a in parts]


def _scatter_sems(n):
    return [pltpu.SemaphoreType.DMA((3 * n,)), pltpu.SemaphoreType.DMA((3 * n,))]


def _scatter_phases(p, out, send_sems, recv_sems):
    x, y, c, chips = _place()

    def copies():
        return [pltpu.make_async_remote_copy(
            src_ref=p[w].at[2 * px + py], dst_ref=out[w].at[j], send_sem=send_sems.at[3 * w + j],
            recv_sem=recv_sems.at[3 * w + j], device_id=(px, py, c), device_id_type=MESH)
            for w in range(len(p)) for j, (px, py) in enumerate(chips)]

    def send():
        for cp in copies():
            cp.start()

    def finish():
        for cp in copies():
            cp.wait()

    return send, finish


def join_halves(arrays):
    n = len(arrays)

    def body(*refs):
        out = refs[n:2 * n]
        send_sems, recv_sems = refs[2 * n:]
        x, y, c, _ = _place()

        def copy(w, core):
            h = out[w].shape[0] // 2
            rows = out[w].at[pl.ds(core * h, h)]
            return pltpu.make_async_remote_copy(
                src_ref=rows, dst_ref=rows, send_sem=send_sems.at[w], recv_sem=recv_sems.at[w],
                device_id=(x, y, 1 - c), device_id_type=MESH)

        for w in range(n):
            copy(w, c).start()
        for w in range(n):
            copy(w, 1 - c).wait_recv()
        for w in range(n):
            copy(w, c).wait_send()

    return pl.pallas_call(
        body, name="join_halves", in_specs=[ANY] * n, out_specs=[ANY] * n,
        out_shape=[jax.ShapeDtypeStruct(a.shape, a.dtype) for a in arrays],
        input_output_aliases={w: w for w in range(n)},
        scratch_shapes=[pltpu.SemaphoreType.DMA((n,)), pltpu.SemaphoreType.DMA((n,))],
        compiler_params=pltpu.CompilerParams(has_side_effects=True),
    )(*arrays)


def allreduce_small(packed):
    r, c = packed.shape
    n_dev = 8

    def body(x_ref, all_ref, sum_ref, send_sems, recv_sems, local_sem):
        x, y, cc, chips = _place()
        me, sibling = (x, y, cc), (x, y, 1 - cc)

        def rows(px, py, pc):
            return all_ref.at[4 * px + 2 * py + pc]

        def copy(k, block, to, src=None):
            return pltpu.make_async_remote_copy(
                src_ref=rows(*block) if src is None else src, dst_ref=rows(*block), send_sem=send_sems.at[k],
                recv_sem=recv_sems.at[k], device_id=to, device_id_type=MESH)

        mine = pltpu.make_async_copy(x_ref, rows(*me), local_sem)
        mine.start()
        first = [copy(0, me, sibling, src=x_ref)]
        first += [copy(1 + j, me, (*chip, cc), src=x_ref) for j, chip in enumerate(chips)]
        for cp in first:
            cp.start()
        passed = [copy(4 + j, (*chip, cc), sibling) for j, chip in enumerate(chips)]
        for j, chip in enumerate(chips):
            copy(1 + j, (*chip, cc), me).wait_recv()
            passed[j].start()
        copy(0, sibling, me).wait_recv()
        for j, chip in enumerate(chips):
            copy(4 + j, (*chip, 1 - cc), me).wait_recv()
        for cp in first + passed:
            cp.wait_send()
        mine.wait()
        acc = all_ref[0]
        for k in range(1, n_dev):
            acc = acc + all_ref[k]
        sum_ref[...] = acc

    vm = pl.BlockSpec(memory_space=pltpu.VMEM)
    return pl.pallas_call(
        body, name="allreduce_small", in_specs=[vm], out_specs=[vm, vm],
        out_shape=[jax.ShapeDtypeStruct((n_dev, r, c), F32), jax.ShapeDtypeStruct((r, c), F32)],
        scratch_shapes=[pltpu.SemaphoreType.DMA((7,)), pltpu.SemaphoreType.DMA((7,)), pltpu.SemaphoreType.DMA],
        compiler_params=pltpu.CompilerParams(has_side_effects=True, vmem_limit_bytes=VMEM_LIMIT),
    )(packed)[1]


def local_step(x, target, vecs, w_s, bs_t, bg, wg_in, late, core=None, order=None):
    on_mesh = core is not None

    def add(names, grads, recv):
        return [add_halves("add_" + n, g, r, core, min(r.shape[1], 256)) for n, g, r in zip(names, grads, recv)]

    g_pre, ln_g, ln_b, g_post, g_fpre, g_fpost = vecs
    s = x.shape[0]
    if order is None:
        order = jnp.arange(N_CHIPS, dtype=jnp.int32)
    logc = _attn_tables(s)
    ka, kb = _alibi_tables(s)

    h = norm_pre(x, g_pre)
    if not on_mesh:
        wg_a, wg_b, wg_out, wg_ff1, wg_ff2 = late
    if on_mesh:
        cut = D // 4
        z, (wg_in,), (wg_a, wg_b, wg_out, wg_ff1, wg_ff2) = mm_in(
            "mm_in_own", h, wg_in, order, 0, 1, None, [Span(wg_in, 0, D, (0, 1))], casting=late)
        z, (wg_in,), _ = mm_in("mm_in_near", h, wg_in, order, 1, 2, z, [Span(wg_in, 0, D, ())], relay=True)
        z, _, _ = mm_in("mm_in_far", h, wg_in, order, 3, 1, z, [Span(wg_in, 0, D, ())])
        ya, (wg_a,) = gating_fwd(z, ln_g, ln_b, w_s, bs_t, [wg_a])
        yb, lse, (wg_b, wg_ff1, wg_ff2, bg) = attn_fwd(z, logc, ka, kb, [wg_b, wg_ff1, Span(wg_ff2, cut, D), bg])
        bg = jnp.transpose(bg[:, :2, :], (1, 0, 2)).reshape(2, D)
    else:
        z, _, _ = mm_in("mm_in", h, wg_in, order, 0, N_CHIPS, None, [Span(wg_in, 0, D, ())])
        ya, _ = gating_fwd(z, ln_g, ln_b, w_s, bs_t, [])
        yb, lse, _ = attn_fwd(z, logc, ka, kb, [])
    merged, pa, pb, got = proj_merge(ya, yb, wg_a.reshape(D, D), wg_b.reshape(D, D), z, bg, [wg_out] if on_mesh else [])
    wg_out = got[0] if on_mesh else wg_out
    w_out = wg_out.reshape(D, D)
    o, x1, h2 = out_norm(merged, w_out, x, g_post, g_fpre)
    a, rl, got = mm_ff1(h2, wg_ff1, [Span(wg_ff2, 0, cut)] if on_mesh else [])
    w_ff2 = (got[0] if on_mesh else wg_ff2).reshape(D_FF, D)
    dy, df, d_gfpost, loss = ff2_loss(rl, w_ff2, x1, target, g_fpost)

    half_cols = pl.BlockSpec((D, D // 2), lambda i, j: (0, j))
    d_wff2 = mm_tn("dw_ff2", rl, df, D // 2, D, (D_FF, D), pl.BlockSpec((D // 2, D), lambda i, j: (i, 0)))
    da = ff2_bwd(df, w_ff2, a)
    d_wff1 = mm_tn("dw_ff1", h2, da, D, D // 2, (N_CHIPS, D, D),
                   pl.BlockSpec((None, D, D // 2), lambda i, j: (j // 2, 0, j % 2)))
    d_ff = [d_wff1, d_wff2.reshape(N_CHIPS, D, D)]
    dx1, do, d_gfpre, d_gpost, recv_ff = ff1_bwd_norms(da, wg_ff1, x1, o, dy, g_fpre, g_post, d_ff if on_mesh else [])
    d_wout = mm_tn("dw_out", merged, do, D, D // 2, (D, D), half_cols)
    dpa, dpb, dga, dgb, d_bg = out_bwd_gates(do, w_out, pa, pb, z, bg)
    d_wa = mm_tn("dw_a", ya, dpa, D, D // 2, (D, D), half_cols)
    d_wb = mm_tn("dw_b", yb, dpb, D, D // 2, (D, D), half_cols)
    dya = mm_nt("dy_a", dpa, wg_a.reshape(D, D))
    dyb = mm_nt("dy_b", dpb, wg_b.reshape(D, D))
    d_proj = [d_wa.reshape(N_CHIPS, D // N_CHIPS, D), d_wb.reshape(N_CHIPS, D // N_CHIPS, D),
              d_wout.reshape(N_CHIPS, D // N_CHIPS, D)]
    du, dv, d_ws, d_bs, d_lng, d_lnb, recv_proj = gating_bwd(z, dya, ln_g, ln_b, w_s, bs_t, d_proj if on_mesh else [])
    early = d_proj + d_ff
    parts_early = add(BIG[1:], early, list(recv_proj) + list(recv_ff)) if on_mesh else []
    dq, dk, dvb, got_early = attn_bwd(z, yb, dyb, lse, logc, ka, kb, parts_early)
    dz = jnp.concatenate([du, dv, dq, dk, dvb, dga, dgb], axis=1)
    if on_mesh:
        for_sibling, _ = dw_in_half("dw_in_sibling", h, dz, 1 - core, [])
        mine, from_sibling = dw_in_half("dw_in_mine", h, dz, core, [for_sibling])
        d_win = None
        parts_late = [add_halves("add_w_in", mine, from_sibling[0], jnp.zeros((1,), jnp.int32), 256)]
    else:
        half = IN_SHARD // 2
        d_win = mm_tn("dw_in", h, dz, D, half, (N_CHIPS, D, IN_SHARD),
                      pl.BlockSpec((None, D, half), lambda i, j: (j // 2, 0, j % 2)))
        parts_late = []
    dx, d_gpre, got_late = in_bwd_norm(dz, wg_in, x, dx1, g_pre, parts_late)

    small = dict(norm_mix_pre=d_gpre, b_gate=d_bg, ln_v_g=d_lng, ln_v_b=d_lnb, w_s=d_ws, b_s=d_bs[:, 0, :],
                 norm_mix_post=d_gpost, norm_ffn_pre=d_gfpre, norm_ffn_post=d_gfpost)
    return loss[0, 0], dx, [d_win] + early, small, parts_late + parts_early, list(got_late) + list(got_early)


BIG = ("w_in", "w_a_proj", "w_b_proj", "w_out", "w_ff1", "w_ff2")
SMALL = ("norm_mix_pre", "ln_v_g", "ln_v_b", "b_s", "norm_mix_post", "norm_ffn_pre", "norm_ffn_post", "w_s", "b_gate")
ORDER = ("norm_mix_pre", "w_in", "b_gate", "ln_v_g", "ln_v_b", "w_s", "b_s", "w_a_proj", "w_b_proj", "w_out",
         "norm_mix_post", "norm_ffn_pre", "w_ff1", "w_ff2", "norm_ffn_post")
VEC_ROWS = D // 128
WS_ROW = 7 * VEC_ROWS
BG_ROW = WS_ROW + GROUPS * CHUNK
LOSS_ROW = BG_ROW + 2 * VEC_ROWS
PACK_ROWS = LOSS_ROW + 8


def pack_small(small, loss):
    vectors = [small[n] for n in SMALL[:7]]

    def body(*refs):
        out = refs[-1]
        ws_ref, bg_ref, loss_ref = refs[7:10]
        for i, n in enumerate(SMALL[:7]):
            if n == "b_s":
                out[i * VEC_ROWS:(i + 1) * VEC_ROWS, :] = refs[i][...]
            else:
                for j in range(VEC_ROWS):
                    out[i * VEC_ROWS + j:i * VEC_ROWS + j + 1, :] = refs[i][:, j * 128:(j + 1) * 128]
        for g in range(GROUPS):
            out[WS_ROW + g * CHUNK:WS_ROW + (g + 1) * CHUNK, :] = ws_ref[g]
        for r in range(2):
            for j in range(VEC_ROWS):
                row = BG_ROW + r * VEC_ROWS + j
                out[row:row + 1, :] = bg_ref[r:r + 1, j * 128:(j + 1) * 128]
        lane = lax.broadcasted_iota(jnp.int32, (8, 128), 1)
        sub = lax.broadcasted_iota(jnp.int32, (8, 128), 0)
        out[LOSS_ROW:LOSS_ROW + 8, :] = jnp.where((lane == 0) & (sub == 0), loss_ref[...], 0.0)

    return pl.pallas_call(
        body, name="pack_small", out_shape=jax.ShapeDtypeStruct((PACK_ROWS, 128), F32),
        compiler_params=_params(),
    )(*vectors, small["w_s"], small["b_gate"], loss)


def adamw_small(summed, chip, w, m, v):
    shapes = {n: (1, D) for n in SMALL}
    shapes.update(b_s=(GROUPS, CHUNK), w_s=(GROUPS * CHUNK, CHUNK), b_gate=(2, D // N_CHIPS))
    flat = lambda t: [t[n].reshape(shapes[n]) for n in SMALL]
    per = D // N_CHIPS // 128

    def body(chip_ref, sum_ref, *refs):
        params, outs = refs[:27], refs[27:]
        sub = lax.broadcasted_iota(jnp.int32, (VEC_ROWS, 128), 0)

        def gate_row(r):
            rows = sum_ref[BG_ROW + r * VEC_ROWS:BG_ROW + (r + 1) * VEC_ROWS, :]
            return jnp.concatenate([jnp.sum(jnp.where(sub == per * chip_ref[0] + j, rows, 0.0), axis=0, keepdims=True)
                                    for j in range(per)], axis=1)

        for i, n in enumerate(SMALL):
            if n == "b_s":
                g = sum_ref[i * VEC_ROWS:(i + 1) * VEC_ROWS, :]
            elif n == "w_s":
                g = sum_ref[WS_ROW:BG_ROW, :]
            elif n == "b_gate":
                g = jnp.concatenate([gate_row(0), gate_row(1)], axis=0)
            else:
                g = jnp.concatenate([sum_ref[i * VEC_ROWS + j:i * VEC_ROWS + j + 1, :] for j in range(VEC_ROWS)],
                                    axis=1)
            delta, nm, nv = _adamw_math(params[i][...], g, params[9 + i][...], params[18 + i][...])
            outs[4 * i][...], outs[4 * i + 1][...], outs[4 * i + 2][...], outs[4 * i + 3][...] = g, delta, nm, nv

    vm = pl.BlockSpec(memory_space=pltpu.VMEM)
    res = pl.pallas_call(
        body, name="adamw_small",
        in_specs=[pl.BlockSpec(memory_space=pltpu.SMEM)] + [vm] * 28, out_specs=[vm] * 36,
        out_shape=[jax.ShapeDtypeStruct(shapes[n], F32) for n in SMALL for _ in range(4)],
        compiler_params=_params(),
    )(chip, summed, *flat(w), *flat(m), *flat(v))
    return {n: tuple(r.reshape(w[n].shape) for r in res[4 * i:4 * i + 4]) for i, n in enumerate(SMALL)}


def kernel(x, norm_mix_pre, w_in, b_gate, ln_v_g, ln_v_b, w_s, b_s, w_a_proj, w_b_proj, w_out, norm_mix_post, norm_ffn_pre, w_ff1, w_ff2, norm_ffn_post, loss_target, m_norm_mix_pre, m_w_in, m_b_gate, m_ln_v_g, m_ln_v_b, m_w_s, m_b_s, m_w_a_proj, m_w_b_proj, m_w_out, m_norm_mix_post, m_norm_ffn_pre, m_w_ff1, m_w_ff2, m_norm_ffn_post, v_norm_mix_pre, v_w_in, v_b_gate, v_ln_v_g, v_ln_v_b, v_w_s, v_b_s, v_w_a_proj, v_w_b_proj, v_w_out, v_norm_mix_post, v_norm_ffn_pre, v_w_ff1, v_w_ff2, v_norm_ffn_post):
    w = dict(norm_mix_pre=norm_mix_pre, w_in=w_in, b_gate=b_gate, ln_v_g=ln_v_g, ln_v_b=ln_v_b, w_s=w_s, b_s=b_s,
             w_a_proj=w_a_proj, w_b_proj=w_b_proj, w_out=w_out, norm_mix_post=norm_mix_post,
             norm_ffn_pre=norm_ffn_pre, w_ff1=w_ff1, w_ff2=w_ff2, norm_ffn_post=norm_ffn_post)
    m = dict(norm_mix_pre=m_norm_mix_pre, w_in=m_w_in, b_gate=m_b_gate, ln_v_g=m_ln_v_g, ln_v_b=m_ln_v_b, w_s=m_w_s,
             b_s=m_b_s, w_a_proj=m_w_a_proj, w_b_proj=m_w_b_proj, w_out=m_w_out, norm_mix_post=m_norm_mix_post,
             norm_ffn_pre=m_norm_ffn_pre, w_ff1=m_w_ff1, w_ff2=m_w_ff2, norm_ffn_post=m_norm_ffn_post)
    v = dict(norm_mix_pre=v_norm_mix_pre, w_in=v_w_in, b_gate=v_b_gate, ln_v_g=v_ln_v_g, ln_v_b=v_ln_v_b, w_s=v_w_s,
             b_s=v_b_s, w_a_proj=v_w_a_proj, w_b_proj=v_w_b_proj, w_out=v_w_out, norm_mix_post=v_norm_mix_post,
             norm_ffn_pre=v_norm_ffn_pre, w_ff1=v_w_ff1, w_ff2=v_w_ff2, norm_ffn_post=v_norm_ffn_post)
    chip = 2 * lax.axis_index("x") + lax.axis_index("y")
    core = lax.axis_index("c")

    where = jnp.stack([chip, core]).astype(jnp.int32)
    wg_in = place_shard("place_w_in", w_in[0], where, BF16, 256)
    bg_all = place_shard("place_b_gate", jnp.pad(b_gate[0], ((0, 14), (0, 0))), where, F32, 16)
    vecs = (norm_mix_pre, ln_v_g, ln_v_b, norm_mix_post, norm_ffn_pre, norm_ffn_post)
    loss, dx, _, small, parts, got = local_step(
        x[0], loss_target[0], vecs, w_s[0], b_s[0].T, bg_all, wg_in, [w[n][0] for n in BIG[1:]],
        core=jnp.reshape(core, (1,)).astype(jnp.int32),
        order=jnp.stack([chip, chip ^ 2, chip ^ 1, chip ^ 3]).astype(jnp.int32))

    halves = [sum_chips("sum_" + n, p, r, where, min(p.shape[1], 256)) for n, p, r in zip(BIG, parts, got)]
    grads = dict(zip(BIG, join_halves(halves)))

    summed = allreduce_small(pack_small(small, loss.reshape(1, 1)))
    loss = summed[LOSS_ROW, 0]

    new = adamw_small(summed, jnp.reshape(chip, (1,)).astype(jnp.int32), w, m, v)
    for n in BIG:
        shape = w[n].shape
        res = adamw("adamw_" + n, w[n][0], grads[n], m[n][0], v[n][0], min(shape[1], 256))
        new[n] = tuple(r.reshape(shape) for r in res)

    outs = [loss, dx[None]]
    for i in range(4):
        outs += [new[n][i] for n in ORDER]
    return tuple(outs)
```

```python
import functools
import math
import typing

import numpy as np
import jax
import jax.numpy as jnp
from jax import lax
from jax.experimental import pallas as pl
from jax.experimental.pallas import tpu as pltpu

F32 = jnp.float32
BF16 = jnp.bfloat16
MESH = pl.DeviceIdType.MESH

D = 1024
EPS = 1e-6
CHUNK = 128
GROUPS = 8
HEADS = 16
HEAD_DIM = 64
ATT_T = 256
ATT_GROUP = 8
ATT_BWD_GROUP = 2
N_CHIPS = 4
D_FF = 4 * D
IN_COLS = 7 * D
IN_SHARD = IN_COLS // N_CHIPS
MASKED = -1e30
VMEM_LIMIT = 56 * 2 ** 20

ADAM_LR, ADAM_B1, ADAM_B2, ADAM_EPS, ADAM_WD, ADAM_STEP = 0.001, 0.9, 0.999, 1e-08, 0.01, 10

NN = (((1,), (0,)), ((), ()))
NT = (((1,), (1,)), ((), ()))
TN = (((0,), (0,)), ((), ()))


def _dot(a, b, dims=NN):
    return lax.dot_general(a, b, dims, preferred_element_type=F32)


def _params(*sem, communicates=False):
    return pltpu.CompilerParams(dimension_semantics=sem or None, vmem_limit_bytes=VMEM_LIMIT,
                                has_side_effects=communicates)


def _rows(tr, c, col=0):
    return pl.BlockSpec((tr, c), lambda i: (i, col))


def _full(shape):
    n = len(shape)
    return pl.BlockSpec(shape, lambda *_: (0,) * n)


def _gelu(x):
    k = math.sqrt(2.0 / math.pi)
    return 0.5 * x * (1.0 + jnp.tanh(k * (x + 0.044715 * x * x * x)))


def _gelu_and_grad(x):
    k = math.sqrt(2.0 / math.pi)
    t = jnp.tanh(k * (x + 0.044715 * x * x * x))
    g = 0.5 * x * (1.0 + t)
    dg = 0.5 * (1.0 + t) + 0.5 * x * (1.0 - t * t) * (k * (1.0 + 3.0 * 0.044715 * x * x))
    return g, dg


def _sigmoid(x):
    return 1.0 / (1.0 + jnp.exp(-x))


def _rms(x):
    r = lax.rsqrt(jnp.mean(x * x, axis=-1, keepdims=True) + EPS)
    return x * r, r


def _rms_bwd(dn, xhat, r):
    return r * (dn - xhat * jnp.mean(dn * xhat, axis=-1, keepdims=True))


def norm_pre(x, g):
    s = x.shape[0]
    tr = 512

    def body(x_ref, g_ref, h_ref):
        xhat, _ = _rms(x_ref[...])
        h_ref[...] = (xhat * g_ref[...]).astype(BF16)

    return pl.pallas_call(
        body, name="norm_pre", grid=(s // tr,),
        in_specs=[_rows(tr, D), _full((1, D))], out_specs=_rows(tr, D),
        out_shape=jax.ShapeDtypeStruct((s, D), BF16), compiler_params=_params("parallel"),
    )(x, g)


def mm_in(name, h, wg, order, first, count, z, gathering, relay=False, casting=()):
    s = h.shape[0]
    tm, tn = 1024, IN_SHARD // 2
    per = IN_SHARD // tn
    n, m = len(gathering), len(casting)
    nj, ni = count * per, s // tm
    has_z = z is not None
    arrays = _arrays(gathering)
    at = [k for k, a in enumerate(arrays) if a is wg][0]

    def body(order_ref, *refs):
        a_ref = refs[0]
        cast_in = refs[1 + has_z + n:1 + has_z + n + m]
        o_ref = refs[1 + has_z + n + m]
        held = refs[2 + has_z + n + m:2 + has_z + 2 * n + m]
        cast_out = refs[2 + has_z + 2 * n + m:2 + has_z + 2 * n + 2 * m]
        tile, tile_sem = refs[2 + has_z + 2 * n + 2 * m:4 + has_z + 2 * n + 2 * m]
        j, i = pl.program_id(0), pl.program_id(1)
        sems = refs[4 + has_z + 2 * n + 2 * m:]
        for src, dst in zip(cast_in, cast_out):
            dst[...] = src[...].astype(BF16)
        phases = [_gather_phases(held, *sems[:2], _spans(gathering))]
        if relay:
            phases.append(_relay_phases(held[at], *sems[2:]))
        def each(fs):
            def run():
                for f in fs:
                    f()
            return run

        send, pass_on, finish = [each(fs) for fs in zip(*phases)]

        def fetch(t):
            chip = order_ref[first + t // per]
            return pltpu.make_async_copy(held[at].at[chip, :, pl.ds((t % per) * tn, tn)], tile.at[t % 2],
                                         tile_sem.at[t % 2])

        @pl.when(i == 0)
        def _():
            @pl.when(j == 0)
            def _():
                send()
                fetch(0).start()

            fetch(j).wait()

            @pl.when(j + 1 < nj)
            def _():
                fetch(j + 1).start()

        pl.when((j == nj - 1) & (i == 0))(pass_on)
        rows = pl.ds(pl.multiple_of(i * tm, tm), tm)
        o_ref[...] = _dot(a_ref[rows, :], tile[j % 2]).astype(BF16)
        pl.when((j == nj - 1) & (i == ni - 1))(finish)

    steps = nj * ni
    out = pl.pallas_call(
        body, name=name,
        grid_spec=pltpu.PrefetchScalarGridSpec(
            num_scalar_prefetch=1, grid=(nj, ni),
            in_specs=[pl.BlockSpec((s, D), lambda j, i, o: (0, 0))] + [ANY] * (has_z + n)
            + [pl.BlockSpec((a.shape[0] // steps, a.shape[1]), lambda j, i, o: (j * ni + i, 0)) for a in casting],
            out_specs=[pl.BlockSpec((tm, tn), lambda j, i, o: (i, o[first + j // per] * per + j % per))] + [ANY] * n
            + [pl.BlockSpec((None, a.shape[0] // steps, a.shape[1]), lambda j, i, o: (o[0], j * ni + i, 0))
               for a in casting],
            scratch_shapes=[pltpu.VMEM((2, D, tn), BF16), pltpu.SemaphoreType.DMA((2,))] + _gather_sems(n)
            + (_relay_sems() if relay else [])),
        out_shape=[jax.ShapeDtypeStruct((s, IN_COLS), BF16)] + [jax.ShapeDtypeStruct(a.shape, a.dtype) for a in arrays]
        + [jax.ShapeDtypeStruct((N_CHIPS,) + a.shape, BF16) for a in casting],
        input_output_aliases={**({2: 0} if has_z else {}), **{2 + has_z + w: 1 + w for w in range(n)}},
        compiler_params=_params("arbitrary", "arbitrary", communicates=True),
    )(order, h, *([z] if has_z else []), *arrays, *casting)
    return out[0], out[1:1 + n], out[1 + n:]


def _tril_ws(ws_ref, g):
    r = lax.broadcasted_iota(jnp.int32, (CHUNK, CHUNK), 0)
    c = lax.broadcasted_iota(jnp.int32, (CHUNK, CHUNK), 1)
    return jnp.where(c <= r, ws_ref[g], 0.0).astype(BF16)


def _layer_norm(v):
    mu = jnp.mean(v, axis=-1, keepdims=True)
    d = v - mu
    rstd = lax.rsqrt(jnp.mean(d * d, axis=-1, keepdims=True) + EPS)
    return d * rstd, rstd


def gating_fwd(z, ln_g, ln_b, w_s, bs_t, gathering):
    s = z.shape[0]
    n = len(gathering)
    steps = s // CHUNK

    def body(*refs):
        u_ref, v_ref, lg_ref, lb_ref, ws_ref, bst_ref = refs[:6]
        ya_ref = refs[6 + n]
        ci = pl.program_id(0)
        if n:
            send, pass_on, finish = _gather_phases(refs[7 + n:7 + 2 * n], *refs[7 + 2 * n:], _spans(gathering))
            pl.when(ci == 0)(send)
            pl.when(ci == steps * 3 // 4)(pass_on)
        ug = _gelu(u_ref[...].astype(F32))
        vhat, _ = _layer_norm(_gelu(v_ref[...].astype(F32)))
        vn = (vhat * lg_ref[...] + lb_ref[...]).astype(BF16)
        for g in range(GROUPS):
            cols = slice(g * CHUNK, (g + 1) * CHUNK)
            mixed = _dot(_tril_ws(ws_ref, g), vn[:, cols]) + bst_ref[:, g:g + 1]
            ya_ref[:, cols] = (ug[:, cols] * mixed).astype(BF16)
        if n:
            pl.when(ci == steps - 1)(finish)

    out = pl.pallas_call(
        body, name="gating_fwd", grid=(steps,),
        in_specs=[_rows(CHUNK, D, 0), _rows(CHUNK, D, 1), _full((1, D)), _full((1, D)),
                  _full((GROUPS, CHUNK, CHUNK)), _full((CHUNK, GROUPS))] + [ANY] * n,
        out_specs=[_rows(CHUNK, D)] + [ANY] * n,
        out_shape=[jax.ShapeDtypeStruct((s, D), BF16)]
        + [jax.ShapeDtypeStruct(a.shape, a.dtype) for a in _arrays(gathering)],
        input_output_aliases={6 + w: 1 + w for w in range(n)},
        scratch_shapes=_gather_sems(n) if n else [],
        compiler_params=_params("arbitrary", communicates=bool(n)),
    )(z, z, ln_g, ln_b, w_s, bs_t, *_arrays(gathering))
    return out[0], out[1:]


def _attn_tables(s):
    nd = s // ATT_T
    r = np.arange(ATT_T)[None, :, None]
    c = np.arange(ATT_T)[None, None, :]
    delta = np.arange(nd)[:, None, None] * ATT_T + r - c
    count = np.zeros(delta.shape, np.int64)
    for window, dilation in ((128, 1), (512, 4), (2048, 16)):
        count += (delta >= 0) & (delta % dilation == 0) & (delta <= window)
    logc = np.where(count > 0, np.log(np.maximum(count, 1)), MASKED)
    return jnp.asarray(logc, F32)


AUG = 3


def _split3_np(x):
    terms, rest = [], np.asarray(x, np.float64)
    for _ in range(AUG):
        term = np.asarray(rest.astype(jnp.bfloat16), np.float64)
        terms.append(term)
        rest = rest - term
    return terms


def _split3(x):
    terms, rest = [], x
    for _ in range(AUG):
        term = rest.astype(BF16).astype(F32)
        terms.append(term)
        rest = rest - term
    return terms


def _alibi_tables(s):
    nb = s // ATT_T
    slopes = np.exp2(-8.0 * np.arange(1, HEADS + 1, dtype=np.float64) / HEADS)
    ka = np.zeros((HEADS // 2, 2, ATT_T, 128), np.float32)
    kb = np.zeros((HEADS // 2, 2, nb, 128), np.float32)
    for p in range(HEADS // 2):
        for e in range(2):
            base = HEAD_DIM * (1 - e)
            for a, term in enumerate(_split3_np(slopes[2 * p + e] * np.arange(ATT_T))):
                ka[p, e, :, base + a] = term
            for a, term in enumerate(_split3_np(slopes[2 * p + e] * ATT_T * np.arange(nb))):
                kb[p, e, :, base + AUG + a] = term
            ka[p, e, :, base + 2 * AUG:base + 3 * AUG] = 1.0
    return jnp.asarray(ka), jnp.asarray(kb)


def _head_masks():
    lane = lax.broadcasted_iota(jnp.int32, (1, 128), 1)
    first = lane < HEAD_DIM

    def ones(e, n):
        base = HEAD_DIM * (1 - e)
        return ((lane >= base) & (lane < base + n)).astype(F32)

    return first, lane, ones


def _place3(lane, at, terms, other):
    for a, term in enumerate(terms):
        other = jnp.where(lane == at + a, term, other)
    return other


def attn_fwd(z, logc, ka, kb, gathering):
    s = z.shape[0]
    nq = s // ATT_T
    t = ATT_T
    n = len(gathering)
    grp = ATT_GROUP
    ngrp = HEADS // 2 // grp
    wide = 128 * grp
    qcol, kcol, vcol = 2 * D // wide, 3 * D // wide, 4 * D // wide

    def body(*refs):
        q_ref, k_ref, v_ref, lc_ref, ka_ref, kb_ref = refs[:6]
        y_ref, lse_ref = refs[6 + n:8 + n]
        q_s, k_s, v_s, m_s, l_s, acc_s = refs[8 + 2 * n:14 + 2 * n]
        gi, qi = pl.program_id(0), pl.program_id(1)
        first, lane, ones = _head_masks()
        if n:
            send, pass_on, finish = _gather_phases(refs[8 + n:8 + 2 * n], *refs[14 + 2 * n:], _spans(gathering))
            pl.when((gi == 0) & (qi == 0))(send)
            pl.when((gi == ngrp - 1) & (qi == nq * 3 // 4))(pass_on)

        @pl.when(qi == 0)
        def _():
            sel = jnp.broadcast_to(first.astype(F32), (t, 128))
            for pr in range(grp):
                cols = slice(pr * 128, (pr + 1) * 128)
                for jb in range(nq):
                    kj = k_ref[jb * t:(jb + 1) * t, cols].astype(F32)
                    vj = v_ref[jb * t:(jb + 1) * t, cols].astype(F32)
                    k_s[pr, 0, jb] = jnp.where(first, kj, ka_ref[pr, 0] + kb_ref[pr, 0, jb:jb + 1, :]).astype(BF16)
                    k_s[pr, 1, jb] = jnp.where(first, ka_ref[pr, 1] + kb_ref[pr, 1, jb:jb + 1, :], kj).astype(BF16)
                    v_s[pr, jb, 0:t, 0:128] = jnp.where(first, vj, 0.0).astype(BF16)
                    v_s[pr, jb, t:2 * t, 0:128] = jnp.where(first, 0.0, vj).astype(BF16)
                    v_s[pr, jb, 0:t, 128:256] = sel.astype(BF16)
                    v_s[pr, jb, t:2 * t, 128:256] = (1.0 - sel).astype(BF16)

        for pr in range(grp):
            q = q_ref[:, pr * 128:(pr + 1) * 128].astype(F32) * (1.0 / math.sqrt(HEAD_DIM))
            q_s[pr, 0] = jnp.where(first, q, ones(0, 2 * AUG)).astype(BF16)
            q_s[pr, 1] = jnp.where(first, ones(1, 2 * AUG), q).astype(BF16)
        m_s[...] = jnp.full_like(m_s, MASKED)
        l_s[...] = jnp.zeros_like(l_s)
        acc_s[...] = jnp.zeros_like(acc_s)

        def scores(j):
            return tuple(_dot(q_s[pr, e], k_s[pr, e, j], NT) for pr in range(grp) for e in range(2))

        def step(j, carry):
            softmax_block(j, scores(j))
            return carry

        def softmax_block(j, u):
            lc = lc_ref[qi - j]
            for pr in range(grp):
                u0 = u[2 * pr] + lc
                u1 = u[2 * pr + 1] + lc
                m0, m1 = m_s[pr, 0], m_s[pr, 1]
                n0 = jnp.maximum(m0, jnp.max(u0, axis=-1, keepdims=True))
                n1 = jnp.maximum(m1, jnp.max(u1, axis=-1, keepdims=True))
                m_s[pr, 0], m_s[pr, 1] = n0, n1
                p = jnp.concatenate([jnp.exp(u0 - jnp.concatenate([n0, n0], axis=1)).astype(BF16),
                                     jnp.exp(u1 - jnp.concatenate([n1, n1], axis=1)).astype(BF16)], axis=1)
                pv = _dot(p, v_s[pr, j])
                alpha = jnp.where(first, jnp.exp(m0 - n0), jnp.exp(m1 - n1))
                acc_s[pr] = acc_s[pr] * alpha + pv[:, 0:128]
                l_s[pr] = l_s[pr] * alpha + pv[:, 128:256]

        lax.fori_loop(0, qi + 1, step, 0)
        for pr in range(grp):
            cols = slice(pr * 128, (pr + 1) * 128)
            y_ref[:, cols] = (acc_s[pr] / l_s[pr]).astype(BF16)
            lse_ref[:, cols] = jnp.where(first, m_s[pr, 0], m_s[pr, 1]) + jnp.log(l_s[pr])
        if n:
            pl.when((gi == ngrp - 1) & (qi == nq - 1))(finish)

    out = pl.pallas_call(
        body, name="attn_fwd", grid=(ngrp, nq),
        in_specs=[pl.BlockSpec((t, wide), lambda g, i: (i, qcol + g)),
                  pl.BlockSpec((s, wide), lambda g, i: (0, kcol + g)),
                  pl.BlockSpec((s, wide), lambda g, i: (0, vcol + g)),
                  _full((nq, t, t)),
                  pl.BlockSpec((grp, 2, t, 128), lambda g, i: (g, 0, 0, 0)),
                  pl.BlockSpec((grp, 2, nq, 128), lambda g, i: (g, 0, 0, 0))] + [ANY] * n,
        out_specs=[pl.BlockSpec((t, wide), lambda g, i: (i, g)), pl.BlockSpec((t, wide), lambda g, i: (i, g))]
        + [ANY] * n,
        out_shape=[jax.ShapeDtypeStruct((s, D), BF16), jax.ShapeDtypeStruct((s, D), F32)]
        + [jax.ShapeDtypeStruct(a.shape, a.dtype) for a in _arrays(gathering)],
        input_output_aliases={6 + w: 2 + w for w in range(n)},
        scratch_shapes=[pltpu.VMEM((grp, 2, t, 128), BF16), pltpu.VMEM((grp, 2, nq, t, 128), BF16),
                        pltpu.VMEM((grp, nq, 2 * t, 256), BF16), pltpu.VMEM((grp, 2, t, 128), F32),
                        pltpu.VMEM((grp, t, 128), F32), pltpu.VMEM((grp, t, 128), F32)]
        + (_gather_sems(n) if n else []),
        compiler_params=_params("arbitrary", "arbitrary", communicates=bool(n)),
    )(z, z, z, logc, ka, kb, *_arrays(gathering))
    return out[0], out[1], out[2:]


def proj_merge(ya, yb, wa, wb, z, bg, gathering):
    s = ya.shape[0]
    tm = 512
    n = len(gathering)
    steps = s // tm

    def body(*refs):
        ya_ref, yb_ref, wa_ref, wb_ref, ga_ref, gb_ref, bg_ref = refs[:7]
        mg_ref, pa_ref, pb_ref = refs[7 + n:10 + n]
        i = pl.program_id(0)
        if n:
            send, pass_on, finish = _gather_phases(refs[10 + n:10 + 2 * n], *refs[10 + 2 * n:], _spans(gathering))
            pl.when(i == 0)(send)
            pl.when(i == steps - 1)(pass_on)
        pa = _dot(ya_ref[...], wa_ref[...])
        pb = _dot(yb_ref[...], wb_ref[...])
        sa = _sigmoid(ga_ref[...] + bg_ref[0:1, :])
        sb = _sigmoid(gb_ref[...] + bg_ref[1:2, :])
        mg_ref[...] = (sa * pa + sb * pb).astype(BF16)
        pa_ref[...] = pa.astype(BF16)
        pb_ref[...] = pb.astype(BF16)
        if n:
            pl.when(i == steps - 1)(finish)

    out = jax.ShapeDtypeStruct((s, D), BF16)
    res = pl.pallas_call(
        body, name="proj_merge", grid=(steps,),
        in_specs=[_rows(tm, D), _rows(tm, D), _full((D, D)), _full((D, D)),
                  _rows(tm, D, 5), _rows(tm, D, 6), _full((2, D))] + [ANY] * n,
        out_specs=[_rows(tm, D)] * 3 + [ANY] * n,
        out_shape=[out] * 3 + [jax.ShapeDtypeStruct(a.shape, a.dtype) for a in _arrays(gathering)],
        input_output_aliases={7 + w: 3 + w for w in range(n)},
        scratch_shapes=_gather_sems(n) if n else [],
        compiler_params=_params("arbitrary", communicates=bool(n)),
    )(ya, yb, wa, wb, z, z, bg, *_arrays(gathering))
    return res[0], res[1], res[2], res[3:]


def out_norm(merged, w_out, x, g_post, g_fpre):
    s = x.shape[0]
    tm = 512

    def body(mg_ref, w_ref, x_ref, gp_ref, gf_ref, o_ref, x1_ref, h2_ref):
        o = _dot(mg_ref[...], w_ref[...])
        ohat, _ = _rms(o)
        x1 = x_ref[...] + ohat * gp_ref[...]
        x1hat, _ = _rms(x1)
        o_ref[...] = o
        x1_ref[...] = x1
        h2_ref[...] = (x1hat * gf_ref[...]).astype(BF16)

    return pl.pallas_call(
        body, name="out_norm", grid=(s // tm,),
        in_specs=[_rows(tm, D), _full((D, D)), _rows(tm, D), _full((1, D)), _full((1, D))],
        out_specs=[_rows(tm, D)] * 3,
        out_shape=[jax.ShapeDtypeStruct((s, D), F32), jax.ShapeDtypeStruct((s, D), F32),
                   jax.ShapeDtypeStruct((s, D), BF16)],
        compiler_params=_params("parallel"),
    )(merged, w_out, x, g_post, g_fpre)


def mm_ff1(h2, wg, gathering):
    s = h2.shape[0]
    tm = 1024
    n = len(gathering)
    ni = s // tm

    def body(*refs):
        a_ref, b_ref = refs[:2]
        o_ref, r_ref = refs[2 + n:4 + n]
        i, j = pl.program_id(0), pl.program_id(1)
        if n:
            send, pass_on, finish = _gather_phases(refs[4 + n:4 + 2 * n], *refs[4 + 2 * n:], _spans(gathering))
            pl.when((i == 0) & (j == 0))(send)
            pl.when((i == ni - 1) & (j == N_CHIPS // 2))(pass_on)
        a = _dot(a_ref[...], b_ref[...])
        o_ref[...] = a.astype(BF16)
        r = jnp.maximum(a, 0.0)
        r_ref[...] = (r * r).astype(BF16)
        if n:
            pl.when((i == ni - 1) & (j == N_CHIPS - 1))(finish)

    res = pl.pallas_call(
        body, name="mm_ff1", grid=(ni, N_CHIPS),
        in_specs=[pl.BlockSpec((tm, D), lambda i, j: (i, 0)), pl.BlockSpec((None, D, D), lambda i, j: (j, 0, 0))]
        + [ANY] * n,
        out_specs=[pl.BlockSpec((tm, D), lambda i, j: (i, j))] * 2 + [ANY] * n,
        out_shape=[jax.ShapeDtypeStruct((s, D_FF), BF16), jax.ShapeDtypeStruct((s, D_FF), BF16)]
        + [jax.ShapeDtypeStruct(a.shape, a.dtype) for a in _arrays(gathering)],
        input_output_aliases={2 + w: 2 + w for w in range(n)},
        scratch_shapes=_gather_sems(n) if n else [],
        compiler_params=_params("arbitrary", "arbitrary", communicates=bool(n)),
    )(h2, wg, *_arrays(gathering))
    return res[0], res[1], res[2:]


def ff2_loss(rl, w_ff2, x1, target, g_fpost):
    s = x1.shape[0]
    tm = 256

    def body(rl_ref, w_ref, x1_ref, t_ref, g_ref, dy_ref, df_ref, dg_ref, loss_ref):
        @pl.when(pl.program_id(0) == 0)
        def _():
            dg_ref[...] = jnp.zeros_like(dg_ref)
            loss_ref[...] = jnp.zeros_like(loss_ref)

        f = _dot(rl_ref[...], w_ref[...])
        fhat, r = _rms(f)
        err = x1_ref[...] + fhat * g_ref[...] - t_ref[...]
        loss_ref[...] += 0.5 * jnp.sum(jnp.mean(err * err, axis=-1, keepdims=True), axis=0, keepdims=True)
        dy = err * (1.0 / D)
        dy_ref[...] = dy
        dg_ref[...] += jnp.sum(dy * fhat, axis=0, keepdims=True)
        df_ref[...] = _rms_bwd(dy * g_ref[...], fhat, r).astype(BF16)

    return pl.pallas_call(
        body, name="ff2_loss", grid=(s // tm,),
        in_specs=[_rows(tm, D_FF), _full((D_FF, D)), _rows(tm, D), _rows(tm, D), _full((1, D))],
        out_specs=[_rows(tm, D), _rows(tm, D), _full((1, D)), _full((1, 1))],
        out_shape=[jax.ShapeDtypeStruct((s, D), F32), jax.ShapeDtypeStruct((s, D), BF16),
                   jax.ShapeDtypeStruct((1, D), F32), jax.ShapeDtypeStruct((1, 1), F32)],
        compiler_params=_params("arbitrary"),
    )(rl, w_ff2, x1, target, g_fpost)


def mm_tn(name, a, b, ta, tb, out_shape, out_spec):
    s = a.shape[0]

    def body(a_ref, b_ref, o_ref):
        o_ref[...] = _dot(a_ref[...], b_ref[...], TN)

    return pl.pallas_call(
        body, name=name, grid=(a.shape[1] // ta, b.shape[1] // tb),
        in_specs=[pl.BlockSpec((s, ta), lambda i, j: (0, i)), pl.BlockSpec((s, tb), lambda i, j: (0, j))],
        out_specs=out_spec, out_shape=jax.ShapeDtypeStruct(out_shape, F32),
        compiler_params=_params("parallel", "parallel"),
    )(a, b)


def mm_nt(name, a, w):
    s = a.shape[0]
    tm = 512

    def body(a_ref, w_ref, o_ref):
        o_ref[...] = _dot(a_ref[...], w_ref[...], NT).astype(BF16)

    return pl.pallas_call(
        body, name=name, grid=(s // tm,), in_specs=[_rows(tm, D), _full((D, D))], out_specs=_rows(tm, D),
        out_shape=jax.ShapeDtypeStruct((s, D), BF16), compiler_params=_params("parallel"),
    )(a, w)


def ff2_bwd(df, w_ff2, a):
    s = df.shape[0]
    tm = 1024

    def body(df_ref, w_ref, a_ref, da_ref):
        drl = _dot(df_ref[...], w_ref[...], NT)
        da_ref[...] = (drl * (2.0 * jnp.maximum(a_ref[...].astype(F32), 0.0))).astype(BF16)

    return pl.pallas_call(
        body, name="ff2_bwd", grid=(s // tm, D_FF // D),
        in_specs=[pl.BlockSpec((tm, D), lambda i, j: (i, 0)), pl.BlockSpec((D, D), lambda i, j: (j, 0)),
                  pl.BlockSpec((tm, D), lambda i, j: (i, j))],
        out_specs=pl.BlockSpec((tm, D), lambda i, j: (i, j)),
        out_shape=jax.ShapeDtypeStruct((s, D_FF), BF16), compiler_params=_params("parallel", "parallel"),
    )(df, w_ff2, a)


def ff1_bwd_norms(da, wg, x1, o, dy, g_fpre, g_post, swapping):
    s = x1.shape[0]
    tm = 512
    n = len(swapping)

    def body(*refs):
        da_ref, w_ref, x1_ref, o_ref, dy_ref, gf_ref, gp_ref = refs[:7]
        dx1_ref, do_ref, dgf_ref, dgp_ref = refs[7 + n:11 + n]
        acc_ref = refs[11 + 2 * n]
        i, k = pl.program_id(0), pl.program_id(1)
        if n:
            send, finish = _swap_phases(refs[7:7 + n], refs[11 + n:11 + 2 * n], *refs[12 + 2 * n:])
            pl.when((i == 0) & (k == 0))(send)

        @pl.when((i == 0) & (k == 0))
        def _():
            dgf_ref[...] = jnp.zeros_like(dgf_ref)
            dgp_ref[...] = jnp.zeros_like(dgp_ref)

        part = _dot(da_ref[...], w_ref[...], NT)

        @pl.when(k == 0)
        def _():
            acc_ref[...] = part

        @pl.when(k > 0)
        def _():
            acc_ref[...] += part

        @pl.when(k == N_CHIPS - 1)
        def _():
            dh2 = acc_ref[...]
            x1hat, r2 = _rms(x1_ref[...])
            dgf_ref[...] += jnp.sum(dh2 * x1hat, axis=0, keepdims=True)
            dx1 = dy_ref[...] + _rms_bwd(dh2 * gf_ref[...], x1hat, r2)
            ohat, r1 = _rms(o_ref[...])
            dgp_ref[...] += jnp.sum(dx1 * ohat, axis=0, keepdims=True)
            dx1_ref[...] = dx1
            do_ref[...] = _rms_bwd(dx1 * gp_ref[...], ohat, r1).astype(BF16)

        if n:
            pl.when((i == s // tm - 1) & (k == N_CHIPS - 1))(finish)

    row = pl.BlockSpec((tm, D), lambda i, k: (i, 0))
    vec = pl.BlockSpec((1, D), lambda i, k: (0, 0))
    res = pl.pallas_call(
        body, name="ff1_bwd_norms", grid=(s // tm, N_CHIPS),
        in_specs=[pl.BlockSpec((tm, D), lambda i, k: (i, k)), pl.BlockSpec((None, D, D), lambda i, k: (k, 0, 0)),
                  row, row, row, vec, vec] + [ANY] * n,
        out_specs=[row, row, vec, vec] + [ANY] * n,
        out_shape=[jax.ShapeDtypeStruct((s, D), F32), jax.ShapeDtypeStruct((s, D), BF16),
                   jax.ShapeDtypeStruct((1, D), F32), jax.ShapeDtypeStruct((1, D), F32)] + _swap_shapes(swapping),
        scratch_shapes=[pltpu.VMEM((tm, D), F32)] + (_swap_sems(n) if n else []),
        compiler_params=_params("arbitrary", "arbitrary", communicates=bool(n)),
    )(da, wg, x1, o, dy, g_fpre, g_post, *swapping)
    return res[0], res[1], res[2], res[3], res[4:]


def out_bwd_gates(do, w_out, pa, pb, z, bg):
    s = do.shape[0]
    tm = 512

    def body(do_ref, w_ref, pa_ref, pb_ref, ga_ref, gb_ref, bg_ref, dpa_ref, dpb_ref, dga_ref, dgb_ref, dbg_ref):
        @pl.when(pl.program_id(0) == 0)
        def _():
            dbg_ref[...] = jnp.zeros_like(dbg_ref)

        dm = _dot(do_ref[...], w_ref[...], NT)
        sa = _sigmoid(ga_ref[...] + bg_ref[0:1, :])
        sb = _sigmoid(gb_ref[...] + bg_ref[1:2, :])
        dpa_ref[...] = (dm * sa).astype(BF16)
        dpb_ref[...] = (dm * sb).astype(BF16)
        dga = dm * pa_ref[...].astype(F32) * (sa * (1.0 - sa))
        dgb = dm * pb_ref[...].astype(F32) * (sb * (1.0 - sb))
        dga_ref[...] = dga.astype(BF16)
        dgb_ref[...] = dgb.astype(BF16)
        dbg_ref[0:1, :] += jnp.sum(dga, axis=0, keepdims=True)
        dbg_ref[1:2, :] += jnp.sum(dgb, axis=0, keepdims=True)

    out = jax.ShapeDtypeStruct((s, D), BF16)
    return pl.pallas_call(
        body, name="out_bwd_gates", grid=(s // tm,),
        in_specs=[_rows(tm, D), _full((D, D)), _rows(tm, D), _rows(tm, D), _rows(tm, D, 5), _rows(tm, D, 6),
                  _full((2, D))],
        out_specs=[_rows(tm, D)] * 4 + [_full((2, D))],
        out_shape=[out] * 4 + [jax.ShapeDtypeStruct((2, D), F32)], compiler_params=_params("arbitrary"),
    )(do, w_out, pa, pb, z, z, bg)


def gating_bwd(z, dya, ln_g, ln_b, w_s, bs_t, swapping):
    s = z.shape[0]
    ones = functools.partial(jnp.ones, (8, CHUNK), BF16)
    n = len(swapping)

    def body(*refs):
        u_ref, v_ref, dya_ref, lg_ref, lb_ref, ws_ref, bst_ref = refs[:7]
        du_ref, dv_ref, dws_ref, dbs_ref, dlg_ref, dlb_ref = refs[7 + n:13 + n]
        dvn_ref = refs[13 + 2 * n]
        ci = pl.program_id(0)
        if n:
            send, finish = _swap_phases(refs[7:7 + n], refs[13 + n:13 + 2 * n], *refs[14 + 2 * n:])
            pl.when(ci == 0)(send)

        @pl.when(ci == 0)
        def _():
            dws_ref[...] = jnp.zeros_like(dws_ref)
            dbs_ref[...] = jnp.zeros_like(dbs_ref)
            dlg_ref[...] = jnp.zeros_like(dlg_ref)
            dlb_ref[...] = jnp.zeros_like(dlb_ref)

        ug, dug_du = _gelu_and_grad(u_ref[...].astype(F32))
        vg, dvg_dv = _gelu_and_grad(v_ref[...].astype(F32))
        vhat, rstd = _layer_norm(vg)
        vn = (vhat * lg_ref[...] + lb_ref[...]).astype(BF16)
        dya = dya_ref[...].astype(F32)
        for g in range(GROUPS):
            cols = slice(g * CHUNK, (g + 1) * CHUNK)
            ws = _tril_ws(ws_ref, g)
            mixed = _dot(ws, vn[:, cols]) + bst_ref[:, g:g + 1]
            du_ref[:, cols] = (dya[:, cols] * mixed * dug_du[:, cols]).astype(BF16)
            dmix = (dya[:, cols] * ug[:, cols]).astype(BF16)
            dbs_ref[g] += _dot(ones(), dmix, NT)
            dws_ref[g] += _dot(dmix, vn[:, cols], NT)
            dvn_ref[:, cols] = _dot(ws, dmix, TN)
        dvn = dvn_ref[...]
        dlg_ref[...] += jnp.sum(dvn * vhat, axis=0, keepdims=True)
        dlb_ref[...] += jnp.sum(dvn, axis=0, keepdims=True)
        dvh = dvn * lg_ref[...]
        dvg = rstd * (dvh - jnp.mean(dvh, axis=-1, keepdims=True)
                      - vhat * jnp.mean(dvh * vhat, axis=-1, keepdims=True))
        dv_ref[...] = (dvg * dvg_dv).astype(BF16)

        @pl.when(ci == pl.num_programs(0) - 1)
        def _():
            r = lax.broadcasted_iota(jnp.int32, (CHUNK, CHUNK), 0)
            c = lax.broadcasted_iota(jnp.int32, (CHUNK, CHUNK), 1)
            for g in range(GROUPS):
                dws_ref[g] = jnp.where(c <= r, dws_ref[g], 0.0)

        if n:
            pl.when(ci == pl.num_programs(0) - 1)(finish)

    out = jax.ShapeDtypeStruct((s, D), BF16)
    res = pl.pallas_call(
        body, name="gating_bwd", grid=(s // CHUNK,),
        in_specs=[_rows(CHUNK, D, 0), _rows(CHUNK, D, 1), _rows(CHUNK, D), _full((1, D)), _full((1, D)),
                  _full((GROUPS, CHUNK, CHUNK)), _full((CHUNK, GROUPS))] + [ANY] * n,
        out_specs=[_rows(CHUNK, D), _rows(CHUNK, D), _full((GROUPS, CHUNK, CHUNK)), _full((GROUPS, 8, CHUNK)),
                   _full((1, D)), _full((1, D))] + [ANY] * n,
        out_shape=[out, out, jax.ShapeDtypeStruct((GROUPS, CHUNK, CHUNK), F32),
                   jax.ShapeDtypeStruct((GROUPS, 8, CHUNK), F32),
                   jax.ShapeDtypeStruct((1, D), F32), jax.ShapeDtypeStruct((1, D), F32)] + _swap_shapes(swapping),
        scratch_shapes=[pltpu.VMEM((CHUNK, D), F32)] + (_swap_sems(n) if n else []),
        compiler_params=_params("arbitrary", communicates=bool(n)),
    )(z, z, dya, ln_g, ln_b, w_s, bs_t, *swapping)
    return (*res[:6], res[6:])


def attn_bwd(z, yb, dyb, lse, logc, ka, kb, scattering):
    s = z.shape[0]
    nq = s // ATT_T
    t = ATT_T
    grp = ATT_BWD_GROUP
    ngrp = HEADS // 2 // grp
    wide = 128 * grp
    qcol, kcol, vcol = 2 * D // wide, 3 * D // wide, 4 * D // wide
    scale = 1.0 / math.sqrt(HEAD_DIM)
    n = len(scattering)

    def body(*refs):
        q_ref, k_ref, v_ref, y_ref, dy_ref, lse_ref, lc_ref, ka_ref, kb_ref = refs[:9]
        dq_ref, dk_ref, dv_ref = refs[9 + n:12 + n]
        qa_s, qt_s, da_s, dt_s, dq_s, dkt_s, dvt_s = refs[12 + 2 * n:19 + 2 * n]
        gi, j = pl.program_id(0), pl.program_id(1)
        first, lane, ones = _head_masks()
        if n:
            send, finish = _scatter_phases(refs[9:9 + n], refs[12 + n:12 + 2 * n], *refs[19 + 2 * n:])
            pl.when((gi == 0) & (j == 0))(send)

        @pl.when(j == 0)
        def _():
            dq_s[...] = jnp.zeros_like(dq_s)
            for pr in range(grp):
                cols = slice(pr * 128, (pr + 1) * 128)
                for ib in range(nq):
                    rows = slice(ib * t, (ib + 1) * t)
                    q = q_ref[rows, cols].astype(F32) * scale
                    lse = lse_ref[rows, cols]
                    qa_s[pr, 0, ib] = jnp.where(first, q, _place3(lane, HEAD_DIM + 2 * AUG, _split3(-lse[:, 0:1]),
                                                                  ones(0, 2 * AUG))).astype(BF16)
                    qa_s[pr, 1, ib] = jnp.where(
                        first, _place3(lane, 2 * AUG, _split3(-lse[:, HEAD_DIM:HEAD_DIM + 1]), ones(1, 2 * AUG)),
                        q).astype(BF16)
                    qt_s[pr, ib, :, 0:t] = jnp.where(first, q, 0.0).T.astype(BF16)
                    qt_s[pr, ib, :, t:2 * t] = jnp.where(first, 0.0, q).T.astype(BF16)
                    do = dy_ref[rows, cols].astype(F32)
                    prod = do * y_ref[rows, cols].astype(F32)
                    dd0 = jnp.sum(jnp.where(first, prod, 0.0), axis=-1, keepdims=True)
                    dd1 = jnp.sum(jnp.where(first, 0.0, prod), axis=-1, keepdims=True)
                    da_s[pr, 0, ib] = jnp.where(first, do, _place3(lane, HEAD_DIM, _split3(-dd0), 0.0)).astype(BF16)
                    da_s[pr, 1, ib] = jnp.where(first, _place3(lane, 0, _split3(-dd1), 0.0), do).astype(BF16)
                    dt_s[pr, ib, :, 0:t] = jnp.where(first, do, 0.0).T.astype(BF16)
                    dt_s[pr, ib, :, t:2 * t] = jnp.where(first, 0.0, do).T.astype(BF16)

        keys = []
        for pr in range(grp):
            kj = k_ref[:, pr * 128:(pr + 1) * 128].astype(F32)
            vj = v_ref[:, pr * 128:(pr + 1) * 128].astype(F32)
            keys.append((
                jnp.where(first, kj, ka_ref[pr, 0] + kb_ref[pr, 0, pl.ds(j, 1), :]).astype(BF16),
                jnp.where(first, ka_ref[pr, 1] + kb_ref[pr, 1, pl.ds(j, 1), :], kj).astype(BF16),
                jnp.concatenate([jnp.where(first, kj, 0.0), jnp.where(first, 0.0, kj)], axis=0).astype(BF16),
                jnp.where(first, vj, ones(0, AUG)).astype(BF16),
                jnp.where(first, ones(1, AUG), vj).astype(BF16)))
        dkt_s[...] = jnp.zeros_like(dkt_s)
        dvt_s[...] = jnp.zeros_like(dvt_s)

        def step(i, _):
            lc = lc_ref[i - j]
            rows = pl.ds(pl.multiple_of(i * t, t), t)
            for pr in range(grp):
                k0a, k1a, kst, v0a, v1a = keys[pr]
                p0 = jnp.exp(_dot(qa_s[pr, 0, i], k0a, NT) + lc)
                p1 = jnp.exp(_dot(qa_s[pr, 1, i], k1a, NT) + lc)
                e0 = (p0 * _dot(da_s[pr, 0, i], v0a, NT)).astype(BF16)
                e1 = (p1 * _dot(da_s[pr, 1, i], v1a, NT)).astype(BF16)
                dq_s[pr, rows, :] += _dot(jnp.concatenate([e0, e1], axis=1), kst)
                dvt_s[pr] += _dot(dt_s[pr, i], jnp.concatenate([p0.astype(BF16), p1.astype(BF16)], axis=0))
                dkt_s[pr] += _dot(qt_s[pr, i], jnp.concatenate([e0, e1], axis=0))
            return 0

        lax.fori_loop(j, nq, step, 0)
        for pr in range(grp):
            dk_ref[:, pr * 128:(pr + 1) * 128] = dkt_s[pr].T.astype(BF16)
            dv_ref[:, pr * 128:(pr + 1) * 128] = dvt_s[pr].T.astype(BF16)

        @pl.when(j == nq - 1)
        def _():
            for pr in range(grp):
                dq_ref[:, pr * 128:(pr + 1) * 128] = (dq_s[pr] * scale).astype(BF16)

        if n:
            pl.when((gi == ngrp - 1) & (j == nq - 1))(finish)

    colblock = lambda c: pl.BlockSpec((s, wide), lambda g, j: (0, c + g))
    blk = lambda c: pl.BlockSpec((t, wide), lambda g, j: (j, c + g))
    out = jax.ShapeDtypeStruct((s, D), BF16)
    res = pl.pallas_call(
        body, name="attn_bwd", grid=(ngrp, nq),
        in_specs=[colblock(qcol), blk(kcol), blk(vcol), colblock(0), colblock(0), colblock(0),
                  _full((nq, t, t)), pl.BlockSpec((grp, 2, t, 128), lambda g, j: (g, 0, 0, 0)),
                  pl.BlockSpec((grp, 2, nq, 128), lambda g, j: (g, 0, 0, 0))] + [ANY] * n,
        out_specs=[colblock(0), blk(0), blk(0)] + [ANY] * n, out_shape=[out] * 3 + _scatter_shapes(scattering),
        scratch_shapes=[pltpu.VMEM((grp, 2, nq, t, 128), BF16), pltpu.VMEM((grp, nq, 128, 2 * t), BF16),
                        pltpu.VMEM((grp, 2, nq, t, 128), BF16), pltpu.VMEM((grp, nq, 128, 2 * t), BF16),
                        pltpu.VMEM((grp, s, 128), F32), pltpu.VMEM((grp, 128, t), F32),
                        pltpu.VMEM((grp, 128, t), F32)]
        + (_scatter_sems(n) if n else []),
        compiler_params=_params("arbitrary", "arbitrary", communicates=bool(n)),
    )(z, z, z, yb, dyb, lse, logc, ka, kb, *scattering)
    return res[0], res[1], res[2], res[3:]


def in_bwd_norm(dz, wg, x, dx1, g_pre, scattering):
    s = x.shape[0]
    tm = 512
    n = len(scattering)

    def body(*refs):
        dz_ref, w_ref, x_ref, dx1_ref, g_ref = refs[:5]
        dx_ref, dg_ref = refs[5 + n:7 + n]
        acc_ref = refs[7 + 2 * n]
        i, k = pl.program_id(0), pl.program_id(1)
        if n:
            send, finish = _scatter_phases(refs[5:5 + n], refs[7 + n:7 + 2 * n], *refs[8 + 2 * n:])
            pl.when((i == 0) & (k == 0))(send)

        @pl.when((i == 0) & (k == 0))
        def _():
            dg_ref[...] = jnp.zeros_like(dg_ref)

        part = _dot(dz_ref[...], w_ref[...], NT)

        @pl.when(k == 0)
        def _():
            acc_ref[...] = part

        @pl.when(k > 0)
        def _():
            acc_ref[...] += part

        @pl.when(k == N_CHIPS - 1)
        def _():
            dh = acc_ref[...]
            xhat, r = _rms(x_ref[...])
            dg_ref[...] += jnp.sum(dh * xhat, axis=0, keepdims=True)
            dx_ref[...] = dx1_ref[...] + _rms_bwd(dh * g_ref[...], xhat, r)

        if n:
            pl.when((i == s // tm - 1) & (k == N_CHIPS - 1))(finish)

    row = pl.BlockSpec((tm, D), lambda i, k: (i, 0))
    vec = pl.BlockSpec((1, D), lambda i, k: (0, 0))
    res = pl.pallas_call(
        body, name="in_bwd_norm", grid=(s // tm, N_CHIPS),
        in_specs=[pl.BlockSpec((tm, IN_SHARD), lambda i, k: (i, k)),
                  pl.BlockSpec((None, D, IN_SHARD), lambda i, k: (k, 0, 0)), row, row, vec] + [ANY] * n,
        out_specs=[row, vec] + [ANY] * n,
        out_shape=[jax.ShapeDtypeStruct((s, D), F32), jax.ShapeDtypeStruct((1, D), F32)]
        + _scatter_shapes(scattering),
        scratch_shapes=[pltpu.VMEM((tm, D), F32)] + (_scatter_sems(n) if n else []),
        compiler_params=_params("arbitrary", "arbitrary", communicates=bool(n)),
    )(dz, wg, x, dx1, g_pre, *scattering)
    return res[0], res[1], res[2:]


def _adamw_math(w, g, m, v):
    m = ADAM_B1 * m + (1.0 - ADAM_B1) * g
    v = ADAM_B2 * v + (1.0 - ADAM_B2) * (g * g)
    m_hat = m / (1.0 - ADAM_B1 ** ADAM_STEP)
    v_hat = v / (1.0 - ADAM_B2 ** ADAM_STEP)
    delta = -ADAM_LR * (m_hat / (jnp.sqrt(v_hat) + ADAM_EPS) + ADAM_WD * w)
    return delta, m, v


def adamw(name, w, g, m, v, tr):
    r, c = w.shape

    def body(w_ref, g_ref, m_ref, v_ref, go_ref, d_ref, nm_ref, nv_ref):
        g = g_ref[...]
        go_ref[...] = g
        d_ref[...], nm_ref[...], nv_ref[...] = _adamw_math(w_ref[...], g, m_ref[...], v_ref[...])

    out = jax.ShapeDtypeStruct((r, c), F32)
    return pl.pallas_call(
        body, name=name, grid=(r // tr,), in_specs=[_rows(tr, c)] * 4, out_specs=[_rows(tr, c)] * 4,
        out_shape=[out] * 4, compiler_params=_params("parallel"),
    )(w, g, m, v)


def add_halves(name, g, recv, c_idx, tr):
    n, h, c = recv.shape

    def body(c_ref, g_ref, r_ref, o_ref):
        o_ref[...] = (g_ref[...] + r_ref[...]).astype(BF16)

    nb = h // tr
    return pl.pallas_call(
        body, name=name,
        grid_spec=pltpu.PrefetchScalarGridSpec(
            num_scalar_prefetch=1, grid=(n, nb),
            in_specs=[pl.BlockSpec((None, tr, c), lambda k, i, c_ref: (k, c_ref[0] * nb + i, 0)),
                      pl.BlockSpec((None, tr, c), lambda k, i, c_ref: (k, i, 0))],
            out_specs=pl.BlockSpec((None, tr, c), lambda k, i, c_ref: (k, i, 0))),
        out_shape=jax.ShapeDtypeStruct((n, h, c), BF16), compiler_params=_params("parallel", "parallel"),
    )(c_idx, g, recv)


def sum_chips(name, parts, recv, where, tr):
    n, h, c = recv.shape
    nb = h // tr

    def body(w_ref, p_ref, r_ref, o_ref):
        acc = p_ref[...].astype(F32)
        for k in range(n):
            acc = acc + r_ref[k].astype(F32)
        o_ref[...] = acc

    return pl.pallas_call(
        body, name=name,
        grid_spec=pltpu.PrefetchScalarGridSpec(
            num_scalar_prefetch=1, grid=(nb,),
            in_specs=[pl.BlockSpec((None, tr, c), lambda i, w_ref: (w_ref[0], i, 0)),
                      pl.BlockSpec((n, tr, c), lambda i, w_ref: (0, i, 0))],
            out_specs=pl.BlockSpec((tr, c), lambda i, w_ref: (w_ref[1] * nb + i, 0))),
        out_shape=jax.ShapeDtypeStruct((2 * h, c), F32), compiler_params=_params("parallel"),
    )(where, parts, recv)


def place_shard(name, shard, where, dtype, tr):
    r, c = shard.shape

    def body(w_ref, s_ref, o_ref):
        o_ref[...] = s_ref[...].astype(dtype)

    return pl.pallas_call(
        body, name=name,
        grid_spec=pltpu.PrefetchScalarGridSpec(
            num_scalar_prefetch=1, grid=(r // tr,),
            in_specs=[pl.BlockSpec((tr, c), lambda i, w_ref: (i, 0))],
            out_specs=pl.BlockSpec((None, tr, c), lambda i, w_ref: (w_ref[0], i, 0))),
        out_shape=jax.ShapeDtypeStruct((N_CHIPS, r, c), dtype), compiler_params=_params("parallel"),
    )(where, shard)


ANY = pl.BlockSpec(memory_space=pl.ANY)


def _place():
    x, y, c = lax.axis_index("x"), lax.axis_index("y"), lax.axis_index("c")
    chips = [(1 - x, y), (x, 1 - y), (1 - x, 1 - y)]
    return x, y, c, chips


def gather_shards(arrays):
    n = len(arrays)

    def body(*refs):
        send, pass_on, finish = _gather_phases(refs[n:2 * n], *refs[2 * n:], _spans(arrays))
        send()
        pass_on()
        finish()

    return pl.pallas_call(
        body, name="gather_shards", in_specs=[ANY] * n, out_specs=[ANY] * n,
        out_shape=[jax.ShapeDtypeStruct(a.shape, a.dtype) for a in _arrays(arrays)],
        input_output_aliases={w: w for w in range(n)}, scratch_shapes=_gather_sems(n),
        compiler_params=pltpu.CompilerParams(has_side_effects=True),
    )(*arrays)


def _gather_sems(n):
    return [pltpu.SemaphoreType.DMA((6 * n,)), pltpu.SemaphoreType.DMA((6 * n,))]


class Span(typing.NamedTuple):
    array: jax.Array
    lo: int
    hi: int
    ways: tuple = (0, 1, 2)


def _arrays(gathering):
    return [g.array if isinstance(g, Span) else g for g in gathering]


def _spans(gathering):
    return [(g.lo, g.hi, g.ways) if isinstance(g, Span) else (0, g.shape[1], (0, 1, 2)) for g in gathering]


def _gather_phases(out, send_sems, recv_sems, spans):
    n = len(out)
    if not any(ways for _, _, ways in spans):
        return (lambda: None,) * 3
    x, y, c, chips = _place()
    me = 2 * x + y
    sibling = (x, y, 1 - c)

    def half(w, chip, core):
        lo, hi, _ = spans[w]
        h = (hi - lo) // 2
        return out[w].at[chip, pl.ds(lo + core * h, h)]

    def copy(k, block, to):
        return pltpu.make_async_remote_copy(src_ref=block, dst_ref=block, send_sem=send_sems.at[k],
                                            recv_sem=recv_sems.at[k], device_id=to, device_id_type=MESH)

    def over_ici(w, j, chip):
        return copy(3 * w + j, half(w, chip, c), (chips[j][0], chips[j][1], c))

    def over_d2d(w, j, core):
        return copy(3 * n + 3 * w + j, half(w, 2 * chips[j][0] + chips[j][1], core), sibling)

    pairs = [(w, j) for w in range(n) for j in spans[w][2]]

    def send():
        for w, j in pairs:
            over_ici(w, j, me).start()

    def pass_on():
        for w, j in pairs:
            over_ici(w, j, 2 * chips[j][0] + chips[j][1]).wait_recv()
            over_d2d(w, j, c).start()

    def finish():
        for w, j in pairs:
            over_d2d(w, j, 1 - c).wait_recv()
        for w, j in pairs:
            over_ici(w, j, me).wait_send()
            over_d2d(w, j, c).wait_send()

    return send, pass_on, finish


def _relay_sems():
    return [pltpu.SemaphoreType.DMA((4,)), pltpu.SemaphoreType.DMA((4,))]


def _relay_phases(out, send_sems, recv_sems):
    x, y, c, chips = _place()
    sibling = (x, y, 1 - c)
    rows = out.shape[1]
    quarter = rows // 4
    far = 2 * chips[2][0] + chips[2][1]

    def piece(chip, way, core):
        return out.at[chip, pl.ds(way * (rows // 2) + core * quarter, quarter)]

    def copy(k, block, to):
        return pltpu.make_async_remote_copy(src_ref=block, dst_ref=block, send_sem=send_sems.at[k],
                                            recv_sem=recv_sems.at[k], device_id=to, device_id_type=MESH)

    def over_ici(way, chip):
        return copy(way, piece(chip, way, c), (chips[way][0], chips[way][1], c))

    def over_d2d(way, core):
        return copy(2 + way, piece(far, way, core), sibling)

    def send():
        for way in range(2):
            other = chips[1 - way]
            over_ici(way, 2 * other[0] + other[1]).start()

    def pass_on():
        for way in range(2):
            over_ici(way, far).wait_recv()
            over_d2d(way, c).start()

    def finish():
        for way in range(2):
            over_d2d(way, 1 - c).wait_recv()
        for way in range(2):
            other = chips[1 - way]
            over_ici(way, 2 * other[0] + other[1]).wait_send()
            over_d2d(way, c).wait_send()

    return send, pass_on, finish


def swap_halves(name, grads):
    n = len(grads)

    def body(*refs):
        send, finish = _swap_phases(refs[:n], refs[n:2 * n], *refs[2 * n:])
        send()
        finish()

    return pl.pallas_call(
        body, name=name, in_specs=[ANY] * n, out_specs=[ANY] * n, out_shape=_swap_shapes(grads),
        scratch_shapes=_swap_sems(n), compiler_params=pltpu.CompilerParams(has_side_effects=True),
    )(*grads)


def _swap_shapes(grads):
    return [jax.ShapeDtypeStruct((a.shape[0], a.shape[1] // 2, a.shape[2]), a.dtype) for a in grads]


def _swap_sems(n):
    return [pltpu.SemaphoreType.DMA((n,)), pltpu.SemaphoreType.DMA((n,))]


def _swap_phases(g, out, send_sems, recv_sems):
    x, y, c, _ = _place()

    def copies():
        return [pltpu.make_async_remote_copy(
            src_ref=g[w].at[:, pl.ds((1 - c) * (g[w].shape[1] // 2), g[w].shape[1] // 2)], dst_ref=out[w],
            send_sem=send_sems.at[w], recv_sem=recv_sems.at[w], device_id=(x, y, 1 - c), device_id_type=MESH)
            for w in range(len(g))]

    def send():
        for cp in copies():
            cp.start()

    def finish():
        for cp in copies():
            cp.wait()

    return send, finish


def _send_phases(g, out, send_sems, recv_sems):
    x, y, c, _ = _place()

    def copies():
        return [pltpu.make_async_remote_copy(
            src_ref=g[w], dst_ref=out[w], send_sem=send_sems.at[w], recv_sem=recv_sems.at[w],
            device_id=(x, y, 1 - c), device_id_type=MESH) for w in range(len(g))]

    def send():
        for cp in copies():
            cp.start()

    def finish():
        for cp in copies():
            cp.wait()

    return send, finish


def dw_in_half(name, h, dz, which, sending):
    s = h.shape[0]
    hh, tb = D // 2, IN_SHARD // 2
    n = len(sending)
    steps = IN_COLS // tb

    def body(w_ref, *refs):
        a_ref, b_ref, o_ref = refs[0], refs[1], refs[2 + n]
        j = pl.program_id(0)
        if n:
            send, finish = _send_phases(refs[2:2 + n], refs[3 + n:3 + 2 * n], *refs[3 + 2 * n:])
            pl.when(j == 0)(send)
        o_ref[...] = _dot(a_ref[...], b_ref[...], TN)
        if n:
            pl.when(j == steps - 1)(finish)

    out = pl.pallas_call(
        body, name=name,
        grid_spec=pltpu.PrefetchScalarGridSpec(
            num_scalar_prefetch=1, grid=(steps,),
            in_specs=[pl.BlockSpec((s, hh), lambda j, w: (0, w[0])), pl.BlockSpec((s, tb), lambda j, w: (0, j))]
            + [ANY] * n,
            out_specs=[pl.BlockSpec((None, hh, tb), lambda j, w: (j // 2, 0, j % 2))] + [ANY] * n,
            scratch_shapes=_swap_sems(n) if n else []),
        out_shape=[jax.ShapeDtypeStruct((N_CHIPS, hh, IN_SHARD), F32)]
        + [jax.ShapeDtypeStruct(a.shape, a.dtype) for a in sending],
        compiler_params=_params("arbitrary", communicates=bool(n)),
    )(which, h, dz, *sending)
    return out[0], out[1:]


def scatter_chips(parts):
    n = len(parts)

    def body(*refs):
        send, finish = _scatter_phases(refs[:n], refs[n:2 * n], *refs[2 * n:])
        send()
        finish()

    return pl.pallas_call(
        body, name="scatter_chips", in_specs=[ANY] * n, out_specs=[ANY] * n,
        out_shape=_scatter_shapes(parts), scratch_shapes=_scatter_sems(n),
        compiler_params=pltpu.CompilerParams(has_side_effects=True),
    )(*parts)


def _scatter_shapes(parts):
    return [jax.ShapeDtypeStruct((3,) + a.shape[1:], a.dtype) for a in parts]


def _scatter_sems(n):
    return [pltpu.SemaphoreType.DMA((3 * n,)), pltpu.SemaphoreType.DMA((3 * n,))]


def _scatter_phases(p, out, send_sems, recv_sems):
    x, y, c, chips = _place()

    def copies():
        return [pltpu.make_async_remote_copy(
            src_ref=p[w].at[2 * px + py], dst_ref=out[w].at[j], send_sem=send_sems.at[3 * w + j],
            recv_sem=recv_sems.at[3 * w + j], device_id=(px, py, c), device_id_type=MESH)
            for w in range(len(p)) for j, (px, py) in enumerate(chips)]

    def send():
        for cp in copies():
            cp.start()

    def finish():
        for cp in copies():
            cp.wait()

    return send, finish


def join_halves(arrays):
    n = len(arrays)

    def body(*refs):
        out = refs[n:2 * n]
        send_sems, recv_sems = refs[2 * n:]
        x, y, c, _ = _place()

        def copy(w, core):
            h = out[w].shape[0] // 2
            rows = out[w].at[pl.ds(core * h, h)]
            return pltpu.make_async_remote_copy(
                src_ref=rows, dst_ref=rows, send_sem=send_sems.at[w], recv_sem=recv_sems.at[w],
                device_id=(x, y, 1 - c), device_id_type=MESH)

        for w in range(n):
            copy(w, c).start()
        for w in range(n):
            copy(w, 1 - c).wait_recv()
        for w in range(n):
            copy(w, c).wait_send()

    return pl.pallas_call(
        body, name="join_halves", in_specs=[ANY] * n, out_specs=[ANY] * n,
        out_shape=[jax.ShapeDtypeStruct(a.shape, a.dtype) for a in arrays],
        input_output_aliases={w: w for w in range(n)},
        scratch_shapes=[pltpu.SemaphoreType.DMA((n,)), pltpu.SemaphoreType.DMA((n,))],
        compiler_params=pltpu.CompilerParams(has_side_effects=True),
    )(*arrays)


def allreduce_small(packed):
    r, c = packed.shape
    n_dev = 8

    def body(x_ref, all_ref, sum_ref, send_sems, recv_sems, local_sem):
        x, y, cc, chips = _place()
        me, sibling = (x, y, cc), (x, y, 1 - cc)

        def rows(px, py, pc):
            return all_ref.at[4 * px + 2 * py + pc]

        def copy(k, block, to, src=None):
            return pltpu.make_async_remote_copy(
                src_ref=rows(*block) if src is None else src, dst_ref=rows(*block), send_sem=send_sems.at[k],
                recv_sem=recv_sems.at[k], device_id=to, device_id_type=MESH)

        mine = pltpu.make_async_copy(x_ref, rows(*me), local_sem)
        mine.start()
        first = [copy(0, me, sibling, src=x_ref)]
        first += [copy(1 + j, me, (*chip, cc), src=x_ref) for j, chip in enumerate(chips)]
        for cp in first:
            cp.start()
        passed = [copy(4 + j, (*chip, cc), sibling) for j, chip in enumerate(chips)]
        for j, chip in enumerate(chips):
            copy(1 + j, (*chip, cc), me).wait_recv()
            passed[j].start()
        copy(0, sibling, me).wait_recv()
        for j, chip in enumerate(chips):
            copy(4 + j, (*chip, 1 - cc), me).wait_recv()
        for cp in first + passed:
            cp.wait_send()
        mine.wait()
        acc = all_ref[0]
        for k in range(1, n_dev):
            acc = acc + all_ref[k]
        sum_ref[...] = acc

    vm = pl.BlockSpec(memory_space=pltpu.VMEM)
    return pl.pallas_call(
        body, name="allreduce_small", in_specs=[vm], out_specs=[vm, vm],
        out_shape=[jax.ShapeDtypeStruct((n_dev, r, c), F32), jax.ShapeDtypeStruct((r, c), F32)],
        scratch_shapes=[pltpu.SemaphoreType.DMA((7,)), pltpu.SemaphoreType.DMA((7,)), pltpu.SemaphoreType.DMA],
        compiler_params=pltpu.CompilerParams(has_side_effects=True, vmem_limit_bytes=VMEM_LIMIT),
    )(packed)[1]


def local_step(x, target, vecs, w_s, bs_t, bg, wg_in, late, core=None, order=None):
    on_mesh = core is not None

    def add(names, grads, recv):
        return [add_halves("add_" + n, g, r, core, min(r.shape[1], 256)) for n, g, r in zip(names, grads, recv)]

    g_pre, ln_g, ln_b, g_post, g_fpre, g_fpost = vecs
    s = x.shape[0]
    if order is None:
        order = jnp.arange(N_CHIPS, dtype=jnp.int32)
    logc = _attn_tables(s)
    ka, kb = _alibi_tables(s)

    h = norm_pre(x, g_pre)
    if not on_mesh:
        wg_a, wg_b, wg_out, wg_ff1, wg_ff2 = late
    if on_mesh:
        cut = D // 4
        z, (wg_in,), (wg_a, wg_b, wg_out, wg_ff1, wg_ff2) = mm_in(
            "mm_in_own", h, wg_in, order, 0, 1, None, [Span(wg_in, 0, D, (0, 1))], casting=late)
        z, (wg_in,), _ = mm_in("mm_in_near", h, wg_in, order, 1, 2, z, [Span(wg_in, 0, D, ())], relay=True)
        z, (wg_in, wg_a), _ = mm_in("mm_in_far", h, wg_in, order, 3, 1, z, [Span(wg_in, 0, D, ()), wg_a])
        ya, (wg_b, wg_ff2) = gating_fwd(z, ln_g, ln_b, w_s, bs_t, [wg_b, Span(wg_ff2, 0, cut)])
        yb, lse, (wg_ff1, wg_ff2, bg) = attn_fwd(z, logc, ka, kb, [wg_ff1, Span(wg_ff2, cut, D), bg])
        bg = jnp.transpose(bg[:, :2, :], (1, 0, 2)).reshape(2, D)
    else:
        z, _, _ = mm_in("mm_in", h, wg_in, order, 0, N_CHIPS, None, [Span(wg_in, 0, D, ())])
        ya, _ = gating_fwd(z, ln_g, ln_b, w_s, bs_t, [])
        yb, lse, _ = attn_fwd(z, logc, ka, kb, [])
    merged, pa, pb, got = proj_merge(ya, yb, wg_a.reshape(D, D), wg_b.reshape(D, D), z, bg, [wg_out] if on_mesh else [])
    wg_out = got[0] if on_mesh else wg_out
    w_out = wg_out.reshape(D, D)
    o, x1, h2 = out_norm(merged, w_out, x, g_post, g_fpre)
    a, rl, _ = mm_ff1(h2, wg_ff1, [])
    w_ff2 = wg_ff2.reshape(D_FF, D)
    dy, df, d_gfpost, loss = ff2_loss(rl, w_ff2, x1, target, g_fpost)

    half_cols = pl.BlockSpec((D, D // 2), lambda i, j: (0, j))
    d_wff2 = mm_tn("dw_ff2", rl, df, D // 2, D, (D_FF, D), pl.BlockSpec((D // 2, D), lambda i, j: (i, 0)))
    da = ff2_bwd(df, w_ff2, a)
    d_wff1 = mm_tn("dw_ff1", h2, da, D, D // 2, (N_CHIPS, D, D),
                   pl.BlockSpec((None, D, D // 2), lambda i, j: (j // 2, 0, j % 2)))
    d_ff = [d_wff1, d_wff2.reshape(N_CHIPS, D, D)]
    dx1, do, d_gfpre, d_gpost, recv_ff = ff1_bwd_norms(da, wg_ff1, x1, o, dy, g_fpre, g_post, d_ff if on_mesh else [])
    d_wout = mm_tn("dw_out", merged, do, D, D // 2, (D, D), half_cols)
    dpa, dpb, dga, dgb, d_bg = out_bwd_gates(do, w_out, pa, pb, z, bg)
    d_wa = mm_tn("dw_a", ya, dpa, D, D // 2, (D, D), half_cols)
    d_wb = mm_tn("dw_b", yb, dpb, D, D // 2, (D, D), half_cols)
    dya = mm_nt("dy_a", dpa, wg_a.reshape(D, D))
    dyb = mm_nt("dy_b", dpb, wg_b.reshape(D, D))
    d_proj = [d_wa.reshape(N_CHIPS, D // N_CHIPS, D), d_wb.reshape(N_CHIPS, D // N_CHIPS, D),
              d_wout.reshape(N_CHIPS, D // N_CHIPS, D)]
    du, dv, d_ws, d_bs, d_lng, d_lnb, recv_proj = gating_bwd(z, dya, ln_g, ln_b, w_s, bs_t, d_proj if on_mesh else [])
    early = d_proj + d_ff
    parts_early = add(BIG[1:], early, list(recv_proj) + list(recv_ff)) if on_mesh else []
    dq, dk, dvb, got_early = attn_bwd(z, yb, dyb, lse, logc, ka, kb, parts_early)
    dz = jnp.concatenate([du, dv, dq, dk, dvb, dga, dgb], axis=1)
    if on_mesh:
        for_sibling, _ = dw_in_half("dw_in_sibling", h, dz, 1 - core, [])
        mine, from_sibling = dw_in_half("dw_in_mine", h, dz, core, [for_sibling])
        d_win = None
        parts_late = [add_halves("add_w_in", mine, from_sibling[0], jnp.zeros((1,), jnp.int32), 256)]
    else:
        half = IN_SHARD // 2
        d_win = mm_tn("dw_in", h, dz, D, half, (N_CHIPS, D, IN_SHARD),
                      pl.BlockSpec((None, D, half), lambda i, j: (j // 2, 0, j % 2)))
        parts_late = []
    dx, d_gpre, got_late = in_bwd_norm(dz, wg_in, x, dx1, g_pre, parts_late)

    small = dict(norm_mix_pre=d_gpre, b_gate=d_bg, ln_v_g=d_lng, ln_v_b=d_lnb, w_s=d_ws, b_s=d_bs[:, 0, :],
                 norm_mix_post=d_gpost, norm_ffn_pre=d_gfpre, norm_ffn_post=d_gfpost)
    return loss[0, 0], dx, [d_win] + early, small, parts_late + parts_early, list(got_late) + list(got_early)


BIG = ("w_in", "w_a_proj", "w_b_proj", "w_out", "w_ff1", "w_ff2")
SMALL = ("norm_mix_pre", "ln_v_g", "ln_v_b", "b_s", "norm_mix_post", "norm_ffn_pre", "norm_ffn_post", "w_s", "b_gate")
ORDER = ("norm_mix_pre", "w_in", "b_gate", "ln_v_g", "ln_v_b", "w_s", "b_s", "w_a_proj", "w_b_proj", "w_out",
         "norm_mix_post", "norm_ffn_pre", "w_ff1", "w_ff2", "norm_ffn_post")
VEC_ROWS = D // 128
WS_ROW = 7 * VEC_ROWS
BG_ROW = WS_ROW + GROUPS * CHUNK
LOSS_ROW = BG_ROW + 2 * VEC_ROWS
PACK_ROWS = LOSS_ROW + 8


def pack_small(small, loss):
    vectors = [small[n] for n in SMALL[:7]]

    def body(*refs):
        out = refs[-1]
        ws_ref, bg_ref, loss_ref = refs[7:10]
        for i, n in enumerate(SMALL[:7]):
            if n == "b_s":
                out[i * VEC_ROWS:(i + 1) * VEC_ROWS, :] = refs[i][...]
            else:
                for j in range(VEC_ROWS):
                    out[i * VEC_ROWS + j:i * VEC_ROWS + j + 1, :] = refs[i][:, j * 128:(j + 1) * 128]
        for g in range(GROUPS):
            out[WS_ROW + g * CHUNK:WS_ROW + (g + 1) * CHUNK, :] = ws_ref[g]
        for r in range(2):
            for j in range(VEC_ROWS):
                row = BG_ROW + r * VEC_ROWS + j
                out[row:row + 1, :] = bg_ref[r:r + 1, j * 128:(j + 1) * 128]
        lane = lax.broadcasted_iota(jnp.int32, (8, 128), 1)
        sub = lax.broadcasted_iota(jnp.int32, (8, 128), 0)
        out[LOSS_ROW:LOSS_ROW + 8, :] = jnp.where((lane == 0) & (sub == 0), loss_ref[...], 0.0)

    return pl.pallas_call(
        body, name="pack_small", out_shape=jax.ShapeDtypeStruct((PACK_ROWS, 128), F32),
        compiler_params=_params(),
    )(*vectors, small["w_s"], small["b_gate"], loss)


def adamw_small(summed, chip, w, m, v):
    shapes = {n: (1, D) for n in SMALL}
    shapes.update(b_s=(GROUPS, CHUNK), w_s=(GROUPS * CHUNK, CHUNK), b_gate=(2, D // N_CHIPS))
    flat = lambda t: [t[n].reshape(shapes[n]) for n in SMALL]
    per = D // N_CHIPS // 128

    def body(chip_ref, sum_ref, *refs):
        params, outs = refs[:27], refs[27:]
        sub = lax.broadcasted_iota(jnp.int32, (VEC_ROWS, 128), 0)

        def gate_row(r):
            rows = sum_ref[BG_ROW + r * VEC_ROWS:BG_ROW + (r + 1) * VEC_ROWS, :]
            return jnp.concatenate([jnp.sum(jnp.where(sub == per * chip_ref[0] + j, rows, 0.0), axis=0, keepdims=True)
                                    for j in range(per)], axis=1)

        for i, n in enumerate(SMALL):
            if n == "b_s":
                g = sum_ref[i * VEC_ROWS:(i + 1) * VEC_ROWS, :]
            elif n == "w_s":
                g = sum_ref[WS_ROW:BG_ROW, :]
            elif n == "b_gate":
                g = jnp.concatenate([gate_row(0), gate_row(1)], axis=0)
            else:
                g = jnp.concatenate([sum_ref[i * VEC_ROWS + j:i * VEC_ROWS + j + 1, :] for j in range(VEC_ROWS)],
                                    axis=1)
            delta, nm, nv = _adamw_math(params[i][...], g, params[9 + i][...], params[18 + i][...])
            outs[4 * i][...], outs[4 * i + 1][...], outs[4 * i + 2][...], outs[4 * i + 3][...] = g, delta, nm, nv

    vm = pl.BlockSpec(memory_space=pltpu.VMEM)
    res = pl.pallas_call(
        body, name="adamw_small",
        in_specs=[pl.BlockSpec(memory_space=pltpu.SMEM)] + [vm] * 28, out_specs=[vm] * 36,
        out_shape=[jax.ShapeDtypeStruct(shapes[n], F32) for n in SMALL for _ in range(4)],
        compiler_params=_params(),
    )(chip, summed, *flat(w), *flat(m), *flat(v))
    return {n: tuple(r.reshape(w[n].shape) for r in res[4 * i:4 * i + 4]) for i, n in enumerate(SMALL)}


def kernel(x, norm_mix_pre, w_in, b_gate, ln_v_g, ln_v_b, w_s, b_s, w_a_proj, w_b_proj, w_out, norm_mix_post, norm_ffn_pre, w_ff1, w_ff2, norm_ffn_post, loss_target, m_norm_mix_pre, m_w_in, m_b_gate, m_ln_v_g, m_ln_v_b, m_w_s, m_b_s, m_w_a_proj, m_w_b_proj, m_w_out, m_norm_mix_post, m_norm_ffn_pre, m_w_ff1, m_w_ff2, m_norm_ffn_post, v_norm_mix_pre, v_w_in, v_b_gate, v_ln_v_g, v_ln_v_b, v_w_s, v_b_s, v_w_a_proj, v_w_b_proj, v_w_out, v_norm_mix_post, v_norm_ffn_pre, v_w_ff1, v_w_ff2, v_norm_ffn_post):
    w = dict(norm_mix_pre=norm_mix_pre, w_in=w_in, b_gate=b_gate, ln_v_g=ln_v_g, ln_v_b=ln_v_b, w_s=w_s, b_s=b_s,
             w_a_proj=w_a_proj, w_b_proj=w_b_proj, w_out=w_out, norm_mix_post=norm_mix_post,
             norm_ffn_pre=norm_ffn_pre, w_ff1=w_ff1, w_ff2=w_ff2, norm_ffn_post=norm_ffn_post)
    m = dict(norm_mix_pre=m_norm_mix_pre, w_in=m_w_in, b_gate=m_b_gate, ln_v_g=m_ln_v_g, ln_v_b=m_ln_v_b, w_s=m_w_s,
             b_s=m_b_s, w_a_proj=m_w_a_proj, w_b_proj=m_w_b_proj, w_out=m_w_out, norm_mix_post=m_norm_mix_post,
             norm_ffn_pre=m_norm_ffn_pre, w_ff1=m_w_ff1, w_ff2=m_w_ff2, norm_ffn_post=m_norm_ffn_post)
    v = dict(norm_mix_pre=v_norm_mix_pre, w_in=v_w_in, b_gate=v_b_gate, ln_v_g=v_ln_v_g, ln_v_b=v_ln_v_b, w_s=v_w_s,
             b_s=v_b_s, w_a_proj=v_w_a_proj, w_b_proj=v_w_b_proj, w_out=v_w_out, norm_mix_post=v_norm_mix_post,
             norm_ffn_pre=v_norm_ffn_pre, w_ff1=v_w_ff1, w_ff2=v_w_ff2, norm_ffn_post=v_norm_ffn_post)
    chip = 2 * lax.axis_index("x") + lax.axis_index("y")
    core = lax.axis_index("c")

    where = jnp.stack([chip, core]).astype(jnp.int32)
    wg_in = place_shard("place_w_in", w_in[0], where, BF16, 256)
    bg_all = place_shard("place_b_gate", jnp.pad(b_gate[0], ((0, 14), (0, 0))), where, F32, 16)
    vecs = (norm_mix_pre, ln_v_g, ln_v_b, norm_mix_post, norm_ffn_pre, norm_ffn_post)
    loss, dx, _, small, parts, got = local_step(
        x[0], loss_target[0], vecs, w_s[0], b_s[0].T, bg_all, wg_in, [w[n][0] for n in BIG[1:]],
        core=jnp.reshape(core, (1,)).astype(jnp.int32),
        order=jnp.stack([chip, chip ^ 2, chip ^ 1, chip ^ 3]).astype(jnp.int32))

    halves = [sum_chips("sum_" + n, p, r, where, min(p.shape[1], 256)) for n, p, r in zip(BIG, parts, got)]
    grads = dict(zip(BIG, join_halves(halves)))

    summed = allreduce_small(pack_small(small, loss.reshape(1, 1)))
    loss = summed[LOSS_ROW, 0]

    new = adamw_small(summed, jnp.reshape(chip, (1,)).astype(jnp.int32), w, m, v)
    for n in BIG:
        shape = w[n].shape
        res = adamw("adamw_" + n, w[n][0], grads[n], m[n][0], v[n][0], min(shape[1], 256))
        new[n] = tuple(r.reshape(shape) for r in res)

    outs = [loss, dx[None]]
    for i in range(4):
        outs += [new[n][i] for n in ORDER]
    return tuple(outs)
```

```python
import functools
import math
import typing

import numpy as np
import jax
import jax.numpy as jnp
from jax import lax
from jax.experimental import pallas as pl
from jax.experimental.pallas import tpu as pltpu

F32 = jnp.float32
BF16 = jnp.bfloat16
MESH = pl.DeviceIdType.MESH

D = 1024
EPS = 1e-6
CHUNK = 128
GROUPS = 8
HEADS = 16
HEAD_DIM = 64
ATT_T = 256
ATT_GROUP = 8
ATT_BWD_GROUP = 2
N_CHIPS = 4
D_FF = 4 * D
IN_COLS = 7 * D
IN_SHARD = IN_COLS // N_CHIPS
MASKED = -1e30
VMEM_LIMIT = 56 * 2 ** 20

ADAM_LR, ADAM_B1, ADAM_B2, ADAM_EPS, ADAM_WD, ADAM_STEP = 0.001, 0.9, 0.999, 1e-08, 0.01, 10

NN = (((1,), (0,)), ((), ()))
NT = (((1,), (1,)), ((), ()))
TN = (((0,), (0,)), ((), ()))


def _dot(a, b, dims=NN):
    return lax.dot_general(a, b, dims, preferred_element_type=F32)


def _params(*sem, communicates=False):
    return pltpu.CompilerParams(dimension_semantics=sem or None, vmem_limit_bytes=VMEM_LIMIT,
                                has_side_effects=communicates)


def _rows(tr, c, col=0):
    return pl.BlockSpec((tr, c), lambda i: (i, col))


def _full(shape):
    n = len(shape)
    return pl.BlockSpec(shape, lambda *_: (0,) * n)


def _gelu(x):
    k = math.sqrt(2.0 / math.pi)
    return 0.5 * x * (1.0 + jnp.tanh(k * (x + 0.044715 * x * x * x)))


def _gelu_and_grad(x):
    k = math.sqrt(2.0 / math.pi)
    t = jnp.tanh(k * (x + 0.044715 * x * x * x))
    g = 0.5 * x * (1.0 + t)
    dg = 0.5 * (1.0 + t) + 0.5 * x * (1.0 - t * t) * (k * (1.0 + 3.0 * 0.044715 * x * x))
    return g, dg


def _sigmoid(x):
    return 1.0 / (1.0 + jnp.exp(-x))


def _rms(x):
    r = lax.rsqrt(jnp.mean(x * x, axis=-1, keepdims=True) + EPS)
    return x * r, r


def _rms_bwd(dn, xhat, r):
    return r * (dn - xhat * jnp.mean(dn * xhat, axis=-1, keepdims=True))


def norm_pre(x, g):
    s = x.shape[0]
    tr = 512

    def body(x_ref, g_ref, h_ref):
        xhat, _ = _rms(x_ref[...])
        h_ref[...] = (xhat * g_ref[...]).astype(BF16)

    return pl.pallas_call(
        body, name="norm_pre", grid=(s // tr,),
        in_specs=[_rows(tr, D), _full((1, D))], out_specs=_rows(tr, D),
        out_shape=jax.ShapeDtypeStruct((s, D), BF16), compiler_params=_params("parallel"),
    )(x, g)


def mm_in(name, h, wg, order, first, count, z, gathering, relay=False, casting=()):
    s = h.shape[0]
    tm, tn = 1024, IN_SHARD // 2
    per = IN_SHARD // tn
    n, m = len(gathering), len(casting)
    nj, ni = count * per, s // tm
    has_z = z is not None
    arrays = _arrays(gathering)
    at = [k for k, a in enumerate(arrays) if a is wg][0]

    def body(order_ref, *refs):
        a_ref = refs[0]
        cast_in = refs[1 + has_z + n:1 + has_z + n + m]
        o_ref = refs[1 + has_z + n + m]
        held = refs[2 + has_z + n + m:2 + has_z + 2 * n + m]
        cast_out = refs[2 + has_z + 2 * n + m:2 + has_z + 2 * n + 2 * m]
        tile, tile_sem = refs[2 + has_z + 2 * n + 2 * m:4 + has_z + 2 * n + 2 * m]
        j, i = pl.program_id(0), pl.program_id(1)
        sems = refs[4 + has_z + 2 * n + 2 * m:]
        for src, dst in zip(cast_in, cast_out):
            dst[...] = src[...].astype(BF16)
        phases = [_gather_phases(held, *sems[:2], _spans(gathering))]
        if relay:
            phases.append(_relay_phases(held[at], *sems[2:]))
        def each(fs):
            def run():
                for f in fs:
                    f()
            return run

        send, pass_on, finish = [each(fs) for fs in zip(*phases)]

        def fetch(t):
            chip = order_ref[first + t // per]
            return pltpu.make_async_copy(held[at].at[chip, :, pl.ds((t % per) * tn, tn)], tile.at[t % 2],
                                         tile_sem.at[t % 2])

        @pl.when(i == 0)
        def _():
            @pl.when(j == 0)
            def _():
                send()
                fetch(0).start()

            fetch(j).wait()

            @pl.when(j + 1 < nj)
            def _():
                fetch(j + 1).start()

        pl.when((j == nj - 1) & (i == ni - 1))(pass_on)
        rows = pl.ds(pl.multiple_of(i * tm, tm), tm)
        o_ref[...] = _dot(a_ref[rows, :], tile[j % 2]).astype(BF16)
        pl.when((j == nj - 1) & (i == ni - 1))(finish)

    steps = nj * ni
    out = pl.pallas_call(
        body, name=name,
        grid_spec=pltpu.PrefetchScalarGridSpec(
            num_scalar_prefetch=1, grid=(nj, ni),
            in_specs=[pl.BlockSpec((s, D), lambda j, i, o: (0, 0))] + [ANY] * (has_z + n)
            + [pl.BlockSpec((a.shape[0] // steps, a.shape[1]), lambda j, i, o: (j * ni + i, 0)) for a in casting],
            out_specs=[pl.BlockSpec((tm, tn), lambda j, i, o: (i, o[first + j // per] * per + j % per))] + [ANY] * n
            + [pl.BlockSpec((None, a.shape[0] // steps, a.shape[1]), lambda j, i, o: (o[0], j * ni + i, 0))
               for a in casting],
            scratch_shapes=[pltpu.VMEM((2, D, tn), BF16), pltpu.SemaphoreType.DMA((2,))] + _gather_sems(n)
            + (_relay_sems() if relay else [])),
        out_shape=[jax.ShapeDtypeStruct((s, IN_COLS), BF16)] + [jax.ShapeDtypeStruct(a.shape, a.dtype) for a in arrays]
        + [jax.ShapeDtypeStruct((N_CHIPS,) + a.shape, BF16) for a in casting],
        input_output_aliases={**({2: 0} if has_z else {}), **{2 + has_z + w: 1 + w for w in range(n)}},
        compiler_params=_params("arbitrary", "arbitrary", communicates=True),
    )(order, h, *([z] if has_z else []), *arrays, *casting)
    return out[0], out[1:1 + n], out[1 + n:]


def _tril_ws(ws_ref, g):
    r = lax.broadcasted_iota(jnp.int32, (CHUNK, CHUNK), 0)
    c = lax.broadcasted_iota(jnp.int32, (CHUNK, CHUNK), 1)
    return jnp.where(c <= r, ws_ref[g], 0.0).astype(BF16)


def _layer_norm(v):
    mu = jnp.mean(v, axis=-1, keepdims=True)
    d = v - mu
    rstd = lax.rsqrt(jnp.mean(d * d, axis=-1, keepdims=True) + EPS)
    return d * rstd, rstd


def gating_fwd(z, ln_g, ln_b, w_s, bs_t, gathering):
    s = z.shape[0]
    n = len(gathering)
    steps = s // CHUNK

    def body(*refs):
        u_ref, v_ref, lg_ref, lb_ref, ws_ref, bst_ref = refs[:6]
        ya_ref = refs[6 + n]
        ci = pl.program_id(0)
        if n:
            send, pass_on, finish = _gather_phases(refs[7 + n:7 + 2 * n], *refs[7 + 2 * n:], _spans(gathering))
            pl.when(ci == 0)(send)
            pl.when(ci == steps - 1)(pass_on)
        ug = _gelu(u_ref[...].astype(F32))
        vhat, _ = _layer_norm(_gelu(v_ref[...].astype(F32)))
        vn = (vhat * lg_ref[...] + lb_ref[...]).astype(BF16)
        for g in range(GROUPS):
            cols = slice(g * CHUNK, (g + 1) * CHUNK)
            mixed = _dot(_tril_ws(ws_ref, g), vn[:, cols]) + bst_ref[:, g:g + 1]
            ya_ref[:, cols] = (ug[:, cols] * mixed).astype(BF16)
        if n:
            pl.when(ci == steps - 1)(finish)

    out = pl.pallas_call(
        body, name="gating_fwd", grid=(steps,),
        in_specs=[_rows(CHUNK, D, 0), _rows(CHUNK, D, 1), _full((1, D)), _full((1, D)),
                  _full((GROUPS, CHUNK, CHUNK)), _full((CHUNK, GROUPS))] + [ANY] * n,
        out_specs=[_rows(CHUNK, D)] + [ANY] * n,
        out_shape=[jax.ShapeDtypeStruct((s, D), BF16)]
        + [jax.ShapeDtypeStruct(a.shape, a.dtype) for a in _arrays(gathering)],
        input_output_aliases={6 + w: 1 + w for w in range(n)},
        scratch_shapes=_gather_sems(n) if n else [],
        compiler_params=_params("arbitrary", communicates=bool(n)),
    )(z, z, ln_g, ln_b, w_s, bs_t, *_arrays(gathering))
    return out[0], out[1:]


def _attn_tables(s):
    nd = s // ATT_T
    r = np.arange(ATT_T)[None, :, None]
    c = np.arange(ATT_T)[None, None, :]
    delta = np.arange(nd)[:, None, None] * ATT_T + r - c
    count = np.zeros(delta.shape, np.int64)
    for window, dilation in ((128, 1), (512, 4), (2048, 16)):
        count += (delta >= 0) & (delta % dilation == 0) & (delta <= window)
    logc = np.where(count > 0, np.log(np.maximum(count, 1)), MASKED)
    return jnp.asarray(logc, F32)


AUG = 3


def _split3_np(x):
    terms, rest = [], np.asarray(x, np.float64)
    for _ in range(AUG):
        term = np.asarray(rest.astype(jnp.bfloat16), np.float64)
        terms.append(term)
        rest = rest - term
    return terms


def _split3(x):
    terms, rest = [], x
    for _ in range(AUG):
        term = rest.astype(BF16).astype(F32)
        terms.append(term)
        rest = rest - term
    return terms


def _alibi_tables(s):
    nb = s // ATT_T
    slopes = np.exp2(-8.0 * np.arange(1, HEADS + 1, dtype=np.float64) / HEADS)
    ka = np.zeros((HEADS // 2, 2, ATT_T, 128), np.float32)
    kb = np.zeros((HEADS // 2, 2, nb, 128), np.float32)
    for p in range(HEADS // 2):
        for e in range(2):
            base = HEAD_DIM * (1 - e)
            for a, term in enumerate(_split3_np(slopes[2 * p + e] * np.arange(ATT_T))):
                ka[p, e, :, base + a] = term
            for a, term in enumerate(_split3_np(slopes[2 * p + e] * ATT_T * np.arange(nb))):
                kb[p, e, :, base + AUG + a] = term
            ka[p, e, :, base + 2 * AUG:base + 3 * AUG] = 1.0
    return jnp.asarray(ka), jnp.asarray(kb)


def _head_masks():
    lane = lax.broadcasted_iota(jnp.int32, (1, 128), 1)
    first = lane < HEAD_DIM

    def ones(e, n):
        base = HEAD_DIM * (1 - e)
        return ((lane >= base) & (lane < base + n)).astype(F32)

    return first, lane, ones


def _place3(lane, at, terms, other):
    for a, term in enumerate(terms):
        other = jnp.where(lane == at + a, term, other)
    return other


def attn_fwd(z, logc, ka, kb, gathering):
    s = z.shape[0]
    nq = s // ATT_T
    t = ATT_T
    n = len(gathering)
    grp = ATT_GROUP
    ngrp = HEADS // 2 // grp
    wide = 128 * grp
    qcol, kcol, vcol = 2 * D // wide, 3 * D // wide, 4 * D // wide

    def body(*refs):
        q_ref, k_ref, v_ref, lc_ref, ka_ref, kb_ref = refs[:6]
        y_ref, lse_ref = refs[6 + n:8 + n]
        q_s, k_s, v_s, m_s, l_s, acc_s = refs[8 + 2 * n:14 + 2 * n]
        gi, qi = pl.program_id(0), pl.program_id(1)
        first, lane, ones = _head_masks()
        if n:
            send, pass_on, finish = _gather_phases(refs[8 + n:8 + 2 * n], *refs[14 + 2 * n:], _spans(gathering))
            pl.when((gi == 0) & (qi == 0))(send)
            pl.when((gi == ngrp - 1) & (qi == nq - 1))(pass_on)

        @pl.when(qi == 0)
        def _():
            sel = jnp.broadcast_to(first.astype(F32), (t, 128))
            for pr in range(grp):
                cols = slice(pr * 128, (pr + 1) * 128)
                for jb in range(nq):
                    kj = k_ref[jb * t:(jb + 1) * t, cols].astype(F32)
                    vj = v_ref[jb * t:(jb + 1) * t, cols].astype(F32)
                    k_s[pr, 0, jb] = jnp.where(first, kj, ka_ref[pr, 0] + kb_ref[pr, 0, jb:jb + 1, :]).astype(BF16)
                    k_s[pr, 1, jb] = jnp.where(first, ka_ref[pr, 1] + kb_ref[pr, 1, jb:jb + 1, :], kj).astype(BF16)
                    v_s[pr, jb, 0:t, 0:128] = jnp.where(first, vj, 0.0).astype(BF16)
                    v_s[pr, jb, t:2 * t, 0:128] = jnp.where(first, 0.0, vj).astype(BF16)
                    v_s[pr, jb, 0:t, 128:256] = sel.astype(BF16)
                    v_s[pr, jb, t:2 * t, 128:256] = (1.0 - sel).astype(BF16)

        for pr in range(grp):
            q = q_ref[:, pr * 128:(pr + 1) * 128].astype(F32) * (1.0 / math.sqrt(HEAD_DIM))
            q_s[pr, 0] = jnp.where(first, q, ones(0, 2 * AUG)).astype(BF16)
            q_s[pr, 1] = jnp.where(first, ones(1, 2 * AUG), q).astype(BF16)
        m_s[...] = jnp.full_like(m_s, MASKED)
        l_s[...] = jnp.zeros_like(l_s)
        acc_s[...] = jnp.zeros_like(acc_s)

        def scores(j):
            return tuple(_dot(q_s[pr, e], k_s[pr, e, j], NT) for pr in range(grp) for e in range(2))

        def step(j, carry):
            softmax_block(j, scores(j))
            return carry

        def softmax_block(j, u):
            lc = lc_ref[qi - j]
            for pr in range(grp):
                u0 = u[2 * pr] + lc
                u1 = u[2 * pr + 1] + lc
                m0, m1 = m_s[pr, 0], m_s[pr, 1]
                n0 = jnp.maximum(m0, jnp.max(u0, axis=-1, keepdims=True))
                n1 = jnp.maximum(m1, jnp.max(u1, axis=-1, keepdims=True))
                m_s[pr, 0], m_s[pr, 1] = n0, n1
                p = jnp.concatenate([jnp.exp(u0 - jnp.concatenate([n0, n0], axis=1)).astype(BF16),
                                     jnp.exp(u1 - jnp.concatenate([n1, n1], axis=1)).astype(BF16)], axis=1)
                pv = _dot(p, v_s[pr, j])
                alpha = jnp.where(first, jnp.exp(m0 - n0), jnp.exp(m1 - n1))
                acc_s[pr] = acc_s[pr] * alpha + pv[:, 0:128]
                l_s[pr] = l_s[pr] * alpha + pv[:, 128:256]

        lax.fori_loop(0, qi + 1, step, 0)
        for pr in range(grp):
            cols = slice(pr * 128, (pr + 1) * 128)
            y_ref[:, cols] = (acc_s[pr] / l_s[pr]).astype(BF16)
            lse_ref[:, cols] = jnp.where(first, m_s[pr, 0], m_s[pr, 1]) + jnp.log(l_s[pr])
        if n:
            pl.when((gi == ngrp - 1) & (qi == nq - 1))(finish)

    out = pl.pallas_call(
        body, name="attn_fwd", grid=(ngrp, nq),
        in_specs=[pl.BlockSpec((t, wide), lambda g, i: (i, qcol + g)),
                  pl.BlockSpec((s, wide), lambda g, i: (0, kcol + g)),
                  pl.BlockSpec((s, wide), lambda g, i: (0, vcol + g)),
                  _full((nq, t, t)),
                  pl.BlockSpec((grp, 2, t, 128), lambda g, i: (g, 0, 0, 0)),
                  pl.BlockSpec((grp, 2, nq, 128), lambda g, i: (g, 0, 0, 0))] + [ANY] * n,
        out_specs=[pl.BlockSpec((t, wide), lambda g, i: (i, g)), pl.BlockSpec((t, wide), lambda g, i: (i, g))]
        + [ANY] * n,
        out_shape=[jax.ShapeDtypeStruct((s, D), BF16), jax.ShapeDtypeStruct((s, D), F32)]
        + [jax.ShapeDtypeStruct(a.shape, a.dtype) for a in _arrays(gathering)],
        input_output_aliases={6 + w: 2 + w for w in range(n)},
        scratch_shapes=[pltpu.VMEM((grp, 2, t, 128), BF16), pltpu.VMEM((grp, 2, nq, t, 128), BF16),
                        pltpu.VMEM((grp, nq, 2 * t, 256), BF16), pltpu.VMEM((grp, 2, t, 128), F32),
                        pltpu.VMEM((grp, t, 128), F32), pltpu.VMEM((grp, t, 128), F32)]
        + (_gather_sems(n) if n else []),
        compiler_params=_params("arbitrary", "arbitrary", communicates=bool(n)),
    )(z, z, z, logc, ka, kb, *_arrays(gathering))
    return out[0], out[1], out[2:]


def proj_merge(ya, yb, wa, wb, z, bg, gathering):
    s = ya.shape[0]
    tm = 512
    n = len(gathering)
    steps = s // tm

    def body(*refs):
        ya_ref, yb_ref, wa_ref, wb_ref, ga_ref, gb_ref, bg_ref = refs[:7]
        mg_ref, pa_ref, pb_ref = refs[7 + n:10 + n]
        i = pl.program_id(0)
        if n:
            send, pass_on, finish = _gather_phases(refs[10 + n:10 + 2 * n], *refs[10 + 2 * n:], _spans(gathering))
            pl.when(i == 0)(send)
            pl.when(i == steps - 1)(pass_on)
        pa = _dot(ya_ref[...], wa_ref[...])
        pb = _dot(yb_ref[...], wb_ref[...])
        sa = _sigmoid(ga_ref[...] + bg_ref[0:1, :])
        sb = _sigmoid(gb_ref[...] + bg_ref[1:2, :])
        mg_ref[...] = (sa * pa + sb * pb).astype(BF16)
        pa_ref[...] = pa.astype(BF16)
        pb_ref[...] = pb.astype(BF16)
        if n:
            pl.when(i == steps - 1)(finish)

    out = jax.ShapeDtypeStruct((s, D), BF16)
    res = pl.pallas_call(
        body, name="proj_merge", grid=(steps,),
        in_specs=[_rows(tm, D), _rows(tm, D), _full((D, D)), _full((D, D)),
                  _rows(tm, D, 5), _rows(tm, D, 6), _full((2, D))] + [ANY] * n,
        out_specs=[_rows(tm, D)] * 3 + [ANY] * n,
        out_shape=[out] * 3 + [jax.ShapeDtypeStruct(a.shape, a.dtype) for a in _arrays(gathering)],
        input_output_aliases={7 + w: 3 + w for w in range(n)},
        scratch_shapes=_gather_sems(n) if n else [],
        compiler_params=_params("arbitrary", communicates=bool(n)),
    )(ya, yb, wa, wb, z, z, bg, *_arrays(gathering))
    return res[0], res[1], res[2], res[3:]


def out_norm(merged, w_out, x, g_post, g_fpre, gathering):
    s = x.shape[0]
    tm = 512
    n = len(gathering)
    steps = s // tm

    def body(*refs):
        mg_ref, w_ref, x_ref, gp_ref, gf_ref = refs[:5]
        o_ref, x1_ref, h2_ref = refs[5 + n:8 + n]
        i = pl.program_id(0)
        if n:
            send, pass_on, finish = _gather_phases(refs[8 + n:8 + 2 * n], *refs[8 + 2 * n:], _spans(gathering))
            pl.when(i == 0)(send)
            pl.when(i == steps - 1)(pass_on)
        o = _dot(mg_ref[...], w_ref[...])
        ohat, _ = _rms(o)
        x1 = x_ref[...] + ohat * gp_ref[...]
        x1hat, _ = _rms(x1)
        o_ref[...] = o
        x1_ref[...] = x1
        h2_ref[...] = (x1hat * gf_ref[...]).astype(BF16)
        if n:
            pl.when(i == steps - 1)(finish)

    res = pl.pallas_call(
        body, name="out_norm", grid=(steps,),
        in_specs=[_rows(tm, D), _full((D, D)), _rows(tm, D), _full((1, D)), _full((1, D))] + [ANY] * n,
        out_specs=[_rows(tm, D)] * 3 + [ANY] * n,
        out_shape=[jax.ShapeDtypeStruct((s, D), F32), jax.ShapeDtypeStruct((s, D), F32),
                   jax.ShapeDtypeStruct((s, D), BF16)]
        + [jax.ShapeDtypeStruct(a.shape, a.dtype) for a in _arrays(gathering)],
        input_output_aliases={5 + w: 3 + w for w in range(n)},
        scratch_shapes=_gather_sems(n) if n else [],
        compiler_params=_params("arbitrary", communicates=bool(n)),
    )(merged, w_out, x, g_post, g_fpre, *_arrays(gathering))
    return res[0], res[1], res[2], res[3:]


def mm_ff1(h2, wg, gathering):
    s = h2.shape[0]
    tm = 1024
    n = len(gathering)
    ni = s // tm

    def body(*refs):
        a_ref, b_ref = refs[:2]
        o_ref, r_ref = refs[2 + n:4 + n]
        i, j = pl.program_id(0), pl.program_id(1)
        if n:
            send, pass_on, finish = _gather_phases(refs[4 + n:4 + 2 * n], *refs[4 + 2 * n:], _spans(gathering))
            pl.when((i == 0) & (j == 0))(send)
            pl.when((i == ni - 1) & (j == N_CHIPS // 2))(pass_on)
        a = _dot(a_ref[...], b_ref[...])
        o_ref[...] = a.astype(BF16)
        r = jnp.maximum(a, 0.0)
        r_ref[...] = (r * r).astype(BF16)
        if n:
            pl.when((i == ni - 1) & (j == N_CHIPS - 1))(finish)

    res = pl.pallas_call(
        body, name="mm_ff1", grid=(ni, N_CHIPS),
        in_specs=[pl.BlockSpec((tm, D), lambda i, j: (i, 0)), pl.BlockSpec((None, D, D), lambda i, j: (j, 0, 0))]
        + [ANY] * n,
        out_specs=[pl.BlockSpec((tm, D), lambda i, j: (i, j))] * 2 + [ANY] * n,
        out_shape=[jax.ShapeDtypeStruct((s, D_FF), BF16), jax.ShapeDtypeStruct((s, D_FF), BF16)]
        + [jax.ShapeDtypeStruct(a.shape, a.dtype) for a in _arrays(gathering)],
        input_output_aliases={2 + w: 2 + w for w in range(n)},
        scratch_shapes=_gather_sems(n) if n else [],
        compiler_params=_params("arbitrary", "arbitrary", communicates=bool(n)),
    )(h2, wg, *_arrays(gathering))
    return res[0], res[1], res[2:]


def ff2_loss(rl, w_ff2, x1, target, g_fpost):
    s = x1.shape[0]
    tm = 256

    def body(rl_ref, w_ref, x1_ref, t_ref, g_ref, dy_ref, df_ref, dg_ref, loss_ref):
        @pl.when(pl.program_id(0) == 0)
        def _():
            dg_ref[...] = jnp.zeros_like(dg_ref)
            loss_ref[...] = jnp.zeros_like(loss_ref)

        f = _dot(rl_ref[...], w_ref[...])
        fhat, r = _rms(f)
        err = x1_ref[...] + fhat * g_ref[...] - t_ref[...]
        loss_ref[...] += 0.5 * jnp.sum(jnp.mean(err * err, axis=-1, keepdims=True), axis=0, keepdims=True)
        dy = err * (1.0 / D)
        dy_ref[...] = dy
        dg_ref[...] += jnp.sum(dy * fhat, axis=0, keepdims=True)
        df_ref[...] = _rms_bwd(dy * g_ref[...], fhat, r).astype(BF16)

    return pl.pallas_call(
        body, name="ff2_loss", grid=(s // tm,),
        in_specs=[_rows(tm, D_FF), _full((D_FF, D)), _rows(tm, D), _rows(tm, D), _full((1, D))],
        out_specs=[_rows(tm, D), _rows(tm, D), _full((1, D)), _full((1, 1))],
        out_shape=[jax.ShapeDtypeStruct((s, D), F32), jax.ShapeDtypeStruct((s, D), BF16),
                   jax.ShapeDtypeStruct((1, D), F32), jax.ShapeDtypeStruct((1, 1), F32)],
        compiler_params=_params("arbitrary"),
    )(rl, w_ff2, x1, target, g_fpost)


def mm_tn(name, a, b, ta, tb, out_shape, out_spec):
    s = a.shape[0]

    def body(a_ref, b_ref, o_ref):
        o_ref[...] = _dot(a_ref[...], b_ref[...], TN)

    return pl.pallas_call(
        body, name=name, grid=(a.shape[1] // ta, b.shape[1] // tb),
        in_specs=[pl.BlockSpec((s, ta), lambda i, j: (0, i)), pl.BlockSpec((s, tb), lambda i, j: (0, j))],
        out_specs=out_spec, out_shape=jax.ShapeDtypeStruct(out_shape, F32),
        compiler_params=_params("parallel", "parallel"),
    )(a, b)


def mm_nt(name, a, w):
    s = a.shape[0]
    tm = 512

    def body(a_ref, w_ref, o_ref):
        o_ref[...] = _dot(a_ref[...], w_ref[...], NT).astype(BF16)

    return pl.pallas_call(
        body, name=name, grid=(s // tm,), in_specs=[_rows(tm, D), _full((D, D))], out_specs=_rows(tm, D),
        out_shape=jax.ShapeDtypeStruct((s, D), BF16), compiler_params=_params("parallel"),
    )(a, w)


def ff2_bwd(df, w_ff2, a):
    s = df.shape[0]
    tm = 1024

    def body(df_ref, w_ref, a_ref, da_ref):
        drl = _dot(df_ref[...], w_ref[...], NT)
        da_ref[...] = (drl * (2.0 * jnp.maximum(a_ref[...].astype(F32), 0.0))).astype(BF16)

    return pl.pallas_call(
        body, name="ff2_bwd", grid=(s // tm, D_FF // D),
        in_specs=[pl.BlockSpec((tm, D), lambda i, j: (i, 0)), pl.BlockSpec((D, D), lambda i, j: (j, 0)),
                  pl.BlockSpec((tm, D), lambda i, j: (i, j))],
        out_specs=pl.BlockSpec((tm, D), lambda i, j: (i, j)),
        out_shape=jax.ShapeDtypeStruct((s, D_FF), BF16), compiler_params=_params("parallel", "parallel"),
    )(df, w_ff2, a)


def ff1_bwd_norms(da, wg, x1, o, dy, g_fpre, g_post, swapping):
    s = x1.shape[0]
    tm = 512
    n = len(swapping)

    def body(*refs):
        da_ref, w_ref, x1_ref, o_ref, dy_ref, gf_ref, gp_ref = refs[:7]
        dx1_ref, do_ref, dgf_ref, dgp_ref = refs[7 + n:11 + n]
        acc_ref = refs[11 + 2 * n]
        i, k = pl.program_id(0), pl.program_id(1)
        if n:
            send, finish = _swap_phases(refs[7:7 + n], refs[11 + n:11 + 2 * n], *refs[12 + 2 * n:])
            pl.when((i == 0) & (k == 0))(send)

        @pl.when((i == 0) & (k == 0))
        def _():
            dgf_ref[...] = jnp.zeros_like(dgf_ref)
            dgp_ref[...] = jnp.zeros_like(dgp_ref)

        part = _dot(da_ref[...], w_ref[...], NT)

        @pl.when(k == 0)
        def _():
            acc_ref[...] = part

        @pl.when(k > 0)
        def _():
            acc_ref[...] += part

        @pl.when(k == N_CHIPS - 1)
        def _():
            dh2 = acc_ref[...]
            x1hat, r2 = _rms(x1_ref[...])
            dgf_ref[...] += jnp.sum(dh2 * x1hat, axis=0, keepdims=True)
            dx1 = dy_ref[...] + _rms_bwd(dh2 * gf_ref[...], x1hat, r2)
            ohat, r1 = _rms(o_ref[...])
            dgp_ref[...] += jnp.sum(dx1 * ohat, axis=0, keepdims=True)
            dx1_ref[...] = dx1
            do_ref[...] = _rms_bwd(dx1 * gp_ref[...], ohat, r1).astype(BF16)

        if n:
            pl.when((i == s // tm - 1) & (k == N_CHIPS - 1))(finish)

    row = pl.BlockSpec((tm, D), lambda i, k: (i, 0))
    vec = pl.BlockSpec((1, D), lambda i, k: (0, 0))
    res = pl.pallas_call(
        body, name="ff1_bwd_norms", grid=(s // tm, N_CHIPS),
        in_specs=[pl.BlockSpec((tm, D), lambda i, k: (i, k)), pl.BlockSpec((None, D, D), lambda i, k: (k, 0, 0)),
                  row, row, row, vec, vec] + [ANY] * n,
        out_specs=[row, row, vec, vec] + [ANY] * n,
        out_shape=[jax.ShapeDtypeStruct((s, D), F32), jax.ShapeDtypeStruct((s, D), BF16),
                   jax.ShapeDtypeStruct((1, D), F32), jax.ShapeDtypeStruct((1, D), F32)] + _swap_shapes(swapping),
        scratch_shapes=[pltpu.VMEM((tm, D), F32)] + (_swap_sems(n) if n else []),
        compiler_params=_params("arbitrary", "arbitrary", communicates=bool(n)),
    )(da, wg, x1, o, dy, g_fpre, g_post, *swapping)
    return res[0], res[1], res[2], res[3], res[4:]


def out_bwd_gates(do, w_out, pa, pb, z, bg):
    s = do.shape[0]
    tm = 512

    def body(do_ref, w_ref, pa_ref, pb_ref, ga_ref, gb_ref, bg_ref, dpa_ref, dpb_ref, dga_ref, dgb_ref, dbg_ref):
        @pl.when(pl.program_id(0) == 0)
        def _():
            dbg_ref[...] = jnp.zeros_like(dbg_ref)

        dm = _dot(do_ref[...], w_ref[...], NT)
        sa = _sigmoid(ga_ref[...] + bg_ref[0:1, :])
        sb = _sigmoid(gb_ref[...] + bg_ref[1:2, :])
        dpa_ref[...] = (dm * sa).astype(BF16)
        dpb_ref[...] = (dm * sb).astype(BF16)
        dga = dm * pa_ref[...].astype(F32) * (sa * (1.0 - sa))
        dgb = dm * pb_ref[...].astype(F32) * (sb * (1.0 - sb))
        dga_ref[...] = dga.astype(BF16)
        dgb_ref[...] = dgb.astype(BF16)
        dbg_ref[0:1, :] += jnp.sum(dga, axis=0, keepdims=True)
        dbg_ref[1:2, :] += jnp.sum(dgb, axis=0, keepdims=True)

    out = jax.ShapeDtypeStruct((s, D), BF16)
    return pl.pallas_call(
        body, name="out_bwd_gates", grid=(s // tm,),
        in_specs=[_rows(tm, D), _full((D, D)), _rows(tm, D), _rows(tm, D), _rows(tm, D, 5), _rows(tm, D, 6),
                  _full((2, D))],
        out_specs=[_rows(tm, D)] * 4 + [_full((2, D))],
        out_shape=[out] * 4 + [jax.ShapeDtypeStruct((2, D), F32)], compiler_params=_params("arbitrary"),
    )(do, w_out, pa, pb, z, z, bg)


def gating_bwd(z, dya, ln_g, ln_b, w_s, bs_t, swapping):
    s = z.shape[0]
    ones = functools.partial(jnp.ones, (8, CHUNK), BF16)
    n = len(swapping)

    def body(*refs):
        u_ref, v_ref, dya_ref, lg_ref, lb_ref, ws_ref, bst_ref = refs[:7]
        du_ref, dv_ref, dws_ref, dbs_ref, dlg_ref, dlb_ref = refs[7 + n:13 + n]
        dvn_ref = refs[13 + 2 * n]
        ci = pl.program_id(0)
        if n:
            send, finish = _swap_phases(refs[7:7 + n], refs[13 + n:13 + 2 * n], *refs[14 + 2 * n:])
            pl.when(ci == 0)(send)

        @pl.when(ci == 0)
        def _():
            dws_ref[...] = jnp.zeros_like(dws_ref)
            dbs_ref[...] = jnp.zeros_like(dbs_ref)
            dlg_ref[...] = jnp.zeros_like(dlg_ref)
            dlb_ref[...] = jnp.zeros_like(dlb_ref)

        ug, dug_du = _gelu_and_grad(u_ref[...].astype(F32))
        vg, dvg_dv = _gelu_and_grad(v_ref[...].astype(F32))
        vhat, rstd = _layer_norm(vg)
        vn = (vhat * lg_ref[...] + lb_ref[...]).astype(BF16)
        dya = dya_ref[...].astype(F32)
        for g in range(GROUPS):
            cols = slice(g * CHUNK, (g + 1) * CHUNK)
            ws = _tril_ws(ws_ref, g)
            mixed = _dot(ws, vn[:, cols]) + bst_ref[:, g:g + 1]
            du_ref[:, cols] = (dya[:, cols] * mixed * dug_du[:, cols]).astype(BF16)
            dmix = (dya[:, cols] * ug[:, cols]).astype(BF16)
            dbs_ref[g] += _dot(ones(), dmix, NT)
            dws_ref[g] += _dot(dmix, vn[:, cols], NT)
            dvn_ref[:, cols] = _dot(ws, dmix, TN)
        dvn = dvn_ref[...]
        dlg_ref[...] += jnp.sum(dvn * vhat, axis=0, keepdims=True)
        dlb_ref[...] += jnp.sum(dvn, axis=0, keepdims=True)
        dvh = dvn * lg_ref[...]
        dvg = rstd * (dvh - jnp.mean(dvh, axis=-1, keepdims=True)
                      - vhat * jnp.mean(dvh * vhat, axis=-1, keepdims=True))
        dv_ref[...] = (dvg * dvg_dv).astype(BF16)

        @pl.when(ci == pl.num_programs(0) - 1)
        def _():
            r = lax.broadcasted_iota(jnp.int32, (CHUNK, CHUNK), 0)
            c = lax.broadcasted_iota(jnp.int32, (CHUNK, CHUNK), 1)
            for g in range(GROUPS):
                dws_ref[g] = jnp.where(c <= r, dws_ref[g], 0.0)

        if n:
            pl.when(ci == pl.num_programs(0) - 1)(finish)

    out = jax.ShapeDtypeStruct((s, D), BF16)
    res = pl.pallas_call(
        body, name="gating_bwd", grid=(s // CHUNK,),
        in_specs=[_rows(CHUNK, D, 0), _rows(CHUNK, D, 1), _rows(CHUNK, D), _full((1, D)), _full((1, D)),
                  _full((GROUPS, CHUNK, CHUNK)), _full((CHUNK, GROUPS))] + [ANY] * n,
        out_specs=[_rows(CHUNK, D), _rows(CHUNK, D), _full((GROUPS, CHUNK, CHUNK)), _full((GROUPS, 8, CHUNK)),
                   _full((1, D)), _full((1, D))] + [ANY] * n,
        out_shape=[out, out, jax.ShapeDtypeStruct((GROUPS, CHUNK, CHUNK), F32),
                   jax.ShapeDtypeStruct((GROUPS, 8, CHUNK), F32),
                   jax.ShapeDtypeStruct((1, D), F32), jax.ShapeDtypeStruct((1, D), F32)] + _swap_shapes(swapping),
        scratch_shapes=[pltpu.VMEM((CHUNK, D), F32)] + (_swap_sems(n) if n else []),
        compiler_params=_params("arbitrary", communicates=bool(n)),
    )(z, z, dya, ln_g, ln_b, w_s, bs_t, *swapping)
    return (*res[:6], res[6:])


def attn_bwd(z, yb, dyb, lse, logc, ka, kb, scattering):
    s = z.shape[0]
    nq = s // ATT_T
    t = ATT_T
    grp = ATT_BWD_GROUP
    ngrp = HEADS // 2 // grp
    wide = 128 * grp
    qcol, kcol, vcol = 2 * D // wide, 3 * D // wide, 4 * D // wide
    scale = 1.0 / math.sqrt(HEAD_DIM)
    n = len(scattering)

    def body(*refs):
        q_ref, k_ref, v_ref, y_ref, dy_ref, lse_ref, lc_ref, ka_ref, kb_ref = refs[:9]
        dq_ref, dk_ref, dv_ref = refs[9 + n:12 + n]
        qa_s, qt_s, da_s, dt_s, dq_s, dkt_s, dvt_s = refs[12 + 2 * n:19 + 2 * n]
        gi, j = pl.program_id(0), pl.program_id(1)
        first, lane, ones = _head_masks()
        if n:
            send, finish = _scatter_phases(refs[9:9 + n], refs[12 + n:12 + 2 * n], *refs[19 + 2 * n:])
            pl.when((gi == 0) & (j == 0))(send)

        @pl.when(j == 0)
        def _():
            dq_s[...] = jnp.zeros_like(dq_s)
            for pr in range(grp):
                cols = slice(pr * 128, (pr + 1) * 128)
                for ib in range(nq):
                    rows = slice(ib * t, (ib + 1) * t)
                    q = q_ref[rows, cols].astype(F32) * scale
                    lse = lse_ref[rows, cols]
                    qa_s[pr, 0, ib] = jnp.where(first, q, _place3(lane, HEAD_DIM + 2 * AUG, _split3(-lse[:, 0:1]),
                                                                  ones(0, 2 * AUG))).astype(BF16)
                    qa_s[pr, 1, ib] = jnp.where(
                        first, _place3(lane, 2 * AUG, _split3(-lse[:, HEAD_DIM:HEAD_DIM + 1]), ones(1, 2 * AUG)),
                        q).astype(BF16)
                    qt_s[pr, ib, :, 0:t] = jnp.where(first, q, 0.0).T.astype(BF16)
                    qt_s[pr, ib, :, t:2 * t] = jnp.where(first, 0.0, q).T.astype(BF16)
                    do = dy_ref[rows, cols].astype(F32)
                    prod = do * y_ref[rows, cols].astype(F32)
                    dd0 = jnp.sum(jnp.where(first, prod, 0.0), axis=-1, keepdims=True)
                    dd1 = jnp.sum(jnp.where(first, 0.0, prod), axis=-1, keepdims=True)
                    da_s[pr, 0, ib] = jnp.where(first, do, _place3(lane, HEAD_DIM, _split3(-dd0), 0.0)).astype(BF16)
                    da_s[pr, 1, ib] = jnp.where(first, _place3(lane, 0, _split3(-dd1), 0.0), do).astype(BF16)
                    dt_s[pr, ib, :, 0:t] = jnp.where(first, do, 0.0).T.astype(BF16)
                    dt_s[pr, ib, :, t:2 * t] = jnp.where(first, 0.0, do).T.astype(BF16)

        keys = []
        for pr in range(grp):
            kj = k_ref[:, pr * 128:(pr + 1) * 128].astype(F32)
            vj = v_ref[:, pr * 128:(pr + 1) * 128].astype(F32)
            keys.append((
                jnp.where(first, kj, ka_ref[pr, 0] + kb_ref[pr, 0, pl.ds(j, 1), :]).astype(BF16),
                jnp.where(first, ka_ref[pr, 1] + kb_ref[pr, 1, pl.ds(j, 1), :], kj).astype(BF16),
                jnp.concatenate([jnp.where(first, kj, 0.0), jnp.where(first, 0.0, kj)], axis=0).astype(BF16),
                jnp.where(first, vj, ones(0, AUG)).astype(BF16),
                jnp.where(first, ones(1, AUG), vj).astype(BF16)))
        dkt_s[...] = jnp.zeros_like(dkt_s)
        dvt_s[...] = jnp.zeros_like(dvt_s)

        def step(i, _):
            lc = lc_ref[i - j]
            rows = pl.ds(pl.multiple_of(i * t, t), t)
            for pr in range(grp):
                k0a, k1a, kst, v0a, v1a = keys[pr]
                p0 = jnp.exp(_dot(qa_s[pr, 0, i], k0a, NT) + lc)
                p1 = jnp.exp(_dot(qa_s[pr, 1, i], k1a, NT) + lc)
                e0 = (p0 * _dot(da_s[pr, 0, i], v0a, NT)).astype(BF16)
                e1 = (p1 * _dot(da_s[pr, 1, i], v1a, NT)).astype(BF16)
                dq_s[pr, rows, :] += _dot(jnp.concatenate([e0, e1], axis=1), kst)
                dvt_s[pr] += _dot(dt_s[pr, i], jnp.concatenate([p0.astype(BF16), p1.astype(BF16)], axis=0))
                dkt_s[pr] += _dot(qt_s[pr, i], jnp.concatenate([e0, e1], axis=0))
            return 0

        lax.fori_loop(j, nq, step, 0)
        for pr in range(grp):
            dk_ref[:, pr * 128:(pr + 1) * 128] = dkt_s[pr].T.astype(BF16)
            dv_ref[:, pr * 128:(pr + 1) * 128] = dvt_s[pr].T.astype(BF16)

        @pl.when(j == nq - 1)
        def _():
            for pr in range(grp):
                dq_ref[:, pr * 128:(pr + 1) * 128] = (dq_s[pr] * scale).astype(BF16)

        if n:
            pl.when((gi == ngrp - 1) & (j == nq - 1))(finish)

    colblock = lambda c: pl.BlockSpec((s, wide), lambda g, j: (0, c + g))
    blk = lambda c: pl.BlockSpec((t, wide), lambda g, j: (j, c + g))
    out = jax.ShapeDtypeStruct((s, D), BF16)
    res = pl.pallas_call(
        body, name="attn_bwd", grid=(ngrp, nq),
        in_specs=[colblock(qcol), blk(kcol), blk(vcol), colblock(0), colblock(0), colblock(0),
                  _full((nq, t, t)), pl.BlockSpec((grp, 2, t, 128), lambda g, j: (g, 0, 0, 0)),
                  pl.BlockSpec((grp, 2, nq, 128), lambda g, j: (g, 0, 0, 0))] + [ANY] * n,
        out_specs=[colblock(0), blk(0), blk(0)] + [ANY] * n, out_shape=[out] * 3 + _scatter_shapes(scattering),
        scratch_shapes=[pltpu.VMEM((grp, 2, nq, t, 128), BF16), pltpu.VMEM((grp, nq, 128, 2 * t), BF16),
                        pltpu.VMEM((grp, 2, nq, t, 128), BF16), pltpu.VMEM((grp, nq, 128, 2 * t), BF16),
                        pltpu.VMEM((grp, s, 128), F32), pltpu.VMEM((grp, 128, t), F32),
                        pltpu.VMEM((grp, 128, t), F32)]
        + (_scatter_sems(n) if n else []),
        compiler_params=_params("arbitrary", "arbitrary", communicates=bool(n)),
    )(z, z, z, yb, dyb, lse, logc, ka, kb, *scattering)
    return res[0], res[1], res[2], res[3:]


def in_bwd_norm(dz, wg, x, dx1, g_pre, scattering):
    s = x.shape[0]
    tm = 512
    n = len(scattering)

    def body(*refs):
        dz_ref, w_ref, x_ref, dx1_ref, g_ref = refs[:5]
        dx_ref, dg_ref = refs[5 + n:7 + n]
        acc_ref = refs[7 + 2 * n]
        i, k = pl.program_id(0), pl.program_id(1)
        if n:
            send, finish = _scatter_phases(refs[5:5 + n], refs[7 + n:7 + 2 * n], *refs[8 + 2 * n:])
            pl.when((i == 0) & (k == 0))(send)

        @pl.when((i == 0) & (k == 0))
        def _():
            dg_ref[...] = jnp.zeros_like(dg_ref)

        part = _dot(dz_ref[...], w_ref[...], NT)

        @pl.when(k == 0)
        def _():
            acc_ref[...] = part

        @pl.when(k > 0)
        def _():
            acc_ref[...] += part

        @pl.when(k == N_CHIPS - 1)
        def _():
            dh = acc_ref[...]
            xhat, r = _rms(x_ref[...])
            dg_ref[...] += jnp.sum(dh * xhat, axis=0, keepdims=True)
            dx_ref[...] = dx1_ref[...] + _rms_bwd(dh * g_ref[...], xhat, r)

        if n:
            pl.when((i == s // tm - 1) & (k == N_CHIPS - 1))(finish)

    row = pl.BlockSpec((tm, D), lambda i, k: (i, 0))
    vec = pl.BlockSpec((1, D), lambda i, k: (0, 0))
    res = pl.pallas_call(
        body, name="in_bwd_norm", grid=(s // tm, N_CHIPS),
        in_specs=[pl.BlockSpec((tm, IN_SHARD), lambda i, k: (i, k)),
                  pl.BlockSpec((None, D, IN_SHARD), lambda i, k: (k, 0, 0)), row, row, vec] + [ANY] * n,
        out_specs=[row, vec] + [ANY] * n,
        out_shape=[jax.ShapeDtypeStruct((s, D), F32), jax.ShapeDtypeStruct((1, D), F32)]
        + _scatter_shapes(scattering),
        scratch_shapes=[pltpu.VMEM((tm, D), F32)] + (_scatter_sems(n) if n else []),
        compiler_params=_params("arbitrary", "arbitrary", communicates=bool(n)),
    )(dz, wg, x, dx1, g_pre, *scattering)
    return res[0], res[1], res[2:]


def _adamw_math(w, g, m, v):
    m = ADAM_B1 * m + (1.0 - ADAM_B1) * g
    v = ADAM_B2 * v + (1.0 - ADAM_B2) * (g * g)
    m_hat = m / (1.0 - ADAM_B1 ** ADAM_STEP)
    v_hat = v / (1.0 - ADAM_B2 ** ADAM_STEP)
    delta = -ADAM_LR * (m_hat / (jnp.sqrt(v_hat) + ADAM_EPS) + ADAM_WD * w)
    return delta, m, v


def adamw(name, w, g, m, v, tr):
    r, c = w.shape

    def body(w_ref, g_ref, m_ref, v_ref, go_ref, d_ref, nm_ref, nv_ref):
        g = g_ref[...]
        go_ref[...] = g
        d_ref[...], nm_ref[...], nv_ref[...] = _adamw_math(w_ref[...], g, m_ref[...], v_ref[...])

    out = jax.ShapeDtypeStruct((r, c), F32)
    return pl.pallas_call(
        body, name=name, grid=(r // tr,), in_specs=[_rows(tr, c)] * 4, out_specs=[_rows(tr, c)] * 4,
        out_shape=[out] * 4, compiler_params=_params("parallel"),
    )(w, g, m, v)


def add_halves(name, g, recv, c_idx, tr):
    n, h, c = recv.shape

    def body(c_ref, g_ref, r_ref, o_ref):
        o_ref[...] = (g_ref[...] + r_ref[...]).astype(BF16)

    nb = h // tr
    return pl.pallas_call(
        body, name=name,
        grid_spec=pltpu.PrefetchScalarGridSpec(
            num_scalar_prefetch=1, grid=(n, nb),
            in_specs=[pl.BlockSpec((None, tr, c), lambda k, i, c_ref: (k, c_ref[0] * nb + i, 0)),
                      pl.BlockSpec((None, tr, c), lambda k, i, c_ref: (k, i, 0))],
            out_specs=pl.BlockSpec((None, tr, c), lambda k, i, c_ref: (k, i, 0))),
        out_shape=jax.ShapeDtypeStruct((n, h, c), BF16), compiler_params=_params("parallel", "parallel"),
    )(c_idx, g, recv)


def sum_chips(name, parts, recv, where, tr):
    n, h, c = recv.shape
    nb = h // tr

    def body(w_ref, p_ref, r_ref, o_ref):
        acc = p_ref[...].astype(F32)
        for k in range(n):
            acc = acc + r_ref[k].astype(F32)
        o_ref[...] = acc

    return pl.pallas_call(
        body, name=name,
        grid_spec=pltpu.PrefetchScalarGridSpec(
            num_scalar_prefetch=1, grid=(nb,),
            in_specs=[pl.BlockSpec((None, tr, c), lambda i, w_ref: (w_ref[0], i, 0)),
                      pl.BlockSpec((n, tr, c), lambda i, w_ref: (0, i, 0))],
            out_specs=pl.BlockSpec((tr, c), lambda i, w_ref: (w_ref[1] * nb + i, 0))),
        out_shape=jax.ShapeDtypeStruct((2 * h, c), F32), compiler_params=_params("parallel"),
    )(where, parts, recv)


def place_shard(name, shard, where, dtype, tr):
    r, c = shard.shape

    def body(w_ref, s_ref, o_ref):
        o_ref[...] = s_ref[...].astype(dtype)

    return pl.pallas_call(
        body, name=name,
        grid_spec=pltpu.PrefetchScalarGridSpec(
            num_scalar_prefetch=1, grid=(r // tr,),
            in_specs=[pl.BlockSpec((tr, c), lambda i, w_ref: (i, 0))],
            out_specs=pl.BlockSpec((None, tr, c), lambda i, w_ref: (w_ref[0], i, 0))),
        out_shape=jax.ShapeDtypeStruct((N_CHIPS, r, c), dtype), compiler_params=_params("parallel"),
    )(where, shard)


ANY = pl.BlockSpec(memory_space=pl.ANY)


def _place():
    x, y, c = lax.axis_index("x"), lax.axis_index("y"), lax.axis_index("c")
    chips = [(1 - x, y), (x, 1 - y), (1 - x, 1 - y)]
    return x, y, c, chips


def gather_shards(arrays):
    n = len(arrays)

    def body(*refs):
        send, pass_on, finish = _gather_phases(refs[n:2 * n], *refs[2 * n:], _spans(arrays))
        send()
        pass_on()
        finish()

    return pl.pallas_call(
        body, name="gather_shards", in_specs=[ANY] * n, out_specs=[ANY] * n,
        out_shape=[jax.ShapeDtypeStruct(a.shape, a.dtype) for a in _arrays(arrays)],
        input_output_aliases={w: w for w in range(n)}, scratch_shapes=_gather_sems(n),
        compiler_params=pltpu.CompilerParams(has_side_effects=True),
    )(*_arrays(arrays))


def _gather_sems(n):
    return [pltpu.SemaphoreType.DMA((6 * n,)), pltpu.SemaphoreType.DMA((6 * n,))]


class Span(typing.NamedTuple):
    array: jax.Array
    lo: int
    hi: int
    ways: tuple = (0, 1, 2)


def _arrays(gathering):
    return [g.array if isinstance(g, Span) else g for g in gathering]


def _spans(gathering):
    return [(g.lo, g.hi, g.ways) if isinstance(g, Span) else (0, g.shape[1], (0, 1, 2)) for g in gathering]


def _gather_phases(out, send_sems, recv_sems, spans):
    n = len(out)
    if not any(ways for _, _, ways in spans):
        return (lambda: None,) * 3
    x, y, c, chips = _place()
    me = 2 * x + y
    sibling = (x, y, 1 - c)

    def half(w, chip, core):
        lo, hi, _ = spans[w]
        h = (hi - lo) // 2
        return out[w].at[chip, pl.ds(lo + core * h, h)]

    def copy(k, block, to):
        return pltpu.make_async_remote_copy(src_ref=block, dst_ref=block, send_sem=send_sems.at[k],
                                            recv_sem=recv_sems.at[k], device_id=to, device_id_type=MESH)

    def over_ici(w, j, chip):
        return copy(3 * w + j, half(w, chip, c), (chips[j][0], chips[j][1], c))

    def over_d2d(w, j, core):
        return copy(3 * n + 3 * w + j, half(w, 2 * chips[j][0] + chips[j][1], core), sibling)

    pairs = [(w, j) for w in range(n) for j in spans[w][2]]

    def send():
        for w, j in pairs:
            over_ici(w, j, me).start()

    def pass_on():
        for w, j in pairs:
            over_ici(w, j, 2 * chips[j][0] + chips[j][1]).wait_recv()
            over_d2d(w, j, c).start()

    def finish():
        for w, j in pairs:
            over_d2d(w, j, 1 - c).wait_recv()
        for w, j in pairs:
            over_ici(w, j, me).wait_send()
            over_d2d(w, j, c).wait_send()

    return send, pass_on, finish


def _relay_sems():
    return [pltpu.SemaphoreType.DMA((4,)), pltpu.SemaphoreType.DMA((4,))]


def _relay_phases(out, send_sems, recv_sems):
    x, y, c, chips = _place()
    sibling = (x, y, 1 - c)
    rows = out.shape[1]
    quarter = rows // 4
    far = 2 * chips[2][0] + chips[2][1]

    def piece(chip, way, core):
        return out.at[chip, pl.ds(way * (rows // 2) + core * quarter, quarter)]

    def copy(k, block, to):
        return pltpu.make_async_remote_copy(src_ref=block, dst_ref=block, send_sem=send_sems.at[k],
                                            recv_sem=recv_sems.at[k], device_id=to, device_id_type=MESH)

    def over_ici(way, chip):
        return copy(way, piece(chip, way, c), (chips[way][0], chips[way][1], c))

    def over_d2d(way, core):
        return copy(2 + way, piece(far, way, core), sibling)

    def send():
        for way in range(2):
            other = chips[1 - way]
            over_ici(way, 2 * other[0] + other[1]).start()

    def pass_on():
        for way in range(2):
            over_ici(way, far).wait_recv()
            over_d2d(way, c).start()

    def finish():
        for way in range(2):
            over_d2d(way, 1 - c).wait_recv()
        for way in range(2):
            other = chips[1 - way]
            over_ici(way, 2 * other[0] + other[1]).wait_send()
            over_d2d(way, c).wait_send()

    return send, pass_on, finish


def swap_halves(name, grads):
    n = len(grads)

    def body(*refs):
        send, finish = _swap_phases(refs[:n], refs[n:2 * n], *refs[2 * n:])
        send()
        finish()

    return pl.pallas_call(
        body, name=name, in_specs=[ANY] * n, out_specs=[ANY] * n, out_shape=_swap_shapes(grads),
        scratch_shapes=_swap_sems(n), compiler_params=pltpu.CompilerParams(has_side_effects=True),
    )(*grads)


def _swap_shapes(grads):
    return [jax.ShapeDtypeStruct((a.shape[0], a.shape[1] // 2, a.shape[2]), a.dtype) for a in grads]


def _swap_sems(n):
    return [pltpu.SemaphoreType.DMA((n,)), pltpu.SemaphoreType.DMA((n,))]


def _swap_phases(g, out, send_sems, recv_sems):
    x, y, c, _ = _place()

    def copies():
        return [pltpu.make_async_remote_copy(
            src_ref=g[w].at[:, pl.ds((1 - c) * (g[w].shape[1] // 2), g[w].shape[1] // 2)], dst_ref=out[w],
            send_sem=send_sems.at[w], recv_sem=recv_sems.at[w], device_id=(x, y, 1 - c), device_id_type=MESH)
            for w in range(len(g))]

    def send():
        for cp in copies():
            cp.start()

    def finish():
        for cp in copies():
            cp.wait()

    return send, finish


def _send_phases(g, out, send_sems, recv_sems):
    x, y, c, _ = _place()

    def copies():
        return [pltpu.make_async_remote_copy(
            src_ref=g[w], dst_ref=out[w], send_sem=send_sems.at[w], recv_sem=recv_sems.at[w],
            device_id=(x, y, 1 - c), device_id_type=MESH) for w in range(len(g))]

    def send():
        for cp in copies():
            cp.start()

    def finish():
        for cp in copies():
            cp.wait()

    return send, finish


def dw_in_half(name, h, dz, which, sending):
    s = h.shape[0]
    hh, tb = D // 2, IN_SHARD // 2
    n = len(sending)
    steps = IN_COLS // tb

    def body(w_ref, *refs):
        a_ref, b_ref, o_ref = refs[0], refs[1], refs[2 + n]
        j = pl.program_id(0)
        if n:
            send, finish = _send_phases(refs[2:2 + n], refs[3 + n:3 + 2 * n], *refs[3 + 2 * n:])
            pl.when(j == 0)(send)
        o_ref[...] = _dot(a_ref[...], b_ref[...], TN)
        if n:
            pl.when(j == steps - 1)(finish)

    out = pl.pallas_call(
        body, name=name,
        grid_spec=pltpu.PrefetchScalarGridSpec(
            num_scalar_prefetch=1, grid=(steps,),
            in_specs=[pl.BlockSpec((s, hh), lambda j, w: (0, w[0])), pl.BlockSpec((s, tb), lambda j, w: (0, j))]
            + [ANY] * n,
            out_specs=[pl.BlockSpec((None, hh, tb), lambda j, w: (j // 2, 0, j % 2))] + [ANY] * n,
            scratch_shapes=_swap_sems(n) if n else []),
        out_shape=[jax.ShapeDtypeStruct((N_CHIPS, hh, IN_SHARD), F32)]
        + [jax.ShapeDtypeStruct(a.shape, a.dtype) for a in sending],
        compiler_params=_params("arbitrary", communicates=bool(n)),
    )(which, h, dz, *sending)
    return out[0], out[1:]


def scatter_chips(parts):
    n = len(parts)

    def body(*refs):
        send, finish = _scatter_phases(refs[:n], refs[n:2 * n], *refs[2 * n:])
        send()
        finish()

    return pl.pallas_call(
        body, name="scatter_chips", in_specs=[ANY] * n, out_specs=[ANY] * n,
        out_shape=_scatter_shapes(parts), scratch_shapes=_scatter_sems(n),
        compiler_params=pltpu.CompilerParams(has_side_effects=True),
    )(*parts)


def _scatter_shapes(parts):
    return [jax.ShapeDtypeStruct((3,) + a.shape[1:], a.dtype) for a in parts]


def _scatter_sems(n):
    return [pltpu.SemaphoreType.DMA((3 * n,)), pltpu.SemaphoreType.DMA((3 * n,))]


def _scatter_phases(p, out, send_sems, recv_sems):
    x, y, c, chips = _place()

    def copies():
        return [pltpu.make_async_remote_copy(
            src_ref=p[w].at[2 * px + py], dst_ref=out[w].at[j], send_sem=send_sems.at[3 * w + j],
            recv_sem=recv_sems.at[3 * w + j], device_id=(px, py, c), device_id_type=MESH)
            for w in range(len(p)) for j, (px, py) in enumerate(chips)]

    def send():
        for cp in copies():
            cp.start()

    def finish():
        for cp in copies():
            cp.wait()

    return send, finish


def join_halves(arrays):
    n = len(arrays)

    def body(*refs):
        out = refs[n:2 * n]
        send_sems, recv_sems = refs[2 * n:]
        x, y, c, _ = _place()

        def copy(w, core):
            h = out[w].shape[0] // 2
            rows = out[w].at[pl.ds(core * h, h)]
            return pltpu.make_async_remote_copy(
                src_ref=rows, dst_ref=rows, send_sem=send_sems.at[w], recv_sem=recv_sems.at[w],
                device_id=(x, y, 1 - c), device_id_type=MESH)

        for w in range(n):
            copy(w, c).start()
        for w in range(n):
            copy(w, 1 - c).wait_recv()
        for w in range(n):
            copy(w, c).wait_send()

    return pl.pallas_call(
        body, name="join_halves", in_specs=[ANY] * n, out_specs=[ANY] * n,
        out_shape=[jax.ShapeDtypeStruct(a.shape, a.dtype) for a in arrays],
        input_output_aliases={w: w for w in range(n)},
        scratch_shapes=[pltpu.SemaphoreType.DMA((n,)), pltpu.SemaphoreType.DMA((n,))],
        compiler_params=pltpu.CompilerParams(has_side_effects=True),
    )(*arrays)


def allreduce_small(packed):
    r, c = packed.shape
    n_dev = 8

    def body(x_ref, all_ref, sum_ref, send_sems, recv_sems, local_sem):
        x, y, cc, chips = _place()
        me, sibling = (x, y, cc), (x, y, 1 - cc)

        def rows(px, py, pc):
            return all_ref.at[4 * px + 2 * py + pc]

        def copy(k, block, to, src=None):
            return pltpu.make_async_remote_copy(
                src_ref=rows(*block) if src is None else src, dst_ref=rows(*block), send_sem=send_sems.at[k],
                recv_sem=recv_sems.at[k], device_id=to, device_id_type=MESH)

        mine = pltpu.make_async_copy(x_ref, rows(*me), local_sem)
        mine.start()
        first = [copy(0, me, sibling, src=x_ref)]
        first += [copy(1 + j, me, (*chip, cc), src=x_ref) for j, chip in enumerate(chips)]
        for cp in first:
            cp.start()
        passed = [copy(4 + j, (*chip, cc), sibling) for j, chip in enumerate(chips)]
        for j, chip in enumerate(chips):
            copy(1 + j, (*chip, cc), me).wait_recv()
            passed[j].start()
        copy(0, sibling, me).wait_recv()
        for j, chip in enumerate(chips):
            copy(4 + j, (*chip, 1 - cc), me).wait_recv()
        for cp in first + passed:
            cp.wait_send()
        mine.wait()
        acc = all_ref[0]
        for k in range(1, n_dev):
            acc = acc + all_ref[k]
        sum_ref[...] = acc

    vm = pl.BlockSpec(memory_space=pltpu.VMEM)
    return pl.pallas_call(
        body, name="allreduce_small", in_specs=[vm], out_specs=[vm, vm],
        out_shape=[jax.ShapeDtypeStruct((n_dev, r, c), F32), jax.ShapeDtypeStruct((r, c), F32)],
        scratch_shapes=[pltpu.SemaphoreType.DMA((7,)), pltpu.SemaphoreType.DMA((7,)), pltpu.SemaphoreType.DMA],
        compiler_params=pltpu.CompilerParams(has_side_effects=True, vmem_limit_bytes=VMEM_LIMIT),
    )(packed)[1]


def local_step(x, target, vecs, w_s, bs_t, bg, wg_in, late, core=None, order=None):
    on_mesh = core is not None

    def add(names, grads, recv):
        return [add_halves("add_" + n, g, r, core, min(r.shape[1], 256)) for n, g, r in zip(names, grads, recv)]

    g_pre, ln_g, ln_b, g_post, g_fpre, g_fpost = vecs
    s = x.shape[0]
    if order is None:
        order = jnp.arange(N_CHIPS, dtype=jnp.int32)
    logc = _attn_tables(s)
    ka, kb = _alibi_tables(s)

    h = norm_pre(x, g_pre)
    if not on_mesh:
        wg_a, wg_b, wg_out, wg_ff1, wg_ff2 = late
    if on_mesh:
        cut = D // 4
        z, (wg_in,), (wg_a, wg_b, wg_out, wg_ff1, wg_ff2) = mm_in(
            "mm_in_own", h, wg_in, order, 0, 1, None, [Span(wg_in, 0, D, (0, 1))], casting=late)
        z, (wg_in,), _ = mm_in("mm_in_near", h, wg_in, order, 1, 2, z, [Span(wg_in, 0, D, ())], relay=True)
        z, (wg_in, wg_a), _ = mm_in("mm_in_far", h, wg_in, order, 3, 1, z, [Span(wg_in, 0, D, ()), wg_a])
        ya, (wg_b, wg_ff2) = gating_fwd(z, ln_g, ln_b, w_s, bs_t, [wg_b, Span(wg_ff2, 0, cut)])
        yb, lse, (wg_ff1, wg_ff2, bg) = attn_fwd(z, logc, ka, kb, [wg_ff1, Span(wg_ff2, cut, 3 * cut), bg])
        bg = jnp.transpose(bg[:, :2, :], (1, 0, 2)).reshape(2, D)
    else:
        z, _, _ = mm_in("mm_in", h, wg_in, order, 0, N_CHIPS, None, [Span(wg_in, 0, D, ())])
        ya, _ = gating_fwd(z, ln_g, ln_b, w_s, bs_t, [])
        yb, lse, _ = attn_fwd(z, logc, ka, kb, [])
    merged, pa, pb, got = proj_merge(ya, yb, wg_a.reshape(D, D), wg_b.reshape(D, D), z, bg, [wg_out] if on_mesh else [])
    wg_out = got[0] if on_mesh else wg_out
    w_out = wg_out.reshape(D, D)
    o, x1, h2, got = out_norm(merged, w_out, x, g_post, g_fpre, [Span(wg_ff2, 3 * D // 4, D)] if on_mesh else [])
    a, rl, _ = mm_ff1(h2, wg_ff1, [])
    w_ff2 = (got[0] if on_mesh else wg_ff2).reshape(D_FF, D)
    dy, df, d_gfpost, loss = ff2_loss(rl, w_ff2, x1, target, g_fpost)

    half_cols = pl.BlockSpec((D, D // 2), lambda i, j: (0, j))
    d_wff2 = mm_tn("dw_ff2", rl, df, D // 2, D, (D_FF, D), pl.BlockSpec((D // 2, D), lambda i, j: (i, 0)))
    da = ff2_bwd(df, w_ff2, a)
    d_wff1 = mm_tn("dw_ff1", h2, da, D, D // 2, (N_CHIPS, D, D),
                   pl.BlockSpec((None, D, D // 2), lambda i, j: (j // 2, 0, j % 2)))
    d_ff = [d_wff1, d_wff2.reshape(N_CHIPS, D, D)]
    dx1, do, d_gfpre, d_gpost, recv_ff = ff1_bwd_norms(da, wg_ff1, x1, o, dy, g_fpre, g_post, d_ff if on_mesh else [])
    d_wout = mm_tn("dw_out", merged, do, D, D // 2, (D, D), half_cols)
    dpa, dpb, dga, dgb, d_bg = out_bwd_gates(do, w_out, pa, pb, z, bg)
    d_wa = mm_tn("dw_a", ya, dpa, D, D // 2, (D, D), half_cols)
    d_wb = mm_tn("dw_b", yb, dpb, D, D // 2, (D, D), half_cols)
    dya = mm_nt("dy_a", dpa, wg_a.reshape(D, D))
    dyb = mm_nt("dy_b", dpb, wg_b.reshape(D, D))
    d_proj = [d_wa.reshape(N_CHIPS, D // N_CHIPS, D), d_wb.reshape(N_CHIPS, D // N_CHIPS, D),
              d_wout.reshape(N_CHIPS, D // N_CHIPS, D)]
    du, dv, d_ws, d_bs, d_lng, d_lnb, recv_proj = gating_bwd(z, dya, ln_g, ln_b, w_s, bs_t, d_proj if on_mesh else [])
    early = d_proj + d_ff
    parts_early = add(BIG[1:], early, list(recv_proj) + list(recv_ff)) if on_mesh else []
    dq, dk, dvb, got_early = attn_bwd(z, yb, dyb, lse, logc, ka, kb, parts_early)
    dz = jnp.concatenate([du, dv, dq, dk, dvb, dga, dgb], axis=1)
    if on_mesh:
        for_sibling, _ = dw_in_half("dw_in_sibling", h, dz, 1 - core, [])
        mine, from_sibling = dw_in_half("dw_in_mine", h, dz, core, [for_sibling])
        d_win = None
        parts_late = [add_halves("add_w_in", mine, from_sibling[0], jnp.zeros((1,), jnp.int32), 256)]
    else:
        half = IN_SHARD // 2
        d_win = mm_tn("dw_in", h, dz, D, half, (N_CHIPS, D, IN_SHARD),
                      pl.BlockSpec((None, D, half), lambda i, j: (j // 2, 0, j % 2)))
        parts_late = []
    dx, d_gpre, got_late = in_bwd_norm(dz, wg_in, x, dx1, g_pre, parts_late)

    small = dict(norm_mix_pre=d_gpre, b_gate=d_bg, ln_v_g=d_lng, ln_v_b=d_lnb, w_s=d_ws, b_s=d_bs[:, 0, :],
                 norm_mix_post=d_gpost, norm_ffn_pre=d_gfpre, norm_ffn_post=d_gfpost)
    return loss[0, 0], dx, [d_win] + early, small, parts_late + parts_early, list(got_late) + list(got_early)


BIG = ("w_in", "w_a_proj", "w_b_proj", "w_out", "w_ff1", "w_ff2")
SMALL = ("norm_mix_pre", "ln_v_g", "ln_v_b", "b_s", "norm_mix_post", "norm_ffn_pre", "norm_ffn_post", "w_s", "b_gate")
ORDER = ("norm_mix_pre", "w_in", "b_gate", "ln_v_g", "ln_v_b", "w_s", "b_s", "w_a_proj", "w_b_proj", "w_out",
         "norm_mix_post", "norm_ffn_pre", "w_ff1", "w_ff2", "norm_ffn_post")
VEC_ROWS = D // 128
WS_ROW = 7 * VEC_ROWS
BG_ROW = WS_ROW + GROUPS * CHUNK
LOSS_ROW = BG_ROW + 2 * VEC_ROWS
PACK_ROWS = LOSS_ROW + 8


def pack_small(small, loss):
    vectors = [small[n] for n in SMALL[:7]]

    def body(*refs):
        out = refs[-1]
        ws_ref, bg_ref, loss_ref = refs[7:10]
        for i, n in enumerate(SMALL[:7]):
            if n == "b_s":
                out[i * VEC_ROWS:(i + 1) * VEC_ROWS, :] = refs[i][...]
            else:
                for j in range(VEC_ROWS):
                    out[i * VEC_ROWS + j:i * VEC_ROWS + j + 1, :] = refs[i][:, j * 128:(j + 1) * 128]
        for g in range(GROUPS):
            out[WS_ROW + g * CHUNK:WS_ROW + (g + 1) * CHUNK, :] = ws_ref[g]
        for r in range(2):
            for j in range(VEC_ROWS):
                row = BG_ROW + r * VEC_ROWS + j
                out[row:row + 1, :] = bg_ref[r:r + 1, j * 128:(j + 1) * 128]
        lane = lax.broadcasted_iota(jnp.int32, (8, 128), 1)
        sub = lax.broadcasted_iota(jnp.int32, (8, 128), 0)
        out[LOSS_ROW:LOSS_ROW + 8, :] = jnp.where((lane == 0) & (sub == 0), loss_ref[...], 0.0)

    return pl.pallas_call(
        body, name="pack_small", out_shape=jax.ShapeDtypeStruct((PACK_ROWS, 128), F32),
        compiler_params=_params(),
    )(*vectors, small["w_s"], small["b_gate"], loss)


def adamw_small(summed, chip, w, m, v):
    shapes = {n: (1, D) for n in SMALL}
    shapes.update(b_s=(GROUPS, CHUNK), w_s=(GROUPS * CHUNK, CHUNK), b_gate=(2, D // N_CHIPS))
    flat = lambda t: [t[n].reshape(shapes[n]) for n in SMALL]
    per = D // N_CHIPS // 128

    def body(chip_ref, sum_ref, *refs):
        params, outs = refs[:27], refs[27:]
        sub = lax.broadcasted_iota(jnp.int32, (VEC_ROWS, 128), 0)

        def gate_row(r):
            rows = sum_ref[BG_ROW + r * VEC_ROWS:BG_ROW + (r + 1) * VEC_ROWS, :]
            return jnp.concatenate([jnp.sum(jnp.where(sub == per * chip_ref[0] + j, rows, 0.0), axis=0, keepdims=True)
                                    for j in range(per)], axis=1)

        for i, n in enumerate(SMALL):
            if n == "b_s":
                g = sum_ref[i * VEC_ROWS:(i + 1) * VEC_ROWS, :]
            elif n == "w_s":
                g = sum_ref[WS_ROW:BG_ROW, :]
            elif n == "b_gate":
                g = jnp.concatenate([gate_row(0), gate_row(1)], axis=0)
            else:
                g = jnp.concatenate([sum_ref[i * VEC_ROWS + j:i * VEC_ROWS + j + 1, :] for j in range(VEC_ROWS)],
                                    axis=1)
            delta, nm, nv = _adamw_math(params[i][...], g, params[9 + i][...], params[18 + i][...])
            outs[4 * i][...], outs[4 * i + 1][...], outs[4 * i + 2][...], outs[4 * i + 3][...] = g, delta, nm, nv

    vm = pl.BlockSpec(memory_space=pltpu.VMEM)
    res = pl.pallas_call(
        body, name="adamw_small",
        in_specs=[pl.BlockSpec(memory_space=pltpu.SMEM)] + [vm] * 28, out_specs=[vm] * 36,
        out_shape=[jax.ShapeDtypeStruct(shapes[n], F32) for n in SMALL for _ in range(4)],
        compiler_params=_params(),
    )(chip, summed, *flat(w), *flat(m), *flat(v))
    return {n: tuple(r.reshape(w[n].shape) for r in res[4 * i:4 * i + 4]) for i, n in enumerate(SMALL)}


def kernel(x, norm_mix_pre, w_in, b_gate, ln_v_g, ln_v_b, w_s, b_s, w_a_proj, w_b_proj, w_out, norm_mix_post, norm_ffn_pre, w_ff1, w_ff2, norm_ffn_post, loss_target, m_norm_mix_pre, m_w_in, m_b_gate, m_ln_v_g, m_ln_v_b, m_w_s, m_b_s, m_w_a_proj, m_w_b_proj, m_w_out, m_norm_mix_post, m_norm_ffn_pre, m_w_ff1, m_w_ff2, m_norm_ffn_post, v_norm_mix_pre, v_w_in, v_b_gate, v_ln_v_g, v_ln_v_b, v_w_s, v_b_s, v_w_a_proj, v_w_b_proj, v_w_out, v_norm_mix_post, v_norm_ffn_pre, v_w_ff1, v_w_ff2, v_norm_ffn_post):
    w = dict(norm_mix_pre=norm_mix_pre, w_in=w_in, b_gate=b_gate, ln_v_g=ln_v_g, ln_v_b=ln_v_b, w_s=w_s, b_s=b_s,
             w_a_proj=w_a_proj, w_b_proj=w_b_proj, w_out=w_out, norm_mix_post=norm_mix_post,
             norm_ffn_pre=norm_ffn_pre, w_ff1=w_ff1, w_ff2=w_ff2, norm_ffn_post=norm_ffn_post)
    m = dict(norm_mix_pre=m_norm_mix_pre, w_in=m_w_in, b_gate=m_b_gate, ln_v_g=m_ln_v_g, ln_v_b=m_ln_v_b, w_s=m_w_s,
             b_s=m_b_s, w_a_proj=m_w_a_proj, w_b_proj=m_w_b_proj, w_out=m_w_out, norm_mix_post=m_norm_mix_post,
             norm_ffn_pre=m_norm_ffn_pre, w_ff1=m_w_ff1, w_ff2=m_w_ff2, norm_ffn_post=m_norm_ffn_post)
    v = dict(norm_mix_pre=v_norm_mix_pre, w_in=v_w_in, b_gate=v_b_gate, ln_v_g=v_ln_v_g, ln_v_b=v_ln_v_b, w_s=v_w_s,
             b_s=v_b_s, w_a_proj=v_w_a_proj, w_b_proj=v_w_b_proj, w_out=v_w_out, norm_mix_post=v_norm_mix_post,
             norm_ffn_pre=v_norm_ffn_pre, w_ff1=v_w_ff1, w_ff2=v_w_ff2, norm_ffn_post=v_norm_ffn_post)
    chip = 2 * lax.axis_index("x") + lax.axis_index("y")
    core = lax.axis_index("c")

    where = jnp.stack([chip, core]).astype(jnp.int32)
    wg_in = place_shard("place_w_in", w_in[0], where, BF16, 256)
    bg_all = place_shard("place_b_gate", jnp.pad(b_gate[0], ((0, 14), (0, 0))), where, F32, 16)
    vecs = (norm_mix_pre, ln_v_g, ln_v_b, norm_mix_post, norm_ffn_pre, norm_ffn_post)
    loss, dx, _, small, parts, got = local_step(
        x[0], loss_target[0], vecs, w_s[0], b_s[0].T, bg_all, wg_in, [w[n][0] for n in BIG[1:]],
        core=jnp.reshape(core, (1,)).astype(jnp.int32),
        order=jnp.stack([chip, chip ^ 2, chip ^ 1, chip ^ 3]).astype(jnp.int32))

    halves = [sum_chips("sum_" + n, p, r, where, min(p.shape[1], 256)) for n, p, r in zip(BIG, parts, got)]
    grads = dict(zip(BIG, join_halves(halves)))

    summed = allreduce_small(pack_small(small, loss.reshape(1, 1)))
    loss = summed[LOSS_ROW, 0]

    new = adamw_small(summed, jnp.reshape(chip, (1,)).astype(jnp.int32), w, m, v)
    for n in BIG:
        shape = w[n].shape
        res = adamw("adamw_" + n, w[n][0], grads[n], m[n][0], v[n][0], min(shape[1], 256))
        new[n] = tuple(r.reshape(shape) for r in res)

    outs = [loss, dx[None]]
    for i in range(4):
        outs += [new[n][i] for n in ORDER]
    return tuple(outs)
```

```python
import functools
import math
import typing

import numpy as np
import jax
import jax.numpy as jnp
from jax import lax
from jax.experimental import pallas as pl
from jax.experimental.pallas import tpu as pltpu

F32 = jnp.float32
BF16 = jnp.bfloat16
MESH = pl.DeviceIdType.MESH

D = 1024
EPS = 1e-6
CHUNK = 128
GROUPS = 8
HEADS = 16
HEAD_DIM = 64
ATT_T = 256
ATT_GROUP = 8
ATT_BWD_GROUP = 2
N_CHIPS = 4
D_FF = 4 * D
IN_COLS = 7 * D
IN_SHARD = IN_COLS // N_CHIPS
MASKED = -1e30
VMEM_LIMIT = 56 * 2 ** 20

ADAM_LR, ADAM_B1, ADAM_B2, ADAM_EPS, ADAM_WD, ADAM_STEP = 0.001, 0.9, 0.999, 1e-08, 0.01, 10

NN = (((1,), (0,)), ((), ()))
NT = (((1,), (1,)), ((), ()))
TN = (((0,), (0,)), ((), ()))


def _dot(a, b, dims=NN):
    return lax.dot_general(a, b, dims, preferred_element_type=F32)


def _params(*sem, communicates=False):
    return pltpu.CompilerParams(dimension_semantics=sem or None, vmem_limit_bytes=VMEM_LIMIT,
                                has_side_effects=communicates)


def _rows(tr, c, col=0):
    return pl.BlockSpec((tr, c), lambda i: (i, col))


def _full(shape):
    n = len(shape)
    return pl.BlockSpec(shape, lambda *_: (0,) * n)


def _gelu(x):
    k = math.sqrt(2.0 / math.pi)
    return 0.5 * x * (1.0 + jnp.tanh(k * (x + 0.044715 * x * x * x)))


def _gelu_and_grad(x):
    k = math.sqrt(2.0 / math.pi)
    t = jnp.tanh(k * (x + 0.044715 * x * x * x))
    g = 0.5 * x * (1.0 + t)
    dg = 0.5 * (1.0 + t) + 0.5 * x * (1.0 - t * t) * (k * (1.0 + 3.0 * 0.044715 * x * x))
    return g, dg


def _sigmoid(x):
    return 1.0 / (1.0 + jnp.exp(-x))


def _rms(x):
    r = lax.rsqrt(jnp.mean(x * x, axis=-1, keepdims=True) + EPS)
    return x * r, r


def _rms_bwd(dn, xhat, r):
    return r * (dn - xhat * jnp.mean(dn * xhat, axis=-1, keepdims=True))


def norm_pre(x, g):
    s = x.shape[0]
    tr = 512

    def body(x_ref, g_ref, h_ref):
        xhat, _ = _rms(x_ref[...])
        h_ref[...] = (xhat * g_ref[...]).astype(BF16)

    return pl.pallas_call(
        body, name="norm_pre", grid=(s // tr,),
        in_specs=[_rows(tr, D), _full((1, D))], out_specs=_rows(tr, D),
        out_shape=jax.ShapeDtypeStruct((s, D), BF16), compiler_params=_params("parallel"),
    )(x, g)


def mm_in(name, h, wg, order, first, count, z, gathering, relay=False, casting=()):
    s = h.shape[0]
    tm, tn = 1024, IN_SHARD // 2
    per = IN_SHARD // tn
    n, m = len(gathering), len(casting)
    nj, ni = count * per, s // tm
    has_z = z is not None
    arrays = _arrays(gathering)
    at = [k for k, a in enumerate(arrays) if a is wg][0]

    def body(order_ref, *refs):
        a_ref = refs[0]
        cast_in = refs[1 + has_z + n:1 + has_z + n + m]
        o_ref = refs[1 + has_z + n + m]
        held = refs[2 + has_z + n + m:2 + has_z + 2 * n + m]
        cast_out = refs[2 + has_z + 2 * n + m:2 + has_z + 2 * n + 2 * m]
        tile, tile_sem = refs[2 + has_z + 2 * n + 2 * m:4 + has_z + 2 * n + 2 * m]
        j, i = pl.program_id(0), pl.program_id(1)
        sems = refs[4 + has_z + 2 * n + 2 * m:]
        for src, dst in zip(cast_in, cast_out):
            dst[...] = src[...].astype(BF16)
        phases = [_gather_phases(held, *sems[:2], _spans(gathering))]
        if relay:
            phases.append(_relay_phases(held[at], *sems[2:]))
        def each(fs):
            def run():
                for f in fs:
                    f()
            return run

        send, pass_on, finish = [each(fs) for fs in zip(*phases)]

        def fetch(t):
            chip = order_ref[first + t // per]
            return pltpu.make_async_copy(held[at].at[chip, :, pl.ds((t % per) * tn, tn)], tile.at[t % 2],
                                         tile_sem.at[t % 2])

        @pl.when(i == 0)
        def _():
            @pl.when(j == 0)
            def _():
                send()
                fetch(0).start()

            fetch(j).wait()

            @pl.when(j + 1 < nj)
            def _():
                fetch(j + 1).start()

        pl.when((j == nj - 1) & (i == ni - 1))(pass_on)
        rows = pl.ds(pl.multiple_of(i * tm, tm), tm)
        o_ref[...] = _dot(a_ref[rows, :], tile[j % 2]).astype(BF16)
        pl.when((j == nj - 1) & (i == ni - 1))(finish)

    steps = nj * ni
    out = pl.pallas_call(
        body, name=name,
        grid_spec=pltpu.PrefetchScalarGridSpec(
            num_scalar_prefetch=1, grid=(nj, ni),
            in_specs=[pl.BlockSpec((s, D), lambda j, i, o: (0, 0))] + [ANY] * (has_z + n)
            + [pl.BlockSpec((a.shape[0] // steps, a.shape[1]), lambda j, i, o: (j * ni + i, 0)) for a in casting],
            out_specs=[pl.BlockSpec((tm, tn), lambda j, i, o: (i, o[first + j // per] * per + j % per))] + [ANY] * n
            + [pl.BlockSpec((None, a.shape[0] // steps, a.shape[1]), lambda j, i, o: (o[0], j * ni + i, 0))
               for a in casting],
            scratch_shapes=[pltpu.VMEM((2, D, tn), BF16), pltpu.SemaphoreType.DMA((2,))] + _gather_sems(n)
            + (_relay_sems() if relay else [])),
        out_shape=[jax.ShapeDtypeStruct((s, IN_COLS), BF16)] + [jax.ShapeDtypeStruct(a.shape, a.dtype) for a in arrays]
        + [jax.ShapeDtypeStruct((N_CHIPS,) + a.shape, BF16) for a in casting],
        input_output_aliases={**({2: 0} if has_z else {}), **{2 + has_z + w: 1 + w for w in range(n)}},
        compiler_params=_params("arbitrary", "arbitrary", communicates=True),
    )(order, h, *([z] if has_z else []), *arrays, *casting)
    return out[0], out[1:1 + n], out[1 + n:]


def _tril_ws(ws_ref, g):
    r = lax.broadcasted_iota(jnp.int32, (CHUNK, CHUNK), 0)
    c = lax.broadcasted_iota(jnp.int32, (CHUNK, CHUNK), 1)
    return jnp.where(c <= r, ws_ref[g], 0.0).astype(BF16)


def _layer_norm(v):
    mu = jnp.mean(v, axis=-1, keepdims=True)
    d = v - mu
    rstd = lax.rsqrt(jnp.mean(d * d, axis=-1, keepdims=True) + EPS)
    return d * rstd, rstd


def gating_fwd(z, ln_g, ln_b, w_s, bs_t, gathering):
    s = z.shape[0]
    n = len(gathering)
    steps = s // CHUNK

    def body(*refs):
        u_ref, v_ref, lg_ref, lb_ref, ws_ref, bst_ref = refs[:6]
        ya_ref = refs[6 + n]
        ci = pl.program_id(0)
        if n:
            send, pass_on, finish = _gather_phases(refs[7 + n:7 + 2 * n], *refs[7 + 2 * n:], _spans(gathering))
            pl.when(ci == 0)(send)
            pl.when(ci == steps - 1)(pass_on)
        ug = _gelu(u_ref[...].astype(F32))
        vhat, _ = _layer_norm(_gelu(v_ref[...].astype(F32)))
        vn = (vhat * lg_ref[...] + lb_ref[...]).astype(BF16)
        for g in range(GROUPS):
            cols = slice(g * CHUNK, (g + 1) * CHUNK)
            mixed = _dot(_tril_ws(ws_ref, g), vn[:, cols]) + bst_ref[:, g:g + 1]
            ya_ref[:, cols] = (ug[:, cols] * mixed).astype(BF16)
        if n:
            pl.when(ci == steps - 1)(finish)

    out = pl.pallas_call(
        body, name="gating_fwd", grid=(steps,),
        in_specs=[_rows(CHUNK, D, 0), _rows(CHUNK, D, 1), _full((1, D)), _full((1, D)),
                  _full((GROUPS, CHUNK, CHUNK)), _full((CHUNK, GROUPS))] + [ANY] * n,
        out_specs=[_rows(CHUNK, D)] + [ANY] * n,
        out_shape=[jax.ShapeDtypeStruct((s, D), BF16)]
        + [jax.ShapeDtypeStruct(a.shape, a.dtype) for a in _arrays(gathering)],
        input_output_aliases={6 + w: 1 + w for w in range(n)},
        scratch_shapes=_gather_sems(n) if n else [],
        compiler_params=_params("arbitrary", communicates=bool(n)),
    )(z, z, ln_g, ln_b, w_s, bs_t, *_arrays(gathering))
    return out[0], out[1:]


def _attn_tables(s):
    nd = s // ATT_T
    r = np.arange(ATT_T)[None, :, None]
    c = np.arange(ATT_T)[None, None, :]
    delta = np.arange(nd)[:, None, None] * ATT_T + r - c
    count = np.zeros(delta.shape, np.int64)
    for window, dilation in ((128, 1), (512, 4), (2048, 16)):
        count += (delta >= 0) & (delta % dilation == 0) & (delta <= window)
    logc = np.where(count > 0, np.log(np.maximum(count, 1)), MASKED)
    return jnp.asarray(logc, F32)


AUG = 3


def _split3_np(x):
    terms, rest = [], np.asarray(x, np.float64)
    for _ in range(AUG):
        term = np.asarray(rest.astype(jnp.bfloat16), np.float64)
        terms.append(term)
        rest = rest - term
    return terms


def _split3(x):
    terms, rest = [], x
    for _ in range(AUG):
        term = rest.astype(BF16).astype(F32)
        terms.append(term)
        rest = rest - term
    return terms


def _alibi_tables(s):
    nb = s // ATT_T
    slopes = np.exp2(-8.0 * np.arange(1, HEADS + 1, dtype=np.float64) / HEADS)
    ka = np.zeros((HEADS // 2, 2, ATT_T, 128), np.float32)
    kb = np.zeros((HEADS // 2, 2, nb, 128), np.float32)
    for p in range(HEADS // 2):
        for e in range(2):
            base = HEAD_DIM * (1 - e)
            for a, term in enumerate(_split3_np(slopes[2 * p + e] * np.arange(ATT_T))):
                ka[p, e, :, base + a] = term
            for a, term in enumerate(_split3_np(slopes[2 * p + e] * ATT_T * np.arange(nb))):
                kb[p, e, :, base + AUG + a] = term
            ka[p, e, :, base + 2 * AUG:base + 3 * AUG] = 1.0
    return jnp.asarray(ka), jnp.asarray(kb)


def _head_masks():
    lane = lax.broadcasted_iota(jnp.int32, (1, 128), 1)
    first = lane < HEAD_DIM

    def ones(e, n):
        base = HEAD_DIM * (1 - e)
        return ((lane >= base) & (lane < base + n)).astype(F32)

    return first, lane, ones


def _place3(lane, at, terms, other):
    for a, term in enumerate(terms):
        other = jnp.where(lane == at + a, term, other)
    return other


def attn_fwd(z, logc, ka, kb, gathering):
    s = z.shape[0]
    nq = s // ATT_T
    t = ATT_T
    n = len(gathering)
    grp = ATT_GROUP
    ngrp = HEADS // 2 // grp
    wide = 128 * grp
    qcol, kcol, vcol = 2 * D // wide, 3 * D // wide, 4 * D // wide

    def body(*refs):
        q_ref, k_ref, v_ref, lc_ref, ka_ref, kb_ref = refs[:6]
        y_ref, lse_ref = refs[6 + n:8 + n]
        q_s, k_s, v_s, m_s, l_s, acc_s = refs[8 + 2 * n:14 + 2 * n]
        gi, qi = pl.program_id(0), pl.program_id(1)
        first, lane, ones = _head_masks()
        if n:
            send, pass_on, finish = _gather_phases(refs[8 + n:8 + 2 * n], *refs[14 + 2 * n:], _spans(gathering))
            pl.when((gi == 0) & (qi == 0))(send)
            pl.when((gi == ngrp - 1) & (qi == nq - 1))(pass_on)

        @pl.when(qi == 0)
        def _():
            sel = jnp.broadcast_to(first.astype(F32), (t, 128))
            for pr in range(grp):
                cols = slice(pr * 128, (pr + 1) * 128)
                for jb in range(nq):
                    kj = k_ref[jb * t:(jb + 1) * t, cols].astype(F32)
                    vj = v_ref[jb * t:(jb + 1) * t, cols].astype(F32)
                    k_s[pr, 0, jb] = jnp.where(first, kj, ka_ref[pr, 0] + kb_ref[pr, 0, jb:jb + 1, :]).astype(BF16)
                    k_s[pr, 1, jb] = jnp.where(first, ka_ref[pr, 1] + kb_ref[pr, 1, jb:jb + 1, :], kj).astype(BF16)
                    v_s[pr, jb, 0:t, 0:128] = jnp.where(first, vj, 0.0).astype(BF16)
                    v_s[pr, jb, t:2 * t, 0:128] = jnp.where(first, 0.0, vj).astype(BF16)
                    v_s[pr, jb, 0:t, 128:256] = sel.astype(BF16)
                    v_s[pr, jb, t:2 * t, 128:256] = (1.0 - sel).astype(BF16)

        for pr in range(grp):
            q = q_ref[:, pr * 128:(pr + 1) * 128].astype(F32) * (1.0 / math.sqrt(HEAD_DIM))
            q_s[pr, 0] = jnp.where(first, q, ones(0, 2 * AUG)).astype(BF16)
            q_s[pr, 1] = jnp.where(first, ones(1, 2 * AUG), q).astype(BF16)
        m_s[...] = jnp.full_like(m_s, MASKED)
        l_s[...] = jnp.zeros_like(l_s)
        acc_s[...] = jnp.zeros_like(acc_s)

        def scores(j):
            return tuple(_dot(q_s[pr, e], k_s[pr, e, j], NT) for pr in range(grp) for e in range(2))

        def step(j, carry):
            softmax_block(j, scores(j))
            return carry

        def softmax_block(j, u):
            lc = lc_ref[qi - j]
            for pr in range(grp):
                u0 = u[2 * pr] + lc
                u1 = u[2 * pr + 1] + lc
                m0, m1 = m_s[pr, 0], m_s[pr, 1]
                n0 = jnp.maximum(m0, jnp.max(u0, axis=-1, keepdims=True))
                n1 = jnp.maximum(m1, jnp.max(u1, axis=-1, keepdims=True))
                m_s[pr, 0], m_s[pr, 1] = n0, n1
                p = jnp.concatenate([jnp.exp(u0 - jnp.concatenate([n0, n0], axis=1)).astype(BF16),
                                     jnp.exp(u1 - jnp.concatenate([n1, n1], axis=1)).astype(BF16)], axis=1)
                pv = _dot(p, v_s[pr, j])
                alpha = jnp.where(first, jnp.exp(m0 - n0), jnp.exp(m1 - n1))
                acc_s[pr] = acc_s[pr] * alpha + pv[:, 0:128]
                l_s[pr] = l_s[pr] * alpha + pv[:, 128:256]

        lax.fori_loop(0, qi + 1, step, 0)
        for pr in range(grp):
            cols = slice(pr * 128, (pr + 1) * 128)
            y_ref[:, cols] = (acc_s[pr] / l_s[pr]).astype(BF16)
            lse_ref[:, cols] = jnp.where(first, m_s[pr, 0], m_s[pr, 1]) + jnp.log(l_s[pr])
        if n:
            pl.when((gi == ngrp - 1) & (qi == nq - 1))(finish)

    out = pl.pallas_call(
        body, name="attn_fwd", grid=(ngrp, nq),
        in_specs=[pl.BlockSpec((t, wide), lambda g, i: (i, qcol + g)),
                  pl.BlockSpec((s, wide), lambda g, i: (0, kcol + g)),
                  pl.BlockSpec((s, wide), lambda g, i: (0, vcol + g)),
                  _full((nq, t, t)),
                  pl.BlockSpec((grp, 2, t, 128), lambda g, i: (g, 0, 0, 0)),
                  pl.BlockSpec((grp, 2, nq, 128), lambda g, i: (g, 0, 0, 0))] + [ANY] * n,
        out_specs=[pl.BlockSpec((t, wide), lambda g, i: (i, g)), pl.BlockSpec((t, wide), lambda g, i: (i, g))]
        + [ANY] * n,
        out_shape=[jax.ShapeDtypeStruct((s, D), BF16), jax.ShapeDtypeStruct((s, D), F32)]
        + [jax.ShapeDtypeStruct(a.shape, a.dtype) for a in _arrays(gathering)],
        input_output_aliases={6 + w: 2 + w for w in range(n)},
        scratch_shapes=[pltpu.VMEM((grp, 2, t, 128), BF16), pltpu.VMEM((grp, 2, nq, t, 128), BF16),
                        pltpu.VMEM((grp, nq, 2 * t, 256), BF16), pltpu.VMEM((grp, 2, t, 128), F32),
                        pltpu.VMEM((grp, t, 128), F32), pltpu.VMEM((grp, t, 128), F32)]
        + (_gather_sems(n) if n else []),
        compiler_params=_params("arbitrary", "arbitrary", communicates=bool(n)),
    )(z, z, z, logc, ka, kb, *_arrays(gathering))
    return out[0], out[1], out[2:]


def proj_merge(ya, yb, wa, wb, z, bg, gathering):
    s = ya.shape[0]
    tm = 512
    n = len(gathering)
    steps = s // tm

    def body(*refs):
        ya_ref, yb_ref, wa_ref, wb_ref, ga_ref, gb_ref, bg_ref = refs[:7]
        mg_ref, pa_ref, pb_ref = refs[7 + n:10 + n]
        i = pl.program_id(0)
        if n:
            send, pass_on, finish = _gather_phases(refs[10 + n:10 + 2 * n], *refs[10 + 2 * n:], _spans(gathering))
            pl.when(i == 0)(send)
            pl.when(i == steps - 1)(pass_on)
        pa = _dot(ya_ref[...], wa_ref[...])
        pb = _dot(yb_ref[...], wb_ref[...])
        sa = _sigmoid(ga_ref[...] + bg_ref[0:1, :])
        sb = _sigmoid(gb_ref[...] + bg_ref[1:2, :])
        mg_ref[...] = (sa * pa + sb * pb).astype(BF16)
        pa_ref[...] = pa.astype(BF16)
        pb_ref[...] = pb.astype(BF16)
        if n:
            pl.when(i == steps - 1)(finish)

    out = jax.ShapeDtypeStruct((s, D), BF16)
    res = pl.pallas_call(
        body, name="proj_merge", grid=(steps,),
        in_specs=[_rows(tm, D), _rows(tm, D), _full((D, D)), _full((D, D)),
                  _rows(tm, D, 5), _rows(tm, D, 6), _full((2, D))] + [ANY] * n,
        out_specs=[_rows(tm, D)] * 3 + [ANY] * n,
        out_shape=[out] * 3 + [jax.ShapeDtypeStruct(a.shape, a.dtype) for a in _arrays(gathering)],
        input_output_aliases={7 + w: 3 + w for w in range(n)},
        scratch_shapes=_gather_sems(n) if n else [],
        compiler_params=_params("arbitrary", communicates=bool(n)),
    )(ya, yb, wa, wb, z, z, bg, *_arrays(gathering))
    return res[0], res[1], res[2], res[3:]


def out_norm(merged, w_out, x, g_post, g_fpre, gathering):
    s = x.shape[0]
    tm = 512
    n = len(gathering)
    steps = s // tm

    def body(*refs):
        mg_ref, w_ref, x_ref, gp_ref, gf_ref = refs[:5]
        o_ref, x1_ref, h2_ref = refs[5 + n:8 + n]
        i = pl.program_id(0)
        if n:
            send, pass_on, finish = _gather_phases(refs[8 + n:8 + 2 * n], *refs[8 + 2 * n:], _spans(gathering))
            pl.when(i == 0)(send)
            pl.when(i == steps - 1)(pass_on)
        o = _dot(mg_ref[...], w_ref[...])
        ohat, _ = _rms(o)
        x1 = x_ref[...] + ohat * gp_ref[...]
        x1hat, _ = _rms(x1)
        o_ref[...] = o
        x1_ref[...] = x1
        h2_ref[...] = (x1hat * gf_ref[...]).astype(BF16)
        if n:
            pl.when(i == steps - 1)(finish)

    res = pl.pallas_call(
        body, name="out_norm", grid=(steps,),
        in_specs=[_rows(tm, D), _full((D, D)), _rows(tm, D), _full((1, D)), _full((1, D))] + [ANY] * n,
        out_specs=[_rows(tm, D)] * 3 + [ANY] * n,
        out_shape=[jax.ShapeDtypeStruct((s, D), F32), jax.ShapeDtypeStruct((s, D), F32),
                   jax.ShapeDtypeStruct((s, D), BF16)]
        + [jax.ShapeDtypeStruct(a.shape, a.dtype) for a in _arrays(gathering)],
        input_output_aliases={5 + w: 3 + w for w in range(n)},
        scratch_shapes=_gather_sems(n) if n else [],
        compiler_params=_params("arbitrary", communicates=bool(n)),
    )(merged, w_out, x, g_post, g_fpre, *_arrays(gathering))
    return res[0], res[1], res[2], res[3:]


def mm_ff1(h2, wg, gathering):
    s = h2.shape[0]
    tm = 1024
    n = len(gathering)
    ni = s // tm

    def body(*refs):
        a_ref, b_ref = refs[:2]
        o_ref, r_ref = refs[2 + n:4 + n]
        i, j = pl.program_id(0), pl.program_id(1)
        if n:
            send, pass_on, finish = _gather_phases(refs[4 + n:4 + 2 * n], *refs[4 + 2 * n:], _spans(gathering))
            pl.when((i == 0) & (j == 0))(send)
            pl.when((i == ni - 1) & (j == N_CHIPS // 2))(pass_on)
        a = _dot(a_ref[...], b_ref[...])
        o_ref[...] = a.astype(BF16)
        r = jnp.maximum(a, 0.0)
        r_ref[...] = (r * r).astype(BF16)
        if n:
            pl.when((i == ni - 1) & (j == N_CHIPS - 1))(finish)

    res = pl.pallas_call(
        body, name="mm_ff1", grid=(ni, N_CHIPS),
        in_specs=[pl.BlockSpec((tm, D), lambda i, j: (i, 0)), pl.BlockSpec((None, D, D), lambda i, j: (j, 0, 0))]
        + [ANY] * n,
        out_specs=[pl.BlockSpec((tm, D), lambda i, j: (i, j))] * 2 + [ANY] * n,
        out_shape=[jax.ShapeDtypeStruct((s, D_FF), BF16), jax.ShapeDtypeStruct((s, D_FF), BF16)]
        + [jax.ShapeDtypeStruct(a.shape, a.dtype) for a in _arrays(gathering)],
        input_output_aliases={2 + w: 2 + w for w in range(n)},
        scratch_shapes=_gather_sems(n) if n else [],
        compiler_params=_params("arbitrary", "arbitrary", communicates=bool(n)),
    )(h2, wg, *_arrays(gathering))
    return res[0], res[1], res[2:]


def ff2_loss(rl, w_ff2, x1, target, g_fpost):
    s = x1.shape[0]
    tm = 256

    def body(rl_ref, w_ref, x1_ref, t_ref, g_ref, dy_ref, df_ref, dg_ref, loss_ref):
        @pl.when(pl.program_id(0) == 0)
        def _():
            dg_ref[...] = jnp.zeros_like(dg_ref)
            loss_ref[...] = jnp.zeros_like(loss_ref)

        f = _dot(rl_ref[...], w_ref[...])
        fhat, r = _rms(f)
        err = x1_ref[...] + fhat * g_ref[...] - t_ref[...]
        loss_ref[...] += 0.5 * jnp.sum(jnp.mean(err * err, axis=-1, keepdims=True), axis=0, keepdims=True)
        dy = err * (1.0 / D)
        dy_ref[...] = dy
        dg_ref[...] += jnp.sum(dy * fhat, axis=0, keepdims=True)
        df_ref[...] = _rms_bwd(dy * g_ref[...], fhat, r).astype(BF16)

    return pl.pallas_call(
        body, name="ff2_loss", grid=(s // tm,),
        in_specs=[_rows(tm, D_FF), _full((D_FF, D)), _rows(tm, D), _rows(tm, D), _full((1, D))],
        out_specs=[_rows(tm, D), _rows(tm, D), _full((1, D)), _full((1, 1))],
        out_shape=[jax.ShapeDtypeStruct((s, D), F32), jax.ShapeDtypeStruct((s, D), BF16),
                   jax.ShapeDtypeStruct((1, D), F32), jax.ShapeDtypeStruct((1, 1), F32)],
        compiler_params=_params("arbitrary"),
    )(rl, w_ff2, x1, target, g_fpost)


def mm_tn(name, a, b, ta, tb, out_shape, out_spec):
    s = a.shape[0]

    def body(a_ref, b_ref, o_ref):
        o_ref[...] = _dot(a_ref[...], b_ref[...], TN)

    return pl.pallas_call(
        body, name=name, grid=(a.shape[1] // ta, b.shape[1] // tb),
        in_specs=[pl.BlockSpec((s, ta), lambda i, j: (0, i)), pl.BlockSpec((s, tb), lambda i, j: (0, j))],
        out_specs=out_spec, out_shape=jax.ShapeDtypeStruct(out_shape, F32),
        compiler_params=_params("parallel", "parallel"),
    )(a, b)


def mm_nt(name, a, w):
    s = a.shape[0]
    tm = 512

    def body(a_ref, w_ref, o_ref):
        o_ref[...] = _dot(a_ref[...], w_ref[...], NT).astype(BF16)

    return pl.pallas_call(
        body, name=name, grid=(s // tm,), in_specs=[_rows(tm, D), _full((D, D))], out_specs=_rows(tm, D),
        out_shape=jax.ShapeDtypeStruct((s, D), BF16), compiler_params=_params("parallel"),
    )(a, w)


def ff2_bwd(df, w_ff2, a):
    s = df.shape[0]
    tm = 1024

    def body(df_ref, w_ref, a_ref, da_ref):
        drl = _dot(df_ref[...], w_ref[...], NT)
        da_ref[...] = (drl * (2.0 * jnp.maximum(a_ref[...].astype(F32), 0.0))).astype(BF16)

    return pl.pallas_call(
        body, name="ff2_bwd", grid=(s // tm, D_FF // D),
        in_specs=[pl.BlockSpec((tm, D), lambda i, j: (i, 0)), pl.BlockSpec((D, D), lambda i, j: (j, 0)),
                  pl.BlockSpec((tm, D), lambda i, j: (i, j))],
        out_specs=pl.BlockSpec((tm, D), lambda i, j: (i, j)),
        out_shape=jax.ShapeDtypeStruct((s, D_FF), BF16), compiler_params=_params("parallel", "parallel"),
    )(df, w_ff2, a)


def ff1_bwd_norms(da, wg, x1, o, dy, g_fpre, g_post, swapping):
    s = x1.shape[0]
    tm = 512
    n = len(swapping)

    def body(*refs):
        da_ref, w_ref, x1_ref, o_ref, dy_ref, gf_ref, gp_ref = refs[:7]
        dx1_ref, do_ref, dgf_ref, dgp_ref = refs[7 + n:11 + n]
        acc_ref = refs[11 + 2 * n]
        i, k = pl.program_id(0), pl.program_id(1)
        if n:
            send, finish = _swap_phases(refs[7:7 + n], refs[11 + n:11 + 2 * n], *refs[12 + 2 * n:])
            pl.when((i == 0) & (k == 0))(send)

        @pl.when((i == 0) & (k == 0))
        def _():
            dgf_ref[...] = jnp.zeros_like(dgf_ref)
            dgp_ref[...] = jnp.zeros_like(dgp_ref)

        part = _dot(da_ref[...], w_ref[...], NT)

        @pl.when(k == 0)
        def _():
            acc_ref[...] = part

        @pl.when(k > 0)
        def _():
            acc_ref[...] += part

        @pl.when(k == N_CHIPS - 1)
        def _():
            dh2 = acc_ref[...]
            x1hat, r2 = _rms(x1_ref[...])
            dgf_ref[...] += jnp.sum(dh2 * x1hat, axis=0, keepdims=True)
            dx1 = dy_ref[...] + _rms_bwd(dh2 * gf_ref[...], x1hat, r2)
            ohat, r1 = _rms(o_ref[...])
            dgp_ref[...] += jnp.sum(dx1 * ohat, axis=0, keepdims=True)
            dx1_ref[...] = dx1
            do_ref[...] = _rms_bwd(dx1 * gp_ref[...], ohat, r1).astype(BF16)

        if n:
            pl.when((i == s // tm - 1) & (k == N_CHIPS - 1))(finish)

    row = pl.BlockSpec((tm, D), lambda i, k: (i, 0))
    vec = pl.BlockSpec((1, D), lambda i, k: (0, 0))
    res = pl.pallas_call(
        body, name="ff1_bwd_norms", grid=(s // tm, N_CHIPS),
        in_specs=[pl.BlockSpec((tm, D), lambda i, k: (i, k)), pl.BlockSpec((None, D, D), lambda i, k: (k, 0, 0)),
                  row, row, row, vec, vec] + [ANY] * n,
        out_specs=[row, row, vec, vec] + [ANY] * n,
        out_shape=[jax.ShapeDtypeStruct((s, D), F32), jax.ShapeDtypeStruct((s, D), BF16),
                   jax.ShapeDtypeStruct((1, D), F32), jax.ShapeDtypeStruct((1, D), F32)] + _swap_shapes(swapping),
        scratch_shapes=[pltpu.VMEM((tm, D), F32)] + (_swap_sems(n) if n else []),
        compiler_params=_params("arbitrary", "arbitrary", communicates=bool(n)),
    )(da, wg, x1, o, dy, g_fpre, g_post, *swapping)
    return res[0], res[1], res[2], res[3], res[4:]


def out_bwd_gates(do, w_out, pa, pb, z, bg):
    s = do.shape[0]
    tm = 512

    def body(do_ref, w_ref, pa_ref, pb_ref, ga_ref, gb_ref, bg_ref, dpa_ref, dpb_ref, dga_ref, dgb_ref, dbg_ref):
        @pl.when(pl.program_id(0) == 0)
        def _():
            dbg_ref[...] = jnp.zeros_like(dbg_ref)

        dm = _dot(do_ref[...], w_ref[...], NT)
        sa = _sigmoid(ga_ref[...] + bg_ref[0:1, :])
        sb = _sigmoid(gb_ref[...] + bg_ref[1:2, :])
        dpa_ref[...] = (dm * sa).astype(BF16)
        dpb_ref[...] = (dm * sb).astype(BF16)
        dga = dm * pa_ref[...].astype(F32) * (sa * (1.0 - sa))
        dgb = dm * pb_ref[...].astype(F32) * (sb * (1.0 - sb))
        dga_ref[...] = dga.astype(BF16)
        dgb_ref[...] = dgb.astype(BF16)
        dbg_ref[0:1, :] += jnp.sum(dga, axis=0, keepdims=True)
        dbg_ref[1:2, :] += jnp.sum(dgb, axis=0, keepdims=True)

    out = jax.ShapeDtypeStruct((s, D), BF16)
    return pl.pallas_call(
        body, name="out_bwd_gates", grid=(s // tm,),
        in_specs=[_rows(tm, D), _full((D, D)), _rows(tm, D), _rows(tm, D), _rows(tm, D, 5), _rows(tm, D, 6),
                  _full((2, D))],
        out_specs=[_rows(tm, D)] * 4 + [_full((2, D))],
        out_shape=[out] * 4 + [jax.ShapeDtypeStruct((2, D), F32)], compiler_params=_params("arbitrary"),
    )(do, w_out, pa, pb, z, z, bg)


def gating_bwd(z, dya, ln_g, ln_b, w_s, bs_t, swapping):
    s = z.shape[0]
    ones = functools.partial(jnp.ones, (8, CHUNK), BF16)
    n = len(swapping)

    def body(*refs):
        u_ref, v_ref, dya_ref, lg_ref, lb_ref, ws_ref, bst_ref = refs[:7]
        du_ref, dv_ref, dws_ref, dbs_ref, dlg_ref, dlb_ref = refs[7 + n:13 + n]
        dvn_ref = refs[13 + 2 * n]
        ci = pl.program_id(0)
        if n:
            send, finish = _swap_phases(refs[7:7 + n], refs[13 + n:13 + 2 * n], *refs[14 + 2 * n:])
            pl.when(ci == 0)(send)

        @pl.when(ci == 0)
        def _():
            dws_ref[...] = jnp.zeros_like(dws_ref)
            dbs_ref[...] = jnp.zeros_like(dbs_ref)
            dlg_ref[...] = jnp.zeros_like(dlg_ref)
            dlb_ref[...] = jnp.zeros_like(dlb_ref)

        ug, dug_du = _gelu_and_grad(u_ref[...].astype(F32))
        vg, dvg_dv = _gelu_and_grad(v_ref[...].astype(F32))
        vhat, rstd = _layer_norm(vg)
        vn = (vhat * lg_ref[...] + lb_ref[...]).astype(BF16)
        dya = dya_ref[...].astype(F32)
        for g in range(GROUPS):
            cols = slice(g * CHUNK, (g + 1) * CHUNK)
            ws = _tril_ws(ws_ref, g)
            mixed = _dot(ws, vn[:, cols]) + bst_ref[:, g:g + 1]
            du_ref[:, cols] = (dya[:, cols] * mixed * dug_du[:, cols]).astype(BF16)
            dmix = (dya[:, cols] * ug[:, cols]).astype(BF16)
            dbs_ref[g] += _dot(ones(), dmix, NT)
            dws_ref[g] += _dot(dmix, vn[:, cols], NT)
            dvn_ref[:, cols] = _dot(ws, dmix, TN)
        dvn = dvn_ref[...]
        dlg_ref[...] += jnp.sum(dvn * vhat, axis=0, keepdims=True)
        dlb_ref[...] += jnp.sum(dvn, axis=0, keepdims=True)
        dvh = dvn * lg_ref[...]
        dvg = rstd * (dvh - jnp.mean(dvh, axis=-1, keepdims=True)
                      - vhat * jnp.mean(dvh * vhat, axis=-1, keepdims=True))
        dv_ref[...] = (dvg * dvg_dv).astype(BF16)

        @pl.when(ci == pl.num_programs(0) - 1)
        def _():
            r = lax.broadcasted_iota(jnp.int32, (CHUNK, CHUNK), 0)
            c = lax.broadcasted_iota(jnp.int32, (CHUNK, CHUNK), 1)
            for g in range(GROUPS):
                dws_ref[g] = jnp.where(c <= r, dws_ref[g], 0.0)

        if n:
            pl.when(ci == pl.num_programs(0) - 1)(finish)

    out = jax.ShapeDtypeStruct((s, D), BF16)
    res = pl.pallas_call(
        body, name="gating_bwd", grid=(s // CHUNK,),
        in_specs=[_rows(CHUNK, D, 0), _rows(CHUNK, D, 1), _rows(CHUNK, D), _full((1, D)), _full((1, D)),
                  _full((GROUPS, CHUNK, CHUNK)), _full((CHUNK, GROUPS))] + [ANY] * n,
        out_specs=[_rows(CHUNK, D), _rows(CHUNK, D), _full((GROUPS, CHUNK, CHUNK)), _full((GROUPS, 8, CHUNK)),
                   _full((1, D)), _full((1, D))] + [ANY] * n,
        out_shape=[out, out, jax.ShapeDtypeStruct((GROUPS, CHUNK, CHUNK), F32),
                   jax.ShapeDtypeStruct((GROUPS, 8, CHUNK), F32),
                   jax.ShapeDtypeStruct((1, D), F32), jax.ShapeDtypeStruct((1, D), F32)] + _swap_shapes(swapping),
        scratch_shapes=[pltpu.VMEM((CHUNK, D), F32)] + (_swap_sems(n) if n else []),
        compiler_params=_params("arbitrary", communicates=bool(n)),
    )(z, z, dya, ln_g, ln_b, w_s, bs_t, *swapping)
    return (*res[:6], res[6:])


def attn_bwd(z, yb, dyb, lse, logc, ka, kb, scattering):
    s = z.shape[0]
    nq = s // ATT_T
    t = ATT_T
    grp = ATT_BWD_GROUP
    ngrp = HEADS // 2 // grp
    wide = 128 * grp
    qcol, kcol, vcol = 2 * D // wide, 3 * D // wide, 4 * D // wide
    scale = 1.0 / math.sqrt(HEAD_DIM)
    n = len(scattering)

    def body(*refs):
        q_ref, k_ref, v_ref, y_ref, dy_ref, lse_ref, lc_ref, ka_ref, kb_ref = refs[:9]
        dq_ref, dk_ref, dv_ref = refs[9 + n:12 + n]
        qa_s, qt_s, da_s, dt_s, dq_s, dkt_s, dvt_s = refs[12 + 2 * n:19 + 2 * n]
        gi, j = pl.program_id(0), pl.program_id(1)
        first, lane, ones = _head_masks()
        if n:
            send, finish = _scatter_phases(refs[9:9 + n], refs[12 + n:12 + 2 * n], *refs[19 + 2 * n:])
            pl.when((gi == 0) & (j == 0))(send)

        @pl.when(j == 0)
        def _():
            dq_s[...] = jnp.zeros_like(dq_s)
            for pr in range(grp):
                cols = slice(pr * 128, (pr + 1) * 128)
                for ib in range(nq):
                    rows = slice(ib * t, (ib + 1) * t)
                    q = q_ref[rows, cols].astype(F32) * scale
                    lse = lse_ref[rows, cols]
                    qa_s[pr, 0, ib] = jnp.where(first, q, _place3(lane, HEAD_DIM + 2 * AUG, _split3(-lse[:, 0:1]),
                                                                  ones(0, 2 * AUG))).astype(BF16)
                    qa_s[pr, 1, ib] = jnp.where(
                        first, _place3(lane, 2 * AUG, _split3(-lse[:, HEAD_DIM:HEAD_DIM + 1]), ones(1, 2 * AUG)),
                        q).astype(BF16)
                    qt_s[pr, ib, :, 0:t] = jnp.where(first, q, 0.0).T.astype(BF16)
                    qt_s[pr, ib, :, t:2 * t] = jnp.where(first, 0.0, q).T.astype(BF16)
                    do = dy_ref[rows, cols].astype(F32)
                    prod = do * y_ref[rows, cols].astype(F32)
                    dd0 = jnp.sum(jnp.where(first, prod, 0.0), axis=-1, keepdims=True)
                    dd1 = jnp.sum(jnp.where(first, 0.0, prod), axis=-1, keepdims=True)
                    da_s[pr, 0, ib] = jnp.where(first, do, _place3(lane, HEAD_DIM, _split3(-dd0), 0.0)).astype(BF16)
                    da_s[pr, 1, ib] = jnp.where(first, _place3(lane, 0, _split3(-dd1), 0.0), do).astype(BF16)
                    dt_s[pr, ib, :, 0:t] = jnp.where(first, do, 0.0).T.astype(BF16)
                    dt_s[pr, ib, :, t:2 * t] = jnp.where(first, 0.0, do).T.astype(BF16)

        keys = []
        for pr in range(grp):
            kj = k_ref[:, pr * 128:(pr + 1) * 128].astype(F32)
            vj = v_ref[:, pr * 128:(pr + 1) * 128].astype(F32)
            keys.append((
                jnp.where(first, kj, ka_ref[pr, 0] + kb_ref[pr, 0, pl.ds(j, 1), :]).astype(BF16),
                jnp.where(first, ka_ref[pr, 1] + kb_ref[pr, 1, pl.ds(j, 1), :], kj).astype(BF16),
                jnp.concatenate([jnp.where(first, kj, 0.0), jnp.where(first, 0.0, kj)], axis=0).astype(BF16),
                jnp.where(first, vj, ones(0, AUG)).astype(BF16),
                jnp.where(first, ones(1, AUG), vj).astype(BF16)))
        dkt_s[...] = jnp.zeros_like(dkt_s)
        dvt_s[...] = jnp.zeros_like(dvt_s)

        def step(i, _):
            lc = lc_ref[i - j]
            rows = pl.ds(pl.multiple_of(i * t, t), t)
            for pr in range(grp):
                k0a, k1a, kst, v0a, v1a = keys[pr]
                p0 = jnp.exp(_dot(qa_s[pr, 0, i], k0a, NT) + lc)
                p1 = jnp.exp(_dot(qa_s[pr, 1, i], k1a, NT) + lc)
                e0 = (p0 * _dot(da_s[pr, 0, i], v0a, NT)).astype(BF16)
                e1 = (p1 * _dot(da_s[pr, 1, i], v1a, NT)).astype(BF16)
                dq_s[pr, rows, :] += _dot(jnp.concatenate([e0, e1], axis=1), kst)
                dvt_s[pr] += _dot(dt_s[pr, i], jnp.concatenate([p0.astype(BF16), p1.astype(BF16)], axis=0))
                dkt_s[pr] += _dot(qt_s[pr, i], jnp.concatenate([e0, e1], axis=0))
            return 0

        lax.fori_loop(j, nq, step, 0)
        for pr in range(grp):
            dk_ref[:, pr * 128:(pr + 1) * 128] = dkt_s[pr].T.astype(BF16)
            dv_ref[:, pr * 128:(pr + 1) * 128] = dvt_s[pr].T.astype(BF16)

        @pl.when(j == nq - 1)
        def _():
            for pr in range(grp):
                dq_ref[:, pr * 128:(pr + 1) * 128] = (dq_s[pr] * scale).astype(BF16)

        if n:
            pl.when((gi == ngrp - 1) & (j == nq - 1))(finish)

    colblock = lambda c: pl.BlockSpec((s, wide), lambda g, j: (0, c + g))
    blk = lambda c: pl.BlockSpec((t, wide), lambda g, j: (j, c + g))
    out = jax.ShapeDtypeStruct((s, D), BF16)
    res = pl.pallas_call(
        body, name="attn_bwd", grid=(ngrp, nq),
        in_specs=[colblock(qcol), blk(kcol), blk(vcol), colblock(0), colblock(0), colblock(0),
                  _full((nq, t, t)), pl.BlockSpec((grp, 2, t, 128), lambda g, j: (g, 0, 0, 0)),
                  pl.BlockSpec((grp, 2, nq, 128), lambda g, j: (g, 0, 0, 0))] + [ANY] * n,
        out_specs=[colblock(0), blk(0), blk(0)] + [ANY] * n, out_shape=[out] * 3 + _scatter_shapes(scattering),
        scratch_shapes=[pltpu.VMEM((grp, 2, nq, t, 128), BF16), pltpu.VMEM((grp, nq, 128, 2 * t), BF16),
                        pltpu.VMEM((grp, 2, nq, t, 128), BF16), pltpu.VMEM((grp, nq, 128, 2 * t), BF16),
                        pltpu.VMEM((grp, s, 128), F32), pltpu.VMEM((grp, 128, t), F32),
                        pltpu.VMEM((grp, 128, t), F32)]
        + (_scatter_sems(n) if n else []),
        compiler_params=_params("arbitrary", "arbitrary", communicates=bool(n)),
    )(z, z, z, yb, dyb, lse, logc, ka, kb, *scattering)
    return res[0], res[1], res[2], res[3:]


def in_bwd_norm(dz, wg, x, dx1, g_pre, scattering):
    s = x.shape[0]
    tm = 512
    n = len(scattering)

    def body(*refs):
        dz_ref, w_ref, x_ref, dx1_ref, g_ref = refs[:5]
        dx_ref, dg_ref = refs[5 + n:7 + n]
        acc_ref = refs[7 + 2 * n]
        i, k = pl.program_id(0), pl.program_id(1)
        if n:
            send, finish = _scatter_phases(refs[5:5 + n], refs[7 + n:7 + 2 * n], *refs[8 + 2 * n:])
            pl.when((i == 0) & (k == 0))(send)

        @pl.when((i == 0) & (k == 0))
        def _():
            dg_ref[...] = jnp.zeros_like(dg_ref)

        part = _dot(dz_ref[...], w_ref[...], NT)

        @pl.when(k == 0)
        def _():
            acc_ref[...] = part

        @pl.when(k > 0)
        def _():
            acc_ref[...] += part

        @pl.when(k == N_CHIPS - 1)
        def _():
            dh = acc_ref[...]
            xhat, r = _rms(x_ref[...])
            dg_ref[...] += jnp.sum(dh * xhat, axis=0, keepdims=True)
            dx_ref[...] = dx1_ref[...] + _rms_bwd(dh * g_ref[...], xhat, r)

        if n:
            pl.when((i == s // tm - 1) & (k == N_CHIPS - 1))(finish)

    row = pl.BlockSpec((tm, D), lambda i, k: (i, 0))
    vec = pl.BlockSpec((1, D), lambda i, k: (0, 0))
    res = pl.pallas_call(
        body, name="in_bwd_norm", grid=(s // tm, N_CHIPS),
        in_specs=[pl.BlockSpec((tm, IN_SHARD), lambda i, k: (i, k)),
                  pl.BlockSpec((None, D, IN_SHARD), lambda i, k: (k, 0, 0)), row, row, vec] + [ANY] * n,
        out_specs=[row, vec] + [ANY] * n,
        out_shape=[jax.ShapeDtypeStruct((s, D), F32), jax.ShapeDtypeStruct((1, D), F32)]
        + _scatter_shapes(scattering),
        scratch_shapes=[pltpu.VMEM((tm, D), F32)] + (_scatter_sems(n) if n else []),
        compiler_params=_params("arbitrary", "arbitrary", communicates=bool(n)),
    )(dz, wg, x, dx1, g_pre, *scattering)
    return res[0], res[1], res[2:]


def _adamw_math(w, g, m, v):
    m = ADAM_B1 * m + (1.0 - ADAM_B1) * g
    v = ADAM_B2 * v + (1.0 - ADAM_B2) * (g * g)
    m_hat = m / (1.0 - ADAM_B1 ** ADAM_STEP)
    v_hat = v / (1.0 - ADAM_B2 ** ADAM_STEP)
    delta = -ADAM_LR * (m_hat / (jnp.sqrt(v_hat) + ADAM_EPS) + ADAM_WD * w)
    return delta, m, v


def adamw(name, w, g, m, v, tr, gathering=None):
    r, c = w.shape
    n = 0 if gathering is None else 1
    steps = r // tr

    def body(*refs):
        w_ref, g_ref, m_ref, v_ref = refs[:4]
        go_ref, d_ref, nm_ref, nv_ref = refs[4 + n:8 + n]
        i = pl.program_id(0)
        if n:
            send, pass_on, finish = _allgather8_phases(refs[8 + n], *refs[9 + n:])
            pl.when(i == 0)(send)
            pl.when(i == steps - 1)(pass_on)
        g = g_ref[...]
        go_ref[...] = g
        d_ref[...], nm_ref[...], nv_ref[...] = _adamw_math(w_ref[...], g, m_ref[...], v_ref[...])
        if n:
            pl.when(i == steps - 1)(finish)

    out = jax.ShapeDtypeStruct((r, c), F32)
    res = pl.pallas_call(
        body, name=name, grid=(steps,), in_specs=[_rows(tr, c)] * 4 + [ANY] * n,
        out_specs=[_rows(tr, c)] * 4 + [ANY] * n,
        out_shape=[out] * 4 + ([jax.ShapeDtypeStruct(gathering.shape, gathering.dtype)] if n else []),
        input_output_aliases={4: 4} if n else {},
        scratch_shapes=[pltpu.SemaphoreType.DMA((7,)), pltpu.SemaphoreType.DMA((7,))] if n else [],
        compiler_params=_params("arbitrary", communicates=bool(n)),
    )(w, g, m, v, *([gathering] if n else []))
    return res


def _allgather8_phases(buf, send_sems, recv_sems):
    x, y, c, chips = _place()
    me = 2 * x + y
    sibling = (x, y, 1 - c)
    rows = buf.shape[1] // 2

    def part(chip, core):
        return buf.at[chip, pl.ds(core * rows, rows)]

    def copy(k, block, to):
        return pltpu.make_async_remote_copy(src_ref=block, dst_ref=block, send_sem=send_sems.at[k],
                                            recv_sem=recv_sems.at[k], device_id=to, device_id_type=MESH)

    def chip_of(j):
        return 2 * chips[j][0] + chips[j][1]

    def send():
        copy(0, part(me, c), sibling).start()
        for j in range(3):
            copy(1 + j, part(me, c), (chips[j][0], chips[j][1], c)).start()

    def pass_on():
        for j in range(3):
            copy(1 + j, part(chip_of(j), c), (chips[j][0], chips[j][1], c)).wait_recv()
            copy(4 + j, part(chip_of(j), c), sibling).start()

    def finish():
        copy(0, part(me, 1 - c), sibling).wait_recv()
        for j in range(3):
            copy(4 + j, part(chip_of(j), 1 - c), sibling).wait_recv()
        copy(0, part(me, c), sibling).wait_send()
        for j in range(3):
            copy(1 + j, part(me, c), (chips[j][0], chips[j][1], c)).wait_send()
            copy(4 + j, part(chip_of(j), c), sibling).wait_send()

    return send, pass_on, finish


def add_halves(name, g, recv, c_idx, tr):
    n, h, c = recv.shape

    def body(c_ref, g_ref, r_ref, o_ref):
        o_ref[...] = (g_ref[...] + r_ref[...]).astype(BF16)

    nb = h // tr
    return pl.pallas_call(
        body, name=name,
        grid_spec=pltpu.PrefetchScalarGridSpec(
            num_scalar_prefetch=1, grid=(n, nb),
            in_specs=[pl.BlockSpec((None, tr, c), lambda k, i, c_ref: (k, c_ref[0] * nb + i, 0)),
                      pl.BlockSpec((None, tr, c), lambda k, i, c_ref: (k, i, 0))],
            out_specs=pl.BlockSpec((None, tr, c), lambda k, i, c_ref: (k, i, 0))),
        out_shape=jax.ShapeDtypeStruct((n, h, c), BF16), compiler_params=_params("parallel", "parallel"),
    )(c_idx, g, recv)


def sum_chips(name, parts, recv, where, tr):
    n, h, c = recv.shape
    nb = h // tr

    def body(w_ref, p_ref, r_ref, o_ref):
        acc = p_ref[...].astype(F32)
        for k in range(n):
            acc = acc + r_ref[k].astype(F32)
        o_ref[...] = acc

    return pl.pallas_call(
        body, name=name,
        grid_spec=pltpu.PrefetchScalarGridSpec(
            num_scalar_prefetch=1, grid=(nb,),
            in_specs=[pl.BlockSpec((None, tr, c), lambda i, w_ref: (w_ref[0], i, 0)),
                      pl.BlockSpec((n, tr, c), lambda i, w_ref: (0, i, 0))],
            out_specs=pl.BlockSpec((tr, c), lambda i, w_ref: (w_ref[1] * nb + i, 0))),
        out_shape=jax.ShapeDtypeStruct((2 * h, c), F32), compiler_params=_params("parallel"),
    )(where, parts, recv)


def place_shard(name, shard, where, dtype, tr):
    r, c = shard.shape

    def body(w_ref, s_ref, o_ref):
        o_ref[...] = s_ref[...].astype(dtype)

    return pl.pallas_call(
        body, name=name,
        grid_spec=pltpu.PrefetchScalarGridSpec(
            num_scalar_prefetch=1, grid=(r // tr,),
            in_specs=[pl.BlockSpec((tr, c), lambda i, w_ref: (i, 0))],
            out_specs=pl.BlockSpec((None, tr, c), lambda i, w_ref: (w_ref[0], i, 0))),
        out_shape=jax.ShapeDtypeStruct((N_CHIPS, r, c), dtype), compiler_params=_params("parallel"),
    )(where, shard)


ANY = pl.BlockSpec(memory_space=pl.ANY)


def _place():
    x, y, c = lax.axis_index("x"), lax.axis_index("y"), lax.axis_index("c")
    chips = [(1 - x, y), (x, 1 - y), (1 - x, 1 - y)]
    return x, y, c, chips


def gather_shards(arrays):
    n = len(arrays)

    def body(*refs):
        send, pass_on, finish = _gather_phases(refs[n:2 * n], *refs[2 * n:], _spans(arrays))
        send()
        pass_on()
        finish()

    return pl.pallas_call(
        body, name="gather_shards", in_specs=[ANY] * n, out_specs=[ANY] * n,
        out_shape=[jax.ShapeDtypeStruct(a.shape, a.dtype) for a in _arrays(arrays)],
        input_output_aliases={w: w for w in range(n)}, scratch_shapes=_gather_sems(n),
        compiler_params=pltpu.CompilerParams(has_side_effects=True),
    )(*_arrays(arrays))


def _gather_sems(n):
    return [pltpu.SemaphoreType.DMA((6 * n,)), pltpu.SemaphoreType.DMA((6 * n,))]


class Span(typing.NamedTuple):
    array: jax.Array
    lo: int
    hi: int
    ways: tuple = (0, 1, 2)


def _arrays(gathering):
    return [g.array if isinstance(g, Span) else g for g in gathering]


def _spans(gathering):
    return [(g.lo, g.hi, g.ways) if isinstance(g, Span) else (0, g.shape[1], (0, 1, 2)) for g in gathering]


def _gather_phases(out, send_sems, recv_sems, spans):
    n = len(out)
    if not any(ways for _, _, ways in spans):
        return (lambda: None,) * 3
    x, y, c, chips = _place()
    me = 2 * x + y
    sibling = (x, y, 1 - c)

    def half(w, chip, core):
        lo, hi, _ = spans[w]
        h = (hi - lo) // 2
        return out[w].at[chip, pl.ds(lo + core * h, h)]

    def copy(k, block, to):
        return pltpu.make_async_remote_copy(src_ref=block, dst_ref=block, send_sem=send_sems.at[k],
                                            recv_sem=recv_sems.at[k], device_id=to, device_id_type=MESH)

    def over_ici(w, j, chip):
        return copy(3 * w + j, half(w, chip, c), (chips[j][0], chips[j][1], c))

    def over_d2d(w, j, core):
        return copy(3 * n + 3 * w + j, half(w, 2 * chips[j][0] + chips[j][1], core), sibling)

    pairs = [(w, j) for w in range(n) for j in spans[w][2]]

    def send():
        for w, j in pairs:
            over_ici(w, j, me).start()

    def pass_on():
        for w, j in pairs:
            over_ici(w, j, 2 * chips[j][0] + chips[j][1]).wait_recv()
            over_d2d(w, j, c).start()

    def finish():
        for w, j in pairs:
            over_d2d(w, j, 1 - c).wait_recv()
        for w, j in pairs:
            over_ici(w, j, me).wait_send()
            over_d2d(w, j, c).wait_send()

    return send, pass_on, finish


def _relay_sems():
    return [pltpu.SemaphoreType.DMA((4,)), pltpu.SemaphoreType.DMA((4,))]


def _relay_phases(out, send_sems, recv_sems):
    x, y, c, chips = _place()
    sibling = (x, y, 1 - c)
    rows = out.shape[1]
    quarter = rows // 4
    far = 2 * chips[2][0] + chips[2][1]

    def piece(chip, way, core):
        return out.at[chip, pl.ds(way * (rows // 2) + core * quarter, quarter)]

    def copy(k, block, to):
        return pltpu.make_async_remote_copy(src_ref=block, dst_ref=block, send_sem=send_sems.at[k],
                                            recv_sem=recv_sems.at[k], device_id=to, device_id_type=MESH)

    def over_ici(way, chip):
        return copy(way, piece(chip, way, c), (chips[way][0], chips[way][1], c))

    def over_d2d(way, core):
        return copy(2 + way, piece(far, way, core), sibling)

    def send():
        for way in range(2):
            other = chips[1 - way]
            over_ici(way, 2 * other[0] + other[1]).start()

    def pass_on():
        for way in range(2):
            over_ici(way, far).wait_recv()
            over_d2d(way, c).start()

    def finish():
        for way in range(2):
            over_d2d(way, 1 - c).wait_recv()
        for way in range(2):
            other = chips[1 - way]
            over_ici(way, 2 * other[0] + other[1]).wait_send()
            over_d2d(way, c).wait_send()

    return send, pass_on, finish


def swap_halves(name, grads):
    n = len(grads)

    def body(*refs):
        send, finish = _swap_phases(refs[:n], refs[n:2 * n], *refs[2 * n:])
        send()
        finish()

    return pl.pallas_call(
        body, name=name, in_specs=[ANY] * n, out_specs=[ANY] * n, out_shape=_swap_shapes(grads),
        scratch_shapes=_swap_sems(n), compiler_params=pltpu.CompilerParams(has_side_effects=True),
    )(*grads)


def _swap_shapes(grads):
    return [jax.ShapeDtypeStruct((a.shape[0], a.shape[1] // 2, a.shape[2]), a.dtype) for a in grads]


def _swap_sems(n):
    return [pltpu.SemaphoreType.DMA((n,)), pltpu.SemaphoreType.DMA((n,))]


def _swap_phases(g, out, send_sems, recv_sems):
    x, y, c, _ = _place()

    def copies():
        return [pltpu.make_async_remote_copy(
            src_ref=g[w].at[:, pl.ds((1 - c) * (g[w].shape[1] // 2), g[w].shape[1] // 2)], dst_ref=out[w],
            send_sem=send_sems.at[w], recv_sem=recv_sems.at[w], device_id=(x, y, 1 - c), device_id_type=MESH)
            for w in range(len(g))]

    def send():
        for cp in copies():
            cp.start()

    def finish():
        for cp in copies():
            cp.wait()

    return send, finish


def _send_phases(g, out, send_sems, recv_sems):
    x, y, c, _ = _place()

    def copies():
        return [pltpu.make_async_remote_copy(
            src_ref=g[w], dst_ref=out[w], send_sem=send_sems.at[w], recv_sem=recv_sems.at[w],
            device_id=(x, y, 1 - c), device_id_type=MESH) for w in range(len(g))]

    def send():
        for cp in copies():
            cp.start()

    def finish():
        for cp in copies():
            cp.wait()

    return send, finish


def dw_in_half(name, h, dz, which, sending):
    s = h.shape[0]
    hh, tb = D // 2, IN_SHARD // 2
    n = len(sending)
    steps = IN_COLS // tb

    def body(w_ref, *refs):
        a_ref, b_ref, o_ref = refs[0], refs[1], refs[2 + n]
        j = pl.program_id(0)
        if n:
            send, finish = _send_phases(refs[2:2 + n], refs[3 + n:3 + 2 * n], *refs[3 + 2 * n:])
            pl.when(j == 0)(send)
        o_ref[...] = _dot(a_ref[...], b_ref[...], TN)
        if n:
            pl.when(j == steps - 1)(finish)

    out = pl.pallas_call(
        body, name=name,
        grid_spec=pltpu.PrefetchScalarGridSpec(
            num_scalar_prefetch=1, grid=(steps,),
            in_specs=[pl.BlockSpec((s, hh), lambda j, w: (0, w[0])), pl.BlockSpec((s, tb), lambda j, w: (0, j))]
            + [ANY] * n,
            out_specs=[pl.BlockSpec((None, hh, tb), lambda j, w: (j // 2, 0, j % 2))] + [ANY] * n,
            scratch_shapes=_swap_sems(n) if n else []),
        out_shape=[jax.ShapeDtypeStruct((N_CHIPS, hh, IN_SHARD), F32)]
        + [jax.ShapeDtypeStruct(a.shape, a.dtype) for a in sending],
        compiler_params=_params("arbitrary", communicates=bool(n)),
    )(which, h, dz, *sending)
    return out[0], out[1:]


def scatter_chips(parts):
    n = len(parts)

    def body(*refs):
        send, finish = _scatter_phases(refs[:n], refs[n:2 * n], *refs[2 * n:])
        send()
        finish()

    return pl.pallas_call(
        body, name="scatter_chips", in_specs=[ANY] * n, out_specs=[ANY] * n,
        out_shape=_scatter_shapes(parts), scratch_shapes=_scatter_sems(n),
        compiler_params=pltpu.CompilerParams(has_side_effects=True),
    )(*parts)


def _scatter_shapes(parts):
    return [jax.ShapeDtypeStruct((3,) + a.shape[1:], a.dtype) for a in parts]


def _scatter_sems(n):
    return [pltpu.SemaphoreType.DMA((3 * n,)), pltpu.SemaphoreType.DMA((3 * n,))]


def _scatter_phases(p, out, send_sems, recv_sems):
    x, y, c, chips = _place()

    def copies():
        return [pltpu.make_async_remote_copy(
            src_ref=p[w].at[2 * px + py], dst_ref=out[w].at[j], send_sem=send_sems.at[3 * w + j],
            recv_sem=recv_sems.at[3 * w + j], device_id=(px, py, c), device_id_type=MESH)
            for w in range(len(p)) for j, (px, py) in enumerate(chips)]

    def send():
        for cp in copies():
            cp.start()

    def finish():
        for cp in copies():
            cp.wait()

    return send, finish


def join_halves(arrays):
    n = len(arrays)

    def body(*refs):
        out = refs[n:2 * n]
        send_sems, recv_sems = refs[2 * n:]
        x, y, c, _ = _place()

        def copy(w, core):
            h = out[w].shape[0] // 2
            rows = out[w].at[pl.ds(core * h, h)]
            return pltpu.make_async_remote_copy(
                src_ref=rows, dst_ref=rows, send_sem=send_sems.at[w], recv_sem=recv_sems.at[w],
                device_id=(x, y, 1 - c), device_id_type=MESH)

        for w in range(n):
            copy(w, c).start()
        for w in range(n):
            copy(w, 1 - c).wait_recv()
        for w in range(n):
            copy(w, c).wait_send()

    return pl.pallas_call(
        body, name="join_halves", in_specs=[ANY] * n, out_specs=[ANY] * n,
        out_shape=[jax.ShapeDtypeStruct(a.shape, a.dtype) for a in arrays],
        input_output_aliases={w: w for w in range(n)},
        scratch_shapes=[pltpu.SemaphoreType.DMA((n,)), pltpu.SemaphoreType.DMA((n,))],
        compiler_params=pltpu.CompilerParams(has_side_effects=True),
    )(*arrays)


def allreduce_small(packed):
    r, c = packed.shape
    n_dev = 8

    def body(x_ref, all_ref, sum_ref, send_sems, recv_sems, local_sem):
        x, y, cc, chips = _place()
        me, sibling = (x, y, cc), (x, y, 1 - cc)

        def rows(px, py, pc):
            return all_ref.at[4 * px + 2 * py + pc]

        def copy(k, block, to, src=None):
            return pltpu.make_async_remote_copy(
                src_ref=rows(*block) if src is None else src, dst_ref=rows(*block), send_sem=send_sems.at[k],
                recv_sem=recv_sems.at[k], device_id=to, device_id_type=MESH)

        mine = pltpu.make_async_copy(x_ref, rows(*me), local_sem)
        mine.start()
        first = [copy(0, me, sibling, src=x_ref)]
        first += [copy(1 + j, me, (*chip, cc), src=x_ref) for j, chip in enumerate(chips)]
        for cp in first:
            cp.start()
        passed = [copy(4 + j, (*chip, cc), sibling) for j, chip in enumerate(chips)]
        for j, chip in enumerate(chips):
            copy(1 + j, (*chip, cc), me).wait_recv()
            passed[j].start()
        copy(0, sibling, me).wait_recv()
        for j, chip in enumerate(chips):
            copy(4 + j, (*chip, 1 - cc), me).wait_recv()
        for cp in first + passed:
            cp.wait_send()
        mine.wait()
        acc = all_ref[0]
        for k in range(1, n_dev):
            acc = acc + all_ref[k]
        sum_ref[...] = acc

    vm = pl.BlockSpec(memory_space=pltpu.VMEM)
    return pl.pallas_call(
        body, name="allreduce_small", in_specs=[vm], out_specs=[vm, vm],
        out_shape=[jax.ShapeDtypeStruct((n_dev, r, c), F32), jax.ShapeDtypeStruct((r, c), F32)],
        scratch_shapes=[pltpu.SemaphoreType.DMA((7,)), pltpu.SemaphoreType.DMA((7,)), pltpu.SemaphoreType.DMA],
        compiler_params=pltpu.CompilerParams(has_side_effects=True, vmem_limit_bytes=VMEM_LIMIT),
    )(packed)[1]


def local_step(x, target, vecs, w_s, bs_t, bg, wg_in, late, core=None, order=None):
    on_mesh = core is not None

    def add(names, grads, recv):
        return [add_halves("add_" + n, g, r, core, min(r.shape[1], 256)) for n, g, r in zip(names, grads, recv)]

    g_pre, ln_g, ln_b, g_post, g_fpre, g_fpost = vecs
    s = x.shape[0]
    if order is None:
        order = jnp.arange(N_CHIPS, dtype=jnp.int32)
    logc = _attn_tables(s)
    ka, kb = _alibi_tables(s)

    h = norm_pre(x, g_pre)
    if not on_mesh:
        wg_a, wg_b, wg_out, wg_ff1, wg_ff2 = late
    if on_mesh:
        cut = D // 4
        z, (wg_in,), (wg_a, wg_b, wg_out, wg_ff1, wg_ff2) = mm_in(
            "mm_in_own", h, wg_in, order, 0, 1, None, [Span(wg_in, 0, D, (0, 1))], casting=late)
        z, (wg_in,), _ = mm_in("mm_in_near", h, wg_in, order, 1, 2, z, [Span(wg_in, 0, D, ())], relay=True)
        z, (wg_in, wg_a), _ = mm_in("mm_in_far", h, wg_in, order, 3, 1, z, [Span(wg_in, 0, D, ()), wg_a])
        ya, (wg_b, wg_ff2) = gating_fwd(z, ln_g, ln_b, w_s, bs_t, [wg_b, Span(wg_ff2, 0, cut)])
        yb, lse, (wg_ff1, wg_ff2, bg) = attn_fwd(z, logc, ka, kb, [wg_ff1, Span(wg_ff2, cut, 3 * cut), bg])
        bg = jnp.transpose(bg[:, :2, :], (1, 0, 2)).reshape(2, D)
    else:
        z, _, _ = mm_in("mm_in", h, wg_in, order, 0, N_CHIPS, None, [Span(wg_in, 0, D, ())])
        ya, _ = gating_fwd(z, ln_g, ln_b, w_s, bs_t, [])
        yb, lse, _ = attn_fwd(z, logc, ka, kb, [])
    merged, pa, pb, got = proj_merge(ya, yb, wg_a.reshape(D, D), wg_b.reshape(D, D), z, bg, [wg_out] if on_mesh else [])
    wg_out = got[0] if on_mesh else wg_out
    w_out = wg_out.reshape(D, D)
    o, x1, h2, got = out_norm(merged, w_out, x, g_post, g_fpre, [Span(wg_ff2, 3 * D // 4, D)] if on_mesh else [])
    a, rl, _ = mm_ff1(h2, wg_ff1, [])
    w_ff2 = (got[0] if on_mesh else wg_ff2).reshape(D_FF, D)
    dy, df, d_gfpost, loss = ff2_loss(rl, w_ff2, x1, target, g_fpost)

    half_cols = pl.BlockSpec((D, D // 2), lambda i, j: (0, j))
    d_wff2 = mm_tn("dw_ff2", rl, df, D // 2, D, (D_FF, D), pl.BlockSpec((D // 2, D), lambda i, j: (i, 0)))
    da = ff2_bwd(df, w_ff2, a)
    d_wff1 = mm_tn("dw_ff1", h2, da, D, D // 2, (N_CHIPS, D, D),
                   pl.BlockSpec((None, D, D // 2), lambda i, j: (j // 2, 0, j % 2)))
    d_ff = [d_wff1, d_wff2.reshape(N_CHIPS, D, D)]
    dx1, do, d_gfpre, d_gpost, recv_ff = ff1_bwd_norms(da, wg_ff1, x1, o, dy, g_fpre, g_post, d_ff if on_mesh else [])
    d_wout = mm_tn("dw_out", merged, do, D, D // 2, (D, D), half_cols)
    dpa, dpb, dga, dgb, d_bg = out_bwd_gates(do, w_out, pa, pb, z, bg)
    d_wa = mm_tn("dw_a", ya, dpa, D, D // 2, (D, D), half_cols)
    d_wb = mm_tn("dw_b", yb, dpb, D, D // 2, (D, D), half_cols)
    dya = mm_nt("dy_a", dpa, wg_a.reshape(D, D))
    dyb = mm_nt("dy_b", dpb, wg_b.reshape(D, D))
    d_proj = [d_wa.reshape(N_CHIPS, D // N_CHIPS, D), d_wb.reshape(N_CHIPS, D // N_CHIPS, D),
              d_wout.reshape(N_CHIPS, D // N_CHIPS, D)]
    du, dv, d_ws, d_bs, d_lng, d_lnb, recv_proj = gating_bwd(z, dya, ln_g, ln_b, w_s, bs_t, d_proj if on_mesh else [])
    early = d_proj + d_ff
    parts_early = add(BIG[1:], early, list(recv_proj) + list(recv_ff)) if on_mesh else []
    dq, dk, dvb, got_early = attn_bwd(z, yb, dyb, lse, logc, ka, kb, parts_early)
    dz = jnp.concatenate([du, dv, dq, dk, dvb, dga, dgb], axis=1)
    if on_mesh:
        for_sibling, _ = dw_in_half("dw_in_sibling", h, dz, 1 - core, [])
        mine, from_sibling = dw_in_half("dw_in_mine", h, dz, core, [for_sibling])
        d_win = None
        parts_late = [add_halves("add_w_in", mine, from_sibling[0], jnp.zeros((1,), jnp.int32), 256)]
    else:
        half = IN_SHARD // 2
        d_win = mm_tn("dw_in", h, dz, D, half, (N_CHIPS, D, IN_SHARD),
                      pl.BlockSpec((None, D, half), lambda i, j: (j // 2, 0, j % 2)))
        parts_late = []
    dx, d_gpre, got_late = in_bwd_norm(dz, wg_in, x, dx1, g_pre, parts_late)

    small = dict(norm_mix_pre=d_gpre, b_gate=d_bg, ln_v_g=d_lng, ln_v_b=d_lnb, w_s=d_ws, b_s=d_bs[:, 0, :],
                 norm_mix_post=d_gpost, norm_ffn_pre=d_gfpre, norm_ffn_post=d_gfpost)
    return loss[0, 0], dx, [d_win] + early, small, parts_late + parts_early, list(got_late) + list(got_early)


BIG = ("w_in", "w_a_proj", "w_b_proj", "w_out", "w_ff1", "w_ff2")
SMALL = ("norm_mix_pre", "ln_v_g", "ln_v_b", "b_s", "norm_mix_post", "norm_ffn_pre", "norm_ffn_post", "w_s", "b_gate")
ORDER = ("norm_mix_pre", "w_in", "b_gate", "ln_v_g", "ln_v_b", "w_s", "b_s", "w_a_proj", "w_b_proj", "w_out",
         "norm_mix_post", "norm_ffn_pre", "w_ff1", "w_ff2", "norm_ffn_post")
VEC_ROWS = D // 128
WS_ROW = 7 * VEC_ROWS
BG_ROW = WS_ROW + GROUPS * CHUNK
LOSS_ROW = BG_ROW + 2 * VEC_ROWS
PACK_ROWS = LOSS_ROW + 8


def pack_small(small, loss, where):
    vectors = [small[n] for n in SMALL[:7]]
    operands = vectors + [small["w_s"], small["b_gate"], loss]

    def body(where_ref, *refs):
        out = refs[-1]
        ws_ref, bg_ref, loss_ref = refs[7:10]
        for i, n in enumerate(SMALL[:7]):
            if n == "b_s":
                out[i * VEC_ROWS:(i + 1) * VEC_ROWS, :] = refs[i][...]
            else:
                for j in range(VEC_ROWS):
                    out[i * VEC_ROWS + j:i * VEC_ROWS + j + 1, :] = refs[i][:, j * 128:(j + 1) * 128]
        for g in range(GROUPS):
            out[WS_ROW + g * CHUNK:WS_ROW + (g + 1) * CHUNK, :] = ws_ref[g]
        for r in range(2):
            for j in range(VEC_ROWS):
                row = BG_ROW + r * VEC_ROWS + j
                out[row:row + 1, :] = bg_ref[r:r + 1, j * 128:(j + 1) * 128]
        lane = lax.broadcasted_iota(jnp.int32, (8, 128), 1)
        sub = lax.broadcasted_iota(jnp.int32, (8, 128), 0)
        out[LOSS_ROW:LOSS_ROW + 8, :] = jnp.where((lane == 0) & (sub == 0), loss_ref[...], 0.0)

    return pl.pallas_call(
        body, name="pack_small",
        grid_spec=pltpu.PrefetchScalarGridSpec(
            num_scalar_prefetch=1, grid=(1,), in_specs=[_full(a.shape) for a in operands],
            out_specs=pl.BlockSpec((None, PACK_ROWS, 128), lambda i, w: (w[0], w[1], 0))),
        out_shape=jax.ShapeDtypeStruct((N_CHIPS, 2 * PACK_ROWS, 128), F32), compiler_params=_params("arbitrary"),
    )(where, *operands)


def adamw_small(gathered, chip, w, m, v):
    shapes = {n: (1, D) for n in SMALL}
    shapes.update(b_s=(GROUPS, CHUNK), w_s=(GROUPS * CHUNK, CHUNK), b_gate=(2, D // N_CHIPS))
    flat = lambda t: [t[n].reshape(shapes[n]) for n in SMALL]
    per = D // N_CHIPS // 128

    def body(chip_ref, all_ref, *refs):
        params, outs = refs[:27], refs[27:]
        sub = lax.broadcasted_iota(jnp.int32, (VEC_ROWS, 128), 0)
        sum_ref = outs[36]
        total = all_ref[0, 0:PACK_ROWS, :]
        for k in range(1, 2 * N_CHIPS):
            total = total + all_ref[k // 2, (k % 2) * PACK_ROWS:(k % 2 + 1) * PACK_ROWS, :]
        sum_ref[...] = total

        def gate_row(r):
            rows = sum_ref[BG_ROW + r * VEC_ROWS:BG_ROW + (r + 1) * VEC_ROWS, :]
            return jnp.concatenate([jnp.sum(jnp.where(sub == per * chip_ref[0] + j, rows, 0.0), axis=0, keepdims=True)
                                    for j in range(per)], axis=1)

        for i, n in enumerate(SMALL):
            if n == "b_s":
                g = sum_ref[i * VEC_ROWS:(i + 1) * VEC_ROWS, :]
            elif n == "w_s":
                g = sum_ref[WS_ROW:BG_ROW, :]
            elif n == "b_gate":
                g = jnp.concatenate([gate_row(0), gate_row(1)], axis=0)
            else:
                g = jnp.concatenate([sum_ref[i * VEC_ROWS + j:i * VEC_ROWS + j + 1, :] for j in range(VEC_ROWS)],
                                    axis=1)
            delta, nm, nv = _adamw_math(params[i][...], g, params[9 + i][...], params[18 + i][...])
            outs[4 * i][...], outs[4 * i + 1][...], outs[4 * i + 2][...], outs[4 * i + 3][...] = g, delta, nm, nv

    vm = pl.BlockSpec(memory_space=pltpu.VMEM)
    res = pl.pallas_call(
        body, name="adamw_small",
        in_specs=[pl.BlockSpec(memory_space=pltpu.SMEM)] + [vm] * 28, out_specs=[vm] * 37,
        out_shape=[jax.ShapeDtypeStruct(shapes[n], F32) for n in SMALL for _ in range(4)]
        + [jax.ShapeDtypeStruct((PACK_ROWS, 128), F32)],
        compiler_params=_params(),
    )(chip, gathered, *flat(w), *flat(m), *flat(v))
    new = {n: tuple(r.reshape(w[n].shape) for r in res[4 * i:4 * i + 4]) for i, n in enumerate(SMALL)}
    return new, res[36][LOSS_ROW, 0]


def kernel(x, norm_mix_pre, w_in, b_gate, ln_v_g, ln_v_b, w_s, b_s, w_a_proj, w_b_proj, w_out, norm_mix_post, norm_ffn_pre, w_ff1, w_ff2, norm_ffn_post, loss_target, m_norm_mix_pre, m_w_in, m_b_gate, m_ln_v_g, m_ln_v_b, m_w_s, m_b_s, m_w_a_proj, m_w_b_proj, m_w_out, m_norm_mix_post, m_norm_ffn_pre, m_w_ff1, m_w_ff2, m_norm_ffn_post, v_norm_mix_pre, v_w_in, v_b_gate, v_ln_v_g, v_ln_v_b, v_w_s, v_b_s, v_w_a_proj, v_w_b_proj, v_w_out, v_norm_mix_post, v_norm_ffn_pre, v_w_ff1, v_w_ff2, v_norm_ffn_post):
    w = dict(norm_mix_pre=norm_mix_pre, w_in=w_in, b_gate=b_gate, ln_v_g=ln_v_g, ln_v_b=ln_v_b, w_s=w_s, b_s=b_s,
             w_a_proj=w_a_proj, w_b_proj=w_b_proj, w_out=w_out, norm_mix_post=norm_mix_post,
             norm_ffn_pre=norm_ffn_pre, w_ff1=w_ff1, w_ff2=w_ff2, norm_ffn_post=norm_ffn_post)
    m = dict(norm_mix_pre=m_norm_mix_pre, w_in=m_w_in, b_gate=m_b_gate, ln_v_g=m_ln_v_g, ln_v_b=m_ln_v_b, w_s=m_w_s,
             b_s=m_b_s, w_a_proj=m_w_a_proj, w_b_proj=m_w_b_proj, w_out=m_w_out, norm_mix_post=m_norm_mix_post,
             norm_ffn_pre=m_norm_ffn_pre, w_ff1=m_w_ff1, w_ff2=m_w_ff2, norm_ffn_post=m_norm_ffn_post)
    v = dict(norm_mix_pre=v_norm_mix_pre, w_in=v_w_in, b_gate=v_b_gate, ln_v_g=v_ln_v_g, ln_v_b=v_ln_v_b, w_s=v_w_s,
             b_s=v_b_s, w_a_proj=v_w_a_proj, w_b_proj=v_w_b_proj, w_out=v_w_out, norm_mix_post=v_norm_mix_post,
             norm_ffn_pre=v_norm_ffn_pre, w_ff1=v_w_ff1, w_ff2=v_w_ff2, norm_ffn_post=v_norm_ffn_post)
    chip = 2 * lax.axis_index("x") + lax.axis_index("y")
    core = lax.axis_index("c")

    where = jnp.stack([chip, core]).astype(jnp.int32)
    wg_in = place_shard("place_w_in", w_in[0], where, BF16, 256)
    bg_all = place_shard("place_b_gate", jnp.pad(b_gate[0], ((0, 14), (0, 0))), where, F32, 16)
    vecs = (norm_mix_pre, ln_v_g, ln_v_b, norm_mix_post, norm_ffn_pre, norm_ffn_post)
    loss, dx, _, small, parts, got = local_step(
        x[0], loss_target[0], vecs, w_s[0], b_s[0].T, bg_all, wg_in, [w[n][0] for n in BIG[1:]],
        core=jnp.reshape(core, (1,)).astype(jnp.int32),
        order=jnp.stack([chip, chip ^ 2, chip ^ 1, chip ^ 3]).astype(jnp.int32))

    halves = [sum_chips("sum_" + n, p, r, where, min(p.shape[1], 256)) for n, p, r in zip(BIG, parts, got)]
    grads = dict(zip(BIG, join_halves(halves)))

    packed = pack_small(small, loss.reshape(1, 1), where)
    new = {}
    for n in BIG:
        shape = w[n].shape
        res = adamw("adamw_" + n, w[n][0], grads[n], m[n][0], v[n][0], min(shape[1], 256),
                    packed if n == "w_in" else None)
        new[n] = tuple(r.reshape(shape) for r in res[:4])
        packed = res[4] if n == "w_in" else packed
    small_new, loss = adamw_small(packed, jnp.reshape(chip, (1,)).astype(jnp.int32), w, m, v)
    new.update(small_new)

    outs = [loss, dx[None]]
    for i in range(4):
        outs += [new[n][i] for n in ORDER]
    return tuple(outs)
```

```python
import functools
import math
import typing

import numpy as np
import jax
import jax.numpy as jnp
from jax import lax
from jax.experimental import pallas as pl
from jax.experimental.pallas import tpu as pltpu

F32 = jnp.float32
BF16 = jnp.bfloat16
MESH = pl.DeviceIdType.MESH

D = 1024
EPS = 1e-6
CHUNK = 128
GROUPS = 8
HEADS = 16
HEAD_DIM = 64
ATT_T = 256
ATT_GROUP = 8
ATT_BWD_GROUP = 2
N_CHIPS = 4
D_FF = 4 * D
IN_COLS = 7 * D
IN_SHARD = IN_COLS // N_CHIPS
MASKED = -1e30
VMEM_LIMIT = 56 * 2 ** 20

ADAM_LR, ADAM_B1, ADAM_B2, ADAM_EPS, ADAM_WD, ADAM_STEP = 0.001, 0.9, 0.999, 1e-08, 0.01, 10

NN = (((1,), (0,)), ((), ()))
NT = (((1,), (1,)), ((), ()))
TN = (((0,), (0,)), ((), ()))


def _dot(a, b, dims=NN):
    return lax.dot_general(a, b, dims, preferred_element_type=F32)


def _params(*sem, communicates=False):
    return pltpu.CompilerParams(dimension_semantics=sem or None, vmem_limit_bytes=VMEM_LIMIT,
                                has_side_effects=communicates)


def _rows(tr, c, col=0):
    return pl.BlockSpec((tr, c), lambda i: (i, col))


def _full(shape):
    n = len(shape)
    return pl.BlockSpec(shape, lambda *_: (0,) * n)


def _gelu(x):
    k = math.sqrt(2.0 / math.pi)
    return 0.5 * x * (1.0 + jnp.tanh(k * (x + 0.044715 * x * x * x)))


def _gelu_and_grad(x):
    k = math.sqrt(2.0 / math.pi)
    t = jnp.tanh(k * (x + 0.044715 * x * x * x))
    g = 0.5 * x * (1.0 + t)
    dg = 0.5 * (1.0 + t) + 0.5 * x * (1.0 - t * t) * (k * (1.0 + 3.0 * 0.044715 * x * x))
    return g, dg


def _sigmoid(x):
    return 1.0 / (1.0 + jnp.exp(-x))


def _rms(x):
    r = lax.rsqrt(jnp.mean(x * x, axis=-1, keepdims=True) + EPS)
    return x * r, r


def _rms_bwd(dn, xhat, r):
    return r * (dn - xhat * jnp.mean(dn * xhat, axis=-1, keepdims=True))


def norm_pre(x, g):
    s = x.shape[0]
    tr = 512

    def body(x_ref, g_ref, h_ref):
        xhat, _ = _rms(x_ref[...])
        h_ref[...] = (xhat * g_ref[...]).astype(BF16)

    return pl.pallas_call(
        body, name="norm_pre", grid=(s // tr,),
        in_specs=[_rows(tr, D), _full((1, D))], out_specs=_rows(tr, D),
        out_shape=jax.ShapeDtypeStruct((s, D), BF16), compiler_params=_params("parallel"),
    )(x, g)


def mm_in(name, h, wg, order, first, count, z, gathering, relay=False, casting=()):
    s = h.shape[0]
    tm, tn = 1024, IN_SHARD // 2
    per = IN_SHARD // tn
    n, m = len(gathering), len(casting)
    nj, ni = count * per, s // tm
    has_z = z is not None
    arrays = _arrays(gathering)
    at = [k for k, a in enumerate(arrays) if a is wg][0]

    def body(order_ref, *refs):
        a_ref = refs[0]
        cast_in = refs[1 + has_z + n:1 + has_z + n + m]
        o_ref = refs[1 + has_z + n + m]
        held = refs[2 + has_z + n + m:2 + has_z + 2 * n + m]
        cast_out = refs[2 + has_z + 2 * n + m:2 + has_z + 2 * n + 2 * m]
        tile, tile_sem = refs[2 + has_z + 2 * n + 2 * m:4 + has_z + 2 * n + 2 * m]
        j, i = pl.program_id(0), pl.program_id(1)
        sems = refs[4 + has_z + 2 * n + 2 * m:]
        for src, dst in zip(cast_in, cast_out):
            dst[...] = src[...].astype(BF16)
        phases = [_gather_phases(held, *sems[:2], _spans(gathering))]
        if relay:
            phases.append(_relay_phases(held[at], *sems[2:]))
        def each(fs):
            def run():
                for f in fs:
                    f()
            return run

        send, pass_on, finish = [each(fs) for fs in zip(*phases)]

        def fetch(t):
            chip = order_ref[first + t // per]
            return pltpu.make_async_copy(held[at].at[chip, :, pl.ds((t % per) * tn, tn)], tile.at[t % 2],
                                         tile_sem.at[t % 2])

        @pl.when(i == 0)
        def _():
            @pl.when(j == 0)
            def _():
                send()
                fetch(0).start()

            fetch(j).wait()

            @pl.when(j + 1 < nj)
            def _():
                fetch(j + 1).start()

        pl.when((j == nj - 1) & (i == ni - 1))(pass_on)
        rows = pl.ds(pl.multiple_of(i * tm, tm), tm)
        o_ref[...] = _dot(a_ref[rows, :], tile[j % 2]).astype(BF16)
        pl.when((j == nj - 1) & (i == ni - 1))(finish)

    steps = nj * ni
    out = pl.pallas_call(
        body, name=name,
        grid_spec=pltpu.PrefetchScalarGridSpec(
            num_scalar_prefetch=1, grid=(nj, ni),
            in_specs=[pl.BlockSpec((s, D), lambda j, i, o: (0, 0))] + [ANY] * (has_z + n)
            + [pl.BlockSpec((a.shape[0] // steps, a.shape[1]), lambda j, i, o: (j * ni + i, 0)) for a in casting],
            out_specs=[pl.BlockSpec((tm, tn), lambda j, i, o: (i, o[first + j // per] * per + j % per))] + [ANY] * n
            + [pl.BlockSpec((None, a.shape[0] // steps, a.shape[1]), lambda j, i, o: (o[0], j * ni + i, 0))
               for a in casting],
            scratch_shapes=[pltpu.VMEM((2, D, tn), BF16), pltpu.SemaphoreType.DMA((2,))] + _gather_sems(n)
            + (_relay_sems() if relay else [])),
        out_shape=[jax.ShapeDtypeStruct((s, IN_COLS), BF16)] + [jax.ShapeDtypeStruct(a.shape, a.dtype) for a in arrays]
        + [jax.ShapeDtypeStruct((N_CHIPS,) + a.shape, BF16) for a in casting],
        input_output_aliases={**({2: 0} if has_z else {}), **{2 + has_z + w: 1 + w for w in range(n)}},
        compiler_params=_params("arbitrary", "arbitrary", communicates=True),
    )(order, h, *([z] if has_z else []), *arrays, *casting)
    return out[0], out[1:1 + n], out[1 + n:]


def _tril_ws(ws_ref, g):
    r = lax.broadcasted_iota(jnp.int32, (CHUNK, CHUNK), 0)
    c = lax.broadcasted_iota(jnp.int32, (CHUNK, CHUNK), 1)
    return jnp.where(c <= r, ws_ref[g], 0.0).astype(BF16)


def _layer_norm(v):
    mu = jnp.mean(v, axis=-1, keepdims=True)
    d = v - mu
    rstd = lax.rsqrt(jnp.mean(d * d, axis=-1, keepdims=True) + EPS)
    return d * rstd, rstd


def gating_fwd(z, ln_g, ln_b, w_s, bs_t, gathering):
    s = z.shape[0]
    n = len(gathering)
    steps = s // CHUNK

    def body(*refs):
        u_ref, v_ref, lg_ref, lb_ref, ws_ref, bst_ref = refs[:6]
        ya_ref = refs[6 + n]
        ci = pl.program_id(0)
        if n:
            send, pass_on, finish = _gather_phases(refs[7 + n:7 + 2 * n], *refs[7 + 2 * n:], _spans(gathering))
            pl.when(ci == 0)(send)
            pl.when(ci == steps - 1)(pass_on)
        ug = _gelu(u_ref[...].astype(F32))
        vhat, _ = _layer_norm(_gelu(v_ref[...].astype(F32)))
        vn = (vhat * lg_ref[...] + lb_ref[...]).astype(BF16)
        for g in range(GROUPS):
            cols = slice(g * CHUNK, (g + 1) * CHUNK)
            mixed = _dot(_tril_ws(ws_ref, g), vn[:, cols]) + bst_ref[:, g:g + 1]
            ya_ref[:, cols] = (ug[:, cols] * mixed).astype(BF16)
        if n:
            pl.when(ci == steps - 1)(finish)

    out = pl.pallas_call(
        body, name="gating_fwd", grid=(steps,),
        in_specs=[_rows(CHUNK, D, 0), _rows(CHUNK, D, 1), _full((1, D)), _full((1, D)),
                  _full((GROUPS, CHUNK, CHUNK)), _full((CHUNK, GROUPS))] + [ANY] * n,
        out_specs=[_rows(CHUNK, D)] + [ANY] * n,
        out_shape=[jax.ShapeDtypeStruct((s, D), BF16)]
        + [jax.ShapeDtypeStruct(a.shape, a.dtype) for a in _arrays(gathering)],
        input_output_aliases={6 + w: 1 + w for w in range(n)},
        scratch_shapes=_gather_sems(n) if n else [],
        compiler_params=_params("arbitrary", communicates=bool(n)),
    )(z, z, ln_g, ln_b, w_s, bs_t, *_arrays(gathering))
    return out[0], out[1:]


def _attn_tables(s):
    nd = s // ATT_T
    r = np.arange(ATT_T)[None, :, None]
    c = np.arange(ATT_T)[None, None, :]
    delta = np.arange(nd)[:, None, None] * ATT_T + r - c
    count = np.zeros(delta.shape, np.int64)
    for window, dilation in ((128, 1), (512, 4), (2048, 16)):
        count += (delta >= 0) & (delta % dilation == 0) & (delta <= window)
    logc = np.where(count > 0, np.log(np.maximum(count, 1)), MASKED)
    return jnp.asarray(logc, F32)


AUG = 3


def _split3_np(x):
    terms, rest = [], np.asarray(x, np.float64)
    for _ in range(AUG):
        term = np.asarray(rest.astype(jnp.bfloat16), np.float64)
        terms.append(term)
        rest = rest - term
    return terms


def _split3(x):
    terms, rest = [], x
    for _ in range(AUG):
        term = rest.astype(BF16).astype(F32)
        terms.append(term)
        rest = rest - term
    return terms


def _alibi_tables(s):
    nb = s // ATT_T
    slopes = np.exp2(-8.0 * np.arange(1, HEADS + 1, dtype=np.float64) / HEADS)
    ka = np.zeros((HEADS // 2, 2, ATT_T, 128), np.float32)
    kb = np.zeros((HEADS // 2, 2, nb, 128), np.float32)
    for p in range(HEADS // 2):
        for e in range(2):
            base = HEAD_DIM * (1 - e)
            for a, term in enumerate(_split3_np(slopes[2 * p + e] * np.arange(ATT_T))):
                ka[p, e, :, base + a] = term
            for a, term in enumerate(_split3_np(slopes[2 * p + e] * ATT_T * np.arange(nb))):
                kb[p, e, :, base + AUG + a] = term
            ka[p, e, :, base + 2 * AUG:base + 3 * AUG] = 1.0
    return jnp.asarray(ka), jnp.asarray(kb)


def _head_masks():
    lane = lax.broadcasted_iota(jnp.int32, (1, 128), 1)
    first = lane < HEAD_DIM

    def ones(e, n):
        base = HEAD_DIM * (1 - e)
        return ((lane >= base) & (lane < base + n)).astype(F32)

    return first, lane, ones


def _place3(lane, at, terms, other):
    for a, term in enumerate(terms):
        other = jnp.where(lane == at + a, term, other)
    return other


def attn_fwd(z, logc, ka, kb, gathering):
    s = z.shape[0]
    nq = s // ATT_T
    t = ATT_T
    n = len(gathering)
    grp = ATT_GROUP
    ngrp = HEADS // 2 // grp
    wide = 128 * grp
    qcol, kcol, vcol = 2 * D // wide, 3 * D // wide, 4 * D // wide

    def body(*refs):
        q_ref, k_ref, v_ref, lc_ref, ka_ref, kb_ref = refs[:6]
        y_ref, lse_ref = refs[6 + n:8 + n]
        q_s, k_s, v_s, m_s, l_s, acc_s = refs[8 + 2 * n:14 + 2 * n]
        gi, qi = pl.program_id(0), pl.program_id(1)
        first, lane, ones = _head_masks()
        if n:
            send, pass_on, finish = _gather_phases(refs[8 + n:8 + 2 * n], *refs[14 + 2 * n:], _spans(gathering))
            pl.when((gi == 0) & (qi == 0))(send)
            pl.when((gi == ngrp - 1) & (qi == nq - 1))(pass_on)

        @pl.when(qi == 0)
        def _():
            sel = jnp.broadcast_to(first.astype(F32), (t, 128))
            for pr in range(grp):
                cols = slice(pr * 128, (pr + 1) * 128)
                for jb in range(nq):
                    kj = k_ref[jb * t:(jb + 1) * t, cols].astype(F32)
                    vj = v_ref[jb * t:(jb + 1) * t, cols].astype(F32)
                    k_s[pr, 0, jb] = jnp.where(first, kj, ka_ref[pr, 0] + kb_ref[pr, 0, jb:jb + 1, :]).astype(BF16)
                    k_s[pr, 1, jb] = jnp.where(first, ka_ref[pr, 1] + kb_ref[pr, 1, jb:jb + 1, :], kj).astype(BF16)
                    v_s[pr, jb, 0:t, 0:128] = jnp.where(first, vj, 0.0).astype(BF16)
                    v_s[pr, jb, t:2 * t, 0:128] = jnp.where(first, 0.0, vj).astype(BF16)
                    v_s[pr, jb, 0:t, 128:256] = sel.astype(BF16)
                    v_s[pr, jb, t:2 * t, 128:256] = (1.0 - sel).astype(BF16)

        for pr in range(grp):
            q = q_ref[:, pr * 128:(pr + 1) * 128].astype(F32) * (1.0 / math.sqrt(HEAD_DIM))
            q_s[pr, 0] = jnp.where(first, q, ones(0, 2 * AUG)).astype(BF16)
            q_s[pr, 1] = jnp.where(first, ones(1, 2 * AUG), q).astype(BF16)
        m_s[...] = jnp.full_like(m_s, MASKED)
        l_s[...] = jnp.zeros_like(l_s)
        acc_s[...] = jnp.zeros_like(acc_s)

        def scores(j):
            return tuple(_dot(q_s[pr, e], k_s[pr, e, j], NT) for pr in range(grp) for e in range(2))

        def step(j, carry):
            softmax_block(j, scores(j))
            return carry

        def softmax_block(j, u):
            lc = lc_ref[qi - j]
            for pr in range(grp):
                u0 = u[2 * pr] + lc
                u1 = u[2 * pr + 1] + lc
                m0, m1 = m_s[pr, 0], m_s[pr, 1]
                n0 = jnp.maximum(m0, jnp.max(u0, axis=-1, keepdims=True))
                n1 = jnp.maximum(m1, jnp.max(u1, axis=-1, keepdims=True))
                m_s[pr, 0], m_s[pr, 1] = n0, n1
                p = jnp.concatenate([jnp.exp(u0 - jnp.concatenate([n0, n0], axis=1)).astype(BF16),
                                     jnp.exp(u1 - jnp.concatenate([n1, n1], axis=1)).astype(BF16)], axis=1)
                pv = _dot(p, v_s[pr, j])
                alpha = jnp.where(first, jnp.exp(m0 - n0), jnp.exp(m1 - n1))
                acc_s[pr] = acc_s[pr] * alpha + pv[:, 0:128]
                l_s[pr] = l_s[pr] * alpha + pv[:, 128:256]

        lax.fori_loop(0, qi + 1, step, 0)
        for pr in range(grp):
            cols = slice(pr * 128, (pr + 1) * 128)
            y_ref[:, cols] = (acc_s[pr] / l_s[pr]).astype(BF16)
            lse_ref[:, cols] = jnp.where(first, m_s[pr, 0], m_s[pr, 1]) + jnp.log(l_s[pr])
        if n:
            pl.when((gi == ngrp - 1) & (qi == nq - 1))(finish)

    out = pl.pallas_call(
        body, name="attn_fwd", grid=(ngrp, nq),
        in_specs=[pl.BlockSpec((t, wide), lambda g, i: (i, qcol + g)),
                  pl.BlockSpec((s, wide), lambda g, i: (0, kcol + g)),
                  pl.BlockSpec((s, wide), lambda g, i: (0, vcol + g)),
                  _full((nq, t, t)),
                  pl.BlockSpec((grp, 2, t, 128), lambda g, i: (g, 0, 0, 0)),
                  pl.BlockSpec((grp, 2, nq, 128), lambda g, i: (g, 0, 0, 0))] + [ANY] * n,
        out_specs=[pl.BlockSpec((t, wide), lambda g, i: (i, g)), pl.BlockSpec((t, wide), lambda g, i: (i, g))]
        + [ANY] * n,
        out_shape=[jax.ShapeDtypeStruct((s, D), BF16), jax.ShapeDtypeStruct((s, D), F32)]
        + [jax.ShapeDtypeStruct(a.shape, a.dtype) for a in _arrays(gathering)],
        input_output_aliases={6 + w: 2 + w for w in range(n)},
        scratch_shapes=[pltpu.VMEM((grp, 2, t, 128), BF16), pltpu.VMEM((grp, 2, nq, t, 128), BF16),
                        pltpu.VMEM((grp, nq, 2 * t, 256), BF16), pltpu.VMEM((grp, 2, t, 128), F32),
                        pltpu.VMEM((grp, t, 128), F32), pltpu.VMEM((grp, t, 128), F32)]
        + (_gather_sems(n) if n else []),
        compiler_params=_params("arbitrary", "arbitrary", communicates=bool(n)),
    )(z, z, z, logc, ka, kb, *_arrays(gathering))
    return out[0], out[1], out[2:]


def proj_merge(ya, yb, wa, wb, z, bg, gathering):
    s = ya.shape[0]
    tm = 512
    n = len(gathering)
    steps = s // tm

    def body(*refs):
        ya_ref, yb_ref, wa_ref, wb_ref, ga_ref, gb_ref, bg_ref = refs[:7]
        mg_ref, pa_ref, pb_ref = refs[7 + n:10 + n]
        i = pl.program_id(0)
        if n:
            send, pass_on, finish = _gather_phases(refs[10 + n:10 + 2 * n], *refs[10 + 2 * n:], _spans(gathering))
            pl.when(i == 0)(send)
            pl.when(i == steps - 1)(pass_on)
        pa = _dot(ya_ref[...], wa_ref[...])
        pb = _dot(yb_ref[...], wb_ref[...])
        sa = _sigmoid(ga_ref[...] + bg_ref[0:1, :])
        sb = _sigmoid(gb_ref[...] + bg_ref[1:2, :])
        mg_ref[...] = (sa * pa + sb * pb).astype(BF16)
        pa_ref[...] = pa.astype(BF16)
        pb_ref[...] = pb.astype(BF16)
        if n:
            pl.when(i == steps - 1)(finish)

    out = jax.ShapeDtypeStruct((s, D), BF16)
    res = pl.pallas_call(
        body, name="proj_merge", grid=(steps,),
        in_specs=[_rows(tm, D), _rows(tm, D), _full((D, D)), _full((D, D)),
                  _rows(tm, D, 5), _rows(tm, D, 6), _full((2, D))] + [ANY] * n,
        out_specs=[_rows(tm, D)] * 3 + [ANY] * n,
        out_shape=[out] * 3 + [jax.ShapeDtypeStruct(a.shape, a.dtype) for a in _arrays(gathering)],
        input_output_aliases={7 + w: 3 + w for w in range(n)},
        scratch_shapes=_gather_sems(n) if n else [],
        compiler_params=_params("arbitrary", communicates=bool(n)),
    )(ya, yb, wa, wb, z, z, bg, *_arrays(gathering))
    return res[0], res[1], res[2], res[3:]


def out_norm(merged, w_out, x, g_post, g_fpre, gathering):
    s = x.shape[0]
    tm = 512
    n = len(gathering)
    steps = s // tm

    def body(*refs):
        mg_ref, w_ref, x_ref, gp_ref, gf_ref = refs[:5]
        o_ref, x1_ref, h2_ref = refs[5 + n:8 + n]
        i = pl.program_id(0)
        if n:
            send, pass_on, finish = _gather_phases(refs[8 + n:8 + 2 * n], *refs[8 + 2 * n:], _spans(gathering))
            pl.when(i == 0)(send)
            pl.when(i == steps - 1)(pass_on)
        o = _dot(mg_ref[...], w_ref[...])
        ohat, _ = _rms(o)
        x1 = x_ref[...] + ohat * gp_ref[...]
        x1hat, _ = _rms(x1)
        o_ref[...] = o
        x1_ref[...] = x1
        h2_ref[...] = (x1hat * gf_ref[...]).astype(BF16)
        if n:
            pl.when(i == steps - 1)(finish)

    res = pl.pallas_call(
        body, name="out_norm", grid=(steps,),
        in_specs=[_rows(tm, D), _full((D, D)), _rows(tm, D), _full((1, D)), _full((1, D))] + [ANY] * n,
        out_specs=[_rows(tm, D)] * 3 + [ANY] * n,
        out_shape=[jax.ShapeDtypeStruct((s, D), F32), jax.ShapeDtypeStruct((s, D), F32),
                   jax.ShapeDtypeStruct((s, D), BF16)]
        + [jax.ShapeDtypeStruct(a.shape, a.dtype) for a in _arrays(gathering)],
        input_output_aliases={5 + w: 3 + w for w in range(n)},
        scratch_shapes=_gather_sems(n) if n else [],
        compiler_params=_params("arbitrary", communicates=bool(n)),
    )(merged, w_out, x, g_post, g_fpre, *_arrays(gathering))
    return res[0], res[1], res[2], res[3:]


def mm_ff1(h2, wg, gathering):
    s = h2.shape[0]
    tm = 1024
    n = len(gathering)
    ni = s // tm

    def body(*refs):
        a_ref, b_ref = refs[:2]
        o_ref, r_ref = refs[2 + n:4 + n]
        i, j = pl.program_id(0), pl.program_id(1)
        if n:
            send, pass_on, finish = _gather_phases(refs[4 + n:4 + 2 * n], *refs[4 + 2 * n:], _spans(gathering))
            pl.when((i == 0) & (j == 0))(send)
            pl.when((i == ni - 1) & (j == N_CHIPS // 2))(pass_on)
        a = _dot(a_ref[...], b_ref[...])
        o_ref[...] = a.astype(BF16)
        r = jnp.maximum(a, 0.0)
        r_ref[...] = (r * r).astype(BF16)
        if n:
            pl.when((i == ni - 1) & (j == N_CHIPS - 1))(finish)

    res = pl.pallas_call(
        body, name="mm_ff1", grid=(ni, N_CHIPS),
        in_specs=[pl.BlockSpec((tm, D), lambda i, j: (i, 0)), pl.BlockSpec((None, D, D), lambda i, j: (j, 0, 0))]
        + [ANY] * n,
        out_specs=[pl.BlockSpec((tm, D), lambda i, j: (i, j))] * 2 + [ANY] * n,
        out_shape=[jax.ShapeDtypeStruct((s, D_FF), BF16), jax.ShapeDtypeStruct((s, D_FF), BF16)]
        + [jax.ShapeDtypeStruct(a.shape, a.dtype) for a in _arrays(gathering)],
        input_output_aliases={2 + w: 2 + w for w in range(n)},
        scratch_shapes=_gather_sems(n) if n else [],
        compiler_params=_params("arbitrary", "arbitrary", communicates=bool(n)),
    )(h2, wg, *_arrays(gathering))
    return res[0], res[1], res[2:]


def ff2_loss(rl, w_ff2, x1, target, g_fpost):
    s = x1.shape[0]
    tm = 256

    def body(rl_ref, w_ref, x1_ref, t_ref, g_ref, dy_ref, df_ref, dg_ref, loss_ref):
        @pl.when(pl.program_id(0) == 0)
        def _():
            dg_ref[...] = jnp.zeros_like(dg_ref)
            loss_ref[...] = jnp.zeros_like(loss_ref)

        f = _dot(rl_ref[...], w_ref[...])
        fhat, r = _rms(f)
        err = x1_ref[...] + fhat * g_ref[...] - t_ref[...]
        loss_ref[...] += 0.5 * jnp.sum(jnp.mean(err * err, axis=-1, keepdims=True), axis=0, keepdims=True)
        dy = err * (1.0 / D)
        dy_ref[...] = dy
        dg_ref[...] += jnp.sum(dy * fhat, axis=0, keepdims=True)
        df_ref[...] = _rms_bwd(dy * g_ref[...], fhat, r).astype(BF16)

    return pl.pallas_call(
        body, name="ff2_loss", grid=(s // tm,),
        in_specs=[_rows(tm, D_FF), _full((D_FF, D)), _rows(tm, D), _rows(tm, D), _full((1, D))],
        out_specs=[_rows(tm, D), _rows(tm, D), _full((1, D)), _full((1, 1))],
        out_shape=[jax.ShapeDtypeStruct((s, D), F32), jax.ShapeDtypeStruct((s, D), BF16),
                   jax.ShapeDtypeStruct((1, D), F32), jax.ShapeDtypeStruct((1, 1), F32)],
        compiler_params=_params("arbitrary"),
    )(rl, w_ff2, x1, target, g_fpost)


def mm_tn(name, a, b, ta, tb, out_shape, out_spec):
    s = a.shape[0]

    def body(a_ref, b_ref, o_ref):
        o_ref[...] = _dot(a_ref[...], b_ref[...], TN)

    return pl.pallas_call(
        body, name=name, grid=(a.shape[1] // ta, b.shape[1] // tb),
        in_specs=[pl.BlockSpec((s, ta), lambda i, j: (0, i)), pl.BlockSpec((s, tb), lambda i, j: (0, j))],
        out_specs=out_spec, out_shape=jax.ShapeDtypeStruct(out_shape, F32),
        compiler_params=_params("parallel", "parallel"),
    )(a, b)


def mm_nt(name, a, w):
    s = a.shape[0]
    tm = 512

    def body(a_ref, w_ref, o_ref):
        o_ref[...] = _dot(a_ref[...], w_ref[...], NT).astype(BF16)

    return pl.pallas_call(
        body, name=name, grid=(s // tm,), in_specs=[_rows(tm, D), _full((D, D))], out_specs=_rows(tm, D),
        out_shape=jax.ShapeDtypeStruct((s, D), BF16), compiler_params=_params("parallel"),
    )(a, w)


def ff2_bwd(df, w_ff2, a):
    s = df.shape[0]
    tm = 1024

    def body(df_ref, w_ref, a_ref, da_ref):
        drl = _dot(df_ref[...], w_ref[...], NT)
        da_ref[...] = (drl * (2.0 * jnp.maximum(a_ref[...].astype(F32), 0.0))).astype(BF16)

    return pl.pallas_call(
        body, name="ff2_bwd", grid=(s // tm, D_FF // D),
        in_specs=[pl.BlockSpec((tm, D), lambda i, j: (i, 0)), pl.BlockSpec((D, D), lambda i, j: (j, 0)),
                  pl.BlockSpec((tm, D), lambda i, j: (i, j))],
        out_specs=pl.BlockSpec((tm, D), lambda i, j: (i, j)),
        out_shape=jax.ShapeDtypeStruct((s, D_FF), BF16), compiler_params=_params("parallel", "parallel"),
    )(df, w_ff2, a)


def ff1_bwd_norms(da, wg, x1, o, dy, g_fpre, g_post, swapping):
    s = x1.shape[0]
    tm = 512
    n = len(swapping)

    def body(*refs):
        da_ref, w_ref, x1_ref, o_ref, dy_ref, gf_ref, gp_ref = refs[:7]
        dx1_ref, do_ref, dgf_ref, dgp_ref = refs[7 + n:11 + n]
        acc_ref = refs[11 + 2 * n]
        i, k = pl.program_id(0), pl.program_id(1)
        if n:
            send, finish = _swap_phases(refs[7:7 + n], refs[11 + n:11 + 2 * n], *refs[12 + 2 * n:])
            pl.when((i == 0) & (k == 0))(send)

        @pl.when((i == 0) & (k == 0))
        def _():
            dgf_ref[...] = jnp.zeros_like(dgf_ref)
            dgp_ref[...] = jnp.zeros_like(dgp_ref)

        part = _dot(da_ref[...], w_ref[...], NT)

        @pl.when(k == 0)
        def _():
            acc_ref[...] = part

        @pl.when(k > 0)
        def _():
            acc_ref[...] += part

        @pl.when(k == N_CHIPS - 1)
        def _():
            dh2 = acc_ref[...]
            x1hat, r2 = _rms(x1_ref[...])
            dgf_ref[...] += jnp.sum(dh2 * x1hat, axis=0, keepdims=True)
            dx1 = dy_ref[...] + _rms_bwd(dh2 * gf_ref[...], x1hat, r2)
            ohat, r1 = _rms(o_ref[...])
            dgp_ref[...] += jnp.sum(dx1 * ohat, axis=0, keepdims=True)
            dx1_ref[...] = dx1
            do_ref[...] = _rms_bwd(dx1 * gp_ref[...], ohat, r1).astype(BF16)

        if n:
            pl.when((i == s // tm - 1) & (k == N_CHIPS - 1))(finish)

    row = pl.BlockSpec((tm, D), lambda i, k: (i, 0))
    vec = pl.BlockSpec((1, D), lambda i, k: (0, 0))
    res = pl.pallas_call(
        body, name="ff1_bwd_norms", grid=(s // tm, N_CHIPS),
        in_specs=[pl.BlockSpec((tm, D), lambda i, k: (i, k)), pl.BlockSpec((None, D, D), lambda i, k: (k, 0, 0)),
                  row, row, row, vec, vec] + [ANY] * n,
        out_specs=[row, row, vec, vec] + [ANY] * n,
        out_shape=[jax.ShapeDtypeStruct((s, D), F32), jax.ShapeDtypeStruct((s, D), BF16),
                   jax.ShapeDtypeStruct((1, D), F32), jax.ShapeDtypeStruct((1, D), F32)] + _swap_shapes(swapping),
        scratch_shapes=[pltpu.VMEM((tm, D), F32)] + (_swap_sems(n) if n else []),
        compiler_params=_params("arbitrary", "arbitrary", communicates=bool(n)),
    )(da, wg, x1, o, dy, g_fpre, g_post, *swapping)
    return res[0], res[1], res[2], res[3], res[4:]


def out_bwd_gates(do, w_out, pa, pb, z, bg):
    s = do.shape[0]
    tm = 512

    def body(do_ref, w_ref, pa_ref, pb_ref, ga_ref, gb_ref, bg_ref, dpa_ref, dpb_ref, dga_ref, dgb_ref, dbg_ref):
        @pl.when(pl.program_id(0) == 0)
        def _():
            dbg_ref[...] = jnp.zeros_like(dbg_ref)

        dm = _dot(do_ref[...], w_ref[...], NT)
        sa = _sigmoid(ga_ref[...] + bg_ref[0:1, :])
        sb = _sigmoid(gb_ref[...] + bg_ref[1:2, :])
        dpa_ref[...] = (dm * sa).astype(BF16)
        dpb_ref[...] = (dm * sb).astype(BF16)
        dga = dm * pa_ref[...].astype(F32) * (sa * (1.0 - sa))
        dgb = dm * pb_ref[...].astype(F32) * (sb * (1.0 - sb))
        dga_ref[...] = dga.astype(BF16)
        dgb_ref[...] = dgb.astype(BF16)
        dbg_ref[0:1, :] += jnp.sum(dga, axis=0, keepdims=True)
        dbg_ref[1:2, :] += jnp.sum(dgb, axis=0, keepdims=True)

    out = jax.ShapeDtypeStruct((s, D), BF16)
    return pl.pallas_call(
        body, name="out_bwd_gates", grid=(s // tm,),
        in_specs=[_rows(tm, D), _full((D, D)), _rows(tm, D), _rows(tm, D), _rows(tm, D, 5), _rows(tm, D, 6),
                  _full((2, D))],
        out_specs=[_rows(tm, D)] * 4 + [_full((2, D))],
        out_shape=[out] * 4 + [jax.ShapeDtypeStruct((2, D), F32)], compiler_params=_params("arbitrary"),
    )(do, w_out, pa, pb, z, z, bg)


def gating_bwd(z, dya, ln_g, ln_b, w_s, bs_t, swapping):
    s = z.shape[0]
    ones = functools.partial(jnp.ones, (8, CHUNK), BF16)
    n = len(swapping)

    def body(*refs):
        u_ref, v_ref, dya_ref, lg_ref, lb_ref, ws_ref, bst_ref = refs[:7]
        du_ref, dv_ref, dws_ref, dbs_ref, dlg_ref, dlb_ref = refs[7 + n:13 + n]
        dvn_ref = refs[13 + 2 * n]
        ci = pl.program_id(0)
        if n:
            send, finish = _swap_phases(refs[7:7 + n], refs[13 + n:13 + 2 * n], *refs[14 + 2 * n:])
            pl.when(ci == 0)(send)

        @pl.when(ci == 0)
        def _():
            dws_ref[...] = jnp.zeros_like(dws_ref)
            dbs_ref[...] = jnp.zeros_like(dbs_ref)
            dlg_ref[...] = jnp.zeros_like(dlg_ref)
            dlb_ref[...] = jnp.zeros_like(dlb_ref)

        ug, dug_du = _gelu_and_grad(u_ref[...].astype(F32))
        vg, dvg_dv = _gelu_and_grad(v_ref[...].astype(F32))
        vhat, rstd = _layer_norm(vg)
        vn = (vhat * lg_ref[...] + lb_ref[...]).astype(BF16)
        dya = dya_ref[...].astype(F32)
        for g in range(GROUPS):
            cols = slice(g * CHUNK, (g + 1) * CHUNK)
            ws = _tril_ws(ws_ref, g)
            mixed = _dot(ws, vn[:, cols]) + bst_ref[:, g:g + 1]
            du_ref[:, cols] = (dya[:, cols] * mixed * dug_du[:, cols]).astype(BF16)
            dmix = (dya[:, cols] * ug[:, cols]).astype(BF16)
            dbs_ref[g] += _dot(ones(), dmix, NT)
            dws_ref[g] += _dot(dmix, vn[:, cols], NT)
            dvn_ref[:, cols] = _dot(ws, dmix, TN)
        dvn = dvn_ref[...]
        dlg_ref[...] += jnp.sum(dvn * vhat, axis=0, keepdims=True)
        dlb_ref[...] += jnp.sum(dvn, axis=0, keepdims=True)
        dvh = dvn * lg_ref[...]
        dvg = rstd * (dvh - jnp.mean(dvh, axis=-1, keepdims=True)
                      - vhat * jnp.mean(dvh * vhat, axis=-1, keepdims=True))
        dv_ref[...] = (dvg * dvg_dv).astype(BF16)

        @pl.when(ci == pl.num_programs(0) - 1)
        def _():
            r = lax.broadcasted_iota(jnp.int32, (CHUNK, CHUNK), 0)
            c = lax.broadcasted_iota(jnp.int32, (CHUNK, CHUNK), 1)
            for g in range(GROUPS):
                dws_ref[g] = jnp.where(c <= r, dws_ref[g], 0.0)

        if n:
            pl.when(ci == pl.num_programs(0) - 1)(finish)

    out = jax.ShapeDtypeStruct((s, D), BF16)
    res = pl.pallas_call(
        body, name="gating_bwd", grid=(s // CHUNK,),
        in_specs=[_rows(CHUNK, D, 0), _rows(CHUNK, D, 1), _rows(CHUNK, D), _full((1, D)), _full((1, D)),
                  _full((GROUPS, CHUNK, CHUNK)), _full((CHUNK, GROUPS))] + [ANY] * n,
        out_specs=[_rows(CHUNK, D), _rows(CHUNK, D), _full((GROUPS, CHUNK, CHUNK)), _full((GROUPS, 8, CHUNK)),
                   _full((1, D)), _full((1, D))] + [ANY] * n,
        out_shape=[out, out, jax.ShapeDtypeStruct((GROUPS, CHUNK, CHUNK), F32),
                   jax.ShapeDtypeStruct((GROUPS, 8, CHUNK), F32),
                   jax.ShapeDtypeStruct((1, D), F32), jax.ShapeDtypeStruct((1, D), F32)] + _swap_shapes(swapping),
        scratch_shapes=[pltpu.VMEM((CHUNK, D), F32)] + (_swap_sems(n) if n else []),
        compiler_params=_params("arbitrary", communicates=bool(n)),
    )(z, z, dya, ln_g, ln_b, w_s, bs_t, *swapping)
    return (*res[:6], res[6:])


def attn_bwd(z, yb, dyb, lse, logc, ka, kb, scattering):
    s = z.shape[0]
    nq = s // ATT_T
    t = ATT_T
    grp = ATT_BWD_GROUP
    ngrp = HEADS // 2 // grp
    wide = 128 * grp
    qcol, kcol, vcol = 2 * D // wide, 3 * D // wide, 4 * D // wide
    scale = 1.0 / math.sqrt(HEAD_DIM)
    n = len(scattering)

    def body(*refs):
        q_ref, k_ref, v_ref, y_ref, dy_ref, lse_ref, lc_ref, ka_ref, kb_ref = refs[:9]
        dq_ref, dk_ref, dv_ref = refs[9 + n:12 + n]
        qa_s, qt_s, da_s, dt_s, dq_s, dkt_s, dvt_s = refs[12 + 2 * n:19 + 2 * n]
        gi, j = pl.program_id(0), pl.program_id(1)
        first, lane, ones = _head_masks()
        if n:
            send, finish = _scatter_phases(refs[9:9 + n], refs[12 + n:12 + 2 * n], *refs[19 + 2 * n:])
            pl.when((gi == 0) & (j == 0))(send)

        @pl.when(j == 0)
        def _():
            dq_s[...] = jnp.zeros_like(dq_s)
            for pr in range(grp):
                cols = slice(pr * 128, (pr + 1) * 128)
                for ib in range(nq):
                    rows = slice(ib * t, (ib + 1) * t)
                    q = q_ref[rows, cols].astype(F32) * scale
                    lse = lse_ref[rows, cols]
                    qa_s[pr, 0, ib] = jnp.where(first, q, _place3(lane, HEAD_DIM + 2 * AUG, _split3(-lse[:, 0:1]),
                                                                  ones(0, 2 * AUG))).astype(BF16)
                    qa_s[pr, 1, ib] = jnp.where(
                        first, _place3(lane, 2 * AUG, _split3(-lse[:, HEAD_DIM:HEAD_DIM + 1]), ones(1, 2 * AUG)),
                        q).astype(BF16)
                    qt_s[pr, ib, :, 0:t] = jnp.where(first, q, 0.0).T.astype(BF16)
                    qt_s[pr, ib, :, t:2 * t] = jnp.where(first, 0.0, q).T.astype(BF16)
                    do = dy_ref[rows, cols].astype(F32)
                    prod = do * y_ref[rows, cols].astype(F32)
                    dd0 = jnp.sum(jnp.where(first, prod, 0.0), axis=-1, keepdims=True)
                    dd1 = jnp.sum(jnp.where(first, 0.0, prod), axis=-1, keepdims=True)
                    da_s[pr, 0, ib] = jnp.where(first, do, _place3(lane, HEAD_DIM, _split3(-dd0), 0.0)).astype(BF16)
                    da_s[pr, 1, ib] = jnp.where(first, _place3(lane, 0, _split3(-dd1), 0.0), do).astype(BF16)
                    dt_s[pr, ib, :, 0:t] = jnp.where(first, do, 0.0).T.astype(BF16)
                    dt_s[pr, ib, :, t:2 * t] = jnp.where(first, 0.0, do).T.astype(BF16)

        keys = []
        for pr in range(grp):
            kj = k_ref[:, pr * 128:(pr + 1) * 128].astype(F32)
            vj = v_ref[:, pr * 128:(pr + 1) * 128].astype(F32)
            keys.append((
                jnp.where(first, kj, ka_ref[pr, 0] + kb_ref[pr, 0, pl.ds(j, 1), :]).astype(BF16),
                jnp.where(first, ka_ref[pr, 1] + kb_ref[pr, 1, pl.ds(j, 1), :], kj).astype(BF16),
                jnp.concatenate([jnp.where(first, kj, 0.0), jnp.where(first, 0.0, kj)], axis=0).astype(BF16),
                jnp.where(first, vj, ones(0, AUG)).astype(BF16),
                jnp.where(first, ones(1, AUG), vj).astype(BF16)))
        dkt_s[...] = jnp.zeros_like(dkt_s)
        dvt_s[...] = jnp.zeros_like(dvt_s)

        def step(i, _):
            lc = lc_ref[i - j]
            rows = pl.ds(pl.multiple_of(i * t, t), t)
            for pr in range(grp):
                k0a, k1a, kst, v0a, v1a = keys[pr]
                p0 = jnp.exp(_dot(qa_s[pr, 0, i], k0a, NT) + lc)
                p1 = jnp.exp(_dot(qa_s[pr, 1, i], k1a, NT) + lc)
                e0 = (p0 * _dot(da_s[pr, 0, i], v0a, NT)).astype(BF16)
                e1 = (p1 * _dot(da_s[pr, 1, i], v1a, NT)).astype(BF16)
                dq_s[pr, rows, :] += _dot(jnp.concatenate([e0, e1], axis=1), kst)
                dvt_s[pr] += _dot(dt_s[pr, i], jnp.concatenate([p0.astype(BF16), p1.astype(BF16)], axis=0))
                dkt_s[pr] += _dot(qt_s[pr, i], jnp.concatenate([e0, e1], axis=0))
            return 0

        lax.fori_loop(j, nq, step, 0)
        for pr in range(grp):
            dk_ref[:, pr * 128:(pr + 1) * 128] = dkt_s[pr].T.astype(BF16)
            dv_ref[:, pr * 128:(pr + 1) * 128] = dvt_s[pr].T.astype(BF16)

        @pl.when(j == nq - 1)
        def _():
            for pr in range(grp):
                dq_ref[:, pr * 128:(pr + 1) * 128] = (dq_s[pr] * scale).astype(BF16)

        if n:
            pl.when((gi == ngrp - 1) & (j == nq - 1))(finish)

    colblock = lambda c: pl.BlockSpec((s, wide), lambda g, j: (0, c + g))
    blk = lambda c: pl.BlockSpec((t, wide), lambda g, j: (j, c + g))
    out = jax.ShapeDtypeStruct((s, D), BF16)
    res = pl.pallas_call(
        body, name="attn_bwd", grid=(ngrp, nq),
        in_specs=[colblock(qcol), blk(kcol), blk(vcol), colblock(0), colblock(0), colblock(0),
                  _full((nq, t, t)), pl.BlockSpec((grp, 2, t, 128), lambda g, j: (g, 0, 0, 0)),
                  pl.BlockSpec((grp, 2, nq, 128), lambda g, j: (g, 0, 0, 0))] + [ANY] * n,
        out_specs=[colblock(0), blk(0), blk(0)] + [ANY] * n, out_shape=[out] * 3 + _scatter_shapes(scattering),
        scratch_shapes=[pltpu.VMEM((grp, 2, nq, t, 128), BF16), pltpu.VMEM((grp, nq, 128, 2 * t), BF16),
                        pltpu.VMEM((grp, 2, nq, t, 128), BF16), pltpu.VMEM((grp, nq, 128, 2 * t), BF16),
                        pltpu.VMEM((grp, s, 128), F32), pltpu.VMEM((grp, 128, t), F32),
                        pltpu.VMEM((grp, 128, t), F32)]
        + (_scatter_sems(n) if n else []),
        compiler_params=_params("arbitrary", "arbitrary", communicates=bool(n)),
    )(z, z, z, yb, dyb, lse, logc, ka, kb, *scattering)
    return res[0], res[1], res[2], res[3:]


def in_bwd_norm(dz, wg, x, dx1, g_pre, scattering, gathering=None):
    s = x.shape[0]
    tm = 512
    n = len(scattering)
    g = 0 if gathering is None else 1
    last = (s // tm - 1, N_CHIPS - 1)

    def body(*refs):
        dz_ref, w_ref, x_ref, dx1_ref, g_ref = refs[:5]
        dx_ref, dg_ref = refs[5 + n + g:7 + n + g]
        acc_ref = refs[7 + 2 * n + 2 * g]
        sems = refs[8 + 2 * n + 2 * g:]
        i, k = pl.program_id(0), pl.program_id(1)
        if n:
            send, finish = _scatter_phases(refs[5:5 + n], refs[7 + n + g:7 + 2 * n + g], *sems[:2])
            pl.when((i == 0) & (k == 0))(send)
        if g:
            send8, pass_on8, finish8 = _allgather8_phases(refs[7 + 2 * n + g], *sems[2 * (n > 0):])
            pl.when((i == 0) & (k == 0))(send8)
            pl.when((i == last[0]) & (k == last[1]))(pass_on8)

        @pl.when((i == 0) & (k == 0))
        def _():
            dg_ref[...] = jnp.zeros_like(dg_ref)

        part = _dot(dz_ref[...], w_ref[...], NT)

        @pl.when(k == 0)
        def _():
            acc_ref[...] = part

        @pl.when(k > 0)
        def _():
            acc_ref[...] += part

        @pl.when(k == N_CHIPS - 1)
        def _():
            dh = acc_ref[...]
            xhat, r = _rms(x_ref[...])
            dg_ref[...] += jnp.sum(dh * xhat, axis=0, keepdims=True)
            dx_ref[...] = dx1_ref[...] + _rms_bwd(dh * g_ref[...], xhat, r)

        if n:
            pl.when((i == last[0]) & (k == last[1]))(finish)
        if g:
            pl.when((i == last[0]) & (k == last[1]))(finish8)

    row = pl.BlockSpec((tm, D), lambda i, k: (i, 0))
    vec = pl.BlockSpec((1, D), lambda i, k: (0, 0))
    res = pl.pallas_call(
        body, name="in_bwd_norm", grid=(s // tm, N_CHIPS),
        in_specs=[pl.BlockSpec((tm, IN_SHARD), lambda i, k: (i, k)),
                  pl.BlockSpec((None, D, IN_SHARD), lambda i, k: (k, 0, 0)), row, row, vec] + [ANY] * (n + g),
        out_specs=[row, vec] + [ANY] * (n + g),
        out_shape=[jax.ShapeDtypeStruct((s, D), F32), jax.ShapeDtypeStruct((1, D), F32)]
        + _scatter_shapes(scattering) + ([jax.ShapeDtypeStruct(gathering.shape, gathering.dtype)] if g else []),
        input_output_aliases={5 + n: 2 + n} if g else {},
        scratch_shapes=[pltpu.VMEM((tm, D), F32)] + (_scatter_sems(n) if n else [])
        + ([pltpu.SemaphoreType.DMA((7,)), pltpu.SemaphoreType.DMA((7,))] if g else []),
        compiler_params=_params("arbitrary", "arbitrary", communicates=bool(n + g)),
    )(dz, wg, x, dx1, g_pre, *scattering, *([gathering] if g else []))
    return res[0], res[1], res[2:2 + n], (res[2 + n] if g else None)


def _adamw_math(w, g, m, v):
    m = ADAM_B1 * m + (1.0 - ADAM_B1) * g
    v = ADAM_B2 * v + (1.0 - ADAM_B2) * (g * g)
    m_hat = m / (1.0 - ADAM_B1 ** ADAM_STEP)
    v_hat = v / (1.0 - ADAM_B2 ** ADAM_STEP)
    delta = -ADAM_LR * (m_hat / (jnp.sqrt(v_hat) + ADAM_EPS) + ADAM_WD * w)
    return delta, m, v


def adamw(name, w, g, m, v, tr, gathering=None):
    r, c = w.shape
    n = 0 if gathering is None else 1
    steps = r // tr

    def body(*refs):
        w_ref, g_ref, m_ref, v_ref = refs[:4]
        go_ref, d_ref, nm_ref, nv_ref = refs[4 + n:8 + n]
        i = pl.program_id(0)
        if n:
            send, pass_on, finish = _allgather8_phases(refs[8 + n], *refs[9 + n:])
            pl.when(i == 0)(send)
            pl.when(i == steps - 1)(pass_on)
        g = g_ref[...]
        go_ref[...] = g
        d_ref[...], nm_ref[...], nv_ref[...] = _adamw_math(w_ref[...], g, m_ref[...], v_ref[...])
        if n:
            pl.when(i == steps - 1)(finish)

    out = jax.ShapeDtypeStruct((r, c), F32)
    res = pl.pallas_call(
        body, name=name, grid=(steps,), in_specs=[_rows(tr, c)] * 4 + [ANY] * n,
        out_specs=[_rows(tr, c)] * 4 + [ANY] * n,
        out_shape=[out] * 4 + ([jax.ShapeDtypeStruct(gathering.shape, gathering.dtype)] if n else []),
        input_output_aliases={4: 4} if n else {},
        scratch_shapes=[pltpu.SemaphoreType.DMA((7,)), pltpu.SemaphoreType.DMA((7,))] if n else [],
        compiler_params=_params("arbitrary", communicates=bool(n)),
    )(w, g, m, v, *([gathering] if n else []))
    return res


def _allgather8_phases(buf, send_sems, recv_sems):
    x, y, c, chips = _place()
    me = 2 * x + y
    sibling = (x, y, 1 - c)
    rows = buf.shape[1] // 2

    def part(chip, core):
        return buf.at[chip, pl.ds(core * rows, rows)]

    def copy(k, block, to):
        return pltpu.make_async_remote_copy(src_ref=block, dst_ref=block, send_sem=send_sems.at[k],
                                            recv_sem=recv_sems.at[k], device_id=to, device_id_type=MESH)

    def chip_of(j):
        return 2 * chips[j][0] + chips[j][1]

    def send():
        copy(0, part(me, c), sibling).start()
        for j in range(3):
            copy(1 + j, part(me, c), (chips[j][0], chips[j][1], c)).start()

    def pass_on():
        for j in range(3):
            copy(1 + j, part(chip_of(j), c), (chips[j][0], chips[j][1], c)).wait_recv()
            copy(4 + j, part(chip_of(j), c), sibling).start()

    def finish():
        copy(0, part(me, 1 - c), sibling).wait_recv()
        for j in range(3):
            copy(4 + j, part(chip_of(j), 1 - c), sibling).wait_recv()
        copy(0, part(me, c), sibling).wait_send()
        for j in range(3):
            copy(1 + j, part(me, c), (chips[j][0], chips[j][1], c)).wait_send()
            copy(4 + j, part(chip_of(j), c), sibling).wait_send()

    return send, pass_on, finish


def add_halves(name, g, recv, c_idx, tr):
    n, h, c = recv.shape

    def body(c_ref, g_ref, r_ref, o_ref):
        o_ref[...] = (g_ref[...] + r_ref[...]).astype(BF16)

    nb = h // tr
    return pl.pallas_call(
        body, name=name,
        grid_spec=pltpu.PrefetchScalarGridSpec(
            num_scalar_prefetch=1, grid=(n, nb),
            in_specs=[pl.BlockSpec((None, tr, c), lambda k, i, c_ref: (k, c_ref[0] * nb + i, 0)),
                      pl.BlockSpec((None, tr, c), lambda k, i, c_ref: (k, i, 0))],
            out_specs=pl.BlockSpec((None, tr, c), lambda k, i, c_ref: (k, i, 0))),
        out_shape=jax.ShapeDtypeStruct((n, h, c), BF16), compiler_params=_params("parallel", "parallel"),
    )(c_idx, g, recv)


def sum_chips(name, parts, recv, where, tr):
    n, h, c = recv.shape
    nb = h // tr

    def body(w_ref, p_ref, r_ref, o_ref):
        acc = p_ref[...].astype(F32)
        for k in range(n):
            acc = acc + r_ref[k].astype(F32)
        o_ref[...] = acc

    return pl.pallas_call(
        body, name=name,
        grid_spec=pltpu.PrefetchScalarGridSpec(
            num_scalar_prefetch=1, grid=(nb,),
            in_specs=[pl.BlockSpec((None, tr, c), lambda i, w_ref: (w_ref[0], i, 0)),
                      pl.BlockSpec((n, tr, c), lambda i, w_ref: (0, i, 0))],
            out_specs=pl.BlockSpec((tr, c), lambda i, w_ref: (w_ref[1] * nb + i, 0))),
        out_shape=jax.ShapeDtypeStruct((2 * h, c), F32), compiler_params=_params("parallel"),
    )(where, parts, recv)


def place_shard(name, shard, where, dtype, tr):
    r, c = shard.shape

    def body(w_ref, s_ref, o_ref):
        o_ref[...] = s_ref[...].astype(dtype)

    return pl.pallas_call(
        body, name=name,
        grid_spec=pltpu.PrefetchScalarGridSpec(
            num_scalar_prefetch=1, grid=(r // tr,),
            in_specs=[pl.BlockSpec((tr, c), lambda i, w_ref: (i, 0))],
            out_specs=pl.BlockSpec((None, tr, c), lambda i, w_ref: (w_ref[0], i, 0))),
        out_shape=jax.ShapeDtypeStruct((N_CHIPS, r, c), dtype), compiler_params=_params("parallel"),
    )(where, shard)


ANY = pl.BlockSpec(memory_space=pl.ANY)


def _place():
    x, y, c = lax.axis_index("x"), lax.axis_index("y"), lax.axis_index("c")
    chips = [(1 - x, y), (x, 1 - y), (1 - x, 1 - y)]
    return x, y, c, chips


def gather_shards(arrays):
    n = len(arrays)

    def body(*refs):
        send, pass_on, finish = _gather_phases(refs[n:2 * n], *refs[2 * n:], _spans(arrays))
        send()
        pass_on()
        finish()

    return pl.pallas_call(
        body, name="gather_shards", in_specs=[ANY] * n, out_specs=[ANY] * n,
        out_shape=[jax.ShapeDtypeStruct(a.shape, a.dtype) for a in _arrays(arrays)],
        input_output_aliases={w: w for w in range(n)}, scratch_shapes=_gather_sems(n),
        compiler_params=pltpu.CompilerParams(has_side_effects=True),
    )(*_arrays(arrays))


def _gather_sems(n):
    return [pltpu.SemaphoreType.DMA((6 * n,)), pltpu.SemaphoreType.DMA((6 * n,))]


class Span(typing.NamedTuple):
    array: jax.Array
    lo: int
    hi: int
    ways: tuple = (0, 1, 2)


def _arrays(gathering):
    return [g.array if isinstance(g, Span) else g for g in gathering]


def _spans(gathering):
    return [(g.lo, g.hi, g.ways) if isinstance(g, Span) else (0, g.shape[1], (0, 1, 2)) for g in gathering]


def _gather_phases(out, send_sems, recv_sems, spans):
    n = len(out)
    if not any(ways for _, _, ways in spans):
        return (lambda: None,) * 3
    x, y, c, chips = _place()
    me = 2 * x + y
    sibling = (x, y, 1 - c)

    def half(w, chip, core):
        lo, hi, _ = spans[w]
        h = (hi - lo) // 2
        return out[w].at[chip, pl.ds(lo + core * h, h)]

    def copy(k, block, to):
        return pltpu.make_async_remote_copy(src_ref=block, dst_ref=block, send_sem=send_sems.at[k],
                                            recv_sem=recv_sems.at[k], device_id=to, device_id_type=MESH)

    def over_ici(w, j, chip):
        return copy(3 * w + j, half(w, chip, c), (chips[j][0], chips[j][1], c))

    def over_d2d(w, j, core):
        return copy(3 * n + 3 * w + j, half(w, 2 * chips[j][0] + chips[j][1], core), sibling)

    pairs = [(w, j) for w in range(n) for j in spans[w][2]]

    def send():
        for w, j in pairs:
            over_ici(w, j, me).start()

    def pass_on():
        for w, j in pairs:
            over_ici(w, j, 2 * chips[j][0] + chips[j][1]).wait_recv()
            over_d2d(w, j, c).start()

    def finish():
        for w, j in pairs:
            over_d2d(w, j, 1 - c).wait_recv()
        for w, j in pairs:
            over_ici(w, j, me).wait_send()
            over_d2d(w, j, c).wait_send()

    return send, pass_on, finish


def _relay_sems():
    return [pltpu.SemaphoreType.DMA((4,)), pltpu.SemaphoreType.DMA((4,))]


def _relay_phases(out, send_sems, recv_sems):
    x, y, c, chips = _place()
    sibling = (x, y, 1 - c)
    rows = out.shape[1]
    quarter = rows // 4
    far = 2 * chips[2][0] + chips[2][1]

    def piece(chip, way, core):
        return out.at[chip, pl.ds(way * (rows // 2) + core * quarter, quarter)]

    def copy(k, block, to):
        return pltpu.make_async_remote_copy(src_ref=block, dst_ref=block, send_sem=send_sems.at[k],
                                            recv_sem=recv_sems.at[k], device_id=to, device_id_type=MESH)

    def over_ici(way, chip):
        return copy(way, piece(chip, way, c), (chips[way][0], chips[way][1], c))

    def over_d2d(way, core):
        return copy(2 + way, piece(far, way, core), sibling)

    def send():
        for way in range(2):
            other = chips[1 - way]
            over_ici(way, 2 * other[0] + other[1]).start()

    def pass_on():
        for way in range(2):
            over_ici(way, far).wait_recv()
            over_d2d(way, c).start()

    def finish():
        for way in range(2):
            over_d2d(way, 1 - c).wait_recv()
        for way in range(2):
            other = chips[1 - way]
            over_ici(way, 2 * other[0] + other[1]).wait_send()
            over_d2d(way, c).wait_send()

    return send, pass_on, finish


def swap_halves(name, grads):
    n = len(grads)

    def body(*refs):
        send, finish = _swap_phases(refs[:n], refs[n:2 * n], *refs[2 * n:])
        send()
        finish()

    return pl.pallas_call(
        body, name=name, in_specs=[ANY] * n, out_specs=[ANY] * n, out_shape=_swap_shapes(grads),
        scratch_shapes=_swap_sems(n), compiler_params=pltpu.CompilerParams(has_side_effects=True),
    )(*grads)


def _swap_shapes(grads):
    return [jax.ShapeDtypeStruct((a.shape[0], a.shape[1] // 2, a.shape[2]), a.dtype) for a in grads]


def _swap_sems(n):
    return [pltpu.SemaphoreType.DMA((n,)), pltpu.SemaphoreType.DMA((n,))]


def _swap_phases(g, out, send_sems, recv_sems):
    x, y, c, _ = _place()

    def copies():
        return [pltpu.make_async_remote_copy(
            src_ref=g[w].at[:, pl.ds((1 - c) * (g[w].shape[1] // 2), g[w].shape[1] // 2)], dst_ref=out[w],
            send_sem=send_sems.at[w], recv_sem=recv_sems.at[w], device_id=(x, y, 1 - c), device_id_type=MESH)
            for w in range(len(g))]

    def send():
        for cp in copies():
            cp.start()

    def finish():
        for cp in copies():
            cp.wait()

    return send, finish


def _send_phases(g, out, send_sems, recv_sems):
    x, y, c, _ = _place()

    def copies():
        return [pltpu.make_async_remote_copy(
            src_ref=g[w], dst_ref=out[w], send_sem=send_sems.at[w], recv_sem=recv_sems.at[w],
            device_id=(x, y, 1 - c), device_id_type=MESH) for w in range(len(g))]

    def send():
        for cp in copies():
            cp.start()

    def finish():
        for cp in copies():
            cp.wait()

    return send, finish


def dw_in_half(name, h, dz, which, sending):
    s = h.shape[0]
    hh, tb = D // 2, IN_SHARD // 2
    n = len(sending)
    steps = IN_COLS // tb

    def body(w_ref, *refs):
        a_ref, b_ref, o_ref = refs[0], refs[1], refs[2 + n]
        j = pl.program_id(0)
        if n:
            send, finish = _send_phases(refs[2:2 + n], refs[3 + n:3 + 2 * n], *refs[3 + 2 * n:])
            pl.when(j == 0)(send)
        o_ref[...] = _dot(a_ref[...], b_ref[...], TN)
        if n:
            pl.when(j == steps - 1)(finish)

    out = pl.pallas_call(
        body, name=name,
        grid_spec=pltpu.PrefetchScalarGridSpec(
            num_scalar_prefetch=1, grid=(steps,),
            in_specs=[pl.BlockSpec((s, hh), lambda j, w: (0, w[0])), pl.BlockSpec((s, tb), lambda j, w: (0, j))]
            + [ANY] * n,
            out_specs=[pl.BlockSpec((None, hh, tb), lambda j, w: (j // 2, 0, j % 2))] + [ANY] * n,
            scratch_shapes=_swap_sems(n) if n else []),
        out_shape=[jax.ShapeDtypeStruct((N_CHIPS, hh, IN_SHARD), F32)]
        + [jax.ShapeDtypeStruct(a.shape, a.dtype) for a in sending],
        compiler_params=_params("arbitrary", communicates=bool(n)),
    )(which, h, dz, *sending)
    return out[0], out[1:]


def scatter_chips(parts):
    n = len(parts)

    def body(*refs):
        send, finish = _scatter_phases(refs[:n], refs[n:2 * n], *refs[2 * n:])
        send()
        finish()

    return pl.pallas_call(
        body, name="scatter_chips", in_specs=[ANY] * n, out_specs=[ANY] * n,
        out_shape=_scatter_shapes(parts), scratch_shapes=_scatter_sems(n),
        compiler_params=pltpu.CompilerParams(has_side_effects=True),
    )(*parts)


def _scatter_shapes(parts):
    return [jax.ShapeDtypeStruct((3,) + a.shape[1:], a.dtype) for a in parts]


def _scatter_sems(n):
    return [pltpu.SemaphoreType.DMA((3 * n,)), pltpu.SemaphoreType.DMA((3 * n,))]


def _scatter_phases(p, out, send_sems, recv_sems):
    x, y, c, chips = _place()

    def copies():
        return [pltpu.make_async_remote_copy(
            src_ref=p[w].at[2 * px + py], dst_ref=out[w].at[j], send_sem=send_sems.at[3 * w + j],
            recv_sem=recv_sems.at[3 * w + j], device_id=(px, py, c), device_id_type=MESH)
            for w in range(len(p)) for j, (px, py) in enumerate(chips)]

    def send():
        for cp in copies():
            cp.start()

    def finish():
        for cp in copies():
            cp.wait()

    return send, finish


def join_halves(arrays):
    n = len(arrays)

    def body(*refs):
        out = refs[n:2 * n]
        send_sems, recv_sems = refs[2 * n:]
        x, y, c, _ = _place()

        def copy(w, core):
            h = out[w].shape[0] // 2
            rows = out[w].at[pl.ds(core * h, h)]
            return pltpu.make_async_remote_copy(
                src_ref=rows, dst_ref=rows, send_sem=send_sems.at[w], recv_sem=recv_sems.at[w],
                device_id=(x, y, 1 - c), device_id_type=MESH)

        for w in range(n):
            copy(w, c).start()
        for w in range(n):
            copy(w, 1 - c).wait_recv()
        for w in range(n):
            copy(w, c).wait_send()

    return pl.pallas_call(
        body, name="join_halves", in_specs=[ANY] * n, out_specs=[ANY] * n,
        out_shape=[jax.ShapeDtypeStruct(a.shape, a.dtype) for a in arrays],
        input_output_aliases={w: w for w in range(n)},
        scratch_shapes=[pltpu.SemaphoreType.DMA((n,)), pltpu.SemaphoreType.DMA((n,))],
        compiler_params=pltpu.CompilerParams(has_side_effects=True),
    )(*arrays)


def allreduce_small(packed):
    r, c = packed.shape
    n_dev = 8

    def body(x_ref, all_ref, sum_ref, send_sems, recv_sems, local_sem):
        x, y, cc, chips = _place()
        me, sibling = (x, y, cc), (x, y, 1 - cc)

        def rows(px, py, pc):
            return all_ref.at[4 * px + 2 * py + pc]

        def copy(k, block, to, src=None):
            return pltpu.make_async_remote_copy(
                src_ref=rows(*block) if src is None else src, dst_ref=rows(*block), send_sem=send_sems.at[k],
                recv_sem=recv_sems.at[k], device_id=to, device_id_type=MESH)

        mine = pltpu.make_async_copy(x_ref, rows(*me), local_sem)
        mine.start()
        first = [copy(0, me, sibling, src=x_ref)]
        first += [copy(1 + j, me, (*chip, cc), src=x_ref) for j, chip in enumerate(chips)]
        for cp in first:
            cp.start()
        passed = [copy(4 + j, (*chip, cc), sibling) for j, chip in enumerate(chips)]
        for j, chip in enumerate(chips):
            copy(1 + j, (*chip, cc), me).wait_recv()
            passed[j].start()
        copy(0, sibling, me).wait_recv()
        for j, chip in enumerate(chips):
            copy(4 + j, (*chip, 1 - cc), me).wait_recv()
        for cp in first + passed:
            cp.wait_send()
        mine.wait()
        acc = all_ref[0]
        for k in range(1, n_dev):
            acc = acc + all_ref[k]
        sum_ref[...] = acc

    vm = pl.BlockSpec(memory_space=pltpu.VMEM)
    return pl.pallas_call(
        body, name="allreduce_small", in_specs=[vm], out_specs=[vm, vm],
        out_shape=[jax.ShapeDtypeStruct((n_dev, r, c), F32), jax.ShapeDtypeStruct((r, c), F32)],
        scratch_shapes=[pltpu.SemaphoreType.DMA((7,)), pltpu.SemaphoreType.DMA((7,)), pltpu.SemaphoreType.DMA],
        compiler_params=pltpu.CompilerParams(has_side_effects=True, vmem_limit_bytes=VMEM_LIMIT),
    )(packed)[1]


def local_step(x, target, vecs, w_s, bs_t, bg, wg_in, late, core=None, order=None, where=None):
    on_mesh = core is not None

    def add(names, grads, recv):
        return [add_halves("add_" + n, g, r, core, min(r.shape[1], 256)) for n, g, r in zip(names, grads, recv)]

    g_pre, ln_g, ln_b, g_post, g_fpre, g_fpost = vecs
    s = x.shape[0]
    if order is None:
        order = jnp.arange(N_CHIPS, dtype=jnp.int32)
    logc = _attn_tables(s)
    ka, kb = _alibi_tables(s)

    h = norm_pre(x, g_pre)
    if not on_mesh:
        wg_a, wg_b, wg_out, wg_ff1, wg_ff2 = late
    if on_mesh:
        cut = D // 4
        z, (wg_in,), (wg_a, wg_b, wg_out, wg_ff1, wg_ff2) = mm_in(
            "mm_in_own", h, wg_in, order, 0, 1, None, [Span(wg_in, 0, D, (0, 1))], casting=late)
        z, (wg_in,), _ = mm_in("mm_in_near", h, wg_in, order, 1, 2, z, [Span(wg_in, 0, D, ())], relay=True)
        z, (wg_in, wg_a), _ = mm_in("mm_in_far", h, wg_in, order, 3, 1, z, [Span(wg_in, 0, D, ()), wg_a])
        ya, (wg_b, wg_ff2) = gating_fwd(z, ln_g, ln_b, w_s, bs_t, [wg_b, Span(wg_ff2, 0, cut)])
        yb, lse, (wg_ff1, wg_ff2, bg) = attn_fwd(z, logc, ka, kb, [wg_ff1, Span(wg_ff2, cut, 3 * cut), bg])
        bg = jnp.transpose(bg[:, :2, :], (1, 0, 2)).reshape(2, D)
    else:
        z, _, _ = mm_in("mm_in", h, wg_in, order, 0, N_CHIPS, None, [Span(wg_in, 0, D, ())])
        ya, _ = gating_fwd(z, ln_g, ln_b, w_s, bs_t, [])
        yb, lse, _ = attn_fwd(z, logc, ka, kb, [])
    merged, pa, pb, got = proj_merge(ya, yb, wg_a.reshape(D, D), wg_b.reshape(D, D), z, bg, [wg_out] if on_mesh else [])
    wg_out = got[0] if on_mesh else wg_out
    w_out = wg_out.reshape(D, D)
    o, x1, h2, got = out_norm(merged, w_out, x, g_post, g_fpre, [Span(wg_ff2, 3 * D // 4, D)] if on_mesh else [])
    a, rl, _ = mm_ff1(h2, wg_ff1, [])
    w_ff2 = (got[0] if on_mesh else wg_ff2).reshape(D_FF, D)
    dy, df, d_gfpost, loss = ff2_loss(rl, w_ff2, x1, target, g_fpost)

    half_cols = pl.BlockSpec((D, D // 2), lambda i, j: (0, j))
    d_wff2 = mm_tn("dw_ff2", rl, df, D // 2, D, (D_FF, D), pl.BlockSpec((D // 2, D), lambda i, j: (i, 0)))
    da = ff2_bwd(df, w_ff2, a)
    d_wff1 = mm_tn("dw_ff1", h2, da, D, D // 2, (N_CHIPS, D, D),
                   pl.BlockSpec((None, D, D // 2), lambda i, j: (j // 2, 0, j % 2)))
    d_ff = [d_wff1, d_wff2.reshape(N_CHIPS, D, D)]
    dx1, do, d_gfpre, d_gpost, recv_ff = ff1_bwd_norms(da, wg_ff1, x1, o, dy, g_fpre, g_post, d_ff if on_mesh else [])
    d_wout = mm_tn("dw_out", merged, do, D, D // 2, (D, D), half_cols)
    dpa, dpb, dga, dgb, d_bg = out_bwd_gates(do, w_out, pa, pb, z, bg)
    d_wa = mm_tn("dw_a", ya, dpa, D, D // 2, (D, D), half_cols)
    d_wb = mm_tn("dw_b", yb, dpb, D, D // 2, (D, D), half_cols)
    dya = mm_nt("dy_a", dpa, wg_a.reshape(D, D))
    dyb = mm_nt("dy_b", dpb, wg_b.reshape(D, D))
    d_proj = [d_wa.reshape(N_CHIPS, D // N_CHIPS, D), d_wb.reshape(N_CHIPS, D // N_CHIPS, D),
              d_wout.reshape(N_CHIPS, D // N_CHIPS, D)]
    du, dv, d_ws, d_bs, d_lng, d_lnb, recv_proj = gating_bwd(z, dya, ln_g, ln_b, w_s, bs_t, d_proj if on_mesh else [])
    early = d_proj + d_ff
    parts_early = add(BIG[1:], early, list(recv_proj) + list(recv_ff)) if on_mesh else []
    dq, dk, dvb, got_early = attn_bwd(z, yb, dyb, lse, logc, ka, kb, parts_early)
    dz = jnp.concatenate([du, dv, dq, dk, dvb, dga, dgb], axis=1)
    if on_mesh:
        for_sibling, _ = dw_in_half("dw_in_sibling", h, dz, 1 - core, [])
        mine, from_sibling = dw_in_half("dw_in_mine", h, dz, core, [for_sibling])
        d_win = None
        parts_late = [add_halves("add_w_in", mine, from_sibling[0], jnp.zeros((1,), jnp.int32), 256)]
    else:
        half = IN_SHARD // 2
        d_win = mm_tn("dw_in", h, dz, D, half, (N_CHIPS, D, IN_SHARD),
                      pl.BlockSpec((None, D, half), lambda i, j: (j // 2, 0, j % 2)))
        parts_late = []
    small = dict(b_gate=d_bg, ln_v_g=d_lng, ln_v_b=d_lnb, w_s=d_ws, b_s=d_bs[:, 0, :],
                 norm_mix_post=d_gpost, norm_ffn_pre=d_gfpre, norm_ffn_post=d_gfpost)
    packed = pack_small(dict(small, norm_mix_pre=jnp.zeros((1, D), F32)), loss, where) if on_mesh else None
    dx, d_gpre, got_late, packed = in_bwd_norm(dz, wg_in, x, dx1, g_pre, parts_late, packed)
    small["norm_mix_pre"] = d_gpre
    return (loss[0, 0], dx, [d_win] + early, small, parts_late + parts_early, list(got_late) + list(got_early),
            packed)


BIG = ("w_in", "w_a_proj", "w_b_proj", "w_out", "w_ff1", "w_ff2")
SMALL = ("norm_mix_pre", "ln_v_g", "ln_v_b", "b_s", "norm_mix_post", "norm_ffn_pre", "norm_ffn_post", "w_s", "b_gate")
ORDER = ("norm_mix_pre", "w_in", "b_gate", "ln_v_g", "ln_v_b", "w_s", "b_s", "w_a_proj", "w_b_proj", "w_out",
         "norm_mix_post", "norm_ffn_pre", "w_ff1", "w_ff2", "norm_ffn_post")
VEC_ROWS = D // 128
WS_ROW = 7 * VEC_ROWS
BG_ROW = WS_ROW + GROUPS * CHUNK
LOSS_ROW = BG_ROW + 2 * VEC_ROWS
PACK_ROWS = LOSS_ROW + 8


def pack_small(small, loss, where):
    vectors = [small[n] for n in SMALL[:7]]
    operands = vectors + [small["w_s"], small["b_gate"], loss]

    def body(where_ref, *refs):
        out = refs[-1]
        ws_ref, bg_ref, loss_ref = refs[7:10]
        for i, n in enumerate(SMALL[:7]):
            if n == "b_s":
                out[i * VEC_ROWS:(i + 1) * VEC_ROWS, :] = refs[i][...]
            else:
                for j in range(VEC_ROWS):
                    out[i * VEC_ROWS + j:i * VEC_ROWS + j + 1, :] = refs[i][:, j * 128:(j + 1) * 128]
        for g in range(GROUPS):
            out[WS_ROW + g * CHUNK:WS_ROW + (g + 1) * CHUNK, :] = ws_ref[g]
        for r in range(2):
            for j in range(VEC_ROWS):
                row = BG_ROW + r * VEC_ROWS + j
                out[row:row + 1, :] = bg_ref[r:r + 1, j * 128:(j + 1) * 128]
        lane = lax.broadcasted_iota(jnp.int32, (8, 128), 1)
        sub = lax.broadcasted_iota(jnp.int32, (8, 128), 0)
        out[LOSS_ROW:LOSS_ROW + 8, :] = jnp.where((lane == 0) & (sub == 0), loss_ref[...], 0.0)

    return pl.pallas_call(
        body, name="pack_small",
        grid_spec=pltpu.PrefetchScalarGridSpec(
            num_scalar_prefetch=1, grid=(1,), in_specs=[_full(a.shape) for a in operands],
            out_specs=pl.BlockSpec((None, PACK_ROWS, 128), lambda i, w: (w[0], w[1], 0))),
        out_shape=jax.ShapeDtypeStruct((N_CHIPS, 2 * PACK_ROWS, 128), F32), compiler_params=_params("arbitrary"),
    )(where, *operands)


def pack_vector(vec, where):
    def body(where_ref, v_ref, out):
        for j in range(VEC_ROWS):
            out[j:j + 1, :] = v_ref[:, j * 128:(j + 1) * 128]

    return pl.pallas_call(
        body, name="pack_vector",
        grid_spec=pltpu.PrefetchScalarGridSpec(
            num_scalar_prefetch=1, grid=(1,), in_specs=[_full(vec.shape)],
            out_specs=pl.BlockSpec((None, VEC_ROWS, 128), lambda i, w: (w[0], w[1], 0))),
        out_shape=jax.ShapeDtypeStruct((N_CHIPS, 2 * VEC_ROWS, 128), F32), compiler_params=_params("arbitrary"),
    )(where, vec)


def adamw_small(gathered, first, chip, w, m, v):
    shapes = {n: (1, D) for n in SMALL}
    shapes.update(b_s=(GROUPS, CHUNK), w_s=(GROUPS * CHUNK, CHUNK), b_gate=(2, D // N_CHIPS))
    flat = lambda t: [t[n].reshape(shapes[n]) for n in SMALL]
    per = D // N_CHIPS // 128

    def body(chip_ref, all_ref, first_ref, *refs):
        params, outs = refs[:27], refs[27:]
        sub = lax.broadcasted_iota(jnp.int32, (VEC_ROWS, 128), 0)
        sum_ref = outs[36]
        total = all_ref[0, 0:PACK_ROWS, :]
        head = first_ref[0, 0:VEC_ROWS, :]
        for k in range(1, 2 * N_CHIPS):
            total = total + all_ref[k // 2, (k % 2) * PACK_ROWS:(k % 2 + 1) * PACK_ROWS, :]
            head = head + first_ref[k // 2, (k % 2) * VEC_ROWS:(k % 2 + 1) * VEC_ROWS, :]
        sum_ref[...] = total
        sum_ref[0:VEC_ROWS, :] = head

        def gate_row(r):
            rows = sum_ref[BG_ROW + r * VEC_ROWS:BG_ROW + (r + 1) * VEC_ROWS, :]
            return jnp.concatenate([jnp.sum(jnp.where(sub == per * chip_ref[0] + j, rows, 0.0), axis=0, keepdims=True)
                                    for j in range(per)], axis=1)

        for i, n in enumerate(SMALL):
            if n == "b_s":
                g = sum_ref[i * VEC_ROWS:(i + 1) * VEC_ROWS, :]
            elif n == "w_s":
                g = sum_ref[WS_ROW:BG_ROW, :]
            elif n == "b_gate":
                g = jnp.concatenate([gate_row(0), gate_row(1)], axis=0)
            else:
                g = jnp.concatenate([sum_ref[i * VEC_ROWS + j:i * VEC_ROWS + j + 1, :] for j in range(VEC_ROWS)],
                                    axis=1)
            delta, nm, nv = _adamw_math(params[i][...], g, params[9 + i][...], params[18 + i][...])
            outs[4 * i][...], outs[4 * i + 1][...], outs[4 * i + 2][...], outs[4 * i + 3][...] = g, delta, nm, nv

    vm = pl.BlockSpec(memory_space=pltpu.VMEM)
    res = pl.pallas_call(
        body, name="adamw_small",
        in_specs=[pl.BlockSpec(memory_space=pltpu.SMEM)] + [vm] * 29, out_specs=[vm] * 37,
        out_shape=[jax.ShapeDtypeStruct(shapes[n], F32) for n in SMALL for _ in range(4)]
        + [jax.ShapeDtypeStruct((PACK_ROWS, 128), F32)],
        compiler_params=_params(),
    )(chip, gathered, first, *flat(w), *flat(m), *flat(v))
    new = {n: tuple(r.reshape(w[n].shape) for r in res[4 * i:4 * i + 4]) for i, n in enumerate(SMALL)}
    return new, res[36][LOSS_ROW, 0]


def kernel(x, norm_mix_pre, w_in, b_gate, ln_v_g, ln_v_b, w_s, b_s, w_a_proj, w_b_proj, w_out, norm_mix_post, norm_ffn_pre, w_ff1, w_ff2, norm_ffn_post, loss_target, m_norm_mix_pre, m_w_in, m_b_gate, m_ln_v_g, m_ln_v_b, m_w_s, m_b_s, m_w_a_proj, m_w_b_proj, m_w_out, m_norm_mix_post, m_norm_ffn_pre, m_w_ff1, m_w_ff2, m_norm_ffn_post, v_norm_mix_pre, v_w_in, v_b_gate, v_ln_v_g, v_ln_v_b, v_w_s, v_b_s, v_w_a_proj, v_w_b_proj, v_w_out, v_norm_mix_post, v_norm_ffn_pre, v_w_ff1, v_w_ff2, v_norm_ffn_post):
    w = dict(norm_mix_pre=norm_mix_pre, w_in=w_in, b_gate=b_gate, ln_v_g=ln_v_g, ln_v_b=ln_v_b, w_s=w_s, b_s=b_s,
             w_a_proj=w_a_proj, w_b_proj=w_b_proj, w_out=w_out, norm_mix_post=norm_mix_post,
             norm_ffn_pre=norm_ffn_pre, w_ff1=w_ff1, w_ff2=w_ff2, norm_ffn_post=norm_ffn_post)
    m = dict(norm_mix_pre=m_norm_mix_pre, w_in=m_w_in, b_gate=m_b_gate, ln_v_g=m_ln_v_g, ln_v_b=m_ln_v_b, w_s=m_w_s,
             b_s=m_b_s, w_a_proj=m_w_a_proj, w_b_proj=m_w_b_proj, w_out=m_w_out, norm_mix_post=m_norm_mix_post,
             norm_ffn_pre=m_norm_ffn_pre, w_ff1=m_w_ff1, w_ff2=m_w_ff2, norm_ffn_post=m_norm_ffn_post)
    v = dict(norm_mix_pre=v_norm_mix_pre, w_in=v_w_in, b_gate=v_b_gate, ln_v_g=v_ln_v_g, ln_v_b=v_ln_v_b, w_s=v_w_s,
             b_s=v_b_s, w_a_proj=v_w_a_proj, w_b_proj=v_w_b_proj, w_out=v_w_out, norm_mix_post=v_norm_mix_post,
             norm_ffn_pre=v_norm_ffn_pre, w_ff1=v_w_ff1, w_ff2=v_w_ff2, norm_ffn_post=v_norm_ffn_post)
    chip = 2 * lax.axis_index("x") + lax.axis_index("y")
    core = lax.axis_index("c")

    where = jnp.stack([chip, core]).astype(jnp.int32)
    wg_in = place_shard("place_w_in", w_in[0], where, BF16, 256)
    bg_all = place_shard("place_b_gate", jnp.pad(b_gate[0], ((0, 14), (0, 0))), where, F32, 16)
    vecs = (norm_mix_pre, ln_v_g, ln_v_b, norm_mix_post, norm_ffn_pre, norm_ffn_post)
    loss, dx, _, small, parts, got, packed = local_step(
        x[0], loss_target[0], vecs, w_s[0], b_s[0].T, bg_all, wg_in, [w[n][0] for n in BIG[1:]],
        core=jnp.reshape(core, (1,)).astype(jnp.int32),
        order=jnp.stack([chip, chip ^ 2, chip ^ 1, chip ^ 3]).astype(jnp.int32), where=where)

    halves = [sum_chips("sum_" + n, p, r, where, min(p.shape[1], 256)) for n, p, r in zip(BIG, parts, got)]
    grads = dict(zip(BIG, join_halves(halves)))

    first = pack_vector(small["norm_mix_pre"], where)
    new = {}
    for n in BIG:
        shape = w[n].shape
        res = adamw("adamw_" + n, w[n][0], grads[n], m[n][0], v[n][0], min(shape[1], 256),
                    first if n == "w_in" else None)
        new[n] = tuple(r.reshape(shape) for r in res[:4])
        first = res[4] if n == "w_in" else first
    small_new, loss = adamw_small(packed, first, jnp.reshape(chip, (1,)).astype(jnp.int32), w, m, v)
    new.update(small_new)

    outs = [loss, dx[None]]
    for i in range(4):
        outs += [new[n][i] for n in ORDER]
    return tuple(outs)
```

```python
import functools
import math
import typing

import numpy as np
import jax
import jax.numpy as jnp
from jax import lax
from jax.experimental import pallas as pl
from jax.experimental.pallas import tpu as pltpu

F32 = jnp.float32
BF16 = jnp.bfloat16
MESH = pl.DeviceIdType.MESH

D = 1024
EPS = 1e-6
CHUNK = 128
GROUPS = 8
HEADS = 16
HEAD_DIM = 64
ATT_T = 256
ATT_GROUP = 8
ATT_BWD_GROUP = 2
N_CHIPS = 4
D_FF = 4 * D
IN_COLS = 7 * D
IN_SHARD = IN_COLS // N_CHIPS
MASKED = -1e30
VMEM_LIMIT = 56 * 2 ** 20

ADAM_LR, ADAM_B1, ADAM_B2, ADAM_EPS, ADAM_WD, ADAM_STEP = 0.001, 0.9, 0.999, 1e-08, 0.01, 10

NN = (((1,), (0,)), ((), ()))
NT = (((1,), (1,)), ((), ()))
TN = (((0,), (0,)), ((), ()))


def _dot(a, b, dims=NN):
    return lax.dot_general(a, b, dims, preferred_element_type=F32)


def _params(*sem, communicates=False):
    return pltpu.CompilerParams(dimension_semantics=sem or None, vmem_limit_bytes=VMEM_LIMIT,
                                has_side_effects=communicates)


def _rows(tr, c, col=0):
    return pl.BlockSpec((tr, c), lambda i: (i, col))


def _full(shape):
    n = len(shape)
    return pl.BlockSpec(shape, lambda *_: (0,) * n)


def _gelu(x):
    k = math.sqrt(2.0 / math.pi)
    return 0.5 * x * (1.0 + jnp.tanh(k * (x + 0.044715 * x * x * x)))


def _gelu_and_grad(x):
    k = math.sqrt(2.0 / math.pi)
    t = jnp.tanh(k * (x + 0.044715 * x * x * x))
    g = 0.5 * x * (1.0 + t)
    dg = 0.5 * (1.0 + t) + 0.5 * x * (1.0 - t * t) * (k * (1.0 + 3.0 * 0.044715 * x * x))
    return g, dg


def _sigmoid(x):
    return 1.0 / (1.0 + jnp.exp(-x))


def _rms(x):
    r = lax.rsqrt(jnp.mean(x * x, axis=-1, keepdims=True) + EPS)
    return x * r, r


def _rms_bwd(dn, xhat, r):
    return r * (dn - xhat * jnp.mean(dn * xhat, axis=-1, keepdims=True))


def norm_pre(x, g):
    s = x.shape[0]
    tr = 512

    def body(x_ref, g_ref, h_ref):
        xhat, _ = _rms(x_ref[...])
        h_ref[...] = (xhat * g_ref[...]).astype(BF16)

    return pl.pallas_call(
        body, name="norm_pre", grid=(s // tr,),
        in_specs=[_rows(tr, D), _full((1, D))], out_specs=_rows(tr, D),
        out_shape=jax.ShapeDtypeStruct((s, D), BF16), compiler_params=_params("parallel"),
    )(x, g)


def mm_in(name, h, wg, order, first, count, z, gathering, relay=False, casting=()):
    s = h.shape[0]
    tm, tn = 1024, IN_SHARD // 2
    per = IN_SHARD // tn
    n, m = len(gathering), len(casting)
    nj, ni = count * per, s // tm
    has_z = z is not None
    arrays = _arrays(gathering)
    at = [k for k, a in enumerate(arrays) if a is wg][0]

    def body(order_ref, *refs):
        a_ref = refs[0]
        cast_in = refs[1 + has_z + n:1 + has_z + n + m]
        o_ref = refs[1 + has_z + n + m]
        held = refs[2 + has_z + n + m:2 + has_z + 2 * n + m]
        cast_out = refs[2 + has_z + 2 * n + m:2 + has_z + 2 * n + 2 * m]
        tile, tile_sem = refs[2 + has_z + 2 * n + 2 * m:4 + has_z + 2 * n + 2 * m]
        j, i = pl.program_id(0), pl.program_id(1)
        sems = refs[4 + has_z + 2 * n + 2 * m:]
        for src, dst in zip(cast_in, cast_out):
            dst[...] = src[...].astype(BF16)
        phases = [_gather_phases(held, *sems[:2], _spans(gathering))]
        if relay:
            phases.append(_relay_phases(held[at], *sems[2:]))
        def each(fs):
            def run():
                for f in fs:
                    f()
            return run

        send, pass_on, finish = [each(fs) for fs in zip(*phases)]

        def fetch(t):
            chip = order_ref[first + t // per]
            return pltpu.make_async_copy(held[at].at[chip, :, pl.ds((t % per) * tn, tn)], tile.at[t % 2],
                                         tile_sem.at[t % 2])

        @pl.when(i == 0)
        def _():
            @pl.when(j == 0)
            def _():
                send()
                fetch(0).start()

            fetch(j).wait()

            @pl.when(j + 1 < nj)
            def _():
                fetch(j + 1).start()

        pl.when((j == nj - 1) & (i == ni - 1))(pass_on)
        rows = pl.ds(pl.multiple_of(i * tm, tm), tm)
        o_ref[...] = _dot(a_ref[rows, :], tile[j % 2]).astype(BF16)
        pl.when((j == nj - 1) & (i == ni - 1))(finish)

    steps = nj * ni
    out = pl.pallas_call(
        body, name=name,
        grid_spec=pltpu.PrefetchScalarGridSpec(
            num_scalar_prefetch=1, grid=(nj, ni),
            in_specs=[pl.BlockSpec((s, D), lambda j, i, o: (0, 0))] + [ANY] * (has_z + n)
            + [pl.BlockSpec((a.shape[0] // steps, a.shape[1]), lambda j, i, o: (j * ni + i, 0)) for a in casting],
            out_specs=[pl.BlockSpec((tm, tn), lambda j, i, o: (i, o[first + j // per] * per + j % per))] + [ANY] * n
            + [pl.BlockSpec((None, a.shape[0] // steps, a.shape[1]), lambda j, i, o: (o[0], j * ni + i, 0))
               for a in casting],
            scratch_shapes=[pltpu.VMEM((2, D, tn), BF16), pltpu.SemaphoreType.DMA((2,))] + _gather_sems(n)
            + (_relay_sems() if relay else [])),
        out_shape=[jax.ShapeDtypeStruct((s, IN_COLS), BF16)] + [jax.ShapeDtypeStruct(a.shape, a.dtype) for a in arrays]
        + [jax.ShapeDtypeStruct((N_CHIPS,) + a.shape, BF16) for a in casting],
        input_output_aliases={**({2: 0} if has_z else {}), **{2 + has_z + w: 1 + w for w in range(n)}},
        compiler_params=_params("arbitrary", "arbitrary", communicates=True),
    )(order, h, *([z] if has_z else []), *arrays, *casting)
    return out[0], out[1:1 + n], out[1 + n:]


def _tril_ws(ws_ref, g):
    r = lax.broadcasted_iota(jnp.int32, (CHUNK, CHUNK), 0)
    c = lax.broadcasted_iota(jnp.int32, (CHUNK, CHUNK), 1)
    return jnp.where(c <= r, ws_ref[g], 0.0).astype(BF16)


def _layer_norm(v):
    mu = jnp.mean(v, axis=-1, keepdims=True)
    d = v - mu
    rstd = lax.rsqrt(jnp.mean(d * d, axis=-1, keepdims=True) + EPS)
    return d * rstd, rstd


def gating_fwd(z, ln_g, ln_b, w_s, bs_t, gathering):
    s = z.shape[0]
    n = len(gathering)
    steps = s // CHUNK

    def body(*refs):
        u_ref, v_ref, lg_ref, lb_ref, ws_ref, bst_ref = refs[:6]
        ya_ref = refs[6 + n]
        ci = pl.program_id(0)
        if n:
            send, pass_on, finish = _gather_phases(refs[7 + n:7 + 2 * n], *refs[7 + 2 * n:], _spans(gathering))
            pl.when(ci == 0)(send)
            pl.when(ci == steps - 1)(pass_on)
        ug = _gelu(u_ref[...].astype(F32))
        vhat, _ = _layer_norm(_gelu(v_ref[...].astype(F32)))
        vn = (vhat * lg_ref[...] + lb_ref[...]).astype(BF16)
        for g in range(GROUPS):
            cols = slice(g * CHUNK, (g + 1) * CHUNK)
            mixed = _dot(_tril_ws(ws_ref, g), vn[:, cols]) + bst_ref[:, g:g + 1]
            ya_ref[:, cols] = (ug[:, cols] * mixed).astype(BF16)
        if n:
            pl.when(ci == steps - 1)(finish)

    out = pl.pallas_call(
        body, name="gating_fwd", grid=(steps,),
        in_specs=[_rows(CHUNK, D, 0), _rows(CHUNK, D, 1), _full((1, D)), _full((1, D)),
                  _full((GROUPS, CHUNK, CHUNK)), _full((CHUNK, GROUPS))] + [ANY] * n,
        out_specs=[_rows(CHUNK, D)] + [ANY] * n,
        out_shape=[jax.ShapeDtypeStruct((s, D), BF16)]
        + [jax.ShapeDtypeStruct(a.shape, a.dtype) for a in _arrays(gathering)],
        input_output_aliases={6 + w: 1 + w for w in range(n)},
        scratch_shapes=_gather_sems(n) if n else [],
        compiler_params=_params("arbitrary", communicates=bool(n)),
    )(z, z, ln_g, ln_b, w_s, bs_t, *_arrays(gathering))
    return out[0], out[1:]


def _attn_tables(s):
    nd = s // ATT_T
    r = np.arange(ATT_T)[None, :, None]
    c = np.arange(ATT_T)[None, None, :]
    delta = np.arange(nd)[:, None, None] * ATT_T + r - c
    count = np.zeros(delta.shape, np.int64)
    for window, dilation in ((128, 1), (512, 4), (2048, 16)):
        count += (delta >= 0) & (delta % dilation == 0) & (delta <= window)
    logc = np.where(count > 0, np.log(np.maximum(count, 1)), MASKED)
    return jnp.asarray(logc, F32)


AUG = 3


def _split3_np(x):
    terms, rest = [], np.asarray(x, np.float64)
    for _ in range(AUG):
        term = np.asarray(rest.astype(jnp.bfloat16), np.float64)
        terms.append(term)
        rest = rest - term
    return terms


def _split3(x):
    terms, rest = [], x
    for _ in range(AUG):
        term = rest.astype(BF16).astype(F32)
        terms.append(term)
        rest = rest - term
    return terms


def _alibi_tables(s):
    nb = s // ATT_T
    slopes = np.exp2(-8.0 * np.arange(1, HEADS + 1, dtype=np.float64) / HEADS)
    ka = np.zeros((HEADS // 2, 2, ATT_T, 128), np.float32)
    kb = np.zeros((HEADS // 2, 2, nb, 128), np.float32)
    for p in range(HEADS // 2):
        for e in range(2):
            base = HEAD_DIM * (1 - e)
            for a, term in enumerate(_split3_np(slopes[2 * p + e] * np.arange(ATT_T))):
                ka[p, e, :, base + a] = term
            for a, term in enumerate(_split3_np(slopes[2 * p + e] * ATT_T * np.arange(nb))):
                kb[p, e, :, base + AUG + a] = term
            ka[p, e, :, base + 2 * AUG:base + 3 * AUG] = 1.0
    return jnp.asarray(ka), jnp.asarray(kb)


def _head_masks():
    lane = lax.broadcasted_iota(jnp.int32, (1, 128), 1)
    first = lane < HEAD_DIM

    def ones(e, n):
        base = HEAD_DIM * (1 - e)
        return ((lane >= base) & (lane < base + n)).astype(F32)

    return first, lane, ones


def _place3(lane, at, terms, other):
    for a, term in enumerate(terms):
        other = jnp.where(lane == at + a, term, other)
    return other


def attn_fwd(z, logc, ka, kb, gathering):
    s = z.shape[0]
    nq = s // ATT_T
    t = ATT_T
    n = len(gathering)
    grp = ATT_GROUP
    ngrp = HEADS // 2 // grp
    wide = 128 * grp
    qcol, kcol, vcol = 2 * D // wide, 3 * D // wide, 4 * D // wide

    def body(*refs):
        q_ref, k_ref, v_ref, lc_ref, ka_ref, kb_ref = refs[:6]
        y_ref, lse_ref = refs[6 + n:8 + n]
        q_s, k_s, v_s, m_s, l_s, acc_s = refs[8 + 2 * n:14 + 2 * n]
        gi, qi = pl.program_id(0), pl.program_id(1)
        first, lane, ones = _head_masks()
        if n:
            send, pass_on, finish = _gather_phases(refs[8 + n:8 + 2 * n], *refs[14 + 2 * n:], _spans(gathering))
            pl.when((gi == 0) & (qi == 0))(send)
            pl.when((gi == ngrp - 1) & (qi == nq - 1))(pass_on)

        @pl.when(qi == 0)
        def _():
            sel = jnp.broadcast_to(first.astype(F32), (t, 128))
            for pr in range(grp):
                cols = slice(pr * 128, (pr + 1) * 128)
                for jb in range(nq):
                    kj = k_ref[jb * t:(jb + 1) * t, cols].astype(F32)
                    vj = v_ref[jb * t:(jb + 1) * t, cols].astype(F32)
                    k_s[pr, 0, jb] = jnp.where(first, kj, ka_ref[pr, 0] + kb_ref[pr, 0, jb:jb + 1, :]).astype(BF16)
                    k_s[pr, 1, jb] = jnp.where(first, ka_ref[pr, 1] + kb_ref[pr, 1, jb:jb + 1, :], kj).astype(BF16)
                    v_s[pr, jb, 0:t, 0:128] = jnp.where(first, vj, 0.0).astype(BF16)
                    v_s[pr, jb, t:2 * t, 0:128] = jnp.where(first, 0.0, vj).astype(BF16)
                    v_s[pr, jb, 0:t, 128:256] = sel.astype(BF16)
                    v_s[pr, jb, t:2 * t, 128:256] = (1.0 - sel).astype(BF16)

        for pr in range(grp):
            q = q_ref[:, pr * 128:(pr + 1) * 128].astype(F32) * (1.0 / math.sqrt(HEAD_DIM))
            q_s[pr, 0] = jnp.where(first, q, ones(0, 2 * AUG)).astype(BF16)
            q_s[pr, 1] = jnp.where(first, ones(1, 2 * AUG), q).astype(BF16)
        m_s[...] = jnp.full_like(m_s, MASKED)
        l_s[...] = jnp.zeros_like(l_s)
        acc_s[...] = jnp.zeros_like(acc_s)

        def scores(j):
            return tuple(_dot(q_s[pr, e], k_s[pr, e, j], NT) for pr in range(grp) for e in range(2))

        def step(j, carry):
            softmax_block(j, scores(j))
            return carry

        def softmax_block(j, u):
            lc = lc_ref[qi - j]
            for pr in range(grp):
                u0 = u[2 * pr] + lc
                u1 = u[2 * pr + 1] + lc
                m0, m1 = m_s[pr, 0], m_s[pr, 1]
                n0 = jnp.maximum(m0, jnp.max(u0, axis=-1, keepdims=True))
                n1 = jnp.maximum(m1, jnp.max(u1, axis=-1, keepdims=True))
                m_s[pr, 0], m_s[pr, 1] = n0, n1
                p = jnp.concatenate([jnp.exp(u0 - jnp.concatenate([n0, n0], axis=1)).astype(BF16),
                                     jnp.exp(u1 - jnp.concatenate([n1, n1], axis=1)).astype(BF16)], axis=1)
                pv = _dot(p, v_s[pr, j])
                alpha = jnp.where(first, jnp.exp(m0 - n0), jnp.exp(m1 - n1))
                acc_s[pr] = acc_s[pr] * alpha + pv[:, 0:128]
                l_s[pr] = l_s[pr] * alpha + pv[:, 128:256]

        lax.fori_loop(0, qi + 1, step, 0)
        for pr in range(grp):
            cols = slice(pr * 128, (pr + 1) * 128)
            y_ref[:, cols] = (acc_s[pr] / l_s[pr]).astype(BF16)
            lse_ref[:, cols] = jnp.where(first, m_s[pr, 0], m_s[pr, 1]) + jnp.log(l_s[pr])
        if n:
            pl.when((gi == ngrp - 1) & (qi == nq - 1))(finish)

    out = pl.pallas_call(
        body, name="attn_fwd", grid=(ngrp, nq),
        in_specs=[pl.BlockSpec((t, wide), lambda g, i: (i, qcol + g)),
                  pl.BlockSpec((s, wide), lambda g, i: (0, kcol + g)),
                  pl.BlockSpec((s, wide), lambda g, i: (0, vcol + g)),
                  _full((nq, t, t)),
                  pl.BlockSpec((grp, 2, t, 128), lambda g, i: (g, 0, 0, 0)),
                  pl.BlockSpec((grp, 2, nq, 128), lambda g, i: (g, 0, 0, 0))] + [ANY] * n,
        out_specs=[pl.BlockSpec((t, wide), lambda g, i: (i, g)), pl.BlockSpec((t, wide), lambda g, i: (i, g))]
        + [ANY] * n,
        out_shape=[jax.ShapeDtypeStruct((s, D), BF16), jax.ShapeDtypeStruct((s, D), F32)]
        + [jax.ShapeDtypeStruct(a.shape, a.dtype) for a in _arrays(gathering)],
        input_output_aliases={6 + w: 2 + w for w in range(n)},
        scratch_shapes=[pltpu.VMEM((grp, 2, t, 128), BF16), pltpu.VMEM((grp, 2, nq, t, 128), BF16),
                        pltpu.VMEM((grp, nq, 2 * t, 256), BF16), pltpu.VMEM((grp, 2, t, 128), F32),
                        pltpu.VMEM((grp, t, 128), F32), pltpu.VMEM((grp, t, 128), F32)]
        + (_gather_sems(n) if n else []),
        compiler_params=_params("arbitrary", "arbitrary", communicates=bool(n)),
    )(z, z, z, logc, ka, kb, *_arrays(gathering))
    return out[0], out[1], out[2:]


def proj_merge(ya, yb, wa, wb, z, bg, gathering):
    s = ya.shape[0]
    tm = 512
    n = len(gathering)
    steps = s // tm

    def body(*refs):
        ya_ref, yb_ref, wa_ref, wb_ref, ga_ref, gb_ref, bg_ref = refs[:7]
        mg_ref, pa_ref, pb_ref = refs[7 + n:10 + n]
        i = pl.program_id(0)
        if n:
            send, pass_on, finish = _gather_phases(refs[10 + n:10 + 2 * n], *refs[10 + 2 * n:], _spans(gathering))
            pl.when(i == 0)(send)
            pl.when(i == steps - 1)(pass_on)
        pa = _dot(ya_ref[...], wa_ref[...])
        pb = _dot(yb_ref[...], wb_ref[...])
        sa = _sigmoid(ga_ref[...] + bg_ref[0:1, :])
        sb = _sigmoid(gb_ref[...] + bg_ref[1:2, :])
        mg_ref[...] = (sa * pa + sb * pb).astype(BF16)
        pa_ref[...] = pa.astype(BF16)
        pb_ref[...] = pb.astype(BF16)
        if n:
            pl.when(i == steps - 1)(finish)

    out = jax.ShapeDtypeStruct((s, D), BF16)
    res = pl.pallas_call(
        body, name="proj_merge", grid=(steps,),
        in_specs=[_rows(tm, D), _rows(tm, D), _full((D, D)), _full((D, D)),
                  _rows(tm, D, 5), _rows(tm, D, 6), _full((2, D))] + [ANY] * n,
        out_specs=[_rows(tm, D)] * 3 + [ANY] * n,
        out_shape=[out] * 3 + [jax.ShapeDtypeStruct(a.shape, a.dtype) for a in _arrays(gathering)],
        input_output_aliases={7 + w: 3 + w for w in range(n)},
        scratch_shapes=_gather_sems(n) if n else [],
        compiler_params=_params("arbitrary", communicates=bool(n)),
    )(ya, yb, wa, wb, z, z, bg, *_arrays(gathering))
    return res[0], res[1], res[2], res[3:]


def out_norm(merged, w_out, x, g_post, g_fpre, gathering):
    s = x.shape[0]
    tm = 512
    n = len(gathering)
    steps = s // tm

    def body(*refs):
        mg_ref, w_ref, x_ref, gp_ref, gf_ref = refs[:5]
        o_ref, x1_ref, h2_ref = refs[5 + n:8 + n]
        i = pl.program_id(0)
        if n:
            send, pass_on, finish = _gather_phases(refs[8 + n:8 + 2 * n], *refs[8 + 2 * n:], _spans(gathering))
            pl.when(i == 0)(send)
            pl.when(i == steps - 1)(pass_on)
        o = _dot(mg_ref[...], w_ref[...])
        ohat, _ = _rms(o)
        x1 = x_ref[...] + ohat * gp_ref[...]
        x1hat, _ = _rms(x1)
        o_ref[...] = o
        x1_ref[...] = x1
        h2_ref[...] = (x1hat * gf_ref[...]).astype(BF16)
        if n:
            pl.when(i == steps - 1)(finish)

    res = pl.pallas_call(
        body, name="out_norm", grid=(steps,),
        in_specs=[_rows(tm, D), _full((D, D)), _rows(tm, D), _full((1, D)), _full((1, D))] + [ANY] * n,
        out_specs=[_rows(tm, D)] * 3 + [ANY] * n,
        out_shape=[jax.ShapeDtypeStruct((s, D), F32), jax.ShapeDtypeStruct((s, D), F32),
                   jax.ShapeDtypeStruct((s, D), BF16)]
        + [jax.ShapeDtypeStruct(a.shape, a.dtype) for a in _arrays(gathering)],
        input_output_aliases={5 + w: 3 + w for w in range(n)},
        scratch_shapes=_gather_sems(n) if n else [],
        compiler_params=_params("arbitrary", communicates=bool(n)),
    )(merged, w_out, x, g_post, g_fpre, *_arrays(gathering))
    return res[0], res[1], res[2], res[3:]


def mm_ff1(h2, wg, gathering):
    s = h2.shape[0]
    tm = 1024
    n = len(gathering)
    ni = s // tm

    def body(*refs):
        a_ref, b_ref = refs[:2]
        o_ref, r_ref = refs[2 + n:4 + n]
        i, j = pl.program_id(0), pl.program_id(1)
        if n:
            send, pass_on, finish = _gather_phases(refs[4 + n:4 + 2 * n], *refs[4 + 2 * n:], _spans(gathering))
            pl.when((i == 0) & (j == 0))(send)
            pl.when((i == ni - 1) & (j == N_CHIPS // 2))(pass_on)
        a = _dot(a_ref[...], b_ref[...])
        o_ref[...] = a.astype(BF16)
        r = jnp.maximum(a, 0.0)
        r_ref[...] = (r * r).astype(BF16)
        if n:
            pl.when((i == ni - 1) & (j == N_CHIPS - 1))(finish)

    res = pl.pallas_call(
        body, name="mm_ff1", grid=(ni, N_CHIPS),
        in_specs=[pl.BlockSpec((tm, D), lambda i, j: (i, 0)), pl.BlockSpec((None, D, D), lambda i, j: (j, 0, 0))]
        + [ANY] * n,
        out_specs=[pl.BlockSpec((tm, D), lambda i, j: (i, j))] * 2 + [ANY] * n,
        out_shape=[jax.ShapeDtypeStruct((s, D_FF), BF16), jax.ShapeDtypeStruct((s, D_FF), BF16)]
        + [jax.ShapeDtypeStruct(a.shape, a.dtype) for a in _arrays(gathering)],
        input_output_aliases={2 + w: 2 + w for w in range(n)},
        scratch_shapes=_gather_sems(n) if n else [],
        compiler_params=_params("arbitrary", "arbitrary", communicates=bool(n)),
    )(h2, wg, *_arrays(gathering))
    return res[0], res[1], res[2:]


def ff2_loss(rl, w_ff2, x1, target, g_fpost):
    s = x1.shape[0]
    tm = 256

    def body(rl_ref, w_ref, x1_ref, t_ref, g_ref, dy_ref, df_ref, dg_ref, loss_ref):
        @pl.when(pl.program_id(0) == 0)
        def _():
            dg_ref[...] = jnp.zeros_like(dg_ref)
            loss_ref[...] = jnp.zeros_like(loss_ref)

        f = _dot(rl_ref[...], w_ref[...])
        fhat, r = _rms(f)
        err = x1_ref[...] + fhat * g_ref[...] - t_ref[...]
        loss_ref[...] += 0.5 * jnp.sum(jnp.mean(err * err, axis=-1, keepdims=True), axis=0, keepdims=True)
        dy = err * (1.0 / D)
        dy_ref[...] = dy
        dg_ref[...] += jnp.sum(dy * fhat, axis=0, keepdims=True)
        df_ref[...] = _rms_bwd(dy * g_ref[...], fhat, r).astype(BF16)

    return pl.pallas_call(
        body, name="ff2_loss", grid=(s // tm,),
        in_specs=[_rows(tm, D_FF), _full((D_FF, D)), _rows(tm, D), _rows(tm, D), _full((1, D))],
        out_specs=[_rows(tm, D), _rows(tm, D), _full((1, D)), _full((1, 1))],
        out_shape=[jax.ShapeDtypeStruct((s, D), F32), jax.ShapeDtypeStruct((s, D), BF16),
                   jax.ShapeDtypeStruct((1, D), F32), jax.ShapeDtypeStruct((1, 1), F32)],
        compiler_params=_params("arbitrary"),
    )(rl, w_ff2, x1, target, g_fpost)


def mm_tn(name, a, b, ta, tb, out_shape, out_spec):
    s = a.shape[0]

    def body(a_ref, b_ref, o_ref):
        o_ref[...] = _dot(a_ref[...], b_ref[...], TN)

    return pl.pallas_call(
        body, name=name, grid=(a.shape[1] // ta, b.shape[1] // tb),
        in_specs=[pl.BlockSpec((s, ta), lambda i, j: (0, i)), pl.BlockSpec((s, tb), lambda i, j: (0, j))],
        out_specs=out_spec, out_shape=jax.ShapeDtypeStruct(out_shape, F32),
        compiler_params=_params("parallel", "parallel"),
    )(a, b)


def mm_nt(name, a, w):
    s = a.shape[0]
    tm = 512

    def body(a_ref, w_ref, o_ref):
        o_ref[...] = _dot(a_ref[...], w_ref[...], NT).astype(BF16)

    return pl.pallas_call(
        body, name=name, grid=(s // tm,), in_specs=[_rows(tm, D), _full((D, D))], out_specs=_rows(tm, D),
        out_shape=jax.ShapeDtypeStruct((s, D), BF16), compiler_params=_params("parallel"),
    )(a, w)


def ff2_bwd(df, w_ff2, a):
    s = df.shape[0]
    tm = 1024

    def body(df_ref, w_ref, a_ref, da_ref):
        drl = _dot(df_ref[...], w_ref[...], NT)
        da_ref[...] = (drl * (2.0 * jnp.maximum(a_ref[...].astype(F32), 0.0))).astype(BF16)

    return pl.pallas_call(
        body, name="ff2_bwd", grid=(s // tm, D_FF // D),
        in_specs=[pl.BlockSpec((tm, D), lambda i, j: (i, 0)), pl.BlockSpec((D, D), lambda i, j: (j, 0)),
                  pl.BlockSpec((tm, D), lambda i, j: (i, j))],
        out_specs=pl.BlockSpec((tm, D), lambda i, j: (i, j)),
        out_shape=jax.ShapeDtypeStruct((s, D_FF), BF16), compiler_params=_params("parallel", "parallel"),
    )(df, w_ff2, a)


def ff1_bwd_norms(da, wg, x1, o, dy, g_fpre, g_post, swapping):
    s = x1.shape[0]
    tm = 512
    n = len(swapping)

    def body(*refs):
        da_ref, w_ref, x1_ref, o_ref, dy_ref, gf_ref, gp_ref = refs[:7]
        dx1_ref, do_ref, dgf_ref, dgp_ref = refs[7 + n:11 + n]
        acc_ref = refs[11 + 2 * n]
        i, k = pl.program_id(0), pl.program_id(1)
        if n:
            send, finish = _swap_phases(refs[7:7 + n], refs[11 + n:11 + 2 * n], *refs[12 + 2 * n:])
            pl.when((i == 0) & (k == 0))(send)

        @pl.when((i == 0) & (k == 0))
        def _():
            dgf_ref[...] = jnp.zeros_like(dgf_ref)
            dgp_ref[...] = jnp.zeros_like(dgp_ref)

        part = _dot(da_ref[...], w_ref[...], NT)

        @pl.when(k == 0)
        def _():
            acc_ref[...] = part

        @pl.when(k > 0)
        def _():
            acc_ref[...] += part

        @pl.when(k == N_CHIPS - 1)
        def _():
            dh2 = acc_ref[...]
            x1hat, r2 = _rms(x1_ref[...])
            dgf_ref[...] += jnp.sum(dh2 * x1hat, axis=0, keepdims=True)
            dx1 = dy_ref[...] + _rms_bwd(dh2 * gf_ref[...], x1hat, r2)
            ohat, r1 = _rms(o_ref[...])
            dgp_ref[...] += jnp.sum(dx1 * ohat, axis=0, keepdims=True)
            dx1_ref[...] = dx1
            do_ref[...] = _rms_bwd(dx1 * gp_ref[...], ohat, r1).astype(BF16)

        if n:
            pl.when((i == s // tm - 1) & (k == N_CHIPS - 1))(finish)

    row = pl.BlockSpec((tm, D), lambda i, k: (i, 0))
    vec = pl.BlockSpec((1, D), lambda i, k: (0, 0))
    res = pl.pallas_call(
        body, name="ff1_bwd_norms", grid=(s // tm, N_CHIPS),
        in_specs=[pl.BlockSpec((tm, D), lambda i, k: (i, k)), pl.BlockSpec((None, D, D), lambda i, k: (k, 0, 0)),
                  row, row, row, vec, vec] + [ANY] * n,
        out_specs=[row, row, vec, vec] + [ANY] * n,
        out_shape=[jax.ShapeDtypeStruct((s, D), F32), jax.ShapeDtypeStruct((s, D), BF16),
                   jax.ShapeDtypeStruct((1, D), F32), jax.ShapeDtypeStruct((1, D), F32)] + _swap_shapes(swapping),
        scratch_shapes=[pltpu.VMEM((tm, D), F32)] + (_swap_sems(n) if n else []),
        compiler_params=_params("arbitrary", "arbitrary", communicates=bool(n)),
    )(da, wg, x1, o, dy, g_fpre, g_post, *swapping)
    return res[0], res[1], res[2], res[3], res[4:]


def out_bwd_gates(do, w_out, pa, pb, z, bg):
    s = do.shape[0]
    tm = 512

    def body(do_ref, w_ref, pa_ref, pb_ref, ga_ref, gb_ref, bg_ref, dpa_ref, dpb_ref, dga_ref, dgb_ref, dbg_ref):
        @pl.when(pl.program_id(0) == 0)
        def _():
            dbg_ref[...] = jnp.zeros_like(dbg_ref)

        dm = _dot(do_ref[...], w_ref[...], NT)
        sa = _sigmoid(ga_ref[...] + bg_ref[0:1, :])
        sb = _sigmoid(gb_ref[...] + bg_ref[1:2, :])
        dpa_ref[...] = (dm * sa).astype(BF16)
        dpb_ref[...] = (dm * sb).astype(BF16)
        dga = dm * pa_ref[...].astype(F32) * (sa * (1.0 - sa))
        dgb = dm * pb_ref[...].astype(F32) * (sb * (1.0 - sb))
        dga_ref[...] = dga.astype(BF16)
        dgb_ref[...] = dgb.astype(BF16)
        dbg_ref[0:1, :] += jnp.sum(dga, axis=0, keepdims=True)
        dbg_ref[1:2, :] += jnp.sum(dgb, axis=0, keepdims=True)

    out = jax.ShapeDtypeStruct((s, D), BF16)
    return pl.pallas_call(
        body, name="out_bwd_gates", grid=(s // tm,),
        in_specs=[_rows(tm, D), _full((D, D)), _rows(tm, D), _rows(tm, D), _rows(tm, D, 5), _rows(tm, D, 6),
                  _full((2, D))],
        out_specs=[_rows(tm, D)] * 4 + [_full((2, D))],
        out_shape=[out] * 4 + [jax.ShapeDtypeStruct((2, D), F32)], compiler_params=_params("arbitrary"),
    )(do, w_out, pa, pb, z, z, bg)


def gating_bwd(z, dya, ln_g, ln_b, w_s, bs_t, swapping):
    s = z.shape[0]
    ones = functools.partial(jnp.ones, (8, CHUNK), BF16)
    n = len(swapping)

    def body(*refs):
        u_ref, v_ref, dya_ref, lg_ref, lb_ref, ws_ref, bst_ref = refs[:7]
        du_ref, dv_ref, dws_ref, dbs_ref, dlg_ref, dlb_ref = refs[7 + n:13 + n]
        dvn_ref = refs[13 + 2 * n]
        ci = pl.program_id(0)
        if n:
            send, finish = _swap_phases(refs[7:7 + n], refs[13 + n:13 + 2 * n], *refs[14 + 2 * n:])
            pl.when(ci == 0)(send)

        @pl.when(ci == 0)
        def _():
            dws_ref[...] = jnp.zeros_like(dws_ref)
            dbs_ref[...] = jnp.zeros_like(dbs_ref)
            dlg_ref[...] = jnp.zeros_like(dlg_ref)
            dlb_ref[...] = jnp.zeros_like(dlb_ref)

        ug, dug_du = _gelu_and_grad(u_ref[...].astype(F32))
        vg, dvg_dv = _gelu_and_grad(v_ref[...].astype(F32))
        vhat, rstd = _layer_norm(vg)
        vn = (vhat * lg_ref[...] + lb_ref[...]).astype(BF16)
        dya = dya_ref[...].astype(F32)
        for g in range(GROUPS):
            cols = slice(g * CHUNK, (g + 1) * CHUNK)
            ws = _tril_ws(ws_ref, g)
            mixed = _dot(ws, vn[:, cols]) + bst_ref[:, g:g + 1]
            du_ref[:, cols] = (dya[:, cols] * mixed * dug_du[:, cols]).astype(BF16)
            dmix = (dya[:, cols] * ug[:, cols]).astype(BF16)
            dbs_ref[g] += _dot(ones(), dmix, NT)
            dws_ref[g] += _dot(dmix, vn[:, cols], NT)
            dvn_ref[:, cols] = _dot(ws, dmix, TN)
        dvn = dvn_ref[...]
        dlg_ref[...] += jnp.sum(dvn * vhat, axis=0, keepdims=True)
        dlb_ref[...] += jnp.sum(dvn, axis=0, keepdims=True)
        dvh = dvn * lg_ref[...]
        dvg = rstd * (dvh - jnp.mean(dvh, axis=-1, keepdims=True)
                      - vhat * jnp.mean(dvh * vhat, axis=-1, keepdims=True))
        dv_ref[...] = (dvg * dvg_dv).astype(BF16)

        @pl.when(ci == pl.num_programs(0) - 1)
        def _():
            r = lax.broadcasted_iota(jnp.int32, (CHUNK, CHUNK), 0)
            c = lax.broadcasted_iota(jnp.int32, (CHUNK, CHUNK), 1)
            for g in range(GROUPS):
                dws_ref[g] = jnp.where(c <= r, dws_ref[g], 0.0)

        if n:
            pl.when(ci == pl.num_programs(0) - 1)(finish)

    out = jax.ShapeDtypeStruct((s, D), BF16)
    res = pl.pallas_call(
        body, name="gating_bwd", grid=(s // CHUNK,),
        in_specs=[_rows(CHUNK, D, 0), _rows(CHUNK, D, 1), _rows(CHUNK, D), _full((1, D)), _full((1, D)),
                  _full((GROUPS, CHUNK, CHUNK)), _full((CHUNK, GROUPS))] + [ANY] * n,
        out_specs=[_rows(CHUNK, D), _rows(CHUNK, D), _full((GROUPS, CHUNK, CHUNK)), _full((GROUPS, 8, CHUNK)),
                   _full((1, D)), _full((1, D))] + [ANY] * n,
        out_shape=[out, out, jax.ShapeDtypeStruct((GROUPS, CHUNK, CHUNK), F32),
                   jax.ShapeDtypeStruct((GROUPS, 8, CHUNK), F32),
                   jax.ShapeDtypeStruct((1, D), F32), jax.ShapeDtypeStruct((1, D), F32)] + _swap_shapes(swapping),
        scratch_shapes=[pltpu.VMEM((CHUNK, D), F32)] + (_swap_sems(n) if n else []),
        compiler_params=_params("arbitrary", communicates=bool(n)),
    )(z, z, dya, ln_g, ln_b, w_s, bs_t, *swapping)
    return (*res[:6], res[6:])


def attn_bwd(z, yb, dyb, lse, logc, ka, kb, scattering, gathering=None):
    s = z.shape[0]
    nq = s // ATT_T
    t = ATT_T
    grp = ATT_BWD_GROUP
    ngrp = HEADS // 2 // grp
    wide = 128 * grp
    qcol, kcol, vcol = 2 * D // wide, 3 * D // wide, 4 * D // wide
    scale = 1.0 / math.sqrt(HEAD_DIM)
    n = len(scattering)
    g8 = 0 if gathering is None else 1

    def body(*refs):
        q_ref, k_ref, v_ref, y_ref, dy_ref, lse_ref, lc_ref, ka_ref, kb_ref = refs[:9]
        dq_ref, dk_ref, dv_ref = refs[9 + n + g8:12 + n + g8]
        qa_s, qt_s, da_s, dt_s, dq_s, dkt_s, dvt_s = refs[12 + 2 * n + 2 * g8:19 + 2 * n + 2 * g8]
        sems = refs[19 + 2 * n + 2 * g8:]
        gi, j = pl.program_id(0), pl.program_id(1)
        first, lane, ones = _head_masks()
        if n:
            send, finish = _scatter_phases(refs[9:9 + n], refs[12 + n + g8:12 + 2 * n + g8], *sems[:2])
            pl.when((gi == 0) & (j == 0))(send)
        if g8:
            send8, pass_on8, finish8 = _allgather8_phases(refs[12 + 2 * n + g8], *sems[2 * (n > 0):])
            pl.when((gi == 0) & (j == 0))(send8)
            pl.when((gi == ngrp - 1) & (j == nq - 1))(pass_on8)

        @pl.when(j == 0)
        def _():
            dq_s[...] = jnp.zeros_like(dq_s)
            for pr in range(grp):
                cols = slice(pr * 128, (pr + 1) * 128)
                for ib in range(nq):
                    rows = slice(ib * t, (ib + 1) * t)
                    q = q_ref[rows, cols].astype(F32) * scale
                    lse = lse_ref[rows, cols]
                    qa_s[pr, 0, ib] = jnp.where(first, q, _place3(lane, HEAD_DIM + 2 * AUG, _split3(-lse[:, 0:1]),
                                                                  ones(0, 2 * AUG))).astype(BF16)
                    qa_s[pr, 1, ib] = jnp.where(
                        first, _place3(lane, 2 * AUG, _split3(-lse[:, HEAD_DIM:HEAD_DIM + 1]), ones(1, 2 * AUG)),
                        q).astype(BF16)
                    qt_s[pr, ib, :, 0:t] = jnp.where(first, q, 0.0).T.astype(BF16)
                    qt_s[pr, ib, :, t:2 * t] = jnp.where(first, 0.0, q).T.astype(BF16)
                    do = dy_ref[rows, cols].astype(F32)
                    prod = do * y_ref[rows, cols].astype(F32)
                    dd0 = jnp.sum(jnp.where(first, prod, 0.0), axis=-1, keepdims=True)
                    dd1 = jnp.sum(jnp.where(first, 0.0, prod), axis=-1, keepdims=True)
                    da_s[pr, 0, ib] = jnp.where(first, do, _place3(lane, HEAD_DIM, _split3(-dd0), 0.0)).astype(BF16)
                    da_s[pr, 1, ib] = jnp.where(first, _place3(lane, 0, _split3(-dd1), 0.0), do).astype(BF16)
                    dt_s[pr, ib, :, 0:t] = jnp.where(first, do, 0.0).T.astype(BF16)
                    dt_s[pr, ib, :, t:2 * t] = jnp.where(first, 0.0, do).T.astype(BF16)

        keys = []
        for pr in range(grp):
            kj = k_ref[:, pr * 128:(pr + 1) * 128].astype(F32)
            vj = v_ref[:, pr * 128:(pr + 1) * 128].astype(F32)
            keys.append((
                jnp.where(first, kj, ka_ref[pr, 0] + kb_ref[pr, 0, pl.ds(j, 1), :]).astype(BF16),
                jnp.where(first, ka_ref[pr, 1] + kb_ref[pr, 1, pl.ds(j, 1), :], kj).astype(BF16),
                jnp.concatenate([jnp.where(first, kj, 0.0), jnp.where(first, 0.0, kj)], axis=0).astype(BF16),
                jnp.where(first, vj, ones(0, AUG)).astype(BF16),
                jnp.where(first, ones(1, AUG), vj).astype(BF16)))
        dkt_s[...] = jnp.zeros_like(dkt_s)
        dvt_s[...] = jnp.zeros_like(dvt_s)

        def step(i, _):
            lc = lc_ref[i - j]
            rows = pl.ds(pl.multiple_of(i * t, t), t)
            for pr in range(grp):
                k0a, k1a, kst, v0a, v1a = keys[pr]
                p0 = jnp.exp(_dot(qa_s[pr, 0, i], k0a, NT) + lc)
                p1 = jnp.exp(_dot(qa_s[pr, 1, i], k1a, NT) + lc)
                e0 = (p0 * _dot(da_s[pr, 0, i], v0a, NT)).astype(BF16)
                e1 = (p1 * _dot(da_s[pr, 1, i], v1a, NT)).astype(BF16)
                dq_s[pr, rows, :] += _dot(jnp.concatenate([e0, e1], axis=1), kst)
                dvt_s[pr] += _dot(dt_s[pr, i], jnp.concatenate([p0.astype(BF16), p1.astype(BF16)], axis=0))
                dkt_s[pr] += _dot(qt_s[pr, i], jnp.concatenate([e0, e1], axis=0))
            return 0

        lax.fori_loop(j, nq, step, 0)
        for pr in range(grp):
            dk_ref[:, pr * 128:(pr + 1) * 128] = dkt_s[pr].T.astype(BF16)
            dv_ref[:, pr * 128:(pr + 1) * 128] = dvt_s[pr].T.astype(BF16)

        @pl.when(j == nq - 1)
        def _():
            for pr in range(grp):
                dq_ref[:, pr * 128:(pr + 1) * 128] = (dq_s[pr] * scale).astype(BF16)

        if n:
            pl.when((gi == ngrp - 1) & (j == nq - 1))(finish)
        if g8:
            pl.when((gi == ngrp - 1) & (j == nq - 1))(finish8)

    colblock = lambda c: pl.BlockSpec((s, wide), lambda g, j: (0, c + g))
    blk = lambda c: pl.BlockSpec((t, wide), lambda g, j: (j, c + g))
    out = jax.ShapeDtypeStruct((s, D), BF16)
    res = pl.pallas_call(
        body, name="attn_bwd", grid=(ngrp, nq),
        in_specs=[colblock(qcol), blk(kcol), blk(vcol), colblock(0), colblock(0), colblock(0),
                  _full((nq, t, t)), pl.BlockSpec((grp, 2, t, 128), lambda g, j: (g, 0, 0, 0)),
                  pl.BlockSpec((grp, 2, nq, 128), lambda g, j: (g, 0, 0, 0))] + [ANY] * (n + g8),
        out_specs=[colblock(0), blk(0), blk(0)] + [ANY] * (n + g8),
        out_shape=[out] * 3 + _scatter_shapes(scattering)
        + ([jax.ShapeDtypeStruct(gathering.shape, gathering.dtype)] if g8 else []),
        input_output_aliases={9 + n: 3 + n} if g8 else {},
        scratch_shapes=[pltpu.VMEM((grp, 2, nq, t, 128), BF16), pltpu.VMEM((grp, nq, 128, 2 * t), BF16),
                        pltpu.VMEM((grp, 2, nq, t, 128), BF16), pltpu.VMEM((grp, nq, 128, 2 * t), BF16),
                        pltpu.VMEM((grp, s, 128), F32), pltpu.VMEM((grp, 128, t), F32),
                        pltpu.VMEM((grp, 128, t), F32)]
        + (_scatter_sems(n) if n else [])
        + ([pltpu.SemaphoreType.DMA((7,)), pltpu.SemaphoreType.DMA((7,))] if g8 else []),
        compiler_params=_params("arbitrary", "arbitrary", communicates=bool(n + g8)),
    )(z, z, z, yb, dyb, lse, logc, ka, kb, *scattering, *([gathering] if g8 else []))
    return res[0], res[1], res[2], res[3:3 + n], (res[3 + n] if g8 else None)


def in_bwd_norm(dz, wg, x, dx1, g_pre, scattering, gathering=None):
    s = x.shape[0]
    tm = 512
    n = len(scattering)
    g = 0 if gathering is None else 1
    last = (s // tm - 1, N_CHIPS - 1)

    def body(*refs):
        dz_ref, w_ref, x_ref, dx1_ref, g_ref = refs[:5]
        dx_ref, dg_ref = refs[5 + n + g:7 + n + g]
        acc_ref = refs[7 + 2 * n + 2 * g]
        sems = refs[8 + 2 * n + 2 * g:]
        i, k = pl.program_id(0), pl.program_id(1)
        if n:
            send, finish = _scatter_phases(refs[5:5 + n], refs[7 + n + g:7 + 2 * n + g], *sems[:2])
            pl.when((i == 0) & (k == 0))(send)
        if g:
            send8, pass_on8, finish8 = _allgather8_phases(refs[7 + 2 * n + g], *sems[2 * (n > 0):])
            pl.when((i == 0) & (k == 0))(send8)
            pl.when((i == last[0]) & (k == last[1]))(pass_on8)

        @pl.when((i == 0) & (k == 0))
        def _():
            dg_ref[...] = jnp.zeros_like(dg_ref)

        part = _dot(dz_ref[...], w_ref[...], NT)

        @pl.when(k == 0)
        def _():
            acc_ref[...] = part

        @pl.when(k > 0)
        def _():
            acc_ref[...] += part

        @pl.when(k == N_CHIPS - 1)
        def _():
            dh = acc_ref[...]
            xhat, r = _rms(x_ref[...])
            dg_ref[...] += jnp.sum(dh * xhat, axis=0, keepdims=True)
            dx_ref[...] = dx1_ref[...] + _rms_bwd(dh * g_ref[...], xhat, r)

        if n:
            pl.when((i == last[0]) & (k == last[1]))(finish)
        if g:
            pl.when((i == last[0]) & (k == last[1]))(finish8)

    row = pl.BlockSpec((tm, D), lambda i, k: (i, 0))
    vec = pl.BlockSpec((1, D), lambda i, k: (0, 0))
    res = pl.pallas_call(
        body, name="in_bwd_norm", grid=(s // tm, N_CHIPS),
        in_specs=[pl.BlockSpec((tm, IN_SHARD), lambda i, k: (i, k)),
                  pl.BlockSpec((None, D, IN_SHARD), lambda i, k: (k, 0, 0)), row, row, vec] + [ANY] * (n + g),
        out_specs=[row, vec] + [ANY] * (n + g),
        out_shape=[jax.ShapeDtypeStruct((s, D), F32), jax.ShapeDtypeStruct((1, D), F32)]
        + _scatter_shapes(scattering) + ([jax.ShapeDtypeStruct(gathering.shape, gathering.dtype)] if g else []),
        input_output_aliases={5 + n: 2 + n} if g else {},
        scratch_shapes=[pltpu.VMEM((tm, D), F32)] + (_scatter_sems(n) if n else [])
        + ([pltpu.SemaphoreType.DMA((7,)), pltpu.SemaphoreType.DMA((7,))] if g else []),
        compiler_params=_params("arbitrary", "arbitrary", communicates=bool(n + g)),
    )(dz, wg, x, dx1, g_pre, *scattering, *([gathering] if g else []))
    return res[0], res[1], res[2:2 + n], (res[2 + n] if g else None)


def _adamw_math(w, g, m, v):
    m = ADAM_B1 * m + (1.0 - ADAM_B1) * g
    v = ADAM_B2 * v + (1.0 - ADAM_B2) * (g * g)
    m_hat = m / (1.0 - ADAM_B1 ** ADAM_STEP)
    v_hat = v / (1.0 - ADAM_B2 ** ADAM_STEP)
    delta = -ADAM_LR * (m_hat / (jnp.sqrt(v_hat) + ADAM_EPS) + ADAM_WD * w)
    return delta, m, v


def adamw(name, w, g, m, v, tr, gathering=None):
    r, c = w.shape
    n = 0 if gathering is None else 1
    steps = r // tr

    def body(*refs):
        w_ref, g_ref, m_ref, v_ref = refs[:4]
        go_ref, d_ref, nm_ref, nv_ref = refs[4 + n:8 + n]
        i = pl.program_id(0)
        if n:
            send, pass_on, finish = _allgather8_phases(refs[8 + n], *refs[9 + n:])
            pl.when(i == 0)(send)
            pl.when(i == steps - 1)(pass_on)
        g = g_ref[...]
        go_ref[...] = g
        d_ref[...], nm_ref[...], nv_ref[...] = _adamw_math(w_ref[...], g, m_ref[...], v_ref[...])
        if n:
            pl.when(i == steps - 1)(finish)

    out = jax.ShapeDtypeStruct((r, c), F32)
    res = pl.pallas_call(
        body, name=name, grid=(steps,), in_specs=[_rows(tr, c)] * 4 + [ANY] * n,
        out_specs=[_rows(tr, c)] * 4 + [ANY] * n,
        out_shape=[out] * 4 + ([jax.ShapeDtypeStruct(gathering.shape, gathering.dtype)] if n else []),
        input_output_aliases={4: 4} if n else {},
        scratch_shapes=[pltpu.SemaphoreType.DMA((7,)), pltpu.SemaphoreType.DMA((7,))] if n else [],
        compiler_params=_params("arbitrary", communicates=bool(n)),
    )(w, g, m, v, *([gathering] if n else []))
    return res


def _allgather8_phases(buf, send_sems, recv_sems):
    x, y, c, chips = _place()
    me = 2 * x + y
    sibling = (x, y, 1 - c)
    rows = buf.shape[1] // 2

    def part(chip, core):
        return buf.at[chip, pl.ds(core * rows, rows)]

    def copy(k, block, to):
        return pltpu.make_async_remote_copy(src_ref=block, dst_ref=block, send_sem=send_sems.at[k],
                                            recv_sem=recv_sems.at[k], device_id=to, device_id_type=MESH)

    def chip_of(j):
        return 2 * chips[j][0] + chips[j][1]

    def send():
        copy(0, part(me, c), sibling).start()
        for j in range(3):
            copy(1 + j, part(me, c), (chips[j][0], chips[j][1], c)).start()

    def pass_on():
        for j in range(3):
            copy(1 + j, part(chip_of(j), c), (chips[j][0], chips[j][1], c)).wait_recv()
            copy(4 + j, part(chip_of(j), c), sibling).start()

    def finish():
        copy(0, part(me, 1 - c), sibling).wait_recv()
        for j in range(3):
            copy(4 + j, part(chip_of(j), 1 - c), sibling).wait_recv()
        copy(0, part(me, c), sibling).wait_send()
        for j in range(3):
            copy(1 + j, part(me, c), (chips[j][0], chips[j][1], c)).wait_send()
            copy(4 + j, part(chip_of(j), c), sibling).wait_send()

    return send, pass_on, finish


def add_halves(name, g, recv, c_idx, tr):
    n, h, c = recv.shape

    def body(c_ref, g_ref, r_ref, o_ref):
        o_ref[...] = (g_ref[...] + r_ref[...]).astype(BF16)

    nb = h // tr
    return pl.pallas_call(
        body, name=name,
        grid_spec=pltpu.PrefetchScalarGridSpec(
            num_scalar_prefetch=1, grid=(n, nb),
            in_specs=[pl.BlockSpec((None, tr, c), lambda k, i, c_ref: (k, c_ref[0] * nb + i, 0)),
                      pl.BlockSpec((None, tr, c), lambda k, i, c_ref: (k, i, 0))],
            out_specs=pl.BlockSpec((None, tr, c), lambda k, i, c_ref: (k, i, 0))),
        out_shape=jax.ShapeDtypeStruct((n, h, c), BF16), compiler_params=_params("parallel", "parallel"),
    )(c_idx, g, recv)


def sum_chips(name, parts, recv, where, tr):
    n, h, c = recv.shape
    nb = h // tr

    def body(w_ref, p_ref, r_ref, o_ref):
        acc = p_ref[...].astype(F32)
        for k in range(n):
            acc = acc + r_ref[k].astype(F32)
        o_ref[...] = acc

    return pl.pallas_call(
        body, name=name,
        grid_spec=pltpu.PrefetchScalarGridSpec(
            num_scalar_prefetch=1, grid=(nb,),
            in_specs=[pl.BlockSpec((None, tr, c), lambda i, w_ref: (w_ref[0], i, 0)),
                      pl.BlockSpec((n, tr, c), lambda i, w_ref: (0, i, 0))],
            out_specs=pl.BlockSpec((tr, c), lambda i, w_ref: (w_ref[1] * nb + i, 0))),
        out_shape=jax.ShapeDtypeStruct((2 * h, c), F32), compiler_params=_params("parallel"),
    )(where, parts, recv)


def place_shard(name, shard, where, dtype, tr):
    r, c = shard.shape

    def body(w_ref, s_ref, o_ref):
        o_ref[...] = s_ref[...].astype(dtype)

    return pl.pallas_call(
        body, name=name,
        grid_spec=pltpu.PrefetchScalarGridSpec(
            num_scalar_prefetch=1, grid=(r // tr,),
            in_specs=[pl.BlockSpec((tr, c), lambda i, w_ref: (i, 0))],
            out_specs=pl.BlockSpec((None, tr, c), lambda i, w_ref: (w_ref[0], i, 0))),
        out_shape=jax.ShapeDtypeStruct((N_CHIPS, r, c), dtype), compiler_params=_params("parallel"),
    )(where, shard)


ANY = pl.BlockSpec(memory_space=pl.ANY)


def _place():
    x, y, c = lax.axis_index("x"), lax.axis_index("y"), lax.axis_index("c")
    chips = [(1 - x, y), (x, 1 - y), (1 - x, 1 - y)]
    return x, y, c, chips


def gather_shards(arrays):
    n = len(arrays)

    def body(*refs):
        send, pass_on, finish = _gather_phases(refs[n:2 * n], *refs[2 * n:], _spans(arrays))
        send()
        pass_on()
        finish()

    return pl.pallas_call(
        body, name="gather_shards", in_specs=[ANY] * n, out_specs=[ANY] * n,
        out_shape=[jax.ShapeDtypeStruct(a.shape, a.dtype) for a in _arrays(arrays)],
        input_output_aliases={w: w for w in range(n)}, scratch_shapes=_gather_sems(n),
        compiler_params=pltpu.CompilerParams(has_side_effects=True),
    )(*_arrays(arrays))


def _gather_sems(n):
    return [pltpu.SemaphoreType.DMA((6 * n,)), pltpu.SemaphoreType.DMA((6 * n,))]


class Span(typing.NamedTuple):
    array: jax.Array
    lo: int
    hi: int
    ways: tuple = (0, 1, 2)


def _arrays(gathering):
    return [g.array if isinstance(g, Span) else g for g in gathering]


def _spans(gathering):
    return [(g.lo, g.hi, g.ways) if isinstance(g, Span) else (0, g.shape[1], (0, 1, 2)) for g in gathering]


def _gather_phases(out, send_sems, recv_sems, spans):
    n = len(out)
    if not any(ways for _, _, ways in spans):
        return (lambda: None,) * 3
    x, y, c, chips = _place()
    me = 2 * x + y
    sibling = (x, y, 1 - c)

    def half(w, chip, core):
        lo, hi, _ = spans[w]
        h = (hi - lo) // 2
        return out[w].at[chip, pl.ds(lo + core * h, h)]

    def copy(k, block, to):
        return pltpu.make_async_remote_copy(src_ref=block, dst_ref=block, send_sem=send_sems.at[k],
                                            recv_sem=recv_sems.at[k], device_id=to, device_id_type=MESH)

    def over_ici(w, j, chip):
        return copy(3 * w + j, half(w, chip, c), (chips[j][0], chips[j][1], c))

    def over_d2d(w, j, core):
        return copy(3 * n + 3 * w + j, half(w, 2 * chips[j][0] + chips[j][1], core), sibling)

    pairs = [(w, j) for w in range(n) for j in spans[w][2]]

    def send():
        for w, j in pairs:
            over_ici(w, j, me).start()

    def pass_on():
        for w, j in pairs:
            over_ici(w, j, 2 * chips[j][0] + chips[j][1]).wait_recv()
            over_d2d(w, j, c).start()

    def finish():
        for w, j in pairs:
            over_d2d(w, j, 1 - c).wait_recv()
        for w, j in pairs:
            over_ici(w, j, me).wait_send()
            over_d2d(w, j, c).wait_send()

    return send, pass_on, finish


def _relay_sems():
    return [pltpu.SemaphoreType.DMA((4,)), pltpu.SemaphoreType.DMA((4,))]


def _relay_phases(out, send_sems, recv_sems):
    x, y, c, chips = _place()
    sibling = (x, y, 1 - c)
    rows = out.shape[1]
    quarter = rows // 4
    far = 2 * chips[2][0] + chips[2][1]

    def piece(chip, way, core):
        return out.at[chip, pl.ds(way * (rows // 2) + core * quarter, quarter)]

    def copy(k, block, to):
        return pltpu.make_async_remote_copy(src_ref=block, dst_ref=block, send_sem=send_sems.at[k],
                                            recv_sem=recv_sems.at[k], device_id=to, device_id_type=MESH)

    def over_ici(way, chip):
        return copy(way, piece(chip, way, c), (chips[way][0], chips[way][1], c))

    def over_d2d(way, core):
        return copy(2 + way, piece(far, way, core), sibling)

    def send():
        for way in range(2):
            other = chips[1 - way]
            over_ici(way, 2 * other[0] + other[1]).start()

    def pass_on():
        for way in range(2):
            over_ici(way, far).wait_recv()
            over_d2d(way, c).start()

    def finish():
        for way in range(2):
            over_d2d(way, 1 - c).wait_recv()
        for way in range(2):
            other = chips[1 - way]
            over_ici(way, 2 * other[0] + other[1]).wait_send()
            over_d2d(way, c).wait_send()

    return send, pass_on, finish


def swap_halves(name, grads):
    n = len(grads)

    def body(*refs):
        send, finish = _swap_phases(refs[:n], refs[n:2 * n], *refs[2 * n:])
        send()
        finish()

    return pl.pallas_call(
        body, name=name, in_specs=[ANY] * n, out_specs=[ANY] * n, out_shape=_swap_shapes(grads),
        scratch_shapes=_swap_sems(n), compiler_params=pltpu.CompilerParams(has_side_effects=True),
    )(*grads)


def _swap_shapes(grads):
    return [jax.ShapeDtypeStruct((a.shape[0], a.shape[1] // 2, a.shape[2]), a.dtype) for a in grads]


def _swap_sems(n):
    return [pltpu.SemaphoreType.DMA((n,)), pltpu.SemaphoreType.DMA((n,))]


def _swap_phases(g, out, send_sems, recv_sems):
    x, y, c, _ = _place()

    def copies():
        return [pltpu.make_async_remote_copy(
            src_ref=g[w].at[:, pl.ds((1 - c) * (g[w].shape[1] // 2), g[w].shape[1] // 2)], dst_ref=out[w],
            send_sem=send_sems.at[w], recv_sem=recv_sems.at[w], device_id=(x, y, 1 - c), device_id_type=MESH)
            for w in range(len(g))]

    def send():
        for cp in copies():
            cp.start()

    def finish():
        for cp in copies():
            cp.wait()

    return send, finish


def _send_phases(g, out, send_sems, recv_sems):
    x, y, c, _ = _place()

    def copies():
        return [pltpu.make_async_remote_copy(
            src_ref=g[w], dst_ref=out[w], send_sem=send_sems.at[w], recv_sem=recv_sems.at[w],
            device_id=(x, y, 1 - c), device_id_type=MESH) for w in range(len(g))]

    def send():
        for cp in copies():
            cp.start()

    def finish():
        for cp in copies():
            cp.wait()

    return send, finish


def dw_in_half(name, h, dz, which, sending):
    s = h.shape[0]
    hh, tb = D // 2, IN_SHARD // 2
    n = len(sending)
    steps = IN_COLS // tb

    def body(w_ref, *refs):
        a_ref, b_ref, o_ref = refs[0], refs[1], refs[2 + n]
        j = pl.program_id(0)
        if n:
            send, finish = _send_phases(refs[2:2 + n], refs[3 + n:3 + 2 * n], *refs[3 + 2 * n:])
            pl.when(j == 0)(send)
        o_ref[...] = _dot(a_ref[...], b_ref[...], TN)
        if n:
            pl.when(j == steps - 1)(finish)

    out = pl.pallas_call(
        body, name=name,
        grid_spec=pltpu.PrefetchScalarGridSpec(
            num_scalar_prefetch=1, grid=(steps,),
            in_specs=[pl.BlockSpec((s, hh), lambda j, w: (0, w[0])), pl.BlockSpec((s, tb), lambda j, w: (0, j))]
            + [ANY] * n,
            out_specs=[pl.BlockSpec((None, hh, tb), lambda j, w: (j // 2, 0, j % 2))] + [ANY] * n,
            scratch_shapes=_swap_sems(n) if n else []),
        out_shape=[jax.ShapeDtypeStruct((N_CHIPS, hh, IN_SHARD), F32)]
        + [jax.ShapeDtypeStruct(a.shape, a.dtype) for a in sending],
        compiler_params=_params("arbitrary", communicates=bool(n)),
    )(which, h, dz, *sending)
    return out[0], out[1:]


def scatter_chips(parts):
    n = len(parts)

    def body(*refs):
        send, finish = _scatter_phases(refs[:n], refs[n:2 * n], *refs[2 * n:])
        send()
        finish()

    return pl.pallas_call(
        body, name="scatter_chips", in_specs=[ANY] * n, out_specs=[ANY] * n,
        out_shape=_scatter_shapes(parts), scratch_shapes=_scatter_sems(n),
        compiler_params=pltpu.CompilerParams(has_side_effects=True),
    )(*parts)


def _scatter_shapes(parts):
    return [jax.ShapeDtypeStruct((3,) + a.shape[1:], a.dtype) for a in parts]


def _scatter_sems(n):
    return [pltpu.SemaphoreType.DMA((3 * n,)), pltpu.SemaphoreType.DMA((3 * n,))]


def _scatter_phases(p, out, send_sems, recv_sems):
    x, y, c, chips = _place()

    def copies():
        return [pltpu.make_async_remote_copy(
            src_ref=p[w].at[2 * px + py], dst_ref=out[w].at[j], send_sem=send_sems.at[3 * w + j],
            recv_sem=recv_sems.at[3 * w + j], device_id=(px, py, c), device_id_type=MESH)
            for w in range(len(p)) for j, (px, py) in enumerate(chips)]

    def send():
        for cp in copies():
            cp.start()

    def finish():
        for cp in copies():
            cp.wait()

    return send, finish


def join_halves(arrays):
    n = len(arrays)

    def body(*refs):
        out = refs[n:2 * n]
        send_sems, recv_sems = refs[2 * n:]
        x, y, c, _ = _place()

        def copy(w, core):
            h = out[w].shape[0] // 2
            rows = out[w].at[pl.ds(core * h, h)]
            return pltpu.make_async_remote_copy(
                src_ref=rows, dst_ref=rows, send_sem=send_sems.at[w], recv_sem=recv_sems.at[w],
                device_id=(x, y, 1 - c), device_id_type=MESH)

        for w in range(n):
            copy(w, c).start()
        for w in range(n):
            copy(w, 1 - c).wait_recv()
        for w in range(n):
            copy(w, c).wait_send()

    return pl.pallas_call(
        body, name="join_halves", in_specs=[ANY] * n, out_specs=[ANY] * n,
        out_shape=[jax.ShapeDtypeStruct(a.shape, a.dtype) for a in arrays],
        input_output_aliases={w: w for w in range(n)},
        scratch_shapes=[pltpu.SemaphoreType.DMA((n,)), pltpu.SemaphoreType.DMA((n,))],
        compiler_params=pltpu.CompilerParams(has_side_effects=True),
    )(*arrays)


def allreduce_small(packed):
    r, c = packed.shape
    n_dev = 8

    def body(x_ref, all_ref, sum_ref, send_sems, recv_sems, local_sem):
        x, y, cc, chips = _place()
        me, sibling = (x, y, cc), (x, y, 1 - cc)

        def rows(px, py, pc):
            return all_ref.at[4 * px + 2 * py + pc]

        def copy(k, block, to, src=None):
            return pltpu.make_async_remote_copy(
                src_ref=rows(*block) if src is None else src, dst_ref=rows(*block), send_sem=send_sems.at[k],
                recv_sem=recv_sems.at[k], device_id=to, device_id_type=MESH)

        mine = pltpu.make_async_copy(x_ref, rows(*me), local_sem)
        mine.start()
        first = [copy(0, me, sibling, src=x_ref)]
        first += [copy(1 + j, me, (*chip, cc), src=x_ref) for j, chip in enumerate(chips)]
        for cp in first:
            cp.start()
        passed = [copy(4 + j, (*chip, cc), sibling) for j, chip in enumerate(chips)]
        for j, chip in enumerate(chips):
            copy(1 + j, (*chip, cc), me).wait_recv()
            passed[j].start()
        copy(0, sibling, me).wait_recv()
        for j, chip in enumerate(chips):
            copy(4 + j, (*chip, 1 - cc), me).wait_recv()
        for cp in first + passed:
            cp.wait_send()
        mine.wait()
        acc = all_ref[0]
        for k in range(1, n_dev):
            acc = acc + all_ref[k]
        sum_ref[...] = acc

    vm = pl.BlockSpec(memory_space=pltpu.VMEM)
    return pl.pallas_call(
        body, name="allreduce_small", in_specs=[vm], out_specs=[vm, vm],
        out_shape=[jax.ShapeDtypeStruct((n_dev, r, c), F32), jax.ShapeDtypeStruct((r, c), F32)],
        scratch_shapes=[pltpu.SemaphoreType.DMA((7,)), pltpu.SemaphoreType.DMA((7,)), pltpu.SemaphoreType.DMA],
        compiler_params=pltpu.CompilerParams(has_side_effects=True, vmem_limit_bytes=VMEM_LIMIT),
    )(packed)[1]


def local_step(x, target, vecs, w_s, bs_t, bg, wg_in, late, core=None, order=None, where=None):
    on_mesh = core is not None

    def add(names, grads, recv):
        return [add_halves("add_" + n, g, r, core, min(r.shape[1], 256)) for n, g, r in zip(names, grads, recv)]

    g_pre, ln_g, ln_b, g_post, g_fpre, g_fpost = vecs
    s = x.shape[0]
    if order is None:
        order = jnp.arange(N_CHIPS, dtype=jnp.int32)
    logc = _attn_tables(s)
    ka, kb = _alibi_tables(s)

    h = norm_pre(x, g_pre)
    if not on_mesh:
        wg_a, wg_b, wg_out, wg_ff1, wg_ff2 = late
    if on_mesh:
        cut = D // 4
        z, (wg_in,), (wg_a, wg_b, wg_out, wg_ff1, wg_ff2) = mm_in(
            "mm_in_own", h, wg_in, order, 0, 1, None, [Span(wg_in, 0, D, (0, 1))], casting=late)
        z, (wg_in,), _ = mm_in("mm_in_near", h, wg_in, order, 1, 2, z, [Span(wg_in, 0, D, ())], relay=True)
        z, (wg_in, wg_a), _ = mm_in("mm_in_far", h, wg_in, order, 3, 1, z, [Span(wg_in, 0, D, ()), wg_a])
        ya, (wg_b, wg_ff2) = gating_fwd(z, ln_g, ln_b, w_s, bs_t, [wg_b, Span(wg_ff2, 0, cut)])
        yb, lse, (wg_ff1, wg_ff2, bg) = attn_fwd(z, logc, ka, kb, [wg_ff1, Span(wg_ff2, cut, 3 * cut), bg])
        bg = jnp.transpose(bg[:, :2, :], (1, 0, 2)).reshape(2, D)
    else:
        z, _, _ = mm_in("mm_in", h, wg_in, order, 0, N_CHIPS, None, [Span(wg_in, 0, D, ())])
        ya, _ = gating_fwd(z, ln_g, ln_b, w_s, bs_t, [])
        yb, lse, _ = attn_fwd(z, logc, ka, kb, [])
    merged, pa, pb, got = proj_merge(ya, yb, wg_a.reshape(D, D), wg_b.reshape(D, D), z, bg, [wg_out] if on_mesh else [])
    wg_out = got[0] if on_mesh else wg_out
    w_out = wg_out.reshape(D, D)
    o, x1, h2, got = out_norm(merged, w_out, x, g_post, g_fpre, [Span(wg_ff2, 3 * D // 4, D)] if on_mesh else [])
    a, rl, _ = mm_ff1(h2, wg_ff1, [])
    w_ff2 = (got[0] if on_mesh else wg_ff2).reshape(D_FF, D)
    dy, df, d_gfpost, loss = ff2_loss(rl, w_ff2, x1, target, g_fpost)

    half_cols = pl.BlockSpec((D, D // 2), lambda i, j: (0, j))
    d_wff2 = mm_tn("dw_ff2", rl, df, D // 2, D, (D_FF, D), pl.BlockSpec((D // 2, D), lambda i, j: (i, 0)))
    da = ff2_bwd(df, w_ff2, a)
    d_wff1 = mm_tn("dw_ff1", h2, da, D, D // 2, (N_CHIPS, D, D),
                   pl.BlockSpec((None, D, D // 2), lambda i, j: (j // 2, 0, j % 2)))
    d_ff = [d_wff1, d_wff2.reshape(N_CHIPS, D, D)]
    dx1, do, d_gfpre, d_gpost, recv_ff = ff1_bwd_norms(da, wg_ff1, x1, o, dy, g_fpre, g_post, d_ff if on_mesh else [])
    d_wout = mm_tn("dw_out", merged, do, D, D // 2, (D, D), half_cols)
    dpa, dpb, dga, dgb, d_bg = out_bwd_gates(do, w_out, pa, pb, z, bg)
    d_wa = mm_tn("dw_a", ya, dpa, D, D // 2, (D, D), half_cols)
    d_wb = mm_tn("dw_b", yb, dpb, D, D // 2, (D, D), half_cols)
    dya = mm_nt("dy_a", dpa, wg_a.reshape(D, D))
    dyb = mm_nt("dy_b", dpb, wg_b.reshape(D, D))
    d_proj = [d_wa.reshape(N_CHIPS, D // N_CHIPS, D), d_wb.reshape(N_CHIPS, D // N_CHIPS, D),
              d_wout.reshape(N_CHIPS, D // N_CHIPS, D)]
    du, dv, d_ws, d_bs, d_lng, d_lnb, recv_proj = gating_bwd(z, dya, ln_g, ln_b, w_s, bs_t, d_proj if on_mesh else [])
    early = d_proj + d_ff
    parts_early = add(BIG[1:], early, list(recv_proj) + list(recv_ff)) if on_mesh else []
    small = dict(b_gate=d_bg, ln_v_g=d_lng, ln_v_b=d_lnb, w_s=d_ws, b_s=d_bs[:, 0, :],
                 norm_mix_post=d_gpost, norm_ffn_pre=d_gfpre, norm_ffn_post=d_gfpost)
    packed = pack_small(dict(small, norm_mix_pre=jnp.zeros((1, D), F32)), loss, where) if on_mesh else None
    dq, dk, dvb, got_early, packed = attn_bwd(z, yb, dyb, lse, logc, ka, kb, parts_early, packed)
    dz = jnp.concatenate([du, dv, dq, dk, dvb, dga, dgb], axis=1)
    if on_mesh:
        for_sibling, _ = dw_in_half("dw_in_sibling", h, dz, 1 - core, [])
        mine, from_sibling = dw_in_half("dw_in_mine", h, dz, core, [for_sibling])
        d_win = None
        parts_late = [add_halves("add_w_in", mine, from_sibling[0], jnp.zeros((1,), jnp.int32), 256)]
    else:
        half = IN_SHARD // 2
        d_win = mm_tn("dw_in", h, dz, D, half, (N_CHIPS, D, IN_SHARD),
                      pl.BlockSpec((None, D, half), lambda i, j: (j // 2, 0, j % 2)))
        parts_late = []
    dx, d_gpre, got_late, _ = in_bwd_norm(dz, wg_in, x, dx1, g_pre, parts_late)
    small["norm_mix_pre"] = d_gpre
    return (loss[0, 0], dx, [d_win] + early, small, parts_late + parts_early, list(got_late) + list(got_early),
            packed)


BIG = ("w_in", "w_a_proj", "w_b_proj", "w_out", "w_ff1", "w_ff2")
SMALL = ("norm_mix_pre", "ln_v_g", "ln_v_b", "b_s", "norm_mix_post", "norm_ffn_pre", "norm_ffn_post", "w_s", "b_gate")
ORDER = ("norm_mix_pre", "w_in", "b_gate", "ln_v_g", "ln_v_b", "w_s", "b_s", "w_a_proj", "w_b_proj", "w_out",
         "norm_mix_post", "norm_ffn_pre", "w_ff1", "w_ff2", "norm_ffn_post")
VEC_ROWS = D // 128
WS_ROW = 7 * VEC_ROWS
BG_ROW = WS_ROW + GROUPS * CHUNK
LOSS_ROW = BG_ROW + 2 * VEC_ROWS
PACK_ROWS = LOSS_ROW + 8


def pack_small(small, loss, where):
    vectors = [small[n] for n in SMALL[:7]]
    operands = vectors + [small["w_s"], small["b_gate"], loss]

    def body(where_ref, *refs):
        out = refs[-1]
        ws_ref, bg_ref, loss_ref = refs[7:10]
        for i, n in enumerate(SMALL[:7]):
            if n == "b_s":
                out[i * VEC_ROWS:(i + 1) * VEC_ROWS, :] = refs[i][...]
            else:
                for j in range(VEC_ROWS):
                    out[i * VEC_ROWS + j:i * VEC_ROWS + j + 1, :] = refs[i][:, j * 128:(j + 1) * 128]
        for g in range(GROUPS):
            out[WS_ROW + g * CHUNK:WS_ROW + (g + 1) * CHUNK, :] = ws_ref[g]
        for r in range(2):
            for j in range(VEC_ROWS):
                row = BG_ROW + r * VEC_ROWS + j
                out[row:row + 1, :] = bg_ref[r:r + 1, j * 128:(j + 1) * 128]
        lane = lax.broadcasted_iota(jnp.int32, (8, 128), 1)
        sub = lax.broadcasted_iota(jnp.int32, (8, 128), 0)
        out[LOSS_ROW:LOSS_ROW + 8, :] = jnp.where((lane == 0) & (sub == 0), loss_ref[...], 0.0)

    return pl.pallas_call(
        body, name="pack_small",
        grid_spec=pltpu.PrefetchScalarGridSpec(
            num_scalar_prefetch=1, grid=(1,), in_specs=[_full(a.shape) for a in operands],
            out_specs=pl.BlockSpec((None, PACK_ROWS, 128), lambda i, w: (w[0], w[1], 0))),
        out_shape=jax.ShapeDtypeStruct((N_CHIPS, 2 * PACK_ROWS, 128), F32), compiler_params=_params("arbitrary"),
    )(where, *operands)


def pack_vector(vec, where):
    def body(where_ref, v_ref, out):
        for j in range(VEC_ROWS):
            out[j:j + 1, :] = v_ref[:, j * 128:(j + 1) * 128]

    return pl.pallas_call(
        body, name="pack_vector",
        grid_spec=pltpu.PrefetchScalarGridSpec(
            num_scalar_prefetch=1, grid=(1,), in_specs=[_full(vec.shape)],
            out_specs=pl.BlockSpec((None, VEC_ROWS, 128), lambda i, w: (w[0], w[1], 0))),
        out_shape=jax.ShapeDtypeStruct((N_CHIPS, 2 * VEC_ROWS, 128), F32), compiler_params=_params("arbitrary"),
    )(where, vec)


def adamw_small(gathered, first, chip, w, m, v):
    shapes = {n: (1, D) for n in SMALL}
    shapes.update(b_s=(GROUPS, CHUNK), w_s=(GROUPS * CHUNK, CHUNK), b_gate=(2, D // N_CHIPS))
    flat = lambda t: [t[n].reshape(shapes[n]) for n in SMALL]
    per = D // N_CHIPS // 128

    def body(chip_ref, all_ref, first_ref, *refs):
        params, outs = refs[:27], refs[27:]
        sub = lax.broadcasted_iota(jnp.int32, (VEC_ROWS, 128), 0)
        sum_ref = outs[36]
        total = all_ref[0, 0:PACK_ROWS, :]
        head = first_ref[0, 0:VEC_ROWS, :]
        for k in range(1, 2 * N_CHIPS):
            total = total + all_ref[k // 2, (k % 2) * PACK_ROWS:(k % 2 + 1) * PACK_ROWS, :]
            head = head + first_ref[k // 2, (k % 2) * VEC_ROWS:(k % 2 + 1) * VEC_ROWS, :]
        sum_ref[...] = total
        sum_ref[0:VEC_ROWS, :] = head

        def gate_row(r):
            rows = sum_ref[BG_ROW + r * VEC_ROWS:BG_ROW + (r + 1) * VEC_ROWS, :]
            return jnp.concatenate([jnp.sum(jnp.where(sub == per * chip_ref[0] + j, rows, 0.0), axis=0, keepdims=True)
                                    for j in range(per)], axis=1)

        for i, n in enumerate(SMALL):
            if n == "b_s":
                g = sum_ref[i * VEC_ROWS:(i + 1) * VEC_ROWS, :]
            elif n == "w_s":
                g = sum_ref[WS_ROW:BG_ROW, :]
            elif n == "b_gate":
                g = jnp.concatenate([gate_row(0), gate_row(1)], axis=0)
            else:
                g = jnp.concatenate([sum_ref[i * VEC_ROWS + j:i * VEC_ROWS + j + 1, :] for j in range(VEC_ROWS)],
                                    axis=1)
            delta, nm, nv = _adamw_math(params[i][...], g, params[9 + i][...], params[18 + i][...])
            outs[4 * i][...], outs[4 * i + 1][...], outs[4 * i + 2][...], outs[4 * i + 3][...] = g, delta, nm, nv

    vm = pl.BlockSpec(memory_space=pltpu.VMEM)
    res = pl.pallas_call(
        body, name="adamw_small",
        in_specs=[pl.BlockSpec(memory_space=pltpu.SMEM)] + [vm] * 29, out_specs=[vm] * 37,
        out_shape=[jax.ShapeDtypeStruct(shapes[n], F32) for n in SMALL for _ in range(4)]
        + [jax.ShapeDtypeStruct((PACK_ROWS, 128), F32)],
        compiler_params=_params(),
    )(chip, gathered, first, *flat(w), *flat(m), *flat(v))
    new = {n: tuple(r.reshape(w[n].shape) for r in res[4 * i:4 * i + 4]) for i, n in enumerate(SMALL)}
    return new, res[36][LOSS_ROW, 0]


def kernel(x, norm_mix_pre, w_in, b_gate, ln_v_g, ln_v_b, w_s, b_s, w_a_proj, w_b_proj, w_out, norm_mix_post, norm_ffn_pre, w_ff1, w_ff2, norm_ffn_post, loss_target, m_norm_mix_pre, m_w_in, m_b_gate, m_ln_v_g, m_ln_v_b, m_w_s, m_b_s, m_w_a_proj, m_w_b_proj, m_w_out, m_norm_mix_post, m_norm_ffn_pre, m_w_ff1, m_w_ff2, m_norm_ffn_post, v_norm_mix_pre, v_w_in, v_b_gate, v_ln_v_g, v_ln_v_b, v_w_s, v_b_s, v_w_a_proj, v_w_b_proj, v_w_out, v_norm_mix_post, v_norm_ffn_pre, v_w_ff1, v_w_ff2, v_norm_ffn_post):
    w = dict(norm_mix_pre=norm_mix_pre, w_in=w_in, b_gate=b_gate, ln_v_g=ln_v_g, ln_v_b=ln_v_b, w_s=w_s, b_s=b_s,
             w_a_proj=w_a_proj, w_b_proj=w_b_proj, w_out=w_out, norm_mix_post=norm_mix_post,
             norm_ffn_pre=norm_ffn_pre, w_ff1=w_ff1, w_ff2=w_ff2, norm_ffn_post=norm_ffn_post)
    m = dict(norm_mix_pre=m_norm_mix_pre, w_in=m_w_in, b_gate=m_b_gate, ln_v_g=m_ln_v_g, ln_v_b=m_ln_v_b, w_s=m_w_s,
             b_s=m_b_s, w_a_proj=m_w_a_proj, w_b_proj=m_w_b_proj, w_out=m_w_out, norm_mix_post=m_norm_mix_post,
             norm_ffn_pre=m_norm_ffn_pre, w_ff1=m_w_ff1, w_ff2=m_w_ff2, norm_ffn_post=m_norm_ffn_post)
    v = dict(norm_mix_pre=v_norm_mix_pre, w_in=v_w_in, b_gate=v_b_gate, ln_v_g=v_ln_v_g, ln_v_b=v_ln_v_b, w_s=v_w_s,
             b_s=v_b_s, w_a_proj=v_w_a_proj, w_b_proj=v_w_b_proj, w_out=v_w_out, norm_mix_post=v_norm_mix_post,
             norm_ffn_pre=v_norm_ffn_pre, w_ff1=v_w_ff1, w_ff2=v_w_ff2, norm_ffn_post=v_norm_ffn_post)
    chip = 2 * lax.axis_index("x") + lax.axis_index("y")
    core = lax.axis_index("c")

    where = jnp.stack([chip, core]).astype(jnp.int32)
    wg_in = place_shard("place_w_in", w_in[0], where, BF16, 256)
    bg_all = place_shard("place_b_gate", jnp.pad(b_gate[0], ((0, 14), (0, 0))), where, F32, 16)
    vecs = (norm_mix_pre, ln_v_g, ln_v_b, norm_mix_post, norm_ffn_pre, norm_ffn_post)
    loss, dx, _, small, parts, got, packed = local_step(
        x[0], loss_target[0], vecs, w_s[0], b_s[0].T, bg_all, wg_in, [w[n][0] for n in BIG[1:]],
        core=jnp.reshape(core, (1,)).astype(jnp.int32),
        order=jnp.stack([chip, chip ^ 2, chip ^ 1, chip ^ 3]).astype(jnp.int32), where=where)

    halves = [sum_chips("sum_" + n, p, r, where, min(p.shape[1], 256)) for n, p, r in zip(BIG, parts, got)]
    grads = dict(zip(BIG, join_halves(halves)))

    first = pack_vector(small["norm_mix_pre"], where)
    new = {}
    for n in BIG:
        shape = w[n].shape
        res = adamw("adamw_" + n, w[n][0], grads[n], m[n][0], v[n][0], min(shape[1], 256),
                    first if n == "w_in" else None)
        new[n] = tuple(r.reshape(shape) for r in res[:4])
        first = res[4] if n == "w_in" else first
    small_new, loss = adamw_small(packed, first, jnp.reshape(chip, (1,)).astype(jnp.int32), w, m, v)
    new.update(small_new)

    outs = [loss, dx[None]]
    for i in range(4):
        outs += [new[n][i] for n in ORDER]
    return tuple(outs)
```

```python
import functools
import math
import typing

import numpy as np
import jax
import jax.numpy as jnp
from jax import lax
from jax.experimental import pallas as pl
from jax.experimental.pallas import tpu as pltpu

F32 = jnp.float32
BF16 = jnp.bfloat16
MESH = pl.DeviceIdType.MESH

D = 1024
EPS = 1e-6
CHUNK = 128
GROUPS = 8
HEADS = 16
HEAD_DIM = 64
ATT_T = 256
ATT_GROUP = 8
ATT_BWD_GROUP = 2
N_CHIPS = 4
D_FF = 4 * D
IN_COLS = 7 * D
IN_SHARD = IN_COLS // N_CHIPS
MASKED = -1e30
VMEM_LIMIT = 56 * 2 ** 20

ADAM_LR, ADAM_B1, ADAM_B2, ADAM_EPS, ADAM_WD, ADAM_STEP = 0.001, 0.9, 0.999, 1e-08, 0.01, 10

NN = (((1,), (0,)), ((), ()))
NT = (((1,), (1,)), ((), ()))
TN = (((0,), (0,)), ((), ()))


def _dot(a, b, dims=NN):
    return lax.dot_general(a, b, dims, preferred_element_type=F32)


def _params(*sem, communicates=False):
    return pltpu.CompilerParams(dimension_semantics=sem or None, vmem_limit_bytes=VMEM_LIMIT,
                                has_side_effects=communicates)


def _rows(tr, c, col=0):
    return pl.BlockSpec((tr, c), lambda i: (i, col))


def _full(shape):
    n = len(shape)
    return pl.BlockSpec(shape, lambda *_: (0,) * n)


def _gelu(x):
    k = math.sqrt(2.0 / math.pi)
    return 0.5 * x * (1.0 + jnp.tanh(k * (x + 0.044715 * x * x * x)))


def _gelu_and_grad(x):
    k = math.sqrt(2.0 / math.pi)
    t = jnp.tanh(k * (x + 0.044715 * x * x * x))
    g = 0.5 * x * (1.0 + t)
    dg = 0.5 * (1.0 + t) + 0.5 * x * (1.0 - t * t) * (k * (1.0 + 3.0 * 0.044715 * x * x))
    return g, dg


def _sigmoid(x):
    return 1.0 / (1.0 + jnp.exp(-x))


def _rms(x):
    r = lax.rsqrt(jnp.mean(x * x, axis=-1, keepdims=True) + EPS)
    return x * r, r


def _rms_bwd(dn, xhat, r):
    return r * (dn - xhat * jnp.mean(dn * xhat, axis=-1, keepdims=True))


def norm_pre(x, g):
    s = x.shape[0]
    tr = 512

    def body(x_ref, g_ref, h_ref):
        xhat, _ = _rms(x_ref[...])
        h_ref[...] = (xhat * g_ref[...]).astype(BF16)

    return pl.pallas_call(
        body, name="norm_pre", grid=(s // tr,),
        in_specs=[_rows(tr, D), _full((1, D))], out_specs=_rows(tr, D),
        out_shape=jax.ShapeDtypeStruct((s, D), BF16), compiler_params=_params("parallel"),
    )(x, g)


def mm_in(name, h, wg, order, first, count, z, gathering, relay=False, casting=()):
    s = h.shape[0]
    tm, tn = 1024, IN_SHARD // 2
    per = IN_SHARD // tn
    n, m = len(gathering), len(casting)
    nj, ni = count * per, s // tm
    has_z = z is not None
    arrays = _arrays(gathering)
    at = [k for k, a in enumerate(arrays) if a is wg][0]

    def body(order_ref, *refs):
        a_ref = refs[0]
        cast_in = refs[1 + has_z + n:1 + has_z + n + m]
        o_ref = refs[1 + has_z + n + m]
        held = refs[2 + has_z + n + m:2 + has_z + 2 * n + m]
        cast_out = refs[2 + has_z + 2 * n + m:2 + has_z + 2 * n + 2 * m]
        tile, tile_sem = refs[2 + has_z + 2 * n + 2 * m:4 + has_z + 2 * n + 2 * m]
        j, i = pl.program_id(0), pl.program_id(1)
        sems = refs[4 + has_z + 2 * n + 2 * m:]
        for src, dst in zip(cast_in, cast_out):
            dst[...] = src[...].astype(BF16)
        phases = [_gather_phases(held, *sems[:2], _spans(gathering))]
        if relay:
            phases.append(_relay_phases(held[at], *sems[2:]))
        def each(fs):
            def run():
                for f in fs:
                    f()
            return run

        send, pass_on, finish = [each(fs) for fs in zip(*phases)]

        def fetch(t):
            chip = order_ref[first + t // per]
            return pltpu.make_async_copy(held[at].at[chip, :, pl.ds((t % per) * tn, tn)], tile.at[t % 2],
                                         tile_sem.at[t % 2])

        @pl.when(i == 0)
        def _():
            @pl.when(j == 0)
            def _():
                send()
                fetch(0).start()

            fetch(j).wait()

            @pl.when(j + 1 < nj)
            def _():
                fetch(j + 1).start()

        pl.when((j == nj - 1) & (i == ni - 1))(pass_on)
        rows = pl.ds(pl.multiple_of(i * tm, tm), tm)
        o_ref[...] = _dot(a_ref[rows, :], tile[j % 2]).astype(BF16)
        pl.when((j == nj - 1) & (i == ni - 1))(finish)

    steps = nj * ni
    out = pl.pallas_call(
        body, name=name,
        grid_spec=pltpu.PrefetchScalarGridSpec(
            num_scalar_prefetch=1, grid=(nj, ni),
            in_specs=[pl.BlockSpec((s, D), lambda j, i, o: (0, 0))] + [ANY] * (has_z + n)
            + [pl.BlockSpec((a.shape[0] // steps, a.shape[1]), lambda j, i, o: (j * ni + i, 0)) for a in casting],
            out_specs=[pl.BlockSpec((tm, tn), lambda j, i, o: (i, o[first + j // per] * per + j % per))] + [ANY] * n
            + [pl.BlockSpec((None, a.shape[0] // steps, a.shape[1]), lambda j, i, o: (o[0], j * ni + i, 0))
               for a in casting],
            scratch_shapes=[pltpu.VMEM((2, D, tn), BF16), pltpu.SemaphoreType.DMA((2,))] + _gather_sems(n)
            + (_relay_sems() if relay else [])),
        out_shape=[jax.ShapeDtypeStruct((s, IN_COLS), BF16)] + [jax.ShapeDtypeStruct(a.shape, a.dtype) for a in arrays]
        + [jax.ShapeDtypeStruct((N_CHIPS,) + a.shape, BF16) for a in casting],
        input_output_aliases={**({2: 0} if has_z else {}), **{2 + has_z + w: 1 + w for w in range(n)}},
        compiler_params=_params("arbitrary", "arbitrary", communicates=True),
    )(order, h, *([z] if has_z else []), *arrays, *casting)
    return out[0], out[1:1 + n], out[1 + n:]


def _tril_ws(ws_ref, g):
    r = lax.broadcasted_iota(jnp.int32, (CHUNK, CHUNK), 0)
    c = lax.broadcasted_iota(jnp.int32, (CHUNK, CHUNK), 1)
    return jnp.where(c <= r, ws_ref[g], 0.0).astype(BF16)


def _layer_norm(v):
    mu = jnp.mean(v, axis=-1, keepdims=True)
    d = v - mu
    rstd = lax.rsqrt(jnp.mean(d * d, axis=-1, keepdims=True) + EPS)
    return d * rstd, rstd


def gating_fwd(z, ln_g, ln_b, w_s, bs_t, gathering):
    s = z.shape[0]
    n = len(gathering)
    steps = s // CHUNK

    def body(*refs):
        u_ref, v_ref, lg_ref, lb_ref, ws_ref, bst_ref = refs[:6]
        ya_ref = refs[6 + n]
        ci = pl.program_id(0)
        if n:
            send, pass_on, finish = _gather_phases(refs[7 + n:7 + 2 * n], *refs[7 + 2 * n:], _spans(gathering))
            pl.when(ci == 0)(send)
            pl.when(ci == steps - 1)(pass_on)
        ug = _gelu(u_ref[...].astype(F32))
        vhat, _ = _layer_norm(_gelu(v_ref[...].astype(F32)))
        vn = (vhat * lg_ref[...] + lb_ref[...]).astype(BF16)
        for g in range(GROUPS):
            cols = slice(g * CHUNK, (g + 1) * CHUNK)
            mixed = _dot(_tril_ws(ws_ref, g), vn[:, cols]) + bst_ref[:, g:g + 1]
            ya_ref[:, cols] = (ug[:, cols] * mixed).astype(BF16)
        if n:
            pl.when(ci == steps - 1)(finish)

    out = pl.pallas_call(
        body, name="gating_fwd", grid=(steps,),
        in_specs=[_rows(CHUNK, D, 0), _rows(CHUNK, D, 1), _full((1, D)), _full((1, D)),
                  _full((GROUPS, CHUNK, CHUNK)), _full((CHUNK, GROUPS))] + [ANY] * n,
        out_specs=[_rows(CHUNK, D)] + [ANY] * n,
        out_shape=[jax.ShapeDtypeStruct((s, D), BF16)]
        + [jax.ShapeDtypeStruct(a.shape, a.dtype) for a in _arrays(gathering)],
        input_output_aliases={6 + w: 1 + w for w in range(n)},
        scratch_shapes=_gather_sems(n) if n else [],
        compiler_params=_params("arbitrary", communicates=bool(n)),
    )(z, z, ln_g, ln_b, w_s, bs_t, *_arrays(gathering))
    return out[0], out[1:]


def _attn_tables(s):
    nd = s // ATT_T
    r = np.arange(ATT_T)[None, :, None]
    c = np.arange(ATT_T)[None, None, :]
    delta = np.arange(nd)[:, None, None] * ATT_T + r - c
    count = np.zeros(delta.shape, np.int64)
    for window, dilation in ((128, 1), (512, 4), (2048, 16)):
        count += (delta >= 0) & (delta % dilation == 0) & (delta <= window)
    logc = np.where(count > 0, np.log(np.maximum(count, 1)), MASKED)
    return jnp.asarray(logc, F32)


AUG = 3


def _split3_np(x):
    terms, rest = [], np.asarray(x, np.float64)
    for _ in range(AUG):
        term = np.asarray(rest.astype(jnp.bfloat16), np.float64)
        terms.append(term)
        rest = rest - term
    return terms


def _split3(x):
    terms, rest = [], x
    for _ in range(AUG):
        term = rest.astype(BF16).astype(F32)
        terms.append(term)
        rest = rest - term
    return terms


def _alibi_tables(s):
    nb = s // ATT_T
    slopes = np.exp2(-8.0 * np.arange(1, HEADS + 1, dtype=np.float64) / HEADS)
    ka = np.zeros((HEADS // 2, 2, ATT_T, 128), np.float32)
    kb = np.zeros((HEADS // 2, 2, nb, 128), np.float32)
    for p in range(HEADS // 2):
        for e in range(2):
            base = HEAD_DIM * (1 - e)
            for a, term in enumerate(_split3_np(slopes[2 * p + e] * np.arange(ATT_T))):
                ka[p, e, :, base + a] = term
            for a, term in enumerate(_split3_np(slopes[2 * p + e] * ATT_T * np.arange(nb))):
                kb[p, e, :, base + AUG + a] = term
            ka[p, e, :, base + 2 * AUG:base + 3 * AUG] = 1.0
    return jnp.asarray(ka), jnp.asarray(kb)


def _head_masks():
    lane = lax.broadcasted_iota(jnp.int32, (1, 128), 1)
    first = lane < HEAD_DIM

    def ones(e, n):
        base = HEAD_DIM * (1 - e)
        return ((lane >= base) & (lane < base + n)).astype(F32)

    return first, lane, ones


def _place3(lane, at, terms, other):
    for a, term in enumerate(terms):
        other = jnp.where(lane == at + a, term, other)
    return other


def attn_fwd(z, logc, ka, kb, gathering):
    s = z.shape[0]
    nq = s // ATT_T
    t = ATT_T
    n = len(gathering)
    grp = ATT_GROUP
    ngrp = HEADS // 2 // grp
    wide = 128 * grp
    qcol, kcol, vcol = 2 * D // wide, 3 * D // wide, 4 * D // wide

    def body(*refs):
        q_ref, k_ref, v_ref, lc_ref, ka_ref, kb_ref = refs[:6]
        y_ref, lse_ref = refs[6 + n:8 + n]
        q_s, k_s, v_s, m_s, l_s, acc_s = refs[8 + 2 * n:14 + 2 * n]
        gi, qi = pl.program_id(0), pl.program_id(1)
        first, lane, ones = _head_masks()
        if n:
            send, pass_on, finish = _gather_phases(refs[8 + n:8 + 2 * n], *refs[14 + 2 * n:], _spans(gathering))
            pl.when((gi == 0) & (qi == 0))(send)
            pl.when((gi == ngrp - 1) & (qi == nq - 1))(pass_on)

        @pl.when(qi == 0)
        def _():
            sel = jnp.broadcast_to(first.astype(F32), (t, 128))
            for pr in range(grp):
                cols = slice(pr * 128, (pr + 1) * 128)
                for jb in range(nq):
                    kj = k_ref[jb * t:(jb + 1) * t, cols].astype(F32)
                    vj = v_ref[jb * t:(jb + 1) * t, cols].astype(F32)
                    k_s[pr, 0, jb] = jnp.where(first, kj, ka_ref[pr, 0] + kb_ref[pr, 0, jb:jb + 1, :]).astype(BF16)
                    k_s[pr, 1, jb] = jnp.where(first, ka_ref[pr, 1] + kb_ref[pr, 1, jb:jb + 1, :], kj).astype(BF16)
                    v_s[pr, jb, 0:t, 0:128] = jnp.where(first, vj, 0.0).astype(BF16)
                    v_s[pr, jb, t:2 * t, 0:128] = jnp.where(first, 0.0, vj).astype(BF16)
                    v_s[pr, jb, 0:t, 128:256] = sel.astype(BF16)
                    v_s[pr, jb, t:2 * t, 128:256] = (1.0 - sel).astype(BF16)

        for pr in range(grp):
            q = q_ref[:, pr * 128:(pr + 1) * 128].astype(F32) * (1.0 / math.sqrt(HEAD_DIM))
            q_s[pr, 0] = jnp.where(first, q, ones(0, 2 * AUG)).astype(BF16)
            q_s[pr, 1] = jnp.where(first, ones(1, 2 * AUG), q).astype(BF16)
        m_s[...] = jnp.full_like(m_s, MASKED)
        l_s[...] = jnp.zeros_like(l_s)
        acc_s[...] = jnp.zeros_like(acc_s)

        def scores(j):
            return tuple(_dot(q_s[pr, e], k_s[pr, e, j], NT) for pr in range(grp) for e in range(2))

        def step(j, carry):
            softmax_block(j, scores(j))
            return carry

        def softmax_block(j, u):
            lc = lc_ref[qi - j]
            for pr in range(grp):
                u0 = u[2 * pr] + lc
                u1 = u[2 * pr + 1] + lc
                m0, m1 = m_s[pr, 0], m_s[pr, 1]
                n0 = jnp.maximum(m0, jnp.max(u0, axis=-1, keepdims=True))
                n1 = jnp.maximum(m1, jnp.max(u1, axis=-1, keepdims=True))
                m_s[pr, 0], m_s[pr, 1] = n0, n1
                p = jnp.concatenate([jnp.exp(u0 - jnp.concatenate([n0, n0], axis=1)).astype(BF16),
                                     jnp.exp(u1 - jnp.concatenate([n1, n1], axis=1)).astype(BF16)], axis=1)
                pv = _dot(p, v_s[pr, j])
                alpha = jnp.where(first, jnp.exp(m0 - n0), jnp.exp(m1 - n1))
                acc_s[pr] = acc_s[pr] * alpha + pv[:, 0:128]
                l_s[pr] = l_s[pr] * alpha + pv[:, 128:256]

        lax.fori_loop(0, qi + 1, step, 0)
        for pr in range(grp):
            cols = slice(pr * 128, (pr + 1) * 128)
            y_ref[:, cols] = (acc_s[pr] / l_s[pr]).astype(BF16)
            lse_ref[:, cols] = jnp.where(first, m_s[pr, 0], m_s[pr, 1]) + jnp.log(l_s[pr])
        if n:
            pl.when((gi == ngrp - 1) & (qi == nq - 1))(finish)

    out = pl.pallas_call(
        body, name="attn_fwd", grid=(ngrp, nq),
        in_specs=[pl.BlockSpec((t, wide), lambda g, i: (i, qcol + g)),
                  pl.BlockSpec((s, wide), lambda g, i: (0, kcol + g)),
                  pl.BlockSpec((s, wide), lambda g, i: (0, vcol + g)),
                  _full((nq, t, t)),
                  pl.BlockSpec((grp, 2, t, 128), lambda g, i: (g, 0, 0, 0)),
                  pl.BlockSpec((grp, 2, nq, 128), lambda g, i: (g, 0, 0, 0))] + [ANY] * n,
        out_specs=[pl.BlockSpec((t, wide), lambda g, i: (i, g)), pl.BlockSpec((t, wide), lambda g, i: (i, g))]
        + [ANY] * n,
        out_shape=[jax.ShapeDtypeStruct((s, D), BF16), jax.ShapeDtypeStruct((s, D), F32)]
        + [jax.ShapeDtypeStruct(a.shape, a.dtype) for a in _arrays(gathering)],
        input_output_aliases={6 + w: 2 + w for w in range(n)},
        scratch_shapes=[pltpu.VMEM((grp, 2, t, 128), BF16), pltpu.VMEM((grp, 2, nq, t, 128), BF16),
                        pltpu.VMEM((grp, nq, 2 * t, 256), BF16), pltpu.VMEM((grp, 2, t, 128), F32),
                        pltpu.VMEM((grp, t, 128), F32), pltpu.VMEM((grp, t, 128), F32)]
        + (_gather_sems(n) if n else []),
        compiler_params=_params("arbitrary", "arbitrary", communicates=bool(n)),
    )(z, z, z, logc, ka, kb, *_arrays(gathering))
    return out[0], out[1], out[2:]


def proj_merge(ya, yb, wa, wb, z, bg, gathering):
    s = ya.shape[0]
    tm = 512
    n = len(gathering)
    steps = s // tm

    def body(*refs):
        ya_ref, yb_ref, wa_ref, wb_ref, ga_ref, gb_ref, bg_ref = refs[:7]
        mg_ref, pa_ref, pb_ref = refs[7 + n:10 + n]
        i = pl.program_id(0)
        if n:
            send, pass_on, finish = _gather_phases(refs[10 + n:10 + 2 * n], *refs[10 + 2 * n:], _spans(gathering))
            pl.when(i == 0)(send)
            pl.when(i == steps - 1)(pass_on)
        pa = _dot(ya_ref[...], wa_ref[...])
        pb = _dot(yb_ref[...], wb_ref[...])
        sa = _sigmoid(ga_ref[...] + bg_ref[0:1, :])
        sb = _sigmoid(gb_ref[...] + bg_ref[1:2, :])
        mg_ref[...] = (sa * pa + sb * pb).astype(BF16)
        pa_ref[...] = pa.astype(BF16)
        pb_ref[...] = pb.astype(BF16)
        if n:
            pl.when(i == steps - 1)(finish)

    out = jax.ShapeDtypeStruct((s, D), BF16)
    res = pl.pallas_call(
        body, name="proj_merge", grid=(steps,),
        in_specs=[_rows(tm, D), _rows(tm, D), _full((D, D)), _full((D, D)),
                  _rows(tm, D, 5), _rows(tm, D, 6), _full((2, D))] + [ANY] * n,
        out_specs=[_rows(tm, D)] * 3 + [ANY] * n,
        out_shape=[out] * 3 + [jax.ShapeDtypeStruct(a.shape, a.dtype) for a in _arrays(gathering)],
        input_output_aliases={7 + w: 3 + w for w in range(n)},
        scratch_shapes=_gather_sems(n) if n else [],
        compiler_params=_params("arbitrary", communicates=bool(n)),
    )(ya, yb, wa, wb, z, z, bg, *_arrays(gathering))
    return res[0], res[1], res[2], res[3:]


def out_norm(merged, w_out, x, g_post, g_fpre, gathering):
    s = x.shape[0]
    tm = 512
    n = len(gathering)
    steps = s // tm

    def body(*refs):
        mg_ref, w_ref, x_ref, gp_ref, gf_ref = refs[:5]
        o_ref, x1_ref, h2_ref = refs[5 + n:8 + n]
        i = pl.program_id(0)
        if n:
            send, pass_on, finish = _gather_phases(refs[8 + n:8 + 2 * n], *refs[8 + 2 * n:], _spans(gathering))
            pl.when(i == 0)(send)
            pl.when(i == steps - 1)(pass_on)
        o = _dot(mg_ref[...], w_ref[...])
        ohat, _ = _rms(o)
        x1 = x_ref[...] + ohat * gp_ref[...]
        x1hat, _ = _rms(x1)
        o_ref[...] = o
        x1_ref[...] = x1
        h2_ref[...] = (x1hat * gf_ref[...]).astype(BF16)
        if n:
            pl.when(i == steps - 1)(finish)

    res = pl.pallas_call(
        body, name="out_norm", grid=(steps,),
        in_specs=[_rows(tm, D), _full((D, D)), _rows(tm, D), _full((1, D)), _full((1, D))] + [ANY] * n,
        out_specs=[_rows(tm, D)] * 3 + [ANY] * n,
        out_shape=[jax.ShapeDtypeStruct((s, D), F32), jax.ShapeDtypeStruct((s, D), F32),
                   jax.ShapeDtypeStruct((s, D), BF16)]
        + [jax.ShapeDtypeStruct(a.shape, a.dtype) for a in _arrays(gathering)],
        input_output_aliases={5 + w: 3 + w for w in range(n)},
        scratch_shapes=_gather_sems(n) if n else [],
        compiler_params=_params("arbitrary", communicates=bool(n)),
    )(merged, w_out, x, g_post, g_fpre, *_arrays(gathering))
    return res[0], res[1], res[2], res[3:]


def mm_ff1(h2, wg, gathering):
    s = h2.shape[0]
    tm = 1024
    n = len(gathering)
    ni = s // tm

    def body(*refs):
        a_ref, b_ref = refs[:2]
        o_ref, r_ref = refs[2 + n:4 + n]
        i, j = pl.program_id(0), pl.program_id(1)
        if n:
            send, pass_on, finish = _gather_phases(refs[4 + n:4 + 2 * n], *refs[4 + 2 * n:], _spans(gathering))
            pl.when((i == 0) & (j == 0))(send)
            pl.when((i == ni - 1) & (j == N_CHIPS // 2))(pass_on)
        a = _dot(a_ref[...], b_ref[...])
        o_ref[...] = a.astype(BF16)
        r = jnp.maximum(a, 0.0)
        r_ref[...] = (r * r).astype(BF16)
        if n:
            pl.when((i == ni - 1) & (j == N_CHIPS - 1))(finish)

    res = pl.pallas_call(
        body, name="mm_ff1", grid=(ni, N_CHIPS),
        in_specs=[pl.BlockSpec((tm, D), lambda i, j: (i, 0)), pl.BlockSpec((None, D, D), lambda i, j: (j, 0, 0))]
        + [ANY] * n,
        out_specs=[pl.BlockSpec((tm, D), lambda i, j: (i, j))] * 2 + [ANY] * n,
        out_shape=[jax.ShapeDtypeStruct((s, D_FF), BF16), jax.ShapeDtypeStruct((s, D_FF), BF16)]
        + [jax.ShapeDtypeStruct(a.shape, a.dtype) for a in _arrays(gathering)],
        input_output_aliases={2 + w: 2 + w for w in range(n)},
        scratch_shapes=_gather_sems(n) if n else [],
        compiler_params=_params("arbitrary", "arbitrary", communicates=bool(n)),
    )(h2, wg, *_arrays(gathering))
    return res[0], res[1], res[2:]


def ff2_loss(rl, w_ff2, x1, target, g_fpost):
    s = x1.shape[0]
    tm = 256

    def body(rl_ref, w_ref, x1_ref, t_ref, g_ref, dy_ref, df_ref, dg_ref, loss_ref):
        @pl.when(pl.program_id(0) == 0)
        def _():
            dg_ref[...] = jnp.zeros_like(dg_ref)
            loss_ref[...] = jnp.zeros_like(loss_ref)

        f = _dot(rl_ref[...], w_ref[...])
        fhat, r = _rms(f)
        err = x1_ref[...] + fhat * g_ref[...] - t_ref[...]
        loss_ref[...] += 0.5 * jnp.sum(jnp.mean(err * err, axis=-1, keepdims=True), axis=0, keepdims=True)
        dy = err * (1.0 / D)
        dy_ref[...] = dy
        dg_ref[...] += jnp.sum(dy * fhat, axis=0, keepdims=True)
        df_ref[...] = _rms_bwd(dy * g_ref[...], fhat, r).astype(BF16)

    return pl.pallas_call(
        body, name="ff2_loss", grid=(s // tm,),
        in_specs=[_rows(tm, D_FF), _full((D_FF, D)), _rows(tm, D), _rows(tm, D), _full((1, D))],
        out_specs=[_rows(tm, D), _rows(tm, D), _full((1, D)), _full((1, 1))],
        out_shape=[jax.ShapeDtypeStruct((s, D), F32), jax.ShapeDtypeStruct((s, D), BF16),
                   jax.ShapeDtypeStruct((1, D), F32), jax.ShapeDtypeStruct((1, 1), F32)],
        compiler_params=_params("arbitrary"),
    )(rl, w_ff2, x1, target, g_fpost)


def mm_tn(name, a, b, ta, tb, out_shape, out_spec):
    s = a.shape[0]

    def body(a_ref, b_ref, o_ref):
        o_ref[...] = _dot(a_ref[...], b_ref[...], TN)

    return pl.pallas_call(
        body, name=name, grid=(a.shape[1] // ta, b.shape[1] // tb),
        in_specs=[pl.BlockSpec((s, ta), lambda i, j: (0, i)), pl.BlockSpec((s, tb), lambda i, j: (0, j))],
        out_specs=out_spec, out_shape=jax.ShapeDtypeStruct(out_shape, F32),
        compiler_params=_params("parallel", "parallel"),
    )(a, b)


def mm_nt(name, a, w):
    s = a.shape[0]
    tm = 512

    def body(a_ref, w_ref, o_ref):
        o_ref[...] = _dot(a_ref[...], w_ref[...], NT).astype(BF16)

    return pl.pallas_call(
        body, name=name, grid=(s // tm,), in_specs=[_rows(tm, D), _full((D, D))], out_specs=_rows(tm, D),
        out_shape=jax.ShapeDtypeStruct((s, D), BF16), compiler_params=_params("parallel"),
    )(a, w)


def ff2_bwd(df, w_ff2, a):
    s = df.shape[0]
    tm = 1024

    def body(df_ref, w_ref, a_ref, da_ref):
        drl = _dot(df_ref[...], w_ref[...], NT)
        da_ref[...] = (drl * (2.0 * jnp.maximum(a_ref[...].astype(F32), 0.0))).astype(BF16)

    return pl.pallas_call(
        body, name="ff2_bwd", grid=(s // tm, D_FF // D),
        in_specs=[pl.BlockSpec((tm, D), lambda i, j: (i, 0)), pl.BlockSpec((D, D), lambda i, j: (j, 0)),
                  pl.BlockSpec((tm, D), lambda i, j: (i, j))],
        out_specs=pl.BlockSpec((tm, D), lambda i, j: (i, j)),
        out_shape=jax.ShapeDtypeStruct((s, D_FF), BF16), compiler_params=_params("parallel", "parallel"),
    )(df, w_ff2, a)


def ff1_bwd_norms(da, wg, x1, o, dy, g_fpre, g_post, swapping):
    s = x1.shape[0]
    tm = 512
    n = len(swapping)

    def body(*refs):
        da_ref, w_ref, x1_ref, o_ref, dy_ref, gf_ref, gp_ref = refs[:7]
        dx1_ref, do_ref, dgf_ref, dgp_ref = refs[7 + n:11 + n]
        acc_ref = refs[11 + 2 * n]
        i, k = pl.program_id(0), pl.program_id(1)
        if n:
            send, finish = _swap_phases(refs[7:7 + n], refs[11 + n:11 + 2 * n], *refs[12 + 2 * n:])
            pl.when((i == 0) & (k == 0))(send)

        @pl.when((i == 0) & (k == 0))
        def _():
            dgf_ref[...] = jnp.zeros_like(dgf_ref)
            dgp_ref[...] = jnp.zeros_like(dgp_ref)

        part = _dot(da_ref[...], w_ref[...], NT)

        @pl.when(k == 0)
        def _():
            acc_ref[...] = part

        @pl.when(k > 0)
        def _():
            acc_ref[...] += part

        @pl.when(k == N_CHIPS - 1)
        def _():
            dh2 = acc_ref[...]
            x1hat, r2 = _rms(x1_ref[...])
            dgf_ref[...] += jnp.sum(dh2 * x1hat, axis=0, keepdims=True)
            dx1 = dy_ref[...] + _rms_bwd(dh2 * gf_ref[...], x1hat, r2)
            ohat, r1 = _rms(o_ref[...])
            dgp_ref[...] += jnp.sum(dx1 * ohat, axis=0, keepdims=True)
            dx1_ref[...] = dx1
            do_ref[...] = _rms_bwd(dx1 * gp_ref[...], ohat, r1).astype(BF16)

        if n:
            pl.when((i == s // tm - 1) & (k == N_CHIPS - 1))(finish)

    row = pl.BlockSpec((tm, D), lambda i, k: (i, 0))
    vec = pl.BlockSpec((1, D), lambda i, k: (0, 0))
    res = pl.pallas_call(
        body, name="ff1_bwd_norms", grid=(s // tm, N_CHIPS),
        in_specs=[pl.BlockSpec((tm, D), lambda i, k: (i, k)), pl.BlockSpec((None, D, D), lambda i, k: (k, 0, 0)),
                  row, row, row, vec, vec] + [ANY] * n,
        out_specs=[row, row, vec, vec] + [ANY] * n,
        out_shape=[jax.ShapeDtypeStruct((s, D), F32), jax.ShapeDtypeStruct((s, D), BF16),
                   jax.ShapeDtypeStruct((1, D), F32), jax.ShapeDtypeStruct((1, D), F32)] + _swap_shapes(swapping),
        scratch_shapes=[pltpu.VMEM((tm, D), F32)] + (_swap_sems(n) if n else []),
        compiler_params=_params("arbitrary", "arbitrary", communicates=bool(n)),
    )(da, wg, x1, o, dy, g_fpre, g_post, *swapping)
    return res[0], res[1], res[2], res[3], res[4:]


def out_bwd_gates(do, w_out, pa, pb, z, bg):
    s = do.shape[0]
    tm = 512

    def body(do_ref, w_ref, pa_ref, pb_ref, ga_ref, gb_ref, bg_ref, dpa_ref, dpb_ref, dga_ref, dgb_ref, dbg_ref):
        @pl.when(pl.program_id(0) == 0)
        def _():
            dbg_ref[...] = jnp.zeros_like(dbg_ref)

        dm = _dot(do_ref[...], w_ref[...], NT)
        sa = _sigmoid(ga_ref[...] + bg_ref[0:1, :])
        sb = _sigmoid(gb_ref[...] + bg_ref[1:2, :])
        dpa_ref[...] = (dm * sa).astype(BF16)
        dpb_ref[...] = (dm * sb).astype(BF16)
        dga = dm * pa_ref[...].astype(F32) * (sa * (1.0 - sa))
        dgb = dm * pb_ref[...].astype(F32) * (sb * (1.0 - sb))
        dga_ref[...] = dga.astype(BF16)
        dgb_ref[...] = dgb.astype(BF16)
        dbg_ref[0:1, :] += jnp.sum(dga, axis=0, keepdims=True)
        dbg_ref[1:2, :] += jnp.sum(dgb, axis=0, keepdims=True)

    out = jax.ShapeDtypeStruct((s, D), BF16)
    return pl.pallas_call(
        body, name="out_bwd_gates", grid=(s // tm,),
        in_specs=[_rows(tm, D), _full((D, D)), _rows(tm, D), _rows(tm, D), _rows(tm, D, 5), _rows(tm, D, 6),
                  _full((2, D))],
        out_specs=[_rows(tm, D)] * 4 + [_full((2, D))],
        out_shape=[out] * 4 + [jax.ShapeDtypeStruct((2, D), F32)], compiler_params=_params("arbitrary"),
    )(do, w_out, pa, pb, z, z, bg)


def gating_bwd(z, dya, ln_g, ln_b, w_s, bs_t, swapping):
    s = z.shape[0]
    ones = functools.partial(jnp.ones, (8, CHUNK), BF16)
    n = len(swapping)

    def body(*refs):
        u_ref, v_ref, dya_ref, lg_ref, lb_ref, ws_ref, bst_ref = refs[:7]
        du_ref, dv_ref, dws_ref, dbs_ref, dlg_ref, dlb_ref = refs[7 + n:13 + n]
        dvn_ref = refs[13 + 2 * n]
        ci = pl.program_id(0)
        if n:
            send, finish = _swap_phases(refs[7:7 + n], refs[13 + n:13 + 2 * n], *refs[14 + 2 * n:])
            pl.when(ci == 0)(send)

        @pl.when(ci == 0)
        def _():
            dws_ref[...] = jnp.zeros_like(dws_ref)
            dbs_ref[...] = jnp.zeros_like(dbs_ref)
            dlg_ref[...] = jnp.zeros_like(dlg_ref)
            dlb_ref[...] = jnp.zeros_like(dlb_ref)

        ug, dug_du = _gelu_and_grad(u_ref[...].astype(F32))
        vg, dvg_dv = _gelu_and_grad(v_ref[...].astype(F32))
        vhat, rstd = _layer_norm(vg)
        vn = (vhat * lg_ref[...] + lb_ref[...]).astype(BF16)
        dya = dya_ref[...].astype(F32)
        for g in range(GROUPS):
            cols = slice(g * CHUNK, (g + 1) * CHUNK)
            ws = _tril_ws(ws_ref, g)
            mixed = _dot(ws, vn[:, cols]) + bst_ref[:, g:g + 1]
            du_ref[:, cols] = (dya[:, cols] * mixed * dug_du[:, cols]).astype(BF16)
            dmix = (dya[:, cols] * ug[:, cols]).astype(BF16)
            dbs_ref[g] += _dot(ones(), dmix, NT)
            dws_ref[g] += _dot(dmix, vn[:, cols], NT)
            dvn_ref[:, cols] = _dot(ws, dmix, TN)
        dvn = dvn_ref[...]
        dlg_ref[...] += jnp.sum(dvn * vhat, axis=0, keepdims=True)
        dlb_ref[...] += jnp.sum(dvn, axis=0, keepdims=True)
        dvh = dvn * lg_ref[...]
        dvg = rstd * (dvh - jnp.mean(dvh, axis=-1, keepdims=True)
                      - vhat * jnp.mean(dvh * vhat, axis=-1, keepdims=True))
        dv_ref[...] = (dvg * dvg_dv).astype(BF16)

        @pl.when(ci == pl.num_programs(0) - 1)
        def _():
            r = lax.broadcasted_iota(jnp.int32, (CHUNK, CHUNK), 0)
            c = lax.broadcasted_iota(jnp.int32, (CHUNK, CHUNK), 1)
            for g in range(GROUPS):
                dws_ref[g] = jnp.where(c <= r, dws_ref[g], 0.0)

        if n:
            pl.when(ci == pl.num_programs(0) - 1)(finish)

    out = jax.ShapeDtypeStruct((s, D), BF16)
    res = pl.pallas_call(
        body, name="gating_bwd", grid=(s // CHUNK,),
        in_specs=[_rows(CHUNK, D, 0), _rows(CHUNK, D, 1), _rows(CHUNK, D), _full((1, D)), _full((1, D)),
                  _full((GROUPS, CHUNK, CHUNK)), _full((CHUNK, GROUPS))] + [ANY] * n,
        out_specs=[_rows(CHUNK, D), _rows(CHUNK, D), _full((GROUPS, CHUNK, CHUNK)), _full((GROUPS, 8, CHUNK)),
                   _full((1, D)), _full((1, D))] + [ANY] * n,
        out_shape=[out, out, jax.ShapeDtypeStruct((GROUPS, CHUNK, CHUNK), F32),
                   jax.ShapeDtypeStruct((GROUPS, 8, CHUNK), F32),
                   jax.ShapeDtypeStruct((1, D), F32), jax.ShapeDtypeStruct((1, D), F32)] + _swap_shapes(swapping),
        scratch_shapes=[pltpu.VMEM((CHUNK, D), F32)] + (_swap_sems(n) if n else []),
        compiler_params=_params("arbitrary", communicates=bool(n)),
    )(z, z, dya, ln_g, ln_b, w_s, bs_t, *swapping)
    return (*res[:6], res[6:])


def attn_bwd(z, yb, dyb, lse, logc, ka, kb, scattering, gathering=None):
    s = z.shape[0]
    nq = s // ATT_T
    t = ATT_T
    grp = ATT_BWD_GROUP
    ngrp = HEADS // 2 // grp
    wide = 128 * grp
    qcol, kcol, vcol = 2 * D // wide, 3 * D // wide, 4 * D // wide
    scale = 1.0 / math.sqrt(HEAD_DIM)
    n = len(scattering)
    g8 = 0 if gathering is None else 1

    def body(*refs):
        q_ref, k_ref, v_ref, y_ref, dy_ref, lse_ref, lc_ref, ka_ref, kb_ref = refs[:9]
        dq_ref, dk_ref, dv_ref = refs[9 + n + g8:12 + n + g8]
        qa_s, qt_s, da_s, dt_s, dq_s, dkt_s, dvt_s = refs[12 + 2 * n + 2 * g8:19 + 2 * n + 2 * g8]
        sems = refs[19 + 2 * n + 2 * g8:]
        gi, j = pl.program_id(0), pl.program_id(1)
        first, lane, ones = _head_masks()
        if n:
            send, finish = _scatter_phases(refs[9:9 + n], refs[12 + n + g8:12 + 2 * n + g8], *sems[:2])
            pl.when((gi == 0) & (j == 0))(send)
        if g8:
            send8, pass_on8, finish8 = _allgather8_phases(refs[12 + 2 * n + g8], *sems[2 * (n > 0):])
            pl.when((gi == 0) & (j == 0))(send8)
            pl.when((gi == ngrp - 1) & (j == nq - 1))(pass_on8)

        @pl.when(j == 0)
        def _():
            dq_s[...] = jnp.zeros_like(dq_s)
            for pr in range(grp):
                cols = slice(pr * 128, (pr + 1) * 128)
                for ib in range(nq):
                    rows = slice(ib * t, (ib + 1) * t)
                    q = q_ref[rows, cols].astype(F32) * scale
                    lse = lse_ref[rows, cols]
                    qa_s[pr, 0, ib] = jnp.where(first, q, _place3(lane, HEAD_DIM + 2 * AUG, _split3(-lse[:, 0:1]),
                                                                  ones(0, 2 * AUG))).astype(BF16)
                    qa_s[pr, 1, ib] = jnp.where(
                        first, _place3(lane, 2 * AUG, _split3(-lse[:, HEAD_DIM:HEAD_DIM + 1]), ones(1, 2 * AUG)),
                        q).astype(BF16)
                    qt_s[pr, ib, :, 0:t] = jnp.where(first, q, 0.0).T.astype(BF16)
                    qt_s[pr, ib, :, t:2 * t] = jnp.where(first, 0.0, q).T.astype(BF16)
                    do = dy_ref[rows, cols].astype(F32)
                    prod = do * y_ref[rows, cols].astype(F32)
                    dd0 = jnp.sum(jnp.where(first, prod, 0.0), axis=-1, keepdims=True)
                    dd1 = jnp.sum(jnp.where(first, 0.0, prod), axis=-1, keepdims=True)
                    da_s[pr, 0, ib] = jnp.where(first, do, _place3(lane, HEAD_DIM, _split3(-dd0), 0.0)).astype(BF16)
                    da_s[pr, 1, ib] = jnp.where(first, _place3(lane, 0, _split3(-dd1), 0.0), do).astype(BF16)
                    dt_s[pr, ib, :, 0:t] = jnp.where(first, do, 0.0).T.astype(BF16)
                    dt_s[pr, ib, :, t:2 * t] = jnp.where(first, 0.0, do).T.astype(BF16)

        keys = []
        for pr in range(grp):
            kj = k_ref[:, pr * 128:(pr + 1) * 128].astype(F32)
            vj = v_ref[:, pr * 128:(pr + 1) * 128].astype(F32)
            keys.append((
                jnp.where(first, kj, ka_ref[pr, 0] + kb_ref[pr, 0, pl.ds(j, 1), :]).astype(BF16),
                jnp.where(first, ka_ref[pr, 1] + kb_ref[pr, 1, pl.ds(j, 1), :], kj).astype(BF16),
                jnp.concatenate([jnp.where(first, kj, 0.0), jnp.where(first, 0.0, kj)], axis=0).astype(BF16),
                jnp.where(first, vj, ones(0, AUG)).astype(BF16),
                jnp.where(first, ones(1, AUG), vj).astype(BF16)))
        dkt_s[...] = jnp.zeros_like(dkt_s)
        dvt_s[...] = jnp.zeros_like(dvt_s)

        def step(i, _):
            lc = lc_ref[i - j]
            rows = pl.ds(pl.multiple_of(i * t, t), t)
            for pr in range(grp):
                k0a, k1a, kst, v0a, v1a = keys[pr]
                p0 = jnp.exp(_dot(qa_s[pr, 0, i], k0a, NT) + lc)
                p1 = jnp.exp(_dot(qa_s[pr, 1, i], k1a, NT) + lc)
                e0 = (p0 * _dot(da_s[pr, 0, i], v0a, NT)).astype(BF16)
                e1 = (p1 * _dot(da_s[pr, 1, i], v1a, NT)).astype(BF16)
                dq_s[pr, rows, :] += _dot(jnp.concatenate([e0, e1], axis=1), kst)
                dvt_s[pr] += _dot(dt_s[pr, i], jnp.concatenate([p0.astype(BF16), p1.astype(BF16)], axis=0))
                dkt_s[pr] += _dot(qt_s[pr, i], jnp.concatenate([e0, e1], axis=0))
            return 0

        lax.fori_loop(j, nq, step, 0)
        for pr in range(grp):
            dk_ref[:, pr * 128:(pr + 1) * 128] = dkt_s[pr].T.astype(BF16)
            dv_ref[:, pr * 128:(pr + 1) * 128] = dvt_s[pr].T.astype(BF16)

        @pl.when(j == nq - 1)
        def _():
            for pr in range(grp):
                dq_ref[:, pr * 128:(pr + 1) * 128] = (dq_s[pr] * scale).astype(BF16)

        if n:
            pl.when((gi == ngrp - 1) & (j == nq - 1))(finish)
        if g8:
            pl.when((gi == ngrp - 1) & (j == nq - 1))(finish8)

    colblock = lambda c: pl.BlockSpec((s, wide), lambda g, j: (0, c + g))
    blk = lambda c: pl.BlockSpec((t, wide), lambda g, j: (j, c + g))
    out = jax.ShapeDtypeStruct((s, D), BF16)
    res = pl.pallas_call(
        body, name="attn_bwd", grid=(ngrp, nq),
        in_specs=[colblock(qcol), blk(kcol), blk(vcol), colblock(0), colblock(0), colblock(0),
                  _full((nq, t, t)), pl.BlockSpec((grp, 2, t, 128), lambda g, j: (g, 0, 0, 0)),
                  pl.BlockSpec((grp, 2, nq, 128), lambda g, j: (g, 0, 0, 0))] + [ANY] * (n + g8),
        out_specs=[colblock(0), blk(0), blk(0)] + [ANY] * (n + g8),
        out_shape=[out] * 3 + _scatter_shapes(scattering)
        + ([jax.ShapeDtypeStruct(gathering.shape, gathering.dtype)] if g8 else []),
        input_output_aliases={9 + n: 3 + n} if g8 else {},
        scratch_shapes=[pltpu.VMEM((grp, 2, nq, t, 128), BF16), pltpu.VMEM((grp, nq, 128, 2 * t), BF16),
                        pltpu.VMEM((grp, 2, nq, t, 128), BF16), pltpu.VMEM((grp, nq, 128, 2 * t), BF16),
                        pltpu.VMEM((grp, s, 128), F32), pltpu.VMEM((grp, 128, t), F32),
                        pltpu.VMEM((grp, 128, t), F32)]
        + (_scatter_sems(n) if n else [])
        + ([pltpu.SemaphoreType.DMA((7,)), pltpu.SemaphoreType.DMA((7,))] if g8 else []),
        compiler_params=_params("arbitrary", "arbitrary", communicates=bool(n + g8)),
    )(z, z, z, yb, dyb, lse, logc, ka, kb, *scattering, *([gathering] if g8 else []))
    return res[0], res[1], res[2], res[3:3 + n], (res[3 + n] if g8 else None)


def in_bwd_norm(dz, wg, x, dx1, g_pre, scattering, gathering=None):
    s = x.shape[0]
    tm = 512
    n = len(scattering)
    g = 0 if gathering is None else 1
    last = (s // tm - 1, N_CHIPS - 1)

    def body(*refs):
        dz_ref, w_ref, x_ref, dx1_ref, g_ref = refs[:5]
        dx_ref, dg_ref = refs[5 + n + g:7 + n + g]
        acc_ref = refs[7 + 2 * n + 2 * g]
        sems = refs[8 + 2 * n + 2 * g:]
        i, k = pl.program_id(0), pl.program_id(1)
        if n:
            send, finish = _scatter_phases(refs[5:5 + n], refs[7 + n + g:7 + 2 * n + g], *sems[:2])
            pl.when((i == 0) & (k == 0))(send)
        if g:
            send8, pass_on8, finish8 = _allgather8_phases(refs[7 + 2 * n + g], *sems[2 * (n > 0):])
            pl.when((i == 0) & (k == 0))(send8)
            pl.when((i == last[0]) & (k == last[1]))(pass_on8)

        @pl.when((i == 0) & (k == 0))
        def _():
            dg_ref[...] = jnp.zeros_like(dg_ref)

        part = _dot(dz_ref[...], w_ref[...], NT)

        @pl.when(k == 0)
        def _():
            acc_ref[...] = part

        @pl.when(k > 0)
        def _():
            acc_ref[...] += part

        @pl.when(k == N_CHIPS - 1)
        def _():
            dh = acc_ref[...]
            xhat, r = _rms(x_ref[...])
            dg_ref[...] += jnp.sum(dh * xhat, axis=0, keepdims=True)
            dx_ref[...] = dx1_ref[...] + _rms_bwd(dh * g_ref[...], xhat, r)

        if n:
            pl.when((i == last[0]) & (k == last[1]))(finish)
        if g:
            pl.when((i == last[0]) & (k == last[1]))(finish8)

    row = pl.BlockSpec((tm, D), lambda i, k: (i, 0))
    vec = pl.BlockSpec((1, D), lambda i, k: (0, 0))
    res = pl.pallas_call(
        body, name="in_bwd_norm", grid=(s // tm, N_CHIPS),
        in_specs=[pl.BlockSpec((tm, IN_SHARD), lambda i, k: (i, k)),
                  pl.BlockSpec((None, D, IN_SHARD), lambda i, k: (k, 0, 0)), row, row, vec] + [ANY] * (n + g),
        out_specs=[row, vec] + [ANY] * (n + g),
        out_shape=[jax.ShapeDtypeStruct((s, D), F32), jax.ShapeDtypeStruct((1, D), F32)]
        + _scatter_shapes(scattering) + ([jax.ShapeDtypeStruct(gathering.shape, gathering.dtype)] if g else []),
        input_output_aliases={5 + n: 2 + n} if g else {},
        scratch_shapes=[pltpu.VMEM((tm, D), F32)] + (_scatter_sems(n) if n else [])
        + ([pltpu.SemaphoreType.DMA((7,)), pltpu.SemaphoreType.DMA((7,))] if g else []),
        compiler_params=_params("arbitrary", "arbitrary", communicates=bool(n + g)),
    )(dz, wg, x, dx1, g_pre, *scattering, *([gathering] if g else []))
    return res[0], res[1], res[2:2 + n], (res[2 + n] if g else None)


def _adamw_math(w, g, m, v):
    m = ADAM_B1 * m + (1.0 - ADAM_B1) * g
    v = ADAM_B2 * v + (1.0 - ADAM_B2) * (g * g)
    m_hat = m / (1.0 - ADAM_B1 ** ADAM_STEP)
    v_hat = v / (1.0 - ADAM_B2 ** ADAM_STEP)
    delta = -ADAM_LR * (m_hat / (jnp.sqrt(v_hat) + ADAM_EPS) + ADAM_WD * w)
    return delta, m, v


def adamw(name, w, g, m, v, tr):
    r, c = w.shape

    def body(w_ref, g_ref, m_ref, v_ref, go_ref, d_ref, nm_ref, nv_ref):
        g = g_ref[...]
        go_ref[...] = g
        d_ref[...], nm_ref[...], nv_ref[...] = _adamw_math(w_ref[...], g, m_ref[...], v_ref[...])

    out = jax.ShapeDtypeStruct((r, c), F32)
    return pl.pallas_call(
        body, name=name, grid=(r // tr,), in_specs=[_rows(tr, c)] * 4, out_specs=[_rows(tr, c)] * 4,
        out_shape=[out] * 4, compiler_params=_params("parallel"),
    )(w, g, m, v)


def _allgather8_phases(buf, send_sems, recv_sems):
    x, y, c, chips = _place()
    me = 2 * x + y
    sibling = (x, y, 1 - c)
    rows = buf.shape[1] // 2

    def part(chip, core):
        return buf.at[chip, pl.ds(core * rows, rows)]

    def copy(k, block, to):
        return pltpu.make_async_remote_copy(src_ref=block, dst_ref=block, send_sem=send_sems.at[k],
                                            recv_sem=recv_sems.at[k], device_id=to, device_id_type=MESH)

    def chip_of(j):
        return 2 * chips[j][0] + chips[j][1]

    def send():
        copy(0, part(me, c), sibling).start()
        for j in range(3):
            copy(1 + j, part(me, c), (chips[j][0], chips[j][1], c)).start()

    def pass_on():
        for j in range(3):
            copy(1 + j, part(chip_of(j), c), (chips[j][0], chips[j][1], c)).wait_recv()
            copy(4 + j, part(chip_of(j), c), sibling).start()

    def finish():
        copy(0, part(me, 1 - c), sibling).wait_recv()
        for j in range(3):
            copy(4 + j, part(chip_of(j), 1 - c), sibling).wait_recv()
        copy(0, part(me, c), sibling).wait_send()
        for j in range(3):
            copy(1 + j, part(me, c), (chips[j][0], chips[j][1], c)).wait_send()
            copy(4 + j, part(chip_of(j), c), sibling).wait_send()

    return send, pass_on, finish


def add_halves(name, g, recv, c_idx, tr):
    n, h, c = recv.shape

    def body(c_ref, g_ref, r_ref, o_ref):
        o_ref[...] = (g_ref[...] + r_ref[...]).astype(BF16)

    nb = h // tr
    return pl.pallas_call(
        body, name=name,
        grid_spec=pltpu.PrefetchScalarGridSpec(
            num_scalar_prefetch=1, grid=(n, nb),
            in_specs=[pl.BlockSpec((None, tr, c), lambda k, i, c_ref: (k, c_ref[0] * nb + i, 0)),
                      pl.BlockSpec((None, tr, c), lambda k, i, c_ref: (k, i, 0))],
            out_specs=pl.BlockSpec((None, tr, c), lambda k, i, c_ref: (k, i, 0))),
        out_shape=jax.ShapeDtypeStruct((n, h, c), BF16), compiler_params=_params("parallel", "parallel"),
    )(c_idx, g, recv)


def sum_chips(name, parts, recv, where, tr):
    n, h, c = recv.shape
    nb = h // tr

    def body(w_ref, p_ref, r_ref, o_ref):
        acc = p_ref[...].astype(F32)
        for k in range(n):
            acc = acc + r_ref[k].astype(F32)
        o_ref[...] = acc

    return pl.pallas_call(
        body, name=name,
        grid_spec=pltpu.PrefetchScalarGridSpec(
            num_scalar_prefetch=1, grid=(nb,),
            in_specs=[pl.BlockSpec((None, tr, c), lambda i, w_ref: (w_ref[0], i, 0)),
                      pl.BlockSpec((n, tr, c), lambda i, w_ref: (0, i, 0))],
            out_specs=pl.BlockSpec((tr, c), lambda i, w_ref: (w_ref[1] * nb + i, 0))),
        out_shape=jax.ShapeDtypeStruct((2 * h, c), F32), compiler_params=_params("parallel"),
    )(where, parts, recv)


def place_shard(name, shard, where, dtype, tr):
    r, c = shard.shape

    def body(w_ref, s_ref, o_ref):
        o_ref[...] = s_ref[...].astype(dtype)

    return pl.pallas_call(
        body, name=name,
        grid_spec=pltpu.PrefetchScalarGridSpec(
            num_scalar_prefetch=1, grid=(r // tr,),
            in_specs=[pl.BlockSpec((tr, c), lambda i, w_ref: (i, 0))],
            out_specs=pl.BlockSpec((None, tr, c), lambda i, w_ref: (w_ref[0], i, 0))),
        out_shape=jax.ShapeDtypeStruct((N_CHIPS, r, c), dtype), compiler_params=_params("parallel"),
    )(where, shard)


ANY = pl.BlockSpec(memory_space=pl.ANY)


def _place():
    x, y, c = lax.axis_index("x"), lax.axis_index("y"), lax.axis_index("c")
    chips = [(1 - x, y), (x, 1 - y), (1 - x, 1 - y)]
    return x, y, c, chips


def gather_shards(arrays):
    n = len(arrays)

    def body(*refs):
        send, pass_on, finish = _gather_phases(refs[n:2 * n], *refs[2 * n:], _spans(arrays))
        send()
        pass_on()
        finish()

    return pl.pallas_call(
        body, name="gather_shards", in_specs=[ANY] * n, out_specs=[ANY] * n,
        out_shape=[jax.ShapeDtypeStruct(a.shape, a.dtype) for a in _arrays(arrays)],
        input_output_aliases={w: w for w in range(n)}, scratch_shapes=_gather_sems(n),
        compiler_params=pltpu.CompilerParams(has_side_effects=True),
    )(*_arrays(arrays))


def _gather_sems(n):
    return [pltpu.SemaphoreType.DMA((6 * n,)), pltpu.SemaphoreType.DMA((6 * n,))]


class Span(typing.NamedTuple):
    array: jax.Array
    lo: int
    hi: int
    ways: tuple = (0, 1, 2)


def _arrays(gathering):
    return [g.array if isinstance(g, Span) else g for g in gathering]


def _spans(gathering):
    return [(g.lo, g.hi, g.ways) if isinstance(g, Span) else (0, g.shape[1], (0, 1, 2)) for g in gathering]


def _gather_phases(out, send_sems, recv_sems, spans):
    n = len(out)
    if not any(ways for _, _, ways in spans):
        return (lambda: None,) * 3
    x, y, c, chips = _place()
    me = 2 * x + y
    sibling = (x, y, 1 - c)

    def half(w, chip, core):
        lo, hi, _ = spans[w]
        h = (hi - lo) // 2
        return out[w].at[chip, pl.ds(lo + core * h, h)]

    def copy(k, block, to):
        return pltpu.make_async_remote_copy(src_ref=block, dst_ref=block, send_sem=send_sems.at[k],
                                            recv_sem=recv_sems.at[k], device_id=to, device_id_type=MESH)

    def over_ici(w, j, chip):
        return copy(3 * w + j, half(w, chip, c), (chips[j][0], chips[j][1], c))

    def over_d2d(w, j, core):
        return copy(3 * n + 3 * w + j, half(w, 2 * chips[j][0] + chips[j][1], core), sibling)

    pairs = [(w, j) for w in range(n) for j in spans[w][2]]

    def send():
        for w, j in pairs:
            over_ici(w, j, me).start()

    def pass_on():
        for w, j in pairs:
            over_ici(w, j, 2 * chips[j][0] + chips[j][1]).wait_recv()
            over_d2d(w, j, c).start()

    def finish():
        for w, j in pairs:
            over_d2d(w, j, 1 - c).wait_recv()
        for w, j in pairs:
            over_ici(w, j, me).wait_send()
            over_d2d(w, j, c).wait_send()

    return send, pass_on, finish


def _relay_sems():
    return [pltpu.SemaphoreType.DMA((4,)), pltpu.SemaphoreType.DMA((4,))]


def _relay_phases(out, send_sems, recv_sems):
    x, y, c, chips = _place()
    sibling = (x, y, 1 - c)
    rows = out.shape[1]
    quarter = rows // 4
    far = 2 * chips[2][0] + chips[2][1]

    def piece(chip, way, core):
        return out.at[chip, pl.ds(way * (rows // 2) + core * quarter, quarter)]

    def copy(k, block, to):
        return pltpu.make_async_remote_copy(src_ref=block, dst_ref=block, send_sem=send_sems.at[k],
                                            recv_sem=recv_sems.at[k], device_id=to, device_id_type=MESH)

    def over_ici(way, chip):
        return copy(way, piece(chip, way, c), (chips[way][0], chips[way][1], c))

    def over_d2d(way, core):
        return copy(2 + way, piece(far, way, core), sibling)

    def send():
        for way in range(2):
            other = chips[1 - way]
            over_ici(way, 2 * other[0] + other[1]).start()

    def pass_on():
        for way in range(2):
            over_ici(way, far).wait_recv()
            over_d2d(way, c).start()

    def finish():
        for way in range(2):
            over_d2d(way, 1 - c).wait_recv()
        for way in range(2):
            other = chips[1 - way]
            over_ici(way, 2 * other[0] + other[1]).wait_send()
            over_d2d(way, c).wait_send()

    return send, pass_on, finish


def swap_halves(name, grads):
    n = len(grads)

    def body(*refs):
        send, finish = _swap_phases(refs[:n], refs[n:2 * n], *refs[2 * n:])
        send()
        finish()

    return pl.pallas_call(
        body, name=name, in_specs=[ANY] * n, out_specs=[ANY] * n, out_shape=_swap_shapes(grads),
        scratch_shapes=_swap_sems(n), compiler_params=pltpu.CompilerParams(has_side_effects=True),
    )(*grads)


def _swap_shapes(grads):
    return [jax.ShapeDtypeStruct((a.shape[0], a.shape[1] // 2, a.shape[2]), a.dtype) for a in grads]


def _swap_sems(n):
    return [pltpu.SemaphoreType.DMA((n,)), pltpu.SemaphoreType.DMA((n,))]


def _swap_phases(g, out, send_sems, recv_sems):
    x, y, c, _ = _place()

    def copies():
        return [pltpu.make_async_remote_copy(
            src_ref=g[w].at[:, pl.ds((1 - c) * (g[w].shape[1] // 2), g[w].shape[1] // 2)], dst_ref=out[w],
            send_sem=send_sems.at[w], recv_sem=recv_sems.at[w], device_id=(x, y, 1 - c), device_id_type=MESH)
            for w in range(len(g))]

    def send():
        for cp in copies():
            cp.start()

    def finish():
        for cp in copies():
            cp.wait()

    return send, finish


def _send_phases(g, out, send_sems, recv_sems):
    x, y, c, _ = _place()

    def copies():
        return [pltpu.make_async_remote_copy(
            src_ref=g[w], dst_ref=out[w], send_sem=send_sems.at[w], recv_sem=recv_sems.at[w],
            device_id=(x, y, 1 - c), device_id_type=MESH) for w in range(len(g))]

    def send():
        for cp in copies():
            cp.start()

    def finish():
        for cp in copies():
            cp.wait()

    return send, finish


def dw_in_half(name, h, dz, which, sending):
    s = h.shape[0]
    hh, tb = D // 2, IN_SHARD // 2
    n = len(sending)
    steps = IN_COLS // tb

    def body(w_ref, *refs):
        a_ref, b_ref, o_ref = refs[0], refs[1], refs[2 + n]
        j = pl.program_id(0)
        if n:
            send, finish = _send_phases(refs[2:2 + n], refs[3 + n:3 + 2 * n], *refs[3 + 2 * n:])
            pl.when(j == 0)(send)
        o_ref[...] = _dot(a_ref[...], b_ref[...], TN)
        if n:
            pl.when(j == steps - 1)(finish)

    out = pl.pallas_call(
        body, name=name,
        grid_spec=pltpu.PrefetchScalarGridSpec(
            num_scalar_prefetch=1, grid=(steps,),
            in_specs=[pl.BlockSpec((s, hh), lambda j, w: (0, w[0])), pl.BlockSpec((s, tb), lambda j, w: (0, j))]
            + [ANY] * n,
            out_specs=[pl.BlockSpec((None, hh, tb), lambda j, w: (j // 2, 0, j % 2))] + [ANY] * n,
            scratch_shapes=_swap_sems(n) if n else []),
        out_shape=[jax.ShapeDtypeStruct((N_CHIPS, hh, IN_SHARD), F32)]
        + [jax.ShapeDtypeStruct(a.shape, a.dtype) for a in sending],
        compiler_params=_params("arbitrary", communicates=bool(n)),
    )(which, h, dz, *sending)
    return out[0], out[1:]


def scatter_chips(parts):
    n = len(parts)

    def body(*refs):
        send, finish = _scatter_phases(refs[:n], refs[n:2 * n], *refs[2 * n:])
        send()
        finish()

    return pl.pallas_call(
        body, name="scatter_chips", in_specs=[ANY] * n, out_specs=[ANY] * n,
        out_shape=_scatter_shapes(parts), scratch_shapes=_scatter_sems(n),
        compiler_params=pltpu.CompilerParams(has_side_effects=True),
    )(*parts)


def _scatter_shapes(parts):
    return [jax.ShapeDtypeStruct((3,) + a.shape[1:], a.dtype) for a in parts]


def _scatter_sems(n):
    return [pltpu.SemaphoreType.DMA((3 * n,)), pltpu.SemaphoreType.DMA((3 * n,))]


def _scatter_phases(p, out, send_sems, recv_sems):
    x, y, c, chips = _place()

    def copies():
        return [pltpu.make_async_remote_copy(
            src_ref=p[w].at[2 * px + py], dst_ref=out[w].at[j], send_sem=send_sems.at[3 * w + j],
            recv_sem=recv_sems.at[3 * w + j], device_id=(px, py, c), device_id_type=MESH)
            for w in range(len(p)) for j, (px, py) in enumerate(chips)]

    def send():
        for cp in copies():
            cp.start()

    def finish():
        for cp in copies():
            cp.wait()

    return send, finish


def join_halves(arrays, gathering):
    n = len(arrays)

    def body(*refs):
        out = refs[n + 1:2 * n + 1]
        send_sems, recv_sems = refs[2 * n + 2:2 * n + 4]
        send8, pass_on8, finish8 = _allgather8_phases(refs[2 * n + 1], *refs[2 * n + 4:])
        x, y, c, _ = _place()

        def copy(w, core):
            h = out[w].shape[0] // 2
            rows = out[w].at[pl.ds(core * h, h)]
            return pltpu.make_async_remote_copy(
                src_ref=rows, dst_ref=rows, send_sem=send_sems.at[w], recv_sem=recv_sems.at[w],
                device_id=(x, y, 1 - c), device_id_type=MESH)

        send8()
        for w in range(n):
            copy(w, c).start()
        pass_on8()
        for w in range(n):
            copy(w, 1 - c).wait_recv()
        finish8()
        for w in range(n):
            copy(w, c).wait_send()

    res = pl.pallas_call(
        body, name="join_halves", in_specs=[ANY] * (n + 1), out_specs=[ANY] * (n + 1),
        out_shape=[jax.ShapeDtypeStruct(a.shape, a.dtype) for a in list(arrays) + [gathering]],
        input_output_aliases={w: w for w in range(n + 1)},
        scratch_shapes=[pltpu.SemaphoreType.DMA((n,)), pltpu.SemaphoreType.DMA((n,)),
                        pltpu.SemaphoreType.DMA((7,)), pltpu.SemaphoreType.DMA((7,))],
        compiler_params=pltpu.CompilerParams(has_side_effects=True),
    )(*arrays, gathering)
    return res[:n], res[n]


def allreduce_small(packed):
    r, c = packed.shape
    n_dev = 8

    def body(x_ref, all_ref, sum_ref, send_sems, recv_sems, local_sem):
        x, y, cc, chips = _place()
        me, sibling = (x, y, cc), (x, y, 1 - cc)

        def rows(px, py, pc):
            return all_ref.at[4 * px + 2 * py + pc]

        def copy(k, block, to, src=None):
            return pltpu.make_async_remote_copy(
                src_ref=rows(*block) if src is None else src, dst_ref=rows(*block), send_sem=send_sems.at[k],
                recv_sem=recv_sems.at[k], device_id=to, device_id_type=MESH)

        mine = pltpu.make_async_copy(x_ref, rows(*me), local_sem)
        mine.start()
        first = [copy(0, me, sibling, src=x_ref)]
        first += [copy(1 + j, me, (*chip, cc), src=x_ref) for j, chip in enumerate(chips)]
        for cp in first:
            cp.start()
        passed = [copy(4 + j, (*chip, cc), sibling) for j, chip in enumerate(chips)]
        for j, chip in enumerate(chips):
            copy(1 + j, (*chip, cc), me).wait_recv()
            passed[j].start()
        copy(0, sibling, me).wait_recv()
        for j, chip in enumerate(chips):
            copy(4 + j, (*chip, 1 - cc), me).wait_recv()
        for cp in first + passed:
            cp.wait_send()
        mine.wait()
        acc = all_ref[0]
        for k in range(1, n_dev):
            acc = acc + all_ref[k]
        sum_ref[...] = acc

    vm = pl.BlockSpec(memory_space=pltpu.VMEM)
    return pl.pallas_call(
        body, name="allreduce_small", in_specs=[vm], out_specs=[vm, vm],
        out_shape=[jax.ShapeDtypeStruct((n_dev, r, c), F32), jax.ShapeDtypeStruct((r, c), F32)],
        scratch_shapes=[pltpu.SemaphoreType.DMA((7,)), pltpu.SemaphoreType.DMA((7,)), pltpu.SemaphoreType.DMA],
        compiler_params=pltpu.CompilerParams(has_side_effects=True, vmem_limit_bytes=VMEM_LIMIT),
    )(packed)[1]


def local_step(x, target, vecs, w_s, bs_t, bg, wg_in, late, core=None, order=None, where=None):
    on_mesh = core is not None

    def add(names, grads, recv):
        return [add_halves("add_" + n, g, r, core, min(r.shape[1], 256)) for n, g, r in zip(names, grads, recv)]

    g_pre, ln_g, ln_b, g_post, g_fpre, g_fpost = vecs
    s = x.shape[0]
    if order is None:
        order = jnp.arange(N_CHIPS, dtype=jnp.int32)
    logc = _attn_tables(s)
    ka, kb = _alibi_tables(s)

    h = norm_pre(x, g_pre)
    if not on_mesh:
        wg_a, wg_b, wg_out, wg_ff1, wg_ff2 = late
    if on_mesh:
        cut = D // 4
        z, (wg_in,), (wg_a, wg_b, wg_out, wg_ff1, wg_ff2) = mm_in(
            "mm_in_own", h, wg_in, order, 0, 1, None, [Span(wg_in, 0, D, (0, 1))], casting=late)
        z, (wg_in,), _ = mm_in("mm_in_near", h, wg_in, order, 1, 2, z, [Span(wg_in, 0, D, ())], relay=True)
        z, (wg_in, wg_a), _ = mm_in("mm_in_far", h, wg_in, order, 3, 1, z, [Span(wg_in, 0, D, ()), wg_a])
        ya, (wg_b, wg_ff2) = gating_fwd(z, ln_g, ln_b, w_s, bs_t, [wg_b, Span(wg_ff2, 0, cut)])
        yb, lse, (wg_ff1, wg_ff2, bg) = attn_fwd(z, logc, ka, kb, [wg_ff1, Span(wg_ff2, cut, 3 * cut), bg])
        bg = jnp.transpose(bg[:, :2, :], (1, 0, 2)).reshape(2, D)
    else:
        z, _, _ = mm_in("mm_in", h, wg_in, order, 0, N_CHIPS, None, [Span(wg_in, 0, D, ())])
        ya, _ = gating_fwd(z, ln_g, ln_b, w_s, bs_t, [])
        yb, lse, _ = attn_fwd(z, logc, ka, kb, [])
    merged, pa, pb, got = proj_merge(ya, yb, wg_a.reshape(D, D), wg_b.reshape(D, D), z, bg, [wg_out] if on_mesh else [])
    wg_out = got[0] if on_mesh else wg_out
    w_out = wg_out.reshape(D, D)
    o, x1, h2, got = out_norm(merged, w_out, x, g_post, g_fpre, [Span(wg_ff2, 3 * D // 4, D)] if on_mesh else [])
    a, rl, _ = mm_ff1(h2, wg_ff1, [])
    w_ff2 = (got[0] if on_mesh else wg_ff2).reshape(D_FF, D)
    dy, df, d_gfpost, loss = ff2_loss(rl, w_ff2, x1, target, g_fpost)

    half_cols = pl.BlockSpec((D, D // 2), lambda i, j: (0, j))
    d_wff2 = mm_tn("dw_ff2", rl, df, D // 2, D, (D_FF, D), pl.BlockSpec((D // 2, D), lambda i, j: (i, 0)))
    da = ff2_bwd(df, w_ff2, a)
    d_wff1 = mm_tn("dw_ff1", h2, da, D, D // 2, (N_CHIPS, D, D),
                   pl.BlockSpec((None, D, D // 2), lambda i, j: (j // 2, 0, j % 2)))
    d_ff = [d_wff1, d_wff2.reshape(N_CHIPS, D, D)]
    dx1, do, d_gfpre, d_gpost, recv_ff = ff1_bwd_norms(da, wg_ff1, x1, o, dy, g_fpre, g_post, d_ff if on_mesh else [])
    d_wout = mm_tn("dw_out", merged, do, D, D // 2, (D, D), half_cols)
    dpa, dpb, dga, dgb, d_bg = out_bwd_gates(do, w_out, pa, pb, z, bg)
    d_wa = mm_tn("dw_a", ya, dpa, D, D // 2, (D, D), half_cols)
    d_wb = mm_tn("dw_b", yb, dpb, D, D // 2, (D, D), half_cols)
    dya = mm_nt("dy_a", dpa, wg_a.reshape(D, D))
    dyb = mm_nt("dy_b", dpb, wg_b.reshape(D, D))
    d_proj = [d_wa.reshape(N_CHIPS, D // N_CHIPS, D), d_wb.reshape(N_CHIPS, D // N_CHIPS, D),
              d_wout.reshape(N_CHIPS, D // N_CHIPS, D)]
    du, dv, d_ws, d_bs, d_lng, d_lnb, recv_proj = gating_bwd(z, dya, ln_g, ln_b, w_s, bs_t, d_proj if on_mesh else [])
    early = d_proj + d_ff
    parts_early = add(BIG[1:], early, list(recv_proj) + list(recv_ff)) if on_mesh else []
    small = dict(b_gate=d_bg, ln_v_g=d_lng, ln_v_b=d_lnb, w_s=d_ws, b_s=d_bs[:, 0, :],
                 norm_mix_post=d_gpost, norm_ffn_pre=d_gfpre, norm_ffn_post=d_gfpost)
    packed = pack_small(dict(small, norm_mix_pre=jnp.zeros((1, D), F32)), loss, where) if on_mesh else None
    dq, dk, dvb, got_early, packed = attn_bwd(z, yb, dyb, lse, logc, ka, kb, parts_early, packed)
    dz = jnp.concatenate([du, dv, dq, dk, dvb, dga, dgb], axis=1)
    if on_mesh:
        for_sibling, _ = dw_in_half("dw_in_sibling", h, dz, 1 - core, [])
        mine, from_sibling = dw_in_half("dw_in_mine", h, dz, core, [for_sibling])
        d_win = None
        parts_late = [add_halves("add_w_in", mine, from_sibling[0], jnp.zeros((1,), jnp.int32), 256)]
    else:
        half = IN_SHARD // 2
        d_win = mm_tn("dw_in", h, dz, D, half, (N_CHIPS, D, IN_SHARD),
                      pl.BlockSpec((None, D, half), lambda i, j: (j // 2, 0, j % 2)))
        parts_late = []
    dx, d_gpre, got_late, _ = in_bwd_norm(dz, wg_in, x, dx1, g_pre, parts_late)
    small["norm_mix_pre"] = d_gpre
    return (loss[0, 0], dx, [d_win] + early, small, parts_late + parts_early, list(got_late) + list(got_early),
            packed)


BIG = ("w_in", "w_a_proj", "w_b_proj", "w_out", "w_ff1", "w_ff2")
SMALL = ("norm_mix_pre", "ln_v_g", "ln_v_b", "b_s", "norm_mix_post", "norm_ffn_pre", "norm_ffn_post", "w_s", "b_gate")
ORDER = ("norm_mix_pre", "w_in", "b_gate", "ln_v_g", "ln_v_b", "w_s", "b_s", "w_a_proj", "w_b_proj", "w_out",
         "norm_mix_post", "norm_ffn_pre", "w_ff1", "w_ff2", "norm_ffn_post")
VEC_ROWS = D // 128
WS_ROW = 7 * VEC_ROWS
BG_ROW = WS_ROW + GROUPS * CHUNK
LOSS_ROW = BG_ROW + 2 * VEC_ROWS
PACK_ROWS = LOSS_ROW + 8


def pack_small(small, loss, where):
    vectors = [small[n] for n in SMALL[:7]]
    operands = vectors + [small["w_s"], small["b_gate"], loss]

    def body(where_ref, *refs):
        out = refs[-1]
        ws_ref, bg_ref, loss_ref = refs[7:10]
        for i, n in enumerate(SMALL[:7]):
            if n == "b_s":
                out[i * VEC_ROWS:(i + 1) * VEC_ROWS, :] = refs[i][...]
            else:
                for j in range(VEC_ROWS):
                    out[i * VEC_ROWS + j:i * VEC_ROWS + j + 1, :] = refs[i][:, j * 128:(j + 1) * 128]
        for g in range(GROUPS):
            out[WS_ROW + g * CHUNK:WS_ROW + (g + 1) * CHUNK, :] = ws_ref[g]
        for r in range(2):
            for j in range(VEC_ROWS):
                row = BG_ROW + r * VEC_ROWS + j
                out[row:row + 1, :] = bg_ref[r:r + 1, j * 128:(j + 1) * 128]
        lane = lax.broadcasted_iota(jnp.int32, (8, 128), 1)
        sub = lax.broadcasted_iota(jnp.int32, (8, 128), 0)
        out[LOSS_ROW:LOSS_ROW + 8, :] = jnp.where((lane == 0) & (sub == 0), loss_ref[...], 0.0)

    return pl.pallas_call(
        body, name="pack_small",
        grid_spec=pltpu.PrefetchScalarGridSpec(
            num_scalar_prefetch=1, grid=(1,), in_specs=[_full(a.shape) for a in operands],
            out_specs=pl.BlockSpec((None, PACK_ROWS, 128), lambda i, w: (w[0], w[1], 0))),
        out_shape=jax.ShapeDtypeStruct((N_CHIPS, 2 * PACK_ROWS, 128), F32), compiler_params=_params("arbitrary"),
    )(where, *operands)


def pack_vector(vec, where):
    def body(where_ref, v_ref, out):
        for j in range(VEC_ROWS):
            out[j:j + 1, :] = v_ref[:, j * 128:(j + 1) * 128]

    return pl.pallas_call(
        body, name="pack_vector",
        grid_spec=pltpu.PrefetchScalarGridSpec(
            num_scalar_prefetch=1, grid=(1,), in_specs=[_full(vec.shape)],
            out_specs=pl.BlockSpec((None, VEC_ROWS, 128), lambda i, w: (w[0], w[1], 0))),
        out_shape=jax.ShapeDtypeStruct((N_CHIPS, 2 * VEC_ROWS, 128), F32), compiler_params=_params("arbitrary"),
    )(where, vec)


def adamw_small(gathered, first, chip, w, m, v):
    shapes = {n: (1, D) for n in SMALL}
    shapes.update(b_s=(GROUPS, CHUNK), w_s=(GROUPS * CHUNK, CHUNK), b_gate=(2, D // N_CHIPS))
    flat = lambda t: [t[n].reshape(shapes[n]) for n in SMALL]
    per = D // N_CHIPS // 128

    def body(chip_ref, all_ref, first_ref, *refs):
        params, outs = refs[:27], refs[27:]
        sub = lax.broadcasted_iota(jnp.int32, (VEC_ROWS, 128), 0)
        sum_ref = outs[36]
        total = all_ref[0, 0:PACK_ROWS, :]
        head = first_ref[0, 0:VEC_ROWS, :]
        for k in range(1, 2 * N_CHIPS):
            total = total + all_ref[k // 2, (k % 2) * PACK_ROWS:(k % 2 + 1) * PACK_ROWS, :]
            head = head + first_ref[k // 2, (k % 2) * VEC_ROWS:(k % 2 + 1) * VEC_ROWS, :]
        sum_ref[...] = total
        sum_ref[0:VEC_ROWS, :] = head

        def gate_row(r):
            rows = sum_ref[BG_ROW + r * VEC_ROWS:BG_ROW + (r + 1) * VEC_ROWS, :]
            return jnp.concatenate([jnp.sum(jnp.where(sub == per * chip_ref[0] + j, rows, 0.0), axis=0, keepdims=True)
                                    for j in range(per)], axis=1)

        for i, n in enumerate(SMALL):
            if n == "b_s":
                g = sum_ref[i * VEC_ROWS:(i + 1) * VEC_ROWS, :]
            elif n == "w_s":
                g = sum_ref[WS_ROW:BG_ROW, :]
            elif n == "b_gate":
                g = jnp.concatenate([gate_row(0), gate_row(1)], axis=0)
            else:
                g = jnp.concatenate([sum_ref[i * VEC_ROWS + j:i * VEC_ROWS + j + 1, :] for j in range(VEC_ROWS)],
                                    axis=1)
            delta, nm, nv = _adamw_math(params[i][...], g, params[9 + i][...], params[18 + i][...])
            outs[4 * i][...], outs[4 * i + 1][...], outs[4 * i + 2][...], outs[4 * i + 3][...] = g, delta, nm, nv

    vm = pl.BlockSpec(memory_space=pltpu.VMEM)
    res = pl.pallas_call(
        body, name="adamw_small",
        in_specs=[pl.BlockSpec(memory_space=pltpu.SMEM)] + [vm] * 29, out_specs=[vm] * 37,
        out_shape=[jax.ShapeDtypeStruct(shapes[n], F32) for n in SMALL for _ in range(4)]
        + [jax.ShapeDtypeStruct((PACK_ROWS, 128), F32)],
        compiler_params=_params(),
    )(chip, gathered, first, *flat(w), *flat(m), *flat(v))
    new = {n: tuple(r.reshape(w[n].shape) for r in res[4 * i:4 * i + 4]) for i, n in enumerate(SMALL)}
    return new, res[36][LOSS_ROW, 0]


def kernel(x, norm_mix_pre, w_in, b_gate, ln_v_g, ln_v_b, w_s, b_s, w_a_proj, w_b_proj, w_out, norm_mix_post, norm_ffn_pre, w_ff1, w_ff2, norm_ffn_post, loss_target, m_norm_mix_pre, m_w_in, m_b_gate, m_ln_v_g, m_ln_v_b, m_w_s, m_b_s, m_w_a_proj, m_w_b_proj, m_w_out, m_norm_mix_post, m_norm_ffn_pre, m_w_ff1, m_w_ff2, m_norm_ffn_post, v_norm_mix_pre, v_w_in, v_b_gate, v_ln_v_g, v_ln_v_b, v_w_s, v_b_s, v_w_a_proj, v_w_b_proj, v_w_out, v_norm_mix_post, v_norm_ffn_pre, v_w_ff1, v_w_ff2, v_norm_ffn_post):
    w = dict(norm_mix_pre=norm_mix_pre, w_in=w_in, b_gate=b_gate, ln_v_g=ln_v_g, ln_v_b=ln_v_b, w_s=w_s, b_s=b_s,
             w_a_proj=w_a_proj, w_b_proj=w_b_proj, w_out=w_out, norm_mix_post=norm_mix_post,
             norm_ffn_pre=norm_ffn_pre, w_ff1=w_ff1, w_ff2=w_ff2, norm_ffn_post=norm_ffn_post)
    m = dict(norm_mix_pre=m_norm_mix_pre, w_in=m_w_in, b_gate=m_b_gate, ln_v_g=m_ln_v_g, ln_v_b=m_ln_v_b, w_s=m_w_s,
             b_s=m_b_s, w_a_proj=m_w_a_proj, w_b_proj=m_w_b_proj, w_out=m_w_out, norm_mix_post=m_norm_mix_post,
             norm_ffn_pre=m_norm_ffn_pre, w_ff1=m_w_ff1, w_ff2=m_w_ff2, norm_ffn_post=m_norm_ffn_post)
    v = dict(norm_mix_pre=v_norm_mix_pre, w_in=v_w_in, b_gate=v_b_gate, ln_v_g=v_ln_v_g, ln_v_b=v_ln_v_b, w_s=v_w_s,
             b_s=v_b_s, w_a_proj=v_w_a_proj, w_b_proj=v_w_b_proj, w_out=v_w_out, norm_mix_post=v_norm_mix_post,
             norm_ffn_pre=v_norm_ffn_pre, w_ff1=v_w_ff1, w_ff2=v_w_ff2, norm_ffn_post=v_norm_ffn_post)
    chip = 2 * lax.axis_index("x") + lax.axis_index("y")
    core = lax.axis_index("c")

    where = jnp.stack([chip, core]).astype(jnp.int32)
    wg_in = place_shard("place_w_in", w_in[0], where, BF16, 256)
    bg_all = place_shard("place_b_gate", jnp.pad(b_gate[0], ((0, 14), (0, 0))), where, F32, 16)
    vecs = (norm_mix_pre, ln_v_g, ln_v_b, norm_mix_post, norm_ffn_pre, norm_ffn_post)
    loss, dx, _, small, parts, got, packed = local_step(
        x[0], loss_target[0], vecs, w_s[0], b_s[0].T, bg_all, wg_in, [w[n][0] for n in BIG[1:]],
        core=jnp.reshape(core, (1,)).astype(jnp.int32),
        order=jnp.stack([chip, chip ^ 2, chip ^ 1, chip ^ 3]).astype(jnp.int32), where=where)

    halves = [sum_chips("sum_" + n, p, r, where, min(p.shape[1], 256)) for n, p, r in zip(BIG, parts, got)]
    joined, first = join_halves(halves, pack_vector(small["norm_mix_pre"], where))
    grads = dict(zip(BIG, joined))
    new = {}
    for n in BIG:
        shape = w[n].shape
        res = adamw("adamw_" + n, w[n][0], grads[n], m[n][0], v[n][0], min(shape[1], 256))
        new[n] = tuple(r.reshape(shape) for r in res)
    small_new, loss = adamw_small(packed, first, jnp.reshape(chip, (1,)).astype(jnp.int32), w, m, v)
    new.update(small_new)

    outs = [loss, dx[None]]
    for i in range(4):
        outs += [new[n][i] for n in ORDER]
    return tuple(outs)
```

```python
import functools
import math
import typing

import numpy as np
import jax
import jax.numpy as jnp
from jax import lax
from jax.experimental import pallas as pl
from jax.experimental.pallas import tpu as pltpu

F32 = jnp.float32
BF16 = jnp.bfloat16
MESH = pl.DeviceIdType.MESH

D = 1024
EPS = 1e-6
CHUNK = 128
GROUPS = 8
HEADS = 16
HEAD_DIM = 64
ATT_T = 256
ATT_GROUP = 8
ATT_BWD_GROUP = 2
N_CHIPS = 4
D_FF = 4 * D
IN_COLS = 7 * D
IN_SHARD = IN_COLS // N_CHIPS
MASKED = -1e30
VMEM_LIMIT = 56 * 2 ** 20

ADAM_LR, ADAM_B1, ADAM_B2, ADAM_EPS, ADAM_WD, ADAM_STEP = 0.001, 0.9, 0.999, 1e-08, 0.01, 10

NN = (((1,), (0,)), ((), ()))
NT = (((1,), (1,)), ((), ()))
TN = (((0,), (0,)), ((), ()))


def _dot(a, b, dims=NN):
    return lax.dot_general(a, b, dims, preferred_element_type=F32)


def _params(*sem, communicates=False):
    return pltpu.CompilerParams(dimension_semantics=sem or None, vmem_limit_bytes=VMEM_LIMIT,
                                has_side_effects=communicates)


def _rows(tr, c, col=0):
    return pl.BlockSpec((tr, c), lambda i: (i, col))


def _full(shape):
    n = len(shape)
    return pl.BlockSpec(shape, lambda *_: (0,) * n)


def _gelu(x):
    k = math.sqrt(2.0 / math.pi)
    return 0.5 * x * (1.0 + jnp.tanh(k * (x + 0.044715 * x * x * x)))


def _gelu_and_grad(x):
    k = math.sqrt(2.0 / math.pi)
    t = jnp.tanh(k * (x + 0.044715 * x * x * x))
    g = 0.5 * x * (1.0 + t)
    dg = 0.5 * (1.0 + t) + 0.5 * x * (1.0 - t * t) * (k * (1.0 + 3.0 * 0.044715 * x * x))
    return g, dg


def _sigmoid(x):
    return 1.0 / (1.0 + jnp.exp(-x))


def _rms(x):
    r = lax.rsqrt(jnp.mean(x * x, axis=-1, keepdims=True) + EPS)
    return x * r, r


def _rms_bwd(dn, xhat, r):
    return r * (dn - xhat * jnp.mean(dn * xhat, axis=-1, keepdims=True))


def norm_pre(x, g):
    s = x.shape[0]
    tr = 512

    def body(x_ref, g_ref, h_ref):
        xhat, _ = _rms(x_ref[...])
        h_ref[...] = (xhat * g_ref[...]).astype(BF16)

    return pl.pallas_call(
        body, name="norm_pre", grid=(s // tr,),
        in_specs=[_rows(tr, D), _full((1, D))], out_specs=_rows(tr, D),
        out_shape=jax.ShapeDtypeStruct((s, D), BF16), compiler_params=_params("parallel"),
    )(x, g)


def mm_in(h, wg, order, staged, casting=()):
    s = h.shape[0]
    tm, tn = 1024, IN_SHARD // 2
    per = IN_SHARD // tn
    m = len(casting)
    nj, ni = N_CHIPS * per, s // tm
    cast_steps = per * ni

    def body(order_ref, *refs):
        a_ref = refs[0]
        cast_in = refs[2:2 + m]
        o_ref, held = refs[2 + m], refs[3 + m]
        cast_out = refs[4 + m:4 + 2 * m]
        tile, tile_sem = refs[4 + 2 * m:6 + 2 * m]
        sems = refs[6 + 2 * m:]
        j, i = pl.program_id(0), pl.program_id(1)

        @pl.when(j * ni + i < cast_steps)
        def _():
            for src, dst in zip(cast_in, cast_out):
                dst[...] = src[...].astype(BF16)

        def fetch(t):
            chip = order_ref[t // per]
            return pltpu.make_async_copy(held.at[chip, :, pl.ds((t % per) * tn, tn)], tile.at[t % 2],
                                         tile_sem.at[t % 2])

        if staged:
            near = _gather_phases([held], *sems[:2], [(0, D, (0, 1))])
            far = _relay_phases(held, *sems[2:])

        @pl.when(i == 0)
        def _():
            @pl.when(j == 0)
            def _():
                if staged:
                    near[0]()
                fetch(0).start()

            fetch(j).wait()

        @pl.when(i == ni - 1)
        def _():
            if staged:
                @pl.when(j == per - 1)
                def _():
                    near[1]()
                    near[2]()
                    far[0]()

                @pl.when(j == 3 * per - 1)
                def _():
                    far[1]()
                    far[2]()

            @pl.when(j + 1 < nj)
            def _():
                fetch(j + 1).start()

        rows = pl.ds(pl.multiple_of(i * tm, tm), tm)
        o_ref[...] = _dot(a_ref[rows, :], tile[j % 2]).astype(BF16)

    def cast_block(j, i, o):
        return jnp.minimum(j * ni + i, cast_steps - 1)

    out = pl.pallas_call(
        body, name="mm_in",
        grid_spec=pltpu.PrefetchScalarGridSpec(
            num_scalar_prefetch=1, grid=(nj, ni),
            in_specs=[pl.BlockSpec((s, D), lambda j, i, o: (0, 0)), ANY]
            + [pl.BlockSpec((a.shape[0] // cast_steps, a.shape[1]), lambda j, i, o: (cast_block(j, i, o), 0))
               for a in casting],
            out_specs=[pl.BlockSpec((tm, tn), lambda j, i, o: (i, o[j // per] * per + j % per)), ANY]
            + [pl.BlockSpec((None, a.shape[0] // cast_steps, a.shape[1]),
                            lambda j, i, o: (o[0], cast_block(j, i, o), 0)) for a in casting],
            scratch_shapes=[pltpu.VMEM((2, D, tn), BF16), pltpu.SemaphoreType.DMA((2,))]
            + (_gather_sems(1) + _relay_sems() if staged else [])),
        out_shape=[jax.ShapeDtypeStruct((s, IN_COLS), BF16), jax.ShapeDtypeStruct(wg.shape, wg.dtype)]
        + [jax.ShapeDtypeStruct((N_CHIPS,) + a.shape, BF16) for a in casting],
        input_output_aliases={2: 1},
        compiler_params=_params("arbitrary", "arbitrary", communicates=staged),
    )(order, h, wg, *casting)
    return out[0], out[1], out[2:]


def _tril_ws(ws_ref, g):
    r = lax.broadcasted_iota(jnp.int32, (CHUNK, CHUNK), 0)
    c = lax.broadcasted_iota(jnp.int32, (CHUNK, CHUNK), 1)
    return jnp.where(c <= r, ws_ref[g], 0.0).astype(BF16)


def _layer_norm(v):
    mu = jnp.mean(v, axis=-1, keepdims=True)
    d = v - mu
    rstd = lax.rsqrt(jnp.mean(d * d, axis=-1, keepdims=True) + EPS)
    return d * rstd, rstd


def gating_fwd(z, ln_g, ln_b, w_s, bs_t, gathering):
    s = z.shape[0]
    n = len(gathering)
    steps = s // CHUNK

    def body(*refs):
        u_ref, v_ref, lg_ref, lb_ref, ws_ref, bst_ref = refs[:6]
        ya_ref = refs[6 + n]
        ci = pl.program_id(0)
        if n:
            send, pass_on, finish = _gather_phases(refs[7 + n:7 + 2 * n], *refs[7 + 2 * n:], _spans(gathering))
            pl.when(ci == 0)(send)
            pl.when(ci == steps - 1)(pass_on)
        ug = _gelu(u_ref[...].astype(F32))
        vhat, _ = _layer_norm(_gelu(v_ref[...].astype(F32)))
        vn = (vhat * lg_ref[...] + lb_ref[...]).astype(BF16)
        for g in range(GROUPS):
            cols = slice(g * CHUNK, (g + 1) * CHUNK)
            mixed = _dot(_tril_ws(ws_ref, g), vn[:, cols]) + bst_ref[:, g:g + 1]
            ya_ref[:, cols] = (ug[:, cols] * mixed).astype(BF16)
        if n:
            pl.when(ci == steps - 1)(finish)

    out = pl.pallas_call(
        body, name="gating_fwd", grid=(steps,),
        in_specs=[_rows(CHUNK, D, 0), _rows(CHUNK, D, 1), _full((1, D)), _full((1, D)),
                  _full((GROUPS, CHUNK, CHUNK)), _full((CHUNK, GROUPS))] + [ANY] * n,
        out_specs=[_rows(CHUNK, D)] + [ANY] * n,
        out_shape=[jax.ShapeDtypeStruct((s, D), BF16)]
        + [jax.ShapeDtypeStruct(a.shape, a.dtype) for a in _arrays(gathering)],
        input_output_aliases={6 + w: 1 + w for w in range(n)},
        scratch_shapes=_gather_sems(n) if n else [],
        compiler_params=_params("arbitrary", communicates=bool(n)),
    )(z, z, ln_g, ln_b, w_s, bs_t, *_arrays(gathering))
    return out[0], out[1:]


def _attn_tables(s):
    nd = s // ATT_T
    r = np.arange(ATT_T)[None, :, None]
    c = np.arange(ATT_T)[None, None, :]
    delta = np.arange(nd)[:, None, None] * ATT_T + r - c
    count = np.zeros(delta.shape, np.int64)
    for window, dilation in ((128, 1), (512, 4), (2048, 16)):
        count += (delta >= 0) & (delta % dilation == 0) & (delta <= window)
    logc = np.where(count > 0, np.log(np.maximum(count, 1)), MASKED)
    return jnp.asarray(logc, F32)


AUG = 3


def _split3_np(x):
    terms, rest = [], np.asarray(x, np.float64)
    for _ in range(AUG):
        term = np.asarray(rest.astype(jnp.bfloat16), np.float64)
        terms.append(term)
        rest = rest - term
    return terms


def _split3(x):
    terms, rest = [], x
    for _ in range(AUG):
        term = rest.astype(BF16).astype(F32)
        terms.append(term)
        rest = rest - term
    return terms


def _alibi_tables(s):
    nb = s // ATT_T
    slopes = np.exp2(-8.0 * np.arange(1, HEADS + 1, dtype=np.float64) / HEADS)
    ka = np.zeros((HEADS // 2, 2, ATT_T, 128), np.float32)
    kb = np.zeros((HEADS // 2, 2, nb, 128), np.float32)
    for p in range(HEADS // 2):
        for e in range(2):
            base = HEAD_DIM * (1 - e)
            for a, term in enumerate(_split3_np(slopes[2 * p + e] * np.arange(ATT_T))):
                ka[p, e, :, base + a] = term
            for a, term in enumerate(_split3_np(slopes[2 * p + e] * ATT_T * np.arange(nb))):
                kb[p, e, :, base + AUG + a] = term
            ka[p, e, :, base + 2 * AUG:base + 3 * AUG] = 1.0
    return jnp.asarray(ka), jnp.asarray(kb)


def _head_masks():
    lane = lax.broadcasted_iota(jnp.int32, (1, 128), 1)
    first = lane < HEAD_DIM

    def ones(e, n):
        base = HEAD_DIM * (1 - e)
        return ((lane >= base) & (lane < base + n)).astype(F32)

    return first, lane, ones


def _place3(lane, at, terms, other):
    for a, term in enumerate(terms):
        other = jnp.where(lane == at + a, term, other)
    return other


def attn_fwd(z, logc, ka, kb, gathering):
    s = z.shape[0]
    nq = s // ATT_T
    t = ATT_T
    n = len(gathering)
    grp = ATT_GROUP
    ngrp = HEADS // 2 // grp
    wide = 128 * grp
    qcol, kcol, vcol = 2 * D // wide, 3 * D // wide, 4 * D // wide

    def body(*refs):
        q_ref, k_ref, v_ref, lc_ref, ka_ref, kb_ref = refs[:6]
        y_ref, lse_ref = refs[6 + n:8 + n]
        q_s, k_s, v_s, m_s, l_s, acc_s = refs[8 + 2 * n:14 + 2 * n]
        gi, qi = pl.program_id(0), pl.program_id(1)
        first, lane, ones = _head_masks()
        if n:
            send, pass_on, finish = _gather_phases(refs[8 + n:8 + 2 * n], *refs[14 + 2 * n:], _spans(gathering))
            pl.when((gi == 0) & (qi == 0))(send)
            pl.when((gi == ngrp - 1) & (qi == nq - 1))(pass_on)

        @pl.when(qi == 0)
        def _():
            sel = jnp.broadcast_to(first.astype(F32), (t, 128))
            for pr in range(grp):
                cols = slice(pr * 128, (pr + 1) * 128)
                for jb in range(nq):
                    kj = k_ref[jb * t:(jb + 1) * t, cols].astype(F32)
                    vj = v_ref[jb * t:(jb + 1) * t, cols].astype(F32)
                    k_s[pr, 0, jb] = jnp.where(first, kj, ka_ref[pr, 0] + kb_ref[pr, 0, jb:jb + 1, :]).astype(BF16)
                    k_s[pr, 1, jb] = jnp.where(first, ka_ref[pr, 1] + kb_ref[pr, 1, jb:jb + 1, :], kj).astype(BF16)
                    v_s[pr, jb, 0:t, 0:128] = jnp.where(first, vj, 0.0).astype(BF16)
                    v_s[pr, jb, t:2 * t, 0:128] = jnp.where(first, 0.0, vj).astype(BF16)
                    v_s[pr, jb, 0:t, 128:256] = sel.astype(BF16)
                    v_s[pr, jb, t:2 * t, 128:256] = (1.0 - sel).astype(BF16)

        for pr in range(grp):
            q = q_ref[:, pr * 128:(pr + 1) * 128].astype(F32) * (1.0 / math.sqrt(HEAD_DIM))
            q_s[pr, 0] = jnp.where(first, q, ones(0, 2 * AUG)).astype(BF16)
            q_s[pr, 1] = jnp.where(first, ones(1, 2 * AUG), q).astype(BF16)
        m_s[...] = jnp.full_like(m_s, MASKED)
        l_s[...] = jnp.zeros_like(l_s)
        acc_s[...] = jnp.zeros_like(acc_s)

        def scores(j):
            return tuple(_dot(q_s[pr, e], k_s[pr, e, j], NT) for pr in range(grp) for e in range(2))

        def step(j, carry):
            softmax_block(j, scores(j))
            return carry

        def softmax_block(j, u):
            lc = lc_ref[qi - j]
            for pr in range(grp):
                u0 = u[2 * pr] + lc
                u1 = u[2 * pr + 1] + lc
                m0, m1 = m_s[pr, 0], m_s[pr, 1]
                n0 = jnp.maximum(m0, jnp.max(u0, axis=-1, keepdims=True))
                n1 = jnp.maximum(m1, jnp.max(u1, axis=-1, keepdims=True))
                m_s[pr, 0], m_s[pr, 1] = n0, n1
                p = jnp.concatenate([jnp.exp(u0 - jnp.concatenate([n0, n0], axis=1)).astype(BF16),
                                     jnp.exp(u1 - jnp.concatenate([n1, n1], axis=1)).astype(BF16)], axis=1)
                pv = _dot(p, v_s[pr, j])
                alpha = jnp.where(first, jnp.exp(m0 - n0), jnp.exp(m1 - n1))
                acc_s[pr] = acc_s[pr] * alpha + pv[:, 0:128]
                l_s[pr] = l_s[pr] * alpha + pv[:, 128:256]

        lax.fori_loop(0, qi + 1, step, 0)
        for pr in range(grp):
            cols = slice(pr * 128, (pr + 1) * 128)
            y_ref[:, cols] = (acc_s[pr] / l_s[pr]).astype(BF16)
            lse_ref[:, cols] = jnp.where(first, m_s[pr, 0], m_s[pr, 1]) + jnp.log(l_s[pr])
        if n:
            pl.when((gi == ngrp - 1) & (qi == nq - 1))(finish)

    out = pl.pallas_call(
        body, name="attn_fwd", grid=(ngrp, nq),
        in_specs=[pl.BlockSpec((t, wide), lambda g, i: (i, qcol + g)),
                  pl.BlockSpec((s, wide), lambda g, i: (0, kcol + g)),
                  pl.BlockSpec((s, wide), lambda g, i: (0, vcol + g)),
                  _full((nq, t, t)),
                  pl.BlockSpec((grp, 2, t, 128), lambda g, i: (g, 0, 0, 0)),
                  pl.BlockSpec((grp, 2, nq, 128), lambda g, i: (g, 0, 0, 0))] + [ANY] * n,
        out_specs=[pl.BlockSpec((t, wide), lambda g, i: (i, g)), pl.BlockSpec((t, wide), lambda g, i: (i, g))]
        + [ANY] * n,
        out_shape=[jax.ShapeDtypeStruct((s, D), BF16), jax.ShapeDtypeStruct((s, D), F32)]
        + [jax.ShapeDtypeStruct(a.shape, a.dtype) for a in _arrays(gathering)],
        input_output_aliases={6 + w: 2 + w for w in range(n)},
        scratch_shapes=[pltpu.VMEM((grp, 2, t, 128), BF16), pltpu.VMEM((grp, 2, nq, t, 128), BF16),
                        pltpu.VMEM((grp, nq, 2 * t, 256), BF16), pltpu.VMEM((grp, 2, t, 128), F32),
                        pltpu.VMEM((grp, t, 128), F32), pltpu.VMEM((grp, t, 128), F32)]
        + (_gather_sems(n) if n else []),
        compiler_params=_params("arbitrary", "arbitrary", communicates=bool(n)),
    )(z, z, z, logc, ka, kb, *_arrays(gathering))
    return out[0], out[1], out[2:]


def proj_merge(ya, yb, wa, wb, z, bg, gathering):
    s = ya.shape[0]
    tm = 512
    n = len(gathering)
    steps = s // tm

    def body(*refs):
        ya_ref, yb_ref, wa_ref, wb_ref, ga_ref, gb_ref, bg_ref = refs[:7]
        mg_ref, pa_ref, pb_ref = refs[7 + n:10 + n]
        i = pl.program_id(0)
        if n:
            send, pass_on, finish = _gather_phases(refs[10 + n:10 + 2 * n], *refs[10 + 2 * n:], _spans(gathering))
            pl.when(i == 0)(send)
            pl.when(i == steps - 1)(pass_on)
        pa = _dot(ya_ref[...], wa_ref[...])
        pb = _dot(yb_ref[...], wb_ref[...])
        sa = _sigmoid(ga_ref[...] + bg_ref[0:1, :])
        sb = _sigmoid(gb_ref[...] + bg_ref[1:2, :])
        mg_ref[...] = (sa * pa + sb * pb).astype(BF16)
        pa_ref[...] = pa.astype(BF16)
        pb_ref[...] = pb.astype(BF16)
        if n:
            pl.when(i == steps - 1)(finish)

    out = jax.ShapeDtypeStruct((s, D), BF16)
    res = pl.pallas_call(
        body, name="proj_merge", grid=(steps,),
        in_specs=[_rows(tm, D), _rows(tm, D), _full((D, D)), _full((D, D)),
                  _rows(tm, D, 5), _rows(tm, D, 6), _full((2, D))] + [ANY] * n,
        out_specs=[_rows(tm, D)] * 3 + [ANY] * n,
        out_shape=[out] * 3 + [jax.ShapeDtypeStruct(a.shape, a.dtype) for a in _arrays(gathering)],
        input_output_aliases={7 + w: 3 + w for w in range(n)},
        scratch_shapes=_gather_sems(n) if n else [],
        compiler_params=_params("arbitrary", communicates=bool(n)),
    )(ya, yb, wa, wb, z, z, bg, *_arrays(gathering))
    return res[0], res[1], res[2], res[3:]


def out_norm(merged, w_out, x, g_post, g_fpre, gathering):
    s = x.shape[0]
    tm = 512
    n = len(gathering)
    steps = s // tm

    def body(*refs):
        mg_ref, w_ref, x_ref, gp_ref, gf_ref = refs[:5]
        o_ref, x1_ref, h2_ref = refs[5 + n:8 + n]
        i = pl.program_id(0)
        if n:
            send, pass_on, finish = _gather_phases(refs[8 + n:8 + 2 * n], *refs[8 + 2 * n:], _spans(gathering))
            pl.when(i == 0)(send)
            pl.when(i == steps - 1)(pass_on)
        o = _dot(mg_ref[...], w_ref[...])
        ohat, _ = _rms(o)
        x1 = x_ref[...] + ohat * gp_ref[...]
        x1hat, _ = _rms(x1)
        o_ref[...] = o
        x1_ref[...] = x1
        h2_ref[...] = (x1hat * gf_ref[...]).astype(BF16)
        if n:
            pl.when(i == steps - 1)(finish)

    res = pl.pallas_call(
        body, name="out_norm", grid=(steps,),
        in_specs=[_rows(tm, D), _full((D, D)), _rows(tm, D), _full((1, D)), _full((1, D))] + [ANY] * n,
        out_specs=[_rows(tm, D)] * 3 + [ANY] * n,
        out_shape=[jax.ShapeDtypeStruct((s, D), F32), jax.ShapeDtypeStruct((s, D), F32),
                   jax.ShapeDtypeStruct((s, D), BF16)]
        + [jax.ShapeDtypeStruct(a.shape, a.dtype) for a in _arrays(gathering)],
        input_output_aliases={5 + w: 3 + w for w in range(n)},
        scratch_shapes=_gather_sems(n) if n else [],
        compiler_params=_params("arbitrary", communicates=bool(n)),
    )(merged, w_out, x, g_post, g_fpre, *_arrays(gathering))
    return res[0], res[1], res[2], res[3:]


def mm_ff1(h2, wg, gathering):
    s = h2.shape[0]
    tm = 1024
    n = len(gathering)
    ni = s // tm

    def body(*refs):
        a_ref, b_ref = refs[:2]
        o_ref, r_ref = refs[2 + n:4 + n]
        i, j = pl.program_id(0), pl.program_id(1)
        if n:
            send, pass_on, finish = _gather_phases(refs[4 + n:4 + 2 * n], *refs[4 + 2 * n:], _spans(gathering))
            pl.when((i == 0) & (j == 0))(send)
            pl.when((i == ni - 1) & (j == N_CHIPS // 2))(pass_on)
        a = _dot(a_ref[...], b_ref[...])
        o_ref[...] = a.astype(BF16)
        r = jnp.maximum(a, 0.0)
        r_ref[...] = (r * r).astype(BF16)
        if n:
            pl.when((i == ni - 1) & (j == N_CHIPS - 1))(finish)

    res = pl.pallas_call(
        body, name="mm_ff1", grid=(ni, N_CHIPS),
        in_specs=[pl.BlockSpec((tm, D), lambda i, j: (i, 0)), pl.BlockSpec((None, D, D), lambda i, j: (j, 0, 0))]
        + [ANY] * n,
        out_specs=[pl.BlockSpec((tm, D), lambda i, j: (i, j))] * 2 + [ANY] * n,
        out_shape=[jax.ShapeDtypeStruct((s, D_FF), BF16), jax.ShapeDtypeStruct((s, D_FF), BF16)]
        + [jax.ShapeDtypeStruct(a.shape, a.dtype) for a in _arrays(gathering)],
        input_output_aliases={2 + w: 2 + w for w in range(n)},
        scratch_shapes=_gather_sems(n) if n else [],
        compiler_params=_params("arbitrary", "arbitrary", communicates=bool(n)),
    )(h2, wg, *_arrays(gathering))
    return res[0], res[1], res[2:]


def ff2_loss(rl, w_ff2, x1, target, g_fpost):
    s = x1.shape[0]
    tm = 256

    def body(rl_ref, w_ref, x1_ref, t_ref, g_ref, dy_ref, df_ref, dg_ref, loss_ref):
        @pl.when(pl.program_id(0) == 0)
        def _():
            dg_ref[...] = jnp.zeros_like(dg_ref)
            loss_ref[...] = jnp.zeros_like(loss_ref)

        f = _dot(rl_ref[...], w_ref[...])
        fhat, r = _rms(f)
        err = x1_ref[...] + fhat * g_ref[...] - t_ref[...]
        loss_ref[...] += 0.5 * jnp.sum(jnp.mean(err * err, axis=-1, keepdims=True), axis=0, keepdims=True)
        dy = err * (1.0 / D)
        dy_ref[...] = dy
        dg_ref[...] += jnp.sum(dy * fhat, axis=0, keepdims=True)
        df_ref[...] = _rms_bwd(dy * g_ref[...], fhat, r).astype(BF16)

    return pl.pallas_call(
        body, name="ff2_loss", grid=(s // tm,),
        in_specs=[_rows(tm, D_FF), _full((D_FF, D)), _rows(tm, D), _rows(tm, D), _full((1, D))],
        out_specs=[_rows(tm, D), _rows(tm, D), _full((1, D)), _full((1, 1))],
        out_shape=[jax.ShapeDtypeStruct((s, D), F32), jax.ShapeDtypeStruct((s, D), BF16),
                   jax.ShapeDtypeStruct((1, D), F32), jax.ShapeDtypeStruct((1, 1), F32)],
        compiler_params=_params("arbitrary"),
    )(rl, w_ff2, x1, target, g_fpost)


def mm_tn(name, a, b, ta, tb, out_shape, out_spec):
    s = a.shape[0]

    def body(a_ref, b_ref, o_ref):
        o_ref[...] = _dot(a_ref[...], b_ref[...], TN)

    return pl.pallas_call(
        body, name=name, grid=(a.shape[1] // ta, b.shape[1] // tb),
        in_specs=[pl.BlockSpec((s, ta), lambda i, j: (0, i)), pl.BlockSpec((s, tb), lambda i, j: (0, j))],
        out_specs=out_spec, out_shape=jax.ShapeDtypeStruct(out_shape, F32),
        compiler_params=_params("parallel", "parallel"),
    )(a, b)


def mm_nt(name, a, w):
    s = a.shape[0]
    tm = 512

    def body(a_ref, w_ref, o_ref):
        o_ref[...] = _dot(a_ref[...], w_ref[...], NT).astype(BF16)

    return pl.pallas_call(
        body, name=name, grid=(s // tm,), in_specs=[_rows(tm, D), _full((D, D))], out_specs=_rows(tm, D),
        out_shape=jax.ShapeDtypeStruct((s, D), BF16), compiler_params=_params("parallel"),
    )(a, w)


def ff2_bwd(df, w_ff2, a):
    s = df.shape[0]
    tm = 1024

    def body(df_ref, w_ref, a_ref, da_ref):
        drl = _dot(df_ref[...], w_ref[...], NT)
        da_ref[...] = (drl * (2.0 * jnp.maximum(a_ref[...].astype(F32), 0.0))).astype(BF16)

    return pl.pallas_call(
        body, name="ff2_bwd", grid=(s // tm, D_FF // D),
        in_specs=[pl.BlockSpec((tm, D), lambda i, j: (i, 0)), pl.BlockSpec((D, D), lambda i, j: (j, 0)),
                  pl.BlockSpec((tm, D), lambda i, j: (i, j))],
        out_specs=pl.BlockSpec((tm, D), lambda i, j: (i, j)),
        out_shape=jax.ShapeDtypeStruct((s, D_FF), BF16), compiler_params=_params("parallel", "parallel"),
    )(df, w_ff2, a)


def ff1_bwd_norms(da, wg, x1, o, dy, g_fpre, g_post, swapping):
    s = x1.shape[0]
    tm = 512
    n = len(swapping)

    def body(*refs):
        da_ref, w_ref, x1_ref, o_ref, dy_ref, gf_ref, gp_ref = refs[:7]
        dx1_ref, do_ref, dgf_ref, dgp_ref = refs[7 + n:11 + n]
        acc_ref = refs[11 + 2 * n]
        i, k = pl.program_id(0), pl.program_id(1)
        if n:
            send, finish = _swap_phases(refs[7:7 + n], refs[11 + n:11 + 2 * n], *refs[12 + 2 * n:])
            pl.when((i == 0) & (k == 0))(send)

        @pl.when((i == 0) & (k == 0))
        def _():
            dgf_ref[...] = jnp.zeros_like(dgf_ref)
            dgp_ref[...] = jnp.zeros_like(dgp_ref)

        part = _dot(da_ref[...], w_ref[...], NT)

        @pl.when(k == 0)
        def _():
            acc_ref[...] = part

        @pl.when(k > 0)
        def _():
            acc_ref[...] += part

        @pl.when(k == N_CHIPS - 1)
        def _():
            dh2 = acc_ref[...]
            x1hat, r2 = _rms(x1_ref[...])
            dgf_ref[...] += jnp.sum(dh2 * x1hat, axis=0, keepdims=True)
            dx1 = dy_ref[...] + _rms_bwd(dh2 * gf_ref[...], x1hat, r2)
            ohat, r1 = _rms(o_ref[...])
            dgp_ref[...] += jnp.sum(dx1 * ohat, axis=0, keepdims=True)
            dx1_ref[...] = dx1
            do_ref[...] = _rms_bwd(dx1 * gp_ref[...], ohat, r1).astype(BF16)

        if n:
            pl.when((i == s // tm - 1) & (k == N_CHIPS - 1))(finish)

    row = pl.BlockSpec((tm, D), lambda i, k: (i, 0))
    vec = pl.BlockSpec((1, D), lambda i, k: (0, 0))
    res = pl.pallas_call(
        body, name="ff1_bwd_norms", grid=(s // tm, N_CHIPS),
        in_specs=[pl.BlockSpec((tm, D), lambda i, k: (i, k)), pl.BlockSpec((None, D, D), lambda i, k: (k, 0, 0)),
                  row, row, row, vec, vec] + [ANY] * n,
        out_specs=[row, row, vec, vec] + [ANY] * n,
        out_shape=[jax.ShapeDtypeStruct((s, D), F32), jax.ShapeDtypeStruct((s, D), BF16),
                   jax.ShapeDtypeStruct((1, D), F32), jax.ShapeDtypeStruct((1, D), F32)] + _swap_shapes(swapping),
        scratch_shapes=[pltpu.VMEM((tm, D), F32)] + (_swap_sems(n) if n else []),
        compiler_params=_params("arbitrary", "arbitrary", communicates=bool(n)),
    )(da, wg, x1, o, dy, g_fpre, g_post, *swapping)
    return res[0], res[1], res[2], res[3], res[4:]


def out_bwd_gates(do, w_out, pa, pb, z, bg):
    s = do.shape[0]
    tm = 512

    def body(do_ref, w_ref, pa_ref, pb_ref, ga_ref, gb_ref, bg_ref, dpa_ref, dpb_ref, dga_ref, dgb_ref, dbg_ref):
        @pl.when(pl.program_id(0) == 0)
        def _():
            dbg_ref[...] = jnp.zeros_like(dbg_ref)

        dm = _dot(do_ref[...], w_ref[...], NT)
        sa = _sigmoid(ga_ref[...] + bg_ref[0:1, :])
        sb = _sigmoid(gb_ref[...] + bg_ref[1:2, :])
        dpa_ref[...] = (dm * sa).astype(BF16)
        dpb_ref[...] = (dm * sb).astype(BF16)
        dga = dm * pa_ref[...].astype(F32) * (sa * (1.0 - sa))
        dgb = dm * pb_ref[...].astype(F32) * (sb * (1.0 - sb))
        dga_ref[...] = dga.astype(BF16)
        dgb_ref[...] = dgb.astype(BF16)
        dbg_ref[0:1, :] += jnp.sum(dga, axis=0, keepdims=True)
        dbg_ref[1:2, :] += jnp.sum(dgb, axis=0, keepdims=True)

    out = jax.ShapeDtypeStruct((s, D), BF16)
    return pl.pallas_call(
        body, name="out_bwd_gates", grid=(s // tm,),
        in_specs=[_rows(tm, D), _full((D, D)), _rows(tm, D), _rows(tm, D), _rows(tm, D, 5), _rows(tm, D, 6),
                  _full((2, D))],
        out_specs=[_rows(tm, D)] * 4 + [_full((2, D))],
        out_shape=[out] * 4 + [jax.ShapeDtypeStruct((2, D), F32)], compiler_params=_params("arbitrary"),
    )(do, w_out, pa, pb, z, z, bg)


def gating_bwd(z, dya, ln_g, ln_b, w_s, bs_t, swapping):
    s = z.shape[0]
    ones = functools.partial(jnp.ones, (8, CHUNK), BF16)
    n = len(swapping)

    def body(*refs):
        u_ref, v_ref, dya_ref, lg_ref, lb_ref, ws_ref, bst_ref = refs[:7]
        du_ref, dv_ref, dws_ref, dbs_ref, dlg_ref, dlb_ref = refs[7 + n:13 + n]
        dvn_ref = refs[13 + 2 * n]
        ci = pl.program_id(0)
        if n:
            send, finish = _swap_phases(refs[7:7 + n], refs[13 + n:13 + 2 * n], *refs[14 + 2 * n:])
            pl.when(ci == 0)(send)

        @pl.when(ci == 0)
        def _():
            dws_ref[...] = jnp.zeros_like(dws_ref)
            dbs_ref[...] = jnp.zeros_like(dbs_ref)
            dlg_ref[...] = jnp.zeros_like(dlg_ref)
            dlb_ref[...] = jnp.zeros_like(dlb_ref)

        ug, dug_du = _gelu_and_grad(u_ref[...].astype(F32))
        vg, dvg_dv = _gelu_and_grad(v_ref[...].astype(F32))
        vhat, rstd = _layer_norm(vg)
        vn = (vhat * lg_ref[...] + lb_ref[...]).astype(BF16)
        dya = dya_ref[...].astype(F32)
        for g in range(GROUPS):
            cols = slice(g * CHUNK, (g + 1) * CHUNK)
            ws = _tril_ws(ws_ref, g)
            mixed = _dot(ws, vn[:, cols]) + bst_ref[:, g:g + 1]
            du_ref[:, cols] = (dya[:, cols] * mixed * dug_du[:, cols]).astype(BF16)
            dmix = (dya[:, cols] * ug[:, cols]).astype(BF16)
            dbs_ref[g] += _dot(ones(), dmix, NT)
            dws_ref[g] += _dot(dmix, vn[:, cols], NT)
            dvn_ref[:, cols] = _dot(ws, dmix, TN)
        dvn = dvn_ref[...]
        dlg_ref[...] += jnp.sum(dvn * vhat, axis=0, keepdims=True)
        dlb_ref[...] += jnp.sum(dvn, axis=0, keepdims=True)
        dvh = dvn * lg_ref[...]
        dvg = rstd * (dvh - jnp.mean(dvh, axis=-1, keepdims=True)
                      - vhat * jnp.mean(dvh * vhat, axis=-1, keepdims=True))
        dv_ref[...] = (dvg * dvg_dv).astype(BF16)

        @pl.when(ci == pl.num_programs(0) - 1)
        def _():
            r = lax.broadcasted_iota(jnp.int32, (CHUNK, CHUNK), 0)
            c = lax.broadcasted_iota(jnp.int32, (CHUNK, CHUNK), 1)
            for g in range(GROUPS):
                dws_ref[g] = jnp.where(c <= r, dws_ref[g], 0.0)

        if n:
            pl.when(ci == pl.num_programs(0) - 1)(finish)

    out = jax.ShapeDtypeStruct((s, D), BF16)
    res = pl.pallas_call(
        body, name="gating_bwd", grid=(s // CHUNK,),
        in_specs=[_rows(CHUNK, D, 0), _rows(CHUNK, D, 1), _rows(CHUNK, D), _full((1, D)), _full((1, D)),
                  _full((GROUPS, CHUNK, CHUNK)), _full((CHUNK, GROUPS))] + [ANY] * n,
        out_specs=[_rows(CHUNK, D), _rows(CHUNK, D), _full((GROUPS, CHUNK, CHUNK)), _full((GROUPS, 8, CHUNK)),
                   _full((1, D)), _full((1, D))] + [ANY] * n,
        out_shape=[out, out, jax.ShapeDtypeStruct((GROUPS, CHUNK, CHUNK), F32),
                   jax.ShapeDtypeStruct((GROUPS, 8, CHUNK), F32),
                   jax.ShapeDtypeStruct((1, D), F32), jax.ShapeDtypeStruct((1, D), F32)] + _swap_shapes(swapping),
        scratch_shapes=[pltpu.VMEM((CHUNK, D), F32)] + (_swap_sems(n) if n else []),
        compiler_params=_params("arbitrary", communicates=bool(n)),
    )(z, z, dya, ln_g, ln_b, w_s, bs_t, *swapping)
    return (*res[:6], res[6:])


def attn_bwd(z, yb, dyb, lse, logc, ka, kb, scattering, gathering=None):
    s = z.shape[0]
    nq = s // ATT_T
    t = ATT_T
    grp = ATT_BWD_GROUP
    ngrp = HEADS // 2 // grp
    wide = 128 * grp
    qcol, kcol, vcol = 2 * D // wide, 3 * D // wide, 4 * D // wide
    scale = 1.0 / math.sqrt(HEAD_DIM)
    n = len(scattering)
    g8 = 0 if gathering is None else 1

    def body(*refs):
        q_ref, k_ref, v_ref, y_ref, dy_ref, lse_ref, lc_ref, ka_ref, kb_ref = refs[:9]
        dq_ref, dk_ref, dv_ref = refs[9 + n + g8:12 + n + g8]
        qa_s, qt_s, da_s, dt_s, dq_s, dkt_s, dvt_s = refs[12 + 2 * n + 2 * g8:19 + 2 * n + 2 * g8]
        sems = refs[19 + 2 * n + 2 * g8:]
        gi, j = pl.program_id(0), pl.program_id(1)
        first, lane, ones = _head_masks()
        if n:
            send, finish = _scatter_phases(refs[9:9 + n], refs[12 + n + g8:12 + 2 * n + g8], *sems[:2])
            pl.when((gi == 0) & (j == 0))(send)
        if g8:
            send8, pass_on8, finish8 = _allgather8_phases(refs[12 + 2 * n + g8], *sems[2 * (n > 0):])
            pl.when((gi == 0) & (j == 0))(send8)
            pl.when((gi == ngrp - 1) & (j == nq - 1))(pass_on8)

        @pl.when(j == 0)
        def _():
            dq_s[...] = jnp.zeros_like(dq_s)
            for pr in range(grp):
                cols = slice(pr * 128, (pr + 1) * 128)
                for ib in range(nq):
                    rows = slice(ib * t, (ib + 1) * t)
                    q = q_ref[rows, cols].astype(F32) * scale
                    lse = lse_ref[rows, cols]
                    qa_s[pr, 0, ib] = jnp.where(first, q, _place3(lane, HEAD_DIM + 2 * AUG, _split3(-lse[:, 0:1]),
                                                                  ones(0, 2 * AUG))).astype(BF16)
                    qa_s[pr, 1, ib] = jnp.where(
                        first, _place3(lane, 2 * AUG, _split3(-lse[:, HEAD_DIM:HEAD_DIM + 1]), ones(1, 2 * AUG)),
                        q).astype(BF16)
                    qt_s[pr, ib, :, 0:t] = jnp.where(first, q, 0.0).T.astype(BF16)
                    qt_s[pr, ib, :, t:2 * t] = jnp.where(first, 0.0, q).T.astype(BF16)
                    do = dy_ref[rows, cols].astype(F32)
                    prod = do * y_ref[rows, cols].astype(F32)
                    dd0 = jnp.sum(jnp.where(first, prod, 0.0), axis=-1, keepdims=True)
                    dd1 = jnp.sum(jnp.where(first, 0.0, prod), axis=-1, keepdims=True)
                    da_s[pr, 0, ib] = jnp.where(first, do, _place3(lane, HEAD_DIM, _split3(-dd0), 0.0)).astype(BF16)
                    da_s[pr, 1, ib] = jnp.where(first, _place3(lane, 0, _split3(-dd1), 0.0), do).astype(BF16)
                    dt_s[pr, ib, :, 0:t] = jnp.where(first, do, 0.0).T.astype(BF16)
                    dt_s[pr, ib, :, t:2 * t] = jnp.where(first, 0.0, do).T.astype(BF16)

        keys = []
        for pr in range(grp):
            kj = k_ref[:, pr * 128:(pr + 1) * 128].astype(F32)
            vj = v_ref[:, pr * 128:(pr + 1) * 128].astype(F32)
            keys.append((
                jnp.where(first, kj, ka_ref[pr, 0] + kb_ref[pr, 0, pl.ds(j, 1), :]).astype(BF16),
                jnp.where(first, ka_ref[pr, 1] + kb_ref[pr, 1, pl.ds(j, 1), :], kj).astype(BF16),
                jnp.concatenate([jnp.where(first, kj, 0.0), jnp.where(first, 0.0, kj)], axis=0).astype(BF16),
                jnp.where(first, vj, ones(0, AUG)).astype(BF16),
                jnp.where(first, ones(1, AUG), vj).astype(BF16)))
        dkt_s[...] = jnp.zeros_like(dkt_s)
        dvt_s[...] = jnp.zeros_like(dvt_s)

        def step(i, _):
            lc = lc_ref[i - j]
            rows = pl.ds(pl.multiple_of(i * t, t), t)
            for pr in range(grp):
                k0a, k1a, kst, v0a, v1a = keys[pr]
                p0 = jnp.exp(_dot(qa_s[pr, 0, i], k0a, NT) + lc)
                p1 = jnp.exp(_dot(qa_s[pr, 1, i], k1a, NT) + lc)
                e0 = (p0 * _dot(da_s[pr, 0, i], v0a, NT)).astype(BF16)
                e1 = (p1 * _dot(da_s[pr, 1, i], v1a, NT)).astype(BF16)
                dq_s[pr, rows, :] += _dot(jnp.concatenate([e0, e1], axis=1), kst)
                dvt_s[pr] += _dot(dt_s[pr, i], jnp.concatenate([p0.astype(BF16), p1.astype(BF16)], axis=0))
                dkt_s[pr] += _dot(qt_s[pr, i], jnp.concatenate([e0, e1], axis=0))
            return 0

        lax.fori_loop(j, nq, step, 0)
        for pr in range(grp):
            dk_ref[:, pr * 128:(pr + 1) * 128] = dkt_s[pr].T.astype(BF16)
            dv_ref[:, pr * 128:(pr + 1) * 128] = dvt_s[pr].T.astype(BF16)

        @pl.when(j == nq - 1)
        def _():
            for pr in range(grp):
                dq_ref[:, pr * 128:(pr + 1) * 128] = (dq_s[pr] * scale).astype(BF16)

        if n:
            pl.when((gi == ngrp - 1) & (j == nq - 1))(finish)
        if g8:
            pl.when((gi == ngrp - 1) & (j == nq - 1))(finish8)

    colblock = lambda c: pl.BlockSpec((s, wide), lambda g, j: (0, c + g))
    blk = lambda c: pl.BlockSpec((t, wide), lambda g, j: (j, c + g))
    out = jax.ShapeDtypeStruct((s, D), BF16)
    res = pl.pallas_call(
        body, name="attn_bwd", grid=(ngrp, nq),
        in_specs=[colblock(qcol), blk(kcol), blk(vcol), colblock(0), colblock(0), colblock(0),
                  _full((nq, t, t)), pl.BlockSpec((grp, 2, t, 128), lambda g, j: (g, 0, 0, 0)),
                  pl.BlockSpec((grp, 2, nq, 128), lambda g, j: (g, 0, 0, 0))] + [ANY] * (n + g8),
        out_specs=[colblock(0), blk(0), blk(0)] + [ANY] * (n + g8),
        out_shape=[out] * 3 + _scatter_shapes(scattering)
        + ([jax.ShapeDtypeStruct(gathering.shape, gathering.dtype)] if g8 else []),
        input_output_aliases={9 + n: 3 + n} if g8 else {},
        scratch_shapes=[pltpu.VMEM((grp, 2, nq, t, 128), BF16), pltpu.VMEM((grp, nq, 128, 2 * t), BF16),
                        pltpu.VMEM((grp, 2, nq, t, 128), BF16), pltpu.VMEM((grp, nq, 128, 2 * t), BF16),
                        pltpu.VMEM((grp, s, 128), F32), pltpu.VMEM((grp, 128, t), F32),
                        pltpu.VMEM((grp, 128, t), F32)]
        + (_scatter_sems(n) if n else [])
        + ([pltpu.SemaphoreType.DMA((7,)), pltpu.SemaphoreType.DMA((7,))] if g8 else []),
        compiler_params=_params("arbitrary", "arbitrary", communicates=bool(n + g8)),
    )(z, z, z, yb, dyb, lse, logc, ka, kb, *scattering, *([gathering] if g8 else []))
    return res[0], res[1], res[2], res[3:3 + n], (res[3 + n] if g8 else None)


def in_bwd_norm(dz, wg, x, dx1, g_pre, scattering, gathering=None):
    s = x.shape[0]
    tm = 512
    n = len(scattering)
    g = 0 if gathering is None else 1
    last = (s // tm - 1, N_CHIPS - 1)

    def body(*refs):
        dz_ref, w_ref, x_ref, dx1_ref, g_ref = refs[:5]
        dx_ref, dg_ref = refs[5 + n + g:7 + n + g]
        acc_ref = refs[7 + 2 * n + 2 * g]
        sems = refs[8 + 2 * n + 2 * g:]
        i, k = pl.program_id(0), pl.program_id(1)
        if n:
            send, finish = _scatter_phases(refs[5:5 + n], refs[7 + n + g:7 + 2 * n + g], *sems[:2])
            pl.when((i == 0) & (k == 0))(send)
        if g:
            send8, pass_on8, finish8 = _allgather8_phases(refs[7 + 2 * n + g], *sems[2 * (n > 0):])
            pl.when((i == 0) & (k == 0))(send8)
            pl.when((i == last[0]) & (k == last[1]))(pass_on8)

        @pl.when((i == 0) & (k == 0))
        def _():
            dg_ref[...] = jnp.zeros_like(dg_ref)

        part = _dot(dz_ref[...], w_ref[...], NT)

        @pl.when(k == 0)
        def _():
            acc_ref[...] = part

        @pl.when(k > 0)
        def _():
            acc_ref[...] += part

        @pl.when(k == N_CHIPS - 1)
        def _():
            dh = acc_ref[...]
            xhat, r = _rms(x_ref[...])
            dg_ref[...] += jnp.sum(dh * xhat, axis=0, keepdims=True)
            dx_ref[...] = dx1_ref[...] + _rms_bwd(dh * g_ref[...], xhat, r)

        if n:
            pl.when((i == last[0]) & (k == last[1]))(finish)
        if g:
            pl.when((i == last[0]) & (k == last[1]))(finish8)

    row = pl.BlockSpec((tm, D), lambda i, k: (i, 0))
    vec = pl.BlockSpec((1, D), lambda i, k: (0, 0))
    res = pl.pallas_call(
        body, name="in_bwd_norm", grid=(s // tm, N_CHIPS),
        in_specs=[pl.BlockSpec((tm, IN_SHARD), lambda i, k: (i, k)),
                  pl.BlockSpec((None, D, IN_SHARD), lambda i, k: (k, 0, 0)), row, row, vec] + [ANY] * (n + g),
        out_specs=[row, vec] + [ANY] * (n + g),
        out_shape=[jax.ShapeDtypeStruct((s, D), F32), jax.ShapeDtypeStruct((1, D), F32)]
        + _scatter_shapes(scattering) + ([jax.ShapeDtypeStruct(gathering.shape, gathering.dtype)] if g else []),
        input_output_aliases={5 + n: 2 + n} if g else {},
        scratch_shapes=[pltpu.VMEM((tm, D), F32)] + (_scatter_sems(n) if n else [])
        + ([pltpu.SemaphoreType.DMA((7,)), pltpu.SemaphoreType.DMA((7,))] if g else []),
        compiler_params=_params("arbitrary", "arbitrary", communicates=bool(n + g)),
    )(dz, wg, x, dx1, g_pre, *scattering, *([gathering] if g else []))
    return res[0], res[1], res[2:2 + n], (res[2 + n] if g else None)


def _adamw_math(w, g, m, v):
    m = ADAM_B1 * m + (1.0 - ADAM_B1) * g
    v = ADAM_B2 * v + (1.0 - ADAM_B2) * (g * g)
    m_hat = m / (1.0 - ADAM_B1 ** ADAM_STEP)
    v_hat = v / (1.0 - ADAM_B2 ** ADAM_STEP)
    delta = -ADAM_LR * (m_hat / (jnp.sqrt(v_hat) + ADAM_EPS) + ADAM_WD * w)
    return delta, m, v


def adamw(name, w, g, m, v, tr):
    r, c = w.shape

    def body(w_ref, g_ref, m_ref, v_ref, go_ref, d_ref, nm_ref, nv_ref):
        g = g_ref[...]
        go_ref[...] = g
        d_ref[...], nm_ref[...], nv_ref[...] = _adamw_math(w_ref[...], g, m_ref[...], v_ref[...])

    out = jax.ShapeDtypeStruct((r, c), F32)
    return pl.pallas_call(
        body, name=name, grid=(r // tr,), in_specs=[_rows(tr, c)] * 4, out_specs=[_rows(tr, c)] * 4,
        out_shape=[out] * 4, compiler_params=_params("parallel"),
    )(w, g, m, v)


def _allgather8_phases(buf, send_sems, recv_sems):
    x, y, c, chips = _place()
    me = 2 * x + y
    sibling = (x, y, 1 - c)
    rows = buf.shape[1] // 2

    def part(chip, core):
        return buf.at[chip, pl.ds(core * rows, rows)]

    def copy(k, block, to):
        return pltpu.make_async_remote_copy(src_ref=block, dst_ref=block, send_sem=send_sems.at[k],
                                            recv_sem=recv_sems.at[k], device_id=to, device_id_type=MESH)

    def chip_of(j):
        return 2 * chips[j][0] + chips[j][1]

    def send():
        copy(0, part(me, c), sibling).start()
        for j in range(3):
            copy(1 + j, part(me, c), (chips[j][0], chips[j][1], c)).start()

    def pass_on():
        for j in range(3):
            copy(1 + j, part(chip_of(j), c), (chips[j][0], chips[j][1], c)).wait_recv()
            copy(4 + j, part(chip_of(j), c), sibling).start()

    def finish():
        copy(0, part(me, 1 - c), sibling).wait_recv()
        for j in range(3):
            copy(4 + j, part(chip_of(j), 1 - c), sibling).wait_recv()
        copy(0, part(me, c), sibling).wait_send()
        for j in range(3):
            copy(1 + j, part(me, c), (chips[j][0], chips[j][1], c)).wait_send()
            copy(4 + j, part(chip_of(j), c), sibling).wait_send()

    return send, pass_on, finish


def add_halves(name, g, recv, c_idx, tr):
    n, h, c = recv.shape

    def body(c_ref, g_ref, r_ref, o_ref):
        o_ref[...] = (g_ref[...] + r_ref[...]).astype(BF16)

    nb = h // tr
    return pl.pallas_call(
        body, name=name,
        grid_spec=pltpu.PrefetchScalarGridSpec(
            num_scalar_prefetch=1, grid=(n, nb),
            in_specs=[pl.BlockSpec((None, tr, c), lambda k, i, c_ref: (k, c_ref[0] * nb + i, 0)),
                      pl.BlockSpec((None, tr, c), lambda k, i, c_ref: (k, i, 0))],
            out_specs=pl.BlockSpec((None, tr, c), lambda k, i, c_ref: (k, i, 0))),
        out_shape=jax.ShapeDtypeStruct((n, h, c), BF16), compiler_params=_params("parallel", "parallel"),
    )(c_idx, g, recv)


def sum_chips(name, parts, recv, where, tr):
    n, h, c = recv.shape
    nb = h // tr

    def body(w_ref, p_ref, r_ref, o_ref):
        acc = p_ref[...].astype(F32)
        for k in range(n):
            acc = acc + r_ref[k].astype(F32)
        o_ref[...] = acc

    return pl.pallas_call(
        body, name=name,
        grid_spec=pltpu.PrefetchScalarGridSpec(
            num_scalar_prefetch=1, grid=(nb,),
            in_specs=[pl.BlockSpec((None, tr, c), lambda i, w_ref: (w_ref[0], i, 0)),
                      pl.BlockSpec((n, tr, c), lambda i, w_ref: (0, i, 0))],
            out_specs=pl.BlockSpec((tr, c), lambda i, w_ref: (w_ref[1] * nb + i, 0))),
        out_shape=jax.ShapeDtypeStruct((2 * h, c), F32), compiler_params=_params("parallel"),
    )(where, parts, recv)


def place_shard(name, shard, where, dtype, tr):
    r, c = shard.shape

    def body(w_ref, s_ref, o_ref):
        o_ref[...] = s_ref[...].astype(dtype)

    return pl.pallas_call(
        body, name=name,
        grid_spec=pltpu.PrefetchScalarGridSpec(
            num_scalar_prefetch=1, grid=(r // tr,),
            in_specs=[pl.BlockSpec((tr, c), lambda i, w_ref: (i, 0))],
            out_specs=pl.BlockSpec((None, tr, c), lambda i, w_ref: (w_ref[0], i, 0))),
        out_shape=jax.ShapeDtypeStruct((N_CHIPS, r, c), dtype), compiler_params=_params("parallel"),
    )(where, shard)


ANY = pl.BlockSpec(memory_space=pl.ANY)


def _place():
    x, y, c = lax.axis_index("x"), lax.axis_index("y"), lax.axis_index("c")
    chips = [(1 - x, y), (x, 1 - y), (1 - x, 1 - y)]
    return x, y, c, chips


def gather_shards(arrays):
    n = len(arrays)

    def body(*refs):
        send, pass_on, finish = _gather_phases(refs[n:2 * n], *refs[2 * n:], _spans(arrays))
        send()
        pass_on()
        finish()

    return pl.pallas_call(
        body, name="gather_shards", in_specs=[ANY] * n, out_specs=[ANY] * n,
        out_shape=[jax.ShapeDtypeStruct(a.shape, a.dtype) for a in _arrays(arrays)],
        input_output_aliases={w: w for w in range(n)}, scratch_shapes=_gather_sems(n),
        compiler_params=pltpu.CompilerParams(has_side_effects=True),
    )(*_arrays(arrays))


def _gather_sems(n):
    return [pltpu.SemaphoreType.DMA((6 * n,)), pltpu.SemaphoreType.DMA((6 * n,))]


class Span(typing.NamedTuple):
    array: jax.Array
    lo: int
    hi: int
    ways: tuple = (0, 1, 2)


def _arrays(gathering):
    return [g.array if isinstance(g, Span) else g for g in gathering]


def _spans(gathering):
    return [(g.lo, g.hi, g.ways) if isinstance(g, Span) else (0, g.shape[1], (0, 1, 2)) for g in gathering]


def _gather_phases(out, send_sems, recv_sems, spans):
    n = len(out)
    if not any(ways for _, _, ways in spans):
        return (lambda: None,) * 3
    x, y, c, chips = _place()
    me = 2 * x + y
    sibling = (x, y, 1 - c)

    def half(w, chip, core):
        lo, hi, _ = spans[w]
        h = (hi - lo) // 2
        return out[w].at[chip, pl.ds(lo + core * h, h)]

    def copy(k, block, to):
        return pltpu.make_async_remote_copy(src_ref=block, dst_ref=block, send_sem=send_sems.at[k],
                                            recv_sem=recv_sems.at[k], device_id=to, device_id_type=MESH)

    def over_ici(w, j, chip):
        return copy(3 * w + j, half(w, chip, c), (chips[j][0], chips[j][1], c))

    def over_d2d(w, j, core):
        return copy(3 * n + 3 * w + j, half(w, 2 * chips[j][0] + chips[j][1], core), sibling)

    pairs = [(w, j) for w in range(n) for j in spans[w][2]]

    def send():
        for w, j in pairs:
            over_ici(w, j, me).start()

    def pass_on():
        for w, j in pairs:
            over_ici(w, j, 2 * chips[j][0] + chips[j][1]).wait_recv()
            over_d2d(w, j, c).start()

    def finish():
        for w, j in pairs:
            over_d2d(w, j, 1 - c).wait_recv()
        for w, j in pairs:
            over_ici(w, j, me).wait_send()
            over_d2d(w, j, c).wait_send()

    return send, pass_on, finish


def _relay_sems():
    return [pltpu.SemaphoreType.DMA((4,)), pltpu.SemaphoreType.DMA((4,))]


def _relay_phases(out, send_sems, recv_sems):
    x, y, c, chips = _place()
    sibling = (x, y, 1 - c)
    rows = out.shape[1]
    quarter = rows // 4
    far = 2 * chips[2][0] + chips[2][1]

    def piece(chip, way, core):
        return out.at[chip, pl.ds(way * (rows // 2) + core * quarter, quarter)]

    def copy(k, block, to):
        return pltpu.make_async_remote_copy(src_ref=block, dst_ref=block, send_sem=send_sems.at[k],
                                            recv_sem=recv_sems.at[k], device_id=to, device_id_type=MESH)

    def over_ici(way, chip):
        return copy(way, piece(chip, way, c), (chips[way][0], chips[way][1], c))

    def over_d2d(way, core):
        return copy(2 + way, piece(far, way, core), sibling)

    def send():
        for way in range(2):
            other = chips[1 - way]
            over_ici(way, 2 * other[0] + other[1]).start()

    def pass_on():
        for way in range(2):
            over_ici(way, far).wait_recv()
            over_d2d(way, c).start()

    def finish():
        for way in range(2):
            over_d2d(way, 1 - c).wait_recv()
        for way in range(2):
            other = chips[1 - way]
            over_ici(way, 2 * other[0] + other[1]).wait_send()
            over_d2d(way, c).wait_send()

    return send, pass_on, finish


def swap_halves(name, grads):
    n = len(grads)

    def body(*refs):
        send, finish = _swap_phases(refs[:n], refs[n:2 * n], *refs[2 * n:])
        send()
        finish()

    return pl.pallas_call(
        body, name=name, in_specs=[ANY] * n, out_specs=[ANY] * n, out_shape=_swap_shapes(grads),
        scratch_shapes=_swap_sems(n), compiler_params=pltpu.CompilerParams(has_side_effects=True),
    )(*grads)


def _swap_shapes(grads):
    return [jax.ShapeDtypeStruct((a.shape[0], a.shape[1] // 2, a.shape[2]), a.dtype) for a in grads]


def _swap_sems(n):
    return [pltpu.SemaphoreType.DMA((n,)), pltpu.SemaphoreType.DMA((n,))]


def _swap_phases(g, out, send_sems, recv_sems):
    x, y, c, _ = _place()

    def copies():
        return [pltpu.make_async_remote_copy(
            src_ref=g[w].at[:, pl.ds((1 - c) * (g[w].shape[1] // 2), g[w].shape[1] // 2)], dst_ref=out[w],
            send_sem=send_sems.at[w], recv_sem=recv_sems.at[w], device_id=(x, y, 1 - c), device_id_type=MESH)
            for w in range(len(g))]

    def send():
        for cp in copies():
            cp.start()

    def finish():
        for cp in copies():
            cp.wait()

    return send, finish


def _send_phases(g, out, send_sems, recv_sems):
    x, y, c, _ = _place()

    def copies():
        return [pltpu.make_async_remote_copy(
            src_ref=g[w], dst_ref=out[w], send_sem=send_sems.at[w], recv_sem=recv_sems.at[w],
            device_id=(x, y, 1 - c), device_id_type=MESH) for w in range(len(g))]

    def send():
        for cp in copies():
            cp.start()

    def finish():
        for cp in copies():
            cp.wait()

    return send, finish


def dw_in_half(name, h, dz, which, sending):
    s = h.shape[0]
    hh, tb = D // 2, IN_SHARD // 2
    n = len(sending)
    steps = IN_COLS // tb

    def body(w_ref, *refs):
        a_ref, b_ref, o_ref = refs[0], refs[1], refs[2 + n]
        j = pl.program_id(0)
        if n:
            send, finish = _send_phases(refs[2:2 + n], refs[3 + n:3 + 2 * n], *refs[3 + 2 * n:])
            pl.when(j == 0)(send)
        o_ref[...] = _dot(a_ref[...], b_ref[...], TN)
        if n:
            pl.when(j == steps - 1)(finish)

    out = pl.pallas_call(
        body, name=name,
        grid_spec=pltpu.PrefetchScalarGridSpec(
            num_scalar_prefetch=1, grid=(steps,),
            in_specs=[pl.BlockSpec((s, hh), lambda j, w: (0, w[0])), pl.BlockSpec((s, tb), lambda j, w: (0, j))]
            + [ANY] * n,
            out_specs=[pl.BlockSpec((None, hh, tb), lambda j, w: (j // 2, 0, j % 2))] + [ANY] * n,
            scratch_shapes=_swap_sems(n) if n else []),
        out_shape=[jax.ShapeDtypeStruct((N_CHIPS, hh, IN_SHARD), F32)]
        + [jax.ShapeDtypeStruct(a.shape, a.dtype) for a in sending],
        compiler_params=_params("arbitrary", communicates=bool(n)),
    )(which, h, dz, *sending)
    return out[0], out[1:]


def scatter_chips(parts):
    n = len(parts)

    def body(*refs):
        send, finish = _scatter_phases(refs[:n], refs[n:2 * n], *refs[2 * n:])
        send()
        finish()

    return pl.pallas_call(
        body, name="scatter_chips", in_specs=[ANY] * n, out_specs=[ANY] * n,
        out_shape=_scatter_shapes(parts), scratch_shapes=_scatter_sems(n),
        compiler_params=pltpu.CompilerParams(has_side_effects=True),
    )(*parts)


def _scatter_shapes(parts):
    return [jax.ShapeDtypeStruct((3,) + a.shape[1:], a.dtype) for a in parts]


def _scatter_sems(n):
    return [pltpu.SemaphoreType.DMA((3 * n,)), pltpu.SemaphoreType.DMA((3 * n,))]


def _scatter_phases(p, out, send_sems, recv_sems):
    x, y, c, chips = _place()

    def copies():
        return [pltpu.make_async_remote_copy(
            src_ref=p[w].at[2 * px + py], dst_ref=out[w].at[j], send_sem=send_sems.at[3 * w + j],
            recv_sem=recv_sems.at[3 * w + j], device_id=(px, py, c), device_id_type=MESH)
            for w in range(len(p)) for j, (px, py) in enumerate(chips)]

    def send():
        for cp in copies():
            cp.start()

    def finish():
        for cp in copies():
            cp.wait()

    return send, finish


def join_halves(arrays, gathering):
    n = len(arrays)

    def body(*refs):
        out = refs[n + 1:2 * n + 1]
        send_sems, recv_sems = refs[2 * n + 2:2 * n + 4]
        send8, pass_on8, finish8 = _allgather8_phases(refs[2 * n + 1], *refs[2 * n + 4:])
        x, y, c, _ = _place()

        def copy(w, core):
            h = out[w].shape[0] // 2
            rows = out[w].at[pl.ds(core * h, h)]
            return pltpu.make_async_remote_copy(
                src_ref=rows, dst_ref=rows, send_sem=send_sems.at[w], recv_sem=recv_sems.at[w],
                device_id=(x, y, 1 - c), device_id_type=MESH)

        send8()
        for w in range(n):
            copy(w, c).start()
        pass_on8()
        for w in range(n):
            copy(w, 1 - c).wait_recv()
        finish8()
        for w in range(n):
            copy(w, c).wait_send()

    res = pl.pallas_call(
        body, name="join_halves", in_specs=[ANY] * (n + 1), out_specs=[ANY] * (n + 1),
        out_shape=[jax.ShapeDtypeStruct(a.shape, a.dtype) for a in list(arrays) + [gathering]],
        input_output_aliases={w: w for w in range(n + 1)},
        scratch_shapes=[pltpu.SemaphoreType.DMA((n,)), pltpu.SemaphoreType.DMA((n,)),
                        pltpu.SemaphoreType.DMA((7,)), pltpu.SemaphoreType.DMA((7,))],
        compiler_params=pltpu.CompilerParams(has_side_effects=True),
    )(*arrays, gathering)
    return res[:n], res[n]


def allreduce_small(packed):
    r, c = packed.shape
    n_dev = 8

    def body(x_ref, all_ref, sum_ref, send_sems, recv_sems, local_sem):
        x, y, cc, chips = _place()
        me, sibling = (x, y, cc), (x, y, 1 - cc)

        def rows(px, py, pc):
            return all_ref.at[4 * px + 2 * py + pc]

        def copy(k, block, to, src=None):
            return pltpu.make_async_remote_copy(
                src_ref=rows(*block) if src is None else src, dst_ref=rows(*block), send_sem=send_sems.at[k],
                recv_sem=recv_sems.at[k], device_id=to, device_id_type=MESH)

        mine = pltpu.make_async_copy(x_ref, rows(*me), local_sem)
        mine.start()
        first = [copy(0, me, sibling, src=x_ref)]
        first += [copy(1 + j, me, (*chip, cc), src=x_ref) for j, chip in enumerate(chips)]
        for cp in first:
            cp.start()
        passed = [copy(4 + j, (*chip, cc), sibling) for j, chip in enumerate(chips)]
        for j, chip in enumerate(chips):
            copy(1 + j, (*chip, cc), me).wait_recv()
            passed[j].start()
        copy(0, sibling, me).wait_recv()
        for j, chip in enumerate(chips):
            copy(4 + j, (*chip, 1 - cc), me).wait_recv()
        for cp in first + passed:
            cp.wait_send()
        mine.wait()
        acc = all_ref[0]
        for k in range(1, n_dev):
            acc = acc + all_ref[k]
        sum_ref[...] = acc

    vm = pl.BlockSpec(memory_space=pltpu.VMEM)
    return pl.pallas_call(
        body, name="allreduce_small", in_specs=[vm], out_specs=[vm, vm],
        out_shape=[jax.ShapeDtypeStruct((n_dev, r, c), F32), jax.ShapeDtypeStruct((r, c), F32)],
        scratch_shapes=[pltpu.SemaphoreType.DMA((7,)), pltpu.SemaphoreType.DMA((7,)), pltpu.SemaphoreType.DMA],
        compiler_params=pltpu.CompilerParams(has_side_effects=True, vmem_limit_bytes=VMEM_LIMIT),
    )(packed)[1]


def local_step(x, target, vecs, w_s, bs_t, bg, wg_in, late, core=None, order=None, where=None):
    on_mesh = core is not None

    def add(names, grads, recv):
        return [add_halves("add_" + n, g, r, core, min(r.shape[1], 256)) for n, g, r in zip(names, grads, recv)]

    g_pre, ln_g, ln_b, g_post, g_fpre, g_fpost = vecs
    s = x.shape[0]
    if order is None:
        order = jnp.arange(N_CHIPS, dtype=jnp.int32)
    logc = _attn_tables(s)
    ka, kb = _alibi_tables(s)

    h = norm_pre(x, g_pre)
    if not on_mesh:
        wg_a, wg_b, wg_out, wg_ff1, wg_ff2 = late
    if on_mesh:
        cut = D // 4
        z, wg_in, (wg_a, wg_b, wg_out, wg_ff1, wg_ff2) = mm_in(h, wg_in, order, True, late)
        ya, (wg_b, wg_ff2) = gating_fwd(z, ln_g, ln_b, w_s, bs_t, [wg_b, Span(wg_ff2, 0, cut)])
        yb, lse, (wg_a, wg_ff1, wg_ff2, bg) = attn_fwd(
            z, logc, ka, kb, [wg_a, wg_ff1, Span(wg_ff2, cut, 3 * cut), bg])
        bg = jnp.transpose(bg[:, :2, :], (1, 0, 2)).reshape(2, D)
    else:
        z, _, _ = mm_in(h, wg_in, order, False)
        ya, _ = gating_fwd(z, ln_g, ln_b, w_s, bs_t, [])
        yb, lse, _ = attn_fwd(z, logc, ka, kb, [])
    merged, pa, pb, got = proj_merge(ya, yb, wg_a.reshape(D, D), wg_b.reshape(D, D), z, bg, [wg_out] if on_mesh else [])
    wg_out = got[0] if on_mesh else wg_out
    w_out = wg_out.reshape(D, D)
    o, x1, h2, got = out_norm(merged, w_out, x, g_post, g_fpre, [Span(wg_ff2, 3 * D // 4, D)] if on_mesh else [])
    a, rl, _ = mm_ff1(h2, wg_ff1, [])
    w_ff2 = (got[0] if on_mesh else wg_ff2).reshape(D_FF, D)
    dy, df, d_gfpost, loss = ff2_loss(rl, w_ff2, x1, target, g_fpost)

    half_cols = pl.BlockSpec((D, D // 2), lambda i, j: (0, j))
    d_wff2 = mm_tn("dw_ff2", rl, df, D // 2, D, (D_FF, D), pl.BlockSpec((D // 2, D), lambda i, j: (i, 0)))
    da = ff2_bwd(df, w_ff2, a)
    d_wff1 = mm_tn("dw_ff1", h2, da, D, D // 2, (N_CHIPS, D, D),
                   pl.BlockSpec((None, D, D // 2), lambda i, j: (j // 2, 0, j % 2)))
    d_ff = [d_wff1, d_wff2.reshape(N_CHIPS, D, D)]
    dx1, do, d_gfpre, d_gpost, recv_ff = ff1_bwd_norms(da, wg_ff1, x1, o, dy, g_fpre, g_post, d_ff if on_mesh else [])
    d_wout = mm_tn("dw_out", merged, do, D, D // 2, (D, D), half_cols)
    dpa, dpb, dga, dgb, d_bg = out_bwd_gates(do, w_out, pa, pb, z, bg)
    d_wa = mm_tn("dw_a", ya, dpa, D, D // 2, (D, D), half_cols)
    d_wb = mm_tn("dw_b", yb, dpb, D, D // 2, (D, D), half_cols)
    dya = mm_nt("dy_a", dpa, wg_a.reshape(D, D))
    dyb = mm_nt("dy_b", dpb, wg_b.reshape(D, D))
    d_proj = [d_wa.reshape(N_CHIPS, D // N_CHIPS, D), d_wb.reshape(N_CHIPS, D // N_CHIPS, D),
              d_wout.reshape(N_CHIPS, D // N_CHIPS, D)]
    du, dv, d_ws, d_bs, d_lng, d_lnb, recv_proj = gating_bwd(z, dya, ln_g, ln_b, w_s, bs_t, d_proj if on_mesh else [])
    early = d_proj + d_ff
    parts_early = add(BIG[1:], early, list(recv_proj) + list(recv_ff)) if on_mesh else []
    small = dict(b_gate=d_bg, ln_v_g=d_lng, ln_v_b=d_lnb, w_s=d_ws, b_s=d_bs[:, 0, :],
                 norm_mix_post=d_gpost, norm_ffn_pre=d_gfpre, norm_ffn_post=d_gfpost)
    packed = pack_small(dict(small, norm_mix_pre=jnp.zeros((1, D), F32)), loss, where) if on_mesh else None
    dq, dk, dvb, got_early, packed = attn_bwd(z, yb, dyb, lse, logc, ka, kb, parts_early, packed)
    dz = jnp.concatenate([du, dv, dq, dk, dvb, dga, dgb], axis=1)
    if on_mesh:
        for_sibling, _ = dw_in_half("dw_in_sibling", h, dz, 1 - core, [])
        mine, from_sibling = dw_in_half("dw_in_mine", h, dz, core, [for_sibling])
        d_win = None
        parts_late = [add_halves("add_w_in", mine, from_sibling[0], jnp.zeros((1,), jnp.int32), 256)]
    else:
        half = IN_SHARD // 2
        d_win = mm_tn("dw_in", h, dz, D, half, (N_CHIPS, D, IN_SHARD),
                      pl.BlockSpec((None, D, half), lambda i, j: (j // 2, 0, j % 2)))
        parts_late = []
    dx, d_gpre, got_late, _ = in_bwd_norm(dz, wg_in, x, dx1, g_pre, parts_late)
    small["norm_mix_pre"] = d_gpre
    return (loss[0, 0], dx, [d_win] + early, small, parts_late + parts_early, list(got_late) + list(got_early),
            packed)


BIG = ("w_in", "w_a_proj", "w_b_proj", "w_out", "w_ff1", "w_ff2")
SMALL = ("norm_mix_pre", "ln_v_g", "ln_v_b", "b_s", "norm_mix_post", "norm_ffn_pre", "norm_ffn_post", "w_s", "b_gate")
ORDER = ("norm_mix_pre", "w_in", "b_gate", "ln_v_g", "ln_v_b", "w_s", "b_s", "w_a_proj", "w_b_proj", "w_out",
         "norm_mix_post", "norm_ffn_pre", "w_ff1", "w_ff2", "norm_ffn_post")
VEC_ROWS = D // 128
WS_ROW = 7 * VEC_ROWS
BG_ROW = WS_ROW + GROUPS * CHUNK
LOSS_ROW = BG_ROW + 2 * VEC_ROWS
PACK_ROWS = LOSS_ROW + 8


def pack_small(small, loss, where):
    vectors = [small[n] for n in SMALL[:7]]
    operands = vectors + [small["w_s"], small["b_gate"], loss]

    def body(where_ref, *refs):
        out = refs[-1]
        ws_ref, bg_ref, loss_ref = refs[7:10]
        for i, n in enumerate(SMALL[:7]):
            if n == "b_s":
                out[i * VEC_ROWS:(i + 1) * VEC_ROWS, :] = refs[i][...]
            else:
                for j in range(VEC_ROWS):
                    out[i * VEC_ROWS + j:i * VEC_ROWS + j + 1, :] = refs[i][:, j * 128:(j + 1) * 128]
        for g in range(GROUPS):
            out[WS_ROW + g * CHUNK:WS_ROW + (g + 1) * CHUNK, :] = ws_ref[g]
        for r in range(2):
            for j in range(VEC_ROWS):
                row = BG_ROW + r * VEC_ROWS + j
                out[row:row + 1, :] = bg_ref[r:r + 1, j * 128:(j + 1) * 128]
        lane = lax.broadcasted_iota(jnp.int32, (8, 128), 1)
        sub = lax.broadcasted_iota(jnp.int32, (8, 128), 0)
        out[LOSS_ROW:LOSS_ROW + 8, :] = jnp.where((lane == 0) & (sub == 0), loss_ref[...], 0.0)

    return pl.pallas_call(
        body, name="pack_small",
        grid_spec=pltpu.PrefetchScalarGridSpec(
            num_scalar_prefetch=1, grid=(1,), in_specs=[_full(a.shape) for a in operands],
            out_specs=pl.BlockSpec((None, PACK_ROWS, 128), lambda i, w: (w[0], w[1], 0))),
        out_shape=jax.ShapeDtypeStruct((N_CHIPS, 2 * PACK_ROWS, 128), F32), compiler_params=_params("arbitrary"),
    )(where, *operands)


def pack_vector(vec, where):
    def body(where_ref, v_ref, out):
        for j in range(VEC_ROWS):
            out[j:j + 1, :] = v_ref[:, j * 128:(j + 1) * 128]

    return pl.pallas_call(
        body, name="pack_vector",
        grid_spec=pltpu.PrefetchScalarGridSpec(
            num_scalar_prefetch=1, grid=(1,), in_specs=[_full(vec.shape)],
            out_specs=pl.BlockSpec((None, VEC_ROWS, 128), lambda i, w: (w[0], w[1], 0))),
        out_shape=jax.ShapeDtypeStruct((N_CHIPS, 2 * VEC_ROWS, 128), F32), compiler_params=_params("arbitrary"),
    )(where, vec)


def adamw_small(gathered, first, chip, w, m, v):
    shapes = {n: (1, D) for n in SMALL}
    shapes.update(b_s=(GROUPS, CHUNK), w_s=(GROUPS * CHUNK, CHUNK), b_gate=(2, D // N_CHIPS))
    flat = lambda t: [t[n].reshape(shapes[n]) for n in SMALL]
    per = D // N_CHIPS // 128

    def body(chip_ref, all_ref, first_ref, *refs):
        params, outs = refs[:27], refs[27:]
        sub = lax.broadcasted_iota(jnp.int32, (VEC_ROWS, 128), 0)
        sum_ref = outs[36]
        total = all_ref[0, 0:PACK_ROWS, :]
        head = first_ref[0, 0:VEC_ROWS, :]
        for k in range(1, 2 * N_CHIPS):
            total = total + all_ref[k // 2, (k % 2) * PACK_ROWS:(k % 2 + 1) * PACK_ROWS, :]
            head = head + first_ref[k // 2, (k % 2) * VEC_ROWS:(k % 2 + 1) * VEC_ROWS, :]
        sum_ref[...] = total
        sum_ref[0:VEC_ROWS, :] = head

        def gate_row(r):
            rows = sum_ref[BG_ROW + r * VEC_ROWS:BG_ROW + (r + 1) * VEC_ROWS, :]
            return jnp.concatenate([jnp.sum(jnp.where(sub == per * chip_ref[0] + j, rows, 0.0), axis=0, keepdims=True)
                                    for j in range(per)], axis=1)

        for i, n in enumerate(SMALL):
            if n == "b_s":
                g = sum_ref[i * VEC_ROWS:(i + 1) * VEC_ROWS, :]
            elif n == "w_s":
                g = sum_ref[WS_ROW:BG_ROW, :]
            elif n == "b_gate":
                g = jnp.concatenate([gate_row(0), gate_row(1)], axis=0)
            else:
                g = jnp.concatenate([sum_ref[i * VEC_ROWS + j:i * VEC_ROWS + j + 1, :] for j in range(VEC_ROWS)],
                                    axis=1)
            delta, nm, nv = _adamw_math(params[i][...], g, params[9 + i][...], params[18 + i][...])
            outs[4 * i][...], outs[4 * i + 1][...], outs[4 * i + 2][...], outs[4 * i + 3][...] = g, delta, nm, nv

    vm = pl.BlockSpec(memory_space=pltpu.VMEM)
    res = pl.pallas_call(
        body, name="adamw_small",
        in_specs=[pl.BlockSpec(memory_space=pltpu.SMEM)] + [vm] * 29, out_specs=[vm] * 37,
        out_shape=[jax.ShapeDtypeStruct(shapes[n], F32) for n in SMALL for _ in range(4)]
        + [jax.ShapeDtypeStruct((PACK_ROWS, 128), F32)],
        compiler_params=_params(),
    )(chip, gathered, first, *flat(w), *flat(m), *flat(v))
    new = {n: tuple(r.reshape(w[n].shape) for r in res[4 * i:4 * i + 4]) for i, n in enumerate(SMALL)}
    return new, res[36][LOSS_ROW, 0]


def kernel(x, norm_mix_pre, w_in, b_gate, ln_v_g, ln_v_b, w_s, b_s, w_a_proj, w_b_proj, w_out, norm_mix_post, norm_ffn_pre, w_ff1, w_ff2, norm_ffn_post, loss_target, m_norm_mix_pre, m_w_in, m_b_gate, m_ln_v_g, m_ln_v_b, m_w_s, m_b_s, m_w_a_proj, m_w_b_proj, m_w_out, m_norm_mix_post, m_norm_ffn_pre, m_w_ff1, m_w_ff2, m_norm_ffn_post, v_norm_mix_pre, v_w_in, v_b_gate, v_ln_v_g, v_ln_v_b, v_w_s, v_b_s, v_w_a_proj, v_w_b_proj, v_w_out, v_norm_mix_post, v_norm_ffn_pre, v_w_ff1, v_w_ff2, v_norm_ffn_post):
    w = dict(norm_mix_pre=norm_mix_pre, w_in=w_in, b_gate=b_gate, ln_v_g=ln_v_g, ln_v_b=ln_v_b, w_s=w_s, b_s=b_s,
             w_a_proj=w_a_proj, w_b_proj=w_b_proj, w_out=w_out, norm_mix_post=norm_mix_post,
             norm_ffn_pre=norm_ffn_pre, w_ff1=w_ff1, w_ff2=w_ff2, norm_ffn_post=norm_ffn_post)
    m = dict(norm_mix_pre=m_norm_mix_pre, w_in=m_w_in, b_gate=m_b_gate, ln_v_g=m_ln_v_g, ln_v_b=m_ln_v_b, w_s=m_w_s,
             b_s=m_b_s, w_a_proj=m_w_a_proj, w_b_proj=m_w_b_proj, w_out=m_w_out, norm_mix_post=m_norm_mix_post,
             norm_ffn_pre=m_norm_ffn_pre, w_ff1=m_w_ff1, w_ff2=m_w_ff2, norm_ffn_post=m_norm_ffn_post)
    v = dict(norm_mix_pre=v_norm_mix_pre, w_in=v_w_in, b_gate=v_b_gate, ln_v_g=v_ln_v_g, ln_v_b=v_ln_v_b, w_s=v_w_s,
             b_s=v_b_s, w_a_proj=v_w_a_proj, w_b_proj=v_w_b_proj, w_out=v_w_out, norm_mix_post=v_norm_mix_post,
             norm_ffn_pre=v_norm_ffn_pre, w_ff1=v_w_ff1, w_ff2=v_w_ff2, norm_ffn_post=v_norm_ffn_post)
    chip = 2 * lax.axis_index("x") + lax.axis_index("y")
    core = lax.axis_index("c")

    where = jnp.stack([chip, core]).astype(jnp.int32)
    wg_in = place_shard("place_w_in", w_in[0], where, BF16, 256)
    bg_all = place_shard("place_b_gate", jnp.pad(b_gate[0], ((0, 14), (0, 0))), where, F32, 16)
    vecs = (norm_mix_pre, ln_v_g, ln_v_b, norm_mix_post, norm_ffn_pre, norm_ffn_post)
    loss, dx, _, small, parts, got, packed = local_step(
        x[0], loss_target[0], vecs, w_s[0], b_s[0].T, bg_all, wg_in, [w[n][0] for n in BIG[1:]],
        core=jnp.reshape(core, (1,)).astype(jnp.int32),
        order=jnp.stack([chip, chip ^ 2, chip ^ 1, chip ^ 3]).astype(jnp.int32), where=where)

    halves = [sum_chips("sum_" + n, p, r, where, min(p.shape[1], 256)) for n, p, r in zip(BIG, parts, got)]
    joined, first = join_halves(halves, pack_vector(small["norm_mix_pre"], where))
    grads = dict(zip(BIG, joined))
    new = {}
    for n in BIG:
        shape = w[n].shape
        res = adamw("adamw_" + n, w[n][0], grads[n], m[n][0], v[n][0], min(shape[1], 256))
        new[n] = tuple(r.reshape(shape) for r in res)
    small_new, loss = adamw_small(packed, first, jnp.reshape(chip, (1,)).astype(jnp.int32), w, m, v)
    new.update(small_new)

    outs = [loss, dx[None]]
    for i in range(4):
        outs += [new[n][i] for n in ORDER]
    return tuple(outs)
```

```python
import functools
import math
import typing

import numpy as np
import jax
import jax.numpy as jnp
from jax import lax
from jax.experimental import pallas as pl
from jax.experimental.pallas import tpu as pltpu

F32 = jnp.float32
BF16 = jnp.bfloat16
MESH = pl.DeviceIdType.MESH

D = 1024
EPS = 1e-6
CHUNK = 128
GROUPS = 8
HEADS = 16
HEAD_DIM = 64
ATT_T = 256
ATT_GROUP = 8
ATT_BWD_GROUP = 2
N_CHIPS = 4
D_FF = 4 * D
IN_COLS = 7 * D
IN_SHARD = IN_COLS // N_CHIPS
MASKED = -1e30
VMEM_LIMIT = 56 * 2 ** 20

ADAM_LR, ADAM_B1, ADAM_B2, ADAM_EPS, ADAM_WD, ADAM_STEP = 0.001, 0.9, 0.999, 1e-08, 0.01, 10

NN = (((1,), (0,)), ((), ()))
NT = (((1,), (1,)), ((), ()))
TN = (((0,), (0,)), ((), ()))


def _dot(a, b, dims=NN):
    return lax.dot_general(a, b, dims, preferred_element_type=F32)


def _params(*sem, communicates=False):
    return pltpu.CompilerParams(dimension_semantics=sem or None, vmem_limit_bytes=VMEM_LIMIT,
                                has_side_effects=communicates)


def _rows(tr, c, col=0):
    return pl.BlockSpec((tr, c), lambda i: (i, col))


def _full(shape):
    n = len(shape)
    return pl.BlockSpec(shape, lambda *_: (0,) * n)


def _gelu(x):
    k = math.sqrt(2.0 / math.pi)
    return 0.5 * x * (1.0 + jnp.tanh(k * (x + 0.044715 * x * x * x)))


def _gelu_and_grad(x):
    k = math.sqrt(2.0 / math.pi)
    t = jnp.tanh(k * (x + 0.044715 * x * x * x))
    g = 0.5 * x * (1.0 + t)
    dg = 0.5 * (1.0 + t) + 0.5 * x * (1.0 - t * t) * (k * (1.0 + 3.0 * 0.044715 * x * x))
    return g, dg


def _sigmoid(x):
    return 1.0 / (1.0 + jnp.exp(-x))


def _rms(x):
    r = lax.rsqrt(jnp.mean(x * x, axis=-1, keepdims=True) + EPS)
    return x * r, r


def _rms_bwd(dn, xhat, r):
    return r * (dn - xhat * jnp.mean(dn * xhat, axis=-1, keepdims=True))


def norm_pre(x, g):
    s = x.shape[0]
    tr = 512

    def body(x_ref, g_ref, h_ref):
        xhat, _ = _rms(x_ref[...])
        h_ref[...] = (xhat * g_ref[...]).astype(BF16)

    return pl.pallas_call(
        body, name="norm_pre", grid=(s // tr,),
        in_specs=[_rows(tr, D), _full((1, D))], out_specs=_rows(tr, D),
        out_shape=jax.ShapeDtypeStruct((s, D), BF16), compiler_params=_params("parallel"),
    )(x, g)


def mm_in(h, wg, order, staged, casting=()):
    s = h.shape[0]
    tm, tn = 1024, IN_SHARD // 2
    per = IN_SHARD // tn
    m = len(casting)
    nj, ni = N_CHIPS * per, s // tm
    cast_steps = per * ni

    def body(order_ref, *refs):
        a_ref = refs[0]
        cast_in = refs[2:2 + m]
        o_ref, held = refs[2 + m], refs[3 + m]
        cast_out = refs[4 + m:4 + 2 * m]
        tile, tile_sem = refs[4 + 2 * m:6 + 2 * m]
        sems = refs[6 + 2 * m:]
        j, i = pl.program_id(0), pl.program_id(1)

        @pl.when(j * ni + i < cast_steps)
        def _():
            for src, dst in zip(cast_in, cast_out):
                dst[...] = src[...].astype(BF16)

        def fetch(t):
            chip = order_ref[t // per]
            return pltpu.make_async_copy(held.at[chip, :, pl.ds((t % per) * tn, tn)], tile.at[t % 2],
                                         tile_sem.at[t % 2])

        if staged:
            near = _gather_phases([held], *sems[:2], [(0, D, (0, 1))])
            far = _relay_phases(held, *sems[2:])

        @pl.when(i == 0)
        def _():
            @pl.when(j == 0)
            def _():
                if staged:
                    near[0]()
                fetch(0).start()

            fetch(j).wait()
            ahead = j + 1 < nj
            if staged:
                ahead = ahead & (j + 1 != per) & (j + 1 != 3 * per)
            pl.when(ahead)(lambda: fetch(j + 1).start())

        rows = pl.ds(pl.multiple_of(i * tm, tm), tm)
        o_ref[...] = _dot(a_ref[rows, :], tile[j % 2]).astype(BF16)

        if staged:
            @pl.when((i == ni - 1) & (j == per - 1))
            def _():
                near[1]()
                near[2]()
                far[0]()
                fetch(per).start()

            @pl.when((i == ni - 1) & (j == 3 * per - 1))
            def _():
                far[1]()
                far[2]()
                fetch(3 * per).start()

    def cast_block(j, i, o):
        return jnp.minimum(j * ni + i, cast_steps - 1)

    out = pl.pallas_call(
        body, name="mm_in",
        grid_spec=pltpu.PrefetchScalarGridSpec(
            num_scalar_prefetch=1, grid=(nj, ni),
            in_specs=[pl.BlockSpec((s, D), lambda j, i, o: (0, 0)), ANY]
            + [pl.BlockSpec((a.shape[0] // cast_steps, a.shape[1]), lambda j, i, o: (cast_block(j, i, o), 0))
               for a in casting],
            out_specs=[pl.BlockSpec((tm, tn), lambda j, i, o: (i, o[j // per] * per + j % per)), ANY]
            + [pl.BlockSpec((None, a.shape[0] // cast_steps, a.shape[1]),
                            lambda j, i, o: (o[0], cast_block(j, i, o), 0)) for a in casting],
            scratch_shapes=[pltpu.VMEM((2, D, tn), BF16), pltpu.SemaphoreType.DMA((2,))]
            + (_gather_sems(1) + _relay_sems() if staged else [])),
        out_shape=[jax.ShapeDtypeStruct((s, IN_COLS), BF16), jax.ShapeDtypeStruct(wg.shape, wg.dtype)]
        + [jax.ShapeDtypeStruct((N_CHIPS,) + a.shape, BF16) for a in casting],
        input_output_aliases={2: 1},
        compiler_params=_params("arbitrary", "arbitrary", communicates=staged),
    )(order, h, wg, *casting)
    return out[0], out[1], out[2:]


def _tril_ws(ws_ref, g):
    r = lax.broadcasted_iota(jnp.int32, (CHUNK, CHUNK), 0)
    c = lax.broadcasted_iota(jnp.int32, (CHUNK, CHUNK), 1)
    return jnp.where(c <= r, ws_ref[g], 0.0).astype(BF16)


def _layer_norm(v):
    mu = jnp.mean(v, axis=-1, keepdims=True)
    d = v - mu
    rstd = lax.rsqrt(jnp.mean(d * d, axis=-1, keepdims=True) + EPS)
    return d * rstd, rstd


def gating_fwd(z, ln_g, ln_b, w_s, bs_t, gathering):
    s = z.shape[0]
    n = len(gathering)
    steps = s // CHUNK

    def body(*refs):
        u_ref, v_ref, lg_ref, lb_ref, ws_ref, bst_ref = refs[:6]
        ya_ref = refs[6 + n]
        ci = pl.program_id(0)
        if n:
            send, pass_on, finish = _gather_phases(refs[7 + n:7 + 2 * n], *refs[7 + 2 * n:], _spans(gathering))
            pl.when(ci == 0)(send)
        ug = _gelu(u_ref[...].astype(F32))
        vhat, _ = _layer_norm(_gelu(v_ref[...].astype(F32)))
        vn = (vhat * lg_ref[...] + lb_ref[...]).astype(BF16)
        for g in range(GROUPS):
            cols = slice(g * CHUNK, (g + 1) * CHUNK)
            mixed = _dot(_tril_ws(ws_ref, g), vn[:, cols]) + bst_ref[:, g:g + 1]
            ya_ref[:, cols] = (ug[:, cols] * mixed).astype(BF16)
        if n:
            pl.when(ci == steps - 1)(pass_on)
            pl.when(ci == steps - 1)(finish)

    out = pl.pallas_call(
        body, name="gating_fwd", grid=(steps,),
        in_specs=[_rows(CHUNK, D, 0), _rows(CHUNK, D, 1), _full((1, D)), _full((1, D)),
                  _full((GROUPS, CHUNK, CHUNK)), _full((CHUNK, GROUPS))] + [ANY] * n,
        out_specs=[_rows(CHUNK, D)] + [ANY] * n,
        out_shape=[jax.ShapeDtypeStruct((s, D), BF16)]
        + [jax.ShapeDtypeStruct(a.shape, a.dtype) for a in _arrays(gathering)],
        input_output_aliases={6 + w: 1 + w for w in range(n)},
        scratch_shapes=_gather_sems(n) if n else [],
        compiler_params=_params("arbitrary", communicates=bool(n)),
    )(z, z, ln_g, ln_b, w_s, bs_t, *_arrays(gathering))
    return out[0], out[1:]


def _attn_tables(s):
    nd = s // ATT_T
    r = np.arange(ATT_T)[None, :, None]
    c = np.arange(ATT_T)[None, None, :]
    delta = np.arange(nd)[:, None, None] * ATT_T + r - c
    count = np.zeros(delta.shape, np.int64)
    for window, dilation in ((128, 1), (512, 4), (2048, 16)):
        count += (delta >= 0) & (delta % dilation == 0) & (delta <= window)
    logc = np.where(count > 0, np.log(np.maximum(count, 1)), MASKED)
    return jnp.asarray(logc, F32)


AUG = 3


def _split3_np(x):
    terms, rest = [], np.asarray(x, np.float64)
    for _ in range(AUG):
        term = np.asarray(rest.astype(jnp.bfloat16), np.float64)
        terms.append(term)
        rest = rest - term
    return terms


def _split3(x):
    terms, rest = [], x
    for _ in range(AUG):
        term = rest.astype(BF16).astype(F32)
        terms.append(term)
        rest = rest - term
    return terms


def _alibi_tables(s):
    nb = s // ATT_T
    slopes = np.exp2(-8.0 * np.arange(1, HEADS + 1, dtype=np.float64) / HEADS)
    ka = np.zeros((HEADS // 2, 2, ATT_T, 128), np.float32)
    kb = np.zeros((HEADS // 2, 2, nb, 128), np.float32)
    for p in range(HEADS // 2):
        for e in range(2):
            base = HEAD_DIM * (1 - e)
            for a, term in enumerate(_split3_np(slopes[2 * p + e] * np.arange(ATT_T))):
                ka[p, e, :, base + a] = term
            for a, term in enumerate(_split3_np(slopes[2 * p + e] * ATT_T * np.arange(nb))):
                kb[p, e, :, base + AUG + a] = term
            ka[p, e, :, base + 2 * AUG:base + 3 * AUG] = 1.0
    return jnp.asarray(ka), jnp.asarray(kb)


def _head_masks():
    lane = lax.broadcasted_iota(jnp.int32, (1, 128), 1)
    first = lane < HEAD_DIM

    def ones(e, n):
        base = HEAD_DIM * (1 - e)
        return ((lane >= base) & (lane < base + n)).astype(F32)

    return first, lane, ones


def _place3(lane, at, terms, other):
    for a, term in enumerate(terms):
        other = jnp.where(lane == at + a, term, other)
    return other


def attn_fwd(z, logc, ka, kb, gathering):
    s = z.shape[0]
    nq = s // ATT_T
    t = ATT_T
    n = len(gathering)
    grp = ATT_GROUP
    ngrp = HEADS // 2 // grp
    wide = 128 * grp
    qcol, kcol, vcol = 2 * D // wide, 3 * D // wide, 4 * D // wide

    def body(*refs):
        q_ref, k_ref, v_ref, lc_ref, ka_ref, kb_ref = refs[:6]
        y_ref, lse_ref = refs[6 + n:8 + n]
        q_s, k_s, v_s, m_s, l_s, acc_s = refs[8 + 2 * n:14 + 2 * n]
        gi, qi = pl.program_id(0), pl.program_id(1)
        first, lane, ones = _head_masks()
        if n:
            send, pass_on, finish = _gather_phases(refs[8 + n:8 + 2 * n], *refs[14 + 2 * n:], _spans(gathering))
            pl.when((gi == 0) & (qi == 0))(send)

        @pl.when(qi == 0)
        def _():
            sel = jnp.broadcast_to(first.astype(F32), (t, 128))
            for pr in range(grp):
                cols = slice(pr * 128, (pr + 1) * 128)
                for jb in range(nq):
                    kj = k_ref[jb * t:(jb + 1) * t, cols].astype(F32)
                    vj = v_ref[jb * t:(jb + 1) * t, cols].astype(F32)
                    k_s[pr, 0, jb] = jnp.where(first, kj, ka_ref[pr, 0] + kb_ref[pr, 0, jb:jb + 1, :]).astype(BF16)
                    k_s[pr, 1, jb] = jnp.where(first, ka_ref[pr, 1] + kb_ref[pr, 1, jb:jb + 1, :], kj).astype(BF16)
                    v_s[pr, jb, 0:t, 0:128] = jnp.where(first, vj, 0.0).astype(BF16)
                    v_s[pr, jb, t:2 * t, 0:128] = jnp.where(first, 0.0, vj).astype(BF16)
                    v_s[pr, jb, 0:t, 128:256] = sel.astype(BF16)
                    v_s[pr, jb, t:2 * t, 128:256] = (1.0 - sel).astype(BF16)

        for pr in range(grp):
            q = q_ref[:, pr * 128:(pr + 1) * 128].astype(F32) * (1.0 / math.sqrt(HEAD_DIM))
            q_s[pr, 0] = jnp.where(first, q, ones(0, 2 * AUG)).astype(BF16)
            q_s[pr, 1] = jnp.where(first, ones(1, 2 * AUG), q).astype(BF16)
        m_s[...] = jnp.full_like(m_s, MASKED)
        l_s[...] = jnp.zeros_like(l_s)
        acc_s[...] = jnp.zeros_like(acc_s)

        def scores(j):
            return tuple(_dot(q_s[pr, e], k_s[pr, e, j], NT) for pr in range(grp) for e in range(2))

        def step(j, carry):
            softmax_block(j, scores(j))
            return carry

        def softmax_block(j, u):
            lc = lc_ref[qi - j]
            for pr in range(grp):
                u0 = u[2 * pr] + lc
                u1 = u[2 * pr + 1] + lc
                m0, m1 = m_s[pr, 0], m_s[pr, 1]
                n0 = jnp.maximum(m0, jnp.max(u0, axis=-1, keepdims=True))
                n1 = jnp.maximum(m1, jnp.max(u1, axis=-1, keepdims=True))
                m_s[pr, 0], m_s[pr, 1] = n0, n1
                p = jnp.concatenate([jnp.exp(u0 - jnp.concatenate([n0, n0], axis=1)).astype(BF16),
                                     jnp.exp(u1 - jnp.concatenate([n1, n1], axis=1)).astype(BF16)], axis=1)
                pv = _dot(p, v_s[pr, j])
                alpha = jnp.where(first, jnp.exp(m0 - n0), jnp.exp(m1 - n1))
                acc_s[pr] = acc_s[pr] * alpha + pv[:, 0:128]
                l_s[pr] = l_s[pr] * alpha + pv[:, 128:256]

        lax.fori_loop(0, qi + 1, step, 0)
        for pr in range(grp):
            cols = slice(pr * 128, (pr + 1) * 128)
            y_ref[:, cols] = (acc_s[pr] / l_s[pr]).astype(BF16)
            lse_ref[:, cols] = jnp.where(first, m_s[pr, 0], m_s[pr, 1]) + jnp.log(l_s[pr])
        if n:
            pl.when((gi == ngrp - 1) & (qi == nq - 1))(pass_on)
            pl.when((gi == ngrp - 1) & (qi == nq - 1))(finish)

    out = pl.pallas_call(
        body, name="attn_fwd", grid=(ngrp, nq),
        in_specs=[pl.BlockSpec((t, wide), lambda g, i: (i, qcol + g)),
                  pl.BlockSpec((s, wide), lambda g, i: (0, kcol + g)),
                  pl.BlockSpec((s, wide), lambda g, i: (0, vcol + g)),
                  _full((nq, t, t)),
                  pl.BlockSpec((grp, 2, t, 128), lambda g, i: (g, 0, 0, 0)),
                  pl.BlockSpec((grp, 2, nq, 128), lambda g, i: (g, 0, 0, 0))] + [ANY] * n,
        out_specs=[pl.BlockSpec((t, wide), lambda g, i: (i, g)), pl.BlockSpec((t, wide), lambda g, i: (i, g))]
        + [ANY] * n,
        out_shape=[jax.ShapeDtypeStruct((s, D), BF16), jax.ShapeDtypeStruct((s, D), F32)]
        + [jax.ShapeDtypeStruct(a.shape, a.dtype) for a in _arrays(gathering)],
        input_output_aliases={6 + w: 2 + w for w in range(n)},
        scratch_shapes=[pltpu.VMEM((grp, 2, t, 128), BF16), pltpu.VMEM((grp, 2, nq, t, 128), BF16),
                        pltpu.VMEM((grp, nq, 2 * t, 256), BF16), pltpu.VMEM((grp, 2, t, 128), F32),
                        pltpu.VMEM((grp, t, 128), F32), pltpu.VMEM((grp, t, 128), F32)]
        + (_gather_sems(n) if n else []),
        compiler_params=_params("arbitrary", "arbitrary", communicates=bool(n)),
    )(z, z, z, logc, ka, kb, *_arrays(gathering))
    return out[0], out[1], out[2:]


def proj_merge(ya, yb, wa, wb, z, bg, gathering):
    s = ya.shape[0]
    tm = 512
    n = len(gathering)
    steps = s // tm

    def body(*refs):
        ya_ref, yb_ref, wa_ref, wb_ref, ga_ref, gb_ref, bg_ref = refs[:7]
        mg_ref, pa_ref, pb_ref = refs[7 + n:10 + n]
        i = pl.program_id(0)
        if n:
            send, pass_on, finish = _gather_phases(refs[10 + n:10 + 2 * n], *refs[10 + 2 * n:], _spans(gathering))
            pl.when(i == 0)(send)
            pl.when(i == steps - 1)(pass_on)
        pa = _dot(ya_ref[...], wa_ref[...])
        pb = _dot(yb_ref[...], wb_ref[...])
        sa = _sigmoid(ga_ref[...] + bg_ref[0:1, :])
        sb = _sigmoid(gb_ref[...] + bg_ref[1:2, :])
        mg_ref[...] = (sa * pa + sb * pb).astype(BF16)
        pa_ref[...] = pa.astype(BF16)
        pb_ref[...] = pb.astype(BF16)
        if n:
            pl.when(i == steps - 1)(finish)

    out = jax.ShapeDtypeStruct((s, D), BF16)
    res = pl.pallas_call(
        body, name="proj_merge", grid=(steps,),
        in_specs=[_rows(tm, D), _rows(tm, D), _full((D, D)), _full((D, D)),
                  _rows(tm, D, 5), _rows(tm, D, 6), _full((2, D))] + [ANY] * n,
        out_specs=[_rows(tm, D)] * 3 + [ANY] * n,
        out_shape=[out] * 3 + [jax.ShapeDtypeStruct(a.shape, a.dtype) for a in _arrays(gathering)],
        input_output_aliases={7 + w: 3 + w for w in range(n)},
        scratch_shapes=_gather_sems(n) if n else [],
        compiler_params=_params("arbitrary", communicates=bool(n)),
    )(ya, yb, wa, wb, z, z, bg, *_arrays(gathering))
    return res[0], res[1], res[2], res[3:]


def out_norm(merged, w_out, x, g_post, g_fpre, gathering):
    s = x.shape[0]
    tm = 512
    n = len(gathering)
    steps = s // tm

    def body(*refs):
        mg_ref, w_ref, x_ref, gp_ref, gf_ref = refs[:5]
        o_ref, x1_ref, h2_ref = refs[5 + n:8 + n]
        i = pl.program_id(0)
        if n:
            send, pass_on, finish = _gather_phases(refs[8 + n:8 + 2 * n], *refs[8 + 2 * n:], _spans(gathering))
            pl.when(i == 0)(send)
            pl.when(i == steps - 1)(pass_on)
        o = _dot(mg_ref[...], w_ref[...])
        ohat, _ = _rms(o)
        x1 = x_ref[...] + ohat * gp_ref[...]
        x1hat, _ = _rms(x1)
        o_ref[...] = o
        x1_ref[...] = x1
        h2_ref[...] = (x1hat * gf_ref[...]).astype(BF16)
        if n:
            pl.when(i == steps - 1)(finish)

    res = pl.pallas_call(
        body, name="out_norm", grid=(steps,),
        in_specs=[_rows(tm, D), _full((D, D)), _rows(tm, D), _full((1, D)), _full((1, D))] + [ANY] * n,
        out_specs=[_rows(tm, D)] * 3 + [ANY] * n,
        out_shape=[jax.ShapeDtypeStruct((s, D), F32), jax.ShapeDtypeStruct((s, D), F32),
                   jax.ShapeDtypeStruct((s, D), BF16)]
        + [jax.ShapeDtypeStruct(a.shape, a.dtype) for a in _arrays(gathering)],
        input_output_aliases={5 + w: 3 + w for w in range(n)},
        scratch_shapes=_gather_sems(n) if n else [],
        compiler_params=_params("arbitrary", communicates=bool(n)),
    )(merged, w_out, x, g_post, g_fpre, *_arrays(gathering))
    return res[0], res[1], res[2], res[3:]


def mm_ff1(h2, wg, gathering):
    s = h2.shape[0]
    tm = 1024
    n = len(gathering)
    ni = s // tm

    def body(*refs):
        a_ref, b_ref = refs[:2]
        o_ref, r_ref = refs[2 + n:4 + n]
        i, j = pl.program_id(0), pl.program_id(1)
        if n:
            send, pass_on, finish = _gather_phases(refs[4 + n:4 + 2 * n], *refs[4 + 2 * n:], _spans(gathering))
            pl.when((i == 0) & (j == 0))(send)
            pl.when((i == ni - 1) & (j == N_CHIPS // 2))(pass_on)
        a = _dot(a_ref[...], b_ref[...])
        o_ref[...] = a.astype(BF16)
        r = jnp.maximum(a, 0.0)
        r_ref[...] = (r * r).astype(BF16)
        if n:
            pl.when((i == ni - 1) & (j == N_CHIPS - 1))(finish)

    res = pl.pallas_call(
        body, name="mm_ff1", grid=(ni, N_CHIPS),
        in_specs=[pl.BlockSpec((tm, D), lambda i, j: (i, 0)), pl.BlockSpec((None, D, D), lambda i, j: (j, 0, 0))]
        + [ANY] * n,
        out_specs=[pl.BlockSpec((tm, D), lambda i, j: (i, j))] * 2 + [ANY] * n,
        out_shape=[jax.ShapeDtypeStruct((s, D_FF), BF16), jax.ShapeDtypeStruct((s, D_FF), BF16)]
        + [jax.ShapeDtypeStruct(a.shape, a.dtype) for a in _arrays(gathering)],
        input_output_aliases={2 + w: 2 + w for w in range(n)},
        scratch_shapes=_gather_sems(n) if n else [],
        compiler_params=_params("arbitrary", "arbitrary", communicates=bool(n)),
    )(h2, wg, *_arrays(gathering))
    return res[0], res[1], res[2:]


def ff2_loss(rl, w_ff2, x1, target, g_fpost):
    s = x1.shape[0]
    tm = 256

    def body(rl_ref, w_ref, x1_ref, t_ref, g_ref, dy_ref, df_ref, dg_ref, loss_ref):
        @pl.when(pl.program_id(0) == 0)
        def _():
            dg_ref[...] = jnp.zeros_like(dg_ref)
            loss_ref[...] = jnp.zeros_like(loss_ref)

        f = _dot(rl_ref[...], w_ref[...])
        fhat, r = _rms(f)
        err = x1_ref[...] + fhat * g_ref[...] - t_ref[...]
        loss_ref[...] += 0.5 * jnp.sum(jnp.mean(err * err, axis=-1, keepdims=True), axis=0, keepdims=True)
        dy = err * (1.0 / D)
        dy_ref[...] = dy
        dg_ref[...] += jnp.sum(dy * fhat, axis=0, keepdims=True)
        df_ref[...] = _rms_bwd(dy * g_ref[...], fhat, r).astype(BF16)

    return pl.pallas_call(
        body, name="ff2_loss", grid=(s // tm,),
        in_specs=[_rows(tm, D_FF), _full((D_FF, D)), _rows(tm, D), _rows(tm, D), _full((1, D))],
        out_specs=[_rows(tm, D), _rows(tm, D), _full((1, D)), _full((1, 1))],
        out_shape=[jax.ShapeDtypeStruct((s, D), F32), jax.ShapeDtypeStruct((s, D), BF16),
                   jax.ShapeDtypeStruct((1, D), F32), jax.ShapeDtypeStruct((1, 1), F32)],
        compiler_params=_params("arbitrary"),
    )(rl, w_ff2, x1, target, g_fpost)


def mm_tn(name, a, b, ta, tb, out_shape, out_spec):
    s = a.shape[0]

    def body(a_ref, b_ref, o_ref):
        o_ref[...] = _dot(a_ref[...], b_ref[...], TN)

    return pl.pallas_call(
        body, name=name, grid=(a.shape[1] // ta, b.shape[1] // tb),
        in_specs=[pl.BlockSpec((s, ta), lambda i, j: (0, i)), pl.BlockSpec((s, tb), lambda i, j: (0, j))],
        out_specs=out_spec, out_shape=jax.ShapeDtypeStruct(out_shape, F32),
        compiler_params=_params("parallel", "parallel"),
    )(a, b)


def mm_nt(name, a, w):
    s = a.shape[0]
    tm = 512

    def body(a_ref, w_ref, o_ref):
        o_ref[...] = _dot(a_ref[...], w_ref[...], NT).astype(BF16)

    return pl.pallas_call(
        body, name=name, grid=(s // tm,), in_specs=[_rows(tm, D), _full((D, D))], out_specs=_rows(tm, D),
        out_shape=jax.ShapeDtypeStruct((s, D), BF16), compiler_params=_params("parallel"),
    )(a, w)


def ff2_bwd(df, w_ff2, a):
    s = df.shape[0]
    tm = 1024

    def body(df_ref, w_ref, a_ref, da_ref):
        drl = _dot(df_ref[...], w_ref[...], NT)
        da_ref[...] = (drl * (2.0 * jnp.maximum(a_ref[...].astype(F32), 0.0))).astype(BF16)

    return pl.pallas_call(
        body, name="ff2_bwd", grid=(s // tm, D_FF // D),
        in_specs=[pl.BlockSpec((tm, D), lambda i, j: (i, 0)), pl.BlockSpec((D, D), lambda i, j: (j, 0)),
                  pl.BlockSpec((tm, D), lambda i, j: (i, j))],
        out_specs=pl.BlockSpec((tm, D), lambda i, j: (i, j)),
        out_shape=jax.ShapeDtypeStruct((s, D_FF), BF16), compiler_params=_params("parallel", "parallel"),
    )(df, w_ff2, a)


def ff1_bwd_norms(da, wg, x1, o, dy, g_fpre, g_post, swapping):
    s = x1.shape[0]
    tm = 512
    n = len(swapping)

    def body(*refs):
        da_ref, w_ref, x1_ref, o_ref, dy_ref, gf_ref, gp_ref = refs[:7]
        dx1_ref, do_ref, dgf_ref, dgp_ref = refs[7 + n:11 + n]
        acc_ref = refs[11 + 2 * n]
        i, k = pl.program_id(0), pl.program_id(1)
        if n:
            send, finish = _swap_phases(refs[7:7 + n], refs[11 + n:11 + 2 * n], *refs[12 + 2 * n:])
            pl.when((i == 0) & (k == 0))(send)

        @pl.when((i == 0) & (k == 0))
        def _():
            dgf_ref[...] = jnp.zeros_like(dgf_ref)
            dgp_ref[...] = jnp.zeros_like(dgp_ref)

        part = _dot(da_ref[...], w_ref[...], NT)

        @pl.when(k == 0)
        def _():
            acc_ref[...] = part

        @pl.when(k > 0)
        def _():
            acc_ref[...] += part

        @pl.when(k == N_CHIPS - 1)
        def _():
            dh2 = acc_ref[...]
            x1hat, r2 = _rms(x1_ref[...])
            dgf_ref[...] += jnp.sum(dh2 * x1hat, axis=0, keepdims=True)
            dx1 = dy_ref[...] + _rms_bwd(dh2 * gf_ref[...], x1hat, r2)
            ohat, r1 = _rms(o_ref[...])
            dgp_ref[...] += jnp.sum(dx1 * ohat, axis=0, keepdims=True)
            dx1_ref[...] = dx1
            do_ref[...] = _rms_bwd(dx1 * gp_ref[...], ohat, r1).astype(BF16)

        if n:
            pl.when((i == s // tm - 1) & (k == N_CHIPS - 1))(finish)

    row = pl.BlockSpec((tm, D), lambda i, k: (i, 0))
    vec = pl.BlockSpec((1, D), lambda i, k: (0, 0))
    res = pl.pallas_call(
        body, name="ff1_bwd_norms", grid=(s // tm, N_CHIPS),
        in_specs=[pl.BlockSpec((tm, D), lambda i, k: (i, k)), pl.BlockSpec((None, D, D), lambda i, k: (k, 0, 0)),
                  row, row, row, vec, vec] + [ANY] * n,
        out_specs=[row, row, vec, vec] + [ANY] * n,
        out_shape=[jax.ShapeDtypeStruct((s, D), F32), jax.ShapeDtypeStruct((s, D), BF16),
                   jax.ShapeDtypeStruct((1, D), F32), jax.ShapeDtypeStruct((1, D), F32)] + _swap_shapes(swapping),
        scratch_shapes=[pltpu.VMEM((tm, D), F32)] + (_swap_sems(n) if n else []),
        compiler_params=_params("arbitrary", "arbitrary", communicates=bool(n)),
    )(da, wg, x1, o, dy, g_fpre, g_post, *swapping)
    return res[0], res[1], res[2], res[3], res[4:]


def out_bwd_gates(do, w_out, pa, pb, z, bg):
    s = do.shape[0]
    tm = 512

    def body(do_ref, w_ref, pa_ref, pb_ref, ga_ref, gb_ref, bg_ref, dpa_ref, dpb_ref, dga_ref, dgb_ref, dbg_ref):
        @pl.when(pl.program_id(0) == 0)
        def _():
            dbg_ref[...] = jnp.zeros_like(dbg_ref)

        dm = _dot(do_ref[...], w_ref[...], NT)
        sa = _sigmoid(ga_ref[...] + bg_ref[0:1, :])
        sb = _sigmoid(gb_ref[...] + bg_ref[1:2, :])
        dpa_ref[...] = (dm * sa).astype(BF16)
        dpb_ref[...] = (dm * sb).astype(BF16)
        dga = dm * pa_ref[...].astype(F32) * (sa * (1.0 - sa))
        dgb = dm * pb_ref[...].astype(F32) * (sb * (1.0 - sb))
        dga_ref[...] = dga.astype(BF16)
        dgb_ref[...] = dgb.astype(BF16)
        dbg_ref[0:1, :] += jnp.sum(dga, axis=0, keepdims=True)
        dbg_ref[1:2, :] += jnp.sum(dgb, axis=0, keepdims=True)

    out = jax.ShapeDtypeStruct((s, D), BF16)
    return pl.pallas_call(
        body, name="out_bwd_gates", grid=(s // tm,),
        in_specs=[_rows(tm, D), _full((D, D)), _rows(tm, D), _rows(tm, D), _rows(tm, D, 5), _rows(tm, D, 6),
                  _full((2, D))],
        out_specs=[_rows(tm, D)] * 4 + [_full((2, D))],
        out_shape=[out] * 4 + [jax.ShapeDtypeStruct((2, D), F32)], compiler_params=_params("arbitrary"),
    )(do, w_out, pa, pb, z, z, bg)


def gating_bwd(z, dya, ln_g, ln_b, w_s, bs_t, swapping):
    s = z.shape[0]
    ones = functools.partial(jnp.ones, (8, CHUNK), BF16)
    n = len(swapping)

    def body(*refs):
        u_ref, v_ref, dya_ref, lg_ref, lb_ref, ws_ref, bst_ref = refs[:7]
        du_ref, dv_ref, dws_ref, dbs_ref, dlg_ref, dlb_ref = refs[7 + n:13 + n]
        dvn_ref = refs[13 + 2 * n]
        ci = pl.program_id(0)
        if n:
            send, finish = _swap_phases(refs[7:7 + n], refs[13 + n:13 + 2 * n], *refs[14 + 2 * n:])
            pl.when(ci == 0)(send)

        @pl.when(ci == 0)
        def _():
            dws_ref[...] = jnp.zeros_like(dws_ref)
            dbs_ref[...] = jnp.zeros_like(dbs_ref)
            dlg_ref[...] = jnp.zeros_like(dlg_ref)
            dlb_ref[...] = jnp.zeros_like(dlb_ref)

        ug, dug_du = _gelu_and_grad(u_ref[...].astype(F32))
        vg, dvg_dv = _gelu_and_grad(v_ref[...].astype(F32))
        vhat, rstd = _layer_norm(vg)
        vn = (vhat * lg_ref[...] + lb_ref[...]).astype(BF16)
        dya = dya_ref[...].astype(F32)
        for g in range(GROUPS):
            cols = slice(g * CHUNK, (g + 1) * CHUNK)
            ws = _tril_ws(ws_ref, g)
            mixed = _dot(ws, vn[:, cols]) + bst_ref[:, g:g + 1]
            du_ref[:, cols] = (dya[:, cols] * mixed * dug_du[:, cols]).astype(BF16)
            dmix = (dya[:, cols] * ug[:, cols]).astype(BF16)
            dbs_ref[g] += _dot(ones(), dmix, NT)
            dws_ref[g] += _dot(dmix, vn[:, cols], NT)
            dvn_ref[:, cols] = _dot(ws, dmix, TN)
        dvn = dvn_ref[...]
        dlg_ref[...] += jnp.sum(dvn * vhat, axis=0, keepdims=True)
        dlb_ref[...] += jnp.sum(dvn, axis=0, keepdims=True)
        dvh = dvn * lg_ref[...]
        dvg = rstd * (dvh - jnp.mean(dvh, axis=-1, keepdims=True)
                      - vhat * jnp.mean(dvh * vhat, axis=-1, keepdims=True))
        dv_ref[...] = (dvg * dvg_dv).astype(BF16)

        @pl.when(ci == pl.num_programs(0) - 1)
        def _():
            r = lax.broadcasted_iota(jnp.int32, (CHUNK, CHUNK), 0)
            c = lax.broadcasted_iota(jnp.int32, (CHUNK, CHUNK), 1)
            for g in range(GROUPS):
                dws_ref[g] = jnp.where(c <= r, dws_ref[g], 0.0)

        if n:
            pl.when(ci == pl.num_programs(0) - 1)(finish)

    out = jax.ShapeDtypeStruct((s, D), BF16)
    res = pl.pallas_call(
        body, name="gating_bwd", grid=(s // CHUNK,),
        in_specs=[_rows(CHUNK, D, 0), _rows(CHUNK, D, 1), _rows(CHUNK, D), _full((1, D)), _full((1, D)),
                  _full((GROUPS, CHUNK, CHUNK)), _full((CHUNK, GROUPS))] + [ANY] * n,
        out_specs=[_rows(CHUNK, D), _rows(CHUNK, D), _full((GROUPS, CHUNK, CHUNK)), _full((GROUPS, 8, CHUNK)),
                   _full((1, D)), _full((1, D))] + [ANY] * n,
        out_shape=[out, out, jax.ShapeDtypeStruct((GROUPS, CHUNK, CHUNK), F32),
                   jax.ShapeDtypeStruct((GROUPS, 8, CHUNK), F32),
                   jax.ShapeDtypeStruct((1, D), F32), jax.ShapeDtypeStruct((1, D), F32)] + _swap_shapes(swapping),
        scratch_shapes=[pltpu.VMEM((CHUNK, D), F32)] + (_swap_sems(n) if n else []),
        compiler_params=_params("arbitrary", communicates=bool(n)),
    )(z, z, dya, ln_g, ln_b, w_s, bs_t, *swapping)
    return (*res[:6], res[6:])


def attn_bwd(z, yb, dyb, lse, logc, ka, kb, scattering, gathering=None):
    s = z.shape[0]
    nq = s // ATT_T
    t = ATT_T
    grp = ATT_BWD_GROUP
    ngrp = HEADS // 2 // grp
    wide = 128 * grp
    qcol, kcol, vcol = 2 * D // wide, 3 * D // wide, 4 * D // wide
    scale = 1.0 / math.sqrt(HEAD_DIM)
    n = len(scattering)
    g8 = 0 if gathering is None else 1

    def body(*refs):
        q_ref, k_ref, v_ref, y_ref, dy_ref, lse_ref, lc_ref, ka_ref, kb_ref = refs[:9]
        dq_ref, dk_ref, dv_ref = refs[9 + n + g8:12 + n + g8]
        qa_s, qt_s, da_s, dt_s, dq_s, dkt_s, dvt_s = refs[12 + 2 * n + 2 * g8:19 + 2 * n + 2 * g8]
        sems = refs[19 + 2 * n + 2 * g8:]
        gi, j = pl.program_id(0), pl.program_id(1)
        first, lane, ones = _head_masks()
        if n:
            send, finish = _scatter_phases(refs[9:9 + n], refs[12 + n + g8:12 + 2 * n + g8], *sems[:2])
            pl.when((gi == 0) & (j == 0))(send)
        if g8:
            send8, pass_on8, finish8 = _allgather8_phases(refs[12 + 2 * n + g8], *sems[2 * (n > 0):])
            pl.when((gi == 0) & (j == 0))(send8)
            pl.when((gi == ngrp - 1) & (j == nq - 1))(pass_on8)

        @pl.when(j == 0)
        def _():
            dq_s[...] = jnp.zeros_like(dq_s)
            for pr in range(grp):
                cols = slice(pr * 128, (pr + 1) * 128)
                for ib in range(nq):
                    rows = slice(ib * t, (ib + 1) * t)
                    q = q_ref[rows, cols].astype(F32) * scale
                    lse = lse_ref[rows, cols]
                    qa_s[pr, 0, ib] = jnp.where(first, q, _place3(lane, HEAD_DIM + 2 * AUG, _split3(-lse[:, 0:1]),
                                                                  ones(0, 2 * AUG))).astype(BF16)
                    qa_s[pr, 1, ib] = jnp.where(
                        first, _place3(lane, 2 * AUG, _split3(-lse[:, HEAD_DIM:HEAD_DIM + 1]), ones(1, 2 * AUG)),
                        q).astype(BF16)
                    qt_s[pr, ib, :, 0:t] = jnp.where(first, q, 0.0).T.astype(BF16)
                    qt_s[pr, ib, :, t:2 * t] = jnp.where(first, 0.0, q).T.astype(BF16)
                    do = dy_ref[rows, cols].astype(F32)
                    prod = do * y_ref[rows, cols].astype(F32)
                    dd0 = jnp.sum(jnp.where(first, prod, 0.0), axis=-1, keepdims=True)
                    dd1 = jnp.sum(jnp.where(first, 0.0, prod), axis=-1, keepdims=True)
                    da_s[pr, 0, ib] = jnp.where(first, do, _place3(lane, HEAD_DIM, _split3(-dd0), 0.0)).astype(BF16)
                    da_s[pr, 1, ib] = jnp.where(first, _place3(lane, 0, _split3(-dd1), 0.0), do).astype(BF16)
                    dt_s[pr, ib, :, 0:t] = jnp.where(first, do, 0.0).T.astype(BF16)
                    dt_s[pr, ib, :, t:2 * t] = jnp.where(first, 0.0, do).T.astype(BF16)

        keys = []
        for pr in range(grp):
            kj = k_ref[:, pr * 128:(pr + 1) * 128].astype(F32)
            vj = v_ref[:, pr * 128:(pr + 1) * 128].astype(F32)
            keys.append((
                jnp.where(first, kj, ka_ref[pr, 0] + kb_ref[pr, 0, pl.ds(j, 1), :]).astype(BF16),
                jnp.where(first, ka_ref[pr, 1] + kb_ref[pr, 1, pl.ds(j, 1), :], kj).astype(BF16),
                jnp.concatenate([jnp.where(first, kj, 0.0), jnp.where(first, 0.0, kj)], axis=0).astype(BF16),
                jnp.where(first, vj, ones(0, AUG)).astype(BF16),
                jnp.where(first, ones(1, AUG), vj).astype(BF16)))
        dkt_s[...] = jnp.zeros_like(dkt_s)
        dvt_s[...] = jnp.zeros_like(dvt_s)

        def step(i, _):
            lc = lc_ref[i - j]
            rows = pl.ds(pl.multiple_of(i * t, t), t)
            for pr in range(grp):
                k0a, k1a, kst, v0a, v1a = keys[pr]
                p0 = jnp.exp(_dot(qa_s[pr, 0, i], k0a, NT) + lc)
                p1 = jnp.exp(_dot(qa_s[pr, 1, i], k1a, NT) + lc)
                e0 = (p0 * _dot(da_s[pr, 0, i], v0a, NT)).astype(BF16)
                e1 = (p1 * _dot(da_s[pr, 1, i], v1a, NT)).astype(BF16)
                dq_s[pr, rows, :] += _dot(jnp.concatenate([e0, e1], axis=1), kst)
                dvt_s[pr] += _dot(dt_s[pr, i], jnp.concatenate([p0.astype(BF16), p1.astype(BF16)], axis=0))
                dkt_s[pr] += _dot(qt_s[pr, i], jnp.concatenate([e0, e1], axis=0))
            return 0

        lax.fori_loop(j, nq, step, 0)
        for pr in range(grp):
            dk_ref[:, pr * 128:(pr + 1) * 128] = dkt_s[pr].T.astype(BF16)
            dv_ref[:, pr * 128:(pr + 1) * 128] = dvt_s[pr].T.astype(BF16)

        @pl.when(j == nq - 1)
        def _():
            for pr in range(grp):
                dq_ref[:, pr * 128:(pr + 1) * 128] = (dq_s[pr] * scale).astype(BF16)

        if n:
            pl.when((gi == ngrp - 1) & (j == nq - 1))(finish)
        if g8:
            pl.when((gi == ngrp - 1) & (j == nq - 1))(finish8)

    colblock = lambda c: pl.BlockSpec((s, wide), lambda g, j: (0, c + g))
    blk = lambda c: pl.BlockSpec((t, wide), lambda g, j: (j, c + g))
    out = jax.ShapeDtypeStruct((s, D), BF16)
    res = pl.pallas_call(
        body, name="attn_bwd", grid=(ngrp, nq),
        in_specs=[colblock(qcol), blk(kcol), blk(vcol), colblock(0), colblock(0), colblock(0),
                  _full((nq, t, t)), pl.BlockSpec((grp, 2, t, 128), lambda g, j: (g, 0, 0, 0)),
                  pl.BlockSpec((grp, 2, nq, 128), lambda g, j: (g, 0, 0, 0))] + [ANY] * (n + g8),
        out_specs=[colblock(0), blk(0), blk(0)] + [ANY] * (n + g8),
        out_shape=[out] * 3 + _scatter_shapes(scattering)
        + ([jax.ShapeDtypeStruct(gathering.shape, gathering.dtype)] if g8 else []),
        input_output_aliases={9 + n: 3 + n} if g8 else {},
        scratch_shapes=[pltpu.VMEM((grp, 2, nq, t, 128), BF16), pltpu.VMEM((grp, nq, 128, 2 * t), BF16),
                        pltpu.VMEM((grp, 2, nq, t, 128), BF16), pltpu.VMEM((grp, nq, 128, 2 * t), BF16),
                        pltpu.VMEM((grp, s, 128), F32), pltpu.VMEM((grp, 128, t), F32),
                        pltpu.VMEM((grp, 128, t), F32)]
        + (_scatter_sems(n) if n else [])
        + ([pltpu.SemaphoreType.DMA((7,)), pltpu.SemaphoreType.DMA((7,))] if g8 else []),
        compiler_params=_params("arbitrary", "arbitrary", communicates=bool(n + g8)),
    )(z, z, z, yb, dyb, lse, logc, ka, kb, *scattering, *([gathering] if g8 else []))
    return res[0], res[1], res[2], res[3:3 + n], (res[3 + n] if g8 else None)


def in_bwd_norm(dz, wg, x, dx1, g_pre, scattering, gathering=None):
    s = x.shape[0]
    tm = 512
    n = len(scattering)
    g = 0 if gathering is None else 1
    last = (s // tm - 1, N_CHIPS - 1)

    def body(*refs):
        dz_ref, w_ref, x_ref, dx1_ref, g_ref = refs[:5]
        dx_ref, dg_ref = refs[5 + n + g:7 + n + g]
        acc_ref = refs[7 + 2 * n + 2 * g]
        sems = refs[8 + 2 * n + 2 * g:]
        i, k = pl.program_id(0), pl.program_id(1)
        if n:
            send, finish = _scatter_phases(refs[5:5 + n], refs[7 + n + g:7 + 2 * n + g], *sems[:2])
            pl.when((i == 0) & (k == 0))(send)
        if g:
            send8, pass_on8, finish8 = _allgather8_phases(refs[7 + 2 * n + g], *sems[2 * (n > 0):])
            pl.when((i == 0) & (k == 0))(send8)
            pl.when((i == last[0]) & (k == last[1]))(pass_on8)

        @pl.when((i == 0) & (k == 0))
        def _():
            dg_ref[...] = jnp.zeros_like(dg_ref)

        part = _dot(dz_ref[...], w_ref[...], NT)

        @pl.when(k == 0)
        def _():
            acc_ref[...] = part

        @pl.when(k > 0)
        def _():
            acc_ref[...] += part

        @pl.when(k == N_CHIPS - 1)
        def _():
            dh = acc_ref[...]
            xhat, r = _rms(x_ref[...])
            dg_ref[...] += jnp.sum(dh * xhat, axis=0, keepdims=True)
            dx_ref[...] = dx1_ref[...] + _rms_bwd(dh * g_ref[...], xhat, r)

        if n:
            pl.when((i == last[0]) & (k == last[1]))(finish)
        if g:
            pl.when((i == last[0]) & (k == last[1]))(finish8)

    row = pl.BlockSpec((tm, D), lambda i, k: (i, 0))
    vec = pl.BlockSpec((1, D), lambda i, k: (0, 0))
    res = pl.pallas_call(
        body, name="in_bwd_norm", grid=(s // tm, N_CHIPS),
        in_specs=[pl.BlockSpec((tm, IN_SHARD), lambda i, k: (i, k)),
                  pl.BlockSpec((None, D, IN_SHARD), lambda i, k: (k, 0, 0)), row, row, vec] + [ANY] * (n + g),
        out_specs=[row, vec] + [ANY] * (n + g),
        out_shape=[jax.ShapeDtypeStruct((s, D), F32), jax.ShapeDtypeStruct((1, D), F32)]
        + _scatter_shapes(scattering) + ([jax.ShapeDtypeStruct(gathering.shape, gathering.dtype)] if g else []),
        input_output_aliases={5 + n: 2 + n} if g else {},
        scratch_shapes=[pltpu.VMEM((tm, D), F32)] + (_scatter_sems(n) if n else [])
        + ([pltpu.SemaphoreType.DMA((7,)), pltpu.SemaphoreType.DMA((7,))] if g else []),
        compiler_params=_params("arbitrary", "arbitrary", communicates=bool(n + g)),
    )(dz, wg, x, dx1, g_pre, *scattering, *([gathering] if g else []))
    return res[0], res[1], res[2:2 + n], (res[2 + n] if g else None)


def _adamw_math(w, g, m, v):
    m = ADAM_B1 * m + (1.0 - ADAM_B1) * g
    v = ADAM_B2 * v + (1.0 - ADAM_B2) * (g * g)
    m_hat = m / (1.0 - ADAM_B1 ** ADAM_STEP)
    v_hat = v / (1.0 - ADAM_B2 ** ADAM_STEP)
    delta = -ADAM_LR * (m_hat / (jnp.sqrt(v_hat) + ADAM_EPS) + ADAM_WD * w)
    return delta, m, v


def adamw(name, w, g, m, v, tr):
    r, c = w.shape

    def body(w_ref, g_ref, m_ref, v_ref, go_ref, d_ref, nm_ref, nv_ref):
        g = g_ref[...]
        go_ref[...] = g
        d_ref[...], nm_ref[...], nv_ref[...] = _adamw_math(w_ref[...], g, m_ref[...], v_ref[...])

    out = jax.ShapeDtypeStruct((r, c), F32)
    return pl.pallas_call(
        body, name=name, grid=(r // tr,), in_specs=[_rows(tr, c)] * 4, out_specs=[_rows(tr, c)] * 4,
        out_shape=[out] * 4, compiler_params=_params("parallel"),
    )(w, g, m, v)


def _allgather8_phases(buf, send_sems, recv_sems):
    x, y, c, chips = _place()
    me = 2 * x + y
    sibling = (x, y, 1 - c)
    rows = buf.shape[1] // 2

    def part(chip, core):
        return buf.at[chip, pl.ds(core * rows, rows)]

    def copy(k, block, to):
        return pltpu.make_async_remote_copy(src_ref=block, dst_ref=block, send_sem=send_sems.at[k],
                                            recv_sem=recv_sems.at[k], device_id=to, device_id_type=MESH)

    def chip_of(j):
        return 2 * chips[j][0] + chips[j][1]

    def send():
        copy(0, part(me, c), sibling).start()
        for j in range(3):
            copy(1 + j, part(me, c), (chips[j][0], chips[j][1], c)).start()

    def pass_on():
        for j in range(3):
            copy(1 + j, part(chip_of(j), c), (chips[j][0], chips[j][1], c)).wait_recv()
            copy(4 + j, part(chip_of(j), c), sibling).start()

    def finish():
        copy(0, part(me, 1 - c), sibling).wait_recv()
        for j in range(3):
            copy(4 + j, part(chip_of(j), 1 - c), sibling).wait_recv()
        copy(0, part(me, c), sibling).wait_send()
        for j in range(3):
            copy(1 + j, part(me, c), (chips[j][0], chips[j][1], c)).wait_send()
            copy(4 + j, part(chip_of(j), c), sibling).wait_send()

    return send, pass_on, finish


def add_halves(name, g, recv, c_idx, tr):
    n, h, c = recv.shape

    def body(c_ref, g_ref, r_ref, o_ref):
        o_ref[...] = (g_ref[...] + r_ref[...]).astype(BF16)

    nb = h // tr
    return pl.pallas_call(
        body, name=name,
        grid_spec=pltpu.PrefetchScalarGridSpec(
            num_scalar_prefetch=1, grid=(n, nb),
            in_specs=[pl.BlockSpec((None, tr, c), lambda k, i, c_ref: (k, c_ref[0] * nb + i, 0)),
                      pl.BlockSpec((None, tr, c), lambda k, i, c_ref: (k, i, 0))],
            out_specs=pl.BlockSpec((None, tr, c), lambda k, i, c_ref: (k, i, 0))),
        out_shape=jax.ShapeDtypeStruct((n, h, c), BF16), compiler_params=_params("parallel", "parallel"),
    )(c_idx, g, recv)


def sum_chips(name, parts, recv, where, tr):
    n, h, c = recv.shape
    nb = h // tr

    def body(w_ref, p_ref, r_ref, o_ref):
        acc = p_ref[...].astype(F32)
        for k in range(n):
            acc = acc + r_ref[k].astype(F32)
        o_ref[...] = acc

    return pl.pallas_call(
        body, name=name,
        grid_spec=pltpu.PrefetchScalarGridSpec(
            num_scalar_prefetch=1, grid=(nb,),
            in_specs=[pl.BlockSpec((None, tr, c), lambda i, w_ref: (w_ref[0], i, 0)),
                      pl.BlockSpec((n, tr, c), lambda i, w_ref: (0, i, 0))],
            out_specs=pl.BlockSpec((tr, c), lambda i, w_ref: (w_ref[1] * nb + i, 0))),
        out_shape=jax.ShapeDtypeStruct((2 * h, c), F32), compiler_params=_params("parallel"),
    )(where, parts, recv)


def place_shard(name, shard, where, dtype, tr):
    r, c = shard.shape

    def body(w_ref, s_ref, o_ref):
        o_ref[...] = s_ref[...].astype(dtype)

    return pl.pallas_call(
        body, name=name,
        grid_spec=pltpu.PrefetchScalarGridSpec(
            num_scalar_prefetch=1, grid=(r // tr,),
            in_specs=[pl.BlockSpec((tr, c), lambda i, w_ref: (i, 0))],
            out_specs=pl.BlockSpec((None, tr, c), lambda i, w_ref: (w_ref[0], i, 0))),
        out_shape=jax.ShapeDtypeStruct((N_CHIPS, r, c), dtype), compiler_params=_params("parallel"),
    )(where, shard)


ANY = pl.BlockSpec(memory_space=pl.ANY)


def _place():
    x, y, c = lax.axis_index("x"), lax.axis_index("y"), lax.axis_index("c")
    chips = [(1 - x, y), (x, 1 - y), (1 - x, 1 - y)]
    return x, y, c, chips


def gather_shards(arrays):
    n = len(arrays)

    def body(*refs):
        send, pass_on, finish = _gather_phases(refs[n:2 * n], *refs[2 * n:], _spans(arrays))
        send()
        pass_on()
        finish()

    return pl.pallas_call(
        body, name="gather_shards", in_specs=[ANY] * n, out_specs=[ANY] * n,
        out_shape=[jax.ShapeDtypeStruct(a.shape, a.dtype) for a in _arrays(arrays)],
        input_output_aliases={w: w for w in range(n)}, scratch_shapes=_gather_sems(n),
        compiler_params=pltpu.CompilerParams(has_side_effects=True),
    )(*_arrays(arrays))


def _gather_sems(n):
    return [pltpu.SemaphoreType.DMA((6 * n,)), pltpu.SemaphoreType.DMA((6 * n,))]


class Span(typing.NamedTuple):
    array: jax.Array
    lo: int
    hi: int
    ways: tuple = (0, 1, 2)


def _arrays(gathering):
    return [g.array if isinstance(g, Span) else g for g in gathering]


def _spans(gathering):
    return [(g.lo, g.hi, g.ways) if isinstance(g, Span) else (0, g.shape[1], (0, 1, 2)) for g in gathering]


def _gather_phases(out, send_sems, recv_sems, spans):
    n = len(out)
    if not any(ways for _, _, ways in spans):
        return (lambda: None,) * 3
    x, y, c, chips = _place()
    me = 2 * x + y
    sibling = (x, y, 1 - c)

    def half(w, chip, core):
        lo, hi, _ = spans[w]
        h = (hi - lo) // 2
        return out[w].at[chip, pl.ds(lo + core * h, h)]

    def copy(k, block, to):
        return pltpu.make_async_remote_copy(src_ref=block, dst_ref=block, send_sem=send_sems.at[k],
                                            recv_sem=recv_sems.at[k], device_id=to, device_id_type=MESH)

    def over_ici(w, j, chip):
        return copy(3 * w + j, half(w, chip, c), (chips[j][0], chips[j][1], c))

    def over_d2d(w, j, core):
        return copy(3 * n + 3 * w + j, half(w, 2 * chips[j][0] + chips[j][1], core), sibling)

    pairs = [(w, j) for w in range(n) for j in spans[w][2]]

    def send():
        for w, j in pairs:
            over_ici(w, j, me).start()

    def pass_on():
        for w, j in pairs:
            over_ici(w, j, 2 * chips[j][0] + chips[j][1]).wait_recv()
            over_d2d(w, j, c).start()

    def finish():
        for w, j in pairs:
            over_d2d(w, j, 1 - c).wait_recv()
        for w, j in pairs:
            over_ici(w, j, me).wait_send()
            over_d2d(w, j, c).wait_send()

    return send, pass_on, finish


def _relay_sems():
    return [pltpu.SemaphoreType.DMA((4,)), pltpu.SemaphoreType.DMA((4,))]


def _relay_phases(out, send_sems, recv_sems):
    x, y, c, chips = _place()
    sibling = (x, y, 1 - c)
    rows = out.shape[1]
    quarter = rows // 4
    far = 2 * chips[2][0] + chips[2][1]

    def piece(chip, way, core):
        return out.at[chip, pl.ds(way * (rows // 2) + core * quarter, quarter)]

    def copy(k, block, to):
        return pltpu.make_async_remote_copy(src_ref=block, dst_ref=block, send_sem=send_sems.at[k],
                                            recv_sem=recv_sems.at[k], device_id=to, device_id_type=MESH)

    def over_ici(way, chip):
        return copy(way, piece(chip, way, c), (chips[way][0], chips[way][1], c))

    def over_d2d(way, core):
        return copy(2 + way, piece(far, way, core), sibling)

    def send():
        for way in range(2):
            other = chips[1 - way]
            over_ici(way, 2 * other[0] + other[1]).start()

    def pass_on():
        for way in range(2):
            over_ici(way, far).wait_recv()
            over_d2d(way, c).start()

    def finish():
        for way in range(2):
            over_d2d(way, 1 - c).wait_recv()
        for way in range(2):
            other = chips[1 - way]
            over_ici(way, 2 * other[0] + other[1]).wait_send()
            over_d2d(way, c).wait_send()

    return send, pass_on, finish


def swap_halves(name, grads):
    n = len(grads)

    def body(*refs):
        send, finish = _swap_phases(refs[:n], refs[n:2 * n], *refs[2 * n:])
        send()
        finish()

    return pl.pallas_call(
        body, name=name, in_specs=[ANY] * n, out_specs=[ANY] * n, out_shape=_swap_shapes(grads),
        scratch_shapes=_swap_sems(n), compiler_params=pltpu.CompilerParams(has_side_effects=True),
    )(*grads)


def _swap_shapes(grads):
    return [jax.ShapeDtypeStruct((a.shape[0], a.shape[1] // 2, a.shape[2]), a.dtype) for a in grads]


def _swap_sems(n):
    return [pltpu.SemaphoreType.DMA((n,)), pltpu.SemaphoreType.DMA((n,))]


def _swap_phases(g, out, send_sems, recv_sems):
    x, y, c, _ = _place()

    def copies():
        return [pltpu.make_async_remote_copy(
            src_ref=g[w].at[:, pl.ds((1 - c) * (g[w].shape[1] // 2), g[w].shape[1] // 2)], dst_ref=out[w],
            send_sem=send_sems.at[w], recv_sem=recv_sems.at[w], device_id=(x, y, 1 - c), device_id_type=MESH)
            for w in range(len(g))]

    def send():
        for cp in copies():
            cp.start()

    def finish():
        for cp in copies():
            cp.wait()

    return send, finish


def _send_phases(g, out, send_sems, recv_sems):
    x, y, c, _ = _place()

    def copies():
        return [pltpu.make_async_remote_copy(
            src_ref=g[w], dst_ref=out[w], send_sem=send_sems.at[w], recv_sem=recv_sems.at[w],
            device_id=(x, y, 1 - c), device_id_type=MESH) for w in range(len(g))]

    def send():
        for cp in copies():
            cp.start()

    def finish():
        for cp in copies():
            cp.wait()

    return send, finish


def dw_in_half(name, h, dz, which, sending):
    s = h.shape[0]
    hh, tb = D // 2, IN_SHARD // 2
    n = len(sending)
    steps = IN_COLS // tb

    def body(w_ref, *refs):
        a_ref, b_ref, o_ref = refs[0], refs[1], refs[2 + n]
        j = pl.program_id(0)
        if n:
            send, finish = _send_phases(refs[2:2 + n], refs[3 + n:3 + 2 * n], *refs[3 + 2 * n:])
            pl.when(j == 0)(send)
        o_ref[...] = _dot(a_ref[...], b_ref[...], TN)
        if n:
            pl.when(j == steps - 1)(finish)

    out = pl.pallas_call(
        body, name=name,
        grid_spec=pltpu.PrefetchScalarGridSpec(
            num_scalar_prefetch=1, grid=(steps,),
            in_specs=[pl.BlockSpec((s, hh), lambda j, w: (0, w[0])), pl.BlockSpec((s, tb), lambda j, w: (0, j))]
            + [ANY] * n,
            out_specs=[pl.BlockSpec((None, hh, tb), lambda j, w: (j // 2, 0, j % 2))] + [ANY] * n,
            scratch_shapes=_swap_sems(n) if n else []),
        out_shape=[jax.ShapeDtypeStruct((N_CHIPS, hh, IN_SHARD), F32)]
        + [jax.ShapeDtypeStruct(a.shape, a.dtype) for a in sending],
        compiler_params=_params("arbitrary", communicates=bool(n)),
    )(which, h, dz, *sending)
    return out[0], out[1:]


def scatter_chips(parts):
    n = len(parts)

    def body(*refs):
        send, finish = _scatter_phases(refs[:n], refs[n:2 * n], *refs[2 * n:])
        send()
        finish()

    return pl.pallas_call(
        body, name="scatter_chips", in_specs=[ANY] * n, out_specs=[ANY] * n,
        out_shape=_scatter_shapes(parts), scratch_shapes=_scatter_sems(n),
        compiler_params=pltpu.CompilerParams(has_side_effects=True),
    )(*parts)


def _scatter_shapes(parts):
    return [jax.ShapeDtypeStruct((3,) + a.shape[1:], a.dtype) for a in parts]


def _scatter_sems(n):
    return [pltpu.SemaphoreType.DMA((3 * n,)), pltpu.SemaphoreType.DMA((3 * n,))]


def _scatter_phases(p, out, send_sems, recv_sems):
    x, y, c, chips = _place()

    def copies():
        return [pltpu.make_async_remote_copy(
            src_ref=p[w].at[2 * px + py], dst_ref=out[w].at[j], send_sem=send_sems.at[3 * w + j],
            recv_sem=recv_sems.at[3 * w + j], device_id=(px, py, c), device_id_type=MESH)
            for w in range(len(p)) for j, (px, py) in enumerate(chips)]

    def send():
        for cp in copies():
            cp.start()

    def finish():
        for cp in copies():
            cp.wait()

    return send, finish


def join_halves(arrays, gathering):
    n = len(arrays)

    def body(*refs):
        out = refs[n + 1:2 * n + 1]
        send_sems, recv_sems = refs[2 * n + 2:2 * n + 4]
        send8, pass_on8, finish8 = _allgather8_phases(refs[2 * n + 1], *refs[2 * n + 4:])
        x, y, c, _ = _place()

        def copy(w, core):
            h = out[w].shape[0] // 2
            rows = out[w].at[pl.ds(core * h, h)]
            return pltpu.make_async_remote_copy(
                src_ref=rows, dst_ref=rows, send_sem=send_sems.at[w], recv_sem=recv_sems.at[w],
                device_id=(x, y, 1 - c), device_id_type=MESH)

        send8()
        for w in range(n):
            copy(w, c).start()
        pass_on8()
        for w in range(n):
            copy(w, 1 - c).wait_recv()
        finish8()
        for w in range(n):
            copy(w, c).wait_send()

    res = pl.pallas_call(
        body, name="join_halves", in_specs=[ANY] * (n + 1), out_specs=[ANY] * (n + 1),
        out_shape=[jax.ShapeDtypeStruct(a.shape, a.dtype) for a in list(arrays) + [gathering]],
        input_output_aliases={w: w for w in range(n + 1)},
        scratch_shapes=[pltpu.SemaphoreType.DMA((n,)), pltpu.SemaphoreType.DMA((n,)),
                        pltpu.SemaphoreType.DMA((7,)), pltpu.SemaphoreType.DMA((7,))],
        compiler_params=pltpu.CompilerParams(has_side_effects=True),
    )(*arrays, gathering)
    return res[:n], res[n]


def allreduce_small(packed):
    r, c = packed.shape
    n_dev = 8

    def body(x_ref, all_ref, sum_ref, send_sems, recv_sems, local_sem):
        x, y, cc, chips = _place()
        me, sibling = (x, y, cc), (x, y, 1 - cc)

        def rows(px, py, pc):
            return all_ref.at[4 * px + 2 * py + pc]

        def copy(k, block, to, src=None):
            return pltpu.make_async_remote_copy(
                src_ref=rows(*block) if src is None else src, dst_ref=rows(*block), send_sem=send_sems.at[k],
                recv_sem=recv_sems.at[k], device_id=to, device_id_type=MESH)

        mine = pltpu.make_async_copy(x_ref, rows(*me), local_sem)
        mine.start()
        first = [copy(0, me, sibling, src=x_ref)]
        first += [copy(1 + j, me, (*chip, cc), src=x_ref) for j, chip in enumerate(chips)]
        for cp in first:
            cp.start()
        passed = [copy(4 + j, (*chip, cc), sibling) for j, chip in enumerate(chips)]
        for j, chip in enumerate(chips):
            copy(1 + j, (*chip, cc), me).wait_recv()
            passed[j].start()
        copy(0, sibling, me).wait_recv()
        for j, chip in enumerate(chips):
            copy(4 + j, (*chip, 1 - cc), me).wait_recv()
        for cp in first + passed:
            cp.wait_send()
        mine.wait()
        acc = all_ref[0]
        for k in range(1, n_dev):
            acc = acc + all_ref[k]
        sum_ref[...] = acc

    vm = pl.BlockSpec(memory_space=pltpu.VMEM)
    return pl.pallas_call(
        body, name="allreduce_small", in_specs=[vm], out_specs=[vm, vm],
        out_shape=[jax.ShapeDtypeStruct((n_dev, r, c), F32), jax.ShapeDtypeStruct((r, c), F32)],
        scratch_shapes=[pltpu.SemaphoreType.DMA((7,)), pltpu.SemaphoreType.DMA((7,)), pltpu.SemaphoreType.DMA],
        compiler_params=pltpu.CompilerParams(has_side_effects=True, vmem_limit_bytes=VMEM_LIMIT),
    )(packed)[1]


def local_step(x, target, vecs, w_s, bs_t, bg, wg_in, late, core=None, order=None, where=None):
    on_mesh = core is not None

    def add(names, grads, recv):
        return [add_halves("add_" + n, g, r, core, min(r.shape[1], 256)) for n, g, r in zip(names, grads, recv)]

    g_pre, ln_g, ln_b, g_post, g_fpre, g_fpost = vecs
    s = x.shape[0]
    if order is None:
        order = jnp.arange(N_CHIPS, dtype=jnp.int32)
    logc = _attn_tables(s)
    ka, kb = _alibi_tables(s)

    h = norm_pre(x, g_pre)
    if not on_mesh:
        wg_a, wg_b, wg_out, wg_ff1, wg_ff2 = late
    if on_mesh:
        cut = D // 4
        z, wg_in, (wg_a, wg_b, wg_out, wg_ff1, wg_ff2) = mm_in(h, wg_in, order, True, late)
        ya, (wg_b, wg_ff2) = gating_fwd(z, ln_g, ln_b, w_s, bs_t, [wg_b, Span(wg_ff2, 0, cut)])
        yb, lse, (wg_a, wg_ff1, wg_ff2, bg) = attn_fwd(
            z, logc, ka, kb, [wg_a, wg_ff1, Span(wg_ff2, cut, 3 * cut), bg])
        bg = jnp.transpose(bg[:, :2, :], (1, 0, 2)).reshape(2, D)
    else:
        z, _, _ = mm_in(h, wg_in, order, False)
        ya, _ = gating_fwd(z, ln_g, ln_b, w_s, bs_t, [])
        yb, lse, _ = attn_fwd(z, logc, ka, kb, [])
    merged, pa, pb, got = proj_merge(ya, yb, wg_a.reshape(D, D), wg_b.reshape(D, D), z, bg, [wg_out] if on_mesh else [])
    wg_out = got[0] if on_mesh else wg_out
    w_out = wg_out.reshape(D, D)
    o, x1, h2, got = out_norm(merged, w_out, x, g_post, g_fpre, [Span(wg_ff2, 3 * D // 4, D)] if on_mesh else [])
    a, rl, _ = mm_ff1(h2, wg_ff1, [])
    w_ff2 = (got[0] if on_mesh else wg_ff2).reshape(D_FF, D)
    dy, df, d_gfpost, loss = ff2_loss(rl, w_ff2, x1, target, g_fpost)

    half_cols = pl.BlockSpec((D, D // 2), lambda i, j: (0, j))
    d_wff2 = mm_tn("dw_ff2", rl, df, D // 2, D, (D_FF, D), pl.BlockSpec((D // 2, D), lambda i, j: (i, 0)))
    da = ff2_bwd(df, w_ff2, a)
    d_wff1 = mm_tn("dw_ff1", h2, da, D, D // 2, (N_CHIPS, D, D),
                   pl.BlockSpec((None, D, D // 2), lambda i, j: (j // 2, 0, j % 2)))
    d_ff = [d_wff1, d_wff2.reshape(N_CHIPS, D, D)]
    dx1, do, d_gfpre, d_gpost, recv_ff = ff1_bwd_norms(da, wg_ff1, x1, o, dy, g_fpre, g_post, d_ff if on_mesh else [])
    d_wout = mm_tn("dw_out", merged, do, D, D // 2, (D, D), half_cols)
    dpa, dpb, dga, dgb, d_bg = out_bwd_gates(do, w_out, pa, pb, z, bg)
    d_wa = mm_tn("dw_a", ya, dpa, D, D // 2, (D, D), half_cols)
    d_wb = mm_tn("dw_b", yb, dpb, D, D // 2, (D, D), half_cols)
    dya = mm_nt("dy_a", dpa, wg_a.reshape(D, D))
    dyb = mm_nt("dy_b", dpb, wg_b.reshape(D, D))
    d_proj = [d_wa.reshape(N_CHIPS, D // N_CHIPS, D), d_wb.reshape(N_CHIPS, D // N_CHIPS, D),
              d_wout.reshape(N_CHIPS, D // N_CHIPS, D)]
    du, dv, d_ws, d_bs, d_lng, d_lnb, recv_proj = gating_bwd(z, dya, ln_g, ln_b, w_s, bs_t, d_proj if on_mesh else [])
    early = d_proj + d_ff
    parts_early = add(BIG[1:], early, list(recv_proj) + list(recv_ff)) if on_mesh else []
    small = dict(b_gate=d_bg, ln_v_g=d_lng, ln_v_b=d_lnb, w_s=d_ws, b_s=d_bs[:, 0, :],
                 norm_mix_post=d_gpost, norm_ffn_pre=d_gfpre, norm_ffn_post=d_gfpost)
    packed = pack_small(dict(small, norm_mix_pre=jnp.zeros((1, D), F32)), loss, where) if on_mesh else None
    dq, dk, dvb, got_early, packed = attn_bwd(z, yb, dyb, lse, logc, ka, kb, parts_early, packed)
    dz = jnp.concatenate([du, dv, dq, dk, dvb, dga, dgb], axis=1)
    if on_mesh:
        for_sibling, _ = dw_in_half("dw_in_sibling", h, dz, 1 - core, [])
        mine, from_sibling = dw_in_half("dw_in_mine", h, dz, core, [for_sibling])
        d_win = None
        parts_late = [add_halves("add_w_in", mine, from_sibling[0], jnp.zeros((1,), jnp.int32), 256)]
    else:
        half = IN_SHARD // 2
        d_win = mm_tn("dw_in", h, dz, D, half, (N_CHIPS, D, IN_SHARD),
                      pl.BlockSpec((None, D, half), lambda i, j: (j // 2, 0, j % 2)))
        parts_late = []
    dx, d_gpre, got_late, _ = in_bwd_norm(dz, wg_in, x, dx1, g_pre, parts_late)
    small["norm_mix_pre"] = d_gpre
    return (loss[0, 0], dx, [d_win] + early, small, parts_late + parts_early, list(got_late) + list(got_early),
            packed)


BIG = ("w_in", "w_a_proj", "w_b_proj", "w_out", "w_ff1", "w_ff2")
SMALL = ("norm_mix_pre", "ln_v_g", "ln_v_b", "b_s", "norm_mix_post", "norm_ffn_pre", "norm_ffn_post", "w_s", "b_gate")
ORDER = ("norm_mix_pre", "w_in", "b_gate", "ln_v_g", "ln_v_b", "w_s", "b_s", "w_a_proj", "w_b_proj", "w_out",
         "norm_mix_post", "norm_ffn_pre", "w_ff1", "w_ff2", "norm_ffn_post")
VEC_ROWS = D // 128
WS_ROW = 7 * VEC_ROWS
BG_ROW = WS_ROW + GROUPS * CHUNK
LOSS_ROW = BG_ROW + 2 * VEC_ROWS
PACK_ROWS = LOSS_ROW + 8


def pack_small(small, loss, where):
    vectors = [small[n] for n in SMALL[:7]]
    operands = vectors + [small["w_s"], small["b_gate"], loss]

    def body(where_ref, *refs):
        out = refs[-1]
        ws_ref, bg_ref, loss_ref = refs[7:10]
        for i, n in enumerate(SMALL[:7]):
            if n == "b_s":
                out[i * VEC_ROWS:(i + 1) * VEC_ROWS, :] = refs[i][...]
            else:
                for j in range(VEC_ROWS):
                    out[i * VEC_ROWS + j:i * VEC_ROWS + j + 1, :] = refs[i][:, j * 128:(j + 1) * 128]
        for g in range(GROUPS):
            out[WS_ROW + g * CHUNK:WS_ROW + (g + 1) * CHUNK, :] = ws_ref[g]
        for r in range(2):
            for j in range(VEC_ROWS):
                row = BG_ROW + r * VEC_ROWS + j
                out[row:row + 1, :] = bg_ref[r:r + 1, j * 128:(j + 1) * 128]
        lane = lax.broadcasted_iota(jnp.int32, (8, 128), 1)
        sub = lax.broadcasted_iota(jnp.int32, (8, 128), 0)
        out[LOSS_ROW:LOSS_ROW + 8, :] = jnp.where((lane == 0) & (sub == 0), loss_ref[...], 0.0)

    return pl.pallas_call(
        body, name="pack_small",
        grid_spec=pltpu.PrefetchScalarGridSpec(
            num_scalar_prefetch=1, grid=(1,), in_specs=[_full(a.shape) for a in operands],
            out_specs=pl.BlockSpec((None, PACK_ROWS, 128), lambda i, w: (w[0], w[1], 0))),
        out_shape=jax.ShapeDtypeStruct((N_CHIPS, 2 * PACK_ROWS, 128), F32), compiler_params=_params("arbitrary"),
    )(where, *operands)


def pack_vector(vec, where):
    def body(where_ref, v_ref, out):
        for j in range(VEC_ROWS):
            out[j:j + 1, :] = v_ref[:, j * 128:(j + 1) * 128]

    return pl.pallas_call(
        body, name="pack_vector",
        grid_spec=pltpu.PrefetchScalarGridSpec(
            num_scalar_prefetch=1, grid=(1,), in_specs=[_full(vec.shape)],
            out_specs=pl.BlockSpec((None, VEC_ROWS, 128), lambda i, w: (w[0], w[1], 0))),
        out_shape=jax.ShapeDtypeStruct((N_CHIPS, 2 * VEC_ROWS, 128), F32), compiler_params=_params("arbitrary"),
    )(where, vec)


def adamw_small(gathered, first, chip, w, m, v):
    shapes = {n: (1, D) for n in SMALL}
    shapes.update(b_s=(GROUPS, CHUNK), w_s=(GROUPS * CHUNK, CHUNK), b_gate=(2, D // N_CHIPS))
    flat = lambda t: [t[n].reshape(shapes[n]) for n in SMALL]
    per = D // N_CHIPS // 128

    def body(chip_ref, all_ref, first_ref, *refs):
        params, outs = refs[:27], refs[27:]
        sub = lax.broadcasted_iota(jnp.int32, (VEC_ROWS, 128), 0)
        sum_ref = outs[36]
        total = all_ref[0, 0:PACK_ROWS, :]
        head = first_ref[0, 0:VEC_ROWS, :]
        for k in range(1, 2 * N_CHIPS):
            total = total + all_ref[k // 2, (k % 2) * PACK_ROWS:(k % 2 + 1) * PACK_ROWS, :]
            head = head + first_ref[k // 2, (k % 2) * VEC_ROWS:(k % 2 + 1) * VEC_ROWS, :]
        sum_ref[...] = total
        sum_ref[0:VEC_ROWS, :] = head

        def gate_row(r):
            rows = sum_ref[BG_ROW + r * VEC_ROWS:BG_ROW + (r + 1) * VEC_ROWS, :]
            return jnp.concatenate([jnp.sum(jnp.where(sub == per * chip_ref[0] + j, rows, 0.0), axis=0, keepdims=True)
                                    for j in range(per)], axis=1)

        for i, n in enumerate(SMALL):
            if n == "b_s":
                g = sum_ref[i * VEC_ROWS:(i + 1) * VEC_ROWS, :]
            elif n == "w_s":
                g = sum_ref[WS_ROW:BG_ROW, :]
            elif n == "b_gate":
                g = jnp.concatenate([gate_row(0), gate_row(1)], axis=0)
            else:
                g = jnp.concatenate([sum_ref[i * VEC_ROWS + j:i * VEC_ROWS + j + 1, :] for j in range(VEC_ROWS)],
                                    axis=1)
            delta, nm, nv = _adamw_math(params[i][...], g, params[9 + i][...], params[18 + i][...])
            outs[4 * i][...], outs[4 * i + 1][...], outs[4 * i + 2][...], outs[4 * i + 3][...] = g, delta, nm, nv

    vm = pl.BlockSpec(memory_space=pltpu.VMEM)
    res = pl.pallas_call(
        body, name="adamw_small",
        in_specs=[pl.BlockSpec(memory_space=pltpu.SMEM)] + [vm] * 29, out_specs=[vm] * 37,
        out_shape=[jax.ShapeDtypeStruct(shapes[n], F32) for n in SMALL for _ in range(4)]
        + [jax.ShapeDtypeStruct((PACK_ROWS, 128), F32)],
        compiler_params=_params(),
    )(chip, gathered, first, *flat(w), *flat(m), *flat(v))
    new = {n: tuple(r.reshape(w[n].shape) for r in res[4 * i:4 * i + 4]) for i, n in enumerate(SMALL)}
    return new, res[36][LOSS_ROW, 0]


def kernel(x, norm_mix_pre, w_in, b_gate, ln_v_g, ln_v_b, w_s, b_s, w_a_proj, w_b_proj, w_out, norm_mix_post, norm_ffn_pre, w_ff1, w_ff2, norm_ffn_post, loss_target, m_norm_mix_pre, m_w_in, m_b_gate, m_ln_v_g, m_ln_v_b, m_w_s, m_b_s, m_w_a_proj, m_w_b_proj, m_w_out, m_norm_mix_post, m_norm_ffn_pre, m_w_ff1, m_w_ff2, m_norm_ffn_post, v_norm_mix_pre, v_w_in, v_b_gate, v_ln_v_g, v_ln_v_b, v_w_s, v_b_s, v_w_a_proj, v_w_b_proj, v_w_out, v_norm_mix_post, v_norm_ffn_pre, v_w_ff1, v_w_ff2, v_norm_ffn_post):
    w = dict(norm_mix_pre=norm_mix_pre, w_in=w_in, b_gate=b_gate, ln_v_g=ln_v_g, ln_v_b=ln_v_b, w_s=w_s, b_s=b_s,
             w_a_proj=w_a_proj, w_b_proj=w_b_proj, w_out=w_out, norm_mix_post=norm_mix_post,
             norm_ffn_pre=norm_ffn_pre, w_ff1=w_ff1, w_ff2=w_ff2, norm_ffn_post=norm_ffn_post)
    m = dict(norm_mix_pre=m_norm_mix_pre, w_in=m_w_in, b_gate=m_b_gate, ln_v_g=m_ln_v_g, ln_v_b=m_ln_v_b, w_s=m_w_s,
             b_s=m_b_s, w_a_proj=m_w_a_proj, w_b_proj=m_w_b_proj, w_out=m_w_out, norm_mix_post=m_norm_mix_post,
             norm_ffn_pre=m_norm_ffn_pre, w_ff1=m_w_ff1, w_ff2=m_w_ff2, norm_ffn_post=m_norm_ffn_post)
    v = dict(norm_mix_pre=v_norm_mix_pre, w_in=v_w_in, b_gate=v_b_gate, ln_v_g=v_ln_v_g, ln_v_b=v_ln_v_b, w_s=v_w_s,
             b_s=v_b_s, w_a_proj=v_w_a_proj, w_b_proj=v_w_b_proj, w_out=v_w_out, norm_mix_post=v_norm_mix_post,
             norm_ffn_pre=v_norm_ffn_pre, w_ff1=v_w_ff1, w_ff2=v_w_ff2, norm_ffn_post=v_norm_ffn_post)
    chip = 2 * lax.axis_index("x") + lax.axis_index("y")
    core = lax.axis_index("c")

    where = jnp.stack([chip, core]).astype(jnp.int32)
    wg_in = place_shard("place_w_in", w_in[0], where, BF16, 256)
    bg_all = place_shard("place_b_gate", jnp.pad(b_gate[0], ((0, 14), (0, 0))), where, F32, 16)
    vecs = (norm_mix_pre, ln_v_g, ln_v_b, norm_mix_post, norm_ffn_pre, norm_ffn_post)
    loss, dx, _, small, parts, got, packed = local_step(
        x[0], loss_target[0], vecs, w_s[0], b_s[0].T, bg_all, wg_in, [w[n][0] for n in BIG[1:]],
        core=jnp.reshape(core, (1,)).astype(jnp.int32),
        order=jnp.stack([chip, chip ^ 2, chip ^ 1, chip ^ 3]).astype(jnp.int32), where=where)

    halves = [sum_chips("sum_" + n, p, r, where, min(p.shape[1], 256)) for n, p, r in zip(BIG, parts, got)]
    joined, first = join_halves(halves, pack_vector(small["norm_mix_pre"], where))
    grads = dict(zip(BIG, joined))
    new = {}
    for n in BIG:
        shape = w[n].shape
        res = adamw("adamw_" + n, w[n][0], grads[n], m[n][0], v[n][0], min(shape[1], 256))
        new[n] = tuple(r.reshape(shape) for r in res)
    small_new, loss = adamw_small(packed, first, jnp.reshape(chip, (1,)).astype(jnp.int32), w, m, v)
    new.update(small_new)

    outs = [loss, dx[None]]
    for i in range(4):
        outs += [new[n][i] for n in ORDER]
    return tuple(outs)
```

```python
import functools
import math
import typing

import numpy as np
import jax
import jax.numpy as jnp
from jax import lax
from jax.experimental import pallas as pl
from jax.experimental.pallas import tpu as pltpu

F32 = jnp.float32
BF16 = jnp.bfloat16
MESH = pl.DeviceIdType.MESH

D = 1024
EPS = 1e-6
CHUNK = 128
GROUPS = 8
HEADS = 16
HEAD_DIM = 64
ATT_T = 256
ATT_GROUP = 8
ATT_BWD_GROUP = 2
N_CHIPS = 4
D_FF = 4 * D
IN_COLS = 7 * D
IN_SHARD = IN_COLS // N_CHIPS
MASKED = -1e30
VMEM_LIMIT = 56 * 2 ** 20

ADAM_LR, ADAM_B1, ADAM_B2, ADAM_EPS, ADAM_WD, ADAM_STEP = 0.001, 0.9, 0.999, 1e-08, 0.01, 10

NN = (((1,), (0,)), ((), ()))
NT = (((1,), (1,)), ((), ()))
TN = (((0,), (0,)), ((), ()))


def _dot(a, b, dims=NN):
    return lax.dot_general(a, b, dims, preferred_element_type=F32)


def _params(*sem, communicates=False):
    return pltpu.CompilerParams(dimension_semantics=sem or None, vmem_limit_bytes=VMEM_LIMIT,
                                has_side_effects=communicates)


def _rows(tr, c, col=0):
    return pl.BlockSpec((tr, c), lambda i: (i, col))


def _full(shape):
    n = len(shape)
    return pl.BlockSpec(shape, lambda *_: (0,) * n)


def _gelu(x):
    k = math.sqrt(2.0 / math.pi)
    return 0.5 * x * (1.0 + jnp.tanh(k * (x + 0.044715 * x * x * x)))


def _gelu_and_grad(x):
    k = math.sqrt(2.0 / math.pi)
    t = jnp.tanh(k * (x + 0.044715 * x * x * x))
    g = 0.5 * x * (1.0 + t)
    dg = 0.5 * (1.0 + t) + 0.5 * x * (1.0 - t * t) * (k * (1.0 + 3.0 * 0.044715 * x * x))
    return g, dg


def _sigmoid(x):
    return 1.0 / (1.0 + jnp.exp(-x))


def _rms(x):
    r = lax.rsqrt(jnp.mean(x * x, axis=-1, keepdims=True) + EPS)
    return x * r, r


def _rms_bwd(dn, xhat, r):
    return r * (dn - xhat * jnp.mean(dn * xhat, axis=-1, keepdims=True))


def norm_pre(x, g):
    s = x.shape[0]
    tr = 512

    def body(x_ref, g_ref, h_ref):
        xhat, _ = _rms(x_ref[...])
        h_ref[...] = (xhat * g_ref[...]).astype(BF16)

    return pl.pallas_call(
        body, name="norm_pre", grid=(s // tr,),
        in_specs=[_rows(tr, D), _full((1, D))], out_specs=_rows(tr, D),
        out_shape=jax.ShapeDtypeStruct((s, D), BF16), compiler_params=_params("parallel"),
    )(x, g)


def mm_in(h, wg, order, staged, casting=()):
    s = h.shape[0]
    tm, tn = 1024, IN_SHARD // 2
    per = IN_SHARD // tn
    m = len(casting)
    nj, ni = N_CHIPS * per, s // tm
    cast_steps = per * ni

    def body(order_ref, *refs):
        a_ref = refs[0]
        cast_in = refs[2:2 + m]
        o_ref, held = refs[2 + m], refs[3 + m]
        cast_out = refs[4 + m:4 + 2 * m]
        tile, tile_sem = refs[4 + 2 * m:6 + 2 * m]
        sems = refs[6 + 2 * m:]
        j, i = pl.program_id(0), pl.program_id(1)

        @pl.when(j * ni + i < cast_steps)
        def _():
            for src, dst in zip(cast_in, cast_out):
                dst[...] = src[...].astype(BF16)

        def fetch(t):
            chip = order_ref[t // per]
            return pltpu.make_async_copy(held.at[chip, :, pl.ds((t % per) * tn, tn)], tile.at[t % 2],
                                         tile_sem.at[t % 2])

        if staged:
            near = _gather_phases([held], *sems[:2], [(0, D, (0, 1))])
            far = _relay_phases(held, *sems[2:])

        @pl.when(i == 0)
        def _():
            @pl.when(j == 0)
            def _():
                if staged:
                    near[0]()
                fetch(0).start()

            fetch(j).wait()
            ahead = j + 1 < nj
            if staged:
                ahead = ahead & (j + 1 != per) & (j + 1 != 3 * per)
            pl.when(ahead)(lambda: fetch(j + 1).start())

        rows = pl.ds(pl.multiple_of(i * tm, tm), tm)
        o_ref[...] = _dot(a_ref[rows, :], tile[j % 2]).astype(BF16)

        if staged:
            @pl.when((i == ni - 1) & (j == per - 1))
            def _():
                near[1]()
                near[2]()
                far[0]()
                fetch(per).start()

            @pl.when((i == ni - 1) & (j == 3 * per - 1))
            def _():
                far[1]()
                far[2]()
                fetch(3 * per).start()

    def cast_block(j, i, o):
        return jnp.minimum(j * ni + i, cast_steps - 1)

    out = pl.pallas_call(
        body, name="mm_in",
        grid_spec=pltpu.PrefetchScalarGridSpec(
            num_scalar_prefetch=1, grid=(nj, ni),
            in_specs=[pl.BlockSpec((s, D), lambda j, i, o: (0, 0)), ANY]
            + [pl.BlockSpec((a.shape[0] // cast_steps, a.shape[1]), lambda j, i, o: (cast_block(j, i, o), 0))
               for a in casting],
            out_specs=[pl.BlockSpec((tm, tn), lambda j, i, o: (i, o[j // per] * per + j % per)), ANY]
            + [pl.BlockSpec((None, a.shape[0] // cast_steps, a.shape[1]),
                            lambda j, i, o: (o[0], cast_block(j, i, o), 0)) for a in casting],
            scratch_shapes=[pltpu.VMEM((2, D, tn), BF16), pltpu.SemaphoreType.DMA((2,))]
            + (_gather_sems(1) + _relay_sems() if staged else [])),
        out_shape=[jax.ShapeDtypeStruct((s, IN_COLS), BF16), jax.ShapeDtypeStruct(wg.shape, wg.dtype)]
        + [jax.ShapeDtypeStruct((N_CHIPS,) + a.shape, BF16) for a in casting],
        input_output_aliases={2: 1},
        compiler_params=_params("arbitrary", "arbitrary", communicates=staged),
    )(order, h, wg, *casting)
    return out[0], out[1], out[2:]


def _tril_ws(ws_ref, g):
    r = lax.broadcasted_iota(jnp.int32, (CHUNK, CHUNK), 0)
    c = lax.broadcasted_iota(jnp.int32, (CHUNK, CHUNK), 1)
    return jnp.where(c <= r, ws_ref[g], 0.0).astype(BF16)


def _layer_norm(v):
    mu = jnp.mean(v, axis=-1, keepdims=True)
    d = v - mu
    rstd = lax.rsqrt(jnp.mean(d * d, axis=-1, keepdims=True) + EPS)
    return d * rstd, rstd


def gating_fwd(z, ln_g, ln_b, w_s, bs_t, gathering):
    s = z.shape[0]
    n = len(gathering)
    steps = s // CHUNK

    def body(*refs):
        u_ref, v_ref, lg_ref, lb_ref, ws_ref, bst_ref = refs[:6]
        ya_ref = refs[6 + n]
        ci = pl.program_id(0)
        if n:
            send, pass_on, finish = _gather_phases(refs[7 + n:7 + 2 * n], *refs[7 + 2 * n:], _spans(gathering))
            pl.when(ci == 0)(send)
        ug = _gelu(u_ref[...].astype(F32))
        vhat, _ = _layer_norm(_gelu(v_ref[...].astype(F32)))
        vn = (vhat * lg_ref[...] + lb_ref[...]).astype(BF16)
        for g in range(GROUPS):
            cols = slice(g * CHUNK, (g + 1) * CHUNK)
            mixed = _dot(_tril_ws(ws_ref, g), vn[:, cols]) + bst_ref[:, g:g + 1]
            ya_ref[:, cols] = (ug[:, cols] * mixed).astype(BF16)
        if n:
            pl.when(ci == steps - 1)(pass_on)
            pl.when(ci == steps - 1)(finish)

    out = pl.pallas_call(
        body, name="gating_fwd", grid=(steps,),
        in_specs=[_rows(CHUNK, D, 0), _rows(CHUNK, D, 1), _full((1, D)), _full((1, D)),
                  _full((GROUPS, CHUNK, CHUNK)), _full((CHUNK, GROUPS))] + [ANY] * n,
        out_specs=[_rows(CHUNK, D)] + [ANY] * n,
        out_shape=[jax.ShapeDtypeStruct((s, D), BF16)]
        + [jax.ShapeDtypeStruct(a.shape, a.dtype) for a in _arrays(gathering)],
        input_output_aliases={6 + w: 1 + w for w in range(n)},
        scratch_shapes=_gather_sems(n) if n else [],
        compiler_params=_params("arbitrary", communicates=bool(n)),
    )(z, z, ln_g, ln_b, w_s, bs_t, *_arrays(gathering))
    return out[0], out[1:]


def _attn_tables(s):
    nd = s // ATT_T
    r = np.arange(ATT_T)[None, :, None]
    c = np.arange(ATT_T)[None, None, :]
    delta = np.arange(nd)[:, None, None] * ATT_T + r - c
    count = np.zeros(delta.shape, np.int64)
    for window, dilation in ((128, 1), (512, 4), (2048, 16)):
        count += (delta >= 0) & (delta % dilation == 0) & (delta <= window)
    logc = np.where(count > 0, np.log(np.maximum(count, 1)), MASKED)
    return jnp.asarray(logc, F32)


AUG = 3


def _split3_np(x):
    terms, rest = [], np.asarray(x, np.float64)
    for _ in range(AUG):
        term = np.asarray(rest.astype(jnp.bfloat16), np.float64)
        terms.append(term)
        rest = rest - term
    return terms


def _split3(x):
    terms, rest = [], x
    for _ in range(AUG):
        term = rest.astype(BF16).astype(F32)
        terms.append(term)
        rest = rest - term
    return terms


def _alibi_tables(s):
    nb = s // ATT_T
    slopes = np.exp2(-8.0 * np.arange(1, HEADS + 1, dtype=np.float64) / HEADS)
    ka = np.zeros((HEADS // 2, 2, ATT_T, 128), np.float32)
    kb = np.zeros((HEADS // 2, 2, nb, 128), np.float32)
    for p in range(HEADS // 2):
        for e in range(2):
            base = HEAD_DIM * (1 - e)
            for a, term in enumerate(_split3_np(slopes[2 * p + e] * np.arange(ATT_T))):
                ka[p, e, :, base + a] = term
            for a, term in enumerate(_split3_np(slopes[2 * p + e] * ATT_T * np.arange(nb))):
                kb[p, e, :, base + AUG + a] = term
            ka[p, e, :, base + 2 * AUG:base + 3 * AUG] = 1.0
    return jnp.asarray(ka), jnp.asarray(kb)


def _head_masks():
    lane = lax.broadcasted_iota(jnp.int32, (1, 128), 1)
    first = lane < HEAD_DIM

    def ones(e, n):
        base = HEAD_DIM * (1 - e)
        return ((lane >= base) & (lane < base + n)).astype(F32)

    return first, lane, ones


def _place3(lane, at, terms, other):
    for a, term in enumerate(terms):
        other = jnp.where(lane == at + a, term, other)
    return other


def attn_fwd(z, logc, ka, kb, gathering):
    s = z.shape[0]
    nq = s // ATT_T
    t = ATT_T
    n = len(gathering)
    grp = ATT_GROUP
    ngrp = HEADS // 2 // grp
    wide = 128 * grp
    qcol, kcol, vcol = 2 * D // wide, 3 * D // wide, 4 * D // wide

    def body(*refs):
        q_ref, k_ref, v_ref, lc_ref, ka_ref, kb_ref = refs[:6]
        y_ref, lse_ref = refs[6 + n:8 + n]
        q_s, k_s, v_s, m_s, l_s, acc_s = refs[8 + 2 * n:14 + 2 * n]
        gi, qi = pl.program_id(0), pl.program_id(1)
        first, lane, ones = _head_masks()
        if n:
            send, pass_on, finish = _gather_phases(refs[8 + n:8 + 2 * n], *refs[14 + 2 * n:], _spans(gathering))
            pl.when((gi == 0) & (qi == 0))(send)

        @pl.when(qi == 0)
        def _():
            sel = jnp.broadcast_to(first.astype(F32), (t, 128))
            for pr in range(grp):
                cols = slice(pr * 128, (pr + 1) * 128)
                for jb in range(nq):
                    kj = k_ref[jb * t:(jb + 1) * t, cols].astype(F32)
                    vj = v_ref[jb * t:(jb + 1) * t, cols].astype(F32)
                    k_s[pr, 0, jb] = jnp.where(first, kj, ka_ref[pr, 0] + kb_ref[pr, 0, jb:jb + 1, :]).astype(BF16)
                    k_s[pr, 1, jb] = jnp.where(first, ka_ref[pr, 1] + kb_ref[pr, 1, jb:jb + 1, :], kj).astype(BF16)
                    v_s[pr, jb, 0:t, 0:128] = jnp.where(first, vj, 0.0).astype(BF16)
                    v_s[pr, jb, t:2 * t, 0:128] = jnp.where(first, 0.0, vj).astype(BF16)
                    v_s[pr, jb, 0:t, 128:256] = sel.astype(BF16)
                    v_s[pr, jb, t:2 * t, 128:256] = (1.0 - sel).astype(BF16)

        for pr in range(grp):
            q = q_ref[:, pr * 128:(pr + 1) * 128].astype(F32) * (1.0 / math.sqrt(HEAD_DIM))
            q_s[pr, 0] = jnp.where(first, q, ones(0, 2 * AUG)).astype(BF16)
            q_s[pr, 1] = jnp.where(first, ones(1, 2 * AUG), q).astype(BF16)
        m_s[...] = jnp.full_like(m_s, MASKED)
        l_s[...] = jnp.zeros_like(l_s)
        acc_s[...] = jnp.zeros_like(acc_s)

        def scores(j):
            return tuple(_dot(q_s[pr, e], k_s[pr, e, j], NT) for pr in range(grp) for e in range(2))

        def step(j, carry):
            softmax_block(j, scores(j))
            return carry

        def softmax_block(j, u):
            lc = lc_ref[qi - j]
            for pr in range(grp):
                u0 = u[2 * pr] + lc
                u1 = u[2 * pr + 1] + lc
                m0, m1 = m_s[pr, 0], m_s[pr, 1]
                n0 = jnp.maximum(m0, jnp.max(u0, axis=-1, keepdims=True))
                n1 = jnp.maximum(m1, jnp.max(u1, axis=-1, keepdims=True))
                m_s[pr, 0], m_s[pr, 1] = n0, n1
                p = jnp.concatenate([jnp.exp(u0 - jnp.concatenate([n0, n0], axis=1)).astype(BF16),
                                     jnp.exp(u1 - jnp.concatenate([n1, n1], axis=1)).astype(BF16)], axis=1)
                pv = _dot(p, v_s[pr, j])
                alpha = jnp.where(first, jnp.exp(m0 - n0), jnp.exp(m1 - n1))
                acc_s[pr] = acc_s[pr] * alpha + pv[:, 0:128]
                l_s[pr] = l_s[pr] * alpha + pv[:, 128:256]

        lax.fori_loop(0, qi + 1, step, 0)
        for pr in range(grp):
            cols = slice(pr * 128, (pr + 1) * 128)
            y_ref[:, cols] = (acc_s[pr] / l_s[pr]).astype(BF16)
            lse_ref[:, cols] = jnp.where(first, m_s[pr, 0], m_s[pr, 1]) + jnp.log(l_s[pr])
        if n:
            pl.when((gi == ngrp - 1) & (qi == nq - 1))(pass_on)
            pl.when((gi == ngrp - 1) & (qi == nq - 1))(finish)

    out = pl.pallas_call(
        body, name="attn_fwd", grid=(ngrp, nq),
        in_specs=[pl.BlockSpec((t, wide), lambda g, i: (i, qcol + g)),
                  pl.BlockSpec((s, wide), lambda g, i: (0, kcol + g)),
                  pl.BlockSpec((s, wide), lambda g, i: (0, vcol + g)),
                  _full((nq, t, t)),
                  pl.BlockSpec((grp, 2, t, 128), lambda g, i: (g, 0, 0, 0)),
                  pl.BlockSpec((grp, 2, nq, 128), lambda g, i: (g, 0, 0, 0))] + [ANY] * n,
        out_specs=[pl.BlockSpec((t, wide), lambda g, i: (i, g)), pl.BlockSpec((t, wide), lambda g, i: (i, g))]
        + [ANY] * n,
        out_shape=[jax.ShapeDtypeStruct((s, D), BF16), jax.ShapeDtypeStruct((s, D), F32)]
        + [jax.ShapeDtypeStruct(a.shape, a.dtype) for a in _arrays(gathering)],
        input_output_aliases={6 + w: 2 + w for w in range(n)},
        scratch_shapes=[pltpu.VMEM((grp, 2, t, 128), BF16), pltpu.VMEM((grp, 2, nq, t, 128), BF16),
                        pltpu.VMEM((grp, nq, 2 * t, 256), BF16), pltpu.VMEM((grp, 2, t, 128), F32),
                        pltpu.VMEM((grp, t, 128), F32), pltpu.VMEM((grp, t, 128), F32)]
        + (_gather_sems(n) if n else []),
        compiler_params=_params("arbitrary", "arbitrary", communicates=bool(n)),
    )(z, z, z, logc, ka, kb, *_arrays(gathering))
    return out[0], out[1], out[2:]


def proj_merge(ya, yb, wa, wb, z, bg, gathering):
    s = ya.shape[0]
    tm = 512
    n = len(gathering)
    steps = s // tm

    def body(*refs):
        ya_ref, yb_ref, wa_ref, wb_ref, ga_ref, gb_ref, bg_ref = refs[:7]
        mg_ref, pa_ref, pb_ref = refs[7 + n:10 + n]
        i = pl.program_id(0)
        if n:
            send, pass_on, finish = _gather_phases(refs[10 + n:10 + 2 * n], *refs[10 + 2 * n:], _spans(gathering))
            pl.when(i == 0)(send)
            pl.when(i == steps - 1)(pass_on)
        pa = _dot(ya_ref[...], wa_ref[...])
        pb = _dot(yb_ref[...], wb_ref[...])
        sa = _sigmoid(ga_ref[...] + bg_ref[0:1, :])
        sb = _sigmoid(gb_ref[...] + bg_ref[1:2, :])
        mg_ref[...] = (sa * pa + sb * pb).astype(BF16)
        pa_ref[...] = pa.astype(BF16)
        pb_ref[...] = pb.astype(BF16)
        if n:
            pl.when(i == steps - 1)(finish)

    out = jax.ShapeDtypeStruct((s, D), BF16)
    res = pl.pallas_call(
        body, name="proj_merge", grid=(steps,),
        in_specs=[_rows(tm, D), _rows(tm, D), _full((D, D)), _full((D, D)),
                  _rows(tm, D, 5), _rows(tm, D, 6), _full((2, D))] + [ANY] * n,
        out_specs=[_rows(tm, D)] * 3 + [ANY] * n,
        out_shape=[out] * 3 + [jax.ShapeDtypeStruct(a.shape, a.dtype) for a in _arrays(gathering)],
        input_output_aliases={7 + w: 3 + w for w in range(n)},
        scratch_shapes=_gather_sems(n) if n else [],
        compiler_params=_params("arbitrary", communicates=bool(n)),
    )(ya, yb, wa, wb, z, z, bg, *_arrays(gathering))
    return res[0], res[1], res[2], res[3:]


def out_norm(merged, w_out, x, g_post, g_fpre, gathering):
    s = x.shape[0]
    tm = 512
    n = len(gathering)
    steps = s // tm

    def body(*refs):
        mg_ref, w_ref, x_ref, gp_ref, gf_ref = refs[:5]
        o_ref, x1_ref, h2_ref = refs[5 + n:8 + n]
        i = pl.program_id(0)
        if n:
            send, pass_on, finish = _gather_phases(refs[8 + n:8 + 2 * n], *refs[8 + 2 * n:], _spans(gathering))
            pl.when(i == 0)(send)
            pl.when(i == steps - 1)(pass_on)
        o = _dot(mg_ref[...], w_ref[...])
        ohat, _ = _rms(o)
        x1 = x_ref[...] + ohat * gp_ref[...]
        x1hat, _ = _rms(x1)
        o_ref[...] = o
        x1_ref[...] = x1
        h2_ref[...] = (x1hat * gf_ref[...]).astype(BF16)
        if n:
            pl.when(i == steps - 1)(finish)

    res = pl.pallas_call(
        body, name="out_norm", grid=(steps,),
        in_specs=[_rows(tm, D), _full((D, D)), _rows(tm, D), _full((1, D)), _full((1, D))] + [ANY] * n,
        out_specs=[_rows(tm, D)] * 3 + [ANY] * n,
        out_shape=[jax.ShapeDtypeStruct((s, D), F32), jax.ShapeDtypeStruct((s, D), F32),
                   jax.ShapeDtypeStruct((s, D), BF16)]
        + [jax.ShapeDtypeStruct(a.shape, a.dtype) for a in _arrays(gathering)],
        input_output_aliases={5 + w: 3 + w for w in range(n)},
        scratch_shapes=_gather_sems(n) if n else [],
        compiler_params=_params("arbitrary", communicates=bool(n)),
    )(merged, w_out, x, g_post, g_fpre, *_arrays(gathering))
    return res[0], res[1], res[2], res[3:]


def mm_ff1(h2, wg, gathering):
    s = h2.shape[0]
    tm = 1024
    n = len(gathering)
    ni = s // tm

    def body(*refs):
        a_ref, b_ref = refs[:2]
        o_ref, r_ref = refs[2 + n:4 + n]
        i, j = pl.program_id(0), pl.program_id(1)
        if n:
            send, pass_on, finish = _gather_phases(refs[4 + n:4 + 2 * n], *refs[4 + 2 * n:], _spans(gathering))
            pl.when((i == 0) & (j == 0))(send)
            pl.when((i == ni - 1) & (j == N_CHIPS // 2))(pass_on)
        a = _dot(a_ref[...], b_ref[...])
        o_ref[...] = a.astype(BF16)
        r = jnp.maximum(a, 0.0)
        r_ref[...] = (r * r).astype(BF16)
        if n:
            pl.when((i == ni - 1) & (j == N_CHIPS - 1))(finish)

    res = pl.pallas_call(
        body, name="mm_ff1", grid=(ni, N_CHIPS),
        in_specs=[pl.BlockSpec((tm, D), lambda i, j: (i, 0)), pl.BlockSpec((None, D, D), lambda i, j: (j, 0, 0))]
        + [ANY] * n,
        out_specs=[pl.BlockSpec((tm, D), lambda i, j: (i, j))] * 2 + [ANY] * n,
        out_shape=[jax.ShapeDtypeStruct((s, D_FF), BF16), jax.ShapeDtypeStruct((s, D_FF), BF16)]
        + [jax.ShapeDtypeStruct(a.shape, a.dtype) for a in _arrays(gathering)],
        input_output_aliases={2 + w: 2 + w for w in range(n)},
        scratch_shapes=_gather_sems(n) if n else [],
        compiler_params=_params("arbitrary", "arbitrary", communicates=bool(n)),
    )(h2, wg, *_arrays(gathering))
    return res[0], res[1], res[2:]


def ff2_loss(rl, w_ff2, x1, target, g_fpost):
    s = x1.shape[0]
    tm = 256

    def body(rl_ref, w_ref, x1_ref, t_ref, g_ref, dy_ref, df_ref, dg_ref, loss_ref):
        @pl.when(pl.program_id(0) == 0)
        def _():
            dg_ref[...] = jnp.zeros_like(dg_ref)
            loss_ref[...] = jnp.zeros_like(loss_ref)

        f = _dot(rl_ref[...], w_ref[...])
        fhat, r = _rms(f)
        err = x1_ref[...] + fhat * g_ref[...] - t_ref[...]
        loss_ref[...] += 0.5 * jnp.sum(jnp.mean(err * err, axis=-1, keepdims=True), axis=0, keepdims=True)
        dy = err * (1.0 / D)
        dy_ref[...] = dy
        dg_ref[...] += jnp.sum(dy * fhat, axis=0, keepdims=True)
        df_ref[...] = _rms_bwd(dy * g_ref[...], fhat, r).astype(BF16)

    return pl.pallas_call(
        body, name="ff2_loss", grid=(s // tm,),
        in_specs=[_rows(tm, D_FF), _full((D_FF, D)), _rows(tm, D), _rows(tm, D), _full((1, D))],
        out_specs=[_rows(tm, D), _rows(tm, D), _full((1, D)), _full((1, 1))],
        out_shape=[jax.ShapeDtypeStruct((s, D), F32), jax.ShapeDtypeStruct((s, D), BF16),
                   jax.ShapeDtypeStruct((1, D), F32), jax.ShapeDtypeStruct((1, 1), F32)],
        compiler_params=_params("arbitrary"),
    )(rl, w_ff2, x1, target, g_fpost)


def mm_tn(name, a, b, ta, tb, out_shape, out_spec):
    s = a.shape[0]

    def body(a_ref, b_ref, o_ref):
        o_ref[...] = _dot(a_ref[...], b_ref[...], TN)

    return pl.pallas_call(
        body, name=name, grid=(a.shape[1] // ta, b.shape[1] // tb),
        in_specs=[pl.BlockSpec((s, ta), lambda i, j: (0, i)), pl.BlockSpec((s, tb), lambda i, j: (0, j))],
        out_specs=out_spec, out_shape=jax.ShapeDtypeStruct(out_shape, F32),
        compiler_params=_params("parallel", "parallel"),
    )(a, b)


def mm_nt(name, a, w):
    s = a.shape[0]
    tm = 512

    def body(a_ref, w_ref, o_ref):
        o_ref[...] = _dot(a_ref[...], w_ref[...], NT).astype(BF16)

    return pl.pallas_call(
        body, name=name, grid=(s // tm,), in_specs=[_rows(tm, D), _full((D, D))], out_specs=_rows(tm, D),
        out_shape=jax.ShapeDtypeStruct((s, D), BF16), compiler_params=_params("parallel"),
    )(a, w)


def ff2_bwd(df, w_ff2, a):
    s = df.shape[0]
    tm = 1024

    def body(df_ref, w_ref, a_ref, da_ref):
        drl = _dot(df_ref[...], w_ref[...], NT)
        da_ref[...] = (drl * (2.0 * jnp.maximum(a_ref[...].astype(F32), 0.0))).astype(BF16)

    return pl.pallas_call(
        body, name="ff2_bwd", grid=(s // tm, D_FF // D),
        in_specs=[pl.BlockSpec((tm, D), lambda i, j: (i, 0)), pl.BlockSpec((D, D), lambda i, j: (j, 0)),
                  pl.BlockSpec((tm, D), lambda i, j: (i, j))],
        out_specs=pl.BlockSpec((tm, D), lambda i, j: (i, j)),
        out_shape=jax.ShapeDtypeStruct((s, D_FF), BF16), compiler_params=_params("parallel", "parallel"),
    )(df, w_ff2, a)


def ff1_bwd_norms(da, wg, x1, o, dy, g_fpre, g_post, swapping):
    s = x1.shape[0]
    tm = 256
    n = len(swapping)
    steps = s // tm

    def body(*refs):
        da_ref, w_ref, x1_ref, o_ref, dy_ref, gf_ref, gp_ref = refs[:7]
        dx1_ref, do_ref, dgf_ref, dgp_ref = refs[7 + n:11 + n]
        i = pl.program_id(0)
        if n:
            send, finish = _swap_phases(refs[7:7 + n], refs[11 + n:11 + 2 * n], *refs[11 + 2 * n:])
            pl.when(i == 0)(send)

        @pl.when(i == 0)
        def _():
            dgf_ref[...] = jnp.zeros_like(dgf_ref)
            dgp_ref[...] = jnp.zeros_like(dgp_ref)

        dh2 = _dot(da_ref[:, 0:D], w_ref[0], NT)
        for k in range(1, N_CHIPS):
            dh2 = dh2 + _dot(da_ref[:, k * D:(k + 1) * D], w_ref[k], NT)
        x1hat, r2 = _rms(x1_ref[...])
        dgf_ref[...] += jnp.sum(dh2 * x1hat, axis=0, keepdims=True)
        dx1 = dy_ref[...] + _rms_bwd(dh2 * gf_ref[...], x1hat, r2)
        ohat, r1 = _rms(o_ref[...])
        dgp_ref[...] += jnp.sum(dx1 * ohat, axis=0, keepdims=True)
        dx1_ref[...] = dx1
        do_ref[...] = _rms_bwd(dx1 * gp_ref[...], ohat, r1).astype(BF16)
        if n:
            pl.when(i == steps - 1)(finish)

    res = pl.pallas_call(
        body, name="ff1_bwd_norms", grid=(steps,),
        in_specs=[_rows(tm, D_FF), _full((N_CHIPS, D, D)), _rows(tm, D), _rows(tm, D), _rows(tm, D),
                  _full((1, D)), _full((1, D))] + [ANY] * n,
        out_specs=[_rows(tm, D), _rows(tm, D), _full((1, D)), _full((1, D))] + [ANY] * n,
        out_shape=[jax.ShapeDtypeStruct((s, D), F32), jax.ShapeDtypeStruct((s, D), BF16),
                   jax.ShapeDtypeStruct((1, D), F32), jax.ShapeDtypeStruct((1, D), F32)] + _swap_shapes(swapping),
        scratch_shapes=_swap_sems(n) if n else [],
        compiler_params=_params("arbitrary", communicates=bool(n)),
    )(da, wg, x1, o, dy, g_fpre, g_post, *swapping)
    return res[0], res[1], res[2], res[3], res[4:]


def out_bwd_gates(do, w_out, pa, pb, z, bg):
    s = do.shape[0]
    tm = 512

    def body(do_ref, w_ref, pa_ref, pb_ref, ga_ref, gb_ref, bg_ref, dpa_ref, dpb_ref, dga_ref, dgb_ref, dbg_ref):
        @pl.when(pl.program_id(0) == 0)
        def _():
            dbg_ref[...] = jnp.zeros_like(dbg_ref)

        dm = _dot(do_ref[...], w_ref[...], NT)
        sa = _sigmoid(ga_ref[...] + bg_ref[0:1, :])
        sb = _sigmoid(gb_ref[...] + bg_ref[1:2, :])
        dpa_ref[...] = (dm * sa).astype(BF16)
        dpb_ref[...] = (dm * sb).astype(BF16)
        dga = dm * pa_ref[...].astype(F32) * (sa * (1.0 - sa))
        dgb = dm * pb_ref[...].astype(F32) * (sb * (1.0 - sb))
        dga_ref[...] = dga.astype(BF16)
        dgb_ref[...] = dgb.astype(BF16)
        dbg_ref[0:1, :] += jnp.sum(dga, axis=0, keepdims=True)
        dbg_ref[1:2, :] += jnp.sum(dgb, axis=0, keepdims=True)

    out = jax.ShapeDtypeStruct((s, D), BF16)
    return pl.pallas_call(
        body, name="out_bwd_gates", grid=(s // tm,),
        in_specs=[_rows(tm, D), _full((D, D)), _rows(tm, D), _rows(tm, D), _rows(tm, D, 5), _rows(tm, D, 6),
                  _full((2, D))],
        out_specs=[_rows(tm, D)] * 4 + [_full((2, D))],
        out_shape=[out] * 4 + [jax.ShapeDtypeStruct((2, D), F32)], compiler_params=_params("arbitrary"),
    )(do, w_out, pa, pb, z, z, bg)


def gating_bwd(z, dya, ln_g, ln_b, w_s, bs_t, swapping):
    s = z.shape[0]
    ones = functools.partial(jnp.ones, (8, CHUNK), BF16)
    n = len(swapping)

    def body(*refs):
        u_ref, v_ref, dya_ref, lg_ref, lb_ref, ws_ref, bst_ref = refs[:7]
        du_ref, dv_ref, dws_ref, dbs_ref, dlg_ref, dlb_ref = refs[7 + n:13 + n]
        dvn_ref = refs[13 + 2 * n]
        ci = pl.program_id(0)
        if n:
            send, finish = _swap_phases(refs[7:7 + n], refs[13 + n:13 + 2 * n], *refs[14 + 2 * n:])
            pl.when(ci == 0)(send)

        @pl.when(ci == 0)
        def _():
            dws_ref[...] = jnp.zeros_like(dws_ref)
            dbs_ref[...] = jnp.zeros_like(dbs_ref)
            dlg_ref[...] = jnp.zeros_like(dlg_ref)
            dlb_ref[...] = jnp.zeros_like(dlb_ref)

        ug, dug_du = _gelu_and_grad(u_ref[...].astype(F32))
        vg, dvg_dv = _gelu_and_grad(v_ref[...].astype(F32))
        vhat, rstd = _layer_norm(vg)
        vn = (vhat * lg_ref[...] + lb_ref[...]).astype(BF16)
        dya = dya_ref[...].astype(F32)
        for g in range(GROUPS):
            cols = slice(g * CHUNK, (g + 1) * CHUNK)
            ws = _tril_ws(ws_ref, g)
            mixed = _dot(ws, vn[:, cols]) + bst_ref[:, g:g + 1]
            du_ref[:, cols] = (dya[:, cols] * mixed * dug_du[:, cols]).astype(BF16)
            dmix = (dya[:, cols] * ug[:, cols]).astype(BF16)
            dbs_ref[g] += _dot(ones(), dmix, NT)
            dws_ref[g] += _dot(dmix, vn[:, cols], NT)
            dvn_ref[:, cols] = _dot(ws, dmix, TN)
        dvn = dvn_ref[...]
        dlg_ref[...] += jnp.sum(dvn * vhat, axis=0, keepdims=True)
        dlb_ref[...] += jnp.sum(dvn, axis=0, keepdims=True)
        dvh = dvn * lg_ref[...]
        dvg = rstd * (dvh - jnp.mean(dvh, axis=-1, keepdims=True)
                      - vhat * jnp.mean(dvh * vhat, axis=-1, keepdims=True))
        dv_ref[...] = (dvg * dvg_dv).astype(BF16)

        @pl.when(ci == pl.num_programs(0) - 1)
        def _():
            r = lax.broadcasted_iota(jnp.int32, (CHUNK, CHUNK), 0)
            c = lax.broadcasted_iota(jnp.int32, (CHUNK, CHUNK), 1)
            for g in range(GROUPS):
                dws_ref[g] = jnp.where(c <= r, dws_ref[g], 0.0)

        if n:
            pl.when(ci == pl.num_programs(0) - 1)(finish)

    out = jax.ShapeDtypeStruct((s, D), BF16)
    res = pl.pallas_call(
        body, name="gating_bwd", grid=(s // CHUNK,),
        in_specs=[_rows(CHUNK, D, 0), _rows(CHUNK, D, 1), _rows(CHUNK, D), _full((1, D)), _full((1, D)),
                  _full((GROUPS, CHUNK, CHUNK)), _full((CHUNK, GROUPS))] + [ANY] * n,
        out_specs=[_rows(CHUNK, D), _rows(CHUNK, D), _full((GROUPS, CHUNK, CHUNK)), _full((GROUPS, 8, CHUNK)),
                   _full((1, D)), _full((1, D))] + [ANY] * n,
        out_shape=[out, out, jax.ShapeDtypeStruct((GROUPS, CHUNK, CHUNK), F32),
                   jax.ShapeDtypeStruct((GROUPS, 8, CHUNK), F32),
                   jax.ShapeDtypeStruct((1, D), F32), jax.ShapeDtypeStruct((1, D), F32)] + _swap_shapes(swapping),
        scratch_shapes=[pltpu.VMEM((CHUNK, D), F32)] + (_swap_sems(n) if n else []),
        compiler_params=_params("arbitrary", communicates=bool(n)),
    )(z, z, dya, ln_g, ln_b, w_s, bs_t, *swapping)
    return (*res[:6], res[6:])


def attn_bwd(z, yb, dyb, lse, logc, ka, kb, scattering, gathering=None):
    s = z.shape[0]
    nq = s // ATT_T
    t = ATT_T
    grp = ATT_BWD_GROUP
    ngrp = HEADS // 2 // grp
    wide = 128 * grp
    qcol, kcol, vcol = 2 * D // wide, 3 * D // wide, 4 * D // wide
    scale = 1.0 / math.sqrt(HEAD_DIM)
    n = len(scattering)
    g8 = 0 if gathering is None else 1

    def body(*refs):
        q_ref, k_ref, v_ref, y_ref, dy_ref, lse_ref, lc_ref, ka_ref, kb_ref = refs[:9]
        dq_ref, dk_ref, dv_ref = refs[9 + n + g8:12 + n + g8]
        qa_s, qt_s, da_s, dt_s, dq_s, dkt_s, dvt_s = refs[12 + 2 * n + 2 * g8:19 + 2 * n + 2 * g8]
        sems = refs[19 + 2 * n + 2 * g8:]
        gi, j = pl.program_id(0), pl.program_id(1)
        first, lane, ones = _head_masks()
        if n:
            send, finish = _scatter_phases(refs[9:9 + n], refs[12 + n + g8:12 + 2 * n + g8], *sems[:2])
            pl.when((gi == 0) & (j == 0))(send)
        if g8:
            send8, pass_on8, finish8 = _allgather8_phases(refs[12 + 2 * n + g8], *sems[2 * (n > 0):])
            pl.when((gi == 0) & (j == 0))(send8)
            pl.when((gi == ngrp - 1) & (j == nq - 1))(pass_on8)

        @pl.when(j == 0)
        def _():
            dq_s[...] = jnp.zeros_like(dq_s)
            for pr in range(grp):
                cols = slice(pr * 128, (pr + 1) * 128)
                for ib in range(nq):
                    rows = slice(ib * t, (ib + 1) * t)
                    q = q_ref[rows, cols].astype(F32) * scale
                    lse = lse_ref[rows, cols]
                    qa_s[pr, 0, ib] = jnp.where(first, q, _place3(lane, HEAD_DIM + 2 * AUG, _split3(-lse[:, 0:1]),
                                                                  ones(0, 2 * AUG))).astype(BF16)
                    qa_s[pr, 1, ib] = jnp.where(
                        first, _place3(lane, 2 * AUG, _split3(-lse[:, HEAD_DIM:HEAD_DIM + 1]), ones(1, 2 * AUG)),
                        q).astype(BF16)
                    qt_s[pr, ib, :, 0:t] = jnp.where(first, q, 0.0).T.astype(BF16)
                    qt_s[pr, ib, :, t:2 * t] = jnp.where(first, 0.0, q).T.astype(BF16)
                    do = dy_ref[rows, cols].astype(F32)
                    prod = do * y_ref[rows, cols].astype(F32)
                    dd0 = jnp.sum(jnp.where(first, prod, 0.0), axis=-1, keepdims=True)
                    dd1 = jnp.sum(jnp.where(first, 0.0, prod), axis=-1, keepdims=True)
                    da_s[pr, 0, ib] = jnp.where(first, do, _place3(lane, HEAD_DIM, _split3(-dd0), 0.0)).astype(BF16)
                    da_s[pr, 1, ib] = jnp.where(first, _place3(lane, 0, _split3(-dd1), 0.0), do).astype(BF16)
                    dt_s[pr, ib, :, 0:t] = jnp.where(first, do, 0.0).T.astype(BF16)
                    dt_s[pr, ib, :, t:2 * t] = jnp.where(first, 0.0, do).T.astype(BF16)

        keys = []
        for pr in range(grp):
            kj = k_ref[:, pr * 128:(pr + 1) * 128].astype(F32)
            vj = v_ref[:, pr * 128:(pr + 1) * 128].astype(F32)
            keys.append((
                jnp.where(first, kj, ka_ref[pr, 0] + kb_ref[pr, 0, pl.ds(j, 1), :]).astype(BF16),
                jnp.where(first, ka_ref[pr, 1] + kb_ref[pr, 1, pl.ds(j, 1), :], kj).astype(BF16),
                jnp.concatenate([jnp.where(first, kj, 0.0), jnp.where(first, 0.0, kj)], axis=0).astype(BF16),
                jnp.where(first, vj, ones(0, AUG)).astype(BF16),
                jnp.where(first, ones(1, AUG), vj).astype(BF16)))
        dkt_s[...] = jnp.zeros_like(dkt_s)
        dvt_s[...] = jnp.zeros_like(dvt_s)

        def step(i, _):
            lc = lc_ref[i - j]
            rows = pl.ds(pl.multiple_of(i * t, t), t)
            for pr in range(grp):
                k0a, k1a, kst, v0a, v1a = keys[pr]
                p0 = jnp.exp(_dot(qa_s[pr, 0, i], k0a, NT) + lc)
                p1 = jnp.exp(_dot(qa_s[pr, 1, i], k1a, NT) + lc)
                e0 = (p0 * _dot(da_s[pr, 0, i], v0a, NT)).astype(BF16)
                e1 = (p1 * _dot(da_s[pr, 1, i], v1a, NT)).astype(BF16)
                dq_s[pr, rows, :] += _dot(jnp.concatenate([e0, e1], axis=1), kst)
                dvt_s[pr] += _dot(dt_s[pr, i], jnp.concatenate([p0.astype(BF16), p1.astype(BF16)], axis=0))
                dkt_s[pr] += _dot(qt_s[pr, i], jnp.concatenate([e0, e1], axis=0))
            return 0

        lax.fori_loop(j, nq, step, 0)
        for pr in range(grp):
            dk_ref[:, pr * 128:(pr + 1) * 128] = dkt_s[pr].T.astype(BF16)
            dv_ref[:, pr * 128:(pr + 1) * 128] = dvt_s[pr].T.astype(BF16)

        @pl.when(j == nq - 1)
        def _():
            for pr in range(grp):
                dq_ref[:, pr * 128:(pr + 1) * 128] = (dq_s[pr] * scale).astype(BF16)

        if n:
            pl.when((gi == ngrp - 1) & (j == nq - 1))(finish)
        if g8:
            pl.when((gi == ngrp - 1) & (j == nq - 1))(finish8)

    colblock = lambda c: pl.BlockSpec((s, wide), lambda g, j: (0, c + g))
    blk = lambda c: pl.BlockSpec((t, wide), lambda g, j: (j, c + g))
    out = jax.ShapeDtypeStruct((s, D), BF16)
    res = pl.pallas_call(
        body, name="attn_bwd", grid=(ngrp, nq),
        in_specs=[colblock(qcol), blk(kcol), blk(vcol), colblock(0), colblock(0), colblock(0),
                  _full((nq, t, t)), pl.BlockSpec((grp, 2, t, 128), lambda g, j: (g, 0, 0, 0)),
                  pl.BlockSpec((grp, 2, nq, 128), lambda g, j: (g, 0, 0, 0))] + [ANY] * (n + g8),
        out_specs=[colblock(0), blk(0), blk(0)] + [ANY] * (n + g8),
        out_shape=[out] * 3 + _scatter_shapes(scattering)
        + ([jax.ShapeDtypeStruct(gathering.shape, gathering.dtype)] if g8 else []),
        input_output_aliases={9 + n: 3 + n} if g8 else {},
        scratch_shapes=[pltpu.VMEM((grp, 2, nq, t, 128), BF16), pltpu.VMEM((grp, nq, 128, 2 * t), BF16),
                        pltpu.VMEM((grp, 2, nq, t, 128), BF16), pltpu.VMEM((grp, nq, 128, 2 * t), BF16),
                        pltpu.VMEM((grp, s, 128), F32), pltpu.VMEM((grp, 128, t), F32),
                        pltpu.VMEM((grp, 128, t), F32)]
        + (_scatter_sems(n) if n else [])
        + ([pltpu.SemaphoreType.DMA((7,)), pltpu.SemaphoreType.DMA((7,))] if g8 else []),
        compiler_params=_params("arbitrary", "arbitrary", communicates=bool(n + g8)),
    )(z, z, z, yb, dyb, lse, logc, ka, kb, *scattering, *([gathering] if g8 else []))
    return res[0], res[1], res[2], res[3:3 + n], (res[3 + n] if g8 else None)


def in_bwd_norm(dz, wg, x, dx1, g_pre, scattering, gathering=None):
    s = x.shape[0]
    tm = 512
    n = len(scattering)
    g = 0 if gathering is None else 1
    last = (s // tm - 1, N_CHIPS - 1)

    def body(*refs):
        dz_ref, w_ref, x_ref, dx1_ref, g_ref = refs[:5]
        dx_ref, dg_ref = refs[5 + n + g:7 + n + g]
        acc_ref = refs[7 + 2 * n + 2 * g]
        sems = refs[8 + 2 * n + 2 * g:]
        i, k = pl.program_id(0), pl.program_id(1)
        if n:
            send, finish = _scatter_phases(refs[5:5 + n], refs[7 + n + g:7 + 2 * n + g], *sems[:2])
            pl.when((i == 0) & (k == 0))(send)
        if g:
            send8, pass_on8, finish8 = _allgather8_phases(refs[7 + 2 * n + g], *sems[2 * (n > 0):])
            pl.when((i == 0) & (k == 0))(send8)
            pl.when((i == last[0]) & (k == last[1]))(pass_on8)

        @pl.when((i == 0) & (k == 0))
        def _():
            dg_ref[...] = jnp.zeros_like(dg_ref)

        part = _dot(dz_ref[...], w_ref[...], NT)

        @pl.when(k == 0)
        def _():
            acc_ref[...] = part

        @pl.when(k > 0)
        def _():
            acc_ref[...] += part

        @pl.when(k == N_CHIPS - 1)
        def _():
            dh = acc_ref[...]
            xhat, r = _rms(x_ref[...])
            dg_ref[...] += jnp.sum(dh * xhat, axis=0, keepdims=True)
            dx_ref[...] = dx1_ref[...] + _rms_bwd(dh * g_ref[...], xhat, r)

        if n:
            pl.when((i == last[0]) & (k == last[1]))(finish)
        if g:
            pl.when((i == last[0]) & (k == last[1]))(finish8)

    row = pl.BlockSpec((tm, D), lambda i, k: (i, 0))
    vec = pl.BlockSpec((1, D), lambda i, k: (0, 0))
    res = pl.pallas_call(
        body, name="in_bwd_norm", grid=(s // tm, N_CHIPS),
        in_specs=[pl.BlockSpec((tm, IN_SHARD), lambda i, k: (i, k)),
                  pl.BlockSpec((None, D, IN_SHARD), lambda i, k: (k, 0, 0)), row, row, vec] + [ANY] * (n + g),
        out_specs=[row, vec] + [ANY] * (n + g),
        out_shape=[jax.ShapeDtypeStruct((s, D), F32), jax.ShapeDtypeStruct((1, D), F32)]
        + _scatter_shapes(scattering) + ([jax.ShapeDtypeStruct(gathering.shape, gathering.dtype)] if g else []),
        input_output_aliases={5 + n: 2 + n} if g else {},
        scratch_shapes=[pltpu.VMEM((tm, D), F32)] + (_scatter_sems(n) if n else [])
        + ([pltpu.SemaphoreType.DMA((7,)), pltpu.SemaphoreType.DMA((7,))] if g else []),
        compiler_params=_params("arbitrary", "arbitrary", communicates=bool(n + g)),
    )(dz, wg, x, dx1, g_pre, *scattering, *([gathering] if g else []))
    return res[0], res[1], res[2:2 + n], (res[2 + n] if g else None)


def _adamw_math(w, g, m, v):
    m = ADAM_B1 * m + (1.0 - ADAM_B1) * g
    v = ADAM_B2 * v + (1.0 - ADAM_B2) * (g * g)
    m_hat = m / (1.0 - ADAM_B1 ** ADAM_STEP)
    v_hat = v / (1.0 - ADAM_B2 ** ADAM_STEP)
    delta = -ADAM_LR * (m_hat / (jnp.sqrt(v_hat) + ADAM_EPS) + ADAM_WD * w)
    return delta, m, v


def adamw(name, w, g, m, v, tr):
    r, c = w.shape

    def body(w_ref, g_ref, m_ref, v_ref, go_ref, d_ref, nm_ref, nv_ref):
        g = g_ref[...]
        go_ref[...] = g
        d_ref[...], nm_ref[...], nv_ref[...] = _adamw_math(w_ref[...], g, m_ref[...], v_ref[...])

    out = jax.ShapeDtypeStruct((r, c), F32)
    return pl.pallas_call(
        body, name=name, grid=(r // tr,), in_specs=[_rows(tr, c)] * 4, out_specs=[_rows(tr, c)] * 4,
        out_shape=[out] * 4, compiler_params=_params("parallel"),
    )(w, g, m, v)


def _allgather8_phases(buf, send_sems, recv_sems):
    x, y, c, chips = _place()
    me = 2 * x + y
    sibling = (x, y, 1 - c)
    rows = buf.shape[1] // 2

    def part(chip, core):
        return buf.at[chip, pl.ds(core * rows, rows)]

    def copy(k, block, to):
        return pltpu.make_async_remote_copy(src_ref=block, dst_ref=block, send_sem=send_sems.at[k],
                                            recv_sem=recv_sems.at[k], device_id=to, device_id_type=MESH)

    def chip_of(j):
        return 2 * chips[j][0] + chips[j][1]

    def send():
        copy(0, part(me, c), sibling).start()
        for j in range(3):
            copy(1 + j, part(me, c), (chips[j][0], chips[j][1], c)).start()

    def pass_on():
        for j in range(3):
            copy(1 + j, part(chip_of(j), c), (chips[j][0], chips[j][1], c)).wait_recv()
            copy(4 + j, part(chip_of(j), c), sibling).start()

    def finish():
        copy(0, part(me, 1 - c), sibling).wait_recv()
        for j in range(3):
            copy(4 + j, part(chip_of(j), 1 - c), sibling).wait_recv()
        copy(0, part(me, c), sibling).wait_send()
        for j in range(3):
            copy(1 + j, part(me, c), (chips[j][0], chips[j][1], c)).wait_send()
            copy(4 + j, part(chip_of(j), c), sibling).wait_send()

    return send, pass_on, finish


def add_halves(name, g, recv, c_idx, tr):
    n, h, c = recv.shape

    def body(c_ref, g_ref, r_ref, o_ref):
        o_ref[...] = (g_ref[...] + r_ref[...]).astype(BF16)

    nb = h // tr
    return pl.pallas_call(
        body, name=name,
        grid_spec=pltpu.PrefetchScalarGridSpec(
            num_scalar_prefetch=1, grid=(n, nb),
            in_specs=[pl.BlockSpec((None, tr, c), lambda k, i, c_ref: (k, c_ref[0] * nb + i, 0)),
                      pl.BlockSpec((None, tr, c), lambda k, i, c_ref: (k, i, 0))],
            out_specs=pl.BlockSpec((None, tr, c), lambda k, i, c_ref: (k, i, 0))),
        out_shape=jax.ShapeDtypeStruct((n, h, c), BF16), compiler_params=_params("parallel", "parallel"),
    )(c_idx, g, recv)


def sum_chips(name, parts, recv, where, tr):
    n, h, c = recv.shape
    nb = h // tr

    def body(w_ref, p_ref, r_ref, o_ref):
        acc = p_ref[...].astype(F32)
        for k in range(n):
            acc = acc + r_ref[k].astype(F32)
        o_ref[...] = acc

    return pl.pallas_call(
        body, name=name,
        grid_spec=pltpu.PrefetchScalarGridSpec(
            num_scalar_prefetch=1, grid=(nb,),
            in_specs=[pl.BlockSpec((None, tr, c), lambda i, w_ref: (w_ref[0], i, 0)),
                      pl.BlockSpec((n, tr, c), lambda i, w_ref: (0, i, 0))],
            out_specs=pl.BlockSpec((tr, c), lambda i, w_ref: (w_ref[1] * nb + i, 0))),
        out_shape=jax.ShapeDtypeStruct((2 * h, c), F32), compiler_params=_params("parallel"),
    )(where, parts, recv)


def place_shard(name, shard, where, dtype, tr):
    r, c = shard.shape

    def body(w_ref, s_ref, o_ref):
        o_ref[...] = s_ref[...].astype(dtype)

    return pl.pallas_call(
        body, name=name,
        grid_spec=pltpu.PrefetchScalarGridSpec(
            num_scalar_prefetch=1, grid=(r // tr,),
            in_specs=[pl.BlockSpec((tr, c), lambda i, w_ref: (i, 0))],
            out_specs=pl.BlockSpec((None, tr, c), lambda i, w_ref: (w_ref[0], i, 0))),
        out_shape=jax.ShapeDtypeStruct((N_CHIPS, r, c), dtype), compiler_params=_params("parallel"),
    )(where, shard)


ANY = pl.BlockSpec(memory_space=pl.ANY)


def _place():
    x, y, c = lax.axis_index("x"), lax.axis_index("y"), lax.axis_index("c")
    chips = [(1 - x, y), (x, 1 - y), (1 - x, 1 - y)]
    return x, y, c, chips


def gather_shards(arrays):
    n = len(arrays)

    def body(*refs):
        send, pass_on, finish = _gather_phases(refs[n:2 * n], *refs[2 * n:], _spans(arrays))
        send()
        pass_on()
        finish()

    return pl.pallas_call(
        body, name="gather_shards", in_specs=[ANY] * n, out_specs=[ANY] * n,
        out_shape=[jax.ShapeDtypeStruct(a.shape, a.dtype) for a in _arrays(arrays)],
        input_output_aliases={w: w for w in range(n)}, scratch_shapes=_gather_sems(n),
        compiler_params=pltpu.CompilerParams(has_side_effects=True),
    )(*_arrays(arrays))


def _gather_sems(n):
    return [pltpu.SemaphoreType.DMA((6 * n,)), pltpu.SemaphoreType.DMA((6 * n,))]


class Span(typing.NamedTuple):
    array: jax.Array
    lo: int
    hi: int
    ways: tuple = (0, 1, 2)


def _arrays(gathering):
    return [g.array if isinstance(g, Span) else g for g in gathering]


def _spans(gathering):
    return [(g.lo, g.hi, g.ways) if isinstance(g, Span) else (0, g.shape[1], (0, 1, 2)) for g in gathering]


def _gather_phases(out, send_sems, recv_sems, spans):
    n = len(out)
    if not any(ways for _, _, ways in spans):
        return (lambda: None,) * 3
    x, y, c, chips = _place()
    me = 2 * x + y
    sibling = (x, y, 1 - c)

    def half(w, chip, core):
        lo, hi, _ = spans[w]
        h = (hi - lo) // 2
        return out[w].at[chip, pl.ds(lo + core * h, h)]

    def copy(k, block, to):
        return pltpu.make_async_remote_copy(src_ref=block, dst_ref=block, send_sem=send_sems.at[k],
                                            recv_sem=recv_sems.at[k], device_id=to, device_id_type=MESH)

    def over_ici(w, j, chip):
        return copy(3 * w + j, half(w, chip, c), (chips[j][0], chips[j][1], c))

    def over_d2d(w, j, core):
        return copy(3 * n + 3 * w + j, half(w, 2 * chips[j][0] + chips[j][1], core), sibling)

    pairs = [(w, j) for w in range(n) for j in spans[w][2]]

    def send():
        for w, j in pairs:
            over_ici(w, j, me).start()

    def pass_on():
        for w, j in pairs:
            over_ici(w, j, 2 * chips[j][0] + chips[j][1]).wait_recv()
            over_d2d(w, j, c).start()

    def finish():
        for w, j in pairs:
            over_d2d(w, j, 1 - c).wait_recv()
        for w, j in pairs:
            over_ici(w, j, me).wait_send()
            over_d2d(w, j, c).wait_send()

    return send, pass_on, finish


def _relay_sems():
    return [pltpu.SemaphoreType.DMA((4,)), pltpu.SemaphoreType.DMA((4,))]


def _relay_phases(out, send_sems, recv_sems):
    x, y, c, chips = _place()
    sibling = (x, y, 1 - c)
    rows = out.shape[1]
    quarter = rows // 4
    far = 2 * chips[2][0] + chips[2][1]

    def piece(chip, way, core):
        return out.at[chip, pl.ds(way * (rows // 2) + core * quarter, quarter)]

    def copy(k, block, to):
        return pltpu.make_async_remote_copy(src_ref=block, dst_ref=block, send_sem=send_sems.at[k],
                                            recv_sem=recv_sems.at[k], device_id=to, device_id_type=MESH)

    def over_ici(way, chip):
        return copy(way, piece(chip, way, c), (chips[way][0], chips[way][1], c))

    def over_d2d(way, core):
        return copy(2 + way, piece(far, way, core), sibling)

    def send():
        for way in range(2):
            other = chips[1 - way]
            over_ici(way, 2 * other[0] + other[1]).start()

    def pass_on():
        for way in range(2):
            over_ici(way, far).wait_recv()
            over_d2d(way, c).start()

    def finish():
        for way in range(2):
            over_d2d(way, 1 - c).wait_recv()
        for way in range(2):
            other = chips[1 - way]
            over_ici(way, 2 * other[0] + other[1]).wait_send()
            over_d2d(way, c).wait_send()

    return send, pass_on, finish


def swap_halves(name, grads):
    n = len(grads)

    def body(*refs):
        send, finish = _swap_phases(refs[:n], refs[n:2 * n], *refs[2 * n:])
        send()
        finish()

    return pl.pallas_call(
        body, name=name, in_specs=[ANY] * n, out_specs=[ANY] * n, out_shape=_swap_shapes(grads),
        scratch_shapes=_swap_sems(n), compiler_params=pltpu.CompilerParams(has_side_effects=True),
    )(*grads)


def _swap_shapes(grads):
    return [jax.ShapeDtypeStruct((a.shape[0], a.shape[1] // 2, a.shape[2]), a.dtype) for a in grads]


def _swap_sems(n):
    return [pltpu.SemaphoreType.DMA((n,)), pltpu.SemaphoreType.DMA((n,))]


def _swap_phases(g, out, send_sems, recv_sems):
    x, y, c, _ = _place()

    def copies():
        return [pltpu.make_async_remote_copy(
            src_ref=g[w].at[:, pl.ds((1 - c) * (g[w].shape[1] // 2), g[w].shape[1] // 2)], dst_ref=out[w],
            send_sem=send_sems.at[w], recv_sem=recv_sems.at[w], device_id=(x, y, 1 - c), device_id_type=MESH)
            for w in range(len(g))]

    def send():
        for cp in copies():
            cp.start()

    def finish():
        for cp in copies():
            cp.wait()

    return send, finish


def _send_phases(g, out, send_sems, recv_sems):
    x, y, c, _ = _place()

    def copies():
        return [pltpu.make_async_remote_copy(
            src_ref=g[w], dst_ref=out[w], send_sem=send_sems.at[w], recv_sem=recv_sems.at[w],
            device_id=(x, y, 1 - c), device_id_type=MESH) for w in range(len(g))]

    def send():
        for cp in copies():
            cp.start()

    def finish():
        for cp in copies():
            cp.wait()

    return send, finish


def dw_in_half(name, h, dz, which, sending):
    s = h.shape[0]
    hh, tb = D // 2, IN_SHARD // 2
    n = len(sending)
    steps = IN_COLS // tb

    def body(w_ref, *refs):
        a_ref, b_ref, o_ref = refs[0], refs[1], refs[2 + n]
        j = pl.program_id(0)
        if n:
            send, finish = _send_phases(refs[2:2 + n], refs[3 + n:3 + 2 * n], *refs[3 + 2 * n:])
            pl.when(j == 0)(send)
        o_ref[...] = _dot(a_ref[...], b_ref[...], TN)
        if n:
            pl.when(j == steps - 1)(finish)

    out = pl.pallas_call(
        body, name=name,
        grid_spec=pltpu.PrefetchScalarGridSpec(
            num_scalar_prefetch=1, grid=(steps,),
            in_specs=[pl.BlockSpec((s, hh), lambda j, w: (0, w[0])), pl.BlockSpec((s, tb), lambda j, w: (0, j))]
            + [ANY] * n,
            out_specs=[pl.BlockSpec((None, hh, tb), lambda j, w: (j // 2, 0, j % 2))] + [ANY] * n,
            scratch_shapes=_swap_sems(n) if n else []),
        out_shape=[jax.ShapeDtypeStruct((N_CHIPS, hh, IN_SHARD), F32)]
        + [jax.ShapeDtypeStruct(a.shape, a.dtype) for a in sending],
        compiler_params=_params("arbitrary", communicates=bool(n)),
    )(which, h, dz, *sending)
    return out[0], out[1:]


def scatter_chips(parts):
    n = len(parts)

    def body(*refs):
        send, finish = _scatter_phases(refs[:n], refs[n:2 * n], *refs[2 * n:])
        send()
        finish()

    return pl.pallas_call(
        body, name="scatter_chips", in_specs=[ANY] * n, out_specs=[ANY] * n,
        out_shape=_scatter_shapes(parts), scratch_shapes=_scatter_sems(n),
        compiler_params=pltpu.CompilerParams(has_side_effects=True),
    )(*parts)


def _scatter_shapes(parts):
    return [jax.ShapeDtypeStruct((3,) + a.shape[1:], a.dtype) for a in parts]


def _scatter_sems(n):
    return [pltpu.SemaphoreType.DMA((3 * n,)), pltpu.SemaphoreType.DMA((3 * n,))]


def _scatter_phases(p, out, send_sems, recv_sems):
    x, y, c, chips = _place()

    def copies():
        return [pltpu.make_async_remote_copy(
            src_ref=p[w].at[2 * px + py], dst_ref=out[w].at[j], send_sem=send_sems.at[3 * w + j],
            recv_sem=recv_sems.at[3 * w + j], device_id=(px, py, c), device_id_type=MESH)
            for w in range(len(p)) for j, (px, py) in enumerate(chips)]

    def send():
        for cp in copies():
            cp.start()

    def finish():
        for cp in copies():
            cp.wait()

    return send, finish


def join_halves(arrays, gathering):
    n = len(arrays)

    def body(*refs):
        out = refs[n + 1:2 * n + 1]
        send_sems, recv_sems = refs[2 * n + 2:2 * n + 4]
        send8, pass_on8, finish8 = _allgather8_phases(refs[2 * n + 1], *refs[2 * n + 4:])
        x, y, c, _ = _place()

        def copy(w, core):
            h = out[w].shape[0] // 2
            rows = out[w].at[pl.ds(core * h, h)]
            return pltpu.make_async_remote_copy(
                src_ref=rows, dst_ref=rows, send_sem=send_sems.at[w], recv_sem=recv_sems.at[w],
                device_id=(x, y, 1 - c), device_id_type=MESH)

        send8()
        for w in range(n):
            copy(w, c).start()
        pass_on8()
        for w in range(n):
            copy(w, 1 - c).wait_recv()
        finish8()
        for w in range(n):
            copy(w, c).wait_send()

    res = pl.pallas_call(
        body, name="join_halves", in_specs=[ANY] * (n + 1), out_specs=[ANY] * (n + 1),
        out_shape=[jax.ShapeDtypeStruct(a.shape, a.dtype) for a in list(arrays) + [gathering]],
        input_output_aliases={w: w for w in range(n + 1)},
        scratch_shapes=[pltpu.SemaphoreType.DMA((n,)), pltpu.SemaphoreType.DMA((n,)),
                        pltpu.SemaphoreType.DMA((7,)), pltpu.SemaphoreType.DMA((7,))],
        compiler_params=pltpu.CompilerParams(has_side_effects=True),
    )(*arrays, gathering)
    return res[:n], res[n]


def allreduce_small(packed):
    r, c = packed.shape
    n_dev = 8

    def body(x_ref, all_ref, sum_ref, send_sems, recv_sems, local_sem):
        x, y, cc, chips = _place()
        me, sibling = (x, y, cc), (x, y, 1 - cc)

        def rows(px, py, pc):
            return all_ref.at[4 * px + 2 * py + pc]

        def copy(k, block, to, src=None):
            return pltpu.make_async_remote_copy(
                src_ref=rows(*block) if src is None else src, dst_ref=rows(*block), send_sem=send_sems.at[k],
                recv_sem=recv_sems.at[k], device_id=to, device_id_type=MESH)

        mine = pltpu.make_async_copy(x_ref, rows(*me), local_sem)
        mine.start()
        first = [copy(0, me, sibling, src=x_ref)]
        first += [copy(1 + j, me, (*chip, cc), src=x_ref) for j, chip in enumerate(chips)]
        for cp in first:
            cp.start()
        passed = [copy(4 + j, (*chip, cc), sibling) for j, chip in enumerate(chips)]
        for j, chip in enumerate(chips):
            copy(1 + j, (*chip, cc), me).wait_recv()
            passed[j].start()
        copy(0, sibling, me).wait_recv()
        for j, chip in enumerate(chips):
            copy(4 + j, (*chip, 1 - cc), me).wait_recv()
        for cp in first + passed:
            cp.wait_send()
        mine.wait()
        acc = all_ref[0]
        for k in range(1, n_dev):
            acc = acc + all_ref[k]
        sum_ref[...] = acc

    vm = pl.BlockSpec(memory_space=pltpu.VMEM)
    return pl.pallas_call(
        body, name="allreduce_small", in_specs=[vm], out_specs=[vm, vm],
        out_shape=[jax.ShapeDtypeStruct((n_dev, r, c), F32), jax.ShapeDtypeStruct((r, c), F32)],
        scratch_shapes=[pltpu.SemaphoreType.DMA((7,)), pltpu.SemaphoreType.DMA((7,)), pltpu.SemaphoreType.DMA],
        compiler_params=pltpu.CompilerParams(has_side_effects=True, vmem_limit_bytes=VMEM_LIMIT),
    )(packed)[1]


def local_step(x, target, vecs, w_s, bs_t, bg, wg_in, late, core=None, order=None, where=None):
    on_mesh = core is not None

    def add(names, grads, recv):
        return [add_halves("add_" + n, g, r, core, min(r.shape[1], 256)) for n, g, r in zip(names, grads, recv)]

    g_pre, ln_g, ln_b, g_post, g_fpre, g_fpost = vecs
    s = x.shape[0]
    if order is None:
        order = jnp.arange(N_CHIPS, dtype=jnp.int32)
    logc = _attn_tables(s)
    ka, kb = _alibi_tables(s)

    h = norm_pre(x, g_pre)
    if not on_mesh:
        wg_a, wg_b, wg_out, wg_ff1, wg_ff2 = late
    if on_mesh:
        cut = D // 4
        z, wg_in, (wg_a, wg_b, wg_out, wg_ff1, wg_ff2) = mm_in(h, wg_in, order, True, late)
        ya, (wg_b, wg_ff2) = gating_fwd(z, ln_g, ln_b, w_s, bs_t, [wg_b, Span(wg_ff2, 0, cut)])
        yb, lse, (wg_a, wg_ff1, wg_ff2, bg) = attn_fwd(
            z, logc, ka, kb, [wg_a, wg_ff1, Span(wg_ff2, cut, 3 * cut), bg])
        bg = jnp.transpose(bg[:, :2, :], (1, 0, 2)).reshape(2, D)
    else:
        z, _, _ = mm_in(h, wg_in, order, False)
        ya, _ = gating_fwd(z, ln_g, ln_b, w_s, bs_t, [])
        yb, lse, _ = attn_fwd(z, logc, ka, kb, [])
    merged, pa, pb, got = proj_merge(ya, yb, wg_a.reshape(D, D), wg_b.reshape(D, D), z, bg, [wg_out] if on_mesh else [])
    wg_out = got[0] if on_mesh else wg_out
    w_out = wg_out.reshape(D, D)
    o, x1, h2, got = out_norm(merged, w_out, x, g_post, g_fpre, [Span(wg_ff2, 3 * D // 4, D)] if on_mesh else [])
    a, rl, _ = mm_ff1(h2, wg_ff1, [])
    w_ff2 = (got[0] if on_mesh else wg_ff2).reshape(D_FF, D)
    dy, df, d_gfpost, loss = ff2_loss(rl, w_ff2, x1, target, g_fpost)

    half_cols = pl.BlockSpec((D, D // 2), lambda i, j: (0, j))
    d_wff2 = mm_tn("dw_ff2", rl, df, D // 2, D, (D_FF, D), pl.BlockSpec((D // 2, D), lambda i, j: (i, 0)))
    da = ff2_bwd(df, w_ff2, a)
    d_wff1 = mm_tn("dw_ff1", h2, da, D, D // 2, (N_CHIPS, D, D),
                   pl.BlockSpec((None, D, D // 2), lambda i, j: (j // 2, 0, j % 2)))
    d_ff = [d_wff1, d_wff2.reshape(N_CHIPS, D, D)]
    dx1, do, d_gfpre, d_gpost, recv_ff = ff1_bwd_norms(da, wg_ff1, x1, o, dy, g_fpre, g_post, d_ff if on_mesh else [])
    d_wout = mm_tn("dw_out", merged, do, D, D // 2, (D, D), half_cols)
    dpa, dpb, dga, dgb, d_bg = out_bwd_gates(do, w_out, pa, pb, z, bg)
    d_wa = mm_tn("dw_a", ya, dpa, D, D // 2, (D, D), half_cols)
    d_wb = mm_tn("dw_b", yb, dpb, D, D // 2, (D, D), half_cols)
    dya = mm_nt("dy_a", dpa, wg_a.reshape(D, D))
    dyb = mm_nt("dy_b", dpb, wg_b.reshape(D, D))
    d_proj = [d_wa.reshape(N_CHIPS, D // N_CHIPS, D), d_wb.reshape(N_CHIPS, D // N_CHIPS, D),
              d_wout.reshape(N_CHIPS, D // N_CHIPS, D)]
    du, dv, d_ws, d_bs, d_lng, d_lnb, recv_proj = gating_bwd(z, dya, ln_g, ln_b, w_s, bs_t, d_proj if on_mesh else [])
    early = d_proj + d_ff
    parts_early = add(BIG[1:], early, list(recv_proj) + list(recv_ff)) if on_mesh else []
    small = dict(b_gate=d_bg, ln_v_g=d_lng, ln_v_b=d_lnb, w_s=d_ws, b_s=d_bs[:, 0, :],
                 norm_mix_post=d_gpost, norm_ffn_pre=d_gfpre, norm_ffn_post=d_gfpost)
    packed = pack_small(dict(small, norm_mix_pre=jnp.zeros((1, D), F32)), loss, where) if on_mesh else None
    dq, dk, dvb, got_early, packed = attn_bwd(z, yb, dyb, lse, logc, ka, kb, parts_early, packed)
    dz = jnp.concatenate([du, dv, dq, dk, dvb, dga, dgb], axis=1)
    if on_mesh:
        for_sibling, _ = dw_in_half("dw_in_sibling", h, dz, 1 - core, [])
        mine, from_sibling = dw_in_half("dw_in_mine", h, dz, core, [for_sibling])
        d_win = None
        parts_late = [add_halves("add_w_in", mine, from_sibling[0], jnp.zeros((1,), jnp.int32), 256)]
    else:
        half = IN_SHARD // 2
        d_win = mm_tn("dw_in", h, dz, D, half, (N_CHIPS, D, IN_SHARD),
                      pl.BlockSpec((None, D, half), lambda i, j: (j // 2, 0, j % 2)))
        parts_late = []
    dx, d_gpre, got_late, _ = in_bwd_norm(dz, wg_in, x, dx1, g_pre, parts_late)
    small["norm_mix_pre"] = d_gpre
    return (loss[0, 0], dx, [d_win] + early, small, parts_late + parts_early, list(got_late) + list(got_early),
            packed)


BIG = ("w_in", "w_a_proj", "w_b_proj", "w_out", "w_ff1", "w_ff2")
SMALL = ("norm_mix_pre", "ln_v_g", "ln_v_b", "b_s", "norm_mix_post", "norm_ffn_pre", "norm_ffn_post", "w_s", "b_gate")
ORDER = ("norm_mix_pre", "w_in", "b_gate", "ln_v_g", "ln_v_b", "w_s", "b_s", "w_a_proj", "w_b_proj", "w_out",
         "norm_mix_post", "norm_ffn_pre", "w_ff1", "w_ff2", "norm_ffn_post")
VEC_ROWS = D // 128
WS_ROW = 7 * VEC_ROWS
BG_ROW = WS_ROW + GROUPS * CHUNK
LOSS_ROW = BG_ROW + 2 * VEC_ROWS
PACK_ROWS = LOSS_ROW + 8


def pack_small(small, loss, where):
    vectors = [small[n] for n in SMALL[:7]]
    operands = vectors + [small["w_s"], small["b_gate"], loss]

    def body(where_ref, *refs):
        out = refs[-1]
        ws_ref, bg_ref, loss_ref = refs[7:10]
        for i, n in enumerate(SMALL[:7]):
            if n == "b_s":
                out[i * VEC_ROWS:(i + 1) * VEC_ROWS, :] = refs[i][...]
            else:
                for j in range(VEC_ROWS):
                    out[i * VEC_ROWS + j:i * VEC_ROWS + j + 1, :] = refs[i][:, j * 128:(j + 1) * 128]
        for g in range(GROUPS):
            out[WS_ROW + g * CHUNK:WS_ROW + (g + 1) * CHUNK, :] = ws_ref[g]
        for r in range(2):
            for j in range(VEC_ROWS):
                row = BG_ROW + r * VEC_ROWS + j
                out[row:row + 1, :] = bg_ref[r:r + 1, j * 128:(j + 1) * 128]
        lane = lax.broadcasted_iota(jnp.int32, (8, 128), 1)
        sub = lax.broadcasted_iota(jnp.int32, (8, 128), 0)
        out[LOSS_ROW:LOSS_ROW + 8, :] = jnp.where((lane == 0) & (sub == 0), loss_ref[...], 0.0)

    return pl.pallas_call(
        body, name="pack_small",
        grid_spec=pltpu.PrefetchScalarGridSpec(
            num_scalar_prefetch=1, grid=(1,), in_specs=[_full(a.shape) for a in operands],
            out_specs=pl.BlockSpec((None, PACK_ROWS, 128), lambda i, w: (w[0], w[1], 0))),
        out_shape=jax.ShapeDtypeStruct((N_CHIPS, 2 * PACK_ROWS, 128), F32), compiler_params=_params("arbitrary"),
    )(where, *operands)


def pack_vector(vec, where):
    def body(where_ref, v_ref, out):
        for j in range(VEC_ROWS):
            out[j:j + 1, :] = v_ref[:, j * 128:(j + 1) * 128]

    return pl.pallas_call(
        body, name="pack_vector",
        grid_spec=pltpu.PrefetchScalarGridSpec(
            num_scalar_prefetch=1, grid=(1,), in_specs=[_full(vec.shape)],
            out_specs=pl.BlockSpec((None, VEC_ROWS, 128), lambda i, w: (w[0], w[1], 0))),
        out_shape=jax.ShapeDtypeStruct((N_CHIPS, 2 * VEC_ROWS, 128), F32), compiler_params=_params("arbitrary"),
    )(where, vec)


def adamw_small(gathered, first, chip, w, m, v):
    shapes = {n: (1, D) for n in SMALL}
    shapes.update(b_s=(GROUPS, CHUNK), w_s=(GROUPS * CHUNK, CHUNK), b_gate=(2, D // N_CHIPS))
    flat = lambda t: [t[n].reshape(shapes[n]) for n in SMALL]
    per = D // N_CHIPS // 128

    def body(chip_ref, all_ref, first_ref, *refs):
        params, outs = refs[:27], refs[27:]
        sub = lax.broadcasted_iota(jnp.int32, (VEC_ROWS, 128), 0)
        sum_ref = outs[36]
        total = all_ref[0, 0:PACK_ROWS, :]
        head = first_ref[0, 0:VEC_ROWS, :]
        for k in range(1, 2 * N_CHIPS):
            total = total + all_ref[k // 2, (k % 2) * PACK_ROWS:(k % 2 + 1) * PACK_ROWS, :]
            head = head + first_ref[k // 2, (k % 2) * VEC_ROWS:(k % 2 + 1) * VEC_ROWS, :]
        sum_ref[...] = total
        sum_ref[0:VEC_ROWS, :] = head

        def gate_row(r):
            rows = sum_ref[BG_ROW + r * VEC_ROWS:BG_ROW + (r + 1) * VEC_ROWS, :]
            return jnp.concatenate([jnp.sum(jnp.where(sub == per * chip_ref[0] + j, rows, 0.0), axis=0, keepdims=True)
                                    for j in range(per)], axis=1)

        for i, n in enumerate(SMALL):
            if n == "b_s":
                g = sum_ref[i * VEC_ROWS:(i + 1) * VEC_ROWS, :]
            elif n == "w_s":
                g = sum_ref[WS_ROW:BG_ROW, :]
            elif n == "b_gate":
                g = jnp.concatenate([gate_row(0), gate_row(1)], axis=0)
            else:
                g = jnp.concatenate([sum_ref[i * VEC_ROWS + j:i * VEC_ROWS + j + 1, :] for j in range(VEC_ROWS)],
                                    axis=1)
            delta, nm, nv = _adamw_math(params[i][...], g, params[9 + i][...], params[18 + i][...])
            outs[4 * i][...], outs[4 * i + 1][...], outs[4 * i + 2][...], outs[4 * i + 3][...] = g, delta, nm, nv

    vm = pl.BlockSpec(memory_space=pltpu.VMEM)
    res = pl.pallas_call(
        body, name="adamw_small",
        in_specs=[pl.BlockSpec(memory_space=pltpu.SMEM)] + [vm] * 29, out_specs=[vm] * 37,
        out_shape=[jax.ShapeDtypeStruct(shapes[n], F32) for n in SMALL for _ in range(4)]
        + [jax.ShapeDtypeStruct((PACK_ROWS, 128), F32)],
        compiler_params=_params(),
    )(chip, gathered, first, *flat(w), *flat(m), *flat(v))
    new = {n: tuple(r.reshape(w[n].shape) for r in res[4 * i:4 * i + 4]) for i, n in enumerate(SMALL)}
    return new, res[36][LOSS_ROW, 0]


def kernel(x, norm_mix_pre, w_in, b_gate, ln_v_g, ln_v_b, w_s, b_s, w_a_proj, w_b_proj, w_out, norm_mix_post, norm_ffn_pre, w_ff1, w_ff2, norm_ffn_post, loss_target, m_norm_mix_pre, m_w_in, m_b_gate, m_ln_v_g, m_ln_v_b, m_w_s, m_b_s, m_w_a_proj, m_w_b_proj, m_w_out, m_norm_mix_post, m_norm_ffn_pre, m_w_ff1, m_w_ff2, m_norm_ffn_post, v_norm_mix_pre, v_w_in, v_b_gate, v_ln_v_g, v_ln_v_b, v_w_s, v_b_s, v_w_a_proj, v_w_b_proj, v_w_out, v_norm_mix_post, v_norm_ffn_pre, v_w_ff1, v_w_ff2, v_norm_ffn_post):
    w = dict(norm_mix_pre=norm_mix_pre, w_in=w_in, b_gate=b_gate, ln_v_g=ln_v_g, ln_v_b=ln_v_b, w_s=w_s, b_s=b_s,
             w_a_proj=w_a_proj, w_b_proj=w_b_proj, w_out=w_out, norm_mix_post=norm_mix_post,
             norm_ffn_pre=norm_ffn_pre, w_ff1=w_ff1, w_ff2=w_ff2, norm_ffn_post=norm_ffn_post)
    m = dict(norm_mix_pre=m_norm_mix_pre, w_in=m_w_in, b_gate=m_b_gate, ln_v_g=m_ln_v_g, ln_v_b=m_ln_v_b, w_s=m_w_s,
             b_s=m_b_s, w_a_proj=m_w_a_proj, w_b_proj=m_w_b_proj, w_out=m_w_out, norm_mix_post=m_norm_mix_post,
             norm_ffn_pre=m_norm_ffn_pre, w_ff1=m_w_ff1, w_ff2=m_w_ff2, norm_ffn_post=m_norm_ffn_post)
    v = dict(norm_mix_pre=v_norm_mix_pre, w_in=v_w_in, b_gate=v_b_gate, ln_v_g=v_ln_v_g, ln_v_b=v_ln_v_b, w_s=v_w_s,
             b_s=v_b_s, w_a_proj=v_w_a_proj, w_b_proj=v_w_b_proj, w_out=v_w_out, norm_mix_post=v_norm_mix_post,
             norm_ffn_pre=v_norm_ffn_pre, w_ff1=v_w_ff1, w_ff2=v_w_ff2, norm_ffn_post=v_norm_ffn_post)
    chip = 2 * lax.axis_index("x") + lax.axis_index("y")
    core = lax.axis_index("c")

    where = jnp.stack([chip, core]).astype(jnp.int32)
    wg_in = place_shard("place_w_in", w_in[0], where, BF16, 256)
    bg_all = place_shard("place_b_gate", jnp.pad(b_gate[0], ((0, 14), (0, 0))), where, F32, 16)
    vecs = (norm_mix_pre, ln_v_g, ln_v_b, norm_mix_post, norm_ffn_pre, norm_ffn_post)
    loss, dx, _, small, parts, got, packed = local_step(
        x[0], loss_target[0], vecs, w_s[0], b_s[0].T, bg_all, wg_in, [w[n][0] for n in BIG[1:]],
        core=jnp.reshape(core, (1,)).astype(jnp.int32),
        order=jnp.stack([chip, chip ^ 2, chip ^ 1, chip ^ 3]).astype(jnp.int32), where=where)

    halves = [sum_chips("sum_" + n, p, r, where, min(p.shape[1], 256)) for n, p, r in zip(BIG, parts, got)]
    joined, first = join_halves(halves, pack_vector(small["norm_mix_pre"], where))
    grads = dict(zip(BIG, joined))
    new = {}
    for n in BIG:
        shape = w[n].shape
        res = adamw("adamw_" + n, w[n][0], grads[n], m[n][0], v[n][0], min(shape[1], 256))
        new[n] = tuple(r.reshape(shape) for r in res)
    small_new, loss = adamw_small(packed, first, jnp.reshape(chip, (1,)).astype(jnp.int32), w, m, v)
    new.update(small_new)

    outs = [loss, dx[None]]
    for i in range(4):
        outs += [new[n][i] for n in ORDER]
    return tuple(outs)
```

```python
import functools
import math
import typing

import numpy as np
import jax
import jax.numpy as jnp
from jax import lax
from jax.experimental import pallas as pl
from jax.experimental.pallas import tpu as pltpu

F32 = jnp.float32
BF16 = jnp.bfloat16
MESH = pl.DeviceIdType.MESH

D = 1024
EPS = 1e-6
CHUNK = 128
GROUPS = 8
HEADS = 16
HEAD_DIM = 64
ATT_T = 256
ATT_GROUP = 8
ATT_BWD_GROUP = 2
N_CHIPS = 4
D_FF = 4 * D
IN_COLS = 7 * D
IN_SHARD = IN_COLS // N_CHIPS
MASKED = -1e30
VMEM_LIMIT = 56 * 2 ** 20

ADAM_LR, ADAM_B1, ADAM_B2, ADAM_EPS, ADAM_WD, ADAM_STEP = 0.001, 0.9, 0.999, 1e-08, 0.01, 10

NN = (((1,), (0,)), ((), ()))
NT = (((1,), (1,)), ((), ()))
TN = (((0,), (0,)), ((), ()))


def _dot(a, b, dims=NN):
    return lax.dot_general(a, b, dims, preferred_element_type=F32)


def _params(*sem, communicates=False):
    return pltpu.CompilerParams(dimension_semantics=sem or None, vmem_limit_bytes=VMEM_LIMIT,
                                has_side_effects=communicates)


def _rows(tr, c, col=0):
    return pl.BlockSpec((tr, c), lambda i: (i, col))


def _full(shape):
    n = len(shape)
    return pl.BlockSpec(shape, lambda *_: (0,) * n)


def _gelu(x):
    k = math.sqrt(2.0 / math.pi)
    return 0.5 * x * (1.0 + jnp.tanh(k * (x + 0.044715 * x * x * x)))


def _gelu_and_grad(x):
    k = math.sqrt(2.0 / math.pi)
    t = jnp.tanh(k * (x + 0.044715 * x * x * x))
    g = 0.5 * x * (1.0 + t)
    dg = 0.5 * (1.0 + t) + 0.5 * x * (1.0 - t * t) * (k * (1.0 + 3.0 * 0.044715 * x * x))
    return g, dg


def _sigmoid(x):
    return 1.0 / (1.0 + jnp.exp(-x))


def _rms(x):
    r = lax.rsqrt(jnp.mean(x * x, axis=-1, keepdims=True) + EPS)
    return x * r, r


def _rms_bwd(dn, xhat, r):
    return r * (dn - xhat * jnp.mean(dn * xhat, axis=-1, keepdims=True))


def norm_pre(x, g):
    s = x.shape[0]
    tr = 512

    def body(x_ref, g_ref, h_ref):
        xhat, _ = _rms(x_ref[...])
        h_ref[...] = (xhat * g_ref[...]).astype(BF16)

    return pl.pallas_call(
        body, name="norm_pre", grid=(s // tr,),
        in_specs=[_rows(tr, D), _full((1, D))], out_specs=_rows(tr, D),
        out_shape=jax.ShapeDtypeStruct((s, D), BF16), compiler_params=_params("parallel"),
    )(x, g)


def mm_in(h, wg, order, staged, casting=()):
    s = h.shape[0]
    tm, tn = 1024, IN_SHARD // 2
    per = IN_SHARD // tn
    m = len(casting)
    nj, ni = N_CHIPS * per, s // tm
    cast_steps = per * ni

    def body(order_ref, *refs):
        a_ref = refs[0]
        cast_in = refs[2:2 + m]
        o_ref, held = refs[2 + m], refs[3 + m]
        cast_out = refs[4 + m:4 + 2 * m]
        tile, tile_sem = refs[4 + 2 * m:6 + 2 * m]
        sems = refs[6 + 2 * m:]
        j, i = pl.program_id(0), pl.program_id(1)

        @pl.when(j * ni + i < cast_steps)
        def _():
            for src, dst in zip(cast_in, cast_out):
                dst[...] = src[...].astype(BF16)

        def fetch(t):
            chip = order_ref[t // per]
            return pltpu.make_async_copy(held.at[chip, :, pl.ds((t % per) * tn, tn)], tile.at[t % 2],
                                         tile_sem.at[t % 2])

        if staged:
            near = _gather_phases([held], *sems[:2], [(0, D, (0, 1))])
            far = _relay_phases(held, *sems[2:])

        @pl.when(i == 0)
        def _():
            @pl.when(j == 0)
            def _():
                if staged:
                    near[0]()
                fetch(0).start()

            fetch(j).wait()
            ahead = j + 1 < nj
            if staged:
                ahead = ahead & (j + 1 != per) & (j + 1 != 3 * per)
            pl.when(ahead)(lambda: fetch(j + 1).start())

        rows = pl.ds(pl.multiple_of(i * tm, tm), tm)
        o_ref[...] = _dot(a_ref[rows, :], tile[j % 2]).astype(BF16)

        if staged:
            @pl.when((i == ni - 1) & (j == per - 1))
            def _():
                near[1]()
                near[2]()
                far[0]()
                fetch(per).start()

            @pl.when((i == ni - 1) & (j == 3 * per - 1))
            def _():
                far[1]()
                far[2]()
                fetch(3 * per).start()

    def cast_block(j, i, o):
        return jnp.minimum(j * ni + i, cast_steps - 1)

    out = pl.pallas_call(
        body, name="mm_in",
        grid_spec=pltpu.PrefetchScalarGridSpec(
            num_scalar_prefetch=1, grid=(nj, ni),
            in_specs=[pl.BlockSpec((s, D), lambda j, i, o: (0, 0)), ANY]
            + [pl.BlockSpec((a.shape[0] // cast_steps, a.shape[1]), lambda j, i, o: (cast_block(j, i, o), 0))
               for a in casting],
            out_specs=[pl.BlockSpec((tm, tn), lambda j, i, o: (i, o[j // per] * per + j % per)), ANY]
            + [pl.BlockSpec((None, a.shape[0] // cast_steps, a.shape[1]),
                            lambda j, i, o: (o[0], cast_block(j, i, o), 0)) for a in casting],
            scratch_shapes=[pltpu.VMEM((2, D, tn), BF16), pltpu.SemaphoreType.DMA((2,))]
            + (_gather_sems(1) + _relay_sems() if staged else [])),
        out_shape=[jax.ShapeDtypeStruct((s, IN_COLS), BF16), jax.ShapeDtypeStruct(wg.shape, wg.dtype)]
        + [jax.ShapeDtypeStruct((N_CHIPS,) + a.shape, BF16) for a in casting],
        input_output_aliases={2: 1},
        compiler_params=_params("arbitrary", "arbitrary", communicates=staged),
    )(order, h, wg, *casting)
    return out[0], out[1], out[2:]


def _tril_ws(ws_ref, g):
    r = lax.broadcasted_iota(jnp.int32, (CHUNK, CHUNK), 0)
    c = lax.broadcasted_iota(jnp.int32, (CHUNK, CHUNK), 1)
    return jnp.where(c <= r, ws_ref[g], 0.0).astype(BF16)


def _layer_norm(v):
    mu = jnp.mean(v, axis=-1, keepdims=True)
    d = v - mu
    rstd = lax.rsqrt(jnp.mean(d * d, axis=-1, keepdims=True) + EPS)
    return d * rstd, rstd


def gating_fwd(z, ln_g, ln_b, w_s, bs_t, gathering):
    s = z.shape[0]
    n = len(gathering)
    steps = s // CHUNK

    def body(*refs):
        u_ref, v_ref, lg_ref, lb_ref, ws_ref, bst_ref = refs[:6]
        ya_ref = refs[6 + n]
        ci = pl.program_id(0)
        if n:
            send, pass_on, finish = _gather_phases(refs[7 + n:7 + 2 * n], *refs[7 + 2 * n:], _spans(gathering))
            pl.when(ci == 0)(send)
        ug = _gelu(u_ref[...].astype(F32))
        vhat, _ = _layer_norm(_gelu(v_ref[...].astype(F32)))
        vn = (vhat * lg_ref[...] + lb_ref[...]).astype(BF16)
        for g in range(GROUPS):
            cols = slice(g * CHUNK, (g + 1) * CHUNK)
            mixed = _dot(_tril_ws(ws_ref, g), vn[:, cols]) + bst_ref[:, g:g + 1]
            ya_ref[:, cols] = (ug[:, cols] * mixed).astype(BF16)
        if n:
            pl.when(ci == steps - 1)(pass_on)
            pl.when(ci == steps - 1)(finish)

    out = pl.pallas_call(
        body, name="gating_fwd", grid=(steps,),
        in_specs=[_rows(CHUNK, D, 0), _rows(CHUNK, D, 1), _full((1, D)), _full((1, D)),
                  _full((GROUPS, CHUNK, CHUNK)), _full((CHUNK, GROUPS))] + [ANY] * n,
        out_specs=[_rows(CHUNK, D)] + [ANY] * n,
        out_shape=[jax.ShapeDtypeStruct((s, D), BF16)]
        + [jax.ShapeDtypeStruct(a.shape, a.dtype) for a in _arrays(gathering)],
        input_output_aliases={6 + w: 1 + w for w in range(n)},
        scratch_shapes=_gather_sems(n) if n else [],
        compiler_params=_params("arbitrary", communicates=bool(n)),
    )(z, z, ln_g, ln_b, w_s, bs_t, *_arrays(gathering))
    return out[0], out[1:]


def _attn_tables(s):
    nd = s // ATT_T
    r = np.arange(ATT_T)[None, :, None]
    c = np.arange(ATT_T)[None, None, :]
    delta = np.arange(nd)[:, None, None] * ATT_T + r - c
    count = np.zeros(delta.shape, np.int64)
    for window, dilation in ((128, 1), (512, 4), (2048, 16)):
        count += (delta >= 0) & (delta % dilation == 0) & (delta <= window)
    logc = np.where(count > 0, np.log(np.maximum(count, 1)), MASKED)
    return jnp.asarray(logc, F32)


AUG = 3


def _split3_np(x):
    terms, rest = [], np.asarray(x, np.float64)
    for _ in range(AUG):
        term = np.asarray(rest.astype(jnp.bfloat16), np.float64)
        terms.append(term)
        rest = rest - term
    return terms


def _split3(x):
    terms, rest = [], x
    for _ in range(AUG):
        term = rest.astype(BF16).astype(F32)
        terms.append(term)
        rest = rest - term
    return terms


def _alibi_tables(s):
    nb = s // ATT_T
    slopes = np.exp2(-8.0 * np.arange(1, HEADS + 1, dtype=np.float64) / HEADS)
    ka = np.zeros((HEADS // 2, 2, ATT_T, 128), np.float32)
    kb = np.zeros((HEADS // 2, 2, nb, 128), np.float32)
    for p in range(HEADS // 2):
        for e in range(2):
            base = HEAD_DIM * (1 - e)
            for a, term in enumerate(_split3_np(slopes[2 * p + e] * np.arange(ATT_T))):
                ka[p, e, :, base + a] = term
            for a, term in enumerate(_split3_np(slopes[2 * p + e] * ATT_T * np.arange(nb))):
                kb[p, e, :, base + AUG + a] = term
            ka[p, e, :, base + 2 * AUG:base + 3 * AUG] = 1.0
    return jnp.asarray(ka), jnp.asarray(kb)


def _head_masks():
    lane = lax.broadcasted_iota(jnp.int32, (1, 128), 1)
    first = lane < HEAD_DIM

    def ones(e, n):
        base = HEAD_DIM * (1 - e)
        return ((lane >= base) & (lane < base + n)).astype(F32)

    return first, lane, ones


def _place3(lane, at, terms, other):
    for a, term in enumerate(terms):
        other = jnp.where(lane == at + a, term, other)
    return other


def attn_fwd(z, logc, ka, kb, gathering):
    s = z.shape[0]
    nq = s // ATT_T
    t = ATT_T
    n = len(gathering)
    grp = ATT_GROUP
    ngrp = HEADS // 2 // grp
    wide = 128 * grp
    qcol, kcol, vcol = 2 * D // wide, 3 * D // wide, 4 * D // wide

    def body(*refs):
        q_ref, k_ref, v_ref, lc_ref, ka_ref, kb_ref = refs[:6]
        y_ref, lse_ref = refs[6 + n:8 + n]
        q_s, k_s, v_s, m_s, l_s, acc_s = refs[8 + 2 * n:14 + 2 * n]
        gi, qi = pl.program_id(0), pl.program_id(1)
        first, lane, ones = _head_masks()
        if n:
            send, pass_on, finish = _gather_phases(refs[8 + n:8 + 2 * n], *refs[14 + 2 * n:], _spans(gathering))
            pl.when((gi == 0) & (qi == 0))(send)

        @pl.when(qi == 0)
        def _():
            sel = jnp.broadcast_to(first.astype(F32), (t, 128))
            for pr in range(grp):
                cols = slice(pr * 128, (pr + 1) * 128)
                for jb in range(nq):
                    kj = k_ref[jb * t:(jb + 1) * t, cols].astype(F32)
                    vj = v_ref[jb * t:(jb + 1) * t, cols].astype(F32)
                    k_s[pr, 0, jb] = jnp.where(first, kj, ka_ref[pr, 0] + kb_ref[pr, 0, jb:jb + 1, :]).astype(BF16)
                    k_s[pr, 1, jb] = jnp.where(first, ka_ref[pr, 1] + kb_ref[pr, 1, jb:jb + 1, :], kj).astype(BF16)
                    v_s[pr, jb, 0:t, 0:128] = jnp.where(first, vj, 0.0).astype(BF16)
                    v_s[pr, jb, t:2 * t, 0:128] = jnp.where(first, 0.0, vj).astype(BF16)
                    v_s[pr, jb, 0:t, 128:256] = sel.astype(BF16)
                    v_s[pr, jb, t:2 * t, 128:256] = (1.0 - sel).astype(BF16)

        for pr in range(grp):
            q = q_ref[:, pr * 128:(pr + 1) * 128].astype(F32) * (1.0 / math.sqrt(HEAD_DIM))
            q_s[pr, 0] = jnp.where(first, q, ones(0, 2 * AUG)).astype(BF16)
            q_s[pr, 1] = jnp.where(first, ones(1, 2 * AUG), q).astype(BF16)
        m_s[...] = jnp.full_like(m_s, MASKED)
        l_s[...] = jnp.zeros_like(l_s)
        acc_s[...] = jnp.zeros_like(acc_s)

        def scores(j):
            return tuple(_dot(q_s[pr, e], k_s[pr, e, j], NT) for pr in range(grp) for e in range(2))

        def step(j, carry):
            softmax_block(j, scores(j))
            return carry

        def softmax_block(j, u):
            lc = lc_ref[qi - j]
            for pr in range(grp):
                u0 = u[2 * pr] + lc
                u1 = u[2 * pr + 1] + lc
                m0, m1 = m_s[pr, 0], m_s[pr, 1]
                n0 = jnp.maximum(m0, jnp.max(u0, axis=-1, keepdims=True))
                n1 = jnp.maximum(m1, jnp.max(u1, axis=-1, keepdims=True))
                m_s[pr, 0], m_s[pr, 1] = n0, n1
                p = jnp.concatenate([jnp.exp(u0 - jnp.concatenate([n0, n0], axis=1)).astype(BF16),
                                     jnp.exp(u1 - jnp.concatenate([n1, n1], axis=1)).astype(BF16)], axis=1)
                pv = _dot(p, v_s[pr, j])
                alpha = jnp.where(first, jnp.exp(m0 - n0), jnp.exp(m1 - n1))
                acc_s[pr] = acc_s[pr] * alpha + pv[:, 0:128]
                l_s[pr] = l_s[pr] * alpha + pv[:, 128:256]

        lax.fori_loop(0, qi + 1, step, 0)
        for pr in range(grp):
            cols = slice(pr * 128, (pr + 1) * 128)
            y_ref[:, cols] = (acc_s[pr] / l_s[pr]).astype(BF16)
            lse_ref[:, cols] = jnp.where(first, m_s[pr, 0], m_s[pr, 1]) + jnp.log(l_s[pr])
        if n:
            pl.when((gi == ngrp - 1) & (qi == nq - 1))(pass_on)
            pl.when((gi == ngrp - 1) & (qi == nq - 1))(finish)

    out = pl.pallas_call(
        body, name="attn_fwd", grid=(ngrp, nq),
        in_specs=[pl.BlockSpec((t, wide), lambda g, i: (i, qcol + g)),
                  pl.BlockSpec((s, wide), lambda g, i: (0, kcol + g)),
                  pl.BlockSpec((s, wide), lambda g, i: (0, vcol + g)),
                  _full((nq, t, t)),
                  pl.BlockSpec((grp, 2, t, 128), lambda g, i: (g, 0, 0, 0)),
                  pl.BlockSpec((grp, 2, nq, 128), lambda g, i: (g, 0, 0, 0))] + [ANY] * n,
        out_specs=[pl.BlockSpec((t, wide), lambda g, i: (i, g)), pl.BlockSpec((t, wide), lambda g, i: (i, g))]
        + [ANY] * n,
        out_shape=[jax.ShapeDtypeStruct((s, D), BF16), jax.ShapeDtypeStruct((s, D), F32)]
        + [jax.ShapeDtypeStruct(a.shape, a.dtype) for a in _arrays(gathering)],
        input_output_aliases={6 + w: 2 + w for w in range(n)},
        scratch_shapes=[pltpu.VMEM((grp, 2, t, 128), BF16), pltpu.VMEM((grp, 2, nq, t, 128), BF16),
                        pltpu.VMEM((grp, nq, 2 * t, 256), BF16), pltpu.VMEM((grp, 2, t, 128), F32),
                        pltpu.VMEM((grp, t, 128), F32), pltpu.VMEM((grp, t, 128), F32)]
        + (_gather_sems(n) if n else []),
        compiler_params=_params("arbitrary", "arbitrary", communicates=bool(n)),
    )(z, z, z, logc, ka, kb, *_arrays(gathering))
    return out[0], out[1], out[2:]


def proj_merge(ya, yb, wa, wb, z, bg, gathering):
    s = ya.shape[0]
    tm = 512
    n = len(gathering)
    steps = s // tm

    def body(*refs):
        ya_ref, yb_ref, wa_ref, wb_ref, ga_ref, gb_ref, bg_ref = refs[:7]
        mg_ref, pa_ref, pb_ref = refs[7 + n:10 + n]
        i = pl.program_id(0)
        if n:
            send, pass_on, finish = _gather_phases(refs[10 + n:10 + 2 * n], *refs[10 + 2 * n:], _spans(gathering))
            pl.when(i == 0)(send)
            pl.when(i == steps - 1)(pass_on)
        pa = _dot(ya_ref[...], wa_ref[...])
        pb = _dot(yb_ref[...], wb_ref[...])
        sa = _sigmoid(ga_ref[...] + bg_ref[0:1, :])
        sb = _sigmoid(gb_ref[...] + bg_ref[1:2, :])
        mg_ref[...] = (sa * pa + sb * pb).astype(BF16)
        pa_ref[...] = pa.astype(BF16)
        pb_ref[...] = pb.astype(BF16)
        if n:
            pl.when(i == steps - 1)(finish)

    out = jax.ShapeDtypeStruct((s, D), BF16)
    res = pl.pallas_call(
        body, name="proj_merge", grid=(steps,),
        in_specs=[_rows(tm, D), _rows(tm, D), _full((D, D)), _full((D, D)),
                  _rows(tm, D, 5), _rows(tm, D, 6), _full((2, D))] + [ANY] * n,
        out_specs=[_rows(tm, D)] * 3 + [ANY] * n,
        out_shape=[out] * 3 + [jax.ShapeDtypeStruct(a.shape, a.dtype) for a in _arrays(gathering)],
        input_output_aliases={7 + w: 3 + w for w in range(n)},
        scratch_shapes=_gather_sems(n) if n else [],
        compiler_params=_params("arbitrary", communicates=bool(n)),
    )(ya, yb, wa, wb, z, z, bg, *_arrays(gathering))
    return res[0], res[1], res[2], res[3:]


def out_norm(merged, w_out, x, g_post, g_fpre, gathering):
    s = x.shape[0]
    tm = 512
    n = len(gathering)
    steps = s // tm

    def body(*refs):
        mg_ref, w_ref, x_ref, gp_ref, gf_ref = refs[:5]
        o_ref, x1_ref, h2_ref = refs[5 + n:8 + n]
        i = pl.program_id(0)
        if n:
            send, pass_on, finish = _gather_phases(refs[8 + n:8 + 2 * n], *refs[8 + 2 * n:], _spans(gathering))
            pl.when(i == 0)(send)
            pl.when(i == steps - 1)(pass_on)
        o = _dot(mg_ref[...], w_ref[...])
        ohat, _ = _rms(o)
        x1 = x_ref[...] + ohat * gp_ref[...]
        x1hat, _ = _rms(x1)
        o_ref[...] = o
        x1_ref[...] = x1
        h2_ref[...] = (x1hat * gf_ref[...]).astype(BF16)
        if n:
            pl.when(i == steps - 1)(finish)

    res = pl.pallas_call(
        body, name="out_norm", grid=(steps,),
        in_specs=[_rows(tm, D), _full((D, D)), _rows(tm, D), _full((1, D)), _full((1, D))] + [ANY] * n,
        out_specs=[_rows(tm, D)] * 3 + [ANY] * n,
        out_shape=[jax.ShapeDtypeStruct((s, D), F32), jax.ShapeDtypeStruct((s, D), F32),
                   jax.ShapeDtypeStruct((s, D), BF16)]
        + [jax.ShapeDtypeStruct(a.shape, a.dtype) for a in _arrays(gathering)],
        input_output_aliases={5 + w: 3 + w for w in range(n)},
        scratch_shapes=_gather_sems(n) if n else [],
        compiler_params=_params("arbitrary", communicates=bool(n)),
    )(merged, w_out, x, g_post, g_fpre, *_arrays(gathering))
    return res[0], res[1], res[2], res[3:]


def mm_ff1(h2, wg, gathering):
    s = h2.shape[0]
    tm = 1024
    n = len(gathering)
    ni = s // tm

    def body(*refs):
        a_ref, b_ref = refs[:2]
        o_ref, r_ref = refs[2 + n:4 + n]
        i, j = pl.program_id(0), pl.program_id(1)
        if n:
            send, pass_on, finish = _gather_phases(refs[4 + n:4 + 2 * n], *refs[4 + 2 * n:], _spans(gathering))
            pl.when((i == 0) & (j == 0))(send)
            pl.when((i == ni - 1) & (j == N_CHIPS // 2))(pass_on)
        a = _dot(a_ref[...], b_ref[...])
        o_ref[...] = a.astype(BF16)
        r = jnp.maximum(a, 0.0)
        r_ref[...] = (r * r).astype(BF16)
        if n:
            pl.when((i == ni - 1) & (j == N_CHIPS - 1))(finish)

    res = pl.pallas_call(
        body, name="mm_ff1", grid=(ni, N_CHIPS),
        in_specs=[pl.BlockSpec((tm, D), lambda i, j: (i, 0)), pl.BlockSpec((None, D, D), lambda i, j: (j, 0, 0))]
        + [ANY] * n,
        out_specs=[pl.BlockSpec((tm, D), lambda i, j: (i, j))] * 2 + [ANY] * n,
        out_shape=[jax.ShapeDtypeStruct((s, D_FF), BF16), jax.ShapeDtypeStruct((s, D_FF), BF16)]
        + [jax.ShapeDtypeStruct(a.shape, a.dtype) for a in _arrays(gathering)],
        input_output_aliases={2 + w: 2 + w for w in range(n)},
        scratch_shapes=_gather_sems(n) if n else [],
        compiler_params=_params("arbitrary", "arbitrary", communicates=bool(n)),
    )(h2, wg, *_arrays(gathering))
    return res[0], res[1], res[2:]


def ff2_loss(rl, w_ff2, x1, target, g_fpost):
    s = x1.shape[0]
    tm = 256

    def body(rl_ref, w_ref, x1_ref, t_ref, g_ref, dy_ref, df_ref, dg_ref, loss_ref):
        @pl.when(pl.program_id(0) == 0)
        def _():
            dg_ref[...] = jnp.zeros_like(dg_ref)
            loss_ref[...] = jnp.zeros_like(loss_ref)

        f = _dot(rl_ref[...], w_ref[...])
        fhat, r = _rms(f)
        err = x1_ref[...] + fhat * g_ref[...] - t_ref[...]
        loss_ref[...] += 0.5 * jnp.sum(jnp.mean(err * err, axis=-1, keepdims=True), axis=0, keepdims=True)
        dy = err * (1.0 / D)
        dy_ref[...] = dy
        dg_ref[...] += jnp.sum(dy * fhat, axis=0, keepdims=True)
        df_ref[...] = _rms_bwd(dy * g_ref[...], fhat, r).astype(BF16)

    return pl.pallas_call(
        body, name="ff2_loss", grid=(s // tm,),
        in_specs=[_rows(tm, D_FF), _full((D_FF, D)), _rows(tm, D), _rows(tm, D), _full((1, D))],
        out_specs=[_rows(tm, D), _rows(tm, D), _full((1, D)), _full((1, 1))],
        out_shape=[jax.ShapeDtypeStruct((s, D), F32), jax.ShapeDtypeStruct((s, D), BF16),
                   jax.ShapeDtypeStruct((1, D), F32), jax.ShapeDtypeStruct((1, 1), F32)],
        compiler_params=_params("arbitrary"),
    )(rl, w_ff2, x1, target, g_fpost)


def mm_tn(name, a, b, ta, tb, out_shape, out_spec):
    s = a.shape[0]

    def body(a_ref, b_ref, o_ref):
        o_ref[...] = _dot(a_ref[...], b_ref[...], TN)

    return pl.pallas_call(
        body, name=name, grid=(a.shape[1] // ta, b.shape[1] // tb),
        in_specs=[pl.BlockSpec((s, ta), lambda i, j: (0, i)), pl.BlockSpec((s, tb), lambda i, j: (0, j))],
        out_specs=out_spec, out_shape=jax.ShapeDtypeStruct(out_shape, F32),
        compiler_params=_params("parallel", "parallel"),
    )(a, b)


def mm_nt(name, a, w):
    s = a.shape[0]
    tm = 512

    def body(a_ref, w_ref, o_ref):
        o_ref[...] = _dot(a_ref[...], w_ref[...], NT).astype(BF16)

    return pl.pallas_call(
        body, name=name, grid=(s // tm,), in_specs=[_rows(tm, D), _full((D, D))], out_specs=_rows(tm, D),
        out_shape=jax.ShapeDtypeStruct((s, D), BF16), compiler_params=_params("parallel"),
    )(a, w)


def ff2_bwd(df, w_ff2, a):
    s = df.shape[0]
    tm = 1024

    def body(df_ref, w_ref, a_ref, da_ref):
        drl = _dot(df_ref[...], w_ref[...], NT)
        da_ref[...] = (drl * (2.0 * jnp.maximum(a_ref[...].astype(F32), 0.0))).astype(BF16)

    return pl.pallas_call(
        body, name="ff2_bwd", grid=(s // tm, D_FF // D),
        in_specs=[pl.BlockSpec((tm, D), lambda i, j: (i, 0)), pl.BlockSpec((D, D), lambda i, j: (j, 0)),
                  pl.BlockSpec((tm, D), lambda i, j: (i, j))],
        out_specs=pl.BlockSpec((tm, D), lambda i, j: (i, j)),
        out_shape=jax.ShapeDtypeStruct((s, D_FF), BF16), compiler_params=_params("parallel", "parallel"),
    )(df, w_ff2, a)


def ff1_bwd_norms(da, wg, x1, o, dy, g_fpre, g_post, swapping):
    s = x1.shape[0]
    tm = 256
    n = len(swapping)
    steps = s // tm

    def body(*refs):
        da_ref, w_ref, x1_ref, o_ref, dy_ref, gf_ref, gp_ref = refs[:7]
        dx1_ref, do_ref, dgf_ref, dgp_ref = refs[7 + n:11 + n]
        i = pl.program_id(0)
        if n:
            send, finish = _swap_phases(refs[7:7 + n], refs[11 + n:11 + 2 * n], *refs[11 + 2 * n:])
            pl.when(i == 0)(send)

        @pl.when(i == 0)
        def _():
            dgf_ref[...] = jnp.zeros_like(dgf_ref)
            dgp_ref[...] = jnp.zeros_like(dgp_ref)

        dh2 = _dot(da_ref[:, 0:D], w_ref[0], NT)
        for k in range(1, N_CHIPS):
            dh2 = dh2 + _dot(da_ref[:, k * D:(k + 1) * D], w_ref[k], NT)
        x1hat, r2 = _rms(x1_ref[...])
        dgf_ref[...] += jnp.sum(dh2 * x1hat, axis=0, keepdims=True)
        dx1 = dy_ref[...] + _rms_bwd(dh2 * gf_ref[...], x1hat, r2)
        ohat, r1 = _rms(o_ref[...])
        dgp_ref[...] += jnp.sum(dx1 * ohat, axis=0, keepdims=True)
        dx1_ref[...] = dx1
        do_ref[...] = _rms_bwd(dx1 * gp_ref[...], ohat, r1).astype(BF16)
        if n:
            pl.when(i == steps - 1)(finish)

    res = pl.pallas_call(
        body, name="ff1_bwd_norms", grid=(steps,),
        in_specs=[_rows(tm, D_FF), _full((N_CHIPS, D, D)), _rows(tm, D), _rows(tm, D), _rows(tm, D),
                  _full((1, D)), _full((1, D))] + [ANY] * n,
        out_specs=[_rows(tm, D), _rows(tm, D), _full((1, D)), _full((1, D))] + [ANY] * n,
        out_shape=[jax.ShapeDtypeStruct((s, D), F32), jax.ShapeDtypeStruct((s, D), BF16),
                   jax.ShapeDtypeStruct((1, D), F32), jax.ShapeDtypeStruct((1, D), F32)] + _swap_shapes(swapping),
        scratch_shapes=_swap_sems(n) if n else [],
        compiler_params=_params("arbitrary", communicates=bool(n)),
    )(da, wg, x1, o, dy, g_fpre, g_post, *swapping)
    return res[0], res[1], res[2], res[3], res[4:]


def out_bwd_gates(do, w_out, pa, pb, z, bg):
    s = do.shape[0]
    tm = 512

    def body(do_ref, w_ref, pa_ref, pb_ref, ga_ref, gb_ref, bg_ref, dpa_ref, dpb_ref, dga_ref, dgb_ref, dbg_ref):
        @pl.when(pl.program_id(0) == 0)
        def _():
            dbg_ref[...] = jnp.zeros_like(dbg_ref)

        dm = _dot(do_ref[...], w_ref[...], NT)
        sa = _sigmoid(ga_ref[...] + bg_ref[0:1, :])
        sb = _sigmoid(gb_ref[...] + bg_ref[1:2, :])
        dpa_ref[...] = (dm * sa).astype(BF16)
        dpb_ref[...] = (dm * sb).astype(BF16)
        dga = dm * pa_ref[...].astype(F32) * (sa * (1.0 - sa))
        dgb = dm * pb_ref[...].astype(F32) * (sb * (1.0 - sb))
        dga_ref[...] = dga.astype(BF16)
        dgb_ref[...] = dgb.astype(BF16)
        dbg_ref[0:1, :] += jnp.sum(dga, axis=0, keepdims=True)
        dbg_ref[1:2, :] += jnp.sum(dgb, axis=0, keepdims=True)

    out = jax.ShapeDtypeStruct((s, D), BF16)
    return pl.pallas_call(
        body, name="out_bwd_gates", grid=(s // tm,),
        in_specs=[_rows(tm, D), _full((D, D)), _rows(tm, D), _rows(tm, D), _rows(tm, D, 5), _rows(tm, D, 6),
                  _full((2, D))],
        out_specs=[_rows(tm, D)] * 4 + [_full((2, D))],
        out_shape=[out] * 4 + [jax.ShapeDtypeStruct((2, D), F32)], compiler_params=_params("arbitrary"),
    )(do, w_out, pa, pb, z, z, bg)


def gating_bwd(z, dpa, w_a, ln_g, ln_b, w_s, bs_t, swapping):
    s = z.shape[0]
    ones = functools.partial(jnp.ones, (8, CHUNK), BF16)
    n = len(swapping)

    def body(*refs):
        u_ref, v_ref, dpa_ref, wa_ref, lg_ref, lb_ref, ws_ref, bst_ref = refs[:8]
        du_ref, dv_ref, dws_ref, dbs_ref, dlg_ref, dlb_ref = refs[8 + n:14 + n]
        dvn_ref = refs[14 + 2 * n]
        ci = pl.program_id(0)
        if n:
            send, finish = _swap_phases(refs[8:8 + n], refs[14 + n:14 + 2 * n], *refs[15 + 2 * n:])
            pl.when(ci == 0)(send)

        @pl.when(ci == 0)
        def _():
            dws_ref[...] = jnp.zeros_like(dws_ref)
            dbs_ref[...] = jnp.zeros_like(dbs_ref)
            dlg_ref[...] = jnp.zeros_like(dlg_ref)
            dlb_ref[...] = jnp.zeros_like(dlb_ref)

        ug, dug_du = _gelu_and_grad(u_ref[...].astype(F32))
        vg, dvg_dv = _gelu_and_grad(v_ref[...].astype(F32))
        vhat, rstd = _layer_norm(vg)
        vn = (vhat * lg_ref[...] + lb_ref[...]).astype(BF16)
        dya = _dot(dpa_ref[...], wa_ref[...], NT)
        for g in range(GROUPS):
            cols = slice(g * CHUNK, (g + 1) * CHUNK)
            ws = _tril_ws(ws_ref, g)
            mixed = _dot(ws, vn[:, cols]) + bst_ref[:, g:g + 1]
            du_ref[:, cols] = (dya[:, cols] * mixed * dug_du[:, cols]).astype(BF16)
            dmix = (dya[:, cols] * ug[:, cols]).astype(BF16)
            dbs_ref[g] += _dot(ones(), dmix, NT)
            dws_ref[g] += _dot(dmix, vn[:, cols], NT)
            dvn_ref[:, cols] = _dot(ws, dmix, TN)
        dvn = dvn_ref[...]
        dlg_ref[...] += jnp.sum(dvn * vhat, axis=0, keepdims=True)
        dlb_ref[...] += jnp.sum(dvn, axis=0, keepdims=True)
        dvh = dvn * lg_ref[...]
        dvg = rstd * (dvh - jnp.mean(dvh, axis=-1, keepdims=True)
                      - vhat * jnp.mean(dvh * vhat, axis=-1, keepdims=True))
        dv_ref[...] = (dvg * dvg_dv).astype(BF16)

        @pl.when(ci == pl.num_programs(0) - 1)
        def _():
            r = lax.broadcasted_iota(jnp.int32, (CHUNK, CHUNK), 0)
            c = lax.broadcasted_iota(jnp.int32, (CHUNK, CHUNK), 1)
            for g in range(GROUPS):
                dws_ref[g] = jnp.where(c <= r, dws_ref[g], 0.0)

        if n:
            pl.when(ci == pl.num_programs(0) - 1)(finish)

    out = jax.ShapeDtypeStruct((s, D), BF16)
    res = pl.pallas_call(
        body, name="gating_bwd", grid=(s // CHUNK,),
        in_specs=[_rows(CHUNK, D, 0), _rows(CHUNK, D, 1), _rows(CHUNK, D), _full((D, D)), _full((1, D)), _full((1, D)),
                  _full((GROUPS, CHUNK, CHUNK)), _full((CHUNK, GROUPS))] + [ANY] * n,
        out_specs=[_rows(CHUNK, D), _rows(CHUNK, D), _full((GROUPS, CHUNK, CHUNK)), _full((GROUPS, 8, CHUNK)),
                   _full((1, D)), _full((1, D))] + [ANY] * n,
        out_shape=[out, out, jax.ShapeDtypeStruct((GROUPS, CHUNK, CHUNK), F32),
                   jax.ShapeDtypeStruct((GROUPS, 8, CHUNK), F32),
                   jax.ShapeDtypeStruct((1, D), F32), jax.ShapeDtypeStruct((1, D), F32)] + _swap_shapes(swapping),
        scratch_shapes=[pltpu.VMEM((CHUNK, D), F32)] + (_swap_sems(n) if n else []),
        compiler_params=_params("arbitrary", communicates=bool(n)),
    )(z, z, dpa, w_a, ln_g, ln_b, w_s, bs_t, *swapping)
    return (*res[:6], res[6:])


def attn_bwd(z, yb, dyb, lse, logc, ka, kb, scattering, gathering=None):
    s = z.shape[0]
    nq = s // ATT_T
    t = ATT_T
    grp = ATT_BWD_GROUP
    ngrp = HEADS // 2 // grp
    wide = 128 * grp
    qcol, kcol, vcol = 2 * D // wide, 3 * D // wide, 4 * D // wide
    scale = 1.0 / math.sqrt(HEAD_DIM)
    n = len(scattering)
    g8 = 0 if gathering is None else 1

    def body(*refs):
        q_ref, k_ref, v_ref, y_ref, dy_ref, lse_ref, lc_ref, ka_ref, kb_ref = refs[:9]
        dq_ref, dk_ref, dv_ref = refs[9 + n + g8:12 + n + g8]
        qa_s, qt_s, da_s, dt_s, dq_s, dkt_s, dvt_s = refs[12 + 2 * n + 2 * g8:19 + 2 * n + 2 * g8]
        sems = refs[19 + 2 * n + 2 * g8:]
        gi, j = pl.program_id(0), pl.program_id(1)
        first, lane, ones = _head_masks()
        if n:
            send, finish = _scatter_phases(refs[9:9 + n], refs[12 + n + g8:12 + 2 * n + g8], *sems[:2])
            pl.when((gi == 0) & (j == 0))(send)
        if g8:
            send8, pass_on8, finish8 = _allgather8_phases(refs[12 + 2 * n + g8], *sems[2 * (n > 0):])
            pl.when((gi == 0) & (j == 0))(send8)
            pl.when((gi == ngrp - 1) & (j == nq - 1))(pass_on8)

        @pl.when(j == 0)
        def _():
            dq_s[...] = jnp.zeros_like(dq_s)
            for pr in range(grp):
                cols = slice(pr * 128, (pr + 1) * 128)
                for ib in range(nq):
                    rows = slice(ib * t, (ib + 1) * t)
                    q = q_ref[rows, cols].astype(F32) * scale
                    lse = lse_ref[rows, cols]
                    qa_s[pr, 0, ib] = jnp.where(first, q, _place3(lane, HEAD_DIM + 2 * AUG, _split3(-lse[:, 0:1]),
                                                                  ones(0, 2 * AUG))).astype(BF16)
                    qa_s[pr, 1, ib] = jnp.where(
                        first, _place3(lane, 2 * AUG, _split3(-lse[:, HEAD_DIM:HEAD_DIM + 1]), ones(1, 2 * AUG)),
                        q).astype(BF16)
                    qt_s[pr, ib, :, 0:t] = jnp.where(first, q, 0.0).T.astype(BF16)
                    qt_s[pr, ib, :, t:2 * t] = jnp.where(first, 0.0, q).T.astype(BF16)
                    do = dy_ref[rows, cols].astype(F32)
                    prod = do * y_ref[rows, cols].astype(F32)
                    dd0 = jnp.sum(jnp.where(first, prod, 0.0), axis=-1, keepdims=True)
                    dd1 = jnp.sum(jnp.where(first, 0.0, prod), axis=-1, keepdims=True)
                    da_s[pr, 0, ib] = jnp.where(first, do, _place3(lane, HEAD_DIM, _split3(-dd0), 0.0)).astype(BF16)
                    da_s[pr, 1, ib] = jnp.where(first, _place3(lane, 0, _split3(-dd1), 0.0), do).astype(BF16)
                    dt_s[pr, ib, :, 0:t] = jnp.where(first, do, 0.0).T.astype(BF16)
                    dt_s[pr, ib, :, t:2 * t] = jnp.where(first, 0.0, do).T.astype(BF16)

        keys = []
        for pr in range(grp):
            kj = k_ref[:, pr * 128:(pr + 1) * 128].astype(F32)
            vj = v_ref[:, pr * 128:(pr + 1) * 128].astype(F32)
            keys.append((
                jnp.where(first, kj, ka_ref[pr, 0] + kb_ref[pr, 0, pl.ds(j, 1), :]).astype(BF16),
                jnp.where(first, ka_ref[pr, 1] + kb_ref[pr, 1, pl.ds(j, 1), :], kj).astype(BF16),
                jnp.concatenate([jnp.where(first, kj, 0.0), jnp.where(first, 0.0, kj)], axis=0).astype(BF16),
                jnp.where(first, vj, ones(0, AUG)).astype(BF16),
                jnp.where(first, ones(1, AUG), vj).astype(BF16)))
        dkt_s[...] = jnp.zeros_like(dkt_s)
        dvt_s[...] = jnp.zeros_like(dvt_s)

        def step(i, _):
            lc = lc_ref[i - j]
            rows = pl.ds(pl.multiple_of(i * t, t), t)
            for pr in range(grp):
                k0a, k1a, kst, v0a, v1a = keys[pr]
                p0 = jnp.exp(_dot(qa_s[pr, 0, i], k0a, NT) + lc)
                p1 = jnp.exp(_dot(qa_s[pr, 1, i], k1a, NT) + lc)
                e0 = (p0 * _dot(da_s[pr, 0, i], v0a, NT)).astype(BF16)
                e1 = (p1 * _dot(da_s[pr, 1, i], v1a, NT)).astype(BF16)
                dq_s[pr, rows, :] += _dot(jnp.concatenate([e0, e1], axis=1), kst)
                dvt_s[pr] += _dot(dt_s[pr, i], jnp.concatenate([p0.astype(BF16), p1.astype(BF16)], axis=0))
                dkt_s[pr] += _dot(qt_s[pr, i], jnp.concatenate([e0, e1], axis=0))
            return 0

        lax.fori_loop(j, nq, step, 0)
        for pr in range(grp):
            dk_ref[:, pr * 128:(pr + 1) * 128] = dkt_s[pr].T.astype(BF16)
            dv_ref[:, pr * 128:(pr + 1) * 128] = dvt_s[pr].T.astype(BF16)

        @pl.when(j == nq - 1)
        def _():
            for pr in range(grp):
                dq_ref[:, pr * 128:(pr + 1) * 128] = (dq_s[pr] * scale).astype(BF16)

        if n:
            pl.when((gi == ngrp - 1) & (j == nq - 1))(finish)
        if g8:
            pl.when((gi == ngrp - 1) & (j == nq - 1))(finish8)

    colblock = lambda c: pl.BlockSpec((s, wide), lambda g, j: (0, c + g))
    blk = lambda c: pl.BlockSpec((t, wide), lambda g, j: (j, c + g))
    out = jax.ShapeDtypeStruct((s, D), BF16)
    res = pl.pallas_call(
        body, name="attn_bwd", grid=(ngrp, nq),
        in_specs=[colblock(qcol), blk(kcol), blk(vcol), colblock(0), colblock(0), colblock(0),
                  _full((nq, t, t)), pl.BlockSpec((grp, 2, t, 128), lambda g, j: (g, 0, 0, 0)),
                  pl.BlockSpec((grp, 2, nq, 128), lambda g, j: (g, 0, 0, 0))] + [ANY] * (n + g8),
        out_specs=[colblock(0), blk(0), blk(0)] + [ANY] * (n + g8),
        out_shape=[out] * 3 + _scatter_shapes(scattering)
        + ([jax.ShapeDtypeStruct(gathering.shape, gathering.dtype)] if g8 else []),
        input_output_aliases={9 + n: 3 + n} if g8 else {},
        scratch_shapes=[pltpu.VMEM((grp, 2, nq, t, 128), BF16), pltpu.VMEM((grp, nq, 128, 2 * t), BF16),
                        pltpu.VMEM((grp, 2, nq, t, 128), BF16), pltpu.VMEM((grp, nq, 128, 2 * t), BF16),
                        pltpu.VMEM((grp, s, 128), F32), pltpu.VMEM((grp, 128, t), F32),
                        pltpu.VMEM((grp, 128, t), F32)]
        + (_scatter_sems(n) if n else [])
        + ([pltpu.SemaphoreType.DMA((7,)), pltpu.SemaphoreType.DMA((7,))] if g8 else []),
        compiler_params=_params("arbitrary", "arbitrary", communicates=bool(n + g8)),
    )(z, z, z, yb, dyb, lse, logc, ka, kb, *scattering, *([gathering] if g8 else []))
    return res[0], res[1], res[2], res[3:3 + n], (res[3 + n] if g8 else None)


def in_bwd_norm(dz, wg, x, dx1, g_pre, scattering, gathering=None):
    s = x.shape[0]
    tm = 512
    n = len(scattering)
    g = 0 if gathering is None else 1
    last = (s // tm - 1, N_CHIPS - 1)

    def body(*refs):
        dz_ref, w_ref, x_ref, dx1_ref, g_ref = refs[:5]
        dx_ref, dg_ref = refs[5 + n + g:7 + n + g]
        acc_ref = refs[7 + 2 * n + 2 * g]
        sems = refs[8 + 2 * n + 2 * g:]
        i, k = pl.program_id(0), pl.program_id(1)
        if n:
            send, finish = _scatter_phases(refs[5:5 + n], refs[7 + n + g:7 + 2 * n + g], *sems[:2])
            pl.when((i == 0) & (k == 0))(send)
        if g:
            send8, pass_on8, finish8 = _allgather8_phases(refs[7 + 2 * n + g], *sems[2 * (n > 0):])
            pl.when((i == 0) & (k == 0))(send8)
            pl.when((i == last[0]) & (k == last[1]))(pass_on8)

        @pl.when((i == 0) & (k == 0))
        def _():
            dg_ref[...] = jnp.zeros_like(dg_ref)

        part = _dot(dz_ref[...], w_ref[...], NT)

        @pl.when(k == 0)
        def _():
            acc_ref[...] = part

        @pl.when(k > 0)
        def _():
            acc_ref[...] += part

        @pl.when(k == N_CHIPS - 1)
        def _():
            dh = acc_ref[...]
            xhat, r = _rms(x_ref[...])
            dg_ref[...] += jnp.sum(dh * xhat, axis=0, keepdims=True)
            dx_ref[...] = dx1_ref[...] + _rms_bwd(dh * g_ref[...], xhat, r)

        if n:
            pl.when((i == last[0]) & (k == last[1]))(finish)
        if g:
            pl.when((i == last[0]) & (k == last[1]))(finish8)

    row = pl.BlockSpec((tm, D), lambda i, k: (i, 0))
    vec = pl.BlockSpec((1, D), lambda i, k: (0, 0))
    res = pl.pallas_call(
        body, name="in_bwd_norm", grid=(s // tm, N_CHIPS),
        in_specs=[pl.BlockSpec((tm, IN_SHARD), lambda i, k: (i, k)),
                  pl.BlockSpec((None, D, IN_SHARD), lambda i, k: (k, 0, 0)), row, row, vec] + [ANY] * (n + g),
        out_specs=[row, vec] + [ANY] * (n + g),
        out_shape=[jax.ShapeDtypeStruct((s, D), F32), jax.ShapeDtypeStruct((1, D), F32)]
        + _scatter_shapes(scattering) + ([jax.ShapeDtypeStruct(gathering.shape, gathering.dtype)] if g else []),
        input_output_aliases={5 + n: 2 + n} if g else {},
        scratch_shapes=[pltpu.VMEM((tm, D), F32)] + (_scatter_sems(n) if n else [])
        + ([pltpu.SemaphoreType.DMA((7,)), pltpu.SemaphoreType.DMA((7,))] if g else []),
        compiler_params=_params("arbitrary", "arbitrary", communicates=bool(n + g)),
    )(dz, wg, x, dx1, g_pre, *scattering, *([gathering] if g else []))
    return res[0], res[1], res[2:2 + n], (res[2 + n] if g else None)


def _adamw_math(w, g, m, v):
    m = ADAM_B1 * m + (1.0 - ADAM_B1) * g
    v = ADAM_B2 * v + (1.0 - ADAM_B2) * (g * g)
    m_hat = m / (1.0 - ADAM_B1 ** ADAM_STEP)
    v_hat = v / (1.0 - ADAM_B2 ** ADAM_STEP)
    delta = -ADAM_LR * (m_hat / (jnp.sqrt(v_hat) + ADAM_EPS) + ADAM_WD * w)
    return delta, m, v


def adamw(name, w, g, m, v, tr):
    r, c = w.shape

    def body(w_ref, g_ref, m_ref, v_ref, go_ref, d_ref, nm_ref, nv_ref):
        g = g_ref[...]
        go_ref[...] = g
        d_ref[...], nm_ref[...], nv_ref[...] = _adamw_math(w_ref[...], g, m_ref[...], v_ref[...])

    out = jax.ShapeDtypeStruct((r, c), F32)
    return pl.pallas_call(
        body, name=name, grid=(r // tr,), in_specs=[_rows(tr, c)] * 4, out_specs=[_rows(tr, c)] * 4,
        out_shape=[out] * 4, compiler_params=_params("parallel"),
    )(w, g, m, v)


def _allgather8_phases(buf, send_sems, recv_sems):
    x, y, c, chips = _place()
    me = 2 * x + y
    sibling = (x, y, 1 - c)
    rows = buf.shape[1] // 2

    def part(chip, core):
        return buf.at[chip, pl.ds(core * rows, rows)]

    def copy(k, block, to):
        return pltpu.make_async_remote_copy(src_ref=block, dst_ref=block, send_sem=send_sems.at[k],
                                            recv_sem=recv_sems.at[k], device_id=to, device_id_type=MESH)

    def chip_of(j):
        return 2 * chips[j][0] + chips[j][1]

    def send():
        copy(0, part(me, c), sibling).start()
        for j in range(3):
            copy(1 + j, part(me, c), (chips[j][0], chips[j][1], c)).start()

    def pass_on():
        for j in range(3):
            copy(1 + j, part(chip_of(j), c), (chips[j][0], chips[j][1], c)).wait_recv()
            copy(4 + j, part(chip_of(j), c), sibling).start()

    def finish():
        copy(0, part(me, 1 - c), sibling).wait_recv()
        for j in range(3):
            copy(4 + j, part(chip_of(j), 1 - c), sibling).wait_recv()
        copy(0, part(me, c), sibling).wait_send()
        for j in range(3):
            copy(1 + j, part(me, c), (chips[j][0], chips[j][1], c)).wait_send()
            copy(4 + j, part(chip_of(j), c), sibling).wait_send()

    return send, pass_on, finish


def add_halves(name, g, recv, c_idx, tr):
    n, h, c = recv.shape

    def body(c_ref, g_ref, r_ref, o_ref):
        o_ref[...] = (g_ref[...] + r_ref[...]).astype(BF16)

    nb = h // tr
    return pl.pallas_call(
        body, name=name,
        grid_spec=pltpu.PrefetchScalarGridSpec(
            num_scalar_prefetch=1, grid=(n, nb),
            in_specs=[pl.BlockSpec((None, tr, c), lambda k, i, c_ref: (k, c_ref[0] * nb + i, 0)),
                      pl.BlockSpec((None, tr, c), lambda k, i, c_ref: (k, i, 0))],
            out_specs=pl.BlockSpec((None, tr, c), lambda k, i, c_ref: (k, i, 0))),
        out_shape=jax.ShapeDtypeStruct((n, h, c), BF16), compiler_params=_params("parallel", "parallel"),
    )(c_idx, g, recv)


def sum_chips(name, parts, recv, where, tr):
    n, h, c = recv.shape
    nb = h // tr

    def body(w_ref, p_ref, r_ref, o_ref):
        acc = p_ref[...].astype(F32)
        for k in range(n):
            acc = acc + r_ref[k].astype(F32)
        o_ref[...] = acc

    return pl.pallas_call(
        body, name=name,
        grid_spec=pltpu.PrefetchScalarGridSpec(
            num_scalar_prefetch=1, grid=(nb,),
            in_specs=[pl.BlockSpec((None, tr, c), lambda i, w_ref: (w_ref[0], i, 0)),
                      pl.BlockSpec((n, tr, c), lambda i, w_ref: (0, i, 0))],
            out_specs=pl.BlockSpec((tr, c), lambda i, w_ref: (w_ref[1] * nb + i, 0))),
        out_shape=jax.ShapeDtypeStruct((2 * h, c), F32), compiler_params=_params("parallel"),
    )(where, parts, recv)


def place_shard(name, shard, where, dtype, tr):
    r, c = shard.shape

    def body(w_ref, s_ref, o_ref):
        o_ref[...] = s_ref[...].astype(dtype)

    return pl.pallas_call(
        body, name=name,
        grid_spec=pltpu.PrefetchScalarGridSpec(
            num_scalar_prefetch=1, grid=(r // tr,),
            in_specs=[pl.BlockSpec((tr, c), lambda i, w_ref: (i, 0))],
            out_specs=pl.BlockSpec((None, tr, c), lambda i, w_ref: (w_ref[0], i, 0))),
        out_shape=jax.ShapeDtypeStruct((N_CHIPS, r, c), dtype), compiler_params=_params("parallel"),
    )(where, shard)


ANY = pl.BlockSpec(memory_space=pl.ANY)


def _place():
    x, y, c = lax.axis_index("x"), lax.axis_index("y"), lax.axis_index("c")
    chips = [(1 - x, y), (x, 1 - y), (1 - x, 1 - y)]
    return x, y, c, chips


def gather_shards(arrays):
    n = len(arrays)

    def body(*refs):
        send, pass_on, finish = _gather_phases(refs[n:2 * n], *refs[2 * n:], _spans(arrays))
        send()
        pass_on()
        finish()

    return pl.pallas_call(
        body, name="gather_shards", in_specs=[ANY] * n, out_specs=[ANY] * n,
        out_shape=[jax.ShapeDtypeStruct(a.shape, a.dtype) for a in _arrays(arrays)],
        input_output_aliases={w: w for w in range(n)}, scratch_shapes=_gather_sems(n),
        compiler_params=pltpu.CompilerParams(has_side_effects=True),
    )(*_arrays(arrays))


def _gather_sems(n):
    return [pltpu.SemaphoreType.DMA((6 * n,)), pltpu.SemaphoreType.DMA((6 * n,))]


class Span(typing.NamedTuple):
    array: jax.Array
    lo: int
    hi: int
    ways: tuple = (0, 1, 2)


def _arrays(gathering):
    return [g.array if isinstance(g, Span) else g for g in gathering]


def _spans(gathering):
    return [(g.lo, g.hi, g.ways) if isinstance(g, Span) else (0, g.shape[1], (0, 1, 2)) for g in gathering]


def _gather_phases(out, send_sems, recv_sems, spans):
    n = len(out)
    if not any(ways for _, _, ways in spans):
        return (lambda: None,) * 3
    x, y, c, chips = _place()
    me = 2 * x + y
    sibling = (x, y, 1 - c)

    def half(w, chip, core):
        lo, hi, _ = spans[w]
        h = (hi - lo) // 2
        return out[w].at[chip, pl.ds(lo + core * h, h)]

    def copy(k, block, to):
        return pltpu.make_async_remote_copy(src_ref=block, dst_ref=block, send_sem=send_sems.at[k],
                                            recv_sem=recv_sems.at[k], device_id=to, device_id_type=MESH)

    def over_ici(w, j, chip):
        return copy(3 * w + j, half(w, chip, c), (chips[j][0], chips[j][1], c))

    def over_d2d(w, j, core):
        return copy(3 * n + 3 * w + j, half(w, 2 * chips[j][0] + chips[j][1], core), sibling)

    pairs = [(w, j) for w in range(n) for j in spans[w][2]]

    def send():
        for w, j in pairs:
            over_ici(w, j, me).start()

    def pass_on():
        for w, j in pairs:
            over_ici(w, j, 2 * chips[j][0] + chips[j][1]).wait_recv()
            over_d2d(w, j, c).start()

    def finish():
        for w, j in pairs:
            over_d2d(w, j, 1 - c).wait_recv()
        for w, j in pairs:
            over_ici(w, j, me).wait_send()
            over_d2d(w, j, c).wait_send()

    return send, pass_on, finish


def _relay_sems():
    return [pltpu.SemaphoreType.DMA((4,)), pltpu.SemaphoreType.DMA((4,))]


def _relay_phases(out, send_sems, recv_sems):
    x, y, c, chips = _place()
    sibling = (x, y, 1 - c)
    rows = out.shape[1]
    quarter = rows // 4
    far = 2 * chips[2][0] + chips[2][1]

    def piece(chip, way, core):
        return out.at[chip, pl.ds(way * (rows // 2) + core * quarter, quarter)]

    def copy(k, block, to):
        return pltpu.make_async_remote_copy(src_ref=block, dst_ref=block, send_sem=send_sems.at[k],
                                            recv_sem=recv_sems.at[k], device_id=to, device_id_type=MESH)

    def over_ici(way, chip):
        return copy(way, piece(chip, way, c), (chips[way][0], chips[way][1], c))

    def over_d2d(way, core):
        return copy(2 + way, piece(far, way, core), sibling)

    def send():
        for way in range(2):
            other = chips[1 - way]
            over_ici(way, 2 * other[0] + other[1]).start()

    def pass_on():
        for way in range(2):
            over_ici(way, far).wait_recv()
            over_d2d(way, c).start()

    def finish():
        for way in range(2):
            over_d2d(way, 1 - c).wait_recv()
        for way in range(2):
            other = chips[1 - way]
            over_ici(way, 2 * other[0] + other[1]).wait_send()
            over_d2d(way, c).wait_send()

    return send, pass_on, finish


def swap_halves(name, grads):
    n = len(grads)

    def body(*refs):
        send, finish = _swap_phases(refs[:n], refs[n:2 * n], *refs[2 * n:])
        send()
        finish()

    return pl.pallas_call(
        body, name=name, in_specs=[ANY] * n, out_specs=[ANY] * n, out_shape=_swap_shapes(grads),
        scratch_shapes=_swap_sems(n), compiler_params=pltpu.CompilerParams(has_side_effects=True),
    )(*grads)


def _swap_shapes(grads):
    return [jax.ShapeDtypeStruct((a.shape[0], a.shape[1] // 2, a.shape[2]), a.dtype) for a in grads]


def _swap_sems(n):
    return [pltpu.SemaphoreType.DMA((n,)), pltpu.SemaphoreType.DMA((n,))]


def _swap_phases(g, out, send_sems, recv_sems):
    x, y, c, _ = _place()

    def copies():
        return [pltpu.make_async_remote_copy(
            src_ref=g[w].at[:, pl.ds((1 - c) * (g[w].shape[1] // 2), g[w].shape[1] // 2)], dst_ref=out[w],
            send_sem=send_sems.at[w], recv_sem=recv_sems.at[w], device_id=(x, y, 1 - c), device_id_type=MESH)
            for w in range(len(g))]

    def send():
        for cp in copies():
            cp.start()

    def finish():
        for cp in copies():
            cp.wait()

    return send, finish


def _send_phases(g, out, send_sems, recv_sems):
    x, y, c, _ = _place()

    def copies():
        return [pltpu.make_async_remote_copy(
            src_ref=g[w], dst_ref=out[w], send_sem=send_sems.at[w], recv_sem=recv_sems.at[w],
            device_id=(x, y, 1 - c), device_id_type=MESH) for w in range(len(g))]

    def send():
        for cp in copies():
            cp.start()

    def finish():
        for cp in copies():
            cp.wait()

    return send, finish


def dw_in_half(name, h, dz, which, sending):
    s = h.shape[0]
    hh, tb = D // 2, IN_SHARD // 2
    n = len(sending)
    steps = IN_COLS // tb

    def body(w_ref, *refs):
        a_ref, b_ref, o_ref = refs[0], refs[1], refs[2 + n]
        j = pl.program_id(0)
        if n:
            send, finish = _send_phases(refs[2:2 + n], refs[3 + n:3 + 2 * n], *refs[3 + 2 * n:])
            pl.when(j == 0)(send)
        o_ref[...] = _dot(a_ref[...], b_ref[...], TN)
        if n:
            pl.when(j == steps - 1)(finish)

    out = pl.pallas_call(
        body, name=name,
        grid_spec=pltpu.PrefetchScalarGridSpec(
            num_scalar_prefetch=1, grid=(steps,),
            in_specs=[pl.BlockSpec((s, hh), lambda j, w: (0, w[0])), pl.BlockSpec((s, tb), lambda j, w: (0, j))]
            + [ANY] * n,
            out_specs=[pl.BlockSpec((None, hh, tb), lambda j, w: (j // 2, 0, j % 2))] + [ANY] * n,
            scratch_shapes=_swap_sems(n) if n else []),
        out_shape=[jax.ShapeDtypeStruct((N_CHIPS, hh, IN_SHARD), F32)]
        + [jax.ShapeDtypeStruct(a.shape, a.dtype) for a in sending],
        compiler_params=_params("arbitrary", communicates=bool(n)),
    )(which, h, dz, *sending)
    return out[0], out[1:]


def scatter_chips(parts):
    n = len(parts)

    def body(*refs):
        send, finish = _scatter_phases(refs[:n], refs[n:2 * n], *refs[2 * n:])
        send()
        finish()

    return pl.pallas_call(
        body, name="scatter_chips", in_specs=[ANY] * n, out_specs=[ANY] * n,
        out_shape=_scatter_shapes(parts), scratch_shapes=_scatter_sems(n),
        compiler_params=pltpu.CompilerParams(has_side_effects=True),
    )(*parts)


def _scatter_shapes(parts):
    return [jax.ShapeDtypeStruct((3,) + a.shape[1:], a.dtype) for a in parts]


def _scatter_sems(n):
    return [pltpu.SemaphoreType.DMA((3 * n,)), pltpu.SemaphoreType.DMA((3 * n,))]


def _scatter_phases(p, out, send_sems, recv_sems):
    x, y, c, chips = _place()

    def copies():
        return [pltpu.make_async_remote_copy(
            src_ref=p[w].at[2 * px + py], dst_ref=out[w].at[j], send_sem=send_sems.at[3 * w + j],
            recv_sem=recv_sems.at[3 * w + j], device_id=(px, py, c), device_id_type=MESH)
            for w in range(len(p)) for j, (px, py) in enumerate(chips)]

    def send():
        for cp in copies():
            cp.start()

    def finish():
        for cp in copies():
            cp.wait()

    return send, finish


def join_halves(arrays, gathering):
    n = len(arrays)

    def body(*refs):
        out = refs[n + 1:2 * n + 1]
        send_sems, recv_sems = refs[2 * n + 2:2 * n + 4]
        send8, pass_on8, finish8 = _allgather8_phases(refs[2 * n + 1], *refs[2 * n + 4:])
        x, y, c, _ = _place()

        def copy(w, core):
            h = out[w].shape[0] // 2
            rows = out[w].at[pl.ds(core * h, h)]
            return pltpu.make_async_remote_copy(
                src_ref=rows, dst_ref=rows, send_sem=send_sems.at[w], recv_sem=recv_sems.at[w],
                device_id=(x, y, 1 - c), device_id_type=MESH)

        send8()
        for w in range(n):
            copy(w, c).start()
        pass_on8()
        for w in range(n):
            copy(w, 1 - c).wait_recv()
        finish8()
        for w in range(n):
            copy(w, c).wait_send()

    res = pl.pallas_call(
        body, name="join_halves", in_specs=[ANY] * (n + 1), out_specs=[ANY] * (n + 1),
        out_shape=[jax.ShapeDtypeStruct(a.shape, a.dtype) for a in list(arrays) + [gathering]],
        input_output_aliases={w: w for w in range(n + 1)},
        scratch_shapes=[pltpu.SemaphoreType.DMA((n,)), pltpu.SemaphoreType.DMA((n,)),
                        pltpu.SemaphoreType.DMA((7,)), pltpu.SemaphoreType.DMA((7,))],
        compiler_params=pltpu.CompilerParams(has_side_effects=True),
    )(*arrays, gathering)
    return res[:n], res[n]


def allreduce_small(packed):
    r, c = packed.shape
    n_dev = 8

    def body(x_ref, all_ref, sum_ref, send_sems, recv_sems, local_sem):
        x, y, cc, chips = _place()
        me, sibling = (x, y, cc), (x, y, 1 - cc)

        def rows(px, py, pc):
            return all_ref.at[4 * px + 2 * py + pc]

        def copy(k, block, to, src=None):
            return pltpu.make_async_remote_copy(
                src_ref=rows(*block) if src is None else src, dst_ref=rows(*block), send_sem=send_sems.at[k],
                recv_sem=recv_sems.at[k], device_id=to, device_id_type=MESH)

        mine = pltpu.make_async_copy(x_ref, rows(*me), local_sem)
        mine.start()
        first = [copy(0, me, sibling, src=x_ref)]
        first += [copy(1 + j, me, (*chip, cc), src=x_ref) for j, chip in enumerate(chips)]
        for cp in first:
            cp.start()
        passed = [copy(4 + j, (*chip, cc), sibling) for j, chip in enumerate(chips)]
        for j, chip in enumerate(chips):
            copy(1 + j, (*chip, cc), me).wait_recv()
            passed[j].start()
        copy(0, sibling, me).wait_recv()
        for j, chip in enumerate(chips):
            copy(4 + j, (*chip, 1 - cc), me).wait_recv()
        for cp in first + passed:
            cp.wait_send()
        mine.wait()
        acc = all_ref[0]
        for k in range(1, n_dev):
            acc = acc + all_ref[k]
        sum_ref[...] = acc

    vm = pl.BlockSpec(memory_space=pltpu.VMEM)
    return pl.pallas_call(
        body, name="allreduce_small", in_specs=[vm], out_specs=[vm, vm],
        out_shape=[jax.ShapeDtypeStruct((n_dev, r, c), F32), jax.ShapeDtypeStruct((r, c), F32)],
        scratch_shapes=[pltpu.SemaphoreType.DMA((7,)), pltpu.SemaphoreType.DMA((7,)), pltpu.SemaphoreType.DMA],
        compiler_params=pltpu.CompilerParams(has_side_effects=True, vmem_limit_bytes=VMEM_LIMIT),
    )(packed)[1]


def local_step(x, target, vecs, w_s, bs_t, bg, wg_in, late, core=None, order=None, where=None):
    on_mesh = core is not None

    def add(names, grads, recv):
        return [add_halves("add_" + n, g, r, core, min(r.shape[1], 256)) for n, g, r in zip(names, grads, recv)]

    g_pre, ln_g, ln_b, g_post, g_fpre, g_fpost = vecs
    s = x.shape[0]
    if order is None:
        order = jnp.arange(N_CHIPS, dtype=jnp.int32)
    logc = _attn_tables(s)
    ka, kb = _alibi_tables(s)

    h = norm_pre(x, g_pre)
    if not on_mesh:
        wg_a, wg_b, wg_out, wg_ff1, wg_ff2 = late
    if on_mesh:
        cut = D // 4
        z, wg_in, (wg_a, wg_b, wg_out, wg_ff1, wg_ff2) = mm_in(h, wg_in, order, True, late)
        ya, (wg_b, wg_ff2) = gating_fwd(z, ln_g, ln_b, w_s, bs_t, [wg_b, Span(wg_ff2, 0, cut)])
        yb, lse, (wg_a, wg_ff1, wg_ff2, bg) = attn_fwd(
            z, logc, ka, kb, [wg_a, wg_ff1, Span(wg_ff2, cut, 3 * cut), bg])
        bg = jnp.transpose(bg[:, :2, :], (1, 0, 2)).reshape(2, D)
    else:
        z, _, _ = mm_in(h, wg_in, order, False)
        ya, _ = gating_fwd(z, ln_g, ln_b, w_s, bs_t, [])
        yb, lse, _ = attn_fwd(z, logc, ka, kb, [])
    merged, pa, pb, got = proj_merge(ya, yb, wg_a.reshape(D, D), wg_b.reshape(D, D), z, bg, [wg_out] if on_mesh else [])
    wg_out = got[0] if on_mesh else wg_out
    w_out = wg_out.reshape(D, D)
    o, x1, h2, got = out_norm(merged, w_out, x, g_post, g_fpre, [Span(wg_ff2, 3 * D // 4, D)] if on_mesh else [])
    a, rl, _ = mm_ff1(h2, wg_ff1, [])
    w_ff2 = (got[0] if on_mesh else wg_ff2).reshape(D_FF, D)
    dy, df, d_gfpost, loss = ff2_loss(rl, w_ff2, x1, target, g_fpost)

    half_cols = pl.BlockSpec((D, D // 2), lambda i, j: (0, j))
    d_wff2 = mm_tn("dw_ff2", rl, df, D // 2, D, (D_FF, D), pl.BlockSpec((D // 2, D), lambda i, j: (i, 0)))
    da = ff2_bwd(df, w_ff2, a)
    d_wff1 = mm_tn("dw_ff1", h2, da, D, D // 2, (N_CHIPS, D, D),
                   pl.BlockSpec((None, D, D // 2), lambda i, j: (j // 2, 0, j % 2)))
    d_ff = [d_wff1, d_wff2.reshape(N_CHIPS, D, D)]
    dx1, do, d_gfpre, d_gpost, recv_ff = ff1_bwd_norms(da, wg_ff1, x1, o, dy, g_fpre, g_post, d_ff if on_mesh else [])
    d_wout = mm_tn("dw_out", merged, do, D, D // 2, (D, D), half_cols)
    dpa, dpb, dga, dgb, d_bg = out_bwd_gates(do, w_out, pa, pb, z, bg)
    d_wa = mm_tn("dw_a", ya, dpa, D, D // 2, (D, D), half_cols)
    d_wb = mm_tn("dw_b", yb, dpb, D, D // 2, (D, D), half_cols)
    dyb = mm_nt("dy_b", dpb, wg_b.reshape(D, D))
    d_proj = [d_wa.reshape(N_CHIPS, D // N_CHIPS, D), d_wb.reshape(N_CHIPS, D // N_CHIPS, D),
              d_wout.reshape(N_CHIPS, D // N_CHIPS, D)]
    du, dv, d_ws, d_bs, d_lng, d_lnb, recv_proj = gating_bwd(
        z, dpa, wg_a.reshape(D, D), ln_g, ln_b, w_s, bs_t, d_proj if on_mesh else [])
    early = d_proj + d_ff
    parts_early = add(BIG[1:], early, list(recv_proj) + list(recv_ff)) if on_mesh else []
    small = dict(b_gate=d_bg, ln_v_g=d_lng, ln_v_b=d_lnb, w_s=d_ws, b_s=d_bs[:, 0, :],
                 norm_mix_post=d_gpost, norm_ffn_pre=d_gfpre, norm_ffn_post=d_gfpost)
    packed = pack_small(dict(small, norm_mix_pre=jnp.zeros((1, D), F32)), loss, where) if on_mesh else None
    dq, dk, dvb, got_early, packed = attn_bwd(z, yb, dyb, lse, logc, ka, kb, parts_early, packed)
    dz = jnp.concatenate([du, dv, dq, dk, dvb, dga, dgb], axis=1)
    if on_mesh:
        for_sibling, _ = dw_in_half("dw_in_sibling", h, dz, 1 - core, [])
        mine, from_sibling = dw_in_half("dw_in_mine", h, dz, core, [for_sibling])
        d_win = None
        parts_late = [add_halves("add_w_in", mine, from_sibling[0], jnp.zeros((1,), jnp.int32), 256)]
    else:
        half = IN_SHARD // 2
        d_win = mm_tn("dw_in", h, dz, D, half, (N_CHIPS, D, IN_SHARD),
                      pl.BlockSpec((None, D, half), lambda i, j: (j // 2, 0, j % 2)))
        parts_late = []
    dx, d_gpre, got_late, _ = in_bwd_norm(dz, wg_in, x, dx1, g_pre, parts_late)
    small["norm_mix_pre"] = d_gpre
    return (loss[0, 0], dx, [d_win] + early, small, parts_late + parts_early, list(got_late) + list(got_early),
            packed)


BIG = ("w_in", "w_a_proj", "w_b_proj", "w_out", "w_ff1", "w_ff2")
SMALL = ("norm_mix_pre", "ln_v_g", "ln_v_b", "b_s", "norm_mix_post", "norm_ffn_pre", "norm_ffn_post", "w_s", "b_gate")
ORDER = ("norm_mix_pre", "w_in", "b_gate", "ln_v_g", "ln_v_b", "w_s", "b_s", "w_a_proj", "w_b_proj", "w_out",
         "norm_mix_post", "norm_ffn_pre", "w_ff1", "w_ff2", "norm_ffn_post")
VEC_ROWS = D // 128
WS_ROW = 7 * VEC_ROWS
BG_ROW = WS_ROW + GROUPS * CHUNK
LOSS_ROW = BG_ROW + 2 * VEC_ROWS
PACK_ROWS = LOSS_ROW + 8


def pack_small(small, loss, where):
    vectors = [small[n] for n in SMALL[:7]]
    operands = vectors + [small["w_s"], small["b_gate"], loss]

    def body(where_ref, *refs):
        out = refs[-1]
        ws_ref, bg_ref, loss_ref = refs[7:10]
        for i, n in enumerate(SMALL[:7]):
            if n == "b_s":
                out[i * VEC_ROWS:(i + 1) * VEC_ROWS, :] = refs[i][...]
            else:
                for j in range(VEC_ROWS):
                    out[i * VEC_ROWS + j:i * VEC_ROWS + j + 1, :] = refs[i][:, j * 128:(j + 1) * 128]
        for g in range(GROUPS):
            out[WS_ROW + g * CHUNK:WS_ROW + (g + 1) * CHUNK, :] = ws_ref[g]
        for r in range(2):
            for j in range(VEC_ROWS):
                row = BG_ROW + r * VEC_ROWS + j
                out[row:row + 1, :] = bg_ref[r:r + 1, j * 128:(j + 1) * 128]
        lane = lax.broadcasted_iota(jnp.int32, (8, 128), 1)
        sub = lax.broadcasted_iota(jnp.int32, (8, 128), 0)
        out[LOSS_ROW:LOSS_ROW + 8, :] = jnp.where((lane == 0) & (sub == 0), loss_ref[...], 0.0)

    return pl.pallas_call(
        body, name="pack_small",
        grid_spec=pltpu.PrefetchScalarGridSpec(
            num_scalar_prefetch=1, grid=(1,), in_specs=[_full(a.shape) for a in operands],
            out_specs=pl.BlockSpec((None, PACK_ROWS, 128), lambda i, w: (w[0], w[1], 0))),
        out_shape=jax.ShapeDtypeStruct((N_CHIPS, 2 * PACK_ROWS, 128), F32), compiler_params=_params("arbitrary"),
    )(where, *operands)


def pack_vector(vec, where):
    def body(where_ref, v_ref, out):
        for j in range(VEC_ROWS):
            out[j:j + 1, :] = v_ref[:, j * 128:(j + 1) * 128]

    return pl.pallas_call(
        body, name="pack_vector",
        grid_spec=pltpu.PrefetchScalarGridSpec(
            num_scalar_prefetch=1, grid=(1,), in_specs=[_full(vec.shape)],
            out_specs=pl.BlockSpec((None, VEC_ROWS, 128), lambda i, w: (w[0], w[1], 0))),
        out_shape=jax.ShapeDtypeStruct((N_CHIPS, 2 * VEC_ROWS, 128), F32), compiler_params=_params("arbitrary"),
    )(where, vec)


def adamw_small(gathered, first, chip, w, m, v):
    shapes = {n: (1, D) for n in SMALL}
    shapes.update(b_s=(GROUPS, CHUNK), w_s=(GROUPS * CHUNK, CHUNK), b_gate=(2, D // N_CHIPS))
    flat = lambda t: [t[n].reshape(shapes[n]) for n in SMALL]
    per = D // N_CHIPS // 128

    def body(chip_ref, all_ref, first_ref, *refs):
        params, outs = refs[:27], refs[27:]
        sub = lax.broadcasted_iota(jnp.int32, (VEC_ROWS, 128), 0)
        sum_ref = outs[36]
        total = all_ref[0, 0:PACK_ROWS, :]
        head = first_ref[0, 0:VEC_ROWS, :]
        for k in range(1, 2 * N_CHIPS):
            total = total + all_ref[k // 2, (k % 2) * PACK_ROWS:(k % 2 + 1) * PACK_ROWS, :]
            head = head + first_ref[k // 2, (k % 2) * VEC_ROWS:(k % 2 + 1) * VEC_ROWS, :]
        sum_ref[...] = total
        sum_ref[0:VEC_ROWS, :] = head

        def gate_row(r):
            rows = sum_ref[BG_ROW + r * VEC_ROWS:BG_ROW + (r + 1) * VEC_ROWS, :]
            return jnp.concatenate([jnp.sum(jnp.where(sub == per * chip_ref[0] + j, rows, 0.0), axis=0, keepdims=True)
                                    for j in range(per)], axis=1)

        for i, n in enumerate(SMALL):
            if n == "b_s":
                g = sum_ref[i * VEC_ROWS:(i + 1) * VEC_ROWS, :]
            elif n == "w_s":
                g = sum_ref[WS_ROW:BG_ROW, :]
            elif n == "b_gate":
                g = jnp.concatenate([gate_row(0), gate_row(1)], axis=0)
            else:
                g = jnp.concatenate([sum_ref[i * VEC_ROWS + j:i * VEC_ROWS + j + 1, :] for j in range(VEC_ROWS)],
                                    axis=1)
            delta, nm, nv = _adamw_math(params[i][...], g, params[9 + i][...], params[18 + i][...])
            outs[4 * i][...], outs[4 * i + 1][...], outs[4 * i + 2][...], outs[4 * i + 3][...] = g, delta, nm, nv

    vm = pl.BlockSpec(memory_space=pltpu.VMEM)
    res = pl.pallas_call(
        body, name="adamw_small",
        in_specs=[pl.BlockSpec(memory_space=pltpu.SMEM)] + [vm] * 29, out_specs=[vm] * 37,
        out_shape=[jax.ShapeDtypeStruct(shapes[n], F32) for n in SMALL for _ in range(4)]
        + [jax.ShapeDtypeStruct((PACK_ROWS, 128), F32)],
        compiler_params=_params(),
    )(chip, gathered, first, *flat(w), *flat(m), *flat(v))
    new = {n: tuple(r.reshape(w[n].shape) for r in res[4 * i:4 * i + 4]) for i, n in enumerate(SMALL)}
    return new, res[36][LOSS_ROW, 0]


def kernel(x, norm_mix_pre, w_in, b_gate, ln_v_g, ln_v_b, w_s, b_s, w_a_proj, w_b_proj, w_out, norm_mix_post, norm_ffn_pre, w_ff1, w_ff2, norm_ffn_post, loss_target, m_norm_mix_pre, m_w_in, m_b_gate, m_ln_v_g, m_ln_v_b, m_w_s, m_b_s, m_w_a_proj, m_w_b_proj, m_w_out, m_norm_mix_post, m_norm_ffn_pre, m_w_ff1, m_w_ff2, m_norm_ffn_post, v_norm_mix_pre, v_w_in, v_b_gate, v_ln_v_g, v_ln_v_b, v_w_s, v_b_s, v_w_a_proj, v_w_b_proj, v_w_out, v_norm_mix_post, v_norm_ffn_pre, v_w_ff1, v_w_ff2, v_norm_ffn_post):
    w = dict(norm_mix_pre=norm_mix_pre, w_in=w_in, b_gate=b_gate, ln_v_g=ln_v_g, ln_v_b=ln_v_b, w_s=w_s, b_s=b_s,
             w_a_proj=w_a_proj, w_b_proj=w_b_proj, w_out=w_out, norm_mix_post=norm_mix_post,
             norm_ffn_pre=norm_ffn_pre, w_ff1=w_ff1, w_ff2=w_ff2, norm_ffn_post=norm_ffn_post)
    m = dict(norm_mix_pre=m_norm_mix_pre, w_in=m_w_in, b_gate=m_b_gate, ln_v_g=m_ln_v_g, ln_v_b=m_ln_v_b, w_s=m_w_s,
             b_s=m_b_s, w_a_proj=m_w_a_proj, w_b_proj=m_w_b_proj, w_out=m_w_out, norm_mix_post=m_norm_mix_post,
             norm_ffn_pre=m_norm_ffn_pre, w_ff1=m_w_ff1, w_ff2=m_w_ff2, norm_ffn_post=m_norm_ffn_post)
    v = dict(norm_mix_pre=v_norm_mix_pre, w_in=v_w_in, b_gate=v_b_gate, ln_v_g=v_ln_v_g, ln_v_b=v_ln_v_b, w_s=v_w_s,
             b_s=v_b_s, w_a_proj=v_w_a_proj, w_b_proj=v_w_b_proj, w_out=v_w_out, norm_mix_post=v_norm_mix_post,
             norm_ffn_pre=v_norm_ffn_pre, w_ff1=v_w_ff1, w_ff2=v_w_ff2, norm_ffn_post=v_norm_ffn_post)
    chip = 2 * lax.axis_index("x") + lax.axis_index("y")
    core = lax.axis_index("c")

    where = jnp.stack([chip, core]).astype(jnp.int32)
    wg_in = place_shard("place_w_in", w_in[0], where, BF16, 256)
    bg_all = place_shard("place_b_gate", jnp.pad(b_gate[0], ((0, 14), (0, 0))), where, F32, 16)
    vecs = (norm_mix_pre, ln_v_g, ln_v_b, norm_mix_post, norm_ffn_pre, norm_ffn_post)
    loss, dx, _, small, parts, got, packed = local_step(
        x[0], loss_target[0], vecs, w_s[0], b_s[0].T, bg_all, wg_in, [w[n][0] for n in BIG[1:]],
        core=jnp.reshape(core, (1,)).astype(jnp.int32),
        order=jnp.stack([chip, chip ^ 2, chip ^ 1, chip ^ 3]).astype(jnp.int32), where=where)

    halves = [sum_chips("sum_" + n, p, r, where, min(p.shape[1], 256)) for n, p, r in zip(BIG, parts, got)]
    joined, first = join_halves(halves, pack_vector(small["norm_mix_pre"], where))
    grads = dict(zip(BIG, joined))
    new = {}
    for n in BIG:
        shape = w[n].shape
        res = adamw("adamw_" + n, w[n][0], grads[n], m[n][0], v[n][0], min(shape[1], 256))
        new[n] = tuple(r.reshape(shape) for r in res)
    small_new, loss = adamw_small(packed, first, jnp.reshape(chip, (1,)).astype(jnp.int32), w, m, v)
    new.update(small_new)

    outs = [loss, dx[None]]
    for i in range(4):
        outs += [new[n][i] for n in ORDER]
    return tuple(outs)
```

```python
import functools
import math
import typing

import numpy as np
import jax
import jax.numpy as jnp
from jax import lax
from jax.experimental import pallas as pl
from jax.experimental.pallas import tpu as pltpu

F32 = jnp.float32
BF16 = jnp.bfloat16
MESH = pl.DeviceIdType.MESH

D = 1024
EPS = 1e-6
CHUNK = 128
GROUPS = 8
HEADS = 16
HEAD_DIM = 64
ATT_T = 256
ATT_GROUP = 8
ATT_BWD_GROUP = 4
N_CHIPS = 4
D_FF = 4 * D
IN_COLS = 7 * D
IN_SHARD = IN_COLS // N_CHIPS
MASKED = -1e30
VMEM_LIMIT = 56 * 2 ** 20

ADAM_LR, ADAM_B1, ADAM_B2, ADAM_EPS, ADAM_WD, ADAM_STEP = 0.001, 0.9, 0.999, 1e-08, 0.01, 10

NN = (((1,), (0,)), ((), ()))
NT = (((1,), (1,)), ((), ()))
TN = (((0,), (0,)), ((), ()))


def _dot(a, b, dims=NN):
    return lax.dot_general(a, b, dims, preferred_element_type=F32)


def _params(*sem, communicates=False):
    return pltpu.CompilerParams(dimension_semantics=sem or None, vmem_limit_bytes=VMEM_LIMIT,
                                has_side_effects=communicates)


def _rows(tr, c, col=0):
    return pl.BlockSpec((tr, c), lambda i: (i, col))


def _full(shape):
    n = len(shape)
    return pl.BlockSpec(shape, lambda *_: (0,) * n)


def _gelu(x):
    k = math.sqrt(2.0 / math.pi)
    return 0.5 * x * (1.0 + jnp.tanh(k * (x + 0.044715 * x * x * x)))


def _gelu_and_grad(x):
    k = math.sqrt(2.0 / math.pi)
    t = jnp.tanh(k * (x + 0.044715 * x * x * x))
    g = 0.5 * x * (1.0 + t)
    dg = 0.5 * (1.0 + t) + 0.5 * x * (1.0 - t * t) * (k * (1.0 + 3.0 * 0.044715 * x * x))
    return g, dg


def _sigmoid(x):
    return 1.0 / (1.0 + jnp.exp(-x))


def _rms(x):
    r = lax.rsqrt(jnp.mean(x * x, axis=-1, keepdims=True) + EPS)
    return x * r, r


def _rms_bwd(dn, xhat, r):
    return r * (dn - xhat * jnp.mean(dn * xhat, axis=-1, keepdims=True))


def norm_pre(x, g):
    s = x.shape[0]
    tr = 512

    def body(x_ref, g_ref, h_ref):
        xhat, _ = _rms(x_ref[...])
        h_ref[...] = (xhat * g_ref[...]).astype(BF16)

    return pl.pallas_call(
        body, name="norm_pre", grid=(s // tr,),
        in_specs=[_rows(tr, D), _full((1, D))], out_specs=_rows(tr, D),
        out_shape=jax.ShapeDtypeStruct((s, D), BF16), compiler_params=_params("parallel"),
    )(x, g)


def mm_in(h, wg, order, staged, casting=()):
    s = h.shape[0]
    tm, tn = 1024, IN_SHARD // 2
    per = IN_SHARD // tn
    m = len(casting)
    nj, ni = N_CHIPS * per, s // tm
    cast_steps = per * ni

    def body(order_ref, *refs):
        a_ref = refs[0]
        cast_in = refs[2:2 + m]
        o_ref, held = refs[2 + m], refs[3 + m]
        cast_out = refs[4 + m:4 + 2 * m]
        tile, tile_sem = refs[4 + 2 * m:6 + 2 * m]
        sems = refs[6 + 2 * m:]
        j, i = pl.program_id(0), pl.program_id(1)

        @pl.when(j * ni + i < cast_steps)
        def _():
            for src, dst in zip(cast_in, cast_out):
                dst[...] = src[...].astype(BF16)

        def fetch(t):
            chip = order_ref[t // per]
            return pltpu.make_async_copy(held.at[chip, :, pl.ds((t % per) * tn, tn)], tile.at[t % 2],
                                         tile_sem.at[t % 2])

        if staged:
            near = _gather_phases([held], *sems[:2], [(0, D, (0, 1))])
            far = _relay_phases(held, *sems[2:])

        @pl.when(i == 0)
        def _():
            @pl.when(j == 0)
            def _():
                if staged:
                    near[0]()
                fetch(0).start()

            fetch(j).wait()
            ahead = j + 1 < nj
            if staged:
                ahead = ahead & (j + 1 != per) & (j + 1 != 3 * per)
            pl.when(ahead)(lambda: fetch(j + 1).start())

        rows = pl.ds(pl.multiple_of(i * tm, tm), tm)
        o_ref[...] = _dot(a_ref[rows, :], tile[j % 2]).astype(BF16)

        if staged:
            @pl.when((i == ni - 1) & (j == per - 1))
            def _():
                near[1]()
                near[2]()
                far[0]()
                fetch(per).start()

            @pl.when((i == ni - 1) & (j == 3 * per - 1))
            def _():
                far[1]()
                far[2]()
                fetch(3 * per).start()

    def cast_block(j, i, o):
        return jnp.minimum(j * ni + i, cast_steps - 1)

    out = pl.pallas_call(
        body, name="mm_in",
        grid_spec=pltpu.PrefetchScalarGridSpec(
            num_scalar_prefetch=1, grid=(nj, ni),
            in_specs=[pl.BlockSpec((s, D), lambda j, i, o: (0, 0)), ANY]
            + [pl.BlockSpec((a.shape[0] // cast_steps, a.shape[1]), lambda j, i, o: (cast_block(j, i, o), 0))
               for a in casting],
            out_specs=[pl.BlockSpec((tm, tn), lambda j, i, o: (i, o[j // per] * per + j % per)), ANY]
            + [pl.BlockSpec((None, a.shape[0] // cast_steps, a.shape[1]),
                            lambda j, i, o: (o[0], cast_block(j, i, o), 0)) for a in casting],
            scratch_shapes=[pltpu.VMEM((2, D, tn), BF16), pltpu.SemaphoreType.DMA((2,))]
            + (_gather_sems(1) + _relay_sems() if staged else [])),
        out_shape=[jax.ShapeDtypeStruct((s, IN_COLS), BF16), jax.ShapeDtypeStruct(wg.shape, wg.dtype)]
        + [jax.ShapeDtypeStruct((N_CHIPS,) + a.shape, BF16) for a in casting],
        input_output_aliases={2: 1},
        compiler_params=_params("arbitrary", "arbitrary", communicates=staged),
    )(order, h, wg, *casting)
    return out[0], out[1], out[2:]


def _tril_ws(ws_ref, g):
    r = lax.broadcasted_iota(jnp.int32, (CHUNK, CHUNK), 0)
    c = lax.broadcasted_iota(jnp.int32, (CHUNK, CHUNK), 1)
    return jnp.where(c <= r, ws_ref[g], 0.0).astype(BF16)


def _layer_norm(v):
    mu = jnp.mean(v, axis=-1, keepdims=True)
    d = v - mu
    rstd = lax.rsqrt(jnp.mean(d * d, axis=-1, keepdims=True) + EPS)
    return d * rstd, rstd


def gating_fwd(z, ln_g, ln_b, w_s, bs_t, gathering):
    s = z.shape[0]
    n = len(gathering)
    steps = s // CHUNK

    def body(*refs):
        u_ref, v_ref, lg_ref, lb_ref, ws_ref, bst_ref = refs[:6]
        ya_ref = refs[6 + n]
        ci = pl.program_id(0)
        if n:
            send, pass_on, finish = _gather_phases(refs[7 + n:7 + 2 * n], *refs[7 + 2 * n:], _spans(gathering))
            pl.when(ci == 0)(send)
        ug = _gelu(u_ref[...].astype(F32))
        vhat, _ = _layer_norm(_gelu(v_ref[...].astype(F32)))
        vn = (vhat * lg_ref[...] + lb_ref[...]).astype(BF16)
        for g in range(GROUPS):
            cols = slice(g * CHUNK, (g + 1) * CHUNK)
            mixed = _dot(_tril_ws(ws_ref, g), vn[:, cols]) + bst_ref[:, g:g + 1]
            ya_ref[:, cols] = (ug[:, cols] * mixed).astype(BF16)
        if n:
            pl.when(ci == steps - 1)(pass_on)
            pl.when(ci == steps - 1)(finish)

    out = pl.pallas_call(
        body, name="gating_fwd", grid=(steps,),
        in_specs=[_rows(CHUNK, D, 0), _rows(CHUNK, D, 1), _full((1, D)), _full((1, D)),
                  _full((GROUPS, CHUNK, CHUNK)), _full((CHUNK, GROUPS))] + [ANY] * n,
        out_specs=[_rows(CHUNK, D)] + [ANY] * n,
        out_shape=[jax.ShapeDtypeStruct((s, D), BF16)]
        + [jax.ShapeDtypeStruct(a.shape, a.dtype) for a in _arrays(gathering)],
        input_output_aliases={6 + w: 1 + w for w in range(n)},
        scratch_shapes=_gather_sems(n) if n else [],
        compiler_params=_params("arbitrary", communicates=bool(n)),
    )(z, z, ln_g, ln_b, w_s, bs_t, *_arrays(gathering))
    return out[0], out[1:]


def _attn_tables(s):
    nd = s // ATT_T
    r = np.arange(ATT_T)[None, :, None]
    c = np.arange(ATT_T)[None, None, :]
    delta = np.arange(nd)[:, None, None] * ATT_T + r - c
    count = np.zeros(delta.shape, np.int64)
    for window, dilation in ((128, 1), (512, 4), (2048, 16)):
        count += (delta >= 0) & (delta % dilation == 0) & (delta <= window)
    logc = np.where(count > 0, np.log(np.maximum(count, 1)), MASKED)
    return jnp.asarray(logc, F32)


AUG = 3


def _split3_np(x):
    terms, rest = [], np.asarray(x, np.float64)
    for _ in range(AUG):
        term = np.asarray(rest.astype(jnp.bfloat16), np.float64)
        terms.append(term)
        rest = rest - term
    return terms


def _split3(x):
    terms, rest = [], x
    for _ in range(AUG):
        term = rest.astype(BF16).astype(F32)
        terms.append(term)
        rest = rest - term
    return terms


def _alibi_tables(s):
    nb = s // ATT_T
    slopes = np.exp2(-8.0 * np.arange(1, HEADS + 1, dtype=np.float64) / HEADS)
    ka = np.zeros((HEADS // 2, 2, ATT_T, 128), np.float32)
    kb = np.zeros((HEADS // 2, 2, nb, 128), np.float32)
    for p in range(HEADS // 2):
        for e in range(2):
            base = HEAD_DIM * (1 - e)
            for a, term in enumerate(_split3_np(slopes[2 * p + e] * np.arange(ATT_T))):
                ka[p, e, :, base + a] = term
            for a, term in enumerate(_split3_np(slopes[2 * p + e] * ATT_T * np.arange(nb))):
                kb[p, e, :, base + AUG + a] = term
            ka[p, e, :, base + 2 * AUG:base + 3 * AUG] = 1.0
    return jnp.asarray(ka), jnp.asarray(kb)


def _head_masks():
    lane = lax.broadcasted_iota(jnp.int32, (1, 128), 1)
    first = lane < HEAD_DIM

    def ones(e, n):
        base = HEAD_DIM * (1 - e)
        return ((lane >= base) & (lane < base + n)).astype(F32)

    return first, lane, ones


def _place3(lane, at, terms, other):
    for a, term in enumerate(terms):
        other = jnp.where(lane == at + a, term, other)
    return other


def attn_fwd(z, logc, ka, kb, gathering):
    s = z.shape[0]
    nq = s // ATT_T
    t = ATT_T
    n = len(gathering)
    grp = ATT_GROUP
    ngrp = HEADS // 2 // grp
    wide = 128 * grp
    qcol, kcol, vcol = 2 * D // wide, 3 * D // wide, 4 * D // wide

    def body(*refs):
        q_ref, k_ref, v_ref, lc_ref, ka_ref, kb_ref = refs[:6]
        y_ref, lse_ref = refs[6 + n:8 + n]
        q_s, k_s, v_s, m_s, l_s, acc_s = refs[8 + 2 * n:14 + 2 * n]
        gi, qi = pl.program_id(0), pl.program_id(1)
        first, lane, ones = _head_masks()
        if n:
            send, pass_on, finish = _gather_phases(refs[8 + n:8 + 2 * n], *refs[14 + 2 * n:], _spans(gathering))
            pl.when((gi == 0) & (qi == 0))(send)

        @pl.when(qi == 0)
        def _():
            sel = jnp.broadcast_to(first.astype(F32), (t, 128))
            for pr in range(grp):
                cols = slice(pr * 128, (pr + 1) * 128)
                for jb in range(nq):
                    kj = k_ref[jb * t:(jb + 1) * t, cols].astype(F32)
                    vj = v_ref[jb * t:(jb + 1) * t, cols].astype(F32)
                    k_s[pr, 0, jb] = jnp.where(first, kj, ka_ref[pr, 0] + kb_ref[pr, 0, jb:jb + 1, :]).astype(BF16)
                    k_s[pr, 1, jb] = jnp.where(first, ka_ref[pr, 1] + kb_ref[pr, 1, jb:jb + 1, :], kj).astype(BF16)
                    v_s[pr, jb, 0:t, 0:128] = jnp.where(first, vj, 0.0).astype(BF16)
                    v_s[pr, jb, t:2 * t, 0:128] = jnp.where(first, 0.0, vj).astype(BF16)
                    v_s[pr, jb, 0:t, 128:256] = sel.astype(BF16)
                    v_s[pr, jb, t:2 * t, 128:256] = (1.0 - sel).astype(BF16)

        for pr in range(grp):
            q = q_ref[:, pr * 128:(pr + 1) * 128].astype(F32) * (1.0 / math.sqrt(HEAD_DIM))
            q_s[pr, 0] = jnp.where(first, q, ones(0, 2 * AUG)).astype(BF16)
            q_s[pr, 1] = jnp.where(first, ones(1, 2 * AUG), q).astype(BF16)
        m_s[...] = jnp.full_like(m_s, MASKED)
        l_s[...] = jnp.zeros_like(l_s)
        acc_s[...] = jnp.zeros_like(acc_s)

        def scores(j):
            return tuple(_dot(q_s[pr, e], k_s[pr, e, j], NT) for pr in range(grp) for e in range(2))

        def step(j, carry):
            softmax_block(j, scores(j))
            return carry

        def softmax_block(j, u):
            lc = lc_ref[qi - j]
            for pr in range(grp):
                u0 = u[2 * pr] + lc
                u1 = u[2 * pr + 1] + lc
                m0, m1 = m_s[pr, 0], m_s[pr, 1]
                n0 = jnp.maximum(m0, jnp.max(u0, axis=-1, keepdims=True))
                n1 = jnp.maximum(m1, jnp.max(u1, axis=-1, keepdims=True))
                m_s[pr, 0], m_s[pr, 1] = n0, n1
                p = jnp.concatenate([jnp.exp(u0 - jnp.concatenate([n0, n0], axis=1)).astype(BF16),
                                     jnp.exp(u1 - jnp.concatenate([n1, n1], axis=1)).astype(BF16)], axis=1)
                pv = _dot(p, v_s[pr, j])
                alpha = jnp.where(first, jnp.exp(m0 - n0), jnp.exp(m1 - n1))
                acc_s[pr] = acc_s[pr] * alpha + pv[:, 0:128]
                l_s[pr] = l_s[pr] * alpha + pv[:, 128:256]

        lax.fori_loop(0, qi + 1, step, 0)
        for pr in range(grp):
            cols = slice(pr * 128, (pr + 1) * 128)
            y_ref[:, cols] = (acc_s[pr] / l_s[pr]).astype(BF16)
            lse_ref[:, cols] = jnp.where(first, m_s[pr, 0], m_s[pr, 1]) + jnp.log(l_s[pr])
        if n:
            pl.when((gi == ngrp - 1) & (qi == nq - 1))(pass_on)
            pl.when((gi == ngrp - 1) & (qi == nq - 1))(finish)

    out = pl.pallas_call(
        body, name="attn_fwd", grid=(ngrp, nq),
        in_specs=[pl.BlockSpec((t, wide), lambda g, i: (i, qcol + g)),
                  pl.BlockSpec((s, wide), lambda g, i: (0, kcol + g)),
                  pl.BlockSpec((s, wide), lambda g, i: (0, vcol + g)),
                  _full((nq, t, t)),
                  pl.BlockSpec((grp, 2, t, 128), lambda g, i: (g, 0, 0, 0)),
                  pl.BlockSpec((grp, 2, nq, 128), lambda g, i: (g, 0, 0, 0))] + [ANY] * n,
        out_specs=[pl.BlockSpec((t, wide), lambda g, i: (i, g)), pl.BlockSpec((t, wide), lambda g, i: (i, g))]
        + [ANY] * n,
        out_shape=[jax.ShapeDtypeStruct((s, D), BF16), jax.ShapeDtypeStruct((s, D), F32)]
        + [jax.ShapeDtypeStruct(a.shape, a.dtype) for a in _arrays(gathering)],
        input_output_aliases={6 + w: 2 + w for w in range(n)},
        scratch_shapes=[pltpu.VMEM((grp, 2, t, 128), BF16), pltpu.VMEM((grp, 2, nq, t, 128), BF16),
                        pltpu.VMEM((grp, nq, 2 * t, 256), BF16), pltpu.VMEM((grp, 2, t, 128), F32),
                        pltpu.VMEM((grp, t, 128), F32), pltpu.VMEM((grp, t, 128), F32)]
        + (_gather_sems(n) if n else []),
        compiler_params=_params("arbitrary", "arbitrary", communicates=bool(n)),
    )(z, z, z, logc, ka, kb, *_arrays(gathering))
    return out[0], out[1], out[2:]


def proj_merge(ya, yb, wa, wb, z, bg, gathering):
    s = ya.shape[0]
    tm = 512
    n = len(gathering)
    steps = s // tm

    def body(*refs):
        ya_ref, yb_ref, wa_ref, wb_ref, ga_ref, gb_ref, bg_ref = refs[:7]
        mg_ref, pa_ref, pb_ref = refs[7 + n:10 + n]
        i = pl.program_id(0)
        if n:
            send, pass_on, finish = _gather_phases(refs[10 + n:10 + 2 * n], *refs[10 + 2 * n:], _spans(gathering))
            pl.when(i == 0)(send)
            pl.when(i == steps - 1)(pass_on)
        pa = _dot(ya_ref[...], wa_ref[...])
        pb = _dot(yb_ref[...], wb_ref[...])
        sa = _sigmoid(ga_ref[...] + bg_ref[0:1, :])
        sb = _sigmoid(gb_ref[...] + bg_ref[1:2, :])
        mg_ref[...] = (sa * pa + sb * pb).astype(BF16)
        pa_ref[...] = pa.astype(BF16)
        pb_ref[...] = pb.astype(BF16)
        if n:
            pl.when(i == steps - 1)(finish)

    out = jax.ShapeDtypeStruct((s, D), BF16)
    res = pl.pallas_call(
        body, name="proj_merge", grid=(steps,),
        in_specs=[_rows(tm, D), _rows(tm, D), _full((D, D)), _full((D, D)),
                  _rows(tm, D, 5), _rows(tm, D, 6), _full((2, D))] + [ANY] * n,
        out_specs=[_rows(tm, D)] * 3 + [ANY] * n,
        out_shape=[out] * 3 + [jax.ShapeDtypeStruct(a.shape, a.dtype) for a in _arrays(gathering)],
        input_output_aliases={7 + w: 3 + w for w in range(n)},
        scratch_shapes=_gather_sems(n) if n else [],
        compiler_params=_params("arbitrary", communicates=bool(n)),
    )(ya, yb, wa, wb, z, z, bg, *_arrays(gathering))
    return res[0], res[1], res[2], res[3:]


def out_norm(merged, w_out, x, g_post, g_fpre, gathering):
    s = x.shape[0]
    tm = 512
    n = len(gathering)
    steps = s // tm

    def body(*refs):
        mg_ref, w_ref, x_ref, gp_ref, gf_ref = refs[:5]
        o_ref, x1_ref, h2_ref = refs[5 + n:8 + n]
        i = pl.program_id(0)
        if n:
            send, pass_on, finish = _gather_phases(refs[8 + n:8 + 2 * n], *refs[8 + 2 * n:], _spans(gathering))
            pl.when(i == 0)(send)
            pl.when(i == steps - 1)(pass_on)
        o = _dot(mg_ref[...], w_ref[...])
        ohat, _ = _rms(o)
        x1 = x_ref[...] + ohat * gp_ref[...]
        x1hat, _ = _rms(x1)
        o_ref[...] = o
        x1_ref[...] = x1
        h2_ref[...] = (x1hat * gf_ref[...]).astype(BF16)
        if n:
            pl.when(i == steps - 1)(finish)

    res = pl.pallas_call(
        body, name="out_norm", grid=(steps,),
        in_specs=[_rows(tm, D), _full((D, D)), _rows(tm, D), _full((1, D)), _full((1, D))] + [ANY] * n,
        out_specs=[_rows(tm, D)] * 3 + [ANY] * n,
        out_shape=[jax.ShapeDtypeStruct((s, D), F32), jax.ShapeDtypeStruct((s, D), F32),
                   jax.ShapeDtypeStruct((s, D), BF16)]
        + [jax.ShapeDtypeStruct(a.shape, a.dtype) for a in _arrays(gathering)],
        input_output_aliases={5 + w: 3 + w for w in range(n)},
        scratch_shapes=_gather_sems(n) if n else [],
        compiler_params=_params("arbitrary", communicates=bool(n)),
    )(merged, w_out, x, g_post, g_fpre, *_arrays(gathering))
    return res[0], res[1], res[2], res[3:]


def mm_ff1(h2, wg, gathering):
    s = h2.shape[0]
    tm = 1024
    n = len(gathering)
    ni = s // tm

    def body(*refs):
        a_ref, b_ref = refs[:2]
        o_ref, r_ref = refs[2 + n:4 + n]
        i, j = pl.program_id(0), pl.program_id(1)
        if n:
            send, pass_on, finish = _gather_phases(refs[4 + n:4 + 2 * n], *refs[4 + 2 * n:], _spans(gathering))
            pl.when((i == 0) & (j == 0))(send)
            pl.when((i == ni - 1) & (j == N_CHIPS // 2))(pass_on)
        a = _dot(a_ref[...], b_ref[...])
        o_ref[...] = a.astype(BF16)
        r = jnp.maximum(a, 0.0)
        r_ref[...] = (r * r).astype(BF16)
        if n:
            pl.when((i == ni - 1) & (j == N_CHIPS - 1))(finish)

    res = pl.pallas_call(
        body, name="mm_ff1", grid=(ni, N_CHIPS),
        in_specs=[pl.BlockSpec((tm, D), lambda i, j: (i, 0)), pl.BlockSpec((None, D, D), lambda i, j: (j, 0, 0))]
        + [ANY] * n,
        out_specs=[pl.BlockSpec((tm, D), lambda i, j: (i, j))] * 2 + [ANY] * n,
        out_shape=[jax.ShapeDtypeStruct((s, D_FF), BF16), jax.ShapeDtypeStruct((s, D_FF), BF16)]
        + [jax.ShapeDtypeStruct(a.shape, a.dtype) for a in _arrays(gathering)],
        input_output_aliases={2 + w: 2 + w for w in range(n)},
        scratch_shapes=_gather_sems(n) if n else [],
        compiler_params=_params("arbitrary", "arbitrary", communicates=bool(n)),
    )(h2, wg, *_arrays(gathering))
    return res[0], res[1], res[2:]


def ff2_loss(rl, w_ff2, x1, target, g_fpost):
    s = x1.shape[0]
    tm = 256

    def body(rl_ref, w_ref, x1_ref, t_ref, g_ref, dy_ref, df_ref, dg_ref, loss_ref):
        @pl.when(pl.program_id(0) == 0)
        def _():
            dg_ref[...] = jnp.zeros_like(dg_ref)
            loss_ref[...] = jnp.zeros_like(loss_ref)

        f = _dot(rl_ref[...], w_ref[...])
        fhat, r = _rms(f)
        err = x1_ref[...] + fhat * g_ref[...] - t_ref[...]
        loss_ref[...] += 0.5 * jnp.sum(jnp.mean(err * err, axis=-1, keepdims=True), axis=0, keepdims=True)
        dy = err * (1.0 / D)
        dy_ref[...] = dy
        dg_ref[...] += jnp.sum(dy * fhat, axis=0, keepdims=True)
        df_ref[...] = _rms_bwd(dy * g_ref[...], fhat, r).astype(BF16)

    return pl.pallas_call(
        body, name="ff2_loss", grid=(s // tm,),
        in_specs=[_rows(tm, D_FF), _full((D_FF, D)), _rows(tm, D), _rows(tm, D), _full((1, D))],
        out_specs=[_rows(tm, D), _rows(tm, D), _full((1, D)), _full((1, 1))],
        out_shape=[jax.ShapeDtypeStruct((s, D), F32), jax.ShapeDtypeStruct((s, D), BF16),
                   jax.ShapeDtypeStruct((1, D), F32), jax.ShapeDtypeStruct((1, 1), F32)],
        compiler_params=_params("arbitrary"),
    )(rl, w_ff2, x1, target, g_fpost)


def mm_tn(name, a, b, ta, tb, out_shape, out_spec):
    s = a.shape[0]

    def body(a_ref, b_ref, o_ref):
        o_ref[...] = _dot(a_ref[...], b_ref[...], TN)

    return pl.pallas_call(
        body, name=name, grid=(a.shape[1] // ta, b.shape[1] // tb),
        in_specs=[pl.BlockSpec((s, ta), lambda i, j: (0, i)), pl.BlockSpec((s, tb), lambda i, j: (0, j))],
        out_specs=out_spec, out_shape=jax.ShapeDtypeStruct(out_shape, F32),
        compiler_params=_params("parallel", "parallel"),
    )(a, b)


def mm_nt(name, a, w):
    s = a.shape[0]
    tm = 512

    def body(a_ref, w_ref, o_ref):
        o_ref[...] = _dot(a_ref[...], w_ref[...], NT).astype(BF16)

    return pl.pallas_call(
        body, name=name, grid=(s // tm,), in_specs=[_rows(tm, D), _full((D, D))], out_specs=_rows(tm, D),
        out_shape=jax.ShapeDtypeStruct((s, D), BF16), compiler_params=_params("parallel"),
    )(a, w)


def ff2_bwd(df, w_ff2, a):
    s = df.shape[0]
    tm = 1024

    def body(df_ref, w_ref, a_ref, da_ref):
        drl = _dot(df_ref[...], w_ref[...], NT)
        da_ref[...] = (drl * (2.0 * jnp.maximum(a_ref[...].astype(F32), 0.0))).astype(BF16)

    return pl.pallas_call(
        body, name="ff2_bwd", grid=(s // tm, D_FF // D),
        in_specs=[pl.BlockSpec((tm, D), lambda i, j: (i, 0)), pl.BlockSpec((D, D), lambda i, j: (j, 0)),
                  pl.BlockSpec((tm, D), lambda i, j: (i, j))],
        out_specs=pl.BlockSpec((tm, D), lambda i, j: (i, j)),
        out_shape=jax.ShapeDtypeStruct((s, D_FF), BF16), compiler_params=_params("parallel", "parallel"),
    )(df, w_ff2, a)


def ff1_bwd_norms(da, wg, x1, o, dy, g_fpre, g_post, swapping):
    s = x1.shape[0]
    tm = 256
    n = len(swapping)
    steps = s // tm

    def body(*refs):
        da_ref, w_ref, x1_ref, o_ref, dy_ref, gf_ref, gp_ref = refs[:7]
        dx1_ref, do_ref, dgf_ref, dgp_ref = refs[7 + n:11 + n]
        i = pl.program_id(0)
        if n:
            send, finish = _swap_phases(refs[7:7 + n], refs[11 + n:11 + 2 * n], *refs[11 + 2 * n:])
            pl.when(i == 0)(send)

        @pl.when(i == 0)
        def _():
            dgf_ref[...] = jnp.zeros_like(dgf_ref)
            dgp_ref[...] = jnp.zeros_like(dgp_ref)

        dh2 = _dot(da_ref[:, 0:D], w_ref[0], NT)
        for k in range(1, N_CHIPS):
            dh2 = dh2 + _dot(da_ref[:, k * D:(k + 1) * D], w_ref[k], NT)
        x1hat, r2 = _rms(x1_ref[...])
        dgf_ref[...] += jnp.sum(dh2 * x1hat, axis=0, keepdims=True)
        dx1 = dy_ref[...] + _rms_bwd(dh2 * gf_ref[...], x1hat, r2)
        ohat, r1 = _rms(o_ref[...])
        dgp_ref[...] += jnp.sum(dx1 * ohat, axis=0, keepdims=True)
        dx1_ref[...] = dx1
        do_ref[...] = _rms_bwd(dx1 * gp_ref[...], ohat, r1).astype(BF16)
        if n:
            pl.when(i == steps - 1)(finish)

    res = pl.pallas_call(
        body, name="ff1_bwd_norms", grid=(steps,),
        in_specs=[_rows(tm, D_FF), _full((N_CHIPS, D, D)), _rows(tm, D), _rows(tm, D), _rows(tm, D),
                  _full((1, D)), _full((1, D))] + [ANY] * n,
        out_specs=[_rows(tm, D), _rows(tm, D), _full((1, D)), _full((1, D))] + [ANY] * n,
        out_shape=[jax.ShapeDtypeStruct((s, D), F32), jax.ShapeDtypeStruct((s, D), BF16),
                   jax.ShapeDtypeStruct((1, D), F32), jax.ShapeDtypeStruct((1, D), F32)] + _swap_shapes(swapping),
        scratch_shapes=_swap_sems(n) if n else [],
        compiler_params=_params("arbitrary", communicates=bool(n)),
    )(da, wg, x1, o, dy, g_fpre, g_post, *swapping)
    return res[0], res[1], res[2], res[3], res[4:]


def out_bwd_gates(do, w_out, pa, pb, z, bg):
    s = do.shape[0]
    tm = 512

    def body(do_ref, w_ref, pa_ref, pb_ref, ga_ref, gb_ref, bg_ref, dpa_ref, dpb_ref, dga_ref, dgb_ref, dbg_ref):
        @pl.when(pl.program_id(0) == 0)
        def _():
            dbg_ref[...] = jnp.zeros_like(dbg_ref)

        dm = _dot(do_ref[...], w_ref[...], NT)
        sa = _sigmoid(ga_ref[...] + bg_ref[0:1, :])
        sb = _sigmoid(gb_ref[...] + bg_ref[1:2, :])
        dpa_ref[...] = (dm * sa).astype(BF16)
        dpb_ref[...] = (dm * sb).astype(BF16)
        dga = dm * pa_ref[...].astype(F32) * (sa * (1.0 - sa))
        dgb = dm * pb_ref[...].astype(F32) * (sb * (1.0 - sb))
        dga_ref[...] = dga.astype(BF16)
        dgb_ref[...] = dgb.astype(BF16)
        dbg_ref[0:1, :] += jnp.sum(dga, axis=0, keepdims=True)
        dbg_ref[1:2, :] += jnp.sum(dgb, axis=0, keepdims=True)

    out = jax.ShapeDtypeStruct((s, D), BF16)
    return pl.pallas_call(
        body, name="out_bwd_gates", grid=(s // tm,),
        in_specs=[_rows(tm, D), _full((D, D)), _rows(tm, D), _rows(tm, D), _rows(tm, D, 5), _rows(tm, D, 6),
                  _full((2, D))],
        out_specs=[_rows(tm, D)] * 4 + [_full((2, D))],
        out_shape=[out] * 4 + [jax.ShapeDtypeStruct((2, D), F32)], compiler_params=_params("arbitrary"),
    )(do, w_out, pa, pb, z, z, bg)


def gating_bwd(z, dya, ln_g, ln_b, w_s, bs_t, swapping):
    s = z.shape[0]
    ones = functools.partial(jnp.ones, (8, CHUNK), BF16)
    n = len(swapping)

    def body(*refs):
        u_ref, v_ref, dya_ref, lg_ref, lb_ref, ws_ref, bst_ref = refs[:7]
        du_ref, dv_ref, dws_ref, dbs_ref, dlg_ref, dlb_ref = refs[7 + n:13 + n]
        dvn_ref = refs[13 + 2 * n]
        ci = pl.program_id(0)
        if n:
            send, finish = _swap_phases(refs[7:7 + n], refs[13 + n:13 + 2 * n], *refs[14 + 2 * n:])
            pl.when(ci == 0)(send)

        @pl.when(ci == 0)
        def _():
            dws_ref[...] = jnp.zeros_like(dws_ref)
            dbs_ref[...] = jnp.zeros_like(dbs_ref)
            dlg_ref[...] = jnp.zeros_like(dlg_ref)
            dlb_ref[...] = jnp.zeros_like(dlb_ref)

        ug, dug_du = _gelu_and_grad(u_ref[...].astype(F32))
        vg, dvg_dv = _gelu_and_grad(v_ref[...].astype(F32))
        vhat, rstd = _layer_norm(vg)
        vn = (vhat * lg_ref[...] + lb_ref[...]).astype(BF16)
        dya = dya_ref[...].astype(F32)
        for g in range(GROUPS):
            cols = slice(g * CHUNK, (g + 1) * CHUNK)
            ws = _tril_ws(ws_ref, g)
            mixed = _dot(ws, vn[:, cols]) + bst_ref[:, g:g + 1]
            du_ref[:, cols] = (dya[:, cols] * mixed * dug_du[:, cols]).astype(BF16)
            dmix = (dya[:, cols] * ug[:, cols]).astype(BF16)
            dbs_ref[g] += _dot(ones(), dmix, NT)
            dws_ref[g] += _dot(dmix, vn[:, cols], NT)
            dvn_ref[:, cols] = _dot(ws, dmix, TN)
        dvn = dvn_ref[...]
        dlg_ref[...] += jnp.sum(dvn * vhat, axis=0, keepdims=True)
        dlb_ref[...] += jnp.sum(dvn, axis=0, keepdims=True)
        dvh = dvn * lg_ref[...]
        dvg = rstd * (dvh - jnp.mean(dvh, axis=-1, keepdims=True)
                      - vhat * jnp.mean(dvh * vhat, axis=-1, keepdims=True))
        dv_ref[...] = (dvg * dvg_dv).astype(BF16)

        @pl.when(ci == pl.num_programs(0) - 1)
        def _():
            r = lax.broadcasted_iota(jnp.int32, (CHUNK, CHUNK), 0)
            c = lax.broadcasted_iota(jnp.int32, (CHUNK, CHUNK), 1)
            for g in range(GROUPS):
                dws_ref[g] = jnp.where(c <= r, dws_ref[g], 0.0)

        if n:
            pl.when(ci == pl.num_programs(0) - 1)(finish)

    out = jax.ShapeDtypeStruct((s, D), BF16)
    res = pl.pallas_call(
        body, name="gating_bwd", grid=(s // CHUNK,),
        in_specs=[_rows(CHUNK, D, 0), _rows(CHUNK, D, 1), _rows(CHUNK, D), _full((1, D)), _full((1, D)),
                  _full((GROUPS, CHUNK, CHUNK)), _full((CHUNK, GROUPS))] + [ANY] * n,
        out_specs=[_rows(CHUNK, D), _rows(CHUNK, D), _full((GROUPS, CHUNK, CHUNK)), _full((GROUPS, 8, CHUNK)),
                   _full((1, D)), _full((1, D))] + [ANY] * n,
        out_shape=[out, out, jax.ShapeDtypeStruct((GROUPS, CHUNK, CHUNK), F32),
                   jax.ShapeDtypeStruct((GROUPS, 8, CHUNK), F32),
                   jax.ShapeDtypeStruct((1, D), F32), jax.ShapeDtypeStruct((1, D), F32)] + _swap_shapes(swapping),
        scratch_shapes=[pltpu.VMEM((CHUNK, D), F32)] + (_swap_sems(n) if n else []),
        compiler_params=_params("arbitrary", communicates=bool(n)),
    )(z, z, dya, ln_g, ln_b, w_s, bs_t, *swapping)
    return (*res[:6], res[6:])


def attn_bwd(z, yb, dyb, lse, logc, ka, kb, scattering, gathering=None):
    s = z.shape[0]
    nq = s // ATT_T
    t = ATT_T
    grp = ATT_BWD_GROUP
    ngrp = HEADS // 2 // grp
    wide = 128 * grp
    qcol, kcol, vcol = 2 * D // wide, 3 * D // wide, 4 * D // wide
    scale = 1.0 / math.sqrt(HEAD_DIM)
    n = len(scattering)
    g8 = 0 if gathering is None else 1

    def body(*refs):
        q_ref, k_ref, v_ref, y_ref, dy_ref, lse_ref, lc_ref, ka_ref, kb_ref = refs[:9]
        dq_ref, dk_ref, dv_ref = refs[9 + n + g8:12 + n + g8]
        qa_s, qt_s, da_s, dt_s, dq_s, dkt_s, dvt_s = refs[12 + 2 * n + 2 * g8:19 + 2 * n + 2 * g8]
        sems = refs[19 + 2 * n + 2 * g8:]
        gi, j = pl.program_id(0), pl.program_id(1)
        first, lane, ones = _head_masks()
        if n:
            send, finish = _scatter_phases(refs[9:9 + n], refs[12 + n + g8:12 + 2 * n + g8], *sems[:2])
            pl.when((gi == 0) & (j == 0))(send)
        if g8:
            send8, pass_on8, finish8 = _allgather8_phases(refs[12 + 2 * n + g8], *sems[2 * (n > 0):])
            pl.when((gi == 0) & (j == 0))(send8)
            pl.when((gi == ngrp - 1) & (j == nq - 1))(pass_on8)

        @pl.when(j == 0)
        def _():
            dq_s[...] = jnp.zeros_like(dq_s)
            for pr in range(grp):
                cols = slice(pr * 128, (pr + 1) * 128)
                for ib in range(nq):
                    rows = slice(ib * t, (ib + 1) * t)
                    q = q_ref[rows, cols].astype(F32) * scale
                    lse = lse_ref[rows, cols]
                    qa_s[pr, 0, ib] = jnp.where(first, q, _place3(lane, HEAD_DIM + 2 * AUG, _split3(-lse[:, 0:1]),
                                                                  ones(0, 2 * AUG))).astype(BF16)
                    qa_s[pr, 1, ib] = jnp.where(
                        first, _place3(lane, 2 * AUG, _split3(-lse[:, HEAD_DIM:HEAD_DIM + 1]), ones(1, 2 * AUG)),
                        q).astype(BF16)
                    qt_s[pr, ib, :, 0:t] = jnp.where(first, q, 0.0).T.astype(BF16)
                    qt_s[pr, ib, :, t:2 * t] = jnp.where(first, 0.0, q).T.astype(BF16)
                    do = dy_ref[rows, cols].astype(F32)
                    prod = do * y_ref[rows, cols].astype(F32)
                    dd0 = jnp.sum(jnp.where(first, prod, 0.0), axis=-1, keepdims=True)
                    dd1 = jnp.sum(jnp.where(first, 0.0, prod), axis=-1, keepdims=True)
                    da_s[pr, 0, ib] = jnp.where(first, do, _place3(lane, HEAD_DIM, _split3(-dd0), 0.0)).astype(BF16)
                    da_s[pr, 1, ib] = jnp.where(first, _place3(lane, 0, _split3(-dd1), 0.0), do).astype(BF16)
                    dt_s[pr, ib, :, 0:t] = jnp.where(first, do, 0.0).T.astype(BF16)
                    dt_s[pr, ib, :, t:2 * t] = jnp.where(first, 0.0, do).T.astype(BF16)

        keys = []
        for pr in range(grp):
            kj = k_ref[:, pr * 128:(pr + 1) * 128].astype(F32)
            vj = v_ref[:, pr * 128:(pr + 1) * 128].astype(F32)
            keys.append((
                jnp.where(first, kj, ka_ref[pr, 0] + kb_ref[pr, 0, pl.ds(j, 1), :]).astype(BF16),
                jnp.where(first, ka_ref[pr, 1] + kb_ref[pr, 1, pl.ds(j, 1), :], kj).astype(BF16),
                jnp.concatenate([jnp.where(first, kj, 0.0), jnp.where(first, 0.0, kj)], axis=0).astype(BF16),
                jnp.where(first, vj, ones(0, AUG)).astype(BF16),
                jnp.where(first, ones(1, AUG), vj).astype(BF16)))
        dkt_s[...] = jnp.zeros_like(dkt_s)
        dvt_s[...] = jnp.zeros_like(dvt_s)

        def step(i, _):
            lc = lc_ref[i - j]
            rows = pl.ds(pl.multiple_of(i * t, t), t)
            for pr in range(grp):
                k0a, k1a, kst, v0a, v1a = keys[pr]
                p0 = jnp.exp(_dot(qa_s[pr, 0, i], k0a, NT) + lc)
                p1 = jnp.exp(_dot(qa_s[pr, 1, i], k1a, NT) + lc)
                e0 = (p0 * _dot(da_s[pr, 0, i], v0a, NT)).astype(BF16)
                e1 = (p1 * _dot(da_s[pr, 1, i], v1a, NT)).astype(BF16)
                dq_s[pr, rows, :] += _dot(jnp.concatenate([e0, e1], axis=1), kst)
                dvt_s[pr] += _dot(dt_s[pr, i], jnp.concatenate([p0.astype(BF16), p1.astype(BF16)], axis=0))
                dkt_s[pr] += _dot(qt_s[pr, i], jnp.concatenate([e0, e1], axis=0))
            return 0

        lax.fori_loop(j, nq, step, 0)
        for pr in range(grp):
            dk_ref[:, pr * 128:(pr + 1) * 128] = dkt_s[pr].T.astype(BF16)
            dv_ref[:, pr * 128:(pr + 1) * 128] = dvt_s[pr].T.astype(BF16)

        @pl.when(j == nq - 1)
        def _():
            for pr in range(grp):
                dq_ref[:, pr * 128:(pr + 1) * 128] = (dq_s[pr] * scale).astype(BF16)

        if n:
            pl.when((gi == ngrp - 1) & (j == nq - 1))(finish)
        if g8:
            pl.when((gi == ngrp - 1) & (j == nq - 1))(finish8)

    colblock = lambda c: pl.BlockSpec((s, wide), lambda g, j: (0, c + g))
    once = lambda c: pl.BlockSpec((s, wide), lambda g, j: (0, c + g), pipeline_mode=pl.Buffered(1))
    blk = lambda c: pl.BlockSpec((t, wide), lambda g, j: (j, c + g))
    out = jax.ShapeDtypeStruct((s, D), BF16)
    res = pl.pallas_call(
        body, name="attn_bwd", grid=(ngrp, nq),
        in_specs=[once(qcol), blk(kcol), blk(vcol), once(0), once(0), once(0),
                  pl.BlockSpec((nq, t, t), lambda g, j: (0, 0, 0), pipeline_mode=pl.Buffered(1)),
                  pl.BlockSpec((grp, 2, t, 128), lambda g, j: (g, 0, 0, 0)),
                  pl.BlockSpec((grp, 2, nq, 128), lambda g, j: (g, 0, 0, 0))] + [ANY] * (n + g8),
        out_specs=[colblock(0), blk(0), blk(0)] + [ANY] * (n + g8),
        out_shape=[out] * 3 + _scatter_shapes(scattering)
        + ([jax.ShapeDtypeStruct(gathering.shape, gathering.dtype)] if g8 else []),
        input_output_aliases={9 + n: 3 + n} if g8 else {},
        scratch_shapes=[pltpu.VMEM((grp, 2, nq, t, 128), BF16), pltpu.VMEM((grp, nq, 128, 2 * t), BF16),
                        pltpu.VMEM((grp, 2, nq, t, 128), BF16), pltpu.VMEM((grp, nq, 128, 2 * t), BF16),
                        pltpu.VMEM((grp, s, 128), F32), pltpu.VMEM((grp, 128, t), F32),
                        pltpu.VMEM((grp, 128, t), F32)]
        + (_scatter_sems(n) if n else [])
        + ([pltpu.SemaphoreType.DMA((7,)), pltpu.SemaphoreType.DMA((7,))] if g8 else []),
        compiler_params=_params("arbitrary", "arbitrary", communicates=bool(n + g8)),
    )(z, z, z, yb, dyb, lse, logc, ka, kb, *scattering, *([gathering] if g8 else []))
    return res[0], res[1], res[2], res[3:3 + n], (res[3 + n] if g8 else None)


def in_bwd_norm(dz, wg, x, dx1, g_pre, scattering, gathering=None):
    s = x.shape[0]
    tm = 512
    n = len(scattering)
    g = 0 if gathering is None else 1
    last = (s // tm - 1, N_CHIPS - 1)

    def body(*refs):
        dz_ref, w_ref, x_ref, dx1_ref, g_ref = refs[:5]
        dx_ref, dg_ref = refs[5 + n + g:7 + n + g]
        acc_ref = refs[7 + 2 * n + 2 * g]
        sems = refs[8 + 2 * n + 2 * g:]
        i, k = pl.program_id(0), pl.program_id(1)
        if n:
            send, finish = _scatter_phases(refs[5:5 + n], refs[7 + n + g:7 + 2 * n + g], *sems[:2])
            pl.when((i == 0) & (k == 0))(send)
        if g:
            send8, pass_on8, finish8 = _allgather8_phases(refs[7 + 2 * n + g], *sems[2 * (n > 0):])
            pl.when((i == 0) & (k == 0))(send8)
            pl.when((i == last[0]) & (k == last[1]))(pass_on8)

        @pl.when((i == 0) & (k == 0))
        def _():
            dg_ref[...] = jnp.zeros_like(dg_ref)

        part = _dot(dz_ref[...], w_ref[...], NT)

        @pl.when(k == 0)
        def _():
            acc_ref[...] = part

        @pl.when(k > 0)
        def _():
            acc_ref[...] += part

        @pl.when(k == N_CHIPS - 1)
        def _():
            dh = acc_ref[...]
            xhat, r = _rms(x_ref[...])
            dg_ref[...] += jnp.sum(dh * xhat, axis=0, keepdims=True)
            dx_ref[...] = dx1_ref[...] + _rms_bwd(dh * g_ref[...], xhat, r)

        if n:
            pl.when((i == last[0]) & (k == last[1]))(finish)
        if g:
            pl.when((i == last[0]) & (k == last[1]))(finish8)

    row = pl.BlockSpec((tm, D), lambda i, k: (i, 0))
    vec = pl.BlockSpec((1, D), lambda i, k: (0, 0))
    res = pl.pallas_call(
        body, name="in_bwd_norm", grid=(s // tm, N_CHIPS),
        in_specs=[pl.BlockSpec((tm, IN_SHARD), lambda i, k: (i, k)),
                  pl.BlockSpec((None, D, IN_SHARD), lambda i, k: (k, 0, 0)), row, row, vec] + [ANY] * (n + g),
        out_specs=[row, vec] + [ANY] * (n + g),
        out_shape=[jax.ShapeDtypeStruct((s, D), F32), jax.ShapeDtypeStruct((1, D), F32)]
        + _scatter_shapes(scattering) + ([jax.ShapeDtypeStruct(gathering.shape, gathering.dtype)] if g else []),
        input_output_aliases={5 + n: 2 + n} if g else {},
        scratch_shapes=[pltpu.VMEM((tm, D), F32)] + (_scatter_sems(n) if n else [])
        + ([pltpu.SemaphoreType.DMA((7,)), pltpu.SemaphoreType.DMA((7,))] if g else []),
        compiler_params=_params("arbitrary", "arbitrary", communicates=bool(n + g)),
    )(dz, wg, x, dx1, g_pre, *scattering, *([gathering] if g else []))
    return res[0], res[1], res[2:2 + n], (res[2 + n] if g else None)


def _adamw_math(w, g, m, v):
    m = ADAM_B1 * m + (1.0 - ADAM_B1) * g
    v = ADAM_B2 * v + (1.0 - ADAM_B2) * (g * g)
    m_hat = m / (1.0 - ADAM_B1 ** ADAM_STEP)
    v_hat = v / (1.0 - ADAM_B2 ** ADAM_STEP)
    delta = -ADAM_LR * (m_hat / (jnp.sqrt(v_hat) + ADAM_EPS) + ADAM_WD * w)
    return delta, m, v


def adamw(name, w, g, m, v, tr):
    r, c = w.shape

    def body(w_ref, g_ref, m_ref, v_ref, go_ref, d_ref, nm_ref, nv_ref):
        g = g_ref[...]
        go_ref[...] = g
        d_ref[...], nm_ref[...], nv_ref[...] = _adamw_math(w_ref[...], g, m_ref[...], v_ref[...])

    out = jax.ShapeDtypeStruct((r, c), F32)
    return pl.pallas_call(
        body, name=name, grid=(r // tr,), in_specs=[_rows(tr, c)] * 4, out_specs=[_rows(tr, c)] * 4,
        out_shape=[out] * 4, compiler_params=_params("parallel"),
    )(w, g, m, v)


def _allgather8_phases(buf, send_sems, recv_sems):
    x, y, c, chips = _place()
    me = 2 * x + y
    sibling = (x, y, 1 - c)
    rows = buf.shape[1] // 2

    def part(chip, core):
        return buf.at[chip, pl.ds(core * rows, rows)]

    def copy(k, block, to):
        return pltpu.make_async_remote_copy(src_ref=block, dst_ref=block, send_sem=send_sems.at[k],
                                            recv_sem=recv_sems.at[k], device_id=to, device_id_type=MESH)

    def chip_of(j):
        return 2 * chips[j][0] + chips[j][1]

    def send():
        copy(0, part(me, c), sibling).start()
        for j in range(3):
            copy(1 + j, part(me, c), (chips[j][0], chips[j][1], c)).start()

    def pass_on():
        for j in range(3):
            copy(1 + j, part(chip_of(j), c), (chips[j][0], chips[j][1], c)).wait_recv()
            copy(4 + j, part(chip_of(j), c), sibling).start()

    def finish():
        copy(0, part(me, 1 - c), sibling).wait_recv()
        for j in range(3):
            copy(4 + j, part(chip_of(j), 1 - c), sibling).wait_recv()
        copy(0, part(me, c), sibling).wait_send()
        for j in range(3):
            copy(1 + j, part(me, c), (chips[j][0], chips[j][1], c)).wait_send()
            copy(4 + j, part(chip_of(j), c), sibling).wait_send()

    return send, pass_on, finish


def add_halves(name, g, recv, c_idx, tr):
    n, h, c = recv.shape

    def body(c_ref, g_ref, r_ref, o_ref):
        o_ref[...] = (g_ref[...] + r_ref[...]).astype(BF16)

    nb = h // tr
    return pl.pallas_call(
        body, name=name,
        grid_spec=pltpu.PrefetchScalarGridSpec(
            num_scalar_prefetch=1, grid=(n, nb),
            in_specs=[pl.BlockSpec((None, tr, c), lambda k, i, c_ref: (k, c_ref[0] * nb + i, 0)),
                      pl.BlockSpec((None, tr, c), lambda k, i, c_ref: (k, i, 0))],
            out_specs=pl.BlockSpec((None, tr, c), lambda k, i, c_ref: (k, i, 0))),
        out_shape=jax.ShapeDtypeStruct((n, h, c), BF16), compiler_params=_params("parallel", "parallel"),
    )(c_idx, g, recv)


def sum_chips(name, parts, recv, where, tr):
    n, h, c = recv.shape
    nb = h // tr

    def body(w_ref, p_ref, r_ref, o_ref):
        acc = p_ref[...].astype(F32)
        for k in range(n):
            acc = acc + r_ref[k].astype(F32)
        o_ref[...] = acc

    return pl.pallas_call(
        body, name=name,
        grid_spec=pltpu.PrefetchScalarGridSpec(
            num_scalar_prefetch=1, grid=(nb,),
            in_specs=[pl.BlockSpec((None, tr, c), lambda i, w_ref: (w_ref[0], i, 0)),
                      pl.BlockSpec((n, tr, c), lambda i, w_ref: (0, i, 0))],
            out_specs=pl.BlockSpec((tr, c), lambda i, w_ref: (w_ref[1] * nb + i, 0))),
        out_shape=jax.ShapeDtypeStruct((2 * h, c), F32), compiler_params=_params("parallel"),
    )(where, parts, recv)


def place_shard(name, shard, where, dtype, tr):
    r, c = shard.shape

    def body(w_ref, s_ref, o_ref):
        o_ref[...] = s_ref[...].astype(dtype)

    return pl.pallas_call(
        body, name=name,
        grid_spec=pltpu.PrefetchScalarGridSpec(
            num_scalar_prefetch=1, grid=(r // tr,),
            in_specs=[pl.BlockSpec((tr, c), lambda i, w_ref: (i, 0))],
            out_specs=pl.BlockSpec((None, tr, c), lambda i, w_ref: (w_ref[0], i, 0))),
        out_shape=jax.ShapeDtypeStruct((N_CHIPS, r, c), dtype), compiler_params=_params("parallel"),
    )(where, shard)


ANY = pl.BlockSpec(memory_space=pl.ANY)


def _place():
    x, y, c = lax.axis_index("x"), lax.axis_index("y"), lax.axis_index("c")
    chips = [(1 - x, y), (x, 1 - y), (1 - x, 1 - y)]
    return x, y, c, chips


def gather_shards(arrays):
    n = len(arrays)

    def body(*refs):
        send, pass_on, finish = _gather_phases(refs[n:2 * n], *refs[2 * n:], _spans(arrays))
        send()
        pass_on()
        finish()

    return pl.pallas_call(
        body, name="gather_shards", in_specs=[ANY] * n, out_specs=[ANY] * n,
        out_shape=[jax.ShapeDtypeStruct(a.shape, a.dtype) for a in _arrays(arrays)],
        input_output_aliases={w: w for w in range(n)}, scratch_shapes=_gather_sems(n),
        compiler_params=pltpu.CompilerParams(has_side_effects=True),
    )(*_arrays(arrays))


def _gather_sems(n):
    return [pltpu.SemaphoreType.DMA((6 * n,)), pltpu.SemaphoreType.DMA((6 * n,))]


class Span(typing.NamedTuple):
    array: jax.Array
    lo: int
    hi: int
    ways: tuple = (0, 1, 2)


def _arrays(gathering):
    return [g.array if isinstance(g, Span) else g for g in gathering]


def _spans(gathering):
    return [(g.lo, g.hi, g.ways) if isinstance(g, Span) else (0, g.shape[1], (0, 1, 2)) for g in gathering]


def _gather_phases(out, send_sems, recv_sems, spans):
    n = len(out)
    if not any(ways for _, _, ways in spans):
        return (lambda: None,) * 3
    x, y, c, chips = _place()
    me = 2 * x + y
    sibling = (x, y, 1 - c)

    def half(w, chip, core):
        lo, hi, _ = spans[w]
        h = (hi - lo) // 2
        return out[w].at[chip, pl.ds(lo + core * h, h)]

    def copy(k, block, to):
        return pltpu.make_async_remote_copy(src_ref=block, dst_ref=block, send_sem=send_sems.at[k],
                                            recv_sem=recv_sems.at[k], device_id=to, device_id_type=MESH)

    def over_ici(w, j, chip):
        return copy(3 * w + j, half(w, chip, c), (chips[j][0], chips[j][1], c))

    def over_d2d(w, j, core):
        return copy(3 * n + 3 * w + j, half(w, 2 * chips[j][0] + chips[j][1], core), sibling)

    pairs = [(w, j) for w in range(n) for j in spans[w][2]]

    def send():
        for w, j in pairs:
            over_ici(w, j, me).start()

    def pass_on():
        for w, j in pairs:
            over_ici(w, j, 2 * chips[j][0] + chips[j][1]).wait_recv()
            over_d2d(w, j, c).start()

    def finish():
        for w, j in pairs:
            over_d2d(w, j, 1 - c).wait_recv()
        for w, j in pairs:
            over_ici(w, j, me).wait_send()
            over_d2d(w, j, c).wait_send()

    return send, pass_on, finish


def _relay_sems():
    return [pltpu.SemaphoreType.DMA((4,)), pltpu.SemaphoreType.DMA((4,))]


def _relay_phases(out, send_sems, recv_sems):
    x, y, c, chips = _place()
    sibling = (x, y, 1 - c)
    rows = out.shape[1]
    quarter = rows // 4
    far = 2 * chips[2][0] + chips[2][1]

    def piece(chip, way, core):
        return out.at[chip, pl.ds(way * (rows // 2) + core * quarter, quarter)]

    def copy(k, block, to):
        return pltpu.make_async_remote_copy(src_ref=block, dst_ref=block, send_sem=send_sems.at[k],
                                            recv_sem=recv_sems.at[k], device_id=to, device_id_type=MESH)

    def over_ici(way, chip):
        return copy(way, piece(chip, way, c), (chips[way][0], chips[way][1], c))

    def over_d2d(way, core):
        return copy(2 + way, piece(far, way, core), sibling)

    def send():
        for way in range(2):
            other = chips[1 - way]
            over_ici(way, 2 * other[0] + other[1]).start()

    def pass_on():
        for way in range(2):
            over_ici(way, far).wait_recv()
            over_d2d(way, c).start()

    def finish():
        for way in range(2):
            over_d2d(way, 1 - c).wait_recv()
        for way in range(2):
            other = chips[1 - way]
            over_ici(way, 2 * other[0] + other[1]).wait_send()
            over_d2d(way, c).wait_send()

    return send, pass_on, finish


def swap_halves(name, grads):
    n = len(grads)

    def body(*refs):
        send, finish = _swap_phases(refs[:n], refs[n:2 * n], *refs[2 * n:])
        send()
        finish()

    return pl.pallas_call(
        body, name=name, in_specs=[ANY] * n, out_specs=[ANY] * n, out_shape=_swap_shapes(grads),
        scratch_shapes=_swap_sems(n), compiler_params=pltpu.CompilerParams(has_side_effects=True),
    )(*grads)


def _swap_shapes(grads):
    return [jax.ShapeDtypeStruct((a.shape[0], a.shape[1] // 2, a.shape[2]), a.dtype) for a in grads]


def _swap_sems(n):
    return [pltpu.SemaphoreType.DMA((n,)), pltpu.SemaphoreType.DMA((n,))]


def _swap_phases(g, out, send_sems, recv_sems):
    x, y, c, _ = _place()

    def copies():
        return [pltpu.make_async_remote_copy(
            src_ref=g[w].at[:, pl.ds((1 - c) * (g[w].shape[1] // 2), g[w].shape[1] // 2)], dst_ref=out[w],
            send_sem=send_sems.at[w], recv_sem=recv_sems.at[w], device_id=(x, y, 1 - c), device_id_type=MESH)
            for w in range(len(g))]

    def send():
        for cp in copies():
            cp.start()

    def finish():
        for cp in copies():
            cp.wait()

    return send, finish


def _send_phases(g, out, send_sems, recv_sems):
    x, y, c, _ = _place()

    def copies():
        return [pltpu.make_async_remote_copy(
            src_ref=g[w], dst_ref=out[w], send_sem=send_sems.at[w], recv_sem=recv_sems.at[w],
            device_id=(x, y, 1 - c), device_id_type=MESH) for w in range(len(g))]

    def send():
        for cp in copies():
            cp.start()

    def finish():
        for cp in copies():
            cp.wait()

    return send, finish


def dw_in_half(name, h, dz, which, sending):
    s = h.shape[0]
    hh, tb = D // 2, IN_SHARD // 2
    n = len(sending)
    steps = IN_COLS // tb

    def body(w_ref, *refs):
        a_ref, b_ref, o_ref = refs[0], refs[1], refs[2 + n]
        j = pl.program_id(0)
        if n:
            send, finish = _send_phases(refs[2:2 + n], refs[3 + n:3 + 2 * n], *refs[3 + 2 * n:])
            pl.when(j == 0)(send)
        o_ref[...] = _dot(a_ref[...], b_ref[...], TN)
        if n:
            pl.when(j == steps - 1)(finish)

    out = pl.pallas_call(
        body, name=name,
        grid_spec=pltpu.PrefetchScalarGridSpec(
            num_scalar_prefetch=1, grid=(steps,),
            in_specs=[pl.BlockSpec((s, hh), lambda j, w: (0, w[0])), pl.BlockSpec((s, tb), lambda j, w: (0, j))]
            + [ANY] * n,
            out_specs=[pl.BlockSpec((None, hh, tb), lambda j, w: (j // 2, 0, j % 2))] + [ANY] * n,
            scratch_shapes=_swap_sems(n) if n else []),
        out_shape=[jax.ShapeDtypeStruct((N_CHIPS, hh, IN_SHARD), F32)]
        + [jax.ShapeDtypeStruct(a.shape, a.dtype) for a in sending],
        compiler_params=_params("arbitrary", communicates=bool(n)),
    )(which, h, dz, *sending)
    return out[0], out[1:]


def scatter_chips(parts):
    n = len(parts)

    def body(*refs):
        send, finish = _scatter_phases(refs[:n], refs[n:2 * n], *refs[2 * n:])
        send()
        finish()

    return pl.pallas_call(
        body, name="scatter_chips", in_specs=[ANY] * n, out_specs=[ANY] * n,
        out_shape=_scatter_shapes(parts), scratch_shapes=_scatter_sems(n),
        compiler_params=pltpu.CompilerParams(has_side_effects=True),
    )(*parts)


def _scatter_shapes(parts):
    return [jax.ShapeDtypeStruct((3,) + a.shape[1:], a.dtype) for a in parts]


def _scatter_sems(n):
    return [pltpu.SemaphoreType.DMA((3 * n,)), pltpu.SemaphoreType.DMA((3 * n,))]


def _scatter_phases(p, out, send_sems, recv_sems):
    x, y, c, chips = _place()

    def copies():
        return [pltpu.make_async_remote_copy(
            src_ref=p[w].at[2 * px + py], dst_ref=out[w].at[j], send_sem=send_sems.at[3 * w + j],
            recv_sem=recv_sems.at[3 * w + j], device_id=(px, py, c), device_id_type=MESH)
            for w in range(len(p)) for j, (px, py) in enumerate(chips)]

    def send():
        for cp in copies():
            cp.start()

    def finish():
        for cp in copies():
            cp.wait()

    return send, finish


def join_halves(arrays, gathering):
    n = len(arrays)

    def body(*refs):
        out = refs[n + 1:2 * n + 1]
        send_sems, recv_sems = refs[2 * n + 2:2 * n + 4]
        send8, pass_on8, finish8 = _allgather8_phases(refs[2 * n + 1], *refs[2 * n + 4:])
        x, y, c, _ = _place()

        def copy(w, core):
            h = out[w].shape[0] // 2
            rows = out[w].at[pl.ds(core * h, h)]
            return pltpu.make_async_remote_copy(
                src_ref=rows, dst_ref=rows, send_sem=send_sems.at[w], recv_sem=recv_sems.at[w],
                device_id=(x, y, 1 - c), device_id_type=MESH)

        send8()
        for w in range(n):
            copy(w, c).start()
        pass_on8()
        for w in range(n):
            copy(w, 1 - c).wait_recv()
        finish8()
        for w in range(n):
            copy(w, c).wait_send()

    res = pl.pallas_call(
        body, name="join_halves", in_specs=[ANY] * (n + 1), out_specs=[ANY] * (n + 1),
        out_shape=[jax.ShapeDtypeStruct(a.shape, a.dtype) for a in list(arrays) + [gathering]],
        input_output_aliases={w: w for w in range(n + 1)},
        scratch_shapes=[pltpu.SemaphoreType.DMA((n,)), pltpu.SemaphoreType.DMA((n,)),
                        pltpu.SemaphoreType.DMA((7,)), pltpu.SemaphoreType.DMA((7,))],
        compiler_params=pltpu.CompilerParams(has_side_effects=True),
    )(*arrays, gathering)
    return res[:n], res[n]


def allreduce_small(packed):
    r, c = packed.shape
    n_dev = 8

    def body(x_ref, all_ref, sum_ref, send_sems, recv_sems, local_sem):
        x, y, cc, chips = _place()
        me, sibling = (x, y, cc), (x, y, 1 - cc)

        def rows(px, py, pc):
            return all_ref.at[4 * px + 2 * py + pc]

        def copy(k, block, to, src=None):
            return pltpu.make_async_remote_copy(
                src_ref=rows(*block) if src is None else src, dst_ref=rows(*block), send_sem=send_sems.at[k],
                recv_sem=recv_sems.at[k], device_id=to, device_id_type=MESH)

        mine = pltpu.make_async_copy(x_ref, rows(*me), local_sem)
        mine.start()
        first = [copy(0, me, sibling, src=x_ref)]
        first += [copy(1 + j, me, (*chip, cc), src=x_ref) for j, chip in enumerate(chips)]
        for cp in first:
            cp.start()
        passed = [copy(4 + j, (*chip, cc), sibling) for j, chip in enumerate(chips)]
        for j, chip in enumerate(chips):
            copy(1 + j, (*chip, cc), me).wait_recv()
            passed[j].start()
        copy(0, sibling, me).wait_recv()
        for j, chip in enumerate(chips):
            copy(4 + j, (*chip, 1 - cc), me).wait_recv()
        for cp in first + passed:
            cp.wait_send()
        mine.wait()
        acc = all_ref[0]
        for k in range(1, n_dev):
            acc = acc + all_ref[k]
        sum_ref[...] = acc

    vm = pl.BlockSpec(memory_space=pltpu.VMEM)
    return pl.pallas_call(
        body, name="allreduce_small", in_specs=[vm], out_specs=[vm, vm],
        out_shape=[jax.ShapeDtypeStruct((n_dev, r, c), F32), jax.ShapeDtypeStruct((r, c), F32)],
        scratch_shapes=[pltpu.SemaphoreType.DMA((7,)), pltpu.SemaphoreType.DMA((7,)), pltpu.SemaphoreType.DMA],
        compiler_params=pltpu.CompilerParams(has_side_effects=True, vmem_limit_bytes=VMEM_LIMIT),
    )(packed)[1]


def local_step(x, target, vecs, w_s, bs_t, bg, wg_in, late, core=None, order=None, where=None):
    on_mesh = core is not None

    def add(names, grads, recv):
        return [add_halves("add_" + n, g, r, core, min(r.shape[1], 256)) for n, g, r in zip(names, grads, recv)]

    g_pre, ln_g, ln_b, g_post, g_fpre, g_fpost = vecs
    s = x.shape[0]
    if order is None:
        order = jnp.arange(N_CHIPS, dtype=jnp.int32)
    logc = _attn_tables(s)
    ka, kb = _alibi_tables(s)

    h = norm_pre(x, g_pre)
    if not on_mesh:
        wg_a, wg_b, wg_out, wg_ff1, wg_ff2 = late
    if on_mesh:
        cut = D // 4
        z, wg_in, (wg_a, wg_b, wg_out, wg_ff1, wg_ff2) = mm_in(h, wg_in, order, True, late)
        ya, (wg_b, wg_ff2) = gating_fwd(z, ln_g, ln_b, w_s, bs_t, [wg_b, Span(wg_ff2, 0, cut)])
        yb, lse, (wg_a, wg_ff1, wg_ff2, bg) = attn_fwd(
            z, logc, ka, kb, [wg_a, wg_ff1, Span(wg_ff2, cut, 3 * cut), bg])
        bg = jnp.transpose(bg[:, :2, :], (1, 0, 2)).reshape(2, D)
    else:
        z, _, _ = mm_in(h, wg_in, order, False)
        ya, _ = gating_fwd(z, ln_g, ln_b, w_s, bs_t, [])
        yb, lse, _ = attn_fwd(z, logc, ka, kb, [])
    merged, pa, pb, got = proj_merge(ya, yb, wg_a.reshape(D, D), wg_b.reshape(D, D), z, bg, [wg_out] if on_mesh else [])
    wg_out = got[0] if on_mesh else wg_out
    w_out = wg_out.reshape(D, D)
    o, x1, h2, got = out_norm(merged, w_out, x, g_post, g_fpre, [Span(wg_ff2, 3 * D // 4, D)] if on_mesh else [])
    a, rl, _ = mm_ff1(h2, wg_ff1, [])
    w_ff2 = (got[0] if on_mesh else wg_ff2).reshape(D_FF, D)
    dy, df, d_gfpost, loss = ff2_loss(rl, w_ff2, x1, target, g_fpost)

    half_cols = pl.BlockSpec((D, D // 2), lambda i, j: (0, j))
    d_wff2 = mm_tn("dw_ff2", rl, df, D // 2, D, (D_FF, D), pl.BlockSpec((D // 2, D), lambda i, j: (i, 0)))
    da = ff2_bwd(df, w_ff2, a)
    d_wff1 = mm_tn("dw_ff1", h2, da, D, D // 2, (N_CHIPS, D, D),
                   pl.BlockSpec((None, D, D // 2), lambda i, j: (j // 2, 0, j % 2)))
    d_ff = [d_wff1, d_wff2.reshape(N_CHIPS, D, D)]
    dx1, do, d_gfpre, d_gpost, recv_ff = ff1_bwd_norms(da, wg_ff1, x1, o, dy, g_fpre, g_post, d_ff if on_mesh else [])
    d_wout = mm_tn("dw_out", merged, do, D, D // 2, (D, D), half_cols)
    dpa, dpb, dga, dgb, d_bg = out_bwd_gates(do, w_out, pa, pb, z, bg)
    d_wa = mm_tn("dw_a", ya, dpa, D, D // 2, (D, D), half_cols)
    d_wb = mm_tn("dw_b", yb, dpb, D, D // 2, (D, D), half_cols)
    dya = mm_nt("dy_a", dpa, wg_a.reshape(D, D))
    dyb = mm_nt("dy_b", dpb, wg_b.reshape(D, D))
    d_proj = [d_wa.reshape(N_CHIPS, D // N_CHIPS, D), d_wb.reshape(N_CHIPS, D // N_CHIPS, D),
              d_wout.reshape(N_CHIPS, D // N_CHIPS, D)]
    du, dv, d_ws, d_bs, d_lng, d_lnb, recv_proj = gating_bwd(z, dya, ln_g, ln_b, w_s, bs_t, d_proj if on_mesh else [])
    early = d_proj + d_ff
    parts_early = add(BIG[1:], early, list(recv_proj) + list(recv_ff)) if on_mesh else []
    small = dict(b_gate=d_bg, ln_v_g=d_lng, ln_v_b=d_lnb, w_s=d_ws, b_s=d_bs[:, 0, :],
                 norm_mix_post=d_gpost, norm_ffn_pre=d_gfpre, norm_ffn_post=d_gfpost)
    packed = pack_small(dict(small, norm_mix_pre=jnp.zeros((1, D), F32)), loss, where) if on_mesh else None
    dq, dk, dvb, got_early, packed = attn_bwd(z, yb, dyb, lse, logc, ka, kb, parts_early, packed)
    dz = jnp.concatenate([du, dv, dq, dk, dvb, dga, dgb], axis=1)
    if on_mesh:
        for_sibling, _ = dw_in_half("dw_in_sibling", h, dz, 1 - core, [])
        mine, from_sibling = dw_in_half("dw_in_mine", h, dz, core, [for_sibling])
        d_win = None
        parts_late = [add_halves("add_w_in", mine, from_sibling[0], jnp.zeros((1,), jnp.int32), 256)]
    else:
        half = IN_SHARD // 2
        d_win = mm_tn("dw_in", h, dz, D, half, (N_CHIPS, D, IN_SHARD),
                      pl.BlockSpec((None, D, half), lambda i, j: (j // 2, 0, j % 2)))
        parts_late = []
    dx, d_gpre, got_late, _ = in_bwd_norm(dz, wg_in, x, dx1, g_pre, parts_late)
    small["norm_mix_pre"] = d_gpre
    return (loss[0, 0], dx, [d_win] + early, small, parts_late + parts_early, list(got_late) + list(got_early),
            packed)


BIG = ("w_in", "w_a_proj", "w_b_proj", "w_out", "w_ff1", "w_ff2")
SMALL = ("norm_mix_pre", "ln_v_g", "ln_v_b", "b_s", "norm_mix_post", "norm_ffn_pre", "norm_ffn_post", "w_s", "b_gate")
ORDER = ("norm_mix_pre", "w_in", "b_gate", "ln_v_g", "ln_v_b", "w_s", "b_s", "w_a_proj", "w_b_proj", "w_out",
         "norm_mix_post", "norm_ffn_pre", "w_ff1", "w_ff2", "norm_ffn_post")
VEC_ROWS = D // 128
WS_ROW = 7 * VEC_ROWS
BG_ROW = WS_ROW + GROUPS * CHUNK
LOSS_ROW = BG_ROW + 2 * VEC_ROWS
PACK_ROWS = LOSS_ROW + 8


def pack_small(small, loss, where):
    vectors = [small[n] for n in SMALL[:7]]
    operands = vectors + [small["w_s"], small["b_gate"], loss]

    def body(where_ref, *refs):
        out = refs[-1]
        ws_ref, bg_ref, loss_ref = refs[7:10]
        for i, n in enumerate(SMALL[:7]):
            if n == "b_s":
                out[i * VEC_ROWS:(i + 1) * VEC_ROWS, :] = refs[i][...]
            else:
                for j in range(VEC_ROWS):
                    out[i * VEC_ROWS + j:i * VEC_ROWS + j + 1, :] = refs[i][:, j * 128:(j + 1) * 128]
        for g in range(GROUPS):
            out[WS_ROW + g * CHUNK:WS_ROW + (g + 1) * CHUNK, :] = ws_ref[g]
        for r in range(2):
            for j in range(VEC_ROWS):
                row = BG_ROW + r * VEC_ROWS + j
                out[row:row + 1, :] = bg_ref[r:r + 1, j * 128:(j + 1) * 128]
        lane = lax.broadcasted_iota(jnp.int32, (8, 128), 1)
        sub = lax.broadcasted_iota(jnp.int32, (8, 128), 0)
        out[LOSS_ROW:LOSS_ROW + 8, :] = jnp.where((lane == 0) & (sub == 0), loss_ref[...], 0.0)

    return pl.pallas_call(
        body, name="pack_small",
        grid_spec=pltpu.PrefetchScalarGridSpec(
            num_scalar_prefetch=1, grid=(1,), in_specs=[_full(a.shape) for a in operands],
            out_specs=pl.BlockSpec((None, PACK_ROWS, 128), lambda i, w: (w[0], w[1], 0))),
        out_shape=jax.ShapeDtypeStruct((N_CHIPS, 2 * PACK_ROWS, 128), F32), compiler_params=_params("arbitrary"),
    )(where, *operands)


def pack_vector(vec, where):
    def body(where_ref, v_ref, out):
        for j in range(VEC_ROWS):
            out[j:j + 1, :] = v_ref[:, j * 128:(j + 1) * 128]

    return pl.pallas_call(
        body, name="pack_vector",
        grid_spec=pltpu.PrefetchScalarGridSpec(
            num_scalar_prefetch=1, grid=(1,), in_specs=[_full(vec.shape)],
            out_specs=pl.BlockSpec((None, VEC_ROWS, 128), lambda i, w: (w[0], w[1], 0))),
        out_shape=jax.ShapeDtypeStruct((N_CHIPS, 2 * VEC_ROWS, 128), F32), compiler_params=_params("arbitrary"),
    )(where, vec)


def adamw_small(gathered, first, chip, w, m, v):
    shapes = {n: (1, D) for n in SMALL}
    shapes.update(b_s=(GROUPS, CHUNK), w_s=(GROUPS * CHUNK, CHUNK), b_gate=(2, D // N_CHIPS))
    flat = lambda t: [t[n].reshape(shapes[n]) for n in SMALL]
    per = D // N_CHIPS // 128

    def body(chip_ref, all_ref, first_ref, *refs):
        params, outs = refs[:27], refs[27:]
        sub = lax.broadcasted_iota(jnp.int32, (VEC_ROWS, 128), 0)
        sum_ref = outs[36]
        total = all_ref[0, 0:PACK_ROWS, :]
        head = first_ref[0, 0:VEC_ROWS, :]
        for k in range(1, 2 * N_CHIPS):
            total = total + all_ref[k // 2, (k % 2) * PACK_ROWS:(k % 2 + 1) * PACK_ROWS, :]
            head = head + first_ref[k // 2, (k % 2) * VEC_ROWS:(k % 2 + 1) * VEC_ROWS, :]
        sum_ref[...] = total
        sum_ref[0:VEC_ROWS, :] = head

        def gate_row(r):
            rows = sum_ref[BG_ROW + r * VEC_ROWS:BG_ROW + (r + 1) * VEC_ROWS, :]
            return jnp.concatenate([jnp.sum(jnp.where(sub == per * chip_ref[0] + j, rows, 0.0), axis=0, keepdims=True)
                                    for j in range(per)], axis=1)

        for i, n in enumerate(SMALL):
            if n == "b_s":
                g = sum_ref[i * VEC_ROWS:(i + 1) * VEC_ROWS, :]
            elif n == "w_s":
                g = sum_ref[WS_ROW:BG_ROW, :]
            elif n == "b_gate":
                g = jnp.concatenate([gate_row(0), gate_row(1)], axis=0)
            else:
                g = jnp.concatenate([sum_ref[i * VEC_ROWS + j:i * VEC_ROWS + j + 1, :] for j in range(VEC_ROWS)],
                                    axis=1)
            delta, nm, nv = _adamw_math(params[i][...], g, params[9 + i][...], params[18 + i][...])
            outs[4 * i][...], outs[4 * i + 1][...], outs[4 * i + 2][...], outs[4 * i + 3][...] = g, delta, nm, nv

    vm = pl.BlockSpec(memory_space=pltpu.VMEM)
    res = pl.pallas_call(
        body, name="adamw_small",
        in_specs=[pl.BlockSpec(memory_space=pltpu.SMEM)] + [vm] * 29, out_specs=[vm] * 37,
        out_shape=[jax.ShapeDtypeStruct(shapes[n], F32) for n in SMALL for _ in range(4)]
        + [jax.ShapeDtypeStruct((PACK_ROWS, 128), F32)],
        compiler_params=_params(),
    )(chip, gathered, first, *flat(w), *flat(m), *flat(v))
    new = {n: tuple(r.reshape(w[n].shape) for r in res[4 * i:4 * i + 4]) for i, n in enumerate(SMALL)}
    return new, res[36][LOSS_ROW, 0]


def kernel(x, norm_mix_pre, w_in, b_gate, ln_v_g, ln_v_b, w_s, b_s, w_a_proj, w_b_proj, w_out, norm_mix_post, norm_ffn_pre, w_ff1, w_ff2, norm_ffn_post, loss_target, m_norm_mix_pre, m_w_in, m_b_gate, m_ln_v_g, m_ln_v_b, m_w_s, m_b_s, m_w_a_proj, m_w_b_proj, m_w_out, m_norm_mix_post, m_norm_ffn_pre, m_w_ff1, m_w_ff2, m_norm_ffn_post, v_norm_mix_pre, v_w_in, v_b_gate, v_ln_v_g, v_ln_v_b, v_w_s, v_b_s, v_w_a_proj, v_w_b_proj, v_w_out, v_norm_mix_post, v_norm_ffn_pre, v_w_ff1, v_w_ff2, v_norm_ffn_post):
    w = dict(norm_mix_pre=norm_mix_pre, w_in=w_in, b_gate=b_gate, ln_v_g=ln_v_g, ln_v_b=ln_v_b, w_s=w_s, b_s=b_s,
             w_a_proj=w_a_proj, w_b_proj=w_b_proj, w_out=w_out, norm_mix_post=norm_mix_post,
             norm_ffn_pre=norm_ffn_pre, w_ff1=w_ff1, w_ff2=w_ff2, norm_ffn_post=norm_ffn_post)
    m = dict(norm_mix_pre=m_norm_mix_pre, w_in=m_w_in, b_gate=m_b_gate, ln_v_g=m_ln_v_g, ln_v_b=m_ln_v_b, w_s=m_w_s,
             b_s=m_b_s, w_a_proj=m_w_a_proj, w_b_proj=m_w_b_proj, w_out=m_w_out, norm_mix_post=m_norm_mix_post,
             norm_ffn_pre=m_norm_ffn_pre, w_ff1=m_w_ff1, w_ff2=m_w_ff2, norm_ffn_post=m_norm_ffn_post)
    v = dict(norm_mix_pre=v_norm_mix_pre, w_in=v_w_in, b_gate=v_b_gate, ln_v_g=v_ln_v_g, ln_v_b=v_ln_v_b, w_s=v_w_s,
             b_s=v_b_s, w_a_proj=v_w_a_proj, w_b_proj=v_w_b_proj, w_out=v_w_out, norm_mix_post=v_norm_mix_post,
             norm_ffn_pre=v_norm_ffn_pre, w_ff1=v_w_ff1, w_ff2=v_w_ff2, norm_ffn_post=v_norm_ffn_post)
    chip = 2 * lax.axis_index("x") + lax.axis_index("y")
    core = lax.axis_index("c")

    where = jnp.stack([chip, core]).astype(jnp.int32)
    wg_in = place_shard("place_w_in", w_in[0], where, BF16, 256)
    bg_all = place_shard("place_b_gate", jnp.pad(b_gate[0], ((0, 14), (0, 0))), where, F32, 16)
    vecs = (norm_mix_pre, ln_v_g, ln_v_b, norm_mix_post, norm_ffn_pre, norm_ffn_post)
    loss, dx, _, small, parts, got, packed = local_step(
        x[0], loss_target[0], vecs, w_s[0], b_s[0].T, bg_all, wg_in, [w[n][0] for n in BIG[1:]],
        core=jnp.reshape(core, (1,)).astype(jnp.int32),
        order=jnp.stack([chip, chip ^ 2, chip ^ 1, chip ^ 3]).astype(jnp.int32), where=where)

    halves = [sum_chips("sum_" + n, p, r, where, min(p.shape[1], 256)) for n, p, r in zip(BIG, parts, got)]
    joined, first = join_halves(halves, pack_vector(small["norm_mix_pre"], where))
    grads = dict(zip(BIG, joined))
    new = {}
    for n in BIG:
        shape = w[n].shape
        res = adamw("adamw_" + n, w[n][0], grads[n], m[n][0], v[n][0], min(shape[1], 256))
        new[n] = tuple(r.reshape(shape) for r in res)
    small_new, loss = adamw_small(packed, first, jnp.reshape(chip, (1,)).astype(jnp.int32), w, m, v)
    new.update(small_new)

    outs = [loss, dx[None]]
    for i in range(4):
        outs += [new[n][i] for n in ORDER]
    return tuple(outs)
```

```python
import functools
import math
import typing

import numpy as np
import jax
import jax.numpy as jnp
from jax import lax
from jax.experimental import pallas as pl
from jax.experimental.pallas import tpu as pltpu

F32 = jnp.float32
BF16 = jnp.bfloat16
MESH = pl.DeviceIdType.MESH

D = 1024
EPS = 1e-6
CHUNK = 128
GROUPS = 8
HEADS = 16
HEAD_DIM = 64
ATT_T = 256
ATT_GROUP = 8
ATT_BWD_GROUP = 4
N_CHIPS = 4
D_FF = 4 * D
IN_COLS = 7 * D
IN_SHARD = IN_COLS // N_CHIPS
MASKED = -1e30
VMEM_LIMIT = 56 * 2 ** 20

ADAM_LR, ADAM_B1, ADAM_B2, ADAM_EPS, ADAM_WD, ADAM_STEP = 0.001, 0.9, 0.999, 1e-08, 0.01, 10

NN = (((1,), (0,)), ((), ()))
NT = (((1,), (1,)), ((), ()))
TN = (((0,), (0,)), ((), ()))


def _dot(a, b, dims=NN):
    return lax.dot_general(a, b, dims, preferred_element_type=F32)


def _params(*sem, communicates=False):
    return pltpu.CompilerParams(dimension_semantics=sem or None, vmem_limit_bytes=VMEM_LIMIT,
                                has_side_effects=communicates)


def _rows(tr, c, col=0):
    return pl.BlockSpec((tr, c), lambda i: (i, col))


def _full(shape):
    n = len(shape)
    return pl.BlockSpec(shape, lambda *_: (0,) * n)


def _gelu(x):
    k = math.sqrt(2.0 / math.pi)
    return 0.5 * x * (1.0 + jnp.tanh(k * (x + 0.044715 * x * x * x)))


def _gelu_and_grad(x):
    k = math.sqrt(2.0 / math.pi)
    t = jnp.tanh(k * (x + 0.044715 * x * x * x))
    g = 0.5 * x * (1.0 + t)
    dg = 0.5 * (1.0 + t) + 0.5 * x * (1.0 - t * t) * (k * (1.0 + 3.0 * 0.044715 * x * x))
    return g, dg


def _sigmoid(x):
    return 1.0 / (1.0 + jnp.exp(-x))


def _rms(x):
    r = lax.rsqrt(jnp.mean(x * x, axis=-1, keepdims=True) + EPS)
    return x * r, r


def _rms_bwd(dn, xhat, r):
    return r * (dn - xhat * jnp.mean(dn * xhat, axis=-1, keepdims=True))


def norm_pre(x, g):
    s = x.shape[0]
    tr = 512

    def body(x_ref, g_ref, h_ref):
        xhat, _ = _rms(x_ref[...])
        h_ref[...] = (xhat * g_ref[...]).astype(BF16)

    return pl.pallas_call(
        body, name="norm_pre", grid=(s // tr,),
        in_specs=[_rows(tr, D), _full((1, D))], out_specs=_rows(tr, D),
        out_shape=jax.ShapeDtypeStruct((s, D), BF16), compiler_params=_params("parallel"),
    )(x, g)


def mm_in(h, wg, order, staged, casting=()):
    s = h.shape[0]
    tm, tn = 1024, IN_SHARD // 2
    per = IN_SHARD // tn
    m = len(casting)
    nj, ni = N_CHIPS * per, s // tm
    cast_steps = per * ni

    def body(order_ref, *refs):
        a_ref = refs[0]
        cast_in = refs[2:2 + m]
        o_ref, held = refs[2 + m], refs[3 + m]
        cast_out = refs[4 + m:4 + 2 * m]
        tile, tile_sem = refs[4 + 2 * m:6 + 2 * m]
        sems = refs[6 + 2 * m:]
        j, i = pl.program_id(0), pl.program_id(1)

        @pl.when(j * ni + i < cast_steps)
        def _():
            for src, dst in zip(cast_in, cast_out):
                dst[...] = src[...].astype(BF16)

        def fetch(t):
            chip = order_ref[t // per]
            return pltpu.make_async_copy(held.at[chip, :, pl.ds((t % per) * tn, tn)], tile.at[t % 2],
                                         tile_sem.at[t % 2])

        if staged:
            near = _gather_phases([held], *sems[:2], [(0, D, (0, 1))])
            far = _relay_phases(held, *sems[2:])

        @pl.when(i == 0)
        def _():
            @pl.when(j == 0)
            def _():
                if staged:
                    near[0]()
                fetch(0).start()

            fetch(j).wait()
            ahead = j + 1 < nj
            if staged:
                ahead = ahead & (j + 1 != per) & (j + 1 != 3 * per)
            pl.when(ahead)(lambda: fetch(j + 1).start())

        rows = pl.ds(pl.multiple_of(i * tm, tm), tm)
        o_ref[...] = _dot(a_ref[rows, :], tile[j % 2]).astype(BF16)

        if staged:
            @pl.when((i == ni - 1) & (j == per - 1))
            def _():
                near[1]()
                near[2]()
                far[0]()
                fetch(per).start()

            @pl.when((i == ni - 1) & (j == 3 * per - 1))
            def _():
                far[1]()
                far[2]()
                fetch(3 * per).start()

    def cast_block(j, i, o):
        return jnp.minimum(j * ni + i, cast_steps - 1)

    out = pl.pallas_call(
        body, name="mm_in",
        grid_spec=pltpu.PrefetchScalarGridSpec(
            num_scalar_prefetch=1, grid=(nj, ni),
            in_specs=[pl.BlockSpec((s, D), lambda j, i, o: (0, 0)), ANY]
            + [pl.BlockSpec((a.shape[0] // cast_steps, a.shape[1]), lambda j, i, o: (cast_block(j, i, o), 0))
               for a in casting],
            out_specs=[pl.BlockSpec((tm, tn), lambda j, i, o: (i, o[j // per] * per + j % per)), ANY]
            + [pl.BlockSpec((None, a.shape[0] // cast_steps, a.shape[1]),
                            lambda j, i, o: (o[0], cast_block(j, i, o), 0)) for a in casting],
            scratch_shapes=[pltpu.VMEM((2, D, tn), BF16), pltpu.SemaphoreType.DMA((2,))]
            + (_gather_sems(1) + _relay_sems() if staged else [])),
        out_shape=[jax.ShapeDtypeStruct((s, IN_COLS), BF16), jax.ShapeDtypeStruct(wg.shape, wg.dtype)]
        + [jax.ShapeDtypeStruct((N_CHIPS,) + a.shape, BF16) for a in casting],
        input_output_aliases={2: 1},
        compiler_params=_params("arbitrary", "arbitrary", communicates=staged),
    )(order, h, wg, *casting)
    return out[0], out[1], out[2:]


def _tril_ws(ws_ref, g):
    r = lax.broadcasted_iota(jnp.int32, (CHUNK, CHUNK), 0)
    c = lax.broadcasted_iota(jnp.int32, (CHUNK, CHUNK), 1)
    return jnp.where(c <= r, ws_ref[g], 0.0).astype(BF16)


def _layer_norm(v):
    mu = jnp.mean(v, axis=-1, keepdims=True)
    d = v - mu
    rstd = lax.rsqrt(jnp.mean(d * d, axis=-1, keepdims=True) + EPS)
    return d * rstd, rstd


def gating_fwd(z, ln_g, ln_b, w_s, bs_t, gathering):
    s = z.shape[0]
    n = len(gathering)
    steps = s // CHUNK

    def body(*refs):
        u_ref, v_ref, lg_ref, lb_ref, ws_ref, bst_ref = refs[:6]
        ya_ref = refs[6 + n]
        ci = pl.program_id(0)
        if n:
            send, pass_on, finish = _gather_phases(refs[7 + n:7 + 2 * n], *refs[7 + 2 * n:], _spans(gathering))
            pl.when(ci == 0)(send)
        ug = _gelu(u_ref[...].astype(F32))
        vhat, _ = _layer_norm(_gelu(v_ref[...].astype(F32)))
        vn = (vhat * lg_ref[...] + lb_ref[...]).astype(BF16)
        for g in range(GROUPS):
            cols = slice(g * CHUNK, (g + 1) * CHUNK)
            mixed = _dot(_tril_ws(ws_ref, g), vn[:, cols]) + bst_ref[:, g:g + 1]
            ya_ref[:, cols] = (ug[:, cols] * mixed).astype(BF16)
        if n:
            pl.when(ci == steps - 1)(pass_on)
            pl.when(ci == steps - 1)(finish)

    out = pl.pallas_call(
        body, name="gating_fwd", grid=(steps,),
        in_specs=[_rows(CHUNK, D, 0), _rows(CHUNK, D, 1), _full((1, D)), _full((1, D)),
                  _full((GROUPS, CHUNK, CHUNK)), _full((CHUNK, GROUPS))] + [ANY] * n,
        out_specs=[_rows(CHUNK, D)] + [ANY] * n,
        out_shape=[jax.ShapeDtypeStruct((s, D), BF16)]
        + [jax.ShapeDtypeStruct(a.shape, a.dtype) for a in _arrays(gathering)],
        input_output_aliases={6 + w: 1 + w for w in range(n)},
        scratch_shapes=_gather_sems(n) if n else [],
        compiler_params=_params("arbitrary", communicates=bool(n)),
    )(z, z, ln_g, ln_b, w_s, bs_t, *_arrays(gathering))
    return out[0], out[1:]


def _attn_tables(s):
    nd = s // ATT_T
    r = np.arange(ATT_T)[None, :, None]
    c = np.arange(ATT_T)[None, None, :]
    delta = np.arange(nd)[:, None, None] * ATT_T + r - c
    count = np.zeros(delta.shape, np.int64)
    for window, dilation in ((128, 1), (512, 4), (2048, 16)):
        count += (delta >= 0) & (delta % dilation == 0) & (delta <= window)
    logc = np.where(count > 0, np.log(np.maximum(count, 1)), MASKED)
    return jnp.asarray(logc, F32)


AUG = 3


def _split3_np(x):
    terms, rest = [], np.asarray(x, np.float64)
    for _ in range(AUG):
        term = np.asarray(rest.astype(jnp.bfloat16), np.float64)
        terms.append(term)
        rest = rest - term
    return terms


def _split3(x):
    terms, rest = [], x
    for _ in range(AUG):
        term = rest.astype(BF16).astype(F32)
        terms.append(term)
        rest = rest - term
    return terms


def _alibi_tables(s):
    nb = s // ATT_T
    slopes = np.exp2(-8.0 * np.arange(1, HEADS + 1, dtype=np.float64) / HEADS)
    ka = np.zeros((HEADS // 2, 2, ATT_T, 128), np.float32)
    kb = np.zeros((HEADS // 2, 2, nb, 128), np.float32)
    for p in range(HEADS // 2):
        for e in range(2):
            base = HEAD_DIM * (1 - e)
            for a, term in enumerate(_split3_np(slopes[2 * p + e] * np.arange(ATT_T))):
                ka[p, e, :, base + a] = term
            for a, term in enumerate(_split3_np(slopes[2 * p + e] * ATT_T * np.arange(nb))):
                kb[p, e, :, base + AUG + a] = term
            ka[p, e, :, base + 2 * AUG:base + 3 * AUG] = 1.0
    return jnp.asarray(ka), jnp.asarray(kb)


def _head_masks():
    lane = lax.broadcasted_iota(jnp.int32, (1, 128), 1)
    first = lane < HEAD_DIM

    def ones(e, n):
        base = HEAD_DIM * (1 - e)
        return ((lane >= base) & (lane < base + n)).astype(F32)

    return first, lane, ones


def _place3(lane, at, terms, other):
    for a, term in enumerate(terms):
        other = jnp.where(lane == at + a, term, other)
    return other


def attn_fwd(z, logc, ka, kb, gathering):
    s = z.shape[0]
    nq = s // ATT_T
    t = ATT_T
    n = len(gathering)
    grp = ATT_GROUP
    ngrp = HEADS // 2 // grp
    wide = 128 * grp
    qcol, kcol, vcol = 2 * D // wide, 3 * D // wide, 4 * D // wide

    def body(*refs):
        q_ref, k_ref, v_ref, lc_ref, ka_ref, kb_ref = refs[:6]
        y_ref, lse_ref = refs[6 + n:8 + n]
        q_s, k_s, v_s, m_s, l_s, acc_s = refs[8 + 2 * n:14 + 2 * n]
        gi, qi = pl.program_id(0), pl.program_id(1)
        first, lane, ones = _head_masks()
        if n:
            send, pass_on, finish = _gather_phases(refs[8 + n:8 + 2 * n], *refs[14 + 2 * n:], _spans(gathering))
            pl.when((gi == 0) & (qi == 0))(send)

        @pl.when(qi == 0)
        def _():
            sel = jnp.broadcast_to(first.astype(F32), (t, 128))
            for pr in range(grp):
                cols = slice(pr * 128, (pr + 1) * 128)
                for jb in range(nq):
                    kj = k_ref[jb * t:(jb + 1) * t, cols].astype(F32)
                    vj = v_ref[jb * t:(jb + 1) * t, cols].astype(F32)
                    k_s[pr, 0, jb] = jnp.where(first, kj, ka_ref[pr, 0] + kb_ref[pr, 0, jb:jb + 1, :]).astype(BF16)
                    k_s[pr, 1, jb] = jnp.where(first, ka_ref[pr, 1] + kb_ref[pr, 1, jb:jb + 1, :], kj).astype(BF16)
                    v_s[pr, jb, 0:t, 0:128] = jnp.where(first, vj, 0.0).astype(BF16)
                    v_s[pr, jb, t:2 * t, 0:128] = jnp.where(first, 0.0, vj).astype(BF16)
                    v_s[pr, jb, 0:t, 128:256] = sel.astype(BF16)
                    v_s[pr, jb, t:2 * t, 128:256] = (1.0 - sel).astype(BF16)

        for pr in range(grp):
            q = q_ref[:, pr * 128:(pr + 1) * 128].astype(F32) * (1.0 / math.sqrt(HEAD_DIM))
            q_s[pr, 0] = jnp.where(first, q, ones(0, 2 * AUG)).astype(BF16)
            q_s[pr, 1] = jnp.where(first, ones(1, 2 * AUG), q).astype(BF16)
        m_s[...] = jnp.full_like(m_s, MASKED)
        l_s[...] = jnp.zeros_like(l_s)
        acc_s[...] = jnp.zeros_like(acc_s)

        def scores(j):
            return tuple(_dot(q_s[pr, e], k_s[pr, e, j], NT) for pr in range(grp) for e in range(2))

        def step(j, carry):
            softmax_block(j, scores(j))
            return carry

        def softmax_block(j, u):
            lc = lc_ref[qi - j]
            for pr in range(grp):
                u0 = u[2 * pr] + lc
                u1 = u[2 * pr + 1] + lc
                m0, m1 = m_s[pr, 0], m_s[pr, 1]
                n0 = jnp.maximum(m0, jnp.max(u0, axis=-1, keepdims=True))
                n1 = jnp.maximum(m1, jnp.max(u1, axis=-1, keepdims=True))
                m_s[pr, 0], m_s[pr, 1] = n0, n1
                p = jnp.concatenate([jnp.exp(u0 - jnp.concatenate([n0, n0], axis=1)).astype(BF16),
                                     jnp.exp(u1 - jnp.concatenate([n1, n1], axis=1)).astype(BF16)], axis=1)
                pv = _dot(p, v_s[pr, j])
                alpha = jnp.where(first, jnp.exp(m0 - n0), jnp.exp(m1 - n1))
                acc_s[pr] = acc_s[pr] * alpha + pv[:, 0:128]
                l_s[pr] = l_s[pr] * alpha + pv[:, 128:256]

        lax.fori_loop(0, qi + 1, step, 0)
        for pr in range(grp):
            cols = slice(pr * 128, (pr + 1) * 128)
            y_ref[:, cols] = (acc_s[pr] / l_s[pr]).astype(BF16)
            lse_ref[:, cols] = jnp.where(first, m_s[pr, 0], m_s[pr, 1]) + jnp.log(l_s[pr])
        if n:
            pl.when((gi == ngrp - 1) & (qi == nq - 1))(pass_on)
            pl.when((gi == ngrp - 1) & (qi == nq - 1))(finish)

    out = pl.pallas_call(
        body, name="attn_fwd", grid=(ngrp, nq),
        in_specs=[pl.BlockSpec((t, wide), lambda g, i: (i, qcol + g)),
                  pl.BlockSpec((s, wide), lambda g, i: (0, kcol + g)),
                  pl.BlockSpec((s, wide), lambda g, i: (0, vcol + g)),
                  _full((nq, t, t)),
                  pl.BlockSpec((grp, 2, t, 128), lambda g, i: (g, 0, 0, 0)),
                  pl.BlockSpec((grp, 2, nq, 128), lambda g, i: (g, 0, 0, 0))] + [ANY] * n,
        out_specs=[pl.BlockSpec((t, wide), lambda g, i: (i, g)), pl.BlockSpec((t, wide), lambda g, i: (i, g))]
        + [ANY] * n,
        out_shape=[jax.ShapeDtypeStruct((s, D), BF16), jax.ShapeDtypeStruct((s, D), F32)]
        + [jax.ShapeDtypeStruct(a.shape, a.dtype) for a in _arrays(gathering)],
        input_output_aliases={6 + w: 2 + w for w in range(n)},
        scratch_shapes=[pltpu.VMEM((grp, 2, t, 128), BF16), pltpu.VMEM((grp, 2, nq, t, 128), BF16),
                        pltpu.VMEM((grp, nq, 2 * t, 256), BF16), pltpu.VMEM((grp, 2, t, 128), F32),
                        pltpu.VMEM((grp, t, 128), F32), pltpu.VMEM((grp, t, 128), F32)]
        + (_gather_sems(n) if n else []),
        compiler_params=_params("arbitrary", "arbitrary", communicates=bool(n)),
    )(z, z, z, logc, ka, kb, *_arrays(gathering))
    return out[0], out[1], out[2:]


def proj_merge(ya, yb, wa, wb, z, bg, gathering):
    s = ya.shape[0]
    tm = 512
    n = len(gathering)
    steps = s // tm

    def body(*refs):
        ya_ref, yb_ref, wa_ref, wb_ref, ga_ref, gb_ref, bg_ref = refs[:7]
        mg_ref, pa_ref, pb_ref = refs[7 + n:10 + n]
        i = pl.program_id(0)
        if n:
            send, pass_on, finish = _gather_phases(refs[10 + n:10 + 2 * n], *refs[10 + 2 * n:], _spans(gathering))
            pl.when(i == 0)(send)
            pl.when(i == steps - 1)(pass_on)
        pa = _dot(ya_ref[...], wa_ref[...])
        pb = _dot(yb_ref[...], wb_ref[...])
        sa = _sigmoid(ga_ref[...] + bg_ref[0:1, :])
        sb = _sigmoid(gb_ref[...] + bg_ref[1:2, :])
        mg_ref[...] = (sa * pa + sb * pb).astype(BF16)
        pa_ref[...] = pa.astype(BF16)
        pb_ref[...] = pb.astype(BF16)
        if n:
            pl.when(i == steps - 1)(finish)

    out = jax.ShapeDtypeStruct((s, D), BF16)
    res = pl.pallas_call(
        body, name="proj_merge", grid=(steps,),
        in_specs=[_rows(tm, D), _rows(tm, D), _full((D, D)), _full((D, D)),
                  _rows(tm, D, 5), _rows(tm, D, 6), _full((2, D))] + [ANY] * n,
        out_specs=[_rows(tm, D)] * 3 + [ANY] * n,
        out_shape=[out] * 3 + [jax.ShapeDtypeStruct(a.shape, a.dtype) for a in _arrays(gathering)],
        input_output_aliases={7 + w: 3 + w for w in range(n)},
        scratch_shapes=_gather_sems(n) if n else [],
        compiler_params=_params("arbitrary", communicates=bool(n)),
    )(ya, yb, wa, wb, z, z, bg, *_arrays(gathering))
    return res[0], res[1], res[2], res[3:]


def out_norm(merged, w_out, x, g_post, g_fpre, gathering):
    s = x.shape[0]
    tm = 512
    n = len(gathering)
    steps = s // tm

    def body(*refs):
        mg_ref, w_ref, x_ref, gp_ref, gf_ref = refs[:5]
        o_ref, x1_ref, h2_ref = refs[5 + n:8 + n]
        i = pl.program_id(0)
        if n:
            send, pass_on, finish = _gather_phases(refs[8 + n:8 + 2 * n], *refs[8 + 2 * n:], _spans(gathering))
            pl.when(i == 0)(send)
            pl.when(i == steps - 1)(pass_on)
        o = _dot(mg_ref[...], w_ref[...])
        ohat, _ = _rms(o)
        x1 = x_ref[...] + ohat * gp_ref[...]
        x1hat, _ = _rms(x1)
        o_ref[...] = o
        x1_ref[...] = x1
        h2_ref[...] = (x1hat * gf_ref[...]).astype(BF16)
        if n:
            pl.when(i == steps - 1)(finish)

    res = pl.pallas_call(
        body, name="out_norm", grid=(steps,),
        in_specs=[_rows(tm, D), _full((D, D)), _rows(tm, D), _full((1, D)), _full((1, D))] + [ANY] * n,
        out_specs=[_rows(tm, D)] * 3 + [ANY] * n,
        out_shape=[jax.ShapeDtypeStruct((s, D), F32), jax.ShapeDtypeStruct((s, D), F32),
                   jax.ShapeDtypeStruct((s, D), BF16)]
        + [jax.ShapeDtypeStruct(a.shape, a.dtype) for a in _arrays(gathering)],
        input_output_aliases={5 + w: 3 + w for w in range(n)},
        scratch_shapes=_gather_sems(n) if n else [],
        compiler_params=_params("arbitrary", communicates=bool(n)),
    )(merged, w_out, x, g_post, g_fpre, *_arrays(gathering))
    return res[0], res[1], res[2], res[3:]


def mm_ff1(h2, wg, gathering):
    s = h2.shape[0]
    tm = 1024
    n = len(gathering)
    ni = s // tm

    def body(*refs):
        a_ref, b_ref = refs[:2]
        o_ref, r_ref = refs[2 + n:4 + n]
        i, j = pl.program_id(0), pl.program_id(1)
        if n:
            send, pass_on, finish = _gather_phases(refs[4 + n:4 + 2 * n], *refs[4 + 2 * n:], _spans(gathering))
            pl.when((i == 0) & (j == 0))(send)
            pl.when((i == ni - 1) & (j == N_CHIPS // 2))(pass_on)
        a = _dot(a_ref[...], b_ref[...])
        o_ref[...] = a.astype(BF16)
        r = jnp.maximum(a, 0.0)
        r_ref[...] = (r * r).astype(BF16)
        if n:
            pl.when((i == ni - 1) & (j == N_CHIPS - 1))(finish)

    res = pl.pallas_call(
        body, name="mm_ff1", grid=(ni, N_CHIPS),
        in_specs=[pl.BlockSpec((tm, D), lambda i, j: (i, 0)), pl.BlockSpec((None, D, D), lambda i, j: (j, 0, 0))]
        + [ANY] * n,
        out_specs=[pl.BlockSpec((tm, D), lambda i, j: (i, j))] * 2 + [ANY] * n,
        out_shape=[jax.ShapeDtypeStruct((s, D_FF), BF16), jax.ShapeDtypeStruct((s, D_FF), BF16)]
        + [jax.ShapeDtypeStruct(a.shape, a.dtype) for a in _arrays(gathering)],
        input_output_aliases={2 + w: 2 + w for w in range(n)},
        scratch_shapes=_gather_sems(n) if n else [],
        compiler_params=_params("arbitrary", "arbitrary", communicates=bool(n)),
    )(h2, wg, *_arrays(gathering))
    return res[0], res[1], res[2:]


def ff2_loss(rl, w_ff2, x1, target, g_fpost):
    s = x1.shape[0]
    tm = 256

    def body(rl_ref, w_ref, x1_ref, t_ref, g_ref, dy_ref, df_ref, dg_ref, loss_ref):
        @pl.when(pl.program_id(0) == 0)
        def _():
            dg_ref[...] = jnp.zeros_like(dg_ref)
            loss_ref[...] = jnp.zeros_like(loss_ref)

        f = _dot(rl_ref[...], w_ref[...])
        fhat, r = _rms(f)
        err = x1_ref[...] + fhat * g_ref[...] - t_ref[...]
        loss_ref[...] += 0.5 * jnp.sum(jnp.mean(err * err, axis=-1, keepdims=True), axis=0, keepdims=True)
        dy = err * (1.0 / D)
        dy_ref[...] = dy
        dg_ref[...] += jnp.sum(dy * fhat, axis=0, keepdims=True)
        df_ref[...] = _rms_bwd(dy * g_ref[...], fhat, r).astype(BF16)

    return pl.pallas_call(
        body, name="ff2_loss", grid=(s // tm,),
        in_specs=[_rows(tm, D_FF), _full((D_FF, D)), _rows(tm, D), _rows(tm, D), _full((1, D))],
        out_specs=[_rows(tm, D), _rows(tm, D), _full((1, D)), _full((1, 1))],
        out_shape=[jax.ShapeDtypeStruct((s, D), F32), jax.ShapeDtypeStruct((s, D), BF16),
                   jax.ShapeDtypeStruct((1, D), F32), jax.ShapeDtypeStruct((1, 1), F32)],
        compiler_params=_params("arbitrary"),
    )(rl, w_ff2, x1, target, g_fpost)


def mm_tn(name, a, b, ta, tb, out_shape, out_spec):
    s = a.shape[0]

    def body(a_ref, b_ref, o_ref):
        o_ref[...] = _dot(a_ref[...], b_ref[...], TN)

    return pl.pallas_call(
        body, name=name, grid=(a.shape[1] // ta, b.shape[1] // tb),
        in_specs=[pl.BlockSpec((s, ta), lambda i, j: (0, i)), pl.BlockSpec((s, tb), lambda i, j: (0, j))],
        out_specs=out_spec, out_shape=jax.ShapeDtypeStruct(out_shape, F32),
        compiler_params=_params("parallel", "parallel"),
    )(a, b)


def mm_nt(name, a, w):
    s = a.shape[0]
    tm = 512

    def body(a_ref, w_ref, o_ref):
        o_ref[...] = _dot(a_ref[...], w_ref[...], NT).astype(BF16)

    return pl.pallas_call(
        body, name=name, grid=(s // tm,), in_specs=[_rows(tm, D), _full((D, D))], out_specs=_rows(tm, D),
        out_shape=jax.ShapeDtypeStruct((s, D), BF16), compiler_params=_params("parallel"),
    )(a, w)


def ff2_bwd(df, w_ff2, a):
    s = df.shape[0]
    tm = 1024

    def body(df_ref, w_ref, a_ref, da_ref):
        drl = _dot(df_ref[...], w_ref[...], NT)
        da_ref[...] = (drl * (2.0 * jnp.maximum(a_ref[...].astype(F32), 0.0))).astype(BF16)

    return pl.pallas_call(
        body, name="ff2_bwd", grid=(s // tm, D_FF // D),
        in_specs=[pl.BlockSpec((tm, D), lambda i, j: (i, 0)), pl.BlockSpec((D, D), lambda i, j: (j, 0)),
                  pl.BlockSpec((tm, D), lambda i, j: (i, j))],
        out_specs=pl.BlockSpec((tm, D), lambda i, j: (i, j)),
        out_shape=jax.ShapeDtypeStruct((s, D_FF), BF16), compiler_params=_params("parallel", "parallel"),
    )(df, w_ff2, a)


def ff1_bwd_norms(da, wg, x1, o, dy, g_fpre, g_post, swapping):
    s = x1.shape[0]
    tm = 256
    n = len(swapping)
    steps = s // tm

    def body(*refs):
        da_ref, w_ref, x1_ref, o_ref, dy_ref, gf_ref, gp_ref = refs[:7]
        dx1_ref, do_ref, dgf_ref, dgp_ref = refs[7 + n:11 + n]
        i = pl.program_id(0)
        if n:
            send, finish = _swap_phases(refs[7:7 + n], refs[11 + n:11 + 2 * n], *refs[11 + 2 * n:])
            pl.when(i == 0)(send)

        @pl.when(i == 0)
        def _():
            dgf_ref[...] = jnp.zeros_like(dgf_ref)
            dgp_ref[...] = jnp.zeros_like(dgp_ref)

        dh2 = _dot(da_ref[:, 0:D], w_ref[0], NT)
        for k in range(1, N_CHIPS):
            dh2 = dh2 + _dot(da_ref[:, k * D:(k + 1) * D], w_ref[k], NT)
        x1hat, r2 = _rms(x1_ref[...])
        dgf_ref[...] += jnp.sum(dh2 * x1hat, axis=0, keepdims=True)
        dx1 = dy_ref[...] + _rms_bwd(dh2 * gf_ref[...], x1hat, r2)
        ohat, r1 = _rms(o_ref[...])
        dgp_ref[...] += jnp.sum(dx1 * ohat, axis=0, keepdims=True)
        dx1_ref[...] = dx1
        do_ref[...] = _rms_bwd(dx1 * gp_ref[...], ohat, r1).astype(BF16)
        if n:
            pl.when(i == steps - 1)(finish)

    res = pl.pallas_call(
        body, name="ff1_bwd_norms", grid=(steps,),
        in_specs=[_rows(tm, D_FF), _full((N_CHIPS, D, D)), _rows(tm, D), _rows(tm, D), _rows(tm, D),
                  _full((1, D)), _full((1, D))] + [ANY] * n,
        out_specs=[_rows(tm, D), _rows(tm, D), _full((1, D)), _full((1, D))] + [ANY] * n,
        out_shape=[jax.ShapeDtypeStruct((s, D), F32), jax.ShapeDtypeStruct((s, D), BF16),
                   jax.ShapeDtypeStruct((1, D), F32), jax.ShapeDtypeStruct((1, D), F32)] + _swap_shapes(swapping),
        scratch_shapes=_swap_sems(n) if n else [],
        compiler_params=_params("arbitrary", communicates=bool(n)),
    )(da, wg, x1, o, dy, g_fpre, g_post, *swapping)
    return res[0], res[1], res[2], res[3], res[4:]


def out_bwd_gates(do, w_out, pa, pb, z, bg):
    s = do.shape[0]
    tm = 512

    def body(do_ref, w_ref, pa_ref, pb_ref, ga_ref, gb_ref, bg_ref, dpa_ref, dpb_ref, dga_ref, dgb_ref, dbg_ref):
        @pl.when(pl.program_id(0) == 0)
        def _():
            dbg_ref[...] = jnp.zeros_like(dbg_ref)

        dm = _dot(do_ref[...], w_ref[...], NT)
        sa = _sigmoid(ga_ref[...] + bg_ref[0:1, :])
        sb = _sigmoid(gb_ref[...] + bg_ref[1:2, :])
        dpa_ref[...] = (dm * sa).astype(BF16)
        dpb_ref[...] = (dm * sb).astype(BF16)
        dga = dm * pa_ref[...].astype(F32) * (sa * (1.0 - sa))
        dgb = dm * pb_ref[...].astype(F32) * (sb * (1.0 - sb))
        dga_ref[...] = dga.astype(BF16)
        dgb_ref[...] = dgb.astype(BF16)
        dbg_ref[0:1, :] += jnp.sum(dga, axis=0, keepdims=True)
        dbg_ref[1:2, :] += jnp.sum(dgb, axis=0, keepdims=True)

    out = jax.ShapeDtypeStruct((s, D), BF16)
    return pl.pallas_call(
        body, name="out_bwd_gates", grid=(s // tm,),
        in_specs=[_rows(tm, D), _full((D, D)), _rows(tm, D), _rows(tm, D), _rows(tm, D, 5), _rows(tm, D, 6),
                  _full((2, D))],
        out_specs=[_rows(tm, D)] * 4 + [_full((2, D))],
        out_shape=[out] * 4 + [jax.ShapeDtypeStruct((2, D), F32)], compiler_params=_params("arbitrary"),
    )(do, w_out, pa, pb, z, z, bg)


def gating_bwd(z, dya, ln_g, ln_b, w_s, bs_t, swapping):
    s = z.shape[0]
    ones = functools.partial(jnp.ones, (8, CHUNK), BF16)
    n = len(swapping)

    def body(*refs):
        u_ref, v_ref, dya_ref, lg_ref, lb_ref, ws_ref, bst_ref = refs[:7]
        du_ref, dv_ref, dws_ref, dbs_ref, dlg_ref, dlb_ref = refs[7 + n:13 + n]
        dvn_ref = refs[13 + 2 * n]
        ci = pl.program_id(0)
        if n:
            send, finish = _swap_phases(refs[7:7 + n], refs[13 + n:13 + 2 * n], *refs[14 + 2 * n:])
            pl.when(ci == 0)(send)

        @pl.when(ci == 0)
        def _():
            dws_ref[...] = jnp.zeros_like(dws_ref)
            dbs_ref[...] = jnp.zeros_like(dbs_ref)
            dlg_ref[...] = jnp.zeros_like(dlg_ref)
            dlb_ref[...] = jnp.zeros_like(dlb_ref)

        ug, dug_du = _gelu_and_grad(u_ref[...].astype(F32))
        vg, dvg_dv = _gelu_and_grad(v_ref[...].astype(F32))
        vhat, rstd = _layer_norm(vg)
        vn = (vhat * lg_ref[...] + lb_ref[...]).astype(BF16)
        dya = dya_ref[...].astype(F32)
        for g in range(GROUPS):
            cols = slice(g * CHUNK, (g + 1) * CHUNK)
            ws = _tril_ws(ws_ref, g)
            mixed = _dot(ws, vn[:, cols]) + bst_ref[:, g:g + 1]
            du_ref[:, cols] = (dya[:, cols] * mixed * dug_du[:, cols]).astype(BF16)
            dmix = (dya[:, cols] * ug[:, cols]).astype(BF16)
            dbs_ref[g] += _dot(ones(), dmix, NT)
            dws_ref[g] += _dot(dmix, vn[:, cols], NT)
            dvn_ref[:, cols] = _dot(ws, dmix, TN)
        dvn = dvn_ref[...]
        dlg_ref[...] += jnp.sum(dvn * vhat, axis=0, keepdims=True)
        dlb_ref[...] += jnp.sum(dvn, axis=0, keepdims=True)
        dvh = dvn * lg_ref[...]
        dvg = rstd * (dvh - jnp.mean(dvh, axis=-1, keepdims=True)
                      - vhat * jnp.mean(dvh * vhat, axis=-1, keepdims=True))
        dv_ref[...] = (dvg * dvg_dv).astype(BF16)

        @pl.when(ci == pl.num_programs(0) - 1)
        def _():
            r = lax.broadcasted_iota(jnp.int32, (CHUNK, CHUNK), 0)
            c = lax.broadcasted_iota(jnp.int32, (CHUNK, CHUNK), 1)
            for g in range(GROUPS):
                dws_ref[g] = jnp.where(c <= r, dws_ref[g], 0.0)

        if n:
            pl.when(ci == pl.num_programs(0) - 1)(finish)

    out = jax.ShapeDtypeStruct((s, D), BF16)
    res = pl.pallas_call(
        body, name="gating_bwd", grid=(s // CHUNK,),
        in_specs=[_rows(CHUNK, D, 0), _rows(CHUNK, D, 1), _rows(CHUNK, D), _full((1, D)), _full((1, D)),
                  _full((GROUPS, CHUNK, CHUNK)), _full((CHUNK, GROUPS))] + [ANY] * n,
        out_specs=[_rows(CHUNK, D), _rows(CHUNK, D), _full((GROUPS, CHUNK, CHUNK)), _full((GROUPS, 8, CHUNK)),
                   _full((1, D)), _full((1, D))] + [ANY] * n,
        out_shape=[out, out, jax.ShapeDtypeStruct((GROUPS, CHUNK, CHUNK), F32),
                   jax.ShapeDtypeStruct((GROUPS, 8, CHUNK), F32),
                   jax.ShapeDtypeStruct((1, D), F32), jax.ShapeDtypeStruct((1, D), F32)] + _swap_shapes(swapping),
        scratch_shapes=[pltpu.VMEM((CHUNK, D), F32)] + (_swap_sems(n) if n else []),
        compiler_params=_params("arbitrary", communicates=bool(n)),
    )(z, z, dya, ln_g, ln_b, w_s, bs_t, *swapping)
    return (*res[:6], res[6:])


def attn_bwd(z, yb, dyb, lse, logc, ka, kb, scattering, gathering=None):
    s = z.shape[0]
    nq = s // ATT_T
    t = ATT_T
    grp = ATT_BWD_GROUP
    ngrp = HEADS // 2 // grp
    wide = 128 * grp
    qcol, kcol, vcol = 2 * D // wide, 3 * D // wide, 4 * D // wide
    scale = 1.0 / math.sqrt(HEAD_DIM)
    n = len(scattering)
    g8 = 0 if gathering is None else 1

    def body(*refs):
        q_ref, k_ref, v_ref, y_ref, dy_ref, lse_ref, lc_ref, ka_ref, kb_ref = refs[:9]
        dq_ref, dk_ref, dv_ref = refs[9 + n + g8:12 + n + g8]
        qa_s, qt_s, da_s, dt_s, dq_s, dkt_s, dvt_s = refs[12 + 2 * n + 2 * g8:19 + 2 * n + 2 * g8]
        sems = refs[19 + 2 * n + 2 * g8:]
        gi, j = pl.program_id(0), pl.program_id(1)
        first, lane, ones = _head_masks()
        if n:
            send, finish = _scatter_phases(refs[9:9 + n], refs[12 + n + g8:12 + 2 * n + g8], *sems[:2])
            pl.when((gi == 0) & (j == 0))(send)
        if g8:
            send8, pass_on8, finish8 = _allgather8_phases(refs[12 + 2 * n + g8], *sems[2 * (n > 0):])
            pl.when((gi == 0) & (j == 0))(send8)
            pl.when((gi == ngrp - 1) & (j == nq - 1))(pass_on8)

        @pl.when(j == 0)
        def _():
            dq_s[...] = jnp.zeros_like(dq_s)
            for pr in range(grp):
                cols = slice(pr * 128, (pr + 1) * 128)
                for ib in range(nq):
                    rows = slice(ib * t, (ib + 1) * t)
                    q = q_ref[rows, cols].astype(F32) * scale
                    lse = lse_ref[rows, cols]
                    qa_s[pr, 0, ib] = jnp.where(first, q, _place3(lane, HEAD_DIM + 2 * AUG, _split3(-lse[:, 0:1]),
                                                                  ones(0, 2 * AUG))).astype(BF16)
                    qa_s[pr, 1, ib] = jnp.where(
                        first, _place3(lane, 2 * AUG, _split3(-lse[:, HEAD_DIM:HEAD_DIM + 1]), ones(1, 2 * AUG)),
                        q).astype(BF16)
                    qt_s[pr, ib, :, 0:t] = jnp.where(first, q, 0.0).T.astype(BF16)
                    qt_s[pr, ib, :, t:2 * t] = jnp.where(first, 0.0, q).T.astype(BF16)
                    do = dy_ref[rows, cols].astype(F32)
                    prod = do * y_ref[rows, cols].astype(F32)
                    dd0 = jnp.sum(jnp.where(first, prod, 0.0), axis=-1, keepdims=True)
                    dd1 = jnp.sum(jnp.where(first, 0.0, prod), axis=-1, keepdims=True)
                    da_s[pr, 0, ib] = jnp.where(first, do, _place3(lane, HEAD_DIM, _split3(-dd0), 0.0)).astype(BF16)
                    da_s[pr, 1, ib] = jnp.where(first, _place3(lane, 0, _split3(-dd1), 0.0), do).astype(BF16)
                    dt_s[pr, ib, :, 0:t] = jnp.where(first, do, 0.0).T.astype(BF16)
                    dt_s[pr, ib, :, t:2 * t] = jnp.where(first, 0.0, do).T.astype(BF16)

        keys = []
        for pr in range(grp):
            kj = k_ref[:, pr * 128:(pr + 1) * 128].astype(F32)
            vj = v_ref[:, pr * 128:(pr + 1) * 128].astype(F32)
            keys.append((
                jnp.where(first, kj, ka_ref[pr, 0] + kb_ref[pr, 0, pl.ds(j, 1), :]).astype(BF16),
                jnp.where(first, ka_ref[pr, 1] + kb_ref[pr, 1, pl.ds(j, 1), :], kj).astype(BF16),
                jnp.concatenate([jnp.where(first, kj, 0.0), jnp.where(first, 0.0, kj)], axis=0).astype(BF16),
                jnp.where(first, vj, ones(0, AUG)).astype(BF16),
                jnp.where(first, ones(1, AUG), vj).astype(BF16)))
        dkt_s[...] = jnp.zeros_like(dkt_s)
        dvt_s[...] = jnp.zeros_like(dvt_s)

        def step(i, _):
            lc = lc_ref[i - j]
            rows = pl.ds(pl.multiple_of(i * t, t), t)
            for pr in range(grp):
                k0a, k1a, kst, v0a, v1a = keys[pr]
                p0 = jnp.exp(_dot(qa_s[pr, 0, i], k0a, NT) + lc)
                p1 = jnp.exp(_dot(qa_s[pr, 1, i], k1a, NT) + lc)
                e0 = (p0 * _dot(da_s[pr, 0, i], v0a, NT)).astype(BF16)
                e1 = (p1 * _dot(da_s[pr, 1, i], v1a, NT)).astype(BF16)
                dq_s[pr, rows, :] += _dot(jnp.concatenate([e0, e1], axis=1), kst)
                dvt_s[pr] += _dot(dt_s[pr, i], jnp.concatenate([p0.astype(BF16), p1.astype(BF16)], axis=0))
                dkt_s[pr] += _dot(qt_s[pr, i], jnp.concatenate([e0, e1], axis=0))
            return 0

        lax.fori_loop(j, nq, step, 0)
        for pr in range(grp):
            dk_ref[:, pr * 128:(pr + 1) * 128] = dkt_s[pr].T.astype(BF16)
            dv_ref[:, pr * 128:(pr + 1) * 128] = dvt_s[pr].T.astype(BF16)

        @pl.when(j == nq - 1)
        def _():
            for pr in range(grp):
                dq_ref[:, pr * 128:(pr + 1) * 128] = (dq_s[pr] * scale).astype(BF16)

        if n:
            pl.when((gi == ngrp - 1) & (j == nq - 1))(finish)
        if g8:
            pl.when((gi == ngrp - 1) & (j == nq - 1))(finish8)

    colblock = lambda c: pl.BlockSpec((s, wide), lambda g, j: (0, c + g))
    once = lambda c: pl.BlockSpec((s, wide), lambda g, j: (0, c + g), pipeline_mode=pl.Buffered(1))
    blk = lambda c: pl.BlockSpec((t, wide), lambda g, j: (j, c + g))
    out = jax.ShapeDtypeStruct((s, D), BF16)
    res = pl.pallas_call(
        body, name="attn_bwd", grid=(ngrp, nq),
        in_specs=[once(qcol), blk(kcol), blk(vcol), once(0), once(0), once(0),
                  pl.BlockSpec((nq, t, t), lambda g, j: (0, 0, 0), pipeline_mode=pl.Buffered(1)),
                  pl.BlockSpec((grp, 2, t, 128), lambda g, j: (g, 0, 0, 0)),
                  pl.BlockSpec((grp, 2, nq, 128), lambda g, j: (g, 0, 0, 0))] + [ANY] * (n + g8),
        out_specs=[colblock(0), blk(0), blk(0)] + [ANY] * (n + g8),
        out_shape=[out] * 3 + _scatter_shapes(scattering)
        + ([jax.ShapeDtypeStruct(gathering.shape, gathering.dtype)] if g8 else []),
        input_output_aliases={9 + n: 3 + n} if g8 else {},
        scratch_shapes=[pltpu.VMEM((grp, 2, nq, t, 128), BF16), pltpu.VMEM((grp, nq, 128, 2 * t), BF16),
                        pltpu.VMEM((grp, 2, nq, t, 128), BF16), pltpu.VMEM((grp, nq, 128, 2 * t), BF16),
                        pltpu.VMEM((grp, s, 128), F32), pltpu.VMEM((grp, 128, t), F32),
                        pltpu.VMEM((grp, 128, t), F32)]
        + (_scatter_sems(n) if n else [])
        + ([pltpu.SemaphoreType.DMA((7,)), pltpu.SemaphoreType.DMA((7,))] if g8 else []),
        compiler_params=_params("arbitrary", "arbitrary", communicates=bool(n + g8)),
    )(z, z, z, yb, dyb, lse, logc, ka, kb, *scattering, *([gathering] if g8 else []))
    return res[0], res[1], res[2], res[3:3 + n], (res[3 + n] if g8 else None)


def in_bwd_norm(dz, wg, x, dx1, g_pre, scattering, gathering=None):
    s = x.shape[0]
    tm = 512
    n = len(scattering)
    g = 0 if gathering is None else 1
    last = (s // tm - 1, N_CHIPS - 1)

    def body(*refs):
        dz_ref, w_ref, x_ref, dx1_ref, g_ref = refs[:5]
        dx_ref, dg_ref = refs[5 + n + g:7 + n + g]
        acc_ref = refs[7 + 2 * n + 2 * g]
        sems = refs[8 + 2 * n + 2 * g:]
        i, k = pl.program_id(0), pl.program_id(1)
        if n:
            send, finish = _scatter_phases(refs[5:5 + n], refs[7 + n + g:7 + 2 * n + g], *sems[:2])
            pl.when((i == 0) & (k == 0))(send)
        if g:
            send8, pass_on8, finish8 = _allgather8_phases(refs[7 + 2 * n + g], *sems[2 * (n > 0):])
            pl.when((i == 0) & (k == 0))(send8)
            pl.when((i == last[0]) & (k == last[1]))(pass_on8)

        @pl.when((i == 0) & (k == 0))
        def _():
            dg_ref[...] = jnp.zeros_like(dg_ref)

        part = _dot(dz_ref[...], w_ref[...], NT)

        @pl.when(k == 0)
        def _():
            acc_ref[...] = part

        @pl.when(k > 0)
        def _():
            acc_ref[...] += part

        @pl.when(k == N_CHIPS - 1)
        def _():
            dh = acc_ref[...]
            xhat, r = _rms(x_ref[...])
            dg_ref[...] += jnp.sum(dh * xhat, axis=0, keepdims=True)
            dx_ref[...] = dx1_ref[...] + _rms_bwd(dh * g_ref[...], xhat, r)

        if n:
            pl.when((i == last[0]) & (k == last[1]))(finish)
        if g:
            pl.when((i == last[0]) & (k == last[1]))(finish8)

    row = pl.BlockSpec((tm, D), lambda i, k: (i, 0))
    vec = pl.BlockSpec((1, D), lambda i, k: (0, 0))
    res = pl.pallas_call(
        body, name="in_bwd_norm", grid=(s // tm, N_CHIPS),
        in_specs=[pl.BlockSpec((tm, IN_SHARD), lambda i, k: (i, k)),
                  pl.BlockSpec((None, D, IN_SHARD), lambda i, k: (k, 0, 0)), row, row, vec] + [ANY] * (n + g),
        out_specs=[row, vec] + [ANY] * (n + g),
        out_shape=[jax.ShapeDtypeStruct((s, D), F32), jax.ShapeDtypeStruct((1, D), F32)]
        + _scatter_shapes(scattering) + ([jax.ShapeDtypeStruct(gathering.shape, gathering.dtype)] if g else []),
        input_output_aliases={5 + n: 2 + n} if g else {},
        scratch_shapes=[pltpu.VMEM((tm, D), F32)] + (_scatter_sems(n) if n else [])
        + ([pltpu.SemaphoreType.DMA((7,)), pltpu.SemaphoreType.DMA((7,))] if g else []),
        compiler_params=_params("arbitrary", "arbitrary", communicates=bool(n + g)),
    )(dz, wg, x, dx1, g_pre, *scattering, *([gathering] if g else []))
    return res[0], res[1], res[2:2 + n], (res[2 + n] if g else None)


def _adamw_math(w, g, m, v):
    m = ADAM_B1 * m + (1.0 - ADAM_B1) * g
    v = ADAM_B2 * v + (1.0 - ADAM_B2) * (g * g)
    m_hat = m / (1.0 - ADAM_B1 ** ADAM_STEP)
    v_hat = v / (1.0 - ADAM_B2 ** ADAM_STEP)
    delta = -ADAM_LR * (m_hat / (jnp.sqrt(v_hat) + ADAM_EPS) + ADAM_WD * w)
    return delta, m, v


def adamw(name, w, g, m, v, tr):
    r, c = w.shape

    def body(w_ref, g_ref, m_ref, v_ref, go_ref, d_ref, nm_ref, nv_ref):
        g = g_ref[...]
        go_ref[...] = g
        d_ref[...], nm_ref[...], nv_ref[...] = _adamw_math(w_ref[...], g, m_ref[...], v_ref[...])

    out = jax.ShapeDtypeStruct((r, c), F32)
    return pl.pallas_call(
        body, name=name, grid=(r // tr,), in_specs=[_rows(tr, c)] * 4, out_specs=[_rows(tr, c)] * 4,
        out_shape=[out] * 4, compiler_params=_params("parallel"),
    )(w, g, m, v)


def _allgather8_phases(buf, send_sems, recv_sems):
    x, y, c, chips = _place()
    me = 2 * x + y
    sibling = (x, y, 1 - c)
    rows = buf.shape[1] // 2

    def part(chip, core):
        return buf.at[chip, pl.ds(core * rows, rows)]

    def copy(k, block, to):
        return pltpu.make_async_remote_copy(src_ref=block, dst_ref=block, send_sem=send_sems.at[k],
                                            recv_sem=recv_sems.at[k], device_id=to, device_id_type=MESH)

    def chip_of(j):
        return 2 * chips[j][0] + chips[j][1]

    def send():
        copy(0, part(me, c), sibling).start()
        for j in range(3):
            copy(1 + j, part(me, c), (chips[j][0], chips[j][1], c)).start()

    def pass_on():
        for j in range(3):
            copy(1 + j, part(chip_of(j), c), (chips[j][0], chips[j][1], c)).wait_recv()
            copy(4 + j, part(chip_of(j), c), sibling).start()

    def finish():
        copy(0, part(me, 1 - c), sibling).wait_recv()
        for j in range(3):
            copy(4 + j, part(chip_of(j), 1 - c), sibling).wait_recv()
        copy(0, part(me, c), sibling).wait_send()
        for j in range(3):
            copy(1 + j, part(me, c), (chips[j][0], chips[j][1], c)).wait_send()
            copy(4 + j, part(chip_of(j), c), sibling).wait_send()

    return send, pass_on, finish


def add_halves(name, g, recv, c_idx, tr):
    n, h, c = recv.shape

    def body(c_ref, g_ref, r_ref, o_ref):
        o_ref[...] = (g_ref[...] + r_ref[...]).astype(BF16)

    nb = h // tr
    return pl.pallas_call(
        body, name=name,
        grid_spec=pltpu.PrefetchScalarGridSpec(
            num_scalar_prefetch=1, grid=(n, nb),
            in_specs=[pl.BlockSpec((None, tr, c), lambda k, i, c_ref: (k, c_ref[0] * nb + i, 0)),
                      pl.BlockSpec((None, tr, c), lambda k, i, c_ref: (k, i, 0))],
            out_specs=pl.BlockSpec((None, tr, c), lambda k, i, c_ref: (k, i, 0))),
        out_shape=jax.ShapeDtypeStruct((n, h, c), BF16), compiler_params=_params("parallel", "parallel"),
    )(c_idx, g, recv)


def sum_chips(name, parts, recv, where, tr):
    n, h, c = recv.shape
    nb = h // tr

    def body(w_ref, p_ref, r_ref, o_ref):
        acc = p_ref[...].astype(F32)
        for k in range(n):
            acc = acc + r_ref[k].astype(F32)
        o_ref[...] = acc

    return pl.pallas_call(
        body, name=name,
        grid_spec=pltpu.PrefetchScalarGridSpec(
            num_scalar_prefetch=1, grid=(nb,),
            in_specs=[pl.BlockSpec((None, tr, c), lambda i, w_ref: (w_ref[0], i, 0)),
                      pl.BlockSpec((n, tr, c), lambda i, w_ref: (0, i, 0))],
            out_specs=pl.BlockSpec((tr, c), lambda i, w_ref: (w_ref[1] * nb + i, 0))),
        out_shape=jax.ShapeDtypeStruct((2 * h, c), F32), compiler_params=_params("parallel"),
    )(where, parts, recv)


def place_shard(name, shard, where, dtype, tr):
    r, c = shard.shape

    def body(w_ref, s_ref, o_ref):
        o_ref[...] = s_ref[...].astype(dtype)

    return pl.pallas_call(
        body, name=name,
        grid_spec=pltpu.PrefetchScalarGridSpec(
            num_scalar_prefetch=1, grid=(r // tr,),
            in_specs=[pl.BlockSpec((tr, c), lambda i, w_ref: (i, 0))],
            out_specs=pl.BlockSpec((None, tr, c), lambda i, w_ref: (w_ref[0], i, 0))),
        out_shape=jax.ShapeDtypeStruct((N_CHIPS, r, c), dtype), compiler_params=_params("parallel"),
    )(where, shard)


ANY = pl.BlockSpec(memory_space=pl.ANY)


def _place():
    x, y, c = lax.axis_index("x"), lax.axis_index("y"), lax.axis_index("c")
    chips = [(1 - x, y), (x, 1 - y), (1 - x, 1 - y)]
    return x, y, c, chips


def gather_shards(arrays):
    n = len(arrays)

    def body(*refs):
        send, pass_on, finish = _gather_phases(refs[n:2 * n], *refs[2 * n:], _spans(arrays))
        send()
        pass_on()
        finish()

    return pl.pallas_call(
        body, name="gather_shards", in_specs=[ANY] * n, out_specs=[ANY] * n,
        out_shape=[jax.ShapeDtypeStruct(a.shape, a.dtype) for a in _arrays(arrays)],
        input_output_aliases={w: w for w in range(n)}, scratch_shapes=_gather_sems(n),
        compiler_params=pltpu.CompilerParams(has_side_effects=True),
    )(*_arrays(arrays))


def _gather_sems(n):
    return [pltpu.SemaphoreType.DMA((6 * n,)), pltpu.SemaphoreType.DMA((6 * n,))]


class Span(typing.NamedTuple):
    array: jax.Array
    lo: int
    hi: int
    ways: tuple = (0, 1, 2)


def _arrays(gathering):
    return [g.array if isinstance(g, Span) else g for g in gathering]


def _spans(gathering):
    return [(g.lo, g.hi, g.ways) if isinstance(g, Span) else (0, g.shape[1], (0, 1, 2)) for g in gathering]


def _gather_phases(out, send_sems, recv_sems, spans):
    n = len(out)
    if not any(ways for _, _, ways in spans):
        return (lambda: None,) * 3
    x, y, c, chips = _place()
    me = 2 * x + y
    sibling = (x, y, 1 - c)

    def half(w, chip, core):
        lo, hi, _ = spans[w]
        h = (hi - lo) // 2
        return out[w].at[chip, pl.ds(lo + core * h, h)]

    def copy(k, block, to):
        return pltpu.make_async_remote_copy(src_ref=block, dst_ref=block, send_sem=send_sems.at[k],
                                            recv_sem=recv_sems.at[k], device_id=to, device_id_type=MESH)

    def over_ici(w, j, chip):
        return copy(3 * w + j, half(w, chip, c), (chips[j][0], chips[j][1], c))

    def over_d2d(w, j, core):
        return copy(3 * n + 3 * w + j, half(w, 2 * chips[j][0] + chips[j][1], core), sibling)

    pairs = [(w, j) for w in range(n) for j in spans[w][2]]

    def send():
        for w, j in pairs:
            over_ici(w, j, me).start()

    def pass_on():
        for w, j in pairs:
            over_ici(w, j, 2 * chips[j][0] + chips[j][1]).wait_recv()
            over_d2d(w, j, c).start()

    def finish():
        for w, j in pairs:
            over_d2d(w, j, 1 - c).wait_recv()
        for w, j in pairs:
            over_ici(w, j, me).wait_send()
            over_d2d(w, j, c).wait_send()

    return send, pass_on, finish


def _relay_sems():
    return [pltpu.SemaphoreType.DMA((4,)), pltpu.SemaphoreType.DMA((4,))]


def _relay_phases(out, send_sems, recv_sems):
    x, y, c, chips = _place()
    sibling = (x, y, 1 - c)
    rows = out.shape[1]
    quarter = rows // 4
    far = 2 * chips[2][0] + chips[2][1]

    def piece(chip, way, core):
        return out.at[chip, pl.ds(way * (rows // 2) + core * quarter, quarter)]

    def copy(k, block, to):
        return pltpu.make_async_remote_copy(src_ref=block, dst_ref=block, send_sem=send_sems.at[k],
                                            recv_sem=recv_sems.at[k], device_id=to, device_id_type=MESH)

    def over_ici(way, chip):
        return copy(way, piece(chip, way, c), (chips[way][0], chips[way][1], c))

    def over_d2d(way, core):
        return copy(2 + way, piece(far, way, core), sibling)

    def send():
        for way in range(2):
            other = chips[1 - way]
            over_ici(way, 2 * other[0] + other[1]).start()

    def pass_on():
        for way in range(2):
            over_ici(way, far).wait_recv()
            over_d2d(way, c).start()

    def finish():
        for way in range(2):
            over_d2d(way, 1 - c).wait_recv()
        for way in range(2):
            other = chips[1 - way]
            over_ici(way, 2 * other[0] + other[1]).wait_send()
            over_d2d(way, c).wait_send()

    return send, pass_on, finish


def swap_halves(name, grads):
    n = len(grads)

    def body(*refs):
        send, finish = _swap_phases(refs[:n], refs[n:2 * n], *refs[2 * n:])
        send()
        finish()

    return pl.pallas_call(
        body, name=name, in_specs=[ANY] * n, out_specs=[ANY] * n, out_shape=_swap_shapes(grads),
        scratch_shapes=_swap_sems(n), compiler_params=pltpu.CompilerParams(has_side_effects=True),
    )(*grads)


def _swap_shapes(grads):
    return [jax.ShapeDtypeStruct((a.shape[0], a.shape[1] // 2, a.shape[2]), a.dtype) for a in grads]


def _swap_sems(n):
    return [pltpu.SemaphoreType.DMA((n,)), pltpu.SemaphoreType.DMA((n,))]


def _swap_phases(g, out, send_sems, recv_sems):
    x, y, c, _ = _place()

    def copies():
        return [pltpu.make_async_remote_copy(
            src_ref=g[w].at[:, pl.ds((1 - c) * (g[w].shape[1] // 2), g[w].shape[1] // 2)], dst_ref=out[w],
            send_sem=send_sems.at[w], recv_sem=recv_sems.at[w], device_id=(x, y, 1 - c), device_id_type=MESH)
            for w in range(len(g))]

    def send():
        for cp in copies():
            cp.start()

    def finish():
        for cp in copies():
            cp.wait()

    return send, finish


def _send_phases(g, out, send_sems, recv_sems):
    x, y, c, _ = _place()

    def copies():
        return [pltpu.make_async_remote_copy(
            src_ref=g[w], dst_ref=out[w], send_sem=send_sems.at[w], recv_sem=recv_sems.at[w],
            device_id=(x, y, 1 - c), device_id_type=MESH) for w in range(len(g))]

    def send():
        for cp in copies():
            cp.start()

    def finish():
        for cp in copies():
            cp.wait()

    return send, finish


def dw_in_half(name, h, dz, which, sending):
    s = h.shape[0]
    hh, tb = D // 2, IN_SHARD // 2
    n = len(sending)
    steps = IN_COLS // tb

    def body(w_ref, *refs):
        a_ref, b_ref, o_ref = refs[0], refs[1], refs[2 + n]
        j = pl.program_id(0)
        if n:
            send, finish = _send_phases(refs[2:2 + n], refs[3 + n:3 + 2 * n], *refs[3 + 2 * n:])
            pl.when(j == 0)(send)
        o_ref[...] = _dot(a_ref[...], b_ref[...], TN)
        if n:
            pl.when(j == steps - 1)(finish)

    out = pl.pallas_call(
        body, name=name,
        grid_spec=pltpu.PrefetchScalarGridSpec(
            num_scalar_prefetch=1, grid=(steps,),
            in_specs=[pl.BlockSpec((s, hh), lambda j, w: (0, w[0])), pl.BlockSpec((s, tb), lambda j, w: (0, j))]
            + [ANY] * n,
            out_specs=[pl.BlockSpec((None, hh, tb), lambda j, w: (j // 2, 0, j % 2))] + [ANY] * n,
            scratch_shapes=_swap_sems(n) if n else []),
        out_shape=[jax.ShapeDtypeStruct((N_CHIPS, hh, IN_SHARD), F32)]
        + [jax.ShapeDtypeStruct(a.shape, a.dtype) for a in sending],
        compiler_params=_params("arbitrary", communicates=bool(n)),
    )(which, h, dz, *sending)
    return out[0], out[1:]


def scatter_chips(parts):
    n = len(parts)

    def body(*refs):
        send, finish = _scatter_phases(refs[:n], refs[n:2 * n], *refs[2 * n:])
        send()
        finish()

    return pl.pallas_call(
        body, name="scatter_chips", in_specs=[ANY] * n, out_specs=[ANY] * n,
        out_shape=_scatter_shapes(parts), scratch_shapes=_scatter_sems(n),
        compiler_params=pltpu.CompilerParams(has_side_effects=True),
    )(*parts)


def _scatter_shapes(parts):
    return [jax.ShapeDtypeStruct((3,) + a.shape[1:], a.dtype) for a in parts]


def _scatter_sems(n):
    return [pltpu.SemaphoreType.DMA((3 * n,)), pltpu.SemaphoreType.DMA((3 * n,))]


def _scatter_phases(p, out, send_sems, recv_sems):
    x, y, c, chips = _place()

    def copies():
        return [pltpu.make_async_remote_copy(
            src_ref=p[w].at[2 * px + py], dst_ref=out[w].at[j], send_sem=send_sems.at[3 * w + j],
            recv_sem=recv_sems.at[3 * w + j], device_id=(px, py, c), device_id_type=MESH)
            for w in range(len(p)) for j, (px, py) in enumerate(chips)]

    def send():
        for cp in copies():
            cp.start()

    def finish():
        for cp in copies():
            cp.wait()

    return send, finish


def _join_only(arrays):
    n = len(arrays)

    def body(*refs):
        out = refs[n:2 * n]
        send_sems, recv_sems = refs[2 * n:]
        x, y, c, _ = _place()

        def copy(w, core):
            h = out[w].shape[0] // 2
            rows = out[w].at[pl.ds(core * h, h)]
            return pltpu.make_async_remote_copy(
                src_ref=rows, dst_ref=rows, send_sem=send_sems.at[w], recv_sem=recv_sems.at[w],
                device_id=(x, y, 1 - c), device_id_type=MESH)

        for w in range(n):
            copy(w, c).start()
        for w in range(n):
            copy(w, 1 - c).wait_recv()
        for w in range(n):
            copy(w, c).wait_send()

    return pl.pallas_call(
        body, name="join_last", in_specs=[ANY] * n, out_specs=[ANY] * n,
        out_shape=[jax.ShapeDtypeStruct(a.shape, a.dtype) for a in arrays],
        input_output_aliases={w: w for w in range(n)},
        scratch_shapes=[pltpu.SemaphoreType.DMA((n,)), pltpu.SemaphoreType.DMA((n,))],
        compiler_params=pltpu.CompilerParams(has_side_effects=True),
    )(*arrays)


HBM = pl.BlockSpec(memory_space=pltpu.HBM)
SEM = pl.BlockSpec(memory_space=pltpu.SEMAPHORE)
DATAFLOW = pltpu.SideEffectType.DATAFLOW_SIDE_EFFECTING


def _scatter_copies(p_ref, land_ref, send_sems, recv_sems):
    x, y, c, chips = _place()
    return [pltpu.make_async_remote_copy(
        src_ref=p_ref.at[2 * px + py], dst_ref=land_ref.at[j], send_sem=send_sems[j], recv_sem=recv_sems[j],
        device_id=(px, py, c), device_id_type=MESH) for j, (px, py) in enumerate(chips)]


def scatter_start(p):
    land = jax.ShapeDtypeStruct((3,) + p.shape[1:], p.dtype)

    def body(p_ref, land_ref, *outs):
        for cp in _scatter_copies(p_ref, land_ref, outs[0:3], outs[3:6]):
            cp.start()
        outs[8][...] = jnp.zeros_like(outs[8])

    return pl.pallas_call(
        body, name="scatter_start",
        out_shape=(pltpu.SemaphoreType.DMA(()),) * 6
        + (pltpu.HBM(p.shape, p.dtype), pltpu.HBM(land.shape, land.dtype), jax.ShapeDtypeStruct((8, 128), F32)),
        in_specs=(HBM, HBM), out_specs=(SEM,) * 6 + (HBM, HBM, pl.BlockSpec(memory_space=pltpu.VMEM)),
        input_output_aliases={0: 6, 1: 7},
        compiler_params=pltpu.CompilerParams(has_side_effects=DATAFLOW),
    )(pltpu.with_memory_space_constraint(p, pltpu.HBM),
      pltpu.with_memory_space_constraint(lax.empty(land.shape, land.dtype), pltpu.HBM))


def scatter_wait(started, after):
    sems, p_thru, land_thru = started[0:6], started[6], started[7]

    def body(p_ref, land_ref, *refs):
        for cp in _scatter_copies(p_ref, land_ref, refs[0:3], refs[3:6]):
            cp.wait_send()
            cp.wait_recv()

    return pl.pallas_call(
        body, name="scatter_wait",
        out_shape=(pltpu.HBM(p_thru.shape, p_thru.dtype), pltpu.HBM(land_thru.shape, land_thru.dtype)),
        in_specs=(HBM, HBM) + (SEM,) * 6 + (pl.BlockSpec(memory_space=pl.ANY),), out_specs=(HBM, HBM),
        input_output_aliases={0: 0, 1: 1},
        compiler_params=pltpu.CompilerParams(has_side_effects=DATAFLOW),
    )(p_thru, land_thru, *sems, after)


def join_halves(arrays, gathering=None):
    n = len(arrays)
    if gathering is None:
        return _join_only(arrays), None

    def body(*refs):
        out = refs[n + 1:2 * n + 1]
        send_sems, recv_sems = refs[2 * n + 2:2 * n + 4]
        send8, pass_on8, finish8 = _allgather8_phases(refs[2 * n + 1], *refs[2 * n + 4:])
        x, y, c, _ = _place()

        def copy(w, core):
            h = out[w].shape[0] // 2
            rows = out[w].at[pl.ds(core * h, h)]
            return pltpu.make_async_remote_copy(
                src_ref=rows, dst_ref=rows, send_sem=send_sems.at[w], recv_sem=recv_sems.at[w],
                device_id=(x, y, 1 - c), device_id_type=MESH)

        send8()
        for w in range(n):
            copy(w, c).start()
        pass_on8()
        for w in range(n):
            copy(w, 1 - c).wait_recv()
        finish8()
        for w in range(n):
            copy(w, c).wait_send()

    res = pl.pallas_call(
        body, name="join_halves", in_specs=[ANY] * (n + 1), out_specs=[ANY] * (n + 1),
        out_shape=[jax.ShapeDtypeStruct(a.shape, a.dtype) for a in list(arrays) + [gathering]],
        input_output_aliases={w: w for w in range(n + 1)},
        scratch_shapes=[pltpu.SemaphoreType.DMA((n,)), pltpu.SemaphoreType.DMA((n,)),
                        pltpu.SemaphoreType.DMA((7,)), pltpu.SemaphoreType.DMA((7,))],
        compiler_params=pltpu.CompilerParams(has_side_effects=True),
    )(*arrays, gathering)
    return res[:n], res[n]


def allreduce_small(packed):
    r, c = packed.shape
    n_dev = 8

    def body(x_ref, all_ref, sum_ref, send_sems, recv_sems, local_sem):
        x, y, cc, chips = _place()
        me, sibling = (x, y, cc), (x, y, 1 - cc)

        def rows(px, py, pc):
            return all_ref.at[4 * px + 2 * py + pc]

        def copy(k, block, to, src=None):
            return pltpu.make_async_remote_copy(
                src_ref=rows(*block) if src is None else src, dst_ref=rows(*block), send_sem=send_sems.at[k],
                recv_sem=recv_sems.at[k], device_id=to, device_id_type=MESH)

        mine = pltpu.make_async_copy(x_ref, rows(*me), local_sem)
        mine.start()
        first = [copy(0, me, sibling, src=x_ref)]
        first += [copy(1 + j, me, (*chip, cc), src=x_ref) for j, chip in enumerate(chips)]
        for cp in first:
            cp.start()
        passed = [copy(4 + j, (*chip, cc), sibling) for j, chip in enumerate(chips)]
        for j, chip in enumerate(chips):
            copy(1 + j, (*chip, cc), me).wait_recv()
            passed[j].start()
        copy(0, sibling, me).wait_recv()
        for j, chip in enumerate(chips):
            copy(4 + j, (*chip, 1 - cc), me).wait_recv()
        for cp in first + passed:
            cp.wait_send()
        mine.wait()
        acc = all_ref[0]
        for k in range(1, n_dev):
            acc = acc + all_ref[k]
        sum_ref[...] = acc

    vm = pl.BlockSpec(memory_space=pltpu.VMEM)
    return pl.pallas_call(
        body, name="allreduce_small", in_specs=[vm], out_specs=[vm, vm],
        out_shape=[jax.ShapeDtypeStruct((n_dev, r, c), F32), jax.ShapeDtypeStruct((r, c), F32)],
        scratch_shapes=[pltpu.SemaphoreType.DMA((7,)), pltpu.SemaphoreType.DMA((7,)), pltpu.SemaphoreType.DMA],
        compiler_params=pltpu.CompilerParams(has_side_effects=True, vmem_limit_bytes=VMEM_LIMIT),
    )(packed)[1]


def local_step(x, target, vecs, w_s, bs_t, bg, wg_in, late, core=None, order=None, where=None):
    on_mesh = core is not None

    def add(names, grads, recv):
        return [add_halves("add_" + n, g, r, core, min(r.shape[1], 256)) for n, g, r in zip(names, grads, recv)]

    g_pre, ln_g, ln_b, g_post, g_fpre, g_fpost = vecs
    s = x.shape[0]
    if order is None:
        order = jnp.arange(N_CHIPS, dtype=jnp.int32)
    logc = _attn_tables(s)
    ka, kb = _alibi_tables(s)

    h = norm_pre(x, g_pre)
    if not on_mesh:
        wg_a, wg_b, wg_out, wg_ff1, wg_ff2 = late
    if on_mesh:
        cut = D // 4
        z, wg_in, (wg_a, wg_b, wg_out, wg_ff1, wg_ff2) = mm_in(h, wg_in, order, True, late)
        ya, (wg_b, wg_ff2) = gating_fwd(z, ln_g, ln_b, w_s, bs_t, [wg_b, Span(wg_ff2, 0, cut)])
        yb, lse, (wg_a, wg_ff1, wg_ff2, bg) = attn_fwd(
            z, logc, ka, kb, [wg_a, wg_ff1, Span(wg_ff2, cut, 3 * cut), bg])
        bg = jnp.transpose(bg[:, :2, :], (1, 0, 2)).reshape(2, D)
    else:
        z, _, _ = mm_in(h, wg_in, order, False)
        ya, _ = gating_fwd(z, ln_g, ln_b, w_s, bs_t, [])
        yb, lse, _ = attn_fwd(z, logc, ka, kb, [])
    merged, pa, pb, got = proj_merge(ya, yb, wg_a.reshape(D, D), wg_b.reshape(D, D), z, bg, [wg_out] if on_mesh else [])
    wg_out = got[0] if on_mesh else wg_out
    w_out = wg_out.reshape(D, D)
    o, x1, h2, got = out_norm(merged, w_out, x, g_post, g_fpre, [Span(wg_ff2, 3 * D // 4, D)] if on_mesh else [])
    a, rl, _ = mm_ff1(h2, wg_ff1, [])
    w_ff2 = (got[0] if on_mesh else wg_ff2).reshape(D_FF, D)
    dy, df, d_gfpost, loss = ff2_loss(rl, w_ff2, x1, target, g_fpost)

    half_cols = pl.BlockSpec((D, D // 2), lambda i, j: (0, j))
    d_wff2 = mm_tn("dw_ff2", rl, df, D // 2, D, (D_FF, D), pl.BlockSpec((D // 2, D), lambda i, j: (i, 0)))
    da = ff2_bwd(df, w_ff2, a)
    d_wff1 = mm_tn("dw_ff1", h2, da, D, D // 2, (N_CHIPS, D, D),
                   pl.BlockSpec((None, D, D // 2), lambda i, j: (j // 2, 0, j % 2)))
    d_ff = [d_wff1, d_wff2.reshape(N_CHIPS, D, D)]
    dx1, do, d_gfpre, d_gpost, recv_ff = ff1_bwd_norms(da, wg_ff1, x1, o, dy, g_fpre, g_post, d_ff if on_mesh else [])
    d_wout = mm_tn("dw_out", merged, do, D, D // 2, (D, D), half_cols)
    dpa, dpb, dga, dgb, d_bg = out_bwd_gates(do, w_out, pa, pb, z, bg)
    d_wa = mm_tn("dw_a", ya, dpa, D, D // 2, (D, D), half_cols)
    d_wb = mm_tn("dw_b", yb, dpb, D, D // 2, (D, D), half_cols)
    dya = mm_nt("dy_a", dpa, wg_a.reshape(D, D))
    dyb = mm_nt("dy_b", dpb, wg_b.reshape(D, D))
    d_proj = [d_wa.reshape(N_CHIPS, D // N_CHIPS, D), d_wb.reshape(N_CHIPS, D // N_CHIPS, D),
              d_wout.reshape(N_CHIPS, D // N_CHIPS, D)]
    du, dv, d_ws, d_bs, d_lng, d_lnb, recv_proj = gating_bwd(z, dya, ln_g, ln_b, w_s, bs_t, d_proj if on_mesh else [])
    early = d_proj + d_ff
    parts_early = add(BIG[1:], early, list(recv_proj) + list(recv_ff)) if on_mesh else []
    small = dict(b_gate=d_bg, ln_v_g=d_lng, ln_v_b=d_lnb, w_s=d_ws, b_s=d_bs[:, 0, :],
                 norm_mix_post=d_gpost, norm_ffn_pre=d_gfpre, norm_ffn_post=d_gfpost)
    packed = pack_small(dict(small, norm_mix_pre=jnp.zeros((1, D), F32)), loss, where) if on_mesh else None
    dq, dk, dvb, got_early, packed = attn_bwd(z, yb, dyb, lse, logc, ka, kb, parts_early, packed)
    dz = jnp.concatenate([du, dv, dq, dk, dvb, dga, dgb], axis=1)
    if on_mesh:
        for_sibling, _ = dw_in_half("dw_in_sibling", h, dz, 1 - core, [])
        mine, from_sibling = dw_in_half("dw_in_mine", h, dz, core, [for_sibling])
        d_win = None
        parts_late = [add_halves("add_w_in", mine, from_sibling[0], jnp.zeros((1,), jnp.int32), 256)]
    else:
        half = IN_SHARD // 2
        d_win = mm_tn("dw_in", h, dz, D, half, (N_CHIPS, D, IN_SHARD),
                      pl.BlockSpec((None, D, half), lambda i, j: (j // 2, 0, j % 2)))
        parts_late = []
    started = scatter_start(parts_late[0]) if on_mesh else None
    dx, d_gpre, _, _ = in_bwd_norm(dz, wg_in, x, dx1, g_pre + started[8][0:1, 0:1] if on_mesh else g_pre, [])
    small["norm_mix_pre"] = d_gpre
    return loss[0, 0], dx, [d_win] + early, small, parts_early, list(got_early), packed, started


BIG = ("w_in", "w_a_proj", "w_b_proj", "w_out", "w_ff1", "w_ff2")
SMALL = ("norm_mix_pre", "ln_v_g", "ln_v_b", "b_s", "norm_mix_post", "norm_ffn_pre", "norm_ffn_post", "w_s", "b_gate")
ORDER = ("norm_mix_pre", "w_in", "b_gate", "ln_v_g", "ln_v_b", "w_s", "b_s", "w_a_proj", "w_b_proj", "w_out",
         "norm_mix_post", "norm_ffn_pre", "w_ff1", "w_ff2", "norm_ffn_post")
VEC_ROWS = D // 128
WS_ROW = 7 * VEC_ROWS
BG_ROW = WS_ROW + GROUPS * CHUNK
LOSS_ROW = BG_ROW + 2 * VEC_ROWS
PACK_ROWS = LOSS_ROW + 8


def pack_small(small, loss, where):
    vectors = [small[n] for n in SMALL[:7]]
    operands = vectors + [small["w_s"], small["b_gate"], loss]

    def body(where_ref, *refs):
        out = refs[-1]
        ws_ref, bg_ref, loss_ref = refs[7:10]
        for i, n in enumerate(SMALL[:7]):
            if n == "b_s":
                out[i * VEC_ROWS:(i + 1) * VEC_ROWS, :] = refs[i][...]
            else:
                for j in range(VEC_ROWS):
                    out[i * VEC_ROWS + j:i * VEC_ROWS + j + 1, :] = refs[i][:, j * 128:(j + 1) * 128]
        for g in range(GROUPS):
            out[WS_ROW + g * CHUNK:WS_ROW + (g + 1) * CHUNK, :] = ws_ref[g]
        for r in range(2):
            for j in range(VEC_ROWS):
                row = BG_ROW + r * VEC_ROWS + j
                out[row:row + 1, :] = bg_ref[r:r + 1, j * 128:(j + 1) * 128]
        lane = lax.broadcasted_iota(jnp.int32, (8, 128), 1)
        sub = lax.broadcasted_iota(jnp.int32, (8, 128), 0)
        out[LOSS_ROW:LOSS_ROW + 8, :] = jnp.where((lane == 0) & (sub == 0), loss_ref[...], 0.0)

    return pl.pallas_call(
        body, name="pack_small",
        grid_spec=pltpu.PrefetchScalarGridSpec(
            num_scalar_prefetch=1, grid=(1,), in_specs=[_full(a.shape) for a in operands],
            out_specs=pl.BlockSpec((None, PACK_ROWS, 128), lambda i, w: (w[0], w[1], 0))),
        out_shape=jax.ShapeDtypeStruct((N_CHIPS, 2 * PACK_ROWS, 128), F32), compiler_params=_params("arbitrary"),
    )(where, *operands)


def pack_vector(vec, where):
    def body(where_ref, v_ref, out):
        for j in range(VEC_ROWS):
            out[j:j + 1, :] = v_ref[:, j * 128:(j + 1) * 128]

    return pl.pallas_call(
        body, name="pack_vector",
        grid_spec=pltpu.PrefetchScalarGridSpec(
            num_scalar_prefetch=1, grid=(1,), in_specs=[_full(vec.shape)],
            out_specs=pl.BlockSpec((None, VEC_ROWS, 128), lambda i, w: (w[0], w[1], 0))),
        out_shape=jax.ShapeDtypeStruct((N_CHIPS, 2 * VEC_ROWS, 128), F32), compiler_params=_params("arbitrary"),
    )(where, vec)


def adamw_small(gathered, first, chip, w, m, v):
    shapes = {n: (1, D) for n in SMALL}
    shapes.update(b_s=(GROUPS, CHUNK), w_s=(GROUPS * CHUNK, CHUNK), b_gate=(2, D // N_CHIPS))
    flat = lambda t: [t[n].reshape(shapes[n]) for n in SMALL]
    per = D // N_CHIPS // 128

    def body(chip_ref, all_ref, first_ref, *refs):
        params, outs = refs[:27], refs[27:]
        sub = lax.broadcasted_iota(jnp.int32, (VEC_ROWS, 128), 0)
        sum_ref = outs[36]
        total = all_ref[0, 0:PACK_ROWS, :]
        head = first_ref[0, 0:VEC_ROWS, :]
        for k in range(1, 2 * N_CHIPS):
            total = total + all_ref[k // 2, (k % 2) * PACK_ROWS:(k % 2 + 1) * PACK_ROWS, :]
            head = head + first_ref[k // 2, (k % 2) * VEC_ROWS:(k % 2 + 1) * VEC_ROWS, :]
        sum_ref[...] = total
        sum_ref[0:VEC_ROWS, :] = head

        def gate_row(r):
            rows = sum_ref[BG_ROW + r * VEC_ROWS:BG_ROW + (r + 1) * VEC_ROWS, :]
            return jnp.concatenate([jnp.sum(jnp.where(sub == per * chip_ref[0] + j, rows, 0.0), axis=0, keepdims=True)
                                    for j in range(per)], axis=1)

        for i, n in enumerate(SMALL):
            if n == "b_s":
                g = sum_ref[i * VEC_ROWS:(i + 1) * VEC_ROWS, :]
            elif n == "w_s":
                g = sum_ref[WS_ROW:BG_ROW, :]
            elif n == "b_gate":
                g = jnp.concatenate([gate_row(0), gate_row(1)], axis=0)
            else:
                g = jnp.concatenate([sum_ref[i * VEC_ROWS + j:i * VEC_ROWS + j + 1, :] for j in range(VEC_ROWS)],
                                    axis=1)
            delta, nm, nv = _adamw_math(params[i][...], g, params[9 + i][...], params[18 + i][...])
            outs[4 * i][...], outs[4 * i + 1][...], outs[4 * i + 2][...], outs[4 * i + 3][...] = g, delta, nm, nv

    vm = pl.BlockSpec(memory_space=pltpu.VMEM)
    res = pl.pallas_call(
        body, name="adamw_small",
        in_specs=[pl.BlockSpec(memory_space=pltpu.SMEM)] + [vm] * 29, out_specs=[vm] * 37,
        out_shape=[jax.ShapeDtypeStruct(shapes[n], F32) for n in SMALL for _ in range(4)]
        + [jax.ShapeDtypeStruct((PACK_ROWS, 128), F32)],
        compiler_params=_params(),
    )(chip, gathered, first, *flat(w), *flat(m), *flat(v))
    new = {n: tuple(r.reshape(w[n].shape) for r in res[4 * i:4 * i + 4]) for i, n in enumerate(SMALL)}
    return new, res[36][LOSS_ROW, 0]


def kernel(x, norm_mix_pre, w_in, b_gate, ln_v_g, ln_v_b, w_s, b_s, w_a_proj, w_b_proj, w_out, norm_mix_post, norm_ffn_pre, w_ff1, w_ff2, norm_ffn_post, loss_target, m_norm_mix_pre, m_w_in, m_b_gate, m_ln_v_g, m_ln_v_b, m_w_s, m_b_s, m_w_a_proj, m_w_b_proj, m_w_out, m_norm_mix_post, m_norm_ffn_pre, m_w_ff1, m_w_ff2, m_norm_ffn_post, v_norm_mix_pre, v_w_in, v_b_gate, v_ln_v_g, v_ln_v_b, v_w_s, v_b_s, v_w_a_proj, v_w_b_proj, v_w_out, v_norm_mix_post, v_norm_ffn_pre, v_w_ff1, v_w_ff2, v_norm_ffn_post):
    w = dict(norm_mix_pre=norm_mix_pre, w_in=w_in, b_gate=b_gate, ln_v_g=ln_v_g, ln_v_b=ln_v_b, w_s=w_s, b_s=b_s,
             w_a_proj=w_a_proj, w_b_proj=w_b_proj, w_out=w_out, norm_mix_post=norm_mix_post,
             norm_ffn_pre=norm_ffn_pre, w_ff1=w_ff1, w_ff2=w_ff2, norm_ffn_post=norm_ffn_post)
    m = dict(norm_mix_pre=m_norm_mix_pre, w_in=m_w_in, b_gate=m_b_gate, ln_v_g=m_ln_v_g, ln_v_b=m_ln_v_b, w_s=m_w_s,
             b_s=m_b_s, w_a_proj=m_w_a_proj, w_b_proj=m_w_b_proj, w_out=m_w_out, norm_mix_post=m_norm_mix_post,
             norm_ffn_pre=m_norm_ffn_pre, w_ff1=m_w_ff1, w_ff2=m_w_ff2, norm_ffn_post=m_norm_ffn_post)
    v = dict(norm_mix_pre=v_norm_mix_pre, w_in=v_w_in, b_gate=v_b_gate, ln_v_g=v_ln_v_g, ln_v_b=v_ln_v_b, w_s=v_w_s,
             b_s=v_b_s, w_a_proj=v_w_a_proj, w_b_proj=v_w_b_proj, w_out=v_w_out, norm_mix_post=v_norm_mix_post,
             norm_ffn_pre=v_norm_ffn_pre, w_ff1=v_w_ff1, w_ff2=v_w_ff2, norm_ffn_post=v_norm_ffn_post)
    chip = 2 * lax.axis_index("x") + lax.axis_index("y")
    core = lax.axis_index("c")

    where = jnp.stack([chip, core]).astype(jnp.int32)
    wg_in = place_shard("place_w_in", w_in[0], where, BF16, 256)
    bg_all = place_shard("place_b_gate", jnp.pad(b_gate[0], ((0, 14), (0, 0))), where, F32, 16)
    vecs = (norm_mix_pre, ln_v_g, ln_v_b, norm_mix_post, norm_ffn_pre, norm_ffn_post)
    loss, dx, _, small, parts, got, packed, started = local_step(
        x[0], loss_target[0], vecs, w_s[0], b_s[0].T, bg_all, wg_in, [w[n][0] for n in BIG[1:]],
        core=jnp.reshape(core, (1,)).astype(jnp.int32),
        order=jnp.stack([chip, chip ^ 2, chip ^ 1, chip ^ 3]).astype(jnp.int32), where=where)

    halves = [sum_chips("sum_" + n, p, r, where, min(p.shape[1], 256)) for n, p, r in zip(BIG[1:], parts, got)]
    joined, first = join_halves(halves, pack_vector(small["norm_mix_pre"], where))
    grads = dict(zip(BIG[1:], joined))
    new = {}

    def update(n):
        shape = w[n].shape
        res = adamw("adamw_" + n, w[n][0], grads[n], m[n][0], v[n][0], min(shape[1], 256))
        new[n] = tuple(r.reshape(shape) for r in res)

    for n in BIG[1:]:
        update(n)
    p_in, got_in = scatter_wait(started, new["w_ff2"][1])
    (grads["w_in"],), _ = join_halves([sum_chips("sum_w_in", p_in, got_in, where, 256)])
    update("w_in")
    small_new, loss = adamw_small(packed, first, jnp.reshape(chip, (1,)).astype(jnp.int32), w, m, v)
    new.update(small_new)

    outs = [loss, dx[None]]
    for i in range(4):
        outs += [new[n][i] for n in ORDER]
    return tuple(outs)
```

```python
import functools
import math
import typing

import numpy as np
import jax
import jax.numpy as jnp
from jax import lax
from jax.experimental import pallas as pl
from jax.experimental.pallas import tpu as pltpu

F32 = jnp.float32
BF16 = jnp.bfloat16
MESH = pl.DeviceIdType.MESH

D = 1024
EPS = 1e-6
CHUNK = 128
GROUPS = 8
HEADS = 16
HEAD_DIM = 64
ATT_T = 256
ATT_GROUP = 8
ATT_BWD_GROUP = 4
N_CHIPS = 4
D_FF = 4 * D
IN_COLS = 7 * D
IN_SHARD = IN_COLS // N_CHIPS
MASKED = -1e30
VMEM_LIMIT = 56 * 2 ** 20

ADAM_LR, ADAM_B1, ADAM_B2, ADAM_EPS, ADAM_WD, ADAM_STEP = 0.001, 0.9, 0.999, 1e-08, 0.01, 10

NN = (((1,), (0,)), ((), ()))
NT = (((1,), (1,)), ((), ()))
TN = (((0,), (0,)), ((), ()))


def _dot(a, b, dims=NN):
    return lax.dot_general(a, b, dims, preferred_element_type=F32)


def _params(*sem, communicates=False):
    return pltpu.CompilerParams(dimension_semantics=sem or None, vmem_limit_bytes=VMEM_LIMIT,
                                has_side_effects=communicates)


def _rows(tr, c, col=0):
    return pl.BlockSpec((tr, c), lambda i: (i, col))


def _full(shape):
    n = len(shape)
    return pl.BlockSpec(shape, lambda *_: (0,) * n)


def _gelu(x):
    k = math.sqrt(2.0 / math.pi)
    return 0.5 * x * (1.0 + jnp.tanh(k * (x + 0.044715 * x * x * x)))


def _gelu_and_grad(x):
    k = math.sqrt(2.0 / math.pi)
    t = jnp.tanh(k * (x + 0.044715 * x * x * x))
    g = 0.5 * x * (1.0 + t)
    dg = 0.5 * (1.0 + t) + 0.5 * x * (1.0 - t * t) * (k * (1.0 + 3.0 * 0.044715 * x * x))
    return g, dg


def _sigmoid(x):
    return 1.0 / (1.0 + jnp.exp(-x))


def _rms(x):
    r = lax.rsqrt(jnp.mean(x * x, axis=-1, keepdims=True) + EPS)
    return x * r, r


def _rms_bwd(dn, xhat, r):
    return r * (dn - xhat * jnp.mean(dn * xhat, axis=-1, keepdims=True))


def norm_pre(x, g):
    s = x.shape[0]
    tr = 512

    def body(x_ref, g_ref, h_ref):
        xhat, _ = _rms(x_ref[...])
        h_ref[...] = (xhat * g_ref[...]).astype(BF16)

    return pl.pallas_call(
        body, name="norm_pre", grid=(s // tr,),
        in_specs=[_rows(tr, D), _full((1, D))], out_specs=_rows(tr, D),
        out_shape=jax.ShapeDtypeStruct((s, D), BF16), compiler_params=_params("parallel"),
    )(x, g)


def mm_in(h, wg, order, staged, casting=()):
    s = h.shape[0]
    tm, tn = 1024, IN_SHARD // 2
    per = IN_SHARD // tn
    m = len(casting)
    nj, ni = N_CHIPS * per, s // tm
    cast_steps = per * ni

    def body(order_ref, *refs):
        a_ref = refs[0]
        cast_in = refs[2:2 + m]
        o_ref, held = refs[2 + m], refs[3 + m]
        cast_out = refs[4 + m:4 + 2 * m]
        tile, tile_sem = refs[4 + 2 * m:6 + 2 * m]
        sems = refs[6 + 2 * m:]
        j, i = pl.program_id(0), pl.program_id(1)

        @pl.when(j * ni + i < cast_steps)
        def _():
            for src, dst in zip(cast_in, cast_out):
                dst[...] = src[...].astype(BF16)

        def fetch(t):
            chip = order_ref[t // per]
            return pltpu.make_async_copy(held.at[chip, :, pl.ds((t % per) * tn, tn)], tile.at[t % 2],
                                         tile_sem.at[t % 2])

        if staged:
            near = _gather_phases([held], *sems[:2], [(0, D, (0, 1))])
            far = _relay_phases(held, *sems[2:])

        @pl.when(i == 0)
        def _():
            @pl.when(j == 0)
            def _():
                if staged:
                    near[0]()
                fetch(0).start()

            fetch(j).wait()
            ahead = j + 1 < nj
            if staged:
                ahead = ahead & (j + 1 != per) & (j + 1 != 3 * per)
            pl.when(ahead)(lambda: fetch(j + 1).start())

        rows = pl.ds(pl.multiple_of(i * tm, tm), tm)
        o_ref[...] = _dot(a_ref[rows, :], tile[j % 2]).astype(BF16)

        if staged:
            @pl.when((i == ni - 1) & (j == per - 1))
            def _():
                near[1]()
                near[2]()
                far[0]()
                fetch(per).start()

            @pl.when((i == ni - 1) & (j == 3 * per - 1))
            def _():
                far[1]()
                far[2]()
                fetch(3 * per).start()

    def cast_block(j, i, o):
        return jnp.minimum(j * ni + i, cast_steps - 1)

    out = pl.pallas_call(
        body, name="mm_in",
        grid_spec=pltpu.PrefetchScalarGridSpec(
            num_scalar_prefetch=1, grid=(nj, ni),
            in_specs=[pl.BlockSpec((s, D), lambda j, i, o: (0, 0)), ANY]
            + [pl.BlockSpec((a.shape[0] // cast_steps, a.shape[1]), lambda j, i, o: (cast_block(j, i, o), 0))
               for a in casting],
            out_specs=[pl.BlockSpec((tm, tn), lambda j, i, o: (i, o[j // per] * per + j % per)), ANY]
            + [pl.BlockSpec((None, a.shape[0] // cast_steps, a.shape[1]),
                            lambda j, i, o: (o[0], cast_block(j, i, o), 0)) for a in casting],
            scratch_shapes=[pltpu.VMEM((2, D, tn), BF16), pltpu.SemaphoreType.DMA((2,))]
            + (_gather_sems(1) + _relay_sems() if staged else [])),
        out_shape=[jax.ShapeDtypeStruct((s, IN_COLS), BF16), jax.ShapeDtypeStruct(wg.shape, wg.dtype)]
        + [jax.ShapeDtypeStruct((N_CHIPS,) + a.shape, BF16) for a in casting],
        input_output_aliases={2: 1},
        compiler_params=_params("arbitrary", "arbitrary", communicates=staged),
    )(order, h, wg, *casting)
    return out[0], out[1], out[2:]


def _tril_ws(ws_ref, g):
    r = lax.broadcasted_iota(jnp.int32, (CHUNK, CHUNK), 0)
    c = lax.broadcasted_iota(jnp.int32, (CHUNK, CHUNK), 1)
    return jnp.where(c <= r, ws_ref[g], 0.0).astype(BF16)


def _layer_norm(v):
    mu = jnp.mean(v, axis=-1, keepdims=True)
    d = v - mu
    rstd = lax.rsqrt(jnp.mean(d * d, axis=-1, keepdims=True) + EPS)
    return d * rstd, rstd


def gating_fwd(z, ln_g, ln_b, w_s, bs_t, gathering):
    s = z.shape[0]
    n = len(gathering)
    steps = s // CHUNK

    def body(*refs):
        u_ref, v_ref, lg_ref, lb_ref, ws_ref, bst_ref = refs[:6]
        ya_ref = refs[6 + n]
        ci = pl.program_id(0)
        if n:
            send, pass_on, finish = _gather_phases(refs[7 + n:7 + 2 * n], *refs[7 + 2 * n:], _spans(gathering))
            pl.when(ci == 0)(send)
        ug = _gelu(u_ref[...].astype(F32))
        vhat, _ = _layer_norm(_gelu(v_ref[...].astype(F32)))
        vn = (vhat * lg_ref[...] + lb_ref[...]).astype(BF16)
        for g in range(GROUPS):
            cols = slice(g * CHUNK, (g + 1) * CHUNK)
            mixed = _dot(_tril_ws(ws_ref, g), vn[:, cols]) + bst_ref[:, g:g + 1]
            ya_ref[:, cols] = (ug[:, cols] * mixed).astype(BF16)
        if n:
            pl.when(ci == steps - 1)(pass_on)
            pl.when(ci == steps - 1)(finish)

    out = pl.pallas_call(
        body, name="gating_fwd", grid=(steps,),
        in_specs=[_rows(CHUNK, D, 0), _rows(CHUNK, D, 1), _full((1, D)), _full((1, D)),
                  _full((GROUPS, CHUNK, CHUNK)), _full((CHUNK, GROUPS))] + [ANY] * n,
        out_specs=[_rows(CHUNK, D)] + [ANY] * n,
        out_shape=[jax.ShapeDtypeStruct((s, D), BF16)]
        + [jax.ShapeDtypeStruct(a.shape, a.dtype) for a in _arrays(gathering)],
        input_output_aliases={6 + w: 1 + w for w in range(n)},
        scratch_shapes=_gather_sems(n) if n else [],
        compiler_params=_params("arbitrary", communicates=bool(n)),
    )(z, z, ln_g, ln_b, w_s, bs_t, *_arrays(gathering))
    return out[0], out[1:]


def _attn_tables(s):
    nd = s // ATT_T
    r = np.arange(ATT_T)[None, :, None]
    c = np.arange(ATT_T)[None, None, :]
    delta = np.arange(nd)[:, None, None] * ATT_T + r - c
    count = np.zeros(delta.shape, np.int64)
    for window, dilation in ((128, 1), (512, 4), (2048, 16)):
        count += (delta >= 0) & (delta % dilation == 0) & (delta <= window)
    logc = np.where(count > 0, np.log(np.maximum(count, 1)), MASKED)
    return jnp.asarray(logc, F32)


AUG = 3


def _split3_np(x):
    terms, rest = [], np.asarray(x, np.float64)
    for _ in range(AUG):
        term = np.asarray(rest.astype(jnp.bfloat16), np.float64)
        terms.append(term)
        rest = rest - term
    return terms


def _split3(x):
    terms, rest = [], x
    for _ in range(AUG):
        term = rest.astype(BF16).astype(F32)
        terms.append(term)
        rest = rest - term
    return terms


def _alibi_tables(s):
    nb = s // ATT_T
    slopes = np.exp2(-8.0 * np.arange(1, HEADS + 1, dtype=np.float64) / HEADS)
    ka = np.zeros((HEADS // 2, 2, ATT_T, 128), np.float32)
    kb = np.zeros((HEADS // 2, 2, nb, 128), np.float32)
    for p in range(HEADS // 2):
        for e in range(2):
            base = HEAD_DIM * (1 - e)
            for a, term in enumerate(_split3_np(slopes[2 * p + e] * np.arange(ATT_T))):
                ka[p, e, :, base + a] = term
            for a, term in enumerate(_split3_np(slopes[2 * p + e] * ATT_T * np.arange(nb))):
                kb[p, e, :, base + AUG + a] = term
            ka[p, e, :, base + 2 * AUG:base + 3 * AUG] = 1.0
    return jnp.asarray(ka), jnp.asarray(kb)


def _head_masks():
    lane = lax.broadcasted_iota(jnp.int32, (1, 128), 1)
    first = lane < HEAD_DIM

    def ones(e, n):
        base = HEAD_DIM * (1 - e)
        return ((lane >= base) & (lane < base + n)).astype(F32)

    return first, lane, ones


def _place3(lane, at, terms, other):
    for a, term in enumerate(terms):
        other = jnp.where(lane == at + a, term, other)
    return other


def attn_fwd(z, logc, ka, kb, gathering):
    s = z.shape[0]
    nq = s // ATT_T
    t = ATT_T
    n = len(gathering)
    grp = ATT_GROUP
    ngrp = HEADS // 2 // grp
    wide = 128 * grp
    qcol, kcol, vcol = 2 * D // wide, 3 * D // wide, 4 * D // wide

    def body(*refs):
        q_ref, k_ref, v_ref, lc_ref, ka_ref, kb_ref = refs[:6]
        y_ref, lse_ref = refs[6 + n:8 + n]
        q_s, k_s, v_s, m_s, l_s, acc_s = refs[8 + 2 * n:14 + 2 * n]
        gi, qi = pl.program_id(0), pl.program_id(1)
        first, lane, ones = _head_masks()
        if n:
            send, pass_on, finish = _gather_phases(refs[8 + n:8 + 2 * n], *refs[14 + 2 * n:], _spans(gathering))
            pl.when((gi == 0) & (qi == 0))(send)

        @pl.when(qi == 0)
        def _():
            sel = jnp.broadcast_to(first.astype(F32), (t, 128))
            for pr in range(grp):
                cols = slice(pr * 128, (pr + 1) * 128)
                for jb in range(nq):
                    kj = k_ref[jb * t:(jb + 1) * t, cols].astype(F32)
                    vj = v_ref[jb * t:(jb + 1) * t, cols].astype(F32)
                    k_s[pr, 0, jb] = jnp.where(first, kj, ka_ref[pr, 0] + kb_ref[pr, 0, jb:jb + 1, :]).astype(BF16)
                    k_s[pr, 1, jb] = jnp.where(first, ka_ref[pr, 1] + kb_ref[pr, 1, jb:jb + 1, :], kj).astype(BF16)
                    v_s[pr, jb, 0:t, 0:128] = jnp.where(first, vj, 0.0).astype(BF16)
                    v_s[pr, jb, t:2 * t, 0:128] = jnp.where(first, 0.0, vj).astype(BF16)
                    v_s[pr, jb, 0:t, 128:256] = sel.astype(BF16)
                    v_s[pr, jb, t:2 * t, 128:256] = (1.0 - sel).astype(BF16)

        for pr in range(grp):
            q = q_ref[:, pr * 128:(pr + 1) * 128].astype(F32) * (1.0 / math.sqrt(HEAD_DIM))
            q_s[pr, 0] = jnp.where(first, q, ones(0, 2 * AUG)).astype(BF16)
            q_s[pr, 1] = jnp.where(first, ones(1, 2 * AUG), q).astype(BF16)
        m_s[...] = jnp.full_like(m_s, MASKED)
        l_s[...] = jnp.zeros_like(l_s)
        acc_s[...] = jnp.zeros_like(acc_s)

        def scores(j):
            return tuple(_dot(q_s[pr, e], k_s[pr, e, j], NT) for pr in range(grp) for e in range(2))

        def step(j, carry):
            softmax_block(j, scores(j))
            return carry

        def softmax_block(j, u):
            lc = lc_ref[qi - j]
            for pr in range(grp):
                u0 = u[2 * pr] + lc
                u1 = u[2 * pr + 1] + lc
                m0, m1 = m_s[pr, 0], m_s[pr, 1]
                n0 = jnp.maximum(m0, jnp.max(u0, axis=-1, keepdims=True))
                n1 = jnp.maximum(m1, jnp.max(u1, axis=-1, keepdims=True))
                m_s[pr, 0], m_s[pr, 1] = n0, n1
                p = jnp.concatenate([jnp.exp(u0 - jnp.concatenate([n0, n0], axis=1)).astype(BF16),
                                     jnp.exp(u1 - jnp.concatenate([n1, n1], axis=1)).astype(BF16)], axis=1)
                pv = _dot(p, v_s[pr, j])
                alpha = jnp.where(first, jnp.exp(m0 - n0), jnp.exp(m1 - n1))
                acc_s[pr] = acc_s[pr] * alpha + pv[:, 0:128]
                l_s[pr] = l_s[pr] * alpha + pv[:, 128:256]

        lax.fori_loop(0, qi + 1, step, 0)
        for pr in range(grp):
            cols = slice(pr * 128, (pr + 1) * 128)
            y_ref[:, cols] = (acc_s[pr] / l_s[pr]).astype(BF16)
            lse_ref[:, cols] = jnp.where(first, m_s[pr, 0], m_s[pr, 1]) + jnp.log(l_s[pr])
        if n:
            pl.when((gi == ngrp - 1) & (qi == nq - 1))(pass_on)
            pl.when((gi == ngrp - 1) & (qi == nq - 1))(finish)

    out = pl.pallas_call(
        body, name="attn_fwd", grid=(ngrp, nq),
        in_specs=[pl.BlockSpec((t, wide), lambda g, i: (i, qcol + g)),
                  pl.BlockSpec((s, wide), lambda g, i: (0, kcol + g)),
                  pl.BlockSpec((s, wide), lambda g, i: (0, vcol + g)),
                  _full((nq, t, t)),
                  pl.BlockSpec((grp, 2, t, 128), lambda g, i: (g, 0, 0, 0)),
                  pl.BlockSpec((grp, 2, nq, 128), lambda g, i: (g, 0, 0, 0))] + [ANY] * n,
        out_specs=[pl.BlockSpec((t, wide), lambda g, i: (i, g)), pl.BlockSpec((t, wide), lambda g, i: (i, g))]
        + [ANY] * n,
        out_shape=[jax.ShapeDtypeStruct((s, D), BF16), jax.ShapeDtypeStruct((s, D), F32)]
        + [jax.ShapeDtypeStruct(a.shape, a.dtype) for a in _arrays(gathering)],
        input_output_aliases={6 + w: 2 + w for w in range(n)},
        scratch_shapes=[pltpu.VMEM((grp, 2, t, 128), BF16), pltpu.VMEM((grp, 2, nq, t, 128), BF16),
                        pltpu.VMEM((grp, nq, 2 * t, 256), BF16), pltpu.VMEM((grp, 2, t, 128), F32),
                        pltpu.VMEM((grp, t, 128), F32), pltpu.VMEM((grp, t, 128), F32)]
        + (_gather_sems(n) if n else []),
        compiler_params=_params("arbitrary", "arbitrary", communicates=bool(n)),
    )(z, z, z, logc, ka, kb, *_arrays(gathering))
    return out[0], out[1], out[2:]


def proj_merge(ya, yb, wa, wb, z, bg, gathering):
    s = ya.shape[0]
    tm = 512
    n = len(gathering)
    steps = s // tm

    def body(*refs):
        ya_ref, yb_ref, wa_ref, wb_ref, ga_ref, gb_ref, bg_ref = refs[:7]
        mg_ref, pa_ref, pb_ref = refs[7 + n:10 + n]
        i = pl.program_id(0)
        if n:
            send, pass_on, finish = _gather_phases(refs[10 + n:10 + 2 * n], *refs[10 + 2 * n:], _spans(gathering))
            pl.when(i == 0)(send)
            pl.when(i == steps - 1)(pass_on)
        pa = _dot(ya_ref[...], wa_ref[...])
        pb = _dot(yb_ref[...], wb_ref[...])
        sa = _sigmoid(ga_ref[...] + bg_ref[0:1, :])
        sb = _sigmoid(gb_ref[...] + bg_ref[1:2, :])
        mg_ref[...] = (sa * pa + sb * pb).astype(BF16)
        pa_ref[...] = pa.astype(BF16)
        pb_ref[...] = pb.astype(BF16)
        if n:
            pl.when(i == steps - 1)(finish)

    out = jax.ShapeDtypeStruct((s, D), BF16)
    res = pl.pallas_call(
        body, name="proj_merge", grid=(steps,),
        in_specs=[_rows(tm, D), _rows(tm, D), _full((D, D)), _full((D, D)),
                  _rows(tm, D, 5), _rows(tm, D, 6), _full((2, D))] + [ANY] * n,
        out_specs=[_rows(tm, D)] * 3 + [ANY] * n,
        out_shape=[out] * 3 + [jax.ShapeDtypeStruct(a.shape, a.dtype) for a in _arrays(gathering)],
        input_output_aliases={7 + w: 3 + w for w in range(n)},
        scratch_shapes=_gather_sems(n) if n else [],
        compiler_params=_params("arbitrary", communicates=bool(n)),
    )(ya, yb, wa, wb, z, z, bg, *_arrays(gathering))
    return res[0], res[1], res[2], res[3:]


def out_norm(merged, w_out, x, g_post, g_fpre, gathering):
    s = x.shape[0]
    tm = 512
    n = len(gathering)
    steps = s // tm

    def body(*refs):
        mg_ref, w_ref, x_ref, gp_ref, gf_ref = refs[:5]
        o_ref, x1_ref, h2_ref = refs[5 + n:8 + n]
        i = pl.program_id(0)
        if n:
            send, pass_on, finish = _gather_phases(refs[8 + n:8 + 2 * n], *refs[8 + 2 * n:], _spans(gathering))
            pl.when(i == 0)(send)
            pl.when(i == steps - 1)(pass_on)
        o = _dot(mg_ref[...], w_ref[...])
        ohat, _ = _rms(o)
        x1 = x_ref[...] + ohat * gp_ref[...]
        x1hat, _ = _rms(x1)
        o_ref[...] = o
        x1_ref[...] = x1
        h2_ref[...] = (x1hat * gf_ref[...]).astype(BF16)
        if n:
            pl.when(i == steps - 1)(finish)

    res = pl.pallas_call(
        body, name="out_norm", grid=(steps,),
        in_specs=[_rows(tm, D), _full((D, D)), _rows(tm, D), _full((1, D)), _full((1, D))] + [ANY] * n,
        out_specs=[_rows(tm, D)] * 3 + [ANY] * n,
        out_shape=[jax.ShapeDtypeStruct((s, D), F32), jax.ShapeDtypeStruct((s, D), F32),
                   jax.ShapeDtypeStruct((s, D), BF16)]
        + [jax.ShapeDtypeStruct(a.shape, a.dtype) for a in _arrays(gathering)],
        input_output_aliases={5 + w: 3 + w for w in range(n)},
        scratch_shapes=_gather_sems(n) if n else [],
        compiler_params=_params("arbitrary", communicates=bool(n)),
    )(merged, w_out, x, g_post, g_fpre, *_arrays(gathering))
    return res[0], res[1], res[2], res[3:]


def mm_ff1(h2, wg, gathering):
    s = h2.shape[0]
    tm = 1024
    n = len(gathering)
    ni = s // tm

    def body(*refs):
        a_ref, b_ref = refs[:2]
        o_ref, r_ref = refs[2 + n:4 + n]
        i, j = pl.program_id(0), pl.program_id(1)
        if n:
            send, pass_on, finish = _gather_phases(refs[4 + n:4 + 2 * n], *refs[4 + 2 * n:], _spans(gathering))
            pl.when((i == 0) & (j == 0))(send)
            pl.when((i == ni - 1) & (j == N_CHIPS // 2))(pass_on)
        a = _dot(a_ref[...], b_ref[...])
        o_ref[...] = a.astype(BF16)
        r = jnp.maximum(a, 0.0)
        r_ref[...] = (r * r).astype(BF16)
        if n:
            pl.when((i == ni - 1) & (j == N_CHIPS - 1))(finish)

    res = pl.pallas_call(
        body, name="mm_ff1", grid=(ni, N_CHIPS),
        in_specs=[pl.BlockSpec((tm, D), lambda i, j: (i, 0)), pl.BlockSpec((None, D, D), lambda i, j: (j, 0, 0))]
        + [ANY] * n,
        out_specs=[pl.BlockSpec((tm, D), lambda i, j: (i, j))] * 2 + [ANY] * n,
        out_shape=[jax.ShapeDtypeStruct((s, D_FF), BF16), jax.ShapeDtypeStruct((s, D_FF), BF16)]
        + [jax.ShapeDtypeStruct(a.shape, a.dtype) for a in _arrays(gathering)],
        input_output_aliases={2 + w: 2 + w for w in range(n)},
        scratch_shapes=_gather_sems(n) if n else [],
        compiler_params=_params("arbitrary", "arbitrary", communicates=bool(n)),
    )(h2, wg, *_arrays(gathering))
    return res[0], res[1], res[2:]


def ff2_loss(rl, w_ff2, x1, target, g_fpost):
    s = x1.shape[0]
    tm = 256

    def body(rl_ref, w_ref, x1_ref, t_ref, g_ref, dy_ref, df_ref, dg_ref, loss_ref):
        @pl.when(pl.program_id(0) == 0)
        def _():
            dg_ref[...] = jnp.zeros_like(dg_ref)
            loss_ref[...] = jnp.zeros_like(loss_ref)

        f = _dot(rl_ref[...], w_ref[...])
        fhat, r = _rms(f)
        err = x1_ref[...] + fhat * g_ref[...] - t_ref[...]
        loss_ref[...] += 0.5 * jnp.sum(jnp.mean(err * err, axis=-1, keepdims=True), axis=0, keepdims=True)
        dy = err * (1.0 / D)
        dy_ref[...] = dy
        dg_ref[...] += jnp.sum(dy * fhat, axis=0, keepdims=True)
        df_ref[...] = _rms_bwd(dy * g_ref[...], fhat, r).astype(BF16)

    return pl.pallas_call(
        body, name="ff2_loss", grid=(s // tm,),
        in_specs=[_rows(tm, D_FF), _full((D_FF, D)), _rows(tm, D), _rows(tm, D), _full((1, D))],
        out_specs=[_rows(tm, D), _rows(tm, D), _full((1, D)), _full((1, 1))],
        out_shape=[jax.ShapeDtypeStruct((s, D), F32), jax.ShapeDtypeStruct((s, D), BF16),
                   jax.ShapeDtypeStruct((1, D), F32), jax.ShapeDtypeStruct((1, 1), F32)],
        compiler_params=_params("arbitrary"),
    )(rl, w_ff2, x1, target, g_fpost)


def mm_tn(name, a, b, ta, tb, out_shape, out_spec):
    s = a.shape[0]

    def body(a_ref, b_ref, o_ref):
        o_ref[...] = _dot(a_ref[...], b_ref[...], TN)

    return pl.pallas_call(
        body, name=name, grid=(a.shape[1] // ta, b.shape[1] // tb),
        in_specs=[pl.BlockSpec((s, ta), lambda i, j: (0, i)), pl.BlockSpec((s, tb), lambda i, j: (0, j))],
        out_specs=out_spec, out_shape=jax.ShapeDtypeStruct(out_shape, F32),
        compiler_params=_params("parallel", "parallel"),
    )(a, b)


def mm_nt(name, a, w):
    s = a.shape[0]
    tm = 512

    def body(a_ref, w_ref, o_ref):
        o_ref[...] = _dot(a_ref[...], w_ref[...], NT).astype(BF16)

    return pl.pallas_call(
        body, name=name, grid=(s // tm,), in_specs=[_rows(tm, D), _full((D, D))], out_specs=_rows(tm, D),
        out_shape=jax.ShapeDtypeStruct((s, D), BF16), compiler_params=_params("parallel"),
    )(a, w)


def ff2_bwd(df, w_ff2, a):
    s = df.shape[0]
    tm = 1024

    def body(df_ref, w_ref, a_ref, da_ref):
        drl = _dot(df_ref[...], w_ref[...], NT)
        da_ref[...] = (drl * (2.0 * jnp.maximum(a_ref[...].astype(F32), 0.0))).astype(BF16)

    return pl.pallas_call(
        body, name="ff2_bwd", grid=(s // tm, D_FF // D),
        in_specs=[pl.BlockSpec((tm, D), lambda i, j: (i, 0)), pl.BlockSpec((D, D), lambda i, j: (j, 0)),
                  pl.BlockSpec((tm, D), lambda i, j: (i, j))],
        out_specs=pl.BlockSpec((tm, D), lambda i, j: (i, j)),
        out_shape=jax.ShapeDtypeStruct((s, D_FF), BF16), compiler_params=_params("parallel", "parallel"),
    )(df, w_ff2, a)


def ff1_bwd_norms(da, wg, x1, o, dy, g_fpre, g_post, swapping):
    s = x1.shape[0]
    tm = 256
    n = len(swapping)
    steps = s // tm

    def body(*refs):
        da_ref, w_ref, x1_ref, o_ref, dy_ref, gf_ref, gp_ref = refs[:7]
        dx1_ref, do_ref, dgf_ref, dgp_ref = refs[7 + n:11 + n]
        i = pl.program_id(0)
        if n:
            send, finish = _swap_phases(refs[7:7 + n], refs[11 + n:11 + 2 * n], *refs[11 + 2 * n:])
            pl.when(i == 0)(send)

        @pl.when(i == 0)
        def _():
            dgf_ref[...] = jnp.zeros_like(dgf_ref)
            dgp_ref[...] = jnp.zeros_like(dgp_ref)

        dh2 = _dot(da_ref[:, 0:D], w_ref[0], NT)
        for k in range(1, N_CHIPS):
            dh2 = dh2 + _dot(da_ref[:, k * D:(k + 1) * D], w_ref[k], NT)
        x1hat, r2 = _rms(x1_ref[...])
        dgf_ref[...] += jnp.sum(dh2 * x1hat, axis=0, keepdims=True)
        dx1 = dy_ref[...] + _rms_bwd(dh2 * gf_ref[...], x1hat, r2)
        ohat, r1 = _rms(o_ref[...])
        dgp_ref[...] += jnp.sum(dx1 * ohat, axis=0, keepdims=True)
        dx1_ref[...] = dx1
        do_ref[...] = _rms_bwd(dx1 * gp_ref[...], ohat, r1).astype(BF16)
        if n:
            pl.when(i == steps - 1)(finish)

    res = pl.pallas_call(
        body, name="ff1_bwd_norms", grid=(steps,),
        in_specs=[_rows(tm, D_FF), _full((N_CHIPS, D, D)), _rows(tm, D), _rows(tm, D), _rows(tm, D),
                  _full((1, D)), _full((1, D))] + [ANY] * n,
        out_specs=[_rows(tm, D), _rows(tm, D), _full((1, D)), _full((1, D))] + [ANY] * n,
        out_shape=[jax.ShapeDtypeStruct((s, D), F32), jax.ShapeDtypeStruct((s, D), BF16),
                   jax.ShapeDtypeStruct((1, D), F32), jax.ShapeDtypeStruct((1, D), F32)] + _swap_shapes(swapping),
        scratch_shapes=_swap_sems(n) if n else [],
        compiler_params=_params("arbitrary", communicates=bool(n)),
    )(da, wg, x1, o, dy, g_fpre, g_post, *swapping)
    return res[0], res[1], res[2], res[3], res[4:]


def out_bwd_gates(do, w_out, pa, pb, z, bg):
    s = do.shape[0]
    tm = 512

    def body(do_ref, w_ref, pa_ref, pb_ref, ga_ref, gb_ref, bg_ref, dpa_ref, dpb_ref, dga_ref, dgb_ref, dbg_ref):
        @pl.when(pl.program_id(0) == 0)
        def _():
            dbg_ref[...] = jnp.zeros_like(dbg_ref)

        dm = _dot(do_ref[...], w_ref[...], NT)
        sa = _sigmoid(ga_ref[...] + bg_ref[0:1, :])
        sb = _sigmoid(gb_ref[...] + bg_ref[1:2, :])
        dpa_ref[...] = (dm * sa).astype(BF16)
        dpb_ref[...] = (dm * sb).astype(BF16)
        dga = dm * pa_ref[...].astype(F32) * (sa * (1.0 - sa))
        dgb = dm * pb_ref[...].astype(F32) * (sb * (1.0 - sb))
        dga_ref[...] = dga.astype(BF16)
        dgb_ref[...] = dgb.astype(BF16)
        dbg_ref[0:1, :] += jnp.sum(dga, axis=0, keepdims=True)
        dbg_ref[1:2, :] += jnp.sum(dgb, axis=0, keepdims=True)

    out = jax.ShapeDtypeStruct((s, D), BF16)
    return pl.pallas_call(
        body, name="out_bwd_gates", grid=(s // tm,),
        in_specs=[_rows(tm, D), _full((D, D)), _rows(tm, D), _rows(tm, D), _rows(tm, D, 5), _rows(tm, D, 6),
                  _full((2, D))],
        out_specs=[_rows(tm, D)] * 4 + [_full((2, D))],
        out_shape=[out] * 4 + [jax.ShapeDtypeStruct((2, D), F32)], compiler_params=_params("arbitrary"),
    )(do, w_out, pa, pb, z, z, bg)


def gating_bwd(z, dya, ln_g, ln_b, w_s, bs_t, swapping):
    s = z.shape[0]
    ones = functools.partial(jnp.ones, (8, CHUNK), BF16)
    n = len(swapping)

    def body(*refs):
        u_ref, v_ref, dya_ref, lg_ref, lb_ref, ws_ref, bst_ref = refs[:7]
        du_ref, dv_ref, dws_ref, dbs_ref, dlg_ref, dlb_ref = refs[7 + n:13 + n]
        dvn_ref = refs[13 + 2 * n]
        ci = pl.program_id(0)
        if n:
            send, finish = _swap_phases(refs[7:7 + n], refs[13 + n:13 + 2 * n], *refs[14 + 2 * n:])
            pl.when(ci == 0)(send)

        @pl.when(ci == 0)
        def _():
            dws_ref[...] = jnp.zeros_like(dws_ref)
            dbs_ref[...] = jnp.zeros_like(dbs_ref)
            dlg_ref[...] = jnp.zeros_like(dlg_ref)
            dlb_ref[...] = jnp.zeros_like(dlb_ref)

        ug, dug_du = _gelu_and_grad(u_ref[...].astype(F32))
        vg, dvg_dv = _gelu_and_grad(v_ref[...].astype(F32))
        vhat, rstd = _layer_norm(vg)
        vn = (vhat * lg_ref[...] + lb_ref[...]).astype(BF16)
        dya = dya_ref[...].astype(F32)
        for g in range(GROUPS):
            cols = slice(g * CHUNK, (g + 1) * CHUNK)
            ws = _tril_ws(ws_ref, g)
            mixed = _dot(ws, vn[:, cols]) + bst_ref[:, g:g + 1]
            du_ref[:, cols] = (dya[:, cols] * mixed * dug_du[:, cols]).astype(BF16)
            dmix = (dya[:, cols] * ug[:, cols]).astype(BF16)
            dbs_ref[g] += _dot(ones(), dmix, NT)
            dws_ref[g] += _dot(dmix, vn[:, cols], NT)
            dvn_ref[:, cols] = _dot(ws, dmix, TN)
        dvn = dvn_ref[...]
        dlg_ref[...] += jnp.sum(dvn * vhat, axis=0, keepdims=True)
        dlb_ref[...] += jnp.sum(dvn, axis=0, keepdims=True)
        dvh = dvn * lg_ref[...]
        dvg = rstd * (dvh - jnp.mean(dvh, axis=-1, keepdims=True)
                      - vhat * jnp.mean(dvh * vhat, axis=-1, keepdims=True))
        dv_ref[...] = (dvg * dvg_dv).astype(BF16)

        @pl.when(ci == pl.num_programs(0) - 1)
        def _():
            r = lax.broadcasted_iota(jnp.int32, (CHUNK, CHUNK), 0)
            c = lax.broadcasted_iota(jnp.int32, (CHUNK, CHUNK), 1)
            for g in range(GROUPS):
                dws_ref[g] = jnp.where(c <= r, dws_ref[g], 0.0)

        if n:
            pl.when(ci == pl.num_programs(0) - 1)(finish)

    out = jax.ShapeDtypeStruct((s, D), BF16)
    res = pl.pallas_call(
        body, name="gating_bwd", grid=(s // CHUNK,),
        in_specs=[_rows(CHUNK, D, 0), _rows(CHUNK, D, 1), _rows(CHUNK, D), _full((1, D)), _full((1, D)),
                  _full((GROUPS, CHUNK, CHUNK)), _full((CHUNK, GROUPS))] + [ANY] * n,
        out_specs=[_rows(CHUNK, D), _rows(CHUNK, D), _full((GROUPS, CHUNK, CHUNK)), _full((GROUPS, 8, CHUNK)),
                   _full((1, D)), _full((1, D))] + [ANY] * n,
        out_shape=[out, out, jax.ShapeDtypeStruct((GROUPS, CHUNK, CHUNK), F32),
                   jax.ShapeDtypeStruct((GROUPS, 8, CHUNK), F32),
                   jax.ShapeDtypeStruct((1, D), F32), jax.ShapeDtypeStruct((1, D), F32)] + _swap_shapes(swapping),
        scratch_shapes=[pltpu.VMEM((CHUNK, D), F32)] + (_swap_sems(n) if n else []),
        compiler_params=_params("arbitrary", communicates=bool(n)),
    )(z, z, dya, ln_g, ln_b, w_s, bs_t, *swapping)
    return (*res[:6], res[6:])


def attn_bwd(z, yb, dyb, lse, logc, ka, kb, scattering, gathering=None):
    s = z.shape[0]
    nq = s // ATT_T
    t = ATT_T
    grp = ATT_BWD_GROUP
    ngrp = HEADS // 2 // grp
    wide = 128 * grp
    qcol, kcol, vcol = 2 * D // wide, 3 * D // wide, 4 * D // wide
    scale = 1.0 / math.sqrt(HEAD_DIM)
    n = len(scattering)
    g8 = 0 if gathering is None else 1

    def body(*refs):
        q_ref, k_ref, v_ref, y_ref, dy_ref, lse_ref, lc_ref, ka_ref, kb_ref = refs[:9]
        dq_ref, dk_ref, dv_ref = refs[9 + n + g8:12 + n + g8]
        qa_s, qt_s, da_s, dt_s, dq_s, dkt_s, dvt_s = refs[12 + 2 * n + 2 * g8:19 + 2 * n + 2 * g8]
        sems = refs[19 + 2 * n + 2 * g8:]
        gi, j = pl.program_id(0), pl.program_id(1)
        first, lane, ones = _head_masks()
        if n:
            send, finish = _scatter_phases(refs[9:9 + n], refs[12 + n + g8:12 + 2 * n + g8], *sems[:2])
            pl.when((gi == 0) & (j == 0))(send)
        if g8:
            send8, pass_on8, finish8 = _allgather8_phases(refs[12 + 2 * n + g8], *sems[2 * (n > 0):])
            pl.when((gi == 0) & (j == 0))(send8)
            pl.when((gi == ngrp - 1) & (j == nq - 1))(pass_on8)

        @pl.when(j == 0)
        def _():
            dq_s[...] = jnp.zeros_like(dq_s)
            for pr in range(grp):
                cols = slice(pr * 128, (pr + 1) * 128)
                for ib in range(nq):
                    rows = slice(ib * t, (ib + 1) * t)
                    q = q_ref[rows, cols].astype(F32) * scale
                    lse = lse_ref[rows, cols]
                    qa_s[pr, 0, ib] = jnp.where(first, q, _place3(lane, HEAD_DIM + 2 * AUG, _split3(-lse[:, 0:1]),
                                                                  ones(0, 2 * AUG))).astype(BF16)
                    qa_s[pr, 1, ib] = jnp.where(
                        first, _place3(lane, 2 * AUG, _split3(-lse[:, HEAD_DIM:HEAD_DIM + 1]), ones(1, 2 * AUG)),
                        q).astype(BF16)
                    qt_s[pr, ib, :, 0:t] = jnp.where(first, q, 0.0).T.astype(BF16)
                    qt_s[pr, ib, :, t:2 * t] = jnp.where(first, 0.0, q).T.astype(BF16)
                    do = dy_ref[rows, cols].astype(F32)
                    prod = do * y_ref[rows, cols].astype(F32)
                    dd0 = jnp.sum(jnp.where(first, prod, 0.0), axis=-1, keepdims=True)
                    dd1 = jnp.sum(jnp.where(first, 0.0, prod), axis=-1, keepdims=True)
                    da_s[pr, 0, ib] = jnp.where(first, do, _place3(lane, HEAD_DIM, _split3(-dd0), 0.0)).astype(BF16)
                    da_s[pr, 1, ib] = jnp.where(first, _place3(lane, 0, _split3(-dd1), 0.0), do).astype(BF16)
                    dt_s[pr, ib, :, 0:t] = jnp.where(first, do, 0.0).T.astype(BF16)
                    dt_s[pr, ib, :, t:2 * t] = jnp.where(first, 0.0, do).T.astype(BF16)

        keys = []
        for pr in range(grp):
            kj = k_ref[:, pr * 128:(pr + 1) * 128].astype(F32)
            vj = v_ref[:, pr * 128:(pr + 1) * 128].astype(F32)
            keys.append((
                jnp.where(first, kj, ka_ref[pr, 0] + kb_ref[pr, 0, pl.ds(j, 1), :]).astype(BF16),
                jnp.where(first, ka_ref[pr, 1] + kb_ref[pr, 1, pl.ds(j, 1), :], kj).astype(BF16),
                jnp.concatenate([jnp.where(first, kj, 0.0), jnp.where(first, 0.0, kj)], axis=0).astype(BF16),
                jnp.where(first, vj, ones(0, AUG)).astype(BF16),
                jnp.where(first, ones(1, AUG), vj).astype(BF16)))
        dkt_s[...] = jnp.zeros_like(dkt_s)
        dvt_s[...] = jnp.zeros_like(dvt_s)

        def step(i, _):
            lc = lc_ref[i - j]
            rows = pl.ds(pl.multiple_of(i * t, t), t)
            for pr in range(grp):
                k0a, k1a, kst, v0a, v1a = keys[pr]
                p0 = jnp.exp(_dot(qa_s[pr, 0, i], k0a, NT) + lc)
                p1 = jnp.exp(_dot(qa_s[pr, 1, i], k1a, NT) + lc)
                e0 = (p0 * _dot(da_s[pr, 0, i], v0a, NT)).astype(BF16)
                e1 = (p1 * _dot(da_s[pr, 1, i], v1a, NT)).astype(BF16)
                dq_s[pr, rows, :] += _dot(jnp.concatenate([e0, e1], axis=1), kst)
                dvt_s[pr] += _dot(dt_s[pr, i], jnp.concatenate([p0.astype(BF16), p1.astype(BF16)], axis=0))
                dkt_s[pr] += _dot(qt_s[pr, i], jnp.concatenate([e0, e1], axis=0))
            return 0

        lax.fori_loop(j, nq, step, 0)
        for pr in range(grp):
            dk_ref[:, pr * 128:(pr + 1) * 128] = dkt_s[pr].T.astype(BF16)
            dv_ref[:, pr * 128:(pr + 1) * 128] = dvt_s[pr].T.astype(BF16)

        @pl.when(j == nq - 1)
        def _():
            for pr in range(grp):
                dq_ref[:, pr * 128:(pr + 1) * 128] = (dq_s[pr] * scale).astype(BF16)

        if n:
            pl.when((gi == ngrp - 1) & (j == nq - 1))(finish)
        if g8:
            pl.when((gi == ngrp - 1) & (j == nq - 1))(finish8)

    colblock = lambda c: pl.BlockSpec((s, wide), lambda g, j: (0, c + g))
    once = lambda c: pl.BlockSpec((s, wide), lambda g, j: (0, c + g), pipeline_mode=pl.Buffered(1))
    blk = lambda c: pl.BlockSpec((t, wide), lambda g, j: (j, c + g))
    out = jax.ShapeDtypeStruct((s, D), BF16)
    res = pl.pallas_call(
        body, name="attn_bwd", grid=(ngrp, nq),
        in_specs=[once(qcol), blk(kcol), blk(vcol), once(0), once(0), once(0),
                  pl.BlockSpec((nq, t, t), lambda g, j: (0, 0, 0), pipeline_mode=pl.Buffered(1)),
                  pl.BlockSpec((grp, 2, t, 128), lambda g, j: (g, 0, 0, 0)),
                  pl.BlockSpec((grp, 2, nq, 128), lambda g, j: (g, 0, 0, 0))] + [ANY] * (n + g8),
        out_specs=[colblock(0), blk(0), blk(0)] + [ANY] * (n + g8),
        out_shape=[out] * 3 + _scatter_shapes(scattering)
        + ([jax.ShapeDtypeStruct(gathering.shape, gathering.dtype)] if g8 else []),
        input_output_aliases={9 + n: 3 + n} if g8 else {},
        scratch_shapes=[pltpu.VMEM((grp, 2, nq, t, 128), BF16), pltpu.VMEM((grp, nq, 128, 2 * t), BF16),
                        pltpu.VMEM((grp, 2, nq, t, 128), BF16), pltpu.VMEM((grp, nq, 128, 2 * t), BF16),
                        pltpu.VMEM((grp, s, 128), F32), pltpu.VMEM((grp, 128, t), F32),
                        pltpu.VMEM((grp, 128, t), F32)]
        + (_scatter_sems(n) if n else [])
        + ([pltpu.SemaphoreType.DMA((7,)), pltpu.SemaphoreType.DMA((7,))] if g8 else []),
        compiler_params=_params("arbitrary", "arbitrary", communicates=bool(n + g8)),
    )(z, z, z, yb, dyb, lse, logc, ka, kb, *scattering, *([gathering] if g8 else []))
    return res[0], res[1], res[2], res[3:3 + n], (res[3 + n] if g8 else None)


def in_bwd_norm(dz, wg, x, dx1, g_pre, scattering, gathering=None):
    s = x.shape[0]
    tm = 512
    n = len(scattering)
    g = 0 if gathering is None else 1
    last = (s // tm - 1, N_CHIPS - 1)

    def body(*refs):
        dz_ref, w_ref, x_ref, dx1_ref, g_ref = refs[:5]
        dx_ref, dg_ref = refs[5 + n + g:7 + n + g]
        acc_ref = refs[7 + 2 * n + 2 * g]
        sems = refs[8 + 2 * n + 2 * g:]
        i, k = pl.program_id(0), pl.program_id(1)
        if n:
            send, finish = _scatter_phases(refs[5:5 + n], refs[7 + n + g:7 + 2 * n + g], *sems[:2])
            pl.when((i == 0) & (k == 0))(send)
        if g:
            send8, pass_on8, finish8 = _allgather8_phases(refs[7 + 2 * n + g], *sems[2 * (n > 0):])
            pl.when((i == 0) & (k == 0))(send8)
            pl.when((i == last[0]) & (k == last[1]))(pass_on8)

        @pl.when((i == 0) & (k == 0))
        def _():
            dg_ref[...] = jnp.zeros_like(dg_ref)

        part = _dot(dz_ref[...], w_ref[...], NT)

        @pl.when(k == 0)
        def _():
            acc_ref[...] = part

        @pl.when(k > 0)
        def _():
            acc_ref[...] += part

        @pl.when(k == N_CHIPS - 1)
        def _():
            dh = acc_ref[...]
            xhat, r = _rms(x_ref[...])
            dg_ref[...] += jnp.sum(dh * xhat, axis=0, keepdims=True)
            dx_ref[...] = dx1_ref[...] + _rms_bwd(dh * g_ref[...], xhat, r)

        if n:
            pl.when((i == last[0]) & (k == last[1]))(finish)
        if g:
            pl.when((i == last[0]) & (k == last[1]))(finish8)

    row = pl.BlockSpec((tm, D), lambda i, k: (i, 0))
    vec = pl.BlockSpec((1, D), lambda i, k: (0, 0))
    res = pl.pallas_call(
        body, name="in_bwd_norm", grid=(s // tm, N_CHIPS),
        in_specs=[pl.BlockSpec((tm, IN_SHARD), lambda i, k: (i, k)),
                  pl.BlockSpec((None, D, IN_SHARD), lambda i, k: (k, 0, 0)), row, row, vec] + [ANY] * (n + g),
        out_specs=[row, vec] + [ANY] * (n + g),
        out_shape=[jax.ShapeDtypeStruct((s, D), F32), jax.ShapeDtypeStruct((1, D), F32)]
        + _scatter_shapes(scattering) + ([jax.ShapeDtypeStruct(gathering.shape, gathering.dtype)] if g else []),
        input_output_aliases={5 + n: 2 + n} if g else {},
        scratch_shapes=[pltpu.VMEM((tm, D), F32)] + (_scatter_sems(n) if n else [])
        + ([pltpu.SemaphoreType.DMA((7,)), pltpu.SemaphoreType.DMA((7,))] if g else []),
        compiler_params=_params("arbitrary", "arbitrary", communicates=bool(n + g)),
    )(dz, wg, x, dx1, g_pre, *scattering, *([gathering] if g else []))
    return res[0], res[1], res[2:2 + n], (res[2 + n] if g else None)


def _adamw_math(w, g, m, v):
    m = ADAM_B1 * m + (1.0 - ADAM_B1) * g
    v = ADAM_B2 * v + (1.0 - ADAM_B2) * (g * g)
    m_hat = m / (1.0 - ADAM_B1 ** ADAM_STEP)
    v_hat = v / (1.0 - ADAM_B2 ** ADAM_STEP)
    delta = -ADAM_LR * (m_hat / (jnp.sqrt(v_hat) + ADAM_EPS) + ADAM_WD * w)
    return delta, m, v


def adamw(name, w, g, m, v, tr):
    r, c = w.shape

    def body(w_ref, g_ref, m_ref, v_ref, go_ref, d_ref, nm_ref, nv_ref):
        g = g_ref[...]
        go_ref[...] = g
        d_ref[...], nm_ref[...], nv_ref[...] = _adamw_math(w_ref[...], g, m_ref[...], v_ref[...])

    out = jax.ShapeDtypeStruct((r, c), F32)
    return pl.pallas_call(
        body, name=name, grid=(r // tr,), in_specs=[_rows(tr, c)] * 4, out_specs=[_rows(tr, c)] * 4,
        out_shape=[out] * 4, compiler_params=_params("parallel"),
    )(w, g, m, v)


def _allgather8_phases(buf, send_sems, recv_sems):
    x, y, c, chips = _place()
    me = 2 * x + y
    sibling = (x, y, 1 - c)
    rows = buf.shape[1] // 2

    def part(chip, core):
        return buf.at[chip, pl.ds(core * rows, rows)]

    def copy(k, block, to):
        return pltpu.make_async_remote_copy(src_ref=block, dst_ref=block, send_sem=send_sems.at[k],
                                            recv_sem=recv_sems.at[k], device_id=to, device_id_type=MESH)

    def chip_of(j):
        return 2 * chips[j][0] + chips[j][1]

    def send():
        copy(0, part(me, c), sibling).start()
        for j in range(3):
            copy(1 + j, part(me, c), (chips[j][0], chips[j][1], c)).start()

    def pass_on():
        for j in range(3):
            copy(1 + j, part(chip_of(j), c), (chips[j][0], chips[j][1], c)).wait_recv()
            copy(4 + j, part(chip_of(j), c), sibling).start()

    def finish():
        copy(0, part(me, 1 - c), sibling).wait_recv()
        for j in range(3):
            copy(4 + j, part(chip_of(j), 1 - c), sibling).wait_recv()
        copy(0, part(me, c), sibling).wait_send()
        for j in range(3):
            copy(1 + j, part(me, c), (chips[j][0], chips[j][1], c)).wait_send()
            copy(4 + j, part(chip_of(j), c), sibling).wait_send()

    return send, pass_on, finish


def add_halves(name, g, recv, c_idx, tr):
    n, h, c = recv.shape

    def body(c_ref, g_ref, r_ref, o_ref):
        o_ref[...] = (g_ref[...] + r_ref[...]).astype(BF16)

    nb = h // tr
    return pl.pallas_call(
        body, name=name,
        grid_spec=pltpu.PrefetchScalarGridSpec(
            num_scalar_prefetch=1, grid=(n, nb),
            in_specs=[pl.BlockSpec((None, tr, c), lambda k, i, c_ref: (k, c_ref[0] * nb + i, 0)),
                      pl.BlockSpec((None, tr, c), lambda k, i, c_ref: (k, i, 0))],
            out_specs=pl.BlockSpec((None, tr, c), lambda k, i, c_ref: (k, i, 0))),
        out_shape=jax.ShapeDtypeStruct((n, h, c), BF16), compiler_params=_params("parallel", "parallel"),
    )(c_idx, g, recv)


def sum_chips(name, parts, recv, where, tr):
    n, h, c = recv.shape
    nb = h // tr

    def body(w_ref, p_ref, r_ref, o_ref):
        acc = p_ref[...].astype(F32)
        for k in range(n):
            acc = acc + r_ref[k].astype(F32)
        o_ref[...] = acc

    return pl.pallas_call(
        body, name=name,
        grid_spec=pltpu.PrefetchScalarGridSpec(
            num_scalar_prefetch=1, grid=(nb,),
            in_specs=[pl.BlockSpec((None, tr, c), lambda i, w_ref: (w_ref[0], i, 0)),
                      pl.BlockSpec((n, tr, c), lambda i, w_ref: (0, i, 0))],
            out_specs=pl.BlockSpec((tr, c), lambda i, w_ref: (w_ref[1] * nb + i, 0))),
        out_shape=jax.ShapeDtypeStruct((2 * h, c), F32), compiler_params=_params("parallel"),
    )(where, parts, recv)


def place_shard(name, shard, where, dtype, tr):
    r, c = shard.shape

    def body(w_ref, s_ref, o_ref):
        o_ref[...] = s_ref[...].astype(dtype)

    return pl.pallas_call(
        body, name=name,
        grid_spec=pltpu.PrefetchScalarGridSpec(
            num_scalar_prefetch=1, grid=(r // tr,),
            in_specs=[pl.BlockSpec((tr, c), lambda i, w_ref: (i, 0))],
            out_specs=pl.BlockSpec((None, tr, c), lambda i, w_ref: (w_ref[0], i, 0))),
        out_shape=jax.ShapeDtypeStruct((N_CHIPS, r, c), dtype), compiler_params=_params("parallel"),
    )(where, shard)


ANY = pl.BlockSpec(memory_space=pl.ANY)


def _place():
    x, y, c = lax.axis_index("x"), lax.axis_index("y"), lax.axis_index("c")
    chips = [(1 - x, y), (x, 1 - y), (1 - x, 1 - y)]
    return x, y, c, chips


def gather_shards(arrays):
    n = len(arrays)

    def body(*refs):
        send, pass_on, finish = _gather_phases(refs[n:2 * n], *refs[2 * n:], _spans(arrays))
        send()
        pass_on()
        finish()

    return pl.pallas_call(
        body, name="gather_shards", in_specs=[ANY] * n, out_specs=[ANY] * n,
        out_shape=[jax.ShapeDtypeStruct(a.shape, a.dtype) for a in _arrays(arrays)],
        input_output_aliases={w: w for w in range(n)}, scratch_shapes=_gather_sems(n),
        compiler_params=pltpu.CompilerParams(has_side_effects=True),
    )(*_arrays(arrays))


def _gather_sems(n):
    return [pltpu.SemaphoreType.DMA((6 * n,)), pltpu.SemaphoreType.DMA((6 * n,))]


class Span(typing.NamedTuple):
    array: jax.Array
    lo: int
    hi: int
    ways: tuple = (0, 1, 2)


def _arrays(gathering):
    return [g.array if isinstance(g, Span) else g for g in gathering]


def _spans(gathering):
    return [(g.lo, g.hi, g.ways) if isinstance(g, Span) else (0, g.shape[1], (0, 1, 2)) for g in gathering]


def _gather_phases(out, send_sems, recv_sems, spans):
    n = len(out)
    if not any(ways for _, _, ways in spans):
        return (lambda: None,) * 3
    x, y, c, chips = _place()
    me = 2 * x + y
    sibling = (x, y, 1 - c)

    def half(w, chip, core):
        lo, hi, _ = spans[w]
        h = (hi - lo) // 2
        return out[w].at[chip, pl.ds(lo + core * h, h)]

    def copy(k, block, to):
        return pltpu.make_async_remote_copy(src_ref=block, dst_ref=block, send_sem=send_sems.at[k],
                                            recv_sem=recv_sems.at[k], device_id=to, device_id_type=MESH)

    def over_ici(w, j, chip):
        return copy(3 * w + j, half(w, chip, c), (chips[j][0], chips[j][1], c))

    def over_d2d(w, j, core):
        return copy(3 * n + 3 * w + j, half(w, 2 * chips[j][0] + chips[j][1], core), sibling)

    pairs = [(w, j) for w in range(n) for j in spans[w][2]]

    def send():
        for w, j in pairs:
            over_ici(w, j, me).start()

    def pass_on():
        for w, j in pairs:
            over_ici(w, j, 2 * chips[j][0] + chips[j][1]).wait_recv()
            over_d2d(w, j, c).start()

    def finish():
        for w, j in pairs:
            over_d2d(w, j, 1 - c).wait_recv()
        for w, j in pairs:
            over_ici(w, j, me).wait_send()
            over_d2d(w, j, c).wait_send()

    return send, pass_on, finish


def _relay_sems():
    return [pltpu.SemaphoreType.DMA((4,)), pltpu.SemaphoreType.DMA((4,))]


def _relay_phases(out, send_sems, recv_sems):
    x, y, c, chips = _place()
    sibling = (x, y, 1 - c)
    rows = out.shape[1]
    quarter = rows // 4
    far = 2 * chips[2][0] + chips[2][1]

    def piece(chip, way, core):
        return out.at[chip, pl.ds(way * (rows // 2) + core * quarter, quarter)]

    def copy(k, block, to):
        return pltpu.make_async_remote_copy(src_ref=block, dst_ref=block, send_sem=send_sems.at[k],
                                            recv_sem=recv_sems.at[k], device_id=to, device_id_type=MESH)

    def over_ici(way, chip):
        return copy(way, piece(chip, way, c), (chips[way][0], chips[way][1], c))

    def over_d2d(way, core):
        return copy(2 + way, piece(far, way, core), sibling)

    def send():
        for way in range(2):
            other = chips[1 - way]
            over_ici(way, 2 * other[0] + other[1]).start()

    def pass_on():
        for way in range(2):
            over_ici(way, far).wait_recv()
            over_d2d(way, c).start()

    def finish():
        for way in range(2):
            over_d2d(way, 1 - c).wait_recv()
        for way in range(2):
            other = chips[1 - way]
            over_ici(way, 2 * other[0] + other[1]).wait_send()
            over_d2d(way, c).wait_send()

    return send, pass_on, finish


def swap_halves(name, grads):
    n = len(grads)

    def body(*refs):
        send, finish = _swap_phases(refs[:n], refs[n:2 * n], *refs[2 * n:])
        send()
        finish()

    return pl.pallas_call(
        body, name=name, in_specs=[ANY] * n, out_specs=[ANY] * n, out_shape=_swap_shapes(grads),
        scratch_shapes=_swap_sems(n), compiler_params=pltpu.CompilerParams(has_side_effects=True),
    )(*grads)


def _swap_shapes(grads):
    return [jax.ShapeDtypeStruct((a.shape[0], a.shape[1] // 2, a.shape[2]), a.dtype) for a in grads]


def _swap_sems(n):
    return [pltpu.SemaphoreType.DMA((n,)), pltpu.SemaphoreType.DMA((n,))]


def _swap_phases(g, out, send_sems, recv_sems):
    x, y, c, _ = _place()

    def copies():
        return [pltpu.make_async_remote_copy(
            src_ref=g[w].at[:, pl.ds((1 - c) * (g[w].shape[1] // 2), g[w].shape[1] // 2)], dst_ref=out[w],
            send_sem=send_sems.at[w], recv_sem=recv_sems.at[w], device_id=(x, y, 1 - c), device_id_type=MESH)
            for w in range(len(g))]

    def send():
        for cp in copies():
            cp.start()

    def finish():
        for cp in copies():
            cp.wait()

    return send, finish


def _send_phases(g, out, send_sems, recv_sems):
    x, y, c, _ = _place()

    def copies():
        return [pltpu.make_async_remote_copy(
            src_ref=g[w], dst_ref=out[w], send_sem=send_sems.at[w], recv_sem=recv_sems.at[w],
            device_id=(x, y, 1 - c), device_id_type=MESH) for w in range(len(g))]

    def send():
        for cp in copies():
            cp.start()

    def finish():
        for cp in copies():
            cp.wait()

    return send, finish


def dw_in_half(name, h, dz, which, sending):
    s = h.shape[0]
    hh, tb = D // 2, IN_SHARD // 2
    n = len(sending)
    steps = IN_COLS // tb

    def body(w_ref, *refs):
        a_ref, b_ref, o_ref = refs[0], refs[1], refs[2 + n]
        j = pl.program_id(0)
        if n:
            send, finish = _send_phases(refs[2:2 + n], refs[3 + n:3 + 2 * n], *refs[3 + 2 * n:])
            pl.when(j == 0)(send)
        o_ref[...] = _dot(a_ref[...], b_ref[...], TN)
        if n:
            pl.when(j == steps - 1)(finish)

    out = pl.pallas_call(
        body, name=name,
        grid_spec=pltpu.PrefetchScalarGridSpec(
            num_scalar_prefetch=1, grid=(steps,),
            in_specs=[pl.BlockSpec((s, hh), lambda j, w: (0, w[0])), pl.BlockSpec((s, tb), lambda j, w: (0, j))]
            + [ANY] * n,
            out_specs=[pl.BlockSpec((None, hh, tb), lambda j, w: (j // 2, 0, j % 2))] + [ANY] * n,
            scratch_shapes=_swap_sems(n) if n else []),
        out_shape=[jax.ShapeDtypeStruct((N_CHIPS, hh, IN_SHARD), F32)]
        + [jax.ShapeDtypeStruct(a.shape, a.dtype) for a in sending],
        compiler_params=_params("arbitrary", communicates=bool(n)),
    )(which, h, dz, *sending)
    return out[0], out[1:]


def scatter_chips(parts):
    n = len(parts)

    def body(*refs):
        send, finish = _scatter_phases(refs[:n], refs[n:2 * n], *refs[2 * n:])
        send()
        finish()

    return pl.pallas_call(
        body, name="scatter_chips", in_specs=[ANY] * n, out_specs=[ANY] * n,
        out_shape=_scatter_shapes(parts), scratch_shapes=_scatter_sems(n),
        compiler_params=pltpu.CompilerParams(has_side_effects=True),
    )(*parts)


def _scatter_shapes(parts):
    return [jax.ShapeDtypeStruct((3,) + a.shape[1:], a.dtype) for a in parts]


def _scatter_sems(n):
    return [pltpu.SemaphoreType.DMA((3 * n,)), pltpu.SemaphoreType.DMA((3 * n,))]


def _scatter_phases(p, out, send_sems, recv_sems):
    x, y, c, chips = _place()

    def copies():
        return [pltpu.make_async_remote_copy(
            src_ref=p[w].at[2 * px + py], dst_ref=out[w].at[j], send_sem=send_sems.at[3 * w + j],
            recv_sem=recv_sems.at[3 * w + j], device_id=(px, py, c), device_id_type=MESH)
            for w in range(len(p)) for j, (px, py) in enumerate(chips)]

    def send():
        for cp in copies():
            cp.start()

    def finish():
        for cp in copies():
            cp.wait()

    return send, finish


def _join_only(arrays):
    n = len(arrays)

    def body(*refs):
        out = refs[n:2 * n]
        send_sems, recv_sems = refs[2 * n:]
        x, y, c, _ = _place()

        def copy(w, core):
            h = out[w].shape[0] // 2
            rows = out[w].at[pl.ds(core * h, h)]
            return pltpu.make_async_remote_copy(
                src_ref=rows, dst_ref=rows, send_sem=send_sems.at[w], recv_sem=recv_sems.at[w],
                device_id=(x, y, 1 - c), device_id_type=MESH)

        for w in range(n):
            copy(w, c).start()
        for w in range(n):
            copy(w, 1 - c).wait_recv()
        for w in range(n):
            copy(w, c).wait_send()

    return pl.pallas_call(
        body, name="join_only", in_specs=[ANY] * n, out_specs=[ANY] * n,
        out_shape=[jax.ShapeDtypeStruct(a.shape, a.dtype) for a in arrays],
        input_output_aliases={w: w for w in range(n)},
        scratch_shapes=[pltpu.SemaphoreType.DMA((n,)), pltpu.SemaphoreType.DMA((n,))],
        compiler_params=pltpu.CompilerParams(has_side_effects=True),
    )(*arrays)


HBM = pl.BlockSpec(memory_space=pltpu.HBM)
SEM = pl.BlockSpec(memory_space=pltpu.SEMAPHORE)
DATAFLOW = pltpu.SideEffectType.DATAFLOW_SIDE_EFFECTING


def _scatter_copies(p_ref, land_ref, send_sems, recv_sems):
    x, y, c, chips = _place()
    return [pltpu.make_async_remote_copy(
        src_ref=p_ref.at[2 * px + py], dst_ref=land_ref.at[j], send_sem=send_sems[j], recv_sem=recv_sems[j],
        device_id=(px, py, c), device_id_type=MESH) for j, (px, py) in enumerate(chips)]


def scatter_start(p):
    land = jax.ShapeDtypeStruct((3,) + p.shape[1:], p.dtype)

    def body(p_ref, land_ref, *outs):
        for cp in _scatter_copies(p_ref, land_ref, outs[0:3], outs[3:6]):
            cp.start()
        outs[8][...] = jnp.zeros_like(outs[8])

    return pl.pallas_call(
        body, name="scatter_start",
        out_shape=(pltpu.SemaphoreType.DMA(()),) * 6
        + (pltpu.HBM(p.shape, p.dtype), pltpu.HBM(land.shape, land.dtype), jax.ShapeDtypeStruct((8, 128), F32)),
        in_specs=(HBM, HBM), out_specs=(SEM,) * 6 + (HBM, HBM, pl.BlockSpec(memory_space=pltpu.VMEM)),
        input_output_aliases={0: 6, 1: 7},
        compiler_params=pltpu.CompilerParams(has_side_effects=DATAFLOW),
    )(pltpu.with_memory_space_constraint(p, pltpu.HBM),
      pltpu.with_memory_space_constraint(lax.empty(land.shape, land.dtype), pltpu.HBM))


def scatter_wait(started, after):
    sems, p_thru, land_thru = started[0:6], started[6], started[7]

    def body(p_ref, land_ref, *refs):
        for cp in _scatter_copies(p_ref, land_ref, refs[0:3], refs[3:6]):
            cp.wait_send()
            cp.wait_recv()

    return pl.pallas_call(
        body, name="scatter_wait",
        out_shape=(pltpu.HBM(p_thru.shape, p_thru.dtype), pltpu.HBM(land_thru.shape, land_thru.dtype)),
        in_specs=(HBM, HBM) + (SEM,) * 6 + (pl.BlockSpec(memory_space=pl.ANY),), out_specs=(HBM, HBM),
        input_output_aliases={0: 0, 1: 1},
        compiler_params=pltpu.CompilerParams(has_side_effects=DATAFLOW),
    )(p_thru, land_thru, *sems, after)


def join_halves(arrays, gathering=None):
    n = len(arrays)
    if gathering is None:
        return _join_only(arrays), None

    def body(*refs):
        out = refs[n + 1:2 * n + 1]
        send_sems, recv_sems = refs[2 * n + 2:2 * n + 4]
        send8, pass_on8, finish8 = _allgather8_phases(refs[2 * n + 1], *refs[2 * n + 4:])
        x, y, c, _ = _place()

        def copy(w, core):
            h = out[w].shape[0] // 2
            rows = out[w].at[pl.ds(core * h, h)]
            return pltpu.make_async_remote_copy(
                src_ref=rows, dst_ref=rows, send_sem=send_sems.at[w], recv_sem=recv_sems.at[w],
                device_id=(x, y, 1 - c), device_id_type=MESH)

        send8()
        for w in range(n):
            copy(w, c).start()
        pass_on8()
        for w in range(n):
            copy(w, 1 - c).wait_recv()
        finish8()
        for w in range(n):
            copy(w, c).wait_send()

    res = pl.pallas_call(
        body, name="join_halves", in_specs=[ANY] * (n + 1), out_specs=[ANY] * (n + 1),
        out_shape=[jax.ShapeDtypeStruct(a.shape, a.dtype) for a in list(arrays) + [gathering]],
        input_output_aliases={w: w for w in range(n + 1)},
        scratch_shapes=[pltpu.SemaphoreType.DMA((n,)), pltpu.SemaphoreType.DMA((n,)),
                        pltpu.SemaphoreType.DMA((7,)), pltpu.SemaphoreType.DMA((7,))],
        compiler_params=pltpu.CompilerParams(has_side_effects=True),
    )(*arrays, gathering)
    return res[:n], res[n]


def allreduce_small(packed):
    r, c = packed.shape
    n_dev = 8

    def body(x_ref, all_ref, sum_ref, send_sems, recv_sems, local_sem):
        x, y, cc, chips = _place()
        me, sibling = (x, y, cc), (x, y, 1 - cc)

        def rows(px, py, pc):
            return all_ref.at[4 * px + 2 * py + pc]

        def copy(k, block, to, src=None):
            return pltpu.make_async_remote_copy(
                src_ref=rows(*block) if src is None else src, dst_ref=rows(*block), send_sem=send_sems.at[k],
                recv_sem=recv_sems.at[k], device_id=to, device_id_type=MESH)

        mine = pltpu.make_async_copy(x_ref, rows(*me), local_sem)
        mine.start()
        first = [copy(0, me, sibling, src=x_ref)]
        first += [copy(1 + j, me, (*chip, cc), src=x_ref) for j, chip in enumerate(chips)]
        for cp in first:
            cp.start()
        passed = [copy(4 + j, (*chip, cc), sibling) for j, chip in enumerate(chips)]
        for j, chip in enumerate(chips):
            copy(1 + j, (*chip, cc), me).wait_recv()
            passed[j].start()
        copy(0, sibling, me).wait_recv()
        for j, chip in enumerate(chips):
            copy(4 + j, (*chip, 1 - cc), me).wait_recv()
        for cp in first + passed:
            cp.wait_send()
        mine.wait()
        acc = all_ref[0]
        for k in range(1, n_dev):
            acc = acc + all_ref[k]
        sum_ref[...] = acc

    vm = pl.BlockSpec(memory_space=pltpu.VMEM)
    return pl.pallas_call(
        body, name="allreduce_small", in_specs=[vm], out_specs=[vm, vm],
        out_shape=[jax.ShapeDtypeStruct((n_dev, r, c), F32), jax.ShapeDtypeStruct((r, c), F32)],
        scratch_shapes=[pltpu.SemaphoreType.DMA((7,)), pltpu.SemaphoreType.DMA((7,)), pltpu.SemaphoreType.DMA],
        compiler_params=pltpu.CompilerParams(has_side_effects=True, vmem_limit_bytes=VMEM_LIMIT),
    )(packed)[1]


def local_step(x, target, vecs, w_s, bs_t, bg, wg_in, late, core=None, order=None, where=None):
    on_mesh = core is not None

    def add(names, grads, recv):
        return [add_halves("add_" + n, g, r, core, min(r.shape[1], 256)) for n, g, r in zip(names, grads, recv)]

    g_pre, ln_g, ln_b, g_post, g_fpre, g_fpost = vecs
    s = x.shape[0]
    if order is None:
        order = jnp.arange(N_CHIPS, dtype=jnp.int32)
    logc = _attn_tables(s)
    ka, kb = _alibi_tables(s)

    h = norm_pre(x, g_pre)
    if not on_mesh:
        wg_a, wg_b, wg_out, wg_ff1, wg_ff2 = late
    if on_mesh:
        cut = D // 4
        z, wg_in, (wg_a, wg_b, wg_out, wg_ff1, wg_ff2) = mm_in(h, wg_in, order, True, late)
        ya, (wg_b, wg_ff2) = gating_fwd(z, ln_g, ln_b, w_s, bs_t, [wg_b, Span(wg_ff2, 0, cut)])
        yb, lse, (wg_a, wg_ff1, wg_ff2, bg) = attn_fwd(
            z, logc, ka, kb, [wg_a, wg_ff1, Span(wg_ff2, cut, 3 * cut), bg])
        bg = jnp.transpose(bg[:, :2, :], (1, 0, 2)).reshape(2, D)
    else:
        z, _, _ = mm_in(h, wg_in, order, False)
        ya, _ = gating_fwd(z, ln_g, ln_b, w_s, bs_t, [])
        yb, lse, _ = attn_fwd(z, logc, ka, kb, [])
    merged, pa, pb, got = proj_merge(ya, yb, wg_a.reshape(D, D), wg_b.reshape(D, D), z, bg, [wg_out] if on_mesh else [])
    wg_out = got[0] if on_mesh else wg_out
    w_out = wg_out.reshape(D, D)
    o, x1, h2, got = out_norm(merged, w_out, x, g_post, g_fpre, [Span(wg_ff2, 3 * D // 4, D)] if on_mesh else [])
    a, rl, _ = mm_ff1(h2, wg_ff1, [])
    w_ff2 = (got[0] if on_mesh else wg_ff2).reshape(D_FF, D)
    dy, df, d_gfpost, loss = ff2_loss(rl, w_ff2, x1, target, g_fpost)

    half_cols = pl.BlockSpec((D, D // 2), lambda i, j: (0, j))
    d_wff2 = mm_tn("dw_ff2", rl, df, D // 2, D, (D_FF, D), pl.BlockSpec((D // 2, D), lambda i, j: (i, 0)))
    da = ff2_bwd(df, w_ff2, a)
    d_wff1 = mm_tn("dw_ff1", h2, da, D, D // 2, (N_CHIPS, D, D),
                   pl.BlockSpec((None, D, D // 2), lambda i, j: (j // 2, 0, j % 2)))
    d_ff = [d_wff1, d_wff2.reshape(N_CHIPS, D, D)]
    dx1, do, d_gfpre, d_gpost, recv_ff = ff1_bwd_norms(da, wg_ff1, x1, o, dy, g_fpre, g_post, d_ff if on_mesh else [])
    d_wout = mm_tn("dw_out", merged, do, D, D // 2, (D, D), half_cols)
    dpa, dpb, dga, dgb, d_bg = out_bwd_gates(do, w_out, pa, pb, z, bg)
    d_wa = mm_tn("dw_a", ya, dpa, D, D // 2, (D, D), half_cols)
    d_wb = mm_tn("dw_b", yb, dpb, D, D // 2, (D, D), half_cols)
    dya = mm_nt("dy_a", dpa, wg_a.reshape(D, D))
    dyb = mm_nt("dy_b", dpb, wg_b.reshape(D, D))
    d_proj = [d_wa.reshape(N_CHIPS, D // N_CHIPS, D), d_wb.reshape(N_CHIPS, D // N_CHIPS, D),
              d_wout.reshape(N_CHIPS, D // N_CHIPS, D)]
    du, dv, d_ws, d_bs, d_lng, d_lnb, recv_proj = gating_bwd(z, dya, ln_g, ln_b, w_s, bs_t, d_proj if on_mesh else [])
    early = d_proj + d_ff
    parts_early = add(BIG[1:], early, list(recv_proj) + list(recv_ff)) if on_mesh else []
    small = dict(b_gate=d_bg, ln_v_g=d_lng, ln_v_b=d_lnb, w_s=d_ws, b_s=d_bs[:, 0, :],
                 norm_mix_post=d_gpost, norm_ffn_pre=d_gfpre, norm_ffn_post=d_gfpost)
    packed = pack_small(dict(small, norm_mix_pre=jnp.zeros((1, D), F32)), loss, where) if on_mesh else None
    dq, dk, dvb, got_early, packed = attn_bwd(z, yb, dyb, lse, logc, ka, kb, parts_early, packed)
    dz = jnp.concatenate([du, dv, dq, dk, dvb, dga, dgb], axis=1)
    if on_mesh:
        for_sibling, _ = dw_in_half("dw_in_sibling", h, dz, 1 - core, [])
        mine, from_sibling = dw_in_half("dw_in_mine", h, dz, core, [for_sibling])
        d_win = None
        parts_late = [add_halves("add_w_in", mine, from_sibling[0], jnp.zeros((1,), jnp.int32), 256)]
    else:
        half = IN_SHARD // 2
        d_win = mm_tn("dw_in", h, dz, D, half, (N_CHIPS, D, IN_SHARD),
                      pl.BlockSpec((None, D, half), lambda i, j: (j // 2, 0, j % 2)))
        parts_late = []
    started = scatter_start(parts_late[0]) if on_mesh else None
    dx, d_gpre, _, _ = in_bwd_norm(dz, wg_in, x, dx1, g_pre + started[8][0:1, 0:1] if on_mesh else g_pre, [])
    small["norm_mix_pre"] = d_gpre
    return loss[0, 0], dx, [d_win] + early, small, parts_early, list(got_early), packed, started


BIG = ("w_in", "w_a_proj", "w_b_proj", "w_out", "w_ff1", "w_ff2")
SMALL = ("norm_mix_pre", "ln_v_g", "ln_v_b", "b_s", "norm_mix_post", "norm_ffn_pre", "norm_ffn_post", "w_s", "b_gate")
ORDER = ("norm_mix_pre", "w_in", "b_gate", "ln_v_g", "ln_v_b", "w_s", "b_s", "w_a_proj", "w_b_proj", "w_out",
         "norm_mix_post", "norm_ffn_pre", "w_ff1", "w_ff2", "norm_ffn_post")
VEC_ROWS = D // 128
WS_ROW = 7 * VEC_ROWS
BG_ROW = WS_ROW + GROUPS * CHUNK
LOSS_ROW = BG_ROW + 2 * VEC_ROWS
PACK_ROWS = LOSS_ROW + 8


def pack_small(small, loss, where):
    vectors = [small[n] for n in SMALL[:7]]
    operands = vectors + [small["w_s"], small["b_gate"], loss]

    def body(where_ref, *refs):
        out = refs[-1]
        ws_ref, bg_ref, loss_ref = refs[7:10]
        for i, n in enumerate(SMALL[:7]):
            if n == "b_s":
                out[i * VEC_ROWS:(i + 1) * VEC_ROWS, :] = refs[i][...]
            else:
                for j in range(VEC_ROWS):
                    out[i * VEC_ROWS + j:i * VEC_ROWS + j + 1, :] = refs[i][:, j * 128:(j + 1) * 128]
        for g in range(GROUPS):
            out[WS_ROW + g * CHUNK:WS_ROW + (g + 1) * CHUNK, :] = ws_ref[g]
        for r in range(2):
            for j in range(VEC_ROWS):
                row = BG_ROW + r * VEC_ROWS + j
                out[row:row + 1, :] = bg_ref[r:r + 1, j * 128:(j + 1) * 128]
        lane = lax.broadcasted_iota(jnp.int32, (8, 128), 1)
        sub = lax.broadcasted_iota(jnp.int32, (8, 128), 0)
        out[LOSS_ROW:LOSS_ROW + 8, :] = jnp.where((lane == 0) & (sub == 0), loss_ref[...], 0.0)

    return pl.pallas_call(
        body, name="pack_small",
        grid_spec=pltpu.PrefetchScalarGridSpec(
            num_scalar_prefetch=1, grid=(1,), in_specs=[_full(a.shape) for a in operands],
            out_specs=pl.BlockSpec((None, PACK_ROWS, 128), lambda i, w: (w[0], w[1], 0))),
        out_shape=jax.ShapeDtypeStruct((N_CHIPS, 2 * PACK_ROWS, 128), F32), compiler_params=_params("arbitrary"),
    )(where, *operands)


def pack_vector(vec, where):
    def body(where_ref, v_ref, out):
        for j in range(VEC_ROWS):
            out[j:j + 1, :] = v_ref[:, j * 128:(j + 1) * 128]

    return pl.pallas_call(
        body, name="pack_vector",
        grid_spec=pltpu.PrefetchScalarGridSpec(
            num_scalar_prefetch=1, grid=(1,), in_specs=[_full(vec.shape)],
            out_specs=pl.BlockSpec((None, VEC_ROWS, 128), lambda i, w: (w[0], w[1], 0))),
        out_shape=jax.ShapeDtypeStruct((N_CHIPS, 2 * VEC_ROWS, 128), F32), compiler_params=_params("arbitrary"),
    )(where, vec)


def adamw_small(gathered, first, chip, w, m, v):
    shapes = {n: (1, D) for n in SMALL}
    shapes.update(b_s=(GROUPS, CHUNK), w_s=(GROUPS * CHUNK, CHUNK), b_gate=(2, D // N_CHIPS))
    flat = lambda t: [t[n].reshape(shapes[n]) for n in SMALL]
    per = D // N_CHIPS // 128

    def body(chip_ref, all_ref, first_ref, *refs):
        params, outs = refs[:27], refs[27:]
        sub = lax.broadcasted_iota(jnp.int32, (VEC_ROWS, 128), 0)
        sum_ref = outs[36]
        total = all_ref[0, 0:PACK_ROWS, :]
        head = first_ref[0, 0:VEC_ROWS, :]
        for k in range(1, 2 * N_CHIPS):
            total = total + all_ref[k // 2, (k % 2) * PACK_ROWS:(k % 2 + 1) * PACK_ROWS, :]
            head = head + first_ref[k // 2, (k % 2) * VEC_ROWS:(k % 2 + 1) * VEC_ROWS, :]
        sum_ref[...] = total
        sum_ref[0:VEC_ROWS, :] = head

        def gate_row(r):
            rows = sum_ref[BG_ROW + r * VEC_ROWS:BG_ROW + (r + 1) * VEC_ROWS, :]
            return jnp.concatenate([jnp.sum(jnp.where(sub == per * chip_ref[0] + j, rows, 0.0), axis=0, keepdims=True)
                                    for j in range(per)], axis=1)

        for i, n in enumerate(SMALL):
            if n == "b_s":
                g = sum_ref[i * VEC_ROWS:(i + 1) * VEC_ROWS, :]
            elif n == "w_s":
                g = sum_ref[WS_ROW:BG_ROW, :]
            elif n == "b_gate":
                g = jnp.concatenate([gate_row(0), gate_row(1)], axis=0)
            else:
                g = jnp.concatenate([sum_ref[i * VEC_ROWS + j:i * VEC_ROWS + j + 1, :] for j in range(VEC_ROWS)],
                                    axis=1)
            delta, nm, nv = _adamw_math(params[i][...], g, params[9 + i][...], params[18 + i][...])
            outs[4 * i][...], outs[4 * i + 1][...], outs[4 * i + 2][...], outs[4 * i + 3][...] = g, delta, nm, nv

    vm = pl.BlockSpec(memory_space=pltpu.VMEM)
    res = pl.pallas_call(
        body, name="adamw_small",
        in_specs=[pl.BlockSpec(memory_space=pltpu.SMEM)] + [vm] * 29, out_specs=[vm] * 37,
        out_shape=[jax.ShapeDtypeStruct(shapes[n], F32) for n in SMALL for _ in range(4)]
        + [jax.ShapeDtypeStruct((PACK_ROWS, 128), F32)],
        compiler_params=_params(),
    )(chip, gathered, first, *flat(w), *flat(m), *flat(v))
    new = {n: tuple(r.reshape(w[n].shape) for r in res[4 * i:4 * i + 4]) for i, n in enumerate(SMALL)}
    return new, res[36][LOSS_ROW, 0]


def kernel(x, norm_mix_pre, w_in, b_gate, ln_v_g, ln_v_b, w_s, b_s, w_a_proj, w_b_proj, w_out, norm_mix_post, norm_ffn_pre, w_ff1, w_ff2, norm_ffn_post, loss_target, m_norm_mix_pre, m_w_in, m_b_gate, m_ln_v_g, m_ln_v_b, m_w_s, m_b_s, m_w_a_proj, m_w_b_proj, m_w_out, m_norm_mix_post, m_norm_ffn_pre, m_w_ff1, m_w_ff2, m_norm_ffn_post, v_norm_mix_pre, v_w_in, v_b_gate, v_ln_v_g, v_ln_v_b, v_w_s, v_b_s, v_w_a_proj, v_w_b_proj, v_w_out, v_norm_mix_post, v_norm_ffn_pre, v_w_ff1, v_w_ff2, v_norm_ffn_post):
    w = dict(norm_mix_pre=norm_mix_pre, w_in=w_in, b_gate=b_gate, ln_v_g=ln_v_g, ln_v_b=ln_v_b, w_s=w_s, b_s=b_s,
             w_a_proj=w_a_proj, w_b_proj=w_b_proj, w_out=w_out, norm_mix_post=norm_mix_post,
             norm_ffn_pre=norm_ffn_pre, w_ff1=w_ff1, w_ff2=w_ff2, norm_ffn_post=norm_ffn_post)
    m = dict(norm_mix_pre=m_norm_mix_pre, w_in=m_w_in, b_gate=m_b_gate, ln_v_g=m_ln_v_g, ln_v_b=m_ln_v_b, w_s=m_w_s,
             b_s=m_b_s, w_a_proj=m_w_a_proj, w_b_proj=m_w_b_proj, w_out=m_w_out, norm_mix_post=m_norm_mix_post,
             norm_ffn_pre=m_norm_ffn_pre, w_ff1=m_w_ff1, w_ff2=m_w_ff2, norm_ffn_post=m_norm_ffn_post)
    v = dict(norm_mix_pre=v_norm_mix_pre, w_in=v_w_in, b_gate=v_b_gate, ln_v_g=v_ln_v_g, ln_v_b=v_ln_v_b, w_s=v_w_s,
             b_s=v_b_s, w_a_proj=v_w_a_proj, w_b_proj=v_w_b_proj, w_out=v_w_out, norm_mix_post=v_norm_mix_post,
             norm_ffn_pre=v_norm_ffn_pre, w_ff1=v_w_ff1, w_ff2=v_w_ff2, norm_ffn_post=v_norm_ffn_post)
    chip = 2 * lax.axis_index("x") + lax.axis_index("y")
    core = lax.axis_index("c")

    where = jnp.stack([chip, core]).astype(jnp.int32)
    wg_in = place_shard("place_w_in", w_in[0], where, BF16, 256)
    bg_all = place_shard("place_b_gate", jnp.pad(b_gate[0], ((0, 14), (0, 0))), where, F32, 16)
    vecs = (norm_mix_pre, ln_v_g, ln_v_b, norm_mix_post, norm_ffn_pre, norm_ffn_post)
    loss, dx, _, small, parts, got, packed, started = local_step(
        x[0], loss_target[0], vecs, w_s[0], b_s[0].T, bg_all, wg_in, [w[n][0] for n in BIG[1:]],
        core=jnp.reshape(core, (1,)).astype(jnp.int32),
        order=jnp.stack([chip, chip ^ 2, chip ^ 1, chip ^ 3]).astype(jnp.int32), where=where)

    halves = [sum_chips("sum_" + n, p, r, where, min(p.shape[1], 256)) for n, p, r in zip(BIG[1:], parts, got)]
    joined, _ = join_halves(halves)
    grads = dict(zip(BIG[1:], joined))
    new = {}

    def update(n):
        shape = w[n].shape
        res = adamw("adamw_" + n, w[n][0], grads[n], m[n][0], v[n][0], min(shape[1], 256))
        new[n] = tuple(r.reshape(shape) for r in res)

    for n in BIG[1:]:
        update(n)
    p_in, got_in = scatter_wait(started, new["w_ff2"][1])
    (grads["w_in"],), first = join_halves([sum_chips("sum_w_in", p_in, got_in, where, 256)],
                                          pack_vector(small["norm_mix_pre"], where))
    update("w_in")
    small_new, loss = adamw_small(packed, first, jnp.reshape(chip, (1,)).astype(jnp.int32), w, m, v)
    new.update(small_new)

    outs = [loss, dx[None]]
    for i in range(4):
        outs += [new[n][i] for n in ORDER]
    return tuple(outs)
```

```python
import functools
import math
import typing

import numpy as np
import jax
import jax.numpy as jnp
from jax import lax
from jax.experimental import pallas as pl
from jax.experimental.pallas import tpu as pltpu

F32 = jnp.float32
BF16 = jnp.bfloat16
MESH = pl.DeviceIdType.MESH

D = 1024
EPS = 1e-6
CHUNK = 128
GROUPS = 8
HEADS = 16
HEAD_DIM = 64
ATT_T = 256
ATT_GROUP = 8
ATT_BWD_GROUP = 4
N_CHIPS = 4
D_FF = 4 * D
IN_COLS = 7 * D
IN_SHARD = IN_COLS // N_CHIPS
MASKED = -1e30
VMEM_LIMIT = 56 * 2 ** 20

ADAM_LR, ADAM_B1, ADAM_B2, ADAM_EPS, ADAM_WD, ADAM_STEP = 0.001, 0.9, 0.999, 1e-08, 0.01, 10

NN = (((1,), (0,)), ((), ()))
NT = (((1,), (1,)), ((), ()))
TN = (((0,), (0,)), ((), ()))


def _dot(a, b, dims=NN):
    return lax.dot_general(a, b, dims, preferred_element_type=F32)


def _params(*sem, communicates=False):
    return pltpu.CompilerParams(dimension_semantics=sem or None, vmem_limit_bytes=VMEM_LIMIT,
                                has_side_effects=communicates)


def _rows(tr, c, col=0):
    return pl.BlockSpec((tr, c), lambda i: (i, col))


def _full(shape):
    n = len(shape)
    return pl.BlockSpec(shape, lambda *_: (0,) * n)


def _gelu(x):
    k = math.sqrt(2.0 / math.pi)
    return 0.5 * x * (1.0 + jnp.tanh(k * (x + 0.044715 * x * x * x)))


def _gelu_and_grad(x):
    k = math.sqrt(2.0 / math.pi)
    t = jnp.tanh(k * (x + 0.044715 * x * x * x))
    g = 0.5 * x * (1.0 + t)
    dg = 0.5 * (1.0 + t) + 0.5 * x * (1.0 - t * t) * (k * (1.0 + 3.0 * 0.044715 * x * x))
    return g, dg


def _sigmoid(x):
    return 1.0 / (1.0 + jnp.exp(-x))


def _rms(x):
    r = lax.rsqrt(jnp.mean(x * x, axis=-1, keepdims=True) + EPS)
    return x * r, r


def _rms_bwd(dn, xhat, r):
    return r * (dn - xhat * jnp.mean(dn * xhat, axis=-1, keepdims=True))


def norm_pre(x, g):
    s = x.shape[0]
    tr = 512

    def body(x_ref, g_ref, h_ref):
        xhat, _ = _rms(x_ref[...])
        h_ref[...] = (xhat * g_ref[...]).astype(BF16)

    return pl.pallas_call(
        body, name="norm_pre", grid=(s // tr,),
        in_specs=[_rows(tr, D), _full((1, D))], out_specs=_rows(tr, D),
        out_shape=jax.ShapeDtypeStruct((s, D), BF16), compiler_params=_params("parallel"),
    )(x, g)


def mm_in(h, wg, order, staged, casting=()):
    s = h.shape[0]
    tm, tn = 1024, IN_SHARD // 2
    per = IN_SHARD // tn
    m = len(casting)
    nj, ni = N_CHIPS * per, s // tm
    cast_steps = per * ni

    def body(order_ref, *refs):
        a_ref = refs[0]
        cast_in = refs[2:2 + m]
        o_ref, held = refs[2 + m], refs[3 + m]
        cast_out = refs[4 + m:4 + 2 * m]
        tile, tile_sem = refs[4 + 2 * m:6 + 2 * m]
        sems = refs[6 + 2 * m:]
        j, i = pl.program_id(0), pl.program_id(1)

        @pl.when(j * ni + i < cast_steps)
        def _():
            for src, dst in zip(cast_in, cast_out):
                dst[...] = src[...].astype(BF16)

        def fetch(t):
            chip = order_ref[t // per]
            return pltpu.make_async_copy(held.at[chip, :, pl.ds((t % per) * tn, tn)], tile.at[t % 2],
                                         tile_sem.at[t % 2])

        if staged:
            near = _gather_phases([held], *sems[:2], [(0, D, (0, 1))])
            far = _relay_phases(held, *sems[2:])

        @pl.when(i == 0)
        def _():
            @pl.when(j == 0)
            def _():
                if staged:
                    near[0]()
                fetch(0).start()

            fetch(j).wait()
            ahead = j + 1 < nj
            if staged:
                ahead = ahead & (j + 1 != per) & (j + 1 != 3 * per)
            pl.when(ahead)(lambda: fetch(j + 1).start())

        rows = pl.ds(pl.multiple_of(i * tm, tm), tm)
        o_ref[...] = _dot(a_ref[rows, :], tile[j % 2]).astype(BF16)

        if staged:
            @pl.when((i == ni - 1) & (j == per - 1))
            def _():
                near[1]()
                near[2]()
                far[0]()
                fetch(per).start()

            @pl.when((i == ni - 1) & (j == 3 * per - 1))
            def _():
                far[1]()
                far[2]()
                fetch(3 * per).start()

    def cast_block(j, i, o):
        return jnp.minimum(j * ni + i, cast_steps - 1)

    out = pl.pallas_call(
        body, name="mm_in",
        grid_spec=pltpu.PrefetchScalarGridSpec(
            num_scalar_prefetch=1, grid=(nj, ni),
            in_specs=[pl.BlockSpec((s, D), lambda j, i, o: (0, 0)), ANY]
            + [pl.BlockSpec((a.shape[0] // cast_steps, a.shape[1]), lambda j, i, o: (cast_block(j, i, o), 0))
               for a in casting],
            out_specs=[pl.BlockSpec((tm, tn), lambda j, i, o: (i, o[j // per] * per + j % per)), ANY]
            + [pl.BlockSpec((None, a.shape[0] // cast_steps, a.shape[1]),
                            lambda j, i, o: (o[0], cast_block(j, i, o), 0)) for a in casting],
            scratch_shapes=[pltpu.VMEM((2, D, tn), BF16), pltpu.SemaphoreType.DMA((2,))]
            + (_gather_sems(1) + _relay_sems() if staged else [])),
        out_shape=[jax.ShapeDtypeStruct((s, IN_COLS), BF16), jax.ShapeDtypeStruct(wg.shape, wg.dtype)]
        + [jax.ShapeDtypeStruct((N_CHIPS,) + a.shape, BF16) for a in casting],
        input_output_aliases={2: 1},
        compiler_params=_params("arbitrary", "arbitrary", communicates=staged),
    )(order, h, wg, *casting)
    return out[0], out[1], out[2:]


def _tril_ws(ws_ref, g):
    r = lax.broadcasted_iota(jnp.int32, (CHUNK, CHUNK), 0)
    c = lax.broadcasted_iota(jnp.int32, (CHUNK, CHUNK), 1)
    return jnp.where(c <= r, ws_ref[g], 0.0).astype(BF16)


def _layer_norm(v):
    mu = jnp.mean(v, axis=-1, keepdims=True)
    d = v - mu
    rstd = lax.rsqrt(jnp.mean(d * d, axis=-1, keepdims=True) + EPS)
    return d * rstd, rstd


def gating_fwd(z, ln_g, ln_b, w_s, bs_t, gathering):
    s = z.shape[0]
    n = len(gathering)
    steps = s // CHUNK

    def body(*refs):
        u_ref, v_ref, lg_ref, lb_ref, ws_ref, bst_ref = refs[:6]
        ya_ref = refs[6 + n]
        ci = pl.program_id(0)
        if n:
            send, pass_on, finish = _gather_phases(refs[7 + n:7 + 2 * n], *refs[7 + 2 * n:], _spans(gathering))
            pl.when(ci == 0)(send)
        ug = _gelu(u_ref[...].astype(F32))
        vhat, _ = _layer_norm(_gelu(v_ref[...].astype(F32)))
        vn = (vhat * lg_ref[...] + lb_ref[...]).astype(BF16)
        for g in range(GROUPS):
            cols = slice(g * CHUNK, (g + 1) * CHUNK)
            mixed = _dot(_tril_ws(ws_ref, g), vn[:, cols]) + bst_ref[:, g:g + 1]
            ya_ref[:, cols] = (ug[:, cols] * mixed).astype(BF16)
        if n:
            pl.when(ci == steps - 1)(pass_on)
            pl.when(ci == steps - 1)(finish)

    out = pl.pallas_call(
        body, name="gating_fwd", grid=(steps,),
        in_specs=[_rows(CHUNK, D, 0), _rows(CHUNK, D, 1), _full((1, D)), _full((1, D)),
                  _full((GROUPS, CHUNK, CHUNK)), _full((CHUNK, GROUPS))] + [ANY] * n,
        out_specs=[_rows(CHUNK, D)] + [ANY] * n,
        out_shape=[jax.ShapeDtypeStruct((s, D), BF16)]
        + [jax.ShapeDtypeStruct(a.shape, a.dtype) for a in _arrays(gathering)],
        input_output_aliases={6 + w: 1 + w for w in range(n)},
        scratch_shapes=_gather_sems(n) if n else [],
        compiler_params=_params("arbitrary", communicates=bool(n)),
    )(z, z, ln_g, ln_b, w_s, bs_t, *_arrays(gathering))
    return out[0], out[1:]


def _attn_tables(s):
    nd = s // ATT_T
    r = np.arange(ATT_T)[None, :, None]
    c = np.arange(ATT_T)[None, None, :]
    delta = np.arange(nd)[:, None, None] * ATT_T + r - c
    count = np.zeros(delta.shape, np.int64)
    for window, dilation in ((128, 1), (512, 4), (2048, 16)):
        count += (delta >= 0) & (delta % dilation == 0) & (delta <= window)
    logc = np.where(count > 0, np.log(np.maximum(count, 1)), MASKED)
    return jnp.asarray(logc, F32)


AUG = 3


def _split3_np(x):
    terms, rest = [], np.asarray(x, np.float64)
    for _ in range(AUG):
        term = np.asarray(rest.astype(jnp.bfloat16), np.float64)
        terms.append(term)
        rest = rest - term
    return terms


def _split3(x):
    terms, rest = [], x
    for _ in range(AUG):
        term = rest.astype(BF16).astype(F32)
        terms.append(term)
        rest = rest - term
    return terms


def _alibi_tables(s):
    nb = s // ATT_T
    slopes = np.exp2(-8.0 * np.arange(1, HEADS + 1, dtype=np.float64) / HEADS)
    ka = np.zeros((HEADS // 2, 2, ATT_T, 128), np.float32)
    kb = np.zeros((HEADS // 2, 2, nb, 128), np.float32)
    for p in range(HEADS // 2):
        for e in range(2):
            base = HEAD_DIM * (1 - e)
            for a, term in enumerate(_split3_np(slopes[2 * p + e] * np.arange(ATT_T))):
                ka[p, e, :, base + a] = term
            for a, term in enumerate(_split3_np(slopes[2 * p + e] * ATT_T * np.arange(nb))):
                kb[p, e, :, base + AUG + a] = term
            ka[p, e, :, base + 2 * AUG:base + 3 * AUG] = 1.0
    return jnp.asarray(ka), jnp.asarray(kb)


def _head_masks():
    lane = lax.broadcasted_iota(jnp.int32, (1, 128), 1)
    first = lane < HEAD_DIM

    def ones(e, n):
        base = HEAD_DIM * (1 - e)
        return ((lane >= base) & (lane < base + n)).astype(F32)

    return first, lane, ones


def _place3(lane, at, terms, other):
    for a, term in enumerate(terms):
        other = jnp.where(lane == at + a, term, other)
    return other


def attn_fwd(z, logc, ka, kb, gathering):
    s = z.shape[0]
    nq = s // ATT_T
    t = ATT_T
    n = len(gathering)
    grp = ATT_GROUP
    ngrp = HEADS // 2 // grp
    wide = 128 * grp
    qcol, kcol, vcol = 2 * D // wide, 3 * D // wide, 4 * D // wide

    def body(*refs):
        q_ref, k_ref, v_ref, lc_ref, ka_ref, kb_ref = refs[:6]
        y_ref, lse_ref = refs[6 + n:8 + n]
        q_s, k_s, v_s, m_s, l_s, acc_s = refs[8 + 2 * n:14 + 2 * n]
        gi, qi = pl.program_id(0), pl.program_id(1)
        first, lane, ones = _head_masks()
        if n:
            send, pass_on, finish = _gather_phases(refs[8 + n:8 + 2 * n], *refs[14 + 2 * n:], _spans(gathering))
            pl.when((gi == 0) & (qi == 0))(send)

        @pl.when(qi == 0)
        def _():
            sel = jnp.broadcast_to(first.astype(F32), (t, 128))
            for pr in range(grp):
                cols = slice(pr * 128, (pr + 1) * 128)
                for jb in range(nq):
                    kj = k_ref[jb * t:(jb + 1) * t, cols].astype(F32)
                    vj = v_ref[jb * t:(jb + 1) * t, cols].astype(F32)
                    k_s[pr, 0, jb] = jnp.where(first, kj, ka_ref[pr, 0] + kb_ref[pr, 0, jb:jb + 1, :]).astype(BF16)
                    k_s[pr, 1, jb] = jnp.where(first, ka_ref[pr, 1] + kb_ref[pr, 1, jb:jb + 1, :], kj).astype(BF16)
                    v_s[pr, jb, 0:t, 0:128] = jnp.where(first, vj, 0.0).astype(BF16)
                    v_s[pr, jb, t:2 * t, 0:128] = jnp.where(first, 0.0, vj).astype(BF16)
                    v_s[pr, jb, 0:t, 128:256] = sel.astype(BF16)
                    v_s[pr, jb, t:2 * t, 128:256] = (1.0 - sel).astype(BF16)

        for pr in range(grp):
            q = q_ref[:, pr * 128:(pr + 1) * 128].astype(F32) * (1.0 / math.sqrt(HEAD_DIM))
            q_s[pr, 0] = jnp.where(first, q, ones(0, 2 * AUG)).astype(BF16)
            q_s[pr, 1] = jnp.where(first, ones(1, 2 * AUG), q).astype(BF16)
        m_s[...] = jnp.full_like(m_s, MASKED)
        l_s[...] = jnp.zeros_like(l_s)
        acc_s[...] = jnp.zeros_like(acc_s)

        def scores(j):
            return tuple(_dot(q_s[pr, e], k_s[pr, e, j], NT) for pr in range(grp) for e in range(2))

        def step(j, carry):
            softmax_block(j, scores(j))
            return carry

        def softmax_block(j, u):
            lc = lc_ref[qi - j]
            for pr in range(grp):
                u0 = u[2 * pr] + lc
                u1 = u[2 * pr + 1] + lc
                m0, m1 = m_s[pr, 0], m_s[pr, 1]
                n0 = jnp.maximum(m0, jnp.max(u0, axis=-1, keepdims=True))
                n1 = jnp.maximum(m1, jnp.max(u1, axis=-1, keepdims=True))
                m_s[pr, 0], m_s[pr, 1] = n0, n1
                p = jnp.concatenate([jnp.exp(u0 - jnp.concatenate([n0, n0], axis=1)).astype(BF16),
                                     jnp.exp(u1 - jnp.concatenate([n1, n1], axis=1)).astype(BF16)], axis=1)
                pv = _dot(p, v_s[pr, j])
                alpha = jnp.where(first, jnp.exp(m0 - n0), jnp.exp(m1 - n1))
                acc_s[pr] = acc_s[pr] * alpha + pv[:, 0:128]
                l_s[pr] = l_s[pr] * alpha + pv[:, 128:256]

        lax.fori_loop(0, qi + 1, step, 0)
        for pr in range(grp):
            cols = slice(pr * 128, (pr + 1) * 128)
            y_ref[:, cols] = (acc_s[pr] / l_s[pr]).astype(BF16)
            lse_ref[:, cols] = jnp.where(first, m_s[pr, 0], m_s[pr, 1]) + jnp.log(l_s[pr])
        if n:
            pl.when((gi == ngrp - 1) & (qi == nq - 1))(pass_on)
            pl.when((gi == ngrp - 1) & (qi == nq - 1))(finish)

    out = pl.pallas_call(
        body, name="attn_fwd", grid=(ngrp, nq),
        in_specs=[pl.BlockSpec((t, wide), lambda g, i: (i, qcol + g)),
                  pl.BlockSpec((s, wide), lambda g, i: (0, kcol + g)),
                  pl.BlockSpec((s, wide), lambda g, i: (0, vcol + g)),
                  _full((nq, t, t)),
                  pl.BlockSpec((grp, 2, t, 128), lambda g, i: (g, 0, 0, 0)),
                  pl.BlockSpec((grp, 2, nq, 128), lambda g, i: (g, 0, 0, 0))] + [ANY] * n,
        out_specs=[pl.BlockSpec((t, wide), lambda g, i: (i, g)), pl.BlockSpec((t, wide), lambda g, i: (i, g))]
        + [ANY] * n,
        out_shape=[jax.ShapeDtypeStruct((s, D), BF16), jax.ShapeDtypeStruct((s, D), F32)]
        + [jax.ShapeDtypeStruct(a.shape, a.dtype) for a in _arrays(gathering)],
        input_output_aliases={6 + w: 2 + w for w in range(n)},
        scratch_shapes=[pltpu.VMEM((grp, 2, t, 128), BF16), pltpu.VMEM((grp, 2, nq, t, 128), BF16),
                        pltpu.VMEM((grp, nq, 2 * t, 256), BF16), pltpu.VMEM((grp, 2, t, 128), F32),
                        pltpu.VMEM((grp, t, 128), F32), pltpu.VMEM((grp, t, 128), F32)]
        + (_gather_sems(n) if n else []),
        compiler_params=_params("arbitrary", "arbitrary", communicates=bool(n)),
    )(z, z, z, logc, ka, kb, *_arrays(gathering))
    return out[0], out[1], out[2:]


def proj_merge(ya, yb, wa, wb, z, bg, gathering):
    s = ya.shape[0]
    tm = 512
    n = len(gathering)
    steps = s // tm

    def body(*refs):
        ya_ref, yb_ref, wa_ref, wb_ref, ga_ref, gb_ref, bg_ref = refs[:7]
        mg_ref, pa_ref, pb_ref = refs[7 + n:10 + n]
        i = pl.program_id(0)
        if n:
            send, pass_on, finish = _gather_phases(refs[10 + n:10 + 2 * n], *refs[10 + 2 * n:], _spans(gathering))
            pl.when(i == 0)(send)
            pl.when(i == steps - 1)(pass_on)
        pa = _dot(ya_ref[...], wa_ref[...])
        pb = _dot(yb_ref[...], wb_ref[...])
        sa = _sigmoid(ga_ref[...] + bg_ref[0:1, :])
        sb = _sigmoid(gb_ref[...] + bg_ref[1:2, :])
        mg_ref[...] = (sa * pa + sb * pb).astype(BF16)
        pa_ref[...] = pa.astype(BF16)
        pb_ref[...] = pb.astype(BF16)
        if n:
            pl.when(i == steps - 1)(finish)

    out = jax.ShapeDtypeStruct((s, D), BF16)
    res = pl.pallas_call(
        body, name="proj_merge", grid=(steps,),
        in_specs=[_rows(tm, D), _rows(tm, D), _full((D, D)), _full((D, D)),
                  _rows(tm, D, 5), _rows(tm, D, 6), _full((2, D))] + [ANY] * n,
        out_specs=[_rows(tm, D)] * 3 + [ANY] * n,
        out_shape=[out] * 3 + [jax.ShapeDtypeStruct(a.shape, a.dtype) for a in _arrays(gathering)],
        input_output_aliases={7 + w: 3 + w for w in range(n)},
        scratch_shapes=_gather_sems(n) if n else [],
        compiler_params=_params("arbitrary", communicates=bool(n)),
    )(ya, yb, wa, wb, z, z, bg, *_arrays(gathering))
    return res[0], res[1], res[2], res[3:]


def out_norm(merged, w_out, x, g_post, g_fpre, gathering):
    s = x.shape[0]
    tm = 512
    n = len(gathering)
    steps = s // tm

    def body(*refs):
        mg_ref, w_ref, x_ref, gp_ref, gf_ref = refs[:5]
        o_ref, x1_ref, h2_ref = refs[5 + n:8 + n]
        i = pl.program_id(0)
        if n:
            send, pass_on, finish = _gather_phases(refs[8 + n:8 + 2 * n], *refs[8 + 2 * n:], _spans(gathering))
            pl.when(i == 0)(send)
            pl.when(i == steps - 1)(pass_on)
        o = _dot(mg_ref[...], w_ref[...])
        ohat, _ = _rms(o)
        x1 = x_ref[...] + ohat * gp_ref[...]
        x1hat, _ = _rms(x1)
        o_ref[...] = o
        x1_ref[...] = x1
        h2_ref[...] = (x1hat * gf_ref[...]).astype(BF16)
        if n:
            pl.when(i == steps - 1)(finish)

    res = pl.pallas_call(
        body, name="out_norm", grid=(steps,),
        in_specs=[_rows(tm, D), _full((D, D)), _rows(tm, D), _full((1, D)), _full((1, D))] + [ANY] * n,
        out_specs=[_rows(tm, D)] * 3 + [ANY] * n,
        out_shape=[jax.ShapeDtypeStruct((s, D), F32), jax.ShapeDtypeStruct((s, D), F32),
                   jax.ShapeDtypeStruct((s, D), BF16)]
        + [jax.ShapeDtypeStruct(a.shape, a.dtype) for a in _arrays(gathering)],
        input_output_aliases={5 + w: 3 + w for w in range(n)},
        scratch_shapes=_gather_sems(n) if n else [],
        compiler_params=_params("arbitrary", communicates=bool(n)),
    )(merged, w_out, x, g_post, g_fpre, *_arrays(gathering))
    return res[0], res[1], res[2], res[3:]


def mm_ff1(h2, wg, gathering):
    s = h2.shape[0]
    tm = 1024
    n = len(gathering)
    ni = s // tm

    def body(*refs):
        a_ref, b_ref = refs[:2]
        o_ref, r_ref = refs[2 + n:4 + n]
        i, j = pl.program_id(0), pl.program_id(1)
        if n:
            send, pass_on, finish = _gather_phases(refs[4 + n:4 + 2 * n], *refs[4 + 2 * n:], _spans(gathering))
            pl.when((i == 0) & (j == 0))(send)
            pl.when((i == ni - 1) & (j == N_CHIPS // 2))(pass_on)
        a = _dot(a_ref[...], b_ref[...])
        o_ref[...] = a.astype(BF16)
        r = jnp.maximum(a, 0.0)
        r_ref[...] = (r * r).astype(BF16)
        if n:
            pl.when((i == ni - 1) & (j == N_CHIPS - 1))(finish)

    res = pl.pallas_call(
        body, name="mm_ff1", grid=(ni, N_CHIPS),
        in_specs=[pl.BlockSpec((tm, D), lambda i, j: (i, 0)), pl.BlockSpec((None, D, D), lambda i, j: (j, 0, 0))]
        + [ANY] * n,
        out_specs=[pl.BlockSpec((tm, D), lambda i, j: (i, j))] * 2 + [ANY] * n,
        out_shape=[jax.ShapeDtypeStruct((s, D_FF), BF16), jax.ShapeDtypeStruct((s, D_FF), BF16)]
        + [jax.ShapeDtypeStruct(a.shape, a.dtype) for a in _arrays(gathering)],
        input_output_aliases={2 + w: 2 + w for w in range(n)},
        scratch_shapes=_gather_sems(n) if n else [],
        compiler_params=_params("arbitrary", "arbitrary", communicates=bool(n)),
    )(h2, wg, *_arrays(gathering))
    return res[0], res[1], res[2:]


def ff2_loss(rl, w_ff2, x1, target, g_fpost):
    s = x1.shape[0]
    tm = 256

    def body(rl_ref, w_ref, x1_ref, t_ref, g_ref, dy_ref, df_ref, dg_ref, loss_ref):
        @pl.when(pl.program_id(0) == 0)
        def _():
            dg_ref[...] = jnp.zeros_like(dg_ref)
            loss_ref[...] = jnp.zeros_like(loss_ref)

        f = _dot(rl_ref[...], w_ref[...])
        fhat, r = _rms(f)
        err = x1_ref[...] + fhat * g_ref[...] - t_ref[...]
        loss_ref[...] += 0.5 * jnp.sum(jnp.mean(err * err, axis=-1, keepdims=True), axis=0, keepdims=True)
        dy = err * (1.0 / D)
        dy_ref[...] = dy
        dg_ref[...] += jnp.sum(dy * fhat, axis=0, keepdims=True)
        df_ref[...] = _rms_bwd(dy * g_ref[...], fhat, r).astype(BF16)

    return pl.pallas_call(
        body, name="ff2_loss", grid=(s // tm,),
        in_specs=[_rows(tm, D_FF), _full((D_FF, D)), _rows(tm, D), _rows(tm, D), _full((1, D))],
        out_specs=[_rows(tm, D), _rows(tm, D), _full((1, D)), _full((1, 1))],
        out_shape=[jax.ShapeDtypeStruct((s, D), F32), jax.ShapeDtypeStruct((s, D), BF16),
                   jax.ShapeDtypeStruct((1, D), F32), jax.ShapeDtypeStruct((1, 1), F32)],
        compiler_params=_params("arbitrary"),
    )(rl, w_ff2, x1, target, g_fpost)


def mm_tn(name, a, b, ta, tb, out_shape, out_spec):
    s = a.shape[0]

    def body(a_ref, b_ref, o_ref):
        o_ref[...] = _dot(a_ref[...], b_ref[...], TN)

    return pl.pallas_call(
        body, name=name, grid=(a.shape[1] // ta, b.shape[1] // tb),
        in_specs=[pl.BlockSpec((s, ta), lambda i, j: (0, i)), pl.BlockSpec((s, tb), lambda i, j: (0, j))],
        out_specs=out_spec, out_shape=jax.ShapeDtypeStruct(out_shape, F32),
        compiler_params=_params("parallel", "parallel"),
    )(a, b)


def mm_nt(name, a, w):
    s = a.shape[0]
    tm = 512

    def body(a_ref, w_ref, o_ref):
        o_ref[...] = _dot(a_ref[...], w_ref[...], NT).astype(BF16)

    return pl.pallas_call(
        body, name=name, grid=(s // tm,), in_specs=[_rows(tm, D), _full((D, D))], out_specs=_rows(tm, D),
        out_shape=jax.ShapeDtypeStruct((s, D), BF16), compiler_params=_params("parallel"),
    )(a, w)


def ff2_bwd(df, w_ff2, a):
    s = df.shape[0]
    tm = 1024

    def body(df_ref, w_ref, a_ref, da_ref):
        drl = _dot(df_ref[...], w_ref[...], NT)
        da_ref[...] = (drl * (2.0 * jnp.maximum(a_ref[...].astype(F32), 0.0))).astype(BF16)

    return pl.pallas_call(
        body, name="ff2_bwd", grid=(s // tm, D_FF // D),
        in_specs=[pl.BlockSpec((tm, D), lambda i, j: (i, 0)), pl.BlockSpec((D, D), lambda i, j: (j, 0)),
                  pl.BlockSpec((tm, D), lambda i, j: (i, j))],
        out_specs=pl.BlockSpec((tm, D), lambda i, j: (i, j)),
        out_shape=jax.ShapeDtypeStruct((s, D_FF), BF16), compiler_params=_params("parallel", "parallel"),
    )(df, w_ff2, a)


def ff1_bwd_norms(da, wg, x1, o, dy, g_fpre, g_post, swapping):
    s = x1.shape[0]
    tm = 256
    n = len(swapping)
    steps = s // tm

    def body(*refs):
        da_ref, w_ref, x1_ref, o_ref, dy_ref, gf_ref, gp_ref = refs[:7]
        dx1_ref, do_ref, dgf_ref, dgp_ref = refs[7 + n:11 + n]
        i = pl.program_id(0)
        if n:
            send, finish = _swap_phases(refs[7:7 + n], refs[11 + n:11 + 2 * n], *refs[11 + 2 * n:])
            pl.when(i == 0)(send)

        @pl.when(i == 0)
        def _():
            dgf_ref[...] = jnp.zeros_like(dgf_ref)
            dgp_ref[...] = jnp.zeros_like(dgp_ref)

        dh2 = _dot(da_ref[:, 0:D], w_ref[0], NT)
        for k in range(1, N_CHIPS):
            dh2 = dh2 + _dot(da_ref[:, k * D:(k + 1) * D], w_ref[k], NT)
        x1hat, r2 = _rms(x1_ref[...])
        dgf_ref[...] += jnp.sum(dh2 * x1hat, axis=0, keepdims=True)
        dx1 = dy_ref[...] + _rms_bwd(dh2 * gf_ref[...], x1hat, r2)
        ohat, r1 = _rms(o_ref[...])
        dgp_ref[...] += jnp.sum(dx1 * ohat, axis=0, keepdims=True)
        dx1_ref[...] = dx1
        do_ref[...] = _rms_bwd(dx1 * gp_ref[...], ohat, r1).astype(BF16)
        if n:
            pl.when(i == steps - 1)(finish)

    res = pl.pallas_call(
        body, name="ff1_bwd_norms", grid=(steps,),
        in_specs=[_rows(tm, D_FF), _full((N_CHIPS, D, D)), _rows(tm, D), _rows(tm, D), _rows(tm, D),
                  _full((1, D)), _full((1, D))] + [ANY] * n,
        out_specs=[_rows(tm, D), _rows(tm, D), _full((1, D)), _full((1, D))] + [ANY] * n,
        out_shape=[jax.ShapeDtypeStruct((s, D), F32), jax.ShapeDtypeStruct((s, D), BF16),
                   jax.ShapeDtypeStruct((1, D), F32), jax.ShapeDtypeStruct((1, D), F32)] + _swap_shapes(swapping),
        scratch_shapes=_swap_sems(n) if n else [],
        compiler_params=_params("arbitrary", communicates=bool(n)),
    )(da, wg, x1, o, dy, g_fpre, g_post, *swapping)
    return res[0], res[1], res[2], res[3], res[4:]


def out_bwd_gates(do, w_out, pa, pb, z, bg):
    s = do.shape[0]
    tm = 512

    def body(do_ref, w_ref, pa_ref, pb_ref, ga_ref, gb_ref, bg_ref, dpa_ref, dpb_ref, dga_ref, dgb_ref, dbg_ref):
        @pl.when(pl.program_id(0) == 0)
        def _():
            dbg_ref[...] = jnp.zeros_like(dbg_ref)

        dm = _dot(do_ref[...], w_ref[...], NT)
        sa = _sigmoid(ga_ref[...] + bg_ref[0:1, :])
        sb = _sigmoid(gb_ref[...] + bg_ref[1:2, :])
        dpa_ref[...] = (dm * sa).astype(BF16)
        dpb_ref[...] = (dm * sb).astype(BF16)
        dga = dm * pa_ref[...].astype(F32) * (sa * (1.0 - sa))
        dgb = dm * pb_ref[...].astype(F32) * (sb * (1.0 - sb))
        dga_ref[...] = dga.astype(BF16)
        dgb_ref[...] = dgb.astype(BF16)
        dbg_ref[0:1, :] += jnp.sum(dga, axis=0, keepdims=True)
        dbg_ref[1:2, :] += jnp.sum(dgb, axis=0, keepdims=True)

    out = jax.ShapeDtypeStruct((s, D), BF16)
    return pl.pallas_call(
        body, name="out_bwd_gates", grid=(s // tm,),
        in_specs=[_rows(tm, D), _full((D, D)), _rows(tm, D), _rows(tm, D), _rows(tm, D, 5), _rows(tm, D, 6),
                  _full((2, D))],
        out_specs=[_rows(tm, D)] * 4 + [_full((2, D))],
        out_shape=[out] * 4 + [jax.ShapeDtypeStruct((2, D), F32)], compiler_params=_params("arbitrary"),
    )(do, w_out, pa, pb, z, z, bg)


def gating_bwd(z, dya, ln_g, ln_b, w_s, bs_t, swapping):
    s = z.shape[0]
    ones = functools.partial(jnp.ones, (8, CHUNK), BF16)
    n = len(swapping)

    def body(*refs):
        u_ref, v_ref, dya_ref, lg_ref, lb_ref, ws_ref, bst_ref = refs[:7]
        du_ref, dv_ref, dws_ref, dbs_ref, dlg_ref, dlb_ref = refs[7 + n:13 + n]
        dvn_ref = refs[13 + 2 * n]
        ci = pl.program_id(0)
        if n:
            send, finish = _swap_phases(refs[7:7 + n], refs[13 + n:13 + 2 * n], *refs[14 + 2 * n:])
            pl.when(ci == 0)(send)

        @pl.when(ci == 0)
        def _():
            dws_ref[...] = jnp.zeros_like(dws_ref)
            dbs_ref[...] = jnp.zeros_like(dbs_ref)
            dlg_ref[...] = jnp.zeros_like(dlg_ref)
            dlb_ref[...] = jnp.zeros_like(dlb_ref)

        ug, dug_du = _gelu_and_grad(u_ref[...].astype(F32))
        vg, dvg_dv = _gelu_and_grad(v_ref[...].astype(F32))
        vhat, rstd = _layer_norm(vg)
        vn = (vhat * lg_ref[...] + lb_ref[...]).astype(BF16)
        dya = dya_ref[...].astype(F32)
        for g in range(GROUPS):
            cols = slice(g * CHUNK, (g + 1) * CHUNK)
            ws = _tril_ws(ws_ref, g)
            mixed = _dot(ws, vn[:, cols]) + bst_ref[:, g:g + 1]
            du_ref[:, cols] = (dya[:, cols] * mixed * dug_du[:, cols]).astype(BF16)
            dmix = (dya[:, cols] * ug[:, cols]).astype(BF16)
            dbs_ref[g] += _dot(ones(), dmix, NT)
            dws_ref[g] += _dot(dmix, vn[:, cols], NT)
            dvn_ref[:, cols] = _dot(ws, dmix, TN)
        dvn = dvn_ref[...]
        dlg_ref[...] += jnp.sum(dvn * vhat, axis=0, keepdims=True)
        dlb_ref[...] += jnp.sum(dvn, axis=0, keepdims=True)
        dvh = dvn * lg_ref[...]
        dvg = rstd * (dvh - jnp.mean(dvh, axis=-1, keepdims=True)
                      - vhat * jnp.mean(dvh * vhat, axis=-1, keepdims=True))
        dv_ref[...] = (dvg * dvg_dv).astype(BF16)

        @pl.when(ci == pl.num_programs(0) - 1)
        def _():
            r = lax.broadcasted_iota(jnp.int32, (CHUNK, CHUNK), 0)
            c = lax.broadcasted_iota(jnp.int32, (CHUNK, CHUNK), 1)
            for g in range(GROUPS):
                dws_ref[g] = jnp.where(c <= r, dws_ref[g], 0.0)

        if n:
            pl.when(ci == pl.num_programs(0) - 1)(finish)

    out = jax.ShapeDtypeStruct((s, D), BF16)
    res = pl.pallas_call(
        body, name="gating_bwd", grid=(s // CHUNK,),
        in_specs=[_rows(CHUNK, D, 0), _rows(CHUNK, D, 1), _rows(CHUNK, D), _full((1, D)), _full((1, D)),
                  _full((GROUPS, CHUNK, CHUNK)), _full((CHUNK, GROUPS))] + [ANY] * n,
        out_specs=[_rows(CHUNK, D), _rows(CHUNK, D), _full((GROUPS, CHUNK, CHUNK)), _full((GROUPS, 8, CHUNK)),
                   _full((1, D)), _full((1, D))] + [ANY] * n,
        out_shape=[out, out, jax.ShapeDtypeStruct((GROUPS, CHUNK, CHUNK), F32),
                   jax.ShapeDtypeStruct((GROUPS, 8, CHUNK), F32),
                   jax.ShapeDtypeStruct((1, D), F32), jax.ShapeDtypeStruct((1, D), F32)] + _swap_shapes(swapping),
        scratch_shapes=[pltpu.VMEM((CHUNK, D), F32)] + (_swap_sems(n) if n else []),
        compiler_params=_params("arbitrary", communicates=bool(n)),
    )(z, z, dya, ln_g, ln_b, w_s, bs_t, *swapping)
    return (*res[:6], res[6:])


def attn_bwd(z, yb, dyb, lse, logc, ka, kb, scattering, gathering=None):
    s = z.shape[0]
    nq = s // ATT_T
    t = ATT_T
    grp = ATT_BWD_GROUP
    ngrp = HEADS // 2 // grp
    wide = 128 * grp
    qcol, kcol, vcol = 2 * D // wide, 3 * D // wide, 4 * D // wide
    scale = 1.0 / math.sqrt(HEAD_DIM)
    n = len(scattering)
    g8 = 0 if gathering is None else 1

    def body(*refs):
        q_ref, k_ref, v_ref, y_ref, dy_ref, lse_ref, lc_ref, ka_ref, kb_ref = refs[:9]
        dq_ref, dk_ref, dv_ref = refs[9 + n + g8:12 + n + g8]
        qa_s, qt_s, da_s, dt_s, dq_s, dkt_s, dvt_s = refs[12 + 2 * n + 2 * g8:19 + 2 * n + 2 * g8]
        sems = refs[19 + 2 * n + 2 * g8:]
        gi, j = pl.program_id(0), pl.program_id(1)
        first, lane, ones = _head_masks()
        if n:
            send, finish = _scatter_phases(refs[9:9 + n], refs[12 + n + g8:12 + 2 * n + g8], *sems[:2])
            pl.when((gi == 0) & (j == 0))(send)
        if g8:
            send8, pass_on8, finish8 = _allgather8_phases(refs[12 + 2 * n + g8], *sems[2 * (n > 0):])
            pl.when((gi == 0) & (j == 0))(send8)
            pl.when((gi == ngrp - 1) & (j == nq - 1))(pass_on8)

        @pl.when(j == 0)
        def _():
            dq_s[...] = jnp.zeros_like(dq_s)
            for pr in range(grp):
                cols = slice(pr * 128, (pr + 1) * 128)
                for ib in range(nq):
                    rows = slice(ib * t, (ib + 1) * t)
                    q = q_ref[rows, cols].astype(F32) * scale
                    lse = lse_ref[rows, cols]
                    qa_s[pr, 0, ib] = jnp.where(first, q, _place3(lane, HEAD_DIM + 2 * AUG, _split3(-lse[:, 0:1]),
                                                                  ones(0, 2 * AUG))).astype(BF16)
                    qa_s[pr, 1, ib] = jnp.where(
                        first, _place3(lane, 2 * AUG, _split3(-lse[:, HEAD_DIM:HEAD_DIM + 1]), ones(1, 2 * AUG)),
                        q).astype(BF16)
                    qt_s[pr, ib, :, 0:t] = jnp.where(first, q, 0.0).T.astype(BF16)
                    qt_s[pr, ib, :, t:2 * t] = jnp.where(first, 0.0, q).T.astype(BF16)
                    do = dy_ref[rows, cols].astype(F32)
                    prod = do * y_ref[rows, cols].astype(F32)
                    dd0 = jnp.sum(jnp.where(first, prod, 0.0), axis=-1, keepdims=True)
                    dd1 = jnp.sum(jnp.where(first, 0.0, prod), axis=-1, keepdims=True)
                    da_s[pr, 0, ib] = jnp.where(first, do, _place3(lane, HEAD_DIM, _split3(-dd0), 0.0)).astype(BF16)
                    da_s[pr, 1, ib] = jnp.where(first, _place3(lane, 0, _split3(-dd1), 0.0), do).astype(BF16)
                    dt_s[pr, ib, :, 0:t] = jnp.where(first, do, 0.0).T.astype(BF16)
                    dt_s[pr, ib, :, t:2 * t] = jnp.where(first, 0.0, do).T.astype(BF16)

        keys = []
        for pr in range(grp):
            kj = k_ref[:, pr * 128:(pr + 1) * 128].astype(F32)
            vj = v_ref[:, pr * 128:(pr + 1) * 128].astype(F32)
            keys.append((
                jnp.where(first, kj, ka_ref[pr, 0] + kb_ref[pr, 0, pl.ds(j, 1), :]).astype(BF16),
                jnp.where(first, ka_ref[pr, 1] + kb_ref[pr, 1, pl.ds(j, 1), :], kj).astype(BF16),
                jnp.concatenate([jnp.where(first, kj, 0.0), jnp.where(first, 0.0, kj)], axis=0).astype(BF16),
                jnp.where(first, vj, ones(0, AUG)).astype(BF16),
                jnp.where(first, ones(1, AUG), vj).astype(BF16)))
        dkt_s[...] = jnp.zeros_like(dkt_s)
        dvt_s[...] = jnp.zeros_like(dvt_s)

        def step(i, _):
            lc = lc_ref[i - j]
            rows = pl.ds(pl.multiple_of(i * t, t), t)
            for pr in range(grp):
                k0a, k1a, kst, v0a, v1a = keys[pr]
                p0 = jnp.exp(_dot(qa_s[pr, 0, i], k0a, NT) + lc)
                p1 = jnp.exp(_dot(qa_s[pr, 1, i], k1a, NT) + lc)
                e0 = (p0 * _dot(da_s[pr, 0, i], v0a, NT)).astype(BF16)
                e1 = (p1 * _dot(da_s[pr, 1, i], v1a, NT)).astype(BF16)
                dq_s[pr, rows, :] += _dot(jnp.concatenate([e0, e1], axis=1), kst)
                dvt_s[pr] += _dot(dt_s[pr, i], jnp.concatenate([p0.astype(BF16), p1.astype(BF16)], axis=0))
                dkt_s[pr] += _dot(qt_s[pr, i], jnp.concatenate([e0, e1], axis=0))
            return 0

        lax.fori_loop(j, nq, step, 0)
        for pr in range(grp):
            dk_ref[:, pr * 128:(pr + 1) * 128] = dkt_s[pr].T.astype(BF16)
            dv_ref[:, pr * 128:(pr + 1) * 128] = dvt_s[pr].T.astype(BF16)

        @pl.when(j == nq - 1)
        def _():
            for pr in range(grp):
                dq_ref[:, pr * 128:(pr + 1) * 128] = (dq_s[pr] * scale).astype(BF16)

        if n:
            pl.when((gi == ngrp - 1) & (j == nq - 1))(finish)
        if g8:
            pl.when((gi == ngrp - 1) & (j == nq - 1))(finish8)

    colblock = lambda c: pl.BlockSpec((s, wide), lambda g, j: (0, c + g))
    once = lambda c: pl.BlockSpec((s, wide), lambda g, j: (0, c + g), pipeline_mode=pl.Buffered(1))
    blk = lambda c: pl.BlockSpec((t, wide), lambda g, j: (j, c + g))
    out = jax.ShapeDtypeStruct((s, D), BF16)
    res = pl.pallas_call(
        body, name="attn_bwd", grid=(ngrp, nq),
        in_specs=[once(qcol), blk(kcol), blk(vcol), once(0), once(0), once(0),
                  pl.BlockSpec((nq, t, t), lambda g, j: (0, 0, 0), pipeline_mode=pl.Buffered(1)),
                  pl.BlockSpec((grp, 2, t, 128), lambda g, j: (g, 0, 0, 0)),
                  pl.BlockSpec((grp, 2, nq, 128), lambda g, j: (g, 0, 0, 0))] + [ANY] * (n + g8),
        out_specs=[colblock(0), blk(0), blk(0)] + [ANY] * (n + g8),
        out_shape=[out] * 3 + _scatter_shapes(scattering)
        + ([jax.ShapeDtypeStruct(gathering.shape, gathering.dtype)] if g8 else []),
        input_output_aliases={9 + n: 3 + n} if g8 else {},
        scratch_shapes=[pltpu.VMEM((grp, 2, nq, t, 128), BF16), pltpu.VMEM((grp, nq, 128, 2 * t), BF16),
                        pltpu.VMEM((grp, 2, nq, t, 128), BF16), pltpu.VMEM((grp, nq, 128, 2 * t), BF16),
                        pltpu.VMEM((grp, s, 128), F32), pltpu.VMEM((grp, 128, t), F32),
                        pltpu.VMEM((grp, 128, t), F32)]
        + (_scatter_sems(n) if n else [])
        + ([pltpu.SemaphoreType.DMA((7,)), pltpu.SemaphoreType.DMA((7,))] if g8 else []),
        compiler_params=_params("arbitrary", "arbitrary", communicates=bool(n + g8)),
    )(z, z, z, yb, dyb, lse, logc, ka, kb, *scattering, *([gathering] if g8 else []))
    return res[0], res[1], res[2], res[3:3 + n], (res[3 + n] if g8 else None)


def in_bwd_norm(dz, wg, x, dx1, g_pre, scattering, gathering=None):
    s = x.shape[0]
    tm = 512
    n = len(scattering)
    g = 0 if gathering is None else 1
    last = (s // tm - 1, N_CHIPS - 1)

    def body(*refs):
        dz_ref, w_ref, x_ref, dx1_ref, g_ref = refs[:5]
        dx_ref, dg_ref = refs[5 + n + g:7 + n + g]
        acc_ref = refs[7 + 2 * n + 2 * g]
        sems = refs[8 + 2 * n + 2 * g:]
        i, k = pl.program_id(0), pl.program_id(1)
        if n:
            send, finish = _scatter_phases(refs[5:5 + n], refs[7 + n + g:7 + 2 * n + g], *sems[:2])
            pl.when((i == 0) & (k == 0))(send)
        if g:
            send8, pass_on8, finish8 = _allgather8_phases(refs[7 + 2 * n + g], *sems[2 * (n > 0):])
            pl.when((i == 0) & (k == 0))(send8)
            pl.when((i == last[0]) & (k == last[1]))(pass_on8)

        @pl.when((i == 0) & (k == 0))
        def _():
            dg_ref[...] = jnp.zeros_like(dg_ref)

        part = _dot(dz_ref[...], w_ref[...], NT)

        @pl.when(k == 0)
        def _():
            acc_ref[...] = part

        @pl.when(k > 0)
        def _():
            acc_ref[...] += part

        @pl.when(k == N_CHIPS - 1)
        def _():
            dh = acc_ref[...]
            xhat, r = _rms(x_ref[...])
            dg_ref[...] += jnp.sum(dh * xhat, axis=0, keepdims=True)
            dx_ref[...] = dx1_ref[...] + _rms_bwd(dh * g_ref[...], xhat, r)

        if n:
            pl.when((i == last[0]) & (k == last[1]))(finish)
        if g:
            pl.when((i == last[0]) & (k == last[1]))(finish8)

    row = pl.BlockSpec((tm, D), lambda i, k: (i, 0))
    vec = pl.BlockSpec((1, D), lambda i, k: (0, 0))
    res = pl.pallas_call(
        body, name="in_bwd_norm", grid=(s // tm, N_CHIPS),
        in_specs=[pl.BlockSpec((tm, IN_SHARD), lambda i, k: (i, k)),
                  pl.BlockSpec((None, D, IN_SHARD), lambda i, k: (k, 0, 0)), row, row, vec] + [ANY] * (n + g),
        out_specs=[row, vec] + [ANY] * (n + g),
        out_shape=[jax.ShapeDtypeStruct((s, D), F32), jax.ShapeDtypeStruct((1, D), F32)]
        + _scatter_shapes(scattering) + ([jax.ShapeDtypeStruct(gathering.shape, gathering.dtype)] if g else []),
        input_output_aliases={5 + n: 2 + n} if g else {},
        scratch_shapes=[pltpu.VMEM((tm, D), F32)] + (_scatter_sems(n) if n else [])
        + ([pltpu.SemaphoreType.DMA((7,)), pltpu.SemaphoreType.DMA((7,))] if g else []),
        compiler_params=_params("arbitrary", "arbitrary", communicates=bool(n + g)),
    )(dz, wg, x, dx1, g_pre, *scattering, *([gathering] if g else []))
    return res[0], res[1], res[2:2 + n], (res[2 + n] if g else None)


def _adamw_math(w, g, m, v):
    m = ADAM_B1 * m + (1.0 - ADAM_B1) * g
    v = ADAM_B2 * v + (1.0 - ADAM_B2) * (g * g)
    m_hat = m / (1.0 - ADAM_B1 ** ADAM_STEP)
    v_hat = v / (1.0 - ADAM_B2 ** ADAM_STEP)
    delta = -ADAM_LR * (m_hat / (jnp.sqrt(v_hat) + ADAM_EPS) + ADAM_WD * w)
    return delta, m, v


def adamw(name, w, g, m, v, tr):
    r, c = w.shape

    def body(w_ref, g_ref, m_ref, v_ref, go_ref, d_ref, nm_ref, nv_ref):
        g = g_ref[...]
        go_ref[...] = g
        d_ref[...], nm_ref[...], nv_ref[...] = _adamw_math(w_ref[...], g, m_ref[...], v_ref[...])

    out = jax.ShapeDtypeStruct((r, c), F32)
    return pl.pallas_call(
        body, name=name, grid=(r // tr,), in_specs=[_rows(tr, c)] * 4, out_specs=[_rows(tr, c)] * 4,
        out_shape=[out] * 4, compiler_params=_params("parallel"),
    )(w, g, m, v)


def _allgather8_phases(buf, send_sems, recv_sems):
    x, y, c, chips = _place()
    me = 2 * x + y
    sibling = (x, y, 1 - c)
    rows = buf.shape[1] // 2

    def part(chip, core):
        return buf.at[chip, pl.ds(core * rows, rows)]

    def copy(k, block, to):
        return pltpu.make_async_remote_copy(src_ref=block, dst_ref=block, send_sem=send_sems.at[k],
                                            recv_sem=recv_sems.at[k], device_id=to, device_id_type=MESH)

    def chip_of(j):
        return 2 * chips[j][0] + chips[j][1]

    def send():
        copy(0, part(me, c), sibling).start()
        for j in range(3):
            copy(1 + j, part(me, c), (chips[j][0], chips[j][1], c)).start()

    def pass_on():
        for j in range(3):
            copy(1 + j, part(chip_of(j), c), (chips[j][0], chips[j][1], c)).wait_recv()
            copy(4 + j, part(chip_of(j), c), sibling).start()

    def finish():
        copy(0, part(me, 1 - c), sibling).wait_recv()
        for j in range(3):
            copy(4 + j, part(chip_of(j), 1 - c), sibling).wait_recv()
        copy(0, part(me, c), sibling).wait_send()
        for j in range(3):
            copy(1 + j, part(me, c), (chips[j][0], chips[j][1], c)).wait_send()
            copy(4 + j, part(chip_of(j), c), sibling).wait_send()

    return send, pass_on, finish


def add_halves(name, g, recv, c_idx, tr):
    n, h, c = recv.shape

    def body(c_ref, g_ref, r_ref, o_ref):
        o_ref[...] = (g_ref[...] + r_ref[...]).astype(BF16)

    nb = h // tr
    return pl.pallas_call(
        body, name=name,
        grid_spec=pltpu.PrefetchScalarGridSpec(
            num_scalar_prefetch=1, grid=(n, nb),
            in_specs=[pl.BlockSpec((None, tr, c), lambda k, i, c_ref: (k, c_ref[0] * nb + i, 0)),
                      pl.BlockSpec((None, tr, c), lambda k, i, c_ref: (k, i, 0))],
            out_specs=pl.BlockSpec((None, tr, c), lambda k, i, c_ref: (k, i, 0))),
        out_shape=jax.ShapeDtypeStruct((n, h, c), BF16), compiler_params=_params("parallel", "parallel"),
    )(c_idx, g, recv)


def sum_chips(name, parts, recv, where, tr, after=None):
    n, h, c = recv.shape
    nb = h // tr

    def body(w_ref, p_ref, r_ref, *rest):
        acc = p_ref[...].astype(F32)
        for k in range(n):
            acc = acc + r_ref[k].astype(F32)
        rest[-1][...] = acc

    return pl.pallas_call(
        body, name=name,
        grid_spec=pltpu.PrefetchScalarGridSpec(
            num_scalar_prefetch=1, grid=(nb,),
            in_specs=[pl.BlockSpec((None, tr, c), lambda i, w_ref: (w_ref[0], i, 0)),
                      pl.BlockSpec((n, tr, c), lambda i, w_ref: (0, i, 0))] + ([ANY] if after is not None else []),
            out_specs=pl.BlockSpec((tr, c), lambda i, w_ref: (w_ref[1] * nb + i, 0))),
        out_shape=jax.ShapeDtypeStruct((2 * h, c), F32), compiler_params=_params("parallel"),
    )(where, parts, recv, *([after] if after is not None else []))


def place_shard(name, shard, where, dtype, tr):
    r, c = shard.shape

    def body(w_ref, s_ref, o_ref):
        o_ref[...] = s_ref[...].astype(dtype)

    return pl.pallas_call(
        body, name=name,
        grid_spec=pltpu.PrefetchScalarGridSpec(
            num_scalar_prefetch=1, grid=(r // tr,),
            in_specs=[pl.BlockSpec((tr, c), lambda i, w_ref: (i, 0))],
            out_specs=pl.BlockSpec((None, tr, c), lambda i, w_ref: (w_ref[0], i, 0))),
        out_shape=jax.ShapeDtypeStruct((N_CHIPS, r, c), dtype), compiler_params=_params("parallel"),
    )(where, shard)


ANY = pl.BlockSpec(memory_space=pl.ANY)


def _place():
    x, y, c = lax.axis_index("x"), lax.axis_index("y"), lax.axis_index("c")
    chips = [(1 - x, y), (x, 1 - y), (1 - x, 1 - y)]
    return x, y, c, chips


def gather_shards(arrays):
    n = len(arrays)

    def body(*refs):
        send, pass_on, finish = _gather_phases(refs[n:2 * n], *refs[2 * n:], _spans(arrays))
        send()
        pass_on()
        finish()

    return pl.pallas_call(
        body, name="gather_shards", in_specs=[ANY] * n, out_specs=[ANY] * n,
        out_shape=[jax.ShapeDtypeStruct(a.shape, a.dtype) for a in _arrays(arrays)],
        input_output_aliases={w: w for w in range(n)}, scratch_shapes=_gather_sems(n),
        compiler_params=pltpu.CompilerParams(has_side_effects=True),
    )(*_arrays(arrays))


def _gather_sems(n):
    return [pltpu.SemaphoreType.DMA((6 * n,)), pltpu.SemaphoreType.DMA((6 * n,))]


class Span(typing.NamedTuple):
    array: jax.Array
    lo: int
    hi: int
    ways: tuple = (0, 1, 2)


def _arrays(gathering):
    return [g.array if isinstance(g, Span) else g for g in gathering]


def _spans(gathering):
    return [(g.lo, g.hi, g.ways) if isinstance(g, Span) else (0, g.shape[1], (0, 1, 2)) for g in gathering]


def _gather_phases(out, send_sems, recv_sems, spans):
    n = len(out)
    if not any(ways for _, _, ways in spans):
        return (lambda: None,) * 3
    x, y, c, chips = _place()
    me = 2 * x + y
    sibling = (x, y, 1 - c)

    def half(w, chip, core):
        lo, hi, _ = spans[w]
        h = (hi - lo) // 2
        return out[w].at[chip, pl.ds(lo + core * h, h)]

    def copy(k, block, to):
        return pltpu.make_async_remote_copy(src_ref=block, dst_ref=block, send_sem=send_sems.at[k],
                                            recv_sem=recv_sems.at[k], device_id=to, device_id_type=MESH)

    def over_ici(w, j, chip):
        return copy(3 * w + j, half(w, chip, c), (chips[j][0], chips[j][1], c))

    def over_d2d(w, j, core):
        return copy(3 * n + 3 * w + j, half(w, 2 * chips[j][0] + chips[j][1], core), sibling)

    pairs = [(w, j) for w in range(n) for j in spans[w][2]]

    def send():
        for w, j in pairs:
            over_ici(w, j, me).start()

    def pass_on():
        for w, j in pairs:
            over_ici(w, j, 2 * chips[j][0] + chips[j][1]).wait_recv()
            over_d2d(w, j, c).start()

    def finish():
        for w, j in pairs:
            over_d2d(w, j, 1 - c).wait_recv()
        for w, j in pairs:
            over_ici(w, j, me).wait_send()
            over_d2d(w, j, c).wait_send()

    return send, pass_on, finish


def _relay_sems():
    return [pltpu.SemaphoreType.DMA((4,)), pltpu.SemaphoreType.DMA((4,))]


def _relay_phases(out, send_sems, recv_sems):
    x, y, c, chips = _place()
    sibling = (x, y, 1 - c)
    rows = out.shape[1]
    quarter = rows // 4
    far = 2 * chips[2][0] + chips[2][1]

    def piece(chip, way, core):
        return out.at[chip, pl.ds(way * (rows // 2) + core * quarter, quarter)]

    def copy(k, block, to):
        return pltpu.make_async_remote_copy(src_ref=block, dst_ref=block, send_sem=send_sems.at[k],
                                            recv_sem=recv_sems.at[k], device_id=to, device_id_type=MESH)

    def over_ici(way, chip):
        return copy(way, piece(chip, way, c), (chips[way][0], chips[way][1], c))

    def over_d2d(way, core):
        return copy(2 + way, piece(far, way, core), sibling)

    def send():
        for way in range(2):
            other = chips[1 - way]
            over_ici(way, 2 * other[0] + other[1]).start()

    def pass_on():
        for way in range(2):
            over_ici(way, far).wait_recv()
            over_d2d(way, c).start()

    def finish():
        for way in range(2):
            over_d2d(way, 1 - c).wait_recv()
        for way in range(2):
            other = chips[1 - way]
            over_ici(way, 2 * other[0] + other[1]).wait_send()
            over_d2d(way, c).wait_send()

    return send, pass_on, finish


def swap_halves(name, grads):
    n = len(grads)

    def body(*refs):
        send, finish = _swap_phases(refs[:n], refs[n:2 * n], *refs[2 * n:])
        send()
        finish()

    return pl.pallas_call(
        body, name=name, in_specs=[ANY] * n, out_specs=[ANY] * n, out_shape=_swap_shapes(grads),
        scratch_shapes=_swap_sems(n), compiler_params=pltpu.CompilerParams(has_side_effects=True),
    )(*grads)


def _swap_shapes(grads):
    return [jax.ShapeDtypeStruct((a.shape[0], a.shape[1] // 2, a.shape[2]), a.dtype) for a in grads]


def _swap_sems(n):
    return [pltpu.SemaphoreType.DMA((n,)), pltpu.SemaphoreType.DMA((n,))]


def _swap_phases(g, out, send_sems, recv_sems):
    x, y, c, _ = _place()

    def copies():
        return [pltpu.make_async_remote_copy(
            src_ref=g[w].at[:, pl.ds((1 - c) * (g[w].shape[1] // 2), g[w].shape[1] // 2)], dst_ref=out[w],
            send_sem=send_sems.at[w], recv_sem=recv_sems.at[w], device_id=(x, y, 1 - c), device_id_type=MESH)
            for w in range(len(g))]

    def send():
        for cp in copies():
            cp.start()

    def finish():
        for cp in copies():
            cp.wait()

    return send, finish


def _send_phases(g, out, send_sems, recv_sems):
    x, y, c, _ = _place()

    def copies():
        return [pltpu.make_async_remote_copy(
            src_ref=g[w], dst_ref=out[w], send_sem=send_sems.at[w], recv_sem=recv_sems.at[w],
            device_id=(x, y, 1 - c), device_id_type=MESH) for w in range(len(g))]

    def send():
        for cp in copies():
            cp.start()

    def finish():
        for cp in copies():
            cp.wait()

    return send, finish


def dw_in_half(name, h, dz, which, sending):
    s = h.shape[0]
    hh, tb = D // 2, IN_SHARD // 2
    n = len(sending)
    steps = IN_COLS // tb

    def body(w_ref, *refs):
        a_ref, b_ref, o_ref = refs[0], refs[1], refs[2 + n]
        j = pl.program_id(0)
        if n:
            send, finish = _send_phases(refs[2:2 + n], refs[3 + n:3 + 2 * n], *refs[3 + 2 * n:])
            pl.when(j == 0)(send)
        o_ref[...] = _dot(a_ref[...], b_ref[...], TN)
        if n:
            pl.when(j == steps - 1)(finish)

    out = pl.pallas_call(
        body, name=name,
        grid_spec=pltpu.PrefetchScalarGridSpec(
            num_scalar_prefetch=1, grid=(steps,),
            in_specs=[pl.BlockSpec((s, hh), lambda j, w: (0, w[0])), pl.BlockSpec((s, tb), lambda j, w: (0, j))]
            + [ANY] * n,
            out_specs=[pl.BlockSpec((None, hh, tb), lambda j, w: (j // 2, 0, j % 2))] + [ANY] * n,
            scratch_shapes=_swap_sems(n) if n else []),
        out_shape=[jax.ShapeDtypeStruct((N_CHIPS, hh, IN_SHARD), F32)]
        + [jax.ShapeDtypeStruct(a.shape, a.dtype) for a in sending],
        compiler_params=_params("arbitrary", communicates=bool(n)),
    )(which, h, dz, *sending)
    return out[0], out[1:]


def scatter_chips(parts):
    n = len(parts)

    def body(*refs):
        send, finish = _scatter_phases(refs[:n], refs[n:2 * n], *refs[2 * n:])
        send()
        finish()

    return pl.pallas_call(
        body, name="scatter_chips", in_specs=[ANY] * n, out_specs=[ANY] * n,
        out_shape=_scatter_shapes(parts), scratch_shapes=_scatter_sems(n),
        compiler_params=pltpu.CompilerParams(has_side_effects=True),
    )(*parts)


def _scatter_shapes(parts):
    return [jax.ShapeDtypeStruct((3,) + a.shape[1:], a.dtype) for a in parts]


def _scatter_sems(n):
    return [pltpu.SemaphoreType.DMA((3 * n,)), pltpu.SemaphoreType.DMA((3 * n,))]


def _scatter_phases(p, out, send_sems, recv_sems):
    x, y, c, chips = _place()

    def copies():
        return [pltpu.make_async_remote_copy(
            src_ref=p[w].at[2 * px + py], dst_ref=out[w].at[j], send_sem=send_sems.at[3 * w + j],
            recv_sem=recv_sems.at[3 * w + j], device_id=(px, py, c), device_id_type=MESH)
            for w in range(len(p)) for j, (px, py) in enumerate(chips)]

    def send():
        for cp in copies():
            cp.start()

    def finish():
        for cp in copies():
            cp.wait()

    return send, finish


def _join_only(arrays):
    n = len(arrays)

    def body(*refs):
        out = refs[n:2 * n]
        send_sems, recv_sems = refs[2 * n:]
        x, y, c, _ = _place()

        def copy(w, core):
            h = out[w].shape[0] // 2
            rows = out[w].at[pl.ds(core * h, h)]
            return pltpu.make_async_remote_copy(
                src_ref=rows, dst_ref=rows, send_sem=send_sems.at[w], recv_sem=recv_sems.at[w],
                device_id=(x, y, 1 - c), device_id_type=MESH)

        for w in range(n):
            copy(w, c).start()
        for w in range(n):
            copy(w, 1 - c).wait_recv()
        for w in range(n):
            copy(w, c).wait_send()

    return pl.pallas_call(
        body, name="join_only", in_specs=[ANY] * n, out_specs=[ANY] * n,
        out_shape=[jax.ShapeDtypeStruct(a.shape, a.dtype) for a in arrays],
        input_output_aliases={w: w for w in range(n)},
        scratch_shapes=[pltpu.SemaphoreType.DMA((n,)), pltpu.SemaphoreType.DMA((n,))],
        compiler_params=pltpu.CompilerParams(has_side_effects=True),
    )(*arrays)


HBM = pl.BlockSpec(memory_space=pltpu.HBM)
SEM = pl.BlockSpec(memory_space=pltpu.SEMAPHORE)
DATAFLOW = pltpu.SideEffectType.DATAFLOW_SIDE_EFFECTING


def _scatter_copies(p_ref, land_ref, send_sems, recv_sems):
    x, y, c, chips = _place()
    return [pltpu.make_async_remote_copy(
        src_ref=p_ref.at[2 * px + py], dst_ref=land_ref.at[j], send_sem=send_sems[j], recv_sem=recv_sems[j],
        device_id=(px, py, c), device_id_type=MESH) for j, (px, py) in enumerate(chips)]


def scatter_start(p):
    land = jax.ShapeDtypeStruct((3,) + p.shape[1:], p.dtype)

    def body(p_ref, land_ref, *outs):
        for cp in _scatter_copies(p_ref, land_ref, outs[0:3], outs[3:6]):
            cp.start()
        outs[8][...] = jnp.zeros_like(outs[8])

    return pl.pallas_call(
        body, name="scatter_start",
        out_shape=(pltpu.SemaphoreType.DMA(()),) * 6
        + (pltpu.HBM(p.shape, p.dtype), pltpu.HBM(land.shape, land.dtype), jax.ShapeDtypeStruct((8, 128), F32)),
        in_specs=(HBM, HBM), out_specs=(SEM,) * 6 + (HBM, HBM, pl.BlockSpec(memory_space=pltpu.VMEM)),
        input_output_aliases={0: 6, 1: 7},
        compiler_params=pltpu.CompilerParams(has_side_effects=DATAFLOW),
    )(pltpu.with_memory_space_constraint(p, pltpu.HBM),
      pltpu.with_memory_space_constraint(lax.empty(land.shape, land.dtype), pltpu.HBM))


def scatter_wait(started, after):
    sems, p_thru, land_thru = started[0:6], started[6], started[7]

    def body(p_ref, land_ref, *refs):
        for cp in _scatter_copies(p_ref, land_ref, refs[0:3], refs[3:6]):
            cp.wait_send()
            cp.wait_recv()

    return pl.pallas_call(
        body, name="scatter_wait",
        out_shape=(pltpu.HBM(p_thru.shape, p_thru.dtype), pltpu.HBM(land_thru.shape, land_thru.dtype)),
        in_specs=(HBM, HBM) + (SEM,) * 6 + (pl.BlockSpec(memory_space=pl.ANY),) * len(after), out_specs=(HBM, HBM),
        input_output_aliases={0: 0, 1: 1},
        compiler_params=pltpu.CompilerParams(has_side_effects=DATAFLOW),
    )(p_thru, land_thru, *sems, *after)


def join_halves(arrays, gathering=None):
    n = len(arrays)
    if gathering is None:
        return _join_only(arrays), None

    def body(*refs):
        out = refs[n + 1:2 * n + 1]
        send_sems, recv_sems = refs[2 * n + 2:2 * n + 4]
        send8, pass_on8, finish8 = _allgather8_phases(refs[2 * n + 1], *refs[2 * n + 4:])
        x, y, c, _ = _place()

        def copy(w, core):
            h = out[w].shape[0] // 2
            rows = out[w].at[pl.ds(core * h, h)]
            return pltpu.make_async_remote_copy(
                src_ref=rows, dst_ref=rows, send_sem=send_sems.at[w], recv_sem=recv_sems.at[w],
                device_id=(x, y, 1 - c), device_id_type=MESH)

        send8()
        for w in range(n):
            copy(w, c).start()
        pass_on8()
        for w in range(n):
            copy(w, 1 - c).wait_recv()
        finish8()
        for w in range(n):
            copy(w, c).wait_send()

    res = pl.pallas_call(
        body, name="join_halves", in_specs=[ANY] * (n + 1), out_specs=[ANY] * (n + 1),
        out_shape=[jax.ShapeDtypeStruct(a.shape, a.dtype) for a in list(arrays) + [gathering]],
        input_output_aliases={w: w for w in range(n + 1)},
        scratch_shapes=[pltpu.SemaphoreType.DMA((n,)), pltpu.SemaphoreType.DMA((n,)),
                        pltpu.SemaphoreType.DMA((7,)), pltpu.SemaphoreType.DMA((7,))],
        compiler_params=pltpu.CompilerParams(has_side_effects=True),
    )(*arrays, gathering)
    return res[:n], res[n]


def allreduce_small(packed):
    r, c = packed.shape
    n_dev = 8

    def body(x_ref, all_ref, sum_ref, send_sems, recv_sems, local_sem):
        x, y, cc, chips = _place()
        me, sibling = (x, y, cc), (x, y, 1 - cc)

        def rows(px, py, pc):
            return all_ref.at[4 * px + 2 * py + pc]

        def copy(k, block, to, src=None):
            return pltpu.make_async_remote_copy(
                src_ref=rows(*block) if src is None else src, dst_ref=rows(*block), send_sem=send_sems.at[k],
                recv_sem=recv_sems.at[k], device_id=to, device_id_type=MESH)

        mine = pltpu.make_async_copy(x_ref, rows(*me), local_sem)
        mine.start()
        first = [copy(0, me, sibling, src=x_ref)]
        first += [copy(1 + j, me, (*chip, cc), src=x_ref) for j, chip in enumerate(chips)]
        for cp in first:
            cp.start()
        passed = [copy(4 + j, (*chip, cc), sibling) for j, chip in enumerate(chips)]
        for j, chip in enumerate(chips):
            copy(1 + j, (*chip, cc), me).wait_recv()
            passed[j].start()
        copy(0, sibling, me).wait_recv()
        for j, chip in enumerate(chips):
            copy(4 + j, (*chip, 1 - cc), me).wait_recv()
        for cp in first + passed:
            cp.wait_send()
        mine.wait()
        acc = all_ref[0]
        for k in range(1, n_dev):
            acc = acc + all_ref[k]
        sum_ref[...] = acc

    vm = pl.BlockSpec(memory_space=pltpu.VMEM)
    return pl.pallas_call(
        body, name="allreduce_small", in_specs=[vm], out_specs=[vm, vm],
        out_shape=[jax.ShapeDtypeStruct((n_dev, r, c), F32), jax.ShapeDtypeStruct((r, c), F32)],
        scratch_shapes=[pltpu.SemaphoreType.DMA((7,)), pltpu.SemaphoreType.DMA((7,)), pltpu.SemaphoreType.DMA],
        compiler_params=pltpu.CompilerParams(has_side_effects=True, vmem_limit_bytes=VMEM_LIMIT),
    )(packed)[1]


def local_step(x, target, vecs, w_s, bs_t, bg, wg_in, late, core=None, order=None, where=None):
    on_mesh = core is not None

    def add(names, grads, recv):
        return [add_halves("add_" + n, g, r, core, min(r.shape[1], 256)) for n, g, r in zip(names, grads, recv)]

    g_pre, ln_g, ln_b, g_post, g_fpre, g_fpost = vecs
    s = x.shape[0]
    if order is None:
        order = jnp.arange(N_CHIPS, dtype=jnp.int32)
    logc = _attn_tables(s)
    ka, kb = _alibi_tables(s)

    h = norm_pre(x, g_pre)
    if not on_mesh:
        wg_a, wg_b, wg_out, wg_ff1, wg_ff2 = late
    if on_mesh:
        cut = D // 4
        z, wg_in, (wg_a, wg_b, wg_out, wg_ff1, wg_ff2) = mm_in(h, wg_in, order, True, late)
        ya, (wg_b, wg_ff2) = gating_fwd(z, ln_g, ln_b, w_s, bs_t, [wg_b, Span(wg_ff2, 0, cut)])
        yb, lse, (wg_a, wg_ff1, wg_ff2, bg) = attn_fwd(
            z, logc, ka, kb, [wg_a, wg_ff1, Span(wg_ff2, cut, 3 * cut), bg])
        bg = jnp.transpose(bg[:, :2, :], (1, 0, 2)).reshape(2, D)
    else:
        z, _, _ = mm_in(h, wg_in, order, False)
        ya, _ = gating_fwd(z, ln_g, ln_b, w_s, bs_t, [])
        yb, lse, _ = attn_fwd(z, logc, ka, kb, [])
    merged, pa, pb, got = proj_merge(ya, yb, wg_a.reshape(D, D), wg_b.reshape(D, D), z, bg, [wg_out] if on_mesh else [])
    wg_out = got[0] if on_mesh else wg_out
    w_out = wg_out.reshape(D, D)
    o, x1, h2, got = out_norm(merged, w_out, x, g_post, g_fpre, [Span(wg_ff2, 3 * D // 4, D)] if on_mesh else [])
    a, rl, _ = mm_ff1(h2, wg_ff1, [])
    w_ff2 = (got[0] if on_mesh else wg_ff2).reshape(D_FF, D)
    dy, df, d_gfpost, loss = ff2_loss(rl, w_ff2, x1, target, g_fpost)

    half_cols = pl.BlockSpec((D, D // 2), lambda i, j: (0, j))
    d_wff2 = mm_tn("dw_ff2", rl, df, D // 2, D, (D_FF, D), pl.BlockSpec((D // 2, D), lambda i, j: (i, 0)))
    da = ff2_bwd(df, w_ff2, a)
    d_wff1 = mm_tn("dw_ff1", h2, da, D, D // 2, (N_CHIPS, D, D),
                   pl.BlockSpec((None, D, D // 2), lambda i, j: (j // 2, 0, j % 2)))
    d_ff = [d_wff1, d_wff2.reshape(N_CHIPS, D, D)]
    dx1, do, d_gfpre, d_gpost, recv_ff = ff1_bwd_norms(da, wg_ff1, x1, o, dy, g_fpre, g_post, d_ff if on_mesh else [])
    d_wout = mm_tn("dw_out", merged, do, D, D // 2, (D, D), half_cols)
    dpa, dpb, dga, dgb, d_bg = out_bwd_gates(do, w_out, pa, pb, z, bg)
    d_wa = mm_tn("dw_a", ya, dpa, D, D // 2, (D, D), half_cols)
    d_wb = mm_tn("dw_b", yb, dpb, D, D // 2, (D, D), half_cols)
    dya = mm_nt("dy_a", dpa, wg_a.reshape(D, D))
    dyb = mm_nt("dy_b", dpb, wg_b.reshape(D, D))
    d_proj = [d_wa.reshape(N_CHIPS, D // N_CHIPS, D), d_wb.reshape(N_CHIPS, D // N_CHIPS, D),
              d_wout.reshape(N_CHIPS, D // N_CHIPS, D)]
    du, dv, d_ws, d_bs, d_lng, d_lnb, recv_proj = gating_bwd(z, dya, ln_g, ln_b, w_s, bs_t, d_proj if on_mesh else [])
    early = d_proj + d_ff
    parts_early = add(BIG[1:], early, list(recv_proj) + list(recv_ff)) if on_mesh else []
    small = dict(b_gate=d_bg, ln_v_g=d_lng, ln_v_b=d_lnb, w_s=d_ws, b_s=d_bs[:, 0, :],
                 norm_mix_post=d_gpost, norm_ffn_pre=d_gfpre, norm_ffn_post=d_gfpost)
    packed = pack_small(dict(small, norm_mix_pre=jnp.zeros((1, D), F32)), loss, where) if on_mesh else None
    dq, dk, dvb, got_early, packed = attn_bwd(z, yb, dyb, lse, logc, ka, kb, parts_early, packed)
    dz = jnp.concatenate([du, dv, dq, dk, dvb, dga, dgb], axis=1)
    if on_mesh:
        for_sibling, _ = dw_in_half("dw_in_sibling", h, dz, 1 - core, [])
        mine, from_sibling = dw_in_half("dw_in_mine", h, dz, core, [for_sibling])
        d_win = None
        parts_late = [add_halves("add_w_in", mine, from_sibling[0], jnp.zeros((1,), jnp.int32), 256)]
    else:
        half = IN_SHARD // 2
        d_win = mm_tn("dw_in", h, dz, D, half, (N_CHIPS, D, IN_SHARD),
                      pl.BlockSpec((None, D, half), lambda i, j: (j // 2, 0, j % 2)))
        parts_late = []
    started = scatter_start(parts_late[0]) if on_mesh else None
    dx, d_gpre, _, _ = in_bwd_norm(dz, wg_in, x, dx1, g_pre + started[8][0:1, 0:1] if on_mesh else g_pre, [])
    small["norm_mix_pre"] = d_gpre
    return loss[0, 0], dx, [d_win] + early, small, parts_early, list(got_early), packed, started


BIG = ("w_in", "w_a_proj", "w_b_proj", "w_out", "w_ff1", "w_ff2")
SMALL = ("norm_mix_pre", "ln_v_g", "ln_v_b", "b_s", "norm_mix_post", "norm_ffn_pre", "norm_ffn_post", "w_s", "b_gate")
ORDER = ("norm_mix_pre", "w_in", "b_gate", "ln_v_g", "ln_v_b", "w_s", "b_s", "w_a_proj", "w_b_proj", "w_out",
         "norm_mix_post", "norm_ffn_pre", "w_ff1", "w_ff2", "norm_ffn_post")
VEC_ROWS = D // 128
WS_ROW = 7 * VEC_ROWS
BG_ROW = WS_ROW + GROUPS * CHUNK
LOSS_ROW = BG_ROW + 2 * VEC_ROWS
PACK_ROWS = LOSS_ROW + 8


def pack_small(small, loss, where):
    vectors = [small[n] for n in SMALL[:7]]
    operands = vectors + [small["w_s"], small["b_gate"], loss]

    def body(where_ref, *refs):
        out = refs[-1]
        ws_ref, bg_ref, loss_ref = refs[7:10]
        for i, n in enumerate(SMALL[:7]):
            if n == "b_s":
                out[i * VEC_ROWS:(i + 1) * VEC_ROWS, :] = refs[i][...]
            else:
                for j in range(VEC_ROWS):
                    out[i * VEC_ROWS + j:i * VEC_ROWS + j + 1, :] = refs[i][:, j * 128:(j + 1) * 128]
        for g in range(GROUPS):
            out[WS_ROW + g * CHUNK:WS_ROW + (g + 1) * CHUNK, :] = ws_ref[g]
        for r in range(2):
            for j in range(VEC_ROWS):
                row = BG_ROW + r * VEC_ROWS + j
                out[row:row + 1, :] = bg_ref[r:r + 1, j * 128:(j + 1) * 128]
        lane = lax.broadcasted_iota(jnp.int32, (8, 128), 1)
        sub = lax.broadcasted_iota(jnp.int32, (8, 128), 0)
        out[LOSS_ROW:LOSS_ROW + 8, :] = jnp.where((lane == 0) & (sub == 0), loss_ref[...], 0.0)

    return pl.pallas_call(
        body, name="pack_small",
        grid_spec=pltpu.PrefetchScalarGridSpec(
            num_scalar_prefetch=1, grid=(1,), in_specs=[_full(a.shape) for a in operands],
            out_specs=pl.BlockSpec((None, PACK_ROWS, 128), lambda i, w: (w[0], w[1], 0))),
        out_shape=jax.ShapeDtypeStruct((N_CHIPS, 2 * PACK_ROWS, 128), F32), compiler_params=_params("arbitrary"),
    )(where, *operands)


def pack_vector(vec, where):
    def body(where_ref, v_ref, out):
        for j in range(VEC_ROWS):
            out[j:j + 1, :] = v_ref[:, j * 128:(j + 1) * 128]

    return pl.pallas_call(
        body, name="pack_vector",
        grid_spec=pltpu.PrefetchScalarGridSpec(
            num_scalar_prefetch=1, grid=(1,), in_specs=[_full(vec.shape)],
            out_specs=pl.BlockSpec((None, VEC_ROWS, 128), lambda i, w: (w[0], w[1], 0))),
        out_shape=jax.ShapeDtypeStruct((N_CHIPS, 2 * VEC_ROWS, 128), F32), compiler_params=_params("arbitrary"),
    )(where, vec)


def adamw_small(gathered, first, chip, w, m, v):
    shapes = {n: (1, D) for n in SMALL}
    shapes.update(b_s=(GROUPS, CHUNK), w_s=(GROUPS * CHUNK, CHUNK), b_gate=(2, D // N_CHIPS))
    flat = lambda t: [t[n].reshape(shapes[n]) for n in SMALL]
    per = D // N_CHIPS // 128

    def body(chip_ref, all_ref, first_ref, *refs):
        params, outs = refs[:27], refs[27:]
        sub = lax.broadcasted_iota(jnp.int32, (VEC_ROWS, 128), 0)
        sum_ref = outs[36]
        total = all_ref[0, 0:PACK_ROWS, :]
        head = first_ref[0, 0:VEC_ROWS, :]
        for k in range(1, 2 * N_CHIPS):
            total = total + all_ref[k // 2, (k % 2) * PACK_ROWS:(k % 2 + 1) * PACK_ROWS, :]
            head = head + first_ref[k // 2, (k % 2) * VEC_ROWS:(k % 2 + 1) * VEC_ROWS, :]
        sum_ref[...] = total
        sum_ref[0:VEC_ROWS, :] = head

        def gate_row(r):
            rows = sum_ref[BG_ROW + r * VEC_ROWS:BG_ROW + (r + 1) * VEC_ROWS, :]
            return jnp.concatenate([jnp.sum(jnp.where(sub == per * chip_ref[0] + j, rows, 0.0), axis=0, keepdims=True)
                                    for j in range(per)], axis=1)

        for i, n in enumerate(SMALL):
            if n == "b_s":
                g = sum_ref[i * VEC_ROWS:(i + 1) * VEC_ROWS, :]
            elif n == "w_s":
                g = sum_ref[WS_ROW:BG_ROW, :]
            elif n == "b_gate":
                g = jnp.concatenate([gate_row(0), gate_row(1)], axis=0)
            else:
                g = jnp.concatenate([sum_ref[i * VEC_ROWS + j:i * VEC_ROWS + j + 1, :] for j in range(VEC_ROWS)],
                                    axis=1)
            delta, nm, nv = _adamw_math(params[i][...], g, params[9 + i][...], params[18 + i][...])
            outs[4 * i][...], outs[4 * i + 1][...], outs[4 * i + 2][...], outs[4 * i + 3][...] = g, delta, nm, nv

    vm = pl.BlockSpec(memory_space=pltpu.VMEM)
    res = pl.pallas_call(
        body, name="adamw_small",
        in_specs=[pl.BlockSpec(memory_space=pltpu.SMEM)] + [vm] * 29, out_specs=[vm] * 37,
        out_shape=[jax.ShapeDtypeStruct(shapes[n], F32) for n in SMALL for _ in range(4)]
        + [jax.ShapeDtypeStruct((PACK_ROWS, 128), F32)],
        compiler_params=_params(),
    )(chip, gathered, first, *flat(w), *flat(m), *flat(v))
    new = {n: tuple(r.reshape(w[n].shape) for r in res[4 * i:4 * i + 4]) for i, n in enumerate(SMALL)}
    return new, res[36][LOSS_ROW, 0]


def kernel(x, norm_mix_pre, w_in, b_gate, ln_v_g, ln_v_b, w_s, b_s, w_a_proj, w_b_proj, w_out, norm_mix_post, norm_ffn_pre, w_ff1, w_ff2, norm_ffn_post, loss_target, m_norm_mix_pre, m_w_in, m_b_gate, m_ln_v_g, m_ln_v_b, m_w_s, m_b_s, m_w_a_proj, m_w_b_proj, m_w_out, m_norm_mix_post, m_norm_ffn_pre, m_w_ff1, m_w_ff2, m_norm_ffn_post, v_norm_mix_pre, v_w_in, v_b_gate, v_ln_v_g, v_ln_v_b, v_w_s, v_b_s, v_w_a_proj, v_w_b_proj, v_w_out, v_norm_mix_post, v_norm_ffn_pre, v_w_ff1, v_w_ff2, v_norm_ffn_post):
    w = dict(norm_mix_pre=norm_mix_pre, w_in=w_in, b_gate=b_gate, ln_v_g=ln_v_g, ln_v_b=ln_v_b, w_s=w_s, b_s=b_s,
             w_a_proj=w_a_proj, w_b_proj=w_b_proj, w_out=w_out, norm_mix_post=norm_mix_post,
             norm_ffn_pre=norm_ffn_pre, w_ff1=w_ff1, w_ff2=w_ff2, norm_ffn_post=norm_ffn_post)
    m = dict(norm_mix_pre=m_norm_mix_pre, w_in=m_w_in, b_gate=m_b_gate, ln_v_g=m_ln_v_g, ln_v_b=m_ln_v_b, w_s=m_w_s,
             b_s=m_b_s, w_a_proj=m_w_a_proj, w_b_proj=m_w_b_proj, w_out=m_w_out, norm_mix_post=m_norm_mix_post,
             norm_ffn_pre=m_norm_ffn_pre, w_ff1=m_w_ff1, w_ff2=m_w_ff2, norm_ffn_post=m_norm_ffn_post)
    v = dict(norm_mix_pre=v_norm_mix_pre, w_in=v_w_in, b_gate=v_b_gate, ln_v_g=v_ln_v_g, ln_v_b=v_ln_v_b, w_s=v_w_s,
             b_s=v_b_s, w_a_proj=v_w_a_proj, w_b_proj=v_w_b_proj, w_out=v_w_out, norm_mix_post=v_norm_mix_post,
             norm_ffn_pre=v_norm_ffn_pre, w_ff1=v_w_ff1, w_ff2=v_w_ff2, norm_ffn_post=v_norm_ffn_post)
    chip = 2 * lax.axis_index("x") + lax.axis_index("y")
    core = lax.axis_index("c")

    where = jnp.stack([chip, core]).astype(jnp.int32)
    wg_in = place_shard("place_w_in", w_in[0], where, BF16, 256)
    bg_all = place_shard("place_b_gate", jnp.pad(b_gate[0], ((0, 14), (0, 0))), where, F32, 16)
    vecs = (norm_mix_pre, ln_v_g, ln_v_b, norm_mix_post, norm_ffn_pre, norm_ffn_post)
    loss, dx, _, small, parts, got, packed, started = local_step(
        x[0], loss_target[0], vecs, w_s[0], b_s[0].T, bg_all, wg_in, [w[n][0] for n in BIG[1:]],
        core=jnp.reshape(core, (1,)).astype(jnp.int32),
        order=jnp.stack([chip, chip ^ 2, chip ^ 1, chip ^ 3]).astype(jnp.int32), where=where)

    halves = [sum_chips("sum_" + n, p, r, where, min(p.shape[1], 256), started[8])
              for n, p, r in zip(BIG[1:], parts, got)]
    joined, _ = join_halves(halves)
    grads = dict(zip(BIG[1:], joined))
    new = {}

    def update(n):
        shape = w[n].shape
        res = adamw("adamw_" + n, w[n][0], grads[n], m[n][0], v[n][0], min(shape[1], 256))
        new[n] = tuple(r.reshape(shape) for r in res)

    for n in BIG[1:]:
        update(n)
    p_in, got_in = scatter_wait(started, [new["w_ff2"][1], dx])
    (grads["w_in"],), first = join_halves([sum_chips("sum_w_in", p_in, got_in, where, 256)],
                                          pack_vector(small["norm_mix_pre"], where))
    update("w_in")
    small_new, loss = adamw_small(packed, first, jnp.reshape(chip, (1,)).astype(jnp.int32), w, m, v)
    new.update(small_new)

    outs = [loss, dx[None]]
    for i in range(4):
        outs += [new[n][i] for n in ORDER]
    return tuple(outs)
```

```python
import functools
import math
import typing

import numpy as np
import jax
import jax.numpy as jnp
from jax import lax
from jax.experimental import pallas as pl
from jax.experimental.pallas import tpu as pltpu

F32 = jnp.float32
BF16 = jnp.bfloat16
MESH = pl.DeviceIdType.MESH

D = 1024
EPS = 1e-6
CHUNK = 128
GROUPS = 8
HEADS = 16
HEAD_DIM = 64
ATT_T = 256
ATT_GROUP = 8
ATT_BWD_GROUP = 4
N_CHIPS = 4
D_FF = 4 * D
IN_COLS = 7 * D
IN_SHARD = IN_COLS // N_CHIPS
MASKED = -1e30
VMEM_LIMIT = 56 * 2 ** 20

ADAM_LR, ADAM_B1, ADAM_B2, ADAM_EPS, ADAM_WD, ADAM_STEP = 0.001, 0.9, 0.999, 1e-08, 0.01, 10

NN = (((1,), (0,)), ((), ()))
NT = (((1,), (1,)), ((), ()))
TN = (((0,), (0,)), ((), ()))


def _dot(a, b, dims=NN):
    return lax.dot_general(a, b, dims, preferred_element_type=F32)


def _params(*sem, communicates=False):
    return pltpu.CompilerParams(dimension_semantics=sem or None, vmem_limit_bytes=VMEM_LIMIT,
                                has_side_effects=communicates)


def _rows(tr, c, col=0):
    return pl.BlockSpec((tr, c), lambda i: (i, col))


def _full(shape):
    n = len(shape)
    return pl.BlockSpec(shape, lambda *_: (0,) * n)


def _gelu(x):
    k = math.sqrt(2.0 / math.pi)
    return 0.5 * x * (1.0 + jnp.tanh(k * (x + 0.044715 * x * x * x)))


def _gelu_and_grad(x):
    k = math.sqrt(2.0 / math.pi)
    t = jnp.tanh(k * (x + 0.044715 * x * x * x))
    g = 0.5 * x * (1.0 + t)
    dg = 0.5 * (1.0 + t) + 0.5 * x * (1.0 - t * t) * (k * (1.0 + 3.0 * 0.044715 * x * x))
    return g, dg


def _sigmoid(x):
    return 1.0 / (1.0 + jnp.exp(-x))


def _rms(x):
    r = lax.rsqrt(jnp.mean(x * x, axis=-1, keepdims=True) + EPS)
    return x * r, r


def _rms_bwd(dn, xhat, r):
    return r * (dn - xhat * jnp.mean(dn * xhat, axis=-1, keepdims=True))


def norm_pre(x, g):
    s = x.shape[0]
    tr = 512

    def body(x_ref, g_ref, h_ref):
        xhat, _ = _rms(x_ref[...])
        h_ref[...] = (xhat * g_ref[...]).astype(BF16)

    return pl.pallas_call(
        body, name="norm_pre", grid=(s // tr,),
        in_specs=[_rows(tr, D), _full((1, D))], out_specs=_rows(tr, D),
        out_shape=jax.ShapeDtypeStruct((s, D), BF16), compiler_params=_params("parallel"),
    )(x, g)


def mm_in(h, wg, order, staged, casting=()):
    s = h.shape[0]
    tm, tn = 1024, IN_SHARD // 2
    per = IN_SHARD // tn
    m = len(casting)
    nj, ni = N_CHIPS * per, s // tm
    cast_steps = per * ni

    def body(order_ref, *refs):
        a_ref = refs[0]
        cast_in = refs[2:2 + m]
        o_ref, held = refs[2 + m], refs[3 + m]
        cast_out = refs[4 + m:4 + 2 * m]
        tile, tile_sem = refs[4 + 2 * m:6 + 2 * m]
        sems = refs[6 + 2 * m:]
        j, i = pl.program_id(0), pl.program_id(1)

        @pl.when(j * ni + i < cast_steps)
        def _():
            for src, dst in zip(cast_in, cast_out):
                dst[...] = src[...].astype(BF16)

        def fetch(t):
            chip = order_ref[t // per]
            return pltpu.make_async_copy(held.at[chip, :, pl.ds((t % per) * tn, tn)], tile.at[t % 2],
                                         tile_sem.at[t % 2])

        if staged:
            near = _gather_phases([held], *sems[:2], [(0, D, (0, 1))])
            far = _relay_phases(held, *sems[2:])

        @pl.when(i == 0)
        def _():
            @pl.when(j == 0)
            def _():
                if staged:
                    near[0]()
                fetch(0).start()

            fetch(j).wait()
            ahead = j + 1 < nj
            if staged:
                ahead = ahead & (j + 1 != per) & (j + 1 != 3 * per)
            pl.when(ahead)(lambda: fetch(j + 1).start())

        rows = pl.ds(pl.multiple_of(i * tm, tm), tm)
        o_ref[...] = _dot(a_ref[rows, :], tile[j % 2]).astype(BF16)

        if staged:
            @pl.when((i == ni - 1) & (j == per - 1))
            def _():
                near[1]()
                near[2]()
                far[0]()
                fetch(per).start()

            @pl.when((i == ni - 1) & (j == 3 * per - 1))
            def _():
                far[1]()
                far[2]()
                fetch(3 * per).start()

    def cast_block(j, i, o):
        return jnp.minimum(j * ni + i, cast_steps - 1)

    out = pl.pallas_call(
        body, name="mm_in",
        grid_spec=pltpu.PrefetchScalarGridSpec(
            num_scalar_prefetch=1, grid=(nj, ni),
            in_specs=[pl.BlockSpec((s, D), lambda j, i, o: (0, 0)), ANY]
            + [pl.BlockSpec((a.shape[0] // cast_steps, a.shape[1]), lambda j, i, o: (cast_block(j, i, o), 0))
               for a in casting],
            out_specs=[pl.BlockSpec((tm, tn), lambda j, i, o: (i, o[j // per] * per + j % per)), ANY]
            + [pl.BlockSpec((None, a.shape[0] // cast_steps, a.shape[1]),
                            lambda j, i, o: (o[0], cast_block(j, i, o), 0)) for a in casting],
            scratch_shapes=[pltpu.VMEM((2, D, tn), BF16), pltpu.SemaphoreType.DMA((2,))]
            + (_gather_sems(1) + _relay_sems() if staged else [])),
        out_shape=[jax.ShapeDtypeStruct((s, IN_COLS), BF16), jax.ShapeDtypeStruct(wg.shape, wg.dtype)]
        + [jax.ShapeDtypeStruct((N_CHIPS,) + a.shape, BF16) for a in casting],
        input_output_aliases={2: 1},
        compiler_params=_params("arbitrary", "arbitrary", communicates=staged),
    )(order, h, wg, *casting)
    return out[0], out[1], out[2:]


def _tril_ws(ws_ref, g):
    r = lax.broadcasted_iota(jnp.int32, (CHUNK, CHUNK), 0)
    c = lax.broadcasted_iota(jnp.int32, (CHUNK, CHUNK), 1)
    return jnp.where(c <= r, ws_ref[g], 0.0).astype(BF16)


def _layer_norm(v):
    mu = jnp.mean(v, axis=-1, keepdims=True)
    d = v - mu
    rstd = lax.rsqrt(jnp.mean(d * d, axis=-1, keepdims=True) + EPS)
    return d * rstd, rstd


def gating_fwd(z, ln_g, ln_b, w_s, bs_t, gathering):
    s = z.shape[0]
    n = len(gathering)
    steps = s // CHUNK

    def body(*refs):
        u_ref, v_ref, lg_ref, lb_ref, ws_ref, bst_ref = refs[:6]
        ya_ref = refs[6 + n]
        ci = pl.program_id(0)
        if n:
            send, pass_on, finish = _gather_phases(refs[7 + n:7 + 2 * n], *refs[7 + 2 * n:], _spans(gathering))
            pl.when(ci == 0)(send)
        ug = _gelu(u_ref[...].astype(F32))
        vhat, _ = _layer_norm(_gelu(v_ref[...].astype(F32)))
        vn = (vhat * lg_ref[...] + lb_ref[...]).astype(BF16)
        for g in range(GROUPS):
            cols = slice(g * CHUNK, (g + 1) * CHUNK)
            mixed = _dot(_tril_ws(ws_ref, g), vn[:, cols]) + bst_ref[:, g:g + 1]
            ya_ref[:, cols] = (ug[:, cols] * mixed).astype(BF16)
        if n:
            pl.when(ci == steps - 1)(pass_on)
            pl.when(ci == steps - 1)(finish)

    out = pl.pallas_call(
        body, name="gating_fwd", grid=(steps,),
        in_specs=[_rows(CHUNK, D, 0), _rows(CHUNK, D, 1), _full((1, D)), _full((1, D)),
                  _full((GROUPS, CHUNK, CHUNK)), _full((CHUNK, GROUPS))] + [ANY] * n,
        out_specs=[_rows(CHUNK, D)] + [ANY] * n,
        out_shape=[jax.ShapeDtypeStruct((s, D), BF16)]
        + [jax.ShapeDtypeStruct(a.shape, a.dtype) for a in _arrays(gathering)],
        input_output_aliases={6 + w: 1 + w for w in range(n)},
        scratch_shapes=_gather_sems(n) if n else [],
        compiler_params=_params("arbitrary", communicates=bool(n)),
    )(z, z, ln_g, ln_b, w_s, bs_t, *_arrays(gathering))
    return out[0], out[1:]


def _attn_tables(s):
    nd = s // ATT_T
    r = np.arange(ATT_T)[None, :, None]
    c = np.arange(ATT_T)[None, None, :]
    delta = np.arange(nd)[:, None, None] * ATT_T + r - c
    count = np.zeros(delta.shape, np.int64)
    for window, dilation in ((128, 1), (512, 4), (2048, 16)):
        count += (delta >= 0) & (delta % dilation == 0) & (delta <= window)
    logc = np.where(count > 0, np.log(np.maximum(count, 1)), MASKED)
    return jnp.asarray(logc, F32)


AUG = 3


def _split3_np(x):
    terms, rest = [], np.asarray(x, np.float64)
    for _ in range(AUG):
        term = np.asarray(rest.astype(jnp.bfloat16), np.float64)
        terms.append(term)
        rest = rest - term
    return terms


def _split3(x):
    terms, rest = [], x
    for _ in range(AUG):
        term = rest.astype(BF16).astype(F32)
        terms.append(term)
        rest = rest - term
    return terms


def _alibi_tables(s):
    nb = s // ATT_T
    slopes = np.exp2(-8.0 * np.arange(1, HEADS + 1, dtype=np.float64) / HEADS)
    ka = np.zeros((HEADS // 2, 2, ATT_T, 128), np.float32)
    kb = np.zeros((HEADS // 2, 2, nb, 128), np.float32)
    for p in range(HEADS // 2):
        for e in range(2):
            base = HEAD_DIM * (1 - e)
            for a, term in enumerate(_split3_np(slopes[2 * p + e] * np.arange(ATT_T))):
                ka[p, e, :, base + a] = term
            for a, term in enumerate(_split3_np(slopes[2 * p + e] * ATT_T * np.arange(nb))):
                kb[p, e, :, base + AUG + a] = term
            ka[p, e, :, base + 2 * AUG:base + 3 * AUG] = 1.0
    return jnp.asarray(ka), jnp.asarray(kb)


def _head_masks():
    lane = lax.broadcasted_iota(jnp.int32, (1, 128), 1)
    first = lane < HEAD_DIM

    def ones(e, n):
        base = HEAD_DIM * (1 - e)
        return ((lane >= base) & (lane < base + n)).astype(F32)

    return first, lane, ones


def _place3(lane, at, terms, other):
    for a, term in enumerate(terms):
        other = jnp.where(lane == at + a, term, other)
    return other


def attn_fwd(z, logc, ka, kb, gathering):
    s = z.shape[0]
    nq = s // ATT_T
    t = ATT_T
    n = len(gathering)
    grp = ATT_GROUP
    ngrp = HEADS // 2 // grp
    wide = 128 * grp
    qcol, kcol, vcol = 2 * D // wide, 3 * D // wide, 4 * D // wide

    def body(*refs):
        q_ref, k_ref, v_ref, lc_ref, ka_ref, kb_ref = refs[:6]
        y_ref, lse_ref = refs[6 + n:8 + n]
        q_s, k_s, v_s, m_s, l_s, acc_s = refs[8 + 2 * n:14 + 2 * n]
        gi, qi = pl.program_id(0), pl.program_id(1)
        first, lane, ones = _head_masks()
        if n:
            send, pass_on, finish = _gather_phases(refs[8 + n:8 + 2 * n], *refs[14 + 2 * n:], _spans(gathering))
            pl.when((gi == 0) & (qi == 0))(send)

        @pl.when(qi == 0)
        def _():
            sel = jnp.broadcast_to(first.astype(F32), (t, 128))
            for pr in range(grp):
                cols = slice(pr * 128, (pr + 1) * 128)
                for jb in range(nq):
                    kj = k_ref[jb * t:(jb + 1) * t, cols].astype(F32)
                    vj = v_ref[jb * t:(jb + 1) * t, cols].astype(F32)
                    k_s[pr, 0, jb] = jnp.where(first, kj, ka_ref[pr, 0] + kb_ref[pr, 0, jb:jb + 1, :]).astype(BF16)
                    k_s[pr, 1, jb] = jnp.where(first, ka_ref[pr, 1] + kb_ref[pr, 1, jb:jb + 1, :], kj).astype(BF16)
                    v_s[pr, jb, 0:t, 0:128] = jnp.where(first, vj, 0.0).astype(BF16)
                    v_s[pr, jb, t:2 * t, 0:128] = jnp.where(first, 0.0, vj).astype(BF16)
                    v_s[pr, jb, 0:t, 128:256] = sel.astype(BF16)
                    v_s[pr, jb, t:2 * t, 128:256] = (1.0 - sel).astype(BF16)

        for pr in range(grp):
            q = q_ref[:, pr * 128:(pr + 1) * 128].astype(F32) * (1.0 / math.sqrt(HEAD_DIM))
            q_s[pr, 0] = jnp.where(first, q, ones(0, 2 * AUG)).astype(BF16)
            q_s[pr, 1] = jnp.where(first, ones(1, 2 * AUG), q).astype(BF16)
        m_s[...] = jnp.full_like(m_s, MASKED)
        l_s[...] = jnp.zeros_like(l_s)
        acc_s[...] = jnp.zeros_like(acc_s)

        def scores(j):
            return tuple(_dot(q_s[pr, e], k_s[pr, e, j], NT) for pr in range(grp) for e in range(2))

        def step(j, carry):
            softmax_block(j, scores(j))
            return carry

        def softmax_block(j, u):
            lc = lc_ref[qi - j]
            for pr in range(grp):
                u0 = u[2 * pr] + lc
                u1 = u[2 * pr + 1] + lc
                m0, m1 = m_s[pr, 0], m_s[pr, 1]
                n0 = jnp.maximum(m0, jnp.max(u0, axis=-1, keepdims=True))
                n1 = jnp.maximum(m1, jnp.max(u1, axis=-1, keepdims=True))
                m_s[pr, 0], m_s[pr, 1] = n0, n1
                p = jnp.concatenate([jnp.exp(u0 - jnp.concatenate([n0, n0], axis=1)).astype(BF16),
                                     jnp.exp(u1 - jnp.concatenate([n1, n1], axis=1)).astype(BF16)], axis=1)
                pv = _dot(p, v_s[pr, j])
                alpha = jnp.where(first, jnp.exp(m0 - n0), jnp.exp(m1 - n1))
                acc_s[pr] = acc_s[pr] * alpha + pv[:, 0:128]
                l_s[pr] = l_s[pr] * alpha + pv[:, 128:256]

        lax.fori_loop(0, qi + 1, step, 0)
        for pr in range(grp):
            cols = slice(pr * 128, (pr + 1) * 128)
            y_ref[:, cols] = (acc_s[pr] / l_s[pr]).astype(BF16)
            lse_ref[:, cols] = jnp.where(first, m_s[pr, 0], m_s[pr, 1]) + jnp.log(l_s[pr])
        if n:
            pl.when((gi == ngrp - 1) & (qi == nq - 1))(pass_on)
            pl.when((gi == ngrp - 1) & (qi == nq - 1))(finish)

    out = pl.pallas_call(
        body, name="attn_fwd", grid=(ngrp, nq),
        in_specs=[pl.BlockSpec((t, wide), lambda g, i: (i, qcol + g)),
                  pl.BlockSpec((s, wide), lambda g, i: (0, kcol + g)),
                  pl.BlockSpec((s, wide), lambda g, i: (0, vcol + g)),
                  _full((nq, t, t)),
                  pl.BlockSpec((grp, 2, t, 128), lambda g, i: (g, 0, 0, 0)),
                  pl.BlockSpec((grp, 2, nq, 128), lambda g, i: (g, 0, 0, 0))] + [ANY] * n,
        out_specs=[pl.BlockSpec((t, wide), lambda g, i: (i, g)), pl.BlockSpec((t, wide), lambda g, i: (i, g))]
        + [ANY] * n,
        out_shape=[jax.ShapeDtypeStruct((s, D), BF16), jax.ShapeDtypeStruct((s, D), F32)]
        + [jax.ShapeDtypeStruct(a.shape, a.dtype) for a in _arrays(gathering)],
        input_output_aliases={6 + w: 2 + w for w in range(n)},
        scratch_shapes=[pltpu.VMEM((grp, 2, t, 128), BF16), pltpu.VMEM((grp, 2, nq, t, 128), BF16),
                        pltpu.VMEM((grp, nq, 2 * t, 256), BF16), pltpu.VMEM((grp, 2, t, 128), F32),
                        pltpu.VMEM((grp, t, 128), F32), pltpu.VMEM((grp, t, 128), F32)]
        + (_gather_sems(n) if n else []),
        compiler_params=_params("arbitrary", "arbitrary", communicates=bool(n)),
    )(z, z, z, logc, ka, kb, *_arrays(gathering))
    return out[0], out[1], out[2:]


def proj_merge(ya, yb, wa, wb, z, bg, gathering):
    s = ya.shape[0]
    tm = 512
    n = len(gathering)
    steps = s // tm

    def body(*refs):
        ya_ref, yb_ref, wa_ref, wb_ref, ga_ref, gb_ref, bg_ref = refs[:7]
        mg_ref, pa_ref, pb_ref = refs[7 + n:10 + n]
        i = pl.program_id(0)
        if n:
            send, pass_on, finish = _gather_phases(refs[10 + n:10 + 2 * n], *refs[10 + 2 * n:], _spans(gathering))
            pl.when(i == 0)(send)
            pl.when(i == steps - 1)(pass_on)
        pa = _dot(ya_ref[...], wa_ref[...])
        pb = _dot(yb_ref[...], wb_ref[...])
        sa = _sigmoid(ga_ref[...] + bg_ref[0:1, :])
        sb = _sigmoid(gb_ref[...] + bg_ref[1:2, :])
        mg_ref[...] = (sa * pa + sb * pb).astype(BF16)
        pa_ref[...] = pa.astype(BF16)
        pb_ref[...] = pb.astype(BF16)
        if n:
            pl.when(i == steps - 1)(finish)

    out = jax.ShapeDtypeStruct((s, D), BF16)
    res = pl.pallas_call(
        body, name="proj_merge", grid=(steps,),
        in_specs=[_rows(tm, D), _rows(tm, D), _full((D, D)), _full((D, D)),
                  _rows(tm, D, 5), _rows(tm, D, 6), _full((2, D))] + [ANY] * n,
        out_specs=[_rows(tm, D)] * 3 + [ANY] * n,
        out_shape=[out] * 3 + [jax.ShapeDtypeStruct(a.shape, a.dtype) for a in _arrays(gathering)],
        input_output_aliases={7 + w: 3 + w for w in range(n)},
        scratch_shapes=_gather_sems(n) if n else [],
        compiler_params=_params("arbitrary", communicates=bool(n)),
    )(ya, yb, wa, wb, z, z, bg, *_arrays(gathering))
    return res[0], res[1], res[2], res[3:]


def out_norm(merged, w_out, x, g_post, g_fpre, gathering):
    s = x.shape[0]
    tm = 512
    n = len(gathering)
    steps = s // tm

    def body(*refs):
        mg_ref, w_ref, x_ref, gp_ref, gf_ref = refs[:5]
        o_ref, x1_ref, h2_ref = refs[5 + n:8 + n]
        i = pl.program_id(0)
        if n:
            send, pass_on, finish = _gather_phases(refs[8 + n:8 + 2 * n], *refs[8 + 2 * n:], _spans(gathering))
            pl.when(i == 0)(send)
            pl.when(i == steps - 1)(pass_on)
        o = _dot(mg_ref[...], w_ref[...])
        ohat, _ = _rms(o)
        x1 = x_ref[...] + ohat * gp_ref[...]
        x1hat, _ = _rms(x1)
        o_ref[...] = o
        x1_ref[...] = x1
        h2_ref[...] = (x1hat * gf_ref[...]).astype(BF16)
        if n:
            pl.when(i == steps - 1)(finish)

    res = pl.pallas_call(
        body, name="out_norm", grid=(steps,),
        in_specs=[_rows(tm, D), _full((D, D)), _rows(tm, D), _full((1, D)), _full((1, D))] + [ANY] * n,
        out_specs=[_rows(tm, D)] * 3 + [ANY] * n,
        out_shape=[jax.ShapeDtypeStruct((s, D), F32), jax.ShapeDtypeStruct((s, D), F32),
                   jax.ShapeDtypeStruct((s, D), BF16)]
        + [jax.ShapeDtypeStruct(a.shape, a.dtype) for a in _arrays(gathering)],
        input_output_aliases={5 + w: 3 + w for w in range(n)},
        scratch_shapes=_gather_sems(n) if n else [],
        compiler_params=_params("arbitrary", communicates=bool(n)),
    )(merged, w_out, x, g_post, g_fpre, *_arrays(gathering))
    return res[0], res[1], res[2], res[3:]


def mm_ff1(h2, wg, gathering):
    s = h2.shape[0]
    tm = 1024
    n = len(gathering)
    ni = s // tm

    def body(*refs):
        a_ref, b_ref = refs[:2]
        o_ref, r_ref = refs[2 + n:4 + n]
        i, j = pl.program_id(0), pl.program_id(1)
        if n:
            send, pass_on, finish = _gather_phases(refs[4 + n:4 + 2 * n], *refs[4 + 2 * n:], _spans(gathering))
            pl.when((i == 0) & (j == 0))(send)
            pl.when((i == ni - 1) & (j == N_CHIPS // 2))(pass_on)
        a = _dot(a_ref[...], b_ref[...])
        o_ref[...] = a.astype(BF16)
        r = jnp.maximum(a, 0.0)
        r_ref[...] = (r * r).astype(BF16)
        if n:
            pl.when((i == ni - 1) & (j == N_CHIPS - 1))(finish)

    res = pl.pallas_call(
        body, name="mm_ff1", grid=(ni, N_CHIPS),
        in_specs=[pl.BlockSpec((tm, D), lambda i, j: (i, 0)), pl.BlockSpec((None, D, D), lambda i, j: (j, 0, 0))]
        + [ANY] * n,
        out_specs=[pl.BlockSpec((tm, D), lambda i, j: (i, j))] * 2 + [ANY] * n,
        out_shape=[jax.ShapeDtypeStruct((s, D_FF), BF16), jax.ShapeDtypeStruct((s, D_FF), BF16)]
        + [jax.ShapeDtypeStruct(a.shape, a.dtype) for a in _arrays(gathering)],
        input_output_aliases={2 + w: 2 + w for w in range(n)},
        scratch_shapes=_gather_sems(n) if n else [],
        compiler_params=_params("arbitrary", "arbitrary", communicates=bool(n)),
    )(h2, wg, *_arrays(gathering))
    return res[0], res[1], res[2:]


def ff2_loss(rl, w_ff2, x1, target, g_fpost):
    s = x1.shape[0]
    tm = 256

    def body(rl_ref, w_ref, x1_ref, t_ref, g_ref, dy_ref, df_ref, dg_ref, loss_ref):
        @pl.when(pl.program_id(0) == 0)
        def _():
            dg_ref[...] = jnp.zeros_like(dg_ref)
            loss_ref[...] = jnp.zeros_like(loss_ref)

        f = _dot(rl_ref[...], w_ref[...])
        fhat, r = _rms(f)
        err = x1_ref[...] + fhat * g_ref[...] - t_ref[...]
        loss_ref[...] += 0.5 * jnp.sum(jnp.mean(err * err, axis=-1, keepdims=True), axis=0, keepdims=True)
        dy = err * (1.0 / D)
        dy_ref[...] = dy
        dg_ref[...] += jnp.sum(dy * fhat, axis=0, keepdims=True)
        df_ref[...] = _rms_bwd(dy * g_ref[...], fhat, r).astype(BF16)

    return pl.pallas_call(
        body, name="ff2_loss", grid=(s // tm,),
        in_specs=[_rows(tm, D_FF), _full((D_FF, D)), _rows(tm, D), _rows(tm, D), _full((1, D))],
        out_specs=[_rows(tm, D), _rows(tm, D), _full((1, D)), _full((1, 1))],
        out_shape=[jax.ShapeDtypeStruct((s, D), F32), jax.ShapeDtypeStruct((s, D), BF16),
                   jax.ShapeDtypeStruct((1, D), F32), jax.ShapeDtypeStruct((1, 1), F32)],
        compiler_params=_params("arbitrary"),
    )(rl, w_ff2, x1, target, g_fpost)


def mm_tn(name, a, b, ta, tb, out_shape, out_spec):
    s = a.shape[0]

    def body(a_ref, b_ref, o_ref):
        o_ref[...] = _dot(a_ref[...], b_ref[...], TN)

    return pl.pallas_call(
        body, name=name, grid=(a.shape[1] // ta, b.shape[1] // tb),
        in_specs=[pl.BlockSpec((s, ta), lambda i, j: (0, i)), pl.BlockSpec((s, tb), lambda i, j: (0, j))],
        out_specs=out_spec, out_shape=jax.ShapeDtypeStruct(out_shape, F32),
        compiler_params=_params("parallel", "parallel"),
    )(a, b)


def mm_nt(name, a, w):
    s = a.shape[0]
    tm = 512

    def body(a_ref, w_ref, o_ref):
        o_ref[...] = _dot(a_ref[...], w_ref[...], NT).astype(BF16)

    return pl.pallas_call(
        body, name=name, grid=(s // tm,), in_specs=[_rows(tm, D), _full((D, D))], out_specs=_rows(tm, D),
        out_shape=jax.ShapeDtypeStruct((s, D), BF16), compiler_params=_params("parallel"),
    )(a, w)


def ff2_bwd(df, w_ff2, a):
    s = df.shape[0]
    tm = 1024

    def body(df_ref, w_ref, a_ref, da_ref):
        drl = _dot(df_ref[...], w_ref[...], NT)
        da_ref[...] = (drl * (2.0 * jnp.maximum(a_ref[...].astype(F32), 0.0))).astype(BF16)

    return pl.pallas_call(
        body, name="ff2_bwd", grid=(s // tm, D_FF // D),
        in_specs=[pl.BlockSpec((tm, D), lambda i, j: (i, 0)), pl.BlockSpec((D, D), lambda i, j: (j, 0)),
                  pl.BlockSpec((tm, D), lambda i, j: (i, j))],
        out_specs=pl.BlockSpec((tm, D), lambda i, j: (i, j)),
        out_shape=jax.ShapeDtypeStruct((s, D_FF), BF16), compiler_params=_params("parallel", "parallel"),
    )(df, w_ff2, a)


def ff1_bwd_norms(da, wg, x1, o, dy, g_fpre, g_post, swapping):
    s = x1.shape[0]
    tm = 256
    n = len(swapping)
    steps = s // tm

    def body(*refs):
        da_ref, w_ref, x1_ref, o_ref, dy_ref, gf_ref, gp_ref = refs[:7]
        dx1_ref, do_ref, dgf_ref, dgp_ref = refs[7 + n:11 + n]
        i = pl.program_id(0)
        if n:
            send, finish = _swap_phases(refs[7:7 + n], refs[11 + n:11 + 2 * n], *refs[11 + 2 * n:])
            pl.when(i == 0)(send)

        @pl.when(i == 0)
        def _():
            dgf_ref[...] = jnp.zeros_like(dgf_ref)
            dgp_ref[...] = jnp.zeros_like(dgp_ref)

        dh2 = _dot(da_ref[:, 0:D], w_ref[0], NT)
        for k in range(1, N_CHIPS):
            dh2 = dh2 + _dot(da_ref[:, k * D:(k + 1) * D], w_ref[k], NT)
        x1hat, r2 = _rms(x1_ref[...])
        dgf_ref[...] += jnp.sum(dh2 * x1hat, axis=0, keepdims=True)
        dx1 = dy_ref[...] + _rms_bwd(dh2 * gf_ref[...], x1hat, r2)
        ohat, r1 = _rms(o_ref[...])
        dgp_ref[...] += jnp.sum(dx1 * ohat, axis=0, keepdims=True)
        dx1_ref[...] = dx1
        do_ref[...] = _rms_bwd(dx1 * gp_ref[...], ohat, r1).astype(BF16)
        if n:
            pl.when(i == steps - 1)(finish)

    res = pl.pallas_call(
        body, name="ff1_bwd_norms", grid=(steps,),
        in_specs=[_rows(tm, D_FF), _full((N_CHIPS, D, D)), _rows(tm, D), _rows(tm, D), _rows(tm, D),
                  _full((1, D)), _full((1, D))] + [ANY] * n,
        out_specs=[_rows(tm, D), _rows(tm, D), _full((1, D)), _full((1, D))] + [ANY] * n,
        out_shape=[jax.ShapeDtypeStruct((s, D), F32), jax.ShapeDtypeStruct((s, D), BF16),
                   jax.ShapeDtypeStruct((1, D), F32), jax.ShapeDtypeStruct((1, D), F32)] + _swap_shapes(swapping),
        scratch_shapes=_swap_sems(n) if n else [],
        compiler_params=_params("arbitrary", communicates=bool(n)),
    )(da, wg, x1, o, dy, g_fpre, g_post, *swapping)
    return res[0], res[1], res[2], res[3], res[4:]


def out_bwd_gates(do, w_out, pa, pb, z, bg):
    s = do.shape[0]
    tm = 512

    def body(do_ref, w_ref, pa_ref, pb_ref, ga_ref, gb_ref, bg_ref, dpa_ref, dpb_ref, dga_ref, dgb_ref, dbg_ref):
        @pl.when(pl.program_id(0) == 0)
        def _():
            dbg_ref[...] = jnp.zeros_like(dbg_ref)

        dm = _dot(do_ref[...], w_ref[...], NT)
        sa = _sigmoid(ga_ref[...] + bg_ref[0:1, :])
        sb = _sigmoid(gb_ref[...] + bg_ref[1:2, :])
        dpa_ref[...] = (dm * sa).astype(BF16)
        dpb_ref[...] = (dm * sb).astype(BF16)
        dga = dm * pa_ref[...].astype(F32) * (sa * (1.0 - sa))
        dgb = dm * pb_ref[...].astype(F32) * (sb * (1.0 - sb))
        dga_ref[...] = dga.astype(BF16)
        dgb_ref[...] = dgb.astype(BF16)
        dbg_ref[0:1, :] += jnp.sum(dga, axis=0, keepdims=True)
        dbg_ref[1:2, :] += jnp.sum(dgb, axis=0, keepdims=True)

    out = jax.ShapeDtypeStruct((s, D), BF16)
    return pl.pallas_call(
        body, name="out_bwd_gates", grid=(s // tm,),
        in_specs=[_rows(tm, D), _full((D, D)), _rows(tm, D), _rows(tm, D), _rows(tm, D, 5), _rows(tm, D, 6),
                  _full((2, D))],
        out_specs=[_rows(tm, D)] * 4 + [_full((2, D))],
        out_shape=[out] * 4 + [jax.ShapeDtypeStruct((2, D), F32)], compiler_params=_params("arbitrary"),
    )(do, w_out, pa, pb, z, z, bg)


def gating_bwd(z, dya, ln_g, ln_b, w_s, bs_t, swapping):
    s = z.shape[0]
    ones = functools.partial(jnp.ones, (8, CHUNK), BF16)
    n = len(swapping)

    def body(*refs):
        u_ref, v_ref, dya_ref, lg_ref, lb_ref, ws_ref, bst_ref = refs[:7]
        du_ref, dv_ref, dws_ref, dbs_ref, dlg_ref, dlb_ref = refs[7 + n:13 + n]
        dvn_ref = refs[13 + 2 * n]
        ci = pl.program_id(0)
        if n:
            send, finish = _swap_phases(refs[7:7 + n], refs[13 + n:13 + 2 * n], *refs[14 + 2 * n:])
            pl.when(ci == 0)(send)

        @pl.when(ci == 0)
        def _():
            dws_ref[...] = jnp.zeros_like(dws_ref)
            dbs_ref[...] = jnp.zeros_like(dbs_ref)
            dlg_ref[...] = jnp.zeros_like(dlg_ref)
            dlb_ref[...] = jnp.zeros_like(dlb_ref)

        ug, dug_du = _gelu_and_grad(u_ref[...].astype(F32))
        vg, dvg_dv = _gelu_and_grad(v_ref[...].astype(F32))
        vhat, rstd = _layer_norm(vg)
        vn = (vhat * lg_ref[...] + lb_ref[...]).astype(BF16)
        dya = dya_ref[...].astype(F32)
        for g in range(GROUPS):
            cols = slice(g * CHUNK, (g + 1) * CHUNK)
            ws = _tril_ws(ws_ref, g)
            mixed = _dot(ws, vn[:, cols]) + bst_ref[:, g:g + 1]
            du_ref[:, cols] = (dya[:, cols] * mixed * dug_du[:, cols]).astype(BF16)
            dmix = (dya[:, cols] * ug[:, cols]).astype(BF16)
            dbs_ref[g] += _dot(ones(), dmix, NT)
            dws_ref[g] += _dot(dmix, vn[:, cols], NT)
            dvn_ref[:, cols] = _dot(ws, dmix, TN)
        dvn = dvn_ref[...]
        dlg_ref[...] += jnp.sum(dvn * vhat, axis=0, keepdims=True)
        dlb_ref[...] += jnp.sum(dvn, axis=0, keepdims=True)
        dvh = dvn * lg_ref[...]
        dvg = rstd * (dvh - jnp.mean(dvh, axis=-1, keepdims=True)
                      - vhat * jnp.mean(dvh * vhat, axis=-1, keepdims=True))
        dv_ref[...] = (dvg * dvg_dv).astype(BF16)

        @pl.when(ci == pl.num_programs(0) - 1)
        def _():
            r = lax.broadcasted_iota(jnp.int32, (CHUNK, CHUNK), 0)
            c = lax.broadcasted_iota(jnp.int32, (CHUNK, CHUNK), 1)
            for g in range(GROUPS):
                dws_ref[g] = jnp.where(c <= r, dws_ref[g], 0.0)

        if n:
            pl.when(ci == pl.num_programs(0) - 1)(finish)

    out = jax.ShapeDtypeStruct((s, D), BF16)
    res = pl.pallas_call(
        body, name="gating_bwd", grid=(s // CHUNK,),
        in_specs=[_rows(CHUNK, D, 0), _rows(CHUNK, D, 1), _rows(CHUNK, D), _full((1, D)), _full((1, D)),
                  _full((GROUPS, CHUNK, CHUNK)), _full((CHUNK, GROUPS))] + [ANY] * n,
        out_specs=[_rows(CHUNK, D), _rows(CHUNK, D), _full((GROUPS, CHUNK, CHUNK)), _full((GROUPS, 8, CHUNK)),
                   _full((1, D)), _full((1, D))] + [ANY] * n,
        out_shape=[out, out, jax.ShapeDtypeStruct((GROUPS, CHUNK, CHUNK), F32),
                   jax.ShapeDtypeStruct((GROUPS, 8, CHUNK), F32),
                   jax.ShapeDtypeStruct((1, D), F32), jax.ShapeDtypeStruct((1, D), F32)] + _swap_shapes(swapping),
        scratch_shapes=[pltpu.VMEM((CHUNK, D), F32)] + (_swap_sems(n) if n else []),
        compiler_params=_params("arbitrary", communicates=bool(n)),
    )(z, z, dya, ln_g, ln_b, w_s, bs_t, *swapping)
    return (*res[:6], res[6:])


def attn_bwd(z, yb, dyb, lse, logc, ka, kb, scattering, gathering=None):
    s = z.shape[0]
    nq = s // ATT_T
    t = ATT_T
    grp = ATT_BWD_GROUP
    ngrp = HEADS // 2 // grp
    wide = 128 * grp
    qcol, kcol, vcol = 2 * D // wide, 3 * D // wide, 4 * D // wide
    scale = 1.0 / math.sqrt(HEAD_DIM)
    n = len(scattering)
    g8 = 0 if gathering is None else 1

    def body(*refs):
        q_ref, k_ref, v_ref, y_ref, dy_ref, lse_ref, lc_ref, ka_ref, kb_ref = refs[:9]
        dq_ref, dk_ref, dv_ref = refs[9 + n + g8:12 + n + g8]
        qa_s, qt_s, da_s, dt_s, dq_s, dkt_s, dvt_s = refs[12 + 2 * n + 2 * g8:19 + 2 * n + 2 * g8]
        sems = refs[19 + 2 * n + 2 * g8:]
        gi, j = pl.program_id(0), pl.program_id(1)
        first, lane, ones = _head_masks()
        if n:
            send, finish = _scatter_phases(refs[9:9 + n], refs[12 + n + g8:12 + 2 * n + g8], *sems[:2])
            pl.when((gi == 0) & (j == 0))(send)
        if g8:
            send8, pass_on8, finish8 = _allgather8_phases(refs[12 + 2 * n + g8], *sems[2 * (n > 0):])
            pl.when((gi == 0) & (j == 0))(send8)
            pl.when((gi == ngrp - 1) & (j == nq - 1))(pass_on8)

        @pl.when(j == 0)
        def _():
            dq_s[...] = jnp.zeros_like(dq_s)
            for pr in range(grp):
                cols = slice(pr * 128, (pr + 1) * 128)
                for ib in range(nq):
                    rows = slice(ib * t, (ib + 1) * t)
                    q = q_ref[rows, cols].astype(F32) * scale
                    lse = lse_ref[rows, cols]
                    qa_s[pr, 0, ib] = jnp.where(first, q, _place3(lane, HEAD_DIM + 2 * AUG, _split3(-lse[:, 0:1]),
                                                                  ones(0, 2 * AUG))).astype(BF16)
                    qa_s[pr, 1, ib] = jnp.where(
                        first, _place3(lane, 2 * AUG, _split3(-lse[:, HEAD_DIM:HEAD_DIM + 1]), ones(1, 2 * AUG)),
                        q).astype(BF16)
                    qt_s[pr, ib, :, 0:t] = jnp.where(first, q, 0.0).T.astype(BF16)
                    qt_s[pr, ib, :, t:2 * t] = jnp.where(first, 0.0, q).T.astype(BF16)
                    do = dy_ref[rows, cols].astype(F32)
                    prod = do * y_ref[rows, cols].astype(F32)
                    dd0 = jnp.sum(jnp.where(first, prod, 0.0), axis=-1, keepdims=True)
                    dd1 = jnp.sum(jnp.where(first, 0.0, prod), axis=-1, keepdims=True)
                    da_s[pr, 0, ib] = jnp.where(first, do, _place3(lane, HEAD_DIM, _split3(-dd0), 0.0)).astype(BF16)
                    da_s[pr, 1, ib] = jnp.where(first, _place3(lane, 0, _split3(-dd1), 0.0), do).astype(BF16)
                    dt_s[pr, ib, :, 0:t] = jnp.where(first, do, 0.0).T.astype(BF16)
                    dt_s[pr, ib, :, t:2 * t] = jnp.where(first, 0.0, do).T.astype(BF16)

        keys = []
        for pr in range(grp):
            kj = k_ref[:, pr * 128:(pr + 1) * 128].astype(F32)
            vj = v_ref[:, pr * 128:(pr + 1) * 128].astype(F32)
            keys.append((
                jnp.where(first, kj, ka_ref[pr, 0] + kb_ref[pr, 0, pl.ds(j, 1), :]).astype(BF16),
                jnp.where(first, ka_ref[pr, 1] + kb_ref[pr, 1, pl.ds(j, 1), :], kj).astype(BF16),
                jnp.concatenate([jnp.where(first, kj, 0.0), jnp.where(first, 0.0, kj)], axis=0).astype(BF16),
                jnp.where(first, vj, ones(0, AUG)).astype(BF16),
                jnp.where(first, ones(1, AUG), vj).astype(BF16)))
        dkt_s[...] = jnp.zeros_like(dkt_s)
        dvt_s[...] = jnp.zeros_like(dvt_s)

        def step(i, _):
            lc = lc_ref[i - j]
            rows = pl.ds(pl.multiple_of(i * t, t), t)
            for pr in range(grp):
                k0a, k1a, kst, v0a, v1a = keys[pr]
                p0 = jnp.exp(_dot(qa_s[pr, 0, i], k0a, NT) + lc)
                p1 = jnp.exp(_dot(qa_s[pr, 1, i], k1a, NT) + lc)
                e0 = (p0 * _dot(da_s[pr, 0, i], v0a, NT)).astype(BF16)
                e1 = (p1 * _dot(da_s[pr, 1, i], v1a, NT)).astype(BF16)
                dq_s[pr, rows, :] += _dot(jnp.concatenate([e0, e1], axis=1), kst)
                dvt_s[pr] += _dot(dt_s[pr, i], jnp.concatenate([p0.astype(BF16), p1.astype(BF16)], axis=0))
                dkt_s[pr] += _dot(qt_s[pr, i], jnp.concatenate([e0, e1], axis=0))
            return 0

        lax.fori_loop(j, nq, step, 0)
        for pr in range(grp):
            dk_ref[:, pr * 128:(pr + 1) * 128] = dkt_s[pr].T.astype(BF16)
            dv_ref[:, pr * 128:(pr + 1) * 128] = dvt_s[pr].T.astype(BF16)

        @pl.when(j == nq - 1)
        def _():
            for pr in range(grp):
                dq_ref[:, pr * 128:(pr + 1) * 128] = (dq_s[pr] * scale).astype(BF16)

        if n:
            pl.when((gi == ngrp - 1) & (j == nq - 1))(finish)
        if g8:
            pl.when((gi == ngrp - 1) & (j == nq - 1))(finish8)

    colblock = lambda c: pl.BlockSpec((s, wide), lambda g, j: (0, c + g))
    once = lambda c: pl.BlockSpec((s, wide), lambda g, j: (0, c + g), pipeline_mode=pl.Buffered(1))
    blk = lambda c: pl.BlockSpec((t, wide), lambda g, j: (j, c + g))
    out = jax.ShapeDtypeStruct((s, D), BF16)
    res = pl.pallas_call(
        body, name="attn_bwd", grid=(ngrp, nq),
        in_specs=[once(qcol), blk(kcol), blk(vcol), once(0), once(0), once(0),
                  pl.BlockSpec((nq, t, t), lambda g, j: (0, 0, 0), pipeline_mode=pl.Buffered(1)),
                  pl.BlockSpec((grp, 2, t, 128), lambda g, j: (g, 0, 0, 0)),
                  pl.BlockSpec((grp, 2, nq, 128), lambda g, j: (g, 0, 0, 0))] + [ANY] * (n + g8),
        out_specs=[colblock(0), blk(0), blk(0)] + [ANY] * (n + g8),
        out_shape=[out] * 3 + _scatter_shapes(scattering)
        + ([jax.ShapeDtypeStruct(gathering.shape, gathering.dtype)] if g8 else []),
        input_output_aliases={9 + n: 3 + n} if g8 else {},
        scratch_shapes=[pltpu.VMEM((grp, 2, nq, t, 128), BF16), pltpu.VMEM((grp, nq, 128, 2 * t), BF16),
                        pltpu.VMEM((grp, 2, nq, t, 128), BF16), pltpu.VMEM((grp, nq, 128, 2 * t), BF16),
                        pltpu.VMEM((grp, s, 128), F32), pltpu.VMEM((grp, 128, t), F32),
                        pltpu.VMEM((grp, 128, t), F32)]
        + (_scatter_sems(n) if n else [])
        + ([pltpu.SemaphoreType.DMA((7,)), pltpu.SemaphoreType.DMA((7,))] if g8 else []),
        compiler_params=_params("arbitrary", "arbitrary", communicates=bool(n + g8)),
    )(z, z, z, yb, dyb, lse, logc, ka, kb, *scattering, *([gathering] if g8 else []))
    return res[0], res[1], res[2], res[3:3 + n], (res[3 + n] if g8 else None)


def in_bwd_norm(dz, wg, x, dx1, g_pre, scattering, gathering=None):
    s = x.shape[0]
    tm = 512
    n = len(scattering)
    g = 0 if gathering is None else 1
    last = (s // tm - 1, N_CHIPS - 1)

    def body(*refs):
        dz_ref, w_ref, x_ref, dx1_ref, g_ref = refs[:5]
        dx_ref, dg_ref = refs[5 + n + g:7 + n + g]
        acc_ref = refs[7 + 2 * n + 2 * g]
        sems = refs[8 + 2 * n + 2 * g:]
        i, k = pl.program_id(0), pl.program_id(1)
        if n:
            send, finish = _scatter_phases(refs[5:5 + n], refs[7 + n + g:7 + 2 * n + g], *sems[:2])
            pl.when((i == 0) & (k == 0))(send)
        if g:
            send8, pass_on8, finish8 = _allgather8_phases(refs[7 + 2 * n + g], *sems[2 * (n > 0):])
            pl.when((i == 0) & (k == 0))(send8)
            pl.when((i == last[0]) & (k == last[1]))(pass_on8)

        @pl.when((i == 0) & (k == 0))
        def _():
            dg_ref[...] = jnp.zeros_like(dg_ref)

        part = _dot(dz_ref[...], w_ref[...], NT)

        @pl.when(k == 0)
        def _():
            acc_ref[...] = part

        @pl.when(k > 0)
        def _():
            acc_ref[...] += part

        @pl.when(k == N_CHIPS - 1)
        def _():
            dh = acc_ref[...]
            xhat, r = _rms(x_ref[...])
            dg_ref[...] += jnp.sum(dh * xhat, axis=0, keepdims=True)
            dx_ref[...] = dx1_ref[...] + _rms_bwd(dh * g_ref[...], xhat, r)

        if n:
            pl.when((i == last[0]) & (k == last[1]))(finish)
        if g:
            pl.when((i == last[0]) & (k == last[1]))(finish8)

    row = pl.BlockSpec((tm, D), lambda i, k: (i, 0))
    vec = pl.BlockSpec((1, D), lambda i, k: (0, 0))
    res = pl.pallas_call(
        body, name="in_bwd_norm", grid=(s // tm, N_CHIPS),
        in_specs=[pl.BlockSpec((tm, IN_SHARD), lambda i, k: (i, k)),
                  pl.BlockSpec((None, D, IN_SHARD), lambda i, k: (k, 0, 0)), row, row, vec] + [ANY] * (n + g),
        out_specs=[row, vec] + [ANY] * (n + g),
        out_shape=[jax.ShapeDtypeStruct((s, D), F32), jax.ShapeDtypeStruct((1, D), F32)]
        + _scatter_shapes(scattering) + ([jax.ShapeDtypeStruct(gathering.shape, gathering.dtype)] if g else []),
        input_output_aliases={5 + n: 2 + n} if g else {},
        scratch_shapes=[pltpu.VMEM((tm, D), F32)] + (_scatter_sems(n) if n else [])
        + ([pltpu.SemaphoreType.DMA((7,)), pltpu.SemaphoreType.DMA((7,))] if g else []),
        compiler_params=_params("arbitrary", "arbitrary", communicates=bool(n + g)),
    )(dz, wg, x, dx1, g_pre, *scattering, *([gathering] if g else []))
    return res[0], res[1], res[2:2 + n], (res[2 + n] if g else None)


def _adamw_math(w, g, m, v):
    m = ADAM_B1 * m + (1.0 - ADAM_B1) * g
    v = ADAM_B2 * v + (1.0 - ADAM_B2) * (g * g)
    m_hat = m / (1.0 - ADAM_B1 ** ADAM_STEP)
    v_hat = v / (1.0 - ADAM_B2 ** ADAM_STEP)
    delta = -ADAM_LR * (m_hat / (jnp.sqrt(v_hat) + ADAM_EPS) + ADAM_WD * w)
    return delta, m, v


def adamw(name, w, g, m, v, tr):
    r, c = w.shape

    def body(w_ref, g_ref, m_ref, v_ref, go_ref, d_ref, nm_ref, nv_ref):
        g = g_ref[...]
        go_ref[...] = g
        d_ref[...], nm_ref[...], nv_ref[...] = _adamw_math(w_ref[...], g, m_ref[...], v_ref[...])

    out = jax.ShapeDtypeStruct((r, c), F32)
    return pl.pallas_call(
        body, name=name, grid=(r // tr,), in_specs=[_rows(tr, c)] * 4, out_specs=[_rows(tr, c)] * 4,
        out_shape=[out] * 4, compiler_params=_params("parallel"),
    )(w, g, m, v)


def _allgather8_phases(buf, send_sems, recv_sems):
    x, y, c, chips = _place()
    me = 2 * x + y
    sibling = (x, y, 1 - c)
    rows = buf.shape[1] // 2

    def part(chip, core):
        return buf.at[chip, pl.ds(core * rows, rows)]

    def copy(k, block, to):
        return pltpu.make_async_remote_copy(src_ref=block, dst_ref=block, send_sem=send_sems.at[k],
                                            recv_sem=recv_sems.at[k], device_id=to, device_id_type=MESH)

    def chip_of(j):
        return 2 * chips[j][0] + chips[j][1]

    def send():
        copy(0, part(me, c), sibling).start()
        for j in range(3):
            copy(1 + j, part(me, c), (chips[j][0], chips[j][1], c)).start()

    def pass_on():
        for j in range(3):
            copy(1 + j, part(chip_of(j), c), (chips[j][0], chips[j][1], c)).wait_recv()
            copy(4 + j, part(chip_of(j), c), sibling).start()

    def finish():
        copy(0, part(me, 1 - c), sibling).wait_recv()
        for j in range(3):
            copy(4 + j, part(chip_of(j), 1 - c), sibling).wait_recv()
        copy(0, part(me, c), sibling).wait_send()
        for j in range(3):
            copy(1 + j, part(me, c), (chips[j][0], chips[j][1], c)).wait_send()
            copy(4 + j, part(chip_of(j), c), sibling).wait_send()

    return send, pass_on, finish


def add_halves(name, g, recv, c_idx, tr):
    n, h, c = recv.shape

    def body(c_ref, g_ref, r_ref, o_ref):
        o_ref[...] = (g_ref[...] + r_ref[...]).astype(BF16)

    nb = h // tr
    return pl.pallas_call(
        body, name=name,
        grid_spec=pltpu.PrefetchScalarGridSpec(
            num_scalar_prefetch=1, grid=(n, nb),
            in_specs=[pl.BlockSpec((None, tr, c), lambda k, i, c_ref: (k, c_ref[0] * nb + i, 0)),
                      pl.BlockSpec((None, tr, c), lambda k, i, c_ref: (k, i, 0))],
            out_specs=pl.BlockSpec((None, tr, c), lambda k, i, c_ref: (k, i, 0))),
        out_shape=jax.ShapeDtypeStruct((n, h, c), BF16), compiler_params=_params("parallel", "parallel"),
    )(c_idx, g, recv)


def sum_chips(name, parts, recv, where, tr, after=None):
    n, h, c = recv.shape
    nb = h // tr

    def body(w_ref, p_ref, r_ref, *rest):
        acc = p_ref[...].astype(F32)
        for k in range(n):
            acc = acc + r_ref[k].astype(F32)
        rest[-1][...] = acc

    return pl.pallas_call(
        body, name=name,
        grid_spec=pltpu.PrefetchScalarGridSpec(
            num_scalar_prefetch=1, grid=(nb,),
            in_specs=[pl.BlockSpec((None, tr, c), lambda i, w_ref: (w_ref[0], i, 0)),
                      pl.BlockSpec((n, tr, c), lambda i, w_ref: (0, i, 0))] + ([ANY] if after is not None else []),
            out_specs=pl.BlockSpec((tr, c), lambda i, w_ref: (w_ref[1] * nb + i, 0))),
        out_shape=jax.ShapeDtypeStruct((2 * h, c), F32), compiler_params=_params("parallel"),
    )(where, parts, recv, *([after] if after is not None else []))


def place_shard(name, shard, where, dtype, tr):
    r, c = shard.shape

    def body(w_ref, s_ref, o_ref):
        o_ref[...] = s_ref[...].astype(dtype)

    return pl.pallas_call(
        body, name=name,
        grid_spec=pltpu.PrefetchScalarGridSpec(
            num_scalar_prefetch=1, grid=(r // tr,),
            in_specs=[pl.BlockSpec((tr, c), lambda i, w_ref: (i, 0))],
            out_specs=pl.BlockSpec((None, tr, c), lambda i, w_ref: (w_ref[0], i, 0))),
        out_shape=jax.ShapeDtypeStruct((N_CHIPS, r, c), dtype), compiler_params=_params("parallel"),
    )(where, shard)


ANY = pl.BlockSpec(memory_space=pl.ANY)


def _place():
    x, y, c = lax.axis_index("x"), lax.axis_index("y"), lax.axis_index("c")
    chips = [(1 - x, y), (x, 1 - y), (1 - x, 1 - y)]
    return x, y, c, chips


def gather_shards(arrays):
    n = len(arrays)

    def body(*refs):
        send, pass_on, finish = _gather_phases(refs[n:2 * n], *refs[2 * n:], _spans(arrays))
        send()
        pass_on()
        finish()

    return pl.pallas_call(
        body, name="gather_shards", in_specs=[ANY] * n, out_specs=[ANY] * n,
        out_shape=[jax.ShapeDtypeStruct(a.shape, a.dtype) for a in _arrays(arrays)],
        input_output_aliases={w: w for w in range(n)}, scratch_shapes=_gather_sems(n),
        compiler_params=pltpu.CompilerParams(has_side_effects=True),
    )(*_arrays(arrays))


def _gather_sems(n):
    return [pltpu.SemaphoreType.DMA((6 * n,)), pltpu.SemaphoreType.DMA((6 * n,))]


class Span(typing.NamedTuple):
    array: jax.Array
    lo: int
    hi: int
    ways: tuple = (0, 1, 2)


def _arrays(gathering):
    return [g.array if isinstance(g, Span) else g for g in gathering]


def _spans(gathering):
    return [(g.lo, g.hi, g.ways) if isinstance(g, Span) else (0, g.shape[1], (0, 1, 2)) for g in gathering]


def _gather_phases(out, send_sems, recv_sems, spans):
    n = len(out)
    if not any(ways for _, _, ways in spans):
        return (lambda: None,) * 3
    x, y, c, chips = _place()
    me = 2 * x + y
    sibling = (x, y, 1 - c)

    def half(w, chip, core):
        lo, hi, _ = spans[w]
        h = (hi - lo) // 2
        return out[w].at[chip, pl.ds(lo + core * h, h)]

    def copy(k, block, to):
        return pltpu.make_async_remote_copy(src_ref=block, dst_ref=block, send_sem=send_sems.at[k],
                                            recv_sem=recv_sems.at[k], device_id=to, device_id_type=MESH)

    def over_ici(w, j, chip):
        return copy(3 * w + j, half(w, chip, c), (chips[j][0], chips[j][1], c))

    def over_d2d(w, j, core):
        return copy(3 * n + 3 * w + j, half(w, 2 * chips[j][0] + chips[j][1], core), sibling)

    pairs = [(w, j) for w in range(n) for j in spans[w][2]]

    def send():
        for w, j in pairs:
            over_ici(w, j, me).start()

    def pass_on():
        for w, j in pairs:
            over_ici(w, j, 2 * chips[j][0] + chips[j][1]).wait_recv()
            over_d2d(w, j, c).start()

    def finish():
        for w, j in pairs:
            over_d2d(w, j, 1 - c).wait_recv()
        for w, j in pairs:
            over_ici(w, j, me).wait_send()
            over_d2d(w, j, c).wait_send()

    return send, pass_on, finish


def _relay_sems():
    return [pltpu.SemaphoreType.DMA((4,)), pltpu.SemaphoreType.DMA((4,))]


def _relay_phases(out, send_sems, recv_sems):
    x, y, c, chips = _place()
    sibling = (x, y, 1 - c)
    rows = out.shape[1]
    quarter = rows // 4
    far = 2 * chips[2][0] + chips[2][1]

    def piece(chip, way, core):
        return out.at[chip, pl.ds(way * (rows // 2) + core * quarter, quarter)]

    def copy(k, block, to):
        return pltpu.make_async_remote_copy(src_ref=block, dst_ref=block, send_sem=send_sems.at[k],
                                            recv_sem=recv_sems.at[k], device_id=to, device_id_type=MESH)

    def over_ici(way, chip):
        return copy(way, piece(chip, way, c), (chips[way][0], chips[way][1], c))

    def over_d2d(way, core):
        return copy(2 + way, piece(far, way, core), sibling)

    def send():
        for way in range(2):
            other = chips[1 - way]
            over_ici(way, 2 * other[0] + other[1]).start()

    def pass_on():
        for way in range(2):
            over_ici(way, far).wait_recv()
            over_d2d(way, c).start()

    def finish():
        for way in range(2):
            over_d2d(way, 1 - c).wait_recv()
        for way in range(2):
            other = chips[1 - way]
            over_ici(way, 2 * other[0] + other[1]).wait_send()
            over_d2d(way, c).wait_send()

    return send, pass_on, finish


def swap_halves(name, grads):
    n = len(grads)

    def body(*refs):
        send, finish = _swap_phases(refs[:n], refs[n:2 * n], *refs[2 * n:])
        send()
        finish()

    return pl.pallas_call(
        body, name=name, in_specs=[ANY] * n, out_specs=[ANY] * n, out_shape=_swap_shapes(grads),
        scratch_shapes=_swap_sems(n), compiler_params=pltpu.CompilerParams(has_side_effects=True),
    )(*grads)


def _swap_shapes(grads):
    return [jax.ShapeDtypeStruct((a.shape[0], a.shape[1] // 2, a.shape[2]), a.dtype) for a in grads]


def _swap_sems(n):
    return [pltpu.SemaphoreType.DMA((n,)), pltpu.SemaphoreType.DMA((n,))]


def _swap_phases(g, out, send_sems, recv_sems):
    x, y, c, _ = _place()

    def copies():
        return [pltpu.make_async_remote_copy(
            src_ref=g[w].at[:, pl.ds((1 - c) * (g[w].shape[1] // 2), g[w].shape[1] // 2)], dst_ref=out[w],
            send_sem=send_sems.at[w], recv_sem=recv_sems.at[w], device_id=(x, y, 1 - c), device_id_type=MESH)
            for w in range(len(g))]

    def send():
        for cp in copies():
            cp.start()

    def finish():
        for cp in copies():
            cp.wait()

    return send, finish


def _send_phases(g, out, send_sems, recv_sems):
    x, y, c, _ = _place()

    def copies():
        return [pltpu.make_async_remote_copy(
            src_ref=g[w], dst_ref=out[w], send_sem=send_sems.at[w], recv_sem=recv_sems.at[w],
            device_id=(x, y, 1 - c), device_id_type=MESH) for w in range(len(g))]

    def send():
        for cp in copies():
            cp.start()

    def finish():
        for cp in copies():
            cp.wait()

    return send, finish


def dw_in_half(name, h, dz, which, sending):
    s = h.shape[0]
    hh, tb = D // 2, IN_SHARD // 2
    n = len(sending)
    steps = IN_COLS // tb

    def body(w_ref, *refs):
        a_ref, b_ref, o_ref = refs[0], refs[1], refs[2 + n]
        j = pl.program_id(0)
        if n:
            send, finish = _send_phases(refs[2:2 + n], refs[3 + n:3 + 2 * n], *refs[3 + 2 * n:])
            pl.when(j == 0)(send)
        o_ref[...] = _dot(a_ref[...], b_ref[...], TN)
        if n:
            pl.when(j == steps - 1)(finish)

    out = pl.pallas_call(
        body, name=name,
        grid_spec=pltpu.PrefetchScalarGridSpec(
            num_scalar_prefetch=1, grid=(steps,),
            in_specs=[pl.BlockSpec((s, hh), lambda j, w: (0, w[0])), pl.BlockSpec((s, tb), lambda j, w: (0, j))]
            + [ANY] * n,
            out_specs=[pl.BlockSpec((None, hh, tb), lambda j, w: (j // 2, 0, j % 2))] + [ANY] * n,
            scratch_shapes=_swap_sems(n) if n else []),
        out_shape=[jax.ShapeDtypeStruct((N_CHIPS, hh, IN_SHARD), F32)]
        + [jax.ShapeDtypeStruct(a.shape, a.dtype) for a in sending],
        compiler_params=_params("arbitrary", communicates=bool(n)),
    )(which, h, dz, *sending)
    return out[0], out[1:]


def scatter_chips(parts):
    n = len(parts)

    def body(*refs):
        send, finish = _scatter_phases(refs[:n], refs[n:2 * n], *refs[2 * n:])
        send()
        finish()

    return pl.pallas_call(
        body, name="scatter_chips", in_specs=[ANY] * n, out_specs=[ANY] * n,
        out_shape=_scatter_shapes(parts), scratch_shapes=_scatter_sems(n),
        compiler_params=pltpu.CompilerParams(has_side_effects=True),
    )(*parts)


def _scatter_shapes(parts):
    return [jax.ShapeDtypeStruct((3,) + a.shape[1:], a.dtype) for a in parts]


def _scatter_sems(n):
    return [pltpu.SemaphoreType.DMA((3 * n,)), pltpu.SemaphoreType.DMA((3 * n,))]


def _scatter_phases(p, out, send_sems, recv_sems):
    x, y, c, chips = _place()

    def copies():
        return [pltpu.make_async_remote_copy(
            src_ref=p[w].at[2 * px + py], dst_ref=out[w].at[j], send_sem=send_sems.at[3 * w + j],
            recv_sem=recv_sems.at[3 * w + j], device_id=(px, py, c), device_id_type=MESH)
            for w in range(len(p)) for j, (px, py) in enumerate(chips)]

    def send():
        for cp in copies():
            cp.start()

    def finish():
        for cp in copies():
            cp.wait()

    return send, finish


def _join_only(arrays):
    n = len(arrays)

    def body(*refs):
        out = refs[n:2 * n]
        send_sems, recv_sems = refs[2 * n:]
        x, y, c, _ = _place()

        def copy(w, core):
            h = out[w].shape[0] // 2
            rows = out[w].at[pl.ds(core * h, h)]
            return pltpu.make_async_remote_copy(
                src_ref=rows, dst_ref=rows, send_sem=send_sems.at[w], recv_sem=recv_sems.at[w],
                device_id=(x, y, 1 - c), device_id_type=MESH)

        for w in range(n):
            copy(w, c).start()
        for w in range(n):
            copy(w, 1 - c).wait_recv()
        for w in range(n):
            copy(w, c).wait_send()

    return pl.pallas_call(
        body, name="join_only", in_specs=[ANY] * n, out_specs=[ANY] * n,
        out_shape=[jax.ShapeDtypeStruct(a.shape, a.dtype) for a in arrays],
        input_output_aliases={w: w for w in range(n)},
        scratch_shapes=[pltpu.SemaphoreType.DMA((n,)), pltpu.SemaphoreType.DMA((n,))],
        compiler_params=pltpu.CompilerParams(has_side_effects=True),
    )(*arrays)


HBM = pl.BlockSpec(memory_space=pltpu.HBM)
SEM = pl.BlockSpec(memory_space=pltpu.SEMAPHORE)
DATAFLOW = pltpu.SideEffectType.DATAFLOW_SIDE_EFFECTING


def _scatter_copies(p_ref, land_ref, send_sems, recv_sems):
    x, y, c, chips = _place()
    return [pltpu.make_async_remote_copy(
        src_ref=p_ref.at[2 * px + py], dst_ref=land_ref.at[j], send_sem=send_sems[j], recv_sem=recv_sems[j],
        device_id=(px, py, c), device_id_type=MESH) for j, (px, py) in enumerate(chips)]


def scatter_start(p):
    land = jax.ShapeDtypeStruct((3,) + p.shape[1:], p.dtype)

    def body(p_ref, land_ref, *outs):
        for cp in _scatter_copies(p_ref, land_ref, outs[0:3], outs[3:6]):
            cp.start()
        outs[8][...] = jnp.zeros_like(outs[8])

    return pl.pallas_call(
        body, name="scatter_start",
        out_shape=(pltpu.SemaphoreType.DMA(()),) * 6
        + (pltpu.HBM(p.shape, p.dtype), pltpu.HBM(land.shape, land.dtype), jax.ShapeDtypeStruct((8, 128), F32)),
        in_specs=(HBM, HBM), out_specs=(SEM,) * 6 + (HBM, HBM, pl.BlockSpec(memory_space=pltpu.VMEM)),
        input_output_aliases={0: 6, 1: 7},
        compiler_params=pltpu.CompilerParams(has_side_effects=DATAFLOW),
    )(pltpu.with_memory_space_constraint(p, pltpu.HBM),
      pltpu.with_memory_space_constraint(lax.empty(land.shape, land.dtype), pltpu.HBM))


def scatter_wait(started, after):
    sems, p_thru, land_thru = started[0:6], started[6], started[7]

    def body(p_ref, land_ref, *refs):
        for cp in _scatter_copies(p_ref, land_ref, refs[0:3], refs[3:6]):
            cp.wait_send()
            cp.wait_recv()

    return pl.pallas_call(
        body, name="scatter_wait",
        out_shape=(pltpu.HBM(p_thru.shape, p_thru.dtype), pltpu.HBM(land_thru.shape, land_thru.dtype)),
        in_specs=(HBM, HBM) + (SEM,) * 6 + (pl.BlockSpec(memory_space=pl.ANY),) * len(after), out_specs=(HBM, HBM),
        input_output_aliases={0: 0, 1: 1},
        compiler_params=pltpu.CompilerParams(has_side_effects=DATAFLOW),
    )(p_thru, land_thru, *sems, *after)


def _gather_leg_copies(buf_ref, leg, send_sems, recv_sems):
    x, y, c, chips = _place()
    h = buf_ref.shape[1] // 2
    copies = []
    for j, (px, py) in enumerate(chips):
        chip, to = (2 * x + y, (px, py, c)) if leg == "ici" else (2 * px + py, (x, y, 1 - c))
        block = buf_ref.at[chip, pl.ds(c * h, h)]
        copies.append(pltpu.make_async_remote_copy(
            src_ref=block, dst_ref=block, send_sem=send_sems[j], recv_sem=recv_sems[j],
            device_id=to, device_id_type=MESH))
    return copies


def gather_leg_start(name, buf, leg):
    def body(buf_ref, *outs):
        for cp in _gather_leg_copies(buf_ref, leg, outs[0:3], outs[3:6]):
            cp.start()
        outs[7][...] = jnp.zeros_like(outs[7])

    return pl.pallas_call(
        body, name=name,
        out_shape=(pltpu.SemaphoreType.DMA(()),) * 6
        + (pltpu.HBM(buf.shape, buf.dtype), jax.ShapeDtypeStruct((8, 128), F32)),
        in_specs=(HBM,), out_specs=(SEM,) * 6 + (HBM, pl.BlockSpec(memory_space=pltpu.VMEM)),
        input_output_aliases={0: 6},
        compiler_params=pltpu.CompilerParams(has_side_effects=DATAFLOW),
    )(pltpu.with_memory_space_constraint(buf, pltpu.HBM))


def gather_leg_wait(name, started, leg, after):
    def body(buf_ref, *refs):
        for cp in _gather_leg_copies(buf_ref, leg, refs[0:3], refs[3:6]):
            cp.wait_send()
            cp.wait_recv()

    buf = started[6]
    return pl.pallas_call(
        body, name=name, out_shape=pltpu.HBM(buf.shape, buf.dtype),
        in_specs=(HBM,) + (SEM,) * 6 + (pl.BlockSpec(memory_space=pl.ANY),) * len(after), out_specs=HBM,
        input_output_aliases={0: 0},
        compiler_params=pltpu.CompilerParams(has_side_effects=DATAFLOW),
    )(buf, *started[0:6], *after)


def join_halves(arrays, gathering=None):
    n = len(arrays)
    if gathering is None:
        return _join_only(arrays), None

    def body(*refs):
        out = refs[n + 1:2 * n + 1]
        send_sems, recv_sems = refs[2 * n + 2:2 * n + 4]
        send8, pass_on8, finish8 = _allgather8_phases(refs[2 * n + 1], *refs[2 * n + 4:])
        x, y, c, _ = _place()

        def copy(w, core):
            h = out[w].shape[0] // 2
            rows = out[w].at[pl.ds(core * h, h)]
            return pltpu.make_async_remote_copy(
                src_ref=rows, dst_ref=rows, send_sem=send_sems.at[w], recv_sem=recv_sems.at[w],
                device_id=(x, y, 1 - c), device_id_type=MESH)

        send8()
        for w in range(n):
            copy(w, c).start()
        pass_on8()
        for w in range(n):
            copy(w, 1 - c).wait_recv()
        finish8()
        for w in range(n):
            copy(w, c).wait_send()

    res = pl.pallas_call(
        body, name="join_halves", in_specs=[ANY] * (n + 1), out_specs=[ANY] * (n + 1),
        out_shape=[jax.ShapeDtypeStruct(a.shape, a.dtype) for a in list(arrays) + [gathering]],
        input_output_aliases={w: w for w in range(n + 1)},
        scratch_shapes=[pltpu.SemaphoreType.DMA((n,)), pltpu.SemaphoreType.DMA((n,)),
                        pltpu.SemaphoreType.DMA((7,)), pltpu.SemaphoreType.DMA((7,))],
        compiler_params=pltpu.CompilerParams(has_side_effects=True),
    )(*arrays, gathering)
    return res[:n], res[n]


def allreduce_small(packed):
    r, c = packed.shape
    n_dev = 8

    def body(x_ref, all_ref, sum_ref, send_sems, recv_sems, local_sem):
        x, y, cc, chips = _place()
        me, sibling = (x, y, cc), (x, y, 1 - cc)

        def rows(px, py, pc):
            return all_ref.at[4 * px + 2 * py + pc]

        def copy(k, block, to, src=None):
            return pltpu.make_async_remote_copy(
                src_ref=rows(*block) if src is None else src, dst_ref=rows(*block), send_sem=send_sems.at[k],
                recv_sem=recv_sems.at[k], device_id=to, device_id_type=MESH)

        mine = pltpu.make_async_copy(x_ref, rows(*me), local_sem)
        mine.start()
        first = [copy(0, me, sibling, src=x_ref)]
        first += [copy(1 + j, me, (*chip, cc), src=x_ref) for j, chip in enumerate(chips)]
        for cp in first:
            cp.start()
        passed = [copy(4 + j, (*chip, cc), sibling) for j, chip in enumerate(chips)]
        for j, chip in enumerate(chips):
            copy(1 + j, (*chip, cc), me).wait_recv()
            passed[j].start()
        copy(0, sibling, me).wait_recv()
        for j, chip in enumerate(chips):
            copy(4 + j, (*chip, 1 - cc), me).wait_recv()
        for cp in first + passed:
            cp.wait_send()
        mine.wait()
        acc = all_ref[0]
        for k in range(1, n_dev):
            acc = acc + all_ref[k]
        sum_ref[...] = acc

    vm = pl.BlockSpec(memory_space=pltpu.VMEM)
    return pl.pallas_call(
        body, name="allreduce_small", in_specs=[vm], out_specs=[vm, vm],
        out_shape=[jax.ShapeDtypeStruct((n_dev, r, c), F32), jax.ShapeDtypeStruct((r, c), F32)],
        scratch_shapes=[pltpu.SemaphoreType.DMA((7,)), pltpu.SemaphoreType.DMA((7,)), pltpu.SemaphoreType.DMA],
        compiler_params=pltpu.CompilerParams(has_side_effects=True, vmem_limit_bytes=VMEM_LIMIT),
    )(packed)[1]


def local_step(x, target, vecs, w_s, bs_t, bg, wg_in, late, core=None, order=None, where=None):
    on_mesh = core is not None

    def add(names, grads, recv):
        return [add_halves("add_" + n, g, r, core, min(r.shape[1], 256)) for n, g, r in zip(names, grads, recv)]

    g_pre, ln_g, ln_b, g_post, g_fpre, g_fpost = vecs
    s = x.shape[0]
    if order is None:
        order = jnp.arange(N_CHIPS, dtype=jnp.int32)
    logc = _attn_tables(s)
    ka, kb = _alibi_tables(s)

    h = norm_pre(x, g_pre)
    if not on_mesh:
        wg_a, wg_b, wg_out, wg_ff1, wg_ff2 = late
    if on_mesh:
        z, wg_in, (wg_a, wg_b, wg_out, wg_ff1, wg_ff2) = mm_in(h, wg_in, order, True, late)
        over_ici = gather_leg_start("ff2_ici_start", wg_ff2, "ici")
        ya, (wg_b,) = gating_fwd(z, ln_g, ln_b + over_ici[7][0:1, 0:1], w_s, bs_t, [wg_b])
        yb, lse, (wg_a, wg_ff1, bg) = attn_fwd(z, logc, ka, kb, [wg_a, wg_ff1, bg])
        bg = jnp.transpose(bg[:, :2, :], (1, 0, 2)).reshape(2, D)
        over_d2d = gather_leg_start("ff2_d2d_start", gather_leg_wait("ff2_ici_wait", over_ici, "ici", [yb]), "d2d")
        bg = bg + over_d2d[7][0:1, 0:1]
    else:
        z, _, _ = mm_in(h, wg_in, order, False)
        ya, _ = gating_fwd(z, ln_g, ln_b, w_s, bs_t, [])
        yb, lse, _ = attn_fwd(z, logc, ka, kb, [])
    merged, pa, pb, got = proj_merge(ya, yb, wg_a.reshape(D, D), wg_b.reshape(D, D), z, bg, [wg_out] if on_mesh else [])
    wg_out = got[0] if on_mesh else wg_out
    w_out = wg_out.reshape(D, D)
    o, x1, h2, _ = out_norm(merged, w_out, x, g_post, g_fpre, [])
    a, rl, _ = mm_ff1(h2, wg_ff1, [])
    if on_mesh:
        wg_ff2 = gather_leg_wait("ff2_d2d_wait", over_d2d, "d2d", [rl])
    w_ff2 = wg_ff2.reshape(D_FF, D)
    dy, df, d_gfpost, loss = ff2_loss(rl, w_ff2, x1, target, g_fpost)

    half_cols = pl.BlockSpec((D, D // 2), lambda i, j: (0, j))
    d_wff2 = mm_tn("dw_ff2", rl, df, D // 2, D, (D_FF, D), pl.BlockSpec((D // 2, D), lambda i, j: (i, 0)))
    da = ff2_bwd(df, w_ff2, a)
    d_wff1 = mm_tn("dw_ff1", h2, da, D, D // 2, (N_CHIPS, D, D),
                   pl.BlockSpec((None, D, D // 2), lambda i, j: (j // 2, 0, j % 2)))
    d_ff = [d_wff1, d_wff2.reshape(N_CHIPS, D, D)]
    dx1, do, d_gfpre, d_gpost, recv_ff = ff1_bwd_norms(da, wg_ff1, x1, o, dy, g_fpre, g_post, d_ff if on_mesh else [])
    d_wout = mm_tn("dw_out", merged, do, D, D // 2, (D, D), half_cols)
    dpa, dpb, dga, dgb, d_bg = out_bwd_gates(do, w_out, pa, pb, z, bg)
    d_wa = mm_tn("dw_a", ya, dpa, D, D // 2, (D, D), half_cols)
    d_wb = mm_tn("dw_b", yb, dpb, D, D // 2, (D, D), half_cols)
    dya = mm_nt("dy_a", dpa, wg_a.reshape(D, D))
    dyb = mm_nt("dy_b", dpb, wg_b.reshape(D, D))
    d_proj = [d_wa.reshape(N_CHIPS, D // N_CHIPS, D), d_wb.reshape(N_CHIPS, D // N_CHIPS, D),
              d_wout.reshape(N_CHIPS, D // N_CHIPS, D)]
    du, dv, d_ws, d_bs, d_lng, d_lnb, recv_proj = gating_bwd(z, dya, ln_g, ln_b, w_s, bs_t, d_proj if on_mesh else [])
    early = d_proj + d_ff
    parts_early = add(BIG[1:], early, list(recv_proj) + list(recv_ff)) if on_mesh else []
    small = dict(b_gate=d_bg, ln_v_g=d_lng, ln_v_b=d_lnb, w_s=d_ws, b_s=d_bs[:, 0, :],
                 norm_mix_post=d_gpost, norm_ffn_pre=d_gfpre, norm_ffn_post=d_gfpost)
    packed = pack_small(dict(small, norm_mix_pre=jnp.zeros((1, D), F32)), loss, where) if on_mesh else None
    dq, dk, dvb, got_early, packed = attn_bwd(z, yb, dyb, lse, logc, ka, kb, parts_early, packed)
    dz = jnp.concatenate([du, dv, dq, dk, dvb, dga, dgb], axis=1)
    if on_mesh:
        for_sibling, _ = dw_in_half("dw_in_sibling", h, dz, 1 - core, [])
        mine, from_sibling = dw_in_half("dw_in_mine", h, dz, core, [for_sibling])
        d_win = None
        parts_late = [add_halves("add_w_in", mine, from_sibling[0], jnp.zeros((1,), jnp.int32), 256)]
    else:
        half = IN_SHARD // 2
        d_win = mm_tn("dw_in", h, dz, D, half, (N_CHIPS, D, IN_SHARD),
                      pl.BlockSpec((None, D, half), lambda i, j: (j // 2, 0, j % 2)))
        parts_late = []
    started = scatter_start(parts_late[0]) if on_mesh else None
    dx, d_gpre, _, _ = in_bwd_norm(dz, wg_in, x, dx1, g_pre + started[8][0:1, 0:1] if on_mesh else g_pre, [])
    small["norm_mix_pre"] = d_gpre
    return loss[0, 0], dx, [d_win] + early, small, parts_early, list(got_early), packed, started


BIG = ("w_in", "w_a_proj", "w_b_proj", "w_out", "w_ff1", "w_ff2")
SMALL = ("norm_mix_pre", "ln_v_g", "ln_v_b", "b_s", "norm_mix_post", "norm_ffn_pre", "norm_ffn_post", "w_s", "b_gate")
ORDER = ("norm_mix_pre", "w_in", "b_gate", "ln_v_g", "ln_v_b", "w_s", "b_s", "w_a_proj", "w_b_proj", "w_out",
         "norm_mix_post", "norm_ffn_pre", "w_ff1", "w_ff2", "norm_ffn_post")
VEC_ROWS = D // 128
WS_ROW = 7 * VEC_ROWS
BG_ROW = WS_ROW + GROUPS * CHUNK
LOSS_ROW = BG_ROW + 2 * VEC_ROWS
PACK_ROWS = LOSS_ROW + 8


def pack_small(small, loss, where):
    vectors = [small[n] for n in SMALL[:7]]
    operands = vectors + [small["w_s"], small["b_gate"], loss]

    def body(where_ref, *refs):
        out = refs[-1]
        ws_ref, bg_ref, loss_ref = refs[7:10]
        for i, n in enumerate(SMALL[:7]):
            if n == "b_s":
                out[i * VEC_ROWS:(i + 1) * VEC_ROWS, :] = refs[i][...]
            else:
                for j in range(VEC_ROWS):
                    out[i * VEC_ROWS + j:i * VEC_ROWS + j + 1, :] = refs[i][:, j * 128:(j + 1) * 128]
        for g in range(GROUPS):
            out[WS_ROW + g * CHUNK:WS_ROW + (g + 1) * CHUNK, :] = ws_ref[g]
        for r in range(2):
            for j in range(VEC_ROWS):
                row = BG_ROW + r * VEC_ROWS + j
                out[row:row + 1, :] = bg_ref[r:r + 1, j * 128:(j + 1) * 128]
        lane = lax.broadcasted_iota(jnp.int32, (8, 128), 1)
        sub = lax.broadcasted_iota(jnp.int32, (8, 128), 0)
        out[LOSS_ROW:LOSS_ROW + 8, :] = jnp.where((lane == 0) & (sub == 0), loss_ref[...], 0.0)

    return pl.pallas_call(
        body, name="pack_small",
        grid_spec=pltpu.PrefetchScalarGridSpec(
            num_scalar_prefetch=1, grid=(1,), in_specs=[_full(a.shape) for a in operands],
            out_specs=pl.BlockSpec((None, PACK_ROWS, 128), lambda i, w: (w[0], w[1], 0))),
        out_shape=jax.ShapeDtypeStruct((N_CHIPS, 2 * PACK_ROWS, 128), F32), compiler_params=_params("arbitrary"),
    )(where, *operands)


def pack_vector(vec, where):
    def body(where_ref, v_ref, out):
        for j in range(VEC_ROWS):
            out[j:j + 1, :] = v_ref[:, j * 128:(j + 1) * 128]

    return pl.pallas_call(
        body, name="pack_vector",
        grid_spec=pltpu.PrefetchScalarGridSpec(
            num_scalar_prefetch=1, grid=(1,), in_specs=[_full(vec.shape)],
            out_specs=pl.BlockSpec((None, VEC_ROWS, 128), lambda i, w: (w[0], w[1], 0))),
        out_shape=jax.ShapeDtypeStruct((N_CHIPS, 2 * VEC_ROWS, 128), F32), compiler_params=_params("arbitrary"),
    )(where, vec)


def adamw_small(gathered, first, chip, w, m, v):
    shapes = {n: (1, D) for n in SMALL}
    shapes.update(b_s=(GROUPS, CHUNK), w_s=(GROUPS * CHUNK, CHUNK), b_gate=(2, D // N_CHIPS))
    flat = lambda t: [t[n].reshape(shapes[n]) for n in SMALL]
    per = D // N_CHIPS // 128

    def body(chip_ref, all_ref, first_ref, *refs):
        params, outs = refs[:27], refs[27:]
        sub = lax.broadcasted_iota(jnp.int32, (VEC_ROWS, 128), 0)
        sum_ref = outs[36]
        total = all_ref[0, 0:PACK_ROWS, :]
        head = first_ref[0, 0:VEC_ROWS, :]
        for k in range(1, 2 * N_CHIPS):
            total = total + all_ref[k // 2, (k % 2) * PACK_ROWS:(k % 2 + 1) * PACK_ROWS, :]
            head = head + first_ref[k // 2, (k % 2) * VEC_ROWS:(k % 2 + 1) * VEC_ROWS, :]
        sum_ref[...] = total
        sum_ref[0:VEC_ROWS, :] = head

        def gate_row(r):
            rows = sum_ref[BG_ROW + r * VEC_ROWS:BG_ROW + (r + 1) * VEC_ROWS, :]
            return jnp.concatenate([jnp.sum(jnp.where(sub == per * chip_ref[0] + j, rows, 0.0), axis=0, keepdims=True)
                                    for j in range(per)], axis=1)

        for i, n in enumerate(SMALL):
            if n == "b_s":
                g = sum_ref[i * VEC_ROWS:(i + 1) * VEC_ROWS, :]
            elif n == "w_s":
                g = sum_ref[WS_ROW:BG_ROW, :]
            elif n == "b_gate":
                g = jnp.concatenate([gate_row(0), gate_row(1)], axis=0)
            else:
                g = jnp.concatenate([sum_ref[i * VEC_ROWS + j:i * VEC_ROWS + j + 1, :] for j in range(VEC_ROWS)],
                                    axis=1)
            delta, nm, nv = _adamw_math(params[i][...], g, params[9 + i][...], params[18 + i][...])
            outs[4 * i][...], outs[4 * i + 1][...], outs[4 * i + 2][...], outs[4 * i + 3][...] = g, delta, nm, nv

    vm = pl.BlockSpec(memory_space=pltpu.VMEM)
    res = pl.pallas_call(
        body, name="adamw_small",
        in_specs=[pl.BlockSpec(memory_space=pltpu.SMEM)] + [vm] * 29, out_specs=[vm] * 37,
        out_shape=[jax.ShapeDtypeStruct(shapes[n], F32) for n in SMALL for _ in range(4)]
        + [jax.ShapeDtypeStruct((PACK_ROWS, 128), F32)],
        compiler_params=_params(),
    )(chip, gathered, first, *flat(w), *flat(m), *flat(v))
    new = {n: tuple(r.reshape(w[n].shape) for r in res[4 * i:4 * i + 4]) for i, n in enumerate(SMALL)}
    return new, res[36][LOSS_ROW, 0]


def kernel(x, norm_mix_pre, w_in, b_gate, ln_v_g, ln_v_b, w_s, b_s, w_a_proj, w_b_proj, w_out, norm_mix_post, norm_ffn_pre, w_ff1, w_ff2, norm_ffn_post, loss_target, m_norm_mix_pre, m_w_in, m_b_gate, m_ln_v_g, m_ln_v_b, m_w_s, m_b_s, m_w_a_proj, m_w_b_proj, m_w_out, m_norm_mix_post, m_norm_ffn_pre, m_w_ff1, m_w_ff2, m_norm_ffn_post, v_norm_mix_pre, v_w_in, v_b_gate, v_ln_v_g, v_ln_v_b, v_w_s, v_b_s, v_w_a_proj, v_w_b_proj, v_w_out, v_norm_mix_post, v_norm_ffn_pre, v_w_ff1, v_w_ff2, v_norm_ffn_post):
    w = dict(norm_mix_pre=norm_mix_pre, w_in=w_in, b_gate=b_gate, ln_v_g=ln_v_g, ln_v_b=ln_v_b, w_s=w_s, b_s=b_s,
             w_a_proj=w_a_proj, w_b_proj=w_b_proj, w_out=w_out, norm_mix_post=norm_mix_post,
             norm_ffn_pre=norm_ffn_pre, w_ff1=w_ff1, w_ff2=w_ff2, norm_ffn_post=norm_ffn_post)
    m = dict(norm_mix_pre=m_norm_mix_pre, w_in=m_w_in, b_gate=m_b_gate, ln_v_g=m_ln_v_g, ln_v_b=m_ln_v_b, w_s=m_w_s,
             b_s=m_b_s, w_a_proj=m_w_a_proj, w_b_proj=m_w_b_proj, w_out=m_w_out, norm_mix_post=m_norm_mix_post,
             norm_ffn_pre=m_norm_ffn_pre, w_ff1=m_w_ff1, w_ff2=m_w_ff2, norm_ffn_post=m_norm_ffn_post)
    v = dict(norm_mix_pre=v_norm_mix_pre, w_in=v_w_in, b_gate=v_b_gate, ln_v_g=v_ln_v_g, ln_v_b=v_ln_v_b, w_s=v_w_s,
             b_s=v_b_s, w_a_proj=v_w_a_proj, w_b_proj=v_w_b_proj, w_out=v_w_out, norm_mix_post=v_norm_mix_post,
             norm_ffn_pre=v_norm_ffn_pre, w_ff1=v_w_ff1, w_ff2=v_w_ff2, norm_ffn_post=v_norm_ffn_post)
    chip = 2 * lax.axis_index("x") + lax.axis_index("y")
    core = lax.axis_index("c")

    where = jnp.stack([chip, core]).astype(jnp.int32)
    wg_in = place_shard("place_w_in", w_in[0], where, BF16, 256)
    bg_all = place_shard("place_b_gate", jnp.pad(b_gate[0], ((0, 14), (0, 0))), where, F32, 16)
    vecs = (norm_mix_pre, ln_v_g, ln_v_b, norm_mix_post, norm_ffn_pre, norm_ffn_post)
    loss, dx, _, small, parts, got, packed, started = local_step(
        x[0], loss_target[0], vecs, w_s[0], b_s[0].T, bg_all, wg_in, [w[n][0] for n in BIG[1:]],
        core=jnp.reshape(core, (1,)).astype(jnp.int32),
        order=jnp.stack([chip, chip ^ 2, chip ^ 1, chip ^ 3]).astype(jnp.int32), where=where)

    halves = [sum_chips("sum_" + n, p, r, where, min(p.shape[1], 256), started[8])
              for n, p, r in zip(BIG[1:], parts, got)]
    joined, _ = join_halves(halves)
    grads = dict(zip(BIG[1:], joined))
    new = {}

    def update(n):
        shape = w[n].shape
        res = adamw("adamw_" + n, w[n][0], grads[n], m[n][0], v[n][0], min(shape[1], 256))
        new[n] = tuple(r.reshape(shape) for r in res)

    for n in BIG[1:]:
        update(n)
    p_in, got_in = scatter_wait(started, [new["w_ff2"][1], dx])
    (grads["w_in"],), first = join_halves([sum_chips("sum_w_in", p_in, got_in, where, 256)],
                                          pack_vector(small["norm_mix_pre"], where))
    update("w_in")
    small_new, loss = adamw_small(packed, first, jnp.reshape(chip, (1,)).astype(jnp.int32), w, m, v)
    new.update(small_new)

    outs = [loss, dx[None]]
    for i in range(4):
        outs += [new[n][i] for n in ORDER]
    return tuple(outs)
```

```python
import functools
import math
import typing

import numpy as np
import jax
import jax.numpy as jnp
from jax import lax
from jax.experimental import pallas as pl
from jax.experimental.pallas import tpu as pltpu

F32 = jnp.float32
BF16 = jnp.bfloat16
MESH = pl.DeviceIdType.MESH

D = 1024
EPS = 1e-6
CHUNK = 128
GROUPS = 8
HEADS = 16
HEAD_DIM = 64
ATT_T = 256
ATT_GROUP = 8
ATT_BWD_GROUP = 4
N_CHIPS = 4
D_FF = 4 * D
IN_COLS = 7 * D
IN_SHARD = IN_COLS // N_CHIPS
MASKED = -1e30
VMEM_LIMIT = 56 * 2 ** 20

ADAM_LR, ADAM_B1, ADAM_B2, ADAM_EPS, ADAM_WD, ADAM_STEP = 0.001, 0.9, 0.999, 1e-08, 0.01, 10

NN = (((1,), (0,)), ((), ()))
NT = (((1,), (1,)), ((), ()))
TN = (((0,), (0,)), ((), ()))


def _dot(a, b, dims=NN):
    return lax.dot_general(a, b, dims, preferred_element_type=F32)


def _params(*sem, communicates=False):
    return pltpu.CompilerParams(dimension_semantics=sem or None, vmem_limit_bytes=VMEM_LIMIT,
                                has_side_effects=communicates)


def _rows(tr, c, col=0):
    return pl.BlockSpec((tr, c), lambda i: (i, col))


def _full(shape):
    n = len(shape)
    return pl.BlockSpec(shape, lambda *_: (0,) * n)


def _gelu(x):
    k = math.sqrt(2.0 / math.pi)
    return 0.5 * x * (1.0 + jnp.tanh(k * (x + 0.044715 * x * x * x)))


def _gelu_and_grad(x):
    k = math.sqrt(2.0 / math.pi)
    t = jnp.tanh(k * (x + 0.044715 * x * x * x))
    g = 0.5 * x * (1.0 + t)
    dg = 0.5 * (1.0 + t) + 0.5 * x * (1.0 - t * t) * (k * (1.0 + 3.0 * 0.044715 * x * x))
    return g, dg


def _sigmoid(x):
    return 1.0 / (1.0 + jnp.exp(-x))


def _rms(x):
    r = lax.rsqrt(jnp.mean(x * x, axis=-1, keepdims=True) + EPS)
    return x * r, r


def _rms_bwd(dn, xhat, r):
    return r * (dn - xhat * jnp.mean(dn * xhat, axis=-1, keepdims=True))


def norm_pre(x, g):
    s = x.shape[0]
    tr = 512

    def body(x_ref, g_ref, h_ref):
        xhat, _ = _rms(x_ref[...])
        h_ref[...] = (xhat * g_ref[...]).astype(BF16)

    return pl.pallas_call(
        body, name="norm_pre", grid=(s // tr,),
        in_specs=[_rows(tr, D), _full((1, D))], out_specs=_rows(tr, D),
        out_shape=jax.ShapeDtypeStruct((s, D), BF16), compiler_params=_params("parallel"),
    )(x, g)


def mm_in(h, wg, order, staged, casting=()):
    s = h.shape[0]
    tm, tn = 1024, IN_SHARD // 2
    per = IN_SHARD // tn
    m = len(casting)
    nj, ni = N_CHIPS * per, s // tm
    cast_steps = per * ni

    def body(order_ref, *refs):
        a_ref = refs[0]
        cast_in = refs[2:2 + m]
        o_ref, held = refs[2 + m], refs[3 + m]
        cast_out = refs[4 + m:4 + 2 * m]
        tile, tile_sem = refs[4 + 2 * m:6 + 2 * m]
        sems = refs[6 + 2 * m:]
        j, i = pl.program_id(0), pl.program_id(1)

        @pl.when(j * ni + i < cast_steps)
        def _():
            for src, dst in zip(cast_in, cast_out):
                dst[...] = src[...].astype(BF16)

        def fetch(t):
            chip = order_ref[t // per]
            return pltpu.make_async_copy(held.at[chip, :, pl.ds((t % per) * tn, tn)], tile.at[t % 2],
                                         tile_sem.at[t % 2])

        if staged:
            near = _gather_phases([held], *sems[:2], [(0, D, (0, 1))])
            far = _relay_phases(held, *sems[2:])

        @pl.when(i == 0)
        def _():
            @pl.when(j == 0)
            def _():
                if staged:
                    near[0]()
                fetch(0).start()

            fetch(j).wait()
            ahead = j + 1 < nj
            if staged:
                ahead = ahead & (j + 1 != per) & (j + 1 != 3 * per)
            pl.when(ahead)(lambda: fetch(j + 1).start())

        rows = pl.ds(pl.multiple_of(i * tm, tm), tm)
        o_ref[...] = _dot(a_ref[rows, :], tile[j % 2]).astype(BF16)

        if staged:
            @pl.when((i == ni - 1) & (j == per - 1))
            def _():
                near[1]()
                near[2]()
                far[0]()
                fetch(per).start()

            @pl.when((i == ni - 1) & (j == 3 * per - 1))
            def _():
                far[1]()
                far[2]()
                fetch(3 * per).start()

    def cast_block(j, i, o):
        return jnp.minimum(j * ni + i, cast_steps - 1)

    out = pl.pallas_call(
        body, name="mm_in",
        grid_spec=pltpu.PrefetchScalarGridSpec(
            num_scalar_prefetch=1, grid=(nj, ni),
            in_specs=[pl.BlockSpec((s, D), lambda j, i, o: (0, 0)), ANY]
            + [pl.BlockSpec((a.shape[0] // cast_steps, a.shape[1]), lambda j, i, o: (cast_block(j, i, o), 0))
               for a in casting],
            out_specs=[pl.BlockSpec((tm, tn), lambda j, i, o: (i, o[j // per] * per + j % per)), ANY]
            + [pl.BlockSpec((None, a.shape[0] // cast_steps, a.shape[1]),
                            lambda j, i, o: (o[0], cast_block(j, i, o), 0)) for a in casting],
            scratch_shapes=[pltpu.VMEM((2, D, tn), BF16), pltpu.SemaphoreType.DMA((2,))]
            + (_gather_sems(1) + _relay_sems() if staged else [])),
        out_shape=[jax.ShapeDtypeStruct((s, IN_COLS), BF16), jax.ShapeDtypeStruct(wg.shape, wg.dtype)]
        + [jax.ShapeDtypeStruct((N_CHIPS,) + a.shape, BF16) for a in casting],
        input_output_aliases={2: 1},
        compiler_params=_params("arbitrary", "arbitrary", communicates=staged),
    )(order, h, wg, *casting)
    return out[0], out[1], out[2:]


def _tril_ws(ws_ref, g):
    r = lax.broadcasted_iota(jnp.int32, (CHUNK, CHUNK), 0)
    c = lax.broadcasted_iota(jnp.int32, (CHUNK, CHUNK), 1)
    return jnp.where(c <= r, ws_ref[g], 0.0).astype(BF16)


def _layer_norm(v):
    mu = jnp.mean(v, axis=-1, keepdims=True)
    d = v - mu
    rstd = lax.rsqrt(jnp.mean(d * d, axis=-1, keepdims=True) + EPS)
    return d * rstd, rstd


def gating_fwd(z, ln_g, ln_b, w_s, bs_t, gathering):
    s = z.shape[0]
    n = len(gathering)
    steps = s // CHUNK

    def body(*refs):
        u_ref, v_ref, lg_ref, lb_ref, ws_ref, bst_ref = refs[:6]
        ya_ref = refs[6 + n]
        ci = pl.program_id(0)
        if n:
            send, pass_on, finish = _gather_phases(refs[7 + n:7 + 2 * n], *refs[7 + 2 * n:], _spans(gathering))
            pl.when(ci == 0)(send)
        ug = _gelu(u_ref[...].astype(F32))
        vhat, _ = _layer_norm(_gelu(v_ref[...].astype(F32)))
        vn = (vhat * lg_ref[...] + lb_ref[...]).astype(BF16)
        for g in range(GROUPS):
            cols = slice(g * CHUNK, (g + 1) * CHUNK)
            mixed = _dot(_tril_ws(ws_ref, g), vn[:, cols]) + bst_ref[:, g:g + 1]
            ya_ref[:, cols] = (ug[:, cols] * mixed).astype(BF16)
        if n:
            pl.when(ci == steps - 1)(pass_on)
            pl.when(ci == steps - 1)(finish)

    out = pl.pallas_call(
        body, name="gating_fwd", grid=(steps,),
        in_specs=[_rows(CHUNK, D, 0), _rows(CHUNK, D, 1), _full((1, D)), _full((1, D)),
                  _full((GROUPS, CHUNK, CHUNK)), _full((CHUNK, GROUPS))] + [ANY] * n,
        out_specs=[_rows(CHUNK, D)] + [ANY] * n,
        out_shape=[jax.ShapeDtypeStruct((s, D), BF16)]
        + [jax.ShapeDtypeStruct(a.shape, a.dtype) for a in _arrays(gathering)],
        input_output_aliases={6 + w: 1 + w for w in range(n)},
        scratch_shapes=_gather_sems(n) if n else [],
        compiler_params=_params("arbitrary", communicates=bool(n)),
    )(z, z, ln_g, ln_b, w_s, bs_t, *_arrays(gathering))
    return out[0], out[1:]


def _attn_tables(s):
    nd = s // ATT_T
    r = np.arange(ATT_T)[None, :, None]
    c = np.arange(ATT_T)[None, None, :]
    delta = np.arange(nd)[:, None, None] * ATT_T + r - c
    count = np.zeros(delta.shape, np.int64)
    for window, dilation in ((128, 1), (512, 4), (2048, 16)):
        count += (delta >= 0) & (delta % dilation == 0) & (delta <= window)
    logc = np.where(count > 0, np.log(np.maximum(count, 1)), MASKED)
    return jnp.asarray(logc, F32)


AUG = 3


def _split3_np(x):
    terms, rest = [], np.asarray(x, np.float64)
    for _ in range(AUG):
        term = np.asarray(rest.astype(jnp.bfloat16), np.float64)
        terms.append(term)
        rest = rest - term
    return terms


def _split3(x):
    terms, rest = [], x
    for _ in range(AUG):
        term = rest.astype(BF16).astype(F32)
        terms.append(term)
        rest = rest - term
    return terms


def _alibi_tables(s):
    nb = s // ATT_T
    slopes = np.exp2(-8.0 * np.arange(1, HEADS + 1, dtype=np.float64) / HEADS)
    ka = np.zeros((HEADS // 2, 2, ATT_T, 128), np.float32)
    kb = np.zeros((HEADS // 2, 2, nb, 128), np.float32)
    for p in range(HEADS // 2):
        for e in range(2):
            base = HEAD_DIM * (1 - e)
            for a, term in enumerate(_split3_np(slopes[2 * p + e] * np.arange(ATT_T))):
                ka[p, e, :, base + a] = term
            for a, term in enumerate(_split3_np(slopes[2 * p + e] * ATT_T * np.arange(nb))):
                kb[p, e, :, base + AUG + a] = term
            ka[p, e, :, base + 2 * AUG:base + 3 * AUG] = 1.0
    return jnp.asarray(ka), jnp.asarray(kb)


def _head_masks():
    lane = lax.broadcasted_iota(jnp.int32, (1, 128), 1)
    first = lane < HEAD_DIM

    def ones(e, n):
        base = HEAD_DIM * (1 - e)
        return ((lane >= base) & (lane < base + n)).astype(F32)

    return first, lane, ones


def _place3(lane, at, terms, other):
    for a, term in enumerate(terms):
        other = jnp.where(lane == at + a, term, other)
    return other


def attn_fwd(z, logc, ka, kb, gathering):
    s = z.shape[0]
    nq = s // ATT_T
    t = ATT_T
    n = len(gathering)
    grp = ATT_GROUP
    ngrp = HEADS // 2 // grp
    wide = 128 * grp
    qcol, kcol, vcol = 2 * D // wide, 3 * D // wide, 4 * D // wide

    def body(*refs):
        q_ref, k_ref, v_ref, lc_ref, ka_ref, kb_ref = refs[:6]
        y_ref, lse_ref = refs[6 + n:8 + n]
        q_s, k_s, v_s, m_s, l_s, acc_s = refs[8 + 2 * n:14 + 2 * n]
        gi, qi = pl.program_id(0), pl.program_id(1)
        first, lane, ones = _head_masks()
        if n:
            send, pass_on, finish = _gather_phases(refs[8 + n:8 + 2 * n], *refs[14 + 2 * n:], _spans(gathering))
            pl.when((gi == 0) & (qi == 0))(send)

        @pl.when(qi == 0)
        def _():
            sel = jnp.broadcast_to(first.astype(F32), (t, 128))
            for pr in range(grp):
                cols = slice(pr * 128, (pr + 1) * 128)
                for jb in range(nq):
                    kj = k_ref[jb * t:(jb + 1) * t, cols].astype(F32)
                    vj = v_ref[jb * t:(jb + 1) * t, cols].astype(F32)
                    k_s[pr, 0, jb] = jnp.where(first, kj, ka_ref[pr, 0] + kb_ref[pr, 0, jb:jb + 1, :]).astype(BF16)
                    k_s[pr, 1, jb] = jnp.where(first, ka_ref[pr, 1] + kb_ref[pr, 1, jb:jb + 1, :], kj).astype(BF16)
                    v_s[pr, jb, 0:t, 0:128] = jnp.where(first, vj, 0.0).astype(BF16)
                    v_s[pr, jb, t:2 * t, 0:128] = jnp.where(first, 0.0, vj).astype(BF16)
                    v_s[pr, jb, 0:t, 128:256] = sel.astype(BF16)
                    v_s[pr, jb, t:2 * t, 128:256] = (1.0 - sel).astype(BF16)

        for pr in range(grp):
            q = q_ref[:, pr * 128:(pr + 1) * 128].astype(F32) * (1.0 / math.sqrt(HEAD_DIM))
            q_s[pr, 0] = jnp.where(first, q, ones(0, 2 * AUG)).astype(BF16)
            q_s[pr, 1] = jnp.where(first, ones(1, 2 * AUG), q).astype(BF16)
        m_s[...] = jnp.full_like(m_s, MASKED)
        l_s[...] = jnp.zeros_like(l_s)
        acc_s[...] = jnp.zeros_like(acc_s)

        def scores(j):
            return tuple(_dot(q_s[pr, e], k_s[pr, e, j], NT) for pr in range(grp) for e in range(2))

        def step(j, carry):
            softmax_block(j, scores(j))
            return carry

        def softmax_block(j, u):
            lc = lc_ref[qi - j]
            for pr in range(grp):
                u0 = u[2 * pr] + lc
                u1 = u[2 * pr + 1] + lc
                m0, m1 = m_s[pr, 0], m_s[pr, 1]
                n0 = jnp.maximum(m0, jnp.max(u0, axis=-1, keepdims=True))
                n1 = jnp.maximum(m1, jnp.max(u1, axis=-1, keepdims=True))
                m_s[pr, 0], m_s[pr, 1] = n0, n1
                p = jnp.concatenate([jnp.exp(u0 - jnp.concatenate([n0, n0], axis=1)).astype(BF16),
                                     jnp.exp(u1 - jnp.concatenate([n1, n1], axis=1)).astype(BF16)], axis=1)
                pv = _dot(p, v_s[pr, j])
                alpha = jnp.where(first, jnp.exp(m0 - n0), jnp.exp(m1 - n1))
                acc_s[pr] = acc_s[pr] * alpha + pv[:, 0:128]
                l_s[pr] = l_s[pr] * alpha + pv[:, 128:256]

        lax.fori_loop(0, qi + 1, step, 0)
        for pr in range(grp):
            cols = slice(pr * 128, (pr + 1) * 128)
            y_ref[:, cols] = (acc_s[pr] / l_s[pr]).astype(BF16)
            lse_ref[:, cols] = jnp.where(first, m_s[pr, 0], m_s[pr, 1]) + jnp.log(l_s[pr])
        if n:
            pl.when((gi == ngrp - 1) & (qi == nq - 1))(pass_on)
            pl.when((gi == ngrp - 1) & (qi == nq - 1))(finish)

    out = pl.pallas_call(
        body, name="attn_fwd", grid=(ngrp, nq),
        in_specs=[pl.BlockSpec((t, wide), lambda g, i: (i, qcol + g)),
                  pl.BlockSpec((s, wide), lambda g, i: (0, kcol + g)),
                  pl.BlockSpec((s, wide), lambda g, i: (0, vcol + g)),
                  _full((nq, t, t)),
                  pl.BlockSpec((grp, 2, t, 128), lambda g, i: (g, 0, 0, 0)),
                  pl.BlockSpec((grp, 2, nq, 128), lambda g, i: (g, 0, 0, 0))] + [ANY] * n,
        out_specs=[pl.BlockSpec((t, wide), lambda g, i: (i, g)), pl.BlockSpec((t, wide), lambda g, i: (i, g))]
        + [ANY] * n,
        out_shape=[jax.ShapeDtypeStruct((s, D), BF16), jax.ShapeDtypeStruct((s, D), F32)]
        + [jax.ShapeDtypeStruct(a.shape, a.dtype) for a in _arrays(gathering)],
        input_output_aliases={6 + w: 2 + w for w in range(n)},
        scratch_shapes=[pltpu.VMEM((grp, 2, t, 128), BF16), pltpu.VMEM((grp, 2, nq, t, 128), BF16),
                        pltpu.VMEM((grp, nq, 2 * t, 256), BF16), pltpu.VMEM((grp, 2, t, 128), F32),
                        pltpu.VMEM((grp, t, 128), F32), pltpu.VMEM((grp, t, 128), F32)]
        + (_gather_sems(n) if n else []),
        compiler_params=_params("arbitrary", "arbitrary", communicates=bool(n)),
    )(z, z, z, logc, ka, kb, *_arrays(gathering))
    return out[0], out[1], out[2:]


def proj_merge(ya, yb, wa, wb, z, bg, gathering):
    s = ya.shape[0]
    tm = 512
    n = len(gathering)
    steps = s // tm

    def body(*refs):
        ya_ref, yb_ref, wa_ref, wb_ref, ga_ref, gb_ref, bg_ref = refs[:7]
        mg_ref, pa_ref, pb_ref = refs[7 + n:10 + n]
        i = pl.program_id(0)
        if n:
            send, pass_on, finish = _gather_phases(refs[10 + n:10 + 2 * n], *refs[10 + 2 * n:], _spans(gathering))
            pl.when(i == 0)(send)
            pl.when(i == steps - 1)(pass_on)
        pa = _dot(ya_ref[...], wa_ref[...])
        pb = _dot(yb_ref[...], wb_ref[...])
        sa = _sigmoid(ga_ref[...] + bg_ref[0:1, :])
        sb = _sigmoid(gb_ref[...] + bg_ref[1:2, :])
        mg_ref[...] = (sa * pa + sb * pb).astype(BF16)
        pa_ref[...] = pa.astype(BF16)
        pb_ref[...] = pb.astype(BF16)
        if n:
            pl.when(i == steps - 1)(finish)

    out = jax.ShapeDtypeStruct((s, D), BF16)
    res = pl.pallas_call(
        body, name="proj_merge", grid=(steps,),
        in_specs=[_rows(tm, D), _rows(tm, D), _full((D, D)), _full((D, D)),
                  _rows(tm, D, 5), _rows(tm, D, 6), _full((2, D))] + [ANY] * n,
        out_specs=[_rows(tm, D)] * 3 + [ANY] * n,
        out_shape=[out] * 3 + [jax.ShapeDtypeStruct(a.shape, a.dtype) for a in _arrays(gathering)],
        input_output_aliases={7 + w: 3 + w for w in range(n)},
        scratch_shapes=_gather_sems(n) if n else [],
        compiler_params=_params("arbitrary", communicates=bool(n)),
    )(ya, yb, wa, wb, z, z, bg, *_arrays(gathering))
    return res[0], res[1], res[2], res[3:]


def out_norm(merged, w_out, x, g_post, g_fpre, gathering):
    s = x.shape[0]
    tm = 512
    n = len(gathering)
    steps = s // tm

    def body(*refs):
        mg_ref, w_ref, x_ref, gp_ref, gf_ref = refs[:5]
        o_ref, x1_ref, h2_ref = refs[5 + n:8 + n]
        i = pl.program_id(0)
        if n:
            send, pass_on, finish = _gather_phases(refs[8 + n:8 + 2 * n], *refs[8 + 2 * n:], _spans(gathering))
            pl.when(i == 0)(send)
            pl.when(i == steps - 1)(pass_on)
        o = _dot(mg_ref[...], w_ref[...])
        ohat, _ = _rms(o)
        x1 = x_ref[...] + ohat * gp_ref[...]
        x1hat, _ = _rms(x1)
        o_ref[...] = o
        x1_ref[...] = x1
        h2_ref[...] = (x1hat * gf_ref[...]).astype(BF16)
        if n:
            pl.when(i == steps - 1)(finish)

    res = pl.pallas_call(
        body, name="out_norm", grid=(steps,),
        in_specs=[_rows(tm, D), _full((D, D)), _rows(tm, D), _full((1, D)), _full((1, D))] + [ANY] * n,
        out_specs=[_rows(tm, D)] * 3 + [ANY] * n,
        out_shape=[jax.ShapeDtypeStruct((s, D), F32), jax.ShapeDtypeStruct((s, D), F32),
                   jax.ShapeDtypeStruct((s, D), BF16)]
        + [jax.ShapeDtypeStruct(a.shape, a.dtype) for a in _arrays(gathering)],
        input_output_aliases={5 + w: 3 + w for w in range(n)},
        scratch_shapes=_gather_sems(n) if n else [],
        compiler_params=_params("arbitrary", communicates=bool(n)),
    )(merged, w_out, x, g_post, g_fpre, *_arrays(gathering))
    return res[0], res[1], res[2], res[3:]


def mm_ff1(h2, wg, gathering):
    s = h2.shape[0]
    tm = 1024
    n = len(gathering)
    ni = s // tm

    def body(*refs):
        a_ref, b_ref = refs[:2]
        o_ref, r_ref = refs[2 + n:4 + n]
        i, j = pl.program_id(0), pl.program_id(1)
        if n:
            send, pass_on, finish = _gather_phases(refs[4 + n:4 + 2 * n], *refs[4 + 2 * n:], _spans(gathering))
            pl.when((i == 0) & (j == 0))(send)
            pl.when((i == ni - 1) & (j == N_CHIPS // 2))(pass_on)
        a = _dot(a_ref[...], b_ref[...])
        o_ref[...] = a.astype(BF16)
        r = jnp.maximum(a, 0.0)
        r_ref[...] = (r * r).astype(BF16)
        if n:
            pl.when((i == ni - 1) & (j == N_CHIPS - 1))(finish)

    res = pl.pallas_call(
        body, name="mm_ff1", grid=(ni, N_CHIPS),
        in_specs=[pl.BlockSpec((tm, D), lambda i, j: (i, 0)), pl.BlockSpec((None, D, D), lambda i, j: (j, 0, 0))]
        + [ANY] * n,
        out_specs=[pl.BlockSpec((tm, D), lambda i, j: (i, j))] * 2 + [ANY] * n,
        out_shape=[jax.ShapeDtypeStruct((s, D_FF), BF16), jax.ShapeDtypeStruct((s, D_FF), BF16)]
        + [jax.ShapeDtypeStruct(a.shape, a.dtype) for a in _arrays(gathering)],
        input_output_aliases={2 + w: 2 + w for w in range(n)},
        scratch_shapes=_gather_sems(n) if n else [],
        compiler_params=_params("arbitrary", "arbitrary", communicates=bool(n)),
    )(h2, wg, *_arrays(gathering))
    return res[0], res[1], res[2:]


def ff2_loss(rl, w_ff2, x1, target, g_fpost):
    s = x1.shape[0]
    tm = 256

    def body(rl_ref, w_ref, x1_ref, t_ref, g_ref, dy_ref, df_ref, dg_ref, loss_ref):
        @pl.when(pl.program_id(0) == 0)
        def _():
            dg_ref[...] = jnp.zeros_like(dg_ref)
            loss_ref[...] = jnp.zeros_like(loss_ref)

        f = _dot(rl_ref[...], w_ref[...])
        fhat, r = _rms(f)
        err = x1_ref[...] + fhat * g_ref[...] - t_ref[...]
        loss_ref[...] += 0.5 * jnp.sum(jnp.mean(err * err, axis=-1, keepdims=True), axis=0, keepdims=True)
        dy = err * (1.0 / D)
        dy_ref[...] = dy
        dg_ref[...] += jnp.sum(dy * fhat, axis=0, keepdims=True)
        df_ref[...] = _rms_bwd(dy * g_ref[...], fhat, r).astype(BF16)

    return pl.pallas_call(
        body, name="ff2_loss", grid=(s // tm,),
        in_specs=[_rows(tm, D_FF), _full((D_FF, D)), _rows(tm, D), _rows(tm, D), _full((1, D))],
        out_specs=[_rows(tm, D), _rows(tm, D), _full((1, D)), _full((1, 1))],
        out_shape=[jax.ShapeDtypeStruct((s, D), F32), jax.ShapeDtypeStruct((s, D), BF16),
                   jax.ShapeDtypeStruct((1, D), F32), jax.ShapeDtypeStruct((1, 1), F32)],
        compiler_params=_params("arbitrary"),
    )(rl, w_ff2, x1, target, g_fpost)


def mm_tn(name, a, b, ta, tb, out_shape, out_spec):
    s = a.shape[0]

    def body(a_ref, b_ref, o_ref):
        o_ref[...] = _dot(a_ref[...], b_ref[...], TN)

    return pl.pallas_call(
        body, name=name, grid=(a.shape[1] // ta, b.shape[1] // tb),
        in_specs=[pl.BlockSpec((s, ta), lambda i, j: (0, i)), pl.BlockSpec((s, tb), lambda i, j: (0, j))],
        out_specs=out_spec, out_shape=jax.ShapeDtypeStruct(out_shape, F32),
        compiler_params=_params("parallel", "parallel"),
    )(a, b)


def mm_nt(name, a, w):
    s = a.shape[0]
    tm = 512

    def body(a_ref, w_ref, o_ref):
        o_ref[...] = _dot(a_ref[...], w_ref[...], NT).astype(BF16)

    return pl.pallas_call(
        body, name=name, grid=(s // tm,), in_specs=[_rows(tm, D), _full((D, D))], out_specs=_rows(tm, D),
        out_shape=jax.ShapeDtypeStruct((s, D), BF16), compiler_params=_params("parallel"),
    )(a, w)


def ff2_bwd(df, w_ff2, a):
    s = df.shape[0]
    tm = 1024

    def body(df_ref, w_ref, a_ref, da_ref):
        drl = _dot(df_ref[...], w_ref[...], NT)
        da_ref[...] = (drl * (2.0 * jnp.maximum(a_ref[...].astype(F32), 0.0))).astype(BF16)

    return pl.pallas_call(
        body, name="ff2_bwd", grid=(s // tm, D_FF // D),
        in_specs=[pl.BlockSpec((tm, D), lambda i, j: (i, 0)), pl.BlockSpec((D, D), lambda i, j: (j, 0)),
                  pl.BlockSpec((tm, D), lambda i, j: (i, j))],
        out_specs=pl.BlockSpec((tm, D), lambda i, j: (i, j)),
        out_shape=jax.ShapeDtypeStruct((s, D_FF), BF16), compiler_params=_params("parallel", "parallel"),
    )(df, w_ff2, a)


def ff1_bwd_norms(da, wg, x1, o, dy, g_fpre, g_post, swapping):
    s = x1.shape[0]
    tm = 256
    n = len(swapping)
    steps = s // tm

    def body(*refs):
        da_ref, w_ref, x1_ref, o_ref, dy_ref, gf_ref, gp_ref = refs[:7]
        dx1_ref, do_ref, dgf_ref, dgp_ref = refs[7 + n:11 + n]
        i = pl.program_id(0)
        if n:
            send, finish = _swap_phases(refs[7:7 + n], refs[11 + n:11 + 2 * n], *refs[11 + 2 * n:])
            pl.when(i == 0)(send)

        @pl.when(i == 0)
        def _():
            dgf_ref[...] = jnp.zeros_like(dgf_ref)
            dgp_ref[...] = jnp.zeros_like(dgp_ref)

        dh2 = _dot(da_ref[:, 0:D], w_ref[0], NT)
        for k in range(1, N_CHIPS):
            dh2 = dh2 + _dot(da_ref[:, k * D:(k + 1) * D], w_ref[k], NT)
        x1hat, r2 = _rms(x1_ref[...])
        dgf_ref[...] += jnp.sum(dh2 * x1hat, axis=0, keepdims=True)
        dx1 = dy_ref[...] + _rms_bwd(dh2 * gf_ref[...], x1hat, r2)
        ohat, r1 = _rms(o_ref[...])
        dgp_ref[...] += jnp.sum(dx1 * ohat, axis=0, keepdims=True)
        dx1_ref[...] = dx1
        do_ref[...] = _rms_bwd(dx1 * gp_ref[...], ohat, r1).astype(BF16)
        if n:
            pl.when(i == steps - 1)(finish)

    res = pl.pallas_call(
        body, name="ff1_bwd_norms", grid=(steps,),
        in_specs=[_rows(tm, D_FF), _full((N_CHIPS, D, D)), _rows(tm, D), _rows(tm, D), _rows(tm, D),
                  _full((1, D)), _full((1, D))] + [ANY] * n,
        out_specs=[_rows(tm, D), _rows(tm, D), _full((1, D)), _full((1, D))] + [ANY] * n,
        out_shape=[jax.ShapeDtypeStruct((s, D), F32), jax.ShapeDtypeStruct((s, D), BF16),
                   jax.ShapeDtypeStruct((1, D), F32), jax.ShapeDtypeStruct((1, D), F32)] + _swap_shapes(swapping),
        scratch_shapes=_swap_sems(n) if n else [],
        compiler_params=_params("arbitrary", communicates=bool(n)),
    )(da, wg, x1, o, dy, g_fpre, g_post, *swapping)
    return res[0], res[1], res[2], res[3], res[4:]


def out_bwd_gates(do, w_out, pa, pb, z, bg):
    s = do.shape[0]
    tm = 512

    def body(do_ref, w_ref, pa_ref, pb_ref, ga_ref, gb_ref, bg_ref, dpa_ref, dpb_ref, dga_ref, dgb_ref, dbg_ref):
        @pl.when(pl.program_id(0) == 0)
        def _():
            dbg_ref[...] = jnp.zeros_like(dbg_ref)

        dm = _dot(do_ref[...], w_ref[...], NT)
        sa = _sigmoid(ga_ref[...] + bg_ref[0:1, :])
        sb = _sigmoid(gb_ref[...] + bg_ref[1:2, :])
        dpa_ref[...] = (dm * sa).astype(BF16)
        dpb_ref[...] = (dm * sb).astype(BF16)
        dga = dm * pa_ref[...].astype(F32) * (sa * (1.0 - sa))
        dgb = dm * pb_ref[...].astype(F32) * (sb * (1.0 - sb))
        dga_ref[...] = dga.astype(BF16)
        dgb_ref[...] = dgb.astype(BF16)
        dbg_ref[0:1, :] += jnp.sum(dga, axis=0, keepdims=True)
        dbg_ref[1:2, :] += jnp.sum(dgb, axis=0, keepdims=True)

    out = jax.ShapeDtypeStruct((s, D), BF16)
    return pl.pallas_call(
        body, name="out_bwd_gates", grid=(s // tm,),
        in_specs=[_rows(tm, D), _full((D, D)), _rows(tm, D), _rows(tm, D), _rows(tm, D, 5), _rows(tm, D, 6),
                  _full((2, D))],
        out_specs=[_rows(tm, D)] * 4 + [_full((2, D))],
        out_shape=[out] * 4 + [jax.ShapeDtypeStruct((2, D), F32)], compiler_params=_params("arbitrary"),
    )(do, w_out, pa, pb, z, z, bg)


def gating_bwd(z, dya, ln_g, ln_b, w_s, bs_t, swapping):
    s = z.shape[0]
    ones = functools.partial(jnp.ones, (8, CHUNK), BF16)
    n = len(swapping)

    def body(*refs):
        u_ref, v_ref, dya_ref, lg_ref, lb_ref, ws_ref, bst_ref = refs[:7]
        du_ref, dv_ref, dws_ref, dbs_ref, dlg_ref, dlb_ref = refs[7 + n:13 + n]
        dvn_ref = refs[13 + 2 * n]
        ci = pl.program_id(0)
        if n:
            send, finish = _swap_phases(refs[7:7 + n], refs[13 + n:13 + 2 * n], *refs[14 + 2 * n:])
            pl.when(ci == 0)(send)

        @pl.when(ci == 0)
        def _():
            dws_ref[...] = jnp.zeros_like(dws_ref)
            dbs_ref[...] = jnp.zeros_like(dbs_ref)
            dlg_ref[...] = jnp.zeros_like(dlg_ref)
            dlb_ref[...] = jnp.zeros_like(dlb_ref)

        ug, dug_du = _gelu_and_grad(u_ref[...].astype(F32))
        vg, dvg_dv = _gelu_and_grad(v_ref[...].astype(F32))
        vhat, rstd = _layer_norm(vg)
        vn = (vhat * lg_ref[...] + lb_ref[...]).astype(BF16)
        dya = dya_ref[...].astype(F32)
        for g in range(GROUPS):
            cols = slice(g * CHUNK, (g + 1) * CHUNK)
            ws = _tril_ws(ws_ref, g)
            mixed = _dot(ws, vn[:, cols]) + bst_ref[:, g:g + 1]
            du_ref[:, cols] = (dya[:, cols] * mixed * dug_du[:, cols]).astype(BF16)
            dmix = (dya[:, cols] * ug[:, cols]).astype(BF16)
            dbs_ref[g] += _dot(ones(), dmix, NT)
            dws_ref[g] += _dot(dmix, vn[:, cols], NT)
            dvn_ref[:, cols] = _dot(ws, dmix, TN)
        dvn = dvn_ref[...]
        dlg_ref[...] += jnp.sum(dvn * vhat, axis=0, keepdims=True)
        dlb_ref[...] += jnp.sum(dvn, axis=0, keepdims=True)
        dvh = dvn * lg_ref[...]
        dvg = rstd * (dvh - jnp.mean(dvh, axis=-1, keepdims=True)
                      - vhat * jnp.mean(dvh * vhat, axis=-1, keepdims=True))
        dv_ref[...] = (dvg * dvg_dv).astype(BF16)

        @pl.when(ci == pl.num_programs(0) - 1)
        def _():
            r = lax.broadcasted_iota(jnp.int32, (CHUNK, CHUNK), 0)
            c = lax.broadcasted_iota(jnp.int32, (CHUNK, CHUNK), 1)
            for g in range(GROUPS):
                dws_ref[g] = jnp.where(c <= r, dws_ref[g], 0.0)

        if n:
            pl.when(ci == pl.num_programs(0) - 1)(finish)

    out = jax.ShapeDtypeStruct((s, D), BF16)
    res = pl.pallas_call(
        body, name="gating_bwd", grid=(s // CHUNK,),
        in_specs=[_rows(CHUNK, D, 0), _rows(CHUNK, D, 1), _rows(CHUNK, D), _full((1, D)), _full((1, D)),
                  _full((GROUPS, CHUNK, CHUNK)), _full((CHUNK, GROUPS))] + [ANY] * n,
        out_specs=[_rows(CHUNK, D), _rows(CHUNK, D), _full((GROUPS, CHUNK, CHUNK)), _full((GROUPS, 8, CHUNK)),
                   _full((1, D)), _full((1, D))] + [ANY] * n,
        out_shape=[out, out, jax.ShapeDtypeStruct((GROUPS, CHUNK, CHUNK), F32),
                   jax.ShapeDtypeStruct((GROUPS, 8, CHUNK), F32),
                   jax.ShapeDtypeStruct((1, D), F32), jax.ShapeDtypeStruct((1, D), F32)] + _swap_shapes(swapping),
        scratch_shapes=[pltpu.VMEM((CHUNK, D), F32)] + (_swap_sems(n) if n else []),
        compiler_params=_params("arbitrary", communicates=bool(n)),
    )(z, z, dya, ln_g, ln_b, w_s, bs_t, *swapping)
    return (*res[:6], res[6:])


def attn_bwd(z, yb, dyb, lse, logc, ka, kb, scattering, gathering=None):
    s = z.shape[0]
    nq = s // ATT_T
    t = ATT_T
    grp = ATT_BWD_GROUP
    ngrp = HEADS // 2 // grp
    wide = 128 * grp
    qcol, kcol, vcol = 2 * D // wide, 3 * D // wide, 4 * D // wide
    scale = 1.0 / math.sqrt(HEAD_DIM)
    n = len(scattering)
    g8 = 0 if gathering is None else 1

    def body(*refs):
        q_ref, k_ref, v_ref, y_ref, dy_ref, lse_ref, lc_ref, ka_ref, kb_ref = refs[:9]
        dq_ref, dk_ref, dv_ref = refs[9 + n + g8:12 + n + g8]
        qa_s, qt_s, da_s, dt_s, dq_s, dkt_s, dvt_s = refs[12 + 2 * n + 2 * g8:19 + 2 * n + 2 * g8]
        sems = refs[19 + 2 * n + 2 * g8:]
        gi, j = pl.program_id(0), pl.program_id(1)
        first, lane, ones = _head_masks()
        if n:
            send, finish = _scatter_phases(refs[9:9 + n], refs[12 + n + g8:12 + 2 * n + g8], *sems[:2])
            pl.when((gi == 0) & (j == 0))(send)
        if g8:
            send8, pass_on8, finish8 = _allgather8_phases(refs[12 + 2 * n + g8], *sems[2 * (n > 0):])
            pl.when((gi == 0) & (j == 0))(send8)
            pl.when((gi == ngrp - 1) & (j == nq - 1))(pass_on8)

        @pl.when(j == 0)
        def _():
            dq_s[...] = jnp.zeros_like(dq_s)
            for pr in range(grp):
                cols = slice(pr * 128, (pr + 1) * 128)
                for ib in range(nq):
                    rows = slice(ib * t, (ib + 1) * t)
                    q = q_ref[rows, cols].astype(F32) * scale
                    lse = lse_ref[rows, cols]
                    qa_s[pr, 0, ib] = jnp.where(first, q, _place3(lane, HEAD_DIM + 2 * AUG, _split3(-lse[:, 0:1]),
                                                                  ones(0, 2 * AUG))).astype(BF16)
                    qa_s[pr, 1, ib] = jnp.where(
                        first, _place3(lane, 2 * AUG, _split3(-lse[:, HEAD_DIM:HEAD_DIM + 1]), ones(1, 2 * AUG)),
                        q).astype(BF16)
                    qt_s[pr, ib, :, 0:t] = jnp.where(first, q, 0.0).T.astype(BF16)
                    qt_s[pr, ib, :, t:2 * t] = jnp.where(first, 0.0, q).T.astype(BF16)
                    do = dy_ref[rows, cols].astype(F32)
                    prod = do * y_ref[rows, cols].astype(F32)
                    dd0 = jnp.sum(jnp.where(first, prod, 0.0), axis=-1, keepdims=True)
                    dd1 = jnp.sum(jnp.where(first, 0.0, prod), axis=-1, keepdims=True)
                    da_s[pr, 0, ib] = jnp.where(first, do, _place3(lane, HEAD_DIM, _split3(-dd0), 0.0)).astype(BF16)
                    da_s[pr, 1, ib] = jnp.where(first, _place3(lane, 0, _split3(-dd1), 0.0), do).astype(BF16)
                    dt_s[pr, ib, :, 0:t] = jnp.where(first, do, 0.0).T.astype(BF16)
                    dt_s[pr, ib, :, t:2 * t] = jnp.where(first, 0.0, do).T.astype(BF16)

        keys = []
        for pr in range(grp):
            kj = k_ref[:, pr * 128:(pr + 1) * 128].astype(F32)
            vj = v_ref[:, pr * 128:(pr + 1) * 128].astype(F32)
            keys.append((
                jnp.where(first, kj, ka_ref[pr, 0] + kb_ref[pr, 0, pl.ds(j, 1), :]).astype(BF16),
                jnp.where(first, ka_ref[pr, 1] + kb_ref[pr, 1, pl.ds(j, 1), :], kj).astype(BF16),
                jnp.concatenate([jnp.where(first, kj, 0.0), jnp.where(first, 0.0, kj)], axis=0).astype(BF16),
                jnp.where(first, vj, ones(0, AUG)).astype(BF16),
                jnp.where(first, ones(1, AUG), vj).astype(BF16)))
        dkt_s[...] = jnp.zeros_like(dkt_s)
        dvt_s[...] = jnp.zeros_like(dvt_s)

        def step(i, _):
            lc = lc_ref[i - j]
            rows = pl.ds(pl.multiple_of(i * t, t), t)
            for pr in range(grp):
                k0a, k1a, kst, v0a, v1a = keys[pr]
                p0 = jnp.exp(_dot(qa_s[pr, 0, i], k0a, NT) + lc)
                p1 = jnp.exp(_dot(qa_s[pr, 1, i], k1a, NT) + lc)
                e0 = (p0 * _dot(da_s[pr, 0, i], v0a, NT)).astype(BF16)
                e1 = (p1 * _dot(da_s[pr, 1, i], v1a, NT)).astype(BF16)
                dq_s[pr, rows, :] += _dot(jnp.concatenate([e0, e1], axis=1), kst)
                dvt_s[pr] += _dot(dt_s[pr, i], jnp.concatenate([p0.astype(BF16), p1.astype(BF16)], axis=0))
                dkt_s[pr] += _dot(qt_s[pr, i], jnp.concatenate([e0, e1], axis=0))
            return 0

        lax.fori_loop(j, nq, step, 0)
        for pr in range(grp):
            dk_ref[:, pr * 128:(pr + 1) * 128] = dkt_s[pr].T.astype(BF16)
            dv_ref[:, pr * 128:(pr + 1) * 128] = dvt_s[pr].T.astype(BF16)

        @pl.when(j == nq - 1)
        def _():
            for pr in range(grp):
                dq_ref[:, pr * 128:(pr + 1) * 128] = (dq_s[pr] * scale).astype(BF16)

        if n:
            pl.when((gi == ngrp - 1) & (j == nq - 1))(finish)
        if g8:
            pl.when((gi == ngrp - 1) & (j == nq - 1))(finish8)

    colblock = lambda c: pl.BlockSpec((s, wide), lambda g, j: (0, c + g))
    once = lambda c: pl.BlockSpec((s, wide), lambda g, j: (0, c + g), pipeline_mode=pl.Buffered(1))
    blk = lambda c: pl.BlockSpec((t, wide), lambda g, j: (j, c + g))
    out = jax.ShapeDtypeStruct((s, D), BF16)
    res = pl.pallas_call(
        body, name="attn_bwd", grid=(ngrp, nq),
        in_specs=[once(qcol), blk(kcol), blk(vcol), once(0), once(0), once(0),
                  pl.BlockSpec((nq, t, t), lambda g, j: (0, 0, 0), pipeline_mode=pl.Buffered(1)),
                  pl.BlockSpec((grp, 2, t, 128), lambda g, j: (g, 0, 0, 0)),
                  pl.BlockSpec((grp, 2, nq, 128), lambda g, j: (g, 0, 0, 0))] + [ANY] * (n + g8),
        out_specs=[colblock(0), blk(0), blk(0)] + [ANY] * (n + g8),
        out_shape=[out] * 3 + _scatter_shapes(scattering)
        + ([jax.ShapeDtypeStruct(gathering.shape, gathering.dtype)] if g8 else []),
        input_output_aliases={9 + n: 3 + n} if g8 else {},
        scratch_shapes=[pltpu.VMEM((grp, 2, nq, t, 128), BF16), pltpu.VMEM((grp, nq, 128, 2 * t), BF16),
                        pltpu.VMEM((grp, 2, nq, t, 128), BF16), pltpu.VMEM((grp, nq, 128, 2 * t), BF16),
                        pltpu.VMEM((grp, s, 128), F32), pltpu.VMEM((grp, 128, t), F32),
                        pltpu.VMEM((grp, 128, t), F32)]
        + (_scatter_sems(n) if n else [])
        + ([pltpu.SemaphoreType.DMA((7,)), pltpu.SemaphoreType.DMA((7,))] if g8 else []),
        compiler_params=_params("arbitrary", "arbitrary", communicates=bool(n + g8)),
    )(z, z, z, yb, dyb, lse, logc, ka, kb, *scattering, *([gathering] if g8 else []))
    return res[0], res[1], res[2], res[3:3 + n], (res[3 + n] if g8 else None)


def in_bwd_norm(dz, wg, x, dx1, g_pre, scattering, gathering=None):
    s = x.shape[0]
    tm = 512
    n = len(scattering)
    g = 0 if gathering is None else 1
    last = (s // tm - 1, N_CHIPS - 1)

    def body(*refs):
        dz_ref, w_ref, x_ref, dx1_ref, g_ref = refs[:5]
        dx_ref, dg_ref = refs[5 + n + g:7 + n + g]
        acc_ref = refs[7 + 2 * n + 2 * g]
        sems = refs[8 + 2 * n + 2 * g:]
        i, k = pl.program_id(0), pl.program_id(1)
        if n:
            send, finish = _scatter_phases(refs[5:5 + n], refs[7 + n + g:7 + 2 * n + g], *sems[:2])
            pl.when((i == 0) & (k == 0))(send)
        if g:
            send8, pass_on8, finish8 = _allgather8_phases(refs[7 + 2 * n + g], *sems[2 * (n > 0):])
            pl.when((i == 0) & (k == 0))(send8)
            pl.when((i == last[0]) & (k == last[1]))(pass_on8)

        @pl.when((i == 0) & (k == 0))
        def _():
            dg_ref[...] = jnp.zeros_like(dg_ref)

        part = _dot(dz_ref[...], w_ref[...], NT)

        @pl.when(k == 0)
        def _():
            acc_ref[...] = part

        @pl.when(k > 0)
        def _():
            acc_ref[...] += part

        @pl.when(k == N_CHIPS - 1)
        def _():
            dh = acc_ref[...]
            xhat, r = _rms(x_ref[...])
            dg_ref[...] += jnp.sum(dh * xhat, axis=0, keepdims=True)
            dx_ref[...] = dx1_ref[...] + _rms_bwd(dh * g_ref[...], xhat, r)

        if n:
            pl.when((i == last[0]) & (k == last[1]))(finish)
        if g:
            pl.when((i == last[0]) & (k == last[1]))(finish8)

    row = pl.BlockSpec((tm, D), lambda i, k: (i, 0))
    vec = pl.BlockSpec((1, D), lambda i, k: (0, 0))
    res = pl.pallas_call(
        body, name="in_bwd_norm", grid=(s // tm, N_CHIPS),
        in_specs=[pl.BlockSpec((tm, IN_SHARD), lambda i, k: (i, k)),
                  pl.BlockSpec((None, D, IN_SHARD), lambda i, k: (k, 0, 0)), row, row, vec] + [ANY] * (n + g),
        out_specs=[row, vec] + [ANY] * (n + g),
        out_shape=[jax.ShapeDtypeStruct((s, D), F32), jax.ShapeDtypeStruct((1, D), F32)]
        + _scatter_shapes(scattering) + ([jax.ShapeDtypeStruct(gathering.shape, gathering.dtype)] if g else []),
        input_output_aliases={5 + n: 2 + n} if g else {},
        scratch_shapes=[pltpu.VMEM((tm, D), F32)] + (_scatter_sems(n) if n else [])
        + ([pltpu.SemaphoreType.DMA((7,)), pltpu.SemaphoreType.DMA((7,))] if g else []),
        compiler_params=_params("arbitrary", "arbitrary", communicates=bool(n + g)),
    )(dz, wg, x, dx1, g_pre, *scattering, *([gathering] if g else []))
    return res[0], res[1], res[2:2 + n], (res[2 + n] if g else None)


def _adamw_math(w, g, m, v):
    m = ADAM_B1 * m + (1.0 - ADAM_B1) * g
    v = ADAM_B2 * v + (1.0 - ADAM_B2) * (g * g)
    m_hat = m / (1.0 - ADAM_B1 ** ADAM_STEP)
    v_hat = v / (1.0 - ADAM_B2 ** ADAM_STEP)
    delta = -ADAM_LR * (m_hat / (jnp.sqrt(v_hat) + ADAM_EPS) + ADAM_WD * w)
    return delta, m, v


def adamw(name, w, g, m, v, tr):
    r, c = w.shape

    def body(w_ref, g_ref, m_ref, v_ref, go_ref, d_ref, nm_ref, nv_ref):
        g = g_ref[...]
        go_ref[...] = g
        d_ref[...], nm_ref[...], nv_ref[...] = _adamw_math(w_ref[...], g, m_ref[...], v_ref[...])

    out = jax.ShapeDtypeStruct((r, c), F32)
    return pl.pallas_call(
        body, name=name, grid=(r // tr,), in_specs=[_rows(tr, c)] * 4, out_specs=[_rows(tr, c)] * 4,
        out_shape=[out] * 4, compiler_params=_params("parallel"),
    )(w, g, m, v)


def _allgather8_phases(buf, send_sems, recv_sems):
    x, y, c, chips = _place()
    me = 2 * x + y
    sibling = (x, y, 1 - c)
    rows = buf.shape[1] // 2

    def part(chip, core):
        return buf.at[chip, pl.ds(core * rows, rows)]

    def copy(k, block, to):
        return pltpu.make_async_remote_copy(src_ref=block, dst_ref=block, send_sem=send_sems.at[k],
                                            recv_sem=recv_sems.at[k], device_id=to, device_id_type=MESH)

    def chip_of(j):
        return 2 * chips[j][0] + chips[j][1]

    def send():
        copy(0, part(me, c), sibling).start()
        for j in range(3):
            copy(1 + j, part(me, c), (chips[j][0], chips[j][1], c)).start()

    def pass_on():
        for j in range(3):
            copy(1 + j, part(chip_of(j), c), (chips[j][0], chips[j][1], c)).wait_recv()
            copy(4 + j, part(chip_of(j), c), sibling).start()

    def finish():
        copy(0, part(me, 1 - c), sibling).wait_recv()
        for j in range(3):
            copy(4 + j, part(chip_of(j), 1 - c), sibling).wait_recv()
        copy(0, part(me, c), sibling).wait_send()
        for j in range(3):
            copy(1 + j, part(me, c), (chips[j][0], chips[j][1], c)).wait_send()
            copy(4 + j, part(chip_of(j), c), sibling).wait_send()

    return send, pass_on, finish


def add_halves(name, g, recv, c_idx, tr):
    n, h, c = recv.shape

    def body(c_ref, g_ref, r_ref, o_ref):
        o_ref[...] = (g_ref[...] + r_ref[...]).astype(BF16)

    nb = h // tr
    return pl.pallas_call(
        body, name=name,
        grid_spec=pltpu.PrefetchScalarGridSpec(
            num_scalar_prefetch=1, grid=(n, nb),
            in_specs=[pl.BlockSpec((None, tr, c), lambda k, i, c_ref: (k, c_ref[0] * nb + i, 0)),
                      pl.BlockSpec((None, tr, c), lambda k, i, c_ref: (k, i, 0))],
            out_specs=pl.BlockSpec((None, tr, c), lambda k, i, c_ref: (k, i, 0))),
        out_shape=jax.ShapeDtypeStruct((n, h, c), BF16), compiler_params=_params("parallel", "parallel"),
    )(c_idx, g, recv)


def sum_chips(name, parts, recv, where, tr, after=None):
    n, h, c = recv.shape
    nb = h // tr

    def body(w_ref, p_ref, r_ref, *rest):
        acc = p_ref[...].astype(F32)
        for k in range(n):
            acc = acc + r_ref[k].astype(F32)
        rest[-1][...] = acc

    return pl.pallas_call(
        body, name=name,
        grid_spec=pltpu.PrefetchScalarGridSpec(
            num_scalar_prefetch=1, grid=(nb,),
            in_specs=[pl.BlockSpec((None, tr, c), lambda i, w_ref: (w_ref[0], i, 0)),
                      pl.BlockSpec((n, tr, c), lambda i, w_ref: (0, i, 0))] + ([ANY] if after is not None else []),
            out_specs=pl.BlockSpec((tr, c), lambda i, w_ref: (w_ref[1] * nb + i, 0))),
        out_shape=jax.ShapeDtypeStruct((2 * h, c), F32), compiler_params=_params("parallel"),
    )(where, parts, recv, *([after] if after is not None else []))


def place_shard(name, shard, where, dtype, tr):
    r, c = shard.shape

    def body(w_ref, s_ref, o_ref):
        o_ref[...] = s_ref[...].astype(dtype)

    return pl.pallas_call(
        body, name=name,
        grid_spec=pltpu.PrefetchScalarGridSpec(
            num_scalar_prefetch=1, grid=(r // tr,),
            in_specs=[pl.BlockSpec((tr, c), lambda i, w_ref: (i, 0))],
            out_specs=pl.BlockSpec((None, tr, c), lambda i, w_ref: (w_ref[0], i, 0))),
        out_shape=jax.ShapeDtypeStruct((N_CHIPS, r, c), dtype), compiler_params=_params("parallel"),
    )(where, shard)


ANY = pl.BlockSpec(memory_space=pl.ANY)


def _place():
    x, y, c = lax.axis_index("x"), lax.axis_index("y"), lax.axis_index("c")
    chips = [(1 - x, y), (x, 1 - y), (1 - x, 1 - y)]
    return x, y, c, chips


def gather_shards(arrays):
    n = len(arrays)

    def body(*refs):
        send, pass_on, finish = _gather_phases(refs[n:2 * n], *refs[2 * n:], _spans(arrays))
        send()
        pass_on()
        finish()

    return pl.pallas_call(
        body, name="gather_shards", in_specs=[ANY] * n, out_specs=[ANY] * n,
        out_shape=[jax.ShapeDtypeStruct(a.shape, a.dtype) for a in _arrays(arrays)],
        input_output_aliases={w: w for w in range(n)}, scratch_shapes=_gather_sems(n),
        compiler_params=pltpu.CompilerParams(has_side_effects=True),
    )(*_arrays(arrays))


def _gather_sems(n):
    return [pltpu.SemaphoreType.DMA((6 * n,)), pltpu.SemaphoreType.DMA((6 * n,))]


class Span(typing.NamedTuple):
    array: jax.Array
    lo: int
    hi: int
    ways: tuple = (0, 1, 2)


def _arrays(gathering):
    return [g.array if isinstance(g, Span) else g for g in gathering]


def _spans(gathering):
    return [(g.lo, g.hi, g.ways) if isinstance(g, Span) else (0, g.shape[1], (0, 1, 2)) for g in gathering]


def _gather_phases(out, send_sems, recv_sems, spans):
    n = len(out)
    if not any(ways for _, _, ways in spans):
        return (lambda: None,) * 3
    x, y, c, chips = _place()
    me = 2 * x + y
    sibling = (x, y, 1 - c)

    def half(w, chip, core):
        lo, hi, _ = spans[w]
        h = (hi - lo) // 2
        return out[w].at[chip, pl.ds(lo + core * h, h)]

    def copy(k, block, to):
        return pltpu.make_async_remote_copy(src_ref=block, dst_ref=block, send_sem=send_sems.at[k],
                                            recv_sem=recv_sems.at[k], device_id=to, device_id_type=MESH)

    def over_ici(w, j, chip):
        return copy(3 * w + j, half(w, chip, c), (chips[j][0], chips[j][1], c))

    def over_d2d(w, j, core):
        return copy(3 * n + 3 * w + j, half(w, 2 * chips[j][0] + chips[j][1], core), sibling)

    pairs = [(w, j) for w in range(n) for j in spans[w][2]]

    def send():
        for w, j in pairs:
            over_ici(w, j, me).start()

    def pass_on():
        for w, j in pairs:
            over_ici(w, j, 2 * chips[j][0] + chips[j][1]).wait_recv()
            over_d2d(w, j, c).start()

    def finish():
        for w, j in pairs:
            over_d2d(w, j, 1 - c).wait_recv()
        for w, j in pairs:
            over_ici(w, j, me).wait_send()
            over_d2d(w, j, c).wait_send()

    return send, pass_on, finish


def _relay_sems():
    return [pltpu.SemaphoreType.DMA((4,)), pltpu.SemaphoreType.DMA((4,))]


def _relay_phases(out, send_sems, recv_sems):
    x, y, c, chips = _place()
    sibling = (x, y, 1 - c)
    rows = out.shape[1]
    quarter = rows // 4
    far = 2 * chips[2][0] + chips[2][1]

    def piece(chip, way, core):
        return out.at[chip, pl.ds(way * (rows // 2) + core * quarter, quarter)]

    def copy(k, block, to):
        return pltpu.make_async_remote_copy(src_ref=block, dst_ref=block, send_sem=send_sems.at[k],
                                            recv_sem=recv_sems.at[k], device_id=to, device_id_type=MESH)

    def over_ici(way, chip):
        return copy(way, piece(chip, way, c), (chips[way][0], chips[way][1], c))

    def over_d2d(way, core):
        return copy(2 + way, piece(far, way, core), sibling)

    def send():
        for way in range(2):
            other = chips[1 - way]
            over_ici(way, 2 * other[0] + other[1]).start()

    def pass_on():
        for way in range(2):
            over_ici(way, far).wait_recv()
            over_d2d(way, c).start()

    def finish():
        for way in range(2):
            over_d2d(way, 1 - c).wait_recv()
        for way in range(2):
            other = chips[1 - way]
            over_ici(way, 2 * other[0] + other[1]).wait_send()
            over_d2d(way, c).wait_send()

    return send, pass_on, finish


def swap_halves(name, grads):
    n = len(grads)

    def body(*refs):
        send, finish = _swap_phases(refs[:n], refs[n:2 * n], *refs[2 * n:])
        send()
        finish()

    return pl.pallas_call(
        body, name=name, in_specs=[ANY] * n, out_specs=[ANY] * n, out_shape=_swap_shapes(grads),
        scratch_shapes=_swap_sems(n), compiler_params=pltpu.CompilerParams(has_side_effects=True),
    )(*grads)


def _swap_shapes(grads):
    return [jax.ShapeDtypeStruct((a.shape[0], a.shape[1] // 2, a.shape[2]), a.dtype) for a in grads]


def _swap_sems(n):
    return [pltpu.SemaphoreType.DMA((n,)), pltpu.SemaphoreType.DMA((n,))]


def _swap_phases(g, out, send_sems, recv_sems):
    x, y, c, _ = _place()

    def copies():
        return [pltpu.make_async_remote_copy(
            src_ref=g[w].at[:, pl.ds((1 - c) * (g[w].shape[1] // 2), g[w].shape[1] // 2)], dst_ref=out[w],
            send_sem=send_sems.at[w], recv_sem=recv_sems.at[w], device_id=(x, y, 1 - c), device_id_type=MESH)
            for w in range(len(g))]

    def send():
        for cp in copies():
            cp.start()

    def finish():
        for cp in copies():
            cp.wait()

    return send, finish


def _send_phases(g, out, send_sems, recv_sems):
    x, y, c, _ = _place()

    def copies():
        return [pltpu.make_async_remote_copy(
            src_ref=g[w], dst_ref=out[w], send_sem=send_sems.at[w], recv_sem=recv_sems.at[w],
            device_id=(x, y, 1 - c), device_id_type=MESH) for w in range(len(g))]

    def send():
        for cp in copies():
            cp.start()

    def finish():
        for cp in copies():
            cp.wait()

    return send, finish


def dw_in_half(name, h, dz, which, sending):
    s = h.shape[0]
    hh, tb = D // 2, IN_SHARD // 2
    n = len(sending)
    steps = IN_COLS // tb

    def body(w_ref, *refs):
        a_ref, b_ref, o_ref = refs[0], refs[1], refs[2 + n]
        j = pl.program_id(0)
        if n:
            send, finish = _send_phases(refs[2:2 + n], refs[3 + n:3 + 2 * n], *refs[3 + 2 * n:])
            pl.when(j == 0)(send)
        o_ref[...] = _dot(a_ref[...], b_ref[...], TN)
        if n:
            pl.when(j == steps - 1)(finish)

    out = pl.pallas_call(
        body, name=name,
        grid_spec=pltpu.PrefetchScalarGridSpec(
            num_scalar_prefetch=1, grid=(steps,),
            in_specs=[pl.BlockSpec((s, hh), lambda j, w: (0, w[0])), pl.BlockSpec((s, tb), lambda j, w: (0, j))]
            + [ANY] * n,
            out_specs=[pl.BlockSpec((None, hh, tb), lambda j, w: (j // 2, 0, j % 2))] + [ANY] * n,
            scratch_shapes=_swap_sems(n) if n else []),
        out_shape=[jax.ShapeDtypeStruct((N_CHIPS, hh, IN_SHARD), F32)]
        + [jax.ShapeDtypeStruct(a.shape, a.dtype) for a in sending],
        compiler_params=_params("arbitrary", communicates=bool(n)),
    )(which, h, dz, *sending)
    return out[0], out[1:]


def scatter_chips(parts):
    n = len(parts)

    def body(*refs):
        send, finish = _scatter_phases(refs[:n], refs[n:2 * n], *refs[2 * n:])
        send()
        finish()

    return pl.pallas_call(
        body, name="scatter_chips", in_specs=[ANY] * n, out_specs=[ANY] * n,
        out_shape=_scatter_shapes(parts), scratch_shapes=_scatter_sems(n),
        compiler_params=pltpu.CompilerParams(has_side_effects=True),
    )(*parts)


def _scatter_shapes(parts):
    return [jax.ShapeDtypeStruct((3,) + a.shape[1:], a.dtype) for a in parts]


def _scatter_sems(n):
    return [pltpu.SemaphoreType.DMA((3 * n,)), pltpu.SemaphoreType.DMA((3 * n,))]


def _scatter_phases(p, out, send_sems, recv_sems):
    x, y, c, chips = _place()

    def copies():
        return [pltpu.make_async_remote_copy(
            src_ref=p[w].at[2 * px + py], dst_ref=out[w].at[j], send_sem=send_sems.at[3 * w + j],
            recv_sem=recv_sems.at[3 * w + j], device_id=(px, py, c), device_id_type=MESH)
            for w in range(len(p)) for j, (px, py) in enumerate(chips)]

    def send():
        for cp in copies():
            cp.start()

    def finish():
        for cp in copies():
            cp.wait()

    return send, finish


def _join_only(arrays):
    n = len(arrays)

    def body(*refs):
        out = refs[n:2 * n]
        send_sems, recv_sems = refs[2 * n:]
        x, y, c, _ = _place()

        def copy(w, core):
            h = out[w].shape[0] // 2
            rows = out[w].at[pl.ds(core * h, h)]
            return pltpu.make_async_remote_copy(
                src_ref=rows, dst_ref=rows, send_sem=send_sems.at[w], recv_sem=recv_sems.at[w],
                device_id=(x, y, 1 - c), device_id_type=MESH)

        for w in range(n):
            copy(w, c).start()
        for w in range(n):
            copy(w, 1 - c).wait_recv()
        for w in range(n):
            copy(w, c).wait_send()

    return pl.pallas_call(
        body, name="join_only", in_specs=[ANY] * n, out_specs=[ANY] * n,
        out_shape=[jax.ShapeDtypeStruct(a.shape, a.dtype) for a in arrays],
        input_output_aliases={w: w for w in range(n)},
        scratch_shapes=[pltpu.SemaphoreType.DMA((n,)), pltpu.SemaphoreType.DMA((n,))],
        compiler_params=pltpu.CompilerParams(has_side_effects=True),
    )(*arrays)


HBM = pl.BlockSpec(memory_space=pltpu.HBM)
SEM = pl.BlockSpec(memory_space=pltpu.SEMAPHORE)
DATAFLOW = pltpu.SideEffectType.DATAFLOW_SIDE_EFFECTING


def _scatter_copies(p_ref, land_ref, send_sems, recv_sems):
    x, y, c, chips = _place()
    return [pltpu.make_async_remote_copy(
        src_ref=p_ref.at[2 * px + py], dst_ref=land_ref.at[j], send_sem=send_sems[j], recv_sem=recv_sems[j],
        device_id=(px, py, c), device_id_type=MESH) for j, (px, py) in enumerate(chips)]


def scatter_start(p):
    land = jax.ShapeDtypeStruct((3,) + p.shape[1:], p.dtype)

    def body(p_ref, land_ref, *outs):
        for cp in _scatter_copies(p_ref, land_ref, outs[0:3], outs[3:6]):
            cp.start()
        outs[8][...] = jnp.zeros_like(outs[8])

    return pl.pallas_call(
        body, name="scatter_start",
        out_shape=(pltpu.SemaphoreType.DMA(()),) * 6
        + (pltpu.HBM(p.shape, p.dtype), pltpu.HBM(land.shape, land.dtype), jax.ShapeDtypeStruct((8, 128), F32)),
        in_specs=(HBM, HBM), out_specs=(SEM,) * 6 + (HBM, HBM, pl.BlockSpec(memory_space=pltpu.VMEM)),
        input_output_aliases={0: 6, 1: 7},
        compiler_params=pltpu.CompilerParams(has_side_effects=DATAFLOW),
    )(pltpu.with_memory_space_constraint(p, pltpu.HBM),
      pltpu.with_memory_space_constraint(lax.empty(land.shape, land.dtype), pltpu.HBM))


def scatter_wait(started, after):
    sems, p_thru, land_thru = started[0:6], started[6], started[7]

    def body(p_ref, land_ref, *refs):
        for cp in _scatter_copies(p_ref, land_ref, refs[0:3], refs[3:6]):
            cp.wait_send()
            cp.wait_recv()

    return pl.pallas_call(
        body, name="scatter_wait",
        out_shape=(pltpu.HBM(p_thru.shape, p_thru.dtype), pltpu.HBM(land_thru.shape, land_thru.dtype)),
        in_specs=(HBM, HBM) + (SEM,) * 6 + (pl.BlockSpec(memory_space=pl.ANY),) * len(after), out_specs=(HBM, HBM),
        input_output_aliases={0: 0, 1: 1},
        compiler_params=pltpu.CompilerParams(has_side_effects=DATAFLOW),
    )(p_thru, land_thru, *sems, *after)


def _gather_leg_copies(buf_ref, leg, send_sems, recv_sems):
    x, y, c, chips = _place()
    h = buf_ref.shape[1] // 2
    copies = []
    for j, (px, py) in enumerate(chips):
        chip, to = (2 * x + y, (px, py, c)) if leg == "ici" else (2 * px + py, (x, y, 1 - c))
        block = buf_ref.at[chip, pl.ds(c * h, h)]
        copies.append(pltpu.make_async_remote_copy(
            src_ref=block, dst_ref=block, send_sem=send_sems[j], recv_sem=recv_sems[j],
            device_id=to, device_id_type=MESH))
    return copies


def gather_leg_start(name, buf, leg):
    def body(buf_ref, *outs):
        for cp in _gather_leg_copies(buf_ref, leg, outs[0:3], outs[3:6]):
            cp.start()
        outs[7][...] = jnp.zeros_like(outs[7])

    return pl.pallas_call(
        body, name=name,
        out_shape=(pltpu.SemaphoreType.DMA(()),) * 6
        + (pltpu.HBM(buf.shape, buf.dtype), jax.ShapeDtypeStruct((8, 128), F32)),
        in_specs=(HBM,), out_specs=(SEM,) * 6 + (HBM, pl.BlockSpec(memory_space=pltpu.VMEM)),
        input_output_aliases={0: 6},
        compiler_params=pltpu.CompilerParams(has_side_effects=DATAFLOW),
    )(pltpu.with_memory_space_constraint(buf, pltpu.HBM))


def gather_leg_wait(name, started, leg, after):
    def body(buf_ref, *refs):
        for cp in _gather_leg_copies(buf_ref, leg, refs[0:3], refs[3:6]):
            cp.wait_send()
            cp.wait_recv()

    buf = started[6]
    return pl.pallas_call(
        body, name=name, out_shape=pltpu.HBM(buf.shape, buf.dtype),
        in_specs=(HBM,) + (SEM,) * 6 + (pl.BlockSpec(memory_space=pl.ANY),) * len(after), out_specs=HBM,
        input_output_aliases={0: 0},
        compiler_params=pltpu.CompilerParams(has_side_effects=DATAFLOW),
    )(buf, *started[0:6], *after)


def join_halves(arrays, gathering=None):
    n = len(arrays)
    if gathering is None:
        return _join_only(arrays), None

    def body(*refs):
        out = refs[n + 1:2 * n + 1]
        send_sems, recv_sems = refs[2 * n + 2:2 * n + 4]
        send8, pass_on8, finish8 = _allgather8_phases(refs[2 * n + 1], *refs[2 * n + 4:])
        x, y, c, _ = _place()

        def copy(w, core):
            h = out[w].shape[0] // 2
            rows = out[w].at[pl.ds(core * h, h)]
            return pltpu.make_async_remote_copy(
                src_ref=rows, dst_ref=rows, send_sem=send_sems.at[w], recv_sem=recv_sems.at[w],
                device_id=(x, y, 1 - c), device_id_type=MESH)

        send8()
        for w in range(n):
            copy(w, c).start()
        pass_on8()
        for w in range(n):
            copy(w, 1 - c).wait_recv()
        finish8()
        for w in range(n):
            copy(w, c).wait_send()

    res = pl.pallas_call(
        body, name="join_halves", in_specs=[ANY] * (n + 1), out_specs=[ANY] * (n + 1),
        out_shape=[jax.ShapeDtypeStruct(a.shape, a.dtype) for a in list(arrays) + [gathering]],
        input_output_aliases={w: w for w in range(n + 1)},
        scratch_shapes=[pltpu.SemaphoreType.DMA((n,)), pltpu.SemaphoreType.DMA((n,)),
                        pltpu.SemaphoreType.DMA((7,)), pltpu.SemaphoreType.DMA((7,))],
        compiler_params=pltpu.CompilerParams(has_side_effects=True),
    )(*arrays, gathering)
    return res[:n], res[n]


def allreduce_small(packed):
    r, c = packed.shape
    n_dev = 8

    def body(x_ref, all_ref, sum_ref, send_sems, recv_sems, local_sem):
        x, y, cc, chips = _place()
        me, sibling = (x, y, cc), (x, y, 1 - cc)

        def rows(px, py, pc):
            return all_ref.at[4 * px + 2 * py + pc]

        def copy(k, block, to, src=None):
            return pltpu.make_async_remote_copy(
                src_ref=rows(*block) if src is None else src, dst_ref=rows(*block), send_sem=send_sems.at[k],
                recv_sem=recv_sems.at[k], device_id=to, device_id_type=MESH)

        mine = pltpu.make_async_copy(x_ref, rows(*me), local_sem)
        mine.start()
        first = [copy(0, me, sibling, src=x_ref)]
        first += [copy(1 + j, me, (*chip, cc), src=x_ref) for j, chip in enumerate(chips)]
        for cp in first:
            cp.start()
        passed = [copy(4 + j, (*chip, cc), sibling) for j, chip in enumerate(chips)]
        for j, chip in enumerate(chips):
            copy(1 + j, (*chip, cc), me).wait_recv()
            passed[j].start()
        copy(0, sibling, me).wait_recv()
        for j, chip in enumerate(chips):
            copy(4 + j, (*chip, 1 - cc), me).wait_recv()
        for cp in first + passed:
            cp.wait_send()
        mine.wait()
        acc = all_ref[0]
        for k in range(1, n_dev):
            acc = acc + all_ref[k]
        sum_ref[...] = acc

    vm = pl.BlockSpec(memory_space=pltpu.VMEM)
    return pl.pallas_call(
        body, name="allreduce_small", in_specs=[vm], out_specs=[vm, vm],
        out_shape=[jax.ShapeDtypeStruct((n_dev, r, c), F32), jax.ShapeDtypeStruct((r, c), F32)],
        scratch_shapes=[pltpu.SemaphoreType.DMA((7,)), pltpu.SemaphoreType.DMA((7,)), pltpu.SemaphoreType.DMA],
        compiler_params=pltpu.CompilerParams(has_side_effects=True, vmem_limit_bytes=VMEM_LIMIT),
    )(packed)[1]


def local_step(x, target, vecs, w_s, bs_t, bg, wg_in, late, core=None, order=None, where=None):
    on_mesh = core is not None

    def add(names, grads, recv):
        return [add_halves("add_" + n, g, r, core, min(r.shape[1], 256)) for n, g, r in zip(names, grads, recv)]

    g_pre, ln_g, ln_b, g_post, g_fpre, g_fpost = vecs
    s = x.shape[0]
    if order is None:
        order = jnp.arange(N_CHIPS, dtype=jnp.int32)
    logc = _attn_tables(s)
    ka, kb = _alibi_tables(s)

    h = norm_pre(x, g_pre)
    if not on_mesh:
        wg_a, wg_b, wg_out, wg_ff1, wg_ff2 = late
    if on_mesh:
        z, wg_in, (wg_a, wg_b, wg_out, wg_ff1, wg_ff2) = mm_in(h, wg_in, order, True, late)
        ya, (wg_b,) = gating_fwd(z, ln_g, ln_b, w_s, bs_t, [wg_b])
        yb, lse, (wg_a, wg_ff1, wg_out, bg) = attn_fwd(z, logc, ka, kb, [wg_a, wg_ff1, wg_out, bg])
        over_ici = gather_leg_start("ff2_ici_start", wg_ff2, "ici")
        bg = jnp.transpose(bg[:, :2, :], (1, 0, 2)).reshape(2, D) + over_ici[7][0:1, 0:1]
    else:
        z, _, _ = mm_in(h, wg_in, order, False)
        ya, _ = gating_fwd(z, ln_g, ln_b, w_s, bs_t, [])
        yb, lse, _ = attn_fwd(z, logc, ka, kb, [])
    merged, pa, pb, _ = proj_merge(ya, yb, wg_a.reshape(D, D), wg_b.reshape(D, D), z, bg, [])
    w_out = wg_out.reshape(D, D)
    o, x1, h2, _ = out_norm(merged, w_out, x, g_post, g_fpre, [])
    if on_mesh:
        over_d2d = gather_leg_start("ff2_d2d_start", gather_leg_wait("ff2_ici_wait", over_ici, "ici", [h2]), "d2d")
        h2 = h2 + over_d2d[7][0:1, 0:1].astype(BF16)
    a, rl, _ = mm_ff1(h2, wg_ff1, [])
    if on_mesh:
        wg_ff2 = gather_leg_wait("ff2_d2d_wait", over_d2d, "d2d", [rl])
    w_ff2 = wg_ff2.reshape(D_FF, D)
    dy, df, d_gfpost, loss = ff2_loss(rl, w_ff2, x1, target, g_fpost)

    half_cols = pl.BlockSpec((D, D // 2), lambda i, j: (0, j))
    d_wff2 = mm_tn("dw_ff2", rl, df, D // 2, D, (D_FF, D), pl.BlockSpec((D // 2, D), lambda i, j: (i, 0)))
    da = ff2_bwd(df, w_ff2, a)
    d_wff1 = mm_tn("dw_ff1", h2, da, D, D // 2, (N_CHIPS, D, D),
                   pl.BlockSpec((None, D, D // 2), lambda i, j: (j // 2, 0, j % 2)))
    d_ff = [d_wff1, d_wff2.reshape(N_CHIPS, D, D)]
    dx1, do, d_gfpre, d_gpost, recv_ff = ff1_bwd_norms(da, wg_ff1, x1, o, dy, g_fpre, g_post, d_ff if on_mesh else [])
    d_wout = mm_tn("dw_out", merged, do, D, D // 2, (D, D), half_cols)
    dpa, dpb, dga, dgb, d_bg = out_bwd_gates(do, w_out, pa, pb, z, bg)
    d_wa = mm_tn("dw_a", ya, dpa, D, D // 2, (D, D), half_cols)
    d_wb = mm_tn("dw_b", yb, dpb, D, D // 2, (D, D), half_cols)
    dya = mm_nt("dy_a", dpa, wg_a.reshape(D, D))
    dyb = mm_nt("dy_b", dpb, wg_b.reshape(D, D))
    d_proj = [d_wa.reshape(N_CHIPS, D // N_CHIPS, D), d_wb.reshape(N_CHIPS, D // N_CHIPS, D),
              d_wout.reshape(N_CHIPS, D // N_CHIPS, D)]
    du, dv, d_ws, d_bs, d_lng, d_lnb, recv_proj = gating_bwd(z, dya, ln_g, ln_b, w_s, bs_t, d_proj if on_mesh else [])
    early = d_proj + d_ff
    parts_early = add(BIG[1:], early, list(recv_proj) + list(recv_ff)) if on_mesh else []
    small = dict(b_gate=d_bg, ln_v_g=d_lng, ln_v_b=d_lnb, w_s=d_ws, b_s=d_bs[:, 0, :],
                 norm_mix_post=d_gpost, norm_ffn_pre=d_gfpre, norm_ffn_post=d_gfpost)
    packed = pack_small(dict(small, norm_mix_pre=jnp.zeros((1, D), F32)), loss, where) if on_mesh else None
    dq, dk, dvb, got_early, packed = attn_bwd(z, yb, dyb, lse, logc, ka, kb, parts_early, packed)
    dz = jnp.concatenate([du, dv, dq, dk, dvb, dga, dgb], axis=1)
    if on_mesh:
        for_sibling, _ = dw_in_half("dw_in_sibling", h, dz, 1 - core, [])
        mine, from_sibling = dw_in_half("dw_in_mine", h, dz, core, [for_sibling])
        d_win = None
        parts_late = [add_halves("add_w_in", mine, from_sibling[0], jnp.zeros((1,), jnp.int32), 256)]
    else:
        half = IN_SHARD // 2
        d_win = mm_tn("dw_in", h, dz, D, half, (N_CHIPS, D, IN_SHARD),
                      pl.BlockSpec((None, D, half), lambda i, j: (j // 2, 0, j % 2)))
        parts_late = []
    started = scatter_start(parts_late[0]) if on_mesh else None
    dx, d_gpre, _, _ = in_bwd_norm(dz, wg_in, x, dx1, g_pre + started[8][0:1, 0:1] if on_mesh else g_pre, [])
    small["norm_mix_pre"] = d_gpre
    return loss[0, 0], dx, [d_win] + early, small, parts_early, list(got_early), packed, started


BIG = ("w_in", "w_a_proj", "w_b_proj", "w_out", "w_ff1", "w_ff2")
SMALL = ("norm_mix_pre", "ln_v_g", "ln_v_b", "b_s", "norm_mix_post", "norm_ffn_pre", "norm_ffn_post", "w_s", "b_gate")
ORDER = ("norm_mix_pre", "w_in", "b_gate", "ln_v_g", "ln_v_b", "w_s", "b_s", "w_a_proj", "w_b_proj", "w_out",
         "norm_mix_post", "norm_ffn_pre", "w_ff1", "w_ff2", "norm_ffn_post")
VEC_ROWS = D // 128
WS_ROW = 7 * VEC_ROWS
BG_ROW = WS_ROW + GROUPS * CHUNK
LOSS_ROW = BG_ROW + 2 * VEC_ROWS
PACK_ROWS = LOSS_ROW + 8


def pack_small(small, loss, where):
    vectors = [small[n] for n in SMALL[:7]]
    operands = vectors + [small["w_s"], small["b_gate"], loss]

    def body(where_ref, *refs):
        out = refs[-1]
        ws_ref, bg_ref, loss_ref = refs[7:10]
        for i, n in enumerate(SMALL[:7]):
            if n == "b_s":
                out[i * VEC_ROWS:(i + 1) * VEC_ROWS, :] = refs[i][...]
            else:
                for j in range(VEC_ROWS):
                    out[i * VEC_ROWS + j:i * VEC_ROWS + j + 1, :] = refs[i][:, j * 128:(j + 1) * 128]
        for g in range(GROUPS):
            out[WS_ROW + g * CHUNK:WS_ROW + (g + 1) * CHUNK, :] = ws_ref[g]
        for r in range(2):
            for j in range(VEC_ROWS):
                row = BG_ROW + r * VEC_ROWS + j
                out[row:row + 1, :] = bg_ref[r:r + 1, j * 128:(j + 1) * 128]
        lane = lax.broadcasted_iota(jnp.int32, (8, 128), 1)
        sub = lax.broadcasted_iota(jnp.int32, (8, 128), 0)
        out[LOSS_ROW:LOSS_ROW + 8, :] = jnp.where((lane == 0) & (sub == 0), loss_ref[...], 0.0)

    return pl.pallas_call(
        body, name="pack_small",
        grid_spec=pltpu.PrefetchScalarGridSpec(
            num_scalar_prefetch=1, grid=(1,), in_specs=[_full(a.shape) for a in operands],
            out_specs=pl.BlockSpec((None, PACK_ROWS, 128), lambda i, w: (w[0], w[1], 0))),
        out_shape=jax.ShapeDtypeStruct((N_CHIPS, 2 * PACK_ROWS, 128), F32), compiler_params=_params("arbitrary"),
    )(where, *operands)


def pack_vector(vec, where):
    def body(where_ref, v_ref, out):
        for j in range(VEC_ROWS):
            out[j:j + 1, :] = v_ref[:, j * 128:(j + 1) * 128]

    return pl.pallas_call(
        body, name="pack_vector",
        grid_spec=pltpu.PrefetchScalarGridSpec(
            num_scalar_prefetch=1, grid=(1,), in_specs=[_full(vec.shape)],
            out_specs=pl.BlockSpec((None, VEC_ROWS, 128), lambda i, w: (w[0], w[1], 0))),
        out_shape=jax.ShapeDtypeStruct((N_CHIPS, 2 * VEC_ROWS, 128), F32), compiler_params=_params("arbitrary"),
    )(where, vec)


def adamw_small(gathered, first, chip, w, m, v):
    shapes = {n: (1, D) for n in SMALL}
    shapes.update(b_s=(GROUPS, CHUNK), w_s=(GROUPS * CHUNK, CHUNK), b_gate=(2, D // N_CHIPS))
    flat = lambda t: [t[n].reshape(shapes[n]) for n in SMALL]
    per = D // N_CHIPS // 128

    def body(chip_ref, all_ref, first_ref, *refs):
        params, outs = refs[:27], refs[27:]
        sub = lax.broadcasted_iota(jnp.int32, (VEC_ROWS, 128), 0)
        sum_ref = outs[36]
        total = all_ref[0, 0:PACK_ROWS, :]
        head = first_ref[0, 0:VEC_ROWS, :]
        for k in range(1, 2 * N_CHIPS):
            total = total + all_ref[k // 2, (k % 2) * PACK_ROWS:(k % 2 + 1) * PACK_ROWS, :]
            head = head + first_ref[k // 2, (k % 2) * VEC_ROWS:(k % 2 + 1) * VEC_ROWS, :]
        sum_ref[...] = total
        sum_ref[0:VEC_ROWS, :] = head

        def gate_row(r):
            rows = sum_ref[BG_ROW + r * VEC_ROWS:BG_ROW + (r + 1) * VEC_ROWS, :]
            return jnp.concatenate([jnp.sum(jnp.where(sub == per * chip_ref[0] + j, rows, 0.0), axis=0, keepdims=True)
                                    for j in range(per)], axis=1)

        for i, n in enumerate(SMALL):
            if n == "b_s":
                g = sum_ref[i * VEC_ROWS:(i + 1) * VEC_ROWS, :]
            elif n == "w_s":
                g = sum_ref[WS_ROW:BG_ROW, :]
            elif n == "b_gate":
                g = jnp.concatenate([gate_row(0), gate_row(1)], axis=0)
            else:
                g = jnp.concatenate([sum_ref[i * VEC_ROWS + j:i * VEC_ROWS + j + 1, :] for j in range(VEC_ROWS)],
                                    axis=1)
            delta, nm, nv = _adamw_math(params[i][...], g, params[9 + i][...], params[18 + i][...])
            outs[4 * i][...], outs[4 * i + 1][...], outs[4 * i + 2][...], outs[4 * i + 3][...] = g, delta, nm, nv

    vm = pl.BlockSpec(memory_space=pltpu.VMEM)
    res = pl.pallas_call(
        body, name="adamw_small",
        in_specs=[pl.BlockSpec(memory_space=pltpu.SMEM)] + [vm] * 29, out_specs=[vm] * 37,
        out_shape=[jax.ShapeDtypeStruct(shapes[n], F32) for n in SMALL for _ in range(4)]
        + [jax.ShapeDtypeStruct((PACK_ROWS, 128), F32)],
        compiler_params=_params(),
    )(chip, gathered, first, *flat(w), *flat(m), *flat(v))
    new = {n: tuple(r.reshape(w[n].shape) for r in res[4 * i:4 * i + 4]) for i, n in enumerate(SMALL)}
    return new, res[36][LOSS_ROW, 0]


def kernel(x, norm_mix_pre, w_in, b_gate, ln_v_g, ln_v_b, w_s, b_s, w_a_proj, w_b_proj, w_out, norm_mix_post, norm_ffn_pre, w_ff1, w_ff2, norm_ffn_post, loss_target, m_norm_mix_pre, m_w_in, m_b_gate, m_ln_v_g, m_ln_v_b, m_w_s, m_b_s, m_w_a_proj, m_w_b_proj, m_w_out, m_norm_mix_post, m_norm_ffn_pre, m_w_ff1, m_w_ff2, m_norm_ffn_post, v_norm_mix_pre, v_w_in, v_b_gate, v_ln_v_g, v_ln_v_b, v_w_s, v_b_s, v_w_a_proj, v_w_b_proj, v_w_out, v_norm_mix_post, v_norm_ffn_pre, v_w_ff1, v_w_ff2, v_norm_ffn_post):
    w = dict(norm_mix_pre=norm_mix_pre, w_in=w_in, b_gate=b_gate, ln_v_g=ln_v_g, ln_v_b=ln_v_b, w_s=w_s, b_s=b_s,
             w_a_proj=w_a_proj, w_b_proj=w_b_proj, w_out=w_out, norm_mix_post=norm_mix_post,
             norm_ffn_pre=norm_ffn_pre, w_ff1=w_ff1, w_ff2=w_ff2, norm_ffn_post=norm_ffn_post)
    m = dict(norm_mix_pre=m_norm_mix_pre, w_in=m_w_in, b_gate=m_b_gate, ln_v_g=m_ln_v_g, ln_v_b=m_ln_v_b, w_s=m_w_s,
             b_s=m_b_s, w_a_proj=m_w_a_proj, w_b_proj=m_w_b_proj, w_out=m_w_out, norm_mix_post=m_norm_mix_post,
             norm_ffn_pre=m_norm_ffn_pre, w_ff1=m_w_ff1, w_ff2=m_w_ff2, norm_ffn_post=m_norm_ffn_post)
    v = dict(norm_mix_pre=v_norm_mix_pre, w_in=v_w_in, b_gate=v_b_gate, ln_v_g=v_ln_v_g, ln_v_b=v_ln_v_b, w_s=v_w_s,
             b_s=v_b_s, w_a_proj=v_w_a_proj, w_b_proj=v_w_b_proj, w_out=v_w_out, norm_mix_post=v_norm_mix_post,
             norm_ffn_pre=v_norm_ffn_pre, w_ff1=v_w_ff1, w_ff2=v_w_ff2, norm_ffn_post=v_norm_ffn_post)
    chip = 2 * lax.axis_index("x") + lax.axis_index("y")
    core = lax.axis_index("c")

    where = jnp.stack([chip, core]).astype(jnp.int32)
    wg_in = place_shard("place_w_in", w_in[0], where, BF16, 256)
    bg_all = place_shard("place_b_gate", jnp.pad(b_gate[0], ((0, 14), (0, 0))), where, F32, 16)
    vecs = (norm_mix_pre, ln_v_g, ln_v_b, norm_mix_post, norm_ffn_pre, norm_ffn_post)
    loss, dx, _, small, parts, got, packed, started = local_step(
        x[0], loss_target[0], vecs, w_s[0], b_s[0].T, bg_all, wg_in, [w[n][0] for n in BIG[1:]],
        core=jnp.reshape(core, (1,)).astype(jnp.int32),
        order=jnp.stack([chip, chip ^ 2, chip ^ 1, chip ^ 3]).astype(jnp.int32), where=where)

    halves = [sum_chips("sum_" + n, p, r, where, min(p.shape[1], 256), started[8])
              for n, p, r in zip(BIG[1:], parts, got)]
    joined, _ = join_halves(halves)
    grads = dict(zip(BIG[1:], joined))
    new = {}

    def update(n):
        shape = w[n].shape
        res = adamw("adamw_" + n, w[n][0], grads[n], m[n][0], v[n][0], min(shape[1], 256))
        new[n] = tuple(r.reshape(shape) for r in res)

    for n in BIG[1:]:
        update(n)
    p_in, got_in = scatter_wait(started, [new["w_ff2"][1], dx])
    (grads["w_in"],), first = join_halves([sum_chips("sum_w_in", p_in, got_in, where, 256)],
                                          pack_vector(small["norm_mix_pre"], where))
    update("w_in")
    small_new, loss = adamw_small(packed, first, jnp.reshape(chip, (1,)).astype(jnp.int32), w, m, v)
    new.update(small_new)

    outs = [loss, dx[None]]
    for i in range(4):
        outs += [new[n][i] for n in ORDER]
    return tuple(outs)
```

```python
import functools
import math
import typing

import numpy as np
import jax
import jax.numpy as jnp
from jax import lax
from jax.experimental import pallas as pl
from jax.experimental.pallas import tpu as pltpu

F32 = jnp.float32
BF16 = jnp.bfloat16
MESH = pl.DeviceIdType.MESH

D = 1024
EPS = 1e-6
CHUNK = 128
GROUPS = 8
HEADS = 16
HEAD_DIM = 64
ATT_T = 256
ATT_GROUP = 8
ATT_BWD_GROUP = 4
N_CHIPS = 4
D_FF = 4 * D
IN_COLS = 7 * D
IN_SHARD = IN_COLS // N_CHIPS
MASKED = -1e30
VMEM_LIMIT = 56 * 2 ** 20

ADAM_LR, ADAM_B1, ADAM_B2, ADAM_EPS, ADAM_WD, ADAM_STEP = 0.001, 0.9, 0.999, 1e-08, 0.01, 10

NN = (((1,), (0,)), ((), ()))
NT = (((1,), (1,)), ((), ()))
TN = (((0,), (0,)), ((), ()))


def _dot(a, b, dims=NN):
    return lax.dot_general(a, b, dims, preferred_element_type=F32)


def _params(*sem, communicates=False):
    return pltpu.CompilerParams(dimension_semantics=sem or None, vmem_limit_bytes=VMEM_LIMIT,
                                has_side_effects=communicates)


def _rows(tr, c, col=0):
    return pl.BlockSpec((tr, c), lambda i: (i, col))


def _full(shape):
    n = len(shape)
    return pl.BlockSpec(shape, lambda *_: (0,) * n)


def _gelu(x):
    k = math.sqrt(2.0 / math.pi)
    return 0.5 * x * (1.0 + jnp.tanh(k * (x + 0.044715 * x * x * x)))


def _gelu_and_grad(x):
    k = math.sqrt(2.0 / math.pi)
    t = jnp.tanh(k * (x + 0.044715 * x * x * x))
    g = 0.5 * x * (1.0 + t)
    dg = 0.5 * (1.0 + t) + 0.5 * x * (1.0 - t * t) * (k * (1.0 + 3.0 * 0.044715 * x * x))
    return g, dg


def _sigmoid(x):
    return 1.0 / (1.0 + jnp.exp(-x))


def _rms(x):
    r = lax.rsqrt(jnp.mean(x * x, axis=-1, keepdims=True) + EPS)
    return x * r, r


def _rms_bwd(dn, xhat, r):
    return r * (dn - xhat * jnp.mean(dn * xhat, axis=-1, keepdims=True))


def norm_pre(x, g):
    s = x.shape[0]
    tr = 512

    def body(x_ref, g_ref, h_ref):
        xhat, _ = _rms(x_ref[...])
        h_ref[...] = (xhat * g_ref[...]).astype(BF16)

    return pl.pallas_call(
        body, name="norm_pre", grid=(s // tr,),
        in_specs=[_rows(tr, D), _full((1, D))], out_specs=_rows(tr, D),
        out_shape=jax.ShapeDtypeStruct((s, D), BF16), compiler_params=_params("parallel"),
    )(x, g)


def mm_in(h, wg, order, staged, casting=()):
    s = h.shape[0]
    tm, tn = 1024, IN_SHARD // 2
    per = IN_SHARD // tn
    m = len(casting)
    nj, ni = N_CHIPS * per, s // tm
    cast_steps = per * ni

    def body(order_ref, *refs):
        a_ref = refs[0]
        cast_in = refs[2:2 + m]
        o_ref, held = refs[2 + m], refs[3 + m]
        cast_out = refs[4 + m:4 + 2 * m]
        tile, tile_sem = refs[4 + 2 * m:6 + 2 * m]
        sems = refs[6 + 2 * m:]
        j, i = pl.program_id(0), pl.program_id(1)

        @pl.when(j * ni + i < cast_steps)
        def _():
            for src, dst in zip(cast_in, cast_out):
                dst[...] = src[...].astype(BF16)

        def fetch(t):
            chip = order_ref[t // per]
            return pltpu.make_async_copy(held.at[chip, :, pl.ds((t % per) * tn, tn)], tile.at[t % 2],
                                         tile_sem.at[t % 2])

        if staged:
            near = _gather_phases([held], *sems[:2], [(0, D, (0, 1))])
            far = _relay_phases(held, *sems[2:])

        @pl.when(i == 0)
        def _():
            @pl.when(j == 0)
            def _():
                if staged:
                    near[0]()
                fetch(0).start()

            fetch(j).wait()
            ahead = j + 1 < nj
            if staged:
                ahead = ahead & (j + 1 != per) & (j + 1 != 3 * per)
            pl.when(ahead)(lambda: fetch(j + 1).start())

        rows = pl.ds(pl.multiple_of(i * tm, tm), tm)
        o_ref[...] = _dot(a_ref[rows, :], tile[j % 2]).astype(BF16)

        if staged:
            @pl.when((i == ni - 1) & (j == per - 1))
            def _():
                near[1]()
                near[2]()
                far[0]()
                fetch(per).start()

            @pl.when((i == ni - 1) & (j == 3 * per - 1))
            def _():
                far[1]()
                far[2]()
                fetch(3 * per).start()

    def cast_block(j, i, o):
        return jnp.minimum(j * ni + i, cast_steps - 1)

    out = pl.pallas_call(
        body, name="mm_in",
        grid_spec=pltpu.PrefetchScalarGridSpec(
            num_scalar_prefetch=1, grid=(nj, ni),
            in_specs=[pl.BlockSpec((s, D), lambda j, i, o: (0, 0)), ANY]
            + [pl.BlockSpec((a.shape[0] // cast_steps, a.shape[1]), lambda j, i, o: (cast_block(j, i, o), 0))
               for a in casting],
            out_specs=[pl.BlockSpec((tm, tn), lambda j, i, o: (i, o[j // per] * per + j % per)), ANY]
            + [pl.BlockSpec((None, a.shape[0] // cast_steps, a.shape[1]),
                            lambda j, i, o: (o[0], cast_block(j, i, o), 0)) for a in casting],
            scratch_shapes=[pltpu.VMEM((2, D, tn), BF16), pltpu.SemaphoreType.DMA((2,))]
            + (_gather_sems(1) + _relay_sems() if staged else [])),
        out_shape=[jax.ShapeDtypeStruct((s, IN_COLS), BF16), jax.ShapeDtypeStruct(wg.shape, wg.dtype)]
        + [jax.ShapeDtypeStruct((N_CHIPS,) + a.shape, BF16) for a in casting],
        input_output_aliases={2: 1},
        compiler_params=_params("arbitrary", "arbitrary", communicates=staged),
    )(order, h, wg, *casting)
    return out[0], out[1], out[2:]


def _tril_ws(ws_ref, g):
    r = lax.broadcasted_iota(jnp.int32, (CHUNK, CHUNK), 0)
    c = lax.broadcasted_iota(jnp.int32, (CHUNK, CHUNK), 1)
    return jnp.where(c <= r, ws_ref[g], 0.0).astype(BF16)


def _layer_norm(v):
    mu = jnp.mean(v, axis=-1, keepdims=True)
    d = v - mu
    rstd = lax.rsqrt(jnp.mean(d * d, axis=-1, keepdims=True) + EPS)
    return d * rstd, rstd


def gating_fwd(z, ln_g, ln_b, w_s, bs_t, gathering):
    s = z.shape[0]
    n = len(gathering)
    steps = s // CHUNK

    def body(*refs):
        u_ref, v_ref, lg_ref, lb_ref, ws_ref, bst_ref = refs[:6]
        ya_ref = refs[6 + n]
        ci = pl.program_id(0)
        if n:
            send, pass_on, finish = _gather_phases(refs[7 + n:7 + 2 * n], *refs[7 + 2 * n:], _spans(gathering))
            pl.when(ci == 0)(send)
        ug = _gelu(u_ref[...].astype(F32))
        vhat, _ = _layer_norm(_gelu(v_ref[...].astype(F32)))
        vn = (vhat * lg_ref[...] + lb_ref[...]).astype(BF16)
        for g in range(GROUPS):
            cols = slice(g * CHUNK, (g + 1) * CHUNK)
            mixed = _dot(_tril_ws(ws_ref, g), vn[:, cols]) + bst_ref[:, g:g + 1]
            ya_ref[:, cols] = (ug[:, cols] * mixed).astype(BF16)
        if n:
            pl.when(ci == steps - 1)(pass_on)
            pl.when(ci == steps - 1)(finish)

    out = pl.pallas_call(
        body, name="gating_fwd", grid=(steps,),
        in_specs=[_rows(CHUNK, D, 0), _rows(CHUNK, D, 1), _full((1, D)), _full((1, D)),
                  _full((GROUPS, CHUNK, CHUNK)), _full((CHUNK, GROUPS))] + [ANY] * n,
        out_specs=[_rows(CHUNK, D)] + [ANY] * n,
        out_shape=[jax.ShapeDtypeStruct((s, D), BF16)]
        + [jax.ShapeDtypeStruct(a.shape, a.dtype) for a in _arrays(gathering)],
        input_output_aliases={6 + w: 1 + w for w in range(n)},
        scratch_shapes=_gather_sems(n) if n else [],
        compiler_params=_params("arbitrary", communicates=bool(n)),
    )(z, z, ln_g, ln_b, w_s, bs_t, *_arrays(gathering))
    return out[0], out[1:]


def _attn_tables(s):
    nd = s // ATT_T
    r = np.arange(ATT_T)[None, :, None]
    c = np.arange(ATT_T)[None, None, :]
    delta = np.arange(nd)[:, None, None] * ATT_T + r - c
    count = np.zeros(delta.shape, np.int64)
    for window, dilation in ((128, 1), (512, 4), (2048, 16)):
        count += (delta >= 0) & (delta % dilation == 0) & (delta <= window)
    logc = np.where(count > 0, np.log(np.maximum(count, 1)), MASKED)
    return jnp.asarray(logc, F32)


AUG = 3


def _split3_np(x):
    terms, rest = [], np.asarray(x, np.float64)
    for _ in range(AUG):
        term = np.asarray(rest.astype(jnp.bfloat16), np.float64)
        terms.append(term)
        rest = rest - term
    return terms


def _split3(x):
    terms, rest = [], x
    for _ in range(AUG):
        term = rest.astype(BF16).astype(F32)
        terms.append(term)
        rest = rest - term
    return terms


def _alibi_tables(s):
    nb = s // ATT_T
    slopes = np.exp2(-8.0 * np.arange(1, HEADS + 1, dtype=np.float64) / HEADS)
    ka = np.zeros((HEADS // 2, 2, ATT_T, 128), np.float32)
    kb = np.zeros((HEADS // 2, 2, nb, 128), np.float32)
    for p in range(HEADS // 2):
        for e in range(2):
            base = HEAD_DIM * (1 - e)
            for a, term in enumerate(_split3_np(slopes[2 * p + e] * np.arange(ATT_T))):
                ka[p, e, :, base + a] = term
            for a, term in enumerate(_split3_np(slopes[2 * p + e] * ATT_T * np.arange(nb))):
                kb[p, e, :, base + AUG + a] = term
            ka[p, e, :, base + 2 * AUG:base + 3 * AUG] = 1.0
    return jnp.asarray(ka), jnp.asarray(kb)


def _head_masks():
    lane = lax.broadcasted_iota(jnp.int32, (1, 128), 1)
    first = lane < HEAD_DIM

    def ones(e, n):
        base = HEAD_DIM * (1 - e)
        return ((lane >= base) & (lane < base + n)).astype(F32)

    return first, lane, ones


def _place3(lane, at, terms, other):
    for a, term in enumerate(terms):
        other = jnp.where(lane == at + a, term, other)
    return other


def attn_fwd(z, logc, ka, kb, gathering):
    s = z.shape[0]
    nq = s // ATT_T
    t = ATT_T
    n = len(gathering)
    grp = ATT_GROUP
    ngrp = HEADS // 2 // grp
    wide = 128 * grp
    qcol, kcol, vcol = 2 * D // wide, 3 * D // wide, 4 * D // wide

    def body(*refs):
        q_ref, k_ref, v_ref, lc_ref, ka_ref, kb_ref = refs[:6]
        y_ref, lse_ref = refs[6 + n:8 + n]
        q_s, k_s, v_s, m_s, l_s, acc_s = refs[8 + 2 * n:14 + 2 * n]
        gi, qi = pl.program_id(0), pl.program_id(1)
        first, lane, ones = _head_masks()
        if n:
            send, pass_on, finish = _gather_phases(refs[8 + n:8 + 2 * n], *refs[14 + 2 * n:], _spans(gathering))
            pl.when((gi == 0) & (qi == 0))(send)

        @pl.when(qi == 0)
        def _():
            sel = jnp.broadcast_to(first.astype(F32), (t, 128))
            for pr in range(grp):
                cols = slice(pr * 128, (pr + 1) * 128)
                for jb in range(nq):
                    kj = k_ref[jb * t:(jb + 1) * t, cols].astype(F32)
                    vj = v_ref[jb * t:(jb + 1) * t, cols].astype(F32)
                    k_s[pr, 0, jb] = jnp.where(first, kj, ka_ref[pr, 0] + kb_ref[pr, 0, jb:jb + 1, :]).astype(BF16)
                    k_s[pr, 1, jb] = jnp.where(first, ka_ref[pr, 1] + kb_ref[pr, 1, jb:jb + 1, :], kj).astype(BF16)
                    v_s[pr, jb, 0:t, 0:128] = jnp.where(first, vj, 0.0).astype(BF16)
                    v_s[pr, jb, t:2 * t, 0:128] = jnp.where(first, 0.0, vj).astype(BF16)
                    v_s[pr, jb, 0:t, 128:256] = sel.astype(BF16)
                    v_s[pr, jb, t:2 * t, 128:256] = (1.0 - sel).astype(BF16)

        for pr in range(grp):
            q = q_ref[:, pr * 128:(pr + 1) * 128].astype(F32) * (1.0 / math.sqrt(HEAD_DIM))
            q_s[pr, 0] = jnp.where(first, q, ones(0, 2 * AUG)).astype(BF16)
            q_s[pr, 1] = jnp.where(first, ones(1, 2 * AUG), q).astype(BF16)
        m_s[...] = jnp.full_like(m_s, MASKED)
        l_s[...] = jnp.zeros_like(l_s)
        acc_s[...] = jnp.zeros_like(acc_s)

        def scores(j):
            return tuple(_dot(q_s[pr, e], k_s[pr, e, j], NT) for pr in range(grp) for e in range(2))

        def step(j, carry):
            softmax_block(j, scores(j))
            return carry

        def softmax_block(j, u):
            lc = lc_ref[qi - j]
            for pr in range(grp):
                u0 = u[2 * pr] + lc
                u1 = u[2 * pr + 1] + lc
                m0, m1 = m_s[pr, 0], m_s[pr, 1]
                n0 = jnp.maximum(m0, jnp.max(u0, axis=-1, keepdims=True))
                n1 = jnp.maximum(m1, jnp.max(u1, axis=-1, keepdims=True))
                m_s[pr, 0], m_s[pr, 1] = n0, n1
                p = jnp.concatenate([jnp.exp(u0 - jnp.concatenate([n0, n0], axis=1)).astype(BF16),
                                     jnp.exp(u1 - jnp.concatenate([n1, n1], axis=1)).astype(BF16)], axis=1)
                pv = _dot(p, v_s[pr, j])
                alpha = jnp.where(first, jnp.exp(m0 - n0), jnp.exp(m1 - n1))
                acc_s[pr] = acc_s[pr] * alpha + pv[:, 0:128]
                l_s[pr] = l_s[pr] * alpha + pv[:, 128:256]

        lax.fori_loop(0, qi + 1, step, 0)
        for pr in range(grp):
            cols = slice(pr * 128, (pr + 1) * 128)
            y_ref[:, cols] = (acc_s[pr] / l_s[pr]).astype(BF16)
            lse_ref[:, cols] = jnp.where(first, m_s[pr, 0], m_s[pr, 1]) + jnp.log(l_s[pr])
        if n:
            pl.when((gi == ngrp - 1) & (qi == nq - 1))(pass_on)
            pl.when((gi == ngrp - 1) & (qi == nq - 1))(finish)

    out = pl.pallas_call(
        body, name="attn_fwd", grid=(ngrp, nq),
        in_specs=[pl.BlockSpec((t, wide), lambda g, i: (i, qcol + g)),
                  pl.BlockSpec((s, wide), lambda g, i: (0, kcol + g)),
                  pl.BlockSpec((s, wide), lambda g, i: (0, vcol + g)),
                  _full((nq, t, t)),
                  pl.BlockSpec((grp, 2, t, 128), lambda g, i: (g, 0, 0, 0)),
                  pl.BlockSpec((grp, 2, nq, 128), lambda g, i: (g, 0, 0, 0))] + [ANY] * n,
        out_specs=[pl.BlockSpec((t, wide), lambda g, i: (i, g)), pl.BlockSpec((t, wide), lambda g, i: (i, g))]
        + [ANY] * n,
        out_shape=[jax.ShapeDtypeStruct((s, D), BF16), jax.ShapeDtypeStruct((s, D), F32)]
        + [jax.ShapeDtypeStruct(a.shape, a.dtype) for a in _arrays(gathering)],
        input_output_aliases={6 + w: 2 + w for w in range(n)},
        scratch_shapes=[pltpu.VMEM((grp, 2, t, 128), BF16), pltpu.VMEM((grp, 2, nq, t, 128), BF16),
                        pltpu.VMEM((grp, nq, 2 * t, 256), BF16), pltpu.VMEM((grp, 2, t, 128), F32),
                        pltpu.VMEM((grp, t, 128), F32), pltpu.VMEM((grp, t, 128), F32)]
        + (_gather_sems(n) if n else []),
        compiler_params=_params("arbitrary", "arbitrary", communicates=bool(n)),
    )(z, z, z, logc, ka, kb, *_arrays(gathering))
    return out[0], out[1], out[2:]


def proj_merge(ya, yb, wa, wb, z, bg, gathering):
    s = ya.shape[0]
    tm = 512
    n = len(gathering)
    steps = s // tm

    def body(*refs):
        ya_ref, yb_ref, wa_ref, wb_ref, ga_ref, gb_ref, bg_ref = refs[:7]
        mg_ref, pa_ref, pb_ref = refs[7 + n:10 + n]
        i = pl.program_id(0)
        if n:
            send, pass_on, finish = _gather_phases(refs[10 + n:10 + 2 * n], *refs[10 + 2 * n:], _spans(gathering))
            pl.when(i == 0)(send)
            pl.when(i == steps - 1)(pass_on)
        pa = _dot(ya_ref[...], wa_ref[...])
        pb = _dot(yb_ref[...], wb_ref[...])
        sa = _sigmoid(ga_ref[...] + bg_ref[0:1, :])
        sb = _sigmoid(gb_ref[...] + bg_ref[1:2, :])
        mg_ref[...] = (sa * pa + sb * pb).astype(BF16)
        pa_ref[...] = pa.astype(BF16)
        pb_ref[...] = pb.astype(BF16)
        if n:
            pl.when(i == steps - 1)(finish)

    out = jax.ShapeDtypeStruct((s, D), BF16)
    res = pl.pallas_call(
        body, name="proj_merge", grid=(steps,),
        in_specs=[_rows(tm, D), _rows(tm, D), _full((D, D)), _full((D, D)),
                  _rows(tm, D, 5), _rows(tm, D, 6), _full((2, D))] + [ANY] * n,
        out_specs=[_rows(tm, D)] * 3 + [ANY] * n,
        out_shape=[out] * 3 + [jax.ShapeDtypeStruct(a.shape, a.dtype) for a in _arrays(gathering)],
        input_output_aliases={7 + w: 3 + w for w in range(n)},
        scratch_shapes=_gather_sems(n) if n else [],
        compiler_params=_params("arbitrary", communicates=bool(n)),
    )(ya, yb, wa, wb, z, z, bg, *_arrays(gathering))
    return res[0], res[1], res[2], res[3:]


def out_norm(merged, w_out, x, g_post, g_fpre, gathering):
    s = x.shape[0]
    tm = 512
    n = len(gathering)
    steps = s // tm

    def body(*refs):
        mg_ref, w_ref, x_ref, gp_ref, gf_ref = refs[:5]
        o_ref, x1_ref, h2_ref = refs[5 + n:8 + n]
        i = pl.program_id(0)
        if n:
            send, pass_on, finish = _gather_phases(refs[8 + n:8 + 2 * n], *refs[8 + 2 * n:], _spans(gathering))
            pl.when(i == 0)(send)
            pl.when(i == steps - 1)(pass_on)
        o = _dot(mg_ref[...], w_ref[...])
        ohat, _ = _rms(o)
        x1 = x_ref[...] + ohat * gp_ref[...]
        x1hat, _ = _rms(x1)
        o_ref[...] = o
        x1_ref[...] = x1
        h2_ref[...] = (x1hat * gf_ref[...]).astype(BF16)
        if n:
            pl.when(i == steps - 1)(finish)

    res = pl.pallas_call(
        body, name="out_norm", grid=(steps,),
        in_specs=[_rows(tm, D), _full((D, D)), _rows(tm, D), _full((1, D)), _full((1, D))] + [ANY] * n,
        out_specs=[_rows(tm, D)] * 3 + [ANY] * n,
        out_shape=[jax.ShapeDtypeStruct((s, D), F32), jax.ShapeDtypeStruct((s, D), F32),
                   jax.ShapeDtypeStruct((s, D), BF16)]
        + [jax.ShapeDtypeStruct(a.shape, a.dtype) for a in _arrays(gathering)],
        input_output_aliases={5 + w: 3 + w for w in range(n)},
        scratch_shapes=_gather_sems(n) if n else [],
        compiler_params=_params("arbitrary", communicates=bool(n)),
    )(merged, w_out, x, g_post, g_fpre, *_arrays(gathering))
    return res[0], res[1], res[2], res[3:]


def mm_ff1(h2, wg, gathering):
    s = h2.shape[0]
    tm = 1024
    n = len(gathering)
    ni = s // tm

    def body(*refs):
        a_ref, b_ref = refs[:2]
        o_ref, r_ref = refs[2 + n:4 + n]
        i, j = pl.program_id(0), pl.program_id(1)
        if n:
            send, pass_on, finish = _gather_phases(refs[4 + n:4 + 2 * n], *refs[4 + 2 * n:], _spans(gathering))
            pl.when((i == 0) & (j == 0))(send)
            pl.when((i == ni - 1) & (j == N_CHIPS // 2))(pass_on)
        a = _dot(a_ref[...], b_ref[...])
        o_ref[...] = a.astype(BF16)
        r = jnp.maximum(a, 0.0)
        r_ref[...] = (r * r).astype(BF16)
        if n:
            pl.when((i == ni - 1) & (j == N_CHIPS - 1))(finish)

    res = pl.pallas_call(
        body, name="mm_ff1", grid=(ni, N_CHIPS),
        in_specs=[pl.BlockSpec((tm, D), lambda i, j: (i, 0)), pl.BlockSpec((None, D, D), lambda i, j: (j, 0, 0))]
        + [ANY] * n,
        out_specs=[pl.BlockSpec((tm, D), lambda i, j: (i, j))] * 2 + [ANY] * n,
        out_shape=[jax.ShapeDtypeStruct((s, D_FF), BF16), jax.ShapeDtypeStruct((s, D_FF), BF16)]
        + [jax.ShapeDtypeStruct(a.shape, a.dtype) for a in _arrays(gathering)],
        input_output_aliases={2 + w: 2 + w for w in range(n)},
        scratch_shapes=_gather_sems(n) if n else [],
        compiler_params=_params("arbitrary", "arbitrary", communicates=bool(n)),
    )(h2, wg, *_arrays(gathering))
    return res[0], res[1], res[2:]


def ff2_loss(rl, w_ff2, x1, target, g_fpost):
    s = x1.shape[0]
    tm = 256

    def body(rl_ref, w_ref, x1_ref, t_ref, g_ref, dy_ref, df_ref, dg_ref, loss_ref):
        @pl.when(pl.program_id(0) == 0)
        def _():
            dg_ref[...] = jnp.zeros_like(dg_ref)
            loss_ref[...] = jnp.zeros_like(loss_ref)

        f = _dot(rl_ref[...], w_ref[...])
        fhat, r = _rms(f)
        err = x1_ref[...] + fhat * g_ref[...] - t_ref[...]
        loss_ref[...] += 0.5 * jnp.sum(jnp.mean(err * err, axis=-1, keepdims=True), axis=0, keepdims=True)
        dy = err * (1.0 / D)
        dy_ref[...] = dy
        dg_ref[...] += jnp.sum(dy * fhat, axis=0, keepdims=True)
        df_ref[...] = _rms_bwd(dy * g_ref[...], fhat, r).astype(BF16)

    return pl.pallas_call(
        body, name="ff2_loss", grid=(s // tm,),
        in_specs=[_rows(tm, D_FF), _full((D_FF, D)), _rows(tm, D), _rows(tm, D), _full((1, D))],
        out_specs=[_rows(tm, D), _rows(tm, D), _full((1, D)), _full((1, 1))],
        out_shape=[jax.ShapeDtypeStruct((s, D), F32), jax.ShapeDtypeStruct((s, D), BF16),
                   jax.ShapeDtypeStruct((1, D), F32), jax.ShapeDtypeStruct((1, 1), F32)],
        compiler_params=_params("arbitrary"),
    )(rl, w_ff2, x1, target, g_fpost)


def mm_tn(name, a, b, ta, tb, out_shape, out_spec):
    s = a.shape[0]

    def body(a_ref, b_ref, o_ref):
        o_ref[...] = _dot(a_ref[...], b_ref[...], TN)

    return pl.pallas_call(
        body, name=name, grid=(a.shape[1] // ta, b.shape[1] // tb),
        in_specs=[pl.BlockSpec((s, ta), lambda i, j: (0, i)), pl.BlockSpec((s, tb), lambda i, j: (0, j))],
        out_specs=out_spec, out_shape=jax.ShapeDtypeStruct(out_shape, F32),
        compiler_params=_params("parallel", "parallel"),
    )(a, b)


def mm_nt(name, a, w):
    s = a.shape[0]
    tm = 512

    def body(a_ref, w_ref, o_ref):
        o_ref[...] = _dot(a_ref[...], w_ref[...], NT).astype(BF16)

    return pl.pallas_call(
        body, name=name, grid=(s // tm,), in_specs=[_rows(tm, D), _full((D, D))], out_specs=_rows(tm, D),
        out_shape=jax.ShapeDtypeStruct((s, D), BF16), compiler_params=_params("parallel"),
    )(a, w)


def ff2_bwd(df, w_ff2, a):
    s = df.shape[0]
    tm = 1024

    def body(df_ref, w_ref, a_ref, da_ref):
        drl = _dot(df_ref[...], w_ref[...], NT)
        da_ref[...] = (drl * (2.0 * jnp.maximum(a_ref[...].astype(F32), 0.0))).astype(BF16)

    return pl.pallas_call(
        body, name="ff2_bwd", grid=(s // tm, D_FF // D),
        in_specs=[pl.BlockSpec((tm, D), lambda i, j: (i, 0)), pl.BlockSpec((D, D), lambda i, j: (j, 0)),
                  pl.BlockSpec((tm, D), lambda i, j: (i, j))],
        out_specs=pl.BlockSpec((tm, D), lambda i, j: (i, j)),
        out_shape=jax.ShapeDtypeStruct((s, D_FF), BF16), compiler_params=_params("parallel", "parallel"),
    )(df, w_ff2, a)


def ff1_bwd_norms(da, wg, x1, o, dy, g_fpre, g_post, swapping):
    s = x1.shape[0]
    tm = 256
    n = len(swapping)
    steps = s // tm

    def body(*refs):
        da_ref, w_ref, x1_ref, o_ref, dy_ref, gf_ref, gp_ref = refs[:7]
        dx1_ref, do_ref, dgf_ref, dgp_ref = refs[7 + n:11 + n]
        i = pl.program_id(0)
        if n:
            send, finish = _swap_phases(refs[7:7 + n], refs[11 + n:11 + 2 * n], *refs[11 + 2 * n:])
            pl.when(i == 0)(send)

        @pl.when(i == 0)
        def _():
            dgf_ref[...] = jnp.zeros_like(dgf_ref)
            dgp_ref[...] = jnp.zeros_like(dgp_ref)

        dh2 = _dot(da_ref[:, 0:D], w_ref[0], NT)
        for k in range(1, N_CHIPS):
            dh2 = dh2 + _dot(da_ref[:, k * D:(k + 1) * D], w_ref[k], NT)
        x1hat, r2 = _rms(x1_ref[...])
        dgf_ref[...] += jnp.sum(dh2 * x1hat, axis=0, keepdims=True)
        dx1 = dy_ref[...] + _rms_bwd(dh2 * gf_ref[...], x1hat, r2)
        ohat, r1 = _rms(o_ref[...])
        dgp_ref[...] += jnp.sum(dx1 * ohat, axis=0, keepdims=True)
        dx1_ref[...] = dx1
        do_ref[...] = _rms_bwd(dx1 * gp_ref[...], ohat, r1).astype(BF16)
        if n:
            pl.when(i == steps - 1)(finish)

    res = pl.pallas_call(
        body, name="ff1_bwd_norms", grid=(steps,),
        in_specs=[_rows(tm, D_FF), _full((N_CHIPS, D, D)), _rows(tm, D), _rows(tm, D), _rows(tm, D),
                  _full((1, D)), _full((1, D))] + [ANY] * n,
        out_specs=[_rows(tm, D), _rows(tm, D), _full((1, D)), _full((1, D))] + [ANY] * n,
        out_shape=[jax.ShapeDtypeStruct((s, D), F32), jax.ShapeDtypeStruct((s, D), BF16),
                   jax.ShapeDtypeStruct((1, D), F32), jax.ShapeDtypeStruct((1, D), F32)] + _swap_shapes(swapping),
        scratch_shapes=_swap_sems(n) if n else [],
        compiler_params=_params("arbitrary", communicates=bool(n)),
    )(da, wg, x1, o, dy, g_fpre, g_post, *swapping)
    return res[0], res[1], res[2], res[3], res[4:]


def out_bwd_gates(do, w_out, pa, pb, z, bg):
    s = do.shape[0]
    tm = 512

    def body(do_ref, w_ref, pa_ref, pb_ref, ga_ref, gb_ref, bg_ref, dpa_ref, dpb_ref, dga_ref, dgb_ref, dbg_ref):
        @pl.when(pl.program_id(0) == 0)
        def _():
            dbg_ref[...] = jnp.zeros_like(dbg_ref)

        dm = _dot(do_ref[...], w_ref[...], NT)
        sa = _sigmoid(ga_ref[...] + bg_ref[0:1, :])
        sb = _sigmoid(gb_ref[...] + bg_ref[1:2, :])
        dpa_ref[...] = (dm * sa).astype(BF16)
        dpb_ref[...] = (dm * sb).astype(BF16)
        dga = dm * pa_ref[...].astype(F32) * (sa * (1.0 - sa))
        dgb = dm * pb_ref[...].astype(F32) * (sb * (1.0 - sb))
        dga_ref[...] = dga.astype(BF16)
        dgb_ref[...] = dgb.astype(BF16)
        dbg_ref[0:1, :] += jnp.sum(dga, axis=0, keepdims=True)
        dbg_ref[1:2, :] += jnp.sum(dgb, axis=0, keepdims=True)

    out = jax.ShapeDtypeStruct((s, D), BF16)
    return pl.pallas_call(
        body, name="out_bwd_gates", grid=(s // tm,),
        in_specs=[_rows(tm, D), _full((D, D)), _rows(tm, D), _rows(tm, D), _rows(tm, D, 5), _rows(tm, D, 6),
                  _full((2, D))],
        out_specs=[_rows(tm, D)] * 4 + [_full((2, D))],
        out_shape=[out] * 4 + [jax.ShapeDtypeStruct((2, D), F32)], compiler_params=_params("arbitrary"),
    )(do, w_out, pa, pb, z, z, bg)


def gating_bwd(z, dya, ln_g, ln_b, w_s, bs_t, swapping):
    s = z.shape[0]
    ones = functools.partial(jnp.ones, (8, CHUNK), BF16)
    n = len(swapping)

    def body(*refs):
        u_ref, v_ref, dya_ref, lg_ref, lb_ref, ws_ref, bst_ref = refs[:7]
        du_ref, dv_ref, dws_ref, dbs_ref, dlg_ref, dlb_ref = refs[7 + n:13 + n]
        dvn_ref = refs[13 + 2 * n]
        ci = pl.program_id(0)
        if n:
            send, finish = _swap_phases(refs[7:7 + n], refs[13 + n:13 + 2 * n], *refs[14 + 2 * n:])
            pl.when(ci == 0)(send)

        @pl.when(ci == 0)
        def _():
            dws_ref[...] = jnp.zeros_like(dws_ref)
            dbs_ref[...] = jnp.zeros_like(dbs_ref)
            dlg_ref[...] = jnp.zeros_like(dlg_ref)
            dlb_ref[...] = jnp.zeros_like(dlb_ref)

        ug, dug_du = _gelu_and_grad(u_ref[...].astype(F32))
        vg, dvg_dv = _gelu_and_grad(v_ref[...].astype(F32))
        vhat, rstd = _layer_norm(vg)
        vn = (vhat * lg_ref[...] + lb_ref[...]).astype(BF16)
        dya = dya_ref[...].astype(F32)
        for g in range(GROUPS):
            cols = slice(g * CHUNK, (g + 1) * CHUNK)
            ws = _tril_ws(ws_ref, g)
            mixed = _dot(ws, vn[:, cols]) + bst_ref[:, g:g + 1]
            du_ref[:, cols] = (dya[:, cols] * mixed * dug_du[:, cols]).astype(BF16)
            dmix = (dya[:, cols] * ug[:, cols]).astype(BF16)
            dbs_ref[g] += _dot(ones(), dmix, NT)
            dws_ref[g] += _dot(dmix, vn[:, cols], NT)
            dvn_ref[:, cols] = _dot(ws, dmix, TN)
        dvn = dvn_ref[...]
        dlg_ref[...] += jnp.sum(dvn * vhat, axis=0, keepdims=True)
        dlb_ref[...] += jnp.sum(dvn, axis=0, keepdims=True)
        dvh = dvn * lg_ref[...]
        dvg = rstd * (dvh - jnp.mean(dvh, axis=-1, keepdims=True)
                      - vhat * jnp.mean(dvh * vhat, axis=-1, keepdims=True))
        dv_ref[...] = (dvg * dvg_dv).astype(BF16)

        @pl.when(ci == pl.num_programs(0) - 1)
        def _():
            r = lax.broadcasted_iota(jnp.int32, (CHUNK, CHUNK), 0)
            c = lax.broadcasted_iota(jnp.int32, (CHUNK, CHUNK), 1)
            for g in range(GROUPS):
                dws_ref[g] = jnp.where(c <= r, dws_ref[g], 0.0)

        if n:
            pl.when(ci == pl.num_programs(0) - 1)(finish)

    out = jax.ShapeDtypeStruct((s, D), BF16)
    res = pl.pallas_call(
        body, name="gating_bwd", grid=(s // CHUNK,),
        in_specs=[_rows(CHUNK, D, 0), _rows(CHUNK, D, 1), _rows(CHUNK, D), _full((1, D)), _full((1, D)),
                  _full((GROUPS, CHUNK, CHUNK)), _full((CHUNK, GROUPS))] + [ANY] * n,
        out_specs=[_rows(CHUNK, D), _rows(CHUNK, D), _full((GROUPS, CHUNK, CHUNK)), _full((GROUPS, 8, CHUNK)),
                   _full((1, D)), _full((1, D))] + [ANY] * n,
        out_shape=[out, out, jax.ShapeDtypeStruct((GROUPS, CHUNK, CHUNK), F32),
                   jax.ShapeDtypeStruct((GROUPS, 8, CHUNK), F32),
                   jax.ShapeDtypeStruct((1, D), F32), jax.ShapeDtypeStruct((1, D), F32)] + _swap_shapes(swapping),
        scratch_shapes=[pltpu.VMEM((CHUNK, D), F32)] + (_swap_sems(n) if n else []),
        compiler_params=_params("arbitrary", communicates=bool(n)),
    )(z, z, dya, ln_g, ln_b, w_s, bs_t, *swapping)
    return (*res[:6], res[6:])


def attn_bwd(z, yb, dyb, lse, logc, ka, kb, scattering, gathering=None):
    s = z.shape[0]
    nq = s // ATT_T
    t = ATT_T
    grp = ATT_BWD_GROUP
    ngrp = HEADS // 2 // grp
    wide = 128 * grp
    qcol, kcol, vcol = 2 * D // wide, 3 * D // wide, 4 * D // wide
    scale = 1.0 / math.sqrt(HEAD_DIM)
    n = len(scattering)
    g8 = 0 if gathering is None else 1

    def body(*refs):
        q_ref, k_ref, v_ref, y_ref, dy_ref, lse_ref, lc_ref, ka_ref, kb_ref = refs[:9]
        dq_ref, dk_ref, dv_ref = refs[9 + n + g8:12 + n + g8]
        qa_s, qt_s, da_s, dt_s, dq_s, dkt_s, dvt_s = refs[12 + 2 * n + 2 * g8:19 + 2 * n + 2 * g8]
        sems = refs[19 + 2 * n + 2 * g8:]
        gi, j = pl.program_id(0), pl.program_id(1)
        first, lane, ones = _head_masks()
        if n:
            send, finish = _scatter_phases(refs[9:9 + n], refs[12 + n + g8:12 + 2 * n + g8], *sems[:2])
            pl.when((gi == 0) & (j == 0))(send)
        if g8:
            send8, pass_on8, finish8 = _allgather8_phases(refs[12 + 2 * n + g8], *sems[2 * (n > 0):])
            pl.when((gi == 0) & (j == 0))(send8)
            pl.when((gi == ngrp - 1) & (j == nq - 1))(pass_on8)

        @pl.when(j == 0)
        def _():
            dq_s[...] = jnp.zeros_like(dq_s)
            for pr in range(grp):
                cols = slice(pr * 128, (pr + 1) * 128)
                for ib in range(nq):
                    rows = slice(ib * t, (ib + 1) * t)
                    q = q_ref[rows, cols].astype(F32) * scale
                    lse = lse_ref[rows, cols]
                    qa_s[pr, 0, ib] = jnp.where(first, q, _place3(lane, HEAD_DIM + 2 * AUG, _split3(-lse[:, 0:1]),
                                                                  ones(0, 2 * AUG))).astype(BF16)
                    qa_s[pr, 1, ib] = jnp.where(
                        first, _place3(lane, 2 * AUG, _split3(-lse[:, HEAD_DIM:HEAD_DIM + 1]), ones(1, 2 * AUG)),
                        q).astype(BF16)
                    qt_s[pr, ib, :, 0:t] = jnp.where(first, q, 0.0).T.astype(BF16)
                    qt_s[pr, ib, :, t:2 * t] = jnp.where(first, 0.0, q).T.astype(BF16)
                    do = dy_ref[rows, cols].astype(F32)
                    prod = do * y_ref[rows, cols].astype(F32)
                    dd0 = jnp.sum(jnp.where(first, prod, 0.0), axis=-1, keepdims=True)
                    dd1 = jnp.sum(jnp.where(first, 0.0, prod), axis=-1, keepdims=True)
                    da_s[pr, 0, ib] = jnp.where(first, do, _place3(lane, HEAD_DIM, _split3(-dd0), 0.0)).astype(BF16)
                    da_s[pr, 1, ib] = jnp.where(first, _place3(lane, 0, _split3(-dd1), 0.0), do).astype(BF16)
                    dt_s[pr, ib, :, 0:t] = jnp.where(first, do, 0.0).T.astype(BF16)
                    dt_s[pr, ib, :, t:2 * t] = jnp.where(first, 0.0, do).T.astype(BF16)

        keys = []
        for pr in range(grp):
            kj = k_ref[:, pr * 128:(pr + 1) * 128].astype(F32)
            vj = v_ref[:, pr * 128:(pr + 1) * 128].astype(F32)
            keys.append((
                jnp.where(first, kj, ka_ref[pr, 0] + kb_ref[pr, 0, pl.ds(j, 1), :]).astype(BF16),
                jnp.where(first, ka_ref[pr, 1] + kb_ref[pr, 1, pl.ds(j, 1), :], kj).astype(BF16),
                jnp.concatenate([jnp.where(first, kj, 0.0), jnp.where(first, 0.0, kj)], axis=0).astype(BF16),
                jnp.where(first, vj, ones(0, AUG)).astype(BF16),
                jnp.where(first, ones(1, AUG), vj).astype(BF16)))
        dkt_s[...] = jnp.zeros_like(dkt_s)
        dvt_s[...] = jnp.zeros_like(dvt_s)

        def step(i, _):
            lc = lc_ref[i - j]
            rows = pl.ds(pl.multiple_of(i * t, t), t)
            for pr in range(grp):
                k0a, k1a, kst, v0a, v1a = keys[pr]
                p0 = jnp.exp(_dot(qa_s[pr, 0, i], k0a, NT) + lc)
                p1 = jnp.exp(_dot(qa_s[pr, 1, i], k1a, NT) + lc)
                e0 = (p0 * _dot(da_s[pr, 0, i], v0a, NT)).astype(BF16)
                e1 = (p1 * _dot(da_s[pr, 1, i], v1a, NT)).astype(BF16)
                dq_s[pr, rows, :] += _dot(jnp.concatenate([e0, e1], axis=1), kst)
                dvt_s[pr] += _dot(dt_s[pr, i], jnp.concatenate([p0.astype(BF16), p1.astype(BF16)], axis=0))
                dkt_s[pr] += _dot(qt_s[pr, i], jnp.concatenate([e0, e1], axis=0))
            return 0

        lax.fori_loop(j, nq, step, 0)
        for pr in range(grp):
            dk_ref[:, pr * 128:(pr + 1) * 128] = dkt_s[pr].T.astype(BF16)
            dv_ref[:, pr * 128:(pr + 1) * 128] = dvt_s[pr].T.astype(BF16)

        @pl.when(j == nq - 1)
        def _():
            for pr in range(grp):
                dq_ref[:, pr * 128:(pr + 1) * 128] = (dq_s[pr] * scale).astype(BF16)

        if n:
            pl.when((gi == ngrp - 1) & (j == nq - 1))(finish)
        if g8:
            pl.when((gi == ngrp - 1) & (j == nq - 1))(finish8)

    colblock = lambda c: pl.BlockSpec((s, wide), lambda g, j: (0, c + g))
    once = lambda c: pl.BlockSpec((s, wide), lambda g, j: (0, c + g), pipeline_mode=pl.Buffered(1))
    blk = lambda c: pl.BlockSpec((t, wide), lambda g, j: (j, c + g))
    out = jax.ShapeDtypeStruct((s, D), BF16)
    res = pl.pallas_call(
        body, name="attn_bwd", grid=(ngrp, nq),
        in_specs=[once(qcol), blk(kcol), blk(vcol), once(0), once(0), once(0),
                  pl.BlockSpec((nq, t, t), lambda g, j: (0, 0, 0), pipeline_mode=pl.Buffered(1)),
                  pl.BlockSpec((grp, 2, t, 128), lambda g, j: (g, 0, 0, 0)),
                  pl.BlockSpec((grp, 2, nq, 128), lambda g, j: (g, 0, 0, 0))] + [ANY] * (n + g8),
        out_specs=[colblock(0), blk(0), blk(0)] + [ANY] * (n + g8),
        out_shape=[out] * 3 + _scatter_shapes(scattering)
        + ([jax.ShapeDtypeStruct(gathering.shape, gathering.dtype)] if g8 else []),
        input_output_aliases={9 + n: 3 + n} if g8 else {},
        scratch_shapes=[pltpu.VMEM((grp, 2, nq, t, 128), BF16), pltpu.VMEM((grp, nq, 128, 2 * t), BF16),
                        pltpu.VMEM((grp, 2, nq, t, 128), BF16), pltpu.VMEM((grp, nq, 128, 2 * t), BF16),
                        pltpu.VMEM((grp, s, 128), F32), pltpu.VMEM((grp, 128, t), F32),
                        pltpu.VMEM((grp, 128, t), F32)]
        + (_scatter_sems(n) if n else [])
        + ([pltpu.SemaphoreType.DMA((7,)), pltpu.SemaphoreType.DMA((7,))] if g8 else []),
        compiler_params=_params("arbitrary", "arbitrary", communicates=bool(n + g8)),
    )(z, z, z, yb, dyb, lse, logc, ka, kb, *scattering, *([gathering] if g8 else []))
    return res[0], res[1], res[2], res[3:3 + n], (res[3 + n] if g8 else None)


def in_bwd_norm(dz, wg, x, dx1, g_pre, scattering, gathering=None):
    s = x.shape[0]
    tm = 512
    n = len(scattering)
    g = 0 if gathering is None else 1
    last = (s // tm - 1, N_CHIPS - 1)

    def body(*refs):
        dz_ref, w_ref, x_ref, dx1_ref, g_ref = refs[:5]
        dx_ref, dg_ref = refs[5 + n + g:7 + n + g]
        acc_ref = refs[7 + 2 * n + 2 * g]
        sems = refs[8 + 2 * n + 2 * g:]
        i, k = pl.program_id(0), pl.program_id(1)
        if n:
            send, finish = _scatter_phases(refs[5:5 + n], refs[7 + n + g:7 + 2 * n + g], *sems[:2])
            pl.when((i == 0) & (k == 0))(send)
        if g:
            send8, pass_on8, finish8 = _allgather8_phases(refs[7 + 2 * n + g], *sems[2 * (n > 0):])
            pl.when((i == 0) & (k == 0))(send8)
            pl.when((i == last[0]) & (k == last[1]))(pass_on8)

        @pl.when((i == 0) & (k == 0))
        def _():
            dg_ref[...] = jnp.zeros_like(dg_ref)

        part = _dot(dz_ref[...], w_ref[...], NT)

        @pl.when(k == 0)
        def _():
            acc_ref[...] = part

        @pl.when(k > 0)
        def _():
            acc_ref[...] += part

        @pl.when(k == N_CHIPS - 1)
        def _():
            dh = acc_ref[...]
            xhat, r = _rms(x_ref[...])
            dg_ref[...] += jnp.sum(dh * xhat, axis=0, keepdims=True)
            dx_ref[...] = dx1_ref[...] + _rms_bwd(dh * g_ref[...], xhat, r)

        if n:
            pl.when((i == last[0]) & (k == last[1]))(finish)
        if g:
            pl.when((i == last[0]) & (k == last[1]))(finish8)

    row = pl.BlockSpec((tm, D), lambda i, k: (i, 0))
    vec = pl.BlockSpec((1, D), lambda i, k: (0, 0))
    res = pl.pallas_call(
        body, name="in_bwd_norm", grid=(s // tm, N_CHIPS),
        in_specs=[pl.BlockSpec((tm, IN_SHARD), lambda i, k: (i, k)),
                  pl.BlockSpec((None, D, IN_SHARD), lambda i, k: (k, 0, 0)), row, row, vec] + [ANY] * (n + g),
        out_specs=[row, vec] + [ANY] * (n + g),
        out_shape=[jax.ShapeDtypeStruct((s, D), F32), jax.ShapeDtypeStruct((1, D), F32)]
        + _scatter_shapes(scattering) + ([jax.ShapeDtypeStruct(gathering.shape, gathering.dtype)] if g else []),
        input_output_aliases={5 + n: 2 + n} if g else {},
        scratch_shapes=[pltpu.VMEM((tm, D), F32)] + (_scatter_sems(n) if n else [])
        + ([pltpu.SemaphoreType.DMA((7,)), pltpu.SemaphoreType.DMA((7,))] if g else []),
        compiler_params=_params("arbitrary", "arbitrary", communicates=bool(n + g)),
    )(dz, wg, x, dx1, g_pre, *scattering, *([gathering] if g else []))
    return res[0], res[1], res[2:2 + n], (res[2 + n] if g else None)


def _adamw_math(w, g, m, v):
    m = ADAM_B1 * m + (1.0 - ADAM_B1) * g
    v = ADAM_B2 * v + (1.0 - ADAM_B2) * (g * g)
    m_hat = m / (1.0 - ADAM_B1 ** ADAM_STEP)
    v_hat = v / (1.0 - ADAM_B2 ** ADAM_STEP)
    delta = -ADAM_LR * (m_hat / (jnp.sqrt(v_hat) + ADAM_EPS) + ADAM_WD * w)
    return delta, m, v


def adamw(name, w, g, m, v, tr):
    r, c = w.shape

    def body(w_ref, g_ref, m_ref, v_ref, go_ref, d_ref, nm_ref, nv_ref):
        g = g_ref[...]
        go_ref[...] = g
        d_ref[...], nm_ref[...], nv_ref[...] = _adamw_math(w_ref[...], g, m_ref[...], v_ref[...])

    out = jax.ShapeDtypeStruct((r, c), F32)
    return pl.pallas_call(
        body, name=name, grid=(r // tr,), in_specs=[_rows(tr, c)] * 4, out_specs=[_rows(tr, c)] * 4,
        out_shape=[out] * 4, compiler_params=_params("parallel"),
    )(w, g, m, v)


def _allgather8_phases(buf, send_sems, recv_sems):
    x, y, c, chips = _place()
    me = 2 * x + y
    sibling = (x, y, 1 - c)
    rows = buf.shape[1] // 2

    def part(chip, core):
        return buf.at[chip, pl.ds(core * rows, rows)]

    def copy(k, block, to):
        return pltpu.make_async_remote_copy(src_ref=block, dst_ref=block, send_sem=send_sems.at[k],
                                            recv_sem=recv_sems.at[k], device_id=to, device_id_type=MESH)

    def chip_of(j):
        return 2 * chips[j][0] + chips[j][1]

    def send():
        copy(0, part(me, c), sibling).start()
        for j in range(3):
            copy(1 + j, part(me, c), (chips[j][0], chips[j][1], c)).start()

    def pass_on():
        for j in range(3):
            copy(1 + j, part(chip_of(j), c), (chips[j][0], chips[j][1], c)).wait_recv()
            copy(4 + j, part(chip_of(j), c), sibling).start()

    def finish():
        copy(0, part(me, 1 - c), sibling).wait_recv()
        for j in range(3):
            copy(4 + j, part(chip_of(j), 1 - c), sibling).wait_recv()
        copy(0, part(me, c), sibling).wait_send()
        for j in range(3):
            copy(1 + j, part(me, c), (chips[j][0], chips[j][1], c)).wait_send()
            copy(4 + j, part(chip_of(j), c), sibling).wait_send()

    return send, pass_on, finish


def add_halves(name, g, recv, c_idx, tr):
    n, h, c = recv.shape

    def body(c_ref, g_ref, r_ref, o_ref):
        o_ref[...] = (g_ref[...] + r_ref[...]).astype(BF16)

    nb = h // tr
    return pl.pallas_call(
        body, name=name,
        grid_spec=pltpu.PrefetchScalarGridSpec(
            num_scalar_prefetch=1, grid=(n, nb),
            in_specs=[pl.BlockSpec((None, tr, c), lambda k, i, c_ref: (k, c_ref[0] * nb + i, 0)),
                      pl.BlockSpec((None, tr, c), lambda k, i, c_ref: (k, i, 0))],
            out_specs=pl.BlockSpec((None, tr, c), lambda k, i, c_ref: (k, i, 0))),
        out_shape=jax.ShapeDtypeStruct((n, h, c), BF16), compiler_params=_params("parallel", "parallel"),
    )(c_idx, g, recv)


def sum_chips(name, parts, recv, where, tr, after=None):
    n, h, c = recv.shape
    nb = h // tr

    def body(w_ref, p_ref, r_ref, *rest):
        acc = p_ref[...].astype(F32)
        for k in range(n):
            acc = acc + r_ref[k].astype(F32)
        rest[-1][...] = acc

    return pl.pallas_call(
        body, name=name,
        grid_spec=pltpu.PrefetchScalarGridSpec(
            num_scalar_prefetch=1, grid=(nb,),
            in_specs=[pl.BlockSpec((None, tr, c), lambda i, w_ref: (w_ref[0], i, 0)),
                      pl.BlockSpec((n, tr, c), lambda i, w_ref: (0, i, 0))] + ([ANY] if after is not None else []),
            out_specs=pl.BlockSpec((tr, c), lambda i, w_ref: (w_ref[1] * nb + i, 0))),
        out_shape=jax.ShapeDtypeStruct((2 * h, c), F32), compiler_params=_params("parallel"),
    )(where, parts, recv, *([after] if after is not None else []))


def place_shard(name, shard, where, dtype, tr):
    r, c = shard.shape

    def body(w_ref, s_ref, o_ref):
        o_ref[...] = s_ref[...].astype(dtype)

    return pl.pallas_call(
        body, name=name,
        grid_spec=pltpu.PrefetchScalarGridSpec(
            num_scalar_prefetch=1, grid=(r // tr,),
            in_specs=[pl.BlockSpec((tr, c), lambda i, w_ref: (i, 0))],
            out_specs=pl.BlockSpec((None, tr, c), lambda i, w_ref: (w_ref[0], i, 0))),
        out_shape=jax.ShapeDtypeStruct((N_CHIPS, r, c), dtype), compiler_params=_params("parallel"),
    )(where, shard)


ANY = pl.BlockSpec(memory_space=pl.ANY)


def _place():
    x, y, c = lax.axis_index("x"), lax.axis_index("y"), lax.axis_index("c")
    chips = [(1 - x, y), (x, 1 - y), (1 - x, 1 - y)]
    return x, y, c, chips


def gather_shards(arrays):
    n = len(arrays)

    def body(*refs):
        send, pass_on, finish = _gather_phases(refs[n:2 * n], *refs[2 * n:], _spans(arrays))
        send()
        pass_on()
        finish()

    return pl.pallas_call(
        body, name="gather_shards", in_specs=[ANY] * n, out_specs=[ANY] * n,
        out_shape=[jax.ShapeDtypeStruct(a.shape, a.dtype) for a in _arrays(arrays)],
        input_output_aliases={w: w for w in range(n)}, scratch_shapes=_gather_sems(n),
        compiler_params=pltpu.CompilerParams(has_side_effects=True),
    )(*_arrays(arrays))


def _gather_sems(n):
    return [pltpu.SemaphoreType.DMA((6 * n,)), pltpu.SemaphoreType.DMA((6 * n,))]


class Span(typing.NamedTuple):
    array: jax.Array
    lo: int
    hi: int
    ways: tuple = (0, 1, 2)


def _arrays(gathering):
    return [g.array if isinstance(g, Span) else g for g in gathering]


def _spans(gathering):
    return [(g.lo, g.hi, g.ways) if isinstance(g, Span) else (0, g.shape[1], (0, 1, 2)) for g in gathering]


def _gather_phases(out, send_sems, recv_sems, spans):
    n = len(out)
    if not any(ways for _, _, ways in spans):
        return (lambda: None,) * 3
    x, y, c, chips = _place()
    me = 2 * x + y
    sibling = (x, y, 1 - c)

    def half(w, chip, core):
        lo, hi, _ = spans[w]
        h = (hi - lo) // 2
        return out[w].at[chip, pl.ds(lo + core * h, h)]

    def copy(k, block, to):
        return pltpu.make_async_remote_copy(src_ref=block, dst_ref=block, send_sem=send_sems.at[k],
                                            recv_sem=recv_sems.at[k], device_id=to, device_id_type=MESH)

    def over_ici(w, j, chip):
        return copy(3 * w + j, half(w, chip, c), (chips[j][0], chips[j][1], c))

    def over_d2d(w, j, core):
        return copy(3 * n + 3 * w + j, half(w, 2 * chips[j][0] + chips[j][1], core), sibling)

    pairs = [(w, j) for w in range(n) for j in spans[w][2]]

    def send():
        for w, j in pairs:
            over_ici(w, j, me).start()

    def pass_on():
        for w, j in pairs:
            over_ici(w, j, 2 * chips[j][0] + chips[j][1]).wait_recv()
            over_d2d(w, j, c).start()

    def finish():
        for w, j in pairs:
            over_d2d(w, j, 1 - c).wait_recv()
        for w, j in pairs:
            over_ici(w, j, me).wait_send()
            over_d2d(w, j, c).wait_send()

    return send, pass_on, finish


def _relay_sems():
    return [pltpu.SemaphoreType.DMA((4,)), pltpu.SemaphoreType.DMA((4,))]


def _relay_phases(out, send_sems, recv_sems):
    x, y, c, chips = _place()
    sibling = (x, y, 1 - c)
    rows = out.shape[1]
    quarter = rows // 4
    far = 2 * chips[2][0] + chips[2][1]

    def piece(chip, way, core):
        return out.at[chip, pl.ds(way * (rows // 2) + core * quarter, quarter)]

    def copy(k, block, to):
        return pltpu.make_async_remote_copy(src_ref=block, dst_ref=block, send_sem=send_sems.at[k],
                                            recv_sem=recv_sems.at[k], device_id=to, device_id_type=MESH)

    def over_ici(way, chip):
        return copy(way, piece(chip, way, c), (chips[way][0], chips[way][1], c))

    def over_d2d(way, core):
        return copy(2 + way, piece(far, way, core), sibling)

    def send():
        for way in range(2):
            other = chips[1 - way]
            over_ici(way, 2 * other[0] + other[1]).start()

    def pass_on():
        for way in range(2):
            over_ici(way, far).wait_recv()
            over_d2d(way, c).start()

    def finish():
        for way in range(2):
            over_d2d(way, 1 - c).wait_recv()
        for way in range(2):
            other = chips[1 - way]
            over_ici(way, 2 * other[0] + other[1]).wait_send()
            over_d2d(way, c).wait_send()

    return send, pass_on, finish


def swap_halves(name, grads):
    n = len(grads)

    def body(*refs):
        send, finish = _swap_phases(refs[:n], refs[n:2 * n], *refs[2 * n:])
        send()
        finish()

    return pl.pallas_call(
        body, name=name, in_specs=[ANY] * n, out_specs=[ANY] * n, out_shape=_swap_shapes(grads),
        scratch_shapes=_swap_sems(n), compiler_params=pltpu.CompilerParams(has_side_effects=True),
    )(*grads)


def _swap_shapes(grads):
    return [jax.ShapeDtypeStruct((a.shape[0], a.shape[1] // 2, a.shape[2]), a.dtype) for a in grads]


def _swap_sems(n):
    return [pltpu.SemaphoreType.DMA((n,)), pltpu.SemaphoreType.DMA((n,))]


def _swap_phases(g, out, send_sems, recv_sems):
    x, y, c, _ = _place()

    def copies():
        return [pltpu.make_async_remote_copy(
            src_ref=g[w].at[:, pl.ds((1 - c) * (g[w].shape[1] // 2), g[w].shape[1] // 2)], dst_ref=out[w],
            send_sem=send_sems.at[w], recv_sem=recv_sems.at[w], device_id=(x, y, 1 - c), device_id_type=MESH)
            for w in range(len(g))]

    def send():
        for cp in copies():
            cp.start()

    def finish():
        for cp in copies():
            cp.wait()

    return send, finish


def _send_phases(g, out, send_sems, recv_sems):
    x, y, c, _ = _place()

    def copies():
        return [pltpu.make_async_remote_copy(
            src_ref=g[w], dst_ref=out[w], send_sem=send_sems.at[w], recv_sem=recv_sems.at[w],
            device_id=(x, y, 1 - c), device_id_type=MESH) for w in range(len(g))]

    def send():
        for cp in copies():
            cp.start()

    def finish():
        for cp in copies():
            cp.wait()

    return send, finish


def dw_in_half(name, h, dz, which, sending):
    s = h.shape[0]
    hh, tb = D // 2, IN_SHARD // 2
    n = len(sending)
    steps = IN_COLS // tb

    def body(w_ref, *refs):
        a_ref, b_ref, o_ref = refs[0], refs[1], refs[2 + n]
        j = pl.program_id(0)
        if n:
            send, finish = _send_phases(refs[2:2 + n], refs[3 + n:3 + 2 * n], *refs[3 + 2 * n:])
            pl.when(j == 0)(send)
        o_ref[...] = _dot(a_ref[...], b_ref[...], TN)
        if n:
            pl.when(j == steps - 1)(finish)

    out = pl.pallas_call(
        body, name=name,
        grid_spec=pltpu.PrefetchScalarGridSpec(
            num_scalar_prefetch=1, grid=(steps,),
            in_specs=[pl.BlockSpec((s, hh), lambda j, w: (0, w[0])), pl.BlockSpec((s, tb), lambda j, w: (0, j))]
            + [ANY] * n,
            out_specs=[pl.BlockSpec((None, hh, tb), lambda j, w: (j // 2, 0, j % 2))] + [ANY] * n,
            scratch_shapes=_swap_sems(n) if n else []),
        out_shape=[jax.ShapeDtypeStruct((N_CHIPS, hh, IN_SHARD), F32)]
        + [jax.ShapeDtypeStruct(a.shape, a.dtype) for a in sending],
        compiler_params=_params("arbitrary", communicates=bool(n)),
    )(which, h, dz, *sending)
    return out[0], out[1:]


def scatter_chips(parts):
    n = len(parts)

    def body(*refs):
        send, finish = _scatter_phases(refs[:n], refs[n:2 * n], *refs[2 * n:])
        send()
        finish()

    return pl.pallas_call(
        body, name="scatter_chips", in_specs=[ANY] * n, out_specs=[ANY] * n,
        out_shape=_scatter_shapes(parts), scratch_shapes=_scatter_sems(n),
        compiler_params=pltpu.CompilerParams(has_side_effects=True),
    )(*parts)


def _scatter_shapes(parts):
    return [jax.ShapeDtypeStruct((3,) + a.shape[1:], a.dtype) for a in parts]


def _scatter_sems(n):
    return [pltpu.SemaphoreType.DMA((3 * n,)), pltpu.SemaphoreType.DMA((3 * n,))]


def _scatter_phases(p, out, send_sems, recv_sems):
    x, y, c, chips = _place()

    def copies():
        return [pltpu.make_async_remote_copy(
            src_ref=p[w].at[2 * px + py], dst_ref=out[w].at[j], send_sem=send_sems.at[3 * w + j],
            recv_sem=recv_sems.at[3 * w + j], device_id=(px, py, c), device_id_type=MESH)
            for w in range(len(p)) for j, (px, py) in enumerate(chips)]

    def send():
        for cp in copies():
            cp.start()

    def finish():
        for cp in copies():
            cp.wait()

    return send, finish


def _join_only(arrays):
    n = len(arrays)

    def body(*refs):
        out = refs[n:2 * n]
        send_sems, recv_sems = refs[2 * n:]
        x, y, c, _ = _place()

        def copy(w, core):
            h = out[w].shape[0] // 2
            rows = out[w].at[pl.ds(core * h, h)]
            return pltpu.make_async_remote_copy(
                src_ref=rows, dst_ref=rows, send_sem=send_sems.at[w], recv_sem=recv_sems.at[w],
                device_id=(x, y, 1 - c), device_id_type=MESH)

        for w in range(n):
            copy(w, c).start()
        for w in range(n):
            copy(w, 1 - c).wait_recv()
        for w in range(n):
            copy(w, c).wait_send()

    return pl.pallas_call(
        body, name="join_only", in_specs=[ANY] * n, out_specs=[ANY] * n,
        out_shape=[jax.ShapeDtypeStruct(a.shape, a.dtype) for a in arrays],
        input_output_aliases={w: w for w in range(n)},
        scratch_shapes=[pltpu.SemaphoreType.DMA((n,)), pltpu.SemaphoreType.DMA((n,))],
        compiler_params=pltpu.CompilerParams(has_side_effects=True),
    )(*arrays)


HBM = pl.BlockSpec(memory_space=pltpu.HBM)
SEM = pl.BlockSpec(memory_space=pltpu.SEMAPHORE)
DATAFLOW = pltpu.SideEffectType.DATAFLOW_SIDE_EFFECTING


def _scatter_copies(p_ref, land_ref, send_sems, recv_sems):
    x, y, c, chips = _place()
    return [pltpu.make_async_remote_copy(
        src_ref=p_ref.at[2 * px + py], dst_ref=land_ref.at[j], send_sem=send_sems[j], recv_sem=recv_sems[j],
        device_id=(px, py, c), device_id_type=MESH) for j, (px, py) in enumerate(chips)]


def scatter_start(p):
    land = jax.ShapeDtypeStruct((3,) + p.shape[1:], p.dtype)

    def body(p_ref, land_ref, *outs):
        for cp in _scatter_copies(p_ref, land_ref, outs[0:3], outs[3:6]):
            cp.start()
        outs[8][...] = jnp.zeros_like(outs[8])

    return pl.pallas_call(
        body, name="scatter_start",
        out_shape=(pltpu.SemaphoreType.DMA(()),) * 6
        + (pltpu.HBM(p.shape, p.dtype), pltpu.HBM(land.shape, land.dtype), jax.ShapeDtypeStruct((8, 128), F32)),
        in_specs=(HBM, HBM), out_specs=(SEM,) * 6 + (HBM, HBM, pl.BlockSpec(memory_space=pltpu.VMEM)),
        input_output_aliases={0: 6, 1: 7},
        compiler_params=pltpu.CompilerParams(has_side_effects=DATAFLOW),
    )(pltpu.with_memory_space_constraint(p, pltpu.HBM),
      pltpu.with_memory_space_constraint(lax.empty(land.shape, land.dtype), pltpu.HBM))


def scatter_wait(started, after):
    sems, p_thru, land_thru = started[0:6], started[6], started[7]

    def body(p_ref, land_ref, *refs):
        for cp in _scatter_copies(p_ref, land_ref, refs[0:3], refs[3:6]):
            cp.wait_send()
            cp.wait_recv()

    return pl.pallas_call(
        body, name="scatter_wait",
        out_shape=(pltpu.HBM(p_thru.shape, p_thru.dtype), pltpu.HBM(land_thru.shape, land_thru.dtype)),
        in_specs=(HBM, HBM) + (SEM,) * 6 + (pl.BlockSpec(memory_space=pl.ANY),) * len(after), out_specs=(HBM, HBM),
        input_output_aliases={0: 0, 1: 1},
        compiler_params=pltpu.CompilerParams(has_side_effects=DATAFLOW),
    )(p_thru, land_thru, *sems, *after)


def _gather_leg_copies(buf_ref, leg, send_sems, recv_sems):
    x, y, c, chips = _place()
    h = buf_ref.shape[1] // 2
    copies = []
    for j, (px, py) in enumerate(chips):
        chip, to = (2 * x + y, (px, py, c)) if leg == "ici" else (2 * px + py, (x, y, 1 - c))
        block = buf_ref.at[chip, pl.ds(c * h, h)]
        copies.append(pltpu.make_async_remote_copy(
            src_ref=block, dst_ref=block, send_sem=send_sems[j], recv_sem=recv_sems[j],
            device_id=to, device_id_type=MESH))
    return copies


def gather_leg_start(name, buf, leg):
    def body(buf_ref, *outs):
        for cp in _gather_leg_copies(buf_ref, leg, outs[0:3], outs[3:6]):
            cp.start()
        outs[7][...] = jnp.zeros_like(outs[7])

    return pl.pallas_call(
        body, name=name,
        out_shape=(pltpu.SemaphoreType.DMA(()),) * 6
        + (pltpu.HBM(buf.shape, buf.dtype), jax.ShapeDtypeStruct((8, 128), F32)),
        in_specs=(HBM,), out_specs=(SEM,) * 6 + (HBM, pl.BlockSpec(memory_space=pltpu.VMEM)),
        input_output_aliases={0: 6},
        compiler_params=pltpu.CompilerParams(has_side_effects=DATAFLOW),
    )(pltpu.with_memory_space_constraint(buf, pltpu.HBM))


def gather_leg_wait(name, started, leg, after):
    def body(buf_ref, *refs):
        for cp in _gather_leg_copies(buf_ref, leg, refs[0:3], refs[3:6]):
            cp.wait_send()
            cp.wait_recv()

    buf = started[6]
    return pl.pallas_call(
        body, name=name, out_shape=pltpu.HBM(buf.shape, buf.dtype),
        in_specs=(HBM,) + (SEM,) * 6 + (pl.BlockSpec(memory_space=pl.ANY),) * len(after), out_specs=HBM,
        input_output_aliases={0: 0},
        compiler_params=pltpu.CompilerParams(has_side_effects=DATAFLOW),
    )(buf, *started[0:6], *after)


def join_halves(arrays, gathering=None):
    n = len(arrays)
    if gathering is None:
        return _join_only(arrays), None

    def body(*refs):
        out = refs[n + 1:2 * n + 1]
        send_sems, recv_sems = refs[2 * n + 2:2 * n + 4]
        send8, pass_on8, finish8 = _allgather8_phases(refs[2 * n + 1], *refs[2 * n + 4:])
        x, y, c, _ = _place()

        def copy(w, core):
            h = out[w].shape[0] // 2
            rows = out[w].at[pl.ds(core * h, h)]
            return pltpu.make_async_remote_copy(
                src_ref=rows, dst_ref=rows, send_sem=send_sems.at[w], recv_sem=recv_sems.at[w],
                device_id=(x, y, 1 - c), device_id_type=MESH)

        send8()
        for w in range(n):
            copy(w, c).start()
        pass_on8()
        for w in range(n):
            copy(w, 1 - c).wait_recv()
        finish8()
        for w in range(n):
            copy(w, c).wait_send()

    res = pl.pallas_call(
        body, name="join_halves", in_specs=[ANY] * (n + 1), out_specs=[ANY] * (n + 1),
        out_shape=[jax.ShapeDtypeStruct(a.shape, a.dtype) for a in list(arrays) + [gathering]],
        input_output_aliases={w: w for w in range(n + 1)},
        scratch_shapes=[pltpu.SemaphoreType.DMA((n,)), pltpu.SemaphoreType.DMA((n,)),
                        pltpu.SemaphoreType.DMA((7,)), pltpu.SemaphoreType.DMA((7,))],
        compiler_params=pltpu.CompilerParams(has_side_effects=True),
    )(*arrays, gathering)
    return res[:n], res[n]


def allreduce_small(packed):
    r, c = packed.shape
    n_dev = 8

    def body(x_ref, all_ref, sum_ref, send_sems, recv_sems, local_sem):
        x, y, cc, chips = _place()
        me, sibling = (x, y, cc), (x, y, 1 - cc)

        def rows(px, py, pc):
            return all_ref.at[4 * px + 2 * py + pc]

        def copy(k, block, to, src=None):
            return pltpu.make_async_remote_copy(
                src_ref=rows(*block) if src is None else src, dst_ref=rows(*block), send_sem=send_sems.at[k],
                recv_sem=recv_sems.at[k], device_id=to, device_id_type=MESH)

        mine = pltpu.make_async_copy(x_ref, rows(*me), local_sem)
        mine.start()
        first = [copy(0, me, sibling, src=x_ref)]
        first += [copy(1 + j, me, (*chip, cc), src=x_ref) for j, chip in enumerate(chips)]
        for cp in first:
            cp.start()
        passed = [copy(4 + j, (*chip, cc), sibling) for j, chip in enumerate(chips)]
        for j, chip in enumerate(chips):
            copy(1 + j, (*chip, cc), me).wait_recv()
            passed[j].start()
        copy(0, sibling, me).wait_recv()
        for j, chip in enumerate(chips):
            copy(4 + j, (*chip, 1 - cc), me).wait_recv()
        for cp in first + passed:
            cp.wait_send()
        mine.wait()
        acc = all_ref[0]
        for k in range(1, n_dev):
            acc = acc + all_ref[k]
        sum_ref[...] = acc

    vm = pl.BlockSpec(memory_space=pltpu.VMEM)
    return pl.pallas_call(
        body, name="allreduce_small", in_specs=[vm], out_specs=[vm, vm],
        out_shape=[jax.ShapeDtypeStruct((n_dev, r, c), F32), jax.ShapeDtypeStruct((r, c), F32)],
        scratch_shapes=[pltpu.SemaphoreType.DMA((7,)), pltpu.SemaphoreType.DMA((7,)), pltpu.SemaphoreType.DMA],
        compiler_params=pltpu.CompilerParams(has_side_effects=True, vmem_limit_bytes=VMEM_LIMIT),
    )(packed)[1]


def local_step(x, target, vecs, w_s, bs_t, bg, wg_in, late, core=None, order=None, where=None):
    on_mesh = core is not None

    def add(names, grads, recv):
        return [add_halves("add_" + n, g, r, core, min(r.shape[1], 256)) for n, g, r in zip(names, grads, recv)]

    g_pre, ln_g, ln_b, g_post, g_fpre, g_fpost = vecs
    s = x.shape[0]
    if order is None:
        order = jnp.arange(N_CHIPS, dtype=jnp.int32)
    logc = _attn_tables(s)
    ka, kb = _alibi_tables(s)

    h = norm_pre(x, g_pre)
    if not on_mesh:
        wg_a, wg_b, wg_out, wg_ff1, wg_ff2 = late
    if on_mesh:
        z, wg_in, (wg_a, wg_b, wg_out, wg_ff1, wg_ff2) = mm_in(h, wg_in, order, True, late)
        ya, (wg_b,) = gating_fwd(z, ln_g, ln_b, w_s, bs_t, [wg_b])
        yb, lse, (wg_a, wg_ff1, wg_out, bg) = attn_fwd(z, logc, ka, kb, [wg_a, wg_ff1, wg_out, bg])
        over_ici = gather_leg_start("ff2_ici_start", wg_ff2, "ici")
        bg = jnp.transpose(bg[:, :2, :], (1, 0, 2)).reshape(2, D) + over_ici[7][0:1, 0:1]
    else:
        z, _, _ = mm_in(h, wg_in, order, False)
        ya, _ = gating_fwd(z, ln_g, ln_b, w_s, bs_t, [])
        yb, lse, _ = attn_fwd(z, logc, ka, kb, [])
    merged, pa, pb, _ = proj_merge(ya, yb, wg_a.reshape(D, D), wg_b.reshape(D, D), z, bg, [])
    w_out = wg_out.reshape(D, D)
    o, x1, h2, _ = out_norm(merged, w_out, x, g_post, g_fpre, [])
    a, rl, _ = mm_ff1(h2, wg_ff1, [])
    if on_mesh:
        over_d2d = gather_leg_start("ff2_d2d_start", gather_leg_wait("ff2_ici_wait", over_ici, "ici", [rl]), "d2d")
        wg_ff2 = gather_leg_wait("ff2_d2d_wait", over_d2d, "d2d", [])
    w_ff2 = wg_ff2.reshape(D_FF, D)
    dy, df, d_gfpost, loss = ff2_loss(rl, w_ff2, x1, target, g_fpost)

    half_cols = pl.BlockSpec((D, D // 2), lambda i, j: (0, j))
    d_wff2 = mm_tn("dw_ff2", rl, df, D // 2, D, (D_FF, D), pl.BlockSpec((D // 2, D), lambda i, j: (i, 0)))
    da = ff2_bwd(df, w_ff2, a)
    d_wff1 = mm_tn("dw_ff1", h2, da, D, D // 2, (N_CHIPS, D, D),
                   pl.BlockSpec((None, D, D // 2), lambda i, j: (j // 2, 0, j % 2)))
    d_ff = [d_wff1, d_wff2.reshape(N_CHIPS, D, D)]
    dx1, do, d_gfpre, d_gpost, recv_ff = ff1_bwd_norms(da, wg_ff1, x1, o, dy, g_fpre, g_post, d_ff if on_mesh else [])
    d_wout = mm_tn("dw_out", merged, do, D, D // 2, (D, D), half_cols)
    dpa, dpb, dga, dgb, d_bg = out_bwd_gates(do, w_out, pa, pb, z, bg)
    d_wa = mm_tn("dw_a", ya, dpa, D, D // 2, (D, D), half_cols)
    d_wb = mm_tn("dw_b", yb, dpb, D, D // 2, (D, D), half_cols)
    dya = mm_nt("dy_a", dpa, wg_a.reshape(D, D))
    dyb = mm_nt("dy_b", dpb, wg_b.reshape(D, D))
    d_proj = [d_wa.reshape(N_CHIPS, D // N_CHIPS, D), d_wb.reshape(N_CHIPS, D // N_CHIPS, D),
              d_wout.reshape(N_CHIPS, D // N_CHIPS, D)]
    du, dv, d_ws, d_bs, d_lng, d_lnb, recv_proj = gating_bwd(z, dya, ln_g, ln_b, w_s, bs_t, d_proj if on_mesh else [])
    early = d_proj + d_ff
    parts_early = add(BIG[1:], early, list(recv_proj) + list(recv_ff)) if on_mesh else []
    small = dict(b_gate=d_bg, ln_v_g=d_lng, ln_v_b=d_lnb, w_s=d_ws, b_s=d_bs[:, 0, :],
                 norm_mix_post=d_gpost, norm_ffn_pre=d_gfpre, norm_ffn_post=d_gfpost)
    packed = pack_small(dict(small, norm_mix_pre=jnp.zeros((1, D), F32)), loss, where) if on_mesh else None
    dq, dk, dvb, got_early, packed = attn_bwd(z, yb, dyb, lse, logc, ka, kb, parts_early, packed)
    dz = jnp.concatenate([du, dv, dq, dk, dvb, dga, dgb], axis=1)
    if on_mesh:
        for_sibling, _ = dw_in_half("dw_in_sibling", h, dz, 1 - core, [])
        mine, from_sibling = dw_in_half("dw_in_mine", h, dz, core, [for_sibling])
        d_win = None
        parts_late = [add_halves("add_w_in", mine, from_sibling[0], jnp.zeros((1,), jnp.int32), 256)]
    else:
        half = IN_SHARD // 2
        d_win = mm_tn("dw_in", h, dz, D, half, (N_CHIPS, D, IN_SHARD),
                      pl.BlockSpec((None, D, half), lambda i, j: (j // 2, 0, j % 2)))
        parts_late = []
    started = scatter_start(parts_late[0]) if on_mesh else None
    dx, d_gpre, _, _ = in_bwd_norm(dz, wg_in, x, dx1, g_pre + started[8][0:1, 0:1] if on_mesh else g_pre, [])
    small["norm_mix_pre"] = d_gpre
    return loss[0, 0], dx, [d_win] + early, small, parts_early, list(got_early), packed, started


BIG = ("w_in", "w_a_proj", "w_b_proj", "w_out", "w_ff1", "w_ff2")
SMALL = ("norm_mix_pre", "ln_v_g", "ln_v_b", "b_s", "norm_mix_post", "norm_ffn_pre", "norm_ffn_post", "w_s", "b_gate")
ORDER = ("norm_mix_pre", "w_in", "b_gate", "ln_v_g", "ln_v_b", "w_s", "b_s", "w_a_proj", "w_b_proj", "w_out",
         "norm_mix_post", "norm_ffn_pre", "w_ff1", "w_ff2", "norm_ffn_post")
VEC_ROWS = D // 128
WS_ROW = 7 * VEC_ROWS
BG_ROW = WS_ROW + GROUPS * CHUNK
LOSS_ROW = BG_ROW + 2 * VEC_ROWS
PACK_ROWS = LOSS_ROW + 8


def pack_small(small, loss, where):
    vectors = [small[n] for n in SMALL[:7]]
    operands = vectors + [small["w_s"], small["b_gate"], loss]

    def body(where_ref, *refs):
        out = refs[-1]
        ws_ref, bg_ref, loss_ref = refs[7:10]
        for i, n in enumerate(SMALL[:7]):
            if n == "b_s":
                out[i * VEC_ROWS:(i + 1) * VEC_ROWS, :] = refs[i][...]
            else:
                for j in range(VEC_ROWS):
                    out[i * VEC_ROWS + j:i * VEC_ROWS + j + 1, :] = refs[i][:, j * 128:(j + 1) * 128]
        for g in range(GROUPS):
            out[WS_ROW + g * CHUNK:WS_ROW + (g + 1) * CHUNK, :] = ws_ref[g]
        for r in range(2):
            for j in range(VEC_ROWS):
                row = BG_ROW + r * VEC_ROWS + j
                out[row:row + 1, :] = bg_ref[r:r + 1, j * 128:(j + 1) * 128]
        lane = lax.broadcasted_iota(jnp.int32, (8, 128), 1)
        sub = lax.broadcasted_iota(jnp.int32, (8, 128), 0)
        out[LOSS_ROW:LOSS_ROW + 8, :] = jnp.where((lane == 0) & (sub == 0), loss_ref[...], 0.0)

    return pl.pallas_call(
        body, name="pack_small",
        grid_spec=pltpu.PrefetchScalarGridSpec(
            num_scalar_prefetch=1, grid=(1,), in_specs=[_full(a.shape) for a in operands],
            out_specs=pl.BlockSpec((None, PACK_ROWS, 128), lambda i, w: (w[0], w[1], 0))),
        out_shape=jax.ShapeDtypeStruct((N_CHIPS, 2 * PACK_ROWS, 128), F32), compiler_params=_params("arbitrary"),
    )(where, *operands)


def pack_vector(vec, where):
    def body(where_ref, v_ref, out):
        for j in range(VEC_ROWS):
            out[j:j + 1, :] = v_ref[:, j * 128:(j + 1) * 128]

    return pl.pallas_call(
        body, name="pack_vector",
        grid_spec=pltpu.PrefetchScalarGridSpec(
            num_scalar_prefetch=1, grid=(1,), in_specs=[_full(vec.shape)],
            out_specs=pl.BlockSpec((None, VEC_ROWS, 128), lambda i, w: (w[0], w[1], 0))),
        out_shape=jax.ShapeDtypeStruct((N_CHIPS, 2 * VEC_ROWS, 128), F32), compiler_params=_params("arbitrary"),
    )(where, vec)


def adamw_small(gathered, first, chip, w, m, v):
    shapes = {n: (1, D) for n in SMALL}
    shapes.update(b_s=(GROUPS, CHUNK), w_s=(GROUPS * CHUNK, CHUNK), b_gate=(2, D // N_CHIPS))
    flat = lambda t: [t[n].reshape(shapes[n]) for n in SMALL]
    per = D // N_CHIPS // 128

    def body(chip_ref, all_ref, first_ref, *refs):
        params, outs = refs[:27], refs[27:]
        sub = lax.broadcasted_iota(jnp.int32, (VEC_ROWS, 128), 0)
        sum_ref = outs[36]
        total = all_ref[0, 0:PACK_ROWS, :]
        head = first_ref[0, 0:VEC_ROWS, :]
        for k in range(1, 2 * N_CHIPS):
            total = total + all_ref[k // 2, (k % 2) * PACK_ROWS:(k % 2 + 1) * PACK_ROWS, :]
            head = head + first_ref[k // 2, (k % 2) * VEC_ROWS:(k % 2 + 1) * VEC_ROWS, :]
        sum_ref[...] = total
        sum_ref[0:VEC_ROWS, :] = head

        def gate_row(r):
            rows = sum_ref[BG_ROW + r * VEC_ROWS:BG_ROW + (r + 1) * VEC_ROWS, :]
            return jnp.concatenate([jnp.sum(jnp.where(sub == per * chip_ref[0] + j, rows, 0.0), axis=0, keepdims=True)
                                    for j in range(per)], axis=1)

        for i, n in enumerate(SMALL):
            if n == "b_s":
                g = sum_ref[i * VEC_ROWS:(i + 1) * VEC_ROWS, :]
            elif n == "w_s":
                g = sum_ref[WS_ROW:BG_ROW, :]
            elif n == "b_gate":
                g = jnp.concatenate([gate_row(0), gate_row(1)], axis=0)
            else:
                g = jnp.concatenate([sum_ref[i * VEC_ROWS + j:i * VEC_ROWS + j + 1, :] for j in range(VEC_ROWS)],
                                    axis=1)
            delta, nm, nv = _adamw_math(params[i][...], g, params[9 + i][...], params[18 + i][...])
            outs[4 * i][...], outs[4 * i + 1][...], outs[4 * i + 2][...], outs[4 * i + 3][...] = g, delta, nm, nv

    vm = pl.BlockSpec(memory_space=pltpu.VMEM)
    res = pl.pallas_call(
        body, name="adamw_small",
        in_specs=[pl.BlockSpec(memory_space=pltpu.SMEM)] + [vm] * 29, out_specs=[vm] * 37,
        out_shape=[jax.ShapeDtypeStruct(shapes[n], F32) for n in SMALL for _ in range(4)]
        + [jax.ShapeDtypeStruct((PACK_ROWS, 128), F32)],
        compiler_params=_params(),
    )(chip, gathered, first, *flat(w), *flat(m), *flat(v))
    new = {n: tuple(r.reshape(w[n].shape) for r in res[4 * i:4 * i + 4]) for i, n in enumerate(SMALL)}
    return new, res[36][LOSS_ROW, 0]


def kernel(x, norm_mix_pre, w_in, b_gate, ln_v_g, ln_v_b, w_s, b_s, w_a_proj, w_b_proj, w_out, norm_mix_post, norm_ffn_pre, w_ff1, w_ff2, norm_ffn_post, loss_target, m_norm_mix_pre, m_w_in, m_b_gate, m_ln_v_g, m_ln_v_b, m_w_s, m_b_s, m_w_a_proj, m_w_b_proj, m_w_out, m_norm_mix_post, m_norm_ffn_pre, m_w_ff1, m_w_ff2, m_norm_ffn_post, v_norm_mix_pre, v_w_in, v_b_gate, v_ln_v_g, v_ln_v_b, v_w_s, v_b_s, v_w_a_proj, v_w_b_proj, v_w_out, v_norm_mix_post, v_norm_ffn_pre, v_w_ff1, v_w_ff2, v_norm_ffn_post):
    w = dict(norm_mix_pre=norm_mix_pre, w_in=w_in, b_gate=b_gate, ln_v_g=ln_v_g, ln_v_b=ln_v_b, w_s=w_s, b_s=b_s,
             w_a_proj=w_a_proj, w_b_proj=w_b_proj, w_out=w_out, norm_mix_post=norm_mix_post,
             norm_ffn_pre=norm_ffn_pre, w_ff1=w_ff1, w_ff2=w_ff2, norm_ffn_post=norm_ffn_post)
    m = dict(norm_mix_pre=m_norm_mix_pre, w_in=m_w_in, b_gate=m_b_gate, ln_v_g=m_ln_v_g, ln_v_b=m_ln_v_b, w_s=m_w_s,
             b_s=m_b_s, w_a_proj=m_w_a_proj, w_b_proj=m_w_b_proj, w_out=m_w_out, norm_mix_post=m_norm_mix_post,
             norm_ffn_pre=m_norm_ffn_pre, w_ff1=m_w_ff1, w_ff2=m_w_ff2, norm_ffn_post=m_norm_ffn_post)
    v = dict(norm_mix_pre=v_norm_mix_pre, w_in=v_w_in, b_gate=v_b_gate, ln_v_g=v_ln_v_g, ln_v_b=v_ln_v_b, w_s=v_w_s,
             b_s=v_b_s, w_a_proj=v_w_a_proj, w_b_proj=v_w_b_proj, w_out=v_w_out, norm_mix_post=v_norm_mix_post,
             norm_ffn_pre=v_norm_ffn_pre, w_ff1=v_w_ff1, w_ff2=v_w_ff2, norm_ffn_post=v_norm_ffn_post)
    chip = 2 * lax.axis_index("x") + lax.axis_index("y")
    core = lax.axis_index("c")

    where = jnp.stack([chip, core]).astype(jnp.int32)
    wg_in = place_shard("place_w_in", w_in[0], where, BF16, 256)
    bg_all = place_shard("place_b_gate", jnp.pad(b_gate[0], ((0, 14), (0, 0))), where, F32, 16)
    vecs = (norm_mix_pre, ln_v_g, ln_v_b, norm_mix_post, norm_ffn_pre, norm_ffn_post)
    loss, dx, _, small, parts, got, packed, started = local_step(
        x[0], loss_target[0], vecs, w_s[0], b_s[0].T, bg_all, wg_in, [w[n][0] for n in BIG[1:]],
        core=jnp.reshape(core, (1,)).astype(jnp.int32),
        order=jnp.stack([chip, chip ^ 2, chip ^ 1, chip ^ 3]).astype(jnp.int32), where=where)

    halves = [sum_chips("sum_" + n, p, r, where, min(p.shape[1], 256), started[8])
              for n, p, r in zip(BIG[1:], parts, got)]
    joined, _ = join_halves(halves)
    grads = dict(zip(BIG[1:], joined))
    new = {}

    def update(n):
        shape = w[n].shape
        res = adamw("adamw_" + n, w[n][0], grads[n], m[n][0], v[n][0], min(shape[1], 256))
        new[n] = tuple(r.reshape(shape) for r in res)

    for n in BIG[1:]:
        update(n)
    p_in, got_in = scatter_wait(started, [new["w_ff2"][1], dx])
    (grads["w_in"],), first = join_halves([sum_chips("sum_w_in", p_in, got_in, where, 256)],
                                          pack_vector(small["norm_mix_pre"], where))
    update("w_in")
    small_new, loss = adamw_small(packed, first, jnp.reshape(chip, (1,)).astype(jnp.int32), w, m, v)
    new.update(small_new)

    outs = [loss, dx[None]]
    for i in range(4):
        outs += [new[n][i] for n in ORDER]
    return tuple(outs)
```

```python
import functools
import math
import typing

import numpy as np
import jax
import jax.numpy as jnp
from jax import lax
from jax.experimental import pallas as pl
from jax.experimental.pallas import tpu as pltpu

F32 = jnp.float32
BF16 = jnp.bfloat16
MESH = pl.DeviceIdType.MESH

D = 1024
EPS = 1e-6
CHUNK = 128
GROUPS = 8
HEADS = 16
HEAD_DIM = 64
ATT_T = 256
ATT_GROUP = 8
ATT_BWD_GROUP = 4
N_CHIPS = 4
D_FF = 4 * D
IN_COLS = 7 * D
IN_SHARD = IN_COLS // N_CHIPS
MASKED = -1e30
VMEM_LIMIT = 56 * 2 ** 20

ADAM_LR, ADAM_B1, ADAM_B2, ADAM_EPS, ADAM_WD, ADAM_STEP = 0.001, 0.9, 0.999, 1e-08, 0.01, 10

NN = (((1,), (0,)), ((), ()))
NT = (((1,), (1,)), ((), ()))
TN = (((0,), (0,)), ((), ()))


def _dot(a, b, dims=NN):
    return lax.dot_general(a, b, dims, preferred_element_type=F32)


def _params(*sem, communicates=False):
    return pltpu.CompilerParams(dimension_semantics=sem or None, vmem_limit_bytes=VMEM_LIMIT,
                                has_side_effects=communicates)


def _rows(tr, c, col=0):
    return pl.BlockSpec((tr, c), lambda i: (i, col))


def _full(shape):
    n = len(shape)
    return pl.BlockSpec(shape, lambda *_: (0,) * n)


def _gelu(x):
    k = math.sqrt(2.0 / math.pi)
    return 0.5 * x * (1.0 + jnp.tanh(k * (x + 0.044715 * x * x * x)))


def _gelu_and_grad(x):
    k = math.sqrt(2.0 / math.pi)
    t = jnp.tanh(k * (x + 0.044715 * x * x * x))
    g = 0.5 * x * (1.0 + t)
    dg = 0.5 * (1.0 + t) + 0.5 * x * (1.0 - t * t) * (k * (1.0 + 3.0 * 0.044715 * x * x))
    return g, dg


def _sigmoid(x):
    return 1.0 / (1.0 + jnp.exp(-x))


def _rms(x):
    r = lax.rsqrt(jnp.mean(x * x, axis=-1, keepdims=True) + EPS)
    return x * r, r


def _rms_bwd(dn, xhat, r):
    return r * (dn - xhat * jnp.mean(dn * xhat, axis=-1, keepdims=True))


def norm_pre(x, g):
    s = x.shape[0]
    tr = 512

    def body(x_ref, g_ref, h_ref):
        xhat, _ = _rms(x_ref[...])
        h_ref[...] = (xhat * g_ref[...]).astype(BF16)

    return pl.pallas_call(
        body, name="norm_pre", grid=(s // tr,),
        in_specs=[_rows(tr, D), _full((1, D))], out_specs=_rows(tr, D),
        out_shape=jax.ShapeDtypeStruct((s, D), BF16), compiler_params=_params("parallel"),
    )(x, g)


def mm_in(h, wg, order, staged, casting=()):
    s = h.shape[0]
    tm, tn = 1024, IN_SHARD // 2
    per = IN_SHARD // tn
    m = len(casting)
    nj, ni = N_CHIPS * per, s // tm
    cast_steps = per * ni

    def body(order_ref, *refs):
        a_ref = refs[0]
        cast_in = refs[2:2 + m]
        o_ref, held = refs[2 + m], refs[3 + m]
        cast_out = refs[4 + m:4 + 2 * m]
        tile, tile_sem = refs[4 + 2 * m:6 + 2 * m]
        sems = refs[6 + 2 * m:]
        j, i = pl.program_id(0), pl.program_id(1)

        @pl.when(j * ni + i < cast_steps)
        def _():
            for src, dst in zip(cast_in, cast_out):
                dst[...] = src[...].astype(BF16)

        def fetch(t):
            chip = order_ref[t // per]
            return pltpu.make_async_copy(held.at[chip, :, pl.ds((t % per) * tn, tn)], tile.at[t % 2],
                                         tile_sem.at[t % 2])

        if staged:
            near = _gather_phases([held], *sems[:2], [(0, D, (0, 1))])
            far = _relay_phases(held, *sems[2:])

        @pl.when(i == 0)
        def _():
            @pl.when(j == 0)
            def _():
                if staged:
                    near[0]()
                fetch(0).start()

            fetch(j).wait()
            ahead = j + 1 < nj
            if staged:
                ahead = ahead & (j + 1 != per) & (j + 1 != 3 * per)
            pl.when(ahead)(lambda: fetch(j + 1).start())

        rows = pl.ds(pl.multiple_of(i * tm, tm), tm)
        o_ref[...] = _dot(a_ref[rows, :], tile[j % 2]).astype(BF16)

        if staged:
            @pl.when((i == ni - 1) & (j == per - 1))
            def _():
                near[1]()
                near[2]()
                far[0]()
                fetch(per).start()

            @pl.when((i == ni - 1) & (j == 3 * per - 1))
            def _():
                far[1]()
                far[2]()
                fetch(3 * per).start()

    def cast_block(j, i, o):
        return jnp.minimum(j * ni + i, cast_steps - 1)

    out = pl.pallas_call(
        body, name="mm_in",
        grid_spec=pltpu.PrefetchScalarGridSpec(
            num_scalar_prefetch=1, grid=(nj, ni),
            in_specs=[pl.BlockSpec((s, D), lambda j, i, o: (0, 0)), ANY]
            + [pl.BlockSpec((a.shape[0] // cast_steps, a.shape[1]), lambda j, i, o: (cast_block(j, i, o), 0))
               for a in casting],
            out_specs=[pl.BlockSpec((tm, tn), lambda j, i, o: (i, o[j // per] * per + j % per)), ANY]
            + [pl.BlockSpec((None, a.shape[0] // cast_steps, a.shape[1]),
                            lambda j, i, o: (o[0], cast_block(j, i, o), 0)) for a in casting],
            scratch_shapes=[pltpu.VMEM((2, D, tn), BF16), pltpu.SemaphoreType.DMA((2,))]
            + (_gather_sems(1) + _relay_sems() if staged else [])),
        out_shape=[jax.ShapeDtypeStruct((s, IN_COLS), BF16), jax.ShapeDtypeStruct(wg.shape, wg.dtype)]
        + [jax.ShapeDtypeStruct((N_CHIPS,) + a.shape, BF16) for a in casting],
        input_output_aliases={2: 1},
        compiler_params=_params("arbitrary", "arbitrary", communicates=staged),
    )(order, h, wg, *casting)
    return out[0], out[1], out[2:]


def _tril_ws(ws_ref, g):
    r = lax.broadcasted_iota(jnp.int32, (CHUNK, CHUNK), 0)
    c = lax.broadcasted_iota(jnp.int32, (CHUNK, CHUNK), 1)
    return jnp.where(c <= r, ws_ref[g], 0.0).astype(BF16)


def _layer_norm(v):
    mu = jnp.mean(v, axis=-1, keepdims=True)
    d = v - mu
    rstd = lax.rsqrt(jnp.mean(d * d, axis=-1, keepdims=True) + EPS)
    return d * rstd, rstd


def gating_fwd(z, ln_g, ln_b, w_s, bs_t, gathering):
    s = z.shape[0]
    n = len(gathering)
    steps = s // CHUNK

    def body(*refs):
        u_ref, v_ref, lg_ref, lb_ref, ws_ref, bst_ref = refs[:6]
        ya_ref = refs[6 + n]
        ci = pl.program_id(0)
        if n:
            send, pass_on, finish = _gather_phases(refs[7 + n:7 + 2 * n], *refs[7 + 2 * n:], _spans(gathering))
            pl.when(ci == 0)(send)
        ug = _gelu(u_ref[...].astype(F32))
        vhat, _ = _layer_norm(_gelu(v_ref[...].astype(F32)))
        vn = (vhat * lg_ref[...] + lb_ref[...]).astype(BF16)
        for g in range(GROUPS):
            cols = slice(g * CHUNK, (g + 1) * CHUNK)
            mixed = _dot(_tril_ws(ws_ref, g), vn[:, cols]) + bst_ref[:, g:g + 1]
            ya_ref[:, cols] = (ug[:, cols] * mixed).astype(BF16)
        if n:
            pl.when(ci == steps - 1)(pass_on)
            pl.when(ci == steps - 1)(finish)

    out = pl.pallas_call(
        body, name="gating_fwd", grid=(steps,),
        in_specs=[_rows(CHUNK, D, 0), _rows(CHUNK, D, 1), _full((1, D)), _full((1, D)),
                  _full((GROUPS, CHUNK, CHUNK)), _full((CHUNK, GROUPS))] + [ANY] * n,
        out_specs=[_rows(CHUNK, D)] + [ANY] * n,
        out_shape=[jax.ShapeDtypeStruct((s, D), BF16)]
        + [jax.ShapeDtypeStruct(a.shape, a.dtype) for a in _arrays(gathering)],
        input_output_aliases={6 + w: 1 + w for w in range(n)},
        scratch_shapes=_gather_sems(n) if n else [],
        compiler_params=_params("arbitrary", communicates=bool(n)),
    )(z, z, ln_g, ln_b, w_s, bs_t, *_arrays(gathering))
    return out[0], out[1:]


def _attn_tables(s):
    nd = s // ATT_T
    r = np.arange(ATT_T)[None, :, None]
    c = np.arange(ATT_T)[None, None, :]
    delta = np.arange(nd)[:, None, None] * ATT_T + r - c
    count = np.zeros(delta.shape, np.int64)
    for window, dilation in ((128, 1), (512, 4), (2048, 16)):
        count += (delta >= 0) & (delta % dilation == 0) & (delta <= window)
    logc = np.where(count > 0, np.log(np.maximum(count, 1)), MASKED)
    return jnp.asarray(logc, F32)


AUG = 3


def _split3_np(x):
    terms, rest = [], np.asarray(x, np.float64)
    for _ in range(AUG):
        term = np.asarray(rest.astype(jnp.bfloat16), np.float64)
        terms.append(term)
        rest = rest - term
    return terms


def _split3(x):
    terms, rest = [], x
    for _ in range(AUG):
        term = rest.astype(BF16).astype(F32)
        terms.append(term)
        rest = rest - term
    return terms


def _alibi_tables(s):
    nb = s // ATT_T
    slopes = np.exp2(-8.0 * np.arange(1, HEADS + 1, dtype=np.float64) / HEADS)
    ka = np.zeros((HEADS // 2, 2, ATT_T, 128), np.float32)
    kb = np.zeros((HEADS // 2, 2, nb, 128), np.float32)
    for p in range(HEADS // 2):
        for e in range(2):
            base = HEAD_DIM * (1 - e)
            for a, term in enumerate(_split3_np(slopes[2 * p + e] * np.arange(ATT_T))):
                ka[p, e, :, base + a] = term
            for a, term in enumerate(_split3_np(slopes[2 * p + e] * ATT_T * np.arange(nb))):
                kb[p, e, :, base + AUG + a] = term
            ka[p, e, :, base + 2 * AUG:base + 3 * AUG] = 1.0
    return jnp.asarray(ka), jnp.asarray(kb)


def _head_masks():
    lane = lax.broadcasted_iota(jnp.int32, (1, 128), 1)
    first = lane < HEAD_DIM

    def ones(e, n):
        base = HEAD_DIM * (1 - e)
        return ((lane >= base) & (lane < base + n)).astype(F32)

    return first, lane, ones


def _place3(lane, at, terms, other):
    for a, term in enumerate(terms):
        other = jnp.where(lane == at + a, term, other)
    return other


def attn_fwd(z, logc, ka, kb, gathering):
    s = z.shape[0]
    nq = s // ATT_T
    t = ATT_T
    n = len(gathering)
    grp = ATT_GROUP
    ngrp = HEADS // 2 // grp
    wide = 128 * grp
    qcol, kcol, vcol = 2 * D // wide, 3 * D // wide, 4 * D // wide

    def body(*refs):
        q_ref, k_ref, v_ref, lc_ref, ka_ref, kb_ref = refs[:6]
        y_ref, lse_ref = refs[6 + n:8 + n]
        q_s, k_s, v_s, m_s, l_s, acc_s = refs[8 + 2 * n:14 + 2 * n]
        gi, qi = pl.program_id(0), pl.program_id(1)
        first, lane, ones = _head_masks()
        if n:
            send, pass_on, finish = _gather_phases(refs[8 + n:8 + 2 * n], *refs[14 + 2 * n:], _spans(gathering))
            pl.when((gi == 0) & (qi == 0))(send)

        @pl.when(qi == 0)
        def _():
            sel = jnp.broadcast_to(first.astype(F32), (t, 128))
            for pr in range(grp):
                cols = slice(pr * 128, (pr + 1) * 128)
                for jb in range(nq):
                    kj = k_ref[jb * t:(jb + 1) * t, cols].astype(F32)
                    vj = v_ref[jb * t:(jb + 1) * t, cols].astype(F32)
                    k_s[pr, 0, jb] = jnp.where(first, kj, ka_ref[pr, 0] + kb_ref[pr, 0, jb:jb + 1, :]).astype(BF16)
                    k_s[pr, 1, jb] = jnp.where(first, ka_ref[pr, 1] + kb_ref[pr, 1, jb:jb + 1, :], kj).astype(BF16)
                    v_s[pr, jb, 0:t, 0:128] = jnp.where(first, vj, 0.0).astype(BF16)
                    v_s[pr, jb, t:2 * t, 0:128] = jnp.where(first, 0.0, vj).astype(BF16)
                    v_s[pr, jb, 0:t, 128:256] = sel.astype(BF16)
                    v_s[pr, jb, t:2 * t, 128:256] = (1.0 - sel).astype(BF16)

        for pr in range(grp):
            q = q_ref[:, pr * 128:(pr + 1) * 128].astype(F32) * (1.0 / math.sqrt(HEAD_DIM))
            q_s[pr, 0] = jnp.where(first, q, ones(0, 2 * AUG)).astype(BF16)
            q_s[pr, 1] = jnp.where(first, ones(1, 2 * AUG), q).astype(BF16)
        m_s[...] = jnp.full_like(m_s, MASKED)
        l_s[...] = jnp.zeros_like(l_s)
        acc_s[...] = jnp.zeros_like(acc_s)

        def scores(j):
            return tuple(_dot(q_s[pr, e], k_s[pr, e, j], NT) for pr in range(grp) for e in range(2))

        def step(j, carry):
            softmax_block(j, scores(j))
            return carry

        def softmax_block(j, u):
            lc = lc_ref[qi - j]
            for pr in range(grp):
                u0 = u[2 * pr] + lc
                u1 = u[2 * pr + 1] + lc
                m0, m1 = m_s[pr, 0], m_s[pr, 1]
                n0 = jnp.maximum(m0, jnp.max(u0, axis=-1, keepdims=True))
                n1 = jnp.maximum(m1, jnp.max(u1, axis=-1, keepdims=True))
                m_s[pr, 0], m_s[pr, 1] = n0, n1
                p = jnp.concatenate([jnp.exp(u0 - jnp.concatenate([n0, n0], axis=1)).astype(BF16),
                                     jnp.exp(u1 - jnp.concatenate([n1, n1], axis=1)).astype(BF16)], axis=1)
                pv = _dot(p, v_s[pr, j])
                alpha = jnp.where(first, jnp.exp(m0 - n0), jnp.exp(m1 - n1))
                acc_s[pr] = acc_s[pr] * alpha + pv[:, 0:128]
                l_s[pr] = l_s[pr] * alpha + pv[:, 128:256]

        lax.fori_loop(0, qi + 1, step, 0)
        for pr in range(grp):
            cols = slice(pr * 128, (pr + 1) * 128)
            y_ref[:, cols] = (acc_s[pr] / l_s[pr]).astype(BF16)
            lse_ref[:, cols] = jnp.where(first, m_s[pr, 0], m_s[pr, 1]) + jnp.log(l_s[pr])
        if n:
            pl.when((gi == ngrp - 1) & (qi == nq - 1))(pass_on)
            pl.when((gi == ngrp - 1) & (qi == nq - 1))(finish)

    out = pl.pallas_call(
        body, name="attn_fwd", grid=(ngrp, nq),
        in_specs=[pl.BlockSpec((t, wide), lambda g, i: (i, qcol + g)),
                  pl.BlockSpec((s, wide), lambda g, i: (0, kcol + g)),
                  pl.BlockSpec((s, wide), lambda g, i: (0, vcol + g)),
                  _full((nq, t, t)),
                  pl.BlockSpec((grp, 2, t, 128), lambda g, i: (g, 0, 0, 0)),
                  pl.BlockSpec((grp, 2, nq, 128), lambda g, i: (g, 0, 0, 0))] + [ANY] * n,
        out_specs=[pl.BlockSpec((t, wide), lambda g, i: (i, g)), pl.BlockSpec((t, wide), lambda g, i: (i, g))]
        + [ANY] * n,
        out_shape=[jax.ShapeDtypeStruct((s, D), BF16), jax.ShapeDtypeStruct((s, D), F32)]
        + [jax.ShapeDtypeStruct(a.shape, a.dtype) for a in _arrays(gathering)],
        input_output_aliases={6 + w: 2 + w for w in range(n)},
        scratch_shapes=[pltpu.VMEM((grp, 2, t, 128), BF16), pltpu.VMEM((grp, 2, nq, t, 128), BF16),
                        pltpu.VMEM((grp, nq, 2 * t, 256), BF16), pltpu.VMEM((grp, 2, t, 128), F32),
                        pltpu.VMEM((grp, t, 128), F32), pltpu.VMEM((grp, t, 128), F32)]
        + (_gather_sems(n) if n else []),
        compiler_params=_params("arbitrary", "arbitrary", communicates=bool(n)),
    )(z, z, z, logc, ka, kb, *_arrays(gathering))
    return out[0], out[1], out[2:]


def proj_merge(ya, yb, wa, wb, z, bg, gathering):
    s = ya.shape[0]
    tm = 512
    n = len(gathering)
    steps = s // tm

    def body(*refs):
        ya_ref, yb_ref, wa_ref, wb_ref, ga_ref, gb_ref, bg_ref = refs[:7]
        mg_ref, pa_ref, pb_ref = refs[7 + n:10 + n]
        i = pl.program_id(0)
        if n:
            send, pass_on, finish = _gather_phases(refs[10 + n:10 + 2 * n], *refs[10 + 2 * n:], _spans(gathering))
            pl.when(i == 0)(send)
            pl.when(i == steps - 1)(pass_on)
        pa = _dot(ya_ref[...], wa_ref[...])
        pb = _dot(yb_ref[...], wb_ref[...])
        sa = _sigmoid(ga_ref[...] + bg_ref[0:1, :])
        sb = _sigmoid(gb_ref[...] + bg_ref[1:2, :])
        mg_ref[...] = (sa * pa + sb * pb).astype(BF16)
        pa_ref[...] = pa.astype(BF16)
        pb_ref[...] = pb.astype(BF16)
        if n:
            pl.when(i == steps - 1)(finish)

    out = jax.ShapeDtypeStruct((s, D), BF16)
    res = pl.pallas_call(
        body, name="proj_merge", grid=(steps,),
        in_specs=[_rows(tm, D), _rows(tm, D), _full((D, D)), _full((D, D)),
                  _rows(tm, D, 5), _rows(tm, D, 6), _full((2, D))] + [ANY] * n,
        out_specs=[_rows(tm, D)] * 3 + [ANY] * n,
        out_shape=[out] * 3 + [jax.ShapeDtypeStruct(a.shape, a.dtype) for a in _arrays(gathering)],
        input_output_aliases={7 + w: 3 + w for w in range(n)},
        scratch_shapes=_gather_sems(n) if n else [],
        compiler_params=_params("arbitrary", communicates=bool(n)),
    )(ya, yb, wa, wb, z, z, bg, *_arrays(gathering))
    return res[0], res[1], res[2], res[3:]


def out_norm(merged, w_out, x, g_post, g_fpre, gathering):
    s = x.shape[0]
    tm = 512
    n = len(gathering)
    steps = s // tm

    def body(*refs):
        mg_ref, w_ref, x_ref, gp_ref, gf_ref = refs[:5]
        o_ref, x1_ref, h2_ref = refs[5 + n:8 + n]
        i = pl.program_id(0)
        if n:
            send, pass_on, finish = _gather_phases(refs[8 + n:8 + 2 * n], *refs[8 + 2 * n:], _spans(gathering))
            pl.when(i == 0)(send)
            pl.when(i == steps - 1)(pass_on)
        o = _dot(mg_ref[...], w_ref[...])
        ohat, _ = _rms(o)
        x1 = x_ref[...] + ohat * gp_ref[...]
        x1hat, _ = _rms(x1)
        o_ref[...] = o
        x1_ref[...] = x1
        h2_ref[...] = (x1hat * gf_ref[...]).astype(BF16)
        if n:
            pl.when(i == steps - 1)(finish)

    res = pl.pallas_call(
        body, name="out_norm", grid=(steps,),
        in_specs=[_rows(tm, D), _full((D, D)), _rows(tm, D), _full((1, D)), _full((1, D))] + [ANY] * n,
        out_specs=[_rows(tm, D)] * 3 + [ANY] * n,
        out_shape=[jax.ShapeDtypeStruct((s, D), F32), jax.ShapeDtypeStruct((s, D), F32),
                   jax.ShapeDtypeStruct((s, D), BF16)]
        + [jax.ShapeDtypeStruct(a.shape, a.dtype) for a in _arrays(gathering)],
        input_output_aliases={5 + w: 3 + w for w in range(n)},
        scratch_shapes=_gather_sems(n) if n else [],
        compiler_params=_params("arbitrary", communicates=bool(n)),
    )(merged, w_out, x, g_post, g_fpre, *_arrays(gathering))
    return res[0], res[1], res[2], res[3:]


def mm_ff1(h2, wg, gathering):
    s = h2.shape[0]
    tm = 1024
    n = len(gathering)
    ni = s // tm

    def body(*refs):
        a_ref, b_ref = refs[:2]
        o_ref, r_ref = refs[2 + n:4 + n]
        i, j = pl.program_id(0), pl.program_id(1)
        if n:
            send, pass_on, finish = _gather_phases(refs[4 + n:4 + 2 * n], *refs[4 + 2 * n:], _spans(gathering))
            pl.when((i == 0) & (j == 0))(send)
            pl.when((i == ni - 1) & (j == N_CHIPS // 2))(pass_on)
        a = _dot(a_ref[...], b_ref[...])
        o_ref[...] = a.astype(BF16)
        r = jnp.maximum(a, 0.0)
        r_ref[...] = (r * r).astype(BF16)
        if n:
            pl.when((i == ni - 1) & (j == N_CHIPS - 1))(finish)

    res = pl.pallas_call(
        body, name="mm_ff1", grid=(ni, N_CHIPS),
        in_specs=[pl.BlockSpec((tm, D), lambda i, j: (i, 0)), pl.BlockSpec((None, D, D), lambda i, j: (j, 0, 0))]
        + [ANY] * n,
        out_specs=[pl.BlockSpec((tm, D), lambda i, j: (i, j))] * 2 + [ANY] * n,
        out_shape=[jax.ShapeDtypeStruct((s, D_FF), BF16), jax.ShapeDtypeStruct((s, D_FF), BF16)]
        + [jax.ShapeDtypeStruct(a.shape, a.dtype) for a in _arrays(gathering)],
        input_output_aliases={2 + w: 2 + w for w in range(n)},
        scratch_shapes=_gather_sems(n) if n else [],
        compiler_params=_params("arbitrary", "arbitrary", communicates=bool(n)),
    )(h2, wg, *_arrays(gathering))
    return res[0], res[1], res[2:]


def ff2_loss(rl, w_ff2, x1, target, g_fpost):
    s = x1.shape[0]
    tm = 256

    def body(rl_ref, w_ref, x1_ref, t_ref, g_ref, dy_ref, df_ref, dg_ref, loss_ref):
        @pl.when(pl.program_id(0) == 0)
        def _():
            dg_ref[...] = jnp.zeros_like(dg_ref)
            loss_ref[...] = jnp.zeros_like(loss_ref)

        f = _dot(rl_ref[...], w_ref[...])
        fhat, r = _rms(f)
        err = x1_ref[...] + fhat * g_ref[...] - t_ref[...]
        loss_ref[...] += 0.5 * jnp.sum(jnp.mean(err * err, axis=-1, keepdims=True), axis=0, keepdims=True)
        dy = err * (1.0 / D)
        dy_ref[...] = dy
        dg_ref[...] += jnp.sum(dy * fhat, axis=0, keepdims=True)
        df_ref[...] = _rms_bwd(dy * g_ref[...], fhat, r).astype(BF16)

    return pl.pallas_call(
        body, name="ff2_loss", grid=(s // tm,),
        in_specs=[_rows(tm, D_FF), _full((D_FF, D)), _rows(tm, D), _rows(tm, D), _full((1, D))],
        out_specs=[_rows(tm, D), _rows(tm, D), _full((1, D)), _full((1, 1))],
        out_shape=[jax.ShapeDtypeStruct((s, D), F32), jax.ShapeDtypeStruct((s, D), BF16),
                   jax.ShapeDtypeStruct((1, D), F32), jax.ShapeDtypeStruct((1, 1), F32)],
        compiler_params=_params("arbitrary"),
    )(rl, w_ff2, x1, target, g_fpost)


def mm_tn(name, a, b, ta, tb, out_shape, out_spec):
    s = a.shape[0]

    def body(a_ref, b_ref, o_ref):
        o_ref[...] = _dot(a_ref[...], b_ref[...], TN)

    return pl.pallas_call(
        body, name=name, grid=(a.shape[1] // ta, b.shape[1] // tb),
        in_specs=[pl.BlockSpec((s, ta), lambda i, j: (0, i)), pl.BlockSpec((s, tb), lambda i, j: (0, j))],
        out_specs=out_spec, out_shape=jax.ShapeDtypeStruct(out_shape, F32),
        compiler_params=_params("parallel", "parallel"),
    )(a, b)


def mm_nt(name, a, w):
    s = a.shape[0]
    tm = 512

    def body(a_ref, w_ref, o_ref):
        o_ref[...] = _dot(a_ref[...], w_ref[...], NT).astype(BF16)

    return pl.pallas_call(
        body, name=name, grid=(s // tm,), in_specs=[_rows(tm, D), _full((D, D))], out_specs=_rows(tm, D),
        out_shape=jax.ShapeDtypeStruct((s, D), BF16), compiler_params=_params("parallel"),
    )(a, w)


def ff2_bwd(df, w_ff2, a):
    s = df.shape[0]
    tm = 1024

    def body(df_ref, w_ref, a_ref, da_ref):
        drl = _dot(df_ref[...], w_ref[...], NT)
        da_ref[...] = (drl * (2.0 * jnp.maximum(a_ref[...].astype(F32), 0.0))).astype(BF16)

    return pl.pallas_call(
        body, name="ff2_bwd", grid=(s // tm, D_FF // D),
        in_specs=[pl.BlockSpec((tm, D), lambda i, j: (i, 0)), pl.BlockSpec((D, D), lambda i, j: (j, 0)),
                  pl.BlockSpec((tm, D), lambda i, j: (i, j))],
        out_specs=pl.BlockSpec((tm, D), lambda i, j: (i, j)),
        out_shape=jax.ShapeDtypeStruct((s, D_FF), BF16), compiler_params=_params("parallel", "parallel"),
    )(df, w_ff2, a)


def ff1_bwd_norms(da, wg, x1, o, dy, g_fpre, g_post, swapping):
    s = x1.shape[0]
    tm = 256
    n = len(swapping)
    steps = s // tm

    def body(*refs):
        da_ref, w_ref, x1_ref, o_ref, dy_ref, gf_ref, gp_ref = refs[:7]
        dx1_ref, do_ref, dgf_ref, dgp_ref = refs[7 + n:11 + n]
        i = pl.program_id(0)
        if n:
            send, finish = _swap_phases(refs[7:7 + n], refs[11 + n:11 + 2 * n], *refs[11 + 2 * n:])
            pl.when(i == 0)(send)

        @pl.when(i == 0)
        def _():
            dgf_ref[...] = jnp.zeros_like(dgf_ref)
            dgp_ref[...] = jnp.zeros_like(dgp_ref)

        dh2 = _dot(da_ref[:, 0:D], w_ref[0], NT)
        for k in range(1, N_CHIPS):
            dh2 = dh2 + _dot(da_ref[:, k * D:(k + 1) * D], w_ref[k], NT)
        x1hat, r2 = _rms(x1_ref[...])
        dgf_ref[...] += jnp.sum(dh2 * x1hat, axis=0, keepdims=True)
        dx1 = dy_ref[...] + _rms_bwd(dh2 * gf_ref[...], x1hat, r2)
        ohat, r1 = _rms(o_ref[...])
        dgp_ref[...] += jnp.sum(dx1 * ohat, axis=0, keepdims=True)
        dx1_ref[...] = dx1
        do_ref[...] = _rms_bwd(dx1 * gp_ref[...], ohat, r1).astype(BF16)
        if n:
            pl.when(i == steps - 1)(finish)

    res = pl.pallas_call(
        body, name="ff1_bwd_norms", grid=(steps,),
        in_specs=[_rows(tm, D_FF), _full((N_CHIPS, D, D)), _rows(tm, D), _rows(tm, D), _rows(tm, D),
                  _full((1, D)), _full((1, D))] + [ANY] * n,
        out_specs=[_rows(tm, D), _rows(tm, D), _full((1, D)), _full((1, D))] + [ANY] * n,
        out_shape=[jax.ShapeDtypeStruct((s, D), F32), jax.ShapeDtypeStruct((s, D), BF16),
                   jax.ShapeDtypeStruct((1, D), F32), jax.ShapeDtypeStruct((1, D), F32)] + _swap_shapes(swapping),
        scratch_shapes=_swap_sems(n) if n else [],
        compiler_params=_params("arbitrary", communicates=bool(n)),
    )(da, wg, x1, o, dy, g_fpre, g_post, *swapping)
    return res[0], res[1], res[2], res[3], res[4:]


def out_bwd_gates(do, w_out, pa, pb, z, bg):
    s = do.shape[0]
    tm = 512

    def body(do_ref, w_ref, pa_ref, pb_ref, ga_ref, gb_ref, bg_ref, dpa_ref, dpb_ref, dga_ref, dgb_ref, dbg_ref):
        @pl.when(pl.program_id(0) == 0)
        def _():
            dbg_ref[...] = jnp.zeros_like(dbg_ref)

        dm = _dot(do_ref[...], w_ref[...], NT)
        sa = _sigmoid(ga_ref[...] + bg_ref[0:1, :])
        sb = _sigmoid(gb_ref[...] + bg_ref[1:2, :])
        dpa_ref[...] = (dm * sa).astype(BF16)
        dpb_ref[...] = (dm * sb).astype(BF16)
        dga = dm * pa_ref[...].astype(F32) * (sa * (1.0 - sa))
        dgb = dm * pb_ref[...].astype(F32) * (sb * (1.0 - sb))
        dga_ref[...] = dga.astype(BF16)
        dgb_ref[...] = dgb.astype(BF16)
        dbg_ref[0:1, :] += jnp.sum(dga, axis=0, keepdims=True)
        dbg_ref[1:2, :] += jnp.sum(dgb, axis=0, keepdims=True)

    out = jax.ShapeDtypeStruct((s, D), BF16)
    return pl.pallas_call(
        body, name="out_bwd_gates", grid=(s // tm,),
        in_specs=[_rows(tm, D), _full((D, D)), _rows(tm, D), _rows(tm, D), _rows(tm, D, 5), _rows(tm, D, 6),
                  _full((2, D))],
        out_specs=[_rows(tm, D)] * 4 + [_full((2, D))],
        out_shape=[out] * 4 + [jax.ShapeDtypeStruct((2, D), F32)], compiler_params=_params("arbitrary"),
    )(do, w_out, pa, pb, z, z, bg)


def gating_bwd(z, dya, ln_g, ln_b, w_s, bs_t, swapping):
    s = z.shape[0]
    ones = functools.partial(jnp.ones, (8, CHUNK), BF16)
    n = len(swapping)

    def body(*refs):
        u_ref, v_ref, dya_ref, lg_ref, lb_ref, ws_ref, bst_ref = refs[:7]
        du_ref, dv_ref, dws_ref, dbs_ref, dlg_ref, dlb_ref = refs[7 + n:13 + n]
        dvn_ref = refs[13 + 2 * n]
        ci = pl.program_id(0)
        if n:
            send, finish = _swap_phases(refs[7:7 + n], refs[13 + n:13 + 2 * n], *refs[14 + 2 * n:])
            pl.when(ci == 0)(send)

        @pl.when(ci == 0)
        def _():
            dws_ref[...] = jnp.zeros_like(dws_ref)
            dbs_ref[...] = jnp.zeros_like(dbs_ref)
            dlg_ref[...] = jnp.zeros_like(dlg_ref)
            dlb_ref[...] = jnp.zeros_like(dlb_ref)

        ug, dug_du = _gelu_and_grad(u_ref[...].astype(F32))
        vg, dvg_dv = _gelu_and_grad(v_ref[...].astype(F32))
        vhat, rstd = _layer_norm(vg)
        vn = (vhat * lg_ref[...] + lb_ref[...]).astype(BF16)
        dya = dya_ref[...].astype(F32)
        for g in range(GROUPS):
            cols = slice(g * CHUNK, (g + 1) * CHUNK)
            ws = _tril_ws(ws_ref, g)
            mixed = _dot(ws, vn[:, cols]) + bst_ref[:, g:g + 1]
            du_ref[:, cols] = (dya[:, cols] * mixed * dug_du[:, cols]).astype(BF16)
            dmix = (dya[:, cols] * ug[:, cols]).astype(BF16)
            dbs_ref[g] += _dot(ones(), dmix, NT)
            dws_ref[g] += _dot(dmix, vn[:, cols], NT)
            dvn_ref[:, cols] = _dot(ws, dmix, TN)
        dvn = dvn_ref[...]
        dlg_ref[...] += jnp.sum(dvn * vhat, axis=0, keepdims=True)
        dlb_ref[...] += jnp.sum(dvn, axis=0, keepdims=True)
        dvh = dvn * lg_ref[...]
        dvg = rstd * (dvh - jnp.mean(dvh, axis=-1, keepdims=True)
                      - vhat * jnp.mean(dvh * vhat, axis=-1, keepdims=True))
        dv_ref[...] = (dvg * dvg_dv).astype(BF16)

        @pl.when(ci == pl.num_programs(0) - 1)
        def _():
            r = lax.broadcasted_iota(jnp.int32, (CHUNK, CHUNK), 0)
            c = lax.broadcasted_iota(jnp.int32, (CHUNK, CHUNK), 1)
            for g in range(GROUPS):
                dws_ref[g] = jnp.where(c <= r, dws_ref[g], 0.0)

        if n:
            pl.when(ci == pl.num_programs(0) - 1)(finish)

    out = jax.ShapeDtypeStruct((s, D), BF16)
    res = pl.pallas_call(
        body, name="gating_bwd", grid=(s // CHUNK,),
        in_specs=[_rows(CHUNK, D, 0), _rows(CHUNK, D, 1), _rows(CHUNK, D), _full((1, D)), _full((1, D)),
                  _full((GROUPS, CHUNK, CHUNK)), _full((CHUNK, GROUPS))] + [ANY] * n,
        out_specs=[_rows(CHUNK, D), _rows(CHUNK, D), _full((GROUPS, CHUNK, CHUNK)), _full((GROUPS, 8, CHUNK)),
                   _full((1, D)), _full((1, D))] + [ANY] * n,
        out_shape=[out, out, jax.ShapeDtypeStruct((GROUPS, CHUNK, CHUNK), F32),
                   jax.ShapeDtypeStruct((GROUPS, 8, CHUNK), F32),
                   jax.ShapeDtypeStruct((1, D), F32), jax.ShapeDtypeStruct((1, D), F32)] + _swap_shapes(swapping),
        scratch_shapes=[pltpu.VMEM((CHUNK, D), F32)] + (_swap_sems(n) if n else []),
        compiler_params=_params("arbitrary", communicates=bool(n)),
    )(z, z, dya, ln_g, ln_b, w_s, bs_t, *swapping)
    return (*res[:6], res[6:])


def attn_bwd(z, yb, dyb, lse, logc, ka, kb, scattering, gathering=None):
    s = z.shape[0]
    nq = s // ATT_T
    t = ATT_T
    grp = ATT_BWD_GROUP
    ngrp = HEADS // 2 // grp
    wide = 128 * grp
    qcol, kcol, vcol = 2 * D // wide, 3 * D // wide, 4 * D // wide
    scale = 1.0 / math.sqrt(HEAD_DIM)
    n = len(scattering)
    g8 = 0 if gathering is None else 1

    def body(*refs):
        q_ref, k_ref, v_ref, y_ref, dy_ref, lse_ref, lc_ref, ka_ref, kb_ref = refs[:9]
        dq_ref, dk_ref, dv_ref = refs[9 + n + g8:12 + n + g8]
        qa_s, qt_s, da_s, dt_s, dq_s, dkt_s, dvt_s = refs[12 + 2 * n + 2 * g8:19 + 2 * n + 2 * g8]
        sems = refs[19 + 2 * n + 2 * g8:]
        gi, j = pl.program_id(0), pl.program_id(1)
        first, lane, ones = _head_masks()
        if n:
            send, finish = _scatter_phases(refs[9:9 + n], refs[12 + n + g8:12 + 2 * n + g8], *sems[:2])
            pl.when((gi == 0) & (j == 0))(send)
        if g8:
            send8, pass_on8, finish8 = _allgather8_phases(refs[12 + 2 * n + g8], *sems[2 * (n > 0):])
            pl.when((gi == 0) & (j == 0))(send8)
            pl.when((gi == ngrp - 1) & (j == nq - 1))(pass_on8)

        @pl.when(j == 0)
        def _():
            dq_s[...] = jnp.zeros_like(dq_s)
            for pr in range(grp):
                cols = slice(pr * 128, (pr + 1) * 128)
                for ib in range(nq):
                    rows = slice(ib * t, (ib + 1) * t)
                    q = q_ref[rows, cols].astype(F32) * scale
                    lse = lse_ref[rows, cols]
                    qa_s[pr, 0, ib] = jnp.where(first, q, _place3(lane, HEAD_DIM + 2 * AUG, _split3(-lse[:, 0:1]),
                                                                  ones(0, 2 * AUG))).astype(BF16)
                    qa_s[pr, 1, ib] = jnp.where(
                        first, _place3(lane, 2 * AUG, _split3(-lse[:, HEAD_DIM:HEAD_DIM + 1]), ones(1, 2 * AUG)),
                        q).astype(BF16)
                    qt_s[pr, ib, :, 0:t] = jnp.where(first, q, 0.0).T.astype(BF16)
                    qt_s[pr, ib, :, t:2 * t] = jnp.where(first, 0.0, q).T.astype(BF16)
                    do = dy_ref[rows, cols].astype(F32)
                    prod = do * y_ref[rows, cols].astype(F32)
                    dd0 = jnp.sum(jnp.where(first, prod, 0.0), axis=-1, keepdims=True)
                    dd1 = jnp.sum(jnp.where(first, 0.0, prod), axis=-1, keepdims=True)
                    da_s[pr, 0, ib] = jnp.where(first, do, _place3(lane, HEAD_DIM, _split3(-dd0), 0.0)).astype(BF16)
                    da_s[pr, 1, ib] = jnp.where(first, _place3(lane, 0, _split3(-dd1), 0.0), do).astype(BF16)
                    dt_s[pr, ib, :, 0:t] = jnp.where(first, do, 0.0).T.astype(BF16)
                    dt_s[pr, ib, :, t:2 * t] = jnp.where(first, 0.0, do).T.astype(BF16)

        keys = []
        for pr in range(grp):
            kj = k_ref[:, pr * 128:(pr + 1) * 128].astype(F32)
            vj = v_ref[:, pr * 128:(pr + 1) * 128].astype(F32)
            keys.append((
                jnp.where(first, kj, ka_ref[pr, 0] + kb_ref[pr, 0, pl.ds(j, 1), :]).astype(BF16),
                jnp.where(first, ka_ref[pr, 1] + kb_ref[pr, 1, pl.ds(j, 1), :], kj).astype(BF16),
                jnp.concatenate([jnp.where(first, kj, 0.0), jnp.where(first, 0.0, kj)], axis=0).astype(BF16),
                jnp.where(first, vj, ones(0, AUG)).astype(BF16),
                jnp.where(first, ones(1, AUG), vj).astype(BF16)))
        dkt_s[...] = jnp.zeros_like(dkt_s)
        dvt_s[...] = jnp.zeros_like(dvt_s)

        def step(i, _):
            lc = lc_ref[i - j]
            rows = pl.ds(pl.multiple_of(i * t, t), t)
            for pr in range(grp):
                k0a, k1a, kst, v0a, v1a = keys[pr]
                p0 = jnp.exp(_dot(qa_s[pr, 0, i], k0a, NT) + lc)
                p1 = jnp.exp(_dot(qa_s[pr, 1, i], k1a, NT) + lc)
                e0 = (p0 * _dot(da_s[pr, 0, i], v0a, NT)).astype(BF16)
                e1 = (p1 * _dot(da_s[pr, 1, i], v1a, NT)).astype(BF16)
                dq_s[pr, rows, :] += _dot(jnp.concatenate([e0, e1], axis=1), kst)
                dvt_s[pr] += _dot(dt_s[pr, i], jnp.concatenate([p0.astype(BF16), p1.astype(BF16)], axis=0))
                dkt_s[pr] += _dot(qt_s[pr, i], jnp.concatenate([e0, e1], axis=0))
            return 0

        lax.fori_loop(j, nq, step, 0)
        for pr in range(grp):
            dk_ref[:, pr * 128:(pr + 1) * 128] = dkt_s[pr].T.astype(BF16)
            dv_ref[:, pr * 128:(pr + 1) * 128] = dvt_s[pr].T.astype(BF16)

        @pl.when(j == nq - 1)
        def _():
            for pr in range(grp):
                dq_ref[:, pr * 128:(pr + 1) * 128] = (dq_s[pr] * scale).astype(BF16)

        if n:
            pl.when((gi == ngrp - 1) & (j == nq - 1))(finish)
        if g8:
            pl.when((gi == ngrp - 1) & (j == nq - 1))(finish8)

    colblock = lambda c: pl.BlockSpec((s, wide), lambda g, j: (0, c + g))
    once = lambda c: pl.BlockSpec((s, wide), lambda g, j: (0, c + g), pipeline_mode=pl.Buffered(1))
    blk = lambda c: pl.BlockSpec((t, wide), lambda g, j: (j, c + g))
    out = jax.ShapeDtypeStruct((s, D), BF16)
    res = pl.pallas_call(
        body, name="attn_bwd", grid=(ngrp, nq),
        in_specs=[once(qcol), blk(kcol), blk(vcol), once(0), once(0), once(0),
                  pl.BlockSpec((nq, t, t), lambda g, j: (0, 0, 0), pipeline_mode=pl.Buffered(1)),
                  pl.BlockSpec((grp, 2, t, 128), lambda g, j: (g, 0, 0, 0)),
                  pl.BlockSpec((grp, 2, nq, 128), lambda g, j: (g, 0, 0, 0))] + [ANY] * (n + g8),
        out_specs=[colblock(0), blk(0), blk(0)] + [ANY] * (n + g8),
        out_shape=[out] * 3 + _scatter_shapes(scattering)
        + ([jax.ShapeDtypeStruct(gathering.shape, gathering.dtype)] if g8 else []),
        input_output_aliases={9 + n: 3 + n} if g8 else {},
        scratch_shapes=[pltpu.VMEM((grp, 2, nq, t, 128), BF16), pltpu.VMEM((grp, nq, 128, 2 * t), BF16),
                        pltpu.VMEM((grp, 2, nq, t, 128), BF16), pltpu.VMEM((grp, nq, 128, 2 * t), BF16),
                        pltpu.VMEM((grp, s, 128), F32), pltpu.VMEM((grp, 128, t), F32),
                        pltpu.VMEM((grp, 128, t), F32)]
        + (_scatter_sems(n) if n else [])
        + ([pltpu.SemaphoreType.DMA((7,)), pltpu.SemaphoreType.DMA((7,))] if g8 else []),
        compiler_params=_params("arbitrary", "arbitrary", communicates=bool(n + g8)),
    )(z, z, z, yb, dyb, lse, logc, ka, kb, *scattering, *([gathering] if g8 else []))
    return res[0], res[1], res[2], res[3:3 + n], (res[3 + n] if g8 else None)


def in_bwd_norm(dz, wg, x, dx1, g_pre, scattering, gathering=None):
    s = x.shape[0]
    tm = 512
    n = len(scattering)
    g = 0 if gathering is None else 1
    last = (s // tm - 1, N_CHIPS - 1)

    def body(*refs):
        dz_ref, w_ref, x_ref, dx1_ref, g_ref = refs[:5]
        dx_ref, dg_ref = refs[5 + n + g:7 + n + g]
        acc_ref = refs[7 + 2 * n + 2 * g]
        sems = refs[8 + 2 * n + 2 * g:]
        i, k = pl.program_id(0), pl.program_id(1)
        if n:
            send, finish = _scatter_phases(refs[5:5 + n], refs[7 + n + g:7 + 2 * n + g], *sems[:2])
            pl.when((i == 0) & (k == 0))(send)
        if g:
            send8, pass_on8, finish8 = _allgather8_phases(refs[7 + 2 * n + g], *sems[2 * (n > 0):])
            pl.when((i == 0) & (k == 0))(send8)
            pl.when((i == last[0]) & (k == last[1]))(pass_on8)

        @pl.when((i == 0) & (k == 0))
        def _():
            dg_ref[...] = jnp.zeros_like(dg_ref)

        part = _dot(dz_ref[...], w_ref[...], NT)

        @pl.when(k == 0)
        def _():
            acc_ref[...] = part

        @pl.when(k > 0)
        def _():
            acc_ref[...] += part

        @pl.when(k == N_CHIPS - 1)
        def _():
            dh = acc_ref[...]
            xhat, r = _rms(x_ref[...])
            dg_ref[...] += jnp.sum(dh * xhat, axis=0, keepdims=True)
            dx_ref[...] = dx1_ref[...] + _rms_bwd(dh * g_ref[...], xhat, r)

        if n:
            pl.when((i == last[0]) & (k == last[1]))(finish)
        if g:
            pl.when((i == last[0]) & (k == last[1]))(finish8)

    row = pl.BlockSpec((tm, D), lambda i, k: (i, 0))
    vec = pl.BlockSpec((1, D), lambda i, k: (0, 0))
    res = pl.pallas_call(
        body, name="in_bwd_norm", grid=(s // tm, N_CHIPS),
        in_specs=[pl.BlockSpec((tm, IN_SHARD), lambda i, k: (i, k)),
                  pl.BlockSpec((None, D, IN_SHARD), lambda i, k: (k, 0, 0)), row, row, vec] + [ANY] * (n + g),
        out_specs=[row, vec] + [ANY] * (n + g),
        out_shape=[jax.ShapeDtypeStruct((s, D), F32), jax.ShapeDtypeStruct((1, D), F32)]
        + _scatter_shapes(scattering) + ([jax.ShapeDtypeStruct(gathering.shape, gathering.dtype)] if g else []),
        input_output_aliases={5 + n: 2 + n} if g else {},
        scratch_shapes=[pltpu.VMEM((tm, D), F32)] + (_scatter_sems(n) if n else [])
        + ([pltpu.SemaphoreType.DMA((7,)), pltpu.SemaphoreType.DMA((7,))] if g else []),
        compiler_params=_params("arbitrary", "arbitrary", communicates=bool(n + g)),
    )(dz, wg, x, dx1, g_pre, *scattering, *([gathering] if g else []))
    return res[0], res[1], res[2:2 + n], (res[2 + n] if g else None)


def _adamw_math(w, g, m, v):
    m = ADAM_B1 * m + (1.0 - ADAM_B1) * g
    v = ADAM_B2 * v + (1.0 - ADAM_B2) * (g * g)
    m_hat = m / (1.0 - ADAM_B1 ** ADAM_STEP)
    v_hat = v / (1.0 - ADAM_B2 ** ADAM_STEP)
    delta = -ADAM_LR * (m_hat / (jnp.sqrt(v_hat) + ADAM_EPS) + ADAM_WD * w)
    return delta, m, v


def adamw(name, w, g, m, v, tr):
    r, c = w.shape

    def body(w_ref, g_ref, m_ref, v_ref, go_ref, d_ref, nm_ref, nv_ref):
        g = g_ref[...]
        go_ref[...] = g
        d_ref[...], nm_ref[...], nv_ref[...] = _adamw_math(w_ref[...], g, m_ref[...], v_ref[...])

    out = jax.ShapeDtypeStruct((r, c), F32)
    return pl.pallas_call(
        body, name=name, grid=(r // tr,), in_specs=[_rows(tr, c)] * 4, out_specs=[_rows(tr, c)] * 4,
        out_shape=[out] * 4, compiler_params=_params("parallel"),
    )(w, g, m, v)


def _allgather8_phases(buf, send_sems, recv_sems):
    x, y, c, chips = _place()
    me = 2 * x + y
    sibling = (x, y, 1 - c)
    rows = buf.shape[1] // 2

    def part(chip, core):
        return buf.at[chip, pl.ds(core * rows, rows)]

    def copy(k, block, to):
        return pltpu.make_async_remote_copy(src_ref=block, dst_ref=block, send_sem=send_sems.at[k],
                                            recv_sem=recv_sems.at[k], device_id=to, device_id_type=MESH)

    def chip_of(j):
        return 2 * chips[j][0] + chips[j][1]

    def send():
        copy(0, part(me, c), sibling).start()
        for j in range(3):
            copy(1 + j, part(me, c), (chips[j][0], chips[j][1], c)).start()

    def pass_on():
        for j in range(3):
            copy(1 + j, part(chip_of(j), c), (chips[j][0], chips[j][1], c)).wait_recv()
            copy(4 + j, part(chip_of(j), c), sibling).start()

    def finish():
        copy(0, part(me, 1 - c), sibling).wait_recv()
        for j in range(3):
            copy(4 + j, part(chip_of(j), 1 - c), sibling).wait_recv()
        copy(0, part(me, c), sibling).wait_send()
        for j in range(3):
            copy(1 + j, part(me, c), (chips[j][0], chips[j][1], c)).wait_send()
            copy(4 + j, part(chip_of(j), c), sibling).wait_send()

    return send, pass_on, finish


def add_halves(name, g, recv, c_idx, tr):
    n, h, c = recv.shape

    def body(c_ref, g_ref, r_ref, o_ref):
        o_ref[...] = (g_ref[...] + r_ref[...]).astype(BF16)

    nb = h // tr
    return pl.pallas_call(
        body, name=name,
        grid_spec=pltpu.PrefetchScalarGridSpec(
            num_scalar_prefetch=1, grid=(n, nb),
            in_specs=[pl.BlockSpec((None, tr, c), lambda k, i, c_ref: (k, c_ref[0] * nb + i, 0)),
                      pl.BlockSpec((None, tr, c), lambda k, i, c_ref: (k, i, 0))],
            out_specs=pl.BlockSpec((None, tr, c), lambda k, i, c_ref: (k, i, 0))),
        out_shape=jax.ShapeDtypeStruct((n, h, c), BF16), compiler_params=_params("parallel", "parallel"),
    )(c_idx, g, recv)


def sum_chips(name, parts, recv, where, tr, after=None):
    n, h, c = recv.shape
    nb = h // tr

    def body(w_ref, p_ref, r_ref, *rest):
        acc = p_ref[...].astype(F32)
        for k in range(n):
            acc = acc + r_ref[k].astype(F32)
        rest[-1][...] = acc

    return pl.pallas_call(
        body, name=name,
        grid_spec=pltpu.PrefetchScalarGridSpec(
            num_scalar_prefetch=1, grid=(nb,),
            in_specs=[pl.BlockSpec((None, tr, c), lambda i, w_ref: (w_ref[0], i, 0)),
                      pl.BlockSpec((n, tr, c), lambda i, w_ref: (0, i, 0))] + ([ANY] if after is not None else []),
            out_specs=pl.BlockSpec((tr, c), lambda i, w_ref: (w_ref[1] * nb + i, 0))),
        out_shape=jax.ShapeDtypeStruct((2 * h, c), F32), compiler_params=_params("parallel"),
    )(where, parts, recv, *([after] if after is not None else []))


def place_shard(name, shard, where, dtype, tr):
    r, c = shard.shape

    def body(w_ref, s_ref, o_ref):
        o_ref[...] = s_ref[...].astype(dtype)

    return pl.pallas_call(
        body, name=name,
        grid_spec=pltpu.PrefetchScalarGridSpec(
            num_scalar_prefetch=1, grid=(r // tr,),
            in_specs=[pl.BlockSpec((tr, c), lambda i, w_ref: (i, 0))],
            out_specs=pl.BlockSpec((None, tr, c), lambda i, w_ref: (w_ref[0], i, 0))),
        out_shape=jax.ShapeDtypeStruct((N_CHIPS, r, c), dtype), compiler_params=_params("parallel"),
    )(where, shard)


ANY = pl.BlockSpec(memory_space=pl.ANY)


def _place():
    x, y, c = lax.axis_index("x"), lax.axis_index("y"), lax.axis_index("c")
    chips = [(1 - x, y), (x, 1 - y), (1 - x, 1 - y)]
    return x, y, c, chips


def gather_shards(arrays):
    n = len(arrays)

    def body(*refs):
        send, pass_on, finish = _gather_phases(refs[n:2 * n], *refs[2 * n:], _spans(arrays))
        send()
        pass_on()
        finish()

    return pl.pallas_call(
        body, name="gather_shards", in_specs=[ANY] * n, out_specs=[ANY] * n,
        out_shape=[jax.ShapeDtypeStruct(a.shape, a.dtype) for a in _arrays(arrays)],
        input_output_aliases={w: w for w in range(n)}, scratch_shapes=_gather_sems(n),
        compiler_params=pltpu.CompilerParams(has_side_effects=True),
    )(*_arrays(arrays))


def _gather_sems(n):
    return [pltpu.SemaphoreType.DMA((6 * n,)), pltpu.SemaphoreType.DMA((6 * n,))]


class Span(typing.NamedTuple):
    array: jax.Array
    lo: int
    hi: int
    ways: tuple = (0, 1, 2)


def _arrays(gathering):
    return [g.array if isinstance(g, Span) else g for g in gathering]


def _spans(gathering):
    return [(g.lo, g.hi, g.ways) if isinstance(g, Span) else (0, g.shape[1], (0, 1, 2)) for g in gathering]


def _gather_phases(out, send_sems, recv_sems, spans):
    n = len(out)
    if not any(ways for _, _, ways in spans):
        return (lambda: None,) * 3
    x, y, c, chips = _place()
    me = 2 * x + y
    sibling = (x, y, 1 - c)

    def half(w, chip, core):
        lo, hi, _ = spans[w]
        h = (hi - lo) // 2
        return out[w].at[chip, pl.ds(lo + core * h, h)]

    def copy(k, block, to):
        return pltpu.make_async_remote_copy(src_ref=block, dst_ref=block, send_sem=send_sems.at[k],
                                            recv_sem=recv_sems.at[k], device_id=to, device_id_type=MESH)

    def over_ici(w, j, chip):
        return copy(3 * w + j, half(w, chip, c), (chips[j][0], chips[j][1], c))

    def over_d2d(w, j, core):
        return copy(3 * n + 3 * w + j, half(w, 2 * chips[j][0] + chips[j][1], core), sibling)

    pairs = [(w, j) for w in range(n) for j in spans[w][2]]

    def send():
        for w, j in pairs:
            over_ici(w, j, me).start()

    def pass_on():
        for w, j in pairs:
            over_ici(w, j, 2 * chips[j][0] + chips[j][1]).wait_recv()
            over_d2d(w, j, c).start()

    def finish():
        for w, j in pairs:
            over_d2d(w, j, 1 - c).wait_recv()
        for w, j in pairs:
            over_ici(w, j, me).wait_send()
            over_d2d(w, j, c).wait_send()

    return send, pass_on, finish


def _relay_sems():
    return [pltpu.SemaphoreType.DMA((4,)), pltpu.SemaphoreType.DMA((4,))]


def _relay_phases(out, send_sems, recv_sems):
    x, y, c, chips = _place()
    sibling = (x, y, 1 - c)
    rows = out.shape[1]
    quarter = rows // 4
    far = 2 * chips[2][0] + chips[2][1]

    def piece(chip, way, core):
        return out.at[chip, pl.ds(way * (rows // 2) + core * quarter, quarter)]

    def copy(k, block, to):
        return pltpu.make_async_remote_copy(src_ref=block, dst_ref=block, send_sem=send_sems.at[k],
                                            recv_sem=recv_sems.at[k], device_id=to, device_id_type=MESH)

    def over_ici(way, chip):
        return copy(way, piece(chip, way, c), (chips[way][0], chips[way][1], c))

    def over_d2d(way, core):
        return copy(2 + way, piece(far, way, core), sibling)

    def send():
        for way in range(2):
            other = chips[1 - way]
            over_ici(way, 2 * other[0] + other[1]).start()

    def pass_on():
        for way in range(2):
            over_ici(way, far).wait_recv()
            over_d2d(way, c).start()

    def finish():
        for way in range(2):
            over_d2d(way, 1 - c).wait_recv()
        for way in range(2):
            other = chips[1 - way]
            over_ici(way, 2 * other[0] + other[1]).wait_send()
            over_d2d(way, c).wait_send()

    return send, pass_on, finish


def swap_halves(name, grads):
    n = len(grads)

    def body(*refs):
        send, finish = _swap_phases(refs[:n], refs[n:2 * n], *refs[2 * n:])
        send()
        finish()

    return pl.pallas_call(
        body, name=name, in_specs=[ANY] * n, out_specs=[ANY] * n, out_shape=_swap_shapes(grads),
        scratch_shapes=_swap_sems(n), compiler_params=pltpu.CompilerParams(has_side_effects=True),
    )(*grads)


def _swap_shapes(grads):
    return [jax.ShapeDtypeStruct((a.shape[0], a.shape[1] // 2, a.shape[2]), a.dtype) for a in grads]


def _swap_sems(n):
    return [pltpu.SemaphoreType.DMA((n,)), pltpu.SemaphoreType.DMA((n,))]


def _swap_phases(g, out, send_sems, recv_sems):
    x, y, c, _ = _place()

    def copies():
        return [pltpu.make_async_remote_copy(
            src_ref=g[w].at[:, pl.ds((1 - c) * (g[w].shape[1] // 2), g[w].shape[1] // 2)], dst_ref=out[w],
            send_sem=send_sems.at[w], recv_sem=recv_sems.at[w], device_id=(x, y, 1 - c), device_id_type=MESH)
            for w in range(len(g))]

    def send():
        for cp in copies():
            cp.start()

    def finish():
        for cp in copies():
            cp.wait()

    return send, finish


def _send_phases(g, out, send_sems, recv_sems):
    x, y, c, _ = _place()

    def copies():
        return [pltpu.make_async_remote_copy(
            src_ref=g[w], dst_ref=out[w], send_sem=send_sems.at[w], recv_sem=recv_sems.at[w],
            device_id=(x, y, 1 - c), device_id_type=MESH) for w in range(len(g))]

    def send():
        for cp in copies():
            cp.start()

    def finish():
        for cp in copies():
            cp.wait()

    return send, finish


def dw_in_half(name, h, dz, which, sending):
    s = h.shape[0]
    hh, tb = D // 2, IN_SHARD // 2
    n = len(sending)
    steps = IN_COLS // tb

    def body(w_ref, *refs):
        a_ref, b_ref, o_ref = refs[0], refs[1], refs[2 + n]
        j = pl.program_id(0)
        if n:
            send, finish = _send_phases(refs[2:2 + n], refs[3 + n:3 + 2 * n], *refs[3 + 2 * n:])
            pl.when(j == 0)(send)
        o_ref[...] = _dot(a_ref[...], b_ref[...], TN)
        if n:
            pl.when(j == steps - 1)(finish)

    out = pl.pallas_call(
        body, name=name,
        grid_spec=pltpu.PrefetchScalarGridSpec(
            num_scalar_prefetch=1, grid=(steps,),
            in_specs=[pl.BlockSpec((s, hh), lambda j, w: (0, w[0])), pl.BlockSpec((s, tb), lambda j, w: (0, j))]
            + [ANY] * n,
            out_specs=[pl.BlockSpec((None, hh, tb), lambda j, w: (j // 2, 0, j % 2))] + [ANY] * n,
            scratch_shapes=_swap_sems(n) if n else []),
        out_shape=[jax.ShapeDtypeStruct((N_CHIPS, hh, IN_SHARD), F32)]
        + [jax.ShapeDtypeStruct(a.shape, a.dtype) for a in sending],
        compiler_params=_params("arbitrary", communicates=bool(n)),
    )(which, h, dz, *sending)
    return out[0], out[1:]


def scatter_chips(parts):
    n = len(parts)

    def body(*refs):
        send, finish = _scatter_phases(refs[:n], refs[n:2 * n], *refs[2 * n:])
        send()
        finish()

    return pl.pallas_call(
        body, name="scatter_chips", in_specs=[ANY] * n, out_specs=[ANY] * n,
        out_shape=_scatter_shapes(parts), scratch_shapes=_scatter_sems(n),
        compiler_params=pltpu.CompilerParams(has_side_effects=True),
    )(*parts)


def _scatter_shapes(parts):
    return [jax.ShapeDtypeStruct((3,) + a.shape[1:], a.dtype) for a in parts]


def _scatter_sems(n):
    return [pltpu.SemaphoreType.DMA((3 * n,)), pltpu.SemaphoreType.DMA((3 * n,))]


def _scatter_phases(p, out, send_sems, recv_sems):
    x, y, c, chips = _place()

    def copies():
        return [pltpu.make_async_remote_copy(
            src_ref=p[w].at[2 * px + py], dst_ref=out[w].at[j], send_sem=send_sems.at[3 * w + j],
            recv_sem=recv_sems.at[3 * w + j], device_id=(px, py, c), device_id_type=MESH)
            for w in range(len(p)) for j, (px, py) in enumerate(chips)]

    def send():
        for cp in copies():
            cp.start()

    def finish():
        for cp in copies():
            cp.wait()

    return send, finish


def _join_only(arrays):
    n = len(arrays)

    def body(*refs):
        out = refs[n:2 * n]
        send_sems, recv_sems = refs[2 * n:]
        x, y, c, _ = _place()

        def copy(w, core):
            h = out[w].shape[0] // 2
            rows = out[w].at[pl.ds(core * h, h)]
            return pltpu.make_async_remote_copy(
                src_ref=rows, dst_ref=rows, send_sem=send_sems.at[w], recv_sem=recv_sems.at[w],
                device_id=(x, y, 1 - c), device_id_type=MESH)

        for w in range(n):
            copy(w, c).start()
        for w in range(n):
            copy(w, 1 - c).wait_recv()
        for w in range(n):
            copy(w, c).wait_send()

    return pl.pallas_call(
        body, name="join_only", in_specs=[ANY] * n, out_specs=[ANY] * n,
        out_shape=[jax.ShapeDtypeStruct(a.shape, a.dtype) for a in arrays],
        input_output_aliases={w: w for w in range(n)},
        scratch_shapes=[pltpu.SemaphoreType.DMA((n,)), pltpu.SemaphoreType.DMA((n,))],
        compiler_params=pltpu.CompilerParams(has_side_effects=True),
    )(*arrays)


HBM = pl.BlockSpec(memory_space=pltpu.HBM)
SEM = pl.BlockSpec(memory_space=pltpu.SEMAPHORE)
DATAFLOW = pltpu.SideEffectType.DATAFLOW_SIDE_EFFECTING


def _scatter_copies(p_ref, land_ref, send_sems, recv_sems):
    x, y, c, chips = _place()
    return [pltpu.make_async_remote_copy(
        src_ref=p_ref.at[2 * px + py], dst_ref=land_ref.at[j], send_sem=send_sems[j], recv_sem=recv_sems[j],
        device_id=(px, py, c), device_id_type=MESH) for j, (px, py) in enumerate(chips)]


def scatter_start(p):
    land = jax.ShapeDtypeStruct((3,) + p.shape[1:], p.dtype)

    def body(p_ref, land_ref, *outs):
        for cp in _scatter_copies(p_ref, land_ref, outs[0:3], outs[3:6]):
            cp.start()
        outs[8][...] = jnp.zeros_like(outs[8])

    return pl.pallas_call(
        body, name="scatter_start",
        out_shape=(pltpu.SemaphoreType.DMA(()),) * 6
        + (pltpu.HBM(p.shape, p.dtype), pltpu.HBM(land.shape, land.dtype), jax.ShapeDtypeStruct((8, 128), F32)),
        in_specs=(HBM, HBM), out_specs=(SEM,) * 6 + (HBM, HBM, pl.BlockSpec(memory_space=pltpu.VMEM)),
        input_output_aliases={0: 6, 1: 7},
        compiler_params=pltpu.CompilerParams(has_side_effects=DATAFLOW),
    )(pltpu.with_memory_space_constraint(p, pltpu.HBM),
      pltpu.with_memory_space_constraint(lax.empty(land.shape, land.dtype), pltpu.HBM))


def scatter_wait(started, after):
    sems, p_thru, land_thru = started[0:6], started[6], started[7]

    def body(p_ref, land_ref, *refs):
        for cp in _scatter_copies(p_ref, land_ref, refs[0:3], refs[3:6]):
            cp.wait_send()
            cp.wait_recv()

    return pl.pallas_call(
        body, name="scatter_wait",
        out_shape=(pltpu.HBM(p_thru.shape, p_thru.dtype), pltpu.HBM(land_thru.shape, land_thru.dtype)),
        in_specs=(HBM, HBM) + (SEM,) * 6 + (pl.BlockSpec(memory_space=pl.ANY),) * len(after), out_specs=(HBM, HBM),
        input_output_aliases={0: 0, 1: 1},
        compiler_params=pltpu.CompilerParams(has_side_effects=DATAFLOW),
    )(p_thru, land_thru, *sems, *after)


def _gather_leg_copies(buf_ref, leg, send_sems, recv_sems):
    x, y, c, chips = _place()
    h = buf_ref.shape[1] // 2
    copies = []
    for j, (px, py) in enumerate(chips):
        chip, to = (2 * x + y, (px, py, c)) if leg == "ici" else (2 * px + py, (x, y, 1 - c))
        block = buf_ref.at[chip, pl.ds(c * h, h)]
        copies.append(pltpu.make_async_remote_copy(
            src_ref=block, dst_ref=block, send_sem=send_sems[j], recv_sem=recv_sems[j],
            device_id=to, device_id_type=MESH))
    return copies


def gather_leg_start(name, buf, leg):
    def body(buf_ref, *outs):
        for cp in _gather_leg_copies(buf_ref, leg, outs[0:3], outs[3:6]):
            cp.start()
        outs[7][...] = jnp.zeros_like(outs[7])

    return pl.pallas_call(
        body, name=name,
        out_shape=(pltpu.SemaphoreType.DMA(()),) * 6
        + (pltpu.HBM(buf.shape, buf.dtype), jax.ShapeDtypeStruct((8, 128), F32)),
        in_specs=(HBM,), out_specs=(SEM,) * 6 + (HBM, pl.BlockSpec(memory_space=pltpu.VMEM)),
        input_output_aliases={0: 6},
        compiler_params=pltpu.CompilerParams(has_side_effects=DATAFLOW),
    )(pltpu.with_memory_space_constraint(buf, pltpu.HBM))


def gather_leg_wait(name, started, leg, after):
    def body(buf_ref, *refs):
        for cp in _gather_leg_copies(buf_ref, leg, refs[0:3], refs[3:6]):
            cp.wait_send()
            cp.wait_recv()

    buf = started[6]
    return pl.pallas_call(
        body, name=name, out_shape=pltpu.HBM(buf.shape, buf.dtype),
        in_specs=(HBM,) + (SEM,) * 6 + (pl.BlockSpec(memory_space=pl.ANY),) * len(after), out_specs=HBM,
        input_output_aliases={0: 0},
        compiler_params=pltpu.CompilerParams(has_side_effects=DATAFLOW),
    )(buf, *started[0:6], *after)


def gather_leg(name, buf, leg):
    def body(buf_ref, out_ref, send_sems, recv_sems):
        copies = _gather_leg_copies(out_ref, leg, [send_sems.at[j] for j in range(3)],
                                    [recv_sems.at[j] for j in range(3)])
        for cp in copies:
            cp.start()
        for cp in copies:
            cp.wait_send()
            cp.wait_recv()

    return pl.pallas_call(
        body, name=name, out_shape=pltpu.HBM(buf.shape, buf.dtype), in_specs=(HBM,), out_specs=HBM,
        input_output_aliases={0: 0},
        scratch_shapes=[pltpu.SemaphoreType.DMA((3,)), pltpu.SemaphoreType.DMA((3,))],
        compiler_params=pltpu.CompilerParams(has_side_effects=True),
    )(buf)


def join_halves(arrays, gathering=None):
    n = len(arrays)
    if gathering is None:
        return _join_only(arrays), None

    def body(*refs):
        out = refs[n + 1:2 * n + 1]
        send_sems, recv_sems = refs[2 * n + 2:2 * n + 4]
        send8, pass_on8, finish8 = _allgather8_phases(refs[2 * n + 1], *refs[2 * n + 4:])
        x, y, c, _ = _place()

        def copy(w, core):
            h = out[w].shape[0] // 2
            rows = out[w].at[pl.ds(core * h, h)]
            return pltpu.make_async_remote_copy(
                src_ref=rows, dst_ref=rows, send_sem=send_sems.at[w], recv_sem=recv_sems.at[w],
                device_id=(x, y, 1 - c), device_id_type=MESH)

        send8()
        for w in range(n):
            copy(w, c).start()
        pass_on8()
        for w in range(n):
            copy(w, 1 - c).wait_recv()
        finish8()
        for w in range(n):
            copy(w, c).wait_send()

    res = pl.pallas_call(
        body, name="join_halves", in_specs=[ANY] * (n + 1), out_specs=[ANY] * (n + 1),
        out_shape=[jax.ShapeDtypeStruct(a.shape, a.dtype) for a in list(arrays) + [gathering]],
        input_output_aliases={w: w for w in range(n + 1)},
        scratch_shapes=[pltpu.SemaphoreType.DMA((n,)), pltpu.SemaphoreType.DMA((n,)),
                        pltpu.SemaphoreType.DMA((7,)), pltpu.SemaphoreType.DMA((7,))],
        compiler_params=pltpu.CompilerParams(has_side_effects=True),
    )(*arrays, gathering)
    return res[:n], res[n]


def allreduce_small(packed):
    r, c = packed.shape
    n_dev = 8

    def body(x_ref, all_ref, sum_ref, send_sems, recv_sems, local_sem):
        x, y, cc, chips = _place()
        me, sibling = (x, y, cc), (x, y, 1 - cc)

        def rows(px, py, pc):
            return all_ref.at[4 * px + 2 * py + pc]

        def copy(k, block, to, src=None):
            return pltpu.make_async_remote_copy(
                src_ref=rows(*block) if src is None else src, dst_ref=rows(*block), send_sem=send_sems.at[k],
                recv_sem=recv_sems.at[k], device_id=to, device_id_type=MESH)

        mine = pltpu.make_async_copy(x_ref, rows(*me), local_sem)
        mine.start()
        first = [copy(0, me, sibling, src=x_ref)]
        first += [copy(1 + j, me, (*chip, cc), src=x_ref) for j, chip in enumerate(chips)]
        for cp in first:
            cp.start()
        passed = [copy(4 + j, (*chip, cc), sibling) for j, chip in enumerate(chips)]
        for j, chip in enumerate(chips):
            copy(1 + j, (*chip, cc), me).wait_recv()
            passed[j].start()
        copy(0, sibling, me).wait_recv()
        for j, chip in enumerate(chips):
            copy(4 + j, (*chip, 1 - cc), me).wait_recv()
        for cp in first + passed:
            cp.wait_send()
        mine.wait()
        acc = all_ref[0]
        for k in range(1, n_dev):
            acc = acc + all_ref[k]
        sum_ref[...] = acc

    vm = pl.BlockSpec(memory_space=pltpu.VMEM)
    return pl.pallas_call(
        body, name="allreduce_small", in_specs=[vm], out_specs=[vm, vm],
        out_shape=[jax.ShapeDtypeStruct((n_dev, r, c), F32), jax.ShapeDtypeStruct((r, c), F32)],
        scratch_shapes=[pltpu.SemaphoreType.DMA((7,)), pltpu.SemaphoreType.DMA((7,)), pltpu.SemaphoreType.DMA],
        compiler_params=pltpu.CompilerParams(has_side_effects=True, vmem_limit_bytes=VMEM_LIMIT),
    )(packed)[1]


def local_step(x, target, vecs, w_s, bs_t, bg, wg_in, late, core=None, order=None, where=None):
    on_mesh = core is not None

    def add(names, grads, recv):
        return [add_halves("add_" + n, g, r, core, min(r.shape[1], 256)) for n, g, r in zip(names, grads, recv)]

    g_pre, ln_g, ln_b, g_post, g_fpre, g_fpost = vecs
    s = x.shape[0]
    if order is None:
        order = jnp.arange(N_CHIPS, dtype=jnp.int32)
    logc = _attn_tables(s)
    ka, kb = _alibi_tables(s)

    h = norm_pre(x, g_pre)
    if not on_mesh:
        wg_a, wg_b, wg_out, wg_ff1, wg_ff2 = late
    if on_mesh:
        z, wg_in, (wg_a, wg_b, wg_out, wg_ff1, wg_ff2) = mm_in(h, wg_in, order, True, late)
        ya, (wg_b,) = gating_fwd(z, ln_g, ln_b, w_s, bs_t, [wg_b])
        yb, lse, (wg_a, wg_ff1, wg_out, bg) = attn_fwd(z, logc, ka, kb, [wg_a, wg_ff1, wg_out, bg])
        over_ici = gather_leg_start("ff2_ici_start", wg_ff2, "ici")
        bg = jnp.transpose(bg[:, :2, :], (1, 0, 2)).reshape(2, D) + over_ici[7][0:1, 0:1]
    else:
        z, _, _ = mm_in(h, wg_in, order, False)
        ya, _ = gating_fwd(z, ln_g, ln_b, w_s, bs_t, [])
        yb, lse, _ = attn_fwd(z, logc, ka, kb, [])
    merged, pa, pb, _ = proj_merge(ya, yb, wg_a.reshape(D, D), wg_b.reshape(D, D), z, bg, [])
    w_out = wg_out.reshape(D, D)
    o, x1, h2, _ = out_norm(merged, w_out, x, g_post, g_fpre, [])
    a, rl, _ = mm_ff1(h2, wg_ff1, [])
    if on_mesh:
        wg_ff2 = gather_leg("ff2_d2d", gather_leg_wait("ff2_ici_wait", over_ici, "ici", [rl]), "d2d")
    w_ff2 = wg_ff2.reshape(D_FF, D)
    dy, df, d_gfpost, loss = ff2_loss(rl, w_ff2, x1, target, g_fpost)

    half_cols = pl.BlockSpec((D, D // 2), lambda i, j: (0, j))
    d_wff2 = mm_tn("dw_ff2", rl, df, D // 2, D, (D_FF, D), pl.BlockSpec((D // 2, D), lambda i, j: (i, 0)))
    da = ff2_bwd(df, w_ff2, a)
    d_wff1 = mm_tn("dw_ff1", h2, da, D, D // 2, (N_CHIPS, D, D),
                   pl.BlockSpec((None, D, D // 2), lambda i, j: (j // 2, 0, j % 2)))
    d_ff = [d_wff1, d_wff2.reshape(N_CHIPS, D, D)]
    dx1, do, d_gfpre, d_gpost, recv_ff = ff1_bwd_norms(da, wg_ff1, x1, o, dy, g_fpre, g_post, d_ff if on_mesh else [])
    d_wout = mm_tn("dw_out", merged, do, D, D // 2, (D, D), half_cols)
    dpa, dpb, dga, dgb, d_bg = out_bwd_gates(do, w_out, pa, pb, z, bg)
    d_wa = mm_tn("dw_a", ya, dpa, D, D // 2, (D, D), half_cols)
    d_wb = mm_tn("dw_b", yb, dpb, D, D // 2, (D, D), half_cols)
    dya = mm_nt("dy_a", dpa, wg_a.reshape(D, D))
    dyb = mm_nt("dy_b", dpb, wg_b.reshape(D, D))
    d_proj = [d_wa.reshape(N_CHIPS, D // N_CHIPS, D), d_wb.reshape(N_CHIPS, D // N_CHIPS, D),
              d_wout.reshape(N_CHIPS, D // N_CHIPS, D)]
    du, dv, d_ws, d_bs, d_lng, d_lnb, recv_proj = gating_bwd(z, dya, ln_g, ln_b, w_s, bs_t, d_proj if on_mesh else [])
    early = d_proj + d_ff
    parts_early = add(BIG[1:], early, list(recv_proj) + list(recv_ff)) if on_mesh else []
    small = dict(b_gate=d_bg, ln_v_g=d_lng, ln_v_b=d_lnb, w_s=d_ws, b_s=d_bs[:, 0, :],
                 norm_mix_post=d_gpost, norm_ffn_pre=d_gfpre, norm_ffn_post=d_gfpost)
    packed = pack_small(dict(small, norm_mix_pre=jnp.zeros((1, D), F32)), loss, where) if on_mesh else None
    dq, dk, dvb, got_early, packed = attn_bwd(z, yb, dyb, lse, logc, ka, kb, parts_early, packed)
    dz = jnp.concatenate([du, dv, dq, dk, dvb, dga, dgb], axis=1)
    if on_mesh:
        for_sibling, _ = dw_in_half("dw_in_sibling", h, dz, 1 - core, [])
        mine, from_sibling = dw_in_half("dw_in_mine", h, dz, core, [for_sibling])
        d_win = None
        parts_late = [add_halves("add_w_in", mine, from_sibling[0], jnp.zeros((1,), jnp.int32), 256)]
    else:
        half = IN_SHARD // 2
        d_win = mm_tn("dw_in", h, dz, D, half, (N_CHIPS, D, IN_SHARD),
                      pl.BlockSpec((None, D, half), lambda i, j: (j // 2, 0, j % 2)))
        parts_late = []
    started = scatter_start(parts_late[0]) if on_mesh else None
    dx, d_gpre, _, _ = in_bwd_norm(dz, wg_in, x, dx1, g_pre + started[8][0:1, 0:1] if on_mesh else g_pre, [])
    small["norm_mix_pre"] = d_gpre
    return loss[0, 0], dx, [d_win] + early, small, parts_early, list(got_early), packed, started


BIG = ("w_in", "w_a_proj", "w_b_proj", "w_out", "w_ff1", "w_ff2")
SMALL = ("norm_mix_pre", "ln_v_g", "ln_v_b", "b_s", "norm_mix_post", "norm_ffn_pre", "norm_ffn_post", "w_s", "b_gate")
ORDER = ("norm_mix_pre", "w_in", "b_gate", "ln_v_g", "ln_v_b", "w_s", "b_s", "w_a_proj", "w_b_proj", "w_out",
         "norm_mix_post", "norm_ffn_pre", "w_ff1", "w_ff2", "norm_ffn_post")
VEC_ROWS = D // 128
WS_ROW = 7 * VEC_ROWS
BG_ROW = WS_ROW + GROUPS * CHUNK
LOSS_ROW = BG_ROW + 2 * VEC_ROWS
PACK_ROWS = LOSS_ROW + 8


def pack_small(small, loss, where):
    vectors = [small[n] for n in SMALL[:7]]
    operands = vectors + [small["w_s"], small["b_gate"], loss]

    def body(where_ref, *refs):
        out = refs[-1]
        ws_ref, bg_ref, loss_ref = refs[7:10]
        for i, n in enumerate(SMALL[:7]):
            if n == "b_s":
                out[i * VEC_ROWS:(i + 1) * VEC_ROWS, :] = refs[i][...]
            else:
                for j in range(VEC_ROWS):
                    out[i * VEC_ROWS + j:i * VEC_ROWS + j + 1, :] = refs[i][:, j * 128:(j + 1) * 128]
        for g in range(GROUPS):
            out[WS_ROW + g * CHUNK:WS_ROW + (g + 1) * CHUNK, :] = ws_ref[g]
        for r in range(2):
            for j in range(VEC_ROWS):
                row = BG_ROW + r * VEC_ROWS + j
                out[row:row + 1, :] = bg_ref[r:r + 1, j * 128:(j + 1) * 128]
        lane = lax.broadcasted_iota(jnp.int32, (8, 128), 1)
        sub = lax.broadcasted_iota(jnp.int32, (8, 128), 0)
        out[LOSS_ROW:LOSS_ROW + 8, :] = jnp.where((lane == 0) & (sub == 0), loss_ref[...], 0.0)

    return pl.pallas_call(
        body, name="pack_small",
        grid_spec=pltpu.PrefetchScalarGridSpec(
            num_scalar_prefetch=1, grid=(1,), in_specs=[_full(a.shape) for a in operands],
            out_specs=pl.BlockSpec((None, PACK_ROWS, 128), lambda i, w: (w[0], w[1], 0))),
        out_shape=jax.ShapeDtypeStruct((N_CHIPS, 2 * PACK_ROWS, 128), F32), compiler_params=_params("arbitrary"),
    )(where, *operands)


def pack_vector(vec, where):
    def body(where_ref, v_ref, out):
        for j in range(VEC_ROWS):
            out[j:j + 1, :] = v_ref[:, j * 128:(j + 1) * 128]

    return pl.pallas_call(
        body, name="pack_vector",
        grid_spec=pltpu.PrefetchScalarGridSpec(
            num_scalar_prefetch=1, grid=(1,), in_specs=[_full(vec.shape)],
            out_specs=pl.BlockSpec((None, VEC_ROWS, 128), lambda i, w: (w[0], w[1], 0))),
        out_shape=jax.ShapeDtypeStruct((N_CHIPS, 2 * VEC_ROWS, 128), F32), compiler_params=_params("arbitrary"),
    )(where, vec)


def adamw_small(gathered, first, chip, w, m, v):
    shapes = {n: (1, D) for n in SMALL}
    shapes.update(b_s=(GROUPS, CHUNK), w_s=(GROUPS * CHUNK, CHUNK), b_gate=(2, D // N_CHIPS))
    flat = lambda t: [t[n].reshape(shapes[n]) for n in SMALL]
    per = D // N_CHIPS // 128

    def body(chip_ref, all_ref, first_ref, *refs):
        params, outs = refs[:27], refs[27:]
        sub = lax.broadcasted_iota(jnp.int32, (VEC_ROWS, 128), 0)
        sum_ref = outs[36]
        total = all_ref[0, 0:PACK_ROWS, :]
        head = first_ref[0, 0:VEC_ROWS, :]
        for k in range(1, 2 * N_CHIPS):
            total = total + all_ref[k // 2, (k % 2) * PACK_ROWS:(k % 2 + 1) * PACK_ROWS, :]
            head = head + first_ref[k // 2, (k % 2) * VEC_ROWS:(k % 2 + 1) * VEC_ROWS, :]
        sum_ref[...] = total
        sum_ref[0:VEC_ROWS, :] = head

        def gate_row(r):
            rows = sum_ref[BG_ROW + r * VEC_ROWS:BG_ROW + (r + 1) * VEC_ROWS, :]
            return jnp.concatenate([jnp.sum(jnp.where(sub == per * chip_ref[0] + j, rows, 0.0), axis=0, keepdims=True)
                                    for j in range(per)], axis=1)

        for i, n in enumerate(SMALL):
            if n == "b_s":
                g = sum_ref[i * VEC_ROWS:(i + 1) * VEC_ROWS, :]
            elif n == "w_s":
                g = sum_ref[WS_ROW:BG_ROW, :]
            elif n == "b_gate":
                g = jnp.concatenate([gate_row(0), gate_row(1)], axis=0)
            else:
                g = jnp.concatenate([sum_ref[i * VEC_ROWS + j:i * VEC_ROWS + j + 1, :] for j in range(VEC_ROWS)],
                                    axis=1)
            delta, nm, nv = _adamw_math(params[i][...], g, params[9 + i][...], params[18 + i][...])
            outs[4 * i][...], outs[4 * i + 1][...], outs[4 * i + 2][...], outs[4 * i + 3][...] = g, delta, nm, nv

    vm = pl.BlockSpec(memory_space=pltpu.VMEM)
    res = pl.pallas_call(
        body, name="adamw_small",
        in_specs=[pl.BlockSpec(memory_space=pltpu.SMEM)] + [vm] * 29, out_specs=[vm] * 37,
        out_shape=[jax.ShapeDtypeStruct(shapes[n], F32) for n in SMALL for _ in range(4)]
        + [jax.ShapeDtypeStruct((PACK_ROWS, 128), F32)],
        compiler_params=_params(),
    )(chip, gathered, first, *flat(w), *flat(m), *flat(v))
    new = {n: tuple(r.reshape(w[n].shape) for r in res[4 * i:4 * i + 4]) for i, n in enumerate(SMALL)}
    return new, res[36][LOSS_ROW, 0]


def kernel(x, norm_mix_pre, w_in, b_gate, ln_v_g, ln_v_b, w_s, b_s, w_a_proj, w_b_proj, w_out, norm_mix_post, norm_ffn_pre, w_ff1, w_ff2, norm_ffn_post, loss_target, m_norm_mix_pre, m_w_in, m_b_gate, m_ln_v_g, m_ln_v_b, m_w_s, m_b_s, m_w_a_proj, m_w_b_proj, m_w_out, m_norm_mix_post, m_norm_ffn_pre, m_w_ff1, m_w_ff2, m_norm_ffn_post, v_norm_mix_pre, v_w_in, v_b_gate, v_ln_v_g, v_ln_v_b, v_w_s, v_b_s, v_w_a_proj, v_w_b_proj, v_w_out, v_norm_mix_post, v_norm_ffn_pre, v_w_ff1, v_w_ff2, v_norm_ffn_post):
    w = dict(norm_mix_pre=norm_mix_pre, w_in=w_in, b_gate=b_gate, ln_v_g=ln_v_g, ln_v_b=ln_v_b, w_s=w_s, b_s=b_s,
             w_a_proj=w_a_proj, w_b_proj=w_b_proj, w_out=w_out, norm_mix_post=norm_mix_post,
             norm_ffn_pre=norm_ffn_pre, w_ff1=w_ff1, w_ff2=w_ff2, norm_ffn_post=norm_ffn_post)
    m = dict(norm_mix_pre=m_norm_mix_pre, w_in=m_w_in, b_gate=m_b_gate, ln_v_g=m_ln_v_g, ln_v_b=m_ln_v_b, w_s=m_w_s,
             b_s=m_b_s, w_a_proj=m_w_a_proj, w_b_proj=m_w_b_proj, w_out=m_w_out, norm_mix_post=m_norm_mix_post,
             norm_ffn_pre=m_norm_ffn_pre, w_ff1=m_w_ff1, w_ff2=m_w_ff2, norm_ffn_post=m_norm_ffn_post)
    v = dict(norm_mix_pre=v_norm_mix_pre, w_in=v_w_in, b_gate=v_b_gate, ln_v_g=v_ln_v_g, ln_v_b=v_ln_v_b, w_s=v_w_s,
             b_s=v_b_s, w_a_proj=v_w_a_proj, w_b_proj=v_w_b_proj, w_out=v_w_out, norm_mix_post=v_norm_mix_post,
             norm_ffn_pre=v_norm_ffn_pre, w_ff1=v_w_ff1, w_ff2=v_w_ff2, norm_ffn_post=v_norm_ffn_post)
    chip = 2 * lax.axis_index("x") + lax.axis_index("y")
    core = lax.axis_index("c")

    where = jnp.stack([chip, core]).astype(jnp.int32)
    wg_in = place_shard("place_w_in", w_in[0], where, BF16, 256)
    bg_all = place_shard("place_b_gate", jnp.pad(b_gate[0], ((0, 14), (0, 0))), where, F32, 16)
    vecs = (norm_mix_pre, ln_v_g, ln_v_b, norm_mix_post, norm_ffn_pre, norm_ffn_post)
    loss, dx, _, small, parts, got, packed, started = local_step(
        x[0], loss_target[0], vecs, w_s[0], b_s[0].T, bg_all, wg_in, [w[n][0] for n in BIG[1:]],
        core=jnp.reshape(core, (1,)).astype(jnp.int32),
        order=jnp.stack([chip, chip ^ 2, chip ^ 1, chip ^ 3]).astype(jnp.int32), where=where)

    halves = [sum_chips("sum_" + n, p, r, where, min(p.shape[1], 256), started[8])
              for n, p, r in zip(BIG[1:], parts, got)]
    joined, _ = join_halves(halves)
    grads = dict(zip(BIG[1:], joined))
    new = {}

    def update(n):
        shape = w[n].shape
        res = adamw("adamw_" + n, w[n][0], grads[n], m[n][0], v[n][0], min(shape[1], 256))
        new[n] = tuple(r.reshape(shape) for r in res)

    for n in BIG[1:]:
        update(n)
    p_in, got_in = scatter_wait(started, [new["w_ff2"][1], dx])
    (grads["w_in"],), first = join_halves([sum_chips("sum_w_in", p_in, got_in, where, 256)],
                                          pack_vector(small["norm_mix_pre"], where))
    update("w_in")
    small_new, loss = adamw_small(packed, first, jnp.reshape(chip, (1,)).astype(jnp.int32), w, m, v)
    new.update(small_new)

    outs = [loss, dx[None]]
    for i in range(4):
        outs += [new[n][i] for n in ORDER]
    return tuple(outs)
```

```python
import functools
import math
import typing

import numpy as np
import jax
import jax.numpy as jnp
from jax import lax
from jax.experimental import pallas as pl
from jax.experimental.pallas import tpu as pltpu

F32 = jnp.float32
BF16 = jnp.bfloat16
MESH = pl.DeviceIdType.MESH

D = 1024
EPS = 1e-6
CHUNK = 128
GROUPS = 8
HEADS = 16
HEAD_DIM = 64
ATT_T = 256
ATT_GROUP = 8
ATT_BWD_GROUP = 4
N_CHIPS = 4
D_FF = 4 * D
IN_COLS = 7 * D
IN_SHARD = IN_COLS // N_CHIPS
MASKED = -1e30
VMEM_LIMIT = 56 * 2 ** 20

ADAM_LR, ADAM_B1, ADAM_B2, ADAM_EPS, ADAM_WD, ADAM_STEP = 0.001, 0.9, 0.999, 1e-08, 0.01, 10

NN = (((1,), (0,)), ((), ()))
NT = (((1,), (1,)), ((), ()))
TN = (((0,), (0,)), ((), ()))


def _dot(a, b, dims=NN):
    return lax.dot_general(a, b, dims, preferred_element_type=F32)


def _params(*sem, communicates=False):
    return pltpu.CompilerParams(dimension_semantics=sem or None, vmem_limit_bytes=VMEM_LIMIT,
                                has_side_effects=communicates)


def _rows(tr, c, col=0):
    return pl.BlockSpec((tr, c), lambda i: (i, col))


def _full(shape):
    n = len(shape)
    return pl.BlockSpec(shape, lambda *_: (0,) * n)


def _gelu(x):
    k = math.sqrt(2.0 / math.pi)
    return 0.5 * x * (1.0 + jnp.tanh(k * (x + 0.044715 * x * x * x)))


def _gelu_and_grad(x):
    k = math.sqrt(2.0 / math.pi)
    t = jnp.tanh(k * (x + 0.044715 * x * x * x))
    g = 0.5 * x * (1.0 + t)
    dg = 0.5 * (1.0 + t) + 0.5 * x * (1.0 - t * t) * (k * (1.0 + 3.0 * 0.044715 * x * x))
    return g, dg


def _sigmoid(x):
    return 1.0 / (1.0 + jnp.exp(-x))


def _rms(x):
    r = lax.rsqrt(jnp.mean(x * x, axis=-1, keepdims=True) + EPS)
    return x * r, r


def _rms_bwd(dn, xhat, r):
    return r * (dn - xhat * jnp.mean(dn * xhat, axis=-1, keepdims=True))


def norm_pre(x, g):
    s = x.shape[0]
    tr = 512

    def body(x_ref, g_ref, h_ref):
        xhat, _ = _rms(x_ref[...])
        h_ref[...] = (xhat * g_ref[...]).astype(BF16)

    return pl.pallas_call(
        body, name="norm_pre", grid=(s // tr,),
        in_specs=[_rows(tr, D), _full((1, D))], out_specs=_rows(tr, D),
        out_shape=jax.ShapeDtypeStruct((s, D), BF16), compiler_params=_params("parallel"),
    )(x, g)


def mm_in(h, wg, order, staged, casting=()):
    s = h.shape[0]
    tm, tn = 1024, IN_SHARD // 2
    per = IN_SHARD // tn
    m = len(casting)
    nj, ni = N_CHIPS * per, s // tm
    cast_steps = per * ni

    def body(order_ref, *refs):
        a_ref = refs[0]
        cast_in = refs[2:2 + m]
        o_ref, held = refs[2 + m], refs[3 + m]
        cast_out = refs[4 + m:4 + 2 * m]
        tile, tile_sem = refs[4 + 2 * m:6 + 2 * m]
        sems = refs[6 + 2 * m:]
        j, i = pl.program_id(0), pl.program_id(1)

        @pl.when(j * ni + i < cast_steps)
        def _():
            for src, dst in zip(cast_in, cast_out):
                dst[...] = src[...].astype(BF16)

        def fetch(t):
            chip = order_ref[t // per]
            return pltpu.make_async_copy(held.at[chip, :, pl.ds((t % per) * tn, tn)], tile.at[t % 2],
                                         tile_sem.at[t % 2])

        if staged:
            near = _gather_phases([held], *sems[:2], [(0, D, (0, 1))])
            far = _relay_phases(held, *sems[2:])

        @pl.when(i == 0)
        def _():
            @pl.when(j == 0)
            def _():
                if staged:
                    near[0]()
                fetch(0).start()

            fetch(j).wait()
            ahead = j + 1 < nj
            if staged:
                ahead = ahead & (j + 1 != per) & (j + 1 != 3 * per)
            pl.when(ahead)(lambda: fetch(j + 1).start())

        rows = pl.ds(pl.multiple_of(i * tm, tm), tm)
        o_ref[...] = _dot(a_ref[rows, :], tile[j % 2]).astype(BF16)

        if staged:
            @pl.when((i == ni - 1) & (j == per - 1))
            def _():
                near[1]()
                near[2]()
                far[0]()
                fetch(per).start()

            @pl.when((i == ni - 1) & (j == 3 * per - 1))
            def _():
                far[1]()
                far[2]()
                fetch(3 * per).start()

    def cast_block(j, i, o):
        return jnp.minimum(j * ni + i, cast_steps - 1)

    out = pl.pallas_call(
        body, name="mm_in",
        grid_spec=pltpu.PrefetchScalarGridSpec(
            num_scalar_prefetch=1, grid=(nj, ni),
            in_specs=[pl.BlockSpec((s, D), lambda j, i, o: (0, 0)), ANY]
            + [pl.BlockSpec((a.shape[0] // cast_steps, a.shape[1]), lambda j, i, o: (cast_block(j, i, o), 0))
               for a in casting],
            out_specs=[pl.BlockSpec((tm, tn), lambda j, i, o: (i, o[j // per] * per + j % per)), ANY]
            + [pl.BlockSpec((None, a.shape[0] // cast_steps, a.shape[1]),
                            lambda j, i, o: (o[0], cast_block(j, i, o), 0)) for a in casting],
            scratch_shapes=[pltpu.VMEM((2, D, tn), BF16), pltpu.SemaphoreType.DMA((2,))]
            + (_gather_sems(1) + _relay_sems() if staged else [])),
        out_shape=[jax.ShapeDtypeStruct((s, IN_COLS), BF16), jax.ShapeDtypeStruct(wg.shape, wg.dtype)]
        + [jax.ShapeDtypeStruct((N_CHIPS,) + a.shape, BF16) for a in casting],
        input_output_aliases={2: 1},
        compiler_params=_params("arbitrary", "arbitrary", communicates=staged),
    )(order, h, wg, *casting)
    return out[0], out[1], out[2:]


def _tril_ws(ws_ref, g):
    r = lax.broadcasted_iota(jnp.int32, (CHUNK, CHUNK), 0)
    c = lax.broadcasted_iota(jnp.int32, (CHUNK, CHUNK), 1)
    return jnp.where(c <= r, ws_ref[g], 0.0).astype(BF16)


def _layer_norm(v):
    mu = jnp.mean(v, axis=-1, keepdims=True)
    d = v - mu
    rstd = lax.rsqrt(jnp.mean(d * d, axis=-1, keepdims=True) + EPS)
    return d * rstd, rstd


def gating_fwd(z, ln_g, ln_b, w_s, bs_t, gathering):
    s = z.shape[0]
    n = len(gathering)
    steps = s // CHUNK

    def body(*refs):
        u_ref, v_ref, lg_ref, lb_ref, ws_ref, bst_ref = refs[:6]
        ya_ref = refs[6 + n]
        ci = pl.program_id(0)
        if n:
            send, pass_on, finish = _gather_phases(refs[7 + n:7 + 2 * n], *refs[7 + 2 * n:], _spans(gathering))
            pl.when(ci == 0)(send)
        ug = _gelu(u_ref[...].astype(F32))
        vhat, _ = _layer_norm(_gelu(v_ref[...].astype(F32)))
        vn = (vhat * lg_ref[...] + lb_ref[...]).astype(BF16)
        for g in range(GROUPS):
            cols = slice(g * CHUNK, (g + 1) * CHUNK)
            mixed = _dot(_tril_ws(ws_ref, g), vn[:, cols]) + bst_ref[:, g:g + 1]
            ya_ref[:, cols] = (ug[:, cols] * mixed).astype(BF16)
        if n:
            pl.when(ci == steps - 1)(pass_on)
            pl.when(ci == steps - 1)(finish)

    out = pl.pallas_call(
        body, name="gating_fwd", grid=(steps,),
        in_specs=[_rows(CHUNK, D, 0), _rows(CHUNK, D, 1), _full((1, D)), _full((1, D)),
                  _full((GROUPS, CHUNK, CHUNK)), _full((CHUNK, GROUPS))] + [ANY] * n,
        out_specs=[_rows(CHUNK, D)] + [ANY] * n,
        out_shape=[jax.ShapeDtypeStruct((s, D), BF16)]
        + [jax.ShapeDtypeStruct(a.shape, a.dtype) for a in _arrays(gathering)],
        input_output_aliases={6 + w: 1 + w for w in range(n)},
        scratch_shapes=_gather_sems(n) if n else [],
        compiler_params=_params("arbitrary", communicates=bool(n)),
    )(z, z, ln_g, ln_b, w_s, bs_t, *_arrays(gathering))
    return out[0], out[1:]


def _attn_tables(s):
    nd = s // ATT_T
    r = np.arange(ATT_T)[None, :, None]
    c = np.arange(ATT_T)[None, None, :]
    delta = np.arange(nd)[:, None, None] * ATT_T + r - c
    count = np.zeros(delta.shape, np.int64)
    for window, dilation in ((128, 1), (512, 4), (2048, 16)):
        count += (delta >= 0) & (delta % dilation == 0) & (delta <= window)
    logc = np.where(count > 0, np.log(np.maximum(count, 1)), MASKED)
    return jnp.asarray(logc, F32)


AUG = 3


def _split3_np(x):
    terms, rest = [], np.asarray(x, np.float64)
    for _ in range(AUG):
        term = np.asarray(rest.astype(jnp.bfloat16), np.float64)
        terms.append(term)
        rest = rest - term
    return terms


def _split3(x):
    terms, rest = [], x
    for _ in range(AUG):
        term = rest.astype(BF16).astype(F32)
        terms.append(term)
        rest = rest - term
    return terms


def _alibi_tables(s):
    nb = s // ATT_T
    slopes = np.exp2(-8.0 * np.arange(1, HEADS + 1, dtype=np.float64) / HEADS)
    ka = np.zeros((HEADS // 2, 2, ATT_T, 128), np.float32)
    kb = np.zeros((HEADS // 2, 2, nb, 128), np.float32)
    for p in range(HEADS // 2):
        for e in range(2):
            base = HEAD_DIM * (1 - e)
            for a, term in enumerate(_split3_np(slopes[2 * p + e] * np.arange(ATT_T))):
                ka[p, e, :, base + a] = term
            for a, term in enumerate(_split3_np(slopes[2 * p + e] * ATT_T * np.arange(nb))):
                kb[p, e, :, base + AUG + a] = term
            ka[p, e, :, base + 2 * AUG:base + 3 * AUG] = 1.0
    return jnp.asarray(ka), jnp.asarray(kb)


def _head_masks():
    lane = lax.broadcasted_iota(jnp.int32, (1, 128), 1)
    first = lane < HEAD_DIM

    def ones(e, n):
        base = HEAD_DIM * (1 - e)
        return ((lane >= base) & (lane < base + n)).astype(F32)

    return first, lane, ones


def _place3(lane, at, terms, other):
    for a, term in enumerate(terms):
        other = jnp.where(lane == at + a, term, other)
    return other


def attn_fwd(z, logc, ka, kb, gathering):
    s = z.shape[0]
    nq = s // ATT_T
    t = ATT_T
    n = len(gathering)
    grp = ATT_GROUP
    ngrp = HEADS // 2 // grp
    wide = 128 * grp
    qcol, kcol, vcol = 2 * D // wide, 3 * D // wide, 4 * D // wide

    def body(*refs):
        q_ref, k_ref, v_ref, lc_ref, ka_ref, kb_ref = refs[:6]
        y_ref, lse_ref = refs[6 + n:8 + n]
        q_s, k_s, v_s, m_s, l_s, acc_s = refs[8 + 2 * n:14 + 2 * n]
        gi, qi = pl.program_id(0), pl.program_id(1)
        first, lane, ones = _head_masks()
        if n:
            send, pass_on, finish = _gather_phases(refs[8 + n:8 + 2 * n], *refs[14 + 2 * n:], _spans(gathering))
            pl.when((gi == 0) & (qi == 0))(send)

        @pl.when(qi == 0)
        def _():
            sel = jnp.broadcast_to(first.astype(F32), (t, 128))
            for pr in range(grp):
                cols = slice(pr * 128, (pr + 1) * 128)
                for jb in range(nq):
                    kj = k_ref[jb * t:(jb + 1) * t, cols].astype(F32)
                    vj = v_ref[jb * t:(jb + 1) * t, cols].astype(F32)
                    k_s[pr, 0, jb] = jnp.where(first, kj, ka_ref[pr, 0] + kb_ref[pr, 0, jb:jb + 1, :]).astype(BF16)
                    k_s[pr, 1, jb] = jnp.where(first, ka_ref[pr, 1] + kb_ref[pr, 1, jb:jb + 1, :], kj).astype(BF16)
                    v_s[pr, jb, 0:t, 0:128] = jnp.where(first, vj, 0.0).astype(BF16)
                    v_s[pr, jb, t:2 * t, 0:128] = jnp.where(first, 0.0, vj).astype(BF16)
                    v_s[pr, jb, 0:t, 128:256] = sel.astype(BF16)
                    v_s[pr, jb, t:2 * t, 128:256] = (1.0 - sel).astype(BF16)

        for pr in range(grp):
            q = q_ref[:, pr * 128:(pr + 1) * 128].astype(F32) * (1.0 / math.sqrt(HEAD_DIM))
            q_s[pr, 0] = jnp.where(first, q, ones(0, 2 * AUG)).astype(BF16)
            q_s[pr, 1] = jnp.where(first, ones(1, 2 * AUG), q).astype(BF16)
        m_s[...] = jnp.full_like(m_s, MASKED)
        l_s[...] = jnp.zeros_like(l_s)
        acc_s[...] = jnp.zeros_like(acc_s)

        def scores(j):
            return tuple(_dot(q_s[pr, e], k_s[pr, e, j], NT) for pr in range(grp) for e in range(2))

        def step(j, carry):
            softmax_block(j, scores(j))
            return carry

        def softmax_block(j, u):
            lc = lc_ref[qi - j]
            for pr in range(grp):
                u0 = u[2 * pr] + lc
                u1 = u[2 * pr + 1] + lc
                m0, m1 = m_s[pr, 0], m_s[pr, 1]
                n0 = jnp.maximum(m0, jnp.max(u0, axis=-1, keepdims=True))
                n1 = jnp.maximum(m1, jnp.max(u1, axis=-1, keepdims=True))
                m_s[pr, 0], m_s[pr, 1] = n0, n1
                p = jnp.concatenate([jnp.exp(u0 - jnp.concatenate([n0, n0], axis=1)).astype(BF16),
                                     jnp.exp(u1 - jnp.concatenate([n1, n1], axis=1)).astype(BF16)], axis=1)
                pv = _dot(p, v_s[pr, j])
                alpha = jnp.where(first, jnp.exp(m0 - n0), jnp.exp(m1 - n1))
                acc_s[pr] = acc_s[pr] * alpha + pv[:, 0:128]
                l_s[pr] = l_s[pr] * alpha + pv[:, 128:256]

        lax.fori_loop(0, qi + 1, step, 0)
        for pr in range(grp):
            cols = slice(pr * 128, (pr + 1) * 128)
            y_ref[:, cols] = (acc_s[pr] / l_s[pr]).astype(BF16)
            lse_ref[:, cols] = jnp.where(first, m_s[pr, 0], m_s[pr, 1]) + jnp.log(l_s[pr])
        if n:
            pl.when((gi == ngrp - 1) & (qi == nq - 1))(pass_on)
            pl.when((gi == ngrp - 1) & (qi == nq - 1))(finish)

    out = pl.pallas_call(
        body, name="attn_fwd", grid=(ngrp, nq),
        in_specs=[pl.BlockSpec((t, wide), lambda g, i: (i, qcol + g)),
                  pl.BlockSpec((s, wide), lambda g, i: (0, kcol + g)),
                  pl.BlockSpec((s, wide), lambda g, i: (0, vcol + g)),
                  _full((nq, t, t)),
                  pl.BlockSpec((grp, 2, t, 128), lambda g, i: (g, 0, 0, 0)),
                  pl.BlockSpec((grp, 2, nq, 128), lambda g, i: (g, 0, 0, 0))] + [ANY] * n,
        out_specs=[pl.BlockSpec((t, wide), lambda g, i: (i, g)), pl.BlockSpec((t, wide), lambda g, i: (i, g))]
        + [ANY] * n,
        out_shape=[jax.ShapeDtypeStruct((s, D), BF16), jax.ShapeDtypeStruct((s, D), F32)]
        + [jax.ShapeDtypeStruct(a.shape, a.dtype) for a in _arrays(gathering)],
        input_output_aliases={6 + w: 2 + w for w in range(n)},
        scratch_shapes=[pltpu.VMEM((grp, 2, t, 128), BF16), pltpu.VMEM((grp, 2, nq, t, 128), BF16),
                        pltpu.VMEM((grp, nq, 2 * t, 256), BF16), pltpu.VMEM((grp, 2, t, 128), F32),
                        pltpu.VMEM((grp, t, 128), F32), pltpu.VMEM((grp, t, 128), F32)]
        + (_gather_sems(n) if n else []),
        compiler_params=_params("arbitrary", "arbitrary", communicates=bool(n)),
    )(z, z, z, logc, ka, kb, *_arrays(gathering))
    return out[0], out[1], out[2:]


def proj_merge(ya, yb, wa, wb, z, bg, gathering):
    s = ya.shape[0]
    tm = 512
    n = len(gathering)
    steps = s // tm

    def body(*refs):
        ya_ref, yb_ref, wa_ref, wb_ref, ga_ref, gb_ref, bg_ref = refs[:7]
        mg_ref, pa_ref, pb_ref = refs[7 + n:10 + n]
        i = pl.program_id(0)
        if n:
            send, pass_on, finish = _gather_phases(refs[10 + n:10 + 2 * n], *refs[10 + 2 * n:], _spans(gathering))
            pl.when(i == 0)(send)
            pl.when(i == steps - 1)(pass_on)
        pa = _dot(ya_ref[...], wa_ref[...])
        pb = _dot(yb_ref[...], wb_ref[...])
        sa = _sigmoid(ga_ref[...] + bg_ref[0:1, :])
        sb = _sigmoid(gb_ref[...] + bg_ref[1:2, :])
        mg_ref[...] = (sa * pa + sb * pb).astype(BF16)
        pa_ref[...] = pa.astype(BF16)
        pb_ref[...] = pb.astype(BF16)
        if n:
            pl.when(i == steps - 1)(finish)

    out = jax.ShapeDtypeStruct((s, D), BF16)
    res = pl.pallas_call(
        body, name="proj_merge", grid=(steps,),
        in_specs=[_rows(tm, D), _rows(tm, D), _full((D, D)), _full((D, D)),
                  _rows(tm, D, 5), _rows(tm, D, 6), _full((2, D))] + [ANY] * n,
        out_specs=[_rows(tm, D)] * 3 + [ANY] * n,
        out_shape=[out] * 3 + [jax.ShapeDtypeStruct(a.shape, a.dtype) for a in _arrays(gathering)],
        input_output_aliases={7 + w: 3 + w for w in range(n)},
        scratch_shapes=_gather_sems(n) if n else [],
        compiler_params=_params("arbitrary", communicates=bool(n)),
    )(ya, yb, wa, wb, z, z, bg, *_arrays(gathering))
    return res[0], res[1], res[2], res[3:]


def out_norm(merged, w_out, x, g_post, g_fpre, gathering):
    s = x.shape[0]
    tm = 512
    n = len(gathering)
    steps = s // tm

    def body(*refs):
        mg_ref, w_ref, x_ref, gp_ref, gf_ref = refs[:5]
        o_ref, x1_ref, h2_ref = refs[5 + n:8 + n]
        i = pl.program_id(0)
        if n:
            send, pass_on, finish = _gather_phases(refs[8 + n:8 + 2 * n], *refs[8 + 2 * n:], _spans(gathering))
            pl.when(i == 0)(send)
            pl.when(i == steps - 1)(pass_on)
        o = _dot(mg_ref[...], w_ref[...])
        ohat, _ = _rms(o)
        x1 = x_ref[...] + ohat * gp_ref[...]
        x1hat, _ = _rms(x1)
        o_ref[...] = o
        x1_ref[...] = x1
        h2_ref[...] = (x1hat * gf_ref[...]).astype(BF16)
        if n:
            pl.when(i == steps - 1)(finish)

    res = pl.pallas_call(
        body, name="out_norm", grid=(steps,),
        in_specs=[_rows(tm, D), _full((D, D)), _rows(tm, D), _full((1, D)), _full((1, D))] + [ANY] * n,
        out_specs=[_rows(tm, D)] * 3 + [ANY] * n,
        out_shape=[jax.ShapeDtypeStruct((s, D), F32), jax.ShapeDtypeStruct((s, D), F32),
                   jax.ShapeDtypeStruct((s, D), BF16)]
        + [jax.ShapeDtypeStruct(a.shape, a.dtype) for a in _arrays(gathering)],
        input_output_aliases={5 + w: 3 + w for w in range(n)},
        scratch_shapes=_gather_sems(n) if n else [],
        compiler_params=_params("arbitrary", communicates=bool(n)),
    )(merged, w_out, x, g_post, g_fpre, *_arrays(gathering))
    return res[0], res[1], res[2], res[3:]


def mm_ff1(h2, wg, gathering):
    s = h2.shape[0]
    tm = 1024
    n = len(gathering)
    ni = s // tm

    def body(*refs):
        a_ref, b_ref = refs[:2]
        o_ref, r_ref = refs[2 + n:4 + n]
        i, j = pl.program_id(0), pl.program_id(1)
        if n:
            send, pass_on, finish = _gather_phases(refs[4 + n:4 + 2 * n], *refs[4 + 2 * n:], _spans(gathering))
            pl.when((i == 0) & (j == 0))(send)
            pl.when((i == ni - 1) & (j == N_CHIPS // 2))(pass_on)
        a = _dot(a_ref[...], b_ref[...])
        o_ref[...] = a.astype(BF16)
        r = jnp.maximum(a, 0.0)
        r_ref[...] = (r * r).astype(BF16)
        if n:
            pl.when((i == ni - 1) & (j == N_CHIPS - 1))(finish)

    res = pl.pallas_call(
        body, name="mm_ff1", grid=(ni, N_CHIPS),
        in_specs=[pl.BlockSpec((tm, D), lambda i, j: (i, 0)), pl.BlockSpec((None, D, D), lambda i, j: (j, 0, 0))]
        + [ANY] * n,
        out_specs=[pl.BlockSpec((tm, D), lambda i, j: (i, j))] * 2 + [ANY] * n,
        out_shape=[jax.ShapeDtypeStruct((s, D_FF), BF16), jax.ShapeDtypeStruct((s, D_FF), BF16)]
        + [jax.ShapeDtypeStruct(a.shape, a.dtype) for a in _arrays(gathering)],
        input_output_aliases={2 + w: 2 + w for w in range(n)},
        scratch_shapes=_gather_sems(n) if n else [],
        compiler_params=_params("arbitrary", "arbitrary", communicates=bool(n)),
    )(h2, wg, *_arrays(gathering))
    return res[0], res[1], res[2:]


def ff2_loss(rl, w_ff2, x1, target, g_fpost):
    s = x1.shape[0]
    tm = 256

    def body(rl_ref, w_ref, x1_ref, t_ref, g_ref, dy_ref, df_ref, dg_ref, loss_ref):
        @pl.when(pl.program_id(0) == 0)
        def _():
            dg_ref[...] = jnp.zeros_like(dg_ref)
            loss_ref[...] = jnp.zeros_like(loss_ref)

        f = _dot(rl_ref[...], w_ref[...])
        fhat, r = _rms(f)
        err = x1_ref[...] + fhat * g_ref[...] - t_ref[...]
        loss_ref[...] += 0.5 * jnp.sum(jnp.mean(err * err, axis=-1, keepdims=True), axis=0, keepdims=True)
        dy = err * (1.0 / D)
        dy_ref[...] = dy
        dg_ref[...] += jnp.sum(dy * fhat, axis=0, keepdims=True)
        df_ref[...] = _rms_bwd(dy * g_ref[...], fhat, r).astype(BF16)

    return pl.pallas_call(
        body, name="ff2_loss", grid=(s // tm,),
        in_specs=[_rows(tm, D_FF), _full((D_FF, D)), _rows(tm, D), _rows(tm, D), _full((1, D))],
        out_specs=[_rows(tm, D), _rows(tm, D), _full((1, D)), _full((1, 1))],
        out_shape=[jax.ShapeDtypeStruct((s, D), F32), jax.ShapeDtypeStruct((s, D), BF16),
                   jax.ShapeDtypeStruct((1, D), F32), jax.ShapeDtypeStruct((1, 1), F32)],
        compiler_params=_params("arbitrary"),
    )(rl, w_ff2, x1, target, g_fpost)


def mm_tn(name, a, b, ta, tb, out_shape, out_spec):
    s = a.shape[0]

    def body(a_ref, b_ref, o_ref):
        o_ref[...] = _dot(a_ref[...], b_ref[...], TN)

    return pl.pallas_call(
        body, name=name, grid=(a.shape[1] // ta, b.shape[1] // tb),
        in_specs=[pl.BlockSpec((s, ta), lambda i, j: (0, i)), pl.BlockSpec((s, tb), lambda i, j: (0, j))],
        out_specs=out_spec, out_shape=jax.ShapeDtypeStruct(out_shape, F32),
        compiler_params=_params("parallel", "parallel"),
    )(a, b)


def mm_nt(name, a, w):
    s = a.shape[0]
    tm = 512

    def body(a_ref, w_ref, o_ref):
        o_ref[...] = _dot(a_ref[...], w_ref[...], NT).astype(BF16)

    return pl.pallas_call(
        body, name=name, grid=(s // tm,), in_specs=[_rows(tm, D), _full((D, D))], out_specs=_rows(tm, D),
        out_shape=jax.ShapeDtypeStruct((s, D), BF16), compiler_params=_params("parallel"),
    )(a, w)


def ff2_bwd(df, w_ff2, a):
    s = df.shape[0]
    tm = 1024

    def body(df_ref, w_ref, a_ref, da_ref):
        drl = _dot(df_ref[...], w_ref[...], NT)
        da_ref[...] = (drl * (2.0 * jnp.maximum(a_ref[...].astype(F32), 0.0))).astype(BF16)

    return pl.pallas_call(
        body, name="ff2_bwd", grid=(s // tm, D_FF // D),
        in_specs=[pl.BlockSpec((tm, D), lambda i, j: (i, 0)), pl.BlockSpec((D, D), lambda i, j: (j, 0)),
                  pl.BlockSpec((tm, D), lambda i, j: (i, j))],
        out_specs=pl.BlockSpec((tm, D), lambda i, j: (i, j)),
        out_shape=jax.ShapeDtypeStruct((s, D_FF), BF16), compiler_params=_params("parallel", "parallel"),
    )(df, w_ff2, a)


def ff1_bwd_norms(da, wg, x1, o, dy, g_fpre, g_post, swapping):
    s = x1.shape[0]
    tm = 256
    n = len(swapping)
    steps = s // tm

    def body(*refs):
        da_ref, w_ref, x1_ref, o_ref, dy_ref, gf_ref, gp_ref = refs[:7]
        dx1_ref, do_ref, dgf_ref, dgp_ref = refs[7 + n:11 + n]
        i = pl.program_id(0)
        if n:
            send, finish = _swap_phases(refs[7:7 + n], refs[11 + n:11 + 2 * n], *refs[11 + 2 * n:])
            pl.when(i == 0)(send)

        @pl.when(i == 0)
        def _():
            dgf_ref[...] = jnp.zeros_like(dgf_ref)
            dgp_ref[...] = jnp.zeros_like(dgp_ref)

        dh2 = _dot(da_ref[:, 0:D], w_ref[0], NT)
        for k in range(1, N_CHIPS):
            dh2 = dh2 + _dot(da_ref[:, k * D:(k + 1) * D], w_ref[k], NT)
        x1hat, r2 = _rms(x1_ref[...])
        dgf_ref[...] += jnp.sum(dh2 * x1hat, axis=0, keepdims=True)
        dx1 = dy_ref[...] + _rms_bwd(dh2 * gf_ref[...], x1hat, r2)
        ohat, r1 = _rms(o_ref[...])
        dgp_ref[...] += jnp.sum(dx1 * ohat, axis=0, keepdims=True)
        dx1_ref[...] = dx1
        do_ref[...] = _rms_bwd(dx1 * gp_ref[...], ohat, r1).astype(BF16)
        if n:
            pl.when(i == steps - 1)(finish)

    res = pl.pallas_call(
        body, name="ff1_bwd_norms", grid=(steps,),
        in_specs=[_rows(tm, D_FF), _full((N_CHIPS, D, D)), _rows(tm, D), _rows(tm, D), _rows(tm, D),
                  _full((1, D)), _full((1, D))] + [ANY] * n,
        out_specs=[_rows(tm, D), _rows(tm, D), _full((1, D)), _full((1, D))] + [ANY] * n,
        out_shape=[jax.ShapeDtypeStruct((s, D), F32), jax.ShapeDtypeStruct((s, D), BF16),
                   jax.ShapeDtypeStruct((1, D), F32), jax.ShapeDtypeStruct((1, D), F32)] + _swap_shapes(swapping),
        scratch_shapes=_swap_sems(n) if n else [],
        compiler_params=_params("arbitrary", communicates=bool(n)),
    )(da, wg, x1, o, dy, g_fpre, g_post, *swapping)
    return res[0], res[1], res[2], res[3], res[4:]


def out_bwd_gates(do, w_out, pa, pb, z, bg):
    s = do.shape[0]
    tm = 512

    def body(do_ref, w_ref, pa_ref, pb_ref, ga_ref, gb_ref, bg_ref, dpa_ref, dpb_ref, dga_ref, dgb_ref, dbg_ref):
        @pl.when(pl.program_id(0) == 0)
        def _():
            dbg_ref[...] = jnp.zeros_like(dbg_ref)

        dm = _dot(do_ref[...], w_ref[...], NT)
        sa = _sigmoid(ga_ref[...] + bg_ref[0:1, :])
        sb = _sigmoid(gb_ref[...] + bg_ref[1:2, :])
        dpa_ref[...] = (dm * sa).astype(BF16)
        dpb_ref[...] = (dm * sb).astype(BF16)
        dga = dm * pa_ref[...].astype(F32) * (sa * (1.0 - sa))
        dgb = dm * pb_ref[...].astype(F32) * (sb * (1.0 - sb))
        dga_ref[...] = dga.astype(BF16)
        dgb_ref[...] = dgb.astype(BF16)
        dbg_ref[0:1, :] += jnp.sum(dga, axis=0, keepdims=True)
        dbg_ref[1:2, :] += jnp.sum(dgb, axis=0, keepdims=True)

    out = jax.ShapeDtypeStruct((s, D), BF16)
    return pl.pallas_call(
        body, name="out_bwd_gates", grid=(s // tm,),
        in_specs=[_rows(tm, D), _full((D, D)), _rows(tm, D), _rows(tm, D), _rows(tm, D, 5), _rows(tm, D, 6),
                  _full((2, D))],
        out_specs=[_rows(tm, D)] * 4 + [_full((2, D))],
        out_shape=[out] * 4 + [jax.ShapeDtypeStruct((2, D), F32)], compiler_params=_params("arbitrary"),
    )(do, w_out, pa, pb, z, z, bg)


def gating_bwd(z, dya, ln_g, ln_b, w_s, bs_t, swapping):
    s = z.shape[0]
    ones = functools.partial(jnp.ones, (8, CHUNK), BF16)
    n = len(swapping)

    def body(*refs):
        u_ref, v_ref, dya_ref, lg_ref, lb_ref, ws_ref, bst_ref = refs[:7]
        du_ref, dv_ref, dws_ref, dbs_ref, dlg_ref, dlb_ref = refs[7 + n:13 + n]
        dvn_ref = refs[13 + 2 * n]
        ci = pl.program_id(0)
        if n:
            send, finish = _swap_phases(refs[7:7 + n], refs[13 + n:13 + 2 * n], *refs[14 + 2 * n:])
            pl.when(ci == 0)(send)

        @pl.when(ci == 0)
        def _():
            dws_ref[...] = jnp.zeros_like(dws_ref)
            dbs_ref[...] = jnp.zeros_like(dbs_ref)
            dlg_ref[...] = jnp.zeros_like(dlg_ref)
            dlb_ref[...] = jnp.zeros_like(dlb_ref)

        ug, dug_du = _gelu_and_grad(u_ref[...].astype(F32))
        vg, dvg_dv = _gelu_and_grad(v_ref[...].astype(F32))
        vhat, rstd = _layer_norm(vg)
        vn = (vhat * lg_ref[...] + lb_ref[...]).astype(BF16)
        dya = dya_ref[...].astype(F32)
        for g in range(GROUPS):
            cols = slice(g * CHUNK, (g + 1) * CHUNK)
            ws = _tril_ws(ws_ref, g)
            mixed = _dot(ws, vn[:, cols]) + bst_ref[:, g:g + 1]
            du_ref[:, cols] = (dya[:, cols] * mixed * dug_du[:, cols]).astype(BF16)
            dmix = (dya[:, cols] * ug[:, cols]).astype(BF16)
            dbs_ref[g] += _dot(ones(), dmix, NT)
            dws_ref[g] += _dot(dmix, vn[:, cols], NT)
            dvn_ref[:, cols] = _dot(ws, dmix, TN)
        dvn = dvn_ref[...]
        dlg_ref[...] += jnp.sum(dvn * vhat, axis=0, keepdims=True)
        dlb_ref[...] += jnp.sum(dvn, axis=0, keepdims=True)
        dvh = dvn * lg_ref[...]
        dvg = rstd * (dvh - jnp.mean(dvh, axis=-1, keepdims=True)
                      - vhat * jnp.mean(dvh * vhat, axis=-1, keepdims=True))
        dv_ref[...] = (dvg * dvg_dv).astype(BF16)

        @pl.when(ci == pl.num_programs(0) - 1)
        def _():
            r = lax.broadcasted_iota(jnp.int32, (CHUNK, CHUNK), 0)
            c = lax.broadcasted_iota(jnp.int32, (CHUNK, CHUNK), 1)
            for g in range(GROUPS):
                dws_ref[g] = jnp.where(c <= r, dws_ref[g], 0.0)

        if n:
            pl.when(ci == pl.num_programs(0) - 1)(finish)

    out = jax.ShapeDtypeStruct((s, D), BF16)
    res = pl.pallas_call(
        body, name="gating_bwd", grid=(s // CHUNK,),
        in_specs=[_rows(CHUNK, D, 0), _rows(CHUNK, D, 1), _rows(CHUNK, D), _full((1, D)), _full((1, D)),
                  _full((GROUPS, CHUNK, CHUNK)), _full((CHUNK, GROUPS))] + [ANY] * n,
        out_specs=[_rows(CHUNK, D), _rows(CHUNK, D), _full((GROUPS, CHUNK, CHUNK)), _full((GROUPS, 8, CHUNK)),
                   _full((1, D)), _full((1, D))] + [ANY] * n,
        out_shape=[out, out, jax.ShapeDtypeStruct((GROUPS, CHUNK, CHUNK), F32),
                   jax.ShapeDtypeStruct((GROUPS, 8, CHUNK), F32),
                   jax.ShapeDtypeStruct((1, D), F32), jax.ShapeDtypeStruct((1, D), F32)] + _swap_shapes(swapping),
        scratch_shapes=[pltpu.VMEM((CHUNK, D), F32)] + (_swap_sems(n) if n else []),
        compiler_params=_params("arbitrary", communicates=bool(n)),
    )(z, z, dya, ln_g, ln_b, w_s, bs_t, *swapping)
    return (*res[:6], res[6:])


def attn_bwd(z, yb, dyb, lse, logc, ka, kb, scattering, gathering=None):
    s = z.shape[0]
    nq = s // ATT_T
    t = ATT_T
    grp = ATT_BWD_GROUP
    ngrp = HEADS // 2 // grp
    wide = 128 * grp
    qcol, kcol, vcol = 2 * D // wide, 3 * D // wide, 4 * D // wide
    scale = 1.0 / math.sqrt(HEAD_DIM)
    n = len(scattering)
    g8 = 0 if gathering is None else 1

    def body(*refs):
        q_ref, k_ref, v_ref, y_ref, dy_ref, lse_ref, lc_ref, ka_ref, kb_ref = refs[:9]
        dq_ref, dk_ref, dv_ref = refs[9 + n + g8:12 + n + g8]
        qa_s, qt_s, da_s, dt_s, dq_s, dkt_s, dvt_s = refs[12 + 2 * n + 2 * g8:19 + 2 * n + 2 * g8]
        sems = refs[19 + 2 * n + 2 * g8:]
        gi, j = pl.program_id(0), pl.program_id(1)
        first, lane, ones = _head_masks()
        if n:
            send, finish = _scatter_phases(refs[9:9 + n], refs[12 + n + g8:12 + 2 * n + g8], *sems[:2])
            pl.when((gi == 0) & (j == 0))(send)
        if g8:
            send8, pass_on8, finish8 = _allgather8_phases(refs[12 + 2 * n + g8], *sems[2 * (n > 0):])
            pl.when((gi == 0) & (j == 0))(send8)
            pl.when((gi == ngrp - 1) & (j == nq - 1))(pass_on8)

        @pl.when(j == 0)
        def _():
            dq_s[...] = jnp.zeros_like(dq_s)
            for pr in range(grp):
                cols = slice(pr * 128, (pr + 1) * 128)
                for ib in range(nq):
                    rows = slice(ib * t, (ib + 1) * t)
                    q = q_ref[rows, cols].astype(F32) * scale
                    lse = lse_ref[rows, cols]
                    qa_s[pr, 0, ib] = jnp.where(first, q, _place3(lane, HEAD_DIM + 2 * AUG, _split3(-lse[:, 0:1]),
                                                                  ones(0, 2 * AUG))).astype(BF16)
                    qa_s[pr, 1, ib] = jnp.where(
                        first, _place3(lane, 2 * AUG, _split3(-lse[:, HEAD_DIM:HEAD_DIM + 1]), ones(1, 2 * AUG)),
                        q).astype(BF16)
                    qt_s[pr, ib, :, 0:t] = jnp.where(first, q, 0.0).T.astype(BF16)
                    qt_s[pr, ib, :, t:2 * t] = jnp.where(first, 0.0, q).T.astype(BF16)
                    do = dy_ref[rows, cols].astype(F32)
                    prod = do * y_ref[rows, cols].astype(F32)
                    dd0 = jnp.sum(jnp.where(first, prod, 0.0), axis=-1, keepdims=True)
                    dd1 = jnp.sum(jnp.where(first, 0.0, prod), axis=-1, keepdims=True)
                    da_s[pr, 0, ib] = jnp.where(first, do, _place3(lane, HEAD_DIM, _split3(-dd0), 0.0)).astype(BF16)
                    da_s[pr, 1, ib] = jnp.where(first, _place3(lane, 0, _split3(-dd1), 0.0), do).astype(BF16)
                    dt_s[pr, ib, :, 0:t] = jnp.where(first, do, 0.0).T.astype(BF16)
                    dt_s[pr, ib, :, t:2 * t] = jnp.where(first, 0.0, do).T.astype(BF16)

        keys = []
        for pr in range(grp):
            kj = k_ref[:, pr * 128:(pr + 1) * 128].astype(F32)
            vj = v_ref[:, pr * 128:(pr + 1) * 128].astype(F32)
            keys.append((
                jnp.where(first, kj, ka_ref[pr, 0] + kb_ref[pr, 0, pl.ds(j, 1), :]).astype(BF16),
                jnp.where(first, ka_ref[pr, 1] + kb_ref[pr, 1, pl.ds(j, 1), :], kj).astype(BF16),
                jnp.concatenate([jnp.where(first, kj, 0.0), jnp.where(first, 0.0, kj)], axis=0).astype(BF16),
                jnp.where(first, vj, ones(0, AUG)).astype(BF16),
                jnp.where(first, ones(1, AUG), vj).astype(BF16)))
        dkt_s[...] = jnp.zeros_like(dkt_s)
        dvt_s[...] = jnp.zeros_like(dvt_s)

        def step(i, _):
            lc = lc_ref[i - j]
            rows = pl.ds(pl.multiple_of(i * t, t), t)
            for pr in range(grp):
                k0a, k1a, kst, v0a, v1a = keys[pr]
                p0 = jnp.exp(_dot(qa_s[pr, 0, i], k0a, NT) + lc)
                p1 = jnp.exp(_dot(qa_s[pr, 1, i], k1a, NT) + lc)
                e0 = (p0 * _dot(da_s[pr, 0, i], v0a, NT)).astype(BF16)
                e1 = (p1 * _dot(da_s[pr, 1, i], v1a, NT)).astype(BF16)
                dq_s[pr, rows, :] += _dot(jnp.concatenate([e0, e1], axis=1), kst)
                dvt_s[pr] += _dot(dt_s[pr, i], jnp.concatenate([p0.astype(BF16), p1.astype(BF16)], axis=0))
                dkt_s[pr] += _dot(qt_s[pr, i], jnp.concatenate([e0, e1], axis=0))
            return 0

        lax.fori_loop(j, nq, step, 0)
        for pr in range(grp):
            dk_ref[:, pr * 128:(pr + 1) * 128] = dkt_s[pr].T.astype(BF16)
            dv_ref[:, pr * 128:(pr + 1) * 128] = dvt_s[pr].T.astype(BF16)

        @pl.when(j == nq - 1)
        def _():
            for pr in range(grp):
                dq_ref[:, pr * 128:(pr + 1) * 128] = (dq_s[pr] * scale).astype(BF16)

        if n:
            pl.when((gi == ngrp - 1) & (j == nq - 1))(finish)
        if g8:
            pl.when((gi == ngrp - 1) & (j == nq - 1))(finish8)

    colblock = lambda c: pl.BlockSpec((s, wide), lambda g, j: (0, c + g))
    once = lambda c: pl.BlockSpec((s, wide), lambda g, j: (0, c + g), pipeline_mode=pl.Buffered(1))
    blk = lambda c: pl.BlockSpec((t, wide), lambda g, j: (j, c + g))
    out = jax.ShapeDtypeStruct((s, D), BF16)
    res = pl.pallas_call(
        body, name="attn_bwd", grid=(ngrp, nq),
        in_specs=[once(qcol), blk(kcol), blk(vcol), once(0), once(0), once(0),
                  pl.BlockSpec((nq, t, t), lambda g, j: (0, 0, 0), pipeline_mode=pl.Buffered(1)),
                  pl.BlockSpec((grp, 2, t, 128), lambda g, j: (g, 0, 0, 0)),
                  pl.BlockSpec((grp, 2, nq, 128), lambda g, j: (g, 0, 0, 0))] + [ANY] * (n + g8),
        out_specs=[colblock(0), blk(0), blk(0)] + [ANY] * (n + g8),
        out_shape=[out] * 3 + _scatter_shapes(scattering)
        + ([jax.ShapeDtypeStruct(gathering.shape, gathering.dtype)] if g8 else []),
        input_output_aliases={9 + n: 3 + n} if g8 else {},
        scratch_shapes=[pltpu.VMEM((grp, 2, nq, t, 128), BF16), pltpu.VMEM((grp, nq, 128, 2 * t), BF16),
                        pltpu.VMEM((grp, 2, nq, t, 128), BF16), pltpu.VMEM((grp, nq, 128, 2 * t), BF16),
                        pltpu.VMEM((grp, s, 128), F32), pltpu.VMEM((grp, 128, t), F32),
                        pltpu.VMEM((grp, 128, t), F32)]
        + (_scatter_sems(n) if n else [])
        + ([pltpu.SemaphoreType.DMA((7,)), pltpu.SemaphoreType.DMA((7,))] if g8 else []),
        compiler_params=_params("arbitrary", "arbitrary", communicates=bool(n + g8)),
    )(z, z, z, yb, dyb, lse, logc, ka, kb, *scattering, *([gathering] if g8 else []))
    return res[0], res[1], res[2], res[3:3 + n], (res[3 + n] if g8 else None)


def in_bwd_norm(dz, wg, x, dx1, g_pre, scattering, gathering=None):
    s = x.shape[0]
    tm = 512
    n = len(scattering)
    g = 0 if gathering is None else 1
    last = (s // tm - 1, N_CHIPS - 1)

    def body(*refs):
        dz_ref, w_ref, x_ref, dx1_ref, g_ref = refs[:5]
        dx_ref, dg_ref = refs[5 + n + g:7 + n + g]
        acc_ref = refs[7 + 2 * n + 2 * g]
        sems = refs[8 + 2 * n + 2 * g:]
        k, i = pl.program_id(0), pl.program_id(1)
        rows = pl.ds(pl.multiple_of(i * tm, tm), tm)
        if n:
            send, finish = _scatter_phases(refs[5:5 + n], refs[7 + n + g:7 + 2 * n + g], *sems[:2])
            pl.when((i == 0) & (k == 0))(send)
        if g:
            send8, pass_on8, finish8 = _allgather8_phases(refs[7 + 2 * n + g], *sems[2 * (n > 0):])
            pl.when((i == 0) & (k == 0))(send8)
            pl.when((i == last[0]) & (k == last[1]))(pass_on8)

        @pl.when((i == 0) & (k == 0))
        def _():
            dg_ref[...] = jnp.zeros_like(dg_ref)

        part = _dot(dz_ref[...], w_ref[...], NT)

        @pl.when(k == 0)
        def _():
            acc_ref[rows, :] = part

        @pl.when((k > 0) & (k < N_CHIPS - 1))
        def _():
            acc_ref[rows, :] += part

        @pl.when(k == N_CHIPS - 1)
        def _():
            dh = acc_ref[rows, :] + part
            xhat, r = _rms(x_ref[...])
            dg_ref[...] += jnp.sum(dh * xhat, axis=0, keepdims=True)
            dx_ref[...] = dx1_ref[...] + _rms_bwd(dh * g_ref[...], xhat, r)

        if n:
            pl.when((i == last[0]) & (k == last[1]))(finish)
        if g:
            pl.when((i == last[0]) & (k == last[1]))(finish8)

    row = pl.BlockSpec((tm, D), lambda k, i: (jnp.where(k == N_CHIPS - 1, i, 0), 0))
    vec = pl.BlockSpec((1, D), lambda k, i: (0, 0))
    res = pl.pallas_call(
        body, name="in_bwd_norm", grid=(N_CHIPS, s // tm),
        in_specs=[pl.BlockSpec((tm, IN_SHARD), lambda k, i: (i, k)),
                  pl.BlockSpec((None, D, IN_SHARD), lambda k, i: (k, 0, 0)), row, row, vec] + [ANY] * (n + g),
        out_specs=[row, vec] + [ANY] * (n + g),
        out_shape=[jax.ShapeDtypeStruct((s, D), F32), jax.ShapeDtypeStruct((1, D), F32)]
        + _scatter_shapes(scattering) + ([jax.ShapeDtypeStruct(gathering.shape, gathering.dtype)] if g else []),
        input_output_aliases={5 + n: 2 + n} if g else {},
        scratch_shapes=[pltpu.VMEM((s, D), F32)] + (_scatter_sems(n) if n else [])
        + ([pltpu.SemaphoreType.DMA((7,)), pltpu.SemaphoreType.DMA((7,))] if g else []),
        compiler_params=_params("arbitrary", "arbitrary", communicates=bool(n + g)),
    )(dz, wg, x, dx1, g_pre, *scattering, *([gathering] if g else []))
    return res[0], res[1], res[2:2 + n], (res[2 + n] if g else None)


def _adamw_math(w, g, m, v):
    m = ADAM_B1 * m + (1.0 - ADAM_B1) * g
    v = ADAM_B2 * v + (1.0 - ADAM_B2) * (g * g)
    m_hat = m / (1.0 - ADAM_B1 ** ADAM_STEP)
    v_hat = v / (1.0 - ADAM_B2 ** ADAM_STEP)
    delta = -ADAM_LR * (m_hat / (jnp.sqrt(v_hat) + ADAM_EPS) + ADAM_WD * w)
    return delta, m, v


def adamw(name, w, g, m, v, tr):
    r, c = w.shape

    def body(w_ref, g_ref, m_ref, v_ref, go_ref, d_ref, nm_ref, nv_ref):
        g = g_ref[...]
        go_ref[...] = g
        d_ref[...], nm_ref[...], nv_ref[...] = _adamw_math(w_ref[...], g, m_ref[...], v_ref[...])

    out = jax.ShapeDtypeStruct((r, c), F32)
    return pl.pallas_call(
        body, name=name, grid=(r // tr,), in_specs=[_rows(tr, c)] * 4, out_specs=[_rows(tr, c)] * 4,
        out_shape=[out] * 4, compiler_params=_params("parallel"),
    )(w, g, m, v)


def _allgather8_phases(buf, send_sems, recv_sems):
    x, y, c, chips = _place()
    me = 2 * x + y
    sibling = (x, y, 1 - c)
    rows = buf.shape[1] // 2

    def part(chip, core):
        return buf.at[chip, pl.ds(core * rows, rows)]

    def copy(k, block, to):
        return pltpu.make_async_remote_copy(src_ref=block, dst_ref=block, send_sem=send_sems.at[k],
                                            recv_sem=recv_sems.at[k], device_id=to, device_id_type=MESH)

    def chip_of(j):
        return 2 * chips[j][0] + chips[j][1]

    def send():
        copy(0, part(me, c), sibling).start()
        for j in range(3):
            copy(1 + j, part(me, c), (chips[j][0], chips[j][1], c)).start()

    def pass_on():
        for j in range(3):
            copy(1 + j, part(chip_of(j), c), (chips[j][0], chips[j][1], c)).wait_recv()
            copy(4 + j, part(chip_of(j), c), sibling).start()

    def finish():
        copy(0, part(me, 1 - c), sibling).wait_recv()
        for j in range(3):
            copy(4 + j, part(chip_of(j), 1 - c), sibling).wait_recv()
        copy(0, part(me, c), sibling).wait_send()
        for j in range(3):
            copy(1 + j, part(me, c), (chips[j][0], chips[j][1], c)).wait_send()
            copy(4 + j, part(chip_of(j), c), sibling).wait_send()

    return send, pass_on, finish


def add_halves(name, g, recv, c_idx, tr):
    n, h, c = recv.shape

    def body(c_ref, g_ref, r_ref, o_ref):
        o_ref[...] = (g_ref[...] + r_ref[...]).astype(BF16)

    nb = h // tr
    return pl.pallas_call(
        body, name=name,
        grid_spec=pltpu.PrefetchScalarGridSpec(
            num_scalar_prefetch=1, grid=(n, nb),
            in_specs=[pl.BlockSpec((None, tr, c), lambda k, i, c_ref: (k, c_ref[0] * nb + i, 0)),
                      pl.BlockSpec((None, tr, c), lambda k, i, c_ref: (k, i, 0))],
            out_specs=pl.BlockSpec((None, tr, c), lambda k, i, c_ref: (k, i, 0))),
        out_shape=jax.ShapeDtypeStruct((n, h, c), BF16), compiler_params=_params("parallel", "parallel"),
    )(c_idx, g, recv)


def sum_chips(name, parts, recv, where, tr, after=None):
    n, h, c = recv.shape
    nb = h // tr

    def body(w_ref, p_ref, r_ref, *rest):
        acc = p_ref[...].astype(F32)
        for k in range(n):
            acc = acc + r_ref[k].astype(F32)
        rest[-1][...] = acc

    return pl.pallas_call(
        body, name=name,
        grid_spec=pltpu.PrefetchScalarGridSpec(
            num_scalar_prefetch=1, grid=(nb,),
            in_specs=[pl.BlockSpec((None, tr, c), lambda i, w_ref: (w_ref[0], i, 0)),
                      pl.BlockSpec((n, tr, c), lambda i, w_ref: (0, i, 0))] + ([ANY] if after is not None else []),
            out_specs=pl.BlockSpec((tr, c), lambda i, w_ref: (w_ref[1] * nb + i, 0))),
        out_shape=jax.ShapeDtypeStruct((2 * h, c), F32), compiler_params=_params("parallel"),
    )(where, parts, recv, *([after] if after is not None else []))


def place_shard(name, shard, where, dtype, tr):
    r, c = shard.shape

    def body(w_ref, s_ref, o_ref):
        o_ref[...] = s_ref[...].astype(dtype)

    return pl.pallas_call(
        body, name=name,
        grid_spec=pltpu.PrefetchScalarGridSpec(
            num_scalar_prefetch=1, grid=(r // tr,),
            in_specs=[pl.BlockSpec((tr, c), lambda i, w_ref: (i, 0))],
            out_specs=pl.BlockSpec((None, tr, c), lambda i, w_ref: (w_ref[0], i, 0))),
        out_shape=jax.ShapeDtypeStruct((N_CHIPS, r, c), dtype), compiler_params=_params("parallel"),
    )(where, shard)


ANY = pl.BlockSpec(memory_space=pl.ANY)


def _place():
    x, y, c = lax.axis_index("x"), lax.axis_index("y"), lax.axis_index("c")
    chips = [(1 - x, y), (x, 1 - y), (1 - x, 1 - y)]
    return x, y, c, chips


def gather_shards(arrays):
    n = len(arrays)

    def body(*refs):
        send, pass_on, finish = _gather_phases(refs[n:2 * n], *refs[2 * n:], _spans(arrays))
        send()
        pass_on()
        finish()

    return pl.pallas_call(
        body, name="gather_shards", in_specs=[ANY] * n, out_specs=[ANY] * n,
        out_shape=[jax.ShapeDtypeStruct(a.shape, a.dtype) for a in _arrays(arrays)],
        input_output_aliases={w: w for w in range(n)}, scratch_shapes=_gather_sems(n),
        compiler_params=pltpu.CompilerParams(has_side_effects=True),
    )(*_arrays(arrays))


def _gather_sems(n):
    return [pltpu.SemaphoreType.DMA((6 * n,)), pltpu.SemaphoreType.DMA((6 * n,))]


class Span(typing.NamedTuple):
    array: jax.Array
    lo: int
    hi: int
    ways: tuple = (0, 1, 2)


def _arrays(gathering):
    return [g.array if isinstance(g, Span) else g for g in gathering]


def _spans(gathering):
    return [(g.lo, g.hi, g.ways) if isinstance(g, Span) else (0, g.shape[1], (0, 1, 2)) for g in gathering]


def _gather_phases(out, send_sems, recv_sems, spans):
    n = len(out)
    if not any(ways for _, _, ways in spans):
        return (lambda: None,) * 3
    x, y, c, chips = _place()
    me = 2 * x + y
    sibling = (x, y, 1 - c)

    def half(w, chip, core):
        lo, hi, _ = spans[w]
        h = (hi - lo) // 2
        return out[w].at[chip, pl.ds(lo + core * h, h)]

    def copy(k, block, to):
        return pltpu.make_async_remote_copy(src_ref=block, dst_ref=block, send_sem=send_sems.at[k],
                                            recv_sem=recv_sems.at[k], device_id=to, device_id_type=MESH)

    def over_ici(w, j, chip):
        return copy(3 * w + j, half(w, chip, c), (chips[j][0], chips[j][1], c))

    def over_d2d(w, j, core):
        return copy(3 * n + 3 * w + j, half(w, 2 * chips[j][0] + chips[j][1], core), sibling)

    pairs = [(w, j) for w in range(n) for j in spans[w][2]]

    def send():
        for w, j in pairs:
            over_ici(w, j, me).start()

    def pass_on():
        for w, j in pairs:
            over_ici(w, j, 2 * chips[j][0] + chips[j][1]).wait_recv()
            over_d2d(w, j, c).start()

    def finish():
        for w, j in pairs:
            over_d2d(w, j, 1 - c).wait_recv()
        for w, j in pairs:
            over_ici(w, j, me).wait_send()
            over_d2d(w, j, c).wait_send()

    return send, pass_on, finish


def _relay_sems():
    return [pltpu.SemaphoreType.DMA((4,)), pltpu.SemaphoreType.DMA((4,))]


def _relay_phases(out, send_sems, recv_sems):
    x, y, c, chips = _place()
    sibling = (x, y, 1 - c)
    rows = out.shape[1]
    quarter = rows // 4
    far = 2 * chips[2][0] + chips[2][1]

    def piece(chip, way, core):
        return out.at[chip, pl.ds(way * (rows // 2) + core * quarter, quarter)]

    def copy(k, block, to):
        return pltpu.make_async_remote_copy(src_ref=block, dst_ref=block, send_sem=send_sems.at[k],
                                            recv_sem=recv_sems.at[k], device_id=to, device_id_type=MESH)

    def over_ici(way, chip):
        return copy(way, piece(chip, way, c), (chips[way][0], chips[way][1], c))

    def over_d2d(way, core):
        return copy(2 + way, piece(far, way, core), sibling)

    def send():
        for way in range(2):
            other = chips[1 - way]
            over_ici(way, 2 * other[0] + other[1]).start()

    def pass_on():
        for way in range(2):
            over_ici(way, far).wait_recv()
            over_d2d(way, c).start()

    def finish():
        for way in range(2):
            over_d2d(way, 1 - c).wait_recv()
        for way in range(2):
            other = chips[1 - way]
            over_ici(way, 2 * other[0] + other[1]).wait_send()
            over_d2d(way, c).wait_send()

    return send, pass_on, finish


def swap_halves(name, grads):
    n = len(grads)

    def body(*refs):
        send, finish = _swap_phases(refs[:n], refs[n:2 * n], *refs[2 * n:])
        send()
        finish()

    return pl.pallas_call(
        body, name=name, in_specs=[ANY] * n, out_specs=[ANY] * n, out_shape=_swap_shapes(grads),
        scratch_shapes=_swap_sems(n), compiler_params=pltpu.CompilerParams(has_side_effects=True),
    )(*grads)


def _swap_shapes(grads):
    return [jax.ShapeDtypeStruct((a.shape[0], a.shape[1] // 2, a.shape[2]), a.dtype) for a in grads]


def _swap_sems(n):
    return [pltpu.SemaphoreType.DMA((n,)), pltpu.SemaphoreType.DMA((n,))]


def _swap_phases(g, out, send_sems, recv_sems):
    x, y, c, _ = _place()

    def copies():
        return [pltpu.make_async_remote_copy(
            src_ref=g[w].at[:, pl.ds((1 - c) * (g[w].shape[1] // 2), g[w].shape[1] // 2)], dst_ref=out[w],
            send_sem=send_sems.at[w], recv_sem=recv_sems.at[w], device_id=(x, y, 1 - c), device_id_type=MESH)
            for w in range(len(g))]

    def send():
        for cp in copies():
            cp.start()

    def finish():
        for cp in copies():
            cp.wait()

    return send, finish


def _send_phases(g, out, send_sems, recv_sems):
    x, y, c, _ = _place()

    def copies():
        return [pltpu.make_async_remote_copy(
            src_ref=g[w], dst_ref=out[w], send_sem=send_sems.at[w], recv_sem=recv_sems.at[w],
            device_id=(x, y, 1 - c), device_id_type=MESH) for w in range(len(g))]

    def send():
        for cp in copies():
            cp.start()

    def finish():
        for cp in copies():
            cp.wait()

    return send, finish


def dw_in_half(name, h, dz, which, sending):
    s = h.shape[0]
    hh, tb = D // 2, IN_SHARD // 2
    n = len(sending)
    steps = IN_COLS // tb

    def body(w_ref, *refs):
        a_ref, b_ref, o_ref = refs[0], refs[1], refs[2 + n]
        j = pl.program_id(0)
        if n:
            send, finish = _send_phases(refs[2:2 + n], refs[3 + n:3 + 2 * n], *refs[3 + 2 * n:])
            pl.when(j == 0)(send)
        o_ref[...] = _dot(a_ref[...], b_ref[...], TN)
        if n:
            pl.when(j == steps - 1)(finish)

    out = pl.pallas_call(
        body, name=name,
        grid_spec=pltpu.PrefetchScalarGridSpec(
            num_scalar_prefetch=1, grid=(steps,),
            in_specs=[pl.BlockSpec((s, hh), lambda j, w: (0, w[0])), pl.BlockSpec((s, tb), lambda j, w: (0, j))]
            + [ANY] * n,
            out_specs=[pl.BlockSpec((None, hh, tb), lambda j, w: (j // 2, 0, j % 2))] + [ANY] * n,
            scratch_shapes=_swap_sems(n) if n else []),
        out_shape=[jax.ShapeDtypeStruct((N_CHIPS, hh, IN_SHARD), F32)]
        + [jax.ShapeDtypeStruct(a.shape, a.dtype) for a in sending],
        compiler_params=_params("arbitrary", communicates=bool(n)),
    )(which, h, dz, *sending)
    return out[0], out[1:]


def scatter_chips(parts):
    n = len(parts)

    def body(*refs):
        send, finish = _scatter_phases(refs[:n], refs[n:2 * n], *refs[2 * n:])
        send()
        finish()

    return pl.pallas_call(
        body, name="scatter_chips", in_specs=[ANY] * n, out_specs=[ANY] * n,
        out_shape=_scatter_shapes(parts), scratch_shapes=_scatter_sems(n),
        compiler_params=pltpu.CompilerParams(has_side_effects=True),
    )(*parts)


def _scatter_shapes(parts):
    return [jax.ShapeDtypeStruct((3,) + a.shape[1:], a.dtype) for a in parts]


def _scatter_sems(n):
    return [pltpu.SemaphoreType.DMA((3 * n,)), pltpu.SemaphoreType.DMA((3 * n,))]


def _scatter_phases(p, out, send_sems, recv_sems):
    x, y, c, chips = _place()

    def copies():
        return [pltpu.make_async_remote_copy(
            src_ref=p[w].at[2 * px + py], dst_ref=out[w].at[j], send_sem=send_sems.at[3 * w + j],
            recv_sem=recv_sems.at[3 * w + j], device_id=(px, py, c), device_id_type=MESH)
            for w in range(len(p)) for j, (px, py) in enumerate(chips)]

    def send():
        for cp in copies():
            cp.start()

    def finish():
        for cp in copies():
            cp.wait()

    return send, finish


def _join_only(arrays):
    n = len(arrays)

    def body(*refs):
        out = refs[n:2 * n]
        send_sems, recv_sems = refs[2 * n:]
        x, y, c, _ = _place()

        def copy(w, core):
            h = out[w].shape[0] // 2
            rows = out[w].at[pl.ds(core * h, h)]
            return pltpu.make_async_remote_copy(
                src_ref=rows, dst_ref=rows, send_sem=send_sems.at[w], recv_sem=recv_sems.at[w],
                device_id=(x, y, 1 - c), device_id_type=MESH)

        for w in range(n):
            copy(w, c).start()
        for w in range(n):
            copy(w, 1 - c).wait_recv()
        for w in range(n):
            copy(w, c).wait_send()

    return pl.pallas_call(
        body, name="join_only", in_specs=[ANY] * n, out_specs=[ANY] * n,
        out_shape=[jax.ShapeDtypeStruct(a.shape, a.dtype) for a in arrays],
        input_output_aliases={w: w for w in range(n)},
        scratch_shapes=[pltpu.SemaphoreType.DMA((n,)), pltpu.SemaphoreType.DMA((n,))],
        compiler_params=pltpu.CompilerParams(has_side_effects=True),
    )(*arrays)


HBM = pl.BlockSpec(memory_space=pltpu.HBM)
SEM = pl.BlockSpec(memory_space=pltpu.SEMAPHORE)
DATAFLOW = pltpu.SideEffectType.DATAFLOW_SIDE_EFFECTING


def _scatter_copies(p_ref, land_ref, send_sems, recv_sems):
    x, y, c, chips = _place()
    return [pltpu.make_async_remote_copy(
        src_ref=p_ref.at[2 * px + py], dst_ref=land_ref.at[j], send_sem=send_sems[j], recv_sem=recv_sems[j],
        device_id=(px, py, c), device_id_type=MESH) for j, (px, py) in enumerate(chips)]


def scatter_start(p):
    land = jax.ShapeDtypeStruct((3,) + p.shape[1:], p.dtype)

    def body(p_ref, land_ref, *outs):
        for cp in _scatter_copies(p_ref, land_ref, outs[0:3], outs[3:6]):
            cp.start()
        outs[8][...] = jnp.zeros_like(outs[8])

    return pl.pallas_call(
        body, name="scatter_start",
        out_shape=(pltpu.SemaphoreType.DMA(()),) * 6
        + (pltpu.HBM(p.shape, p.dtype), pltpu.HBM(land.shape, land.dtype), jax.ShapeDtypeStruct((8, 128), F32)),
        in_specs=(HBM, HBM), out_specs=(SEM,) * 6 + (HBM, HBM, pl.BlockSpec(memory_space=pltpu.VMEM)),
        input_output_aliases={0: 6, 1: 7},
        compiler_params=pltpu.CompilerParams(has_side_effects=DATAFLOW),
    )(pltpu.with_memory_space_constraint(p, pltpu.HBM),
      pltpu.with_memory_space_constraint(lax.empty(land.shape, land.dtype), pltpu.HBM))


def scatter_wait(started, after):
    sems, p_thru, land_thru = started[0:6], started[6], started[7]

    def body(p_ref, land_ref, *refs):
        for cp in _scatter_copies(p_ref, land_ref, refs[0:3], refs[3:6]):
            cp.wait_send()
            cp.wait_recv()

    return pl.pallas_call(
        body, name="scatter_wait",
        out_shape=(pltpu.HBM(p_thru.shape, p_thru.dtype), pltpu.HBM(land_thru.shape, land_thru.dtype)),
        in_specs=(HBM, HBM) + (SEM,) * 6 + (pl.BlockSpec(memory_space=pl.ANY),) * len(after), out_specs=(HBM, HBM),
        input_output_aliases={0: 0, 1: 1},
        compiler_params=pltpu.CompilerParams(has_side_effects=DATAFLOW),
    )(p_thru, land_thru, *sems, *after)


def _gather_leg_copies(buf_ref, leg, send_sems, recv_sems):
    x, y, c, chips = _place()
    h = buf_ref.shape[1] // 2
    copies = []
    for j, (px, py) in enumerate(chips):
        chip, to = (2 * x + y, (px, py, c)) if leg == "ici" else (2 * px + py, (x, y, 1 - c))
        block = buf_ref.at[chip, pl.ds(c * h, h)]
        copies.append(pltpu.make_async_remote_copy(
            src_ref=block, dst_ref=block, send_sem=send_sems[j], recv_sem=recv_sems[j],
            device_id=to, device_id_type=MESH))
    return copies


def gather_leg_start(name, buf, leg):
    def body(buf_ref, *outs):
        for cp in _gather_leg_copies(buf_ref, leg, outs[0:3], outs[3:6]):
            cp.start()
        outs[7][...] = jnp.zeros_like(outs[7])

    return pl.pallas_call(
        body, name=name,
        out_shape=(pltpu.SemaphoreType.DMA(()),) * 6
        + (pltpu.HBM(buf.shape, buf.dtype), jax.ShapeDtypeStruct((8, 128), F32)),
        in_specs=(HBM,), out_specs=(SEM,) * 6 + (HBM, pl.BlockSpec(memory_space=pltpu.VMEM)),
        input_output_aliases={0: 6},
        compiler_params=pltpu.CompilerParams(has_side_effects=DATAFLOW),
    )(pltpu.with_memory_space_constraint(buf, pltpu.HBM))


def gather_leg_wait(name, started, leg, after):
    def body(buf_ref, *refs):
        for cp in _gather_leg_copies(buf_ref, leg, refs[0:3], refs[3:6]):
            cp.wait_send()
            cp.wait_recv()

    buf = started[6]
    return pl.pallas_call(
        body, name=name, out_shape=pltpu.HBM(buf.shape, buf.dtype),
        in_specs=(HBM,) + (SEM,) * 6 + (pl.BlockSpec(memory_space=pl.ANY),) * len(after), out_specs=HBM,
        input_output_aliases={0: 0},
        compiler_params=pltpu.CompilerParams(has_side_effects=DATAFLOW),
    )(buf, *started[0:6], *after)


def gather_leg(name, buf, leg):
    def body(buf_ref, out_ref, send_sems, recv_sems):
        copies = _gather_leg_copies(out_ref, leg, [send_sems.at[j] for j in range(3)],
                                    [recv_sems.at[j] for j in range(3)])
        for cp in copies:
            cp.start()
        for cp in copies:
            cp.wait_send()
            cp.wait_recv()

    return pl.pallas_call(
        body, name=name, out_shape=pltpu.HBM(buf.shape, buf.dtype), in_specs=(HBM,), out_specs=HBM,
        input_output_aliases={0: 0},
        scratch_shapes=[pltpu.SemaphoreType.DMA((3,)), pltpu.SemaphoreType.DMA((3,))],
        compiler_params=pltpu.CompilerParams(has_side_effects=True),
    )(buf)


def join_halves(arrays, gathering=None):
    n = len(arrays)
    if gathering is None:
        return _join_only(arrays), None

    def body(*refs):
        out = refs[n + 1:2 * n + 1]
        send_sems, recv_sems = refs[2 * n + 2:2 * n + 4]
        send8, pass_on8, finish8 = _allgather8_phases(refs[2 * n + 1], *refs[2 * n + 4:])
        x, y, c, _ = _place()

        def copy(w, core):
            h = out[w].shape[0] // 2
            rows = out[w].at[pl.ds(core * h, h)]
            return pltpu.make_async_remote_copy(
                src_ref=rows, dst_ref=rows, send_sem=send_sems.at[w], recv_sem=recv_sems.at[w],
                device_id=(x, y, 1 - c), device_id_type=MESH)

        send8()
        for w in range(n):
            copy(w, c).start()
        pass_on8()
        for w in range(n):
            copy(w, 1 - c).wait_recv()
        finish8()
        for w in range(n):
            copy(w, c).wait_send()

    res = pl.pallas_call(
        body, name="join_halves", in_specs=[ANY] * (n + 1), out_specs=[ANY] * (n + 1),
        out_shape=[jax.ShapeDtypeStruct(a.shape, a.dtype) for a in list(arrays) + [gathering]],
        input_output_aliases={w: w for w in range(n + 1)},
        scratch_shapes=[pltpu.SemaphoreType.DMA((n,)), pltpu.SemaphoreType.DMA((n,)),
                        pltpu.SemaphoreType.DMA((7,)), pltpu.SemaphoreType.DMA((7,))],
        compiler_params=pltpu.CompilerParams(has_side_effects=True),
    )(*arrays, gathering)
    return res[:n], res[n]


def allreduce_small(packed):
    r, c = packed.shape
    n_dev = 8

    def body(x_ref, all_ref, sum_ref, send_sems, recv_sems, local_sem):
        x, y, cc, chips = _place()
        me, sibling = (x, y, cc), (x, y, 1 - cc)

        def rows(px, py, pc):
            return all_ref.at[4 * px + 2 * py + pc]

        def copy(k, block, to, src=None):
            return pltpu.make_async_remote_copy(
                src_ref=rows(*block) if src is None else src, dst_ref=rows(*block), send_sem=send_sems.at[k],
                recv_sem=recv_sems.at[k], device_id=to, device_id_type=MESH)

        mine = pltpu.make_async_copy(x_ref, rows(*me), local_sem)
        mine.start()
        first = [copy(0, me, sibling, src=x_ref)]
        first += [copy(1 + j, me, (*chip, cc), src=x_ref) for j, chip in enumerate(chips)]
        for cp in first:
            cp.start()
        passed = [copy(4 + j, (*chip, cc), sibling) for j, chip in enumerate(chips)]
        for j, chip in enumerate(chips):
            copy(1 + j, (*chip, cc), me).wait_recv()
            passed[j].start()
        copy(0, sibling, me).wait_recv()
        for j, chip in enumerate(chips):
            copy(4 + j, (*chip, 1 - cc), me).wait_recv()
        for cp in first + passed:
            cp.wait_send()
        mine.wait()
        acc = all_ref[0]
        for k in range(1, n_dev):
            acc = acc + all_ref[k]
        sum_ref[...] = acc

    vm = pl.BlockSpec(memory_space=pltpu.VMEM)
    return pl.pallas_call(
        body, name="allreduce_small", in_specs=[vm], out_specs=[vm, vm],
        out_shape=[jax.ShapeDtypeStruct((n_dev, r, c), F32), jax.ShapeDtypeStruct((r, c), F32)],
        scratch_shapes=[pltpu.SemaphoreType.DMA((7,)), pltpu.SemaphoreType.DMA((7,)), pltpu.SemaphoreType.DMA],
        compiler_params=pltpu.CompilerParams(has_side_effects=True, vmem_limit_bytes=VMEM_LIMIT),
    )(packed)[1]


def local_step(x, target, vecs, w_s, bs_t, bg, wg_in, late, core=None, order=None, where=None):
    on_mesh = core is not None

    def add(names, grads, recv):
        return [add_halves("add_" + n, g, r, core, min(r.shape[1], 256)) for n, g, r in zip(names, grads, recv)]

    g_pre, ln_g, ln_b, g_post, g_fpre, g_fpost = vecs
    s = x.shape[0]
    if order is None:
        order = jnp.arange(N_CHIPS, dtype=jnp.int32)
    logc = _attn_tables(s)
    ka, kb = _alibi_tables(s)

    h = norm_pre(x, g_pre)
    if not on_mesh:
        wg_a, wg_b, wg_out, wg_ff1, wg_ff2 = late
    if on_mesh:
        z, wg_in, (wg_a, wg_b, wg_out, wg_ff1, wg_ff2) = mm_in(h, wg_in, order, True, late)
        ya, (wg_b,) = gating_fwd(z, ln_g, ln_b, w_s, bs_t, [wg_b])
        yb, lse, (wg_a, wg_ff1, wg_out, bg) = attn_fwd(z, logc, ka, kb, [wg_a, wg_ff1, wg_out, bg])
        over_ici = gather_leg_start("ff2_ici_start", wg_ff2, "ici")
        bg = jnp.transpose(bg[:, :2, :], (1, 0, 2)).reshape(2, D) + over_ici[7][0:1, 0:1]
    else:
        z, _, _ = mm_in(h, wg_in, order, False)
        ya, _ = gating_fwd(z, ln_g, ln_b, w_s, bs_t, [])
        yb, lse, _ = attn_fwd(z, logc, ka, kb, [])
    merged, pa, pb, _ = proj_merge(ya, yb, wg_a.reshape(D, D), wg_b.reshape(D, D), z, bg, [])
    w_out = wg_out.reshape(D, D)
    o, x1, h2, _ = out_norm(merged, w_out, x, g_post, g_fpre, [])
    a, rl, _ = mm_ff1(h2, wg_ff1, [])
    if on_mesh:
        wg_ff2 = gather_leg("ff2_d2d", gather_leg_wait("ff2_ici_wait", over_ici, "ici", [rl]), "d2d")
    w_ff2 = wg_ff2.reshape(D_FF, D)
    dy, df, d_gfpost, loss = ff2_loss(rl, w_ff2, x1, target, g_fpost)

    half_cols = pl.BlockSpec((D, D // 2), lambda i, j: (0, j))
    d_wff2 = mm_tn("dw_ff2", rl, df, D // 2, D, (D_FF, D), pl.BlockSpec((D // 2, D), lambda i, j: (i, 0)))
    da = ff2_bwd(df, w_ff2, a)
    d_wff1 = mm_tn("dw_ff1", h2, da, D, D // 2, (N_CHIPS, D, D),
                   pl.BlockSpec((None, D, D // 2), lambda i, j: (j // 2, 0, j % 2)))
    d_ff = [d_wff1, d_wff2.reshape(N_CHIPS, D, D)]
    dx1, do, d_gfpre, d_gpost, recv_ff = ff1_bwd_norms(da, wg_ff1, x1, o, dy, g_fpre, g_post, d_ff if on_mesh else [])
    d_wout = mm_tn("dw_out", merged, do, D, D // 2, (D, D), half_cols)
    dpa, dpb, dga, dgb, d_bg = out_bwd_gates(do, w_out, pa, pb, z, bg)
    d_wa = mm_tn("dw_a", ya, dpa, D, D // 2, (D, D), half_cols)
    d_wb = mm_tn("dw_b", yb, dpb, D, D // 2, (D, D), half_cols)
    dya = mm_nt("dy_a", dpa, wg_a.reshape(D, D))
    dyb = mm_nt("dy_b", dpb, wg_b.reshape(D, D))
    d_proj = [d_wa.reshape(N_CHIPS, D // N_CHIPS, D), d_wb.reshape(N_CHIPS, D // N_CHIPS, D),
              d_wout.reshape(N_CHIPS, D // N_CHIPS, D)]
    du, dv, d_ws, d_bs, d_lng, d_lnb, recv_proj = gating_bwd(z, dya, ln_g, ln_b, w_s, bs_t, d_proj if on_mesh else [])
    early = d_proj + d_ff
    parts_early = add(BIG[1:], early, list(recv_proj) + list(recv_ff)) if on_mesh else []
    small = dict(b_gate=d_bg, ln_v_g=d_lng, ln_v_b=d_lnb, w_s=d_ws, b_s=d_bs[:, 0, :],
                 norm_mix_post=d_gpost, norm_ffn_pre=d_gfpre, norm_ffn_post=d_gfpost)
    packed = pack_small(dict(small, norm_mix_pre=jnp.zeros((1, D), F32)), loss, where) if on_mesh else None
    dq, dk, dvb, got_early, packed = attn_bwd(z, yb, dyb, lse, logc, ka, kb, parts_early, packed)
    dz = jnp.concatenate([du, dv, dq, dk, dvb, dga, dgb], axis=1)
    if on_mesh:
        for_sibling, _ = dw_in_half("dw_in_sibling", h, dz, 1 - core, [])
        mine, from_sibling = dw_in_half("dw_in_mine", h, dz, core, [for_sibling])
        d_win = None
        parts_late = [add_halves("add_w_in", mine, from_sibling[0], jnp.zeros((1,), jnp.int32), 256)]
    else:
        half = IN_SHARD // 2
        d_win = mm_tn("dw_in", h, dz, D, half, (N_CHIPS, D, IN_SHARD),
                      pl.BlockSpec((None, D, half), lambda i, j: (j // 2, 0, j % 2)))
        parts_late = []
    started = scatter_start(parts_late[0]) if on_mesh else None
    dx, d_gpre, _, _ = in_bwd_norm(dz, wg_in, x, dx1, g_pre + started[8][0:1, 0:1] if on_mesh else g_pre, [])
    small["norm_mix_pre"] = d_gpre
    return loss[0, 0], dx, [d_win] + early, small, parts_early, list(got_early), packed, started


BIG = ("w_in", "w_a_proj", "w_b_proj", "w_out", "w_ff1", "w_ff2")
SMALL = ("norm_mix_pre", "ln_v_g", "ln_v_b", "b_s", "norm_mix_post", "norm_ffn_pre", "norm_ffn_post", "w_s", "b_gate")
ORDER = ("norm_mix_pre", "w_in", "b_gate", "ln_v_g", "ln_v_b", "w_s", "b_s", "w_a_proj", "w_b_proj", "w_out",
         "norm_mix_post", "norm_ffn_pre", "w_ff1", "w_ff2", "norm_ffn_post")
VEC_ROWS = D // 128
WS_ROW = 7 * VEC_ROWS
BG_ROW = WS_ROW + GROUPS * CHUNK
LOSS_ROW = BG_ROW + 2 * VEC_ROWS
PACK_ROWS = LOSS_ROW + 8


def pack_small(small, loss, where):
    vectors = [small[n] for n in SMALL[:7]]
    operands = vectors + [small["w_s"], small["b_gate"], loss]

    def body(where_ref, *refs):
        out = refs[-1]
        ws_ref, bg_ref, loss_ref = refs[7:10]
        for i, n in enumerate(SMALL[:7]):
            if n == "b_s":
                out[i * VEC_ROWS:(i + 1) * VEC_ROWS, :] = refs[i][...]
            else:
                for j in range(VEC_ROWS):
                    out[i * VEC_ROWS + j:i * VEC_ROWS + j + 1, :] = refs[i][:, j * 128:(j + 1) * 128]
        for g in range(GROUPS):
            out[WS_ROW + g * CHUNK:WS_ROW + (g + 1) * CHUNK, :] = ws_ref[g]
        for r in range(2):
            for j in range(VEC_ROWS):
                row = BG_ROW + r * VEC_ROWS + j
                out[row:row + 1, :] = bg_ref[r:r + 1, j * 128:(j + 1) * 128]
        lane = lax.broadcasted_iota(jnp.int32, (8, 128), 1)
        sub = lax.broadcasted_iota(jnp.int32, (8, 128), 0)
        out[LOSS_ROW:LOSS_ROW + 8, :] = jnp.where((lane == 0) & (sub == 0), loss_ref[...], 0.0)

    return pl.pallas_call(
        body, name="pack_small",
        grid_spec=pltpu.PrefetchScalarGridSpec(
            num_scalar_prefetch=1, grid=(1,), in_specs=[_full(a.shape) for a in operands],
            out_specs=pl.BlockSpec((None, PACK_ROWS, 128), lambda i, w: (w[0], w[1], 0))),
        out_shape=jax.ShapeDtypeStruct((N_CHIPS, 2 * PACK_ROWS, 128), F32), compiler_params=_params("arbitrary"),
    )(where, *operands)


def pack_vector(vec, where):
    def body(where_ref, v_ref, out):
        for j in range(VEC_ROWS):
            out[j:j + 1, :] = v_ref[:, j * 128:(j + 1) * 128]

    return pl.pallas_call(
        body, name="pack_vector",
        grid_spec=pltpu.PrefetchScalarGridSpec(
            num_scalar_prefetch=1, grid=(1,), in_specs=[_full(vec.shape)],
            out_specs=pl.BlockSpec((None, VEC_ROWS, 128), lambda i, w: (w[0], w[1], 0))),
        out_shape=jax.ShapeDtypeStruct((N_CHIPS, 2 * VEC_ROWS, 128), F32), compiler_params=_params("arbitrary"),
    )(where, vec)


def adamw_small(gathered, first, chip, w, m, v):
    shapes = {n: (1, D) for n in SMALL}
    shapes.update(b_s=(GROUPS, CHUNK), w_s=(GROUPS * CHUNK, CHUNK), b_gate=(2, D // N_CHIPS))
    flat = lambda t: [t[n].reshape(shapes[n]) for n in SMALL]
    per = D // N_CHIPS // 128

    def body(chip_ref, all_ref, first_ref, *refs):
        params, outs = refs[:27], refs[27:]
        sub = lax.broadcasted_iota(jnp.int32, (VEC_ROWS, 128), 0)
        sum_ref = outs[36]
        total = all_ref[0, 0:PACK_ROWS, :]
        head = first_ref[0, 0:VEC_ROWS, :]
        for k in range(1, 2 * N_CHIPS):
            total = total + all_ref[k // 2, (k % 2) * PACK_ROWS:(k % 2 + 1) * PACK_ROWS, :]
            head = head + first_ref[k // 2, (k % 2) * VEC_ROWS:(k % 2 + 1) * VEC_ROWS, :]
        sum_ref[...] = total
        sum_ref[0:VEC_ROWS, :] = head

        def gate_row(r):
            rows = sum_ref[BG_ROW + r * VEC_ROWS:BG_ROW + (r + 1) * VEC_ROWS, :]
            return jnp.concatenate([jnp.sum(jnp.where(sub == per * chip_ref[0] + j, rows, 0.0), axis=0, keepdims=True)
                                    for j in range(per)], axis=1)

        for i, n in enumerate(SMALL):
            if n == "b_s":
                g = sum_ref[i * VEC_ROWS:(i + 1) * VEC_ROWS, :]
            elif n == "w_s":
                g = sum_ref[WS_ROW:BG_ROW, :]
            elif n == "b_gate":
                g = jnp.concatenate([gate_row(0), gate_row(1)], axis=0)
            else:
                g = jnp.concatenate([sum_ref[i * VEC_ROWS + j:i * VEC_ROWS + j + 1, :] for j in range(VEC_ROWS)],
                                    axis=1)
            delta, nm, nv = _adamw_math(params[i][...], g, params[9 + i][...], params[18 + i][...])
            outs[4 * i][...], outs[4 * i + 1][...], outs[4 * i + 2][...], outs[4 * i + 3][...] = g, delta, nm, nv

    vm = pl.BlockSpec(memory_space=pltpu.VMEM)
    res = pl.pallas_call(
        body, name="adamw_small",
        in_specs=[pl.BlockSpec(memory_space=pltpu.SMEM)] + [vm] * 29, out_specs=[vm] * 37,
        out_shape=[jax.ShapeDtypeStruct(shapes[n], F32) for n in SMALL for _ in range(4)]
        + [jax.ShapeDtypeStruct((PACK_ROWS, 128), F32)],
        compiler_params=_params(),
    )(chip, gathered, first, *flat(w), *flat(m), *flat(v))
    new = {n: tuple(r.reshape(w[n].shape) for r in res[4 * i:4 * i + 4]) for i, n in enumerate(SMALL)}
    return new, res[36][LOSS_ROW, 0]


def kernel(x, norm_mix_pre, w_in, b_gate, ln_v_g, ln_v_b, w_s, b_s, w_a_proj, w_b_proj, w_out, norm_mix_post, norm_ffn_pre, w_ff1, w_ff2, norm_ffn_post, loss_target, m_norm_mix_pre, m_w_in, m_b_gate, m_ln_v_g, m_ln_v_b, m_w_s, m_b_s, m_w_a_proj, m_w_b_proj, m_w_out, m_norm_mix_post, m_norm_ffn_pre, m_w_ff1, m_w_ff2, m_norm_ffn_post, v_norm_mix_pre, v_w_in, v_b_gate, v_ln_v_g, v_ln_v_b, v_w_s, v_b_s, v_w_a_proj, v_w_b_proj, v_w_out, v_norm_mix_post, v_norm_ffn_pre, v_w_ff1, v_w_ff2, v_norm_ffn_post):
    w = dict(norm_mix_pre=norm_mix_pre, w_in=w_in, b_gate=b_gate, ln_v_g=ln_v_g, ln_v_b=ln_v_b, w_s=w_s, b_s=b_s,
             w_a_proj=w_a_proj, w_b_proj=w_b_proj, w_out=w_out, norm_mix_post=norm_mix_post,
             norm_ffn_pre=norm_ffn_pre, w_ff1=w_ff1, w_ff2=w_ff2, norm_ffn_post=norm_ffn_post)
    m = dict(norm_mix_pre=m_norm_mix_pre, w_in=m_w_in, b_gate=m_b_gate, ln_v_g=m_ln_v_g, ln_v_b=m_ln_v_b, w_s=m_w_s,
             b_s=m_b_s, w_a_proj=m_w_a_proj, w_b_proj=m_w_b_proj, w_out=m_w_out, norm_mix_post=m_norm_mix_post,
             norm_ffn_pre=m_norm_ffn_pre, w_ff1=m_w_ff1, w_ff2=m_w_ff2, norm_ffn_post=m_norm_ffn_post)
    v = dict(norm_mix_pre=v_norm_mix_pre, w_in=v_w_in, b_gate=v_b_gate, ln_v_g=v_ln_v_g, ln_v_b=v_ln_v_b, w_s=v_w_s,
             b_s=v_b_s, w_a_proj=v_w_a_proj, w_b_proj=v_w_b_proj, w_out=v_w_out, norm_mix_post=v_norm_mix_post,
             norm_ffn_pre=v_norm_ffn_pre, w_ff1=v_w_ff1, w_ff2=v_w_ff2, norm_ffn_post=v_norm_ffn_post)
    chip = 2 * lax.axis_index("x") + lax.axis_index("y")
    core = lax.axis_index("c")

    where = jnp.stack([chip, core]).astype(jnp.int32)
    wg_in = place_shard("place_w_in", w_in[0], where, BF16, 256)
    bg_all = place_shard("place_b_gate", jnp.pad(b_gate[0], ((0, 14), (0, 0))), where, F32, 16)
    vecs = (norm_mix_pre, ln_v_g, ln_v_b, norm_mix_post, norm_ffn_pre, norm_ffn_post)
    loss, dx, _, small, parts, got, packed, started = local_step(
        x[0], loss_target[0], vecs, w_s[0], b_s[0].T, bg_all, wg_in, [w[n][0] for n in BIG[1:]],
        core=jnp.reshape(core, (1,)).astype(jnp.int32),
        order=jnp.stack([chip, chip ^ 2, chip ^ 1, chip ^ 3]).astype(jnp.int32), where=where)

    halves = [sum_chips("sum_" + n, p, r, where, min(p.shape[1], 256), started[8])
              for n, p, r in zip(BIG[1:], parts, got)]
    joined, _ = join_halves(halves)
    grads = dict(zip(BIG[1:], joined))
    new = {}

    def update(n):
        shape = w[n].shape
        res = adamw("adamw_" + n, w[n][0], grads[n], m[n][0], v[n][0], min(shape[1], 256))
        new[n] = tuple(r.reshape(shape) for r in res)

    for n in BIG[1:]:
        update(n)
    p_in, got_in = scatter_wait(started, [new["w_ff2"][1], dx])
    (grads["w_in"],), first = join_halves([sum_chips("sum_w_in", p_in, got_in, where, 256)],
                                          pack_vector(small["norm_mix_pre"], where))
    update("w_in")
    small_new, loss = adamw_small(packed, first, jnp.reshape(chip, (1,)).astype(jnp.int32), w, m, v)
    new.update(small_new)

    outs = [loss, dx[None]]
    for i in range(4):
        outs += [new[n][i] for n in ORDER]
    return tuple(outs)
```

```python
import functools
import math
import typing

import numpy as np
import jax
import jax.numpy as jnp
from jax import lax
from jax.experimental import pallas as pl
from jax.experimental.pallas import tpu as pltpu

F32 = jnp.float32
BF16 = jnp.bfloat16
MESH = pl.DeviceIdType.MESH

D = 1024
EPS = 1e-6
CHUNK = 128
GROUPS = 8
HEADS = 16
HEAD_DIM = 64
ATT_T = 256
ATT_GROUP = 8
ATT_BWD_GROUP = 4
N_CHIPS = 4
D_FF = 4 * D
IN_COLS = 7 * D
IN_SHARD = IN_COLS // N_CHIPS
MASKED = -1e30
VMEM_LIMIT = 56 * 2 ** 20

ADAM_LR, ADAM_B1, ADAM_B2, ADAM_EPS, ADAM_WD, ADAM_STEP = 0.001, 0.9, 0.999, 1e-08, 0.01, 10

NN = (((1,), (0,)), ((), ()))
NT = (((1,), (1,)), ((), ()))
TN = (((0,), (0,)), ((), ()))


def _dot(a, b, dims=NN):
    return lax.dot_general(a, b, dims, preferred_element_type=F32)


def _params(*sem, communicates=False):
    return pltpu.CompilerParams(dimension_semantics=sem or None, vmem_limit_bytes=VMEM_LIMIT,
                                has_side_effects=communicates)


def _rows(tr, c, col=0):
    return pl.BlockSpec((tr, c), lambda i: (i, col))


def _full(shape):
    n = len(shape)
    return pl.BlockSpec(shape, lambda *_: (0,) * n)


def _gelu(x):
    k = math.sqrt(2.0 / math.pi)
    return 0.5 * x * (1.0 + jnp.tanh(k * (x + 0.044715 * x * x * x)))


def _gelu_and_grad(x):
    k = math.sqrt(2.0 / math.pi)
    t = jnp.tanh(k * (x + 0.044715 * x * x * x))
    g = 0.5 * x * (1.0 + t)
    dg = 0.5 * (1.0 + t) + 0.5 * x * (1.0 - t * t) * (k * (1.0 + 3.0 * 0.044715 * x * x))
    return g, dg


def _sigmoid(x):
    return 1.0 / (1.0 + jnp.exp(-x))


def _rms(x):
    r = lax.rsqrt(jnp.mean(x * x, axis=-1, keepdims=True) + EPS)
    return x * r, r


def _rms_bwd(dn, xhat, r):
    return r * (dn - xhat * jnp.mean(dn * xhat, axis=-1, keepdims=True))


def norm_pre(x, g):
    s = x.shape[0]
    tr = 512

    def body(x_ref, g_ref, h_ref):
        xhat, _ = _rms(x_ref[...])
        h_ref[...] = (xhat * g_ref[...]).astype(BF16)

    return pl.pallas_call(
        body, name="norm_pre", grid=(s // tr,),
        in_specs=[_rows(tr, D), _full((1, D))], out_specs=_rows(tr, D),
        out_shape=jax.ShapeDtypeStruct((s, D), BF16), compiler_params=_params("parallel"),
    )(x, g)


def mm_in(h, wg, order, staged, casting=()):
    s = h.shape[0]
    tm, tn = 1024, IN_SHARD // 2
    per = IN_SHARD // tn
    m = len(casting)
    nj, ni = N_CHIPS * per, s // tm
    cast_steps = per * ni

    def body(order_ref, *refs):
        a_ref = refs[0]
        cast_in = refs[2:2 + m]
        o_ref, held = refs[2 + m], refs[3 + m]
        cast_out = refs[4 + m:4 + 2 * m]
        tile, tile_sem = refs[4 + 2 * m:6 + 2 * m]
        sems = refs[6 + 2 * m:]
        j, i = pl.program_id(0), pl.program_id(1)

        @pl.when(j * ni + i < cast_steps)
        def _():
            for src, dst in zip(cast_in, cast_out):
                dst[...] = src[...].astype(BF16)

        def fetch(t):
            chip = order_ref[t // per]
            return pltpu.make_async_copy(held.at[chip, :, pl.ds((t % per) * tn, tn)], tile.at[t % 2],
                                         tile_sem.at[t % 2])

        if staged:
            near = _gather_phases([held], *sems[:2], [(0, D, (0, 1))])
            far = _relay_phases(held, *sems[2:])

        @pl.when(i == 0)
        def _():
            @pl.when(j == 0)
            def _():
                if staged:
                    near[0]()
                fetch(0).start()

            fetch(j).wait()
            ahead = j + 1 < nj
            if staged:
                ahead = ahead & (j + 1 != per) & (j + 1 != 3 * per)
            pl.when(ahead)(lambda: fetch(j + 1).start())

        rows = pl.ds(pl.multiple_of(i * tm, tm), tm)
        o_ref[...] = _dot(a_ref[rows, :], tile[j % 2]).astype(BF16)

        if staged:
            @pl.when((i == ni - 1) & (j == per - 1))
            def _():
                near[1]()
                near[2]()
                far[0]()
                fetch(per).start()

            @pl.when((i == ni - 1) & (j == 3 * per - 1))
            def _():
                far[1]()
                far[2]()
                fetch(3 * per).start()

    def cast_block(j, i, o):
        return jnp.minimum(j * ni + i, cast_steps - 1)

    out = pl.pallas_call(
        body, name="mm_in",
        grid_spec=pltpu.PrefetchScalarGridSpec(
            num_scalar_prefetch=1, grid=(nj, ni),
            in_specs=[pl.BlockSpec((s, D), lambda j, i, o: (0, 0)), ANY]
            + [pl.BlockSpec((a.shape[0] // cast_steps, a.shape[1]), lambda j, i, o: (cast_block(j, i, o), 0))
               for a in casting],
            out_specs=[pl.BlockSpec((tm, tn), lambda j, i, o: (i, o[j // per] * per + j % per)), ANY]
            + [pl.BlockSpec((None, a.shape[0] // cast_steps, a.shape[1]),
                            lambda j, i, o: (o[0], cast_block(j, i, o), 0)) for a in casting],
            scratch_shapes=[pltpu.VMEM((2, D, tn), BF16), pltpu.SemaphoreType.DMA((2,))]
            + (_gather_sems(1) + _relay_sems() if staged else [])),
        out_shape=[jax.ShapeDtypeStruct((s, IN_COLS), BF16), jax.ShapeDtypeStruct(wg.shape, wg.dtype)]
        + [jax.ShapeDtypeStruct((N_CHIPS,) + a.shape, BF16) for a in casting],
        input_output_aliases={2: 1},
        compiler_params=_params("arbitrary", "arbitrary", communicates=staged),
    )(order, h, wg, *casting)
    return out[0], out[1], out[2:]


def _tril_ws(ws_ref, g):
    r = lax.broadcasted_iota(jnp.int32, (CHUNK, CHUNK), 0)
    c = lax.broadcasted_iota(jnp.int32, (CHUNK, CHUNK), 1)
    return jnp.where(c <= r, ws_ref[g], 0.0).astype(BF16)


def _layer_norm(v):
    mu = jnp.mean(v, axis=-1, keepdims=True)
    d = v - mu
    rstd = lax.rsqrt(jnp.mean(d * d, axis=-1, keepdims=True) + EPS)
    return d * rstd, rstd


def gating_fwd(z, ln_g, ln_b, w_s, bs_t, gathering):
    s = z.shape[0]
    n = len(gathering)
    steps = s // CHUNK

    def body(*refs):
        u_ref, v_ref, lg_ref, lb_ref, ws_ref, bst_ref = refs[:6]
        ya_ref = refs[6 + n]
        ci = pl.program_id(0)
        if n:
            send, pass_on, finish = _gather_phases(refs[7 + n:7 + 2 * n], *refs[7 + 2 * n:], _spans(gathering))
            pl.when(ci == 0)(send)
        ug = _gelu(u_ref[...].astype(F32))
        vhat, _ = _layer_norm(_gelu(v_ref[...].astype(F32)))
        vn = (vhat * lg_ref[...] + lb_ref[...]).astype(BF16)
        for g in range(GROUPS):
            cols = slice(g * CHUNK, (g + 1) * CHUNK)
            mixed = _dot(_tril_ws(ws_ref, g), vn[:, cols]) + bst_ref[:, g:g + 1]
            ya_ref[:, cols] = (ug[:, cols] * mixed).astype(BF16)
        if n:
            pl.when(ci == steps - 1)(pass_on)
            pl.when(ci == steps - 1)(finish)

    out = pl.pallas_call(
        body, name="gating_fwd", grid=(steps,),
        in_specs=[_rows(CHUNK, D, 0), _rows(CHUNK, D, 1), _full((1, D)), _full((1, D)),
                  _full((GROUPS, CHUNK, CHUNK)), _full((CHUNK, GROUPS))] + [ANY] * n,
        out_specs=[_rows(CHUNK, D)] + [ANY] * n,
        out_shape=[jax.ShapeDtypeStruct((s, D), BF16)]
        + [jax.ShapeDtypeStruct(a.shape, a.dtype) for a in _arrays(gathering)],
        input_output_aliases={6 + w: 1 + w for w in range(n)},
        scratch_shapes=_gather_sems(n) if n else [],
        compiler_params=_params("arbitrary", communicates=bool(n)),
    )(z, z, ln_g, ln_b, w_s, bs_t, *_arrays(gathering))
    return out[0], out[1:]


def _attn_tables(s):
    nd = s // ATT_T
    r = np.arange(ATT_T)[None, :, None]
    c = np.arange(ATT_T)[None, None, :]
    delta = np.arange(nd)[:, None, None] * ATT_T + r - c
    count = np.zeros(delta.shape, np.int64)
    for window, dilation in ((128, 1), (512, 4), (2048, 16)):
        count += (delta >= 0) & (delta % dilation == 0) & (delta <= window)
    logc = np.where(count > 0, np.log(np.maximum(count, 1)), MASKED)
    return jnp.asarray(logc, F32)


AUG = 3


def _split3_np(x):
    terms, rest = [], np.asarray(x, np.float64)
    for _ in range(AUG):
        term = np.asarray(rest.astype(jnp.bfloat16), np.float64)
        terms.append(term)
        rest = rest - term
    return terms


def _split3(x):
    terms, rest = [], x
    for _ in range(AUG):
        term = rest.astype(BF16).astype(F32)
        terms.append(term)
        rest = rest - term
    return terms


def _alibi_tables(s):
    nb = s // ATT_T
    slopes = np.exp2(-8.0 * np.arange(1, HEADS + 1, dtype=np.float64) / HEADS)
    ka = np.zeros((HEADS // 2, 2, ATT_T, 128), np.float32)
    kb = np.zeros((HEADS // 2, 2, nb, 128), np.float32)
    for p in range(HEADS // 2):
        for e in range(2):
            base = HEAD_DIM * (1 - e)
            for a, term in enumerate(_split3_np(slopes[2 * p + e] * np.arange(ATT_T))):
                ka[p, e, :, base + a] = term
            for a, term in enumerate(_split3_np(slopes[2 * p + e] * ATT_T * np.arange(nb))):
                kb[p, e, :, base + AUG + a] = term
            ka[p, e, :, base + 2 * AUG:base + 3 * AUG] = 1.0
    return jnp.asarray(ka), jnp.asarray(kb)


def _head_masks():
    lane = lax.broadcasted_iota(jnp.int32, (1, 128), 1)
    first = lane < HEAD_DIM

    def ones(e, n):
        base = HEAD_DIM * (1 - e)
        return ((lane >= base) & (lane < base + n)).astype(F32)

    return first, lane, ones


def _place3(lane, at, terms, other):
    for a, term in enumerate(terms):
        other = jnp.where(lane == at + a, term, other)
    return other


def attn_fwd(z, logc, ka, kb, gathering):
    s = z.shape[0]
    nq = s // ATT_T
    t = ATT_T
    n = len(gathering)
    grp = ATT_GROUP
    ngrp = HEADS // 2 // grp
    wide = 128 * grp
    qcol, kcol, vcol = 2 * D // wide, 3 * D // wide, 4 * D // wide

    def body(*refs):
        q_ref, k_ref, v_ref, lc_ref, ka_ref, kb_ref = refs[:6]
        y_ref, lse_ref = refs[6 + n:8 + n]
        q_s, k_s, v_s, m_s, l_s, acc_s = refs[8 + 2 * n:14 + 2 * n]
        gi, qi = pl.program_id(0), pl.program_id(1)
        first, lane, ones = _head_masks()
        if n:
            send, pass_on, finish = _gather_phases(refs[8 + n:8 + 2 * n], *refs[14 + 2 * n:], _spans(gathering))
            pl.when((gi == 0) & (qi == 0))(send)

        @pl.when(qi == 0)
        def _():
            sel = jnp.broadcast_to(first.astype(F32), (t, 128))
            for pr in range(grp):
                cols = slice(pr * 128, (pr + 1) * 128)
                for jb in range(nq):
                    kj = k_ref[jb * t:(jb + 1) * t, cols].astype(F32)
                    vj = v_ref[jb * t:(jb + 1) * t, cols].astype(F32)
                    k_s[pr, 0, jb] = jnp.where(first, kj, ka_ref[pr, 0] + kb_ref[pr, 0, jb:jb + 1, :]).astype(BF16)
                    k_s[pr, 1, jb] = jnp.where(first, ka_ref[pr, 1] + kb_ref[pr, 1, jb:jb + 1, :], kj).astype(BF16)
                    v_s[pr, jb, 0:t, 0:128] = jnp.where(first, vj, 0.0).astype(BF16)
                    v_s[pr, jb, t:2 * t, 0:128] = jnp.where(first, 0.0, vj).astype(BF16)
                    v_s[pr, jb, 0:t, 128:256] = sel.astype(BF16)
                    v_s[pr, jb, t:2 * t, 128:256] = (1.0 - sel).astype(BF16)

        for pr in range(grp):
            q = q_ref[:, pr * 128:(pr + 1) * 128].astype(F32) * (1.0 / math.sqrt(HEAD_DIM))
            q_s[pr, 0] = jnp.where(first, q, ones(0, 2 * AUG)).astype(BF16)
            q_s[pr, 1] = jnp.where(first, ones(1, 2 * AUG), q).astype(BF16)
        m_s[...] = jnp.full_like(m_s, MASKED)
        l_s[...] = jnp.zeros_like(l_s)
        acc_s[...] = jnp.zeros_like(acc_s)

        def scores(j):
            return tuple(_dot(q_s[pr, e], k_s[pr, e, j], NT) for pr in range(grp) for e in range(2))

        def step(j, carry):
            softmax_block(j, scores(j))
            return carry

        def softmax_block(j, u):
            lc = lc_ref[qi - j]
            for pr in range(grp):
                u0 = u[2 * pr] + lc
                u1 = u[2 * pr + 1] + lc
                m0, m1 = m_s[pr, 0], m_s[pr, 1]
                n0 = jnp.maximum(m0, jnp.max(u0, axis=-1, keepdims=True))
                n1 = jnp.maximum(m1, jnp.max(u1, axis=-1, keepdims=True))
                m_s[pr, 0], m_s[pr, 1] = n0, n1
                p = jnp.concatenate([jnp.exp(u0 - jnp.concatenate([n0, n0], axis=1)).astype(BF16),
                                     jnp.exp(u1 - jnp.concatenate([n1, n1], axis=1)).astype(BF16)], axis=1)
                pv = _dot(p, v_s[pr, j])
                alpha = jnp.where(first, jnp.exp(m0 - n0), jnp.exp(m1 - n1))
                acc_s[pr] = acc_s[pr] * alpha + pv[:, 0:128]
                l_s[pr] = l_s[pr] * alpha + pv[:, 128:256]

        lax.fori_loop(0, qi + 1, step, 0)
        for pr in range(grp):
            cols = slice(pr * 128, (pr + 1) * 128)
            y_ref[:, cols] = (acc_s[pr] / l_s[pr]).astype(BF16)
            lse_ref[:, cols] = jnp.where(first, m_s[pr, 0], m_s[pr, 1]) + jnp.log(l_s[pr])
        if n:
            pl.when((gi == ngrp - 1) & (qi == nq - 1))(pass_on)
            pl.when((gi == ngrp - 1) & (qi == nq - 1))(finish)

    out = pl.pallas_call(
        body, name="attn_fwd", grid=(ngrp, nq),
        in_specs=[pl.BlockSpec((t, wide), lambda g, i: (i, qcol + g)),
                  pl.BlockSpec((s, wide), lambda g, i: (0, kcol + g)),
                  pl.BlockSpec((s, wide), lambda g, i: (0, vcol + g)),
                  _full((nq, t, t)),
                  pl.BlockSpec((grp, 2, t, 128), lambda g, i: (g, 0, 0, 0)),
                  pl.BlockSpec((grp, 2, nq, 128), lambda g, i: (g, 0, 0, 0))] + [ANY] * n,
        out_specs=[pl.BlockSpec((t, wide), lambda g, i: (i, g)), pl.BlockSpec((t, wide), lambda g, i: (i, g))]
        + [ANY] * n,
        out_shape=[jax.ShapeDtypeStruct((s, D), BF16), jax.ShapeDtypeStruct((s, D), F32)]
        + [jax.ShapeDtypeStruct(a.shape, a.dtype) for a in _arrays(gathering)],
        input_output_aliases={6 + w: 2 + w for w in range(n)},
        scratch_shapes=[pltpu.VMEM((grp, 2, t, 128), BF16), pltpu.VMEM((grp, 2, nq, t, 128), BF16),
                        pltpu.VMEM((grp, nq, 2 * t, 256), BF16), pltpu.VMEM((grp, 2, t, 128), F32),
                        pltpu.VMEM((grp, t, 128), F32), pltpu.VMEM((grp, t, 128), F32)]
        + (_gather_sems(n) if n else []),
        compiler_params=_params("arbitrary", "arbitrary", communicates=bool(n)),
    )(z, z, z, logc, ka, kb, *_arrays(gathering))
    return out[0], out[1], out[2:]


def proj_merge(ya, yb, wa, wb, z, bg, gathering):
    s = ya.shape[0]
    tm = 512
    n = len(gathering)
    steps = s // tm

    def body(*refs):
        ya_ref, yb_ref, wa_ref, wb_ref, ga_ref, gb_ref, bg_ref = refs[:7]
        mg_ref, pa_ref, pb_ref = refs[7 + n:10 + n]
        i = pl.program_id(0)
        if n:
            send, pass_on, finish = _gather_phases(refs[10 + n:10 + 2 * n], *refs[10 + 2 * n:], _spans(gathering))
            pl.when(i == 0)(send)
            pl.when(i == steps - 1)(pass_on)
        pa = _dot(ya_ref[...], wa_ref[...])
        pb = _dot(yb_ref[...], wb_ref[...])
        sa = _sigmoid(ga_ref[...] + bg_ref[0:1, :])
        sb = _sigmoid(gb_ref[...] + bg_ref[1:2, :])
        mg_ref[...] = (sa * pa + sb * pb).astype(BF16)
        pa_ref[...] = pa.astype(BF16)
        pb_ref[...] = pb.astype(BF16)
        if n:
            pl.when(i == steps - 1)(finish)

    out = jax.ShapeDtypeStruct((s, D), BF16)
    res = pl.pallas_call(
        body, name="proj_merge", grid=(steps,),
        in_specs=[_rows(tm, D), _rows(tm, D), _full((D, D)), _full((D, D)),
                  _rows(tm, D, 5), _rows(tm, D, 6), _full((2, D))] + [ANY] * n,
        out_specs=[_rows(tm, D)] * 3 + [ANY] * n,
        out_shape=[out] * 3 + [jax.ShapeDtypeStruct(a.shape, a.dtype) for a in _arrays(gathering)],
        input_output_aliases={7 + w: 3 + w for w in range(n)},
        scratch_shapes=_gather_sems(n) if n else [],
        compiler_params=_params("arbitrary", communicates=bool(n)),
    )(ya, yb, wa, wb, z, z, bg, *_arrays(gathering))
    return res[0], res[1], res[2], res[3:]


def out_norm(merged, w_out, x, g_post, g_fpre, gathering):
    s = x.shape[0]
    tm = 512
    n = len(gathering)
    steps = s // tm

    def body(*refs):
        mg_ref, w_ref, x_ref, gp_ref, gf_ref = refs[:5]
        o_ref, x1_ref, h2_ref = refs[5 + n:8 + n]
        i = pl.program_id(0)
        if n:
            send, pass_on, finish = _gather_phases(refs[8 + n:8 + 2 * n], *refs[8 + 2 * n:], _spans(gathering))
            pl.when(i == 0)(send)
            pl.when(i == steps - 1)(pass_on)
        o = _dot(mg_ref[...], w_ref[...])
        ohat, _ = _rms(o)
        x1 = x_ref[...] + ohat * gp_ref[...]
        x1hat, _ = _rms(x1)
        o_ref[...] = o
        x1_ref[...] = x1
        h2_ref[...] = (x1hat * gf_ref[...]).astype(BF16)
        if n:
            pl.when(i == steps - 1)(finish)

    res = pl.pallas_call(
        body, name="out_norm", grid=(steps,),
        in_specs=[_rows(tm, D), _full((D, D)), _rows(tm, D), _full((1, D)), _full((1, D))] + [ANY] * n,
        out_specs=[_rows(tm, D)] * 3 + [ANY] * n,
        out_shape=[jax.ShapeDtypeStruct((s, D), F32), jax.ShapeDtypeStruct((s, D), F32),
                   jax.ShapeDtypeStruct((s, D), BF16)]
        + [jax.ShapeDtypeStruct(a.shape, a.dtype) for a in _arrays(gathering)],
        input_output_aliases={5 + w: 3 + w for w in range(n)},
        scratch_shapes=_gather_sems(n) if n else [],
        compiler_params=_params("arbitrary", communicates=bool(n)),
    )(merged, w_out, x, g_post, g_fpre, *_arrays(gathering))
    return res[0], res[1], res[2], res[3:]


def mm_ff1(h2, wg, gathering):
    s = h2.shape[0]
    tm = 1024
    n = len(gathering)
    ni = s // tm

    def body(*refs):
        a_ref, b_ref = refs[:2]
        o_ref, r_ref = refs[2 + n:4 + n]
        i, j = pl.program_id(0), pl.program_id(1)
        if n:
            send, pass_on, finish = _gather_phases(refs[4 + n:4 + 2 * n], *refs[4 + 2 * n:], _spans(gathering))
            pl.when((i == 0) & (j == 0))(send)
            pl.when((i == ni - 1) & (j == N_CHIPS // 2))(pass_on)
        a = _dot(a_ref[...], b_ref[...])
        o_ref[...] = a.astype(BF16)
        r = jnp.maximum(a, 0.0)
        r_ref[...] = (r * r).astype(BF16)
        if n:
            pl.when((i == ni - 1) & (j == N_CHIPS - 1))(finish)

    res = pl.pallas_call(
        body, name="mm_ff1", grid=(ni, N_CHIPS),
        in_specs=[pl.BlockSpec((tm, D), lambda i, j: (i, 0)), pl.BlockSpec((None, D, D), lambda i, j: (j, 0, 0))]
        + [ANY] * n,
        out_specs=[pl.BlockSpec((tm, D), lambda i, j: (i, j))] * 2 + [ANY] * n,
        out_shape=[jax.ShapeDtypeStruct((s, D_FF), BF16), jax.ShapeDtypeStruct((s, D_FF), BF16)]
        + [jax.ShapeDtypeStruct(a.shape, a.dtype) for a in _arrays(gathering)],
        input_output_aliases={2 + w: 2 + w for w in range(n)},
        scratch_shapes=_gather_sems(n) if n else [],
        compiler_params=_params("arbitrary", "arbitrary", communicates=bool(n)),
    )(h2, wg, *_arrays(gathering))
    return res[0], res[1], res[2:]


def ff2_loss(rl, w_ff2, x1, target, g_fpost):
    s = x1.shape[0]
    tm = 256

    def body(rl_ref, w_ref, x1_ref, t_ref, g_ref, dy_ref, df_ref, dg_ref, loss_ref):
        @pl.when(pl.program_id(0) == 0)
        def _():
            dg_ref[...] = jnp.zeros_like(dg_ref)
            loss_ref[...] = jnp.zeros_like(loss_ref)

        f = _dot(rl_ref[...], w_ref[...])
        fhat, r = _rms(f)
        err = x1_ref[...] + fhat * g_ref[...] - t_ref[...]
        loss_ref[...] += 0.5 * jnp.sum(jnp.mean(err * err, axis=-1, keepdims=True), axis=0, keepdims=True)
        dy = err * (1.0 / D)
        dy_ref[...] = dy
        dg_ref[...] += jnp.sum(dy * fhat, axis=0, keepdims=True)
        df_ref[...] = _rms_bwd(dy * g_ref[...], fhat, r).astype(BF16)

    return pl.pallas_call(
        body, name="ff2_loss", grid=(s // tm,),
        in_specs=[_rows(tm, D_FF), _full((D_FF, D)), _rows(tm, D), _rows(tm, D), _full((1, D))],
        out_specs=[_rows(tm, D), _rows(tm, D), _full((1, D)), _full((1, 1))],
        out_shape=[jax.ShapeDtypeStruct((s, D), F32), jax.ShapeDtypeStruct((s, D), BF16),
                   jax.ShapeDtypeStruct((1, D), F32), jax.ShapeDtypeStruct((1, 1), F32)],
        compiler_params=_params("arbitrary"),
    )(rl, w_ff2, x1, target, g_fpost)


def mm_tn(name, a, b, ta, tb, out_shape, out_spec):
    s = a.shape[0]

    def body(a_ref, b_ref, o_ref):
        o_ref[...] = _dot(a_ref[...], b_ref[...], TN)

    return pl.pallas_call(
        body, name=name, grid=(a.shape[1] // ta, b.shape[1] // tb),
        in_specs=[pl.BlockSpec((s, ta), lambda i, j: (0, i)), pl.BlockSpec((s, tb), lambda i, j: (0, j))],
        out_specs=out_spec, out_shape=jax.ShapeDtypeStruct(out_shape, F32),
        compiler_params=_params("parallel", "parallel"),
    )(a, b)


def mm_nt(name, a, w):
    s = a.shape[0]
    tm = 512

    def body(a_ref, w_ref, o_ref):
        o_ref[...] = _dot(a_ref[...], w_ref[...], NT).astype(BF16)

    return pl.pallas_call(
        body, name=name, grid=(s // tm,), in_specs=[_rows(tm, D), _full((D, D))], out_specs=_rows(tm, D),
        out_shape=jax.ShapeDtypeStruct((s, D), BF16), compiler_params=_params("parallel"),
    )(a, w)


def ff2_bwd(df, w_ff2, a):
    s = df.shape[0]
    tm = 1024

    def body(df_ref, w_ref, a_ref, da_ref):
        drl = _dot(df_ref[...], w_ref[...], NT)
        da_ref[...] = (drl * (2.0 * jnp.maximum(a_ref[...].astype(F32), 0.0))).astype(BF16)

    return pl.pallas_call(
        body, name="ff2_bwd", grid=(s // tm, D_FF // D),
        in_specs=[pl.BlockSpec((tm, D), lambda i, j: (i, 0)), pl.BlockSpec((D, D), lambda i, j: (j, 0)),
                  pl.BlockSpec((tm, D), lambda i, j: (i, j))],
        out_specs=pl.BlockSpec((tm, D), lambda i, j: (i, j)),
        out_shape=jax.ShapeDtypeStruct((s, D_FF), BF16), compiler_params=_params("parallel", "parallel"),
    )(df, w_ff2, a)


def ff1_bwd_norms(da, wg, x1, o, dy, g_fpre, g_post, swapping):
    s = x1.shape[0]
    tm = 256
    n = len(swapping)
    steps = s // tm

    def body(*refs):
        da_ref, w_ref, x1_ref, o_ref, dy_ref, gf_ref, gp_ref = refs[:7]
        dx1_ref, do_ref, dgf_ref, dgp_ref = refs[7 + n:11 + n]
        i = pl.program_id(0)
        if n:
            send, finish = _swap_phases(refs[7:7 + n], refs[11 + n:11 + 2 * n], *refs[11 + 2 * n:])
            pl.when(i == 0)(send)

        @pl.when(i == 0)
        def _():
            dgf_ref[...] = jnp.zeros_like(dgf_ref)
            dgp_ref[...] = jnp.zeros_like(dgp_ref)

        dh2 = _dot(da_ref[:, 0:D], w_ref[0], NT)
        for k in range(1, N_CHIPS):
            dh2 = dh2 + _dot(da_ref[:, k * D:(k + 1) * D], w_ref[k], NT)
        x1hat, r2 = _rms(x1_ref[...])
        dgf_ref[...] += jnp.sum(dh2 * x1hat, axis=0, keepdims=True)
        dx1 = dy_ref[...] + _rms_bwd(dh2 * gf_ref[...], x1hat, r2)
        ohat, r1 = _rms(o_ref[...])
        dgp_ref[...] += jnp.sum(dx1 * ohat, axis=0, keepdims=True)
        dx1_ref[...] = dx1
        do_ref[...] = _rms_bwd(dx1 * gp_ref[...], ohat, r1).astype(BF16)
        if n:
            pl.when(i == steps - 1)(finish)

    res = pl.pallas_call(
        body, name="ff1_bwd_norms", grid=(steps,),
        in_specs=[_rows(tm, D_FF), _full((N_CHIPS, D, D)), _rows(tm, D), _rows(tm, D), _rows(tm, D),
                  _full((1, D)), _full((1, D))] + [ANY] * n,
        out_specs=[_rows(tm, D), _rows(tm, D), _full((1, D)), _full((1, D))] + [ANY] * n,
        out_shape=[jax.ShapeDtypeStruct((s, D), F32), jax.ShapeDtypeStruct((s, D), BF16),
                   jax.ShapeDtypeStruct((1, D), F32), jax.ShapeDtypeStruct((1, D), F32)] + _swap_shapes(swapping),
        scratch_shapes=_swap_sems(n) if n else [],
        compiler_params=_params("arbitrary", communicates=bool(n)),
    )(da, wg, x1, o, dy, g_fpre, g_post, *swapping)
    return res[0], res[1], res[2], res[3], res[4:]


def out_bwd_gates(do, w_out, pa, pb, z, bg):
    s = do.shape[0]
    tm = 512

    def body(do_ref, w_ref, pa_ref, pb_ref, ga_ref, gb_ref, bg_ref, dpa_ref, dpb_ref, dga_ref, dgb_ref, dbg_ref):
        @pl.when(pl.program_id(0) == 0)
        def _():
            dbg_ref[...] = jnp.zeros_like(dbg_ref)

        dm = _dot(do_ref[...], w_ref[...], NT)
        sa = _sigmoid(ga_ref[...] + bg_ref[0:1, :])
        sb = _sigmoid(gb_ref[...] + bg_ref[1:2, :])
        dpa_ref[...] = (dm * sa).astype(BF16)
        dpb_ref[...] = (dm * sb).astype(BF16)
        dga = dm * pa_ref[...].astype(F32) * (sa * (1.0 - sa))
        dgb = dm * pb_ref[...].astype(F32) * (sb * (1.0 - sb))
        dga_ref[...] = dga.astype(BF16)
        dgb_ref[...] = dgb.astype(BF16)
        dbg_ref[0:1, :] += jnp.sum(dga, axis=0, keepdims=True)
        dbg_ref[1:2, :] += jnp.sum(dgb, axis=0, keepdims=True)

    out = jax.ShapeDtypeStruct((s, D), BF16)
    return pl.pallas_call(
        body, name="out_bwd_gates", grid=(s // tm,),
        in_specs=[_rows(tm, D), _full((D, D)), _rows(tm, D), _rows(tm, D), _rows(tm, D, 5), _rows(tm, D, 6),
                  _full((2, D))],
        out_specs=[_rows(tm, D)] * 4 + [_full((2, D))],
        out_shape=[out] * 4 + [jax.ShapeDtypeStruct((2, D), F32)], compiler_params=_params("arbitrary"),
    )(do, w_out, pa, pb, z, z, bg)


def gating_bwd(z, dya, ln_g, ln_b, w_s, bs_t, swapping):
    s = z.shape[0]
    ones = functools.partial(jnp.ones, (8, CHUNK), BF16)
    n = len(swapping)

    def body(*refs):
        u_ref, v_ref, dya_ref, lg_ref, lb_ref, ws_ref, bst_ref = refs[:7]
        du_ref, dv_ref, dws_ref, dbs_ref, dlg_ref, dlb_ref = refs[7 + n:13 + n]
        dvn_ref = refs[13 + 2 * n]
        ci = pl.program_id(0)
        if n:
            send, finish = _swap_phases(refs[7:7 + n], refs[13 + n:13 + 2 * n], *refs[14 + 2 * n:])
            pl.when(ci == 0)(send)

        @pl.when(ci == 0)
        def _():
            dws_ref[...] = jnp.zeros_like(dws_ref)
            dbs_ref[...] = jnp.zeros_like(dbs_ref)
            dlg_ref[...] = jnp.zeros_like(dlg_ref)
            dlb_ref[...] = jnp.zeros_like(dlb_ref)

        ug, dug_du = _gelu_and_grad(u_ref[...].astype(F32))
        vg, dvg_dv = _gelu_and_grad(v_ref[...].astype(F32))
        vhat, rstd = _layer_norm(vg)
        vn = (vhat * lg_ref[...] + lb_ref[...]).astype(BF16)
        dya = dya_ref[...].astype(F32)
        for g in range(GROUPS):
            cols = slice(g * CHUNK, (g + 1) * CHUNK)
            ws = _tril_ws(ws_ref, g)
            mixed = _dot(ws, vn[:, cols]) + bst_ref[:, g:g + 1]
            du_ref[:, cols] = (dya[:, cols] * mixed * dug_du[:, cols]).astype(BF16)
            dmix = (dya[:, cols] * ug[:, cols]).astype(BF16)
            dbs_ref[g] += _dot(ones(), dmix, NT)
            dws_ref[g] += _dot(dmix, vn[:, cols], NT)
            dvn_ref[:, cols] = _dot(ws, dmix, TN)
        dvn = dvn_ref[...]
        dlg_ref[...] += jnp.sum(dvn * vhat, axis=0, keepdims=True)
        dlb_ref[...] += jnp.sum(dvn, axis=0, keepdims=True)
        dvh = dvn * lg_ref[...]
        dvg = rstd * (dvh - jnp.mean(dvh, axis=-1, keepdims=True)
                      - vhat * jnp.mean(dvh * vhat, axis=-1, keepdims=True))
        dv_ref[...] = (dvg * dvg_dv).astype(BF16)

        @pl.when(ci == pl.num_programs(0) - 1)
        def _():
            r = lax.broadcasted_iota(jnp.int32, (CHUNK, CHUNK), 0)
            c = lax.broadcasted_iota(jnp.int32, (CHUNK, CHUNK), 1)
            for g in range(GROUPS):
                dws_ref[g] = jnp.where(c <= r, dws_ref[g], 0.0)

        if n:
            pl.when(ci == pl.num_programs(0) - 1)(finish)

    out = jax.ShapeDtypeStruct((s, D), BF16)
    res = pl.pallas_call(
        body, name="gating_bwd", grid=(s // CHUNK,),
        in_specs=[_rows(CHUNK, D, 0), _rows(CHUNK, D, 1), _rows(CHUNK, D), _full((1, D)), _full((1, D)),
                  _full((GROUPS, CHUNK, CHUNK)), _full((CHUNK, GROUPS))] + [ANY] * n,
        out_specs=[_rows(CHUNK, D), _rows(CHUNK, D), _full((GROUPS, CHUNK, CHUNK)), _full((GROUPS, 8, CHUNK)),
                   _full((1, D)), _full((1, D))] + [ANY] * n,
        out_shape=[out, out, jax.ShapeDtypeStruct((GROUPS, CHUNK, CHUNK), F32),
                   jax.ShapeDtypeStruct((GROUPS, 8, CHUNK), F32),
                   jax.ShapeDtypeStruct((1, D), F32), jax.ShapeDtypeStruct((1, D), F32)] + _swap_shapes(swapping),
        scratch_shapes=[pltpu.VMEM((CHUNK, D), F32)] + (_swap_sems(n) if n else []),
        compiler_params=_params("arbitrary", communicates=bool(n)),
    )(z, z, dya, ln_g, ln_b, w_s, bs_t, *swapping)
    return (*res[:6], res[6:])


def attn_bwd(z, yb, dyb, lse, logc, ka, kb, scattering, gathering=None):
    s = z.shape[0]
    nq = s // ATT_T
    t = ATT_T
    grp = ATT_BWD_GROUP
    ngrp = HEADS // 2 // grp
    wide = 128 * grp
    qcol, kcol, vcol = 2 * D // wide, 3 * D // wide, 4 * D // wide
    scale = 1.0 / math.sqrt(HEAD_DIM)
    n = len(scattering)
    g8 = 0 if gathering is None else 1

    def body(*refs):
        q_ref, k_ref, v_ref, y_ref, dy_ref, lse_ref, lc_ref, ka_ref, kb_ref = refs[:9]
        dq_ref, dk_ref, dv_ref = refs[9 + n + g8:12 + n + g8]
        qa_s, qt_s, da_s, dt_s, dq_s, dkt_s, dvt_s = refs[12 + 2 * n + 2 * g8:19 + 2 * n + 2 * g8]
        sems = refs[19 + 2 * n + 2 * g8:]
        gi, j = pl.program_id(0), pl.program_id(1)
        first, lane, ones = _head_masks()
        if n:
            send, finish = _scatter_phases(refs[9:9 + n], refs[12 + n + g8:12 + 2 * n + g8], *sems[:2])
            pl.when((gi == 0) & (j == 0))(send)
        if g8:
            send8, pass_on8, finish8 = _allgather8_phases(refs[12 + 2 * n + g8], *sems[2 * (n > 0):])
            pl.when((gi == 0) & (j == 0))(send8)
            pl.when((gi == ngrp - 1) & (j == nq - 1))(pass_on8)

        @pl.when(j == 0)
        def _():
            dq_s[...] = jnp.zeros_like(dq_s)
            for pr in range(grp):
                cols = slice(pr * 128, (pr + 1) * 128)
                for ib in range(nq):
                    rows = slice(ib * t, (ib + 1) * t)
                    q = q_ref[rows, cols].astype(F32) * scale
                    lse = lse_ref[rows, cols]
                    qa_s[pr, 0, ib] = jnp.where(first, q, _place3(lane, HEAD_DIM + 2 * AUG, _split3(-lse[:, 0:1]),
                                                                  ones(0, 2 * AUG))).astype(BF16)
                    qa_s[pr, 1, ib] = jnp.where(
                        first, _place3(lane, 2 * AUG, _split3(-lse[:, HEAD_DIM:HEAD_DIM + 1]), ones(1, 2 * AUG)),
                        q).astype(BF16)
                    qt_s[pr, ib, :, 0:t] = jnp.where(first, q, 0.0).T.astype(BF16)
                    qt_s[pr, ib, :, t:2 * t] = jnp.where(first, 0.0, q).T.astype(BF16)
                    do = dy_ref[rows, cols].astype(F32)
                    prod = do * y_ref[rows, cols].astype(F32)
                    dd0 = jnp.sum(jnp.where(first, prod, 0.0), axis=-1, keepdims=True)
                    dd1 = jnp.sum(jnp.where(first, 0.0, prod), axis=-1, keepdims=True)
                    da_s[pr, 0, ib] = jnp.where(first, do, _place3(lane, HEAD_DIM, _split3(-dd0), 0.0)).astype(BF16)
                    da_s[pr, 1, ib] = jnp.where(first, _place3(lane, 0, _split3(-dd1), 0.0), do).astype(BF16)
                    dt_s[pr, ib, :, 0:t] = jnp.where(first, do, 0.0).T.astype(BF16)
                    dt_s[pr, ib, :, t:2 * t] = jnp.where(first, 0.0, do).T.astype(BF16)

        keys = []
        for pr in range(grp):
            kj = k_ref[:, pr * 128:(pr + 1) * 128].astype(F32)
            vj = v_ref[:, pr * 128:(pr + 1) * 128].astype(F32)
            keys.append((
                jnp.where(first, kj, ka_ref[pr, 0] + kb_ref[pr, 0, pl.ds(j, 1), :]).astype(BF16),
                jnp.where(first, ka_ref[pr, 1] + kb_ref[pr, 1, pl.ds(j, 1), :], kj).astype(BF16),
                jnp.concatenate([jnp.where(first, kj, 0.0), jnp.where(first, 0.0, kj)], axis=0).astype(BF16),
                jnp.where(first, vj, ones(0, AUG)).astype(BF16),
                jnp.where(first, ones(1, AUG), vj).astype(BF16)))
        dkt_s[...] = jnp.zeros_like(dkt_s)
        dvt_s[...] = jnp.zeros_like(dvt_s)

        def step(i, _):
            lc = lc_ref[i - j]
            rows = pl.ds(pl.multiple_of(i * t, t), t)
            for pr in range(grp):
                k0a, k1a, kst, v0a, v1a = keys[pr]
                p0 = jnp.exp(_dot(qa_s[pr, 0, i], k0a, NT) + lc)
                p1 = jnp.exp(_dot(qa_s[pr, 1, i], k1a, NT) + lc)
                e0 = (p0 * _dot(da_s[pr, 0, i], v0a, NT)).astype(BF16)
                e1 = (p1 * _dot(da_s[pr, 1, i], v1a, NT)).astype(BF16)
                dq_s[pr, rows, :] += _dot(jnp.concatenate([e0, e1], axis=1), kst)
                dvt_s[pr] += _dot(dt_s[pr, i], jnp.concatenate([p0.astype(BF16), p1.astype(BF16)], axis=0))
                dkt_s[pr] += _dot(qt_s[pr, i], jnp.concatenate([e0, e1], axis=0))
            return 0

        lax.fori_loop(j, nq, step, 0)
        for pr in range(grp):
            dk_ref[:, pr * 128:(pr + 1) * 128] = dkt_s[pr].T.astype(BF16)
            dv_ref[:, pr * 128:(pr + 1) * 128] = dvt_s[pr].T.astype(BF16)

        @pl.when(j == nq - 1)
        def _():
            for pr in range(grp):
                dq_ref[:, pr * 128:(pr + 1) * 128] = (dq_s[pr] * scale).astype(BF16)

        if n:
            pl.when((gi == ngrp - 1) & (j == nq - 1))(finish)
        if g8:
            pl.when((gi == ngrp - 1) & (j == nq - 1))(finish8)

    colblock = lambda c: pl.BlockSpec((s, wide), lambda g, j: (0, c + g))
    once = lambda c: pl.BlockSpec((s, wide), lambda g, j: (0, c + g), pipeline_mode=pl.Buffered(1))
    blk = lambda c: pl.BlockSpec((t, wide), lambda g, j: (j, c + g))
    out = jax.ShapeDtypeStruct((s, D), BF16)
    res = pl.pallas_call(
        body, name="attn_bwd", grid=(ngrp, nq),
        in_specs=[once(qcol), blk(kcol), blk(vcol), once(0), once(0), once(0),
                  pl.BlockSpec((nq, t, t), lambda g, j: (0, 0, 0), pipeline_mode=pl.Buffered(1)),
                  pl.BlockSpec((grp, 2, t, 128), lambda g, j: (g, 0, 0, 0)),
                  pl.BlockSpec((grp, 2, nq, 128), lambda g, j: (g, 0, 0, 0))] + [ANY] * (n + g8),
        out_specs=[colblock(0), blk(0), blk(0)] + [ANY] * (n + g8),
        out_shape=[out] * 3 + _scatter_shapes(scattering)
        + ([jax.ShapeDtypeStruct(gathering.shape, gathering.dtype)] if g8 else []),
        input_output_aliases={9 + n: 3 + n} if g8 else {},
        scratch_shapes=[pltpu.VMEM((grp, 2, nq, t, 128), BF16), pltpu.VMEM((grp, nq, 128, 2 * t), BF16),
                        pltpu.VMEM((grp, 2, nq, t, 128), BF16), pltpu.VMEM((grp, nq, 128, 2 * t), BF16),
                        pltpu.VMEM((grp, s, 128), F32), pltpu.VMEM((grp, 128, t), F32),
                        pltpu.VMEM((grp, 128, t), F32)]
        + (_scatter_sems(n) if n else [])
        + ([pltpu.SemaphoreType.DMA((7,)), pltpu.SemaphoreType.DMA((7,))] if g8 else []),
        compiler_params=_params("arbitrary", "arbitrary", communicates=bool(n + g8)),
    )(z, z, z, yb, dyb, lse, logc, ka, kb, *scattering, *([gathering] if g8 else []))
    return res[0], res[1], res[2], res[3:3 + n], (res[3 + n] if g8 else None)


def in_bwd_norm(dz, wg, x, dx1, g_pre, scattering, gathering=None):
    s = x.shape[0]
    tm = 512
    n = len(scattering)
    g = 0 if gathering is None else 1
    last = (s // tm - 1, N_CHIPS - 1)

    def body(*refs):
        dz_ref, w_ref, x_ref, dx1_ref, g_ref = refs[:5]
        dx_ref, dg_ref = refs[5 + n + g:7 + n + g]
        acc_ref = refs[7 + 2 * n + 2 * g]
        sems = refs[8 + 2 * n + 2 * g:]
        i, k = pl.program_id(0), pl.program_id(1)
        if n:
            send, finish = _scatter_phases(refs[5:5 + n], refs[7 + n + g:7 + 2 * n + g], *sems[:2])
            pl.when((i == 0) & (k == 0))(send)
        if g:
            send8, pass_on8, finish8 = _allgather8_phases(refs[7 + 2 * n + g], *sems[2 * (n > 0):])
            pl.when((i == 0) & (k == 0))(send8)
            pl.when((i == last[0]) & (k == last[1]))(pass_on8)

        @pl.when((i == 0) & (k == 0))
        def _():
            dg_ref[...] = jnp.zeros_like(dg_ref)

        part = _dot(dz_ref[...], w_ref[...], NT)

        @pl.when(k == 0)
        def _():
            acc_ref[...] = part

        @pl.when(k > 0)
        def _():
            acc_ref[...] += part

        @pl.when(k == N_CHIPS - 1)
        def _():
            dh = acc_ref[...]
            xhat, r = _rms(x_ref[...])
            dg_ref[...] += jnp.sum(dh * xhat, axis=0, keepdims=True)
            dx_ref[...] = dx1_ref[...] + _rms_bwd(dh * g_ref[...], xhat, r)

        if n:
            pl.when((i == last[0]) & (k == last[1]))(finish)
        if g:
            pl.when((i == last[0]) & (k == last[1]))(finish8)

    row = pl.BlockSpec((tm, D), lambda i, k: (i, 0))
    vec = pl.BlockSpec((1, D), lambda i, k: (0, 0))
    res = pl.pallas_call(
        body, name="in_bwd_norm", grid=(s // tm, N_CHIPS),
        in_specs=[pl.BlockSpec((tm, IN_SHARD), lambda i, k: (i, k)),
                  pl.BlockSpec((None, D, IN_SHARD), lambda i, k: (k, 0, 0)), row, row, vec] + [ANY] * (n + g),
        out_specs=[row, vec] + [ANY] * (n + g),
        out_shape=[jax.ShapeDtypeStruct((s, D), F32), jax.ShapeDtypeStruct((1, D), F32)]
        + _scatter_shapes(scattering) + ([jax.ShapeDtypeStruct(gathering.shape, gathering.dtype)] if g else []),
        input_output_aliases={5 + n: 2 + n} if g else {},
        scratch_shapes=[pltpu.VMEM((tm, D), F32)] + (_scatter_sems(n) if n else [])
        + ([pltpu.SemaphoreType.DMA((7,)), pltpu.SemaphoreType.DMA((7,))] if g else []),
        compiler_params=_params("arbitrary", "arbitrary", communicates=bool(n + g)),
    )(dz, wg, x, dx1, g_pre, *scattering, *([gathering] if g else []))
    return res[0], res[1], res[2:2 + n], (res[2 + n] if g else None)


def _adamw_math(w, g, m, v):
    m = ADAM_B1 * m + (1.0 - ADAM_B1) * g
    v = ADAM_B2 * v + (1.0 - ADAM_B2) * (g * g)
    m_hat = m / (1.0 - ADAM_B1 ** ADAM_STEP)
    v_hat = v / (1.0 - ADAM_B2 ** ADAM_STEP)
    delta = -ADAM_LR * (m_hat / (jnp.sqrt(v_hat) + ADAM_EPS) + ADAM_WD * w)
    return delta, m, v


def adamw(name, w, g, m, v, tr):
    r, c = w.shape

    def body(w_ref, g_ref, m_ref, v_ref, go_ref, d_ref, nm_ref, nv_ref):
        g = g_ref[...]
        go_ref[...] = g
        d_ref[...], nm_ref[...], nv_ref[...] = _adamw_math(w_ref[...], g, m_ref[...], v_ref[...])

    out = jax.ShapeDtypeStruct((r, c), F32)
    return pl.pallas_call(
        body, name=name, grid=(r // tr,), in_specs=[_rows(tr, c)] * 4, out_specs=[_rows(tr, c)] * 4,
        out_shape=[out] * 4, compiler_params=_params("parallel"),
    )(w, g, m, v)


def _allgather8_phases(buf, send_sems, recv_sems):
    x, y, c, chips = _place()
    me = 2 * x + y
    sibling = (x, y, 1 - c)
    rows = buf.shape[1] // 2

    def part(chip, core):
        return buf.at[chip, pl.ds(core * rows, rows)]

    def copy(k, block, to):
        return pltpu.make_async_remote_copy(src_ref=block, dst_ref=block, send_sem=send_sems.at[k],
                                            recv_sem=recv_sems.at[k], device_id=to, device_id_type=MESH)

    def chip_of(j):
        return 2 * chips[j][0] + chips[j][1]

    def send():
        copy(0, part(me, c), sibling).start()
        for j in range(3):
            copy(1 + j, part(me, c), (chips[j][0], chips[j][1], c)).start()

    def pass_on():
        for j in range(3):
            copy(1 + j, part(chip_of(j), c), (chips[j][0], chips[j][1], c)).wait_recv()
            copy(4 + j, part(chip_of(j), c), sibling).start()

    def finish():
        copy(0, part(me, 1 - c), sibling).wait_recv()
        for j in range(3):
            copy(4 + j, part(chip_of(j), 1 - c), sibling).wait_recv()
        copy(0, part(me, c), sibling).wait_send()
        for j in range(3):
            copy(1 + j, part(me, c), (chips[j][0], chips[j][1], c)).wait_send()
            copy(4 + j, part(chip_of(j), c), sibling).wait_send()

    return send, pass_on, finish


def add_halves(name, g, recv, c_idx, tr):
    n, h, c = recv.shape

    def body(c_ref, g_ref, r_ref, o_ref):
        o_ref[...] = (g_ref[...] + r_ref[...]).astype(BF16)

    nb = h // tr
    return pl.pallas_call(
        body, name=name,
        grid_spec=pltpu.PrefetchScalarGridSpec(
            num_scalar_prefetch=1, grid=(n, nb),
            in_specs=[pl.BlockSpec((None, tr, c), lambda k, i, c_ref: (k, c_ref[0] * nb + i, 0)),
                      pl.BlockSpec((None, tr, c), lambda k, i, c_ref: (k, i, 0))],
            out_specs=pl.BlockSpec((None, tr, c), lambda k, i, c_ref: (k, i, 0))),
        out_shape=jax.ShapeDtypeStruct((n, h, c), BF16), compiler_params=_params("parallel", "parallel"),
    )(c_idx, g, recv)


def sum_chips(name, parts, recv, where, tr, after=None):
    n, h, c = recv.shape
    nb = h // tr

    def body(w_ref, p_ref, r_ref, *rest):
        acc = p_ref[...].astype(F32)
        for k in range(n):
            acc = acc + r_ref[k].astype(F32)
        rest[-1][...] = acc

    return pl.pallas_call(
        body, name=name,
        grid_spec=pltpu.PrefetchScalarGridSpec(
            num_scalar_prefetch=1, grid=(nb,),
            in_specs=[pl.BlockSpec((None, tr, c), lambda i, w_ref: (w_ref[0], i, 0)),
                      pl.BlockSpec((n, tr, c), lambda i, w_ref: (0, i, 0))] + ([ANY] if after is not None else []),
            out_specs=pl.BlockSpec((tr, c), lambda i, w_ref: (w_ref[1] * nb + i, 0))),
        out_shape=jax.ShapeDtypeStruct((2 * h, c), F32), compiler_params=_params("parallel"),
    )(where, parts, recv, *([after] if after is not None else []))


def place_shard(name, shard, where, dtype, tr):
    r, c = shard.shape

    def body(w_ref, s_ref, o_ref):
        o_ref[...] = s_ref[...].astype(dtype)

    return pl.pallas_call(
        body, name=name,
        grid_spec=pltpu.PrefetchScalarGridSpec(
            num_scalar_prefetch=1, grid=(r // tr,),
            in_specs=[pl.BlockSpec((tr, c), lambda i, w_ref: (i, 0))],
            out_specs=pl.BlockSpec((None, tr, c), lambda i, w_ref: (w_ref[0], i, 0))),
        out_shape=jax.ShapeDtypeStruct((N_CHIPS, r, c), dtype), compiler_params=_params("parallel"),
    )(where, shard)


ANY = pl.BlockSpec(memory_space=pl.ANY)


def _place():
    x, y, c = lax.axis_index("x"), lax.axis_index("y"), lax.axis_index("c")
    chips = [(1 - x, y), (x, 1 - y), (1 - x, 1 - y)]
    return x, y, c, chips


def gather_shards(arrays):
    n = len(arrays)

    def body(*refs):
        send, pass_on, finish = _gather_phases(refs[n:2 * n], *refs[2 * n:], _spans(arrays))
        send()
        pass_on()
        finish()

    return pl.pallas_call(
        body, name="gather_shards", in_specs=[ANY] * n, out_specs=[ANY] * n,
        out_shape=[jax.ShapeDtypeStruct(a.shape, a.dtype) for a in _arrays(arrays)],
        input_output_aliases={w: w for w in range(n)}, scratch_shapes=_gather_sems(n),
        compiler_params=pltpu.CompilerParams(has_side_effects=True),
    )(*_arrays(arrays))


def _gather_sems(n):
    return [pltpu.SemaphoreType.DMA((6 * n,)), pltpu.SemaphoreType.DMA((6 * n,))]


class Span(typing.NamedTuple):
    array: jax.Array
    lo: int
    hi: int
    ways: tuple = (0, 1, 2)


def _arrays(gathering):
    return [g.array if isinstance(g, Span) else g for g in gathering]


def _spans(gathering):
    return [(g.lo, g.hi, g.ways) if isinstance(g, Span) else (0, g.shape[1], (0, 1, 2)) for g in gathering]


def _gather_phases(out, send_sems, recv_sems, spans):
    n = len(out)
    if not any(ways for _, _, ways in spans):
        return (lambda: None,) * 3
    x, y, c, chips = _place()
    me = 2 * x + y
    sibling = (x, y, 1 - c)

    def half(w, chip, core):
        lo, hi, _ = spans[w]
        h = (hi - lo) // 2
        return out[w].at[chip, pl.ds(lo + core * h, h)]

    def copy(k, block, to):
        return pltpu.make_async_remote_copy(src_ref=block, dst_ref=block, send_sem=send_sems.at[k],
                                            recv_sem=recv_sems.at[k], device_id=to, device_id_type=MESH)

    def over_ici(w, j, chip):
        return copy(3 * w + j, half(w, chip, c), (chips[j][0], chips[j][1], c))

    def over_d2d(w, j, core):
        return copy(3 * n + 3 * w + j, half(w, 2 * chips[j][0] + chips[j][1], core), sibling)

    pairs = [(w, j) for w in range(n) for j in spans[w][2]]

    def send():
        for w, j in pairs:
            over_ici(w, j, me).start()

    def pass_on():
        for w, j in pairs:
            over_ici(w, j, 2 * chips[j][0] + chips[j][1]).wait_recv()
            over_d2d(w, j, c).start()

    def finish():
        for w, j in pairs:
            over_d2d(w, j, 1 - c).wait_recv()
        for w, j in pairs:
            over_ici(w, j, me).wait_send()
            over_d2d(w, j, c).wait_send()

    return send, pass_on, finish


def _relay_sems():
    return [pltpu.SemaphoreType.DMA((4,)), pltpu.SemaphoreType.DMA((4,))]


def _relay_phases(out, send_sems, recv_sems):
    x, y, c, chips = _place()
    sibling = (x, y, 1 - c)
    rows = out.shape[1]
    quarter = rows // 4
    far = 2 * chips[2][0] + chips[2][1]

    def piece(chip, way, core):
        return out.at[chip, pl.ds(way * (rows // 2) + core * quarter, quarter)]

    def copy(k, block, to):
        return pltpu.make_async_remote_copy(src_ref=block, dst_ref=block, send_sem=send_sems.at[k],
                                            recv_sem=recv_sems.at[k], device_id=to, device_id_type=MESH)

    def over_ici(way, chip):
        return copy(way, piece(chip, way, c), (chips[way][0], chips[way][1], c))

    def over_d2d(way, core):
        return copy(2 + way, piece(far, way, core), sibling)

    def send():
        for way in range(2):
            other = chips[1 - way]
            over_ici(way, 2 * other[0] + other[1]).start()

    def pass_on():
        for way in range(2):
            over_ici(way, far).wait_recv()
            over_d2d(way, c).start()

    def finish():
        for way in range(2):
            over_d2d(way, 1 - c).wait_recv()
        for way in range(2):
            other = chips[1 - way]
            over_ici(way, 2 * other[0] + other[1]).wait_send()
            over_d2d(way, c).wait_send()

    return send, pass_on, finish


def swap_halves(name, grads):
    n = len(grads)

    def body(*refs):
        send, finish = _swap_phases(refs[:n], refs[n:2 * n], *refs[2 * n:])
        send()
        finish()

    return pl.pallas_call(
        body, name=name, in_specs=[ANY] * n, out_specs=[ANY] * n, out_shape=_swap_shapes(grads),
        scratch_shapes=_swap_sems(n), compiler_params=pltpu.CompilerParams(has_side_effects=True),
    )(*grads)


def _swap_shapes(grads):
    return [jax.ShapeDtypeStruct((a.shape[0], a.shape[1] // 2, a.shape[2]), a.dtype) for a in grads]


def _swap_sems(n):
    return [pltpu.SemaphoreType.DMA((n,)), pltpu.SemaphoreType.DMA((n,))]


def _swap_phases(g, out, send_sems, recv_sems):
    x, y, c, _ = _place()

    def copies():
        return [pltpu.make_async_remote_copy(
            src_ref=g[w].at[:, pl.ds((1 - c) * (g[w].shape[1] // 2), g[w].shape[1] // 2)], dst_ref=out[w],
            send_sem=send_sems.at[w], recv_sem=recv_sems.at[w], device_id=(x, y, 1 - c), device_id_type=MESH)
            for w in range(len(g))]

    def send():
        for cp in copies():
            cp.start()

    def finish():
        for cp in copies():
            cp.wait()

    return send, finish


def _send_phases(g, out, send_sems, recv_sems):
    x, y, c, _ = _place()

    def copies():
        return [pltpu.make_async_remote_copy(
            src_ref=g[w], dst_ref=out[w], send_sem=send_sems.at[w], recv_sem=recv_sems.at[w],
            device_id=(x, y, 1 - c), device_id_type=MESH) for w in range(len(g))]

    def send():
        for cp in copies():
            cp.start()

    def finish():
        for cp in copies():
            cp.wait()

    return send, finish


def dw_in_half(name, h, dz, which, sending):
    s = h.shape[0]
    hh, tb = D // 2, IN_SHARD // 2
    n = len(sending)
    steps = IN_COLS // tb

    def body(w_ref, *refs):
        a_ref, b_ref, o_ref = refs[0], refs[1], refs[2 + n]
        j = pl.program_id(0)
        if n:
            send, finish = _send_phases(refs[2:2 + n], refs[3 + n:3 + 2 * n], *refs[3 + 2 * n:])
            pl.when(j == 0)(send)
        o_ref[...] = _dot(a_ref[...], b_ref[...], TN)
        if n:
            pl.when(j == steps - 1)(finish)

    out = pl.pallas_call(
        body, name=name,
        grid_spec=pltpu.PrefetchScalarGridSpec(
            num_scalar_prefetch=1, grid=(steps,),
            in_specs=[pl.BlockSpec((s, hh), lambda j, w: (0, w[0])), pl.BlockSpec((s, tb), lambda j, w: (0, j))]
            + [ANY] * n,
            out_specs=[pl.BlockSpec((None, hh, tb), lambda j, w: (j // 2, 0, j % 2))] + [ANY] * n,
            scratch_shapes=_swap_sems(n) if n else []),
        out_shape=[jax.ShapeDtypeStruct((N_CHIPS, hh, IN_SHARD), F32)]
        + [jax.ShapeDtypeStruct(a.shape, a.dtype) for a in sending],
        compiler_params=_params("arbitrary", communicates=bool(n)),
    )(which, h, dz, *sending)
    return out[0], out[1:]


def scatter_chips(parts):
    n = len(parts)

    def body(*refs):
        send, finish = _scatter_phases(refs[:n], refs[n:2 * n], *refs[2 * n:])
        send()
        finish()

    return pl.pallas_call(
        body, name="scatter_chips", in_specs=[ANY] * n, out_specs=[ANY] * n,
        out_shape=_scatter_shapes(parts), scratch_shapes=_scatter_sems(n),
        compiler_params=pltpu.CompilerParams(has_side_effects=True),
    )(*parts)


def _scatter_shapes(parts):
    return [jax.ShapeDtypeStruct((3,) + a.shape[1:], a.dtype) for a in parts]


def _scatter_sems(n):
    return [pltpu.SemaphoreType.DMA((3 * n,)), pltpu.SemaphoreType.DMA((3 * n,))]


def _scatter_phases(p, out, send_sems, recv_sems):
    x, y, c, chips = _place()

    def copies():
        return [pltpu.make_async_remote_copy(
            src_ref=p[w].at[2 * px + py], dst_ref=out[w].at[j], send_sem=send_sems.at[3 * w + j],
            recv_sem=recv_sems.at[3 * w + j], device_id=(px, py, c), device_id_type=MESH)
            for w in range(len(p)) for j, (px, py) in enumerate(chips)]

    def send():
        for cp in copies():
            cp.start()

    def finish():
        for cp in copies():
            cp.wait()

    return send, finish


def _join_only(arrays):
    n = len(arrays)

    def body(*refs):
        out = refs[n:2 * n]
        send_sems, recv_sems = refs[2 * n:]
        x, y, c, _ = _place()

        def copy(w, core):
            h = out[w].shape[0] // 2
            rows = out[w].at[pl.ds(core * h, h)]
            return pltpu.make_async_remote_copy(
                src_ref=rows, dst_ref=rows, send_sem=send_sems.at[w], recv_sem=recv_sems.at[w],
                device_id=(x, y, 1 - c), device_id_type=MESH)

        for w in range(n):
            copy(w, c).start()
        for w in range(n):
            copy(w, 1 - c).wait_recv()
        for w in range(n):
            copy(w, c).wait_send()

    return pl.pallas_call(
        body, name="join_only", in_specs=[ANY] * n, out_specs=[ANY] * n,
        out_shape=[jax.ShapeDtypeStruct(a.shape, a.dtype) for a in arrays],
        input_output_aliases={w: w for w in range(n)},
        scratch_shapes=[pltpu.SemaphoreType.DMA((n,)), pltpu.SemaphoreType.DMA((n,))],
        compiler_params=pltpu.CompilerParams(has_side_effects=True),
    )(*arrays)


HBM = pl.BlockSpec(memory_space=pltpu.HBM)
SEM = pl.BlockSpec(memory_space=pltpu.SEMAPHORE)
DATAFLOW = pltpu.SideEffectType.DATAFLOW_SIDE_EFFECTING


def _scatter_copies(p_ref, land_ref, send_sems, recv_sems):
    x, y, c, chips = _place()
    return [pltpu.make_async_remote_copy(
        src_ref=p_ref.at[2 * px + py], dst_ref=land_ref.at[j], send_sem=send_sems[j], recv_sem=recv_sems[j],
        device_id=(px, py, c), device_id_type=MESH) for j, (px, py) in enumerate(chips)]


def scatter_start(p):
    land = jax.ShapeDtypeStruct((3,) + p.shape[1:], p.dtype)

    def body(p_ref, land_ref, *outs):
        for cp in _scatter_copies(p_ref, land_ref, outs[0:3], outs[3:6]):
            cp.start()
        outs[8][...] = jnp.zeros_like(outs[8])

    return pl.pallas_call(
        body, name="scatter_start",
        out_shape=(pltpu.SemaphoreType.DMA(()),) * 6
        + (pltpu.HBM(p.shape, p.dtype), pltpu.HBM(land.shape, land.dtype), jax.ShapeDtypeStruct((8, 128), F32)),
        in_specs=(HBM, HBM), out_specs=(SEM,) * 6 + (HBM, HBM, pl.BlockSpec(memory_space=pltpu.VMEM)),
        input_output_aliases={0: 6, 1: 7},
        compiler_params=pltpu.CompilerParams(has_side_effects=DATAFLOW),
    )(pltpu.with_memory_space_constraint(p, pltpu.HBM),
      pltpu.with_memory_space_constraint(lax.empty(land.shape, land.dtype), pltpu.HBM))


def scatter_wait(started, after):
    sems, p_thru, land_thru = started[0:6], started[6], started[7]

    def body(p_ref, land_ref, *refs):
        for cp in _scatter_copies(p_ref, land_ref, refs[0:3], refs[3:6]):
            cp.wait_send()
            cp.wait_recv()

    return pl.pallas_call(
        body, name="scatter_wait",
        out_shape=(pltpu.HBM(p_thru.shape, p_thru.dtype), pltpu.HBM(land_thru.shape, land_thru.dtype)),
        in_specs=(HBM, HBM) + (SEM,) * 6 + (pl.BlockSpec(memory_space=pl.ANY),) * len(after), out_specs=(HBM, HBM),
        input_output_aliases={0: 0, 1: 1},
        compiler_params=pltpu.CompilerParams(has_side_effects=DATAFLOW),
    )(p_thru, land_thru, *sems, *after)


def _gather_leg_copies(buf_ref, leg, send_sems, recv_sems):
    x, y, c, chips = _place()
    h = buf_ref.shape[1] // 2
    copies = []
    for j, (px, py) in enumerate(chips):
        chip, to = (2 * x + y, (px, py, c)) if leg == "ici" else (2 * px + py, (x, y, 1 - c))
        block = buf_ref.at[chip, pl.ds(c * h, h)]
        copies.append(pltpu.make_async_remote_copy(
            src_ref=block, dst_ref=block, send_sem=send_sems[j], recv_sem=recv_sems[j],
            device_id=to, device_id_type=MESH))
    return copies


def gather_leg_start(name, buf, leg):
    def body(buf_ref, *outs):
        for cp in _gather_leg_copies(buf_ref, leg, outs[0:3], outs[3:6]):
            cp.start()
        outs[7][...] = jnp.zeros_like(outs[7])

    return pl.pallas_call(
        body, name=name,
        out_shape=(pltpu.SemaphoreType.DMA(()),) * 6
        + (pltpu.HBM(buf.shape, buf.dtype), jax.ShapeDtypeStruct((8, 128), F32)),
        in_specs=(HBM,), out_specs=(SEM,) * 6 + (HBM, pl.BlockSpec(memory_space=pltpu.VMEM)),
        input_output_aliases={0: 6},
        compiler_params=pltpu.CompilerParams(has_side_effects=DATAFLOW),
    )(pltpu.with_memory_space_constraint(buf, pltpu.HBM))


def gather_leg_wait(name, started, leg, after):
    def body(buf_ref, *refs):
        for cp in _gather_leg_copies(buf_ref, leg, refs[0:3], refs[3:6]):
            cp.wait_send()
            cp.wait_recv()

    buf = started[6]
    return pl.pallas_call(
        body, name=name, out_shape=pltpu.HBM(buf.shape, buf.dtype),
        in_specs=(HBM,) + (SEM,) * 6 + (pl.BlockSpec(memory_space=pl.ANY),) * len(after), out_specs=HBM,
        input_output_aliases={0: 0},
        compiler_params=pltpu.CompilerParams(has_side_effects=DATAFLOW),
    )(buf, *started[0:6], *after)


def _join_copies(refs, send_sems, recv_sems):
    x, y, c, _ = _place()
    copies = []
    for w, ref in enumerate(refs):
        h = ref.shape[0] // 2
        rows = ref.at[pl.ds(c * h, h)]
        copies.append(pltpu.make_async_remote_copy(
            src_ref=rows, dst_ref=rows, send_sem=send_sems[w], recv_sem=recv_sems[w],
            device_id=(x, y, 1 - c), device_id_type=MESH))
    return copies


def join_start(arrays):
    n = len(arrays)

    def body(*refs):
        outs = refs[n:]
        for cp in _join_copies(refs[:n], outs[0:n], outs[n:2 * n]):
            cp.start()
        outs[3 * n][...] = jnp.zeros_like(outs[3 * n])

    return pl.pallas_call(
        body, name="join_start",
        out_shape=(pltpu.SemaphoreType.DMA(()),) * (2 * n)
        + tuple(pltpu.HBM(a.shape, a.dtype) for a in arrays) + (jax.ShapeDtypeStruct((8, 128), F32),),
        in_specs=(HBM,) * n,
        out_specs=(SEM,) * (2 * n) + (HBM,) * n + (pl.BlockSpec(memory_space=pltpu.VMEM),),
        input_output_aliases={w: 2 * n + w for w in range(n)},
        compiler_params=pltpu.CompilerParams(has_side_effects=DATAFLOW),
    )(*[pltpu.with_memory_space_constraint(a, pltpu.HBM) for a in arrays])


def join_wait(started, after):
    n = (len(started) - 1) // 3

    def body(*refs):
        for cp in _join_copies(refs[:n], refs[n:2 * n], refs[2 * n:3 * n]):
            cp.wait_send()
            cp.wait_recv()

    bufs = started[2 * n:3 * n]
    return pl.pallas_call(
        body, name="join_wait", out_shape=tuple(pltpu.HBM(b.shape, b.dtype) for b in bufs),
        in_specs=(HBM,) * n + (SEM,) * (2 * n) + (pl.BlockSpec(memory_space=pl.ANY),) * len(after),
        out_specs=(HBM,) * n, input_output_aliases={w: w for w in range(n)},
        compiler_params=pltpu.CompilerParams(has_side_effects=DATAFLOW),
    )(*bufs, *started[0:2 * n], *after)


def join_halves(arrays, gathering=None):
    n = len(arrays)
    if gathering is None:
        return _join_only(arrays), None

    def body(*refs):
        out = refs[n + 1:2 * n + 1]
        send_sems, recv_sems = refs[2 * n + 2:2 * n + 4]
        send8, pass_on8, finish8 = _allgather8_phases(refs[2 * n + 1], *refs[2 * n + 4:])
        x, y, c, _ = _place()

        def copy(w, core):
            h = out[w].shape[0] // 2
            rows = out[w].at[pl.ds(core * h, h)]
            return pltpu.make_async_remote_copy(
                src_ref=rows, dst_ref=rows, send_sem=send_sems.at[w], recv_sem=recv_sems.at[w],
                device_id=(x, y, 1 - c), device_id_type=MESH)

        send8()
        for w in range(n):
            copy(w, c).start()
        pass_on8()
        for w in range(n):
            copy(w, 1 - c).wait_recv()
        finish8()
        for w in range(n):
            copy(w, c).wait_send()

    res = pl.pallas_call(
        body, name="join_halves", in_specs=[ANY] * (n + 1), out_specs=[ANY] * (n + 1),
        out_shape=[jax.ShapeDtypeStruct(a.shape, a.dtype) for a in list(arrays) + [gathering]],
        input_output_aliases={w: w for w in range(n + 1)},
        scratch_shapes=[pltpu.SemaphoreType.DMA((n,)), pltpu.SemaphoreType.DMA((n,)),
                        pltpu.SemaphoreType.DMA((7,)), pltpu.SemaphoreType.DMA((7,))],
        compiler_params=pltpu.CompilerParams(has_side_effects=True),
    )(*arrays, gathering)
    return res[:n], res[n]


def allreduce_small(packed):
    r, c = packed.shape
    n_dev = 8

    def body(x_ref, all_ref, sum_ref, send_sems, recv_sems, local_sem):
        x, y, cc, chips = _place()
        me, sibling = (x, y, cc), (x, y, 1 - cc)

        def rows(px, py, pc):
            return all_ref.at[4 * px + 2 * py + pc]

        def copy(k, block, to, src=None):
            return pltpu.make_async_remote_copy(
                src_ref=rows(*block) if src is None else src, dst_ref=rows(*block), send_sem=send_sems.at[k],
                recv_sem=recv_sems.at[k], device_id=to, device_id_type=MESH)

        mine = pltpu.make_async_copy(x_ref, rows(*me), local_sem)
        mine.start()
        first = [copy(0, me, sibling, src=x_ref)]
        first += [copy(1 + j, me, (*chip, cc), src=x_ref) for j, chip in enumerate(chips)]
        for cp in first:
            cp.start()
        passed = [copy(4 + j, (*chip, cc), sibling) for j, chip in enumerate(chips)]
        for j, chip in enumerate(chips):
            copy(1 + j, (*chip, cc), me).wait_recv()
            passed[j].start()
        copy(0, sibling, me).wait_recv()
        for j, chip in enumerate(chips):
            copy(4 + j, (*chip, 1 - cc), me).wait_recv()
        for cp in first + passed:
            cp.wait_send()
        mine.wait()
        acc = all_ref[0]
        for k in range(1, n_dev):
            acc = acc + all_ref[k]
        sum_ref[...] = acc

    vm = pl.BlockSpec(memory_space=pltpu.VMEM)
    return pl.pallas_call(
        body, name="allreduce_small", in_specs=[vm], out_specs=[vm, vm],
        out_shape=[jax.ShapeDtypeStruct((n_dev, r, c), F32), jax.ShapeDtypeStruct((r, c), F32)],
        scratch_shapes=[pltpu.SemaphoreType.DMA((7,)), pltpu.SemaphoreType.DMA((7,)), pltpu.SemaphoreType.DMA],
        compiler_params=pltpu.CompilerParams(has_side_effects=True, vmem_limit_bytes=VMEM_LIMIT),
    )(packed)[1]


def local_step(x, target, vecs, w_s, bs_t, bg, wg_in, late, core=None, order=None, where=None):
    on_mesh = core is not None

    def add(names, grads, recv):
        return [add_halves("add_" + n, g, r, core, min(r.shape[1], 256)) for n, g, r in zip(names, grads, recv)]

    g_pre, ln_g, ln_b, g_post, g_fpre, g_fpost = vecs
    s = x.shape[0]
    if order is None:
        order = jnp.arange(N_CHIPS, dtype=jnp.int32)
    logc = _attn_tables(s)
    ka, kb = _alibi_tables(s)

    h = norm_pre(x, g_pre)
    if not on_mesh:
        wg_a, wg_b, wg_out, wg_ff1, wg_ff2 = late
    if on_mesh:
        z, wg_in, (wg_a, wg_b, wg_out, wg_ff1, wg_ff2) = mm_in(h, wg_in, order, True, late)
        ya, (wg_b,) = gating_fwd(z, ln_g, ln_b, w_s, bs_t, [wg_b])
        yb, lse, (wg_a, wg_ff1, wg_out, bg) = attn_fwd(z, logc, ka, kb, [wg_a, wg_ff1, wg_out, bg])
        over_ici = gather_leg_start("ff2_ici_start", wg_ff2, "ici")
        bg = jnp.transpose(bg[:, :2, :], (1, 0, 2)).reshape(2, D) + over_ici[7][0:1, 0:1]
    else:
        z, _, _ = mm_in(h, wg_in, order, False)
        ya, _ = gating_fwd(z, ln_g, ln_b, w_s, bs_t, [])
        yb, lse, _ = attn_fwd(z, logc, ka, kb, [])
    merged, pa, pb, _ = proj_merge(ya, yb, wg_a.reshape(D, D), wg_b.reshape(D, D), z, bg, [])
    w_out = wg_out.reshape(D, D)
    o, x1, h2, _ = out_norm(merged, w_out, x, g_post, g_fpre, [])
    a, rl, _ = mm_ff1(h2, wg_ff1, [])
    if on_mesh:
        over_d2d = gather_leg_start("ff2_d2d_start", gather_leg_wait("ff2_ici_wait", over_ici, "ici", [rl]), "d2d")
        wg_ff2 = gather_leg_wait("ff2_d2d_wait", over_d2d, "d2d", [])
    w_ff2 = wg_ff2.reshape(D_FF, D)
    dy, df, d_gfpost, loss = ff2_loss(rl, w_ff2, x1, target, g_fpost)

    half_cols = pl.BlockSpec((D, D // 2), lambda i, j: (0, j))
    d_wff2 = mm_tn("dw_ff2", rl, df, D // 2, D, (D_FF, D), pl.BlockSpec((D // 2, D), lambda i, j: (i, 0)))
    da = ff2_bwd(df, w_ff2, a)
    d_wff1 = mm_tn("dw_ff1", h2, da, D, D // 2, (N_CHIPS, D, D),
                   pl.BlockSpec((None, D, D // 2), lambda i, j: (j // 2, 0, j % 2)))
    d_ff = [d_wff1, d_wff2.reshape(N_CHIPS, D, D)]
    dx1, do, d_gfpre, d_gpost, recv_ff = ff1_bwd_norms(da, wg_ff1, x1, o, dy, g_fpre, g_post, d_ff if on_mesh else [])
    d_wout = mm_tn("dw_out", merged, do, D, D // 2, (D, D), half_cols)
    dpa, dpb, dga, dgb, d_bg = out_bwd_gates(do, w_out, pa, pb, z, bg)
    d_wa = mm_tn("dw_a", ya, dpa, D, D // 2, (D, D), half_cols)
    d_wb = mm_tn("dw_b", yb, dpb, D, D // 2, (D, D), half_cols)
    dya = mm_nt("dy_a", dpa, wg_a.reshape(D, D))
    dyb = mm_nt("dy_b", dpb, wg_b.reshape(D, D))
    d_proj = [d_wa.reshape(N_CHIPS, D // N_CHIPS, D), d_wb.reshape(N_CHIPS, D // N_CHIPS, D),
              d_wout.reshape(N_CHIPS, D // N_CHIPS, D)]
    du, dv, d_ws, d_bs, d_lng, d_lnb, recv_proj = gating_bwd(z, dya, ln_g, ln_b, w_s, bs_t, d_proj if on_mesh else [])
    early = d_proj + d_ff
    parts_early = add(BIG[1:], early, list(recv_proj) + list(recv_ff)) if on_mesh else []
    small = dict(b_gate=d_bg, ln_v_g=d_lng, ln_v_b=d_lnb, w_s=d_ws, b_s=d_bs[:, 0, :],
                 norm_mix_post=d_gpost, norm_ffn_pre=d_gfpre, norm_ffn_post=d_gfpost)
    packed = pack_small(dict(small, norm_mix_pre=jnp.zeros((1, D), F32)), loss, where) if on_mesh else None
    dq, dk, dvb, got_early, packed = attn_bwd(z, yb, dyb, lse, logc, ka, kb, parts_early, packed)
    dz = jnp.concatenate([du, dv, dq, dk, dvb, dga, dgb], axis=1)
    if on_mesh:
        for_sibling, _ = dw_in_half("dw_in_sibling", h, dz, 1 - core, [])
        mine, from_sibling = dw_in_half("dw_in_mine", h, dz, core, [for_sibling])
        d_win = None
        parts_late = [add_halves("add_w_in", mine, from_sibling[0], jnp.zeros((1,), jnp.int32), 256)]
    else:
        half = IN_SHARD // 2
        d_win = mm_tn("dw_in", h, dz, D, half, (N_CHIPS, D, IN_SHARD),
                      pl.BlockSpec((None, D, half), lambda i, j: (j // 2, 0, j % 2)))
        parts_late = []
    started = scatter_start(parts_late[0]) if on_mesh else None
    dx, d_gpre, _, _ = in_bwd_norm(dz, wg_in, x, dx1, g_pre + started[8][0:1, 0:1] if on_mesh else g_pre, [])
    small["norm_mix_pre"] = d_gpre
    return loss[0, 0], dx, [d_win] + early, small, parts_early, list(got_early), packed, started


BIG = ("w_in", "w_a_proj", "w_b_proj", "w_out", "w_ff1", "w_ff2")
SMALL = ("norm_mix_pre", "ln_v_g", "ln_v_b", "b_s", "norm_mix_post", "norm_ffn_pre", "norm_ffn_post", "w_s", "b_gate")
ORDER = ("norm_mix_pre", "w_in", "b_gate", "ln_v_g", "ln_v_b", "w_s", "b_s", "w_a_proj", "w_b_proj", "w_out",
         "norm_mix_post", "norm_ffn_pre", "w_ff1", "w_ff2", "norm_ffn_post")
VEC_ROWS = D // 128
WS_ROW = 7 * VEC_ROWS
BG_ROW = WS_ROW + GROUPS * CHUNK
LOSS_ROW = BG_ROW + 2 * VEC_ROWS
PACK_ROWS = LOSS_ROW + 8


def pack_small(small, loss, where):
    vectors = [small[n] for n in SMALL[:7]]
    operands = vectors + [small["w_s"], small["b_gate"], loss]

    def body(where_ref, *refs):
        out = refs[-1]
        ws_ref, bg_ref, loss_ref = refs[7:10]
        for i, n in enumerate(SMALL[:7]):
            if n == "b_s":
                out[i * VEC_ROWS:(i + 1) * VEC_ROWS, :] = refs[i][...]
            else:
                for j in range(VEC_ROWS):
                    out[i * VEC_ROWS + j:i * VEC_ROWS + j + 1, :] = refs[i][:, j * 128:(j + 1) * 128]
        for g in range(GROUPS):
            out[WS_ROW + g * CHUNK:WS_ROW + (g + 1) * CHUNK, :] = ws_ref[g]
        for r in range(2):
            for j in range(VEC_ROWS):
                row = BG_ROW + r * VEC_ROWS + j
                out[row:row + 1, :] = bg_ref[r:r + 1, j * 128:(j + 1) * 128]
        lane = lax.broadcasted_iota(jnp.int32, (8, 128), 1)
        sub = lax.broadcasted_iota(jnp.int32, (8, 128), 0)
        out[LOSS_ROW:LOSS_ROW + 8, :] = jnp.where((lane == 0) & (sub == 0), loss_ref[...], 0.0)

    return pl.pallas_call(
        body, name="pack_small",
        grid_spec=pltpu.PrefetchScalarGridSpec(
            num_scalar_prefetch=1, grid=(1,), in_specs=[_full(a.shape) for a in operands],
            out_specs=pl.BlockSpec((None, PACK_ROWS, 128), lambda i, w: (w[0], w[1], 0))),
        out_shape=jax.ShapeDtypeStruct((N_CHIPS, 2 * PACK_ROWS, 128), F32), compiler_params=_params("arbitrary"),
    )(where, *operands)


def pack_vector(vec, where):
    def body(where_ref, v_ref, out):
        for j in range(VEC_ROWS):
            out[j:j + 1, :] = v_ref[:, j * 128:(j + 1) * 128]

    return pl.pallas_call(
        body, name="pack_vector",
        grid_spec=pltpu.PrefetchScalarGridSpec(
            num_scalar_prefetch=1, grid=(1,), in_specs=[_full(vec.shape)],
            out_specs=pl.BlockSpec((None, VEC_ROWS, 128), lambda i, w: (w[0], w[1], 0))),
        out_shape=jax.ShapeDtypeStruct((N_CHIPS, 2 * VEC_ROWS, 128), F32), compiler_params=_params("arbitrary"),
    )(where, vec)


def adamw_small(gathered, first, chip, w, m, v):
    shapes = {n: (1, D) for n in SMALL}
    shapes.update(b_s=(GROUPS, CHUNK), w_s=(GROUPS * CHUNK, CHUNK), b_gate=(2, D // N_CHIPS))
    flat = lambda t: [t[n].reshape(shapes[n]) for n in SMALL]
    per = D // N_CHIPS // 128

    def body(chip_ref, all_ref, first_ref, *refs):
        params, outs = refs[:27], refs[27:]
        sub = lax.broadcasted_iota(jnp.int32, (VEC_ROWS, 128), 0)
        sum_ref = outs[36]
        total = all_ref[0, 0:PACK_ROWS, :]
        head = first_ref[0, 0:VEC_ROWS, :]
        for k in range(1, 2 * N_CHIPS):
            total = total + all_ref[k // 2, (k % 2) * PACK_ROWS:(k % 2 + 1) * PACK_ROWS, :]
            head = head + first_ref[k // 2, (k % 2) * VEC_ROWS:(k % 2 + 1) * VEC_ROWS, :]
        sum_ref[...] = total
        sum_ref[0:VEC_ROWS, :] = head

        def gate_row(r):
            rows = sum_ref[BG_ROW + r * VEC_ROWS:BG_ROW + (r + 1) * VEC_ROWS, :]
            return jnp.concatenate([jnp.sum(jnp.where(sub == per * chip_ref[0] + j, rows, 0.0), axis=0, keepdims=True)
                                    for j in range(per)], axis=1)

        for i, n in enumerate(SMALL):
            if n == "b_s":
                g = sum_ref[i * VEC_ROWS:(i + 1) * VEC_ROWS, :]
            elif n == "w_s":
                g = sum_ref[WS_ROW:BG_ROW, :]
            elif n == "b_gate":
                g = jnp.concatenate([gate_row(0), gate_row(1)], axis=0)
            else:
                g = jnp.concatenate([sum_ref[i * VEC_ROWS + j:i * VEC_ROWS + j + 1, :] for j in range(VEC_ROWS)],
                                    axis=1)
            delta, nm, nv = _adamw_math(params[i][...], g, params[9 + i][...], params[18 + i][...])
            outs[4 * i][...], outs[4 * i + 1][...], outs[4 * i + 2][...], outs[4 * i + 3][...] = g, delta, nm, nv

    vm = pl.BlockSpec(memory_space=pltpu.VMEM)
    res = pl.pallas_call(
        body, name="adamw_small",
        in_specs=[pl.BlockSpec(memory_space=pltpu.SMEM)] + [vm] * 29, out_specs=[vm] * 37,
        out_shape=[jax.ShapeDtypeStruct(shapes[n], F32) for n in SMALL for _ in range(4)]
        + [jax.ShapeDtypeStruct((PACK_ROWS, 128), F32)],
        compiler_params=_params(),
    )(chip, gathered, first, *flat(w), *flat(m), *flat(v))
    new = {n: tuple(r.reshape(w[n].shape) for r in res[4 * i:4 * i + 4]) for i, n in enumerate(SMALL)}
    return new, res[36][LOSS_ROW, 0]


def kernel(x, norm_mix_pre, w_in, b_gate, ln_v_g, ln_v_b, w_s, b_s, w_a_proj, w_b_proj, w_out, norm_mix_post, norm_ffn_pre, w_ff1, w_ff2, norm_ffn_post, loss_target, m_norm_mix_pre, m_w_in, m_b_gate, m_ln_v_g, m_ln_v_b, m_w_s, m_b_s, m_w_a_proj, m_w_b_proj, m_w_out, m_norm_mix_post, m_norm_ffn_pre, m_w_ff1, m_w_ff2, m_norm_ffn_post, v_norm_mix_pre, v_w_in, v_b_gate, v_ln_v_g, v_ln_v_b, v_w_s, v_b_s, v_w_a_proj, v_w_b_proj, v_w_out, v_norm_mix_post, v_norm_ffn_pre, v_w_ff1, v_w_ff2, v_norm_ffn_post):
    w = dict(norm_mix_pre=norm_mix_pre, w_in=w_in, b_gate=b_gate, ln_v_g=ln_v_g, ln_v_b=ln_v_b, w_s=w_s, b_s=b_s,
             w_a_proj=w_a_proj, w_b_proj=w_b_proj, w_out=w_out, norm_mix_post=norm_mix_post,
             norm_ffn_pre=norm_ffn_pre, w_ff1=w_ff1, w_ff2=w_ff2, norm_ffn_post=norm_ffn_post)
    m = dict(norm_mix_pre=m_norm_mix_pre, w_in=m_w_in, b_gate=m_b_gate, ln_v_g=m_ln_v_g, ln_v_b=m_ln_v_b, w_s=m_w_s,
             b_s=m_b_s, w_a_proj=m_w_a_proj, w_b_proj=m_w_b_proj, w_out=m_w_out, norm_mix_post=m_norm_mix_post,
             norm_ffn_pre=m_norm_ffn_pre, w_ff1=m_w_ff1, w_ff2=m_w_ff2, norm_ffn_post=m_norm_ffn_post)
    v = dict(norm_mix_pre=v_norm_mix_pre, w_in=v_w_in, b_gate=v_b_gate, ln_v_g=v_ln_v_g, ln_v_b=v_ln_v_b, w_s=v_w_s,
             b_s=v_b_s, w_a_proj=v_w_a_proj, w_b_proj=v_w_b_proj, w_out=v_w_out, norm_mix_post=v_norm_mix_post,
             norm_ffn_pre=v_norm_ffn_pre, w_ff1=v_w_ff1, w_ff2=v_w_ff2, norm_ffn_post=v_norm_ffn_post)
    chip = 2 * lax.axis_index("x") + lax.axis_index("y")
    core = lax.axis_index("c")

    where = jnp.stack([chip, core]).astype(jnp.int32)
    wg_in = place_shard("place_w_in", w_in[0], where, BF16, 256)
    bg_all = place_shard("place_b_gate", jnp.pad(b_gate[0], ((0, 14), (0, 0))), where, F32, 16)
    vecs = (norm_mix_pre, ln_v_g, ln_v_b, norm_mix_post, norm_ffn_pre, norm_ffn_post)
    loss, dx, _, small, parts, got, packed, started = local_step(
        x[0], loss_target[0], vecs, w_s[0], b_s[0].T, bg_all, wg_in, [w[n][0] for n in BIG[1:]],
        core=jnp.reshape(core, (1,)).astype(jnp.int32),
        order=jnp.stack([chip, chip ^ 2, chip ^ 1, chip ^ 3]).astype(jnp.int32), where=where)

    halves = [sum_chips("sum_" + n, p, r, where, min(p.shape[1], 256), started[8])
              for n, p, r in zip(BIG[1:], parts, got)]
    joined = join_wait(join_start(halves), [dx])
    grads = dict(zip(BIG[1:], joined))
    new = {}

    def update(n):
        shape = w[n].shape
        res = adamw("adamw_" + n, w[n][0], grads[n], m[n][0], v[n][0], min(shape[1], 256))
        new[n] = tuple(r.reshape(shape) for r in res)

    for n in BIG[1:]:
        update(n)
    p_in, got_in = scatter_wait(started, [new["w_ff2"][1], dx])
    (grads["w_in"],), first = join_halves([sum_chips("sum_w_in", p_in, got_in, where, 256)],
                                          pack_vector(small["norm_mix_pre"], where))
    update("w_in")
    small_new, loss = adamw_small(packed, first, jnp.reshape(chip, (1,)).astype(jnp.int32), w, m, v)
    new.update(small_new)

    outs = [loss, dx[None]]
    for i in range(4):
        outs += [new[n][i] for n in ORDER]
    return tuple(outs)
```

```python
import functools
import math
import typing

import numpy as np
import jax
import jax.numpy as jnp
from jax import lax
from jax.experimental import pallas as pl
from jax.experimental.pallas import tpu as pltpu

F32 = jnp.float32
BF16 = jnp.bfloat16
MESH = pl.DeviceIdType.MESH

D = 1024
EPS = 1e-6
CHUNK = 128
GROUPS = 8
HEADS = 16
HEAD_DIM = 64
ATT_T = 256
ATT_GROUP = 8
ATT_BWD_GROUP = 4
N_CHIPS = 4
D_FF = 4 * D
IN_COLS = 7 * D
IN_SHARD = IN_COLS // N_CHIPS
MASKED = -1e30
VMEM_LIMIT = 56 * 2 ** 20

ADAM_LR, ADAM_B1, ADAM_B2, ADAM_EPS, ADAM_WD, ADAM_STEP = 0.001, 0.9, 0.999, 1e-08, 0.01, 10

NN = (((1,), (0,)), ((), ()))
NT = (((1,), (1,)), ((), ()))
TN = (((0,), (0,)), ((), ()))


def _dot(a, b, dims=NN):
    return lax.dot_general(a, b, dims, preferred_element_type=F32)


def _params(*sem, communicates=False):
    return pltpu.CompilerParams(dimension_semantics=sem or None, vmem_limit_bytes=VMEM_LIMIT,
                                has_side_effects=communicates)


def _rows(tr, c, col=0):
    return pl.BlockSpec((tr, c), lambda i: (i, col))


def _full(shape):
    n = len(shape)
    return pl.BlockSpec(shape, lambda *_: (0,) * n)


def _gelu(x):
    k = math.sqrt(2.0 / math.pi)
    return 0.5 * x * (1.0 + jnp.tanh(k * (x + 0.044715 * x * x * x)))


def _gelu_and_grad(x):
    k = math.sqrt(2.0 / math.pi)
    t = jnp.tanh(k * (x + 0.044715 * x * x * x))
    g = 0.5 * x * (1.0 + t)
    dg = 0.5 * (1.0 + t) + 0.5 * x * (1.0 - t * t) * (k * (1.0 + 3.0 * 0.044715 * x * x))
    return g, dg


def _sigmoid(x):
    return 1.0 / (1.0 + jnp.exp(-x))


def _rms(x):
    r = lax.rsqrt(jnp.mean(x * x, axis=-1, keepdims=True) + EPS)
    return x * r, r


def _rms_bwd(dn, xhat, r):
    return r * (dn - xhat * jnp.mean(dn * xhat, axis=-1, keepdims=True))


def norm_pre(x, g):
    s = x.shape[0]
    tr = 512

    def body(x_ref, g_ref, h_ref):
        xhat, _ = _rms(x_ref[...])
        h_ref[...] = (xhat * g_ref[...]).astype(BF16)

    return pl.pallas_call(
        body, name="norm_pre", grid=(s // tr,),
        in_specs=[_rows(tr, D), _full((1, D))], out_specs=_rows(tr, D),
        out_shape=jax.ShapeDtypeStruct((s, D), BF16), compiler_params=_params("parallel"),
    )(x, g)


def mm_in(h, wg, order, staged, casting=()):
    s = h.shape[0]
    tm, tn = 1024, IN_SHARD // 2
    per = IN_SHARD // tn
    m = len(casting)
    nj, ni = N_CHIPS * per, s // tm
    cast_steps = per * ni

    def body(order_ref, *refs):
        a_ref = refs[0]
        cast_in = refs[2:2 + m]
        o_ref, held = refs[2 + m], refs[3 + m]
        cast_out = refs[4 + m:4 + 2 * m]
        tile, tile_sem = refs[4 + 2 * m:6 + 2 * m]
        sems = refs[6 + 2 * m:]
        j, i = pl.program_id(0), pl.program_id(1)

        @pl.when(j * ni + i < cast_steps)
        def _():
            for src, dst in zip(cast_in, cast_out):
                dst[...] = src[...].astype(BF16)

        def fetch(t):
            chip = order_ref[t // per]
            return pltpu.make_async_copy(held.at[chip, :, pl.ds((t % per) * tn, tn)], tile.at[t % 2],
                                         tile_sem.at[t % 2])

        if staged:
            near = _gather_phases([held], *sems[:2], [(0, D, (0, 1))])
            far = _relay_phases(held, *sems[2:])

        @pl.when(i == 0)
        def _():
            @pl.when(j == 0)
            def _():
                if staged:
                    near[0]()
                fetch(0).start()

            fetch(j).wait()
            ahead = j + 1 < nj
            if staged:
                ahead = ahead & (j + 1 != per) & (j + 1 != 3 * per)
            pl.when(ahead)(lambda: fetch(j + 1).start())

        rows = pl.ds(pl.multiple_of(i * tm, tm), tm)
        o_ref[...] = _dot(a_ref[rows, :], tile[j % 2]).astype(BF16)

        if staged:
            @pl.when((i == ni - 1) & (j == per - 1))
            def _():
                near[1]()
                near[2]()
                far[0]()
                fetch(per).start()

            @pl.when((i == ni - 1) & (j == 3 * per - 1))
            def _():
                far[1]()
                far[2]()
                fetch(3 * per).start()

    def cast_block(j, i, o):
        return jnp.minimum(j * ni + i, cast_steps - 1)

    out = pl.pallas_call(
        body, name="mm_in",
        grid_spec=pltpu.PrefetchScalarGridSpec(
            num_scalar_prefetch=1, grid=(nj, ni),
            in_specs=[pl.BlockSpec((s, D), lambda j, i, o: (0, 0)), ANY]
            + [pl.BlockSpec((a.shape[0] // cast_steps, a.shape[1]), lambda j, i, o: (cast_block(j, i, o), 0))
               for a in casting],
            out_specs=[pl.BlockSpec((tm, tn), lambda j, i, o: (i, o[j // per] * per + j % per)), ANY]
            + [pl.BlockSpec((None, a.shape[0] // cast_steps, a.shape[1]),
                            lambda j, i, o: (o[0], cast_block(j, i, o), 0)) for a in casting],
            scratch_shapes=[pltpu.VMEM((2, D, tn), BF16), pltpu.SemaphoreType.DMA((2,))]
            + (_gather_sems(1) + _relay_sems() if staged else [])),
        out_shape=[jax.ShapeDtypeStruct((s, IN_COLS), BF16), jax.ShapeDtypeStruct(wg.shape, wg.dtype)]
        + [jax.ShapeDtypeStruct((N_CHIPS,) + a.shape, BF16) for a in casting],
        input_output_aliases={2: 1},
        compiler_params=_params("arbitrary", "arbitrary", communicates=staged),
    )(order, h, wg, *casting)
    return out[0], out[1], out[2:]


def _tril_ws(ws_ref, g):
    r = lax.broadcasted_iota(jnp.int32, (CHUNK, CHUNK), 0)
    c = lax.broadcasted_iota(jnp.int32, (CHUNK, CHUNK), 1)
    return jnp.where(c <= r, ws_ref[g], 0.0).astype(BF16)


def _layer_norm(v):
    mu = jnp.mean(v, axis=-1, keepdims=True)
    d = v - mu
    rstd = lax.rsqrt(jnp.mean(d * d, axis=-1, keepdims=True) + EPS)
    return d * rstd, rstd


def gating_fwd(z, ln_g, ln_b, w_s, bs_t, gathering):
    s = z.shape[0]
    n = len(gathering)
    steps = s // CHUNK

    def body(*refs):
        u_ref, v_ref, lg_ref, lb_ref, ws_ref, bst_ref = refs[:6]
        ya_ref = refs[6 + n]
        ci = pl.program_id(0)
        if n:
            send, pass_on, finish = _gather_phases(refs[7 + n:7 + 2 * n], *refs[7 + 2 * n:], _spans(gathering))
            pl.when(ci == 0)(send)
        ug = _gelu(u_ref[...].astype(F32))
        vhat, _ = _layer_norm(_gelu(v_ref[...].astype(F32)))
        vn = (vhat * lg_ref[...] + lb_ref[...]).astype(BF16)
        for g in range(GROUPS):
            cols = slice(g * CHUNK, (g + 1) * CHUNK)
            mixed = _dot(_tril_ws(ws_ref, g), vn[:, cols]) + bst_ref[:, g:g + 1]
            ya_ref[:, cols] = (ug[:, cols] * mixed).astype(BF16)
        if n:
            pl.when(ci == steps - 1)(pass_on)
            pl.when(ci == steps - 1)(finish)

    out = pl.pallas_call(
        body, name="gating_fwd", grid=(steps,),
        in_specs=[_rows(CHUNK, D, 0), _rows(CHUNK, D, 1), _full((1, D)), _full((1, D)),
                  _full((GROUPS, CHUNK, CHUNK)), _full((CHUNK, GROUPS))] + [ANY] * n,
        out_specs=[_rows(CHUNK, D)] + [ANY] * n,
        out_shape=[jax.ShapeDtypeStruct((s, D), BF16)]
        + [jax.ShapeDtypeStruct(a.shape, a.dtype) for a in _arrays(gathering)],
        input_output_aliases={6 + w: 1 + w for w in range(n)},
        scratch_shapes=_gather_sems(n) if n else [],
        compiler_params=_params("arbitrary", communicates=bool(n)),
    )(z, z, ln_g, ln_b, w_s, bs_t, *_arrays(gathering))
    return out[0], out[1:]


def _attn_tables(s):
    nd = s // ATT_T
    r = np.arange(ATT_T)[None, :, None]
    c = np.arange(ATT_T)[None, None, :]
    delta = np.arange(nd)[:, None, None] * ATT_T + r - c
    count = np.zeros(delta.shape, np.int64)
    for window, dilation in ((128, 1), (512, 4), (2048, 16)):
        count += (delta >= 0) & (delta % dilation == 0) & (delta <= window)
    logc = np.where(count > 0, np.log(np.maximum(count, 1)), MASKED)
    return jnp.asarray(logc, F32)


AUG = 3


def _split3_np(x):
    terms, rest = [], np.asarray(x, np.float64)
    for _ in range(AUG):
        term = np.asarray(rest.astype(jnp.bfloat16), np.float64)
        terms.append(term)
        rest = rest - term
    return terms


def _split3(x):
    terms, rest = [], x
    for _ in range(AUG):
        term = rest.astype(BF16).astype(F32)
        terms.append(term)
        rest = rest - term
    return terms


def _alibi_tables(s):
    nb = s // ATT_T
    slopes = np.exp2(-8.0 * np.arange(1, HEADS + 1, dtype=np.float64) / HEADS)
    ka = np.zeros((HEADS // 2, 2, ATT_T, 128), np.float32)
    kb = np.zeros((HEADS // 2, 2, nb, 128), np.float32)
    for p in range(HEADS // 2):
        for e in range(2):
            base = HEAD_DIM * (1 - e)
            for a, term in enumerate(_split3_np(slopes[2 * p + e] * np.arange(ATT_T))):
                ka[p, e, :, base + a] = term
            for a, term in enumerate(_split3_np(slopes[2 * p + e] * ATT_T * np.arange(nb))):
                kb[p, e, :, base + AUG + a] = term
            ka[p, e, :, base + 2 * AUG:base + 3 * AUG] = 1.0
    return jnp.asarray(ka), jnp.asarray(kb)


def _head_masks():
    lane = lax.broadcasted_iota(jnp.int32, (1, 128), 1)
    first = lane < HEAD_DIM

    def ones(e, n):
        base = HEAD_DIM * (1 - e)
        return ((lane >= base) & (lane < base + n)).astype(F32)

    return first, lane, ones


def _place3(lane, at, terms, other):
    for a, term in enumerate(terms):
        other = jnp.where(lane == at + a, term, other)
    return other


def attn_fwd(z, logc, ka, kb, gathering):
    s = z.shape[0]
    nq = s // ATT_T
    t = ATT_T
    n = len(gathering)
    grp = ATT_GROUP
    ngrp = HEADS // 2 // grp
    wide = 128 * grp
    qcol, kcol, vcol = 2 * D // wide, 3 * D // wide, 4 * D // wide

    def body(*refs):
        q_ref, k_ref, v_ref, lc_ref, ka_ref, kb_ref = refs[:6]
        y_ref, lse_ref = refs[6 + n:8 + n]
        q_s, k_s, v_s, m_s, l_s, acc_s = refs[8 + 2 * n:14 + 2 * n]
        gi, qi = pl.program_id(0), pl.program_id(1)
        first, lane, ones = _head_masks()
        if n:
            send, pass_on, finish = _gather_phases(refs[8 + n:8 + 2 * n], *refs[14 + 2 * n:], _spans(gathering))
            pl.when((gi == 0) & (qi == 0))(send)

        @pl.when(qi == 0)
        def _():
            sel = jnp.broadcast_to(first.astype(F32), (t, 128))
            for pr in range(grp):
                cols = slice(pr * 128, (pr + 1) * 128)
                for jb in range(nq):
                    kj = k_ref[jb * t:(jb + 1) * t, cols].astype(F32)
                    vj = v_ref[jb * t:(jb + 1) * t, cols].astype(F32)
                    k_s[pr, 0, jb] = jnp.where(first, kj, ka_ref[pr, 0] + kb_ref[pr, 0, jb:jb + 1, :]).astype(BF16)
                    k_s[pr, 1, jb] = jnp.where(first, ka_ref[pr, 1] + kb_ref[pr, 1, jb:jb + 1, :], kj).astype(BF16)
                    v_s[pr, jb, 0:t, 0:128] = jnp.where(first, vj, 0.0).astype(BF16)
                    v_s[pr, jb, t:2 * t, 0:128] = jnp.where(first, 0.0, vj).astype(BF16)
                    v_s[pr, jb, 0:t, 128:256] = sel.astype(BF16)
                    v_s[pr, jb, t:2 * t, 128:256] = (1.0 - sel).astype(BF16)

        for pr in range(grp):
            q = q_ref[:, pr * 128:(pr + 1) * 128].astype(F32) * (1.0 / math.sqrt(HEAD_DIM))
            q_s[pr, 0] = jnp.where(first, q, ones(0, 2 * AUG)).astype(BF16)
            q_s[pr, 1] = jnp.where(first, ones(1, 2 * AUG), q).astype(BF16)
        m_s[...] = jnp.full_like(m_s, MASKED)
        l_s[...] = jnp.zeros_like(l_s)
        acc_s[...] = jnp.zeros_like(acc_s)

        def scores(j):
            return tuple(_dot(q_s[pr, e], k_s[pr, e, j], NT) for pr in range(grp) for e in range(2))

        def step(j, carry):
            softmax_block(j, scores(j))
            return carry

        def softmax_block(j, u):
            lc = lc_ref[qi - j]
            for pr in range(grp):
                u0 = u[2 * pr] + lc
                u1 = u[2 * pr + 1] + lc
                m0, m1 = m_s[pr, 0], m_s[pr, 1]
                n0 = jnp.maximum(m0, jnp.max(u0, axis=-1, keepdims=True))
                n1 = jnp.maximum(m1, jnp.max(u1, axis=-1, keepdims=True))
                m_s[pr, 0], m_s[pr, 1] = n0, n1
                p = jnp.concatenate([jnp.exp(u0 - jnp.concatenate([n0, n0], axis=1)).astype(BF16),
                                     jnp.exp(u1 - jnp.concatenate([n1, n1], axis=1)).astype(BF16)], axis=1)
                pv = _dot(p, v_s[pr, j])
                alpha = jnp.where(first, jnp.exp(m0 - n0), jnp.exp(m1 - n1))
                acc_s[pr] = acc_s[pr] * alpha + pv[:, 0:128]
                l_s[pr] = l_s[pr] * alpha + pv[:, 128:256]

        lax.fori_loop(0, qi + 1, step, 0)
        for pr in range(grp):
            cols = slice(pr * 128, (pr + 1) * 128)
            y_ref[:, cols] = (acc_s[pr] / l_s[pr]).astype(BF16)
            lse_ref[:, cols] = jnp.where(first, m_s[pr, 0], m_s[pr, 1]) + jnp.log(l_s[pr])
        if n:
            pl.when((gi == ngrp - 1) & (qi == nq - 1))(pass_on)
            pl.when((gi == ngrp - 1) & (qi == nq - 1))(finish)

    out = pl.pallas_call(
        body, name="attn_fwd", grid=(ngrp, nq),
        in_specs=[pl.BlockSpec((t, wide), lambda g, i: (i, qcol + g)),
                  pl.BlockSpec((s, wide), lambda g, i: (0, kcol + g)),
                  pl.BlockSpec((s, wide), lambda g, i: (0, vcol + g)),
                  _full((nq, t, t)),
                  pl.BlockSpec((grp, 2, t, 128), lambda g, i: (g, 0, 0, 0)),
                  pl.BlockSpec((grp, 2, nq, 128), lambda g, i: (g, 0, 0, 0))] + [ANY] * n,
        out_specs=[pl.BlockSpec((t, wide), lambda g, i: (i, g)), pl.BlockSpec((t, wide), lambda g, i: (i, g))]
        + [ANY] * n,
        out_shape=[jax.ShapeDtypeStruct((s, D), BF16), jax.ShapeDtypeStruct((s, D), F32)]
        + [jax.ShapeDtypeStruct(a.shape, a.dtype) for a in _arrays(gathering)],
        input_output_aliases={6 + w: 2 + w for w in range(n)},
        scratch_shapes=[pltpu.VMEM((grp, 2, t, 128), BF16), pltpu.VMEM((grp, 2, nq, t, 128), BF16),
                        pltpu.VMEM((grp, nq, 2 * t, 256), BF16), pltpu.VMEM((grp, 2, t, 128), F32),
                        pltpu.VMEM((grp, t, 128), F32), pltpu.VMEM((grp, t, 128), F32)]
        + (_gather_sems(n) if n else []),
        compiler_params=_params("arbitrary", "arbitrary", communicates=bool(n)),
    )(z, z, z, logc, ka, kb, *_arrays(gathering))
    return out[0], out[1], out[2:]


def proj_merge(ya, yb, wa, wb, z, bg, gathering):
    s = ya.shape[0]
    tm = 512
    n = len(gathering)
    steps = s // tm

    def body(*refs):
        ya_ref, yb_ref, wa_ref, wb_ref, ga_ref, gb_ref, bg_ref = refs[:7]
        mg_ref, pa_ref, pb_ref = refs[7 + n:10 + n]
        i = pl.program_id(0)
        if n:
            send, pass_on, finish = _gather_phases(refs[10 + n:10 + 2 * n], *refs[10 + 2 * n:], _spans(gathering))
            pl.when(i == 0)(send)
            pl.when(i == steps - 1)(pass_on)
        pa = _dot(ya_ref[...], wa_ref[...])
        pb = _dot(yb_ref[...], wb_ref[...])
        sa = _sigmoid(ga_ref[...] + bg_ref[0:1, :])
        sb = _sigmoid(gb_ref[...] + bg_ref[1:2, :])
        mg_ref[...] = (sa * pa + sb * pb).astype(BF16)
        pa_ref[...] = pa.astype(BF16)
        pb_ref[...] = pb.astype(BF16)
        if n:
            pl.when(i == steps - 1)(finish)

    out = jax.ShapeDtypeStruct((s, D), BF16)
    res = pl.pallas_call(
        body, name="proj_merge", grid=(steps,),
        in_specs=[_rows(tm, D), _rows(tm, D), _full((D, D)), _full((D, D)),
                  _rows(tm, D, 5), _rows(tm, D, 6), _full((2, D))] + [ANY] * n,
        out_specs=[_rows(tm, D)] * 3 + [ANY] * n,
        out_shape=[out] * 3 + [jax.ShapeDtypeStruct(a.shape, a.dtype) for a in _arrays(gathering)],
        input_output_aliases={7 + w: 3 + w for w in range(n)},
        scratch_shapes=_gather_sems(n) if n else [],
        compiler_params=_params("arbitrary", communicates=bool(n)),
    )(ya, yb, wa, wb, z, z, bg, *_arrays(gathering))
    return res[0], res[1], res[2], res[3:]


def out_norm(merged, w_out, x, g_post, g_fpre, gathering):
    s = x.shape[0]
    tm = 512
    n = len(gathering)
    steps = s // tm

    def body(*refs):
        mg_ref, w_ref, x_ref, gp_ref, gf_ref = refs[:5]
        o_ref, x1_ref, h2_ref = refs[5 + n:8 + n]
        i = pl.program_id(0)
        if n:
            send, pass_on, finish = _gather_phases(refs[8 + n:8 + 2 * n], *refs[8 + 2 * n:], _spans(gathering))
            pl.when(i == 0)(send)
            pl.when(i == steps - 1)(pass_on)
        o = _dot(mg_ref[...], w_ref[...])
        ohat, _ = _rms(o)
        x1 = x_ref[...] + ohat * gp_ref[...]
        x1hat, _ = _rms(x1)
        o_ref[...] = o
        x1_ref[...] = x1
        h2_ref[...] = (x1hat * gf_ref[...]).astype(BF16)
        if n:
            pl.when(i == steps - 1)(finish)

    res = pl.pallas_call(
        body, name="out_norm", grid=(steps,),
        in_specs=[_rows(tm, D), _full((D, D)), _rows(tm, D), _full((1, D)), _full((1, D))] + [ANY] * n,
        out_specs=[_rows(tm, D)] * 3 + [ANY] * n,
        out_shape=[jax.ShapeDtypeStruct((s, D), F32), jax.ShapeDtypeStruct((s, D), F32),
                   jax.ShapeDtypeStruct((s, D), BF16)]
        + [jax.ShapeDtypeStruct(a.shape, a.dtype) for a in _arrays(gathering)],
        input_output_aliases={5 + w: 3 + w for w in range(n)},
        scratch_shapes=_gather_sems(n) if n else [],
        compiler_params=_params("arbitrary", communicates=bool(n)),
    )(merged, w_out, x, g_post, g_fpre, *_arrays(gathering))
    return res[0], res[1], res[2], res[3:]


def mm_ff1(h2, wg, gathering):
    s = h2.shape[0]
    tm = 1024
    n = len(gathering)
    ni = s // tm

    def body(*refs):
        a_ref, b_ref = refs[:2]
        o_ref, r_ref = refs[2 + n:4 + n]
        i, j = pl.program_id(0), pl.program_id(1)
        if n:
            send, pass_on, finish = _gather_phases(refs[4 + n:4 + 2 * n], *refs[4 + 2 * n:], _spans(gathering))
            pl.when((i == 0) & (j == 0))(send)
            pl.when((i == ni - 1) & (j == N_CHIPS // 2))(pass_on)
        a = _dot(a_ref[...], b_ref[...])
        o_ref[...] = a.astype(BF16)
        r = jnp.maximum(a, 0.0)
        r_ref[...] = (r * r).astype(BF16)
        if n:
            pl.when((i == ni - 1) & (j == N_CHIPS - 1))(finish)

    res = pl.pallas_call(
        body, name="mm_ff1", grid=(ni, N_CHIPS),
        in_specs=[pl.BlockSpec((tm, D), lambda i, j: (i, 0)), pl.BlockSpec((None, D, D), lambda i, j: (j, 0, 0))]
        + [ANY] * n,
        out_specs=[pl.BlockSpec((tm, D), lambda i, j: (i, j))] * 2 + [ANY] * n,
        out_shape=[jax.ShapeDtypeStruct((s, D_FF), BF16), jax.ShapeDtypeStruct((s, D_FF), BF16)]
        + [jax.ShapeDtypeStruct(a.shape, a.dtype) for a in _arrays(gathering)],
        input_output_aliases={2 + w: 2 + w for w in range(n)},
        scratch_shapes=_gather_sems(n) if n else [],
        compiler_params=_params("arbitrary", "arbitrary", communicates=bool(n)),
    )(h2, wg, *_arrays(gathering))
    return res[0], res[1], res[2:]


def ff2_loss(rl, w_ff2, x1, target, g_fpost):
    s = x1.shape[0]
    tm = 256

    def body(rl_ref, w_ref, x1_ref, t_ref, g_ref, dy_ref, df_ref, dg_ref, loss_ref):
        @pl.when(pl.program_id(0) == 0)
        def _():
            dg_ref[...] = jnp.zeros_like(dg_ref)
            loss_ref[...] = jnp.zeros_like(loss_ref)

        f = _dot(rl_ref[...], w_ref[...])
        fhat, r = _rms(f)
        err = x1_ref[...] + fhat * g_ref[...] - t_ref[...]
        loss_ref[...] += 0.5 * jnp.sum(jnp.mean(err * err, axis=-1, keepdims=True), axis=0, keepdims=True)
        dy = err * (1.0 / D)
        dy_ref[...] = dy
        dg_ref[...] += jnp.sum(dy * fhat, axis=0, keepdims=True)
        df_ref[...] = _rms_bwd(dy * g_ref[...], fhat, r).astype(BF16)

    return pl.pallas_call(
        body, name="ff2_loss", grid=(s // tm,),
        in_specs=[_rows(tm, D_FF), _full((D_FF, D)), _rows(tm, D), _rows(tm, D), _full((1, D))],
        out_specs=[_rows(tm, D), _rows(tm, D), _full((1, D)), _full((1, 1))],
        out_shape=[jax.ShapeDtypeStruct((s, D), F32), jax.ShapeDtypeStruct((s, D), BF16),
                   jax.ShapeDtypeStruct((1, D), F32), jax.ShapeDtypeStruct((1, 1), F32)],
        compiler_params=_params("arbitrary"),
    )(rl, w_ff2, x1, target, g_fpost)


def mm_tn(name, a, b, ta, tb, out_shape, out_spec):
    s = a.shape[0]

    def body(a_ref, b_ref, o_ref):
        o_ref[...] = _dot(a_ref[...], b_ref[...], TN)

    return pl.pallas_call(
        body, name=name, grid=(a.shape[1] // ta, b.shape[1] // tb),
        in_specs=[pl.BlockSpec((s, ta), lambda i, j: (0, i)), pl.BlockSpec((s, tb), lambda i, j: (0, j))],
        out_specs=out_spec, out_shape=jax.ShapeDtypeStruct(out_shape, F32),
        compiler_params=_params("parallel", "parallel"),
    )(a, b)


def mm_nt(name, a, w):
    s = a.shape[0]
    tm = 512

    def body(a_ref, w_ref, o_ref):
        o_ref[...] = _dot(a_ref[...], w_ref[...], NT).astype(BF16)

    return pl.pallas_call(
        body, name=name, grid=(s // tm,), in_specs=[_rows(tm, D), _full((D, D))], out_specs=_rows(tm, D),
        out_shape=jax.ShapeDtypeStruct((s, D), BF16), compiler_params=_params("parallel"),
    )(a, w)


def ff2_bwd(df, w_ff2, a):
    s = df.shape[0]
    tm = 1024

    def body(df_ref, w_ref, a_ref, da_ref):
        drl = _dot(df_ref[...], w_ref[...], NT)
        da_ref[...] = (drl * (2.0 * jnp.maximum(a_ref[...].astype(F32), 0.0))).astype(BF16)

    return pl.pallas_call(
        body, name="ff2_bwd", grid=(s // tm, D_FF // D),
        in_specs=[pl.BlockSpec((tm, D), lambda i, j: (i, 0)), pl.BlockSpec((D, D), lambda i, j: (j, 0)),
                  pl.BlockSpec((tm, D), lambda i, j: (i, j))],
        out_specs=pl.BlockSpec((tm, D), lambda i, j: (i, j)),
        out_shape=jax.ShapeDtypeStruct((s, D_FF), BF16), compiler_params=_params("parallel", "parallel"),
    )(df, w_ff2, a)


def ff1_bwd_norms(da, wg, x1, o, dy, g_fpre, g_post, swapping):
    s = x1.shape[0]
    tm = 256
    n = len(swapping)
    steps = s // tm

    def body(*refs):
        da_ref, w_ref, x1_ref, o_ref, dy_ref, gf_ref, gp_ref = refs[:7]
        dx1_ref, do_ref, dgf_ref, dgp_ref = refs[7 + n:11 + n]
        i = pl.program_id(0)
        if n:
            send, finish = _swap_phases(refs[7:7 + n], refs[11 + n:11 + 2 * n], *refs[11 + 2 * n:])
            pl.when(i == 0)(send)

        @pl.when(i == 0)
        def _():
            dgf_ref[...] = jnp.zeros_like(dgf_ref)
            dgp_ref[...] = jnp.zeros_like(dgp_ref)

        dh2 = _dot(da_ref[:, 0:D], w_ref[0], NT)
        for k in range(1, N_CHIPS):
            dh2 = dh2 + _dot(da_ref[:, k * D:(k + 1) * D], w_ref[k], NT)
        x1hat, r2 = _rms(x1_ref[...])
        dgf_ref[...] += jnp.sum(dh2 * x1hat, axis=0, keepdims=True)
        dx1 = dy_ref[...] + _rms_bwd(dh2 * gf_ref[...], x1hat, r2)
        ohat, r1 = _rms(o_ref[...])
        dgp_ref[...] += jnp.sum(dx1 * ohat, axis=0, keepdims=True)
        dx1_ref[...] = dx1
        do_ref[...] = _rms_bwd(dx1 * gp_ref[...], ohat, r1).astype(BF16)
        if n:
            pl.when(i == steps - 1)(finish)

    res = pl.pallas_call(
        body, name="ff1_bwd_norms", grid=(steps,),
        in_specs=[_rows(tm, D_FF), _full((N_CHIPS, D, D)), _rows(tm, D), _rows(tm, D), _rows(tm, D),
                  _full((1, D)), _full((1, D))] + [ANY] * n,
        out_specs=[_rows(tm, D), _rows(tm, D), _full((1, D)), _full((1, D))] + [ANY] * n,
        out_shape=[jax.ShapeDtypeStruct((s, D), F32), jax.ShapeDtypeStruct((s, D), BF16),
                   jax.ShapeDtypeStruct((1, D), F32), jax.ShapeDtypeStruct((1, D), F32)] + _swap_shapes(swapping),
        scratch_shapes=_swap_sems(n) if n else [],
        compiler_params=_params("arbitrary", communicates=bool(n)),
    )(da, wg, x1, o, dy, g_fpre, g_post, *swapping)
    return res[0], res[1], res[2], res[3], res[4:]


def out_bwd_gates(do, w_out, pa, pb, z, bg):
    s = do.shape[0]
    tm = 512

    def body(do_ref, w_ref, pa_ref, pb_ref, ga_ref, gb_ref, bg_ref, dpa_ref, dpb_ref, dga_ref, dgb_ref, dbg_ref):
        @pl.when(pl.program_id(0) == 0)
        def _():
            dbg_ref[...] = jnp.zeros_like(dbg_ref)

        dm = _dot(do_ref[...], w_ref[...], NT)
        sa = _sigmoid(ga_ref[...] + bg_ref[0:1, :])
        sb = _sigmoid(gb_ref[...] + bg_ref[1:2, :])
        dpa_ref[...] = (dm * sa).astype(BF16)
        dpb_ref[...] = (dm * sb).astype(BF16)
        dga = dm * pa_ref[...].astype(F32) * (sa * (1.0 - sa))
        dgb = dm * pb_ref[...].astype(F32) * (sb * (1.0 - sb))
        dga_ref[...] = dga.astype(BF16)
        dgb_ref[...] = dgb.astype(BF16)
        dbg_ref[0:1, :] += jnp.sum(dga, axis=0, keepdims=True)
        dbg_ref[1:2, :] += jnp.sum(dgb, axis=0, keepdims=True)

    out = jax.ShapeDtypeStruct((s, D), BF16)
    return pl.pallas_call(
        body, name="out_bwd_gates", grid=(s // tm,),
        in_specs=[_rows(tm, D), _full((D, D)), _rows(tm, D), _rows(tm, D), _rows(tm, D, 5), _rows(tm, D, 6),
                  _full((2, D))],
        out_specs=[_rows(tm, D)] * 4 + [_full((2, D))],
        out_shape=[out] * 4 + [jax.ShapeDtypeStruct((2, D), F32)], compiler_params=_params("arbitrary"),
    )(do, w_out, pa, pb, z, z, bg)


def gating_bwd(z, dya, ln_g, ln_b, w_s, bs_t, swapping):
    s = z.shape[0]
    ones = functools.partial(jnp.ones, (8, CHUNK), BF16)
    n = len(swapping)

    def body(*refs):
        u_ref, v_ref, dya_ref, lg_ref, lb_ref, ws_ref, bst_ref = refs[:7]
        du_ref, dv_ref, dws_ref, dbs_ref, dlg_ref, dlb_ref = refs[7 + n:13 + n]
        dvn_ref = refs[13 + 2 * n]
        ci = pl.program_id(0)
        if n:
            send, finish = _swap_phases(refs[7:7 + n], refs[13 + n:13 + 2 * n], *refs[14 + 2 * n:])
            pl.when(ci == 0)(send)

        @pl.when(ci == 0)
        def _():
            dws_ref[...] = jnp.zeros_like(dws_ref)
            dbs_ref[...] = jnp.zeros_like(dbs_ref)
            dlg_ref[...] = jnp.zeros_like(dlg_ref)
            dlb_ref[...] = jnp.zeros_like(dlb_ref)

        ug, dug_du = _gelu_and_grad(u_ref[...].astype(F32))
        vg, dvg_dv = _gelu_and_grad(v_ref[...].astype(F32))
        vhat, rstd = _layer_norm(vg)
        vn = (vhat * lg_ref[...] + lb_ref[...]).astype(BF16)
        dya = dya_ref[...].astype(F32)
        for g in range(GROUPS):
            cols = slice(g * CHUNK, (g + 1) * CHUNK)
            ws = _tril_ws(ws_ref, g)
            mixed = _dot(ws, vn[:, cols]) + bst_ref[:, g:g + 1]
            du_ref[:, cols] = (dya[:, cols] * mixed * dug_du[:, cols]).astype(BF16)
            dmix = (dya[:, cols] * ug[:, cols]).astype(BF16)
            dbs_ref[g] += _dot(ones(), dmix, NT)
            dws_ref[g] += _dot(dmix, vn[:, cols], NT)
            dvn_ref[:, cols] = _dot(ws, dmix, TN)
        dvn = dvn_ref[...]
        dlg_ref[...] += jnp.sum(dvn * vhat, axis=0, keepdims=True)
        dlb_ref[...] += jnp.sum(dvn, axis=0, keepdims=True)
        dvh = dvn * lg_ref[...]
        dvg = rstd * (dvh - jnp.mean(dvh, axis=-1, keepdims=True)
                      - vhat * jnp.mean(dvh * vhat, axis=-1, keepdims=True))
        dv_ref[...] = (dvg * dvg_dv).astype(BF16)

        @pl.when(ci == pl.num_programs(0) - 1)
        def _():
            r = lax.broadcasted_iota(jnp.int32, (CHUNK, CHUNK), 0)
            c = lax.broadcasted_iota(jnp.int32, (CHUNK, CHUNK), 1)
            for g in range(GROUPS):
                dws_ref[g] = jnp.where(c <= r, dws_ref[g], 0.0)

        if n:
            pl.when(ci == pl.num_programs(0) - 1)(finish)

    out = jax.ShapeDtypeStruct((s, D), BF16)
    res = pl.pallas_call(
        body, name="gating_bwd", grid=(s // CHUNK,),
        in_specs=[_rows(CHUNK, D, 0), _rows(CHUNK, D, 1), _rows(CHUNK, D), _full((1, D)), _full((1, D)),
                  _full((GROUPS, CHUNK, CHUNK)), _full((CHUNK, GROUPS))] + [ANY] * n,
        out_specs=[_rows(CHUNK, D), _rows(CHUNK, D), _full((GROUPS, CHUNK, CHUNK)), _full((GROUPS, 8, CHUNK)),
                   _full((1, D)), _full((1, D))] + [ANY] * n,
        out_shape=[out, out, jax.ShapeDtypeStruct((GROUPS, CHUNK, CHUNK), F32),
                   jax.ShapeDtypeStruct((GROUPS, 8, CHUNK), F32),
                   jax.ShapeDtypeStruct((1, D), F32), jax.ShapeDtypeStruct((1, D), F32)] + _swap_shapes(swapping),
        scratch_shapes=[pltpu.VMEM((CHUNK, D), F32)] + (_swap_sems(n) if n else []),
        compiler_params=_params("arbitrary", communicates=bool(n)),
    )(z, z, dya, ln_g, ln_b, w_s, bs_t, *swapping)
    return (*res[:6], res[6:])


def attn_bwd(z, yb, dyb, lse, logc, ka, kb, scattering, gathering=None):
    s = z.shape[0]
    nq = s // ATT_T
    t = ATT_T
    grp = ATT_BWD_GROUP
    ngrp = HEADS // 2 // grp
    wide = 128 * grp
    qcol, kcol, vcol = 2 * D // wide, 3 * D // wide, 4 * D // wide
    scale = 1.0 / math.sqrt(HEAD_DIM)
    n = len(scattering)
    g8 = 0 if gathering is None else 1

    def body(*refs):
        q_ref, k_ref, v_ref, y_ref, dy_ref, lse_ref, lc_ref, ka_ref, kb_ref = refs[:9]
        dq_ref, dk_ref, dv_ref = refs[9 + n + g8:12 + n + g8]
        qa_s, qt_s, da_s, dt_s, dq_s, dkt_s, dvt_s = refs[12 + 2 * n + 2 * g8:19 + 2 * n + 2 * g8]
        sems = refs[19 + 2 * n + 2 * g8:]
        gi, j = pl.program_id(0), pl.program_id(1)
        first, lane, ones = _head_masks()
        if n:
            send, finish = _scatter_phases(refs[9:9 + n], refs[12 + n + g8:12 + 2 * n + g8], *sems[:2])
            pl.when((gi == 0) & (j == 0))(send)
        if g8:
            send8, pass_on8, finish8 = _allgather8_phases(refs[12 + 2 * n + g8], *sems[2 * (n > 0):])
            pl.when((gi == 0) & (j == 0))(send8)
            pl.when((gi == ngrp - 1) & (j == nq - 1))(pass_on8)

        @pl.when(j == 0)
        def _():
            dq_s[...] = jnp.zeros_like(dq_s)
            for pr in range(grp):
                cols = slice(pr * 128, (pr + 1) * 128)
                for ib in range(nq):
                    rows = slice(ib * t, (ib + 1) * t)
                    q = q_ref[rows, cols].astype(F32) * scale
                    lse = lse_ref[rows, cols]
                    qa_s[pr, 0, ib] = jnp.where(first, q, _place3(lane, HEAD_DIM + 2 * AUG, _split3(-lse[:, 0:1]),
                                                                  ones(0, 2 * AUG))).astype(BF16)
                    qa_s[pr, 1, ib] = jnp.where(
                        first, _place3(lane, 2 * AUG, _split3(-lse[:, HEAD_DIM:HEAD_DIM + 1]), ones(1, 2 * AUG)),
                        q).astype(BF16)
                    qt_s[pr, ib, :, 0:t] = jnp.where(first, q, 0.0).T.astype(BF16)
                    qt_s[pr, ib, :, t:2 * t] = jnp.where(first, 0.0, q).T.astype(BF16)
                    do = dy_ref[rows, cols].astype(F32)
                    prod = do * y_ref[rows, cols].astype(F32)
                    dd0 = jnp.sum(jnp.where(first, prod, 0.0), axis=-1, keepdims=True)
                    dd1 = jnp.sum(jnp.where(first, 0.0, prod), axis=-1, keepdims=True)
                    da_s[pr, 0, ib] = jnp.where(first, do, _place3(lane, HEAD_DIM, _split3(-dd0), 0.0)).astype(BF16)
                    da_s[pr, 1, ib] = jnp.where(first, _place3(lane, 0, _split3(-dd1), 0.0), do).astype(BF16)
                    dt_s[pr, ib, :, 0:t] = jnp.where(first, do, 0.0).T.astype(BF16)
                    dt_s[pr, ib, :, t:2 * t] = jnp.where(first, 0.0, do).T.astype(BF16)

        keys = []
        for pr in range(grp):
            kj = k_ref[:, pr * 128:(pr + 1) * 128].astype(F32)
            vj = v_ref[:, pr * 128:(pr + 1) * 128].astype(F32)
            keys.append((
                jnp.where(first, kj, ka_ref[pr, 0] + kb_ref[pr, 0, pl.ds(j, 1), :]).astype(BF16),
                jnp.where(first, ka_ref[pr, 1] + kb_ref[pr, 1, pl.ds(j, 1), :], kj).astype(BF16),
                jnp.concatenate([jnp.where(first, kj, 0.0), jnp.where(first, 0.0, kj)], axis=0).astype(BF16),
                jnp.where(first, vj, ones(0, AUG)).astype(BF16),
                jnp.where(first, ones(1, AUG), vj).astype(BF16)))
        dkt_s[...] = jnp.zeros_like(dkt_s)
        dvt_s[...] = jnp.zeros_like(dvt_s)

        def step(i, _):
            lc = lc_ref[i - j]
            rows = pl.ds(pl.multiple_of(i * t, t), t)
            for pr in range(grp):
                k0a, k1a, kst, v0a, v1a = keys[pr]
                p0 = jnp.exp(_dot(qa_s[pr, 0, i], k0a, NT) + lc)
                p1 = jnp.exp(_dot(qa_s[pr, 1, i], k1a, NT) + lc)
                e0 = (p0 * _dot(da_s[pr, 0, i], v0a, NT)).astype(BF16)
                e1 = (p1 * _dot(da_s[pr, 1, i], v1a, NT)).astype(BF16)
                dq_s[pr, rows, :] += _dot(jnp.concatenate([e0, e1], axis=1), kst)
                dvt_s[pr] += _dot(dt_s[pr, i], jnp.concatenate([p0.astype(BF16), p1.astype(BF16)], axis=0))
                dkt_s[pr] += _dot(qt_s[pr, i], jnp.concatenate([e0, e1], axis=0))
            return 0

        lax.fori_loop(j, nq, step, 0)
        for pr in range(grp):
            dk_ref[:, pr * 128:(pr + 1) * 128] = dkt_s[pr].T.astype(BF16)
            dv_ref[:, pr * 128:(pr + 1) * 128] = dvt_s[pr].T.astype(BF16)

        @pl.when(j == nq - 1)
        def _():
            for pr in range(grp):
                dq_ref[:, pr * 128:(pr + 1) * 128] = (dq_s[pr] * scale).astype(BF16)

        if n:
            pl.when((gi == ngrp - 1) & (j == nq - 1))(finish)
        if g8:
            pl.when((gi == ngrp - 1) & (j == nq - 1))(finish8)

    colblock = lambda c: pl.BlockSpec((s, wide), lambda g, j: (0, c + g))
    once = lambda c: pl.BlockSpec((s, wide), lambda g, j: (0, c + g), pipeline_mode=pl.Buffered(1))
    blk = lambda c: pl.BlockSpec((t, wide), lambda g, j: (j, c + g))
    out = jax.ShapeDtypeStruct((s, D), BF16)
    res = pl.pallas_call(
        body, name="attn_bwd", grid=(ngrp, nq),
        in_specs=[once(qcol), blk(kcol), blk(vcol), once(0), once(0), once(0),
                  pl.BlockSpec((nq, t, t), lambda g, j: (0, 0, 0), pipeline_mode=pl.Buffered(1)),
                  pl.BlockSpec((grp, 2, t, 128), lambda g, j: (g, 0, 0, 0)),
                  pl.BlockSpec((grp, 2, nq, 128), lambda g, j: (g, 0, 0, 0))] + [ANY] * (n + g8),
        out_specs=[colblock(0), blk(0), blk(0)] + [ANY] * (n + g8),
        out_shape=[out] * 3 + _scatter_shapes(scattering)
        + ([jax.ShapeDtypeStruct(gathering.shape, gathering.dtype)] if g8 else []),
        input_output_aliases={9 + n: 3 + n} if g8 else {},
        scratch_shapes=[pltpu.VMEM((grp, 2, nq, t, 128), BF16), pltpu.VMEM((grp, nq, 128, 2 * t), BF16),
                        pltpu.VMEM((grp, 2, nq, t, 128), BF16), pltpu.VMEM((grp, nq, 128, 2 * t), BF16),
                        pltpu.VMEM((grp, s, 128), F32), pltpu.VMEM((grp, 128, t), F32),
                        pltpu.VMEM((grp, 128, t), F32)]
        + (_scatter_sems(n) if n else [])
        + ([pltpu.SemaphoreType.DMA((7,)), pltpu.SemaphoreType.DMA((7,))] if g8 else []),
        compiler_params=_params("arbitrary", "arbitrary", communicates=bool(n + g8)),
    )(z, z, z, yb, dyb, lse, logc, ka, kb, *scattering, *([gathering] if g8 else []))
    return res[0], res[1], res[2], res[3:3 + n], (res[3 + n] if g8 else None)


def in_bwd_norm(dz, wg, x, dx1, g_pre, scattering, gathering=None):
    s = x.shape[0]
    tm = 512
    n = len(scattering)
    g = 0 if gathering is None else 1
    last = (s // tm - 1, N_CHIPS - 1)

    def body(*refs):
        dz_ref, w_ref, x_ref, dx1_ref, g_ref = refs[:5]
        dx_ref, dg_ref = refs[5 + n + g:7 + n + g]
        acc_ref = refs[7 + 2 * n + 2 * g]
        sems = refs[8 + 2 * n + 2 * g:]
        k, i = pl.program_id(0), pl.program_id(1)
        rows = pl.ds(pl.multiple_of(i * tm, tm), tm)
        if n:
            send, finish = _scatter_phases(refs[5:5 + n], refs[7 + n + g:7 + 2 * n + g], *sems[:2])
            pl.when((i == 0) & (k == 0))(send)
        if g:
            send8, pass_on8, finish8 = _allgather8_phases(refs[7 + 2 * n + g], *sems[2 * (n > 0):])
            pl.when((i == 0) & (k == 0))(send8)
            pl.when((i == last[0]) & (k == last[1]))(pass_on8)

        @pl.when((i == 0) & (k == 0))
        def _():
            dg_ref[...] = jnp.zeros_like(dg_ref)

        part = _dot(dz_ref[...], w_ref[...], NT)

        @pl.when(k == 0)
        def _():
            acc_ref[rows, :] = part

        @pl.when((k > 0) & (k < N_CHIPS - 1))
        def _():
            acc_ref[rows, :] += part

        @pl.when(k == N_CHIPS - 1)
        def _():
            dh = acc_ref[rows, :] + part
            xhat, r = _rms(x_ref[...])
            dg_ref[...] += jnp.sum(dh * xhat, axis=0, keepdims=True)
            dx_ref[...] = dx1_ref[...] + _rms_bwd(dh * g_ref[...], xhat, r)

        if n:
            pl.when((i == last[0]) & (k == last[1]))(finish)
        if g:
            pl.when((i == last[0]) & (k == last[1]))(finish8)

    row = pl.BlockSpec((tm, D), lambda k, i: (jnp.where(k == N_CHIPS - 1, i, 0), 0))
    vec = pl.BlockSpec((1, D), lambda k, i: (0, 0))
    res = pl.pallas_call(
        body, name="in_bwd_norm", grid=(N_CHIPS, s // tm),
        in_specs=[pl.BlockSpec((tm, IN_SHARD), lambda k, i: (i, k)),
                  pl.BlockSpec((None, D, IN_SHARD), lambda k, i: (k, 0, 0)), row, row, vec] + [ANY] * (n + g),
        out_specs=[row, vec] + [ANY] * (n + g),
        out_shape=[jax.ShapeDtypeStruct((s, D), F32), jax.ShapeDtypeStruct((1, D), F32)]
        + _scatter_shapes(scattering) + ([jax.ShapeDtypeStruct(gathering.shape, gathering.dtype)] if g else []),
        input_output_aliases={5 + n: 2 + n} if g else {},
        scratch_shapes=[pltpu.VMEM((s, D), F32)] + (_scatter_sems(n) if n else [])
        + ([pltpu.SemaphoreType.DMA((7,)), pltpu.SemaphoreType.DMA((7,))] if g else []),
        compiler_params=_params("arbitrary", "arbitrary", communicates=bool(n + g)),
    )(dz, wg, x, dx1, g_pre, *scattering, *([gathering] if g else []))
    return res[0], res[1], res[2:2 + n], (res[2 + n] if g else None)


def _adamw_math(w, g, m, v):
    m = ADAM_B1 * m + (1.0 - ADAM_B1) * g
    v = ADAM_B2 * v + (1.0 - ADAM_B2) * (g * g)
    m_hat = m / (1.0 - ADAM_B1 ** ADAM_STEP)
    v_hat = v / (1.0 - ADAM_B2 ** ADAM_STEP)
    delta = -ADAM_LR * (m_hat / (jnp.sqrt(v_hat) + ADAM_EPS) + ADAM_WD * w)
    return delta, m, v


def adamw(name, w, g, m, v, tr):
    r, c = w.shape

    def body(w_ref, g_ref, m_ref, v_ref, go_ref, d_ref, nm_ref, nv_ref):
        g = g_ref[...]
        go_ref[...] = g
        d_ref[...], nm_ref[...], nv_ref[...] = _adamw_math(w_ref[...], g, m_ref[...], v_ref[...])

    out = jax.ShapeDtypeStruct((r, c), F32)
    return pl.pallas_call(
        body, name=name, grid=(r // tr,), in_specs=[_rows(tr, c)] * 4, out_specs=[_rows(tr, c)] * 4,
        out_shape=[out] * 4, compiler_params=_params("parallel"),
    )(w, g, m, v)


def _allgather8_phases(buf, send_sems, recv_sems):
    x, y, c, chips = _place()
    me = 2 * x + y
    sibling = (x, y, 1 - c)
    rows = buf.shape[1] // 2

    def part(chip, core):
        return buf.at[chip, pl.ds(core * rows, rows)]

    def copy(k, block, to):
        return pltpu.make_async_remote_copy(src_ref=block, dst_ref=block, send_sem=send_sems.at[k],
                                            recv_sem=recv_sems.at[k], device_id=to, device_id_type=MESH)

    def chip_of(j):
        return 2 * chips[j][0] + chips[j][1]

    def send():
        copy(0, part(me, c), sibling).start()
        for j in range(3):
            copy(1 + j, part(me, c), (chips[j][0], chips[j][1], c)).start()

    def pass_on():
        for j in range(3):
            copy(1 + j, part(chip_of(j), c), (chips[j][0], chips[j][1], c)).wait_recv()
            copy(4 + j, part(chip_of(j), c), sibling).start()

    def finish():
        copy(0, part(me, 1 - c), sibling).wait_recv()
        for j in range(3):
            copy(4 + j, part(chip_of(j), 1 - c), sibling).wait_recv()
        copy(0, part(me, c), sibling).wait_send()
        for j in range(3):
            copy(1 + j, part(me, c), (chips[j][0], chips[j][1], c)).wait_send()
            copy(4 + j, part(chip_of(j), c), sibling).wait_send()

    return send, pass_on, finish


def add_halves(name, g, recv, c_idx, tr):
    n, h, c = recv.shape

    def body(c_ref, g_ref, r_ref, o_ref):
        o_ref[...] = (g_ref[...] + r_ref[...]).astype(BF16)

    nb = h // tr
    return pl.pallas_call(
        body, name=name,
        grid_spec=pltpu.PrefetchScalarGridSpec(
            num_scalar_prefetch=1, grid=(n, nb),
            in_specs=[pl.BlockSpec((None, tr, c), lambda k, i, c_ref: (k, c_ref[0] * nb + i, 0)),
                      pl.BlockSpec((None, tr, c), lambda k, i, c_ref: (k, i, 0))],
            out_specs=pl.BlockSpec((None, tr, c), lambda k, i, c_ref: (k, i, 0))),
        out_shape=jax.ShapeDtypeStruct((n, h, c), BF16), compiler_params=_params("parallel", "parallel"),
    )(c_idx, g, recv)


def sum_chips(name, parts, recv, where, tr, after=None):
    n, h, c = recv.shape
    nb = h // tr

    def body(w_ref, p_ref, r_ref, *rest):
        acc = p_ref[...].astype(F32)
        for k in range(n):
            acc = acc + r_ref[k].astype(F32)
        rest[-1][...] = acc

    return pl.pallas_call(
        body, name=name,
        grid_spec=pltpu.PrefetchScalarGridSpec(
            num_scalar_prefetch=1, grid=(nb,),
            in_specs=[pl.BlockSpec((None, tr, c), lambda i, w_ref: (w_ref[0], i, 0)),
                      pl.BlockSpec((n, tr, c), lambda i, w_ref: (0, i, 0))] + ([ANY] if after is not None else []),
            out_specs=pl.BlockSpec((tr, c), lambda i, w_ref: (w_ref[1] * nb + i, 0))),
        out_shape=jax.ShapeDtypeStruct((2 * h, c), F32), compiler_params=_params("parallel"),
    )(where, parts, recv, *([after] if after is not None else []))


def place_shard(name, shard, where, dtype, tr):
    r, c = shard.shape

    def body(w_ref, s_ref, o_ref):
        o_ref[...] = s_ref[...].astype(dtype)

    return pl.pallas_call(
        body, name=name,
        grid_spec=pltpu.PrefetchScalarGridSpec(
            num_scalar_prefetch=1, grid=(r // tr,),
            in_specs=[pl.BlockSpec((tr, c), lambda i, w_ref: (i, 0))],
            out_specs=pl.BlockSpec((None, tr, c), lambda i, w_ref: (w_ref[0], i, 0))),
        out_shape=jax.ShapeDtypeStruct((N_CHIPS, r, c), dtype), compiler_params=_params("parallel"),
    )(where, shard)


ANY = pl.BlockSpec(memory_space=pl.ANY)


def _place():
    x, y, c = lax.axis_index("x"), lax.axis_index("y"), lax.axis_index("c")
    chips = [(1 - x, y), (x, 1 - y), (1 - x, 1 - y)]
    return x, y, c, chips


def gather_shards(arrays):
    n = len(arrays)

    def body(*refs):
        send, pass_on, finish = _gather_phases(refs[n:2 * n], *refs[2 * n:], _spans(arrays))
        send()
        pass_on()
        finish()

    return pl.pallas_call(
        body, name="gather_shards", in_specs=[ANY] * n, out_specs=[ANY] * n,
        out_shape=[jax.ShapeDtypeStruct(a.shape, a.dtype) for a in _arrays(arrays)],
        input_output_aliases={w: w for w in range(n)}, scratch_shapes=_gather_sems(n),
        compiler_params=pltpu.CompilerParams(has_side_effects=True),
    )(*_arrays(arrays))


def _gather_sems(n):
    return [pltpu.SemaphoreType.DMA((6 * n,)), pltpu.SemaphoreType.DMA((6 * n,))]


class Span(typing.NamedTuple):
    array: jax.Array
    lo: int
    hi: int
    ways: tuple = (0, 1, 2)


def _arrays(gathering):
    return [g.array if isinstance(g, Span) else g for g in gathering]


def _spans(gathering):
    return [(g.lo, g.hi, g.ways) if isinstance(g, Span) else (0, g.shape[1], (0, 1, 2)) for g in gathering]


def _gather_phases(out, send_sems, recv_sems, spans):
    n = len(out)
    if not any(ways for _, _, ways in spans):
        return (lambda: None,) * 3
    x, y, c, chips = _place()
    me = 2 * x + y
    sibling = (x, y, 1 - c)

    def half(w, chip, core):
        lo, hi, _ = spans[w]
        h = (hi - lo) // 2
        return out[w].at[chip, pl.ds(lo + core * h, h)]

    def copy(k, block, to):
        return pltpu.make_async_remote_copy(src_ref=block, dst_ref=block, send_sem=send_sems.at[k],
                                            recv_sem=recv_sems.at[k], device_id=to, device_id_type=MESH)

    def over_ici(w, j, chip):
        return copy(3 * w + j, half(w, chip, c), (chips[j][0], chips[j][1], c))

    def over_d2d(w, j, core):
        return copy(3 * n + 3 * w + j, half(w, 2 * chips[j][0] + chips[j][1], core), sibling)

    pairs = [(w, j) for w in range(n) for j in spans[w][2]]

    def send():
        for w, j in pairs:
            over_ici(w, j, me).start()

    def pass_on():
        for w, j in pairs:
            over_ici(w, j, 2 * chips[j][0] + chips[j][1]).wait_recv()
            over_d2d(w, j, c).start()

    def finish():
        for w, j in pairs:
            over_d2d(w, j, 1 - c).wait_recv()
        for w, j in pairs:
            over_ici(w, j, me).wait_send()
            over_d2d(w, j, c).wait_send()

    return send, pass_on, finish


def _relay_sems():
    return [pltpu.SemaphoreType.DMA((4,)), pltpu.SemaphoreType.DMA((4,))]


def _relay_phases(out, send_sems, recv_sems):
    x, y, c, chips = _place()
    sibling = (x, y, 1 - c)
    rows = out.shape[1]
    quarter = rows // 4
    far = 2 * chips[2][0] + chips[2][1]

    def piece(chip, way, core):
        return out.at[chip, pl.ds(way * (rows // 2) + core * quarter, quarter)]

    def copy(k, block, to):
        return pltpu.make_async_remote_copy(src_ref=block, dst_ref=block, send_sem=send_sems.at[k],
                                            recv_sem=recv_sems.at[k], device_id=to, device_id_type=MESH)

    def over_ici(way, chip):
        return copy(way, piece(chip, way, c), (chips[way][0], chips[way][1], c))

    def over_d2d(way, core):
        return copy(2 + way, piece(far, way, core), sibling)

    def send():
        for way in range(2):
            other = chips[1 - way]
            over_ici(way, 2 * other[0] + other[1]).start()

    def pass_on():
        for way in range(2):
            over_ici(way, far).wait_recv()
            over_d2d(way, c).start()

    def finish():
        for way in range(2):
            over_d2d(way, 1 - c).wait_recv()
        for way in range(2):
            other = chips[1 - way]
            over_ici(way, 2 * other[0] + other[1]).wait_send()
            over_d2d(way, c).wait_send()

    return send, pass_on, finish


def swap_halves(name, grads):
    n = len(grads)

    def body(*refs):
        send, finish = _swap_phases(refs[:n], refs[n:2 * n], *refs[2 * n:])
        send()
        finish()

    return pl.pallas_call(
        body, name=name, in_specs=[ANY] * n, out_specs=[ANY] * n, out_shape=_swap_shapes(grads),
        scratch_shapes=_swap_sems(n), compiler_params=pltpu.CompilerParams(has_side_effects=True),
    )(*grads)


def _swap_shapes(grads):
    return [jax.ShapeDtypeStruct((a.shape[0], a.shape[1] // 2, a.shape[2]), a.dtype) for a in grads]


def _swap_sems(n):
    return [pltpu.SemaphoreType.DMA((n,)), pltpu.SemaphoreType.DMA((n,))]


def _swap_phases(g, out, send_sems, recv_sems):
    x, y, c, _ = _place()

    def copies():
        return [pltpu.make_async_remote_copy(
            src_ref=g[w].at[:, pl.ds((1 - c) * (g[w].shape[1] // 2), g[w].shape[1] // 2)], dst_ref=out[w],
            send_sem=send_sems.at[w], recv_sem=recv_sems.at[w], device_id=(x, y, 1 - c), device_id_type=MESH)
            for w in range(len(g))]

    def send():
        for cp in copies():
            cp.start()

    def finish():
        for cp in copies():
            cp.wait()

    return send, finish


def _send_phases(g, out, send_sems, recv_sems):
    x, y, c, _ = _place()

    def copies():
        return [pltpu.make_async_remote_copy(
            src_ref=g[w], dst_ref=out[w], send_sem=send_sems.at[w], recv_sem=recv_sems.at[w],
            device_id=(x, y, 1 - c), device_id_type=MESH) for w in range(len(g))]

    def send():
        for cp in copies():
            cp.start()

    def finish():
        for cp in copies():
            cp.wait()

    return send, finish


def dw_in_half(name, h, dz, which, sending):
    s = h.shape[0]
    hh, tb = D // 2, IN_SHARD // 2
    n = len(sending)
    steps = IN_COLS // tb

    def body(w_ref, *refs):
        a_ref, b_ref, o_ref = refs[0], refs[1], refs[2 + n]
        j = pl.program_id(0)
        if n:
            send, finish = _send_phases(refs[2:2 + n], refs[3 + n:3 + 2 * n], *refs[3 + 2 * n:])
            pl.when(j == 0)(send)
        o_ref[...] = _dot(a_ref[...], b_ref[...], TN)
        if n:
            pl.when(j == steps - 1)(finish)

    out = pl.pallas_call(
        body, name=name,
        grid_spec=pltpu.PrefetchScalarGridSpec(
            num_scalar_prefetch=1, grid=(steps,),
            in_specs=[pl.BlockSpec((s, hh), lambda j, w: (0, w[0])), pl.BlockSpec((s, tb), lambda j, w: (0, j))]
            + [ANY] * n,
            out_specs=[pl.BlockSpec((None, hh, tb), lambda j, w: (j // 2, 0, j % 2))] + [ANY] * n,
            scratch_shapes=_swap_sems(n) if n else []),
        out_shape=[jax.ShapeDtypeStruct((N_CHIPS, hh, IN_SHARD), F32)]
        + [jax.ShapeDtypeStruct(a.shape, a.dtype) for a in sending],
        compiler_params=_params("arbitrary", communicates=bool(n)),
    )(which, h, dz, *sending)
    return out[0], out[1:]


def scatter_chips(parts):
    n = len(parts)

    def body(*refs):
        send, finish = _scatter_phases(refs[:n], refs[n:2 * n], *refs[2 * n:])
        send()
        finish()

    return pl.pallas_call(
        body, name="scatter_chips", in_specs=[ANY] * n, out_specs=[ANY] * n,
        out_shape=_scatter_shapes(parts), scratch_shapes=_scatter_sems(n),
        compiler_params=pltpu.CompilerParams(has_side_effects=True),
    )(*parts)


def _scatter_shapes(parts):
    return [jax.ShapeDtypeStruct((3,) + a.shape[1:], a.dtype) for a in parts]


def _scatter_sems(n):
    return [pltpu.SemaphoreType.DMA((3 * n,)), pltpu.SemaphoreType.DMA((3 * n,))]


def _scatter_phases(p, out, send_sems, recv_sems):
    x, y, c, chips = _place()

    def copies():
        return [pltpu.make_async_remote_copy(
            src_ref=p[w].at[2 * px + py], dst_ref=out[w].at[j], send_sem=send_sems.at[3 * w + j],
            recv_sem=recv_sems.at[3 * w + j], device_id=(px, py, c), device_id_type=MESH)
            for w in range(len(p)) for j, (px, py) in enumerate(chips)]

    def send():
        for cp in copies():
            cp.start()

    def finish():
        for cp in copies():
            cp.wait()

    return send, finish


def _join_only(arrays):
    n = len(arrays)

    def body(*refs):
        out = refs[n:2 * n]
        send_sems, recv_sems = refs[2 * n:]
        x, y, c, _ = _place()

        def copy(w, core):
            h = out[w].shape[0] // 2
            rows = out[w].at[pl.ds(core * h, h)]
            return pltpu.make_async_remote_copy(
                src_ref=rows, dst_ref=rows, send_sem=send_sems.at[w], recv_sem=recv_sems.at[w],
                device_id=(x, y, 1 - c), device_id_type=MESH)

        for w in range(n):
            copy(w, c).start()
        for w in range(n):
            copy(w, 1 - c).wait_recv()
        for w in range(n):
            copy(w, c).wait_send()

    return pl.pallas_call(
        body, name="join_only", in_specs=[ANY] * n, out_specs=[ANY] * n,
        out_shape=[jax.ShapeDtypeStruct(a.shape, a.dtype) for a in arrays],
        input_output_aliases={w: w for w in range(n)},
        scratch_shapes=[pltpu.SemaphoreType.DMA((n,)), pltpu.SemaphoreType.DMA((n,))],
        compiler_params=pltpu.CompilerParams(has_side_effects=True),
    )(*arrays)


HBM = pl.BlockSpec(memory_space=pltpu.HBM)
SEM = pl.BlockSpec(memory_space=pltpu.SEMAPHORE)
DATAFLOW = pltpu.SideEffectType.DATAFLOW_SIDE_EFFECTING


def _scatter_copies(p_ref, land_ref, send_sems, recv_sems):
    x, y, c, chips = _place()
    return [pltpu.make_async_remote_copy(
        src_ref=p_ref.at[2 * px + py], dst_ref=land_ref.at[j], send_sem=send_sems[j], recv_sem=recv_sems[j],
        device_id=(px, py, c), device_id_type=MESH) for j, (px, py) in enumerate(chips)]


def scatter_start(p):
    land = jax.ShapeDtypeStruct((3,) + p.shape[1:], p.dtype)

    def body(p_ref, land_ref, *outs):
        for cp in _scatter_copies(p_ref, land_ref, outs[0:3], outs[3:6]):
            cp.start()
        outs[8][...] = jnp.zeros_like(outs[8])

    return pl.pallas_call(
        body, name="scatter_start",
        out_shape=(pltpu.SemaphoreType.DMA(()),) * 6
        + (pltpu.HBM(p.shape, p.dtype), pltpu.HBM(land.shape, land.dtype), jax.ShapeDtypeStruct((8, 128), F32)),
        in_specs=(HBM, HBM), out_specs=(SEM,) * 6 + (HBM, HBM, pl.BlockSpec(memory_space=pltpu.VMEM)),
        input_output_aliases={0: 6, 1: 7},
        compiler_params=pltpu.CompilerParams(has_side_effects=DATAFLOW),
    )(pltpu.with_memory_space_constraint(p, pltpu.HBM),
      pltpu.with_memory_space_constraint(lax.empty(land.shape, land.dtype), pltpu.HBM))


def scatter_wait(started, after):
    sems, p_thru, land_thru = started[0:6], started[6], started[7]

    def body(p_ref, land_ref, *refs):
        for cp in _scatter_copies(p_ref, land_ref, refs[0:3], refs[3:6]):
            cp.wait_send()
            cp.wait_recv()

    return pl.pallas_call(
        body, name="scatter_wait",
        out_shape=(pltpu.HBM(p_thru.shape, p_thru.dtype), pltpu.HBM(land_thru.shape, land_thru.dtype)),
        in_specs=(HBM, HBM) + (SEM,) * 6 + (pl.BlockSpec(memory_space=pl.ANY),) * len(after), out_specs=(HBM, HBM),
        input_output_aliases={0: 0, 1: 1},
        compiler_params=pltpu.CompilerParams(has_side_effects=DATAFLOW),
    )(p_thru, land_thru, *sems, *after)


def _gather_leg_copies(buf_ref, leg, send_sems, recv_sems):
    x, y, c, chips = _place()
    h = buf_ref.shape[1] // 2
    copies = []
    for j, (px, py) in enumerate(chips):
        chip, to = (2 * x + y, (px, py, c)) if leg == "ici" else (2 * px + py, (x, y, 1 - c))
        block = buf_ref.at[chip, pl.ds(c * h, h)]
        copies.append(pltpu.make_async_remote_copy(
            src_ref=block, dst_ref=block, send_sem=send_sems[j], recv_sem=recv_sems[j],
            device_id=to, device_id_type=MESH))
    return copies


def gather_leg_start(name, buf, leg):
    def body(buf_ref, *outs):
        for cp in _gather_leg_copies(buf_ref, leg, outs[0:3], outs[3:6]):
            cp.start()
        outs[7][...] = jnp.zeros_like(outs[7])

    return pl.pallas_call(
        body, name=name,
        out_shape=(pltpu.SemaphoreType.DMA(()),) * 6
        + (pltpu.HBM(buf.shape, buf.dtype), jax.ShapeDtypeStruct((8, 128), F32)),
        in_specs=(HBM,), out_specs=(SEM,) * 6 + (HBM, pl.BlockSpec(memory_space=pltpu.VMEM)),
        input_output_aliases={0: 6},
        compiler_params=pltpu.CompilerParams(has_side_effects=DATAFLOW),
    )(pltpu.with_memory_space_constraint(buf, pltpu.HBM))


def gather_leg_wait(name, started, leg, after):
    def body(buf_ref, *refs):
        for cp in _gather_leg_copies(buf_ref, leg, refs[0:3], refs[3:6]):
            cp.wait_send()
            cp.wait_recv()

    buf = started[6]
    return pl.pallas_call(
        body, name=name, out_shape=pltpu.HBM(buf.shape, buf.dtype),
        in_specs=(HBM,) + (SEM,) * 6 + (pl.BlockSpec(memory_space=pl.ANY),) * len(after), out_specs=HBM,
        input_output_aliases={0: 0},
        compiler_params=pltpu.CompilerParams(has_side_effects=DATAFLOW),
    )(buf, *started[0:6], *after)


def _join_copies(refs, send_sems, recv_sems):
    x, y, c, _ = _place()
    copies = []
    for w, ref in enumerate(refs):
        h = ref.shape[0] // 2
        rows = ref.at[pl.ds(c * h, h)]
        copies.append(pltpu.make_async_remote_copy(
            src_ref=rows, dst_ref=rows, send_sem=send_sems[w], recv_sem=recv_sems[w],
            device_id=(x, y, 1 - c), device_id_type=MESH))
    return copies


def join_start(arrays):
    n = len(arrays)

    def body(*refs):
        outs = refs[n:]
        for cp in _join_copies(refs[:n], outs[0:n], outs[n:2 * n]):
            cp.start()
        outs[3 * n][...] = jnp.zeros_like(outs[3 * n])

    return pl.pallas_call(
        body, name="join_start",
        out_shape=(pltpu.SemaphoreType.DMA(()),) * (2 * n)
        + tuple(pltpu.HBM(a.shape, a.dtype) for a in arrays) + (jax.ShapeDtypeStruct((8, 128), F32),),
        in_specs=(HBM,) * n,
        out_specs=(SEM,) * (2 * n) + (HBM,) * n + (pl.BlockSpec(memory_space=pltpu.VMEM),),
        input_output_aliases={w: 2 * n + w for w in range(n)},
        compiler_params=pltpu.CompilerParams(has_side_effects=DATAFLOW),
    )(*[pltpu.with_memory_space_constraint(a, pltpu.HBM) for a in arrays])


def join_wait(started, after):
    n = (len(started) - 1) // 3

    def body(*refs):
        for cp in _join_copies(refs[:n], refs[n:2 * n], refs[2 * n:3 * n]):
            cp.wait_send()
            cp.wait_recv()

    bufs = started[2 * n:3 * n]
    return pl.pallas_call(
        body, name="join_wait", out_shape=tuple(pltpu.HBM(b.shape, b.dtype) for b in bufs),
        in_specs=(HBM,) * n + (SEM,) * (2 * n) + (pl.BlockSpec(memory_space=pl.ANY),) * len(after),
        out_specs=(HBM,) * n, input_output_aliases={w: w for w in range(n)},
        compiler_params=pltpu.CompilerParams(has_side_effects=DATAFLOW),
    )(*bufs, *started[0:2 * n], *after)


def join_halves(arrays, gathering=None):
    n = len(arrays)
    if gathering is None:
        return _join_only(arrays), None

    def body(*refs):
        out = refs[n + 1:2 * n + 1]
        send_sems, recv_sems = refs[2 * n + 2:2 * n + 4]
        send8, pass_on8, finish8 = _allgather8_phases(refs[2 * n + 1], *refs[2 * n + 4:])
        x, y, c, _ = _place()

        def copy(w, core):
            h = out[w].shape[0] // 2
            rows = out[w].at[pl.ds(core * h, h)]
            return pltpu.make_async_remote_copy(
                src_ref=rows, dst_ref=rows, send_sem=send_sems.at[w], recv_sem=recv_sems.at[w],
                device_id=(x, y, 1 - c), device_id_type=MESH)

        send8()
        for w in range(n):
            copy(w, c).start()
        pass_on8()
        for w in range(n):
            copy(w, 1 - c).wait_recv()
        finish8()
        for w in range(n):
            copy(w, c).wait_send()

    res = pl.pallas_call(
        body, name="join_halves", in_specs=[ANY] * (n + 1), out_specs=[ANY] * (n + 1),
        out_shape=[jax.ShapeDtypeStruct(a.shape, a.dtype) for a in list(arrays) + [gathering]],
        input_output_aliases={w: w for w in range(n + 1)},
        scratch_shapes=[pltpu.SemaphoreType.DMA((n,)), pltpu.SemaphoreType.DMA((n,)),
                        pltpu.SemaphoreType.DMA((7,)), pltpu.SemaphoreType.DMA((7,))],
        compiler_params=pltpu.CompilerParams(has_side_effects=True),
    )(*arrays, gathering)
    return res[:n], res[n]


def allreduce_small(packed):
    r, c = packed.shape
    n_dev = 8

    def body(x_ref, all_ref, sum_ref, send_sems, recv_sems, local_sem):
        x, y, cc, chips = _place()
        me, sibling = (x, y, cc), (x, y, 1 - cc)

        def rows(px, py, pc):
            return all_ref.at[4 * px + 2 * py + pc]

        def copy(k, block, to, src=None):
            return pltpu.make_async_remote_copy(
                src_ref=rows(*block) if src is None else src, dst_ref=rows(*block), send_sem=send_sems.at[k],
                recv_sem=recv_sems.at[k], device_id=to, device_id_type=MESH)

        mine = pltpu.make_async_copy(x_ref, rows(*me), local_sem)
        mine.start()
        first = [copy(0, me, sibling, src=x_ref)]
        first += [copy(1 + j, me, (*chip, cc), src=x_ref) for j, chip in enumerate(chips)]
        for cp in first:
            cp.start()
        passed = [copy(4 + j, (*chip, cc), sibling) for j, chip in enumerate(chips)]
        for j, chip in enumerate(chips):
            copy(1 + j, (*chip, cc), me).wait_recv()
            passed[j].start()
        copy(0, sibling, me).wait_recv()
        for j, chip in enumerate(chips):
            copy(4 + j, (*chip, 1 - cc), me).wait_recv()
        for cp in first + passed:
            cp.wait_send()
        mine.wait()
        acc = all_ref[0]
        for k in range(1, n_dev):
            acc = acc + all_ref[k]
        sum_ref[...] = acc

    vm = pl.BlockSpec(memory_space=pltpu.VMEM)
    return pl.pallas_call(
        body, name="allreduce_small", in_specs=[vm], out_specs=[vm, vm],
        out_shape=[jax.ShapeDtypeStruct((n_dev, r, c), F32), jax.ShapeDtypeStruct((r, c), F32)],
        scratch_shapes=[pltpu.SemaphoreType.DMA((7,)), pltpu.SemaphoreType.DMA((7,)), pltpu.SemaphoreType.DMA],
        compiler_params=pltpu.CompilerParams(has_side_effects=True, vmem_limit_bytes=VMEM_LIMIT),
    )(packed)[1]


def local_step(x, target, vecs, w_s, bs_t, bg, wg_in, late, core=None, order=None, where=None):
    on_mesh = core is not None

    def add(names, grads, recv):
        return [add_halves("add_" + n, g, r, core, min(r.shape[1], 256)) for n, g, r in zip(names, grads, recv)]

    g_pre, ln_g, ln_b, g_post, g_fpre, g_fpost = vecs
    s = x.shape[0]
    if order is None:
        order = jnp.arange(N_CHIPS, dtype=jnp.int32)
    logc = _attn_tables(s)
    ka, kb = _alibi_tables(s)

    h = norm_pre(x, g_pre)
    if not on_mesh:
        wg_a, wg_b, wg_out, wg_ff1, wg_ff2 = late
    if on_mesh:
        z, wg_in, (wg_a, wg_b, wg_out, wg_ff1, wg_ff2) = mm_in(h, wg_in, order, True, late)
        ya, (wg_b,) = gating_fwd(z, ln_g, ln_b, w_s, bs_t, [wg_b])
        yb, lse, (wg_a, wg_ff1, wg_out, bg) = attn_fwd(z, logc, ka, kb, [wg_a, wg_ff1, wg_out, bg])
        over_ici = gather_leg_start("ff2_ici_start", wg_ff2, "ici")
        bg = jnp.transpose(bg[:, :2, :], (1, 0, 2)).reshape(2, D) + over_ici[7][0:1, 0:1]
    else:
        z, _, _ = mm_in(h, wg_in, order, False)
        ya, _ = gating_fwd(z, ln_g, ln_b, w_s, bs_t, [])
        yb, lse, _ = attn_fwd(z, logc, ka, kb, [])
    merged, pa, pb, _ = proj_merge(ya, yb, wg_a.reshape(D, D), wg_b.reshape(D, D), z, bg, [])
    w_out = wg_out.reshape(D, D)
    o, x1, h2, _ = out_norm(merged, w_out, x, g_post, g_fpre, [])
    a, rl, _ = mm_ff1(h2, wg_ff1, [])
    if on_mesh:
        over_d2d = gather_leg_start("ff2_d2d_start", gather_leg_wait("ff2_ici_wait", over_ici, "ici", [rl]), "d2d")
        wg_ff2 = gather_leg_wait("ff2_d2d_wait", over_d2d, "d2d", [])
    w_ff2 = wg_ff2.reshape(D_FF, D)
    dy, df, d_gfpost, loss = ff2_loss(rl, w_ff2, x1, target, g_fpost)

    half_cols = pl.BlockSpec((D, D // 2), lambda i, j: (0, j))
    d_wff2 = mm_tn("dw_ff2", rl, df, D // 2, D, (D_FF, D), pl.BlockSpec((D // 2, D), lambda i, j: (i, 0)))
    da = ff2_bwd(df, w_ff2, a)
    d_wff1 = mm_tn("dw_ff1", h2, da, D, D // 2, (N_CHIPS, D, D),
                   pl.BlockSpec((None, D, D // 2), lambda i, j: (j // 2, 0, j % 2)))
    d_ff = [d_wff1, d_wff2.reshape(N_CHIPS, D, D)]
    dx1, do, d_gfpre, d_gpost, recv_ff = ff1_bwd_norms(da, wg_ff1, x1, o, dy, g_fpre, g_post, d_ff if on_mesh else [])
    d_wout = mm_tn("dw_out", merged, do, D, D // 2, (D, D), half_cols)
    dpa, dpb, dga, dgb, d_bg = out_bwd_gates(do, w_out, pa, pb, z, bg)
    d_wa = mm_tn("dw_a", ya, dpa, D, D // 2, (D, D), half_cols)
    d_wb = mm_tn("dw_b", yb, dpb, D, D // 2, (D, D), half_cols)
    dya = mm_nt("dy_a", dpa, wg_a.reshape(D, D))
    dyb = mm_nt("dy_b", dpb, wg_b.reshape(D, D))
    d_proj = [d_wa.reshape(N_CHIPS, D // N_CHIPS, D), d_wb.reshape(N_CHIPS, D // N_CHIPS, D),
              d_wout.reshape(N_CHIPS, D // N_CHIPS, D)]
    du, dv, d_ws, d_bs, d_lng, d_lnb, recv_proj = gating_bwd(z, dya, ln_g, ln_b, w_s, bs_t, d_proj if on_mesh else [])
    early = d_proj + d_ff
    parts_early = add(BIG[1:], early, list(recv_proj) + list(recv_ff)) if on_mesh else []
    small = dict(b_gate=d_bg, ln_v_g=d_lng, ln_v_b=d_lnb, w_s=d_ws, b_s=d_bs[:, 0, :],
                 norm_mix_post=d_gpost, norm_ffn_pre=d_gfpre, norm_ffn_post=d_gfpost)
    packed = pack_small(dict(small, norm_mix_pre=jnp.zeros((1, D), F32)), loss, where) if on_mesh else None
    dq, dk, dvb, got_early, packed = attn_bwd(z, yb, dyb, lse, logc, ka, kb, parts_early, packed)
    dz = jnp.concatenate([du, dv, dq, dk, dvb, dga, dgb], axis=1)
    if on_mesh:
        for_sibling, _ = dw_in_half("dw_in_sibling", h, dz, 1 - core, [])
        mine, from_sibling = dw_in_half("dw_in_mine", h, dz, core, [for_sibling])
        d_win = None
        parts_late = [add_halves("add_w_in", mine, from_sibling[0], jnp.zeros((1,), jnp.int32), 256)]
    else:
        half = IN_SHARD // 2
        d_win = mm_tn("dw_in", h, dz, D, half, (N_CHIPS, D, IN_SHARD),
                      pl.BlockSpec((None, D, half), lambda i, j: (j // 2, 0, j % 2)))
        parts_late = []
    started = scatter_start(parts_late[0]) if on_mesh else None
    dx, d_gpre, _, _ = in_bwd_norm(dz, wg_in, x, dx1, g_pre + started[8][0:1, 0:1] if on_mesh else g_pre, [])
    small["norm_mix_pre"] = d_gpre
    return loss[0, 0], dx, [d_win] + early, small, parts_early, list(got_early), packed, started


BIG = ("w_in", "w_a_proj", "w_b_proj", "w_out", "w_ff1", "w_ff2")
SMALL = ("norm_mix_pre", "ln_v_g", "ln_v_b", "b_s", "norm_mix_post", "norm_ffn_pre", "norm_ffn_post", "w_s", "b_gate")
ORDER = ("norm_mix_pre", "w_in", "b_gate", "ln_v_g", "ln_v_b", "w_s", "b_s", "w_a_proj", "w_b_proj", "w_out",
         "norm_mix_post", "norm_ffn_pre", "w_ff1", "w_ff2", "norm_ffn_post")
VEC_ROWS = D // 128
WS_ROW = 7 * VEC_ROWS
BG_ROW = WS_ROW + GROUPS * CHUNK
LOSS_ROW = BG_ROW + 2 * VEC_ROWS
PACK_ROWS = LOSS_ROW + 8


def pack_small(small, loss, where):
    vectors = [small[n] for n in SMALL[:7]]
    operands = vectors + [small["w_s"], small["b_gate"], loss]

    def body(where_ref, *refs):
        out = refs[-1]
        ws_ref, bg_ref, loss_ref = refs[7:10]
        for i, n in enumerate(SMALL[:7]):
            if n == "b_s":
                out[i * VEC_ROWS:(i + 1) * VEC_ROWS, :] = refs[i][...]
            else:
                for j in range(VEC_ROWS):
                    out[i * VEC_ROWS + j:i * VEC_ROWS + j + 1, :] = refs[i][:, j * 128:(j + 1) * 128]
        for g in range(GROUPS):
            out[WS_ROW + g * CHUNK:WS_ROW + (g + 1) * CHUNK, :] = ws_ref[g]
        for r in range(2):
            for j in range(VEC_ROWS):
                row = BG_ROW + r * VEC_ROWS + j
                out[row:row + 1, :] = bg_ref[r:r + 1, j * 128:(j + 1) * 128]
        lane = lax.broadcasted_iota(jnp.int32, (8, 128), 1)
        sub = lax.broadcasted_iota(jnp.int32, (8, 128), 0)
        out[LOSS_ROW:LOSS_ROW + 8, :] = jnp.where((lane == 0) & (sub == 0), loss_ref[...], 0.0)

    return pl.pallas_call(
        body, name="pack_small",
        grid_spec=pltpu.PrefetchScalarGridSpec(
            num_scalar_prefetch=1, grid=(1,), in_specs=[_full(a.shape) for a in operands],
            out_specs=pl.BlockSpec((None, PACK_ROWS, 128), lambda i, w: (w[0], w[1], 0))),
        out_shape=jax.ShapeDtypeStruct((N_CHIPS, 2 * PACK_ROWS, 128), F32), compiler_params=_params("arbitrary"),
    )(where, *operands)


def pack_vector(vec, where):
    def body(where_ref, v_ref, out):
        for j in range(VEC_ROWS):
            out[j:j + 1, :] = v_ref[:, j * 128:(j + 1) * 128]

    return pl.pallas_call(
        body, name="pack_vector",
        grid_spec=pltpu.PrefetchScalarGridSpec(
            num_scalar_prefetch=1, grid=(1,), in_specs=[_full(vec.shape)],
            out_specs=pl.BlockSpec((None, VEC_ROWS, 128), lambda i, w: (w[0], w[1], 0))),
        out_shape=jax.ShapeDtypeStruct((N_CHIPS, 2 * VEC_ROWS, 128), F32), compiler_params=_params("arbitrary"),
    )(where, vec)


def adamw_small(gathered, first, chip, w, m, v):
    shapes = {n: (1, D) for n in SMALL}
    shapes.update(b_s=(GROUPS, CHUNK), w_s=(GROUPS * CHUNK, CHUNK), b_gate=(2, D // N_CHIPS))
    flat = lambda t: [t[n].reshape(shapes[n]) for n in SMALL]
    per = D // N_CHIPS // 128

    def body(chip_ref, all_ref, first_ref, *refs):
        params, outs = refs[:27], refs[27:]
        sub = lax.broadcasted_iota(jnp.int32, (VEC_ROWS, 128), 0)
        sum_ref = outs[36]
        total = all_ref[0, 0:PACK_ROWS, :]
        head = first_ref[0, 0:VEC_ROWS, :]
        for k in range(1, 2 * N_CHIPS):
            total = total + all_ref[k // 2, (k % 2) * PACK_ROWS:(k % 2 + 1) * PACK_ROWS, :]
            head = head + first_ref[k // 2, (k % 2) * VEC_ROWS:(k % 2 + 1) * VEC_ROWS, :]
        sum_ref[...] = total
        sum_ref[0:VEC_ROWS, :] = head

        def gate_row(r):
            rows = sum_ref[BG_ROW + r * VEC_ROWS:BG_ROW + (r + 1) * VEC_ROWS, :]
            return jnp.concatenate([jnp.sum(jnp.where(sub == per * chip_ref[0] + j, rows, 0.0), axis=0, keepdims=True)
                                    for j in range(per)], axis=1)

        for i, n in enumerate(SMALL):
            if n == "b_s":
                g = sum_ref[i * VEC_ROWS:(i + 1) * VEC_ROWS, :]
            elif n == "w_s":
                g = sum_ref[WS_ROW:BG_ROW, :]
            elif n == "b_gate":
                g = jnp.concatenate([gate_row(0), gate_row(1)], axis=0)
            else:
                g = jnp.concatenate([sum_ref[i * VEC_ROWS + j:i * VEC_ROWS + j + 1, :] for j in range(VEC_ROWS)],
                                    axis=1)
            delta, nm, nv = _adamw_math(params[i][...], g, params[9 + i][...], params[18 + i][...])
            outs[4 * i][...], outs[4 * i + 1][...], outs[4 * i + 2][...], outs[4 * i + 3][...] = g, delta, nm, nv

    vm = pl.BlockSpec(memory_space=pltpu.VMEM)
    res = pl.pallas_call(
        body, name="adamw_small",
        in_specs=[pl.BlockSpec(memory_space=pltpu.SMEM)] + [vm] * 29, out_specs=[vm] * 37,
        out_shape=[jax.ShapeDtypeStruct(shapes[n], F32) for n in SMALL for _ in range(4)]
        + [jax.ShapeDtypeStruct((PACK_ROWS, 128), F32)],
        compiler_params=_params(),
    )(chip, gathered, first, *flat(w), *flat(m), *flat(v))
    new = {n: tuple(r.reshape(w[n].shape) for r in res[4 * i:4 * i + 4]) for i, n in enumerate(SMALL)}
    return new, res[36][LOSS_ROW, 0]


def kernel(x, norm_mix_pre, w_in, b_gate, ln_v_g, ln_v_b, w_s, b_s, w_a_proj, w_b_proj, w_out, norm_mix_post, norm_ffn_pre, w_ff1, w_ff2, norm_ffn_post, loss_target, m_norm_mix_pre, m_w_in, m_b_gate, m_ln_v_g, m_ln_v_b, m_w_s, m_b_s, m_w_a_proj, m_w_b_proj, m_w_out, m_norm_mix_post, m_norm_ffn_pre, m_w_ff1, m_w_ff2, m_norm_ffn_post, v_norm_mix_pre, v_w_in, v_b_gate, v_ln_v_g, v_ln_v_b, v_w_s, v_b_s, v_w_a_proj, v_w_b_proj, v_w_out, v_norm_mix_post, v_norm_ffn_pre, v_w_ff1, v_w_ff2, v_norm_ffn_post):
    w = dict(norm_mix_pre=norm_mix_pre, w_in=w_in, b_gate=b_gate, ln_v_g=ln_v_g, ln_v_b=ln_v_b, w_s=w_s, b_s=b_s,
             w_a_proj=w_a_proj, w_b_proj=w_b_proj, w_out=w_out, norm_mix_post=norm_mix_post,
             norm_ffn_pre=norm_ffn_pre, w_ff1=w_ff1, w_ff2=w_ff2, norm_ffn_post=norm_ffn_post)
    m = dict(norm_mix_pre=m_norm_mix_pre, w_in=m_w_in, b_gate=m_b_gate, ln_v_g=m_ln_v_g, ln_v_b=m_ln_v_b, w_s=m_w_s,
             b_s=m_b_s, w_a_proj=m_w_a_proj, w_b_proj=m_w_b_proj, w_out=m_w_out, norm_mix_post=m_norm_mix_post,
             norm_ffn_pre=m_norm_ffn_pre, w_ff1=m_w_ff1, w_ff2=m_w_ff2, norm_ffn_post=m_norm_ffn_post)
    v = dict(norm_mix_pre=v_norm_mix_pre, w_in=v_w_in, b_gate=v_b_gate, ln_v_g=v_ln_v_g, ln_v_b=v_ln_v_b, w_s=v_w_s,
             b_s=v_b_s, w_a_proj=v_w_a_proj, w_b_proj=v_w_b_proj, w_out=v_w_out, norm_mix_post=v_norm_mix_post,
             norm_ffn_pre=v_norm_ffn_pre, w_ff1=v_w_ff1, w_ff2=v_w_ff2, norm_ffn_post=v_norm_ffn_post)
    chip = 2 * lax.axis_index("x") + lax.axis_index("y")
    core = lax.axis_index("c")

    where = jnp.stack([chip, core]).astype(jnp.int32)
    wg_in = place_shard("place_w_in", w_in[0], where, BF16, 256)
    bg_all = place_shard("place_b_gate", jnp.pad(b_gate[0], ((0, 14), (0, 0))), where, F32, 16)
    vecs = (norm_mix_pre, ln_v_g, ln_v_b, norm_mix_post, norm_ffn_pre, norm_ffn_post)
    loss, dx, _, small, parts, got, packed, started = local_step(
        x[0], loss_target[0], vecs, w_s[0], b_s[0].T, bg_all, wg_in, [w[n][0] for n in BIG[1:]],
        core=jnp.reshape(core, (1,)).astype(jnp.int32),
        order=jnp.stack([chip, chip ^ 2, chip ^ 1, chip ^ 3]).astype(jnp.int32), where=where)

    halves = [sum_chips("sum_" + n, p, r, where, min(p.shape[1], 256), started[8])
              for n, p, r in zip(BIG[1:], parts, got)]
    joined = join_wait(join_start(halves), [dx])
    grads = dict(zip(BIG[1:], joined))
    new = {}

    def update(n):
        shape = w[n].shape
        res = adamw("adamw_" + n, w[n][0], grads[n], m[n][0], v[n][0], min(shape[1], 256))
        new[n] = tuple(r.reshape(shape) for r in res)

    for n in BIG[1:]:
        update(n)
    p_in, got_in = scatter_wait(started, [new["w_ff2"][1], dx])
    (grads["w_in"],), first = join_halves([sum_chips("sum_w_in", p_in, got_in, where, 256)],
                                          pack_vector(small["norm_mix_pre"], where))
    update("w_in")
    small_new, loss = adamw_small(packed, first, jnp.reshape(chip, (1,)).astype(jnp.int32), w, m, v)
    new.update(small_new)

    outs = [loss, dx[None]]
    for i in range(4):
        outs += [new[n][i] for n in ORDER]
    return tuple(outs)
```
